```python
import jax, jax.numpy as jnp
from jax import lax
import numpy as np

D_MODEL = 1024
BATCH = 8
SEQ = 4096
DEPTH = 2

GMLP_WIDTH = D_MODEL
GMLP_GROUPS = 8
GMLP_GROUP_DIM = GMLP_WIDTH // GMLP_GROUPS
CHUNK = 128
LRU_WIDTH = D_MODEL
LRU_HEADS = 8
LRU_HEAD_DIM = LRU_WIDTH // LRU_HEADS
CONV_WIDTH = 4
CONV_LEFT = 1
LRU_C = 8.0
N_DIRS = 2
D_FF = -(-8 * D_MODEL // (3 * 256)) * 256
N_IN = 2 * GMLP_WIDTH + 2 * LRU_WIDTH + 2 * D_MODEL
EPS = 1e-6

kernel_name = "hybrid_gmlp_rglru_encoder"


def _rmsnorm(x, g):
    x32 = x.astype(jnp.float32)
    y = x32 * lax.rsqrt(jnp.mean(x32 * x32, axis=-1, keepdims=True) + EPS)
    return (y * g.astype(jnp.float32)).astype(x.dtype)


def _layernorm(x, g, b):
    x32 = x.astype(jnp.float32)
    mu = jnp.mean(x32, axis=-1, keepdims=True)
    xc = x32 - mu
    y = xc * lax.rsqrt(jnp.mean(xc * xc, axis=-1, keepdims=True) + EPS)
    return (y * g.astype(jnp.float32) + b.astype(jnp.float32)).astype(x.dtype)


def _blockdiag(x, w, b):
    B, S, _ = x.shape
    xh = x.reshape(B, S, LRU_HEADS, LRU_HEAD_DIM)
    y = jnp.einsum('bshc,hcd->bshd', xh, w.astype(x.dtype))
    return y.reshape(B, S, LRU_WIDTH) + b.astype(x.dtype)


def _lin_combine(left, right):
    a1, b1 = left
    a2, b2 = right
    return a1 * a2, a2 * b1 + b2


def _rglru_scan(x32, w_r, b_r, w_i, b_i, lam, reverse):
    r = jax.nn.sigmoid(_blockdiag(x32, w_r, b_r))
    i = jax.nn.sigmoid(_blockdiag(x32, w_i, b_i))
    log_a = -LRU_C * r * jax.nn.softplus(-lam.astype(jnp.float32))
    a = jnp.exp(log_a)
    mult = jnp.sqrt(jnp.maximum(-jnp.expm1(2.0 * log_a), 0.0))
    bt = mult * (i * x32)
    _, h = lax.associative_scan(_lin_combine, (a, bt), axis=1, reverse=reverse)
    return h


def _gmlp_branch(zu, zv, ln_g, ln_b, w_s, b_s):
    B, S, _ = zu.shape
    u = jax.nn.gelu(zu)
    v = _layernorm(jax.nn.gelu(zv), ln_g, ln_b)
    vc = v.reshape(B, S // CHUNK, CHUNK, GMLP_GROUPS, GMLP_GROUP_DIM)
    mixed = jnp.einsum('gpq,bnqgc->bnpgc', w_s.astype(v.dtype), vc)
    mixed = mixed + b_s.T.astype(v.dtype)[None, None, :, :, None]
    return u * mixed.reshape(B, S, GMLP_WIDTH)


def _rglru_branch(zx, zg, conv_w, conv_b, w_r, b_r, w_i, b_i, lam):
    S = zx.shape[1]
    xp = jnp.pad(zx, ((0, 0), (CONV_LEFT, CONV_WIDTH - 1 - CONV_LEFT), (0, 0)))
    xc = conv_b.astype(zx.dtype) + sum(xp[:, k:k + S] * conv_w[k].astype(zx.dtype) for k in range(CONV_WIDTH))
    x32 = xc.astype(jnp.float32)
    h = (_rglru_scan(x32, w_r[0], b_r[0], w_i[0], b_i[0], lam[0], False)
         + _rglru_scan(x32, w_r[1], b_r[1], w_i[1], b_i[1], lam[1], True))
    return h.astype(zx.dtype) * jax.nn.gelu(zg)


def _fwd_setup_inputs(seed: int = 0) -> dict:
    key = jax.random.key(seed)
    ks = jax.random.split(key, 24)
    f32 = jnp.float32
    nrm = lambda k, shape, s: jax.random.normal(k, shape, f32) * s
    L = DEPTH
    x = jax.random.normal(ks[0], (BATCH, SEQ, D_MODEL), f32)
    norm1_g = 1.0 + nrm(ks[1], (L, D_MODEL), 0.05)
    w_in = nrm(ks[2], (L, D_MODEL, N_IN), D_MODEL ** -0.5)
    gmlp_ln_g = 1.0 + nrm(ks[3], (L, GMLP_WIDTH), 0.05)
    gmlp_ln_b = nrm(ks[4], (L, GMLP_WIDTH), 0.05)
    gmlp_w_s = nrm(ks[5], (L, GMLP_GROUPS, CHUNK, CHUNK), CHUNK ** -0.5)
    gmlp_b_s = 1.0 + nrm(ks[6], (L, GMLP_GROUPS, CHUNK), 0.1)
    conv_w = nrm(ks[7], (L, CONV_WIDTH, LRU_WIDTH), CONV_WIDTH ** -0.5)
    conv_b = nrm(ks[8], (L, LRU_WIDTH), 0.02)
    lru_w_r = nrm(ks[9], (L, N_DIRS, LRU_HEADS, LRU_HEAD_DIM, LRU_HEAD_DIM), LRU_HEAD_DIM ** -0.5)
    lru_b_r = nrm(ks[10], (L, N_DIRS, LRU_WIDTH), 0.02)
    lru_w_i = nrm(ks[11], (L, N_DIRS, LRU_HEADS, LRU_HEAD_DIM, LRU_HEAD_DIM), LRU_HEAD_DIM ** -0.5)
    lru_b_i = nrm(ks[12], (L, N_DIRS, LRU_WIDTH), 0.02)
    a_c = jax.random.uniform(ks[13], (L, N_DIRS, LRU_WIDTH), f32, 0.9, 0.999)
    a0 = a_c ** (1.0 / LRU_C)
    lru_lambda = jnp.log(a0) - jnp.log1p(-a0)
    w_out = nrm(ks[14], (L, D_MODEL, D_MODEL), D_MODEL ** -0.5)
    norm2_g = 1.0 + nrm(ks[15], (L, D_MODEL), 0.05)
    w_ffn_in = nrm(ks[16], (L, D_MODEL, 2 * D_FF), D_MODEL ** -0.5)
    w_ffn_out = nrm(ks[17], (L, D_FF, D_MODEL), D_FF ** -0.5)
    final_g = 1.0 + nrm(ks[18], (D_MODEL,), 0.05)
    return {"x": x, "norm1_g": norm1_g, "w_in": w_in, "gmlp_ln_g": gmlp_ln_g,
            "gmlp_ln_b": gmlp_ln_b, "gmlp_w_s": gmlp_w_s, "gmlp_b_s": gmlp_b_s,
            "conv_w": conv_w, "conv_b": conv_b, "lru_w_r": lru_w_r, "lru_b_r": lru_b_r,
            "lru_w_i": lru_w_i, "lru_b_i": lru_b_i, "lru_lambda": lru_lambda,
            "w_out": w_out, "norm2_g": norm2_g, "w_ffn_in": w_ffn_in,
            "w_ffn_out": w_ffn_out, "final_g": final_g}


def _fwd_reference(x, norm1_g, w_in, gmlp_ln_g, gmlp_ln_b, gmlp_w_s, gmlp_b_s, conv_w, conv_b,
              lru_w_r, lru_b_r, lru_w_i, lru_b_i, lru_lambda, w_out, norm2_g, w_ffn_in,
              w_ffn_out, final_g):
    c0 = GMLP_WIDTH
    c1 = 2 * GMLP_WIDTH
    c2 = c1 + LRU_WIDTH
    c3 = c2 + LRU_WIDTH
    c4 = c3 + D_MODEL
    for l in range(DEPTH):
        h = _rmsnorm(x, norm1_g[l])
        z = h @ w_in[l].astype(h.dtype)
        y_a = _gmlp_branch(z[..., :c0], z[..., c0:c1], gmlp_ln_g[l], gmlp_ln_b[l],
                           gmlp_w_s[l], gmlp_b_s[l])
        y_b = _rglru_branch(z[..., c1:c2], z[..., c2:c3], conv_w[l], conv_b[l],
                            lru_w_r[l], lru_b_r[l], lru_w_i[l], lru_b_i[l], lru_lambda[l])
        merged = jax.nn.sigmoid(z[..., c3:c4]) * y_a + jax.nn.sigmoid(z[..., c4:]) * y_b
        x = x + merged @ w_out[l].astype(merged.dtype)
        h = _rmsnorm(x, norm2_g[l])
        gu = h @ w_ffn_in[l].astype(h.dtype)
        ff = jax.nn.silu(gu[..., :D_FF]) * gu[..., D_FF:]
        x = x + ff @ w_ffn_out[l].astype(ff.dtype)
    return _rmsnorm(x, final_g)


import jax as _jax
import jax.numpy as _jnp

TWIN_FORMAT = 'train_step'
FWD_PARAMS = ['x', 'norm1_g', 'w_in', 'gmlp_ln_g', 'gmlp_ln_b', 'gmlp_w_s', 'gmlp_b_s', 'conv_w', 'conv_b', 'lru_w_r', 'lru_b_r', 'lru_w_i', 'lru_b_i', 'lru_lambda', 'w_out', 'norm2_g', 'w_ffn_in', 'w_ffn_out', 'final_g']
TWIN_WEIGHTS = ['norm1_g', 'w_in', 'gmlp_ln_g', 'gmlp_ln_b', 'gmlp_w_s', 'gmlp_b_s', 'conv_w', 'conv_b', 'lru_w_r', 'lru_b_r', 'lru_w_i', 'lru_b_i', 'lru_lambda', 'w_out', 'norm2_g', 'w_ffn_in', 'w_ffn_out', 'final_g']
TWIN_DIFF_INPUT = 'x'
TWIN_INPUTS = ['x', 'norm1_g', 'w_in', 'gmlp_ln_g', 'gmlp_ln_b', 'gmlp_w_s', 'gmlp_b_s', 'conv_w', 'conv_b', 'lru_w_r', 'lru_b_r', 'lru_w_i', 'lru_b_i', 'lru_lambda', 'w_out', 'norm2_g', 'w_ffn_in', 'w_ffn_out', 'final_g', 'loss_target', 'm_norm1_g', 'm_w_in', 'm_gmlp_ln_g', 'm_gmlp_ln_b', 'm_gmlp_w_s', 'm_gmlp_b_s', 'm_conv_w', 'm_conv_b', 'm_lru_w_r', 'm_lru_b_r', 'm_lru_w_i', 'm_lru_b_i', 'm_lru_lambda', 'm_w_out', 'm_norm2_g', 'm_w_ffn_in', 'm_w_ffn_out', 'm_final_g', 'v_norm1_g', 'v_w_in', 'v_gmlp_ln_g', 'v_gmlp_ln_b', 'v_gmlp_w_s', 'v_gmlp_b_s', 'v_conv_w', 'v_conv_b', 'v_lru_w_r', 'v_lru_b_r', 'v_lru_w_i', 'v_lru_b_i', 'v_lru_lambda', 'v_w_out', 'v_norm2_g', 'v_w_ffn_in', 'v_w_ffn_out', 'v_final_g']
TWIN_OUTPUTS = ['loss', 'grad_x', 'grad_norm1_g', 'grad_w_in', 'grad_gmlp_ln_g', 'grad_gmlp_ln_b', 'grad_gmlp_w_s', 'grad_gmlp_b_s', 'grad_conv_w', 'grad_conv_b', 'grad_lru_w_r', 'grad_lru_b_r', 'grad_lru_w_i', 'grad_lru_b_i', 'grad_lru_lambda', 'grad_w_out', 'grad_norm2_g', 'grad_w_ffn_in', 'grad_w_ffn_out', 'grad_final_g', 'delta_norm1_g', 'delta_w_in', 'delta_gmlp_ln_g', 'delta_gmlp_ln_b', 'delta_gmlp_w_s', 'delta_gmlp_b_s', 'delta_conv_w', 'delta_conv_b', 'delta_lru_w_r', 'delta_lru_b_r', 'delta_lru_w_i', 'delta_lru_b_i', 'delta_lru_lambda', 'delta_w_out', 'delta_norm2_g', 'delta_w_ffn_in', 'delta_w_ffn_out', 'delta_final_g', 'new_m_norm1_g', 'new_m_w_in', 'new_m_gmlp_ln_g', 'new_m_gmlp_ln_b', 'new_m_gmlp_w_s', 'new_m_gmlp_b_s', 'new_m_conv_w', 'new_m_conv_b', 'new_m_lru_w_r', 'new_m_lru_b_r', 'new_m_lru_w_i', 'new_m_lru_b_i', 'new_m_lru_lambda', 'new_m_w_out', 'new_m_norm2_g', 'new_m_w_ffn_in', 'new_m_w_ffn_out', 'new_m_final_g', 'new_v_norm1_g', 'new_v_w_in', 'new_v_gmlp_ln_g', 'new_v_gmlp_ln_b', 'new_v_gmlp_w_s', 'new_v_gmlp_b_s', 'new_v_conv_w', 'new_v_conv_b', 'new_v_lru_w_r', 'new_v_lru_b_r', 'new_v_lru_w_i', 'new_v_lru_b_i', 'new_v_lru_lambda', 'new_v_w_out', 'new_v_norm2_g', 'new_v_w_ffn_in', 'new_v_w_ffn_out', 'new_v_final_g']
TWIN_LEAF_KINDS = {'loss': 'loss', 'grad_x': 'grad_x', 'grad_norm1_g': 'grad_w', 'grad_w_in': 'grad_w', 'grad_gmlp_ln_g': 'grad_w', 'grad_gmlp_ln_b': 'grad_w', 'grad_gmlp_w_s': 'grad_w', 'grad_gmlp_b_s': 'grad_w', 'grad_conv_w': 'grad_w', 'grad_conv_b': 'grad_w', 'grad_lru_w_r': 'grad_w', 'grad_lru_b_r': 'grad_w', 'grad_lru_w_i': 'grad_w', 'grad_lru_b_i': 'grad_w', 'grad_lru_lambda': 'grad_w', 'grad_w_out': 'grad_w', 'grad_norm2_g': 'grad_w', 'grad_w_ffn_in': 'grad_w', 'grad_w_ffn_out': 'grad_w', 'grad_final_g': 'grad_w', 'delta_norm1_g': 'delta_w', 'delta_w_in': 'delta_w', 'delta_gmlp_ln_g': 'delta_w', 'delta_gmlp_ln_b': 'delta_w', 'delta_gmlp_w_s': 'delta_w', 'delta_gmlp_b_s': 'delta_w', 'delta_conv_w': 'delta_w', 'delta_conv_b': 'delta_w', 'delta_lru_w_r': 'delta_w', 'delta_lru_b_r': 'delta_w', 'delta_lru_w_i': 'delta_w', 'delta_lru_b_i': 'delta_w', 'delta_lru_lambda': 'delta_w', 'delta_w_out': 'delta_w', 'delta_norm2_g': 'delta_w', 'delta_w_ffn_in': 'delta_w', 'delta_w_ffn_out': 'delta_w', 'delta_final_g': 'delta_w', 'new_m_norm1_g': 'new_m', 'new_m_w_in': 'new_m', 'new_m_gmlp_ln_g': 'new_m', 'new_m_gmlp_ln_b': 'new_m', 'new_m_gmlp_w_s': 'new_m', 'new_m_gmlp_b_s': 'new_m', 'new_m_conv_w': 'new_m', 'new_m_conv_b': 'new_m', 'new_m_lru_w_r': 'new_m', 'new_m_lru_b_r': 'new_m', 'new_m_lru_w_i': 'new_m', 'new_m_lru_b_i': 'new_m', 'new_m_lru_lambda': 'new_m', 'new_m_w_out': 'new_m', 'new_m_norm2_g': 'new_m', 'new_m_w_ffn_in': 'new_m', 'new_m_w_ffn_out': 'new_m', 'new_m_final_g': 'new_m', 'new_v_norm1_g': 'new_v', 'new_v_w_in': 'new_v', 'new_v_gmlp_ln_g': 'new_v', 'new_v_gmlp_ln_b': 'new_v', 'new_v_gmlp_w_s': 'new_v', 'new_v_gmlp_b_s': 'new_v', 'new_v_conv_w': 'new_v', 'new_v_conv_b': 'new_v', 'new_v_lru_w_r': 'new_v', 'new_v_lru_b_r': 'new_v', 'new_v_lru_w_i': 'new_v', 'new_v_lru_b_i': 'new_v', 'new_v_lru_lambda': 'new_v', 'new_v_w_out': 'new_v', 'new_v_norm2_g': 'new_v', 'new_v_w_ffn_in': 'new_v', 'new_v_w_ffn_out': 'new_v', 'new_v_final_g': 'new_v'}


def _forward(args):
    return _fwd_reference(*[args[k] for k in FWD_PARAMS])


def _output_shape():
    out = _jax.eval_shape(lambda: _forward(_fwd_setup_inputs(0)))
    return out.shape, out.dtype

N_MICROBATCH = 1
ADAM_LR = 0.001
ADAM_B1 = 0.9
ADAM_B2 = 0.999
ADAM_EPS = 1e-08
ADAM_WD = 0.01
ADAM_STEP = 10
PER_EXAMPLE_BATCH_AXIS = {'x': 0, 'loss_target': 0}
SHARED_INPUTS = []
_WEIGHT_DTYPES = {'norm1_g': _jnp.float32, 'w_in': _jnp.float32, 'gmlp_ln_g': _jnp.float32, 'gmlp_ln_b': _jnp.float32, 'gmlp_w_s': _jnp.float32, 'gmlp_b_s': _jnp.float32, 'conv_w': _jnp.float32, 'conv_b': _jnp.float32, 'lru_w_r': _jnp.float32, 'lru_b_r': _jnp.float32, 'lru_w_i': _jnp.float32, 'lru_b_i': _jnp.float32, 'lru_lambda': _jnp.float32, 'w_out': _jnp.float32, 'norm2_g': _jnp.float32, 'w_ffn_in': _jnp.float32, 'w_ffn_out': _jnp.float32, 'final_g': _jnp.float32}
MOMENT_SCALE = {'norm1_g': 1.703977e-01, 'w_in': 6.809453e-02, 'gmlp_ln_g': 5.984364e-02, 'gmlp_ln_b': 5.771352e-02, 'gmlp_w_s': 5.939944e-02, 'gmlp_b_s': 5.807000e-02, 'conv_w': 9.473347e-02, 'conv_b': 1.304925e+00, 'lru_w_r': 1.744977e-02, 'lru_b_r': 1.504813e-02, 'lru_w_i': 3.164618e-02, 'lru_b_i': 1.994414e-02, 'lru_lambda': 3.040608e-02, 'w_out': 1.354692e-01, 'norm2_g': 1.227172e-01, 'w_ffn_in': 5.056061e-02, 'w_ffn_out': 8.247961e-02, 'final_g': 3.210363e+01}


def _to_microbatches(a, axis):
    t = _jnp.moveaxis(a, axis, 0)
    t = t.reshape((N_MICROBATCH, t.shape[0] // N_MICROBATCH) + t.shape[1:])
    return _jnp.moveaxis(t, 1, axis + 1)


def setup_inputs(seed: int = 0) -> dict:
    inp = _fwd_setup_inputs(seed)
    key = _jax.random.fold_in(_jax.random.key(seed), 7919)
    shape, _ = _output_shape()
    out = dict(inp)
    out["loss_target"] = _jax.random.normal(_jax.random.fold_in(key, 0), shape, _jnp.float32)
    for i, name in enumerate(TWIN_WEIGHTS):
        w = inp[name].astype(_jnp.float32)
        if MOMENT_SCALE is None:
            s = _jnp.sqrt(_jnp.mean(_jnp.square(w)) + 1e-30)
        else:
            s = MOMENT_SCALE[name]
        km, kv = _jax.random.split(_jax.random.fold_in(key, i + 1))
        out[name] = w
        out["m_" + name] = s * _jax.random.normal(km, w.shape, _jnp.float32)
        out["v_" + name] = (s * s) * _jax.random.uniform(kv, w.shape, _jnp.float32, 0.5, 1.5)
    if N_MICROBATCH > 1:
        for name, axis in PER_EXAMPLE_BATCH_AXIS.items():
            out[name] = _to_microbatches(out[name], axis)
    return {'x': out['x'], 'norm1_g': out['norm1_g'], 'w_in': out['w_in'], 'gmlp_ln_g': out['gmlp_ln_g'], 'gmlp_ln_b': out['gmlp_ln_b'], 'gmlp_w_s': out['gmlp_w_s'], 'gmlp_b_s': out['gmlp_b_s'], 'conv_w': out['conv_w'], 'conv_b': out['conv_b'], 'lru_w_r': out['lru_w_r'], 'lru_b_r': out['lru_b_r'], 'lru_w_i': out['lru_w_i'], 'lru_b_i': out['lru_b_i'], 'lru_lambda': out['lru_lambda'], 'w_out': out['w_out'], 'norm2_g': out['norm2_g'], 'w_ffn_in': out['w_ffn_in'], 'w_ffn_out': out['w_ffn_out'], 'final_g': out['final_g'], 'loss_target': out['loss_target'], 'm_norm1_g': out['m_norm1_g'], 'm_w_in': out['m_w_in'], 'm_gmlp_ln_g': out['m_gmlp_ln_g'], 'm_gmlp_ln_b': out['m_gmlp_ln_b'], 'm_gmlp_w_s': out['m_gmlp_w_s'], 'm_gmlp_b_s': out['m_gmlp_b_s'], 'm_conv_w': out['m_conv_w'], 'm_conv_b': out['m_conv_b'], 'm_lru_w_r': out['m_lru_w_r'], 'm_lru_b_r': out['m_lru_b_r'], 'm_lru_w_i': out['m_lru_w_i'], 'm_lru_b_i': out['m_lru_b_i'], 'm_lru_lambda': out['m_lru_lambda'], 'm_w_out': out['m_w_out'], 'm_norm2_g': out['m_norm2_g'], 'm_w_ffn_in': out['m_w_ffn_in'], 'm_w_ffn_out': out['m_w_ffn_out'], 'm_final_g': out['m_final_g'], 'v_norm1_g': out['v_norm1_g'], 'v_w_in': out['v_w_in'], 'v_gmlp_ln_g': out['v_gmlp_ln_g'], 'v_gmlp_ln_b': out['v_gmlp_ln_b'], 'v_gmlp_w_s': out['v_gmlp_w_s'], 'v_gmlp_b_s': out['v_gmlp_b_s'], 'v_conv_w': out['v_conv_w'], 'v_conv_b': out['v_conv_b'], 'v_lru_w_r': out['v_lru_w_r'], 'v_lru_b_r': out['v_lru_b_r'], 'v_lru_w_i': out['v_lru_w_i'], 'v_lru_b_i': out['v_lru_b_i'], 'v_lru_lambda': out['v_lru_lambda'], 'v_w_out': out['v_w_out'], 'v_norm2_g': out['v_norm2_g'], 'v_w_ffn_in': out['v_w_ffn_in'], 'v_w_ffn_out': out['v_w_ffn_out'], 'v_final_g': out['v_final_g']}


def _loss(weights, diff, rest, loss_target):
    with _jax.named_scope("forward"):
        args = {**rest, TWIN_DIFF_INPUT: diff, **{k: w.astype(_WEIGHT_DTYPES[k]) for k, w in weights.items()}}
        y = _forward(args)
    with _jax.named_scope("loss_head"):
        err = _jnp.square(y.astype(_jnp.float32) - loss_target)
        return 0.5 * _jnp.sum(_jnp.mean(err, axis=-1)) if err.ndim else 0.5 * err


def _adamw(w, g, m, v):
    m = ADAM_B1 * m + (1.0 - ADAM_B1) * g
    v = ADAM_B2 * v + (1.0 - ADAM_B2) * _jnp.square(g)
    m_hat = m / (1.0 - ADAM_B1 ** ADAM_STEP)
    v_hat = v / (1.0 - ADAM_B2 ** ADAM_STEP)
    delta = -ADAM_LR * (m_hat / (_jnp.sqrt(v_hat) + ADAM_EPS) + ADAM_WD * w)
    return delta, m, v


def reference(x, norm1_g, w_in, gmlp_ln_g, gmlp_ln_b, gmlp_w_s, gmlp_b_s, conv_w, conv_b, lru_w_r, lru_b_r, lru_w_i, lru_b_i, lru_lambda, w_out, norm2_g, w_ffn_in, w_ffn_out, final_g, loss_target, m_norm1_g, m_w_in, m_gmlp_ln_g, m_gmlp_ln_b, m_gmlp_w_s, m_gmlp_b_s, m_conv_w, m_conv_b, m_lru_w_r, m_lru_b_r, m_lru_w_i, m_lru_b_i, m_lru_lambda, m_w_out, m_norm2_g, m_w_ffn_in, m_w_ffn_out, m_final_g, v_norm1_g, v_w_in, v_gmlp_ln_g, v_gmlp_ln_b, v_gmlp_w_s, v_gmlp_b_s, v_conv_w, v_conv_b, v_lru_w_r, v_lru_b_r, v_lru_w_i, v_lru_b_i, v_lru_lambda, v_w_out, v_norm2_g, v_w_ffn_in, v_w_ffn_out, v_final_g):
    given = dict(x=x, norm1_g=norm1_g, w_in=w_in, gmlp_ln_g=gmlp_ln_g, gmlp_ln_b=gmlp_ln_b, gmlp_w_s=gmlp_w_s, gmlp_b_s=gmlp_b_s, conv_w=conv_w, conv_b=conv_b, lru_w_r=lru_w_r, lru_b_r=lru_b_r, lru_w_i=lru_w_i, lru_b_i=lru_b_i, lru_lambda=lru_lambda, w_out=w_out, norm2_g=norm2_g, w_ffn_in=w_ffn_in, w_ffn_out=w_ffn_out, final_g=final_g, loss_target=loss_target, m_norm1_g=m_norm1_g, m_w_in=m_w_in, m_gmlp_ln_g=m_gmlp_ln_g, m_gmlp_ln_b=m_gmlp_ln_b, m_gmlp_w_s=m_gmlp_w_s, m_gmlp_b_s=m_gmlp_b_s, m_conv_w=m_conv_w, m_conv_b=m_conv_b, m_lru_w_r=m_lru_w_r, m_lru_b_r=m_lru_b_r, m_lru_w_i=m_lru_w_i, m_lru_b_i=m_lru_b_i, m_lru_lambda=m_lru_lambda, m_w_out=m_w_out, m_norm2_g=m_norm2_g, m_w_ffn_in=m_w_ffn_in, m_w_ffn_out=m_w_ffn_out, m_final_g=m_final_g, v_norm1_g=v_norm1_g, v_w_in=v_w_in, v_gmlp_ln_g=v_gmlp_ln_g, v_gmlp_ln_b=v_gmlp_ln_b, v_gmlp_w_s=v_gmlp_w_s, v_gmlp_b_s=v_gmlp_b_s, v_conv_w=v_conv_w, v_conv_b=v_conv_b, v_lru_w_r=v_lru_w_r, v_lru_b_r=v_lru_b_r, v_lru_w_i=v_lru_w_i, v_lru_b_i=v_lru_b_i, v_lru_lambda=v_lru_lambda, v_w_out=v_w_out, v_norm2_g=v_norm2_g, v_w_ffn_in=v_w_ffn_in, v_w_ffn_out=v_w_ffn_out, v_final_g=v_final_g)
    weights = {n: given[n] for n in TWIN_WEIGHTS}
    shared = {n: given[n] for n in SHARED_INPUTS}
    per_example = {n: given[n] for n in ['x']}
    grad_fn = _jax.value_and_grad(_loss, argnums=(0, 1))

    def one_microbatch(ex, loss_target):
        ex = dict(ex)
        diff = ex.pop(TWIN_DIFF_INPUT)
        return grad_fn(weights, diff, {**shared, **ex}, loss_target)

    if N_MICROBATCH == 1:
        loss, (grad_w, grad_x) = one_microbatch(per_example, given["loss_target"])
    else:
        def body(carry, xs):
            loss_sum, grad_sum = carry
            l_k, (gw_k, gx_k) = one_microbatch(xs[0], xs[1])
            with _jax.named_scope("update"):
                return (loss_sum + l_k, _jax.tree.map(_jnp.add, grad_sum, gw_k)), gx_k

        init = (_jnp.zeros((), _jnp.float32), _jax.tree.map(_jnp.zeros_like, weights))
        (loss, grad_w), grad_x = _jax.lax.scan(body, init, (per_example, given["loss_target"]))
    with _jax.named_scope("update"):
        delta_w, new_m, new_v = {}, {}, {}
        for n in TWIN_WEIGHTS:
            delta_w[n], new_m[n], new_v[n] = _adamw(weights[n], grad_w[n], given["m_" + n], given["v_" + n])
    return (loss, grad_x, *[grad_w[n] for n in TWIN_WEIGHTS], *[delta_w[n] for n in TWIN_WEIGHTS],
            *[new_m[n] for n in TWIN_WEIGHTS], *[new_v[n] for n in TWIN_WEIGHTS])
```

```python
import functools

import jax
import jax.numpy as jnp
from jax import lax
from jax.experimental import pallas as pl
from jax.experimental.pallas import tpu as pltpu

F32 = jnp.float32
BF16 = jnp.bfloat16
MESH = pl.DeviceIdType.MESH

D = 1024
NH = 8
HD = 128
CHUNK = 128
N_IN_T = 12
DFF = 2816
DFF_SH = 1408
EPS = 1e-6
LRU_C = 8.0
ADAM_LR, ADAM_B1, ADAM_B2, ADAM_EPS, ADAM_WD, ADAM_STEP = 0.001, 0.9, 0.999, 1e-08, 0.01, 10

TM = 512
RT = 128
PADR = 8
VMEM_LIMIT = 56 * 1024 * 1024


def _cp(sem=None, **kw):
    if sem is not None:
        kw["dimension_semantics"] = sem
    return pltpu.CompilerParams(vmem_limit_bytes=VMEM_LIMIT, **kw)


_GC = 0.7978845608028654


def _sigmoid(x):
    return 1.0 / (1.0 + jnp.exp(-x))


def _gelu(x):
    return 0.5 * x * (1.0 + jnp.tanh(_GC * (x + 0.044715 * x * x * x)))


def _gelu_and_grad(x):
    t = jnp.tanh(_GC * (x + 0.044715 * x * x * x))
    g = 0.5 * x * (1.0 + t)
    dg = 0.5 * (1.0 + t) + 0.5 * x * (1.0 - t * t) * _GC * (1.0 + 3 * 0.044715 * x * x)
    return g, dg


def _softplus_neg(lam):
    y = jnp.exp(-jnp.abs(lam))
    u = 1.0 + y
    l1p = jnp.where(u == 1.0, y, jnp.log(u) * y / (u - 1.0))
    return jnp.maximum(-lam, 0.0) + l1p


def _dot(a, b):
    return jnp.dot(a, b, preferred_element_type=F32)


def _dot_nt(a, b):
    return lax.dot_general(a, b, (((1,), (1,)), ((), ())), preferred_element_type=F32)


def _dot_tn(a, b):
    return lax.dot_general(a, b, (((0,), (0,)), ((), ())), preferred_element_type=F32)


def _rms_hat(x):
    r = lax.rsqrt(jnp.mean(x * x, axis=-1, keepdims=True) + EPS)
    return x * r, r


def _rms_bwd(dh, x, g):
    xh, r = _rms_hat(x)
    dxh = dh * g
    dx = r * (dxh - xh * jnp.mean(dxh * xh, axis=-1, keepdims=True))
    return dx, jnp.sum(dh * xh, axis=0, keepdims=True)


def _in_tile(j):
    m, hf = j // 2, j % 2
    orig = jnp.where(m < 2, m, jnp.where(m == 2, 4, jnp.where(m < 5, m - 1, 5)))
    t = orig * 2 + hf
    return t // 3, t % 3


def _mm_in(x, g, w_in, l):
    S = x.shape[0]

    def body(x_ref, g_ref, w_ref, o_ref, hb):
        @pl.when(pl.program_id(1) == 0)
        def _():
            xh, _ = _rms_hat(x_ref[...])
            hb[...] = (xh * g_ref[...]).astype(BF16)
        o_ref[...] = _dot(hb[...], w_ref[...])

    def w_map(i, j):
        sh, tl = _in_tile(j)
        return (l, sh, 0, tl)

    return pl.pallas_call(
        body, name=f"mm_in_{l}", grid=(S // TM, N_IN_T),
        in_specs=[pl.BlockSpec((TM, D), lambda i, j: (i, 0)), pl.BlockSpec((1, D), lambda i, j: (0, 0)),
                  pl.BlockSpec((None, None, D, 512), w_map)],
        out_specs=pl.BlockSpec((None, TM, 512), lambda i, j: (j // 2, i, j % 2)),
        out_shape=jax.ShapeDtypeStruct((6, S, D), F32),
        scratch_shapes=[pltpu.VMEM((TM, D), BF16)],
        compiler_params=_cp(("parallel", "arbitrary")),
    )(x, g, w_in)


def _mm_res(a, w, res, l, name):
    S, K = a.shape

    def body(a_ref, w_ref, r_ref, o_ref):
        o_ref[...] = r_ref[...] + _dot(a_ref[...], w_ref[...])

    return pl.pallas_call(
        body, name=f"{name}_{l}", grid=(S // TM,),
        in_specs=[pl.BlockSpec((TM, K), lambda i: (i, 0)), pl.BlockSpec((None, K, D), lambda i: (l, 0, 0)),
                  pl.BlockSpec((TM, D), lambda i: (i, 0))],
        out_specs=pl.BlockSpec((TM, D), lambda i: (i, 0)),
        out_shape=jax.ShapeDtypeStruct((S, D), F32),
        compiler_params=_cp(("parallel",)),
    )(a, w, res)


def _mm_ffn_in(x, g, w_fi, l):
    S = x.shape[0]

    def body(x_ref, g_ref, w_ref, gu_ref, ff_ref, hb, gbuf):
        k = pl.program_id(1)

        @pl.when(k == 0)
        def _():
            xh, _ = _rms_hat(x_ref[...])
            hb[...] = (xh * g_ref[...]).astype(BF16)
        acc = _dot(hb[...], w_ref[...])
        gu_ref[...] = acc

        @pl.when(k < 2)
        def _():
            gbuf[k] = acc

        @pl.when(k >= 2)
        def _():
            ga = gbuf[k - 2]
            ff_ref[...] = (ga * _sigmoid(ga) * acc).astype(BF16)

    return pl.pallas_call(
        body, name=f"mm_ffn_in_{l}", grid=(S // TM, 4),
        in_specs=[pl.BlockSpec((TM, D), lambda i, k: (i, 0)), pl.BlockSpec((1, D), lambda i, k: (0, 0)),
                  pl.BlockSpec((None, None, D, DFF_SH), lambda i, k: (l, k, 0, 0))],
        out_specs=[pl.BlockSpec((None, TM, DFF_SH), lambda i, k: (k, i, 0)),
                   pl.BlockSpec((TM, DFF_SH), lambda i, k: (i, jnp.maximum(k - 2, 0)))],
        out_shape=[jax.ShapeDtypeStruct((4, S, DFF_SH), F32), jax.ShapeDtypeStruct((S, DFF), BF16)],
        scratch_shapes=[pltpu.VMEM((TM, D), BF16), pltpu.VMEM((2, TM, DFF_SH), F32)],
        compiler_params=_cp(("parallel", "arbitrary")),
    )(x, g, w_fi)


def _gmlp_fwd(z6, ws_b, bs_b, lg, lb):
    S = z6.shape[1]

    def body(z_ref, ws_ref, bs_ref, lg_ref, lb_ref, o_ref, mix):
        gv = _gelu(z_ref[1])
        xc = gv - jnp.mean(gv, axis=-1, keepdims=True)
        rs = lax.rsqrt(jnp.mean(xc * xc, axis=-1, keepdims=True) + EPS)
        vb = (xc * rs * lg_ref[...] + lb_ref[...]).astype(BF16)
        for gi in range(NH):
            cs = slice(gi * HD, (gi + 1) * HD)
            mix[:, cs] = _dot(ws_ref[gi], vb[:, cs])
        o_ref[...] = _sigmoid(z_ref[2]) * _gelu(z_ref[0]) * (mix[...] + bs_ref[...])

    return pl.pallas_call(
        body, name="gmlp_fwd", grid=(S // CHUNK,),
        in_specs=[pl.BlockSpec((3, CHUNK, D), lambda i: (0, i, 0)), pl.BlockSpec((NH, CHUNK, CHUNK), lambda i: (0, 0, 0)),
                  pl.BlockSpec((CHUNK, D), lambda i: (0, 0)), pl.BlockSpec((1, D), lambda i: (0, 0)),
                  pl.BlockSpec((1, D), lambda i: (0, 0))],
        out_specs=pl.BlockSpec((CHUNK, D), lambda i: (i, 0)),
        out_shape=jax.ShapeDtypeStruct((S, D), F32),
        scratch_shapes=[pltpu.VMEM((CHUNK, D), F32)],
        compiler_params=_cp(("parallel",)),
    )(z6, ws_b, bs_b, lg, lb)


def _row_iota():
    return lax.broadcasted_iota(jnp.int32, (RT, HD), 0)


def _scan_up(a, b):
    row = _row_iota()
    d = 1
    while d < RT:
        m = row >= d
        b = jnp.where(m, b + a * pltpu.roll(b, d, 0), b)
        a = jnp.where(m, a * pltpu.roll(a, d, 0), a)
        d *= 2
    return a, b


def _scan_down(a, b):
    row = _row_iota()
    d = 1
    while d < RT:
        m = row < RT - d
        b = jnp.where(m, b + a * pltpu.roll(b, RT - d, 0), b)
        a = jnp.where(m, a * pltpu.roll(a, RT - d, 0), a)
        d *= 2
    return a, b


def _lru_gates(xc, d, wr_ref, br_ref, wi_ref, bi_ref, sp):
    xb = xc.astype(BF16)
    r = _sigmoid(_dot(xb, wr_ref[d]) + br_ref[d:d + 1, :])
    i = _sigmoid(_dot(xb, wi_ref[d]) + bi_ref[d:d + 1, :])
    log_a = -LRU_C * r * sp[d:d + 1, :]
    a = jnp.exp(log_a)
    mult = jnp.sqrt(jnp.maximum(-jnp.tanh(log_a) * (a * a + 1.0), 0.0))
    return r, i, a, mult


def _shifted(win, k):
    w = RT + 2 * PADR
    v = win if k == 0 else pltpu.roll(win, (-k) % w, 0)
    return v[PADR:PADR + RT]


def _conv_taps(win):
    return [_shifted(win, k) for k in (-1, 0, 1, 2)]


def _fill_padded(dst, src_ref, S):
    zeros = jnp.zeros((PADR, HD), F32)
    dst[0:PADR, :] = zeros
    dst[PADR + S:2 * PADR + S, :] = zeros

    def cp(i, c):
        t0 = pl.multiple_of(i * RT, RT)
        dst[pl.ds(t0 + PADR, RT), :] = src_ref[pl.ds(t0, RT), :]
        return c
    lax.fori_loop(0, S // RT, cp, 0)


def _conv_fwd_all(zxp, xc_s, cw_ref, cb_ref, S):
    def cv(i, c):
        t0 = pl.multiple_of(i * RT, RT)
        xm1, x0, xp1, xp2 = _conv_taps(zxp[pl.ds(t0, RT + 2 * PADR), :])
        xc_s[pl.ds(t0, RT), :] = (cb_ref[...] + xm1 * cw_ref[0:1, :] + x0 * cw_ref[1:2, :]
                                  + xp1 * cw_ref[2:3, :] + xp2 * cw_ref[3:4, :])
        return c
    lax.fori_loop(0, S // RT, cv, 0)


def _lru_specs(S):
    head = lambda h: (0, h)
    return [pl.BlockSpec((4, HD), head), pl.BlockSpec((1, HD), head),
            pl.BlockSpec((2, None, HD, HD), lambda h: (0, h, 0, 0)), pl.BlockSpec((2, HD), head),
            pl.BlockSpec((2, None, HD, HD), lambda h: (0, h, 0, 0)), pl.BlockSpec((2, HD), head),
            pl.BlockSpec((2, HD), head)]


def _lru_fwd(z6, ya, cw, cb, wr, br, wi, bi, lam):
    S = z6.shape[1]
    nt = S // RT

    def body(z_ref, ya_ref, cw_ref, cb_ref, wr_ref, br_ref, wi_ref, bi_ref, lam_ref, mg_ref, h0_ref, h1_ref, zxp, xc_s):
        sp = _softplus_neg(lam_ref[...])
        _fill_padded(zxp, z_ref.at[0], S)
        _conv_fwd_all(zxp, xc_s, cw_ref, cb_ref, S)

        def up(i, carry):
            t0 = pl.multiple_of(i * RT, RT)
            xc = xc_s[pl.ds(t0, RT), :]
            _, gi, a, mult = _lru_gates(xc, 0, wr_ref, br_ref, wi_ref, bi_ref, sp)
            pa, hb = _scan_up(a, mult * gi * xc)
            h = hb + pa * carry
            h0_ref[pl.ds(t0, RT), :] = h
            return h[RT - 1:RT, :]
        lax.fori_loop(0, nt, up, jnp.zeros((1, HD), F32))

        def down(i, carry):
            t0 = pl.multiple_of((nt - 1 - i) * RT, RT)
            rows = pl.ds(t0, RT)
            xc = xc_s[rows, :]
            _, gi, a, mult = _lru_gates(xc, 1, wr_ref, br_ref, wi_ref, bi_ref, sp)
            pa, hb = _scan_down(a, mult * gi * xc)
            h = hb + pa * carry
            h1_ref[rows, :] = h
            yb = (h0_ref[rows, :] + h) * _gelu(z_ref[1, rows, :])
            mg_ref[rows, :] = (ya_ref[rows, :] + _sigmoid(z_ref[2, rows, :]) * yb).astype(BF16)
            return h[0:1, :]
        lax.fori_loop(0, nt, down, jnp.zeros((1, HD), F32))

    col = pl.BlockSpec((S, HD), lambda h: (0, h))
    return pl.pallas_call(
        body, name="lru_fwd", grid=(NH,),
        in_specs=[pl.BlockSpec((3, S, HD), lambda h: (1, 0, h)), col] + _lru_specs(S),
        out_specs=[col, col, col],
        out_shape=[jax.ShapeDtypeStruct((S, D), BF16), jax.ShapeDtypeStruct((S, D), F32), jax.ShapeDtypeStruct((S, D), F32)],
        scratch_shapes=[pltpu.VMEM((S + 2 * PADR, HD), F32), pltpu.VMEM((S, HD), F32)],
        compiler_params=_cp(("parallel",)),
    )(z6, ya, cw, cb, wr, br, wi, bi, lam)


def _loss_head(x, tgt, g):
    S = x.shape[0]

    def body(x_ref, t_ref, g_ref, dx_ref, loss_ref, dg_ref):
        @pl.when(pl.program_id(0) == 0)
        def _():
            loss_ref[...] = jnp.zeros_like(loss_ref)
            dg_ref[...] = jnp.zeros_like(dg_ref)
        xv = x_ref[...]
        xh, _ = _rms_hat(xv)
        e = xh * g_ref[...] - t_ref[...]
        loss_ref[...] += jnp.sum(e * e) * (0.5 / D)
        dx, dgs = _rms_bwd(e * (1.0 / D), xv, g_ref[...])
        dx_ref[...] = dx
        dg_ref[...] += dgs

    return pl.pallas_call(
        body, name="loss_head", grid=(S // TM,),
        in_specs=[pl.BlockSpec((TM, D), lambda i: (i, 0)), pl.BlockSpec((TM, D), lambda i: (i, 0)),
                  pl.BlockSpec((1, D), lambda i: (0, 0))],
        out_specs=[pl.BlockSpec((TM, D), lambda i: (i, 0)), pl.BlockSpec((1, 128), lambda i: (0, 0)),
                   pl.BlockSpec((1, D), lambda i: (0, 0))],
        out_shape=[jax.ShapeDtypeStruct((S, D), F32), jax.ShapeDtypeStruct((1, 128), F32), jax.ShapeDtypeStruct((1, D), F32)],
        compiler_params=_cp(("arbitrary",)),
    )(x, tgt, g)


def _bwd_ffn_out(dx, w_fo, gu, l):
    S = dx.shape[0]

    def body(dx_ref, w_ref, ga_ref, gb_ref, o_ref, dxb, dff):
        k = pl.program_id(1)

        @pl.when(k == 0)
        def _():
            dxb[...] = dx_ref[...].astype(BF16)

        @pl.when(k < 2)
        def _():
            d = _dot_nt(dxb[...], w_ref[...])
            dff[k] = d
            ga = ga_ref[...]
            sg = _sigmoid(ga)
            o_ref[...] = (d * gb_ref[...] * sg * (1.0 + ga * (1.0 - sg))).astype(BF16)

        @pl.when(k >= 2)
        def _():
            ga = ga_ref[...]
            o_ref[...] = (dff[k - 2] * ga * _sigmoid(ga)).astype(BF16)

    return pl.pallas_call(
        body, name=f"bwd_ffn_out_{l}", grid=(S // TM, 4),
        in_specs=[pl.BlockSpec((TM, D), lambda i, k: (i, 0)), pl.BlockSpec((None, DFF_SH, D), lambda i, k: (l, k % 2, 0)),
                  pl.BlockSpec((None, TM, DFF_SH), lambda i, k: (k % 2, i, 0)),
                  pl.BlockSpec((None, TM, DFF_SH), lambda i, k: (k % 2 + 2, i, 0))],
        out_specs=pl.BlockSpec((None, TM, DFF_SH), lambda i, k: (k, i, 0)),
        out_shape=jax.ShapeDtypeStruct((4, S, DFF_SH), BF16),
        scratch_shapes=[pltpu.VMEM((TM, D), BF16), pltpu.VMEM((2, TM, DFF_SH), F32)],
        compiler_params=_cp(("parallel", "arbitrary")),
    )(dx, w_fo, gu, gu)


def _mm_tn(a, b, m_blk, name):
    S, M = a.shape

    def body(a_ref, b_ref, o_ref):
        @pl.when(pl.program_id(1) == 0)
        def _():
            o_ref[...] = jnp.zeros_like(o_ref)
        o_ref[...] += _dot_tn(a_ref[...], b_ref[...].astype(BF16))

    return pl.pallas_call(
        body, name=name, grid=(M // m_blk, S // TM),
        in_specs=[pl.BlockSpec((TM, m_blk), lambda m, k: (k, m)), pl.BlockSpec((TM, D), lambda m, k: (k, 0))],
        out_specs=pl.BlockSpec((m_blk, D), lambda m, k: (m, 0)),
        out_shape=jax.ShapeDtypeStruct((M, D), F32),
        compiler_params=_cp(("parallel", "arbitrary")),
    )(a, b)


def _mm_nt_rms_bwd(a, a_spec, w, w_spec, nk, kw, x, g, dres, name):
    S = x.shape[0]

    def body(a_ref, w_ref, x_ref, g_ref, r_ref, dx_ref, dg_ref, acc):
        i, k = pl.program_id(0), pl.program_id(1)

        @pl.when(k == 0)
        def _():
            acc[...] = jnp.zeros_like(acc)
        acc[...] += _dot_nt(a_ref[...], w_ref[...])

        @pl.when(jnp.logical_and(i == 0, k == 0))
        def _():
            dg_ref[...] = jnp.zeros_like(dg_ref)

        @pl.when(k == nk - 1)
        def _():
            dx, dgs = _rms_bwd(acc[...], x_ref[...], g_ref[...])
            dx_ref[...] = r_ref[...] + dx
            dg_ref[...] += dgs

    row = pl.BlockSpec((TM, D), lambda i, k: (i, 0))
    vec = pl.BlockSpec((1, D), lambda i, k: (0, 0))
    return pl.pallas_call(
        body, name=name, grid=(S // TM, nk),
        in_specs=[a_spec, w_spec, row, vec, row],
        out_specs=[row, vec],
        out_shape=[jax.ShapeDtypeStruct((S, D), F32), jax.ShapeDtypeStruct((1, D), F32)],
        scratch_shapes=[pltpu.VMEM((TM, D), F32)],
        compiler_params=_cp(("arbitrary", "arbitrary")),
    )(a, w, x, g, dres)


def _mm_tn_norm(x, g, b, b_spec, out_spec, out_shape, nj, name):
    S = x.shape[0]

    def body(x_ref, g_ref, b_ref, o_ref):
        @pl.when(pl.program_id(1) == 0)
        def _():
            o_ref[...] = jnp.zeros_like(o_ref)
        xh, _ = _rms_hat(x_ref[...])
        o_ref[...] += _dot_tn((xh * g_ref[...]).astype(BF16), b_ref[...])

    return pl.pallas_call(
        body, name=name, grid=(nj, S // TM),
        in_specs=[pl.BlockSpec((TM, D), lambda j, k: (k, 0)), pl.BlockSpec((1, D), lambda j, k: (0, 0)), b_spec],
        out_specs=out_spec, out_shape=out_shape,
        compiler_params=_cp(("parallel", "arbitrary")),
    )(x, g, b)


def _bwd_out(dx, w_o, merged, l):
    S = dx.shape[0]

    def body(dx_ref, w_ref, m_ref, dm_ref, dw_ref):
        @pl.when(pl.program_id(0) == 0)
        def _():
            dw_ref[...] = jnp.zeros_like(dw_ref)
        dxb = dx_ref[...].astype(BF16)
        dm_ref[...] = _dot_nt(dxb, w_ref[...])
        dw_ref[...] += _dot_tn(m_ref[...], dxb)

    row = pl.BlockSpec((TM, D), lambda i: (i, 0))
    return pl.pallas_call(
        body, name=f"bwd_out_{l}", grid=(S // TM,),
        in_specs=[row, pl.BlockSpec((None, D, D), lambda i: (l, 0, 0)), row],
        out_specs=[row, pl.BlockSpec((D, D), lambda i: (0, 0))],
        out_shape=[jax.ShapeDtypeStruct((S, D), F32), jax.ShapeDtypeStruct((D, D), F32)],
        compiler_params=_cp(("arbitrary",)),
    )(dx, w_o, merged)


def _gmlp_bwd(dm, z6, ws_b, wst_b, bs_b, lg, lb):
    S = z6.shape[1]

    def body(dm_ref, z_ref, ws_ref, wst_ref, bs_ref, lg_ref, lb_ref, dz_ref, dws_ref, dbs_ref, dlg_ref, dlb_ref, mix, dv):
        @pl.when(pl.program_id(0) == 0)
        def _():
            dws_ref[...] = jnp.zeros_like(dws_ref)
            dbs_ref[...] = jnp.zeros_like(dbs_ref)
            dlg_ref[...] = jnp.zeros_like(dlg_ref)
            dlb_ref[...] = jnp.zeros_like(dlb_ref)
        gv, dgelu_v = _gelu_and_grad(z_ref[1])
        xc = gv - jnp.mean(gv, axis=-1, keepdims=True)
        rs = lax.rsqrt(jnp.mean(xc * xc, axis=-1, keepdims=True) + EPS)
        vh = xc * rs
        vb = (vh * lg_ref[...] + lb_ref[...]).astype(BF16)
        for gi in range(NH):
            cs = slice(gi * HD, (gi + 1) * HD)
            mix[:, cs] = _dot(ws_ref[gi], vb[:, cs])
        u, dgelu_u = _gelu_and_grad(z_ref[0])
        sa = _sigmoid(z_ref[2])
        mixed = mix[...] + bs_ref[...]
        dyg = dm_ref[...]
        dz_ref[2] = (dyg * u * mixed * sa * (1.0 - sa)).astype(BF16)
        dya = dyg * sa
        dz_ref[0] = (dya * mixed * dgelu_u).astype(BF16)
        dmix = dya * u
        dmb = dmix.astype(BF16)
        for gi in range(NH):
            cs = slice(gi * HD, (gi + 1) * HD)
            dv[:, cs] = _dot(wst_ref[gi], dmb[:, cs])
            dws_ref[gi] += _dot_nt(dmb[:, cs], vb[:, cs])
            dbs_ref[gi] += jnp.broadcast_to(jnp.sum(dmix[:, cs], axis=1, keepdims=True), (CHUNK, HD))
        dvv = dv[...]
        dlg_ref[...] += jnp.sum(dvv * vh, axis=0, keepdims=True)
        dlb_ref[...] += jnp.sum(dvv, axis=0, keepdims=True)
        dvh = dvv * lg_ref[...]
        dgv = rs * (dvh - jnp.mean(dvh, axis=-1, keepdims=True) - vh * jnp.mean(dvh * vh, axis=-1, keepdims=True))
        dz_ref[1] = (dgv * dgelu_v).astype(BF16)

    vec = pl.BlockSpec((1, D), lambda i: (0, 0))
    mat = pl.BlockSpec((NH, CHUNK, CHUNK), lambda i: (0, 0, 0))
    return pl.pallas_call(
        body, name="gmlp_bwd", grid=(S // CHUNK,),
        in_specs=[pl.BlockSpec((CHUNK, D), lambda i: (i, 0)), pl.BlockSpec((3, CHUNK, D), lambda i: (0, i, 0)), mat, mat,
                  pl.BlockSpec((CHUNK, D), lambda i: (0, 0)), vec, vec],
        out_specs=[pl.BlockSpec((3, CHUNK, D), lambda i: (0, i, 0)), mat, mat, vec, vec],
        out_shape=[jax.ShapeDtypeStruct((6, S, D), BF16), jax.ShapeDtypeStruct((NH, CHUNK, CHUNK), F32),
                   jax.ShapeDtypeStruct((NH, CHUNK, HD), F32), jax.ShapeDtypeStruct((1, D), F32), jax.ShapeDtypeStruct((1, D), F32)],
        scratch_shapes=[pltpu.VMEM((CHUNK, D), F32), pltpu.VMEM((CHUNK, D), F32)],
        compiler_params=_cp(("arbitrary",)),
    )(dm, z6, ws_b, wst_b, bs_b, lg, lb)


def _lru_bwd(dz6, dm, z6, h0, h1, cw, cb, wr, br, wi, bi, lam):
    S = z6.shape[1]
    nt = S // RT

    def body(dz_in, dm_ref, z_ref, h0_ref, h1_ref, cw_ref, cb_ref, wr_ref, br_ref, wi_ref, bi_ref, lam_ref,
             dz_ref, dcw_ref, dcb_ref, dwr_ref, dbr_ref, dwi_ref, dbi_ref, dlam_ref, zxp, xc_s, dhs_s, dxcp):
        del dz_in
        lam = lam_ref[...]
        sp = _softplus_neg(lam)
        row = _row_iota()
        _fill_padded(zxp, z_ref.at[0], S)
        _conv_fwd_all(zxp, xc_s, cw_ref, cb_ref, S)
        zeros = jnp.zeros((PADR, HD), F32)
        dxcp[0:PADR, :] = zeros
        dxcp[PADR + S:2 * PADR + S, :] = zeros
        dwr_ref[...] = jnp.zeros_like(dwr_ref)
        dwi_ref[...] = jnp.zeros_like(dwi_ref)

        def gate_bwd(d, t0, lamv, da, xc, first):
            r, gi, a, mult = _lru_gates(xc, d, wr_ref, br_ref, wi_ref, bi_ref, sp)
            dmult = lamv * gi * xc
            dgi = lamv * mult * xc
            dlog = (da - dmult * a / mult) * a
            dpr = (dlog * (-LRU_C) * sp[d:d + 1, :]) * r * (1.0 - r)
            dpi = dgi * gi * (1.0 - gi)
            xb, dprb, dpib = xc.astype(BF16), dpr.astype(BF16), dpi.astype(BF16)
            dwr_ref[d] += _dot_tn(xb, dprb)
            dwi_ref[d] += _dot_tn(xb, dpib)
            dxc = lamv * mult * gi + _dot_nt(dprb, wr_ref[d]) + _dot_nt(dpib, wi_ref[d])
            rows = pl.ds(t0 + PADR, RT)
            if first:
                dxcp[rows, :] = dxc
            else:
                dxcp[rows, :] += dxc
            return (jnp.sum(dlog * r, axis=0, keepdims=True) * (-LRU_C), jnp.sum(dpr, axis=0, keepdims=True),
                    jnp.sum(dpi, axis=0, keepdims=True))

        def down(i, carry):
            q_next, s_sp, s_br, s_bi = carry
            t0 = pl.multiple_of((nt - 1 - i) * RT, RT)
            rows = pl.ds(t0, RT)
            xc = xc_s[rows, :]
            h0t, h1t = h0_ref[rows, :], h1_ref[rows, :]
            hs = h0t + h1t
            dmv = dm_ref[rows, :]
            sb = _sigmoid(z_ref[2, rows, :])
            gg, dgg = _gelu_and_grad(z_ref[1, rows, :])
            dz_ref[2, rows, :] = (dmv * hs * gg * sb * (1.0 - sb)).astype(BF16)
            dyb = dmv * sb
            dz_ref[1, rows, :] = (dyb * hs * dgg).astype(BF16)
            dhs = dyb * gg
            dhs_s[rows, :] = dhs
            _, _, a, _ = _lru_gates(xc, 0, wr_ref, br_ref, wi_ref, bi_ref, sp)
            pa, qb = _scan_down(a, a * dhs)
            q = qb + pa * q_next
            lamv = dhs + jnp.where(row == RT - 1, q_next, pltpu.roll(q, RT - 1, 0))
            tp = pl.multiple_of(jnp.maximum(t0 - PADR, 0), PADR)
            prev = jnp.where(t0 > 0, h0_ref[pl.ds(tp, PADR), :][PADR - 1:PADR, :], 0.0)
            hprev = jnp.where(row == 0, prev, pltpu.roll(h0t, 1, 0))
            g_sp, g_br, g_bi = gate_bwd(0, t0, lamv, lamv * hprev, xc, True)
            return q[0:1, :], s_sp + g_sp, s_br + g_br, s_bi + g_bi

        z1 = jnp.zeros((1, HD), F32)
        _, s_sp0, s_br0, s_bi0 = lax.fori_loop(0, nt, down, (z1, z1, z1, z1))

        def up(i, carry):
            q_prev, s_sp, s_br, s_bi = carry
            t0 = pl.multiple_of(i * RT, RT)
            rows = pl.ds(t0, RT)
            xc = xc_s[rows, :]
            h1t = h1_ref[rows, :]
            dhs = dhs_s[rows, :]
            _, _, a, _ = _lru_gates(xc, 1, wr_ref, br_ref, wi_ref, bi_ref, sp)
            pa, qb = _scan_up(a, a * dhs)
            q = qb + pa * q_prev
            lamv = dhs + jnp.where(row == 0, q_prev, pltpu.roll(q, 1, 0))
            tn = pl.multiple_of(jnp.minimum(t0 + RT, S - PADR), PADR)
            nxt = jnp.where(t0 + RT < S, h1_ref[pl.ds(tn, PADR), :][0:1, :], 0.0)
            hnext = jnp.where(row == RT - 1, nxt, pltpu.roll(h1t, RT - 1, 0))
            g_sp, g_br, g_bi = gate_bwd(1, t0, lamv, lamv * hnext, xc, False)
            return q[RT - 1:RT, :], s_sp + g_sp, s_br + g_br, s_bi + g_bi

        _, s_sp1, s_br1, s_bi1 = lax.fori_loop(0, nt, up, (z1, z1, z1, z1))

        dsp = jnp.concatenate([s_sp0, s_sp1], axis=0)
        dlam_ref[...] = -dsp * _sigmoid(-lam)
        dbr_ref[...] = jnp.concatenate([s_br0, s_br1], axis=0)
        dbi_ref[...] = jnp.concatenate([s_bi0, s_bi1], axis=0)

        def conv_bwd(i, carry):
            c0, c1, c2, c3, cb_ = carry
            t0 = pl.multiple_of(i * RT, RT)
            dwin = dxcp[pl.ds(t0, RT + 2 * PADR), :]
            d0 = _shifted(dwin, 0)
            dz_ref[0, pl.ds(t0, RT), :] = (_shifted(dwin, 1) * cw_ref[0:1, :] + d0 * cw_ref[1:2, :]
                                           + _shifted(dwin, -1) * cw_ref[2:3, :] + _shifted(dwin, -2) * cw_ref[3:4, :]).astype(BF16)
            xm1, x0, xp1, xp2 = _conv_taps(zxp[pl.ds(t0, RT + 2 * PADR), :])
            sm = lambda v: jnp.sum(v, axis=0, keepdims=True)
            return c0 + sm(d0 * xm1), c1 + sm(d0 * x0), c2 + sm(d0 * xp1), c3 + sm(d0 * xp2), cb_ + sm(d0)

        c0, c1, c2, c3, cb_ = lax.fori_loop(0, nt, conv_bwd, (z1, z1, z1, z1, z1))
        dcw_ref[...] = jnp.concatenate([c0, c1, c2, c3], axis=0)
        dcb_ref[...] = cb_

    col = pl.BlockSpec((S, HD), lambda h: (0, h))
    head = lambda h: (0, h)
    wspec = pl.BlockSpec((2, None, HD, HD), lambda h: (0, h, 0, 0))
    return pl.pallas_call(
        body, name="lru_bwd", grid=(NH,),
        in_specs=[pl.BlockSpec(memory_space=pl.ANY), col, pl.BlockSpec((3, S, HD), lambda h: (1, 0, h)), col, col] + _lru_specs(S),
        out_specs=[pl.BlockSpec((3, S, HD), lambda h: (1, 0, h)), pl.BlockSpec((4, HD), head), pl.BlockSpec((1, HD), head),
                   wspec, pl.BlockSpec((2, HD), head), wspec, pl.BlockSpec((2, HD), head), pl.BlockSpec((2, HD), head)],
        out_shape=[jax.ShapeDtypeStruct((6, S, D), BF16), jax.ShapeDtypeStruct((4, D), F32), jax.ShapeDtypeStruct((1, D), F32),
                   jax.ShapeDtypeStruct((2, NH, HD, HD), F32), jax.ShapeDtypeStruct((2, D), F32),
                   jax.ShapeDtypeStruct((2, NH, HD, HD), F32), jax.ShapeDtypeStruct((2, D), F32), jax.ShapeDtypeStruct((2, D), F32)],
        scratch_shapes=[pltpu.VMEM((S + 2 * PADR, HD), F32), pltpu.VMEM((S, HD), F32), pltpu.VMEM((S, HD), F32),
                        pltpu.VMEM((S + 2 * PADR, HD), F32)],
        input_output_aliases={0: 0},
        compiler_params=_cp(("parallel",)),
    )(dz6, dm, z6, h0, h1, cw, cb, wr, br, wi, bi, lam)


def _local_step(x, tgt, p):
    S = x.shape[0]
    saved = []
    for l in range(2):
        g1, g2 = p["norm1_g"][l][None], p["norm2_g"][l][None]
        ws_b = p["gmlp_w_s"][l].astype(BF16)
        tm = dict(ws_b=ws_b, wst_b=jnp.swapaxes(ws_b, 1, 2), bs_b=jnp.repeat(p["gmlp_b_s"][l].T, HD, axis=1),
                  lg=p["gmlp_ln_g"][l][None], lb=p["gmlp_ln_b"][l][None])
        lru = (p["conv_w"][l], p["conv_b"][l][None], p["lru_w_r"][l].astype(BF16), p["lru_b_r"][l],
               p["lru_w_i"][l].astype(BF16), p["lru_b_i"][l], p["lru_lambda"][l])
        z6 = _mm_in(x, g1, p["w_in"], l)
        ya = _gmlp_fwd(z6, tm["ws_b"], tm["bs_b"], tm["lg"], tm["lb"])
        merged, h0, h1 = _lru_fwd(z6, ya, *lru)
        x1 = _mm_res(merged, p["w_out"], x, l, "mm_out")
        gu, ff = _mm_ffn_in(x1, g2, p["w_ffn_in"], l)
        x2 = _mm_res(ff, p["w_ffn_out"], x1, l, "mm_ffn_out")
        saved.append(dict(x=x, z6=z6, h0=h0, h1=h1, merged=merged, x1=x1, gu=gu, ff=ff, g1=g1, g2=g2, tm=tm, lru=lru))
        x = x2

    dx, loss_v, dfg = _loss_head(x, tgt, p["final_g"][None])
    big = [None, None]
    small = {"final_g": dfg[0]}
    per_layer = {k: [None, None] for k in ("norm1_g", "gmlp_ln_g", "gmlp_ln_b", "gmlp_w_s", "gmlp_b_s", "conv_w", "conv_b",
                                           "lru_w_r", "lru_b_r", "lru_w_i", "lru_b_i", "lru_lambda", "norm2_g")}
    for l in (1, 0):
        s = saved[l]
        tm = s["tm"]
        dgu = _bwd_ffn_out(dx, p["w_ffn_out"], s["gu"], l)
        dwfo = _mm_tn(s["ff"], dx, DFF_SH, f"dw_ffn_out_{l}")
        dx1, dg2 = _mm_nt_rms_bwd(
            dgu, pl.BlockSpec((None, TM, DFF_SH), lambda i, k: (k, i, 0)),
            p["w_ffn_in"], pl.BlockSpec((None, None, D, DFF_SH), lambda i, k, l=l: (l, k, 0, 0)),
            4, DFF_SH, s["x1"], s["g2"], dx, f"bwd_ffn_in_{l}")
        dwfi = _mm_tn_norm(
            s["x1"], s["g2"], dgu, pl.BlockSpec((None, TM, DFF_SH), lambda j, k: (j, k, 0)),
            pl.BlockSpec((None, D, DFF_SH), lambda j, k: (j, 0, 0)), jax.ShapeDtypeStruct((4, D, DFF_SH), F32), 4, f"dw_ffn_in_{l}")
        dmg, dwo = _bwd_out(dx1, p["w_out"], s["merged"], l)
        dz6, dws, dbs, dlg, dlb = _gmlp_bwd(dmg, s["z6"], tm["ws_b"], tm["wst_b"], tm["bs_b"], tm["lg"], tm["lb"])
        dz6, dcw, dcb, dwr, dbr, dwi, dbi, dlam = _lru_bwd(dz6, dmg, s["z6"], s["h0"], s["h1"], *s["lru"])

        def w_map(i, k, l=l):
            sh, tl = _in_tile(k)
            return (l, sh, 0, tl)

        def dw_map(j, k):
            sh, tl = _in_tile(j)
            return (sh, 0, tl)

        dx0, dg1 = _mm_nt_rms_bwd(
            dz6, pl.BlockSpec((None, TM, 512), lambda i, k: (k // 2, i, k % 2)),
            p["w_in"], pl.BlockSpec((None, None, D, 512), w_map),
            N_IN_T, 512, s["x"], s["g1"], dx1, f"bwd_in_{l}")
        dwin = _mm_tn_norm(
            s["x"], s["g1"], dz6, pl.BlockSpec((None, TM, 512), lambda j, k: (j // 2, k, j % 2)),
            pl.BlockSpec((None, D, 512), dw_map), jax.ShapeDtypeStruct((4, D, 1536), F32), N_IN_T, f"dw_in_{l}")
        dx = dx0
        big[l] = [dwin, dwo, dwfi, dwfo]
        for k, v in (("norm1_g", dg1[0]), ("gmlp_ln_g", dlg[0]), ("gmlp_ln_b", dlb[0]), ("gmlp_w_s", dws), ("gmlp_b_s", dbs[:, :, 0]),
                     ("conv_w", dcw), ("conv_b", dcb[0]), ("lru_w_r", dwr), ("lru_b_r", dbr), ("lru_w_i", dwi), ("lru_b_i", dbi),
                     ("lru_lambda", dlam), ("norm2_g", dg2[0])):
            per_layer[k][l] = v
    for k, v in per_layer.items():
        small[k] = jnp.stack(v)
    return loss_v, dx, big, small


def _place():
    x, y, c = lax.axis_index("x"), lax.axis_index("y"), lax.axis_index("c")
    return x, y, c, 2 * x + y


def _chip_at(x, y, d):
    px = 1 - x if d & 2 else x
    py = 1 - y if d & 1 else y
    return px, py, 2 * px + py


ANY = pl.BlockSpec(memory_space=pl.ANY)


def _gather_weights(shards, tiny):
    nt = len(shards)
    n_ici = nt * 2 * 3

    def body(*refs):
        s_refs, tiny_ref = refs[:nt], refs[nt]
        o_refs, tiny_o = refs[nt + 1:2 * nt + 1], refs[2 * nt + 1]
        send, recv, fsend, frecv, tsend, trecv, lsem = refs[2 * nt + 2:]
        x, y, c, chip = _place()

        local = [pltpu.make_async_copy(s_refs[t].at[l], o_refs[t].at[l, chip], lsem.at[2 * t + l])
                 for t in range(nt) for l in range(2)]
        local.append(pltpu.make_async_copy(tiny_ref, tiny_o.at[chip], lsem.at[2 * nt]))
        for cp in local:
            cp.start()

        def ici(t, l, d, origin_chip, to):
            k = (2 * t + l) * 3 + d - 1
            return pltpu.make_async_remote_copy(
                src_ref=s_refs[t].at[l, c], dst_ref=o_refs[t].at[l, origin_chip, c],
                send_sem=send.at[k], recv_sem=recv.at[k], device_id=to, device_id_type=MESH)

        def fwd(t, l, d, origin_chip, half):
            k = (2 * t + l) * 3 + d - 1
            blk = o_refs[t].at[l, origin_chip, half]
            return pltpu.make_async_remote_copy(
                src_ref=blk, dst_ref=blk, send_sem=fsend.at[k], recv_sem=frecv.at[k],
                device_id=(x, y, 1 - c), device_id_type=MESH)

        def tin(d, origin_chip, to):
            return pltpu.make_async_remote_copy(
                src_ref=tiny_ref, dst_ref=tiny_o.at[origin_chip], send_sem=tsend.at[d - 1], recv_sem=trecv.at[d - 1],
                device_id=to, device_id_type=MESH)

        sends = []
        for t in range(nt):
            for l in range(2):
                for d in (1, 2, 3):
                    px, py, _ = _chip_at(x, y, d)
                    sends.append(ici(t, l, d, chip, (px, py, c)))
        for d in (1, 2, 3):
            px, py, _ = _chip_at(x, y, d)
            sends.append(tin(d, chip, (px, py, c)))
        for cp in sends:
            cp.start()
        passed = []
        for t in range(nt):
            for l in range(2):
                for d in (1, 2, 3):
                    _, _, pchip = _chip_at(x, y, d)
                    ici(t, l, d, pchip, (x, y, c)).wait_recv()
                    f = fwd(t, l, d, pchip, c)
                    f.start()
                    passed.append(f)
        for t in range(nt):
            for l in range(2):
                for d in (1, 2, 3):
                    _, _, pchip = _chip_at(x, y, d)
                    fwd(t, l, d, pchip, 1 - c).wait_recv()
        for d in (1, 2, 3):
            _, _, pchip = _chip_at(x, y, d)
            tin(d, pchip, (x, y, c)).wait_recv()
        for cp in sends + passed:
            cp.wait_send()
        for cp in local:
            cp.wait()

    out_shape = [jax.ShapeDtypeStruct((2, 4) + s.shape[1:], s.dtype) for s in shards]
    out_shape.append(jax.ShapeDtypeStruct((4,) + tiny.shape, tiny.dtype))
    outs = pl.pallas_call(
        body, name="gather_weights", out_shape=out_shape,
        in_specs=[ANY] * (nt + 1), out_specs=[ANY] * (nt + 1),
        scratch_shapes=[pltpu.SemaphoreType.DMA((n_ici,)), pltpu.SemaphoreType.DMA((n_ici,)),
                        pltpu.SemaphoreType.DMA((n_ici,)), pltpu.SemaphoreType.DMA((n_ici,)),
                        pltpu.SemaphoreType.DMA((3,)), pltpu.SemaphoreType.DMA((3,)), pltpu.SemaphoreType.DMA((2 * nt + 1,))],
        compiler_params=_cp(has_side_effects=True),
    )(*shards, tiny)
    return outs[:nt], outs[nt]


def _to_sibling_halves(gs, l):
    nt = len(gs)

    def body(*refs):
        g_refs, o_refs = refs[:nt], refs[nt:2 * nt]
        send, recv = refs[2 * nt:]
        x, y, c, _ = _place()
        cps = [pltpu.make_async_remote_copy(
            src_ref=g_refs[t].at[k, 1 - c], dst_ref=o_refs[t].at[k], send_sem=send.at[4 * t + k], recv_sem=recv.at[4 * t + k],
            device_id=(x, y, 1 - c), device_id_type=MESH) for t in range(nt) for k in range(4)]
        for cp in cps:
            cp.start()
        for cp in cps:
            cp.wait()

    return pl.pallas_call(
        body, name=f"grads_to_sibling_{l}", out_shape=[jax.ShapeDtypeStruct((4,) + g.shape[2:], g.dtype) for g in gs],
        in_specs=[ANY] * nt, out_specs=[ANY] * nt,
        scratch_shapes=[pltpu.SemaphoreType.DMA((4 * nt,)), pltpu.SemaphoreType.DMA((4 * nt,))],
        compiler_params=_cp(has_side_effects=True),
    )(*gs)


def _to_chips(cs, l):
    nt = len(cs)

    def body(*refs):
        c_refs, o_refs = refs[:nt], refs[nt:2 * nt]
        send, recv = refs[2 * nt:]
        x, y, c, _ = _place()
        cps = []
        for t in range(nt):
            for d in (1, 2, 3):
                px, py, pchip = _chip_at(x, y, d)
                cps.append(pltpu.make_async_remote_copy(
                    src_ref=c_refs[t].at[pchip], dst_ref=o_refs[t].at[d - 1], send_sem=send.at[3 * t + d - 1],
                    recv_sem=recv.at[3 * t + d - 1], device_id=(px, py, c), device_id_type=MESH))
        for cp in cps:
            cp.start()
        for cp in cps:
            cp.wait()

    return pl.pallas_call(
        body, name=f"grads_to_chips_{l}", out_shape=[jax.ShapeDtypeStruct((3,) + a.shape[1:], a.dtype) for a in cs],
        in_specs=[ANY] * nt, out_specs=[ANY] * nt,
        scratch_shapes=[pltpu.SemaphoreType.DMA((3 * nt,)), pltpu.SemaphoreType.DMA((3 * nt,))],
        compiler_params=_cp(has_side_effects=True),
    )(*cs)


def _join_halves(ts, l):
    nt = len(ts)

    def body(*refs):
        t_refs, o_refs = refs[:nt], refs[nt:2 * nt]
        send, recv, lsem = refs[2 * nt:]
        x, y, c, _ = _place()
        loc = [pltpu.make_async_copy(t_refs[t], o_refs[t].at[c], lsem.at[t]) for t in range(nt)]
        cps = [pltpu.make_async_remote_copy(
            src_ref=t_refs[t], dst_ref=o_refs[t].at[c], send_sem=send.at[t], recv_sem=recv.at[t],
            device_id=(x, y, 1 - c), device_id_type=MESH) for t in range(nt)]
        for cp in loc + cps:
            cp.start()
        for cp in cps:
            cp.wait()
        for cp in loc:
            cp.wait()

    return pl.pallas_call(
        body, name=f"grads_join_{l}", out_shape=[jax.ShapeDtypeStruct((2,) + a.shape, a.dtype) for a in ts],
        in_specs=[ANY] * nt, out_specs=[ANY] * nt,
        scratch_shapes=[pltpu.SemaphoreType.DMA((nt,)), pltpu.SemaphoreType.DMA((nt,)), pltpu.SemaphoreType.DMA((nt,))],
        compiler_params=_cp(has_side_effects=True),
    )(*ts)


def _add_half(g, r, c_arr, name):
    _, _, rh, cols = g.shape

    def body(c_ref, g_ref, r_ref, o_ref):
        o_ref[...] = g_ref[...] + r_ref[...]

    blk = pl.BlockSpec((None, rh, cols), lambda k, cr: (k, 0, 0))
    return pl.pallas_call(
        body, name=name, out_shape=jax.ShapeDtypeStruct((4, rh, cols), F32),
        grid_spec=pltpu.PrefetchScalarGridSpec(
            num_scalar_prefetch=1, grid=(4,),
            in_specs=[pl.BlockSpec((None, None, rh, cols), lambda k, cr: (k, cr[0], 0, 0)), blk], out_specs=blk),
        compiler_params=_cp(("parallel",)),
    )(c_arr, g, r)


def _sum_chips(cs, r3, chip_arr, name):
    _, rh, cols = cs.shape
    rb = rh // 4

    def body(chip_ref, a_ref, r0_ref, r1_ref, r2_ref, o_ref):
        o_ref[...] = ((a_ref[...] + r0_ref[...]) + r1_ref[...]) + r2_ref[...]

    def slot(d):
        return pl.BlockSpec((None, rb, cols), lambda i, ch: (d, i, 0))

    return pl.pallas_call(
        body, name=name, out_shape=jax.ShapeDtypeStruct((rh, cols), F32),
        grid_spec=pltpu.PrefetchScalarGridSpec(
            num_scalar_prefetch=1, grid=(4,),
            in_specs=[pl.BlockSpec((None, rb, cols), lambda i, ch: (ch[0], i, 0)), slot(0), slot(1), slot(2)],
            out_specs=pl.BlockSpec((rb, cols), lambda i, ch: (i, 0))),
        compiler_params=_cp(("parallel",)),
    )(chip_arr, cs, r3, r3, r3)


def _allreduce_small(pack):
    rows = pack.shape[0]
    hr = rows // 2

    def body(p_ref, o_ref, sib, slots, s1, r1, s2, r2, s3, r3):
        x, y, c, chip = _place()
        sibling = (x, y, 1 - c)
        ex = pltpu.make_async_remote_copy(src_ref=p_ref, dst_ref=sib, send_sem=s1, recv_sem=r1,
                                          device_id=sibling, device_id_type=MESH)
        ex.start()
        ex.wait()
        half = pl.ds(pl.multiple_of(c * hr, 8), hr)
        slots[0] = p_ref[half, :] + sib[half, :]
        cps = []
        for d in (1, 2, 3):
            px, py, _ = _chip_at(x, y, d)
            cps.append(pltpu.make_async_remote_copy(
                src_ref=slots.at[0], dst_ref=slots.at[d], send_sem=s2.at[d - 1], recv_sem=r2.at[d - 1],
                device_id=(px, py, c), device_id_type=MESH))
        for cp in cps:
            cp.start()
        for cp in cps:
            cp.wait()
        tot = slots[chip]
        for k in (1, 2, 3):
            tot = tot + slots[jnp.bitwise_xor(chip, k)]
        o_ref[half, :] = tot
        back = pltpu.make_async_remote_copy(src_ref=o_ref.at[half, :], dst_ref=o_ref.at[half, :], send_sem=s3, recv_sem=r3,
                                            device_id=sibling, device_id_type=MESH)
        back.start()
        back.wait()

    vm = pl.BlockSpec(memory_space=pltpu.VMEM)
    return pl.pallas_call(
        body, name="allreduce_small", out_shape=jax.ShapeDtypeStruct((rows, 128), F32),
        in_specs=[vm], out_specs=vm,
        scratch_shapes=[pltpu.VMEM((rows, 128), F32), pltpu.VMEM((4, hr, 128), F32),
                        pltpu.SemaphoreType.DMA, pltpu.SemaphoreType.DMA, pltpu.SemaphoreType.DMA((3,)), pltpu.SemaphoreType.DMA((3,)),
                        pltpu.SemaphoreType.DMA, pltpu.SemaphoreType.DMA],
        compiler_params=_cp(has_side_effects=True),
    )(pack)


def _adam(g, w, m, v, name):
    rows, cols = g.shape
    rb = rows // 4

    def body(g_ref, w_ref, m_ref, v_ref, d_ref, m2_ref, v2_ref):
        gv = g_ref[...]
        m2 = ADAM_B1 * m_ref[...] + (1.0 - ADAM_B1) * gv
        v2 = ADAM_B2 * v_ref[...] + (1.0 - ADAM_B2) * (gv * gv)
        m_hat = m2 / (1.0 - ADAM_B1 ** ADAM_STEP)
        v_hat = v2 / (1.0 - ADAM_B2 ** ADAM_STEP)
        d_ref[...] = -ADAM_LR * (m_hat / (jnp.sqrt(v_hat) + ADAM_EPS) + ADAM_WD * w_ref[...])
        m2_ref[...] = m2
        v2_ref[...] = v2

    blk = pl.BlockSpec((rb, cols), lambda i: (i, 0))
    shp = jax.ShapeDtypeStruct((rows, cols), F32)
    return pl.pallas_call(
        body, name=name, grid=(4,), in_specs=[blk] * 4, out_specs=[blk] * 3, out_shape=[shp] * 3,
        compiler_params=_cp(("parallel",)),
    )(g, w, m, v)


def _pack(arrs, mult):
    flat = jnp.concatenate([a.reshape(-1) for a in arrs])
    rows = -(-flat.shape[0] // (128 * mult)) * mult
    return jnp.pad(flat, (0, rows * 128 - flat.shape[0])).reshape(rows, 128)


def _unpack(pack, shapes):
    flat = pack.reshape(-1)
    out, o = [], 0
    for s in shapes:
        n = 1
        for e in s:
            n *= e
        out.append(flat[o:o + n].reshape(s))
        o += n
    return out


WEIGHTS = ['norm1_g', 'w_in', 'gmlp_ln_g', 'gmlp_ln_b', 'gmlp_w_s', 'gmlp_b_s', 'conv_w', 'conv_b', 'lru_w_r', 'lru_b_r', 'lru_w_i',
           'lru_b_i', 'lru_lambda', 'w_out', 'norm2_g', 'w_ffn_in', 'w_ffn_out', 'final_g']
BIG = ['w_in', 'w_out', 'w_ffn_in', 'w_ffn_out']
SMALL = [n for n in WEIGHTS if n not in BIG]
CHIP_SHARDED_SMALL = ['conv_w', 'lru_b_r', 'lru_b_i', 'lru_lambda']


def kernel(x, norm1_g, w_in, gmlp_ln_g, gmlp_ln_b, gmlp_w_s, gmlp_b_s, conv_w, conv_b, lru_w_r, lru_b_r, lru_w_i, lru_b_i, lru_lambda, w_out, norm2_g, w_ffn_in, w_ffn_out, final_g, loss_target, m_norm1_g, m_w_in, m_gmlp_ln_g, m_gmlp_ln_b, m_gmlp_w_s, m_gmlp_b_s, m_conv_w, m_conv_b, m_lru_w_r, m_lru_b_r, m_lru_w_i, m_lru_b_i, m_lru_lambda, m_w_out, m_norm2_g, m_w_ffn_in, m_w_ffn_out, m_final_g, v_norm1_g, v_w_in, v_gmlp_ln_g, v_gmlp_ln_b, v_gmlp_w_s, v_gmlp_b_s, v_conv_w, v_conv_b, v_lru_w_r, v_lru_b_r, v_lru_w_i, v_lru_b_i, v_lru_lambda, v_w_out, v_norm2_g, v_w_ffn_in, v_w_ffn_out, v_final_g):
    a = dict(locals())
    w = {n: a[n] for n in WEIGHTS}
    mom = {n: a["m_" + n] for n in WEIGHTS}
    var = {n: a["v_" + n] for n in WEIGHTS}
    _, _, c, chip = _place()
    c_arr, chip_arr = jnp.reshape(c, (1,)).astype(jnp.int32), jnp.reshape(chip, (1,)).astype(jnp.int32)

    shards = [w[n].astype(BF16).reshape(2, 2, w[n].shape[1] // 2, w[n].shape[2]) for n in BIG]
    tiny = _pack([w[n] for n in CHIP_SHARDED_SMALL], 8)
    full, tiny_full = _gather_weights(shards, tiny)
    p = {n: w[n] for n in SMALL}
    for n, f in zip(BIG, full):
        p[n] = f.reshape(2, 4, 2 * f.shape[3], f.shape[4])
    p["w_out"] = p["w_out"].reshape(2, D, D)
    p["w_ffn_out"] = p["w_ffn_out"].reshape(2, DFF, D)
    parts = [_unpack(tiny_full[k], [w[n].shape for n in CHIP_SHARDED_SMALL]) for k in range(4)]
    for i, n in enumerate(CHIP_SHARDED_SMALL):
        p[n] = jnp.concatenate([parts[k][i] for k in range(4)], axis=-1)

    loss_v, dx, big, small = _local_step(x[0], loss_target[0], p)
    loss = lax.psum(loss_v[0, 0], ("x", "y", "c"))

    out = {}
    res = {n: [None, None] for n in BIG}
    for l in (1, 0):
        gs = [g.reshape(4, 2, -1, g.shape[-1]) for g in big[l]]
        from_sib = _to_sibling_halves(gs, l)
        cs = [_add_half(g, r, c_arr, f"add_half_{n}_{l}") for n, g, r in zip(BIG, gs, from_sib)]
        from_chips = _to_chips(cs, l)
        ts = [_sum_chips(cc, r3, chip_arr, f"sum_chips_{n}_{l}") for n, cc, r3 in zip(BIG, cs, from_chips)]
        joined = _join_halves(ts, l)
        for n, j in zip(BIG, joined):
            g = j.reshape(w[n].shape[1:])
            res[n][l] = (g,) + tuple(_adam(g, w[n][l], mom[n][l], var[n][l], f"adam_{n}_{l}"))
    for n in BIG:
        out[n] = [jnp.stack([res[n][0][i], res[n][1][i]]) for i in range(4)]

    full_shapes = [small[n].shape for n in SMALL]
    red = _unpack(_allreduce_small(_pack([small[n] for n in SMALL], 16)), full_shapes)
    g_small = []
    for n, g in zip(SMALL, red):
        if n in CHIP_SHARDED_SMALL:
            g = lax.dynamic_slice_in_dim(g, chip * w[n].shape[-1], w[n].shape[-1], axis=g.ndim - 1)
        g_small.append(g)
    shapes = [w[n].shape for n in SMALL]
    packs = [_pack(lst, 32) for lst in (g_small, [w[n] for n in SMALL], [mom[n] for n in SMALL], [var[n] for n in SMALL])]
    upd = [_unpack(u, shapes) for u in _adam(*packs, "adam_small")]
    for i, n in enumerate(SMALL):
        out[n] = [g_small[i], upd[0][i], upd[1][i], upd[2][i]]

    return (loss, dx[None]) + tuple(out[n][i] for i in range(4) for n in WEIGHTS)
```

```python
import functools

import jax
import jax.numpy as jnp
from jax import lax
from jax.experimental import pallas as pl
from jax.experimental.pallas import tpu as pltpu

F32 = jnp.float32
BF16 = jnp.bfloat16
MESH = pl.DeviceIdType.MESH

D = 1024
NH = 8
HD = 128
CHUNK = 128
N_IN_T = 12
DFF = 2816
DFF_SH = 1408
EPS = 1e-6
LRU_C = 8.0
ADAM_LR, ADAM_B1, ADAM_B2, ADAM_EPS, ADAM_WD, ADAM_STEP = 0.001, 0.9, 0.999, 1e-08, 0.01, 10

TM = 512
RT = 128
PADR = 8
VMEM_LIMIT = 56 * 1024 * 1024


def _cp(sem=None, **kw):
    if sem is not None:
        kw["dimension_semantics"] = sem
    return pltpu.CompilerParams(vmem_limit_bytes=VMEM_LIMIT, **kw)


_GC = 0.7978845608028654


def _sigmoid(x):
    return 1.0 / (1.0 + jnp.exp(-x))


def _gelu(x):
    return 0.5 * x * (1.0 + jnp.tanh(_GC * (x + 0.044715 * x * x * x)))


def _gelu_and_grad(x):
    t = jnp.tanh(_GC * (x + 0.044715 * x * x * x))
    g = 0.5 * x * (1.0 + t)
    dg = 0.5 * (1.0 + t) + 0.5 * x * (1.0 - t * t) * _GC * (1.0 + 3 * 0.044715 * x * x)
    return g, dg


def _softplus_neg(lam):
    y = jnp.exp(-jnp.abs(lam))
    u = 1.0 + y
    l1p = jnp.where(u == 1.0, y, jnp.log(u) * y / (u - 1.0))
    return jnp.maximum(-lam, 0.0) + l1p


def _dot(a, b):
    return jnp.dot(a, b, preferred_element_type=F32)


def _dot_nt(a, b):
    return lax.dot_general(a, b, (((1,), (1,)), ((), ())), preferred_element_type=F32)


def _dot_tn(a, b):
    return lax.dot_general(a, b, (((0,), (0,)), ((), ())), preferred_element_type=F32)


def _rms_hat(x):
    r = lax.rsqrt(jnp.mean(x * x, axis=-1, keepdims=True) + EPS)
    return x * r, r


def _rms_bwd(dh, x, g):
    xh, r = _rms_hat(x)
    dxh = dh * g
    dx = r * (dxh - xh * jnp.mean(dxh * xh, axis=-1, keepdims=True))
    return dx, jnp.sum(dh * xh, axis=0, keepdims=True)


def _in_tile(j):
    m, hf = j // 2, j % 2
    orig = jnp.where(m < 2, m, jnp.where(m == 2, 4, jnp.where(m < 5, m - 1, 5)))
    t = orig * 2 + hf
    return t // 3, t % 3


def _mm_in(x, g, w_in, l):
    S = x.shape[0]

    def body(x_ref, g_ref, w_ref, o_ref, hb):
        @pl.when(pl.program_id(1) == 0)
        def _():
            xh, _ = _rms_hat(x_ref[...])
            hb[...] = (xh * g_ref[...]).astype(BF16)
        o_ref[...] = _dot(hb[...], w_ref[...])

    def w_map(i, j):
        sh, tl = _in_tile(j)
        return (l, sh, 0, tl)

    return pl.pallas_call(
        body, name=f"mm_in_{l}", grid=(S // TM, N_IN_T),
        in_specs=[pl.BlockSpec((TM, D), lambda i, j: (i, 0)), pl.BlockSpec((1, D), lambda i, j: (0, 0)),
                  pl.BlockSpec((None, None, D, 512), w_map)],
        out_specs=pl.BlockSpec((None, TM, 512), lambda i, j: (j // 2, i, j % 2)),
        out_shape=jax.ShapeDtypeStruct((6, S, D), F32),
        scratch_shapes=[pltpu.VMEM((TM, D), BF16)],
        compiler_params=_cp(("parallel", "arbitrary")),
    )(x, g, w_in)


def _mm_res(a, w, res, l, name):
    S, K = a.shape

    def body(a_ref, w_ref, r_ref, o_ref):
        o_ref[...] = r_ref[...] + _dot(a_ref[...], w_ref[...])

    return pl.pallas_call(
        body, name=f"{name}_{l}", grid=(S // TM,),
        in_specs=[pl.BlockSpec((TM, K), lambda i: (i, 0)), pl.BlockSpec((None, K, D), lambda i: (l, 0, 0)),
                  pl.BlockSpec((TM, D), lambda i: (i, 0))],
        out_specs=pl.BlockSpec((TM, D), lambda i: (i, 0)),
        out_shape=jax.ShapeDtypeStruct((S, D), F32),
        compiler_params=_cp(("parallel",)),
    )(a, w, res)


def _mm_ffn_in(x, g, w_fi, l):
    S = x.shape[0]

    def body(x_ref, g_ref, w_ref, gu_ref, ff_ref, hb, gbuf):
        k = pl.program_id(1)

        @pl.when(k == 0)
        def _():
            xh, _ = _rms_hat(x_ref[...])
            hb[...] = (xh * g_ref[...]).astype(BF16)
        acc = _dot(hb[...], w_ref[...])
        gu_ref[...] = acc

        @pl.when(k < 2)
        def _():
            gbuf[k] = acc

        @pl.when(k >= 2)
        def _():
            ga = gbuf[k - 2]
            ff_ref[...] = (ga * _sigmoid(ga) * acc).astype(BF16)

    return pl.pallas_call(
        body, name=f"mm_ffn_in_{l}", grid=(S // TM, 4),
        in_specs=[pl.BlockSpec((TM, D), lambda i, k: (i, 0)), pl.BlockSpec((1, D), lambda i, k: (0, 0)),
                  pl.BlockSpec((None, None, D, DFF_SH), lambda i, k: (l, k, 0, 0))],
        out_specs=[pl.BlockSpec((None, TM, DFF_SH), lambda i, k: (k, i, 0)),
                   pl.BlockSpec((TM, DFF_SH), lambda i, k: (i, jnp.maximum(k - 2, 0)))],
        out_shape=[jax.ShapeDtypeStruct((4, S, DFF_SH), F32), jax.ShapeDtypeStruct((S, DFF), BF16)],
        scratch_shapes=[pltpu.VMEM((TM, D), BF16), pltpu.VMEM((2, TM, DFF_SH), F32)],
        compiler_params=_cp(("parallel", "arbitrary")),
    )(x, g, w_fi)


def _gmlp_fwd(z6, ws_b, bs_b, lg, lb):
    S = z6.shape[1]

    def body(z_ref, ws_ref, bs_ref, lg_ref, lb_ref, o_ref, mix):
        gv = _gelu(z_ref[1])
        xc = gv - jnp.mean(gv, axis=-1, keepdims=True)
        rs = lax.rsqrt(jnp.mean(xc * xc, axis=-1, keepdims=True) + EPS)
        vb = (xc * rs * lg_ref[...] + lb_ref[...]).astype(BF16)
        for gi in range(NH):
            cs = slice(gi * HD, (gi + 1) * HD)
            mix[:, cs] = _dot(ws_ref[gi], vb[:, cs])
        o_ref[...] = _sigmoid(z_ref[2]) * _gelu(z_ref[0]) * (mix[...] + bs_ref[...])

    return pl.pallas_call(
        body, name="gmlp_fwd", grid=(S // CHUNK,),
        in_specs=[pl.BlockSpec((3, CHUNK, D), lambda i: (0, i, 0)), pl.BlockSpec((NH, CHUNK, CHUNK), lambda i: (0, 0, 0)),
                  pl.BlockSpec((CHUNK, D), lambda i: (0, 0)), pl.BlockSpec((1, D), lambda i: (0, 0)),
                  pl.BlockSpec((1, D), lambda i: (0, 0))],
        out_specs=pl.BlockSpec((CHUNK, D), lambda i: (i, 0)),
        out_shape=jax.ShapeDtypeStruct((S, D), F32),
        scratch_shapes=[pltpu.VMEM((CHUNK, D), F32)],
        compiler_params=_cp(("parallel",)),
    )(z6, ws_b, bs_b, lg, lb)


def _row_iota():
    return lax.broadcasted_iota(jnp.int32, (RT, HD), 0)


def _scan_up(a, b):
    row = _row_iota()
    d = 1
    while d < RT:
        m = row >= d
        b = jnp.where(m, b + a * pltpu.roll(b, d, 0), b)
        a = jnp.where(m, a * pltpu.roll(a, d, 0), a)
        d *= 2
    return a, b


def _scan_down(a, b):
    row = _row_iota()
    d = 1
    while d < RT:
        m = row < RT - d
        b = jnp.where(m, b + a * pltpu.roll(b, RT - d, 0), b)
        a = jnp.where(m, a * pltpu.roll(a, RT - d, 0), a)
        d *= 2
    return a, b


def _lru_gates(xc, d, wr_ref, br_ref, wi_ref, bi_ref, sp):
    xb = xc.astype(BF16)
    r = _sigmoid(_dot(xb, wr_ref[d]) + br_ref[d:d + 1, :])
    i = _sigmoid(_dot(xb, wi_ref[d]) + bi_ref[d:d + 1, :])
    log_a = -LRU_C * r * sp[d:d + 1, :]
    a = jnp.exp(log_a)
    mult = jnp.sqrt(jnp.maximum(-jnp.tanh(log_a) * (a * a + 1.0), 0.0))
    return r, i, a, mult


def _shifted(win, k):
    w = RT + 2 * PADR
    v = win if k == 0 else pltpu.roll(win, (-k) % w, 0)
    return v[PADR:PADR + RT]


def _conv_taps(win):
    return [_shifted(win, k) for k in (-1, 0, 1, 2)]


def _fill_padded(dst, src_ref, S):
    zeros = jnp.zeros((PADR, HD), F32)
    dst[0:PADR, :] = zeros
    dst[PADR + S:2 * PADR + S, :] = zeros

    def cp(i, c):
        t0 = pl.multiple_of(i * RT, RT)
        dst[pl.ds(t0 + PADR, RT), :] = src_ref[pl.ds(t0, RT), :]
        return c
    lax.fori_loop(0, S // RT, cp, 0)


def _conv_fwd_all(zxp, xc_s, cw_ref, cb_ref, S):
    def cv(i, c):
        t0 = pl.multiple_of(i * RT, RT)
        xm1, x0, xp1, xp2 = _conv_taps(zxp[pl.ds(t0, RT + 2 * PADR), :])
        xc_s[pl.ds(t0, RT), :] = (cb_ref[...] + xm1 * cw_ref[0:1, :] + x0 * cw_ref[1:2, :]
                                  + xp1 * cw_ref[2:3, :] + xp2 * cw_ref[3:4, :])
        return c
    lax.fori_loop(0, S // RT, cv, 0)


def _lru_specs(S):
    head = lambda h: (0, h)
    return [pl.BlockSpec((4, HD), head), pl.BlockSpec((1, HD), head),
            pl.BlockSpec((2, None, HD, HD), lambda h: (0, h, 0, 0)), pl.BlockSpec((2, HD), head),
            pl.BlockSpec((2, None, HD, HD), lambda h: (0, h, 0, 0)), pl.BlockSpec((2, HD), head),
            pl.BlockSpec((2, HD), head)]


def _lru_fwd(z6, ya, cw, cb, wr, br, wi, bi, lam):
    S = z6.shape[1]
    nt = S // RT

    def body(z_ref, ya_ref, cw_ref, cb_ref, wr_ref, br_ref, wi_ref, bi_ref, lam_ref, mg_ref, h0_ref, h1_ref, zxp, xc_s):
        sp = _softplus_neg(lam_ref[...])
        _fill_padded(zxp, z_ref.at[0], S)
        _conv_fwd_all(zxp, xc_s, cw_ref, cb_ref, S)

        def up(i, carry):
            t0 = pl.multiple_of(i * RT, RT)
            xc = xc_s[pl.ds(t0, RT), :]
            _, gi, a, mult = _lru_gates(xc, 0, wr_ref, br_ref, wi_ref, bi_ref, sp)
            pa, hb = _scan_up(a, mult * gi * xc)
            h = hb + pa * carry
            h0_ref[pl.ds(t0, RT), :] = h
            return h[RT - 1:RT, :]
        lax.fori_loop(0, nt, up, jnp.zeros((1, HD), F32))

        def down(i, carry):
            t0 = pl.multiple_of((nt - 1 - i) * RT, RT)
            rows = pl.ds(t0, RT)
            xc = xc_s[rows, :]
            _, gi, a, mult = _lru_gates(xc, 1, wr_ref, br_ref, wi_ref, bi_ref, sp)
            pa, hb = _scan_down(a, mult * gi * xc)
            h = hb + pa * carry
            h1_ref[rows, :] = h
            yb = (h0_ref[rows, :] + h) * _gelu(z_ref[1, rows, :])
            mg_ref[rows, :] = (ya_ref[rows, :] + _sigmoid(z_ref[2, rows, :]) * yb).astype(BF16)
            return h[0:1, :]
        lax.fori_loop(0, nt, down, jnp.zeros((1, HD), F32))

    col = pl.BlockSpec((S, HD), lambda h: (0, h))
    return pl.pallas_call(
        body, name="lru_fwd", grid=(NH,),
        in_specs=[pl.BlockSpec((3, S, HD), lambda h: (1, 0, h)), col] + _lru_specs(S),
        out_specs=[col, col, col],
        out_shape=[jax.ShapeDtypeStruct((S, D), BF16), jax.ShapeDtypeStruct((S, D), F32), jax.ShapeDtypeStruct((S, D), F32)],
        scratch_shapes=[pltpu.VMEM((S + 2 * PADR, HD), F32), pltpu.VMEM((S, HD), F32)],
        compiler_params=_cp(("parallel",)),
    )(z6, ya, cw, cb, wr, br, wi, bi, lam)


def _loss_head(x, tgt, g):
    S = x.shape[0]

    def body(x_ref, t_ref, g_ref, dx_ref, loss_ref, dg_ref):
        @pl.when(pl.program_id(0) == 0)
        def _():
            loss_ref[...] = jnp.zeros_like(loss_ref)
            dg_ref[...] = jnp.zeros_like(dg_ref)
        xv = x_ref[...]
        xh, _ = _rms_hat(xv)
        e = xh * g_ref[...] - t_ref[...]
        loss_ref[...] += jnp.sum(e * e) * (0.5 / D)
        dx, dgs = _rms_bwd(e * (1.0 / D), xv, g_ref[...])
        dx_ref[...] = dx
        dg_ref[...] += dgs

    return pl.pallas_call(
        body, name="loss_head", grid=(S // TM,),
        in_specs=[pl.BlockSpec((TM, D), lambda i: (i, 0)), pl.BlockSpec((TM, D), lambda i: (i, 0)),
                  pl.BlockSpec((1, D), lambda i: (0, 0))],
        out_specs=[pl.BlockSpec((TM, D), lambda i: (i, 0)), pl.BlockSpec((1, 128), lambda i: (0, 0)),
                   pl.BlockSpec((1, D), lambda i: (0, 0))],
        out_shape=[jax.ShapeDtypeStruct((S, D), F32), jax.ShapeDtypeStruct((1, 128), F32), jax.ShapeDtypeStruct((1, D), F32)],
        compiler_params=_cp(("arbitrary",)),
    )(x, tgt, g)


def _bwd_ffn_out(dx, w_fo, gu, l):
    S = dx.shape[0]

    def body(dx_ref, w_ref, ga_ref, gb_ref, o_ref, dxb, dff):
        k = pl.program_id(1)

        @pl.when(k == 0)
        def _():
            dxb[...] = dx_ref[...].astype(BF16)

        @pl.when(k < 2)
        def _():
            d = _dot_nt(dxb[...], w_ref[...])
            dff[k] = d
            ga = ga_ref[...]
            sg = _sigmoid(ga)
            o_ref[...] = (d * gb_ref[...] * sg * (1.0 + ga * (1.0 - sg))).astype(BF16)

        @pl.when(k >= 2)
        def _():
            ga = ga_ref[...]
            o_ref[...] = (dff[k - 2] * ga * _sigmoid(ga)).astype(BF16)

    return pl.pallas_call(
        body, name=f"bwd_ffn_out_{l}", grid=(S // TM, 4),
        in_specs=[pl.BlockSpec((TM, D), lambda i, k: (i, 0)), pl.BlockSpec((None, DFF_SH, D), lambda i, k: (l, k % 2, 0)),
                  pl.BlockSpec((None, TM, DFF_SH), lambda i, k: (k % 2, i, 0)),
                  pl.BlockSpec((None, TM, DFF_SH), lambda i, k: (k % 2 + 2, i, 0))],
        out_specs=pl.BlockSpec((None, TM, DFF_SH), lambda i, k: (k, i, 0)),
        out_shape=jax.ShapeDtypeStruct((4, S, DFF_SH), BF16),
        scratch_shapes=[pltpu.VMEM((TM, D), BF16), pltpu.VMEM((2, TM, DFF_SH), F32)],
        compiler_params=_cp(("parallel", "arbitrary")),
    )(dx, w_fo, gu, gu)


def _mm_tn(a, b, m_blk, name):
    S, M = a.shape

    def body(a_ref, b_ref, o_ref):
        @pl.when(pl.program_id(1) == 0)
        def _():
            o_ref[...] = jnp.zeros_like(o_ref)
        o_ref[...] += _dot_tn(a_ref[...], b_ref[...].astype(BF16))

    return pl.pallas_call(
        body, name=name, grid=(M // m_blk, S // TM),
        in_specs=[pl.BlockSpec((TM, m_blk), lambda m, k: (k, m)), pl.BlockSpec((TM, D), lambda m, k: (k, 0))],
        out_specs=pl.BlockSpec((m_blk, D), lambda m, k: (m, 0)),
        out_shape=jax.ShapeDtypeStruct((M, D), F32),
        compiler_params=_cp(("parallel", "arbitrary")),
    )(a, b)


def _mm_nt_rms_bwd(a, a_spec, w, w_spec, nk, kw, x, g, dres, name):
    S = x.shape[0]

    def body(a_ref, w_ref, x_ref, g_ref, r_ref, dx_ref, dg_ref, acc):
        i, k = pl.program_id(0), pl.program_id(1)

        @pl.when(k == 0)
        def _():
            acc[...] = jnp.zeros_like(acc)
        acc[...] += _dot_nt(a_ref[...], w_ref[...])

        @pl.when(jnp.logical_and(i == 0, k == 0))
        def _():
            dg_ref[...] = jnp.zeros_like(dg_ref)

        @pl.when(k == nk - 1)
        def _():
            dx, dgs = _rms_bwd(acc[...], x_ref[...], g_ref[...])
            dx_ref[...] = r_ref[...] + dx
            dg_ref[...] += dgs

    row = pl.BlockSpec((TM, D), lambda i, k: (i, 0))
    vec = pl.BlockSpec((1, D), lambda i, k: (0, 0))
    return pl.pallas_call(
        body, name=name, grid=(S // TM, nk),
        in_specs=[a_spec, w_spec, row, vec, row],
        out_specs=[row, vec],
        out_shape=[jax.ShapeDtypeStruct((S, D), F32), jax.ShapeDtypeStruct((1, D), F32)],
        scratch_shapes=[pltpu.VMEM((TM, D), F32)],
        compiler_params=_cp(("arbitrary", "arbitrary")),
    )(a, w, x, g, dres)


def _mm_tn_norm(x, g, b, b_spec, out_spec, out_shape, nj, name):
    S = x.shape[0]

    def body(x_ref, g_ref, b_ref, o_ref):
        @pl.when(pl.program_id(1) == 0)
        def _():
            o_ref[...] = jnp.zeros_like(o_ref)
        xh, _ = _rms_hat(x_ref[...])
        o_ref[...] += _dot_tn((xh * g_ref[...]).astype(BF16), b_ref[...])

    return pl.pallas_call(
        body, name=name, grid=(nj, S // TM),
        in_specs=[pl.BlockSpec((TM, D), lambda j, k: (k, 0)), pl.BlockSpec((1, D), lambda j, k: (0, 0)), b_spec],
        out_specs=out_spec, out_shape=out_shape,
        compiler_params=_cp(("parallel", "arbitrary")),
    )(x, g, b)


def _bwd_out(dx, w_o, merged, l):
    S = dx.shape[0]

    def body(dx_ref, w_ref, m_ref, dm_ref, dw_ref):
        @pl.when(pl.program_id(0) == 0)
        def _():
            dw_ref[...] = jnp.zeros_like(dw_ref)
        dxb = dx_ref[...].astype(BF16)
        dm_ref[...] = _dot_nt(dxb, w_ref[...])
        dw_ref[...] += _dot_tn(m_ref[...], dxb)

    row = pl.BlockSpec((TM, D), lambda i: (i, 0))
    return pl.pallas_call(
        body, name=f"bwd_out_{l}", grid=(S // TM,),
        in_specs=[row, pl.BlockSpec((None, D, D), lambda i: (l, 0, 0)), row],
        out_specs=[row, pl.BlockSpec((D, D), lambda i: (0, 0))],
        out_shape=[jax.ShapeDtypeStruct((S, D), F32), jax.ShapeDtypeStruct((D, D), F32)],
        compiler_params=_cp(("arbitrary",)),
    )(dx, w_o, merged)


def _gmlp_bwd(dm, z6, ws_b, wst_b, bs_b, lg, lb):
    S = z6.shape[1]

    def body(dm_ref, z_ref, ws_ref, wst_ref, bs_ref, lg_ref, lb_ref, dz_ref, dws_ref, dbs_ref, dlg_ref, dlb_ref, mix, dv):
        @pl.when(pl.program_id(0) == 0)
        def _():
            dws_ref[...] = jnp.zeros_like(dws_ref)
            dbs_ref[...] = jnp.zeros_like(dbs_ref)
            dlg_ref[...] = jnp.zeros_like(dlg_ref)
            dlb_ref[...] = jnp.zeros_like(dlb_ref)
        gv, dgelu_v = _gelu_and_grad(z_ref[1])
        xc = gv - jnp.mean(gv, axis=-1, keepdims=True)
        rs = lax.rsqrt(jnp.mean(xc * xc, axis=-1, keepdims=True) + EPS)
        vh = xc * rs
        vb = (vh * lg_ref[...] + lb_ref[...]).astype(BF16)
        for gi in range(NH):
            cs = slice(gi * HD, (gi + 1) * HD)
            mix[:, cs] = _dot(ws_ref[gi], vb[:, cs])
        u, dgelu_u = _gelu_and_grad(z_ref[0])
        sa = _sigmoid(z_ref[2])
        mixed = mix[...] + bs_ref[...]
        dyg = dm_ref[...]
        dz_ref[2] = (dyg * u * mixed * sa * (1.0 - sa)).astype(BF16)
        dya = dyg * sa
        dz_ref[0] = (dya * mixed * dgelu_u).astype(BF16)
        dmix = dya * u
        dmb = dmix.astype(BF16)
        for gi in range(NH):
            cs = slice(gi * HD, (gi + 1) * HD)
            dv[:, cs] = _dot(wst_ref[gi], dmb[:, cs])
            dws_ref[gi] += _dot_nt(dmb[:, cs], vb[:, cs])
            dbs_ref[gi] += jnp.broadcast_to(jnp.sum(dmix[:, cs], axis=1, keepdims=True), (CHUNK, HD))
        dvv = dv[...]
        dlg_ref[...] += jnp.sum(dvv * vh, axis=0, keepdims=True)
        dlb_ref[...] += jnp.sum(dvv, axis=0, keepdims=True)
        dvh = dvv * lg_ref[...]
        dgv = rs * (dvh - jnp.mean(dvh, axis=-1, keepdims=True) - vh * jnp.mean(dvh * vh, axis=-1, keepdims=True))
        dz_ref[1] = (dgv * dgelu_v).astype(BF16)

    vec = pl.BlockSpec((1, D), lambda i: (0, 0))
    mat = pl.BlockSpec((NH, CHUNK, CHUNK), lambda i: (0, 0, 0))
    return pl.pallas_call(
        body, name="gmlp_bwd", grid=(S // CHUNK,),
        in_specs=[pl.BlockSpec((CHUNK, D), lambda i: (i, 0)), pl.BlockSpec((3, CHUNK, D), lambda i: (0, i, 0)), mat, mat,
                  pl.BlockSpec((CHUNK, D), lambda i: (0, 0)), vec, vec],
        out_specs=[pl.BlockSpec((3, CHUNK, D), lambda i: (0, i, 0)), mat, mat, vec, vec],
        out_shape=[jax.ShapeDtypeStruct((6, S, D), BF16), jax.ShapeDtypeStruct((NH, CHUNK, CHUNK), F32),
                   jax.ShapeDtypeStruct((NH, CHUNK, HD), F32), jax.ShapeDtypeStruct((1, D), F32), jax.ShapeDtypeStruct((1, D), F32)],
        scratch_shapes=[pltpu.VMEM((CHUNK, D), F32), pltpu.VMEM((CHUNK, D), F32)],
        compiler_params=_cp(("arbitrary",)),
    )(dm, z6, ws_b, wst_b, bs_b, lg, lb)


def _lru_bwd(dz6, dm, z6, h0, h1, cw, cb, wr, br, wi, bi, lam):
    S = z6.shape[1]
    nt = S // RT

    def body(dz_in, dm_ref, z_ref, h0_ref, h1_ref, cw_ref, cb_ref, wr_ref, br_ref, wi_ref, bi_ref, lam_ref,
             dz_ref, dcw_ref, dcb_ref, dwr_ref, dbr_ref, dwi_ref, dbi_ref, dlam_ref, zxp, xc_s, dhs_s, dxcp):
        del dz_in
        lam = lam_ref[...]
        sp = _softplus_neg(lam)
        row = _row_iota()
        _fill_padded(zxp, z_ref.at[0], S)
        _conv_fwd_all(zxp, xc_s, cw_ref, cb_ref, S)
        zeros = jnp.zeros((PADR, HD), F32)
        dxcp[0:PADR, :] = zeros
        dxcp[PADR + S:2 * PADR + S, :] = zeros
        dwr_ref[...] = jnp.zeros_like(dwr_ref)
        dwi_ref[...] = jnp.zeros_like(dwi_ref)

        def gate_bwd(d, t0, lamv, da, xc, first):
            r, gi, a, mult = _lru_gates(xc, d, wr_ref, br_ref, wi_ref, bi_ref, sp)
            dmult = lamv * gi * xc
            dgi = lamv * mult * xc
            dlog = (da - dmult * a / mult) * a
            dpr = (dlog * (-LRU_C) * sp[d:d + 1, :]) * r * (1.0 - r)
            dpi = dgi * gi * (1.0 - gi)
            xb, dprb, dpib = xc.astype(BF16), dpr.astype(BF16), dpi.astype(BF16)
            dwr_ref[d] += _dot_tn(xb, dprb)
            dwi_ref[d] += _dot_tn(xb, dpib)
            dxc = lamv * mult * gi + _dot_nt(dprb, wr_ref[d]) + _dot_nt(dpib, wi_ref[d])
            rows = pl.ds(t0 + PADR, RT)
            if first:
                dxcp[rows, :] = dxc
            else:
                dxcp[rows, :] += dxc
            return (jnp.sum(dlog * r, axis=0, keepdims=True) * (-LRU_C), jnp.sum(dpr, axis=0, keepdims=True),
                    jnp.sum(dpi, axis=0, keepdims=True))

        def down(i, carry):
            q_next, s_sp, s_br, s_bi = carry
            t0 = pl.multiple_of((nt - 1 - i) * RT, RT)
            rows = pl.ds(t0, RT)
            xc = xc_s[rows, :]
            h0t, h1t = h0_ref[rows, :], h1_ref[rows, :]
            hs = h0t + h1t
            dmv = dm_ref[rows, :]
            sb = _sigmoid(z_ref[2, rows, :])
            gg, dgg = _gelu_and_grad(z_ref[1, rows, :])
            dz_ref[2, rows, :] = (dmv * hs * gg * sb * (1.0 - sb)).astype(BF16)
            dyb = dmv * sb
            dz_ref[1, rows, :] = (dyb * hs * dgg).astype(BF16)
            dhs = dyb * gg
            dhs_s[rows, :] = dhs
            _, _, a, _ = _lru_gates(xc, 0, wr_ref, br_ref, wi_ref, bi_ref, sp)
            pa, qb = _scan_down(a, a * dhs)
            q = qb + pa * q_next
            lamv = dhs + jnp.where(row == RT - 1, q_next, pltpu.roll(q, RT - 1, 0))
            tp = pl.multiple_of(jnp.maximum(t0 - PADR, 0), PADR)
            prev = jnp.where(t0 > 0, h0_ref[pl.ds(tp, PADR), :][PADR - 1:PADR, :], 0.0)
            hprev = jnp.where(row == 0, prev, pltpu.roll(h0t, 1, 0))
            g_sp, g_br, g_bi = gate_bwd(0, t0, lamv, lamv * hprev, xc, True)
            return q[0:1, :], s_sp + g_sp, s_br + g_br, s_bi + g_bi

        z1 = jnp.zeros((1, HD), F32)
        _, s_sp0, s_br0, s_bi0 = lax.fori_loop(0, nt, down, (z1, z1, z1, z1))

        def up(i, carry):
            q_prev, s_sp, s_br, s_bi = carry
            t0 = pl.multiple_of(i * RT, RT)
            rows = pl.ds(t0, RT)
            xc = xc_s[rows, :]
            h1t = h1_ref[rows, :]
            dhs = dhs_s[rows, :]
            _, _, a, _ = _lru_gates(xc, 1, wr_ref, br_ref, wi_ref, bi_ref, sp)
            pa, qb = _scan_up(a, a * dhs)
            q = qb + pa * q_prev
            lamv = dhs + jnp.where(row == 0, q_prev, pltpu.roll(q, 1, 0))
            tn = pl.multiple_of(jnp.minimum(t0 + RT, S - PADR), PADR)
            nxt = jnp.where(t0 + RT < S, h1_ref[pl.ds(tn, PADR), :][0:1, :], 0.0)
            hnext = jnp.where(row == RT - 1, nxt, pltpu.roll(h1t, RT - 1, 0))
            g_sp, g_br, g_bi = gate_bwd(1, t0, lamv, lamv * hnext, xc, False)
            return q[RT - 1:RT, :], s_sp + g_sp, s_br + g_br, s_bi + g_bi

        _, s_sp1, s_br1, s_bi1 = lax.fori_loop(0, nt, up, (z1, z1, z1, z1))

        dsp = jnp.concatenate([s_sp0, s_sp1], axis=0)
        dlam_ref[...] = -dsp * _sigmoid(-lam)
        dbr_ref[...] = jnp.concatenate([s_br0, s_br1], axis=0)
        dbi_ref[...] = jnp.concatenate([s_bi0, s_bi1], axis=0)

        def conv_bwd(i, carry):
            c0, c1, c2, c3, cb_ = carry
            t0 = pl.multiple_of(i * RT, RT)
            dwin = dxcp[pl.ds(t0, RT + 2 * PADR), :]
            d0 = _shifted(dwin, 0)
            dz_ref[0, pl.ds(t0, RT), :] = (_shifted(dwin, 1) * cw_ref[0:1, :] + d0 * cw_ref[1:2, :]
                                           + _shifted(dwin, -1) * cw_ref[2:3, :] + _shifted(dwin, -2) * cw_ref[3:4, :]).astype(BF16)
            xm1, x0, xp1, xp2 = _conv_taps(zxp[pl.ds(t0, RT + 2 * PADR), :])
            sm = lambda v: jnp.sum(v, axis=0, keepdims=True)
            return c0 + sm(d0 * xm1), c1 + sm(d0 * x0), c2 + sm(d0 * xp1), c3 + sm(d0 * xp2), cb_ + sm(d0)

        c0, c1, c2, c3, cb_ = lax.fori_loop(0, nt, conv_bwd, (z1, z1, z1, z1, z1))
        dcw_ref[...] = jnp.concatenate([c0, c1, c2, c3], axis=0)
        dcb_ref[...] = cb_

    col = pl.BlockSpec((S, HD), lambda h: (0, h))
    head = lambda h: (0, h)
    wspec = pl.BlockSpec((2, None, HD, HD), lambda h: (0, h, 0, 0))
    return pl.pallas_call(
        body, name="lru_bwd", grid=(NH,),
        in_specs=[pl.BlockSpec(memory_space=pl.ANY), col, pl.BlockSpec((3, S, HD), lambda h: (1, 0, h)), col, col] + _lru_specs(S),
        out_specs=[pl.BlockSpec((3, S, HD), lambda h: (1, 0, h)), pl.BlockSpec((4, HD), head), pl.BlockSpec((1, HD), head),
                   wspec, pl.BlockSpec((2, HD), head), wspec, pl.BlockSpec((2, HD), head), pl.BlockSpec((2, HD), head)],
        out_shape=[jax.ShapeDtypeStruct((6, S, D), BF16), jax.ShapeDtypeStruct((4, D), F32), jax.ShapeDtypeStruct((1, D), F32),
                   jax.ShapeDtypeStruct((2, NH, HD, HD), F32), jax.ShapeDtypeStruct((2, D), F32),
                   jax.ShapeDtypeStruct((2, NH, HD, HD), F32), jax.ShapeDtypeStruct((2, D), F32), jax.ShapeDtypeStruct((2, D), F32)],
        scratch_shapes=[pltpu.VMEM((S + 2 * PADR, HD), F32), pltpu.VMEM((S, HD), F32), pltpu.VMEM((S, HD), F32),
                        pltpu.VMEM((S + 2 * PADR, HD), F32)],
        input_output_aliases={0: 0},
        compiler_params=_cp(("parallel",)),
    )(dz6, dm, z6, h0, h1, cw, cb, wr, br, wi, bi, lam)


def _local_step(x, tgt, p):
    S = x.shape[0]
    saved = []
    for l in range(2):
        g1, g2 = p["norm1_g"][l][None], p["norm2_g"][l][None]
        ws_b = p["gmlp_w_s"][l].astype(BF16)
        tm = dict(ws_b=ws_b, wst_b=jnp.swapaxes(ws_b, 1, 2), bs_b=jnp.repeat(p["gmlp_b_s"][l].T, HD, axis=1),
                  lg=p["gmlp_ln_g"][l][None], lb=p["gmlp_ln_b"][l][None])
        lru = (p["conv_w"][l], p["conv_b"][l][None], p["lru_w_r"][l].astype(BF16), p["lru_b_r"][l],
               p["lru_w_i"][l].astype(BF16), p["lru_b_i"][l], p["lru_lambda"][l])
        z6 = _mm_in(x, g1, p["w_in"], l)
        ya = _gmlp_fwd(z6, tm["ws_b"], tm["bs_b"], tm["lg"], tm["lb"])
        merged, h0, h1 = _lru_fwd(z6, ya, *lru)
        x1 = _mm_res(merged, p["w_out"], x, l, "mm_out")
        gu, ff = _mm_ffn_in(x1, g2, p["w_ffn_in"], l)
        x2 = _mm_res(ff, p["w_ffn_out"], x1, l, "mm_ffn_out")
        saved.append(dict(x=x, z6=z6, h0=h0, h1=h1, merged=merged, x1=x1, gu=gu, ff=ff, g1=g1, g2=g2, tm=tm, lru=lru))
        x = x2

    dx, loss_v, dfg = _loss_head(x, tgt, p["final_g"][None])
    big = [None, None]
    small = {"final_g": dfg[0]}
    per_layer = {k: [None, None] for k in ("norm1_g", "gmlp_ln_g", "gmlp_ln_b", "gmlp_w_s", "gmlp_b_s", "conv_w", "conv_b",
                                           "lru_w_r", "lru_b_r", "lru_w_i", "lru_b_i", "lru_lambda", "norm2_g")}
    for l in (1, 0):
        s = saved[l]
        tm = s["tm"]
        dgu = _bwd_ffn_out(dx, p["w_ffn_out"], s["gu"], l)
        dwfo = _mm_tn(s["ff"], dx, DFF_SH, f"dw_ffn_out_{l}")
        dx1, dg2 = _mm_nt_rms_bwd(
            dgu, pl.BlockSpec((None, TM, DFF_SH), lambda i, k: (k, i, 0)),
            p["w_ffn_in"], pl.BlockSpec((None, None, D, DFF_SH), lambda i, k, l=l: (l, k, 0, 0)),
            4, DFF_SH, s["x1"], s["g2"], dx, f"bwd_ffn_in_{l}")
        dwfi = _mm_tn_norm(
            s["x1"], s["g2"], dgu, pl.BlockSpec((None, TM, DFF_SH), lambda j, k: (j, k, 0)),
            pl.BlockSpec((None, D, DFF_SH), lambda j, k: (j, 0, 0)), jax.ShapeDtypeStruct((4, D, DFF_SH), F32), 4, f"dw_ffn_in_{l}")
        dmg, dwo = _bwd_out(dx1, p["w_out"], s["merged"], l)
        dz6, dws, dbs, dlg, dlb = _gmlp_bwd(dmg, s["z6"], tm["ws_b"], tm["wst_b"], tm["bs_b"], tm["lg"], tm["lb"])
        dz6, dcw, dcb, dwr, dbr, dwi, dbi, dlam = _lru_bwd(dz6, dmg, s["z6"], s["h0"], s["h1"], *s["lru"])

        def w_map(i, k, l=l):
            sh, tl = _in_tile(k)
            return (l, sh, 0, tl)

        def dw_map(j, k):
            sh, tl = _in_tile(j)
            return (sh, 0, tl)

        dx0, dg1 = _mm_nt_rms_bwd(
            dz6, pl.BlockSpec((None, TM, 512), lambda i, k: (k // 2, i, k % 2)),
            p["w_in"], pl.BlockSpec((None, None, D, 512), w_map),
            N_IN_T, 512, s["x"], s["g1"], dx1, f"bwd_in_{l}")
        dwin = _mm_tn_norm(
            s["x"], s["g1"], dz6, pl.BlockSpec((None, TM, 512), lambda j, k: (j // 2, k, j % 2)),
            pl.BlockSpec((None, D, 512), dw_map), jax.ShapeDtypeStruct((4, D, 1536), F32), N_IN_T, f"dw_in_{l}")
        dx = dx0
        big[l] = [dwin, dwo, dwfi, dwfo]
        for k, v in (("norm1_g", dg1[0]), ("gmlp_ln_g", dlg[0]), ("gmlp_ln_b", dlb[0]), ("gmlp_w_s", dws), ("gmlp_b_s", dbs[:, :, 0]),
                     ("conv_w", dcw), ("conv_b", dcb[0]), ("lru_w_r", dwr), ("lru_b_r", dbr), ("lru_w_i", dwi), ("lru_b_i", dbi),
                     ("lru_lambda", dlam), ("norm2_g", dg2[0])):
            per_layer[k][l] = v
    for k, v in per_layer.items():
        small[k] = jnp.stack(v)
    return loss_v, dx, big, small


def _place():
    x, y, c = lax.axis_index("x"), lax.axis_index("y"), lax.axis_index("c")
    return x, y, c, 2 * x + y


def _chip_at(x, y, d):
    px = 1 - x if d & 2 else x
    py = 1 - y if d & 1 else y
    return px, py, 2 * px + py


ANY = pl.BlockSpec(memory_space=pl.ANY)


def _cast_into(wf, chip_arr, name):
    _, rows, cols = wf.shape
    rh = rows // 2

    def body(ch_ref, w_ref, o_ref):
        o_ref[...] = w_ref[...].astype(BF16)

    return pl.pallas_call(
        body, name=name, out_shape=jax.ShapeDtypeStruct((2, 4, 2, rh, cols), BF16),
        grid_spec=pltpu.PrefetchScalarGridSpec(
            num_scalar_prefetch=1, grid=(2, 2),
            in_specs=[pl.BlockSpec((None, None, rh, cols), lambda l, h, ch: (l, h, 0, 0))],
            out_specs=pl.BlockSpec((None, None, None, rh, cols), lambda l, h, ch: (l, ch[0], h, 0, 0))),
        compiler_params=_cp(("parallel", "parallel")),
    )(chip_arr, wf.reshape(2, 2, rh, cols))


def _gather_weights(bufs, tiny):
    nt = len(bufs)
    n_ici = nt * 2 * 3

    def body(*refs):
        tiny_ref = refs[nt]
        o_refs, tiny_o = refs[nt + 1:2 * nt + 1], refs[2 * nt + 1]
        send, recv, fsend, frecv, tsend, trecv, lsem = refs[2 * nt + 2:]
        x, y, c, chip = _place()

        local = [pltpu.make_async_copy(tiny_ref, tiny_o.at[chip], lsem)]
        for cp in local:
            cp.start()

        def ici(t, l, d, origin_chip, to):
            k = (2 * t + l) * 3 + d - 1
            blk = o_refs[t].at[l, origin_chip, c]
            return pltpu.make_async_remote_copy(
                src_ref=blk, dst_ref=blk, send_sem=send.at[k], recv_sem=recv.at[k], device_id=to, device_id_type=MESH)

        def fwd(t, l, d, origin_chip, half):
            k = (2 * t + l) * 3 + d - 1
            blk = o_refs[t].at[l, origin_chip, half]
            return pltpu.make_async_remote_copy(
                src_ref=blk, dst_ref=blk, send_sem=fsend.at[k], recv_sem=frecv.at[k],
                device_id=(x, y, 1 - c), device_id_type=MESH)

        def tin(d, origin_chip, to):
            return pltpu.make_async_remote_copy(
                src_ref=tiny_ref, dst_ref=tiny_o.at[origin_chip], send_sem=tsend.at[d - 1], recv_sem=trecv.at[d - 1],
                device_id=to, device_id_type=MESH)

        sends = []
        for t in range(nt):
            for l in range(2):
                for d in (1, 2, 3):
                    px, py, _ = _chip_at(x, y, d)
                    sends.append(ici(t, l, d, chip, (px, py, c)))
        for d in (1, 2, 3):
            px, py, _ = _chip_at(x, y, d)
            sends.append(tin(d, chip, (px, py, c)))
        for cp in sends:
            cp.start()
        passed = []
        for t in range(nt):
            for l in range(2):
                for d in (1, 2, 3):
                    _, _, pchip = _chip_at(x, y, d)
                    ici(t, l, d, pchip, (x, y, c)).wait_recv()
                    f = fwd(t, l, d, pchip, c)
                    f.start()
                    passed.append(f)
        for t in range(nt):
            for l in range(2):
                for d in (1, 2, 3):
                    _, _, pchip = _chip_at(x, y, d)
                    fwd(t, l, d, pchip, 1 - c).wait_recv()
        for d in (1, 2, 3):
            _, _, pchip = _chip_at(x, y, d)
            tin(d, pchip, (x, y, c)).wait_recv()
        for cp in sends + passed:
            cp.wait_send()
        for cp in local:
            cp.wait()

    out_shape = [jax.ShapeDtypeStruct(b.shape, b.dtype) for b in bufs]
    out_shape.append(jax.ShapeDtypeStruct((4,) + tiny.shape, tiny.dtype))
    outs = pl.pallas_call(
        body, name="gather_weights", out_shape=out_shape,
        in_specs=[ANY] * (nt + 1), out_specs=[ANY] * (nt + 1),
        scratch_shapes=[pltpu.SemaphoreType.DMA((n_ici,)), pltpu.SemaphoreType.DMA((n_ici,)),
                        pltpu.SemaphoreType.DMA((n_ici,)), pltpu.SemaphoreType.DMA((n_ici,)),
                        pltpu.SemaphoreType.DMA((3,)), pltpu.SemaphoreType.DMA((3,)), pltpu.SemaphoreType.DMA],
        input_output_aliases={t: t for t in range(nt)},
        compiler_params=_cp(has_side_effects=True),
    )(*bufs, tiny)
    return outs[:nt], outs[nt]


def _to_sibling_halves(gs, l):
    nt = len(gs)

    def body(*refs):
        g_refs, o_refs = refs[:nt], refs[nt:2 * nt]
        send, recv = refs[2 * nt:]
        x, y, c, _ = _place()
        cps = [pltpu.make_async_remote_copy(
            src_ref=g_refs[t].at[k, 1 - c], dst_ref=o_refs[t].at[k], send_sem=send.at[4 * t + k], recv_sem=recv.at[4 * t + k],
            device_id=(x, y, 1 - c), device_id_type=MESH) for t in range(nt) for k in range(4)]
        for cp in cps:
            cp.start()
        for cp in cps:
            cp.wait()

    return pl.pallas_call(
        body, name=f"grads_to_sibling_{l}", out_shape=[jax.ShapeDtypeStruct((4,) + g.shape[2:], g.dtype) for g in gs],
        in_specs=[ANY] * nt, out_specs=[ANY] * nt,
        scratch_shapes=[pltpu.SemaphoreType.DMA((4 * nt,)), pltpu.SemaphoreType.DMA((4 * nt,))],
        compiler_params=_cp(has_side_effects=True),
    )(*gs)


def _to_chips(cs, l):
    nt = len(cs)

    def body(*refs):
        c_refs, o_refs = refs[:nt], refs[nt:2 * nt]
        send, recv = refs[2 * nt:]
        x, y, c, _ = _place()
        cps = []
        for t in range(nt):
            for d in (1, 2, 3):
                px, py, pchip = _chip_at(x, y, d)
                cps.append(pltpu.make_async_remote_copy(
                    src_ref=c_refs[t].at[pchip], dst_ref=o_refs[t].at[d - 1], send_sem=send.at[3 * t + d - 1],
                    recv_sem=recv.at[3 * t + d - 1], device_id=(px, py, c), device_id_type=MESH))
        for cp in cps:
            cp.start()
        for cp in cps:
            cp.wait()

    return pl.pallas_call(
        body, name=f"grads_to_chips_{l}", out_shape=[jax.ShapeDtypeStruct((3,) + a.shape[1:], a.dtype) for a in cs],
        in_specs=[ANY] * nt, out_specs=[ANY] * nt,
        scratch_shapes=[pltpu.SemaphoreType.DMA((3 * nt,)), pltpu.SemaphoreType.DMA((3 * nt,))],
        compiler_params=_cp(has_side_effects=True),
    )(*cs)


def _join_halves(fs, l):
    nt = len(fs)

    def body(*refs):
        o_refs = refs[nt:2 * nt]
        send, recv = refs[2 * nt:]
        x, y, c, _ = _place()
        cps = [pltpu.make_async_remote_copy(
            src_ref=o_refs[t].at[c], dst_ref=o_refs[t].at[c], send_sem=send.at[t], recv_sem=recv.at[t],
            device_id=(x, y, 1 - c), device_id_type=MESH) for t in range(nt)]
        for cp in cps:
            cp.start()
        for cp in cps:
            cp.wait()

    return pl.pallas_call(
        body, name=f"grads_join_{l}", out_shape=[jax.ShapeDtypeStruct(a.shape, a.dtype) for a in fs],
        in_specs=[ANY] * nt, out_specs=[ANY] * nt,
        scratch_shapes=[pltpu.SemaphoreType.DMA((nt,)), pltpu.SemaphoreType.DMA((nt,))],
        input_output_aliases={t: t for t in range(nt)},
        compiler_params=_cp(has_side_effects=True),
    )(*fs)


def _add_half(g, r, c_arr, name):
    _, _, rh, cols = g.shape

    def body(c_ref, g_ref, r_ref, o_ref):
        o_ref[...] = (g_ref[...] + r_ref[...]).astype(BF16)

    blk = pl.BlockSpec((None, rh, cols), lambda k, cr: (k, 0, 0))
    return pl.pallas_call(
        body, name=name, out_shape=jax.ShapeDtypeStruct((4, rh, cols), BF16),
        grid_spec=pltpu.PrefetchScalarGridSpec(
            num_scalar_prefetch=1, grid=(4,),
            in_specs=[pl.BlockSpec((None, None, rh, cols), lambda k, cr: (k, cr[0], 0, 0)), blk], out_specs=blk),
        compiler_params=_cp(("parallel",)),
    )(c_arr, g, r)


def _sum_chips(cs, r3, place_arr, name):
    _, rh, cols = cs.shape
    rb = rh // 2

    def body(pl_ref, a_ref, r0_ref, r1_ref, r2_ref, o_ref):
        up = lambda ref: ref[...].astype(F32)
        o_ref[...] = ((up(a_ref) + up(r0_ref)) + up(r1_ref)) + up(r2_ref)

    def slot(d):
        return pl.BlockSpec((None, rb, cols), lambda i, pa: (d, i, 0))

    return pl.pallas_call(
        body, name=name, out_shape=jax.ShapeDtypeStruct((2, rh, cols), F32),
        grid_spec=pltpu.PrefetchScalarGridSpec(
            num_scalar_prefetch=1, grid=(2,),
            in_specs=[pl.BlockSpec((None, rb, cols), lambda i, pa: (pa[0], i, 0)), slot(0), slot(1), slot(2)],
            out_specs=pl.BlockSpec((None, rb, cols), lambda i, pa: (pa[1], i, 0))),
        compiler_params=_cp(("parallel",)),
    )(place_arr, cs, r3, r3, r3)


def _allreduce_small(pack):
    rows = pack.shape[0]
    hr = rows // 2

    def body(p_ref, o_ref, sib, slots, s1, r1, s2, r2, s3, r3):
        x, y, c, chip = _place()
        sibling = (x, y, 1 - c)
        ex = pltpu.make_async_remote_copy(src_ref=p_ref, dst_ref=sib, send_sem=s1, recv_sem=r1,
                                          device_id=sibling, device_id_type=MESH)
        ex.start()
        ex.wait()
        half = pl.ds(pl.multiple_of(c * hr, 8), hr)
        slots[0] = p_ref[half, :] + sib[half, :]
        cps = []
        for d in (1, 2, 3):
            px, py, _ = _chip_at(x, y, d)
            cps.append(pltpu.make_async_remote_copy(
                src_ref=slots.at[0], dst_ref=slots.at[d], send_sem=s2.at[d - 1], recv_sem=r2.at[d - 1],
                device_id=(px, py, c), device_id_type=MESH))
        for cp in cps:
            cp.start()
        for cp in cps:
            cp.wait()
        tot = slots[chip]
        for k in (1, 2, 3):
            tot = tot + slots[jnp.bitwise_xor(chip, k)]
        o_ref[half, :] = tot
        back = pltpu.make_async_remote_copy(src_ref=o_ref.at[half, :], dst_ref=o_ref.at[half, :], send_sem=s3, recv_sem=r3,
                                            device_id=sibling, device_id_type=MESH)
        back.start()
        back.wait()

    vm = pl.BlockSpec(memory_space=pltpu.VMEM)
    return pl.pallas_call(
        body, name="allreduce_small", out_shape=jax.ShapeDtypeStruct((rows, 128), F32),
        in_specs=[vm], out_specs=vm,
        scratch_shapes=[pltpu.VMEM((rows, 128), F32), pltpu.VMEM((4, hr, 128), F32),
                        pltpu.SemaphoreType.DMA, pltpu.SemaphoreType.DMA, pltpu.SemaphoreType.DMA((3,)), pltpu.SemaphoreType.DMA((3,)),
                        pltpu.SemaphoreType.DMA, pltpu.SemaphoreType.DMA],
        compiler_params=_cp(has_side_effects=True),
    )(pack)


def _adam(g, w, m, v, name):
    rows, cols = g.shape
    rb = rows // 4

    def body(g_ref, w_ref, m_ref, v_ref, d_ref, m2_ref, v2_ref):
        gv = g_ref[...]
        m2 = ADAM_B1 * m_ref[...] + (1.0 - ADAM_B1) * gv
        v2 = ADAM_B2 * v_ref[...] + (1.0 - ADAM_B2) * (gv * gv)
        m_hat = m2 / (1.0 - ADAM_B1 ** ADAM_STEP)
        v_hat = v2 / (1.0 - ADAM_B2 ** ADAM_STEP)
        d_ref[...] = -ADAM_LR * (m_hat / (jnp.sqrt(v_hat) + ADAM_EPS) + ADAM_WD * w_ref[...])
        m2_ref[...] = m2
        v2_ref[...] = v2

    blk = pl.BlockSpec((rb, cols), lambda i: (i, 0))
    shp = jax.ShapeDtypeStruct((rows, cols), F32)
    return pl.pallas_call(
        body, name=name, grid=(4,), in_specs=[blk] * 4, out_specs=[blk] * 3, out_shape=[shp] * 3,
        compiler_params=_cp(("parallel",)),
    )(g, w, m, v)


def _pack(arrs, mult):
    flat = jnp.concatenate([a.reshape(-1) for a in arrs])
    rows = -(-flat.shape[0] // (128 * mult)) * mult
    return jnp.pad(flat, (0, rows * 128 - flat.shape[0])).reshape(rows, 128)


def _unpack(pack, shapes):
    flat = pack.reshape(-1)
    out, o = [], 0
    for s in shapes:
        n = 1
        for e in s:
            n *= e
        out.append(flat[o:o + n].reshape(s))
        o += n
    return out


WEIGHTS = ['norm1_g', 'w_in', 'gmlp_ln_g', 'gmlp_ln_b', 'gmlp_w_s', 'gmlp_b_s', 'conv_w', 'conv_b', 'lru_w_r', 'lru_b_r', 'lru_w_i',
           'lru_b_i', 'lru_lambda', 'w_out', 'norm2_g', 'w_ffn_in', 'w_ffn_out', 'final_g']
BIG = ['w_in', 'w_out', 'w_ffn_in', 'w_ffn_out']
SMALL = [n for n in WEIGHTS if n not in BIG]
CHIP_SHARDED_SMALL = ['conv_w', 'lru_b_r', 'lru_b_i', 'lru_lambda']


def kernel(x, norm1_g, w_in, gmlp_ln_g, gmlp_ln_b, gmlp_w_s, gmlp_b_s, conv_w, conv_b, lru_w_r, lru_b_r, lru_w_i, lru_b_i, lru_lambda, w_out, norm2_g, w_ffn_in, w_ffn_out, final_g, loss_target, m_norm1_g, m_w_in, m_gmlp_ln_g, m_gmlp_ln_b, m_gmlp_w_s, m_gmlp_b_s, m_conv_w, m_conv_b, m_lru_w_r, m_lru_b_r, m_lru_w_i, m_lru_b_i, m_lru_lambda, m_w_out, m_norm2_g, m_w_ffn_in, m_w_ffn_out, m_final_g, v_norm1_g, v_w_in, v_gmlp_ln_g, v_gmlp_ln_b, v_gmlp_w_s, v_gmlp_b_s, v_conv_w, v_conv_b, v_lru_w_r, v_lru_b_r, v_lru_w_i, v_lru_b_i, v_lru_lambda, v_w_out, v_norm2_g, v_w_ffn_in, v_w_ffn_out, v_final_g):
    a = dict(locals())
    w = {n: a[n] for n in WEIGHTS}
    mom = {n: a["m_" + n] for n in WEIGHTS}
    var = {n: a["v_" + n] for n in WEIGHTS}
    _, _, c, chip = _place()
    c_arr, chip_arr = jnp.reshape(c, (1,)).astype(jnp.int32), jnp.reshape(chip, (1,)).astype(jnp.int32)
    place_arr = jnp.stack([chip, c]).astype(jnp.int32)

    bufs = [_cast_into(w[n], chip_arr, f"cast_{n}") for n in BIG]
    tiny = _pack([w[n] for n in CHIP_SHARDED_SMALL], 8)
    full, tiny_full = _gather_weights(bufs, tiny)
    p = {n: w[n] for n in SMALL}
    for n, f in zip(BIG, full):
        p[n] = f.reshape(2, 4, 2 * f.shape[3], f.shape[4])
    p["w_out"] = p["w_out"].reshape(2, D, D)
    p["w_ffn_out"] = p["w_ffn_out"].reshape(2, DFF, D)
    parts = [_unpack(tiny_full[k], [w[n].shape for n in CHIP_SHARDED_SMALL]) for k in range(4)]
    for i, n in enumerate(CHIP_SHARDED_SMALL):
        p[n] = jnp.concatenate([parts[k][i] for k in range(4)], axis=-1)

    loss_v, dx, big, small = _local_step(x[0], loss_target[0], p)
    loss = lax.psum(loss_v[0, 0], ("x", "y", "c"))

    out = {}
    res = {n: [None, None] for n in BIG}
    for l in (1, 0):
        gs = [g.reshape(4, 2, -1, g.shape[-1]) for g in big[l]]
        from_sib = _to_sibling_halves(gs, l)
        cs = [_add_half(g, r, c_arr, f"add_half_{n}_{l}") for n, g, r in zip(BIG, gs, from_sib)]
        from_chips = _to_chips(cs, l)
        ts = [_sum_chips(cc, r3, place_arr, f"sum_chips_{n}_{l}") for n, cc, r3 in zip(BIG, cs, from_chips)]
        joined = _join_halves(ts, l)
        for n, j in zip(BIG, joined):
            g = j.reshape(w[n].shape[1:])
            res[n][l] = (g,) + tuple(_adam(g, w[n][l], mom[n][l], var[n][l], f"adam_{n}_{l}"))
    for n in BIG:
        out[n] = [jnp.stack([res[n][0][i], res[n][1][i]]) for i in range(4)]

    full_shapes = [small[n].shape for n in SMALL]
    red = _unpack(_allreduce_small(_pack([small[n] for n in SMALL], 16)), full_shapes)
    g_small = []
    for n, g in zip(SMALL, red):
        if n in CHIP_SHARDED_SMALL:
            g = lax.dynamic_slice_in_dim(g, chip * w[n].shape[-1], w[n].shape[-1], axis=g.ndim - 1)
        g_small.append(g)
    shapes = [w[n].shape for n in SMALL]
    packs = [_pack(lst, 32) for lst in (g_small, [w[n] for n in SMALL], [mom[n] for n in SMALL], [var[n] for n in SMALL])]
    upd = [_unpack(u, shapes) for u in _adam(*packs, "adam_small")]
    for i, n in enumerate(SMALL):
        out[n] = [g_small[i], upd[0][i], upd[1][i], upd[2][i]]

    return (loss, dx[None]) + tuple(out[n][i] for i in range(4) for n in WEIGHTS)
```

```python
import functools

import jax
import jax.numpy as jnp
from jax import lax
from jax.experimental import pallas as pl
from jax.experimental.pallas import tpu as pltpu

F32 = jnp.float32
BF16 = jnp.bfloat16
MESH = pl.DeviceIdType.MESH

D = 1024
NH = 8
HD = 128
CHUNK = 128
N_IN_T = 12
DFF = 2816
DFF_SH = 1408
EPS = 1e-6
LRU_C = 8.0
ADAM_LR, ADAM_B1, ADAM_B2, ADAM_EPS, ADAM_WD, ADAM_STEP = 0.001, 0.9, 0.999, 1e-08, 0.01, 10

TM = 512
TM_BIG = 1024
RT = 128
PADR = 8
VMEM_LIMIT = 56 * 1024 * 1024


def _cp(sem=None, **kw):
    if sem is not None:
        kw["dimension_semantics"] = sem
    return pltpu.CompilerParams(vmem_limit_bytes=VMEM_LIMIT, **kw)


_GC = 0.7978845608028654


def _sigmoid(x):
    return 1.0 / (1.0 + jnp.exp(-x))


def _gelu(x):
    return 0.5 * x * (1.0 + jnp.tanh(_GC * (x + 0.044715 * x * x * x)))


def _gelu_and_grad(x):
    t = jnp.tanh(_GC * (x + 0.044715 * x * x * x))
    g = 0.5 * x * (1.0 + t)
    dg = 0.5 * (1.0 + t) + 0.5 * x * (1.0 - t * t) * _GC * (1.0 + 3 * 0.044715 * x * x)
    return g, dg


def _softplus_neg(lam):
    y = jnp.exp(-jnp.abs(lam))
    u = 1.0 + y
    l1p = jnp.where(u == 1.0, y, jnp.log(u) * y / (u - 1.0))
    return jnp.maximum(-lam, 0.0) + l1p


def _dot(a, b):
    return jnp.dot(a, b, preferred_element_type=F32)


def _dot_nt(a, b):
    return lax.dot_general(a, b, (((1,), (1,)), ((), ())), preferred_element_type=F32)


def _dot_tn(a, b):
    return lax.dot_general(a, b, (((0,), (0,)), ((), ())), preferred_element_type=F32)


def _rms_hat(x):
    r = lax.rsqrt(jnp.mean(x * x, axis=-1, keepdims=True) + EPS)
    return x * r, r


def _rms_bwd(dh, x, g):
    xh, r = _rms_hat(x)
    dxh = dh * g
    dx = r * (dxh - xh * jnp.mean(dxh * xh, axis=-1, keepdims=True))
    return dx, jnp.sum(dh * xh, axis=0, keepdims=True)


def _in_tile(j):
    m, hf = j // 2, j % 2
    orig = jnp.where(m < 2, m, jnp.where(m == 2, 4, jnp.where(m < 5, m - 1, 5)))
    t = orig * 2 + hf
    return t // 3, t % 3


def _mm_in(x, g, w_in, l):
    S = x.shape[0]
    tm = min(TM_BIG, S)

    def body(x_ref, g_ref, w_ref, o_ref, h_ref):
        @pl.when(pl.program_id(1) == 0)
        def _():
            xh, _ = _rms_hat(x_ref[...])
            h_ref[...] = (xh * g_ref[...]).astype(BF16)
        o_ref[...] = _dot(h_ref[...], w_ref[...]).astype(BF16)

    def w_map(i, j):
        sh, tl = _in_tile(j)
        return (l, sh, 0, tl)

    return pl.pallas_call(
        body, name=f"mm_in_{l}", grid=(S // tm, N_IN_T),
        in_specs=[pl.BlockSpec((tm, D), lambda i, j: (i, 0)), pl.BlockSpec((1, D), lambda i, j: (0, 0)),
                  pl.BlockSpec((None, None, D, 512), w_map)],
        out_specs=[pl.BlockSpec((None, tm, 512), lambda i, j: (j // 2, i, j % 2)), pl.BlockSpec((tm, D), lambda i, j: (i, 0))],
        out_shape=[jax.ShapeDtypeStruct((6, S, D), BF16), jax.ShapeDtypeStruct((S, D), BF16)],
        compiler_params=_cp(("parallel", "arbitrary")),
    )(x, g, w_in)


def _mm_res(a, w, res, l, name):
    S, K = a.shape

    def body(a_ref, w_ref, r_ref, o_ref):
        o_ref[...] = r_ref[...] + _dot(a_ref[...], w_ref[...])

    return pl.pallas_call(
        body, name=f"{name}_{l}", grid=(S // TM,),
        in_specs=[pl.BlockSpec((TM, K), lambda i: (i, 0)), pl.BlockSpec((None, K, D), lambda i: (l, 0, 0)),
                  pl.BlockSpec((TM, D), lambda i: (i, 0))],
        out_specs=pl.BlockSpec((TM, D), lambda i: (i, 0)),
        out_shape=jax.ShapeDtypeStruct((S, D), F32),
        compiler_params=_cp(("parallel",)),
    )(a, w, res)


def _mm_ffn_in(x, g, w_fi, l):
    S = x.shape[0]

    def body(x_ref, g_ref, w_ref, gu_ref, ff_ref, h_ref, gbuf):
        k = pl.program_id(1)

        @pl.when(k == 0)
        def _():
            xh, _ = _rms_hat(x_ref[...])
            h_ref[...] = (xh * g_ref[...]).astype(BF16)
        acc = _dot(h_ref[...], w_ref[...])
        gu_ref[...] = acc.astype(BF16)

        @pl.when(k < 2)
        def _():
            gbuf[k] = acc

        @pl.when(k >= 2)
        def _():
            ga = gbuf[k - 2]
            ff_ref[...] = (ga * _sigmoid(ga) * acc).astype(BF16)

    return pl.pallas_call(
        body, name=f"mm_ffn_in_{l}", grid=(S // TM, 4),
        in_specs=[pl.BlockSpec((TM, D), lambda i, k: (i, 0)), pl.BlockSpec((1, D), lambda i, k: (0, 0)),
                  pl.BlockSpec((None, None, D, DFF_SH), lambda i, k: (l, k, 0, 0))],
        out_specs=[pl.BlockSpec((None, TM, DFF_SH), lambda i, k: (k, i, 0)),
                   pl.BlockSpec((TM, DFF_SH), lambda i, k: (i, jnp.maximum(k - 2, 0))),
                   pl.BlockSpec((TM, D), lambda i, k: (i, 0))],
        out_shape=[jax.ShapeDtypeStruct((4, S, DFF_SH), BF16), jax.ShapeDtypeStruct((S, DFF), BF16),
                   jax.ShapeDtypeStruct((S, D), BF16)],
        scratch_shapes=[pltpu.VMEM((2, TM, DFF_SH), F32)],
        compiler_params=_cp(("parallel", "arbitrary")),
    )(x, g, w_fi)


def _gmlp_fwd(z6, ws_b, bs_b, lg, lb):
    S = z6.shape[1]

    def body(z_ref, ws_ref, bs_ref, lg_ref, lb_ref, o_ref, mix):
        gv = _gelu(z_ref[1].astype(F32))
        xc = gv - jnp.mean(gv, axis=-1, keepdims=True)
        rs = lax.rsqrt(jnp.mean(xc * xc, axis=-1, keepdims=True) + EPS)
        vb = (xc * rs * lg_ref[...] + lb_ref[...]).astype(BF16)
        for gi in range(NH):
            cs = slice(gi * HD, (gi + 1) * HD)
            mix[:, cs] = _dot(ws_ref[gi], vb[:, cs])
        o_ref[...] = (_sigmoid(z_ref[2].astype(F32)) * _gelu(z_ref[0].astype(F32)) * (mix[...] + bs_ref[...])).astype(BF16)

    return pl.pallas_call(
        body, name="gmlp_fwd", grid=(S // CHUNK,),
        in_specs=[pl.BlockSpec((3, CHUNK, D), lambda i: (0, i, 0)), pl.BlockSpec((NH, CHUNK, CHUNK), lambda i: (0, 0, 0)),
                  pl.BlockSpec((CHUNK, D), lambda i: (0, 0)), pl.BlockSpec((1, D), lambda i: (0, 0)),
                  pl.BlockSpec((1, D), lambda i: (0, 0))],
        out_specs=pl.BlockSpec((CHUNK, D), lambda i: (i, 0)),
        out_shape=jax.ShapeDtypeStruct((S, D), BF16),
        scratch_shapes=[pltpu.VMEM((CHUNK, D), F32)],
        compiler_params=_cp(("parallel",)),
    )(z6, ws_b, bs_b, lg, lb)


def _row_iota():
    return lax.broadcasted_iota(jnp.int32, (RT, HD), 0)


SUB = 8


def _scan_up(a, b, carry):
    row = lax.broadcasted_iota(jnp.int32, (SUB, HD), 0)
    masks = [(d, row >= d) for d in (1, 2, 4)]
    c = jnp.broadcast_to(carry, (SUB, HD))
    hs = []
    for j in range(RT // SUB):
        aj, bj = a[SUB * j:SUB * (j + 1)], b[SUB * j:SUB * (j + 1)]
        for d, m in masks:
            bj = bj + aj * jnp.where(m, pltpu.roll(bj, d, 0), 0.0)
            aj = aj * jnp.where(m, pltpu.roll(aj, d, 0), 1.0)
        h = bj + aj * c
        hs.append(h)
        c = jnp.broadcast_to(h[SUB - 1:SUB, :], (SUB, HD))
    return jnp.concatenate(hs, axis=0), hs[-1][SUB - 1:SUB, :]


def _scan_down(a, b, carry):
    row = lax.broadcasted_iota(jnp.int32, (SUB, HD), 0)
    masks = [(d, row < SUB - d) for d in (1, 2, 4)]
    c = jnp.broadcast_to(carry, (SUB, HD))
    hs = []
    for j in reversed(range(RT // SUB)):
        aj, bj = a[SUB * j:SUB * (j + 1)], b[SUB * j:SUB * (j + 1)]
        for d, m in masks:
            bj = bj + aj * jnp.where(m, pltpu.roll(bj, SUB - d, 0), 0.0)
            aj = aj * jnp.where(m, pltpu.roll(aj, SUB - d, 0), 1.0)
        h = bj + aj * c
        hs.append(h)
        c = jnp.broadcast_to(h[0:1, :], (SUB, HD))
    return jnp.concatenate(hs[::-1], axis=0), hs[-1][0:1, :]


def _lru_gates(xc, d, wr_ref, br_ref, wi_ref, bi_ref, sp):
    xb = xc.astype(BF16)
    r = _sigmoid(_dot(xb, wr_ref[d]) + br_ref[d:d + 1, :])
    i = _sigmoid(_dot(xb, wi_ref[d]) + bi_ref[d:d + 1, :])
    log_a = -LRU_C * r * sp[d:d + 1, :]
    a = jnp.exp(log_a)
    mult = jnp.sqrt(jnp.maximum(-jnp.tanh(log_a) * (a * a + 1.0), 0.0))
    return r, i, a, mult


def _shifted(win, k):
    w = RT + 2 * PADR
    v = win if k == 0 else pltpu.roll(win, (-k) % w, 0)
    return v[PADR:PADR + RT]


def _conv_taps(win):
    return [_shifted(win, k) for k in (-1, 0, 1, 2)]


def _fill_padded(dst, src_ref, S):
    zeros = jnp.zeros((PADR, HD), F32)
    dst[0:PADR, :] = zeros
    dst[PADR + S:2 * PADR + S, :] = zeros

    def cp(i, c):
        t0 = pl.multiple_of(i * RT, RT)
        dst[pl.ds(t0 + PADR, RT), :] = src_ref[pl.ds(t0, RT), :].astype(F32)
        return c
    lax.fori_loop(0, S // RT, cp, 0)


def _conv_fwd_all(zxp, xc_s, cw_ref, cb_ref, S):
    def cv(i, c):
        t0 = pl.multiple_of(i * RT, RT)
        xm1, x0, xp1, xp2 = _conv_taps(zxp[pl.ds(t0, RT + 2 * PADR), :])
        xc_s[pl.ds(t0, RT), :] = (cb_ref[...] + xm1 * cw_ref[0:1, :] + x0 * cw_ref[1:2, :]
                                  + xp1 * cw_ref[2:3, :] + xp2 * cw_ref[3:4, :])
        return c
    lax.fori_loop(0, S // RT, cv, 0)


def _lru_specs(S):
    head = lambda h: (0, h)
    return [pl.BlockSpec((4, HD), head), pl.BlockSpec((1, HD), head),
            pl.BlockSpec((2, None, HD, HD), lambda h: (0, h, 0, 0)), pl.BlockSpec((2, HD), head),
            pl.BlockSpec((2, None, HD, HD), lambda h: (0, h, 0, 0)), pl.BlockSpec((2, HD), head),
            pl.BlockSpec((2, HD), head)]


def _lru_fwd(z6, ya, cw, cb, wr, br, wi, bi, lam):
    S = z6.shape[1]
    nt = S // RT

    def body(z_ref, ya_ref, cw_ref, cb_ref, wr_ref, br_ref, wi_ref, bi_ref, lam_ref, mg_ref, h0_ref, h1_ref, zxp, xc_s):
        sp = _softplus_neg(lam_ref[...])
        _fill_padded(zxp, z_ref.at[0], S)
        _conv_fwd_all(zxp, xc_s, cw_ref, cb_ref, S)

        def scans(i, carry):
            cu, cd = carry
            ru = pl.ds(pl.multiple_of(i * RT, RT), RT)
            rd = pl.ds(pl.multiple_of((nt - 1 - i) * RT, RT), RT)
            xu, xd = xc_s[ru, :], xc_s[rd, :]
            _, gi, a, mult = _lru_gates(xu, 0, wr_ref, br_ref, wi_ref, bi_ref, sp)
            hu, cu = _scan_up(a, mult * gi * xu, cu)
            h0_ref[ru, :] = hu
            _, gi, a, mult = _lru_gates(xd, 1, wr_ref, br_ref, wi_ref, bi_ref, sp)
            hd, cd = _scan_down(a, mult * gi * xd, cd)
            h1_ref[rd, :] = hd
            return cu, cd
        z1 = jnp.zeros((1, HD), F32)
        lax.fori_loop(0, nt, scans, (z1, z1))

        def merge(i, c):
            rows = pl.ds(pl.multiple_of(i * RT, RT), RT)
            yb = (h0_ref[rows, :] + h1_ref[rows, :]) * _gelu(z_ref[1, rows, :].astype(F32))
            mg_ref[rows, :] = (ya_ref[rows, :].astype(F32) + _sigmoid(z_ref[2, rows, :].astype(F32)) * yb).astype(BF16)
            return c
        lax.fori_loop(0, nt, merge, 0)

    col = pl.BlockSpec((S, HD), lambda h: (0, h))
    return pl.pallas_call(
        body, name="lru_fwd", grid=(NH,),
        in_specs=[pl.BlockSpec((3, S, HD), lambda h: (1, 0, h)), col] + _lru_specs(S),
        out_specs=[col, col, col],
        out_shape=[jax.ShapeDtypeStruct((S, D), BF16), jax.ShapeDtypeStruct((S, D), F32), jax.ShapeDtypeStruct((S, D), F32)],
        scratch_shapes=[pltpu.VMEM((S + 2 * PADR, HD), F32), pltpu.VMEM((S, HD), F32)],
        compiler_params=_cp(("parallel",)),
    )(z6, ya, cw, cb, wr, br, wi, bi, lam)


def _loss_head(x, tgt, g):
    S = x.shape[0]

    def body(x_ref, t_ref, g_ref, dx_ref, loss_ref, dg_ref):
        @pl.when(pl.program_id(0) == 0)
        def _():
            loss_ref[...] = jnp.zeros_like(loss_ref)
            dg_ref[...] = jnp.zeros_like(dg_ref)
        xv = x_ref[...]
        xh, _ = _rms_hat(xv)
        e = xh * g_ref[...] - t_ref[...]
        loss_ref[...] += jnp.sum(e * e) * (0.5 / D)
        dx, dgs = _rms_bwd(e * (1.0 / D), xv, g_ref[...])
        dx_ref[...] = dx
        dg_ref[...] += dgs

    return pl.pallas_call(
        body, name="loss_head", grid=(S // TM,),
        in_specs=[pl.BlockSpec((TM, D), lambda i: (i, 0)), pl.BlockSpec((TM, D), lambda i: (i, 0)),
                  pl.BlockSpec((1, D), lambda i: (0, 0))],
        out_specs=[pl.BlockSpec((TM, D), lambda i: (i, 0)), pl.BlockSpec((1, 128), lambda i: (0, 0)),
                   pl.BlockSpec((1, D), lambda i: (0, 0))],
        out_shape=[jax.ShapeDtypeStruct((S, D), F32), jax.ShapeDtypeStruct((1, 128), F32), jax.ShapeDtypeStruct((1, D), F32)],
        compiler_params=_cp(("arbitrary",)),
    )(x, tgt, g)


def _bwd_ffn_out(dx, w_fo, gu, l):
    S = dx.shape[0]

    def body(dx_ref, w_ref, ga_ref, gb_ref, o_ref, dxb, dff):
        k = pl.program_id(1)

        @pl.when(k == 0)
        def _():
            dxb[...] = dx_ref[...].astype(BF16)

        @pl.when(k < 2)
        def _():
            d = _dot_nt(dxb[...], w_ref[...])
            dff[k] = d
            ga = ga_ref[...].astype(F32)
            sg = _sigmoid(ga)
            o_ref[...] = (d * gb_ref[...].astype(F32) * sg * (1.0 + ga * (1.0 - sg))).astype(BF16)

        @pl.when(k >= 2)
        def _():
            ga = ga_ref[...].astype(F32)
            o_ref[...] = (dff[k - 2] * ga * _sigmoid(ga)).astype(BF16)

    return pl.pallas_call(
        body, name=f"bwd_ffn_out_{l}", grid=(S // TM, 4),
        in_specs=[pl.BlockSpec((TM, D), lambda i, k: (i, 0)),
                  pl.BlockSpec((None, DFF_SH, D), lambda i, k: (l, jnp.minimum(k, 1), 0)),
                  pl.BlockSpec((None, TM, DFF_SH), lambda i, k: (k % 2, i, 0)),
                  pl.BlockSpec((None, TM, DFF_SH), lambda i, k: (jnp.minimum(k, 1) + 2, i, 0))],
        out_specs=pl.BlockSpec((None, TM, DFF_SH), lambda i, k: (k, i, 0)),
        out_shape=jax.ShapeDtypeStruct((4, S, DFF_SH), BF16),
        scratch_shapes=[pltpu.VMEM((TM, D), BF16), pltpu.VMEM((2, TM, DFF_SH), F32)],
        compiler_params=_cp(("parallel", "arbitrary")),
    )(dx, w_fo, gu, gu)


def _mm_tn(a, b, m_blk, name):
    S, M = a.shape

    def body(a_ref, b_ref, o_ref):
        @pl.when(pl.program_id(1) == 0)
        def _():
            o_ref[...] = jnp.zeros_like(o_ref)
        o_ref[...] += _dot_tn(a_ref[...], b_ref[...].astype(BF16))

    return pl.pallas_call(
        body, name=name, grid=(M // m_blk, S // TM),
        in_specs=[pl.BlockSpec((TM, m_blk), lambda m, k: (k, m)), pl.BlockSpec((TM, D), lambda m, k: (k, 0))],
        out_specs=pl.BlockSpec((m_blk, D), lambda m, k: (m, 0)),
        out_shape=jax.ShapeDtypeStruct((M, D), F32),
        compiler_params=_cp(("parallel", "arbitrary")),
    )(a, b)


def _mm_nt_rms_bwd(a, a_spec, w, w_spec, nk, tm, x, g, dres, name):
    S = x.shape[0]

    def body(a_ref, w_ref, x_ref, g_ref, r_ref, dx_ref, dg_ref, acc):
        i, k = pl.program_id(0), pl.program_id(1)

        @pl.when(k == 0)
        def _():
            acc[...] = jnp.zeros_like(acc)
        acc[...] += _dot_nt(a_ref[...], w_ref[...])

        @pl.when(jnp.logical_and(i == 0, k == 0))
        def _():
            dg_ref[...] = jnp.zeros_like(dg_ref)

        @pl.when(k == nk - 1)
        def _():
            dx, dgs = _rms_bwd(acc[...], x_ref[...], g_ref[...])
            dx_ref[...] = r_ref[...] + dx
            dg_ref[...] += dgs

    row = pl.BlockSpec((tm, D), lambda i, k: (i, 0))
    vec = pl.BlockSpec((1, D), lambda i, k: (0, 0))
    return pl.pallas_call(
        body, name=name, grid=(S // tm, nk),
        in_specs=[a_spec, w_spec, row, vec, row],
        out_specs=[row, vec],
        out_shape=[jax.ShapeDtypeStruct((S, D), F32), jax.ShapeDtypeStruct((1, D), F32)],
        scratch_shapes=[pltpu.VMEM((tm, D), F32)],
        compiler_params=_cp(("arbitrary", "arbitrary")),
    )(a, w, x, g, dres)


def _dw_ffn_in(h, dgu, l):
    S = h.shape[0]

    def body(h_ref, b_ref, o_ref):
        @pl.when(pl.program_id(1) == 0)
        def _():
            o_ref[...] = jnp.zeros_like(o_ref)
        o_ref[...] += _dot_tn(h_ref[...], b_ref[...])

    return pl.pallas_call(
        body, name=f"dw_ffn_in_{l}", grid=(4, S // TM),
        in_specs=[pl.BlockSpec((TM, D), lambda j, k: (k, 0)), pl.BlockSpec((None, TM, DFF_SH), lambda j, k: (j, k, 0))],
        out_specs=pl.BlockSpec((None, D, DFF_SH), lambda j, k: (j, 0, 0)),
        out_shape=jax.ShapeDtypeStruct((4, D, DFF_SH), F32),
        compiler_params=_cp(("parallel", "arbitrary")),
    )(h, dgu)


_HALF_COMPS = ((0, 1, 3), (4, 2, 5))


def _dw_in(h, dz6, l):
    S = h.shape[0]

    def body(h_ref, d0_ref, d1_ref, d2_ref, o_ref):
        @pl.when(pl.program_id(1) == 0)
        def _():
            o_ref[...] = jnp.zeros_like(o_ref)
        hv = h_ref[...]
        for q, d_ref in enumerate((d0_ref, d1_ref, d2_ref)):
            for hf in range(2):
                col = 1024 * q + 512 * hf
                o_ref[col // 1536, :, col % 1536:col % 1536 + 512] += _dot_tn(hv, d_ref[:, 512 * hf:512 * (hf + 1)])

    def comp(q):
        return pl.BlockSpec((None, TM, D), lambda p, k: (jnp.where(p == 0, _HALF_COMPS[0][q], _HALF_COMPS[1][q]), k, 0))

    return pl.pallas_call(
        body, name=f"dw_in_{l}", grid=(2, S // TM),
        in_specs=[pl.BlockSpec((TM, D), lambda p, k: (k, 0)), comp(0), comp(1), comp(2)],
        out_specs=pl.BlockSpec((2, D, 1536), lambda p, k: (p, 0, 0)),
        out_shape=jax.ShapeDtypeStruct((4, D, 1536), F32),
        compiler_params=_cp(("parallel", "arbitrary")),
    )(h, dz6, dz6, dz6)


def _bwd_out(dx, w_o, merged, l):
    S = dx.shape[0]

    def body(dx_ref, w_ref, m_ref, dm_ref, dw_ref):
        @pl.when(pl.program_id(0) == 0)
        def _():
            dw_ref[...] = jnp.zeros_like(dw_ref)
        dxb = dx_ref[...].astype(BF16)
        dm_ref[...] = _dot_nt(dxb, w_ref[...]).astype(BF16)
        dw_ref[...] += _dot_tn(m_ref[...], dxb)

    row = pl.BlockSpec((TM, D), lambda i: (i, 0))
    return pl.pallas_call(
        body, name=f"bwd_out_{l}", grid=(S // TM,),
        in_specs=[row, pl.BlockSpec((None, D, D), lambda i: (l, 0, 0)), row],
        out_specs=[row, pl.BlockSpec((D, D), lambda i: (0, 0))],
        out_shape=[jax.ShapeDtypeStruct((S, D), BF16), jax.ShapeDtypeStruct((D, D), F32)],
        compiler_params=_cp(("arbitrary",)),
    )(dx, w_o, merged)


def _gmlp_bwd(dm, z6, ws_b, wst_b, bs_b, lg, lb):
    S = z6.shape[1]

    def body(dm_ref, z_ref, ws_ref, wst_ref, bs_ref, lg_ref, lb_ref, dz_ref, dws_ref, dbs_ref, dlg_ref, dlb_ref, mix, dv):
        @pl.when(pl.program_id(0) == 0)
        def _():
            dws_ref[...] = jnp.zeros_like(dws_ref)
            dbs_ref[...] = jnp.zeros_like(dbs_ref)
            dlg_ref[...] = jnp.zeros_like(dlg_ref)
            dlb_ref[...] = jnp.zeros_like(dlb_ref)
        gv, dgelu_v = _gelu_and_grad(z_ref[1].astype(F32))
        xc = gv - jnp.mean(gv, axis=-1, keepdims=True)
        rs = lax.rsqrt(jnp.mean(xc * xc, axis=-1, keepdims=True) + EPS)
        vh = xc * rs
        vb = (vh * lg_ref[...] + lb_ref[...]).astype(BF16)
        for gi in range(NH):
            cs = slice(gi * HD, (gi + 1) * HD)
            mix[:, cs] = _dot(ws_ref[gi], vb[:, cs])
        u, dgelu_u = _gelu_and_grad(z_ref[0].astype(F32))
        sa = _sigmoid(z_ref[2].astype(F32))
        mixed = mix[...] + bs_ref[...]
        dyg = dm_ref[...].astype(F32)
        dz_ref[2] = (dyg * u * mixed * sa * (1.0 - sa)).astype(BF16)
        dya = dyg * sa
        dz_ref[0] = (dya * mixed * dgelu_u).astype(BF16)
        dmix = dya * u
        dmb = dmix.astype(BF16)
        for gi in range(NH):
            cs = slice(gi * HD, (gi + 1) * HD)
            dv[:, cs] = _dot(wst_ref[gi], dmb[:, cs])
            dws_ref[gi] += _dot_nt(dmb[:, cs], vb[:, cs])
            dbs_ref[gi] += jnp.broadcast_to(jnp.sum(dmix[:, cs], axis=1, keepdims=True), (CHUNK, HD))
        dvv = dv[...]
        dlg_ref[...] += jnp.sum(dvv * vh, axis=0, keepdims=True)
        dlb_ref[...] += jnp.sum(dvv, axis=0, keepdims=True)
        dvh = dvv * lg_ref[...]
        dgv = rs * (dvh - jnp.mean(dvh, axis=-1, keepdims=True) - vh * jnp.mean(dvh * vh, axis=-1, keepdims=True))
        dz_ref[1] = (dgv * dgelu_v).astype(BF16)

    vec = pl.BlockSpec((1, D), lambda i: (0, 0))
    mat = pl.BlockSpec((NH, CHUNK, CHUNK), lambda i: (0, 0, 0))
    return pl.pallas_call(
        body, name="gmlp_bwd", grid=(S // CHUNK,),
        in_specs=[pl.BlockSpec((CHUNK, D), lambda i: (i, 0)), pl.BlockSpec((3, CHUNK, D), lambda i: (0, i, 0)), mat, mat,
                  pl.BlockSpec((CHUNK, D), lambda i: (0, 0)), vec, vec],
        out_specs=[pl.BlockSpec((3, CHUNK, D), lambda i: (0, i, 0)), mat, mat, vec, vec],
        out_shape=[jax.ShapeDtypeStruct((6, S, D), BF16), jax.ShapeDtypeStruct((NH, CHUNK, CHUNK), F32),
                   jax.ShapeDtypeStruct((NH, CHUNK, HD), F32), jax.ShapeDtypeStruct((1, D), F32), jax.ShapeDtypeStruct((1, D), F32)],
        scratch_shapes=[pltpu.VMEM((CHUNK, D), F32), pltpu.VMEM((CHUNK, D), F32)],
        compiler_params=_cp(("arbitrary",)),
    )(dm, z6, ws_b, wst_b, bs_b, lg, lb)


def _lru_bwd(dz6, dm, z6, h0, h1, cw, cb, wr, br, wi, bi, lam):
    S = z6.shape[1]
    nt = S // RT

    def body(dz_in, dm_ref, z_ref, h0_ref, h1_ref, cw_ref, cb_ref, wr_ref, br_ref, wi_ref, bi_ref, lam_ref,
             dz_ref, dcw_ref, dcb_ref, dwr_ref, dbr_ref, dwi_ref, dbi_ref, dlam_ref, zxp, xc_s, dhs_s, dxcp, dxc1):
        del dz_in
        lam = lam_ref[...]
        sp = _softplus_neg(lam)
        row = _row_iota()
        _fill_padded(zxp, z_ref.at[0], S)
        _conv_fwd_all(zxp, xc_s, cw_ref, cb_ref, S)
        zeros = jnp.zeros((PADR, HD), F32)
        dxcp[0:PADR, :] = zeros
        dxcp[PADR + S:2 * PADR + S, :] = zeros
        dwr_ref[...] = jnp.zeros_like(dwr_ref)
        dwi_ref[...] = jnp.zeros_like(dwi_ref)

        def pre(i, c):
            rows = pl.ds(pl.multiple_of(i * RT, RT), RT)
            hs = h0_ref[rows, :] + h1_ref[rows, :]
            dmv = dm_ref[rows, :].astype(F32)
            sb = _sigmoid(z_ref[2, rows, :].astype(F32))
            gg, dgg = _gelu_and_grad(z_ref[1, rows, :].astype(F32))
            dz_ref[2, rows, :] = (dmv * hs * gg * sb * (1.0 - sb)).astype(BF16)
            dyb = dmv * sb
            dz_ref[1, rows, :] = (dyb * hs * dgg).astype(BF16)
            dhs_s[rows, :] = dyb * gg
            return c
        lax.fori_loop(0, nt, pre, 0)

        def gate_bwd(d, gates, lamv, da, xc):
            r, gi, a, mult = gates
            dmult = lamv * gi * xc
            dgi = lamv * mult * xc
            dlog = (da - dmult * a / mult) * a
            dpr = (dlog * (-LRU_C) * sp[d:d + 1, :]) * r * (1.0 - r)
            dpi = dgi * gi * (1.0 - gi)
            xb, dprb, dpib = xc.astype(BF16), dpr.astype(BF16), dpi.astype(BF16)
            dwr_ref[d] += _dot_tn(xb, dprb)
            dwi_ref[d] += _dot_tn(xb, dpib)
            dxc = lamv * mult * gi + _dot_nt(dprb, wr_ref[d]) + _dot_nt(dpib, wi_ref[d])
            return dxc, (jnp.sum(dlog * r, axis=0, keepdims=True) * (-LRU_C), jnp.sum(dpr, axis=0, keepdims=True),
                         jnp.sum(dpi, axis=0, keepdims=True))

        def down(t0, q_next):
            rows = pl.ds(t0, RT)
            xc, h0t, dhs = xc_s[rows, :], h0_ref[rows, :], dhs_s[rows, :]
            gates = _lru_gates(xc, 0, wr_ref, br_ref, wi_ref, bi_ref, sp)
            a = gates[2]
            q, q_first = _scan_down(a, a * dhs, q_next)
            lamv = dhs + jnp.where(row == RT - 1, q_next, pltpu.roll(q, RT - 1, 0))
            tp = pl.multiple_of(jnp.maximum(t0 - PADR, 0), PADR)
            prev = jnp.where(t0 > 0, h0_ref[pl.ds(tp, PADR), :][PADR - 1:PADR, :], 0.0)
            hprev = jnp.where(row == 0, prev, pltpu.roll(h0t, 1, 0))
            dxc, sums = gate_bwd(0, gates, lamv, lamv * hprev, xc)
            dxcp[pl.ds(t0 + PADR, RT), :] = dxc
            return q_first, sums

        def up(t0, q_prev):
            rows = pl.ds(t0, RT)
            xc, h1t, dhs = xc_s[rows, :], h1_ref[rows, :], dhs_s[rows, :]
            gates = _lru_gates(xc, 1, wr_ref, br_ref, wi_ref, bi_ref, sp)
            a = gates[2]
            q, q_last = _scan_up(a, a * dhs, q_prev)
            lamv = dhs + jnp.where(row == 0, q_prev, pltpu.roll(q, 1, 0))
            tn = pl.multiple_of(jnp.minimum(t0 + RT, S - PADR), PADR)
            nxt = jnp.where(t0 + RT < S, h1_ref[pl.ds(tn, PADR), :][0:1, :], 0.0)
            hnext = jnp.where(row == RT - 1, nxt, pltpu.roll(h1t, RT - 1, 0))
            dxc, sums = gate_bwd(1, gates, lamv, lamv * hnext, xc)
            dxc1[rows, :] = dxc
            return q_last, sums

        def chains(i, carry):
            qn, qp, acc = carry
            qn, s0 = down(pl.multiple_of((nt - 1 - i) * RT, RT), qn)
            qp, s1 = up(pl.multiple_of(i * RT, RT), qp)
            return qn, qp, tuple(x + y for x, y in zip(acc, s0 + s1))

        z1 = jnp.zeros((1, HD), F32)
        _, _, (s_sp0, s_br0, s_bi0, s_sp1, s_br1, s_bi1) = lax.fori_loop(0, nt, chains, (z1, z1, (z1,) * 6))

        def add_dxc(i, c):
            t0 = pl.multiple_of(i * RT, RT)
            dxcp[pl.ds(t0 + PADR, RT), :] += dxc1[pl.ds(t0, RT), :]
            return c
        lax.fori_loop(0, nt, add_dxc, 0)

        dsp = jnp.concatenate([s_sp0, s_sp1], axis=0)
        dlam_ref[...] = -dsp * _sigmoid(-lam)
        dbr_ref[...] = jnp.concatenate([s_br0, s_br1], axis=0)
        dbi_ref[...] = jnp.concatenate([s_bi0, s_bi1], axis=0)

        def conv_bwd(i, carry):
            c0, c1, c2, c3, cb_ = carry
            t0 = pl.multiple_of(i * RT, RT)
            dwin = dxcp[pl.ds(t0, RT + 2 * PADR), :]
            d0 = _shifted(dwin, 0)
            dz_ref[0, pl.ds(t0, RT), :] = (_shifted(dwin, 1) * cw_ref[0:1, :] + d0 * cw_ref[1:2, :]
                                           + _shifted(dwin, -1) * cw_ref[2:3, :] + _shifted(dwin, -2) * cw_ref[3:4, :]).astype(BF16)
            xm1, x0, xp1, xp2 = _conv_taps(zxp[pl.ds(t0, RT + 2 * PADR), :])
            sm = lambda v: jnp.sum(v, axis=0, keepdims=True)
            return c0 + sm(d0 * xm1), c1 + sm(d0 * x0), c2 + sm(d0 * xp1), c3 + sm(d0 * xp2), cb_ + sm(d0)

        c0, c1, c2, c3, cb_ = lax.fori_loop(0, nt, conv_bwd, (z1, z1, z1, z1, z1))
        dcw_ref[...] = jnp.concatenate([c0, c1, c2, c3], axis=0)
        dcb_ref[...] = cb_

    col = pl.BlockSpec((S, HD), lambda h: (0, h))
    head = lambda h: (0, h)
    wspec = pl.BlockSpec((2, None, HD, HD), lambda h: (0, h, 0, 0))
    return pl.pallas_call(
        body, name="lru_bwd", grid=(NH,),
        in_specs=[pl.BlockSpec(memory_space=pl.ANY), col, pl.BlockSpec((3, S, HD), lambda h: (1, 0, h)), col, col] + _lru_specs(S),
        out_specs=[pl.BlockSpec((3, S, HD), lambda h: (1, 0, h)), pl.BlockSpec((4, HD), head), pl.BlockSpec((1, HD), head),
                   wspec, pl.BlockSpec((2, HD), head), wspec, pl.BlockSpec((2, HD), head), pl.BlockSpec((2, HD), head)],
        out_shape=[jax.ShapeDtypeStruct((6, S, D), BF16), jax.ShapeDtypeStruct((4, D), F32), jax.ShapeDtypeStruct((1, D), F32),
                   jax.ShapeDtypeStruct((2, NH, HD, HD), F32), jax.ShapeDtypeStruct((2, D), F32),
                   jax.ShapeDtypeStruct((2, NH, HD, HD), F32), jax.ShapeDtypeStruct((2, D), F32), jax.ShapeDtypeStruct((2, D), F32)],
        scratch_shapes=[pltpu.VMEM((S + 2 * PADR, HD), F32), pltpu.VMEM((S, HD), F32), pltpu.VMEM((S, HD), F32),
                        pltpu.VMEM((S + 2 * PADR, HD), F32), pltpu.VMEM((S, HD), F32)],
        input_output_aliases={0: 0},
        compiler_params=_cp(("parallel",)),
    )(dz6, dm, z6, h0, h1, cw, cb, wr, br, wi, bi, lam)


def _local_step(x, tgt, p):
    S = x.shape[0]
    saved = []
    for l in range(2):
        g1, g2 = p["norm1_g"][l][None], p["norm2_g"][l][None]
        ws_b = p["gmlp_w_s"][l].astype(BF16)
        tm = dict(ws_b=ws_b, wst_b=jnp.swapaxes(ws_b, 1, 2), bs_b=jnp.repeat(p["gmlp_b_s"][l].T, HD, axis=1),
                  lg=p["gmlp_ln_g"][l][None], lb=p["gmlp_ln_b"][l][None])
        lru = (p["conv_w"][l], p["conv_b"][l][None], p["lru_w_r"][l].astype(BF16), p["lru_b_r"][l],
               p["lru_w_i"][l].astype(BF16), p["lru_b_i"][l], p["lru_lambda"][l])
        z6, hn1 = _mm_in(x, g1, p["w_in"], l)
        ya = _gmlp_fwd(z6, tm["ws_b"], tm["bs_b"], tm["lg"], tm["lb"])
        merged, h0, h1 = _lru_fwd(z6, ya, *lru)
        x1 = _mm_res(merged, p["w_out"], x, l, "mm_out")
        gu, ff, hn2 = _mm_ffn_in(x1, g2, p["w_ffn_in"], l)
        x2 = _mm_res(ff, p["w_ffn_out"], x1, l, "mm_ffn_out")
        saved.append(dict(x=x, z6=z6, h0=h0, h1=h1, merged=merged, x1=x1, gu=gu, ff=ff, g1=g1, g2=g2, tm=tm, lru=lru,
                          hn1=hn1, hn2=hn2))
        x = x2

    dx, loss_v, dfg = _loss_head(x, tgt, p["final_g"][None])
    big = [None, None]
    small = {"final_g": dfg[0]}
    per_layer = {k: [None, None] for k in ("norm1_g", "gmlp_ln_g", "gmlp_ln_b", "gmlp_w_s", "gmlp_b_s", "conv_w", "conv_b",
                                           "lru_w_r", "lru_b_r", "lru_w_i", "lru_b_i", "lru_lambda", "norm2_g")}
    for l in (1, 0):
        s = saved[l]
        tm = s["tm"]
        dgu = _bwd_ffn_out(dx, p["w_ffn_out"], s["gu"], l)
        dwfo = _mm_tn(s["ff"], dx, DFF_SH, f"dw_ffn_out_{l}")
        tmb = min(TM_BIG, S)
        dx1, dg2 = _mm_nt_rms_bwd(
            dgu, pl.BlockSpec((None, tmb, DFF_SH), lambda i, k: (k, i, 0)),
            p["w_ffn_in"], pl.BlockSpec((None, None, D, DFF_SH), lambda i, k, l=l: (l, k, 0, 0)),
            4, tmb, s["x1"], s["g2"], dx, f"bwd_ffn_in_{l}")
        dwfi = _dw_ffn_in(s["hn2"], dgu, l)
        dmg, dwo = _bwd_out(dx1, p["w_out"], s["merged"], l)
        dz6, dws, dbs, dlg, dlb = _gmlp_bwd(dmg, s["z6"], tm["ws_b"], tm["wst_b"], tm["bs_b"], tm["lg"], tm["lb"])
        dz6, dcw, dcb, dwr, dbr, dwi, dbi, dlam = _lru_bwd(dz6, dmg, s["z6"], s["h0"], s["h1"], *s["lru"])

        def w_map(i, k, l=l):
            sh, tl = _in_tile(k)
            return (l, sh, 0, tl)

        dx0, dg1 = _mm_nt_rms_bwd(
            dz6, pl.BlockSpec((None, tmb, 512), lambda i, k: (k // 2, i, k % 2)),
            p["w_in"], pl.BlockSpec((None, None, D, 512), w_map),
            N_IN_T, tmb, s["x"], s["g1"], dx1, f"bwd_in_{l}")
        dwin = _dw_in(s["hn1"], dz6, l)
        dx = dx0
        big[l] = [dwin, dwo, dwfi, dwfo]
        for k, v in (("norm1_g", dg1[0]), ("gmlp_ln_g", dlg[0]), ("gmlp_ln_b", dlb[0]), ("gmlp_w_s", dws), ("gmlp_b_s", dbs[:, :, 0]),
                     ("conv_w", dcw), ("conv_b", dcb[0]), ("lru_w_r", dwr), ("lru_b_r", dbr), ("lru_w_i", dwi), ("lru_b_i", dbi),
                     ("lru_lambda", dlam), ("norm2_g", dg2[0])):
            per_layer[k][l] = v
    for k, v in per_layer.items():
        small[k] = jnp.stack(v)
    return loss_v, dx, big, small


def _place():
    x, y, c = lax.axis_index("x"), lax.axis_index("y"), lax.axis_index("c")
    return x, y, c, 2 * x + y


def _chip_at(x, y, d):
    px = 1 - x if d & 2 else x
    py = 1 - y if d & 1 else y
    return px, py, 2 * px + py


ANY = pl.BlockSpec(memory_space=pl.ANY)


def _cast_into(wf, chip_arr, name):
    _, rows, cols = wf.shape
    rh = rows // 2

    def body(ch_ref, w_ref, o_ref):
        o_ref[...] = w_ref[...].astype(BF16)

    return pl.pallas_call(
        body, name=name, out_shape=jax.ShapeDtypeStruct((2, 4, 2, rh, cols), BF16),
        grid_spec=pltpu.PrefetchScalarGridSpec(
            num_scalar_prefetch=1, grid=(2, 2),
            in_specs=[pl.BlockSpec((None, None, rh, cols), lambda l, h, ch: (l, h, 0, 0))],
            out_specs=pl.BlockSpec((None, None, None, rh, cols), lambda l, h, ch: (l, ch[0], h, 0, 0))),
        compiler_params=_cp(("parallel", "parallel")),
    )(chip_arr, wf.reshape(2, 2, rh, cols))


def _gather_weights(bufs, tiny):
    nt = len(bufs)
    n_ici = nt * 2 * 3

    def body(*refs):
        tiny_ref = refs[nt]
        o_refs, tiny_o = refs[nt + 1:2 * nt + 1], refs[2 * nt + 1]
        send, recv, fsend, frecv, tsend, trecv, lsem = refs[2 * nt + 2:]
        x, y, c, chip = _place()

        local = [pltpu.make_async_copy(tiny_ref, tiny_o.at[chip], lsem)]
        for cp in local:
            cp.start()

        def ici(t, l, d, origin_chip, to):
            k = (2 * t + l) * 3 + d - 1
            blk = o_refs[t].at[l, origin_chip, c]
            return pltpu.make_async_remote_copy(
                src_ref=blk, dst_ref=blk, send_sem=send.at[k], recv_sem=recv.at[k], device_id=to, device_id_type=MESH)

        def fwd(t, l, d, origin_chip, half):
            k = (2 * t + l) * 3 + d - 1
            blk = o_refs[t].at[l, origin_chip, half]
            return pltpu.make_async_remote_copy(
                src_ref=blk, dst_ref=blk, send_sem=fsend.at[k], recv_sem=frecv.at[k],
                device_id=(x, y, 1 - c), device_id_type=MESH)

        def tin(d, origin_chip, to):
            return pltpu.make_async_remote_copy(
                src_ref=tiny_ref, dst_ref=tiny_o.at[origin_chip], send_sem=tsend.at[d - 1], recv_sem=trecv.at[d - 1],
                device_id=to, device_id_type=MESH)

        sends = []
        for t in range(nt):
            for l in range(2):
                for d in (1, 2, 3):
                    px, py, _ = _chip_at(x, y, d)
                    sends.append(ici(t, l, d, chip, (px, py, c)))
        for d in (1, 2, 3):
            px, py, _ = _chip_at(x, y, d)
            sends.append(tin(d, chip, (px, py, c)))
        for cp in sends:
            cp.start()
        passed = []
        for t in range(nt):
            for l in range(2):
                for d in (1, 2, 3):
                    _, _, pchip = _chip_at(x, y, d)
                    ici(t, l, d, pchip, (x, y, c)).wait_recv()
                    f = fwd(t, l, d, pchip, c)
                    f.start()
                    passed.append(f)
        for t in range(nt):
            for l in range(2):
                for d in (1, 2, 3):
                    _, _, pchip = _chip_at(x, y, d)
                    fwd(t, l, d, pchip, 1 - c).wait_recv()
        for d in (1, 2, 3):
            _, _, pchip = _chip_at(x, y, d)
            tin(d, pchip, (x, y, c)).wait_recv()
        for cp in sends + passed:
            cp.wait_send()
        for cp in local:
            cp.wait()

    out_shape = [jax.ShapeDtypeStruct(b.shape, b.dtype) for b in bufs]
    out_shape.append(jax.ShapeDtypeStruct((4,) + tiny.shape, tiny.dtype))
    outs = pl.pallas_call(
        body, name="gather_weights", out_shape=out_shape,
        in_specs=[ANY] * (nt + 1), out_specs=[ANY] * (nt + 1),
        scratch_shapes=[pltpu.SemaphoreType.DMA((n_ici,)), pltpu.SemaphoreType.DMA((n_ici,)),
                        pltpu.SemaphoreType.DMA((n_ici,)), pltpu.SemaphoreType.DMA((n_ici,)),
                        pltpu.SemaphoreType.DMA((3,)), pltpu.SemaphoreType.DMA((3,)), pltpu.SemaphoreType.DMA],
        input_output_aliases={t: t for t in range(nt)},
        compiler_params=_cp(has_side_effects=True),
    )(*bufs, tiny)
    return outs[:nt], outs[nt]


def _to_sibling_halves(gs, l):
    nt = len(gs)

    def body(*refs):
        g_refs, o_refs = refs[:nt], refs[nt:2 * nt]
        send, recv = refs[2 * nt:]
        x, y, c, _ = _place()
        cps = [pltpu.make_async_remote_copy(
            src_ref=g_refs[t].at[k, 1 - c], dst_ref=o_refs[t].at[k], send_sem=send.at[4 * t + k], recv_sem=recv.at[4 * t + k],
            device_id=(x, y, 1 - c), device_id_type=MESH) for t in range(nt) for k in range(4)]
        for cp in cps:
            cp.start()
        for cp in cps:
            cp.wait()

    return pl.pallas_call(
        body, name=f"grads_to_sibling_{l}", out_shape=[jax.ShapeDtypeStruct((4,) + g.shape[2:], g.dtype) for g in gs],
        in_specs=[ANY] * nt, out_specs=[ANY] * nt,
        scratch_shapes=[pltpu.SemaphoreType.DMA((4 * nt,)), pltpu.SemaphoreType.DMA((4 * nt,))],
        compiler_params=_cp(has_side_effects=True),
    )(*gs)


def _to_chips(cs, l):
    nt = len(cs)

    def body(*refs):
        c_refs, o_refs = refs[:nt], refs[nt:2 * nt]
        send, recv = refs[2 * nt:]
        x, y, c, _ = _place()
        cps = []
        for t in range(nt):
            for d in (1, 2, 3):
                px, py, pchip = _chip_at(x, y, d)
                cps.append(pltpu.make_async_remote_copy(
                    src_ref=c_refs[t].at[pchip], dst_ref=o_refs[t].at[d - 1], send_sem=send.at[3 * t + d - 1],
                    recv_sem=recv.at[3 * t + d - 1], device_id=(px, py, c), device_id_type=MESH))
        for cp in cps:
            cp.start()
        for cp in cps:
            cp.wait()

    return pl.pallas_call(
        body, name=f"grads_to_chips_{l}", out_shape=[jax.ShapeDtypeStruct((3,) + a.shape[1:], a.dtype) for a in cs],
        in_specs=[ANY] * nt, out_specs=[ANY] * nt,
        scratch_shapes=[pltpu.SemaphoreType.DMA((3 * nt,)), pltpu.SemaphoreType.DMA((3 * nt,))],
        compiler_params=_cp(has_side_effects=True),
    )(*cs)


def _join_halves(fs, l):
    nt = len(fs)

    def body(*refs):
        o_refs = refs[nt:2 * nt]
        send, recv = refs[2 * nt:]
        x, y, c, _ = _place()
        cps = [pltpu.make_async_remote_copy(
            src_ref=o_refs[t].at[c], dst_ref=o_refs[t].at[c], send_sem=send.at[t], recv_sem=recv.at[t],
            device_id=(x, y, 1 - c), device_id_type=MESH) for t in range(nt)]
        for cp in cps:
            cp.start()
        for cp in cps:
            cp.wait()

    return pl.pallas_call(
        body, name=f"grads_join_{l}", out_shape=[jax.ShapeDtypeStruct(a.shape, a.dtype) for a in fs],
        in_specs=[ANY] * nt, out_specs=[ANY] * nt,
        scratch_shapes=[pltpu.SemaphoreType.DMA((nt,)), pltpu.SemaphoreType.DMA((nt,))],
        input_output_aliases={t: t for t in range(nt)},
        compiler_params=_cp(has_side_effects=True),
    )(*fs)


def _add_half(g, r, c_arr, name):
    _, _, rh, cols = g.shape

    def body(c_ref, g_ref, r_ref, o_ref):
        o_ref[...] = (g_ref[...] + r_ref[...]).astype(BF16)

    blk = pl.BlockSpec((None, rh, cols), lambda k, cr: (k, 0, 0))
    return pl.pallas_call(
        body, name=name, out_shape=jax.ShapeDtypeStruct((4, rh, cols), BF16),
        grid_spec=pltpu.PrefetchScalarGridSpec(
            num_scalar_prefetch=1, grid=(4,),
            in_specs=[pl.BlockSpec((None, None, rh, cols), lambda k, cr: (k, cr[0], 0, 0)), blk], out_specs=blk),
        compiler_params=_cp(("parallel",)),
    )(c_arr, g, r)


def _sum_chips(cs, r3, place_arr, name):
    _, rh, cols = cs.shape
    rb = rh // 2

    def body(pl_ref, a_ref, r0_ref, r1_ref, r2_ref, o_ref):
        up = lambda ref: ref[...].astype(F32)
        o_ref[...] = ((up(a_ref) + up(r0_ref)) + up(r1_ref)) + up(r2_ref)

    def slot(d):
        return pl.BlockSpec((None, rb, cols), lambda i, pa: (d, i, 0))

    return pl.pallas_call(
        body, name=name, out_shape=jax.ShapeDtypeStruct((2, rh, cols), F32),
        grid_spec=pltpu.PrefetchScalarGridSpec(
            num_scalar_prefetch=1, grid=(2,),
            in_specs=[pl.BlockSpec((None, rb, cols), lambda i, pa: (pa[0], i, 0)), slot(0), slot(1), slot(2)],
            out_specs=pl.BlockSpec((None, rb, cols), lambda i, pa: (pa[1], i, 0))),
        compiler_params=_cp(("parallel",)),
    )(place_arr, cs, r3, r3, r3)


def _allreduce_small(pack):
    rows = pack.shape[0]
    hr = rows // 2

    def body(p_ref, o_ref, sib, slots, s1, r1, s2, r2, s3, r3):
        x, y, c, chip = _place()
        sibling = (x, y, 1 - c)
        ex = pltpu.make_async_remote_copy(src_ref=p_ref, dst_ref=sib, send_sem=s1, recv_sem=r1,
                                          device_id=sibling, device_id_type=MESH)
        ex.start()
        ex.wait()
        half = pl.ds(pl.multiple_of(c * hr, 8), hr)
        slots[0] = p_ref[half, :] + sib[half, :]
        cps = []
        for d in (1, 2, 3):
            px, py, _ = _chip_at(x, y, d)
            cps.append(pltpu.make_async_remote_copy(
                src_ref=slots.at[0], dst_ref=slots.at[d], send_sem=s2.at[d - 1], recv_sem=r2.at[d - 1],
                device_id=(px, py, c), device_id_type=MESH))
        for cp in cps:
            cp.start()
        for cp in cps:
            cp.wait()
        tot = slots[chip]
        for k in (1, 2, 3):
            tot = tot + slots[jnp.bitwise_xor(chip, k)]
        o_ref[half, :] = tot
        back = pltpu.make_async_remote_copy(src_ref=o_ref.at[half, :], dst_ref=o_ref.at[half, :], send_sem=s3, recv_sem=r3,
                                            device_id=sibling, device_id_type=MESH)
        back.start()
        back.wait()

    vm = pl.BlockSpec(memory_space=pltpu.VMEM)
    return pl.pallas_call(
        body, name="allreduce_small", out_shape=jax.ShapeDtypeStruct((rows, 128), F32),
        in_specs=[vm], out_specs=vm,
        scratch_shapes=[pltpu.VMEM((rows, 128), F32), pltpu.VMEM((4, hr, 128), F32),
                        pltpu.SemaphoreType.DMA, pltpu.SemaphoreType.DMA, pltpu.SemaphoreType.DMA((3,)), pltpu.SemaphoreType.DMA((3,)),
                        pltpu.SemaphoreType.DMA, pltpu.SemaphoreType.DMA],
        compiler_params=_cp(has_side_effects=True),
    )(pack)


def _adam(g, w, m, v, name):
    rows, cols = g.shape
    rb = rows // 4

    def body(g_ref, w_ref, m_ref, v_ref, d_ref, m2_ref, v2_ref):
        gv = g_ref[...]
        m2 = ADAM_B1 * m_ref[...] + (1.0 - ADAM_B1) * gv
        v2 = ADAM_B2 * v_ref[...] + (1.0 - ADAM_B2) * (gv * gv)
        m_hat = m2 / (1.0 - ADAM_B1 ** ADAM_STEP)
        v_hat = v2 / (1.0 - ADAM_B2 ** ADAM_STEP)
        d_ref[...] = -ADAM_LR * (m_hat / (jnp.sqrt(v_hat) + ADAM_EPS) + ADAM_WD * w_ref[...])
        m2_ref[...] = m2
        v2_ref[...] = v2

    blk = pl.BlockSpec((rb, cols), lambda i: (i, 0))
    shp = jax.ShapeDtypeStruct((rows, cols), F32)
    return pl.pallas_call(
        body, name=name, grid=(4,), in_specs=[blk] * 4, out_specs=[blk] * 3, out_shape=[shp] * 3,
        compiler_params=_cp(("parallel",)),
    )(g, w, m, v)


def _pack(arrs, mult):
    flat = jnp.concatenate([a.reshape(-1) for a in arrs])
    rows = -(-flat.shape[0] // (128 * mult)) * mult
    return jnp.pad(flat, (0, rows * 128 - flat.shape[0])).reshape(rows, 128)


def _unpack(pack, shapes):
    flat = pack.reshape(-1)
    out, o = [], 0
    for s in shapes:
        n = 1
        for e in s:
            n *= e
        out.append(flat[o:o + n].reshape(s))
        o += n
    return out


WEIGHTS = ['norm1_g', 'w_in', 'gmlp_ln_g', 'gmlp_ln_b', 'gmlp_w_s', 'gmlp_b_s', 'conv_w', 'conv_b', 'lru_w_r', 'lru_b_r', 'lru_w_i',
           'lru_b_i', 'lru_lambda', 'w_out', 'norm2_g', 'w_ffn_in', 'w_ffn_out', 'final_g']
BIG = ['w_in', 'w_out', 'w_ffn_in', 'w_ffn_out']
SMALL = [n for n in WEIGHTS if n not in BIG]
CHIP_SHARDED_SMALL = ['conv_w', 'lru_b_r', 'lru_b_i', 'lru_lambda']


def kernel(x, norm1_g, w_in, gmlp_ln_g, gmlp_ln_b, gmlp_w_s, gmlp_b_s, conv_w, conv_b, lru_w_r, lru_b_r, lru_w_i, lru_b_i, lru_lambda, w_out, norm2_g, w_ffn_in, w_ffn_out, final_g, loss_target, m_norm1_g, m_w_in, m_gmlp_ln_g, m_gmlp_ln_b, m_gmlp_w_s, m_gmlp_b_s, m_conv_w, m_conv_b, m_lru_w_r, m_lru_b_r, m_lru_w_i, m_lru_b_i, m_lru_lambda, m_w_out, m_norm2_g, m_w_ffn_in, m_w_ffn_out, m_final_g, v_norm1_g, v_w_in, v_gmlp_ln_g, v_gmlp_ln_b, v_gmlp_w_s, v_gmlp_b_s, v_conv_w, v_conv_b, v_lru_w_r, v_lru_b_r, v_lru_w_i, v_lru_b_i, v_lru_lambda, v_w_out, v_norm2_g, v_w_ffn_in, v_w_ffn_out, v_final_g):
    a = dict(locals())
    w = {n: a[n] for n in WEIGHTS}
    mom = {n: a["m_" + n] for n in WEIGHTS}
    var = {n: a["v_" + n] for n in WEIGHTS}
    _, _, c, chip = _place()
    c_arr, chip_arr = jnp.reshape(c, (1,)).astype(jnp.int32), jnp.reshape(chip, (1,)).astype(jnp.int32)
    place_arr = jnp.stack([chip, c]).astype(jnp.int32)

    bufs = [_cast_into(w[n], chip_arr, f"cast_{n}") for n in BIG]
    tiny = _pack([w[n] for n in CHIP_SHARDED_SMALL], 8)
    full, tiny_full = _gather_weights(bufs, tiny)
    p = {n: w[n] for n in SMALL}
    for n, f in zip(BIG, full):
        p[n] = f.reshape(2, 4, 2 * f.shape[3], f.shape[4])
    p["w_out"] = p["w_out"].reshape(2, D, D)
    p["w_ffn_out"] = p["w_ffn_out"].reshape(2, DFF, D)
    parts = [_unpack(tiny_full[k], [w[n].shape for n in CHIP_SHARDED_SMALL]) for k in range(4)]
    for i, n in enumerate(CHIP_SHARDED_SMALL):
        p[n] = jnp.concatenate([parts[k][i] for k in range(4)], axis=-1)

    loss_v, dx, big, small = _local_step(x[0], loss_target[0], p)
    loss = lax.psum(loss_v[0, 0], ("x", "y", "c"))

    out = {}
    res = {n: [None, None] for n in BIG}
    for l in (1, 0):
        gs = [g.reshape(4, 2, -1, g.shape[-1]) for g in big[l]]
        from_sib = _to_sibling_halves(gs, l)
        cs = [_add_half(g, r, c_arr, f"add_half_{n}_{l}") for n, g, r in zip(BIG, gs, from_sib)]
        from_chips = _to_chips(cs, l)
        ts = [_sum_chips(cc, r3, place_arr, f"sum_chips_{n}_{l}") for n, cc, r3 in zip(BIG, cs, from_chips)]
        joined = _join_halves(ts, l)
        for n, j in zip(BIG, joined):
            g = j.reshape(w[n].shape[1:])
            res[n][l] = (g,) + tuple(_adam(g, w[n][l], mom[n][l], var[n][l], f"adam_{n}_{l}"))
    for n in BIG:
        out[n] = [jnp.stack([res[n][0][i], res[n][1][i]]) for i in range(4)]

    full_shapes = [small[n].shape for n in SMALL]
    red = _unpack(_allreduce_small(_pack([small[n] for n in SMALL], 16)), full_shapes)
    g_small = []
    for n, g in zip(SMALL, red):
        if n in CHIP_SHARDED_SMALL:
            g = lax.dynamic_slice_in_dim(g, chip * w[n].shape[-1], w[n].shape[-1], axis=g.ndim - 1)
        g_small.append(g)
    shapes = [w[n].shape for n in SMALL]
    packs = [_pack(lst, 32) for lst in (g_small, [w[n] for n in SMALL], [mom[n] for n in SMALL], [var[n] for n in SMALL])]
    upd = [_unpack(u, shapes) for u in _adam(*packs, "adam_small")]
    for i, n in enumerate(SMALL):
        out[n] = [g_small[i], upd[0][i], upd[1][i], upd[2][i]]

    return (loss, dx[None]) + tuple(out[n][i] for i in range(4) for n in WEIGHTS)
```

```python
import functools

import jax
import jax.numpy as jnp
from jax import lax
from jax.experimental import pallas as pl
from jax.experimental.pallas import tpu as pltpu

F32 = jnp.float32
BF16 = jnp.bfloat16
MESH = pl.DeviceIdType.MESH

D = 1024
NH = 8
HD = 128
CHUNK = 128
N_IN_T = 12
DFF = 2816
DFF_SH = 1408
EPS = 1e-6
LRU_C = 8.0
ADAM_LR, ADAM_B1, ADAM_B2, ADAM_EPS, ADAM_WD, ADAM_STEP = 0.001, 0.9, 0.999, 1e-08, 0.01, 10

TM = 512
TM_BIG = 1024
RT = 128
PADR = 8
VMEM_LIMIT = 56 * 1024 * 1024


def _cp(sem=None, **kw):
    if sem is not None:
        kw["dimension_semantics"] = sem
    return pltpu.CompilerParams(vmem_limit_bytes=VMEM_LIMIT, **kw)


_GC = 0.7978845608028654


def _sigmoid(x):
    return 1.0 / (1.0 + jnp.exp(-x))


def _gelu(x):
    return 0.5 * x * (1.0 + jnp.tanh(_GC * (x + 0.044715 * x * x * x)))


def _gelu_and_grad(x):
    t = jnp.tanh(_GC * (x + 0.044715 * x * x * x))
    g = 0.5 * x * (1.0 + t)
    dg = 0.5 * (1.0 + t) + 0.5 * x * (1.0 - t * t) * _GC * (1.0 + 3 * 0.044715 * x * x)
    return g, dg


def _softplus_neg(lam):
    y = jnp.exp(-jnp.abs(lam))
    u = 1.0 + y
    l1p = jnp.where(u == 1.0, y, jnp.log(u) * y / (u - 1.0))
    return jnp.maximum(-lam, 0.0) + l1p


def _dot(a, b):
    return jnp.dot(a, b, preferred_element_type=F32)


def _dot_nt(a, b):
    return lax.dot_general(a, b, (((1,), (1,)), ((), ())), preferred_element_type=F32)


def _dot_tn(a, b):
    return lax.dot_general(a, b, (((0,), (0,)), ((), ())), preferred_element_type=F32)


def _rms_hat(x):
    r = lax.rsqrt(jnp.mean(x * x, axis=-1, keepdims=True) + EPS)
    return x * r, r


def _rms_bwd(dh, x, g):
    xh, r = _rms_hat(x)
    dxh = dh * g
    dx = r * (dxh - xh * jnp.mean(dxh * xh, axis=-1, keepdims=True))
    return dx, jnp.sum(dh * xh, axis=0, keepdims=True)


def _in_tile(j):
    m, hf = j // 2, j % 2
    orig = jnp.where(m < 2, m, jnp.where(m == 2, 4, jnp.where(m < 5, m - 1, 5)))
    t = orig * 2 + hf
    return t // 3, t % 3


ANY = pl.BlockSpec(memory_space=pl.ANY)


def _mm_in(x, g, w_in, l, after=()):
    S = x.shape[0]
    tm = min(TM_BIG, S)

    def body(x_ref, g_ref, w_ref, *rest):
        o_ref, h_ref = rest[-2:]

        @pl.when(pl.program_id(1) == 0)
        def _():
            xh, _ = _rms_hat(x_ref[...])
            h_ref[...] = (xh * g_ref[...]).astype(BF16)
        o_ref[...] = _dot(h_ref[...], w_ref[...]).astype(BF16)

    def w_map(i, j):
        sh, tl = _in_tile(j)
        return (sh, 0, tl)

    return pl.pallas_call(
        body, name=f"mm_in_{l}", grid=(S // tm, N_IN_T),
        in_specs=[pl.BlockSpec((tm, D), lambda i, j: (i, 0)), pl.BlockSpec((1, D), lambda i, j: (0, 0)),
                  pl.BlockSpec((None, D, 512), w_map)] + [ANY] * len(after),
        out_specs=[pl.BlockSpec((None, tm, 512), lambda i, j: (j // 2, i, j % 2)), pl.BlockSpec((tm, D), lambda i, j: (i, 0))],
        out_shape=[jax.ShapeDtypeStruct((6, S, D), BF16), jax.ShapeDtypeStruct((S, D), BF16)],
        compiler_params=_cp(("parallel", "arbitrary")),
    )(x, g, w_in, *after)


def _mm_res(a, w, res, l, name):
    S, K = a.shape

    def body(a_ref, w_ref, r_ref, o_ref):
        o_ref[...] = r_ref[...] + _dot(a_ref[...], w_ref[...])

    return pl.pallas_call(
        body, name=f"{name}_{l}", grid=(S // TM,),
        in_specs=[pl.BlockSpec((TM, K), lambda i: (i, 0)), pl.BlockSpec((K, D), lambda i: (0, 0)),
                  pl.BlockSpec((TM, D), lambda i: (i, 0))],
        out_specs=pl.BlockSpec((TM, D), lambda i: (i, 0)),
        out_shape=jax.ShapeDtypeStruct((S, D), F32),
        compiler_params=_cp(("parallel",)),
    )(a, w, res)


def _mm_ffn_in(x, g, w_fi, l):
    S = x.shape[0]

    def body(x_ref, g_ref, w_ref, gu_ref, ff_ref, h_ref, gbuf):
        k = pl.program_id(1)

        @pl.when(k == 0)
        def _():
            xh, _ = _rms_hat(x_ref[...])
            h_ref[...] = (xh * g_ref[...]).astype(BF16)
        acc = _dot(h_ref[...], w_ref[...])
        gu_ref[...] = acc.astype(BF16)

        @pl.when(k < 2)
        def _():
            gbuf[k] = acc

        @pl.when(k >= 2)
        def _():
            ga = gbuf[k - 2]
            ff_ref[...] = (ga * _sigmoid(ga) * acc).astype(BF16)

    return pl.pallas_call(
        body, name=f"mm_ffn_in_{l}", grid=(S // TM, 4),
        in_specs=[pl.BlockSpec((TM, D), lambda i, k: (i, 0)), pl.BlockSpec((1, D), lambda i, k: (0, 0)),
                  pl.BlockSpec((None, D, DFF_SH), lambda i, k: (k, 0, 0))],
        out_specs=[pl.BlockSpec((None, TM, DFF_SH), lambda i, k: (k, i, 0)),
                   pl.BlockSpec((TM, DFF_SH), lambda i, k: (i, jnp.maximum(k - 2, 0))),
                   pl.BlockSpec((TM, D), lambda i, k: (i, 0))],
        out_shape=[jax.ShapeDtypeStruct((4, S, DFF_SH), BF16), jax.ShapeDtypeStruct((S, DFF), BF16),
                   jax.ShapeDtypeStruct((S, D), BF16)],
        scratch_shapes=[pltpu.VMEM((2, TM, DFF_SH), F32)],
        compiler_params=_cp(("parallel", "arbitrary")),
    )(x, g, w_fi)


def _gmlp_fwd(z6, ws_b, bs_b, lg, lb):
    S = z6.shape[1]

    def body(z_ref, ws_ref, bs_ref, lg_ref, lb_ref, o_ref, mix):
        gv = _gelu(z_ref[1].astype(F32))
        xc = gv - jnp.mean(gv, axis=-1, keepdims=True)
        rs = lax.rsqrt(jnp.mean(xc * xc, axis=-1, keepdims=True) + EPS)
        vb = (xc * rs * lg_ref[...] + lb_ref[...]).astype(BF16)
        for gi in range(NH):
            cs = slice(gi * HD, (gi + 1) * HD)
            mix[:, cs] = _dot(ws_ref[gi], vb[:, cs])
        o_ref[...] = (_sigmoid(z_ref[2].astype(F32)) * _gelu(z_ref[0].astype(F32)) * (mix[...] + bs_ref[...])).astype(BF16)

    return pl.pallas_call(
        body, name="gmlp_fwd", grid=(S // CHUNK,),
        in_specs=[pl.BlockSpec((3, CHUNK, D), lambda i: (0, i, 0)), pl.BlockSpec((NH, CHUNK, CHUNK), lambda i: (0, 0, 0)),
                  pl.BlockSpec((CHUNK, D), lambda i: (0, 0)), pl.BlockSpec((1, D), lambda i: (0, 0)),
                  pl.BlockSpec((1, D), lambda i: (0, 0))],
        out_specs=pl.BlockSpec((CHUNK, D), lambda i: (i, 0)),
        out_shape=jax.ShapeDtypeStruct((S, D), BF16),
        scratch_shapes=[pltpu.VMEM((CHUNK, D), F32)],
        compiler_params=_cp(("parallel",)),
    )(z6, ws_b, bs_b, lg, lb)


def _row_iota():
    return lax.broadcasted_iota(jnp.int32, (RT, HD), 0)


SUB = 8


def _scan_up(a, b, carry):
    row = lax.broadcasted_iota(jnp.int32, (SUB, HD), 0)
    masks = [(d, row >= d) for d in (1, 2, 4)]
    c = jnp.broadcast_to(carry, (SUB, HD))
    hs = []
    for j in range(RT // SUB):
        aj, bj = a[SUB * j:SUB * (j + 1)], b[SUB * j:SUB * (j + 1)]
        for d, m in masks:
            bj = bj + aj * jnp.where(m, pltpu.roll(bj, d, 0), 0.0)
            aj = aj * jnp.where(m, pltpu.roll(aj, d, 0), 1.0)
        h = bj + aj * c
        hs.append(h)
        c = jnp.broadcast_to(h[SUB - 1:SUB, :], (SUB, HD))
    return jnp.concatenate(hs, axis=0), hs[-1][SUB - 1:SUB, :]


def _scan_down(a, b, carry):
    row = lax.broadcasted_iota(jnp.int32, (SUB, HD), 0)
    masks = [(d, row < SUB - d) for d in (1, 2, 4)]
    c = jnp.broadcast_to(carry, (SUB, HD))
    hs = []
    for j in reversed(range(RT // SUB)):
        aj, bj = a[SUB * j:SUB * (j + 1)], b[SUB * j:SUB * (j + 1)]
        for d, m in masks:
            bj = bj + aj * jnp.where(m, pltpu.roll(bj, SUB - d, 0), 0.0)
            aj = aj * jnp.where(m, pltpu.roll(aj, SUB - d, 0), 1.0)
        h = bj + aj * c
        hs.append(h)
        c = jnp.broadcast_to(h[0:1, :], (SUB, HD))
    return jnp.concatenate(hs[::-1], axis=0), hs[-1][0:1, :]


def _lru_gates(xc, d, wr_ref, br_ref, wi_ref, bi_ref, sp):
    xb = xc.astype(BF16)
    r = _sigmoid(_dot(xb, wr_ref[d]) + br_ref[d:d + 1, :])
    i = _sigmoid(_dot(xb, wi_ref[d]) + bi_ref[d:d + 1, :])
    log_a = -LRU_C * r * sp[d:d + 1, :]
    a = jnp.exp(log_a)
    mult = jnp.sqrt(jnp.maximum(-jnp.tanh(log_a) * (a * a + 1.0), 0.0))
    return r, i, a, mult


def _shifted(win, k):
    w = RT + 2 * PADR
    v = win if k == 0 else pltpu.roll(win, (-k) % w, 0)
    return v[PADR:PADR + RT]


def _conv_taps(win):
    return [_shifted(win, k) for k in (-1, 0, 1, 2)]


def _fill_padded(dst, src_ref, S):
    zeros = jnp.zeros((PADR, HD), F32)
    dst[0:PADR, :] = zeros
    dst[PADR + S:2 * PADR + S, :] = zeros

    def cp(i, c):
        t0 = pl.multiple_of(i * RT, RT)
        dst[pl.ds(t0 + PADR, RT), :] = src_ref[pl.ds(t0, RT), :].astype(F32)
        return c
    lax.fori_loop(0, S // RT, cp, 0)


def _conv_fwd_all(zxp, xc_s, cw_ref, cb_ref, S):
    def cv(i, c):
        t0 = pl.multiple_of(i * RT, RT)
        xm1, x0, xp1, xp2 = _conv_taps(zxp[pl.ds(t0, RT + 2 * PADR), :])
        xc_s[pl.ds(t0, RT), :] = (cb_ref[...] + xm1 * cw_ref[0:1, :] + x0 * cw_ref[1:2, :]
                                  + xp1 * cw_ref[2:3, :] + xp2 * cw_ref[3:4, :])
        return c
    lax.fori_loop(0, S // RT, cv, 0)


def _lru_specs(S):
    head = lambda h: (0, h)
    return [pl.BlockSpec((4, HD), head), pl.BlockSpec((1, HD), head),
            pl.BlockSpec((2, None, HD, HD), lambda h: (0, h, 0, 0)), pl.BlockSpec((2, HD), head),
            pl.BlockSpec((2, None, HD, HD), lambda h: (0, h, 0, 0)), pl.BlockSpec((2, HD), head),
            pl.BlockSpec((2, HD), head)]


def _lru_fwd(z6, ya, cw, cb, wr, br, wi, bi, lam):
    S = z6.shape[1]
    nt = S // RT

    def body(z_ref, ya_ref, cw_ref, cb_ref, wr_ref, br_ref, wi_ref, bi_ref, lam_ref, mg_ref, h0_ref, h1_ref, zxp, xc_s):
        sp = _softplus_neg(lam_ref[...])
        _fill_padded(zxp, z_ref.at[0], S)
        _conv_fwd_all(zxp, xc_s, cw_ref, cb_ref, S)

        def scans(i, carry):
            cu, cd = carry
            ru = pl.ds(pl.multiple_of(i * RT, RT), RT)
            rd = pl.ds(pl.multiple_of((nt - 1 - i) * RT, RT), RT)
            xu, xd = xc_s[ru, :], xc_s[rd, :]
            _, gi, a, mult = _lru_gates(xu, 0, wr_ref, br_ref, wi_ref, bi_ref, sp)
            hu, cu = _scan_up(a, mult * gi * xu, cu)
            h0_ref[ru, :] = hu
            _, gi, a, mult = _lru_gates(xd, 1, wr_ref, br_ref, wi_ref, bi_ref, sp)
            hd, cd = _scan_down(a, mult * gi * xd, cd)
            h1_ref[rd, :] = hd
            return cu, cd
        z1 = jnp.zeros((1, HD), F32)
        lax.fori_loop(0, nt, scans, (z1, z1))

        def merge(i, c):
            rows = pl.ds(pl.multiple_of(i * RT, RT), RT)
            yb = (h0_ref[rows, :] + h1_ref[rows, :]) * _gelu(z_ref[1, rows, :].astype(F32))
            mg_ref[rows, :] = (ya_ref[rows, :].astype(F32) + _sigmoid(z_ref[2, rows, :].astype(F32)) * yb).astype(BF16)
            return c
        lax.fori_loop(0, nt, merge, 0)

    col = pl.BlockSpec((S, HD), lambda h: (0, h))
    return pl.pallas_call(
        body, name="lru_fwd", grid=(NH,),
        in_specs=[pl.BlockSpec((3, S, HD), lambda h: (1, 0, h)), col] + _lru_specs(S),
        out_specs=[col, col, col],
        out_shape=[jax.ShapeDtypeStruct((S, D), BF16), jax.ShapeDtypeStruct((S, D), F32), jax.ShapeDtypeStruct((S, D), F32)],
        scratch_shapes=[pltpu.VMEM((S + 2 * PADR, HD), F32), pltpu.VMEM((S, HD), F32)],
        compiler_params=_cp(("parallel",)),
    )(z6, ya, cw, cb, wr, br, wi, bi, lam)


def _loss_head(x, tgt, g):
    S = x.shape[0]

    def body(x_ref, t_ref, g_ref, dx_ref, loss_ref, dg_ref):
        @pl.when(pl.program_id(0) == 0)
        def _():
            loss_ref[...] = jnp.zeros_like(loss_ref)
            dg_ref[...] = jnp.zeros_like(dg_ref)
        xv = x_ref[...]
        xh, _ = _rms_hat(xv)
        e = xh * g_ref[...] - t_ref[...]
        loss_ref[...] += jnp.sum(e * e) * (0.5 / D)
        dx, dgs = _rms_bwd(e * (1.0 / D), xv, g_ref[...])
        dx_ref[...] = dx
        dg_ref[...] += dgs

    return pl.pallas_call(
        body, name="loss_head", grid=(S // TM,),
        in_specs=[pl.BlockSpec((TM, D), lambda i: (i, 0)), pl.BlockSpec((TM, D), lambda i: (i, 0)),
                  pl.BlockSpec((1, D), lambda i: (0, 0))],
        out_specs=[pl.BlockSpec((TM, D), lambda i: (i, 0)), pl.BlockSpec((1, 128), lambda i: (0, 0)),
                   pl.BlockSpec((1, D), lambda i: (0, 0))],
        out_shape=[jax.ShapeDtypeStruct((S, D), F32), jax.ShapeDtypeStruct((1, 128), F32), jax.ShapeDtypeStruct((1, D), F32)],
        compiler_params=_cp(("arbitrary",)),
    )(x, tgt, g)


def _bwd_ffn_out(dx, w_fo, gu, l, after=()):
    S = dx.shape[0]

    def body(dx_ref, w_ref, ga_ref, gb_ref, *rest):
        o_ref, dxb, dff = rest[-3:]
        k = pl.program_id(1)

        @pl.when(k == 0)
        def _():
            dxb[...] = dx_ref[...].astype(BF16)

        @pl.when(k < 2)
        def _():
            d = _dot_nt(dxb[...], w_ref[...])
            dff[k] = d
            ga = ga_ref[...].astype(F32)
            sg = _sigmoid(ga)
            o_ref[...] = (d * gb_ref[...].astype(F32) * sg * (1.0 + ga * (1.0 - sg))).astype(BF16)

        @pl.when(k >= 2)
        def _():
            ga = ga_ref[...].astype(F32)
            o_ref[...] = (dff[k - 2] * ga * _sigmoid(ga)).astype(BF16)

    return pl.pallas_call(
        body, name=f"bwd_ffn_out_{l}", grid=(S // TM, 4),
        in_specs=[pl.BlockSpec((TM, D), lambda i, k: (i, 0)),
                  pl.BlockSpec((DFF_SH, D), lambda i, k: (jnp.minimum(k, 1), 0)),
                  pl.BlockSpec((None, TM, DFF_SH), lambda i, k: (k % 2, i, 0)),
                  pl.BlockSpec((None, TM, DFF_SH), lambda i, k: (jnp.minimum(k, 1) + 2, i, 0))] + [ANY] * len(after),
        out_specs=pl.BlockSpec((None, TM, DFF_SH), lambda i, k: (k, i, 0)),
        out_shape=jax.ShapeDtypeStruct((4, S, DFF_SH), BF16),
        scratch_shapes=[pltpu.VMEM((TM, D), BF16), pltpu.VMEM((2, TM, DFF_SH), F32)],
        compiler_params=_cp(("parallel", "arbitrary")),
    )(dx, w_fo, gu, gu, *after)


def _mm_tn(a, b, m_blk, name):
    S, M = a.shape

    def body(a_ref, b_ref, o_ref):
        @pl.when(pl.program_id(1) == 0)
        def _():
            o_ref[...] = jnp.zeros_like(o_ref)
        o_ref[...] += _dot_tn(a_ref[...], b_ref[...].astype(BF16))

    return pl.pallas_call(
        body, name=name, grid=(M // m_blk, S // TM),
        in_specs=[pl.BlockSpec((TM, m_blk), lambda m, k: (k, m)), pl.BlockSpec((TM, D), lambda m, k: (k, 0))],
        out_specs=pl.BlockSpec((m_blk, D), lambda m, k: (m, 0)),
        out_shape=jax.ShapeDtypeStruct((M, D), F32),
        compiler_params=_cp(("parallel", "arbitrary")),
    )(a, b)


def _mm_nt_rms_bwd(a, a_spec, w, w_spec, nk, tm, x, g, dres, name):
    S = x.shape[0]

    def body(a_ref, w_ref, x_ref, g_ref, r_ref, dx_ref, dg_ref, acc):
        i, k = pl.program_id(0), pl.program_id(1)

        @pl.when(k == 0)
        def _():
            acc[...] = jnp.zeros_like(acc)
        acc[...] += _dot_nt(a_ref[...], w_ref[...])

        @pl.when(jnp.logical_and(i == 0, k == 0))
        def _():
            dg_ref[...] = jnp.zeros_like(dg_ref)

        @pl.when(k == nk - 1)
        def _():
            dx, dgs = _rms_bwd(acc[...], x_ref[...], g_ref[...])
            dx_ref[...] = r_ref[...] + dx
            dg_ref[...] += dgs

    row = pl.BlockSpec((tm, D), lambda i, k: (i, 0))
    vec = pl.BlockSpec((1, D), lambda i, k: (0, 0))
    return pl.pallas_call(
        body, name=name, grid=(S // tm, nk),
        in_specs=[a_spec, w_spec, row, vec, row],
        out_specs=[row, vec],
        out_shape=[jax.ShapeDtypeStruct((S, D), F32), jax.ShapeDtypeStruct((1, D), F32)],
        scratch_shapes=[pltpu.VMEM((tm, D), F32)],
        compiler_params=_cp(("arbitrary", "arbitrary")),
    )(a, w, x, g, dres)


def _dw_ffn_in(h, dgu, l):
    S = h.shape[0]

    def body(h_ref, b_ref, o_ref):
        @pl.when(pl.program_id(1) == 0)
        def _():
            o_ref[...] = jnp.zeros_like(o_ref)
        o_ref[...] += _dot_tn(h_ref[...], b_ref[...])

    return pl.pallas_call(
        body, name=f"dw_ffn_in_{l}", grid=(4, S // TM),
        in_specs=[pl.BlockSpec((TM, D), lambda j, k: (k, 0)), pl.BlockSpec((None, TM, DFF_SH), lambda j, k: (j, k, 0))],
        out_specs=pl.BlockSpec((None, D, DFF_SH), lambda j, k: (j, 0, 0)),
        out_shape=jax.ShapeDtypeStruct((4, D, DFF_SH), F32),
        compiler_params=_cp(("parallel", "arbitrary")),
    )(h, dgu)


_HALF_COMPS = ((0, 1, 3), (4, 2, 5))


def _dw_in(h, dz6, l):
    S = h.shape[0]

    def body(h_ref, d0_ref, d1_ref, d2_ref, o_ref):
        @pl.when(pl.program_id(1) == 0)
        def _():
            o_ref[...] = jnp.zeros_like(o_ref)
        hv = h_ref[...]
        for q, d_ref in enumerate((d0_ref, d1_ref, d2_ref)):
            for hf in range(2):
                col = 1024 * q + 512 * hf
                o_ref[col // 1536, :, col % 1536:col % 1536 + 512] += _dot_tn(hv, d_ref[:, 512 * hf:512 * (hf + 1)])

    def comp(q):
        return pl.BlockSpec((None, TM, D), lambda p, k: (jnp.where(p == 0, _HALF_COMPS[0][q], _HALF_COMPS[1][q]), k, 0))

    return pl.pallas_call(
        body, name=f"dw_in_{l}", grid=(2, S // TM),
        in_specs=[pl.BlockSpec((TM, D), lambda p, k: (k, 0)), comp(0), comp(1), comp(2)],
        out_specs=pl.BlockSpec((2, D, 1536), lambda p, k: (p, 0, 0)),
        out_shape=jax.ShapeDtypeStruct((4, D, 1536), F32),
        compiler_params=_cp(("parallel", "arbitrary")),
    )(h, dz6, dz6, dz6)


def _bwd_out(dx, w_o, merged, l):
    S = dx.shape[0]

    def body(dx_ref, w_ref, m_ref, dm_ref, dw_ref):
        @pl.when(pl.program_id(0) == 0)
        def _():
            dw_ref[...] = jnp.zeros_like(dw_ref)
        dxb = dx_ref[...].astype(BF16)
        dm_ref[...] = _dot_nt(dxb, w_ref[...]).astype(BF16)
        dw_ref[...] += _dot_tn(m_ref[...], dxb)

    row = pl.BlockSpec((TM, D), lambda i: (i, 0))
    return pl.pallas_call(
        body, name=f"bwd_out_{l}", grid=(S // TM,),
        in_specs=[row, pl.BlockSpec((D, D), lambda i: (0, 0)), row],
        out_specs=[row, pl.BlockSpec((D, D), lambda i: (0, 0))],
        out_shape=[jax.ShapeDtypeStruct((S, D), BF16), jax.ShapeDtypeStruct((D, D), F32)],
        compiler_params=_cp(("arbitrary",)),
    )(dx, w_o, merged)


def _gmlp_bwd(dm, z6, ws_b, wst_b, bs_b, lg, lb):
    S = z6.shape[1]

    def body(dm_ref, z_ref, ws_ref, wst_ref, bs_ref, lg_ref, lb_ref, dz_ref, dws_ref, dbs_ref, dlg_ref, dlb_ref, mix, dv):
        @pl.when(pl.program_id(0) == 0)
        def _():
            dws_ref[...] = jnp.zeros_like(dws_ref)
            dbs_ref[...] = jnp.zeros_like(dbs_ref)
            dlg_ref[...] = jnp.zeros_like(dlg_ref)
            dlb_ref[...] = jnp.zeros_like(dlb_ref)
        gv, dgelu_v = _gelu_and_grad(z_ref[1].astype(F32))
        xc = gv - jnp.mean(gv, axis=-1, keepdims=True)
        rs = lax.rsqrt(jnp.mean(xc * xc, axis=-1, keepdims=True) + EPS)
        vh = xc * rs
        vb = (vh * lg_ref[...] + lb_ref[...]).astype(BF16)
        for gi in range(NH):
            cs = slice(gi * HD, (gi + 1) * HD)
            mix[:, cs] = _dot(ws_ref[gi], vb[:, cs])
        u, dgelu_u = _gelu_and_grad(z_ref[0].astype(F32))
        sa = _sigmoid(z_ref[2].astype(F32))
        mixed = mix[...] + bs_ref[...]
        dyg = dm_ref[...].astype(F32)
        dz_ref[2] = (dyg * u * mixed * sa * (1.0 - sa)).astype(BF16)
        dya = dyg * sa
        dz_ref[0] = (dya * mixed * dgelu_u).astype(BF16)
        dmix = dya * u
        dmb = dmix.astype(BF16)
        for gi in range(NH):
            cs = slice(gi * HD, (gi + 1) * HD)
            dv[:, cs] = _dot(wst_ref[gi], dmb[:, cs])
            dws_ref[gi] += _dot_nt(dmb[:, cs], vb[:, cs])
            dbs_ref[gi] += jnp.broadcast_to(jnp.sum(dmix[:, cs], axis=1, keepdims=True), (CHUNK, HD))
        dvv = dv[...]
        dlg_ref[...] += jnp.sum(dvv * vh, axis=0, keepdims=True)
        dlb_ref[...] += jnp.sum(dvv, axis=0, keepdims=True)
        dvh = dvv * lg_ref[...]
        dgv = rs * (dvh - jnp.mean(dvh, axis=-1, keepdims=True) - vh * jnp.mean(dvh * vh, axis=-1, keepdims=True))
        dz_ref[1] = (dgv * dgelu_v).astype(BF16)

    vec = pl.BlockSpec((1, D), lambda i: (0, 0))
    mat = pl.BlockSpec((NH, CHUNK, CHUNK), lambda i: (0, 0, 0))
    return pl.pallas_call(
        body, name="gmlp_bwd", grid=(S // CHUNK,),
        in_specs=[pl.BlockSpec((CHUNK, D), lambda i: (i, 0)), pl.BlockSpec((3, CHUNK, D), lambda i: (0, i, 0)), mat, mat,
                  pl.BlockSpec((CHUNK, D), lambda i: (0, 0)), vec, vec],
        out_specs=[pl.BlockSpec((3, CHUNK, D), lambda i: (0, i, 0)), mat, mat, vec, vec],
        out_shape=[jax.ShapeDtypeStruct((6, S, D), BF16), jax.ShapeDtypeStruct((NH, CHUNK, CHUNK), F32),
                   jax.ShapeDtypeStruct((NH, CHUNK, HD), F32), jax.ShapeDtypeStruct((1, D), F32), jax.ShapeDtypeStruct((1, D), F32)],
        scratch_shapes=[pltpu.VMEM((CHUNK, D), F32), pltpu.VMEM((CHUNK, D), F32)],
        compiler_params=_cp(("arbitrary",)),
    )(dm, z6, ws_b, wst_b, bs_b, lg, lb)


def _lru_bwd(dz6, dm, z6, h0, h1, cw, cb, wr, br, wi, bi, lam):
    S = z6.shape[1]
    nt = S // RT

    def body(dz_in, dm_ref, z_ref, h0_ref, h1_ref, cw_ref, cb_ref, wr_ref, br_ref, wi_ref, bi_ref, lam_ref,
             dz_ref, dcw_ref, dcb_ref, dwr_ref, dbr_ref, dwi_ref, dbi_ref, dlam_ref, zxp, xc_s, dhs_s, dxcp, dxc1):
        del dz_in
        lam = lam_ref[...]
        sp = _softplus_neg(lam)
        row = _row_iota()
        _fill_padded(zxp, z_ref.at[0], S)
        _conv_fwd_all(zxp, xc_s, cw_ref, cb_ref, S)
        zeros = jnp.zeros((PADR, HD), F32)
        dxcp[0:PADR, :] = zeros
        dxcp[PADR + S:2 * PADR + S, :] = zeros
        dwr_ref[...] = jnp.zeros_like(dwr_ref)
        dwi_ref[...] = jnp.zeros_like(dwi_ref)

        def pre(i, c):
            rows = pl.ds(pl.multiple_of(i * RT, RT), RT)
            hs = h0_ref[rows, :] + h1_ref[rows, :]
            dmv = dm_ref[rows, :].astype(F32)
            sb = _sigmoid(z_ref[2, rows, :].astype(F32))
            gg, dgg = _gelu_and_grad(z_ref[1, rows, :].astype(F32))
            dz_ref[2, rows, :] = (dmv * hs * gg * sb * (1.0 - sb)).astype(BF16)
            dyb = dmv * sb
            dz_ref[1, rows, :] = (dyb * hs * dgg).astype(BF16)
            dhs_s[rows, :] = dyb * gg
            return c
        lax.fori_loop(0, nt, pre, 0)

        def gate_bwd(d, gates, lamv, da, xc):
            r, gi, a, mult = gates
            dmult = lamv * gi * xc
            dgi = lamv * mult * xc
            dlog = (da - dmult * a / mult) * a
            dpr = (dlog * (-LRU_C) * sp[d:d + 1, :]) * r * (1.0 - r)
            dpi = dgi * gi * (1.0 - gi)
            xb, dprb, dpib = xc.astype(BF16), dpr.astype(BF16), dpi.astype(BF16)
            dwr_ref[d] += _dot_tn(xb, dprb)
            dwi_ref[d] += _dot_tn(xb, dpib)
            dxc = lamv * mult * gi + _dot_nt(dprb, wr_ref[d]) + _dot_nt(dpib, wi_ref[d])
            return dxc, (jnp.sum(dlog * r, axis=0, keepdims=True) * (-LRU_C), jnp.sum(dpr, axis=0, keepdims=True),
                         jnp.sum(dpi, axis=0, keepdims=True))

        def down(t0, q_next):
            rows = pl.ds(t0, RT)
            xc, h0t, dhs = xc_s[rows, :], h0_ref[rows, :], dhs_s[rows, :]
            gates = _lru_gates(xc, 0, wr_ref, br_ref, wi_ref, bi_ref, sp)
            a = gates[2]
            q, q_first = _scan_down(a, a * dhs, q_next)
            lamv = dhs + jnp.where(row == RT - 1, q_next, pltpu.roll(q, RT - 1, 0))
            tp = pl.multiple_of(jnp.maximum(t0 - PADR, 0), PADR)
            prev = jnp.where(t0 > 0, h0_ref[pl.ds(tp, PADR), :][PADR - 1:PADR, :], 0.0)
            hprev = jnp.where(row == 0, prev, pltpu.roll(h0t, 1, 0))
            dxc, sums = gate_bwd(0, gates, lamv, lamv * hprev, xc)
            dxcp[pl.ds(t0 + PADR, RT), :] = dxc
            return q_first, sums

        def up(t0, q_prev):
            rows = pl.ds(t0, RT)
            xc, h1t, dhs = xc_s[rows, :], h1_ref[rows, :], dhs_s[rows, :]
            gates = _lru_gates(xc, 1, wr_ref, br_ref, wi_ref, bi_ref, sp)
            a = gates[2]
            q, q_last = _scan_up(a, a * dhs, q_prev)
            lamv = dhs + jnp.where(row == 0, q_prev, pltpu.roll(q, 1, 0))
            tn = pl.multiple_of(jnp.minimum(t0 + RT, S - PADR), PADR)
            nxt = jnp.where(t0 + RT < S, h1_ref[pl.ds(tn, PADR), :][0:1, :], 0.0)
            hnext = jnp.where(row == RT - 1, nxt, pltpu.roll(h1t, RT - 1, 0))
            dxc, sums = gate_bwd(1, gates, lamv, lamv * hnext, xc)
            dxc1[rows, :] = dxc
            return q_last, sums

        def chains(i, carry):
            qn, qp, acc = carry
            qn, s0 = down(pl.multiple_of((nt - 1 - i) * RT, RT), qn)
            qp, s1 = up(pl.multiple_of(i * RT, RT), qp)
            return qn, qp, tuple(x + y for x, y in zip(acc, s0 + s1))

        z1 = jnp.zeros((1, HD), F32)
        _, _, (s_sp0, s_br0, s_bi0, s_sp1, s_br1, s_bi1) = lax.fori_loop(0, nt, chains, (z1, z1, (z1,) * 6))

        def add_dxc(i, c):
            t0 = pl.multiple_of(i * RT, RT)
            dxcp[pl.ds(t0 + PADR, RT), :] += dxc1[pl.ds(t0, RT), :]
            return c
        lax.fori_loop(0, nt, add_dxc, 0)

        dsp = jnp.concatenate([s_sp0, s_sp1], axis=0)
        dlam_ref[...] = -dsp * _sigmoid(-lam)
        dbr_ref[...] = jnp.concatenate([s_br0, s_br1], axis=0)
        dbi_ref[...] = jnp.concatenate([s_bi0, s_bi1], axis=0)

        def conv_bwd(i, carry):
            c0, c1, c2, c3, cb_ = carry
            t0 = pl.multiple_of(i * RT, RT)
            dwin = dxcp[pl.ds(t0, RT + 2 * PADR), :]
            d0 = _shifted(dwin, 0)
            dz_ref[0, pl.ds(t0, RT), :] = (_shifted(dwin, 1) * cw_ref[0:1, :] + d0 * cw_ref[1:2, :]
                                           + _shifted(dwin, -1) * cw_ref[2:3, :] + _shifted(dwin, -2) * cw_ref[3:4, :]).astype(BF16)
            xm1, x0, xp1, xp2 = _conv_taps(zxp[pl.ds(t0, RT + 2 * PADR), :])
            sm = lambda v: jnp.sum(v, axis=0, keepdims=True)
            return c0 + sm(d0 * xm1), c1 + sm(d0 * x0), c2 + sm(d0 * xp1), c3 + sm(d0 * xp2), cb_ + sm(d0)

        c0, c1, c2, c3, cb_ = lax.fori_loop(0, nt, conv_bwd, (z1, z1, z1, z1, z1))
        dcw_ref[...] = jnp.concatenate([c0, c1, c2, c3], axis=0)
        dcb_ref[...] = cb_

    col = pl.BlockSpec((S, HD), lambda h: (0, h))
    head = lambda h: (0, h)
    wspec = pl.BlockSpec((2, None, HD, HD), lambda h: (0, h, 0, 0))
    return pl.pallas_call(
        body, name="lru_bwd", grid=(NH,),
        in_specs=[pl.BlockSpec(memory_space=pl.ANY), col, pl.BlockSpec((3, S, HD), lambda h: (1, 0, h)), col, col] + _lru_specs(S),
        out_specs=[pl.BlockSpec((3, S, HD), lambda h: (1, 0, h)), pl.BlockSpec((4, HD), head), pl.BlockSpec((1, HD), head),
                   wspec, pl.BlockSpec((2, HD), head), wspec, pl.BlockSpec((2, HD), head), pl.BlockSpec((2, HD), head)],
        out_shape=[jax.ShapeDtypeStruct((6, S, D), BF16), jax.ShapeDtypeStruct((4, D), F32), jax.ShapeDtypeStruct((1, D), F32),
                   jax.ShapeDtypeStruct((2, NH, HD, HD), F32), jax.ShapeDtypeStruct((2, D), F32),
                   jax.ShapeDtypeStruct((2, NH, HD, HD), F32), jax.ShapeDtypeStruct((2, D), F32), jax.ShapeDtypeStruct((2, D), F32)],
        scratch_shapes=[pltpu.VMEM((S + 2 * PADR, HD), F32), pltpu.VMEM((S, HD), F32), pltpu.VMEM((S, HD), F32),
                        pltpu.VMEM((S + 2 * PADR, HD), F32), pltpu.VMEM((S, HD), F32)],
        input_output_aliases={0: 0},
        compiler_params=_cp(("parallel",)),
    )(dz6, dm, z6, h0, h1, cw, cb, wr, br, wi, bi, lam)


LAYER_SMALL = ("norm1_g", "gmlp_ln_g", "gmlp_ln_b", "gmlp_w_s", "gmlp_b_s", "conv_w", "conv_b",
               "lru_w_r", "lru_b_r", "lru_w_i", "lru_b_i", "lru_lambda", "norm2_g")


def _forward_layer(l, x, p, wb, after=()):
    g1, g2 = p["norm1_g"][l][None], p["norm2_g"][l][None]
    ws_b = p["gmlp_w_s"][l].astype(BF16)
    tm = dict(ws_b=ws_b, wst_b=jnp.swapaxes(ws_b, 1, 2), bs_b=jnp.repeat(p["gmlp_b_s"][l].T, HD, axis=1),
              lg=p["gmlp_ln_g"][l][None], lb=p["gmlp_ln_b"][l][None])
    lru = (p["conv_w"][l], p["conv_b"][l][None], p["lru_w_r"][l].astype(BF16), p["lru_b_r"][l],
           p["lru_w_i"][l].astype(BF16), p["lru_b_i"][l], p["lru_lambda"][l])
    z6, hn1 = _mm_in(x, g1, wb["w_in"], l, after)
    ya = _gmlp_fwd(z6, tm["ws_b"], tm["bs_b"], tm["lg"], tm["lb"])
    merged, h0, h1 = _lru_fwd(z6, ya, *lru)
    x1 = _mm_res(merged, wb["w_out"], x, l, "mm_out")
    gu, ff, hn2 = _mm_ffn_in(x1, g2, wb["w_ffn_in"], l)
    x2 = _mm_res(ff, wb["w_ffn_out"], x1, l, "mm_ffn_out")
    return x2, dict(x=x, z6=z6, h0=h0, h1=h1, merged=merged, x1=x1, gu=gu, ff=ff, g1=g1, g2=g2, tm=tm, lru=lru,
                    hn1=hn1, hn2=hn2, wb=wb)


def _backward_layer(l, dx, s, after=()):
    S = dx.shape[0]
    tm, wb = s["tm"], s["wb"]
    g2 = s["g2"]
    dgu = _bwd_ffn_out(dx, wb["w_ffn_out"], s["gu"], l, after)
    dwfo = _mm_tn(s["ff"], dx, DFF_SH, f"dw_ffn_out_{l}")
    tmb = min(TM_BIG, S)
    dx1, dg2 = _mm_nt_rms_bwd(
        dgu, pl.BlockSpec((None, tmb, DFF_SH), lambda i, k: (k, i, 0)),
        wb["w_ffn_in"], pl.BlockSpec((None, D, DFF_SH), lambda i, k: (k, 0, 0)),
        4, tmb, s["x1"], g2, dx, f"bwd_ffn_in_{l}")
    dwfi = _dw_ffn_in(s["hn2"], dgu, l)
    dmg, dwo = _bwd_out(dx1, wb["w_out"], s["merged"], l)
    dz6, dws, dbs, dlg, dlb = _gmlp_bwd(dmg, s["z6"], tm["ws_b"], tm["wst_b"], tm["bs_b"], tm["lg"], tm["lb"])
    dz6, dcw, dcb, dwr, dbr, dwi, dbi, dlam = _lru_bwd(dz6, dmg, s["z6"], s["h0"], s["h1"], *s["lru"])

    def w_map(i, k):
        sh, tl = _in_tile(k)
        return (sh, 0, tl)

    dx0, dg1 = _mm_nt_rms_bwd(
        dz6, pl.BlockSpec((None, tmb, 512), lambda i, k: (k // 2, i, k % 2)),
        wb["w_in"], pl.BlockSpec((None, D, 512), w_map),
        N_IN_T, tmb, s["x"], s["g1"], dx1, f"bwd_in_{l}")
    dwin = _dw_in(s["hn1"], dz6, l)
    small = dict(norm1_g=dg1[0], gmlp_ln_g=dlg[0], gmlp_ln_b=dlb[0], gmlp_w_s=dws, gmlp_b_s=dbs[:, :, 0], conv_w=dcw, conv_b=dcb[0],
                 lru_w_r=dwr, lru_b_r=dbr, lru_w_i=dwi, lru_b_i=dbi, lru_lambda=dlam, norm2_g=dg2[0])
    return dx0, [dwin, dwo, dwfi, dwfo], small


def _local_step(x, tgt, p, wbs):
    saved = []
    for l in range(2):
        x, s = _forward_layer(l, x, p, wbs[l])
        saved.append(s)
    dx, loss_v, dfg = _loss_head(x, tgt, p["final_g"][None])
    big, smalls = [None, None], [None, None]
    for l in (1, 0):
        dx, big[l], smalls[l] = _backward_layer(l, dx, saved[l])
    small = {k: jnp.stack([smalls[0][k], smalls[1][k]]) for k in LAYER_SMALL}
    small["final_g"] = dfg[0]
    return loss_v, dx, big, small


def _place():
    x, y, c = lax.axis_index("x"), lax.axis_index("y"), lax.axis_index("c")
    return x, y, c, 2 * x + y


def _chip_at(x, y, d):
    px = 1 - x if d & 2 else x
    py = 1 - y if d & 1 else y
    return px, py, 2 * px + py


HBM = pl.BlockSpec(memory_space=pltpu.HBM)
SEM = pl.BlockSpec(memory_space=pltpu.SEMAPHORE)
DATAFLOW = pltpu.SideEffectType.DATAFLOW_SIDE_EFFECTING


def _in_hbm(a):
    return pltpu.with_memory_space_constraint(a, pltpu.HBM)


def _cast_into(wf, l, chip_arr, name):
    _, rows, cols = wf.shape
    rh = rows // 2

    def body(ch_ref, w_ref, o_ref):
        o_ref[...] = w_ref[...].astype(BF16)

    return pl.pallas_call(
        body, name=name, out_shape=jax.ShapeDtypeStruct((4, 2, rh, cols), BF16),
        grid_spec=pltpu.PrefetchScalarGridSpec(
            num_scalar_prefetch=1, grid=(2,),
            in_specs=[pl.BlockSpec((None, None, rh, cols), lambda h, ch: (l, h, 0, 0))],
            out_specs=pl.BlockSpec((None, None, rh, cols), lambda h, ch: (ch[0], h, 0, 0))),
        compiler_params=_cp(("parallel",)),
    )(chip_arr, wf.reshape(2, 2, rh, cols))


def _half_block(ref, chip, half, to, send_sem, recv_sem):
    blk = ref.at[chip, half]
    return pltpu.make_async_remote_copy(src_ref=blk, dst_ref=blk, send_sem=send_sem, recv_sem=recv_sem,
                                        device_id=to, device_id_type=MESH)


def _gather_weights(bufs, tiny):
    nt = len(bufs)
    n_ici = nt * 3

    def body(*refs):
        tiny_ref = refs[nt]
        o_refs, tiny_o = refs[nt + 1:2 * nt + 1], refs[2 * nt + 1]
        send, recv, fsend, frecv, tsend, trecv, lsem = refs[2 * nt + 2:]
        x, y, c, chip = _place()
        local = pltpu.make_async_copy(tiny_ref, tiny_o.at[chip], lsem)
        local.start()

        def tin(d, origin_chip, to):
            return pltpu.make_async_remote_copy(
                src_ref=tiny_ref, dst_ref=tiny_o.at[origin_chip], send_sem=tsend.at[d - 1], recv_sem=trecv.at[d - 1],
                device_id=to, device_id_type=MESH)

        sends = []
        for t in range(nt):
            for d in (1, 2, 3):
                px, py, _ = _chip_at(x, y, d)
                sends.append(_half_block(o_refs[t], chip, c, (px, py, c), send.at[3 * t + d - 1], recv.at[3 * t + d - 1]))
        for d in (1, 2, 3):
            px, py, _ = _chip_at(x, y, d)
            sends.append(tin(d, chip, (px, py, c)))
        for cp in sends:
            cp.start()
        passed = []
        for t in range(nt):
            for d in (1, 2, 3):
                k = 3 * t + d - 1
                _, _, pchip = _chip_at(x, y, d)
                _half_block(o_refs[t], pchip, c, (x, y, c), send.at[k], recv.at[k]).wait_recv()
                f = _half_block(o_refs[t], pchip, c, (x, y, 1 - c), fsend.at[k], frecv.at[k])
                f.start()
                passed.append(f)
        for t in range(nt):
            for d in (1, 2, 3):
                k = 3 * t + d - 1
                _, _, pchip = _chip_at(x, y, d)
                _half_block(o_refs[t], pchip, 1 - c, (x, y, 1 - c), fsend.at[k], frecv.at[k]).wait_recv()
        for d in (1, 2, 3):
            _, _, pchip = _chip_at(x, y, d)
            tin(d, pchip, (x, y, c)).wait_recv()
        for cp in sends + passed:
            cp.wait_send()
        local.wait()

    out_shape = [jax.ShapeDtypeStruct(b.shape, b.dtype) for b in bufs]
    out_shape.append(jax.ShapeDtypeStruct((4,) + tiny.shape, tiny.dtype))
    outs = pl.pallas_call(
        body, name="gather_weights_0", out_shape=out_shape,
        in_specs=[ANY] * (nt + 1), out_specs=[ANY] * (nt + 1),
        scratch_shapes=[pltpu.SemaphoreType.DMA((n_ici,)), pltpu.SemaphoreType.DMA((n_ici,)),
                        pltpu.SemaphoreType.DMA((n_ici,)), pltpu.SemaphoreType.DMA((n_ici,)),
                        pltpu.SemaphoreType.DMA((3,)), pltpu.SemaphoreType.DMA((3,)), pltpu.SemaphoreType.DMA],
        input_output_aliases={t: t for t in range(nt)},
        compiler_params=_cp(has_side_effects=True),
    )(*bufs, tiny)
    return outs[:nt], outs[nt]


def _gather_start(bufs):
    nt = len(bufs)

    def body(*refs):
        b_refs = refs[:nt]
        send, recv = refs[nt], refs[nt + 1]
        token = refs[2 * nt + 2]
        x, y, c, chip = _place()
        for t in range(nt):
            for d in (1, 2, 3):
                px, py, _ = _chip_at(x, y, d)
                _half_block(b_refs[t], chip, c, (px, py, c), send.at[3 * t + d - 1], recv.at[3 * t + d - 1]).start()
        token[...] = jnp.zeros_like(token)

    outs = pl.pallas_call(
        body, name="gather_start_1",
        out_shape=(pltpu.SemaphoreType.DMA((3 * nt,)), pltpu.SemaphoreType.DMA((3 * nt,)),
                   *[pltpu.HBM(b.shape, b.dtype) for b in bufs], jax.ShapeDtypeStruct((8, 128), F32)),
        in_specs=[HBM] * nt, out_specs=(SEM, SEM, *[HBM] * nt, pl.BlockSpec(memory_space=pltpu.VMEM)),
        input_output_aliases={t: 2 + t for t in range(nt)},
        compiler_params=pltpu.CompilerParams(has_side_effects=DATAFLOW),
    )(*[_in_hbm(b) for b in bufs])
    return outs[0], outs[1], list(outs[2:2 + nt]), outs[2 + nt]


def _gather_wait(send, recv, bufs, after):
    nt = len(bufs)

    def body(*refs):
        b_refs = refs[:nt]
        send_ref, recv_ref = refs[nt], refs[nt + 1]
        x, y, c, chip = _place()
        for t in range(nt):
            for d in (1, 2, 3):
                k = 3 * t + d - 1
                px, py, pchip = _chip_at(x, y, d)
                _half_block(b_refs[t], chip, c, (px, py, c), send_ref.at[k], recv_ref.at[k]).wait_send()
                _half_block(b_refs[t], pchip, c, (px, py, c), send_ref.at[k], recv_ref.at[k]).wait_recv()

    outs = pl.pallas_call(
        body, name="gather_wait_1", out_shape=[pltpu.HBM(b.shape, b.dtype) for b in bufs],
        in_specs=[HBM] * nt + [SEM, SEM, ANY], out_specs=[HBM] * nt,
        input_output_aliases={t: t for t in range(nt)},
        compiler_params=pltpu.CompilerParams(has_side_effects=DATAFLOW),
    )(*bufs, send, recv, after)
    return list(outs)


def _gather_pass_on(bufs):
    nt = len(bufs)

    def body(*refs):
        o_refs = refs[nt:2 * nt]
        fsend, frecv = refs[2 * nt:]
        x, y, c, _ = _place()
        cps = []
        for t in range(nt):
            for d in (1, 2, 3):
                k = 3 * t + d - 1
                _, _, pchip = _chip_at(x, y, d)
                cps.append(_half_block(o_refs[t], pchip, c, (x, y, 1 - c), fsend.at[k], frecv.at[k]))
        for cp in cps:
            cp.start()
        for t in range(nt):
            for d in (1, 2, 3):
                k = 3 * t + d - 1
                _, _, pchip = _chip_at(x, y, d)
                _half_block(o_refs[t], pchip, 1 - c, (x, y, 1 - c), fsend.at[k], frecv.at[k]).wait_recv()
        for cp in cps:
            cp.wait_send()

    return pl.pallas_call(
        body, name="gather_pass_on_1", out_shape=[jax.ShapeDtypeStruct(b.shape, b.dtype) for b in bufs],
        in_specs=[ANY] * nt, out_specs=[ANY] * nt,
        scratch_shapes=[pltpu.SemaphoreType.DMA((3 * nt,)), pltpu.SemaphoreType.DMA((3 * nt,))],
        input_output_aliases={t: t for t in range(nt)},
        compiler_params=_cp(has_side_effects=True),
    )(*bufs)


def _to_sibling_halves(gs, l):
    nt = len(gs)

    def body(*refs):
        g_refs, o_refs = refs[:nt], refs[nt:2 * nt]
        send, recv = refs[2 * nt:]
        x, y, c, _ = _place()
        cps = [pltpu.make_async_remote_copy(
            src_ref=g_refs[t].at[k, 1 - c], dst_ref=o_refs[t].at[k], send_sem=send.at[4 * t + k], recv_sem=recv.at[4 * t + k],
            device_id=(x, y, 1 - c), device_id_type=MESH) for t in range(nt) for k in range(4)]
        for cp in cps:
            cp.start()
        for cp in cps:
            cp.wait()

    return pl.pallas_call(
        body, name=f"grads_to_sibling_{l}", out_shape=[jax.ShapeDtypeStruct((4,) + g.shape[2:], g.dtype) for g in gs],
        in_specs=[ANY] * nt, out_specs=[ANY] * nt,
        scratch_shapes=[pltpu.SemaphoreType.DMA((4 * nt,)), pltpu.SemaphoreType.DMA((4 * nt,))],
        compiler_params=_cp(has_side_effects=True),
    )(*gs)


def _to_chips(cs, l):
    nt = len(cs)

    def body(*refs):
        c_refs, o_refs = refs[:nt], refs[nt:2 * nt]
        send, recv = refs[2 * nt:]
        x, y, c, _ = _place()
        cps = []
        for t in range(nt):
            for d in (1, 2, 3):
                px, py, pchip = _chip_at(x, y, d)
                cps.append(pltpu.make_async_remote_copy(
                    src_ref=c_refs[t].at[pchip], dst_ref=o_refs[t].at[d - 1], send_sem=send.at[3 * t + d - 1],
                    recv_sem=recv.at[3 * t + d - 1], device_id=(px, py, c), device_id_type=MESH))
        for cp in cps:
            cp.start()
        for cp in cps:
            cp.wait()

    return pl.pallas_call(
        body, name=f"grads_to_chips_{l}", out_shape=[jax.ShapeDtypeStruct((3,) + a.shape[1:], a.dtype) for a in cs],
        in_specs=[ANY] * nt, out_specs=[ANY] * nt,
        scratch_shapes=[pltpu.SemaphoreType.DMA((3 * nt,)), pltpu.SemaphoreType.DMA((3 * nt,))],
        compiler_params=_cp(has_side_effects=True),
    )(*cs)


def _chip_copy(c_ref, land_ref, x, y, c, d, send_sem, recv_sem):
    px, py, pchip = _chip_at(x, y, d)
    return pltpu.make_async_remote_copy(src_ref=c_ref.at[pchip], dst_ref=land_ref.at[d - 1], send_sem=send_sem, recv_sem=recv_sem,
                                        device_id=(px, py, c), device_id_type=MESH)


def _to_chips_start(cs, l):
    nt = len(cs)
    lands = [lax.empty((3,) + a.shape[1:], a.dtype) for a in cs]

    def body(*refs):
        c_refs, land_refs = refs[:nt], refs[nt:2 * nt]
        send, recv = refs[2 * nt], refs[2 * nt + 1]
        token = refs[4 * nt + 2]
        x, y, c, _ = _place()
        for t in range(nt):
            for d in (1, 2, 3):
                _chip_copy(c_refs[t], land_refs[t], x, y, c, d, send.at[3 * t + d - 1], recv.at[3 * t + d - 1]).start()
        token[...] = jnp.zeros_like(token)

    outs = pl.pallas_call(
        body, name=f"grads_to_chips_start_{l}",
        out_shape=(pltpu.SemaphoreType.DMA((3 * nt,)), pltpu.SemaphoreType.DMA((3 * nt,)),
                   *[pltpu.HBM(a.shape, a.dtype) for a in cs], *[pltpu.HBM(a.shape, a.dtype) for a in lands],
                   jax.ShapeDtypeStruct((8, 128), F32)),
        in_specs=[HBM] * (2 * nt), out_specs=(SEM, SEM, *[HBM] * (2 * nt), pl.BlockSpec(memory_space=pltpu.VMEM)),
        input_output_aliases={i: 2 + i for i in range(2 * nt)},
        compiler_params=pltpu.CompilerParams(has_side_effects=DATAFLOW),
    )(*[_in_hbm(a) for a in cs], *[_in_hbm(a) for a in lands])
    return outs[0], outs[1], list(outs[2:2 + nt]), list(outs[2 + nt:2 + 2 * nt]), outs[2 + 2 * nt]


def _to_chips_wait(send, recv, cs, lands, after, l):
    nt = len(cs)

    def body(*refs):
        c_refs, land_refs = refs[:nt], refs[nt:2 * nt]
        send_ref, recv_ref = refs[2 * nt], refs[2 * nt + 1]
        x, y, c, _ = _place()
        for t in range(nt):
            for d in (1, 2, 3):
                cp = _chip_copy(c_refs[t], land_refs[t], x, y, c, d, send_ref.at[3 * t + d - 1], recv_ref.at[3 * t + d - 1])
                cp.wait_send()
                cp.wait_recv()

    outs = pl.pallas_call(
        body, name=f"grads_to_chips_wait_{l}", out_shape=[pltpu.HBM(a.shape, a.dtype) for a in cs + lands],
        in_specs=[HBM] * (2 * nt) + [SEM, SEM, ANY], out_specs=[HBM] * (2 * nt),
        input_output_aliases={i: i for i in range(2 * nt)},
        compiler_params=pltpu.CompilerParams(has_side_effects=DATAFLOW),
    )(*cs, *lands, send, recv, after)
    return list(outs[:nt]), list(outs[nt:])


def _join_halves(fs, l):
    nt = len(fs)

    def body(*refs):
        o_refs = refs[nt:2 * nt]
        send, recv = refs[2 * nt:]
        x, y, c, _ = _place()
        cps = [pltpu.make_async_remote_copy(
            src_ref=o_refs[t].at[c], dst_ref=o_refs[t].at[c], send_sem=send.at[t], recv_sem=recv.at[t],
            device_id=(x, y, 1 - c), device_id_type=MESH) for t in range(nt)]
        for cp in cps:
            cp.start()
        for cp in cps:
            cp.wait()

    return pl.pallas_call(
        body, name=f"grads_join_{l}", out_shape=[jax.ShapeDtypeStruct(a.shape, a.dtype) for a in fs],
        in_specs=[ANY] * nt, out_specs=[ANY] * nt,
        scratch_shapes=[pltpu.SemaphoreType.DMA((nt,)), pltpu.SemaphoreType.DMA((nt,))],
        input_output_aliases={t: t for t in range(nt)},
        compiler_params=_cp(has_side_effects=True),
    )(*fs)


def _add_half(g, r, c_arr, name):
    _, _, rh, cols = g.shape

    def body(c_ref, g_ref, r_ref, o_ref):
        o_ref[...] = (g_ref[...] + r_ref[...]).astype(BF16)

    blk = pl.BlockSpec((None, rh, cols), lambda k, cr: (k, 0, 0))
    return pl.pallas_call(
        body, name=name, out_shape=jax.ShapeDtypeStruct((4, rh, cols), BF16),
        grid_spec=pltpu.PrefetchScalarGridSpec(
            num_scalar_prefetch=1, grid=(4,),
            in_specs=[pl.BlockSpec((None, None, rh, cols), lambda k, cr: (k, cr[0], 0, 0)), blk], out_specs=blk),
        compiler_params=_cp(("parallel",)),
    )(c_arr, g, r)


def _sum_chips(cs, r3, place_arr, name):
    _, rh, cols = cs.shape
    rb = rh // 2

    def body(pl_ref, a_ref, r0_ref, r1_ref, r2_ref, o_ref):
        up = lambda ref: ref[...].astype(F32)
        o_ref[...] = ((up(a_ref) + up(r0_ref)) + up(r1_ref)) + up(r2_ref)

    def slot(d):
        return pl.BlockSpec((None, rb, cols), lambda i, pa: (d, i, 0))

    return pl.pallas_call(
        body, name=name, out_shape=jax.ShapeDtypeStruct((2, rh, cols), F32),
        grid_spec=pltpu.PrefetchScalarGridSpec(
            num_scalar_prefetch=1, grid=(2,),
            in_specs=[pl.BlockSpec((None, rb, cols), lambda i, pa: (pa[0], i, 0)), slot(0), slot(1), slot(2)],
            out_specs=pl.BlockSpec((None, rb, cols), lambda i, pa: (pa[1], i, 0))),
        compiler_params=_cp(("parallel",)),
    )(place_arr, cs, r3, r3, r3)


def _allreduce_small(pack):
    rows = pack.shape[0]
    hr = rows // 2

    def body(p_ref, o_ref, sib, slots, s1, r1, s2, r2, s3, r3):
        x, y, c, chip = _place()
        sibling = (x, y, 1 - c)
        ex = pltpu.make_async_remote_copy(src_ref=p_ref, dst_ref=sib, send_sem=s1, recv_sem=r1,
                                          device_id=sibling, device_id_type=MESH)
        ex.start()
        ex.wait()
        half = pl.ds(pl.multiple_of(c * hr, 8), hr)
        slots[0] = p_ref[half, :] + sib[half, :]
        cps = []
        for d in (1, 2, 3):
            px, py, _ = _chip_at(x, y, d)
            cps.append(pltpu.make_async_remote_copy(
                src_ref=slots.at[0], dst_ref=slots.at[d], send_sem=s2.at[d - 1], recv_sem=r2.at[d - 1],
                device_id=(px, py, c), device_id_type=MESH))
        for cp in cps:
            cp.start()
        for cp in cps:
            cp.wait()
        tot = slots[chip]
        for k in (1, 2, 3):
            tot = tot + slots[jnp.bitwise_xor(chip, k)]
        o_ref[half, :] = tot
        back = pltpu.make_async_remote_copy(src_ref=o_ref.at[half, :], dst_ref=o_ref.at[half, :], send_sem=s3, recv_sem=r3,
                                            device_id=sibling, device_id_type=MESH)
        back.start()
        back.wait()

    vm = pl.BlockSpec(memory_space=pltpu.VMEM)
    return pl.pallas_call(
        body, name="allreduce_small", out_shape=jax.ShapeDtypeStruct((rows, 128), F32),
        in_specs=[vm], out_specs=vm,
        scratch_shapes=[pltpu.VMEM((rows, 128), F32), pltpu.VMEM((4, hr, 128), F32),
                        pltpu.SemaphoreType.DMA, pltpu.SemaphoreType.DMA, pltpu.SemaphoreType.DMA((3,)), pltpu.SemaphoreType.DMA((3,)),
                        pltpu.SemaphoreType.DMA, pltpu.SemaphoreType.DMA],
        compiler_params=_cp(has_side_effects=True),
    )(pack)


def _adam(g, w, m, v, name):
    rows, cols = g.shape
    rb = rows // 4

    def body(g_ref, w_ref, m_ref, v_ref, d_ref, m2_ref, v2_ref):
        gv = g_ref[...]
        m2 = ADAM_B1 * m_ref[...] + (1.0 - ADAM_B1) * gv
        v2 = ADAM_B2 * v_ref[...] + (1.0 - ADAM_B2) * (gv * gv)
        m_hat = m2 / (1.0 - ADAM_B1 ** ADAM_STEP)
        v_hat = v2 / (1.0 - ADAM_B2 ** ADAM_STEP)
        d_ref[...] = -ADAM_LR * (m_hat / (jnp.sqrt(v_hat) + ADAM_EPS) + ADAM_WD * w_ref[...])
        m2_ref[...] = m2
        v2_ref[...] = v2

    blk = pl.BlockSpec((rb, cols), lambda i: (i, 0))
    shp = jax.ShapeDtypeStruct((rows, cols), F32)
    return pl.pallas_call(
        body, name=name, grid=(4,), in_specs=[blk] * 4, out_specs=[blk] * 3, out_shape=[shp] * 3,
        compiler_params=_cp(("parallel",)),
    )(g, w, m, v)


def _pack(arrs, mult):
    flat = jnp.concatenate([a.reshape(-1) for a in arrs])
    rows = -(-flat.shape[0] // (128 * mult)) * mult
    return jnp.pad(flat, (0, rows * 128 - flat.shape[0])).reshape(rows, 128)


def _unpack(pack, shapes):
    flat = pack.reshape(-1)
    out, o = [], 0
    for s in shapes:
        n = 1
        for e in s:
            n *= e
        out.append(flat[o:o + n].reshape(s))
        o += n
    return out


WEIGHTS = ['norm1_g', 'w_in', 'gmlp_ln_g', 'gmlp_ln_b', 'gmlp_w_s', 'gmlp_b_s', 'conv_w', 'conv_b', 'lru_w_r', 'lru_b_r', 'lru_w_i',
           'lru_b_i', 'lru_lambda', 'w_out', 'norm2_g', 'w_ffn_in', 'w_ffn_out', 'final_g']
BIG = ['w_in', 'w_out', 'w_ffn_in', 'w_ffn_out']
SMALL = [n for n in WEIGHTS if n not in BIG]
CHIP_SHARDED_SMALL = ['conv_w', 'lru_b_r', 'lru_b_i', 'lru_lambda']


def kernel(x, norm1_g, w_in, gmlp_ln_g, gmlp_ln_b, gmlp_w_s, gmlp_b_s, conv_w, conv_b, lru_w_r, lru_b_r, lru_w_i, lru_b_i, lru_lambda, w_out, norm2_g, w_ffn_in, w_ffn_out, final_g, loss_target, m_norm1_g, m_w_in, m_gmlp_ln_g, m_gmlp_ln_b, m_gmlp_w_s, m_gmlp_b_s, m_conv_w, m_conv_b, m_lru_w_r, m_lru_b_r, m_lru_w_i, m_lru_b_i, m_lru_lambda, m_w_out, m_norm2_g, m_w_ffn_in, m_w_ffn_out, m_final_g, v_norm1_g, v_w_in, v_gmlp_ln_g, v_gmlp_ln_b, v_gmlp_w_s, v_gmlp_b_s, v_conv_w, v_conv_b, v_lru_w_r, v_lru_b_r, v_lru_w_i, v_lru_b_i, v_lru_lambda, v_w_out, v_norm2_g, v_w_ffn_in, v_w_ffn_out, v_final_g):
    a = dict(locals())
    w = {n: a[n] for n in WEIGHTS}
    mom = {n: a["m_" + n] for n in WEIGHTS}
    var = {n: a["v_" + n] for n in WEIGHTS}
    _, _, c, chip = _place()
    c_arr, chip_arr = jnp.reshape(c, (1,)).astype(jnp.int32), jnp.reshape(chip, (1,)).astype(jnp.int32)
    place_arr = jnp.stack([chip, c]).astype(jnp.int32)

    def as_weights(full):
        wb = {n: f.reshape(4, 2 * f.shape[2], f.shape[3]) for n, f in zip(BIG, full)}
        wb["w_out"] = wb["w_out"].reshape(D, D)
        wb["w_ffn_out"] = wb["w_ffn_out"].reshape(DFF, D)
        return wb

    bufs = [[_cast_into(w[n], l, chip_arr, f"cast_{n}_{l}") for n in BIG] for l in range(2)]
    tiny = _pack([w[n] for n in CHIP_SHARDED_SMALL], 8)
    full0, tiny_full = _gather_weights(bufs[0], tiny)
    g_send, g_recv, flying, g_zero = _gather_start(bufs[1])
    p = {n: w[n] for n in SMALL}
    parts = [_unpack(tiny_full[k], [w[n].shape for n in CHIP_SHARDED_SMALL]) for k in range(4)]
    for i, n in enumerate(CHIP_SHARDED_SMALL):
        p[n] = jnp.concatenate([parts[k][i] for k in range(4)], axis=-1)

    xa, saved0 = _forward_layer(0, x[0], p, as_weights(full0), after=(g_zero,))
    full1 = _gather_pass_on(_gather_wait(g_send, g_recv, flying, xa))
    xb, saved1 = _forward_layer(1, xa, p, as_weights(full1))
    dxb, loss_v, dfg = _loss_head(xb, loss_target[0], p["final_g"][None])
    loss = lax.psum(loss_v[0, 0], ("x", "y", "c"))

    def to_chip_sums(big, l):
        gs = [g.reshape(4, 2, -1, g.shape[-1]) for g in big]
        from_sib = _to_sibling_halves(gs, l)
        return [_add_half(g, r, c_arr, f"add_half_{n}_{l}") for n, g, r in zip(BIG, gs, from_sib)]

    def finish(cs, from_chips, l):
        ts = [_sum_chips(cc, r3, place_arr, f"sum_chips_{n}_{l}") for n, cc, r3 in zip(BIG, cs, from_chips)]
        res = {}
        for n, j in zip(BIG, _join_halves(ts, l)):
            g = j.reshape(w[n].shape[1:])
            res[n] = (g,) + tuple(_adam(g, w[n][l], mom[n][l], var[n][l], f"adam_{n}_{l}"))
        return res

    dxa, big1, small1 = _backward_layer(1, dxb, saved1)
    r_send, r_recv, cs1, lands1, r_zero = _to_chips_start(to_chip_sums(big1, 1), 1)
    dx, big0, small0 = _backward_layer(0, dxa, saved0, after=(r_zero,))
    cs1, lands1 = _to_chips_wait(r_send, r_recv, cs1, lands1, dx, 1)
    res1 = finish(cs1, lands1, 1)
    cs0 = to_chip_sums(big0, 0)
    res0 = finish(cs0, _to_chips(cs0, 0), 0)
    out = {n: [jnp.stack([res0[n][i], res1[n][i]]) for i in range(4)] for n in BIG}
    small = {k: jnp.stack([small0[k], small1[k]]) for k in LAYER_SMALL}
    small["final_g"] = dfg[0]

    full_shapes = [small[n].shape for n in SMALL]
    red = _unpack(_allreduce_small(_pack([small[n] for n in SMALL], 16)), full_shapes)
    g_small = []
    for n, g in zip(SMALL, red):
        if n in CHIP_SHARDED_SMALL:
            g = lax.dynamic_slice_in_dim(g, chip * w[n].shape[-1], w[n].shape[-1], axis=g.ndim - 1)
        g_small.append(g)
    shapes = [w[n].shape for n in SMALL]
    packs = [_pack(lst, 32) for lst in (g_small, [w[n] for n in SMALL], [mom[n] for n in SMALL], [var[n] for n in SMALL])]
    upd = [_unpack(u, shapes) for u in _adam(*packs, "adam_small")]
    for i, n in enumerate(SMALL):
        out[n] = [g_small[i], upd[0][i], upd[1][i], upd[2][i]]

    return (loss, dx[None]) + tuple(out[n][i] for i in range(4) for n in WEIGHTS)
```

```python
import functools

import jax
import jax.numpy as jnp
from jax import lax
from jax.experimental import pallas as pl
from jax.experimental.pallas import tpu as pltpu

F32 = jnp.float32
BF16 = jnp.bfloat16
MESH = pl.DeviceIdType.MESH

D = 1024
NH = 8
HD = 128
CHUNK = 128
N_IN_T = 12
DFF = 2816
DFF_SH = 1408
EPS = 1e-6
LRU_C = 8.0
ADAM_LR, ADAM_B1, ADAM_B2, ADAM_EPS, ADAM_WD, ADAM_STEP = 0.001, 0.9, 0.999, 1e-08, 0.01, 10

TM = 512
TM_BIG = 1024
RT = 128
PADR = 8
VMEM_LIMIT = 56 * 1024 * 1024


def _cp(sem=None, **kw):
    if sem is not None:
        kw["dimension_semantics"] = sem
    return pltpu.CompilerParams(vmem_limit_bytes=VMEM_LIMIT, **kw)


_GC = 0.7978845608028654


def _sigmoid(x):
    return 1.0 / (1.0 + jnp.exp(-x))


def _gelu(x):
    return 0.5 * x * (1.0 + jnp.tanh(_GC * (x + 0.044715 * x * x * x)))


def _gelu_and_grad(x):
    t = jnp.tanh(_GC * (x + 0.044715 * x * x * x))
    g = 0.5 * x * (1.0 + t)
    dg = 0.5 * (1.0 + t) + 0.5 * x * (1.0 - t * t) * _GC * (1.0 + 3 * 0.044715 * x * x)
    return g, dg


def _softplus_neg(lam):
    y = jnp.exp(-jnp.abs(lam))
    u = 1.0 + y
    l1p = jnp.where(u == 1.0, y, jnp.log(u) * y / (u - 1.0))
    return jnp.maximum(-lam, 0.0) + l1p


def _dot(a, b):
    return jnp.dot(a, b, preferred_element_type=F32)


def _dot_nt(a, b):
    return lax.dot_general(a, b, (((1,), (1,)), ((), ())), preferred_element_type=F32)


def _dot_tn(a, b):
    return lax.dot_general(a, b, (((0,), (0,)), ((), ())), preferred_element_type=F32)


def _rms_hat(x):
    r = lax.rsqrt(jnp.mean(x * x, axis=-1, keepdims=True) + EPS)
    return x * r, r


def _rms_bwd(dh, x, g):
    xh, r = _rms_hat(x)
    dxh = dh * g
    dx = r * (dxh - xh * jnp.mean(dxh * xh, axis=-1, keepdims=True))
    return dx, jnp.sum(dh * xh, axis=0, keepdims=True)


def _in_tile(j):
    m, hf = j // 2, j % 2
    orig = jnp.where(m < 2, m, jnp.where(m == 2, 4, jnp.where(m < 5, m - 1, 5)))
    t = orig * 2 + hf
    return t // 3, t % 3


ANY = pl.BlockSpec(memory_space=pl.ANY)


def _mm_in(x, g, w_in, l, after=()):
    S = x.shape[0]
    tm = min(TM_BIG, S)

    def body(x_ref, g_ref, w_ref, *rest):
        o_ref, h_ref = rest[-2:]

        @pl.when(pl.program_id(1) == 0)
        def _():
            xh, _ = _rms_hat(x_ref[...])
            h_ref[...] = (xh * g_ref[...]).astype(BF16)
        o_ref[...] = _dot(h_ref[...], w_ref[...]).astype(BF16)

    def w_map(i, j):
        sh, tl = _in_tile(j)
        return (sh, 0, tl)

    return pl.pallas_call(
        body, name=f"mm_in_{l}", grid=(S // tm, N_IN_T),
        in_specs=[pl.BlockSpec((tm, D), lambda i, j: (i, 0)), pl.BlockSpec((1, D), lambda i, j: (0, 0)),
                  pl.BlockSpec((None, D, 512), w_map)] + [ANY] * len(after),
        out_specs=[pl.BlockSpec((None, tm, 512), lambda i, j: (j // 2, i, j % 2)), pl.BlockSpec((tm, D), lambda i, j: (i, 0))],
        out_shape=[jax.ShapeDtypeStruct((6, S, D), BF16), jax.ShapeDtypeStruct((S, D), BF16)],
        compiler_params=_cp(("parallel", "arbitrary")),
    )(x, g, w_in, *after)


def _mm_res(a, w, res, l, name):
    S, K = a.shape

    def body(a_ref, w_ref, r_ref, o_ref):
        o_ref[...] = r_ref[...] + _dot(a_ref[...], w_ref[...])

    return pl.pallas_call(
        body, name=f"{name}_{l}", grid=(S // TM,),
        in_specs=[pl.BlockSpec((TM, K), lambda i: (i, 0)), pl.BlockSpec((K, D), lambda i: (0, 0)),
                  pl.BlockSpec((TM, D), lambda i: (i, 0))],
        out_specs=pl.BlockSpec((TM, D), lambda i: (i, 0)),
        out_shape=jax.ShapeDtypeStruct((S, D), F32),
        compiler_params=_cp(("parallel",)),
    )(a, w, res)


def _mm_ffn_in(x, g, w_fi, l):
    S = x.shape[0]

    def body(x_ref, g_ref, w_ref, gu_ref, ff_ref, h_ref, gbuf):
        k = pl.program_id(1)

        @pl.when(k == 0)
        def _():
            xh, _ = _rms_hat(x_ref[...])
            h_ref[...] = (xh * g_ref[...]).astype(BF16)
        acc = _dot(h_ref[...], w_ref[...])
        gu_ref[...] = acc.astype(BF16)

        @pl.when(k < 2)
        def _():
            gbuf[k] = acc

        @pl.when(k >= 2)
        def _():
            ga = gbuf[k - 2]
            ff_ref[...] = (ga * _sigmoid(ga) * acc).astype(BF16)

    return pl.pallas_call(
        body, name=f"mm_ffn_in_{l}", grid=(S // TM, 4),
        in_specs=[pl.BlockSpec((TM, D), lambda i, k: (i, 0)), pl.BlockSpec((1, D), lambda i, k: (0, 0)),
                  pl.BlockSpec((None, D, DFF_SH), lambda i, k: (k, 0, 0))],
        out_specs=[pl.BlockSpec((None, TM, DFF_SH), lambda i, k: (k, i, 0)),
                   pl.BlockSpec((TM, DFF_SH), lambda i, k: (i, jnp.maximum(k - 2, 0))),
                   pl.BlockSpec((TM, D), lambda i, k: (i, 0))],
        out_shape=[jax.ShapeDtypeStruct((4, S, DFF_SH), BF16), jax.ShapeDtypeStruct((S, DFF), BF16),
                   jax.ShapeDtypeStruct((S, D), BF16)],
        scratch_shapes=[pltpu.VMEM((2, TM, DFF_SH), F32)],
        compiler_params=_cp(("parallel", "arbitrary")),
    )(x, g, w_fi)


def _gmlp_fwd(z6, ws_b, bs_b, lg, lb):
    S = z6.shape[1]

    def body(z_ref, ws_ref, bs_ref, lg_ref, lb_ref, o_ref, mix):
        gv = _gelu(z_ref[1].astype(F32))
        xc = gv - jnp.mean(gv, axis=-1, keepdims=True)
        rs = lax.rsqrt(jnp.mean(xc * xc, axis=-1, keepdims=True) + EPS)
        vb = (xc * rs * lg_ref[...] + lb_ref[...]).astype(BF16)
        for gi in range(NH):
            cs = slice(gi * HD, (gi + 1) * HD)
            mix[:, cs] = _dot(ws_ref[gi], vb[:, cs])
        o_ref[...] = (_sigmoid(z_ref[2].astype(F32)) * _gelu(z_ref[0].astype(F32)) * (mix[...] + bs_ref[...])).astype(BF16)

    return pl.pallas_call(
        body, name="gmlp_fwd", grid=(S // CHUNK,),
        in_specs=[pl.BlockSpec((3, CHUNK, D), lambda i: (0, i, 0)), pl.BlockSpec((NH, CHUNK, CHUNK), lambda i: (0, 0, 0)),
                  pl.BlockSpec((CHUNK, D), lambda i: (0, 0)), pl.BlockSpec((1, D), lambda i: (0, 0)),
                  pl.BlockSpec((1, D), lambda i: (0, 0))],
        out_specs=pl.BlockSpec((CHUNK, D), lambda i: (i, 0)),
        out_shape=jax.ShapeDtypeStruct((S, D), BF16),
        scratch_shapes=[pltpu.VMEM((CHUNK, D), F32)],
        compiler_params=_cp(("parallel",)),
    )(z6, ws_b, bs_b, lg, lb)


def _row_iota():
    return lax.broadcasted_iota(jnp.int32, (RT, HD), 0)


SUB = 8


def _scan_up(a, b, carry):
    row = lax.broadcasted_iota(jnp.int32, (SUB, HD), 0)
    masks = [(d, row >= d) for d in (1, 2, 4)]
    c = jnp.broadcast_to(carry, (SUB, HD))
    hs = []
    for j in range(RT // SUB):
        aj, bj = a[SUB * j:SUB * (j + 1)], b[SUB * j:SUB * (j + 1)]
        for d, m in masks:
            bj = bj + aj * jnp.where(m, pltpu.roll(bj, d, 0), 0.0)
            aj = aj * jnp.where(m, pltpu.roll(aj, d, 0), 1.0)
        h = bj + aj * c
        hs.append(h)
        c = jnp.broadcast_to(h[SUB - 1:SUB, :], (SUB, HD))
    return jnp.concatenate(hs, axis=0), hs[-1][SUB - 1:SUB, :]


def _scan_down(a, b, carry):
    row = lax.broadcasted_iota(jnp.int32, (SUB, HD), 0)
    masks = [(d, row < SUB - d) for d in (1, 2, 4)]
    c = jnp.broadcast_to(carry, (SUB, HD))
    hs = []
    for j in reversed(range(RT // SUB)):
        aj, bj = a[SUB * j:SUB * (j + 1)], b[SUB * j:SUB * (j + 1)]
        for d, m in masks:
            bj = bj + aj * jnp.where(m, pltpu.roll(bj, SUB - d, 0), 0.0)
            aj = aj * jnp.where(m, pltpu.roll(aj, SUB - d, 0), 1.0)
        h = bj + aj * c
        hs.append(h)
        c = jnp.broadcast_to(h[0:1, :], (SUB, HD))
    return jnp.concatenate(hs[::-1], axis=0), hs[-1][0:1, :]


def _lru_gates(xc, d, wr_ref, br_ref, wi_ref, bi_ref, sp):
    xb = xc.astype(BF16)
    r = _sigmoid(_dot(xb, wr_ref[d]) + br_ref[d:d + 1, :])
    i = _sigmoid(_dot(xb, wi_ref[d]) + bi_ref[d:d + 1, :])
    log_a = -LRU_C * r * sp[d:d + 1, :]
    a = jnp.exp(log_a)
    mult = jnp.sqrt(jnp.maximum(-jnp.tanh(log_a) * (a * a + 1.0), 0.0))
    return r, i, a, mult


def _shifted(win, k):
    w = RT + 2 * PADR
    v = win if k == 0 else pltpu.roll(win, (-k) % w, 0)
    return v[PADR:PADR + RT]


def _conv_taps(win):
    return [_shifted(win, k) for k in (-1, 0, 1, 2)]


def _fill_padded(dst, src_ref, S):
    zeros = jnp.zeros((PADR, HD), F32)
    dst[0:PADR, :] = zeros
    dst[PADR + S:2 * PADR + S, :] = zeros

    def cp(i, c):
        t0 = pl.multiple_of(i * RT, RT)
        dst[pl.ds(t0 + PADR, RT), :] = src_ref[pl.ds(t0, RT), :].astype(F32)
        return c
    lax.fori_loop(0, S // RT, cp, 0)


def _conv_fwd_all(zxp, xc_s, cw_ref, cb_ref, S):
    def cv(i, c):
        t0 = pl.multiple_of(i * RT, RT)
        xm1, x0, xp1, xp2 = _conv_taps(zxp[pl.ds(t0, RT + 2 * PADR), :])
        xc_s[pl.ds(t0, RT), :] = (cb_ref[...] + xm1 * cw_ref[0:1, :] + x0 * cw_ref[1:2, :]
                                  + xp1 * cw_ref[2:3, :] + xp2 * cw_ref[3:4, :])
        return c
    lax.fori_loop(0, S // RT, cv, 0)


def _lru_specs(S):
    head = lambda h: (0, h)
    return [pl.BlockSpec((4, HD), head), pl.BlockSpec((1, HD), head),
            pl.BlockSpec((2, None, HD, HD), lambda h: (0, h, 0, 0)), pl.BlockSpec((2, HD), head),
            pl.BlockSpec((2, None, HD, HD), lambda h: (0, h, 0, 0)), pl.BlockSpec((2, HD), head),
            pl.BlockSpec((2, HD), head)]


def _lru_fwd(z6, ya, cw, cb, wr, br, wi, bi, lam):
    S = z6.shape[1]
    nt = S // RT

    def body(z_ref, ya_ref, cw_ref, cb_ref, wr_ref, br_ref, wi_ref, bi_ref, lam_ref, mg_ref, h0_ref, h1_ref, zxp, xc_s):
        sp = _softplus_neg(lam_ref[...])
        _fill_padded(zxp, z_ref.at[0], S)
        _conv_fwd_all(zxp, xc_s, cw_ref, cb_ref, S)

        def scans(i, carry):
            cu, cd = carry
            ru = pl.ds(pl.multiple_of(i * RT, RT), RT)
            rd = pl.ds(pl.multiple_of((nt - 1 - i) * RT, RT), RT)
            xu, xd = xc_s[ru, :], xc_s[rd, :]
            _, gi, a, mult = _lru_gates(xu, 0, wr_ref, br_ref, wi_ref, bi_ref, sp)
            hu, cu = _scan_up(a, mult * gi * xu, cu)
            h0_ref[ru, :] = hu
            _, gi, a, mult = _lru_gates(xd, 1, wr_ref, br_ref, wi_ref, bi_ref, sp)
            hd, cd = _scan_down(a, mult * gi * xd, cd)
            h1_ref[rd, :] = hd
            return cu, cd
        z1 = jnp.zeros((1, HD), F32)
        lax.fori_loop(0, nt, scans, (z1, z1))

        def merge(i, c):
            rows = pl.ds(pl.multiple_of(i * RT, RT), RT)
            yb = (h0_ref[rows, :] + h1_ref[rows, :]) * _gelu(z_ref[1, rows, :].astype(F32))
            mg_ref[rows, :] = (ya_ref[rows, :].astype(F32) + _sigmoid(z_ref[2, rows, :].astype(F32)) * yb).astype(BF16)
            return c
        lax.fori_loop(0, nt, merge, 0)

    col = pl.BlockSpec((S, HD), lambda h: (0, h))
    return pl.pallas_call(
        body, name="lru_fwd", grid=(NH,),
        in_specs=[pl.BlockSpec((3, S, HD), lambda h: (1, 0, h)), col] + _lru_specs(S),
        out_specs=[col, col, col],
        out_shape=[jax.ShapeDtypeStruct((S, D), BF16), jax.ShapeDtypeStruct((S, D), F32), jax.ShapeDtypeStruct((S, D), F32)],
        scratch_shapes=[pltpu.VMEM((S + 2 * PADR, HD), F32), pltpu.VMEM((S, HD), F32)],
        compiler_params=_cp(("parallel",)),
    )(z6, ya, cw, cb, wr, br, wi, bi, lam)


def _loss_head(x, tgt, g):
    S = x.shape[0]

    def body(x_ref, t_ref, g_ref, dx_ref, loss_ref, dg_ref):
        @pl.when(pl.program_id(0) == 0)
        def _():
            loss_ref[...] = jnp.zeros_like(loss_ref)
            dg_ref[...] = jnp.zeros_like(dg_ref)
        xv = x_ref[...]
        xh, _ = _rms_hat(xv)
        e = xh * g_ref[...] - t_ref[...]
        loss_ref[...] += jnp.sum(e * e) * (0.5 / D)
        dx, dgs = _rms_bwd(e * (1.0 / D), xv, g_ref[...])
        dx_ref[...] = dx
        dg_ref[...] += dgs

    return pl.pallas_call(
        body, name="loss_head", grid=(S // TM,),
        in_specs=[pl.BlockSpec((TM, D), lambda i: (i, 0)), pl.BlockSpec((TM, D), lambda i: (i, 0)),
                  pl.BlockSpec((1, D), lambda i: (0, 0))],
        out_specs=[pl.BlockSpec((TM, D), lambda i: (i, 0)), pl.BlockSpec((1, 128), lambda i: (0, 0)),
                   pl.BlockSpec((1, D), lambda i: (0, 0))],
        out_shape=[jax.ShapeDtypeStruct((S, D), F32), jax.ShapeDtypeStruct((1, 128), F32), jax.ShapeDtypeStruct((1, D), F32)],
        compiler_params=_cp(("arbitrary",)),
    )(x, tgt, g)


def _bwd_ffn_out(dx, w_fo, gu, l, after=()):
    S = dx.shape[0]

    def body(dx_ref, w_ref, ga_ref, gb_ref, *rest):
        o_ref, dxb, dff = rest[-3:]
        k = pl.program_id(1)

        @pl.when(k == 0)
        def _():
            dxb[...] = dx_ref[...].astype(BF16)

        @pl.when(k < 2)
        def _():
            d = _dot_nt(dxb[...], w_ref[...])
            dff[k] = d
            ga = ga_ref[...].astype(F32)
            sg = _sigmoid(ga)
            o_ref[...] = (d * gb_ref[...].astype(F32) * sg * (1.0 + ga * (1.0 - sg))).astype(BF16)

        @pl.when(k >= 2)
        def _():
            ga = ga_ref[...].astype(F32)
            o_ref[...] = (dff[k - 2] * ga * _sigmoid(ga)).astype(BF16)

    return pl.pallas_call(
        body, name=f"bwd_ffn_out_{l}", grid=(S // TM, 4),
        in_specs=[pl.BlockSpec((TM, D), lambda i, k: (i, 0)),
                  pl.BlockSpec((DFF_SH, D), lambda i, k: (jnp.minimum(k, 1), 0)),
                  pl.BlockSpec((None, TM, DFF_SH), lambda i, k: (k % 2, i, 0)),
                  pl.BlockSpec((None, TM, DFF_SH), lambda i, k: (jnp.minimum(k, 1) + 2, i, 0))] + [ANY] * len(after),
        out_specs=pl.BlockSpec((None, TM, DFF_SH), lambda i, k: (k, i, 0)),
        out_shape=jax.ShapeDtypeStruct((4, S, DFF_SH), BF16),
        scratch_shapes=[pltpu.VMEM((TM, D), BF16), pltpu.VMEM((2, TM, DFF_SH), F32)],
        compiler_params=_cp(("parallel", "arbitrary")),
    )(dx, w_fo, gu, gu, *after)


def _mm_tn(a, b, m_blk, name):
    S, M = a.shape

    def body(a_ref, b_ref, o_ref):
        @pl.when(pl.program_id(1) == 0)
        def _():
            o_ref[...] = jnp.zeros_like(o_ref)
        o_ref[...] += _dot_tn(a_ref[...], b_ref[...].astype(BF16))

    return pl.pallas_call(
        body, name=name, grid=(M // m_blk, S // TM),
        in_specs=[pl.BlockSpec((TM, m_blk), lambda m, k: (k, m)), pl.BlockSpec((TM, D), lambda m, k: (k, 0))],
        out_specs=pl.BlockSpec((m_blk, D), lambda m, k: (m, 0)),
        out_shape=jax.ShapeDtypeStruct((M, D), F32),
        compiler_params=_cp(("parallel", "arbitrary")),
    )(a, b)


def _mm_nt_rms_bwd(a, a_spec, w, w_spec, nk, tm, x, g, dres, name):
    S = x.shape[0]

    def body(a_ref, w_ref, x_ref, g_ref, r_ref, dx_ref, dg_ref, acc):
        i, k = pl.program_id(0), pl.program_id(1)

        @pl.when(k == 0)
        def _():
            acc[...] = jnp.zeros_like(acc)
        acc[...] += _dot_nt(a_ref[...], w_ref[...])

        @pl.when(jnp.logical_and(i == 0, k == 0))
        def _():
            dg_ref[...] = jnp.zeros_like(dg_ref)

        @pl.when(k == nk - 1)
        def _():
            dx, dgs = _rms_bwd(acc[...], x_ref[...], g_ref[...])
            dx_ref[...] = r_ref[...] + dx
            dg_ref[...] += dgs

    row = pl.BlockSpec((tm, D), lambda i, k: (i, 0))
    vec = pl.BlockSpec((1, D), lambda i, k: (0, 0))
    return pl.pallas_call(
        body, name=name, grid=(S // tm, nk),
        in_specs=[a_spec, w_spec, row, vec, row],
        out_specs=[row, vec],
        out_shape=[jax.ShapeDtypeStruct((S, D), F32), jax.ShapeDtypeStruct((1, D), F32)],
        scratch_shapes=[pltpu.VMEM((tm, D), F32)],
        compiler_params=_cp(("arbitrary", "arbitrary")),
    )(a, w, x, g, dres)


def _dw_ffn_in(h, dgu, l):
    S = h.shape[0]

    def body(h_ref, b_ref, o_ref):
        @pl.when(pl.program_id(1) == 0)
        def _():
            o_ref[...] = jnp.zeros_like(o_ref)
        o_ref[...] += _dot_tn(h_ref[...], b_ref[...])

    return pl.pallas_call(
        body, name=f"dw_ffn_in_{l}", grid=(4, S // TM),
        in_specs=[pl.BlockSpec((TM, D), lambda j, k: (k, 0)), pl.BlockSpec((None, TM, DFF_SH), lambda j, k: (j, k, 0))],
        out_specs=pl.BlockSpec((None, D, DFF_SH), lambda j, k: (j, 0, 0)),
        out_shape=jax.ShapeDtypeStruct((4, D, DFF_SH), F32),
        compiler_params=_cp(("parallel", "arbitrary")),
    )(h, dgu)


_HALF_COMPS = ((0, 1, 3), (4, 2, 5))


def _dw_in(h, dz6, l):
    S = h.shape[0]

    def body(h_ref, d0_ref, d1_ref, d2_ref, o_ref):
        @pl.when(pl.program_id(1) == 0)
        def _():
            o_ref[...] = jnp.zeros_like(o_ref)
        hv = h_ref[...]
        for q, d_ref in enumerate((d0_ref, d1_ref, d2_ref)):
            for hf in range(2):
                col = 1024 * q + 512 * hf
                o_ref[col // 1536, :, col % 1536:col % 1536 + 512] += _dot_tn(hv, d_ref[:, 512 * hf:512 * (hf + 1)])

    def comp(q):
        return pl.BlockSpec((None, TM, D), lambda p, k: (jnp.where(p == 0, _HALF_COMPS[0][q], _HALF_COMPS[1][q]), k, 0))

    return pl.pallas_call(
        body, name=f"dw_in_{l}", grid=(2, S // TM),
        in_specs=[pl.BlockSpec((TM, D), lambda p, k: (k, 0)), comp(0), comp(1), comp(2)],
        out_specs=pl.BlockSpec((2, D, 1536), lambda p, k: (p, 0, 0)),
        out_shape=jax.ShapeDtypeStruct((4, D, 1536), F32),
        compiler_params=_cp(("parallel", "arbitrary")),
    )(h, dz6, dz6, dz6)


def _bwd_out(dx, w_o, merged, l):
    S = dx.shape[0]

    def body(dx_ref, w_ref, m_ref, dm_ref, dw_ref):
        @pl.when(pl.program_id(0) == 0)
        def _():
            dw_ref[...] = jnp.zeros_like(dw_ref)
        dxb = dx_ref[...].astype(BF16)
        dm_ref[...] = _dot_nt(dxb, w_ref[...]).astype(BF16)
        dw_ref[...] += _dot_tn(m_ref[...], dxb)

    row = pl.BlockSpec((TM, D), lambda i: (i, 0))
    return pl.pallas_call(
        body, name=f"bwd_out_{l}", grid=(S // TM,),
        in_specs=[row, pl.BlockSpec((D, D), lambda i: (0, 0)), row],
        out_specs=[row, pl.BlockSpec((D, D), lambda i: (0, 0))],
        out_shape=[jax.ShapeDtypeStruct((S, D), BF16), jax.ShapeDtypeStruct((D, D), F32)],
        compiler_params=_cp(("arbitrary",)),
    )(dx, w_o, merged)


def _gmlp_bwd(dm, z6, ws_b, wst_b, bs_b, lg, lb, after=()):
    S = z6.shape[1]

    def body(dm_ref, z_ref, ws_ref, wst_ref, bs_ref, lg_ref, lb_ref, *rest):
        dz_ref, dws_ref, dbs_ref, dlg_ref, dlb_ref, mix, dv = rest[-7:]

        @pl.when(pl.program_id(0) == 0)
        def _():
            dws_ref[...] = jnp.zeros_like(dws_ref)
            dbs_ref[...] = jnp.zeros_like(dbs_ref)
            dlg_ref[...] = jnp.zeros_like(dlg_ref)
            dlb_ref[...] = jnp.zeros_like(dlb_ref)
        gv, dgelu_v = _gelu_and_grad(z_ref[1].astype(F32))
        xc = gv - jnp.mean(gv, axis=-1, keepdims=True)
        rs = lax.rsqrt(jnp.mean(xc * xc, axis=-1, keepdims=True) + EPS)
        vh = xc * rs
        vb = (vh * lg_ref[...] + lb_ref[...]).astype(BF16)
        for gi in range(NH):
            cs = slice(gi * HD, (gi + 1) * HD)
            mix[:, cs] = _dot(ws_ref[gi], vb[:, cs])
        u, dgelu_u = _gelu_and_grad(z_ref[0].astype(F32))
        sa = _sigmoid(z_ref[2].astype(F32))
        mixed = mix[...] + bs_ref[...]
        dyg = dm_ref[...].astype(F32)
        dz_ref[2] = (dyg * u * mixed * sa * (1.0 - sa)).astype(BF16)
        dya = dyg * sa
        dz_ref[0] = (dya * mixed * dgelu_u).astype(BF16)
        dmix = dya * u
        dmb = dmix.astype(BF16)
        for gi in range(NH):
            cs = slice(gi * HD, (gi + 1) * HD)
            dv[:, cs] = _dot(wst_ref[gi], dmb[:, cs])
            dws_ref[gi] += _dot_nt(dmb[:, cs], vb[:, cs])
            dbs_ref[gi] += jnp.broadcast_to(jnp.sum(dmix[:, cs], axis=1, keepdims=True), (CHUNK, HD))
        dvv = dv[...]
        dlg_ref[...] += jnp.sum(dvv * vh, axis=0, keepdims=True)
        dlb_ref[...] += jnp.sum(dvv, axis=0, keepdims=True)
        dvh = dvv * lg_ref[...]
        dgv = rs * (dvh - jnp.mean(dvh, axis=-1, keepdims=True) - vh * jnp.mean(dvh * vh, axis=-1, keepdims=True))
        dz_ref[1] = (dgv * dgelu_v).astype(BF16)

    vec = pl.BlockSpec((1, D), lambda i: (0, 0))
    mat = pl.BlockSpec((NH, CHUNK, CHUNK), lambda i: (0, 0, 0))
    return pl.pallas_call(
        body, name="gmlp_bwd", grid=(S // CHUNK,),
        in_specs=[pl.BlockSpec((CHUNK, D), lambda i: (i, 0)), pl.BlockSpec((3, CHUNK, D), lambda i: (0, i, 0)), mat, mat,
                  pl.BlockSpec((CHUNK, D), lambda i: (0, 0)), vec, vec] + [ANY] * len(after),
        out_specs=[pl.BlockSpec((3, CHUNK, D), lambda i: (0, i, 0)), mat, mat, vec, vec],
        out_shape=[jax.ShapeDtypeStruct((6, S, D), BF16), jax.ShapeDtypeStruct((NH, CHUNK, CHUNK), F32),
                   jax.ShapeDtypeStruct((NH, CHUNK, HD), F32), jax.ShapeDtypeStruct((1, D), F32), jax.ShapeDtypeStruct((1, D), F32)],
        scratch_shapes=[pltpu.VMEM((CHUNK, D), F32), pltpu.VMEM((CHUNK, D), F32)],
        compiler_params=_cp(("arbitrary",)),
    )(dm, z6, ws_b, wst_b, bs_b, lg, lb, *after)


def _lru_bwd(dz6, dm, z6, h0, h1, cw, cb, wr, br, wi, bi, lam):
    S = z6.shape[1]
    nt = S // RT

    def body(dz_in, dm_ref, z_ref, h0_ref, h1_ref, cw_ref, cb_ref, wr_ref, br_ref, wi_ref, bi_ref, lam_ref,
             dz_ref, dcw_ref, dcb_ref, dwr_ref, dbr_ref, dwi_ref, dbi_ref, dlam_ref, zxp, xc_s, dhs_s, dxcp, dxc1):
        del dz_in
        lam = lam_ref[...]
        sp = _softplus_neg(lam)
        row = _row_iota()
        _fill_padded(zxp, z_ref.at[0], S)
        _conv_fwd_all(zxp, xc_s, cw_ref, cb_ref, S)
        zeros = jnp.zeros((PADR, HD), F32)
        dxcp[0:PADR, :] = zeros
        dxcp[PADR + S:2 * PADR + S, :] = zeros
        dwr_ref[...] = jnp.zeros_like(dwr_ref)
        dwi_ref[...] = jnp.zeros_like(dwi_ref)

        def pre(i, c):
            rows = pl.ds(pl.multiple_of(i * RT, RT), RT)
            hs = h0_ref[rows, :] + h1_ref[rows, :]
            dmv = dm_ref[rows, :].astype(F32)
            sb = _sigmoid(z_ref[2, rows, :].astype(F32))
            gg, dgg = _gelu_and_grad(z_ref[1, rows, :].astype(F32))
            dz_ref[2, rows, :] = (dmv * hs * gg * sb * (1.0 - sb)).astype(BF16)
            dyb = dmv * sb
            dz_ref[1, rows, :] = (dyb * hs * dgg).astype(BF16)
            dhs_s[rows, :] = dyb * gg
            return c
        lax.fori_loop(0, nt, pre, 0)

        def gate_bwd(d, gates, lamv, da, xc):
            r, gi, a, mult = gates
            dmult = lamv * gi * xc
            dgi = lamv * mult * xc
            dlog = (da - dmult * a / mult) * a
            dpr = (dlog * (-LRU_C) * sp[d:d + 1, :]) * r * (1.0 - r)
            dpi = dgi * gi * (1.0 - gi)
            xb, dprb, dpib = xc.astype(BF16), dpr.astype(BF16), dpi.astype(BF16)
            dwr_ref[d] += _dot_tn(xb, dprb)
            dwi_ref[d] += _dot_tn(xb, dpib)
            dxc = lamv * mult * gi + _dot_nt(dprb, wr_ref[d]) + _dot_nt(dpib, wi_ref[d])
            return dxc, (jnp.sum(dlog * r, axis=0, keepdims=True) * (-LRU_C), jnp.sum(dpr, axis=0, keepdims=True),
                         jnp.sum(dpi, axis=0, keepdims=True))

        def down(t0, q_next):
            rows = pl.ds(t0, RT)
            xc, h0t, dhs = xc_s[rows, :], h0_ref[rows, :], dhs_s[rows, :]
            gates = _lru_gates(xc, 0, wr_ref, br_ref, wi_ref, bi_ref, sp)
            a = gates[2]
            q, q_first = _scan_down(a, a * dhs, q_next)
            lamv = dhs + jnp.where(row == RT - 1, q_next, pltpu.roll(q, RT - 1, 0))
            tp = pl.multiple_of(jnp.maximum(t0 - PADR, 0), PADR)
            prev = jnp.where(t0 > 0, h0_ref[pl.ds(tp, PADR), :][PADR - 1:PADR, :], 0.0)
            hprev = jnp.where(row == 0, prev, pltpu.roll(h0t, 1, 0))
            dxc, sums = gate_bwd(0, gates, lamv, lamv * hprev, xc)
            dxcp[pl.ds(t0 + PADR, RT), :] = dxc
            return q_first, sums

        def up(t0, q_prev):
            rows = pl.ds(t0, RT)
            xc, h1t, dhs = xc_s[rows, :], h1_ref[rows, :], dhs_s[rows, :]
            gates = _lru_gates(xc, 1, wr_ref, br_ref, wi_ref, bi_ref, sp)
            a = gates[2]
            q, q_last = _scan_up(a, a * dhs, q_prev)
            lamv = dhs + jnp.where(row == 0, q_prev, pltpu.roll(q, 1, 0))
            tn = pl.multiple_of(jnp.minimum(t0 + RT, S - PADR), PADR)
            nxt = jnp.where(t0 + RT < S, h1_ref[pl.ds(tn, PADR), :][0:1, :], 0.0)
            hnext = jnp.where(row == RT - 1, nxt, pltpu.roll(h1t, RT - 1, 0))
            dxc, sums = gate_bwd(1, gates, lamv, lamv * hnext, xc)
            dxc1[rows, :] = dxc
            return q_last, sums

        def chains(i, carry):
            qn, qp, acc = carry
            qn, s0 = down(pl.multiple_of((nt - 1 - i) * RT, RT), qn)
            qp, s1 = up(pl.multiple_of(i * RT, RT), qp)
            return qn, qp, tuple(x + y for x, y in zip(acc, s0 + s1))

        z1 = jnp.zeros((1, HD), F32)
        _, _, (s_sp0, s_br0, s_bi0, s_sp1, s_br1, s_bi1) = lax.fori_loop(0, nt, chains, (z1, z1, (z1,) * 6))

        def add_dxc(i, c):
            t0 = pl.multiple_of(i * RT, RT)
            dxcp[pl.ds(t0 + PADR, RT), :] += dxc1[pl.ds(t0, RT), :]
            return c
        lax.fori_loop(0, nt, add_dxc, 0)

        dsp = jnp.concatenate([s_sp0, s_sp1], axis=0)
        dlam_ref[...] = -dsp * _sigmoid(-lam)
        dbr_ref[...] = jnp.concatenate([s_br0, s_br1], axis=0)
        dbi_ref[...] = jnp.concatenate([s_bi0, s_bi1], axis=0)

        def conv_bwd(i, carry):
            c0, c1, c2, c3, cb_ = carry
            t0 = pl.multiple_of(i * RT, RT)
            dwin = dxcp[pl.ds(t0, RT + 2 * PADR), :]
            d0 = _shifted(dwin, 0)
            dz_ref[0, pl.ds(t0, RT), :] = (_shifted(dwin, 1) * cw_ref[0:1, :] + d0 * cw_ref[1:2, :]
                                           + _shifted(dwin, -1) * cw_ref[2:3, :] + _shifted(dwin, -2) * cw_ref[3:4, :]).astype(BF16)
            xm1, x0, xp1, xp2 = _conv_taps(zxp[pl.ds(t0, RT + 2 * PADR), :])
            sm = lambda v: jnp.sum(v, axis=0, keepdims=True)
            return c0 + sm(d0 * xm1), c1 + sm(d0 * x0), c2 + sm(d0 * xp1), c3 + sm(d0 * xp2), cb_ + sm(d0)

        c0, c1, c2, c3, cb_ = lax.fori_loop(0, nt, conv_bwd, (z1, z1, z1, z1, z1))
        dcw_ref[...] = jnp.concatenate([c0, c1, c2, c3], axis=0)
        dcb_ref[...] = cb_

    col = pl.BlockSpec((S, HD), lambda h: (0, h))
    head = lambda h: (0, h)
    wspec = pl.BlockSpec((2, None, HD, HD), lambda h: (0, h, 0, 0))
    return pl.pallas_call(
        body, name="lru_bwd", grid=(NH,),
        in_specs=[pl.BlockSpec(memory_space=pl.ANY), col, pl.BlockSpec((3, S, HD), lambda h: (1, 0, h)), col, col] + _lru_specs(S),
        out_specs=[pl.BlockSpec((3, S, HD), lambda h: (1, 0, h)), pl.BlockSpec((4, HD), head), pl.BlockSpec((1, HD), head),
                   wspec, pl.BlockSpec((2, HD), head), wspec, pl.BlockSpec((2, HD), head), pl.BlockSpec((2, HD), head)],
        out_shape=[jax.ShapeDtypeStruct((6, S, D), BF16), jax.ShapeDtypeStruct((4, D), F32), jax.ShapeDtypeStruct((1, D), F32),
                   jax.ShapeDtypeStruct((2, NH, HD, HD), F32), jax.ShapeDtypeStruct((2, D), F32),
                   jax.ShapeDtypeStruct((2, NH, HD, HD), F32), jax.ShapeDtypeStruct((2, D), F32), jax.ShapeDtypeStruct((2, D), F32)],
        scratch_shapes=[pltpu.VMEM((S + 2 * PADR, HD), F32), pltpu.VMEM((S, HD), F32), pltpu.VMEM((S, HD), F32),
                        pltpu.VMEM((S + 2 * PADR, HD), F32), pltpu.VMEM((S, HD), F32)],
        input_output_aliases={0: 0},
        compiler_params=_cp(("parallel",)),
    )(dz6, dm, z6, h0, h1, cw, cb, wr, br, wi, bi, lam)


LAYER_SMALL = ("norm1_g", "gmlp_ln_g", "gmlp_ln_b", "gmlp_w_s", "gmlp_b_s", "conv_w", "conv_b",
               "lru_w_r", "lru_b_r", "lru_w_i", "lru_b_i", "lru_lambda", "norm2_g")


def _forward_layer(l, x, p, wb, after=(), rest=None):
    g1, g2 = p["norm1_g"][l][None], p["norm2_g"][l][None]
    ws_b = p["gmlp_w_s"][l].astype(BF16)
    tm = dict(ws_b=ws_b, wst_b=jnp.swapaxes(ws_b, 1, 2), bs_b=jnp.repeat(p["gmlp_b_s"][l].T, HD, axis=1),
              lg=p["gmlp_ln_g"][l][None], lb=p["gmlp_ln_b"][l][None])
    lru = (p["conv_w"][l], p["conv_b"][l][None], p["lru_w_r"][l].astype(BF16), p["lru_b_r"][l],
           p["lru_w_i"][l].astype(BF16), p["lru_b_i"][l], p["lru_lambda"][l])
    z6, hn1 = _mm_in(x, g1, wb["w_in"], l, after)
    ya = _gmlp_fwd(z6, tm["ws_b"], tm["bs_b"], tm["lg"], tm["lb"])
    merged, h0, h1 = _lru_fwd(z6, ya, *lru)
    if rest is not None:
        wb = dict(wb, **rest(merged))
    x1 = _mm_res(merged, wb["w_out"], x, l, "mm_out")
    gu, ff, hn2 = _mm_ffn_in(x1, g2, wb["w_ffn_in"], l)
    x2 = _mm_res(ff, wb["w_ffn_out"], x1, l, "mm_ffn_out")
    return x2, dict(x=x, z6=z6, h0=h0, h1=h1, merged=merged, x1=x1, gu=gu, ff=ff, g1=g1, g2=g2, tm=tm, lru=lru,
                    hn1=hn1, hn2=hn2, wb=wb)


def _backward_layer(l, dx, s, after=(), midway=None):
    S = dx.shape[0]
    tm, wb = s["tm"], s["wb"]
    g2 = s["g2"]
    dgu = _bwd_ffn_out(dx, wb["w_ffn_out"], s["gu"], l, after)
    dwfo = _mm_tn(s["ff"], dx, DFF_SH, f"dw_ffn_out_{l}")
    tmb = min(TM_BIG, S)
    dx1, dg2 = _mm_nt_rms_bwd(
        dgu, pl.BlockSpec((None, tmb, DFF_SH), lambda i, k: (k, i, 0)),
        wb["w_ffn_in"], pl.BlockSpec((None, D, DFF_SH), lambda i, k: (k, 0, 0)),
        4, tmb, s["x1"], g2, dx, f"bwd_ffn_in_{l}")
    dwfi = _dw_ffn_in(s["hn2"], dgu, l)
    dmg, dwo = _bwd_out(dx1, wb["w_out"], s["merged"], l)
    mid = () if midway is None else tuple(midway([dwo, dwfi, dwfo]))
    dz6, dws, dbs, dlg, dlb = _gmlp_bwd(dmg, s["z6"], tm["ws_b"], tm["wst_b"], tm["bs_b"], tm["lg"], tm["lb"], mid)
    dz6, dcw, dcb, dwr, dbr, dwi, dbi, dlam = _lru_bwd(dz6, dmg, s["z6"], s["h0"], s["h1"], *s["lru"])

    def w_map(i, k):
        sh, tl = _in_tile(k)
        return (sh, 0, tl)

    dx0, dg1 = _mm_nt_rms_bwd(
        dz6, pl.BlockSpec((None, tmb, 512), lambda i, k: (k // 2, i, k % 2)),
        wb["w_in"], pl.BlockSpec((None, D, 512), w_map),
        N_IN_T, tmb, s["x"], s["g1"], dx1, f"bwd_in_{l}")
    dwin = _dw_in(s["hn1"], dz6, l)
    small = dict(norm1_g=dg1[0], gmlp_ln_g=dlg[0], gmlp_ln_b=dlb[0], gmlp_w_s=dws, gmlp_b_s=dbs[:, :, 0], conv_w=dcw, conv_b=dcb[0],
                 lru_w_r=dwr, lru_b_r=dbr, lru_w_i=dwi, lru_b_i=dbi, lru_lambda=dlam, norm2_g=dg2[0])
    return dx0, [dwin, dwo, dwfi, dwfo], small


def _local_step(x, tgt, p, wbs):
    saved = []
    for l in range(2):
        x, s = _forward_layer(l, x, p, wbs[l])
        saved.append(s)
    dx, loss_v, dfg = _loss_head(x, tgt, p["final_g"][None])
    big, smalls = [None, None], [None, None]
    for l in (1, 0):
        dx, big[l], smalls[l] = _backward_layer(l, dx, saved[l])
    small = {k: jnp.stack([smalls[0][k], smalls[1][k]]) for k in LAYER_SMALL}
    small["final_g"] = dfg[0]
    return loss_v, dx, big, small


def _place():
    x, y, c = lax.axis_index("x"), lax.axis_index("y"), lax.axis_index("c")
    return x, y, c, 2 * x + y


def _chip_at(x, y, d):
    px = 1 - x if d & 2 else x
    py = 1 - y if d & 1 else y
    return px, py, 2 * px + py


HBM = pl.BlockSpec(memory_space=pltpu.HBM)
SEM = pl.BlockSpec(memory_space=pltpu.SEMAPHORE)
DATAFLOW = pltpu.SideEffectType.DATAFLOW_SIDE_EFFECTING


def _in_hbm(a):
    return pltpu.with_memory_space_constraint(a, pltpu.HBM)


def _cast_into(wf, l, chip_arr, name):
    _, rows, cols = wf.shape
    rh = rows // 2

    def body(ch_ref, w_ref, o_ref):
        o_ref[...] = w_ref[...].astype(BF16)

    return pl.pallas_call(
        body, name=name, out_shape=jax.ShapeDtypeStruct((4, 2, rh, cols), BF16),
        grid_spec=pltpu.PrefetchScalarGridSpec(
            num_scalar_prefetch=1, grid=(2,),
            in_specs=[pl.BlockSpec((None, None, rh, cols), lambda h, ch: (l, h, 0, 0))],
            out_specs=pl.BlockSpec((None, None, rh, cols), lambda h, ch: (ch[0], h, 0, 0))),
        compiler_params=_cp(("parallel",)),
    )(chip_arr, wf.reshape(2, 2, rh, cols))


def _half_block(ref, chip, half, to, send_sem, recv_sem):
    blk = ref.at[chip, half]
    return pltpu.make_async_remote_copy(src_ref=blk, dst_ref=blk, send_sem=send_sem, recv_sem=recv_sem,
                                        device_id=to, device_id_type=MESH)


def _gather_weights(bufs, tiny):
    nt = len(bufs)
    n_ici = nt * 3

    def body(*refs):
        tiny_ref = refs[nt]
        o_refs, tiny_o = refs[nt + 1:2 * nt + 1], refs[2 * nt + 1]
        send, recv, fsend, frecv, tsend, trecv, lsem = refs[2 * nt + 2:]
        x, y, c, chip = _place()
        local = pltpu.make_async_copy(tiny_ref, tiny_o.at[chip], lsem)
        local.start()

        def tin(d, origin_chip, to):
            return pltpu.make_async_remote_copy(
                src_ref=tiny_ref, dst_ref=tiny_o.at[origin_chip], send_sem=tsend.at[d - 1], recv_sem=trecv.at[d - 1],
                device_id=to, device_id_type=MESH)

        sends = []
        for t in range(nt):
            for d in (1, 2, 3):
                px, py, _ = _chip_at(x, y, d)
                sends.append(_half_block(o_refs[t], chip, c, (px, py, c), send.at[3 * t + d - 1], recv.at[3 * t + d - 1]))
        for d in (1, 2, 3):
            px, py, _ = _chip_at(x, y, d)
            sends.append(tin(d, chip, (px, py, c)))
        for cp in sends:
            cp.start()
        passed = []
        for t in range(nt):
            for d in (1, 2, 3):
                k = 3 * t + d - 1
                _, _, pchip = _chip_at(x, y, d)
                _half_block(o_refs[t], pchip, c, (x, y, c), send.at[k], recv.at[k]).wait_recv()
                f = _half_block(o_refs[t], pchip, c, (x, y, 1 - c), fsend.at[k], frecv.at[k])
                f.start()
                passed.append(f)
        for t in range(nt):
            for d in (1, 2, 3):
                k = 3 * t + d - 1
                _, _, pchip = _chip_at(x, y, d)
                _half_block(o_refs[t], pchip, 1 - c, (x, y, 1 - c), fsend.at[k], frecv.at[k]).wait_recv()
        for d in (1, 2, 3):
            _, _, pchip = _chip_at(x, y, d)
            tin(d, pchip, (x, y, c)).wait_recv()
        for cp in sends + passed:
            cp.wait_send()
        local.wait()

    out_shape = [jax.ShapeDtypeStruct(b.shape, b.dtype) for b in bufs]
    out_shape.append(jax.ShapeDtypeStruct((4,) + tiny.shape, tiny.dtype))
    outs = pl.pallas_call(
        body, name="gather_weights_0", out_shape=out_shape,
        in_specs=[ANY] * (nt + 1), out_specs=[ANY] * (nt + 1),
        scratch_shapes=[pltpu.SemaphoreType.DMA((n_ici,)), pltpu.SemaphoreType.DMA((n_ici,)),
                        pltpu.SemaphoreType.DMA((n_ici,)), pltpu.SemaphoreType.DMA((n_ici,)),
                        pltpu.SemaphoreType.DMA((3,)), pltpu.SemaphoreType.DMA((3,)), pltpu.SemaphoreType.DMA],
        input_output_aliases={t: t for t in range(nt)},
        compiler_params=_cp(has_side_effects=True),
    )(*bufs, tiny)
    return outs[:nt], outs[nt]


def _gather_start(bufs, tag, after=()):
    nt, na = len(bufs), len(after)

    def body(*refs):
        b_refs = refs[:nt]
        send, recv = refs[nt + na], refs[nt + na + 1]
        token = refs[2 * nt + na + 2]
        x, y, c, chip = _place()
        for t in range(nt):
            for d in (1, 2, 3):
                px, py, _ = _chip_at(x, y, d)
                _half_block(b_refs[t], chip, c, (px, py, c), send.at[3 * t + d - 1], recv.at[3 * t + d - 1]).start()
        token[...] = jnp.zeros_like(token)

    outs = pl.pallas_call(
        body, name=f"gather_start_{tag}",
        out_shape=(pltpu.SemaphoreType.DMA((3 * nt,)), pltpu.SemaphoreType.DMA((3 * nt,)),
                   *[pltpu.HBM(b.shape, b.dtype) for b in bufs], jax.ShapeDtypeStruct((8, 128), F32)),
        in_specs=[HBM] * nt + [ANY] * na, out_specs=(SEM, SEM, *[HBM] * nt, pl.BlockSpec(memory_space=pltpu.VMEM)),
        input_output_aliases={t: 2 + t for t in range(nt)},
        compiler_params=pltpu.CompilerParams(has_side_effects=DATAFLOW),
    )(*[_in_hbm(b) for b in bufs], *after)
    return outs[0], outs[1], list(outs[2:2 + nt]), outs[2 + nt]


def _gather_wait(send, recv, bufs, after, tag):
    nt = len(bufs)

    def body(*refs):
        b_refs = refs[:nt]
        send_ref, recv_ref = refs[nt], refs[nt + 1]
        x, y, c, chip = _place()
        for t in range(nt):
            for d in (1, 2, 3):
                k = 3 * t + d - 1
                px, py, pchip = _chip_at(x, y, d)
                _half_block(b_refs[t], chip, c, (px, py, c), send_ref.at[k], recv_ref.at[k]).wait_send()
                _half_block(b_refs[t], pchip, c, (px, py, c), send_ref.at[k], recv_ref.at[k]).wait_recv()

    outs = pl.pallas_call(
        body, name=f"gather_wait_{tag}", out_shape=[pltpu.HBM(b.shape, b.dtype) for b in bufs],
        in_specs=[HBM] * nt + [SEM, SEM, ANY], out_specs=[HBM] * nt,
        input_output_aliases={t: t for t in range(nt)},
        compiler_params=pltpu.CompilerParams(has_side_effects=DATAFLOW),
    )(*bufs, send, recv, after)
    return list(outs)


def _gather_pass_on(bufs, tag):
    nt = len(bufs)

    def body(*refs):
        o_refs = refs[nt:2 * nt]
        fsend, frecv = refs[2 * nt:]
        x, y, c, _ = _place()
        cps = []
        for t in range(nt):
            for d in (1, 2, 3):
                k = 3 * t + d - 1
                _, _, pchip = _chip_at(x, y, d)
                cps.append(_half_block(o_refs[t], pchip, c, (x, y, 1 - c), fsend.at[k], frecv.at[k]))
        for cp in cps:
            cp.start()
        for t in range(nt):
            for d in (1, 2, 3):
                k = 3 * t + d - 1
                _, _, pchip = _chip_at(x, y, d)
                _half_block(o_refs[t], pchip, 1 - c, (x, y, 1 - c), fsend.at[k], frecv.at[k]).wait_recv()
        for cp in cps:
            cp.wait_send()

    return pl.pallas_call(
        body, name=f"gather_pass_on_{tag}", out_shape=[jax.ShapeDtypeStruct(b.shape, b.dtype) for b in bufs],
        in_specs=[ANY] * nt, out_specs=[ANY] * nt,
        scratch_shapes=[pltpu.SemaphoreType.DMA((3 * nt,)), pltpu.SemaphoreType.DMA((3 * nt,))],
        input_output_aliases={t: t for t in range(nt)},
        compiler_params=_cp(has_side_effects=True),
    )(*bufs)


def _to_sibling_halves(gs, l):
    nt = len(gs)

    def body(*refs):
        g_refs, o_refs = refs[:nt], refs[nt:2 * nt]
        send, recv = refs[2 * nt:]
        x, y, c, _ = _place()
        cps = [pltpu.make_async_remote_copy(
            src_ref=g_refs[t].at[k, 1 - c], dst_ref=o_refs[t].at[k], send_sem=send.at[4 * t + k], recv_sem=recv.at[4 * t + k],
            device_id=(x, y, 1 - c), device_id_type=MESH) for t in range(nt) for k in range(4)]
        for cp in cps:
            cp.start()
        for cp in cps:
            cp.wait()

    return pl.pallas_call(
        body, name=f"grads_to_sibling_{l}", out_shape=[jax.ShapeDtypeStruct((4,) + g.shape[2:], g.dtype) for g in gs],
        in_specs=[ANY] * nt, out_specs=[ANY] * nt,
        scratch_shapes=[pltpu.SemaphoreType.DMA((4 * nt,)), pltpu.SemaphoreType.DMA((4 * nt,))],
        compiler_params=_cp(has_side_effects=True),
    )(*gs)


def _chip_copy(c_ref, land_ref, x, y, c, d, send_sem, recv_sem):
    px, py, pchip = _chip_at(x, y, d)
    return pltpu.make_async_remote_copy(src_ref=c_ref.at[pchip], dst_ref=land_ref.at[d - 1], send_sem=send_sem, recv_sem=recv_sem,
                                        device_id=(px, py, c), device_id_type=MESH)


def _to_chips_start(cs, l):
    nt = len(cs)
    lands = [lax.empty((3,) + a.shape[1:], a.dtype) for a in cs]

    def body(*refs):
        c_refs, land_refs = refs[:nt], refs[nt:2 * nt]
        send, recv = refs[2 * nt], refs[2 * nt + 1]
        token = refs[4 * nt + 2]
        x, y, c, _ = _place()
        for t in range(nt):
            for d in (1, 2, 3):
                _chip_copy(c_refs[t], land_refs[t], x, y, c, d, send.at[3 * t + d - 1], recv.at[3 * t + d - 1]).start()
        token[...] = jnp.zeros_like(token)

    outs = pl.pallas_call(
        body, name=f"grads_to_chips_start_{l}",
        out_shape=(pltpu.SemaphoreType.DMA((3 * nt,)), pltpu.SemaphoreType.DMA((3 * nt,)),
                   *[pltpu.HBM(a.shape, a.dtype) for a in cs], *[pltpu.HBM(a.shape, a.dtype) for a in lands],
                   jax.ShapeDtypeStruct((8, 128), F32)),
        in_specs=[HBM] * (2 * nt), out_specs=(SEM, SEM, *[HBM] * (2 * nt), pl.BlockSpec(memory_space=pltpu.VMEM)),
        input_output_aliases={i: 2 + i for i in range(2 * nt)},
        compiler_params=pltpu.CompilerParams(has_side_effects=DATAFLOW),
    )(*[_in_hbm(a) for a in cs], *[_in_hbm(a) for a in lands])
    return outs[0], outs[1], list(outs[2:2 + nt]), list(outs[2 + nt:2 + 2 * nt]), outs[2 + 2 * nt]


def _to_chips_wait(send, recv, cs, lands, after, l):
    nt = len(cs)

    def body(*refs):
        c_refs, land_refs = refs[:nt], refs[nt:2 * nt]
        send_ref, recv_ref = refs[2 * nt], refs[2 * nt + 1]
        x, y, c, _ = _place()
        for t in range(nt):
            for d in (1, 2, 3):
                cp = _chip_copy(c_refs[t], land_refs[t], x, y, c, d, send_ref.at[3 * t + d - 1], recv_ref.at[3 * t + d - 1])
                cp.wait_send()
                cp.wait_recv()

    outs = pl.pallas_call(
        body, name=f"grads_to_chips_wait_{l}", out_shape=[pltpu.HBM(a.shape, a.dtype) for a in cs + lands],
        in_specs=[HBM] * (2 * nt) + [SEM, SEM, ANY], out_specs=[HBM] * (2 * nt),
        input_output_aliases={i: i for i in range(2 * nt)},
        compiler_params=pltpu.CompilerParams(has_side_effects=DATAFLOW),
    )(*cs, *lands, send, recv, after)
    return list(outs[:nt]), list(outs[nt:])


def _join_halves(fs, l):
    nt = len(fs)

    def body(*refs):
        o_refs = refs[nt:2 * nt]
        send, recv = refs[2 * nt:]
        x, y, c, _ = _place()
        cps = [pltpu.make_async_remote_copy(
            src_ref=o_refs[t].at[c], dst_ref=o_refs[t].at[c], send_sem=send.at[t], recv_sem=recv.at[t],
            device_id=(x, y, 1 - c), device_id_type=MESH) for t in range(nt)]
        for cp in cps:
            cp.start()
        for cp in cps:
            cp.wait()

    return pl.pallas_call(
        body, name=f"grads_join_{l}", out_shape=[jax.ShapeDtypeStruct(a.shape, a.dtype) for a in fs],
        in_specs=[ANY] * nt, out_specs=[ANY] * nt,
        scratch_shapes=[pltpu.SemaphoreType.DMA((nt,)), pltpu.SemaphoreType.DMA((nt,))],
        input_output_aliases={t: t for t in range(nt)},
        compiler_params=_cp(has_side_effects=True),
    )(*fs)


def _add_half(g, r, c_arr, name):
    _, _, rh, cols = g.shape

    def body(c_ref, g_ref, r_ref, o_ref):
        o_ref[...] = (g_ref[...] + r_ref[...]).astype(BF16)

    blk = pl.BlockSpec((None, rh, cols), lambda k, cr: (k, 0, 0))
    return pl.pallas_call(
        body, name=name, out_shape=jax.ShapeDtypeStruct((4, rh, cols), BF16),
        grid_spec=pltpu.PrefetchScalarGridSpec(
            num_scalar_prefetch=1, grid=(4,),
            in_specs=[pl.BlockSpec((None, None, rh, cols), lambda k, cr: (k, cr[0], 0, 0)), blk], out_specs=blk),
        compiler_params=_cp(("parallel",)),
    )(c_arr, g, r)


def _sum_chips(cs, r3, place_arr, name):
    _, rh, cols = cs.shape
    rb = rh // 2

    def body(pl_ref, a_ref, r0_ref, r1_ref, r2_ref, o_ref):
        up = lambda ref: ref[...].astype(F32)
        o_ref[...] = ((up(a_ref) + up(r0_ref)) + up(r1_ref)) + up(r2_ref)

    def slot(d):
        return pl.BlockSpec((None, rb, cols), lambda i, pa: (d, i, 0))

    return pl.pallas_call(
        body, name=name, out_shape=jax.ShapeDtypeStruct((2, rh, cols), F32),
        grid_spec=pltpu.PrefetchScalarGridSpec(
            num_scalar_prefetch=1, grid=(2,),
            in_specs=[pl.BlockSpec((None, rb, cols), lambda i, pa: (pa[0], i, 0)), slot(0), slot(1), slot(2)],
            out_specs=pl.BlockSpec((None, rb, cols), lambda i, pa: (pa[1], i, 0))),
        compiler_params=_cp(("parallel",)),
    )(place_arr, cs, r3, r3, r3)


def _allreduce_small(pack):
    rows = pack.shape[0]
    hr = rows // 2

    def body(p_ref, o_ref, sib, slots, s1, r1, s2, r2, s3, r3):
        x, y, c, chip = _place()
        sibling = (x, y, 1 - c)
        ex = pltpu.make_async_remote_copy(src_ref=p_ref, dst_ref=sib, send_sem=s1, recv_sem=r1,
                                          device_id=sibling, device_id_type=MESH)
        ex.start()
        ex.wait()
        half = pl.ds(pl.multiple_of(c * hr, 8), hr)
        slots[0] = p_ref[half, :] + sib[half, :]
        cps = []
        for d in (1, 2, 3):
            px, py, _ = _chip_at(x, y, d)
            cps.append(pltpu.make_async_remote_copy(
                src_ref=slots.at[0], dst_ref=slots.at[d], send_sem=s2.at[d - 1], recv_sem=r2.at[d - 1],
                device_id=(px, py, c), device_id_type=MESH))
        for cp in cps:
            cp.start()
        for cp in cps:
            cp.wait()
        tot = slots[chip]
        for k in (1, 2, 3):
            tot = tot + slots[jnp.bitwise_xor(chip, k)]
        o_ref[half, :] = tot
        back = pltpu.make_async_remote_copy(src_ref=o_ref.at[half, :], dst_ref=o_ref.at[half, :], send_sem=s3, recv_sem=r3,
                                            device_id=sibling, device_id_type=MESH)
        back.start()
        back.wait()

    vm = pl.BlockSpec(memory_space=pltpu.VMEM)
    return pl.pallas_call(
        body, name="allreduce_small", out_shape=jax.ShapeDtypeStruct((rows, 128), F32),
        in_specs=[vm], out_specs=vm,
        scratch_shapes=[pltpu.VMEM((rows, 128), F32), pltpu.VMEM((4, hr, 128), F32),
                        pltpu.SemaphoreType.DMA, pltpu.SemaphoreType.DMA, pltpu.SemaphoreType.DMA((3,)), pltpu.SemaphoreType.DMA((3,)),
                        pltpu.SemaphoreType.DMA, pltpu.SemaphoreType.DMA],
        compiler_params=_cp(has_side_effects=True),
    )(pack)


def _adam_math(gv, wv, mv, vv):
    m2 = ADAM_B1 * mv + (1.0 - ADAM_B1) * gv
    v2 = ADAM_B2 * vv + (1.0 - ADAM_B2) * (gv * gv)
    m_hat = m2 / (1.0 - ADAM_B1 ** ADAM_STEP)
    v_hat = v2 / (1.0 - ADAM_B2 ** ADAM_STEP)
    return -ADAM_LR * (m_hat / (jnp.sqrt(v_hat) + ADAM_EPS) + ADAM_WD * wv), m2, v2


def _adam(g, w, m, v, name):
    rows, cols = g.shape
    rb = rows // 4

    def body(g_ref, w_ref, m_ref, v_ref, d_ref, m2_ref, v2_ref):
        d_ref[...], m2_ref[...], v2_ref[...] = _adam_math(g_ref[...], w_ref[...], m_ref[...], v_ref[...])

    blk = pl.BlockSpec((rb, cols), lambda i: (i, 0))
    shp = jax.ShapeDtypeStruct((rows, cols), F32)
    return pl.pallas_call(
        body, name=name, grid=(4,), in_specs=[blk] * 4, out_specs=[blk] * 3, out_shape=[shp] * 3,
        compiler_params=_cp(("parallel",)),
    )(g, w, m, v)


def _adam_layer(g, w, m, v, l, prev, name):
    rows, cols = g.shape
    rb = rows // 4

    def body(g_ref, w_ref, m_ref, v_ref, *rest):
        go_ref, d_ref, m2_ref, v2_ref = rest[-4:]
        gv = g_ref[...]
        go_ref[...] = gv
        d_ref[...], m2_ref[...], v2_ref[...] = _adam_math(gv, w_ref[...], m_ref[...], v_ref[...])

    lay = pl.BlockSpec((None, rb, cols), lambda i: (l, i, 0))
    shp = jax.ShapeDtypeStruct((2, rows, cols), F32)
    prev = () if prev is None else tuple(prev)
    return pl.pallas_call(
        body, name=name, grid=(4,), in_specs=[pl.BlockSpec((rb, cols), lambda i: (i, 0)), lay, lay, lay] + [ANY] * len(prev),
        out_specs=[lay] * 4, out_shape=[shp] * 4,
        input_output_aliases={4 + j: j for j in range(len(prev))},
        compiler_params=_cp(("parallel",)),
    )(g, w, m, v, *prev)


def _rows128(a):
    return a.reshape(-1, 128)


def _pack(arrs, mult):
    parts = [_rows128(a) for a in arrs]
    rows = sum(q.shape[0] for q in parts)
    pad = -rows % mult
    if pad:
        parts.append(jnp.zeros((pad, 128), F32))
    return jnp.concatenate(parts, axis=0)


def _unpack(pack, shapes):
    out, o = [], 0
    for s in shapes:
        n = 1
        for e in s:
            n *= e
        out.append(pack[o:o + n // 128].reshape(s))
        o += n // 128
    return out


WEIGHTS = ['norm1_g', 'w_in', 'gmlp_ln_g', 'gmlp_ln_b', 'gmlp_w_s', 'gmlp_b_s', 'conv_w', 'conv_b', 'lru_w_r', 'lru_b_r', 'lru_w_i',
           'lru_b_i', 'lru_lambda', 'w_out', 'norm2_g', 'w_ffn_in', 'w_ffn_out', 'final_g']
BIG = ['w_in', 'w_out', 'w_ffn_in', 'w_ffn_out']
SMALL = [n for n in WEIGHTS if n not in BIG]
CHIP_SHARDED_SMALL = ['conv_w', 'lru_b_r', 'lru_b_i', 'lru_lambda']


def kernel(x, norm1_g, w_in, gmlp_ln_g, gmlp_ln_b, gmlp_w_s, gmlp_b_s, conv_w, conv_b, lru_w_r, lru_b_r, lru_w_i, lru_b_i, lru_lambda, w_out, norm2_g, w_ffn_in, w_ffn_out, final_g, loss_target, m_norm1_g, m_w_in, m_gmlp_ln_g, m_gmlp_ln_b, m_gmlp_w_s, m_gmlp_b_s, m_conv_w, m_conv_b, m_lru_w_r, m_lru_b_r, m_lru_w_i, m_lru_b_i, m_lru_lambda, m_w_out, m_norm2_g, m_w_ffn_in, m_w_ffn_out, m_final_g, v_norm1_g, v_w_in, v_gmlp_ln_g, v_gmlp_ln_b, v_gmlp_w_s, v_gmlp_b_s, v_conv_w, v_conv_b, v_lru_w_r, v_lru_b_r, v_lru_w_i, v_lru_b_i, v_lru_lambda, v_w_out, v_norm2_g, v_w_ffn_in, v_w_ffn_out, v_final_g):
    a = dict(locals())
    w = {n: a[n] for n in WEIGHTS}
    mom = {n: a["m_" + n] for n in WEIGHTS}
    var = {n: a["v_" + n] for n in WEIGHTS}
    _, _, c, chip = _place()
    c_arr, chip_arr = jnp.reshape(c, (1,)).astype(jnp.int32), jnp.reshape(chip, (1,)).astype(jnp.int32)
    place_arr = jnp.stack([chip, c]).astype(jnp.int32)

    first, rest = BIG[:1], BIG[1:]

    def as_weights(names, full):
        wb = {n: f.reshape(4, 2 * f.shape[2], f.shape[3]) for n, f in zip(names, full)}
        if "w_out" in wb:
            wb["w_out"] = wb["w_out"].reshape(D, D)
            wb["w_ffn_out"] = wb["w_ffn_out"].reshape(DFF, D)
        return wb

    bufs = [{n: _cast_into(w[n], l, chip_arr, f"cast_{n}_{l}") for n in BIG} for l in range(2)]
    tiny = _pack([w[n] for n in CHIP_SHARDED_SMALL], 8)
    w_in0, tiny_full = _gather_weights([bufs[0]["w_in"]], tiny)
    fly0 = _gather_start([bufs[0][n] for n in rest], "0", after=(tiny_full,))
    fly1 = _gather_start([bufs[1][n] for n in BIG], "1", after=(fly0[3],))
    p = {n: w[n] for n in SMALL}
    parts = [_unpack(tiny_full[k], [w[n].shape for n in CHIP_SHARDED_SMALL]) for k in range(4)]
    for i, n in enumerate(CHIP_SHARDED_SMALL):
        p[n] = jnp.concatenate([parts[k][i] for k in range(4)], axis=-1)

    def landed(fly, names, after, tag):
        return as_weights(names, _gather_pass_on(_gather_wait(fly[0], fly[1], fly[2], after, tag), tag))

    xa, saved0 = _forward_layer(0, x[0], p, as_weights(first, w_in0), after=(fly0[3], fly1[3]),
                                rest=lambda merged: landed(fly0, rest, merged, "0"))
    xb, saved1 = _forward_layer(1, xa, p, landed(fly1, BIG, xa, "1"))
    dxb, loss_v, dfg = _loss_head(xb, loss_target[0], p["final_g"][None])
    loss = lax.psum(loss_v[0, 0], ("x", "y", "c"))

    out, flying = {}, {}

    def reduce_start(grads, names, l, tag):
        gs = [g.reshape(4, 2, -1, g.shape[-1]) for g in grads]
        from_sib = _to_sibling_halves(gs, tag)
        cs = [_add_half(g, r, c_arr, f"add_half_{n}_{l}") for n, g, r in zip(names, gs, from_sib)]
        flying[tag] = (names, l) + tuple(_to_chips_start(cs, tag))
        return (flying[tag][-1],)

    def reduce_finish(tag, after):
        names, l, send, recv, cs, lands, _ = flying[tag]
        cs, lands = _to_chips_wait(send, recv, cs, lands, after, tag)
        ts = [_sum_chips(cc, r3, place_arr, f"sum_chips_{n}_{l}") for n, cc, r3 in zip(names, cs, lands)]
        for n, j in zip(names, _join_halves(ts, tag)):
            out[n] = _adam_layer(j.reshape(w[n].shape[1:]), w[n], mom[n], var[n], l, out.get(n), f"adam_{n}_{l}")
        return out[names[-1]][0]

    def midway0(grads):
        reduce_finish("1a", grads[0])
        reduce_finish("1b", grads[0])
        return reduce_start(grads, rest, 0, "0a")

    dxa, big1, small1 = _backward_layer(1, dxb, saved1, midway=lambda grads: reduce_start(grads, rest, 1, "1a"))
    dx, big0, small0 = _backward_layer(0, dxa, saved0, after=reduce_start(big1[:1], first, 1, "1b"), midway=midway0)
    reduce_start(big0[:1], first, 0, "0b")
    reduce_finish("0b", reduce_finish("0a", dx))
    small = {k: jnp.stack([small0[k], small1[k]]) for k in LAYER_SMALL}
    small["final_g"] = dfg[0]

    full_shapes = [small[n].shape for n in SMALL]
    red = _unpack(_allreduce_small(_pack([small[n] for n in SMALL], 16)), full_shapes)
    g_small = []
    for n, g in zip(SMALL, red):
        if n in CHIP_SHARDED_SMALL:
            g = lax.dynamic_slice_in_dim(g, chip * w[n].shape[-1], w[n].shape[-1], axis=g.ndim - 1)
        g_small.append(g)
    shapes = [w[n].shape for n in SMALL]
    packs = [_pack(lst, 32) for lst in (g_small, [w[n] for n in SMALL], [mom[n] for n in SMALL], [var[n] for n in SMALL])]
    upd = [_unpack(u, shapes) for u in _adam(*packs, "adam_small")]
    for i, n in enumerate(SMALL):
        out[n] = [g_small[i], upd[0][i], upd[1][i], upd[2][i]]

    return (loss, dx[None]) + tuple(out[n][i] for i in range(4) for n in WEIGHTS)
```

```python
import functools

import jax
import jax.numpy as jnp
from jax import lax
from jax.experimental import pallas as pl
from jax.experimental.pallas import tpu as pltpu

F32 = jnp.float32
BF16 = jnp.bfloat16
MESH = pl.DeviceIdType.MESH

D = 1024
NH = 8
HD = 128
CHUNK = 128
N_IN_T = 12
DFF = 2816
DFF_SH = 1408
EPS = 1e-6
LRU_C = 8.0
ADAM_LR, ADAM_B1, ADAM_B2, ADAM_EPS, ADAM_WD, ADAM_STEP = 0.001, 0.9, 0.999, 1e-08, 0.01, 10

TM = 512
TM_BIG = 1024
RT = 128
PADR = 8
VMEM_LIMIT = 56 * 1024 * 1024


def _cp(sem=None, **kw):
    if sem is not None:
        kw["dimension_semantics"] = sem
    return pltpu.CompilerParams(vmem_limit_bytes=VMEM_LIMIT, **kw)


_GC = 0.7978845608028654


def _sigmoid(x):
    return 1.0 / (1.0 + jnp.exp(-x))


def _gelu(x):
    return 0.5 * x * (1.0 + jnp.tanh(_GC * (x + 0.044715 * x * x * x)))


def _gelu_and_grad(x):
    t = jnp.tanh(_GC * (x + 0.044715 * x * x * x))
    g = 0.5 * x * (1.0 + t)
    dg = 0.5 * (1.0 + t) + 0.5 * x * (1.0 - t * t) * _GC * (1.0 + 3 * 0.044715 * x * x)
    return g, dg


def _softplus_neg(lam):
    y = jnp.exp(-jnp.abs(lam))
    u = 1.0 + y
    l1p = jnp.where(u == 1.0, y, jnp.log(u) * y / (u - 1.0))
    return jnp.maximum(-lam, 0.0) + l1p


def _dot(a, b):
    return jnp.dot(a, b, preferred_element_type=F32)


def _dot_nt(a, b):
    return lax.dot_general(a, b, (((1,), (1,)), ((), ())), preferred_element_type=F32)


def _dot_tn(a, b):
    return lax.dot_general(a, b, (((0,), (0,)), ((), ())), preferred_element_type=F32)


def _rms_hat(x):
    r = lax.rsqrt(jnp.mean(x * x, axis=-1, keepdims=True) + EPS)
    return x * r, r


def _rms_bwd(dh, x, g):
    xh, r = _rms_hat(x)
    dxh = dh * g
    dx = r * (dxh - xh * jnp.mean(dxh * xh, axis=-1, keepdims=True))
    return dx, jnp.sum(dh * xh, axis=0, keepdims=True)


def _in_tile(j):
    m, hf = j // 2, j % 2
    orig = jnp.where(m < 2, m, jnp.where(m == 2, 4, jnp.where(m < 5, m - 1, 5)))
    t = orig * 2 + hf
    return t // 3, t % 3


ANY = pl.BlockSpec(memory_space=pl.ANY)


def _mm_in(x, g, w_in, l, after=()):
    S = x.shape[0]
    tm = min(TM_BIG, S)

    def body(x_ref, g_ref, w_ref, *rest):
        o_ref, h_ref = rest[-2:]

        @pl.when(pl.program_id(1) == 0)
        def _():
            xh, _ = _rms_hat(x_ref[...])
            h_ref[...] = (xh * g_ref[...]).astype(BF16)
        o_ref[...] = _dot(h_ref[...], w_ref[...]).astype(BF16)

    def w_map(i, j):
        sh, tl = _in_tile(j)
        return (sh, 0, tl)

    return pl.pallas_call(
        body, name=f"mm_in_{l}", grid=(S // tm, N_IN_T),
        in_specs=[pl.BlockSpec((tm, D), lambda i, j: (i, 0)), pl.BlockSpec((1, D), lambda i, j: (0, 0)),
                  pl.BlockSpec((None, D, 512), w_map)] + [ANY] * len(after),
        out_specs=[pl.BlockSpec((None, tm, 512), lambda i, j: (j // 2, i, j % 2)), pl.BlockSpec((tm, D), lambda i, j: (i, 0))],
        out_shape=[jax.ShapeDtypeStruct((6, S, D), BF16), jax.ShapeDtypeStruct((S, D), BF16)],
        compiler_params=_cp(("parallel", "arbitrary")),
    )(x, g, w_in, *after)


def _mm_res(a, w, res, l, name):
    S, K = a.shape

    def body(a_ref, w_ref, r_ref, o_ref):
        o_ref[...] = r_ref[...] + _dot(a_ref[...], w_ref[...])

    return pl.pallas_call(
        body, name=f"{name}_{l}", grid=(S // TM,),
        in_specs=[pl.BlockSpec((TM, K), lambda i: (i, 0)), pl.BlockSpec((K, D), lambda i: (0, 0)),
                  pl.BlockSpec((TM, D), lambda i: (i, 0))],
        out_specs=pl.BlockSpec((TM, D), lambda i: (i, 0)),
        out_shape=jax.ShapeDtypeStruct((S, D), F32),
        compiler_params=_cp(("parallel",)),
    )(a, w, res)


def _mm_ffn_in(x, g, w_fi, l):
    S = x.shape[0]

    def body(x_ref, g_ref, w_ref, gu_ref, ff_ref, h_ref):
        @pl.when(pl.program_id(1) == 0)
        def _():
            xh, _ = _rms_hat(x_ref[...])
            h_ref[...] = (xh * g_ref[...]).astype(BF16)
        hv = h_ref[...]
        ga = _dot(hv, w_ref[0])
        gb = _dot(hv, w_ref[1])
        gu_ref[0] = ga.astype(BF16)
        gu_ref[1] = gb.astype(BF16)
        ff_ref[...] = (ga * _sigmoid(ga) * gb).astype(BF16)

    gu, ff, h = pl.pallas_call(
        body, name=f"mm_ffn_in_{l}", grid=(S // TM, 2),
        in_specs=[pl.BlockSpec((TM, D), lambda i, s: (i, 0)), pl.BlockSpec((1, D), lambda i, s: (0, 0)),
                  pl.BlockSpec((2, None, D, DFF_SH), lambda i, s: (0, s, 0, 0))],
        out_specs=[pl.BlockSpec((2, None, TM, DFF_SH), lambda i, s: (0, s, i, 0)),
                   pl.BlockSpec((TM, DFF_SH), lambda i, s: (i, s)),
                   pl.BlockSpec((TM, D), lambda i, s: (i, 0))],
        out_shape=[jax.ShapeDtypeStruct((2, 2, S, DFF_SH), BF16), jax.ShapeDtypeStruct((S, DFF), BF16),
                   jax.ShapeDtypeStruct((S, D), BF16)],
        compiler_params=_cp(("parallel", "arbitrary")),
    )(x, g, w_fi.reshape(2, 2, D, DFF_SH))
    return gu.reshape(4, S, DFF_SH), ff, h


def _gmlp_fwd(z6, ws_b, bs_b, lg, lb):
    S = z6.shape[1]

    def body(z_ref, ws_ref, bs_ref, lg_ref, lb_ref, o_ref, mix):
        gv = _gelu(z_ref[1].astype(F32))
        xc = gv - jnp.mean(gv, axis=-1, keepdims=True)
        rs = lax.rsqrt(jnp.mean(xc * xc, axis=-1, keepdims=True) + EPS)
        vb = (xc * rs * lg_ref[...] + lb_ref[...]).astype(BF16)
        for gi in range(NH):
            cs = slice(gi * HD, (gi + 1) * HD)
            mix[:, cs] = _dot(ws_ref[gi], vb[:, cs])
        o_ref[...] = (_sigmoid(z_ref[2].astype(F32)) * _gelu(z_ref[0].astype(F32)) * (mix[...] + bs_ref[...])).astype(BF16)

    return pl.pallas_call(
        body, name="gmlp_fwd", grid=(S // CHUNK,),
        in_specs=[pl.BlockSpec((3, CHUNK, D), lambda i: (0, i, 0)), pl.BlockSpec((NH, CHUNK, CHUNK), lambda i: (0, 0, 0)),
                  pl.BlockSpec((CHUNK, D), lambda i: (0, 0)), pl.BlockSpec((1, D), lambda i: (0, 0)),
                  pl.BlockSpec((1, D), lambda i: (0, 0))],
        out_specs=pl.BlockSpec((CHUNK, D), lambda i: (i, 0)),
        out_shape=jax.ShapeDtypeStruct((S, D), BF16),
        scratch_shapes=[pltpu.VMEM((CHUNK, D), F32)],
        compiler_params=_cp(("parallel",)),
    )(z6, ws_b, bs_b, lg, lb)


def _row_iota():
    return lax.broadcasted_iota(jnp.int32, (RT, HD), 0)


SUB = 8
UNROLL = 4
UNROLL_BWD = 2


def _scan_up(a, b, carry):
    row = lax.broadcasted_iota(jnp.int32, (SUB, HD), 0)
    masks = [(d, row >= d) for d in (1, 2, 4)]
    c = jnp.broadcast_to(carry, (SUB, HD))
    hs = []
    for j in range(RT // SUB):
        aj, bj = a[SUB * j:SUB * (j + 1)], b[SUB * j:SUB * (j + 1)]
        for d, m in masks:
            bj = bj + aj * jnp.where(m, pltpu.roll(bj, d, 0), 0.0)
            aj = aj * jnp.where(m, pltpu.roll(aj, d, 0), 1.0)
        h = bj + aj * c
        hs.append(h)
        c = jnp.broadcast_to(h[SUB - 1:SUB, :], (SUB, HD))
    return jnp.concatenate(hs, axis=0), hs[-1][SUB - 1:SUB, :]


def _scan_down(a, b, carry):
    row = lax.broadcasted_iota(jnp.int32, (SUB, HD), 0)
    masks = [(d, row < SUB - d) for d in (1, 2, 4)]
    c = jnp.broadcast_to(carry, (SUB, HD))
    hs = []
    for j in reversed(range(RT // SUB)):
        aj, bj = a[SUB * j:SUB * (j + 1)], b[SUB * j:SUB * (j + 1)]
        for d, m in masks:
            bj = bj + aj * jnp.where(m, pltpu.roll(bj, SUB - d, 0), 0.0)
            aj = aj * jnp.where(m, pltpu.roll(aj, SUB - d, 0), 1.0)
        h = bj + aj * c
        hs.append(h)
        c = jnp.broadcast_to(h[0:1, :], (SUB, HD))
    return jnp.concatenate(hs[::-1], axis=0), hs[-1][0:1, :]


def _lru_gates(xc, d, wr_ref, br_ref, wi_ref, bi_ref, sp):
    xb = xc.astype(BF16)
    r = _sigmoid(_dot(xb, wr_ref[d]) + br_ref[d:d + 1, :])
    i = _sigmoid(_dot(xb, wi_ref[d]) + bi_ref[d:d + 1, :])
    log_a = -LRU_C * r * sp[d:d + 1, :]
    a = jnp.exp(log_a)
    mult = jnp.sqrt(jnp.maximum(-jnp.tanh(log_a) * (a * a + 1.0), 0.0))
    return r, i, a, mult


def _shifted(win, k):
    w = RT + 2 * PADR
    v = win if k == 0 else pltpu.roll(win, (-k) % w, 0)
    return v[PADR:PADR + RT]


def _conv_taps(win):
    return [_shifted(win, k) for k in (-1, 0, 1, 2)]


def _fill_padded(dst, src_ref, S):
    zeros = jnp.zeros((PADR, HD), F32)
    dst[0:PADR, :] = zeros
    dst[PADR + S:2 * PADR + S, :] = zeros

    def cp(i, c):
        t0 = pl.multiple_of(i * RT, RT)
        dst[pl.ds(t0 + PADR, RT), :] = src_ref[pl.ds(t0, RT), :].astype(F32)
        return c
    lax.fori_loop(0, S // RT, cp, 0)


def _conv_fwd_all(zxp, xc_s, cw_ref, cb_ref, S):
    def cv(i, c):
        t0 = pl.multiple_of(i * RT, RT)
        xm1, x0, xp1, xp2 = _conv_taps(zxp[pl.ds(t0, RT + 2 * PADR), :])
        xc_s[pl.ds(t0, RT), :] = (cb_ref[...] + xm1 * cw_ref[0:1, :] + x0 * cw_ref[1:2, :]
                                  + xp1 * cw_ref[2:3, :] + xp2 * cw_ref[3:4, :])
        return c
    lax.fori_loop(0, S // RT, cv, 0)


def _lru_specs(S):
    head = lambda h: (0, h)
    return [pl.BlockSpec((4, HD), head), pl.BlockSpec((1, HD), head),
            pl.BlockSpec((2, None, HD, HD), lambda h: (0, h, 0, 0)), pl.BlockSpec((2, HD), head),
            pl.BlockSpec((2, None, HD, HD), lambda h: (0, h, 0, 0)), pl.BlockSpec((2, HD), head),
            pl.BlockSpec((2, HD), head)]


def _lru_fwd(z6, ya, cw, cb, wr, br, wi, bi, lam):
    S = z6.shape[1]
    nt = S // RT

    def body(z_ref, ya_ref, cw_ref, cb_ref, wr_ref, br_ref, wi_ref, bi_ref, lam_ref, mg_ref, h0_ref, h1_ref, zxp, xc_s):
        sp = _softplus_neg(lam_ref[...])
        _fill_padded(zxp, z_ref.at[0], S)
        _conv_fwd_all(zxp, xc_s, cw_ref, cb_ref, S)

        def scans(i, carry):
            cu, cd = carry
            for u in range(UNROLL):
                j = i * UNROLL + u
                ru = pl.ds(pl.multiple_of(j * RT, RT), RT)
                rd = pl.ds(pl.multiple_of((nt - 1 - j) * RT, RT), RT)
                xu, xd = xc_s[ru, :], xc_s[rd, :]
                _, gi, a, mult = _lru_gates(xu, 0, wr_ref, br_ref, wi_ref, bi_ref, sp)
                hu, cu = _scan_up(a, mult * gi * xu, cu)
                h0_ref[ru, :] = hu
                _, gi, a, mult = _lru_gates(xd, 1, wr_ref, br_ref, wi_ref, bi_ref, sp)
                hd, cd = _scan_down(a, mult * gi * xd, cd)
                h1_ref[rd, :] = hd
            return cu, cd
        z1 = jnp.zeros((1, HD), F32)
        lax.fori_loop(0, nt // UNROLL, scans, (z1, z1))

        def merge(i, c):
            rows = pl.ds(pl.multiple_of(i * RT, RT), RT)
            yb = (h0_ref[rows, :] + h1_ref[rows, :]) * _gelu(z_ref[1, rows, :].astype(F32))
            mg_ref[rows, :] = (ya_ref[rows, :].astype(F32) + _sigmoid(z_ref[2, rows, :].astype(F32)) * yb).astype(BF16)
            return c
        lax.fori_loop(0, nt, merge, 0)

    col = pl.BlockSpec((S, HD), lambda h: (0, h))
    return pl.pallas_call(
        body, name="lru_fwd", grid=(NH,),
        in_specs=[pl.BlockSpec((3, S, HD), lambda h: (1, 0, h)), col] + _lru_specs(S),
        out_specs=[col, col, col],
        out_shape=[jax.ShapeDtypeStruct((S, D), BF16), jax.ShapeDtypeStruct((S, D), F32), jax.ShapeDtypeStruct((S, D), F32)],
        scratch_shapes=[pltpu.VMEM((S + 2 * PADR, HD), F32), pltpu.VMEM((S, HD), F32)],
        compiler_params=_cp(("parallel",)),
    )(z6, ya, cw, cb, wr, br, wi, bi, lam)


def _loss_head(x, tgt, g):
    S = x.shape[0]

    def body(x_ref, t_ref, g_ref, dx_ref, loss_ref, dg_ref):
        @pl.when(pl.program_id(0) == 0)
        def _():
            loss_ref[...] = jnp.zeros_like(loss_ref)
            dg_ref[...] = jnp.zeros_like(dg_ref)
        xv = x_ref[...]
        xh, _ = _rms_hat(xv)
        e = xh * g_ref[...] - t_ref[...]
        loss_ref[...] += jnp.sum(e * e) * (0.5 / D)
        dx, dgs = _rms_bwd(e * (1.0 / D), xv, g_ref[...])
        dx_ref[...] = dx
        dg_ref[...] += dgs

    return pl.pallas_call(
        body, name="loss_head", grid=(S // TM,),
        in_specs=[pl.BlockSpec((TM, D), lambda i: (i, 0)), pl.BlockSpec((TM, D), lambda i: (i, 0)),
                  pl.BlockSpec((1, D), lambda i: (0, 0))],
        out_specs=[pl.BlockSpec((TM, D), lambda i: (i, 0)), pl.BlockSpec((1, 128), lambda i: (0, 0)),
                   pl.BlockSpec((1, D), lambda i: (0, 0))],
        out_shape=[jax.ShapeDtypeStruct((S, D), F32), jax.ShapeDtypeStruct((1, 128), F32), jax.ShapeDtypeStruct((1, D), F32)],
        compiler_params=_cp(("arbitrary",)),
    )(x, tgt, g)


def _bwd_ffn_out(dx, w_fo, gu, l, after=()):
    S = dx.shape[0]

    def body(dx_ref, w_ref, gu_ref, *rest):
        o_ref = rest[-1]
        d = _dot_nt(dx_ref[...].astype(BF16), w_ref[...])
        ga, gb = gu_ref[0].astype(F32), gu_ref[1].astype(F32)
        sg = _sigmoid(ga)
        o_ref[0] = (d * gb * sg * (1.0 + ga * (1.0 - sg))).astype(BF16)
        o_ref[1] = (d * ga * sg).astype(BF16)

    pair = pl.BlockSpec((2, None, TM, DFF_SH), lambda i, s: (0, s, i, 0))
    dgu = pl.pallas_call(
        body, name=f"bwd_ffn_out_{l}", grid=(S // TM, 2),
        in_specs=[pl.BlockSpec((TM, D), lambda i, s: (i, 0)), pl.BlockSpec((DFF_SH, D), lambda i, s: (s, 0)), pair]
        + [ANY] * len(after),
        out_specs=pair,
        out_shape=jax.ShapeDtypeStruct((2, 2, S, DFF_SH), BF16),
        compiler_params=_cp(("parallel", "arbitrary")),
    )(dx, w_fo, gu.reshape(2, 2, S, DFF_SH), *after)
    return dgu.reshape(4, S, DFF_SH)


def _mm_tn(a, b, m_blk, name):
    S, M = a.shape

    def body(a_ref, b_ref, o_ref):
        @pl.when(pl.program_id(1) == 0)
        def _():
            o_ref[...] = jnp.zeros_like(o_ref)
        o_ref[...] += _dot_tn(a_ref[...], b_ref[...].astype(BF16))

    return pl.pallas_call(
        body, name=name, grid=(M // m_blk, S // TM),
        in_specs=[pl.BlockSpec((TM, m_blk), lambda m, k: (k, m)), pl.BlockSpec((TM, D), lambda m, k: (k, 0))],
        out_specs=pl.BlockSpec((m_blk, D), lambda m, k: (m, 0)),
        out_shape=jax.ShapeDtypeStruct((M, D), F32),
        compiler_params=_cp(("parallel", "arbitrary")),
    )(a, b)


def _mm_nt_rms_bwd(a, a_spec, w, w_spec, nk, tm, x, g, dres, name):
    S = x.shape[0]

    def body(a_ref, w_ref, x_ref, g_ref, r_ref, dx_ref, dg_ref, acc):
        i, k = pl.program_id(0), pl.program_id(1)

        @pl.when(k == 0)
        def _():
            acc[...] = jnp.zeros_like(acc)
        acc[...] += _dot_nt(a_ref[...], w_ref[...])

        @pl.when(jnp.logical_and(i == 0, k == 0))
        def _():
            dg_ref[...] = jnp.zeros_like(dg_ref)

        @pl.when(k == nk - 1)
        def _():
            dx, dgs = _rms_bwd(acc[...], x_ref[...], g_ref[...])
            dx_ref[...] = r_ref[...] + dx
            dg_ref[...] += dgs

    row = pl.BlockSpec((tm, D), lambda i, k: (i, 0))
    vec = pl.BlockSpec((1, D), lambda i, k: (0, 0))
    return pl.pallas_call(
        body, name=name, grid=(S // tm, nk),
        in_specs=[a_spec, w_spec, row, vec, row],
        out_specs=[row, vec],
        out_shape=[jax.ShapeDtypeStruct((S, D), F32), jax.ShapeDtypeStruct((1, D), F32)],
        scratch_shapes=[pltpu.VMEM((tm, D), F32)],
        compiler_params=_cp(("arbitrary", "arbitrary")),
    )(a, w, x, g, dres)


def _dw_ffn_in(h, dgu, l):
    S = h.shape[0]

    def body(h_ref, b_ref, o_ref):
        @pl.when(pl.program_id(1) == 0)
        def _():
            o_ref[...] = jnp.zeros_like(o_ref)
        o_ref[...] += _dot_tn(h_ref[...], b_ref[...])

    return pl.pallas_call(
        body, name=f"dw_ffn_in_{l}", grid=(4, S // TM),
        in_specs=[pl.BlockSpec((TM, D), lambda j, k: (k, 0)), pl.BlockSpec((None, TM, DFF_SH), lambda j, k: (j, k, 0))],
        out_specs=pl.BlockSpec((None, D, DFF_SH), lambda j, k: (j, 0, 0)),
        out_shape=jax.ShapeDtypeStruct((4, D, DFF_SH), F32),
        compiler_params=_cp(("parallel", "arbitrary")),
    )(h, dgu)


_HALF_COMPS = ((0, 1, 3), (4, 2, 5))


def _dw_in(h, dz6, l):
    S = h.shape[0]

    def body(h_ref, d0_ref, d1_ref, d2_ref, o_ref):
        @pl.when(pl.program_id(1) == 0)
        def _():
            o_ref[...] = jnp.zeros_like(o_ref)
        hv = h_ref[...]
        for q, d_ref in enumerate((d0_ref, d1_ref, d2_ref)):
            for hf in range(2):
                col = 1024 * q + 512 * hf
                o_ref[col // 1536, :, col % 1536:col % 1536 + 512] += _dot_tn(hv, d_ref[:, 512 * hf:512 * (hf + 1)])

    def comp(q):
        return pl.BlockSpec((None, TM, D), lambda p, k: (jnp.where(p == 0, _HALF_COMPS[0][q], _HALF_COMPS[1][q]), k, 0))

    return pl.pallas_call(
        body, name=f"dw_in_{l}", grid=(2, S // TM),
        in_specs=[pl.BlockSpec((TM, D), lambda p, k: (k, 0)), comp(0), comp(1), comp(2)],
        out_specs=pl.BlockSpec((2, D, 1536), lambda p, k: (p, 0, 0)),
        out_shape=jax.ShapeDtypeStruct((4, D, 1536), F32),
        compiler_params=_cp(("parallel", "arbitrary")),
    )(h, dz6, dz6, dz6)


def _bwd_out(dx, w_o, merged, l):
    S = dx.shape[0]

    def body(dx_ref, w_ref, m_ref, dm_ref, dw_ref):
        @pl.when(pl.program_id(0) == 0)
        def _():
            dw_ref[...] = jnp.zeros_like(dw_ref)
        dxb = dx_ref[...].astype(BF16)
        dm_ref[...] = _dot_nt(dxb, w_ref[...]).astype(BF16)
        dw_ref[...] += _dot_tn(m_ref[...], dxb)

    row = pl.BlockSpec((TM, D), lambda i: (i, 0))
    return pl.pallas_call(
        body, name=f"bwd_out_{l}", grid=(S // TM,),
        in_specs=[row, pl.BlockSpec((D, D), lambda i: (0, 0)), row],
        out_specs=[row, pl.BlockSpec((D, D), lambda i: (0, 0))],
        out_shape=[jax.ShapeDtypeStruct((S, D), BF16), jax.ShapeDtypeStruct((D, D), F32)],
        compiler_params=_cp(("arbitrary",)),
    )(dx, w_o, merged)


def _gmlp_bwd(dm, z6, ws_b, wst_b, bs_b, lg, lb, after=()):
    S = z6.shape[1]

    def body(dm_ref, z_ref, ws_ref, wst_ref, bs_ref, lg_ref, lb_ref, *rest):
        dz_ref, dws_ref, dbs_ref, dlg_ref, dlb_ref, mix, dv = rest[-7:]

        @pl.when(pl.program_id(0) == 0)
        def _():
            dws_ref[...] = jnp.zeros_like(dws_ref)
            dbs_ref[...] = jnp.zeros_like(dbs_ref)
            dlg_ref[...] = jnp.zeros_like(dlg_ref)
            dlb_ref[...] = jnp.zeros_like(dlb_ref)
        gv, dgelu_v = _gelu_and_grad(z_ref[1].astype(F32))
        xc = gv - jnp.mean(gv, axis=-1, keepdims=True)
        rs = lax.rsqrt(jnp.mean(xc * xc, axis=-1, keepdims=True) + EPS)
        vh = xc * rs
        vb = (vh * lg_ref[...] + lb_ref[...]).astype(BF16)
        for gi in range(NH):
            cs = slice(gi * HD, (gi + 1) * HD)
            mix[:, cs] = _dot(ws_ref[gi], vb[:, cs])
        u, dgelu_u = _gelu_and_grad(z_ref[0].astype(F32))
        sa = _sigmoid(z_ref[2].astype(F32))
        mixed = mix[...] + bs_ref[...]
        dyg = dm_ref[...].astype(F32)
        dz_ref[2] = (dyg * u * mixed * sa * (1.0 - sa)).astype(BF16)
        dya = dyg * sa
        dz_ref[0] = (dya * mixed * dgelu_u).astype(BF16)
        dmix = dya * u
        dmb = dmix.astype(BF16)
        for gi in range(NH):
            cs = slice(gi * HD, (gi + 1) * HD)
            dv[:, cs] = _dot(wst_ref[gi], dmb[:, cs])
            dws_ref[gi] += _dot_nt(dmb[:, cs], vb[:, cs])
            dbs_ref[gi] += jnp.broadcast_to(jnp.sum(dmix[:, cs], axis=1, keepdims=True), (CHUNK, HD))
        dvv = dv[...]
        dlg_ref[...] += jnp.sum(dvv * vh, axis=0, keepdims=True)
        dlb_ref[...] += jnp.sum(dvv, axis=0, keepdims=True)
        dvh = dvv * lg_ref[...]
        dgv = rs * (dvh - jnp.mean(dvh, axis=-1, keepdims=True) - vh * jnp.mean(dvh * vh, axis=-1, keepdims=True))
        dz_ref[1] = (dgv * dgelu_v).astype(BF16)

    vec = pl.BlockSpec((1, D), lambda i: (0, 0))
    mat = pl.BlockSpec((NH, CHUNK, CHUNK), lambda i: (0, 0, 0))
    return pl.pallas_call(
        body, name="gmlp_bwd", grid=(S // CHUNK,),
        in_specs=[pl.BlockSpec((CHUNK, D), lambda i: (i, 0)), pl.BlockSpec((3, CHUNK, D), lambda i: (0, i, 0)), mat, mat,
                  pl.BlockSpec((CHUNK, D), lambda i: (0, 0)), vec, vec] + [ANY] * len(after),
        out_specs=[pl.BlockSpec((3, CHUNK, D), lambda i: (0, i, 0)), mat, mat, vec, vec],
        out_shape=[jax.ShapeDtypeStruct((6, S, D), BF16), jax.ShapeDtypeStruct((NH, CHUNK, CHUNK), F32),
                   jax.ShapeDtypeStruct((NH, CHUNK, HD), F32), jax.ShapeDtypeStruct((1, D), F32), jax.ShapeDtypeStruct((1, D), F32)],
        scratch_shapes=[pltpu.VMEM((CHUNK, D), F32), pltpu.VMEM((CHUNK, D), F32)],
        compiler_params=_cp(("arbitrary",)),
    )(dm, z6, ws_b, wst_b, bs_b, lg, lb, *after)


def _lru_bwd(dz6, dm, z6, h0, h1, cw, cb, wr, br, wi, bi, lam):
    S = z6.shape[1]
    nt = S // RT

    def body(dz_in, dm_ref, z_ref, h0_ref, h1_ref, cw_ref, cb_ref, wr_ref, br_ref, wi_ref, bi_ref, lam_ref,
             dz_ref, dcw_ref, dcb_ref, dwr_ref, dbr_ref, dwi_ref, dbi_ref, dlam_ref, zxp, xc_s, dhs_s, dxcp, dxc1):
        del dz_in
        lam = lam_ref[...]
        sp = _softplus_neg(lam)
        row = _row_iota()
        _fill_padded(zxp, z_ref.at[0], S)
        _conv_fwd_all(zxp, xc_s, cw_ref, cb_ref, S)
        zeros = jnp.zeros((PADR, HD), F32)
        dxcp[0:PADR, :] = zeros
        dxcp[PADR + S:2 * PADR + S, :] = zeros
        dwr_ref[...] = jnp.zeros_like(dwr_ref)
        dwi_ref[...] = jnp.zeros_like(dwi_ref)

        def pre(i, c):
            rows = pl.ds(pl.multiple_of(i * RT, RT), RT)
            hs = h0_ref[rows, :] + h1_ref[rows, :]
            dmv = dm_ref[rows, :].astype(F32)
            sb = _sigmoid(z_ref[2, rows, :].astype(F32))
            gg, dgg = _gelu_and_grad(z_ref[1, rows, :].astype(F32))
            dz_ref[2, rows, :] = (dmv * hs * gg * sb * (1.0 - sb)).astype(BF16)
            dyb = dmv * sb
            dz_ref[1, rows, :] = (dyb * hs * dgg).astype(BF16)
            dhs_s[rows, :] = dyb * gg
            return c
        lax.fori_loop(0, nt, pre, 0)

        def gate_bwd(d, gates, lamv, da, xc):
            r, gi, a, mult = gates
            dmult = lamv * gi * xc
            dgi = lamv * mult * xc
            dlog = (da - dmult * a / mult) * a
            dpr = (dlog * (-LRU_C) * sp[d:d + 1, :]) * r * (1.0 - r)
            dpi = dgi * gi * (1.0 - gi)
            xb, dprb, dpib = xc.astype(BF16), dpr.astype(BF16), dpi.astype(BF16)
            dwr_ref[d] += _dot_tn(xb, dprb)
            dwi_ref[d] += _dot_tn(xb, dpib)
            dxc = lamv * mult * gi + _dot_nt(dprb, wr_ref[d]) + _dot_nt(dpib, wi_ref[d])
            return dxc, (jnp.sum(dlog * r, axis=0, keepdims=True) * (-LRU_C), jnp.sum(dpr, axis=0, keepdims=True),
                         jnp.sum(dpi, axis=0, keepdims=True))

        def down(t0, q_next):
            rows = pl.ds(t0, RT)
            xc, h0t, dhs = xc_s[rows, :], h0_ref[rows, :], dhs_s[rows, :]
            gates = _lru_gates(xc, 0, wr_ref, br_ref, wi_ref, bi_ref, sp)
            a = gates[2]
            q, q_first = _scan_down(a, a * dhs, q_next)
            lamv = dhs + jnp.where(row == RT - 1, q_next, pltpu.roll(q, RT - 1, 0))
            tp = pl.multiple_of(jnp.maximum(t0 - PADR, 0), PADR)
            prev = jnp.where(t0 > 0, h0_ref[pl.ds(tp, PADR), :][PADR - 1:PADR, :], 0.0)
            hprev = jnp.where(row == 0, prev, pltpu.roll(h0t, 1, 0))
            dxc, sums = gate_bwd(0, gates, lamv, lamv * hprev, xc)
            dxcp[pl.ds(t0 + PADR, RT), :] = dxc
            return q_first, sums

        def up(t0, q_prev):
            rows = pl.ds(t0, RT)
            xc, h1t, dhs = xc_s[rows, :], h1_ref[rows, :], dhs_s[rows, :]
            gates = _lru_gates(xc, 1, wr_ref, br_ref, wi_ref, bi_ref, sp)
            a = gates[2]
            q, q_last = _scan_up(a, a * dhs, q_prev)
            lamv = dhs + jnp.where(row == 0, q_prev, pltpu.roll(q, 1, 0))
            tn = pl.multiple_of(jnp.minimum(t0 + RT, S - PADR), PADR)
            nxt = jnp.where(t0 + RT < S, h1_ref[pl.ds(tn, PADR), :][0:1, :], 0.0)
            hnext = jnp.where(row == RT - 1, nxt, pltpu.roll(h1t, RT - 1, 0))
            dxc, sums = gate_bwd(1, gates, lamv, lamv * hnext, xc)
            dxc1[rows, :] = dxc
            return q_last, sums

        def chains(i, carry):
            qn, qp, acc = carry
            for u in range(UNROLL_BWD):
                j = i * UNROLL_BWD + u
                qn, s0 = down(pl.multiple_of((nt - 1 - j) * RT, RT), qn)
                qp, s1 = up(pl.multiple_of(j * RT, RT), qp)
                acc = tuple(x + y for x, y in zip(acc, s0 + s1))
            return qn, qp, acc

        z1 = jnp.zeros((1, HD), F32)
        _, _, (s_sp0, s_br0, s_bi0, s_sp1, s_br1, s_bi1) = lax.fori_loop(0, nt // UNROLL_BWD, chains, (z1, z1, (z1,) * 6))

        def add_dxc(i, c):
            t0 = pl.multiple_of(i * RT, RT)
            dxcp[pl.ds(t0 + PADR, RT), :] += dxc1[pl.ds(t0, RT), :]
            return c
        lax.fori_loop(0, nt, add_dxc, 0)

        dsp = jnp.concatenate([s_sp0, s_sp1], axis=0)
        dlam_ref[...] = -dsp * _sigmoid(-lam)
        dbr_ref[...] = jnp.concatenate([s_br0, s_br1], axis=0)
        dbi_ref[...] = jnp.concatenate([s_bi0, s_bi1], axis=0)

        def conv_bwd(i, carry):
            c0, c1, c2, c3, cb_ = carry
            t0 = pl.multiple_of(i * RT, RT)
            dwin = dxcp[pl.ds(t0, RT + 2 * PADR), :]
            d0 = _shifted(dwin, 0)
            dz_ref[0, pl.ds(t0, RT), :] = (_shifted(dwin, 1) * cw_ref[0:1, :] + d0 * cw_ref[1:2, :]
                                           + _shifted(dwin, -1) * cw_ref[2:3, :] + _shifted(dwin, -2) * cw_ref[3:4, :]).astype(BF16)
            xm1, x0, xp1, xp2 = _conv_taps(zxp[pl.ds(t0, RT + 2 * PADR), :])
            sm = lambda v: jnp.sum(v, axis=0, keepdims=True)
            return c0 + sm(d0 * xm1), c1 + sm(d0 * x0), c2 + sm(d0 * xp1), c3 + sm(d0 * xp2), cb_ + sm(d0)

        c0, c1, c2, c3, cb_ = lax.fori_loop(0, nt, conv_bwd, (z1, z1, z1, z1, z1))
        dcw_ref[...] = jnp.concatenate([c0, c1, c2, c3], axis=0)
        dcb_ref[...] = cb_

    col = pl.BlockSpec((S, HD), lambda h: (0, h))
    head = lambda h: (0, h)
    wspec = pl.BlockSpec((2, None, HD, HD), lambda h: (0, h, 0, 0))
    return pl.pallas_call(
        body, name="lru_bwd", grid=(NH,),
        in_specs=[pl.BlockSpec(memory_space=pl.ANY), col, pl.BlockSpec((3, S, HD), lambda h: (1, 0, h)), col, col] + _lru_specs(S),
        out_specs=[pl.BlockSpec((3, S, HD), lambda h: (1, 0, h)), pl.BlockSpec((4, HD), head), pl.BlockSpec((1, HD), head),
                   wspec, pl.BlockSpec((2, HD), head), wspec, pl.BlockSpec((2, HD), head), pl.BlockSpec((2, HD), head)],
        out_shape=[jax.ShapeDtypeStruct((6, S, D), BF16), jax.ShapeDtypeStruct((4, D), F32), jax.ShapeDtypeStruct((1, D), F32),
                   jax.ShapeDtypeStruct((2, NH, HD, HD), F32), jax.ShapeDtypeStruct((2, D), F32),
                   jax.ShapeDtypeStruct((2, NH, HD, HD), F32), jax.ShapeDtypeStruct((2, D), F32), jax.ShapeDtypeStruct((2, D), F32)],
        scratch_shapes=[pltpu.VMEM((S + 2 * PADR, HD), F32), pltpu.VMEM((S, HD), F32), pltpu.VMEM((S, HD), F32),
                        pltpu.VMEM((S + 2 * PADR, HD), F32), pltpu.VMEM((S, HD), F32)],
        input_output_aliases={0: 0},
        compiler_params=_cp(("parallel",)),
    )(dz6, dm, z6, h0, h1, cw, cb, wr, br, wi, bi, lam)


LAYER_SMALL = ("norm1_g", "gmlp_ln_g", "gmlp_ln_b", "gmlp_w_s", "gmlp_b_s", "conv_w", "conv_b",
               "lru_w_r", "lru_b_r", "lru_w_i", "lru_b_i", "lru_lambda", "norm2_g")


def _forward_layer(l, x, p, wb, after=(), rest=None):
    g1, g2 = p["norm1_g"][l][None], p["norm2_g"][l][None]
    ws_b = p["gmlp_w_s"][l].astype(BF16)
    tm = dict(ws_b=ws_b, wst_b=jnp.swapaxes(ws_b, 1, 2), bs_b=jnp.repeat(p["gmlp_b_s"][l].T, HD, axis=1),
              lg=p["gmlp_ln_g"][l][None], lb=p["gmlp_ln_b"][l][None])
    lru = (p["conv_w"][l], p["conv_b"][l][None], p["lru_w_r"][l].astype(BF16), p["lru_b_r"][l],
           p["lru_w_i"][l].astype(BF16), p["lru_b_i"][l], p["lru_lambda"][l])
    z6, hn1 = _mm_in(x, g1, wb["w_in"], l, after)
    ya = _gmlp_fwd(z6, tm["ws_b"], tm["bs_b"], tm["lg"], tm["lb"])
    merged, h0, h1 = _lru_fwd(z6, ya, *lru)
    if rest is not None:
        wb = dict(wb, **rest(merged))
    x1 = _mm_res(merged, wb["w_out"], x, l, "mm_out")
    gu, ff, hn2 = _mm_ffn_in(x1, g2, wb["w_ffn_in"], l)
    x2 = _mm_res(ff, wb["w_ffn_out"], x1, l, "mm_ffn_out")
    return x2, dict(x=x, z6=z6, h0=h0, h1=h1, merged=merged, x1=x1, gu=gu, ff=ff, g1=g1, g2=g2, tm=tm, lru=lru,
                    hn1=hn1, hn2=hn2, wb=wb)


def _backward_layer(l, dx, s, after=(), midway=None):
    S = dx.shape[0]
    tm, wb = s["tm"], s["wb"]
    g2 = s["g2"]
    dgu = _bwd_ffn_out(dx, wb["w_ffn_out"], s["gu"], l, after)
    dwfo = _mm_tn(s["ff"], dx, DFF_SH, f"dw_ffn_out_{l}")
    tmb = min(TM_BIG, S)
    dx1, dg2 = _mm_nt_rms_bwd(
        dgu, pl.BlockSpec((None, tmb, DFF_SH), lambda i, k: (k, i, 0)),
        wb["w_ffn_in"], pl.BlockSpec((None, D, DFF_SH), lambda i, k: (k, 0, 0)),
        4, tmb, s["x1"], g2, dx, f"bwd_ffn_in_{l}")
    dwfi = _dw_ffn_in(s["hn2"], dgu, l)
    dmg, dwo = _bwd_out(dx1, wb["w_out"], s["merged"], l)
    mid = () if midway is None else tuple(midway([dwo, dwfi, dwfo]))
    dz6, dws, dbs, dlg, dlb = _gmlp_bwd(dmg, s["z6"], tm["ws_b"], tm["wst_b"], tm["bs_b"], tm["lg"], tm["lb"], mid)
    dz6, dcw, dcb, dwr, dbr, dwi, dbi, dlam = _lru_bwd(dz6, dmg, s["z6"], s["h0"], s["h1"], *s["lru"])

    def w_map(i, k):
        sh, tl = _in_tile(k)
        return (sh, 0, tl)

    dx0, dg1 = _mm_nt_rms_bwd(
        dz6, pl.BlockSpec((None, tmb, 512), lambda i, k: (k // 2, i, k % 2)),
        wb["w_in"], pl.BlockSpec((None, D, 512), w_map),
        N_IN_T, tmb, s["x"], s["g1"], dx1, f"bwd_in_{l}")
    dwin = _dw_in(s["hn1"], dz6, l)
    small = dict(norm1_g=dg1[0], gmlp_ln_g=dlg[0], gmlp_ln_b=dlb[0], gmlp_w_s=dws, gmlp_b_s=dbs[:, :, 0], conv_w=dcw, conv_b=dcb[0],
                 lru_w_r=dwr, lru_b_r=dbr, lru_w_i=dwi, lru_b_i=dbi, lru_lambda=dlam, norm2_g=dg2[0])
    return dx0, [dwin, dwo, dwfi, dwfo], small


def _local_step(x, tgt, p, wbs):
    saved = []
    for l in range(2):
        x, s = _forward_layer(l, x, p, wbs[l])
        saved.append(s)
    dx, loss_v, dfg = _loss_head(x, tgt, p["final_g"][None])
    big, smalls = [None, None], [None, None]
    for l in (1, 0):
        dx, big[l], smalls[l] = _backward_layer(l, dx, saved[l])
    small = {k: jnp.stack([smalls[0][k], smalls[1][k]]) for k in LAYER_SMALL}
    small["final_g"] = dfg[0]
    return loss_v, dx, big, small


def _place():
    x, y, c = lax.axis_index("x"), lax.axis_index("y"), lax.axis_index("c")
    return x, y, c, 2 * x + y


def _chip_at(x, y, d):
    px = 1 - x if d & 2 else x
    py = 1 - y if d & 1 else y
    return px, py, 2 * px + py


HBM = pl.BlockSpec(memory_space=pltpu.HBM)
SEM = pl.BlockSpec(memory_space=pltpu.SEMAPHORE)
DATAFLOW = pltpu.SideEffectType.DATAFLOW_SIDE_EFFECTING


def _in_hbm(a):
    return pltpu.with_memory_space_constraint(a, pltpu.HBM)


def _cast_into(wf, l, chip_arr, name):
    _, rows, cols = wf.shape
    rh = rows // 2

    def body(ch_ref, w_ref, o_ref):
        o_ref[...] = w_ref[...].astype(BF16)

    return pl.pallas_call(
        body, name=name, out_shape=jax.ShapeDtypeStruct((4, 2, rh, cols), BF16),
        grid_spec=pltpu.PrefetchScalarGridSpec(
            num_scalar_prefetch=1, grid=(2,),
            in_specs=[pl.BlockSpec((None, None, rh, cols), lambda h, ch: (l, h, 0, 0))],
            out_specs=pl.BlockSpec((None, None, rh, cols), lambda h, ch: (ch[0], h, 0, 0))),
        compiler_params=_cp(("parallel",)),
    )(chip_arr, wf.reshape(2, 2, rh, cols))


def _half_block(ref, chip, half, to, send_sem, recv_sem):
    blk = ref.at[chip, half]
    return pltpu.make_async_remote_copy(src_ref=blk, dst_ref=blk, send_sem=send_sem, recv_sem=recv_sem,
                                        device_id=to, device_id_type=MESH)


def _gather_weights(bufs, tiny):
    nt = len(bufs)
    n_ici = nt * 3

    def body(*refs):
        tiny_ref = refs[nt]
        o_refs, tiny_o = refs[nt + 1:2 * nt + 1], refs[2 * nt + 1]
        send, recv, fsend, frecv, tsend, trecv, lsem = refs[2 * nt + 2:]
        x, y, c, chip = _place()
        local = pltpu.make_async_copy(tiny_ref, tiny_o.at[chip], lsem)
        local.start()

        def tin(d, origin_chip, to):
            return pltpu.make_async_remote_copy(
                src_ref=tiny_ref, dst_ref=tiny_o.at[origin_chip], send_sem=tsend.at[d - 1], recv_sem=trecv.at[d - 1],
                device_id=to, device_id_type=MESH)

        sends = []
        for t in range(nt):
            for d in (1, 2, 3):
                px, py, _ = _chip_at(x, y, d)
                sends.append(_half_block(o_refs[t], chip, c, (px, py, c), send.at[3 * t + d - 1], recv.at[3 * t + d - 1]))
        for d in (1, 2, 3):
            px, py, _ = _chip_at(x, y, d)
            sends.append(tin(d, chip, (px, py, c)))
        for cp in sends:
            cp.start()
        passed = []
        for t in range(nt):
            for d in (1, 2, 3):
                k = 3 * t + d - 1
                _, _, pchip = _chip_at(x, y, d)
                _half_block(o_refs[t], pchip, c, (x, y, c), send.at[k], recv.at[k]).wait_recv()
                f = _half_block(o_refs[t], pchip, c, (x, y, 1 - c), fsend.at[k], frecv.at[k])
                f.start()
                passed.append(f)
        for t in range(nt):
            for d in (1, 2, 3):
                k = 3 * t + d - 1
                _, _, pchip = _chip_at(x, y, d)
                _half_block(o_refs[t], pchip, 1 - c, (x, y, 1 - c), fsend.at[k], frecv.at[k]).wait_recv()
        for d in (1, 2, 3):
            _, _, pchip = _chip_at(x, y, d)
            tin(d, pchip, (x, y, c)).wait_recv()
        for cp in sends + passed:
            cp.wait_send()
        local.wait()

    out_shape = [jax.ShapeDtypeStruct(b.shape, b.dtype) for b in bufs]
    out_shape.append(jax.ShapeDtypeStruct((4,) + tiny.shape, tiny.dtype))
    outs = pl.pallas_call(
        body, name="gather_weights_0", out_shape=out_shape,
        in_specs=[ANY] * (nt + 1), out_specs=[ANY] * (nt + 1),
        scratch_shapes=[pltpu.SemaphoreType.DMA((n_ici,)), pltpu.SemaphoreType.DMA((n_ici,)),
                        pltpu.SemaphoreType.DMA((n_ici,)), pltpu.SemaphoreType.DMA((n_ici,)),
                        pltpu.SemaphoreType.DMA((3,)), pltpu.SemaphoreType.DMA((3,)), pltpu.SemaphoreType.DMA],
        input_output_aliases={t: t for t in range(nt)},
        compiler_params=_cp(has_side_effects=True),
    )(*bufs, tiny)
    return outs[:nt], outs[nt]


def _gather_start(bufs, tag, after=()):
    nt, na = len(bufs), len(after)

    def body(*refs):
        b_refs = refs[:nt]
        send, recv = refs[nt + na], refs[nt + na + 1]
        token = refs[2 * nt + na + 2]
        x, y, c, chip = _place()
        for t in range(nt):
            for d in (1, 2, 3):
                px, py, _ = _chip_at(x, y, d)
                _half_block(b_refs[t], chip, c, (px, py, c), send.at[3 * t + d - 1], recv.at[3 * t + d - 1]).start()
        token[...] = jnp.zeros_like(token)

    outs = pl.pallas_call(
        body, name=f"gather_start_{tag}",
        out_shape=(pltpu.SemaphoreType.DMA((3 * nt,)), pltpu.SemaphoreType.DMA((3 * nt,)),
                   *[pltpu.HBM(b.shape, b.dtype) for b in bufs], jax.ShapeDtypeStruct((8, 128), F32)),
        in_specs=[HBM] * nt + [ANY] * na, out_specs=(SEM, SEM, *[HBM] * nt, pl.BlockSpec(memory_space=pltpu.VMEM)),
        input_output_aliases={t: 2 + t for t in range(nt)},
        compiler_params=pltpu.CompilerParams(has_side_effects=DATAFLOW),
    )(*[_in_hbm(b) for b in bufs], *after)
    return outs[0], outs[1], list(outs[2:2 + nt]), outs[2 + nt]


def _gather_wait(send, recv, bufs, after, tag):
    nt = len(bufs)

    def body(*refs):
        b_refs = refs[:nt]
        send_ref, recv_ref = refs[nt], refs[nt + 1]
        x, y, c, chip = _place()
        for t in range(nt):
            for d in (1, 2, 3):
                k = 3 * t + d - 1
                px, py, pchip = _chip_at(x, y, d)
                _half_block(b_refs[t], chip, c, (px, py, c), send_ref.at[k], recv_ref.at[k]).wait_send()
                _half_block(b_refs[t], pchip, c, (px, py, c), send_ref.at[k], recv_ref.at[k]).wait_recv()

    outs = pl.pallas_call(
        body, name=f"gather_wait_{tag}", out_shape=[pltpu.HBM(b.shape, b.dtype) for b in bufs],
        in_specs=[HBM] * nt + [SEM, SEM, ANY], out_specs=[HBM] * nt,
        input_output_aliases={t: t for t in range(nt)},
        compiler_params=pltpu.CompilerParams(has_side_effects=DATAFLOW),
    )(*bufs, send, recv, after)
    return list(outs)


def _gather_pass_on(bufs, tag):
    nt = len(bufs)

    def body(*refs):
        o_refs = refs[nt:2 * nt]
        fsend, frecv = refs[2 * nt:]
        x, y, c, _ = _place()
        cps = []
        for t in range(nt):
            for d in (1, 2, 3):
                k = 3 * t + d - 1
                _, _, pchip = _chip_at(x, y, d)
                cps.append(_half_block(o_refs[t], pchip, c, (x, y, 1 - c), fsend.at[k], frecv.at[k]))
        for cp in cps:
            cp.start()
        for t in range(nt):
            for d in (1, 2, 3):
                k = 3 * t + d - 1
                _, _, pchip = _chip_at(x, y, d)
                _half_block(o_refs[t], pchip, 1 - c, (x, y, 1 - c), fsend.at[k], frecv.at[k]).wait_recv()
        for cp in cps:
            cp.wait_send()

    return pl.pallas_call(
        body, name=f"gather_pass_on_{tag}", out_shape=[jax.ShapeDtypeStruct(b.shape, b.dtype) for b in bufs],
        in_specs=[ANY] * nt, out_specs=[ANY] * nt,
        scratch_shapes=[pltpu.SemaphoreType.DMA((3 * nt,)), pltpu.SemaphoreType.DMA((3 * nt,))],
        input_output_aliases={t: t for t in range(nt)},
        compiler_params=_cp(has_side_effects=True),
    )(*bufs)


def _to_sibling_halves(gs, l):
    nt = len(gs)

    def body(*refs):
        g_refs, o_refs = refs[:nt], refs[nt:2 * nt]
        send, recv = refs[2 * nt:]
        x, y, c, _ = _place()
        cps = [pltpu.make_async_remote_copy(
            src_ref=g_refs[t].at[k, 1 - c], dst_ref=o_refs[t].at[k], send_sem=send.at[4 * t + k], recv_sem=recv.at[4 * t + k],
            device_id=(x, y, 1 - c), device_id_type=MESH) for t in range(nt) for k in range(4)]
        for cp in cps:
            cp.start()
        for cp in cps:
            cp.wait()

    return pl.pallas_call(
        body, name=f"grads_to_sibling_{l}", out_shape=[jax.ShapeDtypeStruct((4,) + g.shape[2:], g.dtype) for g in gs],
        in_specs=[ANY] * nt, out_specs=[ANY] * nt,
        scratch_shapes=[pltpu.SemaphoreType.DMA((4 * nt,)), pltpu.SemaphoreType.DMA((4 * nt,))],
        compiler_params=_cp(has_side_effects=True),
    )(*gs)


def _chip_copy(c_ref, land_ref, x, y, c, d, send_sem, recv_sem):
    px, py, pchip = _chip_at(x, y, d)
    return pltpu.make_async_remote_copy(src_ref=c_ref.at[pchip], dst_ref=land_ref.at[d - 1], send_sem=send_sem, recv_sem=recv_sem,
                                        device_id=(px, py, c), device_id_type=MESH)


def _to_chips_start(cs, l):
    nt = len(cs)
    lands = [lax.empty((3,) + a.shape[1:], a.dtype) for a in cs]

    def body(*refs):
        c_refs, land_refs = refs[:nt], refs[nt:2 * nt]
        send, recv = refs[2 * nt], refs[2 * nt + 1]
        token = refs[4 * nt + 2]
        x, y, c, _ = _place()
        for t in range(nt):
            for d in (1, 2, 3):
                _chip_copy(c_refs[t], land_refs[t], x, y, c, d, send.at[3 * t + d - 1], recv.at[3 * t + d - 1]).start()
        token[...] = jnp.zeros_like(token)

    outs = pl.pallas_call(
        body, name=f"grads_to_chips_start_{l}",
        out_shape=(pltpu.SemaphoreType.DMA((3 * nt,)), pltpu.SemaphoreType.DMA((3 * nt,)),
                   *[pltpu.HBM(a.shape, a.dtype) for a in cs], *[pltpu.HBM(a.shape, a.dtype) for a in lands],
                   jax.ShapeDtypeStruct((8, 128), F32)),
        in_specs=[HBM] * (2 * nt), out_specs=(SEM, SEM, *[HBM] * (2 * nt), pl.BlockSpec(memory_space=pltpu.VMEM)),
        input_output_aliases={i: 2 + i for i in range(2 * nt)},
        compiler_params=pltpu.CompilerParams(has_side_effects=DATAFLOW),
    )(*[_in_hbm(a) for a in cs], *[_in_hbm(a) for a in lands])
    return outs[0], outs[1], list(outs[2:2 + nt]), list(outs[2 + nt:2 + 2 * nt]), outs[2 + 2 * nt]


def _to_chips_wait(send, recv, cs, lands, after, l):
    nt = len(cs)

    def body(*refs):
        c_refs, land_refs = refs[:nt], refs[nt:2 * nt]
        send_ref, recv_ref = refs[2 * nt], refs[2 * nt + 1]
        x, y, c, _ = _place()
        for t in range(nt):
            for d in (1, 2, 3):
                cp = _chip_copy(c_refs[t], land_refs[t], x, y, c, d, send_ref.at[3 * t + d - 1], recv_ref.at[3 * t + d - 1])
                cp.wait_send()
                cp.wait_recv()

    outs = pl.pallas_call(
        body, name=f"grads_to_chips_wait_{l}", out_shape=[pltpu.HBM(a.shape, a.dtype) for a in cs + lands],
        in_specs=[HBM] * (2 * nt) + [SEM, SEM, ANY], out_specs=[HBM] * (2 * nt),
        input_output_aliases={i: i for i in range(2 * nt)},
        compiler_params=pltpu.CompilerParams(has_side_effects=DATAFLOW),
    )(*cs, *lands, send, recv, after)
    return list(outs[:nt]), list(outs[nt:])


def _join_halves(fs, l):
    nt = len(fs)

    def body(*refs):
        o_refs = refs[nt:2 * nt]
        send, recv = refs[2 * nt:]
        x, y, c, _ = _place()
        cps = [pltpu.make_async_remote_copy(
            src_ref=o_refs[t].at[c], dst_ref=o_refs[t].at[c], send_sem=send.at[t], recv_sem=recv.at[t],
            device_id=(x, y, 1 - c), device_id_type=MESH) for t in range(nt)]
        for cp in cps:
            cp.start()
        for cp in cps:
            cp.wait()

    return pl.pallas_call(
        body, name=f"grads_join_{l}", out_shape=[jax.ShapeDtypeStruct(a.shape, a.dtype) for a in fs],
        in_specs=[ANY] * nt, out_specs=[ANY] * nt,
        scratch_shapes=[pltpu.SemaphoreType.DMA((nt,)), pltpu.SemaphoreType.DMA((nt,))],
        input_output_aliases={t: t for t in range(nt)},
        compiler_params=_cp(has_side_effects=True),
    )(*fs)


def _add_half(g, r, c_arr, name):
    _, _, rh, cols = g.shape

    def body(c_ref, g_ref, r_ref, o_ref):
        o_ref[...] = (g_ref[...] + r_ref[...]).astype(BF16)

    blk = pl.BlockSpec((None, rh, cols), lambda k, cr: (k, 0, 0))
    return pl.pallas_call(
        body, name=name, out_shape=jax.ShapeDtypeStruct((4, rh, cols), BF16),
        grid_spec=pltpu.PrefetchScalarGridSpec(
            num_scalar_prefetch=1, grid=(4,),
            in_specs=[pl.BlockSpec((None, None, rh, cols), lambda k, cr: (k, cr[0], 0, 0)), blk], out_specs=blk),
        compiler_params=_cp(("parallel",)),
    )(c_arr, g, r)


def _sum_chips(cs, r3, place_arr, name):
    _, rh, cols = cs.shape
    rb = rh // 2

    def body(pl_ref, a_ref, r0_ref, r1_ref, r2_ref, o_ref):
        up = lambda ref: ref[...].astype(F32)
        o_ref[...] = ((up(a_ref) + up(r0_ref)) + up(r1_ref)) + up(r2_ref)

    def slot(d):
        return pl.BlockSpec((None, rb, cols), lambda i, pa: (d, i, 0))

    return pl.pallas_call(
        body, name=name, out_shape=jax.ShapeDtypeStruct((2, rh, cols), F32),
        grid_spec=pltpu.PrefetchScalarGridSpec(
            num_scalar_prefetch=1, grid=(2,),
            in_specs=[pl.BlockSpec((None, rb, cols), lambda i, pa: (pa[0], i, 0)), slot(0), slot(1), slot(2)],
            out_specs=pl.BlockSpec((None, rb, cols), lambda i, pa: (pa[1], i, 0))),
        compiler_params=_cp(("parallel",)),
    )(place_arr, cs, r3, r3, r3)


def _allreduce_small(pack):
    rows = pack.shape[0]
    hr = rows // 2

    def body(p_ref, o_ref, sib, slots, s1, r1, s2, r2, s3, r3):
        x, y, c, chip = _place()
        sibling = (x, y, 1 - c)
        ex = pltpu.make_async_remote_copy(src_ref=p_ref, dst_ref=sib, send_sem=s1, recv_sem=r1,
                                          device_id=sibling, device_id_type=MESH)
        ex.start()
        ex.wait()
        half = pl.ds(pl.multiple_of(c * hr, 8), hr)
        slots[0] = p_ref[half, :] + sib[half, :]
        cps = []
        for d in (1, 2, 3):
            px, py, _ = _chip_at(x, y, d)
            cps.append(pltpu.make_async_remote_copy(
                src_ref=slots.at[0], dst_ref=slots.at[d], send_sem=s2.at[d - 1], recv_sem=r2.at[d - 1],
                device_id=(px, py, c), device_id_type=MESH))
        for cp in cps:
            cp.start()
        for cp in cps:
            cp.wait()
        tot = slots[chip]
        for k in (1, 2, 3):
            tot = tot + slots[jnp.bitwise_xor(chip, k)]
        o_ref[half, :] = tot
        back = pltpu.make_async_remote_copy(src_ref=o_ref.at[half, :], dst_ref=o_ref.at[half, :], send_sem=s3, recv_sem=r3,
                                            device_id=sibling, device_id_type=MESH)
        back.start()
        back.wait()

    vm = pl.BlockSpec(memory_space=pltpu.VMEM)
    return pl.pallas_call(
        body, name="allreduce_small", out_shape=jax.ShapeDtypeStruct((rows, 128), F32),
        in_specs=[vm], out_specs=vm,
        scratch_shapes=[pltpu.VMEM((rows, 128), F32), pltpu.VMEM((4, hr, 128), F32),
                        pltpu.SemaphoreType.DMA, pltpu.SemaphoreType.DMA, pltpu.SemaphoreType.DMA((3,)), pltpu.SemaphoreType.DMA((3,)),
                        pltpu.SemaphoreType.DMA, pltpu.SemaphoreType.DMA],
        compiler_params=_cp(has_side_effects=True),
    )(pack)


def _adam_math(gv, wv, mv, vv):
    m2 = ADAM_B1 * mv + (1.0 - ADAM_B1) * gv
    v2 = ADAM_B2 * vv + (1.0 - ADAM_B2) * (gv * gv)
    m_hat = m2 / (1.0 - ADAM_B1 ** ADAM_STEP)
    v_hat = v2 / (1.0 - ADAM_B2 ** ADAM_STEP)
    return -ADAM_LR * (m_hat / (jnp.sqrt(v_hat) + ADAM_EPS) + ADAM_WD * wv), m2, v2


def _adam(g, w, m, v, name):
    rows, cols = g.shape
    rb = rows // 4

    def body(g_ref, w_ref, m_ref, v_ref, d_ref, m2_ref, v2_ref):
        d_ref[...], m2_ref[...], v2_ref[...] = _adam_math(g_ref[...], w_ref[...], m_ref[...], v_ref[...])

    blk = pl.BlockSpec((rb, cols), lambda i: (i, 0))
    shp = jax.ShapeDtypeStruct((rows, cols), F32)
    return pl.pallas_call(
        body, name=name, grid=(4,), in_specs=[blk] * 4, out_specs=[blk] * 3, out_shape=[shp] * 3,
        compiler_params=_cp(("parallel",)),
    )(g, w, m, v)


def _adam_layer(g, w, m, v, l, prev, name):
    rows, cols = g.shape
    rb = rows // 4

    def body(g_ref, w_ref, m_ref, v_ref, *rest):
        go_ref, d_ref, m2_ref, v2_ref = rest[-4:]
        gv = g_ref[...]
        go_ref[...] = gv
        d_ref[...], m2_ref[...], v2_ref[...] = _adam_math(gv, w_ref[...], m_ref[...], v_ref[...])

    lay = pl.BlockSpec((None, rb, cols), lambda i: (l, i, 0))
    shp = jax.ShapeDtypeStruct((2, rows, cols), F32)
    prev = () if prev is None else tuple(prev)
    return pl.pallas_call(
        body, name=name, grid=(4,), in_specs=[pl.BlockSpec((rb, cols), lambda i: (i, 0)), lay, lay, lay] + [ANY] * len(prev),
        out_specs=[lay] * 4, out_shape=[shp] * 4,
        input_output_aliases={4 + j: j for j in range(len(prev))},
        compiler_params=_cp(("parallel",)),
    )(g, w, m, v, *prev)


def _rows128(a):
    return a.reshape(-1, 128)


def _pack(arrs, mult):
    parts = [_rows128(a) for a in arrs]
    rows = sum(q.shape[0] for q in parts)
    pad = -rows % mult
    if pad:
        parts.append(jnp.zeros((pad, 128), F32))
    return jnp.concatenate(parts, axis=0)


def _unpack(pack, shapes):
    out, o = [], 0
    for s in shapes:
        n = 1
        for e in s:
            n *= e
        out.append(pack[o:o + n // 128].reshape(s))
        o += n // 128
    return out


WEIGHTS = ['norm1_g', 'w_in', 'gmlp_ln_g', 'gmlp_ln_b', 'gmlp_w_s', 'gmlp_b_s', 'conv_w', 'conv_b', 'lru_w_r', 'lru_b_r', 'lru_w_i',
           'lru_b_i', 'lru_lambda', 'w_out', 'norm2_g', 'w_ffn_in', 'w_ffn_out', 'final_g']
BIG = ['w_in', 'w_out', 'w_ffn_in', 'w_ffn_out']
SMALL = [n for n in WEIGHTS if n not in BIG]
CHIP_SHARDED_SMALL = ['conv_w', 'lru_b_r', 'lru_b_i', 'lru_lambda']


def kernel(x, norm1_g, w_in, gmlp_ln_g, gmlp_ln_b, gmlp_w_s, gmlp_b_s, conv_w, conv_b, lru_w_r, lru_b_r, lru_w_i, lru_b_i, lru_lambda, w_out, norm2_g, w_ffn_in, w_ffn_out, final_g, loss_target, m_norm1_g, m_w_in, m_gmlp_ln_g, m_gmlp_ln_b, m_gmlp_w_s, m_gmlp_b_s, m_conv_w, m_conv_b, m_lru_w_r, m_lru_b_r, m_lru_w_i, m_lru_b_i, m_lru_lambda, m_w_out, m_norm2_g, m_w_ffn_in, m_w_ffn_out, m_final_g, v_norm1_g, v_w_in, v_gmlp_ln_g, v_gmlp_ln_b, v_gmlp_w_s, v_gmlp_b_s, v_conv_w, v_conv_b, v_lru_w_r, v_lru_b_r, v_lru_w_i, v_lru_b_i, v_lru_lambda, v_w_out, v_norm2_g, v_w_ffn_in, v_w_ffn_out, v_final_g):
    a = dict(locals())
    w = {n: a[n] for n in WEIGHTS}
    mom = {n: a["m_" + n] for n in WEIGHTS}
    var = {n: a["v_" + n] for n in WEIGHTS}
    _, _, c, chip = _place()
    c_arr, chip_arr = jnp.reshape(c, (1,)).astype(jnp.int32), jnp.reshape(chip, (1,)).astype(jnp.int32)
    place_arr = jnp.stack([chip, c]).astype(jnp.int32)

    first, rest = BIG[:1], BIG[1:]

    def as_weights(names, full):
        wb = {n: f.reshape(4, 2 * f.shape[2], f.shape[3]) for n, f in zip(names, full)}
        if "w_out" in wb:
            wb["w_out"] = wb["w_out"].reshape(D, D)
            wb["w_ffn_out"] = wb["w_ffn_out"].reshape(DFF, D)
        return wb

    bufs = [{n: _cast_into(w[n], l, chip_arr, f"cast_{n}_{l}") for n in BIG} for l in range(2)]
    tiny = _pack([w[n] for n in CHIP_SHARDED_SMALL], 8)
    w_in0, tiny_full = _gather_weights([bufs[0]["w_in"]], tiny)
    fly0 = _gather_start([bufs[0][n] for n in rest], "0", after=(tiny_full,))
    fly1 = _gather_start([bufs[1][n] for n in BIG], "1", after=(fly0[3],))
    p = {n: w[n] for n in SMALL}
    parts = [_unpack(tiny_full[k], [w[n].shape for n in CHIP_SHARDED_SMALL]) for k in range(4)]
    for i, n in enumerate(CHIP_SHARDED_SMALL):
        p[n] = jnp.concatenate([parts[k][i] for k in range(4)], axis=-1)

    def landed(fly, names, after, tag):
        return as_weights(names, _gather_pass_on(_gather_wait(fly[0], fly[1], fly[2], after, tag), tag))

    xa, saved0 = _forward_layer(0, x[0], p, as_weights(first, w_in0), after=(fly0[3], fly1[3]),
                                rest=lambda merged: landed(fly0, rest, merged, "0"))
    xb, saved1 = _forward_layer(1, xa, p, landed(fly1, BIG, xa, "1"))
    dxb, loss_v, dfg = _loss_head(xb, loss_target[0], p["final_g"][None])
    loss = lax.psum(loss_v[0, 0], ("x", "y", "c"))

    out, flying = {}, {}

    def reduce_start(grads, names, l, tag):
        gs = [g.reshape(4, 2, -1, g.shape[-1]) for g in grads]
        from_sib = _to_sibling_halves(gs, tag)
        cs = [_add_half(g, r, c_arr, f"add_half_{n}_{l}") for n, g, r in zip(names, gs, from_sib)]
        flying[tag] = (names, l) + tuple(_to_chips_start(cs, tag))
        return (flying[tag][-1],)

    def reduce_finish(tag, after):
        names, l, send, recv, cs, lands, _ = flying[tag]
        cs, lands = _to_chips_wait(send, recv, cs, lands, after, tag)
        ts = [_sum_chips(cc, r3, place_arr, f"sum_chips_{n}_{l}") for n, cc, r3 in zip(names, cs, lands)]
        for n, j in zip(names, _join_halves(ts, tag)):
            out[n] = _adam_layer(j.reshape(w[n].shape[1:]), w[n], mom[n], var[n], l, out.get(n), f"adam_{n}_{l}")
        return out[names[-1]][0]

    def midway0(grads):
        reduce_finish("1a", grads[0])
        reduce_finish("1b", grads[0])
        return reduce_start(grads, rest, 0, "0a")

    dxa, big1, small1 = _backward_layer(1, dxb, saved1, midway=lambda grads: reduce_start(grads, rest, 1, "1a"))
    dx, big0, small0 = _backward_layer(0, dxa, saved0, after=reduce_start(big1[:1], first, 1, "1b"), midway=midway0)
    reduce_start(big0[:1], first, 0, "0b")
    reduce_finish("0b", reduce_finish("0a", dx))
    small = {k: jnp.stack([small0[k], small1[k]]) for k in LAYER_SMALL}
    small["final_g"] = dfg[0]

    full_shapes = [small[n].shape for n in SMALL]
    red = _unpack(_allreduce_small(_pack([small[n] for n in SMALL], 16)), full_shapes)
    g_small = []
    for n, g in zip(SMALL, red):
        if n in CHIP_SHARDED_SMALL:
            g = lax.dynamic_slice_in_dim(g, chip * w[n].shape[-1], w[n].shape[-1], axis=g.ndim - 1)
        g_small.append(g)
    shapes = [w[n].shape for n in SMALL]
    packs = [_pack(lst, 32) for lst in (g_small, [w[n] for n in SMALL], [mom[n] for n in SMALL], [var[n] for n in SMALL])]
    upd = [_unpack(u, shapes) for u in _adam(*packs, "adam_small")]
    for i, n in enumerate(SMALL):
        out[n] = [g_small[i], upd[0][i], upd[1][i], upd[2][i]]

    return (loss, dx[None]) + tuple(out[n][i] for i in range(4) for n in WEIGHTS)
```

```python
import functools

import jax
import jax.numpy as jnp
from jax import lax
from jax.experimental import pallas as pl
from jax.experimental.pallas import tpu as pltpu

F32 = jnp.float32
BF16 = jnp.bfloat16
MESH = pl.DeviceIdType.MESH

D = 1024
NH = 8
HD = 128
CHUNK = 128
N_IN_T = 12
DFF = 2816
DFF_SH = 1408
EPS = 1e-6
LRU_C = 8.0
ADAM_LR, ADAM_B1, ADAM_B2, ADAM_EPS, ADAM_WD, ADAM_STEP = 0.001, 0.9, 0.999, 1e-08, 0.01, 10

TM = 512
TM_BIG = 1024
RT = 128
PADR = 8
VMEM_LIMIT = 56 * 1024 * 1024


def _cp(sem=None, **kw):
    if sem is not None:
        kw["dimension_semantics"] = sem
    return pltpu.CompilerParams(vmem_limit_bytes=VMEM_LIMIT, **kw)


_GC = 0.7978845608028654


def _sigmoid(x):
    return 1.0 / (1.0 + jnp.exp(-x))


def _gelu(x):
    return 0.5 * x * (1.0 + jnp.tanh(_GC * (x + 0.044715 * x * x * x)))


def _gelu_and_grad(x):
    t = jnp.tanh(_GC * (x + 0.044715 * x * x * x))
    g = 0.5 * x * (1.0 + t)
    dg = 0.5 * (1.0 + t) + 0.5 * x * (1.0 - t * t) * _GC * (1.0 + 3 * 0.044715 * x * x)
    return g, dg


def _softplus_neg(lam):
    y = jnp.exp(-jnp.abs(lam))
    u = 1.0 + y
    l1p = jnp.where(u == 1.0, y, jnp.log(u) * y / (u - 1.0))
    return jnp.maximum(-lam, 0.0) + l1p


def _dot(a, b):
    return jnp.dot(a, b, preferred_element_type=F32)


def _dot_nt(a, b):
    return lax.dot_general(a, b, (((1,), (1,)), ((), ())), preferred_element_type=F32)


def _dot_tn(a, b):
    return lax.dot_general(a, b, (((0,), (0,)), ((), ())), preferred_element_type=F32)


def _rms_hat(x):
    r = lax.rsqrt(jnp.mean(x * x, axis=-1, keepdims=True) + EPS)
    return x * r, r


def _rms_bwd(dh, x, g):
    xh, r = _rms_hat(x)
    dxh = dh * g
    dx = r * (dxh - xh * jnp.mean(dxh * xh, axis=-1, keepdims=True))
    return dx, jnp.sum(dh * xh, axis=0, keepdims=True)


def _in_tile(j):
    m, hf = j // 2, j % 2
    orig = jnp.where(m < 2, m, jnp.where(m == 2, 4, jnp.where(m < 5, m - 1, 5)))
    t = orig * 2 + hf
    return t // 3, t % 3


ANY = pl.BlockSpec(memory_space=pl.ANY)


def _mm_in(x, g, w_in, l, after=()):
    S = x.shape[0]
    tm = min(TM_BIG, S)

    def body(x_ref, g_ref, w_ref, *rest):
        o_ref, h_ref = rest[-2:]

        @pl.when(pl.program_id(1) == 0)
        def _():
            xh, _ = _rms_hat(x_ref[...])
            h_ref[...] = (xh * g_ref[...]).astype(BF16)
        o_ref[...] = _dot(h_ref[...], w_ref[...]).astype(BF16)

    def w_map(i, j):
        sh, tl = _in_tile(j)
        return (sh, 0, tl)

    return pl.pallas_call(
        body, name=f"mm_in_{l}", grid=(S // tm, N_IN_T),
        in_specs=[pl.BlockSpec((tm, D), lambda i, j: (i, 0)), pl.BlockSpec((1, D), lambda i, j: (0, 0)),
                  pl.BlockSpec((None, D, 512), w_map)] + [ANY] * len(after),
        out_specs=[pl.BlockSpec((None, tm, 512), lambda i, j: (j // 2, i, j % 2)), pl.BlockSpec((tm, D), lambda i, j: (i, 0))],
        out_shape=[jax.ShapeDtypeStruct((6, S, D), BF16), jax.ShapeDtypeStruct((S, D), BF16)],
        compiler_params=_cp(("parallel", "arbitrary")),
    )(x, g, w_in, *after)


def _mm_res(a, w, res, l, name):
    S, K = a.shape

    def body(a_ref, w_ref, r_ref, o_ref):
        o_ref[...] = r_ref[...] + _dot(a_ref[...], w_ref[...])

    return pl.pallas_call(
        body, name=f"{name}_{l}", grid=(S // TM,),
        in_specs=[pl.BlockSpec((TM, K), lambda i: (i, 0)), pl.BlockSpec((K, D), lambda i: (0, 0)),
                  pl.BlockSpec((TM, D), lambda i: (i, 0))],
        out_specs=pl.BlockSpec((TM, D), lambda i: (i, 0)),
        out_shape=jax.ShapeDtypeStruct((S, D), F32),
        compiler_params=_cp(("parallel",)),
    )(a, w, res)


def _mm_ffn_in(x, g, w_fi, l):
    S = x.shape[0]

    def body(x_ref, g_ref, w_ref, gu_ref, ff_ref, h_ref):
        @pl.when(pl.program_id(1) == 0)
        def _():
            xh, _ = _rms_hat(x_ref[...])
            h_ref[...] = (xh * g_ref[...]).astype(BF16)
        hv = h_ref[...]
        ga = _dot(hv, w_ref[0])
        gb = _dot(hv, w_ref[1])
        gu_ref[0] = ga.astype(BF16)
        gu_ref[1] = gb.astype(BF16)
        ff_ref[...] = (ga * _sigmoid(ga) * gb).astype(BF16)

    gu, ff, h = pl.pallas_call(
        body, name=f"mm_ffn_in_{l}", grid=(S // TM, 2),
        in_specs=[pl.BlockSpec((TM, D), lambda i, s: (i, 0)), pl.BlockSpec((1, D), lambda i, s: (0, 0)),
                  pl.BlockSpec((2, None, D, DFF_SH), lambda i, s: (0, s, 0, 0))],
        out_specs=[pl.BlockSpec((2, None, TM, DFF_SH), lambda i, s: (0, s, i, 0)),
                   pl.BlockSpec((TM, DFF_SH), lambda i, s: (i, s)),
                   pl.BlockSpec((TM, D), lambda i, s: (i, 0))],
        out_shape=[jax.ShapeDtypeStruct((2, 2, S, DFF_SH), BF16), jax.ShapeDtypeStruct((S, DFF), BF16),
                   jax.ShapeDtypeStruct((S, D), BF16)],
        compiler_params=_cp(("parallel", "arbitrary")),
    )(x, g, w_fi.reshape(2, 2, D, DFF_SH))
    return gu.reshape(4, S, DFF_SH), ff, h


def _gmlp_fwd(z6, ws_b, bs_b, lg, lb):
    S = z6.shape[1]

    def body(z_ref, ws_ref, bs_ref, lg_ref, lb_ref, o_ref, mix):
        gv = _gelu(z_ref[1].astype(F32))
        xc = gv - jnp.mean(gv, axis=-1, keepdims=True)
        rs = lax.rsqrt(jnp.mean(xc * xc, axis=-1, keepdims=True) + EPS)
        vb = (xc * rs * lg_ref[...] + lb_ref[...]).astype(BF16)
        for gi in range(NH):
            cs = slice(gi * HD, (gi + 1) * HD)
            mix[:, cs] = _dot(ws_ref[gi], vb[:, cs])
        o_ref[...] = (_sigmoid(z_ref[2].astype(F32)) * _gelu(z_ref[0].astype(F32)) * (mix[...] + bs_ref[...])).astype(BF16)

    return pl.pallas_call(
        body, name="gmlp_fwd", grid=(S // CHUNK,),
        in_specs=[pl.BlockSpec((3, CHUNK, D), lambda i: (0, i, 0)), pl.BlockSpec((NH, CHUNK, CHUNK), lambda i: (0, 0, 0)),
                  pl.BlockSpec((CHUNK, D), lambda i: (0, 0)), pl.BlockSpec((1, D), lambda i: (0, 0)),
                  pl.BlockSpec((1, D), lambda i: (0, 0))],
        out_specs=pl.BlockSpec((CHUNK, D), lambda i: (i, 0)),
        out_shape=jax.ShapeDtypeStruct((S, D), BF16),
        scratch_shapes=[pltpu.VMEM((CHUNK, D), F32)],
        compiler_params=_cp(("parallel",)),
    )(z6, ws_b, bs_b, lg, lb)


def _row_iota():
    return lax.broadcasted_iota(jnp.int32, (RT, HD), 0)


SUB = 8
UNROLL = 4
GRAD_ROWS = 256


def _scan_up(a, b, carry):
    row = lax.broadcasted_iota(jnp.int32, (SUB, HD), 0)
    masks = [(d, row >= d) for d in (1, 2, 4)]
    c = jnp.broadcast_to(carry, (SUB, HD))
    hs = []
    for j in range(RT // SUB):
        aj, bj = a[SUB * j:SUB * (j + 1)], b[SUB * j:SUB * (j + 1)]
        for d, m in masks:
            bj = bj + aj * jnp.where(m, pltpu.roll(bj, d, 0), 0.0)
            aj = aj * jnp.where(m, pltpu.roll(aj, d, 0), 1.0)
        h = bj + aj * c
        hs.append(h)
        c = jnp.broadcast_to(h[SUB - 1:SUB, :], (SUB, HD))
    return jnp.concatenate(hs, axis=0), hs[-1][SUB - 1:SUB, :]


def _scan_down(a, b, carry):
    row = lax.broadcasted_iota(jnp.int32, (SUB, HD), 0)
    masks = [(d, row < SUB - d) for d in (1, 2, 4)]
    c = jnp.broadcast_to(carry, (SUB, HD))
    hs = []
    for j in reversed(range(RT // SUB)):
        aj, bj = a[SUB * j:SUB * (j + 1)], b[SUB * j:SUB * (j + 1)]
        for d, m in masks:
            bj = bj + aj * jnp.where(m, pltpu.roll(bj, SUB - d, 0), 0.0)
            aj = aj * jnp.where(m, pltpu.roll(aj, SUB - d, 0), 1.0)
        h = bj + aj * c
        hs.append(h)
        c = jnp.broadcast_to(h[0:1, :], (SUB, HD))
    return jnp.concatenate(hs[::-1], axis=0), hs[-1][0:1, :]


def _decay(r, sp_d):
    log_a = -LRU_C * r * sp_d
    a = jnp.exp(log_a)
    return a, jnp.sqrt(jnp.maximum(-jnp.tanh(log_a) * (a * a + 1.0), 0.0))


def _lru_gates(xc, d, wr_ref, br_ref, wi_ref, bi_ref, sp):
    xb = xc.astype(BF16)
    r = _sigmoid(_dot(xb, wr_ref[d]) + br_ref[d:d + 1, :])
    i = _sigmoid(_dot(xb, wi_ref[d]) + bi_ref[d:d + 1, :])
    a, mult = _decay(r, sp[d:d + 1, :])
    return r, i, a, mult


def _shifted(win, k):
    w = RT + 2 * PADR
    v = win if k == 0 else pltpu.roll(win, (-k) % w, 0)
    return v[PADR:PADR + RT]


def _conv_taps(win):
    return [_shifted(win, k) for k in (-1, 0, 1, 2)]


def _fill_padded(dst, src_ref, S):
    zeros = jnp.zeros((PADR, HD), F32)
    dst[0:PADR, :] = zeros
    dst[PADR + S:2 * PADR + S, :] = zeros

    def cp(i, c):
        t0 = pl.multiple_of(i * RT, RT)
        dst[pl.ds(t0 + PADR, RT), :] = src_ref[pl.ds(t0, RT), :].astype(F32)
        return c
    lax.fori_loop(0, S // RT, cp, 0)


def _conv_fwd_all(zxp, xc_s, cw_ref, cb_ref, S):
    def cv(i, c):
        t0 = pl.multiple_of(i * RT, RT)
        xm1, x0, xp1, xp2 = _conv_taps(zxp[pl.ds(t0, RT + 2 * PADR), :])
        xc_s[pl.ds(t0, RT), :] = (cb_ref[...] + xm1 * cw_ref[0:1, :] + x0 * cw_ref[1:2, :]
                                  + xp1 * cw_ref[2:3, :] + xp2 * cw_ref[3:4, :])
        return c
    lax.fori_loop(0, S // RT, cv, 0)


def _lru_specs(S):
    head = lambda h: (0, h)
    return [pl.BlockSpec((4, HD), head), pl.BlockSpec((1, HD), head),
            pl.BlockSpec((2, None, HD, HD), lambda h: (0, h, 0, 0)), pl.BlockSpec((2, HD), head),
            pl.BlockSpec((2, None, HD, HD), lambda h: (0, h, 0, 0)), pl.BlockSpec((2, HD), head),
            pl.BlockSpec((2, HD), head)]


def _lru_fwd(z6, ya, cw, cb, wr, br, wi, bi, lam):
    S = z6.shape[1]
    nt = S // RT

    def body(z_ref, ya_ref, cw_ref, cb_ref, wr_ref, br_ref, wi_ref, bi_ref, lam_ref, mg_ref, h0_ref, h1_ref, zxp, xc_s):
        sp = _softplus_neg(lam_ref[...])
        _fill_padded(zxp, z_ref.at[0], S)
        _conv_fwd_all(zxp, xc_s, cw_ref, cb_ref, S)

        def scans(i, carry):
            cu, cd = carry
            for u in range(UNROLL):
                j = i * UNROLL + u
                ru = pl.ds(pl.multiple_of(j * RT, RT), RT)
                rd = pl.ds(pl.multiple_of((nt - 1 - j) * RT, RT), RT)
                xu, xd = xc_s[ru, :], xc_s[rd, :]
                _, gi, a, mult = _lru_gates(xu, 0, wr_ref, br_ref, wi_ref, bi_ref, sp)
                hu, cu = _scan_up(a, mult * gi * xu, cu)
                h0_ref[ru, :] = hu
                _, gi, a, mult = _lru_gates(xd, 1, wr_ref, br_ref, wi_ref, bi_ref, sp)
                hd, cd = _scan_down(a, mult * gi * xd, cd)
                h1_ref[rd, :] = hd
            return cu, cd
        z1 = jnp.zeros((1, HD), F32)
        lax.fori_loop(0, nt // UNROLL, scans, (z1, z1))

        def merge(i, c):
            rows = pl.ds(pl.multiple_of(i * RT, RT), RT)
            yb = (h0_ref[rows, :] + h1_ref[rows, :]) * _gelu(z_ref[1, rows, :].astype(F32))
            mg_ref[rows, :] = (ya_ref[rows, :].astype(F32) + _sigmoid(z_ref[2, rows, :].astype(F32)) * yb).astype(BF16)
            return c
        lax.fori_loop(0, nt, merge, 0)

    col = pl.BlockSpec((S, HD), lambda h: (0, h))
    return pl.pallas_call(
        body, name="lru_fwd", grid=(NH,),
        in_specs=[pl.BlockSpec((3, S, HD), lambda h: (1, 0, h)), col] + _lru_specs(S),
        out_specs=[col, col, col],
        out_shape=[jax.ShapeDtypeStruct((S, D), BF16), jax.ShapeDtypeStruct((S, D), F32), jax.ShapeDtypeStruct((S, D), F32)],
        scratch_shapes=[pltpu.VMEM((S + 2 * PADR, HD), F32), pltpu.VMEM((S, HD), F32)],
        compiler_params=_cp(("parallel",)),
    )(z6, ya, cw, cb, wr, br, wi, bi, lam)


def _loss_head(x, tgt, g):
    S = x.shape[0]

    def body(x_ref, t_ref, g_ref, dx_ref, loss_ref, dg_ref):
        @pl.when(pl.program_id(0) == 0)
        def _():
            loss_ref[...] = jnp.zeros_like(loss_ref)
            dg_ref[...] = jnp.zeros_like(dg_ref)
        xv = x_ref[...]
        xh, _ = _rms_hat(xv)
        e = xh * g_ref[...] - t_ref[...]
        loss_ref[...] += jnp.sum(e * e) * (0.5 / D)
        dx, dgs = _rms_bwd(e * (1.0 / D), xv, g_ref[...])
        dx_ref[...] = dx
        dg_ref[...] += dgs

    return pl.pallas_call(
        body, name="loss_head", grid=(S // TM,),
        in_specs=[pl.BlockSpec((TM, D), lambda i: (i, 0)), pl.BlockSpec((TM, D), lambda i: (i, 0)),
                  pl.BlockSpec((1, D), lambda i: (0, 0))],
        out_specs=[pl.BlockSpec((TM, D), lambda i: (i, 0)), pl.BlockSpec((1, 128), lambda i: (0, 0)),
                   pl.BlockSpec((1, D), lambda i: (0, 0))],
        out_shape=[jax.ShapeDtypeStruct((S, D), F32), jax.ShapeDtypeStruct((1, 128), F32), jax.ShapeDtypeStruct((1, D), F32)],
        compiler_params=_cp(("arbitrary",)),
    )(x, tgt, g)


def _bwd_ffn_out(dx, w_fo, gu, l, after=()):
    S = dx.shape[0]

    def body(dx_ref, w_ref, gu_ref, *rest):
        o_ref = rest[-1]
        d = _dot_nt(dx_ref[...].astype(BF16), w_ref[...])
        ga, gb = gu_ref[0].astype(F32), gu_ref[1].astype(F32)
        sg = _sigmoid(ga)
        o_ref[0] = (d * gb * sg * (1.0 + ga * (1.0 - sg))).astype(BF16)
        o_ref[1] = (d * ga * sg).astype(BF16)

    pair = pl.BlockSpec((2, None, TM, DFF_SH), lambda i, s: (0, s, i, 0))
    dgu = pl.pallas_call(
        body, name=f"bwd_ffn_out_{l}", grid=(S // TM, 2),
        in_specs=[pl.BlockSpec((TM, D), lambda i, s: (i, 0)), pl.BlockSpec((DFF_SH, D), lambda i, s: (s, 0)), pair]
        + [ANY] * len(after),
        out_specs=pair,
        out_shape=jax.ShapeDtypeStruct((2, 2, S, DFF_SH), BF16),
        compiler_params=_cp(("parallel", "arbitrary")),
    )(dx, w_fo, gu.reshape(2, 2, S, DFF_SH), *after)
    return dgu.reshape(4, S, DFF_SH)


def _mm_tn(a, b, m_blk, name):
    S, M = a.shape

    def body(a_ref, b_ref, o_ref):
        @pl.when(pl.program_id(1) == 0)
        def _():
            o_ref[...] = jnp.zeros_like(o_ref)
        o_ref[...] += _dot_tn(a_ref[...], b_ref[...].astype(BF16))

    return pl.pallas_call(
        body, name=name, grid=(M // m_blk, S // TM),
        in_specs=[pl.BlockSpec((TM, m_blk), lambda m, k: (k, m)), pl.BlockSpec((TM, D), lambda m, k: (k, 0))],
        out_specs=pl.BlockSpec((m_blk, D), lambda m, k: (m, 0)),
        out_shape=jax.ShapeDtypeStruct((M, D), F32),
        compiler_params=_cp(("parallel", "arbitrary")),
    )(a, b)


def _mm_nt_rms_bwd(a, a_spec, w, w_spec, nk, tm, x, g, dres, name):
    S = x.shape[0]

    def body(a_ref, w_ref, x_ref, g_ref, r_ref, dx_ref, dg_ref, acc):
        i, k = pl.program_id(0), pl.program_id(1)

        @pl.when(k == 0)
        def _():
            acc[...] = jnp.zeros_like(acc)
        acc[...] += _dot_nt(a_ref[...], w_ref[...])

        @pl.when(jnp.logical_and(i == 0, k == 0))
        def _():
            dg_ref[...] = jnp.zeros_like(dg_ref)

        @pl.when(k == nk - 1)
        def _():
            dx, dgs = _rms_bwd(acc[...], x_ref[...], g_ref[...])
            dx_ref[...] = r_ref[...] + dx
            dg_ref[...] += dgs

    row = pl.BlockSpec((tm, D), lambda i, k: (i, 0))
    vec = pl.BlockSpec((1, D), lambda i, k: (0, 0))
    return pl.pallas_call(
        body, name=name, grid=(S // tm, nk),
        in_specs=[a_spec, w_spec, row, vec, row],
        out_specs=[row, vec],
        out_shape=[jax.ShapeDtypeStruct((S, D), F32), jax.ShapeDtypeStruct((1, D), F32)],
        scratch_shapes=[pltpu.VMEM((tm, D), F32)],
        compiler_params=_cp(("arbitrary", "arbitrary")),
    )(a, w, x, g, dres)


def _dw_ffn_in(h, dgu, l):
    S = h.shape[0]

    def body(h_ref, b_ref, o_ref):
        @pl.when(pl.program_id(1) == 0)
        def _():
            o_ref[...] = jnp.zeros_like(o_ref)
        o_ref[...] += _dot_tn(h_ref[...], b_ref[...])

    return pl.pallas_call(
        body, name=f"dw_ffn_in_{l}", grid=(4, S // TM),
        in_specs=[pl.BlockSpec((TM, D), lambda j, k: (k, 0)), pl.BlockSpec((None, TM, DFF_SH), lambda j, k: (j, k, 0))],
        out_specs=pl.BlockSpec((None, D, DFF_SH), lambda j, k: (j, 0, 0)),
        out_shape=jax.ShapeDtypeStruct((4, D, DFF_SH), F32),
        compiler_params=_cp(("parallel", "arbitrary")),
    )(h, dgu)


_HALF_COMPS = ((0, 1, 3), (4, 2, 5))


def _dw_in(h, dz6, l):
    S = h.shape[0]

    def body(h_ref, d0_ref, d1_ref, d2_ref, o_ref):
        @pl.when(pl.program_id(1) == 0)
        def _():
            o_ref[...] = jnp.zeros_like(o_ref)
        hv = h_ref[...]
        for q, d_ref in enumerate((d0_ref, d1_ref, d2_ref)):
            for hf in range(2):
                col = 1024 * q + 512 * hf
                o_ref[col // 1536, :, col % 1536:col % 1536 + 512] += _dot_tn(hv, d_ref[:, 512 * hf:512 * (hf + 1)])

    def comp(q):
        return pl.BlockSpec((None, TM, D), lambda p, k: (jnp.where(p == 0, _HALF_COMPS[0][q], _HALF_COMPS[1][q]), k, 0))

    return pl.pallas_call(
        body, name=f"dw_in_{l}", grid=(2, S // TM),
        in_specs=[pl.BlockSpec((TM, D), lambda p, k: (k, 0)), comp(0), comp(1), comp(2)],
        out_specs=pl.BlockSpec((2, D, 1536), lambda p, k: (p, 0, 0)),
        out_shape=jax.ShapeDtypeStruct((4, D, 1536), F32),
        compiler_params=_cp(("parallel", "arbitrary")),
    )(h, dz6, dz6, dz6)


def _bwd_out(dx, w_o, merged, l):
    S = dx.shape[0]

    def body(dx_ref, w_ref, m_ref, dm_ref, dw_ref):
        @pl.when(pl.program_id(0) == 0)
        def _():
            dw_ref[...] = jnp.zeros_like(dw_ref)
        dxb = dx_ref[...].astype(BF16)
        dm_ref[...] = _dot_nt(dxb, w_ref[...]).astype(BF16)
        dw_ref[...] += _dot_tn(m_ref[...], dxb)

    row = pl.BlockSpec((TM, D), lambda i: (i, 0))
    return pl.pallas_call(
        body, name=f"bwd_out_{l}", grid=(S // TM,),
        in_specs=[row, pl.BlockSpec((D, D), lambda i: (0, 0)), row],
        out_specs=[row, pl.BlockSpec((D, D), lambda i: (0, 0))],
        out_shape=[jax.ShapeDtypeStruct((S, D), BF16), jax.ShapeDtypeStruct((D, D), F32)],
        compiler_params=_cp(("arbitrary",)),
    )(dx, w_o, merged)


def _gmlp_bwd(dm, z6, ws_b, wst_b, bs_b, lg, lb, after=()):
    S = z6.shape[1]

    def body(dm_ref, z_ref, ws_ref, wst_ref, bs_ref, lg_ref, lb_ref, *rest):
        dz_ref, dws_ref, dbs_ref, dlg_ref, dlb_ref, mix, dv = rest[-7:]

        @pl.when(pl.program_id(0) == 0)
        def _():
            dws_ref[...] = jnp.zeros_like(dws_ref)
            dbs_ref[...] = jnp.zeros_like(dbs_ref)
            dlg_ref[...] = jnp.zeros_like(dlg_ref)
            dlb_ref[...] = jnp.zeros_like(dlb_ref)
        gv, dgelu_v = _gelu_and_grad(z_ref[1].astype(F32))
        xc = gv - jnp.mean(gv, axis=-1, keepdims=True)
        rs = lax.rsqrt(jnp.mean(xc * xc, axis=-1, keepdims=True) + EPS)
        vh = xc * rs
        vb = (vh * lg_ref[...] + lb_ref[...]).astype(BF16)
        for gi in range(NH):
            cs = slice(gi * HD, (gi + 1) * HD)
            mix[:, cs] = _dot(ws_ref[gi], vb[:, cs])
        u, dgelu_u = _gelu_and_grad(z_ref[0].astype(F32))
        sa = _sigmoid(z_ref[2].astype(F32))
        mixed = mix[...] + bs_ref[...]
        dyg = dm_ref[...].astype(F32)
        dz_ref[2] = (dyg * u * mixed * sa * (1.0 - sa)).astype(BF16)
        dya = dyg * sa
        dz_ref[0] = (dya * mixed * dgelu_u).astype(BF16)
        dmix = dya * u
        dmb = dmix.astype(BF16)
        for gi in range(NH):
            cs = slice(gi * HD, (gi + 1) * HD)
            dv[:, cs] = _dot(wst_ref[gi], dmb[:, cs])
            dws_ref[gi] += _dot_nt(dmb[:, cs], vb[:, cs])
            dbs_ref[gi] += jnp.broadcast_to(jnp.sum(dmix[:, cs], axis=1, keepdims=True), (CHUNK, HD))
        dvv = dv[...]
        dlg_ref[...] += jnp.sum(dvv * vh, axis=0, keepdims=True)
        dlb_ref[...] += jnp.sum(dvv, axis=0, keepdims=True)
        dvh = dvv * lg_ref[...]
        dgv = rs * (dvh - jnp.mean(dvh, axis=-1, keepdims=True) - vh * jnp.mean(dvh * vh, axis=-1, keepdims=True))
        dz_ref[1] = (dgv * dgelu_v).astype(BF16)

    vec = pl.BlockSpec((1, D), lambda i: (0, 0))
    mat = pl.BlockSpec((NH, CHUNK, CHUNK), lambda i: (0, 0, 0))
    return pl.pallas_call(
        body, name="gmlp_bwd", grid=(S // CHUNK,),
        in_specs=[pl.BlockSpec((CHUNK, D), lambda i: (i, 0)), pl.BlockSpec((3, CHUNK, D), lambda i: (0, i, 0)), mat, mat,
                  pl.BlockSpec((CHUNK, D), lambda i: (0, 0)), vec, vec] + [ANY] * len(after),
        out_specs=[pl.BlockSpec((3, CHUNK, D), lambda i: (0, i, 0)), mat, mat, vec, vec],
        out_shape=[jax.ShapeDtypeStruct((6, S, D), BF16), jax.ShapeDtypeStruct((NH, CHUNK, CHUNK), F32),
                   jax.ShapeDtypeStruct((NH, CHUNK, HD), F32), jax.ShapeDtypeStruct((1, D), F32), jax.ShapeDtypeStruct((1, D), F32)],
        scratch_shapes=[pltpu.VMEM((CHUNK, D), F32), pltpu.VMEM((CHUNK, D), F32)],
        compiler_params=_cp(("arbitrary",)),
    )(dm, z6, ws_b, wst_b, bs_b, lg, lb, *after)


def _lru_bwd(dz6, dm, z6, h0, h1, cw, cb, wr, br, wi, bi, lam):
    S = z6.shape[1]
    nt = S // RT

    def body(dz_in, dm_ref, z_ref, h0_ref, h1_ref, cw_ref, cb_ref, wr_ref, br_ref, wi_ref, bi_ref, lam_ref,
             dz_ref, dcw_ref, dcb_ref, dwr_ref, dbr_ref, dwi_ref, dbi_ref, dlam_ref, zxp, xc_s, dhs_s, dxcp, r_s, lam_s):
        del dz_in
        lam = lam_ref[...]
        sp = _softplus_neg(lam)
        row = _row_iota()
        _fill_padded(zxp, z_ref.at[0], S)
        _conv_fwd_all(zxp, xc_s, cw_ref, cb_ref, S)
        zeros = jnp.zeros((PADR, HD), F32)
        dxcp[0:PADR, :] = zeros
        dxcp[PADR + S:2 * PADR + S, :] = zeros
        dwr_ref[...] = jnp.zeros_like(dwr_ref)
        dwi_ref[...] = jnp.zeros_like(dwi_ref)

        def pre(i, c):
            rows = pl.ds(pl.multiple_of(i * RT, RT), RT)
            hs = h0_ref[rows, :] + h1_ref[rows, :]
            dmv = dm_ref[rows, :].astype(F32)
            sb = _sigmoid(z_ref[2, rows, :].astype(F32))
            gg, dgg = _gelu_and_grad(z_ref[1, rows, :].astype(F32))
            dz_ref[2, rows, :] = (dmv * hs * gg * sb * (1.0 - sb)).astype(BF16)
            dyb = dmv * sb
            dz_ref[1, rows, :] = (dyb * hs * dgg).astype(BF16)
            dhs_s[rows, :] = dyb * gg
            return c
        lax.fori_loop(0, nt, pre, 0)

        def gate_bwd(d, gates, lamv, da, xc):
            r, gi, a, mult = gates
            dmult = lamv * gi * xc
            dgi = lamv * mult * xc
            dlog = (da - dmult * a / mult) * a
            dpr = (dlog * (-LRU_C) * sp[d:d + 1, :]) * r * (1.0 - r)
            dpi = dgi * gi * (1.0 - gi)
            xb, dprb, dpib = xc.astype(BF16), dpr.astype(BF16), dpi.astype(BF16)
            dwr_ref[d] += _dot_tn(xb, dprb)
            dwi_ref[d] += _dot_tn(xb, dpib)
            dxc = lamv * mult * gi + _dot_nt(dprb, wr_ref[d]) + _dot_nt(dpib, wi_ref[d])
            return dxc, (jnp.sum(dlog * r, axis=0, keepdims=True) * (-LRU_C), jnp.sum(dpr, axis=0, keepdims=True),
                         jnp.sum(dpi, axis=0, keepdims=True))

        def rgates(i, c):
            for u in range(UNROLL):
                rows = pl.ds(pl.multiple_of((i * UNROLL + u) * RT, RT), RT)
                xb = xc_s[rows, :].astype(BF16)
                for d in range(2):
                    r_s[d, rows, :] = _sigmoid(_dot(xb, wr_ref[d]) + br_ref[d:d + 1, :])
            return c
        lax.fori_loop(0, nt // UNROLL, rgates, 0)

        def chains(i, carry):
            qn, qp = carry
            for u in range(UNROLL):
                j = i * UNROLL + u
                rd = pl.ds(pl.multiple_of((nt - 1 - j) * RT, RT), RT)
                a, dhs = _decay(r_s[0, rd, :], sp[0:1, :])[0], dhs_s[rd, :]
                q, q_first = _scan_down(a, a * dhs, qn)
                lam_s[0, rd, :] = dhs + jnp.where(row == RT - 1, qn, pltpu.roll(q, RT - 1, 0))
                qn = q_first
                ru = pl.ds(pl.multiple_of(j * RT, RT), RT)
                a, dhs = _decay(r_s[1, ru, :], sp[1:2, :])[0], dhs_s[ru, :]
                q, q_last = _scan_up(a, a * dhs, qp)
                lam_s[1, ru, :] = dhs + jnp.where(row == 0, qp, pltpu.roll(q, 1, 0))
                qp = q_last
            return qn, qp

        z1 = jnp.zeros((1, HD), F32)
        lax.fori_loop(0, nt // UNROLL, chains, (z1, z1))

        ct = min(GRAD_ROWS, S)
        crow = lax.broadcasted_iota(jnp.int32, (ct, HD), 0)

        def tile_grads(i, acc):
            t0 = pl.multiple_of(i * ct, ct)
            rows = pl.ds(t0, ct)
            xc = xc_s[rows, :]
            xb = xc.astype(BF16)
            tp = pl.multiple_of(jnp.maximum(t0 - PADR, 0), PADR)
            prev = jnp.where(t0 > 0, h0_ref[pl.ds(tp, PADR), :][PADR - 1:PADR, :], 0.0)
            tn = pl.multiple_of(jnp.minimum(t0 + ct, S - PADR), PADR)
            nxt = jnp.where(t0 + ct < S, h1_ref[pl.ds(tn, PADR), :][0:1, :], 0.0)
            hside = (jnp.where(crow == 0, prev, pltpu.roll(h0_ref[rows, :], 1, 0)),
                     jnp.where(crow == ct - 1, nxt, pltpu.roll(h1_ref[rows, :], ct - 1, 0)))
            dxc, sums = 0.0, ()
            for d in range(2):
                r = r_s[d, rows, :]
                gi = _sigmoid(_dot(xb, wi_ref[d]) + bi_ref[d:d + 1, :])
                a, mult = _decay(r, sp[d:d + 1, :])
                lamv = lam_s[d, rows, :]
                dxc_d, s_d = gate_bwd(d, (r, gi, a, mult), lamv, lamv * hside[d], xc)
                dxc = dxc + dxc_d
                sums = sums + s_d
            dxcp[pl.ds(t0 + PADR, ct), :] = dxc
            return tuple(x + y for x, y in zip(acc, sums))

        s_sp0, s_br0, s_bi0, s_sp1, s_br1, s_bi1 = lax.fori_loop(0, S // ct, tile_grads, (z1,) * 6)

        dsp = jnp.concatenate([s_sp0, s_sp1], axis=0)
        dlam_ref[...] = -dsp * _sigmoid(-lam)
        dbr_ref[...] = jnp.concatenate([s_br0, s_br1], axis=0)
        dbi_ref[...] = jnp.concatenate([s_bi0, s_bi1], axis=0)

        def conv_bwd(i, carry):
            c0, c1, c2, c3, cb_ = carry
            t0 = pl.multiple_of(i * RT, RT)
            dwin = dxcp[pl.ds(t0, RT + 2 * PADR), :]
            d0 = _shifted(dwin, 0)
            dz_ref[0, pl.ds(t0, RT), :] = (_shifted(dwin, 1) * cw_ref[0:1, :] + d0 * cw_ref[1:2, :]
                                           + _shifted(dwin, -1) * cw_ref[2:3, :] + _shifted(dwin, -2) * cw_ref[3:4, :]).astype(BF16)
            xm1, x0, xp1, xp2 = _conv_taps(zxp[pl.ds(t0, RT + 2 * PADR), :])
            sm = lambda v: jnp.sum(v, axis=0, keepdims=True)
            return c0 + sm(d0 * xm1), c1 + sm(d0 * x0), c2 + sm(d0 * xp1), c3 + sm(d0 * xp2), cb_ + sm(d0)

        c0, c1, c2, c3, cb_ = lax.fori_loop(0, nt, conv_bwd, (z1, z1, z1, z1, z1))
        dcw_ref[...] = jnp.concatenate([c0, c1, c2, c3], axis=0)
        dcb_ref[...] = cb_

    col = pl.BlockSpec((S, HD), lambda h: (0, h))
    head = lambda h: (0, h)
    wspec = pl.BlockSpec((2, None, HD, HD), lambda h: (0, h, 0, 0))
    return pl.pallas_call(
        body, name="lru_bwd", grid=(NH,),
        in_specs=[pl.BlockSpec(memory_space=pl.ANY), col, pl.BlockSpec((3, S, HD), lambda h: (1, 0, h)), col, col] + _lru_specs(S),
        out_specs=[pl.BlockSpec((3, S, HD), lambda h: (1, 0, h)), pl.BlockSpec((4, HD), head), pl.BlockSpec((1, HD), head),
                   wspec, pl.BlockSpec((2, HD), head), wspec, pl.BlockSpec((2, HD), head), pl.BlockSpec((2, HD), head)],
        out_shape=[jax.ShapeDtypeStruct((6, S, D), BF16), jax.ShapeDtypeStruct((4, D), F32), jax.ShapeDtypeStruct((1, D), F32),
                   jax.ShapeDtypeStruct((2, NH, HD, HD), F32), jax.ShapeDtypeStruct((2, D), F32),
                   jax.ShapeDtypeStruct((2, NH, HD, HD), F32), jax.ShapeDtypeStruct((2, D), F32), jax.ShapeDtypeStruct((2, D), F32)],
        scratch_shapes=[pltpu.VMEM((S + 2 * PADR, HD), F32), pltpu.VMEM((S, HD), F32), pltpu.VMEM((S, HD), F32),
                        pltpu.VMEM((S + 2 * PADR, HD), F32), pltpu.VMEM((2, S, HD), F32), pltpu.VMEM((2, S, HD), F32)],
        input_output_aliases={0: 0},
        compiler_params=_cp(("parallel",)),
    )(dz6, dm, z6, h0, h1, cw, cb, wr, br, wi, bi, lam)


LAYER_SMALL = ("norm1_g", "gmlp_ln_g", "gmlp_ln_b", "gmlp_w_s", "gmlp_b_s", "conv_w", "conv_b",
               "lru_w_r", "lru_b_r", "lru_w_i", "lru_b_i", "lru_lambda", "norm2_g")


def _forward_layer(l, x, p, wb, after=(), rest=None):
    g1, g2 = p["norm1_g"][l][None], p["norm2_g"][l][None]
    ws_b = p["gmlp_w_s"][l].astype(BF16)
    tm = dict(ws_b=ws_b, wst_b=jnp.swapaxes(ws_b, 1, 2), bs_b=jnp.repeat(p["gmlp_b_s"][l].T, HD, axis=1),
              lg=p["gmlp_ln_g"][l][None], lb=p["gmlp_ln_b"][l][None])
    lru = (p["conv_w"][l], p["conv_b"][l][None], p["lru_w_r"][l].astype(BF16), p["lru_b_r"][l],
           p["lru_w_i"][l].astype(BF16), p["lru_b_i"][l], p["lru_lambda"][l])
    z6, hn1 = _mm_in(x, g1, wb["w_in"], l, after)
    ya = _gmlp_fwd(z6, tm["ws_b"], tm["bs_b"], tm["lg"], tm["lb"])
    merged, h0, h1 = _lru_fwd(z6, ya, *lru)
    if rest is not None:
        wb = dict(wb, **rest(merged))
    x1 = _mm_res(merged, wb["w_out"], x, l, "mm_out")
    gu, ff, hn2 = _mm_ffn_in(x1, g2, wb["w_ffn_in"], l)
    x2 = _mm_res(ff, wb["w_ffn_out"], x1, l, "mm_ffn_out")
    return x2, dict(x=x, z6=z6, h0=h0, h1=h1, merged=merged, x1=x1, gu=gu, ff=ff, g1=g1, g2=g2, tm=tm, lru=lru,
                    hn1=hn1, hn2=hn2, wb=wb)


def _backward_layer(l, dx, s, after=(), midway=None):
    S = dx.shape[0]
    tm, wb = s["tm"], s["wb"]
    g2 = s["g2"]
    dgu = _bwd_ffn_out(dx, wb["w_ffn_out"], s["gu"], l, after)
    dwfo = _mm_tn(s["ff"], dx, DFF_SH, f"dw_ffn_out_{l}")
    tmb = min(TM_BIG, S)
    dx1, dg2 = _mm_nt_rms_bwd(
        dgu, pl.BlockSpec((None, tmb, DFF_SH), lambda i, k: (k, i, 0)),
        wb["w_ffn_in"], pl.BlockSpec((None, D, DFF_SH), lambda i, k: (k, 0, 0)),
        4, tmb, s["x1"], g2, dx, f"bwd_ffn_in_{l}")
    dwfi = _dw_ffn_in(s["hn2"], dgu, l)
    dmg, dwo = _bwd_out(dx1, wb["w_out"], s["merged"], l)
    mid = () if midway is None else tuple(midway([dwo, dwfi, dwfo]))
    dz6, dws, dbs, dlg, dlb = _gmlp_bwd(dmg, s["z6"], tm["ws_b"], tm["wst_b"], tm["bs_b"], tm["lg"], tm["lb"], mid)
    dz6, dcw, dcb, dwr, dbr, dwi, dbi, dlam = _lru_bwd(dz6, dmg, s["z6"], s["h0"], s["h1"], *s["lru"])

    def w_map(i, k):
        sh, tl = _in_tile(k)
        return (sh, 0, tl)

    dx0, dg1 = _mm_nt_rms_bwd(
        dz6, pl.BlockSpec((None, tmb, 512), lambda i, k: (k // 2, i, k % 2)),
        wb["w_in"], pl.BlockSpec((None, D, 512), w_map),
        N_IN_T, tmb, s["x"], s["g1"], dx1, f"bwd_in_{l}")
    dwin = _dw_in(s["hn1"], dz6, l)
    small = dict(norm1_g=dg1[0], gmlp_ln_g=dlg[0], gmlp_ln_b=dlb[0], gmlp_w_s=dws, gmlp_b_s=dbs[:, :, 0], conv_w=dcw, conv_b=dcb[0],
                 lru_w_r=dwr, lru_b_r=dbr, lru_w_i=dwi, lru_b_i=dbi, lru_lambda=dlam, norm2_g=dg2[0])
    return dx0, [dwin, dwo, dwfi, dwfo], small


def _local_step(x, tgt, p, wbs):
    saved = []
    for l in range(2):
        x, s = _forward_layer(l, x, p, wbs[l])
        saved.append(s)
    dx, loss_v, dfg = _loss_head(x, tgt, p["final_g"][None])
    big, smalls = [None, None], [None, None]
    for l in (1, 0):
        dx, big[l], smalls[l] = _backward_layer(l, dx, saved[l])
    small = {k: jnp.stack([smalls[0][k], smalls[1][k]]) for k in LAYER_SMALL}
    small["final_g"] = dfg[0]
    return loss_v, dx, big, small


def _place():
    x, y, c = lax.axis_index("x"), lax.axis_index("y"), lax.axis_index("c")
    return x, y, c, 2 * x + y


def _chip_at(x, y, d):
    px = 1 - x if d & 2 else x
    py = 1 - y if d & 1 else y
    return px, py, 2 * px + py


HBM = pl.BlockSpec(memory_space=pltpu.HBM)
SEM = pl.BlockSpec(memory_space=pltpu.SEMAPHORE)
DATAFLOW = pltpu.SideEffectType.DATAFLOW_SIDE_EFFECTING


def _in_hbm(a):
    return pltpu.with_memory_space_constraint(a, pltpu.HBM)


def _cast_into(wf, l, chip_arr, name):
    _, rows, cols = wf.shape
    rh = rows // 2

    def body(ch_ref, w_ref, o_ref):
        o_ref[...] = w_ref[...].astype(BF16)

    return pl.pallas_call(
        body, name=name, out_shape=jax.ShapeDtypeStruct((4, 2, rh, cols), BF16),
        grid_spec=pltpu.PrefetchScalarGridSpec(
            num_scalar_prefetch=1, grid=(2,),
            in_specs=[pl.BlockSpec((None, None, rh, cols), lambda h, ch: (l, h, 0, 0))],
            out_specs=pl.BlockSpec((None, None, rh, cols), lambda h, ch: (ch[0], h, 0, 0))),
        compiler_params=_cp(("parallel",)),
    )(chip_arr, wf.reshape(2, 2, rh, cols))


def _half_block(ref, chip, half, to, send_sem, recv_sem):
    blk = ref.at[chip, half]
    return pltpu.make_async_remote_copy(src_ref=blk, dst_ref=blk, send_sem=send_sem, recv_sem=recv_sem,
                                        device_id=to, device_id_type=MESH)


def _gather_weights(bufs, tiny):
    nt = len(bufs)
    n_ici = nt * 3

    def body(*refs):
        tiny_ref = refs[nt]
        o_refs, tiny_o = refs[nt + 1:2 * nt + 1], refs[2 * nt + 1]
        send, recv, fsend, frecv, tsend, trecv, lsem = refs[2 * nt + 2:]
        x, y, c, chip = _place()
        local = pltpu.make_async_copy(tiny_ref, tiny_o.at[chip], lsem)
        local.start()

        def tin(d, origin_chip, to):
            return pltpu.make_async_remote_copy(
                src_ref=tiny_ref, dst_ref=tiny_o.at[origin_chip], send_sem=tsend.at[d - 1], recv_sem=trecv.at[d - 1],
                device_id=to, device_id_type=MESH)

        sends = []
        for t in range(nt):
            for d in (1, 2, 3):
                px, py, _ = _chip_at(x, y, d)
                sends.append(_half_block(o_refs[t], chip, c, (px, py, c), send.at[3 * t + d - 1], recv.at[3 * t + d - 1]))
        for d in (1, 2, 3):
            px, py, _ = _chip_at(x, y, d)
            sends.append(tin(d, chip, (px, py, c)))
        for cp in sends:
            cp.start()
        passed = []
        for t in range(nt):
            for d in (1, 2, 3):
                k = 3 * t + d - 1
                _, _, pchip = _chip_at(x, y, d)
                _half_block(o_refs[t], pchip, c, (x, y, c), send.at[k], recv.at[k]).wait_recv()
                f = _half_block(o_refs[t], pchip, c, (x, y, 1 - c), fsend.at[k], frecv.at[k])
                f.start()
                passed.append(f)
        for t in range(nt):
            for d in (1, 2, 3):
                k = 3 * t + d - 1
                _, _, pchip = _chip_at(x, y, d)
                _half_block(o_refs[t], pchip, 1 - c, (x, y, 1 - c), fsend.at[k], frecv.at[k]).wait_recv()
        for d in (1, 2, 3):
            _, _, pchip = _chip_at(x, y, d)
            tin(d, pchip, (x, y, c)).wait_recv()
        for cp in sends + passed:
            cp.wait_send()
        local.wait()

    out_shape = [jax.ShapeDtypeStruct(b.shape, b.dtype) for b in bufs]
    out_shape.append(jax.ShapeDtypeStruct((4,) + tiny.shape, tiny.dtype))
    outs = pl.pallas_call(
        body, name="gather_weights_0", out_shape=out_shape,
        in_specs=[ANY] * (nt + 1), out_specs=[ANY] * (nt + 1),
        scratch_shapes=[pltpu.SemaphoreType.DMA((n_ici,)), pltpu.SemaphoreType.DMA((n_ici,)),
                        pltpu.SemaphoreType.DMA((n_ici,)), pltpu.SemaphoreType.DMA((n_ici,)),
                        pltpu.SemaphoreType.DMA((3,)), pltpu.SemaphoreType.DMA((3,)), pltpu.SemaphoreType.DMA],
        input_output_aliases={t: t for t in range(nt)},
        compiler_params=_cp(has_side_effects=True),
    )(*bufs, tiny)
    return outs[:nt], outs[nt]


def _gather_start(bufs, tag, after=()):
    nt, na = len(bufs), len(after)

    def body(*refs):
        b_refs = refs[:nt]
        send, recv = refs[nt + na], refs[nt + na + 1]
        token = refs[2 * nt + na + 2]
        x, y, c, chip = _place()
        for t in range(nt):
            for d in (1, 2, 3):
                px, py, _ = _chip_at(x, y, d)
                _half_block(b_refs[t], chip, c, (px, py, c), send.at[3 * t + d - 1], recv.at[3 * t + d - 1]).start()
        token[...] = jnp.zeros_like(token)

    outs = pl.pallas_call(
        body, name=f"gather_start_{tag}",
        out_shape=(pltpu.SemaphoreType.DMA((3 * nt,)), pltpu.SemaphoreType.DMA((3 * nt,)),
                   *[pltpu.HBM(b.shape, b.dtype) for b in bufs], jax.ShapeDtypeStruct((8, 128), F32)),
        in_specs=[HBM] * nt + [ANY] * na, out_specs=(SEM, SEM, *[HBM] * nt, pl.BlockSpec(memory_space=pltpu.VMEM)),
        input_output_aliases={t: 2 + t for t in range(nt)},
        compiler_params=pltpu.CompilerParams(has_side_effects=DATAFLOW),
    )(*[_in_hbm(b) for b in bufs], *after)
    return outs[0], outs[1], list(outs[2:2 + nt]), outs[2 + nt]


def _gather_wait(send, recv, bufs, after, tag):
    nt = len(bufs)

    def body(*refs):
        b_refs = refs[:nt]
        send_ref, recv_ref = refs[nt], refs[nt + 1]
        x, y, c, chip = _place()
        for t in range(nt):
            for d in (1, 2, 3):
                k = 3 * t + d - 1
                px, py, pchip = _chip_at(x, y, d)
                _half_block(b_refs[t], chip, c, (px, py, c), send_ref.at[k], recv_ref.at[k]).wait_send()
                _half_block(b_refs[t], pchip, c, (px, py, c), send_ref.at[k], recv_ref.at[k]).wait_recv()

    outs = pl.pallas_call(
        body, name=f"gather_wait_{tag}", out_shape=[pltpu.HBM(b.shape, b.dtype) for b in bufs],
        in_specs=[HBM] * nt + [SEM, SEM, ANY], out_specs=[HBM] * nt,
        input_output_aliases={t: t for t in range(nt)},
        compiler_params=pltpu.CompilerParams(has_side_effects=DATAFLOW),
    )(*bufs, send, recv, after)
    return list(outs)


def _gather_pass_on(bufs, tag):
    nt = len(bufs)

    def body(*refs):
        o_refs = refs[nt:2 * nt]
        fsend, frecv = refs[2 * nt:]
        x, y, c, _ = _place()
        cps = []
        for t in range(nt):
            for d in (1, 2, 3):
                k = 3 * t + d - 1
                _, _, pchip = _chip_at(x, y, d)
                cps.append(_half_block(o_refs[t], pchip, c, (x, y, 1 - c), fsend.at[k], frecv.at[k]))
        for cp in cps:
            cp.start()
        for t in range(nt):
            for d in (1, 2, 3):
                k = 3 * t + d - 1
                _, _, pchip = _chip_at(x, y, d)
                _half_block(o_refs[t], pchip, 1 - c, (x, y, 1 - c), fsend.at[k], frecv.at[k]).wait_recv()
        for cp in cps:
            cp.wait_send()

    return pl.pallas_call(
        body, name=f"gather_pass_on_{tag}", out_shape=[jax.ShapeDtypeStruct(b.shape, b.dtype) for b in bufs],
        in_specs=[ANY] * nt, out_specs=[ANY] * nt,
        scratch_shapes=[pltpu.SemaphoreType.DMA((3 * nt,)), pltpu.SemaphoreType.DMA((3 * nt,))],
        input_output_aliases={t: t for t in range(nt)},
        compiler_params=_cp(has_side_effects=True),
    )(*bufs)


def _to_sibling_halves(gs, l):
    nt = len(gs)

    def body(*refs):
        g_refs, o_refs = refs[:nt], refs[nt:2 * nt]
        send, recv = refs[2 * nt:]
        x, y, c, _ = _place()
        cps = [pltpu.make_async_remote_copy(
            src_ref=g_refs[t].at[k, 1 - c], dst_ref=o_refs[t].at[k], send_sem=send.at[4 * t + k], recv_sem=recv.at[4 * t + k],
            device_id=(x, y, 1 - c), device_id_type=MESH) for t in range(nt) for k in range(4)]
        for cp in cps:
            cp.start()
        for cp in cps:
            cp.wait()

    return pl.pallas_call(
        body, name=f"grads_to_sibling_{l}", out_shape=[jax.ShapeDtypeStruct((4,) + g.shape[2:], g.dtype) for g in gs],
        in_specs=[ANY] * nt, out_specs=[ANY] * nt,
        scratch_shapes=[pltpu.SemaphoreType.DMA((4 * nt,)), pltpu.SemaphoreType.DMA((4 * nt,))],
        compiler_params=_cp(has_side_effects=True),
    )(*gs)


def _chip_copy(c_ref, land_ref, x, y, c, d, send_sem, recv_sem):
    px, py, pchip = _chip_at(x, y, d)
    return pltpu.make_async_remote_copy(src_ref=c_ref.at[pchip], dst_ref=land_ref.at[d - 1], send_sem=send_sem, recv_sem=recv_sem,
                                        device_id=(px, py, c), device_id_type=MESH)


def _to_chips_start(cs, l):
    nt = len(cs)
    lands = [lax.empty((3,) + a.shape[1:], a.dtype) for a in cs]

    def body(*refs):
        c_refs, land_refs = refs[:nt], refs[nt:2 * nt]
        send, recv = refs[2 * nt], refs[2 * nt + 1]
        token = refs[4 * nt + 2]
        x, y, c, _ = _place()
        for t in range(nt):
            for d in (1, 2, 3):
                _chip_copy(c_refs[t], land_refs[t], x, y, c, d, send.at[3 * t + d - 1], recv.at[3 * t + d - 1]).start()
        token[...] = jnp.zeros_like(token)

    outs = pl.pallas_call(
        body, name=f"grads_to_chips_start_{l}",
        out_shape=(pltpu.SemaphoreType.DMA((3 * nt,)), pltpu.SemaphoreType.DMA((3 * nt,)),
                   *[pltpu.HBM(a.shape, a.dtype) for a in cs], *[pltpu.HBM(a.shape, a.dtype) for a in lands],
                   jax.ShapeDtypeStruct((8, 128), F32)),
        in_specs=[HBM] * (2 * nt), out_specs=(SEM, SEM, *[HBM] * (2 * nt), pl.BlockSpec(memory_space=pltpu.VMEM)),
        input_output_aliases={i: 2 + i for i in range(2 * nt)},
        compiler_params=pltpu.CompilerParams(has_side_effects=DATAFLOW),
    )(*[_in_hbm(a) for a in cs], *[_in_hbm(a) for a in lands])
    return outs[0], outs[1], list(outs[2:2 + nt]), list(outs[2 + nt:2 + 2 * nt]), outs[2 + 2 * nt]


def _to_chips_wait(send, recv, cs, lands, after, l):
    nt = len(cs)

    def body(*refs):
        c_refs, land_refs = refs[:nt], refs[nt:2 * nt]
        send_ref, recv_ref = refs[2 * nt], refs[2 * nt + 1]
        x, y, c, _ = _place()
        for t in range(nt):
            for d in (1, 2, 3):
                cp = _chip_copy(c_refs[t], land_refs[t], x, y, c, d, send_ref.at[3 * t + d - 1], recv_ref.at[3 * t + d - 1])
                cp.wait_send()
                cp.wait_recv()

    outs = pl.pallas_call(
        body, name=f"grads_to_chips_wait_{l}", out_shape=[pltpu.HBM(a.shape, a.dtype) for a in cs + lands],
        in_specs=[HBM] * (2 * nt) + [SEM, SEM, ANY], out_specs=[HBM] * (2 * nt),
        input_output_aliases={i: i for i in range(2 * nt)},
        compiler_params=pltpu.CompilerParams(has_side_effects=DATAFLOW),
    )(*cs, *lands, send, recv, after)
    return list(outs[:nt]), list(outs[nt:])


def _join_halves(fs, l):
    nt = len(fs)

    def body(*refs):
        o_refs = refs[nt:2 * nt]
        send, recv = refs[2 * nt:]
        x, y, c, _ = _place()
        cps = [pltpu.make_async_remote_copy(
            src_ref=o_refs[t].at[c], dst_ref=o_refs[t].at[c], send_sem=send.at[t], recv_sem=recv.at[t],
            device_id=(x, y, 1 - c), device_id_type=MESH) for t in range(nt)]
        for cp in cps:
            cp.start()
        for cp in cps:
            cp.wait()

    return pl.pallas_call(
        body, name=f"grads_join_{l}", out_shape=[jax.ShapeDtypeStruct(a.shape, a.dtype) for a in fs],
        in_specs=[ANY] * nt, out_specs=[ANY] * nt,
        scratch_shapes=[pltpu.SemaphoreType.DMA((nt,)), pltpu.SemaphoreType.DMA((nt,))],
        input_output_aliases={t: t for t in range(nt)},
        compiler_params=_cp(has_side_effects=True),
    )(*fs)


def _add_half(g, r, c_arr, name):
    _, _, rh, cols = g.shape

    def body(c_ref, g_ref, r_ref, o_ref):
        o_ref[...] = (g_ref[...] + r_ref[...]).astype(BF16)

    blk = pl.BlockSpec((None, rh, cols), lambda k, cr: (k, 0, 0))
    return pl.pallas_call(
        body, name=name, out_shape=jax.ShapeDtypeStruct((4, rh, cols), BF16),
        grid_spec=pltpu.PrefetchScalarGridSpec(
            num_scalar_prefetch=1, grid=(4,),
            in_specs=[pl.BlockSpec((None, None, rh, cols), lambda k, cr: (k, cr[0], 0, 0)), blk], out_specs=blk),
        compiler_params=_cp(("parallel",)),
    )(c_arr, g, r)


def _sum_chips(cs, r3, place_arr, name):
    _, rh, cols = cs.shape
    rb = rh // 2

    def body(pl_ref, a_ref, r0_ref, r1_ref, r2_ref, o_ref):
        up = lambda ref: ref[...].astype(F32)
        o_ref[...] = ((up(a_ref) + up(r0_ref)) + up(r1_ref)) + up(r2_ref)

    def slot(d):
        return pl.BlockSpec((None, rb, cols), lambda i, pa: (d, i, 0))

    return pl.pallas_call(
        body, name=name, out_shape=jax.ShapeDtypeStruct((2, rh, cols), F32),
        grid_spec=pltpu.PrefetchScalarGridSpec(
            num_scalar_prefetch=1, grid=(2,),
            in_specs=[pl.BlockSpec((None, rb, cols), lambda i, pa: (pa[0], i, 0)), slot(0), slot(1), slot(2)],
            out_specs=pl.BlockSpec((None, rb, cols), lambda i, pa: (pa[1], i, 0))),
        compiler_params=_cp(("parallel",)),
    )(place_arr, cs, r3, r3, r3)


def _allreduce_small(pack):
    rows = pack.shape[0]
    hr = rows // 2

    def body(p_ref, o_ref, sib, slots, s1, r1, s2, r2, s3, r3):
        x, y, c, chip = _place()
        sibling = (x, y, 1 - c)
        ex = pltpu.make_async_remote_copy(src_ref=p_ref, dst_ref=sib, send_sem=s1, recv_sem=r1,
                                          device_id=sibling, device_id_type=MESH)
        ex.start()
        ex.wait()
        half = pl.ds(pl.multiple_of(c * hr, 8), hr)
        slots[0] = p_ref[half, :] + sib[half, :]
        cps = []
        for d in (1, 2, 3):
            px, py, _ = _chip_at(x, y, d)
            cps.append(pltpu.make_async_remote_copy(
                src_ref=slots.at[0], dst_ref=slots.at[d], send_sem=s2.at[d - 1], recv_sem=r2.at[d - 1],
                device_id=(px, py, c), device_id_type=MESH))
        for cp in cps:
            cp.start()
        for cp in cps:
            cp.wait()
        tot = slots[chip]
        for k in (1, 2, 3):
            tot = tot + slots[jnp.bitwise_xor(chip, k)]
        o_ref[half, :] = tot
        back = pltpu.make_async_remote_copy(src_ref=o_ref.at[half, :], dst_ref=o_ref.at[half, :], send_sem=s3, recv_sem=r3,
                                            device_id=sibling, device_id_type=MESH)
        back.start()
        back.wait()

    vm = pl.BlockSpec(memory_space=pltpu.VMEM)
    return pl.pallas_call(
        body, name="allreduce_small", out_shape=jax.ShapeDtypeStruct((rows, 128), F32),
        in_specs=[vm], out_specs=vm,
        scratch_shapes=[pltpu.VMEM((rows, 128), F32), pltpu.VMEM((4, hr, 128), F32),
                        pltpu.SemaphoreType.DMA, pltpu.SemaphoreType.DMA, pltpu.SemaphoreType.DMA((3,)), pltpu.SemaphoreType.DMA((3,)),
                        pltpu.SemaphoreType.DMA, pltpu.SemaphoreType.DMA],
        compiler_params=_cp(has_side_effects=True),
    )(pack)


def _adam_math(gv, wv, mv, vv):
    m2 = ADAM_B1 * mv + (1.0 - ADAM_B1) * gv
    v2 = ADAM_B2 * vv + (1.0 - ADAM_B2) * (gv * gv)
    m_hat = m2 / (1.0 - ADAM_B1 ** ADAM_STEP)
    v_hat = v2 / (1.0 - ADAM_B2 ** ADAM_STEP)
    return -ADAM_LR * (m_hat / (jnp.sqrt(v_hat) + ADAM_EPS) + ADAM_WD * wv), m2, v2


def _adam(g, w, m, v, name):
    rows, cols = g.shape
    rb = rows // 4

    def body(g_ref, w_ref, m_ref, v_ref, d_ref, m2_ref, v2_ref):
        d_ref[...], m2_ref[...], v2_ref[...] = _adam_math(g_ref[...], w_ref[...], m_ref[...], v_ref[...])

    blk = pl.BlockSpec((rb, cols), lambda i: (i, 0))
    shp = jax.ShapeDtypeStruct((rows, cols), F32)
    return pl.pallas_call(
        body, name=name, grid=(4,), in_specs=[blk] * 4, out_specs=[blk] * 3, out_shape=[shp] * 3,
        compiler_params=_cp(("parallel",)),
    )(g, w, m, v)


def _adam_layer(g, w, m, v, l, prev, name):
    rows, cols = g.shape
    rb = rows // 4

    def body(g_ref, w_ref, m_ref, v_ref, *rest):
        go_ref, d_ref, m2_ref, v2_ref = rest[-4:]
        gv = g_ref[...]
        go_ref[...] = gv
        d_ref[...], m2_ref[...], v2_ref[...] = _adam_math(gv, w_ref[...], m_ref[...], v_ref[...])

    lay = pl.BlockSpec((None, rb, cols), lambda i: (l, i, 0))
    shp = jax.ShapeDtypeStruct((2, rows, cols), F32)
    prev = () if prev is None else tuple(prev)
    return pl.pallas_call(
        body, name=name, grid=(4,), in_specs=[pl.BlockSpec((rb, cols), lambda i: (i, 0)), lay, lay, lay] + [ANY] * len(prev),
        out_specs=[lay] * 4, out_shape=[shp] * 4,
        input_output_aliases={4 + j: j for j in range(len(prev))},
        compiler_params=_cp(("parallel",)),
    )(g, w, m, v, *prev)


def _rows128(a):
    return a.reshape(-1, 128)


def _pack(arrs, mult):
    parts = [_rows128(a) for a in arrs]
    rows = sum(q.shape[0] for q in parts)
    pad = -rows % mult
    if pad:
        parts.append(jnp.zeros((pad, 128), F32))
    return jnp.concatenate(parts, axis=0)


def _unpack(pack, shapes):
    out, o = [], 0
    for s in shapes:
        n = 1
        for e in s:
            n *= e
        out.append(pack[o:o + n // 128].reshape(s))
        o += n // 128
    return out


WEIGHTS = ['norm1_g', 'w_in', 'gmlp_ln_g', 'gmlp_ln_b', 'gmlp_w_s', 'gmlp_b_s', 'conv_w', 'conv_b', 'lru_w_r', 'lru_b_r', 'lru_w_i',
           'lru_b_i', 'lru_lambda', 'w_out', 'norm2_g', 'w_ffn_in', 'w_ffn_out', 'final_g']
BIG = ['w_in', 'w_out', 'w_ffn_in', 'w_ffn_out']
SMALL = [n for n in WEIGHTS if n not in BIG]
CHIP_SHARDED_SMALL = ['conv_w', 'lru_b_r', 'lru_b_i', 'lru_lambda']


def kernel(x, norm1_g, w_in, gmlp_ln_g, gmlp_ln_b, gmlp_w_s, gmlp_b_s, conv_w, conv_b, lru_w_r, lru_b_r, lru_w_i, lru_b_i, lru_lambda, w_out, norm2_g, w_ffn_in, w_ffn_out, final_g, loss_target, m_norm1_g, m_w_in, m_gmlp_ln_g, m_gmlp_ln_b, m_gmlp_w_s, m_gmlp_b_s, m_conv_w, m_conv_b, m_lru_w_r, m_lru_b_r, m_lru_w_i, m_lru_b_i, m_lru_lambda, m_w_out, m_norm2_g, m_w_ffn_in, m_w_ffn_out, m_final_g, v_norm1_g, v_w_in, v_gmlp_ln_g, v_gmlp_ln_b, v_gmlp_w_s, v_gmlp_b_s, v_conv_w, v_conv_b, v_lru_w_r, v_lru_b_r, v_lru_w_i, v_lru_b_i, v_lru_lambda, v_w_out, v_norm2_g, v_w_ffn_in, v_w_ffn_out, v_final_g):
    a = dict(locals())
    w = {n: a[n] for n in WEIGHTS}
    mom = {n: a["m_" + n] for n in WEIGHTS}
    var = {n: a["v_" + n] for n in WEIGHTS}
    _, _, c, chip = _place()
    c_arr, chip_arr = jnp.reshape(c, (1,)).astype(jnp.int32), jnp.reshape(chip, (1,)).astype(jnp.int32)
    place_arr = jnp.stack([chip, c]).astype(jnp.int32)

    first, rest = BIG[:1], BIG[1:]

    def as_weights(names, full):
        wb = {n: f.reshape(4, 2 * f.shape[2], f.shape[3]) for n, f in zip(names, full)}
        if "w_out" in wb:
            wb["w_out"] = wb["w_out"].reshape(D, D)
            wb["w_ffn_out"] = wb["w_ffn_out"].reshape(DFF, D)
        return wb

    bufs = [{n: _cast_into(w[n], l, chip_arr, f"cast_{n}_{l}") for n in BIG} for l in range(2)]
    tiny = _pack([w[n] for n in CHIP_SHARDED_SMALL], 8)
    w_in0, tiny_full = _gather_weights([bufs[0]["w_in"]], tiny)
    fly0 = _gather_start([bufs[0][n] for n in rest], "0", after=(tiny_full,))
    fly1 = _gather_start([bufs[1][n] for n in BIG], "1", after=(fly0[3],))
    p = {n: w[n] for n in SMALL}
    parts = [_unpack(tiny_full[k], [w[n].shape for n in CHIP_SHARDED_SMALL]) for k in range(4)]
    for i, n in enumerate(CHIP_SHARDED_SMALL):
        p[n] = jnp.concatenate([parts[k][i] for k in range(4)], axis=-1)

    def landed(fly, names, after, tag):
        return as_weights(names, _gather_pass_on(_gather_wait(fly[0], fly[1], fly[2], after, tag), tag))

    xa, saved0 = _forward_layer(0, x[0], p, as_weights(first, w_in0), after=(fly0[3], fly1[3]),
                                rest=lambda merged: landed(fly0, rest, merged, "0"))
    xb, saved1 = _forward_layer(1, xa, p, landed(fly1, BIG, xa, "1"))
    dxb, loss_v, dfg = _loss_head(xb, loss_target[0], p["final_g"][None])
    loss = lax.psum(loss_v[0, 0], ("x", "y", "c"))

    out, flying = {}, {}

    def reduce_start(grads, names, l, tag):
        gs = [g.reshape(4, 2, -1, g.shape[-1]) for g in grads]
        from_sib = _to_sibling_halves(gs, tag)
        cs = [_add_half(g, r, c_arr, f"add_half_{n}_{l}") for n, g, r in zip(names, gs, from_sib)]
        flying[tag] = (names, l) + tuple(_to_chips_start(cs, tag))
        return (flying[tag][-1],)

    def reduce_finish(tag, after):
        names, l, send, recv, cs, lands, _ = flying[tag]
        cs, lands = _to_chips_wait(send, recv, cs, lands, after, tag)
        ts = [_sum_chips(cc, r3, place_arr, f"sum_chips_{n}_{l}") for n, cc, r3 in zip(names, cs, lands)]
        for n, j in zip(names, _join_halves(ts, tag)):
            out[n] = _adam_layer(j.reshape(w[n].shape[1:]), w[n], mom[n], var[n], l, out.get(n), f"adam_{n}_{l}")
        return out[names[-1]][0]

    def midway0(grads):
        reduce_finish("1a", grads[0])
        reduce_finish("1b", grads[0])
        return reduce_start(grads, rest, 0, "0a")

    dxa, big1, small1 = _backward_layer(1, dxb, saved1, midway=lambda grads: reduce_start(grads, rest, 1, "1a"))
    dx, big0, small0 = _backward_layer(0, dxa, saved0, after=reduce_start(big1[:1], first, 1, "1b"), midway=midway0)
    reduce_start(big0[:1], first, 0, "0b")
    reduce_finish("0b", reduce_finish("0a", dx))
    small = {k: jnp.stack([small0[k], small1[k]]) for k in LAYER_SMALL}
    small["final_g"] = dfg[0]

    full_shapes = [small[n].shape for n in SMALL]
    red = _unpack(_allreduce_small(_pack([small[n] for n in SMALL], 16)), full_shapes)
    g_small = []
    for n, g in zip(SMALL, red):
        if n in CHIP_SHARDED_SMALL:
            g = lax.dynamic_slice_in_dim(g, chip * w[n].shape[-1], w[n].shape[-1], axis=g.ndim - 1)
        g_small.append(g)
    shapes = [w[n].shape for n in SMALL]
    packs = [_pack(lst, 32) for lst in (g_small, [w[n] for n in SMALL], [mom[n] for n in SMALL], [var[n] for n in SMALL])]
    upd = [_unpack(u, shapes) for u in _adam(*packs, "adam_small")]
    for i, n in enumerate(SMALL):
        out[n] = [g_small[i], upd[0][i], upd[1][i], upd[2][i]]

    return (loss, dx[None]) + tuple(out[n][i] for i in range(4) for n in WEIGHTS)
```

```python
import functools

import jax
import jax.numpy as jnp
from jax import lax
from jax.experimental import pallas as pl
from jax.experimental.pallas import tpu as pltpu

F32 = jnp.float32
BF16 = jnp.bfloat16
MESH = pl.DeviceIdType.MESH

D = 1024
NH = 8
HD = 128
CHUNK = 128
N_IN_T = 12
DFF = 2816
DFF_SH = 1408
EPS = 1e-6
LRU_C = 8.0
ADAM_LR, ADAM_B1, ADAM_B2, ADAM_EPS, ADAM_WD, ADAM_STEP = 0.001, 0.9, 0.999, 1e-08, 0.01, 10

TM = 512
TM_BIG = 1024
RT = 128
PADR = 8
VMEM_LIMIT = 56 * 1024 * 1024


def _cp(sem=None, **kw):
    if sem is not None:
        kw["dimension_semantics"] = sem
    return pltpu.CompilerParams(vmem_limit_bytes=VMEM_LIMIT, **kw)


_GC = 0.7978845608028654


def _sigmoid(x):
    return 1.0 / (1.0 + jnp.exp(-x))


def _gelu(x):
    return 0.5 * x * (1.0 + jnp.tanh(_GC * (x + 0.044715 * x * x * x)))


def _gelu_and_grad(x):
    t = jnp.tanh(_GC * (x + 0.044715 * x * x * x))
    g = 0.5 * x * (1.0 + t)
    dg = 0.5 * (1.0 + t) + 0.5 * x * (1.0 - t * t) * _GC * (1.0 + 3 * 0.044715 * x * x)
    return g, dg


def _softplus_neg(lam):
    y = jnp.exp(-jnp.abs(lam))
    u = 1.0 + y
    l1p = jnp.where(u == 1.0, y, jnp.log(u) * y / (u - 1.0))
    return jnp.maximum(-lam, 0.0) + l1p


def _dot(a, b):
    return jnp.dot(a, b, preferred_element_type=F32)


def _dot_nt(a, b):
    return lax.dot_general(a, b, (((1,), (1,)), ((), ())), preferred_element_type=F32)


def _dot_tn(a, b):
    return lax.dot_general(a, b, (((0,), (0,)), ((), ())), preferred_element_type=F32)


def _rms_hat(x):
    r = lax.rsqrt(jnp.mean(x * x, axis=-1, keepdims=True) + EPS)
    return x * r, r


def _rms_bwd(dh, x, g):
    xh, r = _rms_hat(x)
    dxh = dh * g
    dx = r * (dxh - xh * jnp.mean(dxh * xh, axis=-1, keepdims=True))
    return dx, jnp.sum(dh * xh, axis=0, keepdims=True)


def _in_tile(j):
    m, hf = j // 2, j % 2
    orig = jnp.where(m < 2, m, jnp.where(m == 2, 4, jnp.where(m < 5, m - 1, 5)))
    t = orig * 2 + hf
    return t // 3, t % 3


ANY = pl.BlockSpec(memory_space=pl.ANY)


def _mm_in(x, g, w_in, l, after=()):
    S = x.shape[0]
    tm = min(2 * TM_BIG if l else TM_BIG, S)

    def body(x_ref, g_ref, w_ref, *rest):
        o_ref, h_ref = rest[-2:]

        @pl.when(pl.program_id(1) == 0)
        def _():
            xh, _ = _rms_hat(x_ref[...])
            h_ref[...] = (xh * g_ref[...]).astype(BF16)
        o_ref[...] = _dot(h_ref[...], w_ref[...]).astype(BF16)

    def w_map(i, j):
        sh, tl = _in_tile(j)
        return (sh, 0, tl)

    return pl.pallas_call(
        body, name=f"mm_in_{l}", grid=(S // tm, N_IN_T),
        in_specs=[pl.BlockSpec((tm, D), lambda i, j: (i, 0)), pl.BlockSpec((1, D), lambda i, j: (0, 0)),
                  pl.BlockSpec((None, D, 512), w_map)] + [ANY] * len(after),
        out_specs=[pl.BlockSpec((None, tm, 512), lambda i, j: (j // 2, i, j % 2)), pl.BlockSpec((tm, D), lambda i, j: (i, 0))],
        out_shape=[jax.ShapeDtypeStruct((6, S, D), BF16), jax.ShapeDtypeStruct((S, D), BF16)],
        compiler_params=_cp(("parallel", "arbitrary")),
    )(x, g, w_in, *after)


def _mm_res(a, w, res, l, name):
    S, K = a.shape

    def body(a_ref, w_ref, r_ref, o_ref):
        o_ref[...] = r_ref[...] + _dot(a_ref[...], w_ref[...])

    return pl.pallas_call(
        body, name=f"{name}_{l}", grid=(S // TM,),
        in_specs=[pl.BlockSpec((TM, K), lambda i: (i, 0)), pl.BlockSpec((K, D), lambda i: (0, 0)),
                  pl.BlockSpec((TM, D), lambda i: (i, 0))],
        out_specs=pl.BlockSpec((TM, D), lambda i: (i, 0)),
        out_shape=jax.ShapeDtypeStruct((S, D), F32),
        compiler_params=_cp(("parallel",)),
    )(a, w, res)


def _mm_ffn_in(x, g, w_fi, l):
    S = x.shape[0]

    def body(x_ref, g_ref, w_ref, gu_ref, ff_ref, h_ref):
        @pl.when(pl.program_id(1) == 0)
        def _():
            xh, _ = _rms_hat(x_ref[...])
            h_ref[...] = (xh * g_ref[...]).astype(BF16)
        hv = h_ref[...]
        ga = _dot(hv, w_ref[0])
        gb = _dot(hv, w_ref[1])
        gu_ref[0] = ga.astype(BF16)
        gu_ref[1] = gb.astype(BF16)
        ff_ref[...] = (ga * _sigmoid(ga) * gb).astype(BF16)

    gu, ff, h = pl.pallas_call(
        body, name=f"mm_ffn_in_{l}", grid=(S // TM, 2),
        in_specs=[pl.BlockSpec((TM, D), lambda i, s: (i, 0)), pl.BlockSpec((1, D), lambda i, s: (0, 0)),
                  pl.BlockSpec((2, None, D, DFF_SH), lambda i, s: (0, s, 0, 0))],
        out_specs=[pl.BlockSpec((2, None, TM, DFF_SH), lambda i, s: (0, s, i, 0)),
                   pl.BlockSpec((TM, DFF_SH), lambda i, s: (i, s)),
                   pl.BlockSpec((TM, D), lambda i, s: (i, 0))],
        out_shape=[jax.ShapeDtypeStruct((2, 2, S, DFF_SH), BF16), jax.ShapeDtypeStruct((S, DFF), BF16),
                   jax.ShapeDtypeStruct((S, D), BF16)],
        compiler_params=_cp(("parallel", "arbitrary")),
    )(x, g, w_fi.reshape(2, 2, D, DFF_SH))
    return gu.reshape(4, S, DFF_SH), ff, h


def _gmlp_fwd(z6, ws_b, bs_b, lg, lb):
    S = z6.shape[1]

    def body(z_ref, ws_ref, bs_ref, lg_ref, lb_ref, o_ref, mix):
        gv = _gelu(z_ref[1].astype(F32))
        xc = gv - jnp.mean(gv, axis=-1, keepdims=True)
        rs = lax.rsqrt(jnp.mean(xc * xc, axis=-1, keepdims=True) + EPS)
        vb = (xc * rs * lg_ref[...] + lb_ref[...]).astype(BF16)
        for gi in range(NH):
            cs = slice(gi * HD, (gi + 1) * HD)
            mix[:, cs] = _dot(ws_ref[gi], vb[:, cs])
        o_ref[...] = (_sigmoid(z_ref[2].astype(F32)) * _gelu(z_ref[0].astype(F32)) * (mix[...] + bs_ref[...])).astype(BF16)

    return pl.pallas_call(
        body, name="gmlp_fwd", grid=(S // CHUNK,),
        in_specs=[pl.BlockSpec((3, CHUNK, D), lambda i: (0, i, 0)), pl.BlockSpec((NH, CHUNK, CHUNK), lambda i: (0, 0, 0)),
                  pl.BlockSpec((CHUNK, D), lambda i: (0, 0)), pl.BlockSpec((1, D), lambda i: (0, 0)),
                  pl.BlockSpec((1, D), lambda i: (0, 0))],
        out_specs=pl.BlockSpec((CHUNK, D), lambda i: (i, 0)),
        out_shape=jax.ShapeDtypeStruct((S, D), BF16),
        scratch_shapes=[pltpu.VMEM((CHUNK, D), F32)],
        compiler_params=_cp(("parallel",)),
    )(z6, ws_b, bs_b, lg, lb)


def _row_iota():
    return lax.broadcasted_iota(jnp.int32, (RT, HD), 0)


SUB = 8
UNROLL = 4
GRAD_ROWS = 256


def _scan_up(a, b, carry):
    row = lax.broadcasted_iota(jnp.int32, (SUB, HD), 0)
    masks = [(d, row >= d) for d in (1, 2, 4)]
    c = jnp.broadcast_to(carry, (SUB, HD))
    hs = []
    for j in range(RT // SUB):
        aj, bj = a[SUB * j:SUB * (j + 1)], b[SUB * j:SUB * (j + 1)]
        for d, m in masks:
            bj = bj + aj * jnp.where(m, pltpu.roll(bj, d, 0), 0.0)
            aj = aj * jnp.where(m, pltpu.roll(aj, d, 0), 1.0)
        h = bj + aj * c
        hs.append(h)
        c = jnp.broadcast_to(h[SUB - 1:SUB, :], (SUB, HD))
    return jnp.concatenate(hs, axis=0), hs[-1][SUB - 1:SUB, :]


def _scan_down(a, b, carry):
    row = lax.broadcasted_iota(jnp.int32, (SUB, HD), 0)
    masks = [(d, row < SUB - d) for d in (1, 2, 4)]
    c = jnp.broadcast_to(carry, (SUB, HD))
    hs = []
    for j in reversed(range(RT // SUB)):
        aj, bj = a[SUB * j:SUB * (j + 1)], b[SUB * j:SUB * (j + 1)]
        for d, m in masks:
            bj = bj + aj * jnp.where(m, pltpu.roll(bj, SUB - d, 0), 0.0)
            aj = aj * jnp.where(m, pltpu.roll(aj, SUB - d, 0), 1.0)
        h = bj + aj * c
        hs.append(h)
        c = jnp.broadcast_to(h[0:1, :], (SUB, HD))
    return jnp.concatenate(hs[::-1], axis=0), hs[-1][0:1, :]


def _decay(r, sp_d):
    log_a = -LRU_C * r * sp_d
    a = jnp.exp(log_a)
    return a, jnp.sqrt(jnp.maximum(-jnp.tanh(log_a) * (a * a + 1.0), 0.0))


def _lru_gates(xc, d, wr_ref, br_ref, wi_ref, bi_ref, sp):
    xb = xc.astype(BF16)
    r = _sigmoid(_dot(xb, wr_ref[d]) + br_ref[d:d + 1, :])
    i = _sigmoid(_dot(xb, wi_ref[d]) + bi_ref[d:d + 1, :])
    a, mult = _decay(r, sp[d:d + 1, :])
    return r, i, a, mult


def _shifted(win, k):
    w = RT + 2 * PADR
    v = win if k == 0 else pltpu.roll(win, (-k) % w, 0)
    return v[PADR:PADR + RT]


def _conv_taps(win):
    return [_shifted(win, k) for k in (-1, 0, 1, 2)]


def _fill_padded(dst, src_ref, S):
    zeros = jnp.zeros((PADR, HD), F32)
    dst[0:PADR, :] = zeros
    dst[PADR + S:2 * PADR + S, :] = zeros

    def cp(i, c):
        t0 = pl.multiple_of(i * RT, RT)
        dst[pl.ds(t0 + PADR, RT), :] = src_ref[pl.ds(t0, RT), :].astype(F32)
        return c
    lax.fori_loop(0, S // RT, cp, 0)


def _conv_fwd_all(zxp, xc_s, cw_ref, cb_ref, S):
    def cv(i, c):
        t0 = pl.multiple_of(i * RT, RT)
        xm1, x0, xp1, xp2 = _conv_taps(zxp[pl.ds(t0, RT + 2 * PADR), :])
        xc_s[pl.ds(t0, RT), :] = (cb_ref[...] + xm1 * cw_ref[0:1, :] + x0 * cw_ref[1:2, :]
                                  + xp1 * cw_ref[2:3, :] + xp2 * cw_ref[3:4, :])
        return c
    lax.fori_loop(0, S // RT, cv, 0)


def _lru_specs(S):
    head = lambda h: (0, h)
    return [pl.BlockSpec((4, HD), head), pl.BlockSpec((1, HD), head),
            pl.BlockSpec((2, None, HD, HD), lambda h: (0, h, 0, 0)), pl.BlockSpec((2, HD), head),
            pl.BlockSpec((2, None, HD, HD), lambda h: (0, h, 0, 0)), pl.BlockSpec((2, HD), head),
            pl.BlockSpec((2, HD), head)]


def _lru_fwd(z6, ya, cw, cb, wr, br, wi, bi, lam):
    S = z6.shape[1]
    nt = S // RT

    def body(z_ref, ya_ref, cw_ref, cb_ref, wr_ref, br_ref, wi_ref, bi_ref, lam_ref, mg_ref, h0_ref, h1_ref, zxp, xc_s):
        sp = _softplus_neg(lam_ref[...])
        _fill_padded(zxp, z_ref.at[0], S)
        _conv_fwd_all(zxp, xc_s, cw_ref, cb_ref, S)

        def scans(i, carry):
            cu, cd = carry
            for u in range(UNROLL):
                j = i * UNROLL + u
                ru = pl.ds(pl.multiple_of(j * RT, RT), RT)
                rd = pl.ds(pl.multiple_of((nt - 1 - j) * RT, RT), RT)
                xu, xd = xc_s[ru, :], xc_s[rd, :]
                _, gi, a, mult = _lru_gates(xu, 0, wr_ref, br_ref, wi_ref, bi_ref, sp)
                hu, cu = _scan_up(a, mult * gi * xu, cu)
                h0_ref[ru, :] = hu
                _, gi, a, mult = _lru_gates(xd, 1, wr_ref, br_ref, wi_ref, bi_ref, sp)
                hd, cd = _scan_down(a, mult * gi * xd, cd)
                h1_ref[rd, :] = hd
            return cu, cd
        z1 = jnp.zeros((1, HD), F32)
        lax.fori_loop(0, nt // UNROLL, scans, (z1, z1))

        def merge(i, c):
            rows = pl.ds(pl.multiple_of(i * RT, RT), RT)
            yb = (h0_ref[rows, :] + h1_ref[rows, :]) * _gelu(z_ref[1, rows, :].astype(F32))
            mg_ref[rows, :] = (ya_ref[rows, :].astype(F32) + _sigmoid(z_ref[2, rows, :].astype(F32)) * yb).astype(BF16)
            return c
        lax.fori_loop(0, nt, merge, 0)

    col = pl.BlockSpec((S, HD), lambda h: (0, h))
    return pl.pallas_call(
        body, name="lru_fwd", grid=(NH,),
        in_specs=[pl.BlockSpec((3, S, HD), lambda h: (1, 0, h)), col] + _lru_specs(S),
        out_specs=[col, col, col],
        out_shape=[jax.ShapeDtypeStruct((S, D), BF16), jax.ShapeDtypeStruct((S, D), F32), jax.ShapeDtypeStruct((S, D), F32)],
        scratch_shapes=[pltpu.VMEM((S + 2 * PADR, HD), F32), pltpu.VMEM((S, HD), F32)],
        compiler_params=_cp(("parallel",)),
    )(z6, ya, cw, cb, wr, br, wi, bi, lam)


def _loss_head(x, tgt, g):
    S = x.shape[0]

    def body(x_ref, t_ref, g_ref, dx_ref, loss_ref, dg_ref):
        @pl.when(pl.program_id(0) == 0)
        def _():
            loss_ref[...] = jnp.zeros_like(loss_ref)
            dg_ref[...] = jnp.zeros_like(dg_ref)
        xv = x_ref[...]
        xh, _ = _rms_hat(xv)
        e = xh * g_ref[...] - t_ref[...]
        loss_ref[...] += jnp.sum(e * e) * (0.5 / D)
        dx, dgs = _rms_bwd(e * (1.0 / D), xv, g_ref[...])
        dx_ref[...] = dx
        dg_ref[...] += dgs

    return pl.pallas_call(
        body, name="loss_head", grid=(S // TM,),
        in_specs=[pl.BlockSpec((TM, D), lambda i: (i, 0)), pl.BlockSpec((TM, D), lambda i: (i, 0)),
                  pl.BlockSpec((1, D), lambda i: (0, 0))],
        out_specs=[pl.BlockSpec((TM, D), lambda i: (i, 0)), pl.BlockSpec((1, 128), lambda i: (0, 0)),
                   pl.BlockSpec((1, D), lambda i: (0, 0))],
        out_shape=[jax.ShapeDtypeStruct((S, D), F32), jax.ShapeDtypeStruct((1, 128), F32), jax.ShapeDtypeStruct((1, D), F32)],
        compiler_params=_cp(("arbitrary",)),
    )(x, tgt, g)


def _bwd_ffn_out(dx, w_fo, gu, l, after=()):
    S = dx.shape[0]

    def body(dx_ref, w_ref, gu_ref, *rest):
        o_ref = rest[-1]
        d = _dot_nt(dx_ref[...].astype(BF16), w_ref[...])
        ga, gb = gu_ref[0].astype(F32), gu_ref[1].astype(F32)
        sg = _sigmoid(ga)
        o_ref[0] = (d * gb * sg * (1.0 + ga * (1.0 - sg))).astype(BF16)
        o_ref[1] = (d * ga * sg).astype(BF16)

    pair = pl.BlockSpec((2, None, TM, DFF_SH), lambda i, s: (0, s, i, 0))
    dgu = pl.pallas_call(
        body, name=f"bwd_ffn_out_{l}", grid=(S // TM, 2),
        in_specs=[pl.BlockSpec((TM, D), lambda i, s: (i, 0)), pl.BlockSpec((DFF_SH, D), lambda i, s: (s, 0)), pair]
        + [ANY] * len(after),
        out_specs=pair,
        out_shape=jax.ShapeDtypeStruct((2, 2, S, DFF_SH), BF16),
        compiler_params=_cp(("parallel", "arbitrary")),
    )(dx, w_fo, gu.reshape(2, 2, S, DFF_SH), *after)
    return dgu.reshape(4, S, DFF_SH)


def _mm_tn(a, b, m_blk, tk, name):
    S, M = a.shape

    def body(a_ref, b_ref, o_ref):
        @pl.when(pl.program_id(1) == 0)
        def _():
            o_ref[...] = jnp.zeros_like(o_ref)
        o_ref[...] += _dot_tn(a_ref[...], b_ref[...].astype(BF16))

    return pl.pallas_call(
        body, name=name, grid=(M // m_blk, S // tk),
        in_specs=[pl.BlockSpec((tk, m_blk), lambda m, k: (k, m)), pl.BlockSpec((tk, D), lambda m, k: (k, 0))],
        out_specs=pl.BlockSpec((m_blk, D), lambda m, k: (m, 0)),
        out_shape=jax.ShapeDtypeStruct((M, D), F32),
        compiler_params=_cp(("parallel", "arbitrary")),
    )(a, b)


def _mm_nt_rms_bwd(a, a_spec, w, w_spec, nk, tm, x, g, dres, name):
    S = x.shape[0]

    def body(a_ref, w_ref, x_ref, g_ref, r_ref, dx_ref, dg_ref, acc):
        i, k = pl.program_id(0), pl.program_id(1)

        @pl.when(k == 0)
        def _():
            acc[...] = jnp.zeros_like(acc)
        acc[...] += _dot_nt(a_ref[...], w_ref[...])

        @pl.when(jnp.logical_and(i == 0, k == 0))
        def _():
            dg_ref[...] = jnp.zeros_like(dg_ref)

        @pl.when(k == nk - 1)
        def _():
            dx, dgs = _rms_bwd(acc[...], x_ref[...], g_ref[...])
            dx_ref[...] = r_ref[...] + dx
            dg_ref[...] += dgs

    row = pl.BlockSpec((tm, D), lambda i, k: (i, 0))
    vec = pl.BlockSpec((1, D), lambda i, k: (0, 0))
    return pl.pallas_call(
        body, name=name, grid=(S // tm, nk),
        in_specs=[a_spec, w_spec, row, vec, row],
        out_specs=[row, vec],
        out_shape=[jax.ShapeDtypeStruct((S, D), F32), jax.ShapeDtypeStruct((1, D), F32)],
        scratch_shapes=[pltpu.VMEM((tm, D), F32)],
        compiler_params=_cp(("arbitrary", "arbitrary")),
    )(a, w, x, g, dres)


def _dw_ffn_in(h, dgu, l):
    S = h.shape[0]

    def body(h_ref, b_ref, o_ref):
        @pl.when(pl.program_id(1) == 0)
        def _():
            o_ref[...] = jnp.zeros_like(o_ref)
        o_ref[...] += _dot_tn(h_ref[...], b_ref[...])

    tk = min(TM_BIG, S) if l else TM
    return pl.pallas_call(
        body, name=f"dw_ffn_in_{l}", grid=(4, S // tk),
        in_specs=[pl.BlockSpec((tk, D), lambda j, k: (k, 0)), pl.BlockSpec((None, tk, DFF_SH), lambda j, k: (j, k, 0))],
        out_specs=pl.BlockSpec((None, D, DFF_SH), lambda j, k: (j, 0, 0)),
        out_shape=jax.ShapeDtypeStruct((4, D, DFF_SH), F32),
        compiler_params=_cp(("parallel", "arbitrary")),
    )(h, dgu)


_HALF_COMPS = ((0, 1, 3), (4, 2, 5))


def _dw_in(h, dz6, l):
    S = h.shape[0]

    def body(h_ref, d0_ref, d1_ref, d2_ref, o_ref):
        @pl.when(pl.program_id(1) == 0)
        def _():
            o_ref[...] = jnp.zeros_like(o_ref)
        hv = h_ref[...]
        for q, d_ref in enumerate((d0_ref, d1_ref, d2_ref)):
            for hf in range(2):
                col = 1024 * q + 512 * hf
                o_ref[col // 1536, :, col % 1536:col % 1536 + 512] += _dot_tn(hv, d_ref[:, 512 * hf:512 * (hf + 1)])

    def comp(q):
        return pl.BlockSpec((None, TM, D), lambda p, k: (jnp.where(p == 0, _HALF_COMPS[0][q], _HALF_COMPS[1][q]), k, 0))

    return pl.pallas_call(
        body, name=f"dw_in_{l}", grid=(2, S // TM),
        in_specs=[pl.BlockSpec((TM, D), lambda p, k: (k, 0)), comp(0), comp(1), comp(2)],
        out_specs=pl.BlockSpec((2, D, 1536), lambda p, k: (p, 0, 0)),
        out_shape=jax.ShapeDtypeStruct((4, D, 1536), F32),
        compiler_params=_cp(("parallel", "arbitrary")),
    )(h, dz6, dz6, dz6)


def _bwd_out(dx, w_o, merged, l):
    S = dx.shape[0]

    def body(dx_ref, w_ref, m_ref, dm_ref, dw_ref):
        @pl.when(pl.program_id(0) == 0)
        def _():
            dw_ref[...] = jnp.zeros_like(dw_ref)
        dxb = dx_ref[...].astype(BF16)
        dm_ref[...] = _dot_nt(dxb, w_ref[...]).astype(BF16)
        dw_ref[...] += _dot_tn(m_ref[...], dxb)

    row = pl.BlockSpec((TM, D), lambda i: (i, 0))
    return pl.pallas_call(
        body, name=f"bwd_out_{l}", grid=(S // TM,),
        in_specs=[row, pl.BlockSpec((D, D), lambda i: (0, 0)), row],
        out_specs=[row, pl.BlockSpec((D, D), lambda i: (0, 0))],
        out_shape=[jax.ShapeDtypeStruct((S, D), BF16), jax.ShapeDtypeStruct((D, D), F32)],
        compiler_params=_cp(("arbitrary",)),
    )(dx, w_o, merged)


def _gmlp_bwd(dm, z6, ws_b, wst_b, bs_b, lg, lb, after=()):
    S = z6.shape[1]

    def body(dm_ref, z_ref, ws_ref, wst_ref, bs_ref, lg_ref, lb_ref, *rest):
        dz_ref, dws_ref, dbs_ref, dlg_ref, dlb_ref, mix, dv = rest[-7:]

        @pl.when(pl.program_id(0) == 0)
        def _():
            dws_ref[...] = jnp.zeros_like(dws_ref)
            dbs_ref[...] = jnp.zeros_like(dbs_ref)
            dlg_ref[...] = jnp.zeros_like(dlg_ref)
            dlb_ref[...] = jnp.zeros_like(dlb_ref)
        gv, dgelu_v = _gelu_and_grad(z_ref[1].astype(F32))
        xc = gv - jnp.mean(gv, axis=-1, keepdims=True)
        rs = lax.rsqrt(jnp.mean(xc * xc, axis=-1, keepdims=True) + EPS)
        vh = xc * rs
        vb = (vh * lg_ref[...] + lb_ref[...]).astype(BF16)
        for gi in range(NH):
            cs = slice(gi * HD, (gi + 1) * HD)
            mix[:, cs] = _dot(ws_ref[gi], vb[:, cs])
        u, dgelu_u = _gelu_and_grad(z_ref[0].astype(F32))
        sa = _sigmoid(z_ref[2].astype(F32))
        mixed = mix[...] + bs_ref[...]
        dyg = dm_ref[...].astype(F32)
        dz_ref[2] = (dyg * u * mixed * sa * (1.0 - sa)).astype(BF16)
        dya = dyg * sa
        dz_ref[0] = (dya * mixed * dgelu_u).astype(BF16)
        dmix = dya * u
        dmb = dmix.astype(BF16)
        for gi in range(NH):
            cs = slice(gi * HD, (gi + 1) * HD)
            dv[:, cs] = _dot(wst_ref[gi], dmb[:, cs])
            dws_ref[gi] += _dot_nt(dmb[:, cs], vb[:, cs])
            dbs_ref[gi] += jnp.broadcast_to(jnp.sum(dmix[:, cs], axis=1, keepdims=True), (CHUNK, HD))
        dvv = dv[...]
        dlg_ref[...] += jnp.sum(dvv * vh, axis=0, keepdims=True)
        dlb_ref[...] += jnp.sum(dvv, axis=0, keepdims=True)
        dvh = dvv * lg_ref[...]
        dgv = rs * (dvh - jnp.mean(dvh, axis=-1, keepdims=True) - vh * jnp.mean(dvh * vh, axis=-1, keepdims=True))
        dz_ref[1] = (dgv * dgelu_v).astype(BF16)

    vec = pl.BlockSpec((1, D), lambda i: (0, 0))
    mat = pl.BlockSpec((NH, CHUNK, CHUNK), lambda i: (0, 0, 0))
    return pl.pallas_call(
        body, name="gmlp_bwd", grid=(S // CHUNK,),
        in_specs=[pl.BlockSpec((CHUNK, D), lambda i: (i, 0)), pl.BlockSpec((3, CHUNK, D), lambda i: (0, i, 0)), mat, mat,
                  pl.BlockSpec((CHUNK, D), lambda i: (0, 0)), vec, vec] + [ANY] * len(after),
        out_specs=[pl.BlockSpec((3, CHUNK, D), lambda i: (0, i, 0)), mat, mat, vec, vec],
        out_shape=[jax.ShapeDtypeStruct((6, S, D), BF16), jax.ShapeDtypeStruct((NH, CHUNK, CHUNK), F32),
                   jax.ShapeDtypeStruct((NH, CHUNK, HD), F32), jax.ShapeDtypeStruct((1, D), F32), jax.ShapeDtypeStruct((1, D), F32)],
        scratch_shapes=[pltpu.VMEM((CHUNK, D), F32), pltpu.VMEM((CHUNK, D), F32)],
        compiler_params=_cp(("arbitrary",)),
    )(dm, z6, ws_b, wst_b, bs_b, lg, lb, *after)


def _lru_bwd(dz6, dm, z6, h0, h1, cw, cb, wr, br, wi, bi, lam):
    S = z6.shape[1]
    nt = S // RT

    def body(dz_in, dm_ref, z_ref, h0_ref, h1_ref, cw_ref, cb_ref, wr_ref, br_ref, wi_ref, bi_ref, lam_ref,
             dz_ref, dcw_ref, dcb_ref, dwr_ref, dbr_ref, dwi_ref, dbi_ref, dlam_ref, zxp, xc_s, dhs_s, dxcp, r_s, lam_s):
        del dz_in
        lam = lam_ref[...]
        sp = _softplus_neg(lam)
        row = _row_iota()
        _fill_padded(zxp, z_ref.at[0], S)
        _conv_fwd_all(zxp, xc_s, cw_ref, cb_ref, S)
        zeros = jnp.zeros((PADR, HD), F32)
        dxcp[0:PADR, :] = zeros
        dxcp[PADR + S:2 * PADR + S, :] = zeros
        dwr_ref[...] = jnp.zeros_like(dwr_ref)
        dwi_ref[...] = jnp.zeros_like(dwi_ref)

        def pre(i, c):
            rows = pl.ds(pl.multiple_of(i * RT, RT), RT)
            hs = h0_ref[rows, :] + h1_ref[rows, :]
            dmv = dm_ref[rows, :].astype(F32)
            sb = _sigmoid(z_ref[2, rows, :].astype(F32))
            gg, dgg = _gelu_and_grad(z_ref[1, rows, :].astype(F32))
            dz_ref[2, rows, :] = (dmv * hs * gg * sb * (1.0 - sb)).astype(BF16)
            dyb = dmv * sb
            dz_ref[1, rows, :] = (dyb * hs * dgg).astype(BF16)
            dhs_s[rows, :] = dyb * gg
            return c
        lax.fori_loop(0, nt, pre, 0)

        def gate_bwd(d, gates, lamv, da, xc):
            r, gi, a, mult = gates
            dmult = lamv * gi * xc
            dgi = lamv * mult * xc
            dlog = (da - dmult * a / mult) * a
            dpr = (dlog * (-LRU_C) * sp[d:d + 1, :]) * r * (1.0 - r)
            dpi = dgi * gi * (1.0 - gi)
            xb, dprb, dpib = xc.astype(BF16), dpr.astype(BF16), dpi.astype(BF16)
            dwr_ref[d] += _dot_tn(xb, dprb)
            dwi_ref[d] += _dot_tn(xb, dpib)
            dxc = lamv * mult * gi + _dot_nt(dprb, wr_ref[d]) + _dot_nt(dpib, wi_ref[d])
            return dxc, (jnp.sum(dlog * r, axis=0, keepdims=True) * (-LRU_C), jnp.sum(dpr, axis=0, keepdims=True),
                         jnp.sum(dpi, axis=0, keepdims=True))

        def rgates(i, c):
            for u in range(UNROLL):
                rows = pl.ds(pl.multiple_of((i * UNROLL + u) * RT, RT), RT)
                xb = xc_s[rows, :].astype(BF16)
                for d in range(2):
                    r_s[d, rows, :] = _sigmoid(_dot(xb, wr_ref[d]) + br_ref[d:d + 1, :])
            return c
        lax.fori_loop(0, nt // UNROLL, rgates, 0)

        def chains(i, carry):
            qn, qp = carry
            for u in range(UNROLL):
                j = i * UNROLL + u
                rd = pl.ds(pl.multiple_of((nt - 1 - j) * RT, RT), RT)
                a, dhs = _decay(r_s[0, rd, :], sp[0:1, :])[0], dhs_s[rd, :]
                q, q_first = _scan_down(a, a * dhs, qn)
                lam_s[0, rd, :] = dhs + jnp.where(row == RT - 1, qn, pltpu.roll(q, RT - 1, 0))
                qn = q_first
                ru = pl.ds(pl.multiple_of(j * RT, RT), RT)
                a, dhs = _decay(r_s[1, ru, :], sp[1:2, :])[0], dhs_s[ru, :]
                q, q_last = _scan_up(a, a * dhs, qp)
                lam_s[1, ru, :] = dhs + jnp.where(row == 0, qp, pltpu.roll(q, 1, 0))
                qp = q_last
            return qn, qp

        z1 = jnp.zeros((1, HD), F32)
        lax.fori_loop(0, nt // UNROLL, chains, (z1, z1))

        ct = min(GRAD_ROWS, S)
        crow = lax.broadcasted_iota(jnp.int32, (ct, HD), 0)

        def tile_grads(i, acc):
            t0 = pl.multiple_of(i * ct, ct)
            rows = pl.ds(t0, ct)
            xc = xc_s[rows, :]
            xb = xc.astype(BF16)
            tp = pl.multiple_of(jnp.maximum(t0 - PADR, 0), PADR)
            prev = jnp.where(t0 > 0, h0_ref[pl.ds(tp, PADR), :][PADR - 1:PADR, :], 0.0)
            tn = pl.multiple_of(jnp.minimum(t0 + ct, S - PADR), PADR)
            nxt = jnp.where(t0 + ct < S, h1_ref[pl.ds(tn, PADR), :][0:1, :], 0.0)
            hside = (jnp.where(crow == 0, prev, pltpu.roll(h0_ref[rows, :], 1, 0)),
                     jnp.where(crow == ct - 1, nxt, pltpu.roll(h1_ref[rows, :], ct - 1, 0)))
            dxc, sums = 0.0, ()
            for d in range(2):
                r = r_s[d, rows, :]
                gi = _sigmoid(_dot(xb, wi_ref[d]) + bi_ref[d:d + 1, :])
                a, mult = _decay(r, sp[d:d + 1, :])
                lamv = lam_s[d, rows, :]
                dxc_d, s_d = gate_bwd(d, (r, gi, a, mult), lamv, lamv * hside[d], xc)
                dxc = dxc + dxc_d
                sums = sums + s_d
            dxcp[pl.ds(t0 + PADR, ct), :] = dxc
            return tuple(x + y for x, y in zip(acc, sums))

        s_sp0, s_br0, s_bi0, s_sp1, s_br1, s_bi1 = lax.fori_loop(0, S // ct, tile_grads, (z1,) * 6)

        dsp = jnp.concatenate([s_sp0, s_sp1], axis=0)
        dlam_ref[...] = -dsp * _sigmoid(-lam)
        dbr_ref[...] = jnp.concatenate([s_br0, s_br1], axis=0)
        dbi_ref[...] = jnp.concatenate([s_bi0, s_bi1], axis=0)

        def conv_bwd(i, carry):
            c0, c1, c2, c3, cb_ = carry
            t0 = pl.multiple_of(i * RT, RT)
            dwin = dxcp[pl.ds(t0, RT + 2 * PADR), :]
            d0 = _shifted(dwin, 0)
            dz_ref[0, pl.ds(t0, RT), :] = (_shifted(dwin, 1) * cw_ref[0:1, :] + d0 * cw_ref[1:2, :]
                                           + _shifted(dwin, -1) * cw_ref[2:3, :] + _shifted(dwin, -2) * cw_ref[3:4, :]).astype(BF16)
            xm1, x0, xp1, xp2 = _conv_taps(zxp[pl.ds(t0, RT + 2 * PADR), :])
            sm = lambda v: jnp.sum(v, axis=0, keepdims=True)
            return c0 + sm(d0 * xm1), c1 + sm(d0 * x0), c2 + sm(d0 * xp1), c3 + sm(d0 * xp2), cb_ + sm(d0)

        c0, c1, c2, c3, cb_ = lax.fori_loop(0, nt, conv_bwd, (z1, z1, z1, z1, z1))
        dcw_ref[...] = jnp.concatenate([c0, c1, c2, c3], axis=0)
        dcb_ref[...] = cb_

    col = pl.BlockSpec((S, HD), lambda h: (0, h))
    head = lambda h: (0, h)
    wspec = pl.BlockSpec((2, None, HD, HD), lambda h: (0, h, 0, 0))
    return pl.pallas_call(
        body, name="lru_bwd", grid=(NH,),
        in_specs=[pl.BlockSpec(memory_space=pl.ANY), col, pl.BlockSpec((3, S, HD), lambda h: (1, 0, h)), col, col] + _lru_specs(S),
        out_specs=[pl.BlockSpec((3, S, HD), lambda h: (1, 0, h)), pl.BlockSpec((4, HD), head), pl.BlockSpec((1, HD), head),
                   wspec, pl.BlockSpec((2, HD), head), wspec, pl.BlockSpec((2, HD), head), pl.BlockSpec((2, HD), head)],
        out_shape=[jax.ShapeDtypeStruct((6, S, D), BF16), jax.ShapeDtypeStruct((4, D), F32), jax.ShapeDtypeStruct((1, D), F32),
                   jax.ShapeDtypeStruct((2, NH, HD, HD), F32), jax.ShapeDtypeStruct((2, D), F32),
                   jax.ShapeDtypeStruct((2, NH, HD, HD), F32), jax.ShapeDtypeStruct((2, D), F32), jax.ShapeDtypeStruct((2, D), F32)],
        scratch_shapes=[pltpu.VMEM((S + 2 * PADR, HD), F32), pltpu.VMEM((S, HD), F32), pltpu.VMEM((S, HD), F32),
                        pltpu.VMEM((S + 2 * PADR, HD), F32), pltpu.VMEM((2, S, HD), F32), pltpu.VMEM((2, S, HD), F32)],
        input_output_aliases={0: 0},
        compiler_params=_cp(("parallel",)),
    )(dz6, dm, z6, h0, h1, cw, cb, wr, br, wi, bi, lam)


LAYER_SMALL = ("norm1_g", "gmlp_ln_g", "gmlp_ln_b", "gmlp_w_s", "gmlp_b_s", "conv_w", "conv_b",
               "lru_w_r", "lru_b_r", "lru_w_i", "lru_b_i", "lru_lambda", "norm2_g")


def _forward_layer(l, x, p, wb, after=(), rest=None):
    g1, g2 = p["norm1_g"][l][None], p["norm2_g"][l][None]
    ws_b = p["gmlp_w_s"][l].astype(BF16)
    tm = dict(ws_b=ws_b, wst_b=jnp.swapaxes(ws_b, 1, 2), bs_b=jnp.repeat(p["gmlp_b_s"][l].T, HD, axis=1),
              lg=p["gmlp_ln_g"][l][None], lb=p["gmlp_ln_b"][l][None])
    lru = (p["conv_w"][l], p["conv_b"][l][None], p["lru_w_r"][l].astype(BF16), p["lru_b_r"][l],
           p["lru_w_i"][l].astype(BF16), p["lru_b_i"][l], p["lru_lambda"][l])
    z6, hn1 = _mm_in(x, g1, wb["w_in"], l, after)
    ya = _gmlp_fwd(z6, tm["ws_b"], tm["bs_b"], tm["lg"], tm["lb"])
    merged, h0, h1 = _lru_fwd(z6, ya, *lru)
    if rest is not None:
        wb = dict(wb, **rest(merged))
    x1 = _mm_res(merged, wb["w_out"], x, l, "mm_out")
    gu, ff, hn2 = _mm_ffn_in(x1, g2, wb["w_ffn_in"], l)
    x2 = _mm_res(ff, wb["w_ffn_out"], x1, l, "mm_ffn_out")
    return x2, dict(x=x, z6=z6, h0=h0, h1=h1, merged=merged, x1=x1, gu=gu, ff=ff, g1=g1, g2=g2, tm=tm, lru=lru,
                    hn1=hn1, hn2=hn2, wb=wb)


def _backward_layer(l, dx, s, after=(), midway=None):
    S = dx.shape[0]
    tm, wb = s["tm"], s["wb"]
    g2 = s["g2"]
    dgu = _bwd_ffn_out(dx, wb["w_ffn_out"], s["gu"], l, after)
    tmb = min(TM_BIG, S)
    dwfo = _mm_tn(s["ff"], dx, DFF_SH, tmb if l else TM, f"dw_ffn_out_{l}")
    dx1, dg2 = _mm_nt_rms_bwd(
        dgu, pl.BlockSpec((None, tmb, DFF_SH), lambda i, k: (k, i, 0)),
        wb["w_ffn_in"], pl.BlockSpec((None, D, DFF_SH), lambda i, k: (k, 0, 0)),
        4, tmb, s["x1"], g2, dx, f"bwd_ffn_in_{l}")
    dwfi = _dw_ffn_in(s["hn2"], dgu, l)
    dmg, dwo = _bwd_out(dx1, wb["w_out"], s["merged"], l)
    mid = () if midway is None else tuple(midway([dwo, dwfi, dwfo]))
    dz6, dws, dbs, dlg, dlb = _gmlp_bwd(dmg, s["z6"], tm["ws_b"], tm["wst_b"], tm["bs_b"], tm["lg"], tm["lb"], mid)
    dz6, dcw, dcb, dwr, dbr, dwi, dbi, dlam = _lru_bwd(dz6, dmg, s["z6"], s["h0"], s["h1"], *s["lru"])

    def w_map(i, k):
        sh, tl = _in_tile(k)
        return (sh, 0, tl)

    dx0, dg1 = _mm_nt_rms_bwd(
        dz6, pl.BlockSpec((None, tmb, 512), lambda i, k: (k // 2, i, k % 2)),
        wb["w_in"], pl.BlockSpec((None, D, 512), w_map),
        N_IN_T, tmb, s["x"], s["g1"], dx1, f"bwd_in_{l}")
    dwin = _dw_in(s["hn1"], dz6, l)
    small = dict(norm1_g=dg1[0], gmlp_ln_g=dlg[0], gmlp_ln_b=dlb[0], gmlp_w_s=dws, gmlp_b_s=dbs[:, :, 0], conv_w=dcw, conv_b=dcb[0],
                 lru_w_r=dwr, lru_b_r=dbr, lru_w_i=dwi, lru_b_i=dbi, lru_lambda=dlam, norm2_g=dg2[0])
    return dx0, [dwin, dwo, dwfi, dwfo], small


def _local_step(x, tgt, p, wbs):
    saved = []
    for l in range(2):
        x, s = _forward_layer(l, x, p, wbs[l])
        saved.append(s)
    dx, loss_v, dfg = _loss_head(x, tgt, p["final_g"][None])
    big, smalls = [None, None], [None, None]
    for l in (1, 0):
        dx, big[l], smalls[l] = _backward_layer(l, dx, saved[l])
    small = {k: jnp.stack([smalls[0][k], smalls[1][k]]) for k in LAYER_SMALL}
    small["final_g"] = dfg[0]
    return loss_v, dx, big, small


def _place():
    x, y, c = lax.axis_index("x"), lax.axis_index("y"), lax.axis_index("c")
    return x, y, c, 2 * x + y


def _chip_at(x, y, d):
    px = 1 - x if d & 2 else x
    py = 1 - y if d & 1 else y
    return px, py, 2 * px + py


HBM = pl.BlockSpec(memory_space=pltpu.HBM)
SEM = pl.BlockSpec(memory_space=pltpu.SEMAPHORE)
DATAFLOW = pltpu.SideEffectType.DATAFLOW_SIDE_EFFECTING


def _in_hbm(a):
    return pltpu.with_memory_space_constraint(a, pltpu.HBM)


def _cast_into(wf, l, chip_arr, name):
    _, rows, cols = wf.shape
    rh = rows // 2

    def body(ch_ref, w_ref, o_ref):
        o_ref[...] = w_ref[...].astype(BF16)

    return pl.pallas_call(
        body, name=name, out_shape=jax.ShapeDtypeStruct((4, 2, rh, cols), BF16),
        grid_spec=pltpu.PrefetchScalarGridSpec(
            num_scalar_prefetch=1, grid=(2,),
            in_specs=[pl.BlockSpec((None, None, rh, cols), lambda h, ch: (l, h, 0, 0))],
            out_specs=pl.BlockSpec((None, None, rh, cols), lambda h, ch: (ch[0], h, 0, 0))),
        compiler_params=_cp(("parallel",)),
    )(chip_arr, wf.reshape(2, 2, rh, cols))


def _half_block(ref, chip, half, to, send_sem, recv_sem):
    blk = ref.at[chip, half]
    return pltpu.make_async_remote_copy(src_ref=blk, dst_ref=blk, send_sem=send_sem, recv_sem=recv_sem,
                                        device_id=to, device_id_type=MESH)


def _gather_weights(bufs, tiny):
    nt = len(bufs)
    n_ici = nt * 3

    def body(*refs):
        tiny_ref = refs[nt]
        o_refs, tiny_o = refs[nt + 1:2 * nt + 1], refs[2 * nt + 1]
        send, recv, fsend, frecv, tsend, trecv, lsem = refs[2 * nt + 2:]
        x, y, c, chip = _place()
        local = pltpu.make_async_copy(tiny_ref, tiny_o.at[chip], lsem)
        local.start()

        def tin(d, origin_chip, to):
            return pltpu.make_async_remote_copy(
                src_ref=tiny_ref, dst_ref=tiny_o.at[origin_chip], send_sem=tsend.at[d - 1], recv_sem=trecv.at[d - 1],
                device_id=to, device_id_type=MESH)

        sends = []
        for t in range(nt):
            for d in (1, 2, 3):
                px, py, _ = _chip_at(x, y, d)
                sends.append(_half_block(o_refs[t], chip, c, (px, py, c), send.at[3 * t + d - 1], recv.at[3 * t + d - 1]))
        for d in (1, 2, 3):
            px, py, _ = _chip_at(x, y, d)
            sends.append(tin(d, chip, (px, py, c)))
        for cp in sends:
            cp.start()
        passed = []
        for t in range(nt):
            for d in (1, 2, 3):
                k = 3 * t + d - 1
                _, _, pchip = _chip_at(x, y, d)
                _half_block(o_refs[t], pchip, c, (x, y, c), send.at[k], recv.at[k]).wait_recv()
                f = _half_block(o_refs[t], pchip, c, (x, y, 1 - c), fsend.at[k], frecv.at[k])
                f.start()
                passed.append(f)
        for t in range(nt):
            for d in (1, 2, 3):
                k = 3 * t + d - 1
                _, _, pchip = _chip_at(x, y, d)
                _half_block(o_refs[t], pchip, 1 - c, (x, y, 1 - c), fsend.at[k], frecv.at[k]).wait_recv()
        for d in (1, 2, 3):
            _, _, pchip = _chip_at(x, y, d)
            tin(d, pchip, (x, y, c)).wait_recv()
        for cp in sends + passed:
            cp.wait_send()
        local.wait()

    out_shape = [jax.ShapeDtypeStruct(b.shape, b.dtype) for b in bufs]
    out_shape.append(jax.ShapeDtypeStruct((4,) + tiny.shape, tiny.dtype))
    outs = pl.pallas_call(
        body, name="gather_weights_0", out_shape=out_shape,
        in_specs=[ANY] * (nt + 1), out_specs=[ANY] * (nt + 1),
        scratch_shapes=[pltpu.SemaphoreType.DMA((n_ici,)), pltpu.SemaphoreType.DMA((n_ici,)),
                        pltpu.SemaphoreType.DMA((n_ici,)), pltpu.SemaphoreType.DMA((n_ici,)),
                        pltpu.SemaphoreType.DMA((3,)), pltpu.SemaphoreType.DMA((3,)), pltpu.SemaphoreType.DMA],
        input_output_aliases={t: t for t in range(nt)},
        compiler_params=_cp(has_side_effects=True),
    )(*bufs, tiny)
    return outs[:nt], outs[nt]


def _gather_start(bufs, tag, after=()):
    nt, na = len(bufs), len(after)

    def body(*refs):
        b_refs = refs[:nt]
        send, recv = refs[nt + na], refs[nt + na + 1]
        token = refs[2 * nt + na + 2]
        x, y, c, chip = _place()
        for t in range(nt):
            for d in (1, 2, 3):
                px, py, _ = _chip_at(x, y, d)
                _half_block(b_refs[t], chip, c, (px, py, c), send.at[3 * t + d - 1], recv.at[3 * t + d - 1]).start()
        token[...] = jnp.zeros_like(token)

    outs = pl.pallas_call(
        body, name=f"gather_start_{tag}",
        out_shape=(pltpu.SemaphoreType.DMA((3 * nt,)), pltpu.SemaphoreType.DMA((3 * nt,)),
                   *[pltpu.HBM(b.shape, b.dtype) for b in bufs], jax.ShapeDtypeStruct((8, 128), F32)),
        in_specs=[HBM] * nt + [ANY] * na, out_specs=(SEM, SEM, *[HBM] * nt, pl.BlockSpec(memory_space=pltpu.VMEM)),
        input_output_aliases={t: 2 + t for t in range(nt)},
        compiler_params=pltpu.CompilerParams(has_side_effects=DATAFLOW),
    )(*[_in_hbm(b) for b in bufs], *after)
    return outs[0], outs[1], list(outs[2:2 + nt]), outs[2 + nt]


def _gather_wait(send, recv, bufs, after, tag):
    nt = len(bufs)

    def body(*refs):
        b_refs = refs[:nt]
        send_ref, recv_ref = refs[nt], refs[nt + 1]
        x, y, c, chip = _place()
        for t in range(nt):
            for d in (1, 2, 3):
                k = 3 * t + d - 1
                px, py, pchip = _chip_at(x, y, d)
                _half_block(b_refs[t], chip, c, (px, py, c), send_ref.at[k], recv_ref.at[k]).wait_send()
                _half_block(b_refs[t], pchip, c, (px, py, c), send_ref.at[k], recv_ref.at[k]).wait_recv()

    outs = pl.pallas_call(
        body, name=f"gather_wait_{tag}", out_shape=[pltpu.HBM(b.shape, b.dtype) for b in bufs],
        in_specs=[HBM] * nt + [SEM, SEM, ANY], out_specs=[HBM] * nt,
        input_output_aliases={t: t for t in range(nt)},
        compiler_params=pltpu.CompilerParams(has_side_effects=DATAFLOW),
    )(*bufs, send, recv, after)
    return list(outs)


def _gather_pass_on(bufs, tag):
    nt = len(bufs)

    def body(*refs):
        o_refs = refs[nt:2 * nt]
        fsend, frecv = refs[2 * nt:]
        x, y, c, _ = _place()
        cps = []
        for t in range(nt):
            for d in (1, 2, 3):
                k = 3 * t + d - 1
                _, _, pchip = _chip_at(x, y, d)
                cps.append(_half_block(o_refs[t], pchip, c, (x, y, 1 - c), fsend.at[k], frecv.at[k]))
        for cp in cps:
            cp.start()
        for t in range(nt):
            for d in (1, 2, 3):
                k = 3 * t + d - 1
                _, _, pchip = _chip_at(x, y, d)
                _half_block(o_refs[t], pchip, 1 - c, (x, y, 1 - c), fsend.at[k], frecv.at[k]).wait_recv()
        for cp in cps:
            cp.wait_send()

    return pl.pallas_call(
        body, name=f"gather_pass_on_{tag}", out_shape=[jax.ShapeDtypeStruct(b.shape, b.dtype) for b in bufs],
        in_specs=[ANY] * nt, out_specs=[ANY] * nt,
        scratch_shapes=[pltpu.SemaphoreType.DMA((3 * nt,)), pltpu.SemaphoreType.DMA((3 * nt,))],
        input_output_aliases={t: t for t in range(nt)},
        compiler_params=_cp(has_side_effects=True),
    )(*bufs)


def _to_sibling_halves(gs, l):
    nt = len(gs)

    def body(*refs):
        g_refs, o_refs = refs[:nt], refs[nt:2 * nt]
        send, recv = refs[2 * nt:]
        x, y, c, _ = _place()
        cps = [pltpu.make_async_remote_copy(
            src_ref=g_refs[t].at[k, 1 - c], dst_ref=o_refs[t].at[k], send_sem=send.at[4 * t + k], recv_sem=recv.at[4 * t + k],
            device_id=(x, y, 1 - c), device_id_type=MESH) for t in range(nt) for k in range(4)]
        for cp in cps:
            cp.start()
        for cp in cps:
            cp.wait()

    return pl.pallas_call(
        body, name=f"grads_to_sibling_{l}", out_shape=[jax.ShapeDtypeStruct((4,) + g.shape[2:], g.dtype) for g in gs],
        in_specs=[ANY] * nt, out_specs=[ANY] * nt,
        scratch_shapes=[pltpu.SemaphoreType.DMA((4 * nt,)), pltpu.SemaphoreType.DMA((4 * nt,))],
        compiler_params=_cp(has_side_effects=True),
    )(*gs)


def _chip_copy(c_ref, land_ref, x, y, c, d, send_sem, recv_sem):
    px, py, pchip = _chip_at(x, y, d)
    return pltpu.make_async_remote_copy(src_ref=c_ref.at[pchip], dst_ref=land_ref.at[d - 1], send_sem=send_sem, recv_sem=recv_sem,
                                        device_id=(px, py, c), device_id_type=MESH)


def _to_chips_start(cs, l):
    nt = len(cs)
    lands = [lax.empty((3,) + a.shape[1:], a.dtype) for a in cs]

    def body(*refs):
        c_refs, land_refs = refs[:nt], refs[nt:2 * nt]
        send, recv = refs[2 * nt], refs[2 * nt + 1]
        token = refs[4 * nt + 2]
        x, y, c, _ = _place()
        for t in range(nt):
            for d in (1, 2, 3):
                _chip_copy(c_refs[t], land_refs[t], x, y, c, d, send.at[3 * t + d - 1], recv.at[3 * t + d - 1]).start()
        token[...] = jnp.zeros_like(token)

    outs = pl.pallas_call(
        body, name=f"grads_to_chips_start_{l}",
        out_shape=(pltpu.SemaphoreType.DMA((3 * nt,)), pltpu.SemaphoreType.DMA((3 * nt,)),
                   *[pltpu.HBM(a.shape, a.dtype) for a in cs], *[pltpu.HBM(a.shape, a.dtype) for a in lands],
                   jax.ShapeDtypeStruct((8, 128), F32)),
        in_specs=[HBM] * (2 * nt), out_specs=(SEM, SEM, *[HBM] * (2 * nt), pl.BlockSpec(memory_space=pltpu.VMEM)),
        input_output_aliases={i: 2 + i for i in range(2 * nt)},
        compiler_params=pltpu.CompilerParams(has_side_effects=DATAFLOW),
    )(*[_in_hbm(a) for a in cs], *[_in_hbm(a) for a in lands])
    return outs[0], outs[1], list(outs[2:2 + nt]), list(outs[2 + nt:2 + 2 * nt]), outs[2 + 2 * nt]


def _to_chips_wait(send, recv, cs, lands, after, l):
    nt = len(cs)

    def body(*refs):
        c_refs, land_refs = refs[:nt], refs[nt:2 * nt]
        send_ref, recv_ref = refs[2 * nt], refs[2 * nt + 1]
        x, y, c, _ = _place()
        for t in range(nt):
            for d in (1, 2, 3):
                cp = _chip_copy(c_refs[t], land_refs[t], x, y, c, d, send_ref.at[3 * t + d - 1], recv_ref.at[3 * t + d - 1])
                cp.wait_send()
                cp.wait_recv()

    outs = pl.pallas_call(
        body, name=f"grads_to_chips_wait_{l}", out_shape=[pltpu.HBM(a.shape, a.dtype) for a in cs + lands],
        in_specs=[HBM] * (2 * nt) + [SEM, SEM, ANY], out_specs=[HBM] * (2 * nt),
        input_output_aliases={i: i for i in range(2 * nt)},
        compiler_params=pltpu.CompilerParams(has_side_effects=DATAFLOW),
    )(*cs, *lands, send, recv, after)
    return list(outs[:nt]), list(outs[nt:])


def _join_halves(fs, l):
    nt = len(fs)

    def body(*refs):
        o_refs = refs[nt:2 * nt]
        send, recv = refs[2 * nt:]
        x, y, c, _ = _place()
        cps = [pltpu.make_async_remote_copy(
            src_ref=o_refs[t].at[c], dst_ref=o_refs[t].at[c], send_sem=send.at[t], recv_sem=recv.at[t],
            device_id=(x, y, 1 - c), device_id_type=MESH) for t in range(nt)]
        for cp in cps:
            cp.start()
        for cp in cps:
            cp.wait()

    return pl.pallas_call(
        body, name=f"grads_join_{l}", out_shape=[jax.ShapeDtypeStruct(a.shape, a.dtype) for a in fs],
        in_specs=[ANY] * nt, out_specs=[ANY] * nt,
        scratch_shapes=[pltpu.SemaphoreType.DMA((nt,)), pltpu.SemaphoreType.DMA((nt,))],
        input_output_aliases={t: t for t in range(nt)},
        compiler_params=_cp(has_side_effects=True),
    )(*fs)


def _add_half(g, r, c_arr, name):
    _, _, rh, cols = g.shape

    def body(c_ref, g_ref, r_ref, o_ref):
        o_ref[...] = (g_ref[...] + r_ref[...]).astype(BF16)

    blk = pl.BlockSpec((None, rh, cols), lambda k, cr: (k, 0, 0))
    return pl.pallas_call(
        body, name=name, out_shape=jax.ShapeDtypeStruct((4, rh, cols), BF16),
        grid_spec=pltpu.PrefetchScalarGridSpec(
            num_scalar_prefetch=1, grid=(4,),
            in_specs=[pl.BlockSpec((None, None, rh, cols), lambda k, cr: (k, cr[0], 0, 0)), blk], out_specs=blk),
        compiler_params=_cp(("parallel",)),
    )(c_arr, g, r)


def _sum_chips(cs, r3, place_arr, name):
    _, rh, cols = cs.shape
    rb = rh // 2

    def body(pl_ref, a_ref, r0_ref, r1_ref, r2_ref, o_ref):
        up = lambda ref: ref[...].astype(F32)
        o_ref[...] = ((up(a_ref) + up(r0_ref)) + up(r1_ref)) + up(r2_ref)

    def slot(d):
        return pl.BlockSpec((None, rb, cols), lambda i, pa: (d, i, 0))

    return pl.pallas_call(
        body, name=name, out_shape=jax.ShapeDtypeStruct((2, rh, cols), F32),
        grid_spec=pltpu.PrefetchScalarGridSpec(
            num_scalar_prefetch=1, grid=(2,),
            in_specs=[pl.BlockSpec((None, rb, cols), lambda i, pa: (pa[0], i, 0)), slot(0), slot(1), slot(2)],
            out_specs=pl.BlockSpec((None, rb, cols), lambda i, pa: (pa[1], i, 0))),
        compiler_params=_cp(("parallel",)),
    )(place_arr, cs, r3, r3, r3)


def _allreduce_small(pack):
    rows = pack.shape[0]
    hr = rows // 2

    def body(p_ref, o_ref, sib, slots, s1, r1, s2, r2, s3, r3):
        x, y, c, chip = _place()
        sibling = (x, y, 1 - c)
        ex = pltpu.make_async_remote_copy(src_ref=p_ref, dst_ref=sib, send_sem=s1, recv_sem=r1,
                                          device_id=sibling, device_id_type=MESH)
        ex.start()
        ex.wait()
        half = pl.ds(pl.multiple_of(c * hr, 16), hr)
        slots[0] = (p_ref[half, :] + sib[half, :]).astype(BF16)
        cps = []
        for d in (1, 2, 3):
            px, py, _ = _chip_at(x, y, d)
            cps.append(pltpu.make_async_remote_copy(
                src_ref=slots.at[0], dst_ref=slots.at[d], send_sem=s2.at[d - 1], recv_sem=r2.at[d - 1],
                device_id=(px, py, c), device_id_type=MESH))
        for cp in cps:
            cp.start()
        for cp in cps:
            cp.wait()
        tot = slots[chip].astype(F32)
        for k in (1, 2, 3):
            tot = tot + slots[jnp.bitwise_xor(chip, k)].astype(F32)
        o_ref[half, :] = tot
        back = pltpu.make_async_remote_copy(src_ref=o_ref.at[half, :], dst_ref=o_ref.at[half, :], send_sem=s3, recv_sem=r3,
                                            device_id=sibling, device_id_type=MESH)
        back.start()
        back.wait()

    vm = pl.BlockSpec(memory_space=pltpu.VMEM)
    return pl.pallas_call(
        body, name="allreduce_small", out_shape=jax.ShapeDtypeStruct((rows, 128), F32),
        in_specs=[vm], out_specs=vm,
        scratch_shapes=[pltpu.VMEM((rows, 128), F32), pltpu.VMEM((4, hr, 128), BF16),
                        pltpu.SemaphoreType.DMA, pltpu.SemaphoreType.DMA, pltpu.SemaphoreType.DMA((3,)), pltpu.SemaphoreType.DMA((3,)),
                        pltpu.SemaphoreType.DMA, pltpu.SemaphoreType.DMA],
        compiler_params=_cp(has_side_effects=True),
    )(pack)


def _adam_math(gv, wv, mv, vv):
    m2 = ADAM_B1 * mv + (1.0 - ADAM_B1) * gv
    v2 = ADAM_B2 * vv + (1.0 - ADAM_B2) * (gv * gv)
    m_hat = m2 / (1.0 - ADAM_B1 ** ADAM_STEP)
    v_hat = v2 / (1.0 - ADAM_B2 ** ADAM_STEP)
    return -ADAM_LR * (m_hat / (jnp.sqrt(v_hat) + ADAM_EPS) + ADAM_WD * wv), m2, v2


def _adam(g, w, m, v, name):
    rows, cols = g.shape
    rb = rows // 4

    def body(g_ref, w_ref, m_ref, v_ref, d_ref, m2_ref, v2_ref):
        d_ref[...], m2_ref[...], v2_ref[...] = _adam_math(g_ref[...], w_ref[...], m_ref[...], v_ref[...])

    blk = pl.BlockSpec((rb, cols), lambda i: (i, 0))
    shp = jax.ShapeDtypeStruct((rows, cols), F32)
    return pl.pallas_call(
        body, name=name, grid=(4,), in_specs=[blk] * 4, out_specs=[blk] * 3, out_shape=[shp] * 3,
        compiler_params=_cp(("parallel",)),
    )(g, w, m, v)


def _adam_layer(g, w, m, v, l, prev, name):
    rows, cols = g.shape
    rb = rows // 4

    def body(g_ref, w_ref, m_ref, v_ref, *rest):
        go_ref, d_ref, m2_ref, v2_ref = rest[-4:]
        gv = g_ref[...]
        go_ref[...] = gv
        d_ref[...], m2_ref[...], v2_ref[...] = _adam_math(gv, w_ref[...], m_ref[...], v_ref[...])

    lay = pl.BlockSpec((None, rb, cols), lambda i: (l, i, 0))
    shp = jax.ShapeDtypeStruct((2, rows, cols), F32)
    prev = () if prev is None else tuple(prev)
    return pl.pallas_call(
        body, name=name, grid=(4,), in_specs=[pl.BlockSpec((rb, cols), lambda i: (i, 0)), lay, lay, lay] + [ANY] * len(prev),
        out_specs=[lay] * 4, out_shape=[shp] * 4,
        input_output_aliases={4 + j: j for j in range(len(prev))},
        compiler_params=_cp(("parallel",)),
    )(g, w, m, v, *prev)


def _rows128(a):
    return a.reshape(-1, 128)


def _pack(arrs, mult):
    parts = [_rows128(a) for a in arrs]
    rows = sum(q.shape[0] for q in parts)
    pad = -rows % mult
    if pad:
        parts.append(jnp.zeros((pad, 128), F32))
    return jnp.concatenate(parts, axis=0)


def _unpack(pack, shapes):
    out, o = [], 0
    for s in shapes:
        n = 1
        for e in s:
            n *= e
        out.append(pack[o:o + n // 128].reshape(s))
        o += n // 128
    return out


WEIGHTS = ['norm1_g', 'w_in', 'gmlp_ln_g', 'gmlp_ln_b', 'gmlp_w_s', 'gmlp_b_s', 'conv_w', 'conv_b', 'lru_w_r', 'lru_b_r', 'lru_w_i',
           'lru_b_i', 'lru_lambda', 'w_out', 'norm2_g', 'w_ffn_in', 'w_ffn_out', 'final_g']
BIG = ['w_in', 'w_out', 'w_ffn_in', 'w_ffn_out']
SMALL = [n for n in WEIGHTS if n not in BIG]
CHIP_SHARDED_SMALL = ['conv_w', 'lru_b_r', 'lru_b_i', 'lru_lambda']


def kernel(x, norm1_g, w_in, gmlp_ln_g, gmlp_ln_b, gmlp_w_s, gmlp_b_s, conv_w, conv_b, lru_w_r, lru_b_r, lru_w_i, lru_b_i, lru_lambda, w_out, norm2_g, w_ffn_in, w_ffn_out, final_g, loss_target, m_norm1_g, m_w_in, m_gmlp_ln_g, m_gmlp_ln_b, m_gmlp_w_s, m_gmlp_b_s, m_conv_w, m_conv_b, m_lru_w_r, m_lru_b_r, m_lru_w_i, m_lru_b_i, m_lru_lambda, m_w_out, m_norm2_g, m_w_ffn_in, m_w_ffn_out, m_final_g, v_norm1_g, v_w_in, v_gmlp_ln_g, v_gmlp_ln_b, v_gmlp_w_s, v_gmlp_b_s, v_conv_w, v_conv_b, v_lru_w_r, v_lru_b_r, v_lru_w_i, v_lru_b_i, v_lru_lambda, v_w_out, v_norm2_g, v_w_ffn_in, v_w_ffn_out, v_final_g):
    a = dict(locals())
    w = {n: a[n] for n in WEIGHTS}
    mom = {n: a["m_" + n] for n in WEIGHTS}
    var = {n: a["v_" + n] for n in WEIGHTS}
    _, _, c, chip = _place()
    c_arr, chip_arr = jnp.reshape(c, (1,)).astype(jnp.int32), jnp.reshape(chip, (1,)).astype(jnp.int32)
    place_arr = jnp.stack([chip, c]).astype(jnp.int32)

    first, rest = BIG[:1], BIG[1:]

    def as_weights(names, full):
        wb = {n: f.reshape(4, 2 * f.shape[2], f.shape[3]) for n, f in zip(names, full)}
        if "w_out" in wb:
            wb["w_out"] = wb["w_out"].reshape(D, D)
            wb["w_ffn_out"] = wb["w_ffn_out"].reshape(DFF, D)
        return wb

    bufs = [{n: _cast_into(w[n], l, chip_arr, f"cast_{n}_{l}") for n in BIG} for l in range(2)]
    tiny = _pack([w[n] for n in CHIP_SHARDED_SMALL], 8)
    w_in0, tiny_full = _gather_weights([bufs[0]["w_in"]], tiny)
    fly0 = _gather_start([bufs[0][n] for n in rest], "0", after=(tiny_full,))
    fly1 = _gather_start([bufs[1][n] for n in BIG], "1", after=(fly0[3],))
    p = {n: w[n] for n in SMALL}
    parts = [_unpack(tiny_full[k], [w[n].shape for n in CHIP_SHARDED_SMALL]) for k in range(4)]
    for i, n in enumerate(CHIP_SHARDED_SMALL):
        p[n] = jnp.concatenate([parts[k][i] for k in range(4)], axis=-1)

    def landed(fly, names, after, tag):
        return as_weights(names, _gather_pass_on(_gather_wait(fly[0], fly[1], fly[2], after, tag), tag))

    xa, saved0 = _forward_layer(0, x[0], p, as_weights(first, w_in0), after=(fly0[3], fly1[3]),
                                rest=lambda merged: landed(fly0, rest, merged, "0"))
    xb, saved1 = _forward_layer(1, xa, p, landed(fly1, BIG, xa, "1"))
    dxb, loss_v, dfg = _loss_head(xb, loss_target[0], p["final_g"][None])
    loss = lax.psum(loss_v[0, 0], ("x", "y", "c"))

    out, flying = {}, {}

    def reduce_start(grads, names, l, tag):
        gs = [g.reshape(4, 2, -1, g.shape[-1]) for g in grads]
        from_sib = _to_sibling_halves(gs, tag)
        cs = [_add_half(g, r, c_arr, f"add_half_{n}_{l}") for n, g, r in zip(names, gs, from_sib)]
        flying[tag] = (names, l) + tuple(_to_chips_start(cs, tag))
        return (flying[tag][-1],)

    def reduce_finish(tag, after):
        names, l, send, recv, cs, lands, _ = flying[tag]
        cs, lands = _to_chips_wait(send, recv, cs, lands, after, tag)
        ts = [_sum_chips(cc, r3, place_arr, f"sum_chips_{n}_{l}") for n, cc, r3 in zip(names, cs, lands)]
        for n, j in zip(names, _join_halves(ts, tag)):
            out[n] = _adam_layer(j.reshape(w[n].shape[1:]), w[n], mom[n], var[n], l, out.get(n), f"adam_{n}_{l}")
        return out[names[-1]][0]

    def midway0(grads):
        reduce_finish("1a", grads[0])
        reduce_finish("1b", grads[0])
        return reduce_start(grads, rest, 0, "0a")

    dxa, big1, small1 = _backward_layer(1, dxb, saved1, midway=lambda grads: reduce_start(grads, rest, 1, "1a"))
    dx, big0, small0 = _backward_layer(0, dxa, saved0, after=reduce_start(big1[:1], first, 1, "1b"), midway=midway0)
    reduce_start(big0[:1], first, 0, "0b")
    reduce_finish("0b", reduce_finish("0a", dx))
    small = {k: jnp.stack([small0[k], small1[k]]) for k in LAYER_SMALL}
    small["final_g"] = dfg[0]

    full_shapes = [small[n].shape for n in SMALL]
    red = _unpack(_allreduce_small(_pack([small[n] for n in SMALL], 32)), full_shapes)
    g_small = []
    for n, g in zip(SMALL, red):
        if n in CHIP_SHARDED_SMALL:
            g = lax.dynamic_slice_in_dim(g, chip * w[n].shape[-1], w[n].shape[-1], axis=g.ndim - 1)
        g_small.append(g)
    shapes = [w[n].shape for n in SMALL]
    packs = [_pack(lst, 32) for lst in (g_small, [w[n] for n in SMALL], [mom[n] for n in SMALL], [var[n] for n in SMALL])]
    upd = [_unpack(u, shapes) for u in _adam(*packs, "adam_small")]
    for i, n in enumerate(SMALL):
        out[n] = [g_small[i], upd[0][i], upd[1][i], upd[2][i]]

    return (loss, dx[None]) + tuple(out[n][i] for i in range(4) for n in WEIGHTS)
```

```python
import functools

import jax
import jax.numpy as jnp
from jax import lax
from jax.experimental import pallas as pl
from jax.experimental.pallas import tpu as pltpu

F32 = jnp.float32
BF16 = jnp.bfloat16
MESH = pl.DeviceIdType.MESH

D = 1024
NH = 8
HD = 128
CHUNK = 128
N_IN_T = 12
DFF = 2816
DFF_SH = 1408
EPS = 1e-6
LRU_C = 8.0
ADAM_LR, ADAM_B1, ADAM_B2, ADAM_EPS, ADAM_WD, ADAM_STEP = 0.001, 0.9, 0.999, 1e-08, 0.01, 10

TM = 512
TM_BIG = 1024
RT = 128
PADR = 8
VMEM_LIMIT = 56 * 1024 * 1024


def _cp(sem=None, **kw):
    if sem is not None:
        kw["dimension_semantics"] = sem
    return pltpu.CompilerParams(vmem_limit_bytes=VMEM_LIMIT, **kw)


_GC = 0.7978845608028654


def _sigmoid(x):
    return 1.0 / (1.0 + jnp.exp(-x))


def _gelu(x):
    return 0.5 * x * (1.0 + jnp.tanh(_GC * (x + 0.044715 * x * x * x)))


def _gelu_and_grad(x):
    t = jnp.tanh(_GC * (x + 0.044715 * x * x * x))
    g = 0.5 * x * (1.0 + t)
    dg = 0.5 * (1.0 + t) + 0.5 * x * (1.0 - t * t) * _GC * (1.0 + 3 * 0.044715 * x * x)
    return g, dg


def _softplus_neg(lam):
    y = jnp.exp(-jnp.abs(lam))
    u = 1.0 + y
    l1p = jnp.where(u == 1.0, y, jnp.log(u) * y / (u - 1.0))
    return jnp.maximum(-lam, 0.0) + l1p


def _dot(a, b):
    return jnp.dot(a, b, preferred_element_type=F32)


def _dot_nt(a, b):
    return lax.dot_general(a, b, (((1,), (1,)), ((), ())), preferred_element_type=F32)


def _dot_tn(a, b):
    return lax.dot_general(a, b, (((0,), (0,)), ((), ())), preferred_element_type=F32)


def _rms_hat(x):
    r = lax.rsqrt(jnp.mean(x * x, axis=-1, keepdims=True) + EPS)
    return x * r, r


def _rms_bwd(dh, x, g):
    xh, r = _rms_hat(x)
    dxh = dh * g
    dx = r * (dxh - xh * jnp.mean(dxh * xh, axis=-1, keepdims=True))
    return dx, jnp.sum(dh * xh, axis=0, keepdims=True)


def _in_tile(j):
    m, hf = j // 2, j % 2
    orig = jnp.where(m < 2, m, jnp.where(m == 2, 4, jnp.where(m < 5, m - 1, 5)))
    t = orig * 2 + hf
    return t // 3, t % 3


ANY = pl.BlockSpec(memory_space=pl.ANY)


def _mm_in(x, g, w_in, l, after=()):
    S = x.shape[0]
    tm = min(2 * TM_BIG, S)

    def body(x_ref, g_ref, w_ref, *rest):
        o_ref, h_ref = rest[-2:]

        @pl.when(pl.program_id(1) == 0)
        def _():
            xh, _ = _rms_hat(x_ref[...])
            h_ref[...] = (xh * g_ref[...]).astype(BF16)
        o_ref[...] = _dot(h_ref[...], w_ref[...]).astype(BF16)

    def w_map(i, j):
        sh, tl = _in_tile(j)
        return (sh, 0, tl)

    return pl.pallas_call(
        body, name=f"mm_in_{l}", grid=(S // tm, N_IN_T),
        in_specs=[pl.BlockSpec((tm, D), lambda i, j: (i, 0)), pl.BlockSpec((1, D), lambda i, j: (0, 0)),
                  pl.BlockSpec((None, D, 512), w_map)] + [ANY] * len(after),
        out_specs=[pl.BlockSpec((None, tm, 512), lambda i, j: (j // 2, i, j % 2)), pl.BlockSpec((tm, D), lambda i, j: (i, 0))],
        out_shape=[jax.ShapeDtypeStruct((6, S, D), BF16), jax.ShapeDtypeStruct((S, D), BF16)],
        compiler_params=_cp(("parallel", "arbitrary")),
    )(x, g, w_in, *after)


def _mm_res(a, w, res, l, name):
    S, K = a.shape

    def body(a_ref, w_ref, r_ref, o_ref):
        o_ref[...] = r_ref[...] + _dot(a_ref[...], w_ref[...])

    tm = min(TM_BIG, S) if l else TM
    return pl.pallas_call(
        body, name=f"{name}_{l}", grid=(S // tm,),
        in_specs=[pl.BlockSpec((tm, K), lambda i: (i, 0)), pl.BlockSpec((K, D), lambda i: (0, 0)),
                  pl.BlockSpec((tm, D), lambda i: (i, 0))],
        out_specs=pl.BlockSpec((tm, D), lambda i: (i, 0)),
        out_shape=jax.ShapeDtypeStruct((S, D), F32),
        compiler_params=_cp(("parallel",)),
    )(a, w, res)


def _mm_ffn_in(x, g, w_fi, l):
    S = x.shape[0]

    def body(x_ref, g_ref, w_ref, gu_ref, ff_ref, h_ref):
        @pl.when(pl.program_id(1) == 0)
        def _():
            xh, _ = _rms_hat(x_ref[...])
            h_ref[...] = (xh * g_ref[...]).astype(BF16)
        hv = h_ref[...]
        ga = _dot(hv, w_ref[0])
        gb = _dot(hv, w_ref[1])
        gu_ref[0] = ga.astype(BF16)
        gu_ref[1] = gb.astype(BF16)
        ff_ref[...] = (ga * _sigmoid(ga) * gb).astype(BF16)

    gu, ff, h = pl.pallas_call(
        body, name=f"mm_ffn_in_{l}", grid=(S // TM, 2),
        in_specs=[pl.BlockSpec((TM, D), lambda i, s: (i, 0)), pl.BlockSpec((1, D), lambda i, s: (0, 0)),
                  pl.BlockSpec((2, None, D, DFF_SH), lambda i, s: (0, s, 0, 0))],
        out_specs=[pl.BlockSpec((2, None, TM, DFF_SH), lambda i, s: (0, s, i, 0)),
                   pl.BlockSpec((TM, DFF_SH), lambda i, s: (i, s)),
                   pl.BlockSpec((TM, D), lambda i, s: (i, 0))],
        out_shape=[jax.ShapeDtypeStruct((2, 2, S, DFF_SH), BF16), jax.ShapeDtypeStruct((S, DFF), BF16),
                   jax.ShapeDtypeStruct((S, D), BF16)],
        compiler_params=_cp(("parallel", "arbitrary")),
    )(x, g, w_fi.reshape(2, 2, D, DFF_SH))
    return gu.reshape(4, S, DFF_SH), ff, h


def _gmlp_fwd(z6, ws_b, bs_b, lg, lb):
    S = z6.shape[1]

    def body(z_ref, ws_ref, bs_ref, lg_ref, lb_ref, o_ref, mix):
        gv = _gelu(z_ref[1].astype(F32))
        xc = gv - jnp.mean(gv, axis=-1, keepdims=True)
        rs = lax.rsqrt(jnp.mean(xc * xc, axis=-1, keepdims=True) + EPS)
        vb = (xc * rs * lg_ref[...] + lb_ref[...]).astype(BF16)
        for gi in range(NH):
            cs = slice(gi * HD, (gi + 1) * HD)
            mix[:, cs] = _dot(ws_ref[gi], vb[:, cs])
        o_ref[...] = (_sigmoid(z_ref[2].astype(F32)) * _gelu(z_ref[0].astype(F32)) * (mix[...] + bs_ref[...])).astype(BF16)

    return pl.pallas_call(
        body, name="gmlp_fwd", grid=(S // CHUNK,),
        in_specs=[pl.BlockSpec((3, CHUNK, D), lambda i: (0, i, 0)), pl.BlockSpec((NH, CHUNK, CHUNK), lambda i: (0, 0, 0)),
                  pl.BlockSpec((CHUNK, D), lambda i: (0, 0)), pl.BlockSpec((1, D), lambda i: (0, 0)),
                  pl.BlockSpec((1, D), lambda i: (0, 0))],
        out_specs=pl.BlockSpec((CHUNK, D), lambda i: (i, 0)),
        out_shape=jax.ShapeDtypeStruct((S, D), BF16),
        scratch_shapes=[pltpu.VMEM((CHUNK, D), F32)],
        compiler_params=_cp(("parallel",)),
    )(z6, ws_b, bs_b, lg, lb)


def _row_iota():
    return lax.broadcasted_iota(jnp.int32, (RT, HD), 0)


SUB = 8
UNROLL = 4
GRAD_ROWS = 256


def _scan_up(a, b, carry):
    row = lax.broadcasted_iota(jnp.int32, (SUB, HD), 0)
    masks = [(d, row >= d) for d in (1, 2, 4)]
    c = jnp.broadcast_to(carry, (SUB, HD))
    hs = []
    for j in range(RT // SUB):
        aj, bj = a[SUB * j:SUB * (j + 1)], b[SUB * j:SUB * (j + 1)]
        for d, m in masks:
            bj = bj + aj * jnp.where(m, pltpu.roll(bj, d, 0), 0.0)
            aj = aj * jnp.where(m, pltpu.roll(aj, d, 0), 1.0)
        h = bj + aj * c
        hs.append(h)
        c = jnp.broadcast_to(h[SUB - 1:SUB, :], (SUB, HD))
    return jnp.concatenate(hs, axis=0), hs[-1][SUB - 1:SUB, :]


def _scan_down(a, b, carry):
    row = lax.broadcasted_iota(jnp.int32, (SUB, HD), 0)
    masks = [(d, row < SUB - d) for d in (1, 2, 4)]
    c = jnp.broadcast_to(carry, (SUB, HD))
    hs = []
    for j in reversed(range(RT // SUB)):
        aj, bj = a[SUB * j:SUB * (j + 1)], b[SUB * j:SUB * (j + 1)]
        for d, m in masks:
            bj = bj + aj * jnp.where(m, pltpu.roll(bj, SUB - d, 0), 0.0)
            aj = aj * jnp.where(m, pltpu.roll(aj, SUB - d, 0), 1.0)
        h = bj + aj * c
        hs.append(h)
        c = jnp.broadcast_to(h[0:1, :], (SUB, HD))
    return jnp.concatenate(hs[::-1], axis=0), hs[-1][0:1, :]


def _decay(r, sp_d):
    log_a = -LRU_C * r * sp_d
    a = jnp.exp(log_a)
    return a, jnp.sqrt(jnp.maximum(-jnp.tanh(log_a) * (a * a + 1.0), 0.0))


def _lru_gates(xc, d, wr_ref, br_ref, wi_ref, bi_ref, sp):
    xb = xc.astype(BF16)
    r = _sigmoid(_dot(xb, wr_ref[d]) + br_ref[d:d + 1, :])
    i = _sigmoid(_dot(xb, wi_ref[d]) + bi_ref[d:d + 1, :])
    a, mult = _decay(r, sp[d:d + 1, :])
    return r, i, a, mult


def _shifted(win, k):
    w = RT + 2 * PADR
    v = win if k == 0 else pltpu.roll(win, (-k) % w, 0)
    return v[PADR:PADR + RT]


def _conv_taps(win):
    return [_shifted(win, k) for k in (-1, 0, 1, 2)]


def _fill_padded(dst, src_ref, S):
    zeros = jnp.zeros((PADR, HD), F32)
    dst[0:PADR, :] = zeros
    dst[PADR + S:2 * PADR + S, :] = zeros

    def cp(i, c):
        t0 = pl.multiple_of(i * RT, RT)
        dst[pl.ds(t0 + PADR, RT), :] = src_ref[pl.ds(t0, RT), :].astype(F32)
        return c
    lax.fori_loop(0, S // RT, cp, 0)


def _conv_fwd_all(zxp, xc_s, cw_ref, cb_ref, S):
    def cv(i, c):
        t0 = pl.multiple_of(i * RT, RT)
        xm1, x0, xp1, xp2 = _conv_taps(zxp[pl.ds(t0, RT + 2 * PADR), :])
        xc_s[pl.ds(t0, RT), :] = (cb_ref[...] + xm1 * cw_ref[0:1, :] + x0 * cw_ref[1:2, :]
                                  + xp1 * cw_ref[2:3, :] + xp2 * cw_ref[3:4, :])
        return c
    lax.fori_loop(0, S // RT, cv, 0)


def _lru_specs(S):
    head = lambda h: (0, h)
    return [pl.BlockSpec((4, HD), head), pl.BlockSpec((1, HD), head),
            pl.BlockSpec((2, None, HD, HD), lambda h: (0, h, 0, 0)), pl.BlockSpec((2, HD), head),
            pl.BlockSpec((2, None, HD, HD), lambda h: (0, h, 0, 0)), pl.BlockSpec((2, HD), head),
            pl.BlockSpec((2, HD), head)]


def _lru_fwd(z6, ya, cw, cb, wr, br, wi, bi, lam):
    S = z6.shape[1]
    nt = S // RT

    def body(z_ref, ya_ref, cw_ref, cb_ref, wr_ref, br_ref, wi_ref, bi_ref, lam_ref, mg_ref, h0_ref, h1_ref, zxp, xc_s):
        sp = _softplus_neg(lam_ref[...])
        _fill_padded(zxp, z_ref.at[0], S)
        _conv_fwd_all(zxp, xc_s, cw_ref, cb_ref, S)

        def scans(i, carry):
            cu, cd = carry
            for u in range(UNROLL):
                j = i * UNROLL + u
                ru = pl.ds(pl.multiple_of(j * RT, RT), RT)
                rd = pl.ds(pl.multiple_of((nt - 1 - j) * RT, RT), RT)
                xu, xd = xc_s[ru, :], xc_s[rd, :]
                _, gi, a, mult = _lru_gates(xu, 0, wr_ref, br_ref, wi_ref, bi_ref, sp)
                hu, cu = _scan_up(a, mult * gi * xu, cu)
                h0_ref[ru, :] = hu
                _, gi, a, mult = _lru_gates(xd, 1, wr_ref, br_ref, wi_ref, bi_ref, sp)
                hd, cd = _scan_down(a, mult * gi * xd, cd)
                h1_ref[rd, :] = hd
            return cu, cd
        z1 = jnp.zeros((1, HD), F32)
        lax.fori_loop(0, nt // UNROLL, scans, (z1, z1))

        def merge(i, c):
            rows = pl.ds(pl.multiple_of(i * RT, RT), RT)
            yb = (h0_ref[rows, :] + h1_ref[rows, :]) * _gelu(z_ref[1, rows, :].astype(F32))
            mg_ref[rows, :] = (ya_ref[rows, :].astype(F32) + _sigmoid(z_ref[2, rows, :].astype(F32)) * yb).astype(BF16)
            return c
        lax.fori_loop(0, nt, merge, 0)

    col = pl.BlockSpec((S, HD), lambda h: (0, h))
    return pl.pallas_call(
        body, name="lru_fwd", grid=(NH,),
        in_specs=[pl.BlockSpec((3, S, HD), lambda h: (1, 0, h)), col] + _lru_specs(S),
        out_specs=[col, col, col],
        out_shape=[jax.ShapeDtypeStruct((S, D), BF16), jax.ShapeDtypeStruct((S, D), F32), jax.ShapeDtypeStruct((S, D), F32)],
        scratch_shapes=[pltpu.VMEM((S + 2 * PADR, HD), F32), pltpu.VMEM((S, HD), F32)],
        compiler_params=_cp(("parallel",)),
    )(z6, ya, cw, cb, wr, br, wi, bi, lam)


def _loss_head(x, tgt, g):
    S = x.shape[0]

    def body(x_ref, t_ref, g_ref, dx_ref, loss_ref, dg_ref):
        @pl.when(pl.program_id(0) == 0)
        def _():
            loss_ref[...] = jnp.zeros_like(loss_ref)
            dg_ref[...] = jnp.zeros_like(dg_ref)
        xv = x_ref[...]
        xh, _ = _rms_hat(xv)
        e = xh * g_ref[...] - t_ref[...]
        loss_ref[...] += jnp.sum(e * e) * (0.5 / D)
        dx, dgs = _rms_bwd(e * (1.0 / D), xv, g_ref[...])
        dx_ref[...] = dx
        dg_ref[...] += dgs

    return pl.pallas_call(
        body, name="loss_head", grid=(S // TM,),
        in_specs=[pl.BlockSpec((TM, D), lambda i: (i, 0)), pl.BlockSpec((TM, D), lambda i: (i, 0)),
                  pl.BlockSpec((1, D), lambda i: (0, 0))],
        out_specs=[pl.BlockSpec((TM, D), lambda i: (i, 0)), pl.BlockSpec((1, 128), lambda i: (0, 0)),
                   pl.BlockSpec((1, D), lambda i: (0, 0))],
        out_shape=[jax.ShapeDtypeStruct((S, D), F32), jax.ShapeDtypeStruct((1, 128), F32), jax.ShapeDtypeStruct((1, D), F32)],
        compiler_params=_cp(("arbitrary",)),
    )(x, tgt, g)


def _bwd_ffn_out(dx, w_fo, gu, l, after=()):
    S = dx.shape[0]

    def body(dx_ref, w_ref, gu_ref, *rest):
        o_ref = rest[-1]
        d = _dot_nt(dx_ref[...].astype(BF16), w_ref[...])
        ga, gb = gu_ref[0].astype(F32), gu_ref[1].astype(F32)
        sg = _sigmoid(ga)
        o_ref[0] = (d * gb * sg * (1.0 + ga * (1.0 - sg))).astype(BF16)
        o_ref[1] = (d * ga * sg).astype(BF16)

    pair = pl.BlockSpec((2, None, TM, DFF_SH), lambda i, s: (0, s, i, 0))
    dgu = pl.pallas_call(
        body, name=f"bwd_ffn_out_{l}", grid=(S // TM, 2),
        in_specs=[pl.BlockSpec((TM, D), lambda i, s: (i, 0)), pl.BlockSpec((DFF_SH, D), lambda i, s: (s, 0)), pair]
        + [ANY] * len(after),
        out_specs=pair,
        out_shape=jax.ShapeDtypeStruct((2, 2, S, DFF_SH), BF16),
        compiler_params=_cp(("parallel", "arbitrary")),
    )(dx, w_fo, gu.reshape(2, 2, S, DFF_SH), *after)
    return dgu.reshape(4, S, DFF_SH)


def _mm_tn(a, b, m_blk, tk, name):
    S, M = a.shape

    def body(a_ref, b_ref, o_ref):
        @pl.when(pl.program_id(1) == 0)
        def _():
            o_ref[...] = jnp.zeros_like(o_ref)
        o_ref[...] += _dot_tn(a_ref[...], b_ref[...].astype(BF16))

    return pl.pallas_call(
        body, name=name, grid=(M // m_blk, S // tk),
        in_specs=[pl.BlockSpec((tk, m_blk), lambda m, k: (k, m)), pl.BlockSpec((tk, D), lambda m, k: (k, 0))],
        out_specs=pl.BlockSpec((m_blk, D), lambda m, k: (m, 0)),
        out_shape=jax.ShapeDtypeStruct((M, D), F32),
        compiler_params=_cp(("parallel", "arbitrary")),
    )(a, b)


def _mm_nt_rms_bwd(a, a_spec, w, w_spec, nk, tm, x, g, dres, name, after=()):
    S = x.shape[0]

    def body(a_ref, w_ref, x_ref, g_ref, r_ref, *rest):
        dx_ref, dg_ref, acc = rest[-3:]
        i, k = pl.program_id(0), pl.program_id(1)

        @pl.when(k == 0)
        def _():
            acc[...] = jnp.zeros_like(acc)
        acc[...] += _dot_nt(a_ref[...], w_ref[...])

        @pl.when(jnp.logical_and(i == 0, k == 0))
        def _():
            dg_ref[...] = jnp.zeros_like(dg_ref)

        @pl.when(k == nk - 1)
        def _():
            dx, dgs = _rms_bwd(acc[...], x_ref[...], g_ref[...])
            dx_ref[...] = r_ref[...] + dx
            dg_ref[...] += dgs

    row = pl.BlockSpec((tm, D), lambda i, k: (i, 0))
    vec = pl.BlockSpec((1, D), lambda i, k: (0, 0))
    return pl.pallas_call(
        body, name=name, grid=(S // tm, nk),
        in_specs=[a_spec, w_spec, row, vec, row] + [ANY] * len(after),
        out_specs=[row, vec],
        out_shape=[jax.ShapeDtypeStruct((S, D), F32), jax.ShapeDtypeStruct((1, D), F32)],
        scratch_shapes=[pltpu.VMEM((tm, D), F32)],
        compiler_params=_cp(("arbitrary", "arbitrary")),
    )(a, w, x, g, dres, *after)


def _dw_ffn_in(h, dgu, l):
    S = h.shape[0]

    def body(h_ref, b_ref, o_ref):
        @pl.when(pl.program_id(1) == 0)
        def _():
            o_ref[...] = jnp.zeros_like(o_ref)
        o_ref[...] += _dot_tn(h_ref[...], b_ref[...])

    tk = min(2 * TM_BIG if l else TM_BIG, S)
    return pl.pallas_call(
        body, name=f"dw_ffn_in_{l}", grid=(4, S // tk),
        in_specs=[pl.BlockSpec((tk, D), lambda j, k: (k, 0)), pl.BlockSpec((None, tk, DFF_SH), lambda j, k: (j, k, 0))],
        out_specs=pl.BlockSpec((None, D, DFF_SH), lambda j, k: (j, 0, 0)),
        out_shape=jax.ShapeDtypeStruct((4, D, DFF_SH), F32),
        compiler_params=_cp(("parallel", "arbitrary")),
    )(h, dgu)


_HALF_COMPS = ((0, 1, 3), (4, 2, 5))


def _dw_in(h, dz6, l):
    S = h.shape[0]

    def body(h_ref, d0_ref, d1_ref, d2_ref, o_ref):
        @pl.when(pl.program_id(1) == 0)
        def _():
            o_ref[...] = jnp.zeros_like(o_ref)
        hv = h_ref[...]
        for q, d_ref in enumerate((d0_ref, d1_ref, d2_ref)):
            for hf in range(2):
                col = 1024 * q + 512 * hf
                o_ref[col // 1536, :, col % 1536:col % 1536 + 512] += _dot_tn(hv, d_ref[:, 512 * hf:512 * (hf + 1)])

    tk = min(TM_BIG, S) if l else TM

    def comp(q):
        return pl.BlockSpec((None, tk, D), lambda p, k: (jnp.where(p == 0, _HALF_COMPS[0][q], _HALF_COMPS[1][q]), k, 0))

    return pl.pallas_call(
        body, name=f"dw_in_{l}", grid=(2, S // tk),
        in_specs=[pl.BlockSpec((tk, D), lambda p, k: (k, 0)), comp(0), comp(1), comp(2)],
        out_specs=pl.BlockSpec((2, D, 1536), lambda p, k: (p, 0, 0)),
        out_shape=jax.ShapeDtypeStruct((4, D, 1536), F32),
        compiler_params=_cp(("parallel", "arbitrary")),
    )(h, dz6, dz6, dz6)


def _bwd_out(dx, w_o, merged, l):
    S = dx.shape[0]

    def body(dx_ref, w_ref, m_ref, dm_ref, dw_ref):
        @pl.when(pl.program_id(0) == 0)
        def _():
            dw_ref[...] = jnp.zeros_like(dw_ref)
        dxb = dx_ref[...].astype(BF16)
        dm_ref[...] = _dot_nt(dxb, w_ref[...]).astype(BF16)
        dw_ref[...] += _dot_tn(m_ref[...], dxb)

    tm = min(TM_BIG, S) if l else TM
    row = pl.BlockSpec((tm, D), lambda i: (i, 0))
    return pl.pallas_call(
        body, name=f"bwd_out_{l}", grid=(S // tm,),
        in_specs=[row, pl.BlockSpec((D, D), lambda i: (0, 0)), row],
        out_specs=[row, pl.BlockSpec((D, D), lambda i: (0, 0))],
        out_shape=[jax.ShapeDtypeStruct((S, D), BF16), jax.ShapeDtypeStruct((D, D), F32)],
        compiler_params=_cp(("arbitrary",)),
    )(dx, w_o, merged)


def _gmlp_bwd(dm, z6, ws_b, wst_b, bs_b, lg, lb, after=()):
    S = z6.shape[1]

    def body(dm_ref, z_ref, ws_ref, wst_ref, bs_ref, lg_ref, lb_ref, *rest):
        dz_ref, dws_ref, dbs_ref, dlg_ref, dlb_ref, mix, dv = rest[-7:]

        @pl.when(pl.program_id(0) == 0)
        def _():
            dws_ref[...] = jnp.zeros_like(dws_ref)
            dbs_ref[...] = jnp.zeros_like(dbs_ref)
            dlg_ref[...] = jnp.zeros_like(dlg_ref)
            dlb_ref[...] = jnp.zeros_like(dlb_ref)
        gv, dgelu_v = _gelu_and_grad(z_ref[1].astype(F32))
        xc = gv - jnp.mean(gv, axis=-1, keepdims=True)
        rs = lax.rsqrt(jnp.mean(xc * xc, axis=-1, keepdims=True) + EPS)
        vh = xc * rs
        vb = (vh * lg_ref[...] + lb_ref[...]).astype(BF16)
        for gi in range(NH):
            cs = slice(gi * HD, (gi + 1) * HD)
            mix[:, cs] = _dot(ws_ref[gi], vb[:, cs])
        u, dgelu_u = _gelu_and_grad(z_ref[0].astype(F32))
        sa = _sigmoid(z_ref[2].astype(F32))
        mixed = mix[...] + bs_ref[...]
        dyg = dm_ref[...].astype(F32)
        dz_ref[2] = (dyg * u * mixed * sa * (1.0 - sa)).astype(BF16)
        dya = dyg * sa
        dz_ref[0] = (dya * mixed * dgelu_u).astype(BF16)
        dmix = dya * u
        dmb = dmix.astype(BF16)
        for gi in range(NH):
            cs = slice(gi * HD, (gi + 1) * HD)
            dv[:, cs] = _dot(wst_ref[gi], dmb[:, cs])
            dws_ref[gi] += _dot_nt(dmb[:, cs], vb[:, cs])
            dbs_ref[gi] += jnp.broadcast_to(jnp.sum(dmix[:, cs], axis=1, keepdims=True), (CHUNK, HD))
        dvv = dv[...]
        dlg_ref[...] += jnp.sum(dvv * vh, axis=0, keepdims=True)
        dlb_ref[...] += jnp.sum(dvv, axis=0, keepdims=True)
        dvh = dvv * lg_ref[...]
        dgv = rs * (dvh - jnp.mean(dvh, axis=-1, keepdims=True) - vh * jnp.mean(dvh * vh, axis=-1, keepdims=True))
        dz_ref[1] = (dgv * dgelu_v).astype(BF16)

    vec = pl.BlockSpec((1, D), lambda i: (0, 0))
    mat = pl.BlockSpec((NH, CHUNK, CHUNK), lambda i: (0, 0, 0))
    return pl.pallas_call(
        body, name="gmlp_bwd", grid=(S // CHUNK,),
        in_specs=[pl.BlockSpec((CHUNK, D), lambda i: (i, 0)), pl.BlockSpec((3, CHUNK, D), lambda i: (0, i, 0)), mat, mat,
                  pl.BlockSpec((CHUNK, D), lambda i: (0, 0)), vec, vec] + [ANY] * len(after),
        out_specs=[pl.BlockSpec((3, CHUNK, D), lambda i: (0, i, 0)), mat, mat, vec, vec],
        out_shape=[jax.ShapeDtypeStruct((6, S, D), BF16), jax.ShapeDtypeStruct((NH, CHUNK, CHUNK), F32),
                   jax.ShapeDtypeStruct((NH, CHUNK, HD), F32), jax.ShapeDtypeStruct((1, D), F32), jax.ShapeDtypeStruct((1, D), F32)],
        scratch_shapes=[pltpu.VMEM((CHUNK, D), F32), pltpu.VMEM((CHUNK, D), F32)],
        compiler_params=_cp(("arbitrary",)),
    )(dm, z6, ws_b, wst_b, bs_b, lg, lb, *after)


def _lru_bwd(dz6, dm, z6, h0, h1, cw, cb, wr, br, wi, bi, lam):
    S = z6.shape[1]
    nt = S // RT

    def body(dz_in, dm_ref, z_ref, h0_ref, h1_ref, cw_ref, cb_ref, wr_ref, br_ref, wi_ref, bi_ref, lam_ref,
             dz_ref, dcw_ref, dcb_ref, dwr_ref, dbr_ref, dwi_ref, dbi_ref, dlam_ref, zxp, xc_s, dhs_s, dxcp, r_s, lam_s):
        del dz_in
        lam = lam_ref[...]
        sp = _softplus_neg(lam)
        row = _row_iota()
        _fill_padded(zxp, z_ref.at[0], S)
        _conv_fwd_all(zxp, xc_s, cw_ref, cb_ref, S)
        zeros = jnp.zeros((PADR, HD), F32)
        dxcp[0:PADR, :] = zeros
        dxcp[PADR + S:2 * PADR + S, :] = zeros
        dwr_ref[...] = jnp.zeros_like(dwr_ref)
        dwi_ref[...] = jnp.zeros_like(dwi_ref)

        def pre(i, c):
            rows = pl.ds(pl.multiple_of(i * RT, RT), RT)
            hs = h0_ref[rows, :] + h1_ref[rows, :]
            dmv = dm_ref[rows, :].astype(F32)
            sb = _sigmoid(z_ref[2, rows, :].astype(F32))
            gg, dgg = _gelu_and_grad(z_ref[1, rows, :].astype(F32))
            dz_ref[2, rows, :] = (dmv * hs * gg * sb * (1.0 - sb)).astype(BF16)
            dyb = dmv * sb
            dz_ref[1, rows, :] = (dyb * hs * dgg).astype(BF16)
            dhs_s[rows, :] = dyb * gg
            return c
        lax.fori_loop(0, nt, pre, 0)

        def gate_bwd(d, gates, lamv, da, xc):
            r, gi, a, mult = gates
            dmult = lamv * gi * xc
            dgi = lamv * mult * xc
            dlog = (da - dmult * a / mult) * a
            dpr = (dlog * (-LRU_C) * sp[d:d + 1, :]) * r * (1.0 - r)
            dpi = dgi * gi * (1.0 - gi)
            xb, dprb, dpib = xc.astype(BF16), dpr.astype(BF16), dpi.astype(BF16)
            dwr_ref[d] += _dot_tn(xb, dprb)
            dwi_ref[d] += _dot_tn(xb, dpib)
            dxc = lamv * mult * gi + _dot_nt(dprb, wr_ref[d]) + _dot_nt(dpib, wi_ref[d])
            return dxc, (jnp.sum(dlog * r, axis=0, keepdims=True) * (-LRU_C), jnp.sum(dpr, axis=0, keepdims=True),
                         jnp.sum(dpi, axis=0, keepdims=True))

        def rgates(i, c):
            for u in range(UNROLL):
                rows = pl.ds(pl.multiple_of((i * UNROLL + u) * RT, RT), RT)
                xb = xc_s[rows, :].astype(BF16)
                for d in range(2):
                    r_s[d, rows, :] = _sigmoid(_dot(xb, wr_ref[d]) + br_ref[d:d + 1, :])
            return c
        lax.fori_loop(0, nt // UNROLL, rgates, 0)

        def chains(i, carry):
            qn, qp = carry
            for u in range(UNROLL):
                j = i * UNROLL + u
                rd = pl.ds(pl.multiple_of((nt - 1 - j) * RT, RT), RT)
                a, dhs = _decay(r_s[0, rd, :], sp[0:1, :])[0], dhs_s[rd, :]
                q, q_first = _scan_down(a, a * dhs, qn)
                lam_s[0, rd, :] = dhs + jnp.where(row == RT - 1, qn, pltpu.roll(q, RT - 1, 0))
                qn = q_first
                ru = pl.ds(pl.multiple_of(j * RT, RT), RT)
                a, dhs = _decay(r_s[1, ru, :], sp[1:2, :])[0], dhs_s[ru, :]
                q, q_last = _scan_up(a, a * dhs, qp)
                lam_s[1, ru, :] = dhs + jnp.where(row == 0, qp, pltpu.roll(q, 1, 0))
                qp = q_last
            return qn, qp

        z1 = jnp.zeros((1, HD), F32)
        lax.fori_loop(0, nt // UNROLL, chains, (z1, z1))

        ct = min(GRAD_ROWS, S)
        crow = lax.broadcasted_iota(jnp.int32, (ct, HD), 0)

        def tile_grads(i, acc):
            t0 = pl.multiple_of(i * ct, ct)
            rows = pl.ds(t0, ct)
            xc = xc_s[rows, :]
            xb = xc.astype(BF16)
            tp = pl.multiple_of(jnp.maximum(t0 - PADR, 0), PADR)
            prev = jnp.where(t0 > 0, h0_ref[pl.ds(tp, PADR), :][PADR - 1:PADR, :], 0.0)
            tn = pl.multiple_of(jnp.minimum(t0 + ct, S - PADR), PADR)
            nxt = jnp.where(t0 + ct < S, h1_ref[pl.ds(tn, PADR), :][0:1, :], 0.0)
            hside = (jnp.where(crow == 0, prev, pltpu.roll(h0_ref[rows, :], 1, 0)),
                     jnp.where(crow == ct - 1, nxt, pltpu.roll(h1_ref[rows, :], ct - 1, 0)))
            dxc, sums = 0.0, ()
            for d in range(2):
                r = r_s[d, rows, :]
                gi = _sigmoid(_dot(xb, wi_ref[d]) + bi_ref[d:d + 1, :])
                a, mult = _decay(r, sp[d:d + 1, :])
                lamv = lam_s[d, rows, :]
                dxc_d, s_d = gate_bwd(d, (r, gi, a, mult), lamv, lamv * hside[d], xc)
                dxc = dxc + dxc_d
                sums = sums + s_d
            dxcp[pl.ds(t0 + PADR, ct), :] = dxc
            return tuple(x + y for x, y in zip(acc, sums))

        s_sp0, s_br0, s_bi0, s_sp1, s_br1, s_bi1 = lax.fori_loop(0, S // ct, tile_grads, (z1,) * 6)

        dsp = jnp.concatenate([s_sp0, s_sp1], axis=0)
        dlam_ref[...] = -dsp * _sigmoid(-lam)
        dbr_ref[...] = jnp.concatenate([s_br0, s_br1], axis=0)
        dbi_ref[...] = jnp.concatenate([s_bi0, s_bi1], axis=0)

        def conv_bwd(i, carry):
            c0, c1, c2, c3, cb_ = carry
            t0 = pl.multiple_of(i * RT, RT)
            dwin = dxcp[pl.ds(t0, RT + 2 * PADR), :]
            d0 = _shifted(dwin, 0)
            dz_ref[0, pl.ds(t0, RT), :] = (_shifted(dwin, 1) * cw_ref[0:1, :] + d0 * cw_ref[1:2, :]
                                           + _shifted(dwin, -1) * cw_ref[2:3, :] + _shifted(dwin, -2) * cw_ref[3:4, :]).astype(BF16)
            xm1, x0, xp1, xp2 = _conv_taps(zxp[pl.ds(t0, RT + 2 * PADR), :])
            sm = lambda v: jnp.sum(v, axis=0, keepdims=True)
            return c0 + sm(d0 * xm1), c1 + sm(d0 * x0), c2 + sm(d0 * xp1), c3 + sm(d0 * xp2), cb_ + sm(d0)

        c0, c1, c2, c3, cb_ = lax.fori_loop(0, nt, conv_bwd, (z1, z1, z1, z1, z1))
        dcw_ref[...] = jnp.concatenate([c0, c1, c2, c3], axis=0)
        dcb_ref[...] = cb_

    col = pl.BlockSpec((S, HD), lambda h: (0, h))
    head = lambda h: (0, h)
    wspec = pl.BlockSpec((2, None, HD, HD), lambda h: (0, h, 0, 0))
    return pl.pallas_call(
        body, name="lru_bwd", grid=(NH,),
        in_specs=[pl.BlockSpec(memory_space=pl.ANY), col, pl.BlockSpec((3, S, HD), lambda h: (1, 0, h)), col, col] + _lru_specs(S),
        out_specs=[pl.BlockSpec((3, S, HD), lambda h: (1, 0, h)), pl.BlockSpec((4, HD), head), pl.BlockSpec((1, HD), head),
                   wspec, pl.BlockSpec((2, HD), head), wspec, pl.BlockSpec((2, HD), head), pl.BlockSpec((2, HD), head)],
        out_shape=[jax.ShapeDtypeStruct((6, S, D), BF16), jax.ShapeDtypeStruct((4, D), F32), jax.ShapeDtypeStruct((1, D), F32),
                   jax.ShapeDtypeStruct((2, NH, HD, HD), F32), jax.ShapeDtypeStruct((2, D), F32),
                   jax.ShapeDtypeStruct((2, NH, HD, HD), F32), jax.ShapeDtypeStruct((2, D), F32), jax.ShapeDtypeStruct((2, D), F32)],
        scratch_shapes=[pltpu.VMEM((S + 2 * PADR, HD), F32), pltpu.VMEM((S, HD), F32), pltpu.VMEM((S, HD), F32),
                        pltpu.VMEM((S + 2 * PADR, HD), F32), pltpu.VMEM((2, S, HD), F32), pltpu.VMEM((2, S, HD), F32)],
        input_output_aliases={0: 0},
        compiler_params=_cp(("parallel",)),
    )(dz6, dm, z6, h0, h1, cw, cb, wr, br, wi, bi, lam)


LAYER_SMALL = ("norm1_g", "gmlp_ln_g", "gmlp_ln_b", "gmlp_w_s", "gmlp_b_s", "conv_w", "conv_b",
               "lru_w_r", "lru_b_r", "lru_w_i", "lru_b_i", "lru_lambda", "norm2_g")


def _forward_layer(l, x, p, wb, after=(), rest=None):
    g1, g2 = p["norm1_g"][l][None], p["norm2_g"][l][None]
    ws_b = p["gmlp_w_s"][l].astype(BF16)
    tm = dict(ws_b=ws_b, wst_b=jnp.swapaxes(ws_b, 1, 2), bs_b=jnp.repeat(p["gmlp_b_s"][l].T, HD, axis=1),
              lg=p["gmlp_ln_g"][l][None], lb=p["gmlp_ln_b"][l][None])
    lru = (p["conv_w"][l], p["conv_b"][l][None], p["lru_w_r"][l].astype(BF16), p["lru_b_r"][l],
           p["lru_w_i"][l].astype(BF16), p["lru_b_i"][l], p["lru_lambda"][l])
    z6, hn1 = _mm_in(x, g1, wb["w_in"], l, after)
    ya = _gmlp_fwd(z6, tm["ws_b"], tm["bs_b"], tm["lg"], tm["lb"])
    merged, h0, h1 = _lru_fwd(z6, ya, *lru)
    if rest is not None:
        wb = dict(wb, **rest(merged))
    x1 = _mm_res(merged, wb["w_out"], x, l, "mm_out")
    gu, ff, hn2 = _mm_ffn_in(x1, g2, wb["w_ffn_in"], l)
    x2 = _mm_res(ff, wb["w_ffn_out"], x1, l, "mm_ffn_out")
    return x2, dict(x=x, z6=z6, h0=h0, h1=h1, merged=merged, x1=x1, gu=gu, ff=ff, g1=g1, g2=g2, tm=tm, lru=lru,
                    hn1=hn1, hn2=hn2, wb=wb)


def _backward_layer(l, dx, s, after=(), midway=None, late=None):
    S = dx.shape[0]
    tm, wb = s["tm"], s["wb"]
    g2 = s["g2"]
    dgu = _bwd_ffn_out(dx, wb["w_ffn_out"], s["gu"], l, after)
    tmb = min(TM_BIG, S)
    dwfo = _mm_tn(s["ff"], dx, DFF_SH, min(2 * TM_BIG, S) if l else tmb, f"dw_ffn_out_{l}")
    dx1, dg2 = _mm_nt_rms_bwd(
        dgu, pl.BlockSpec((None, tmb, DFF_SH), lambda i, k: (k, i, 0)),
        wb["w_ffn_in"], pl.BlockSpec((None, D, DFF_SH), lambda i, k: (k, 0, 0)),
        4, tmb, s["x1"], g2, dx, f"bwd_ffn_in_{l}")
    dwfi = _dw_ffn_in(s["hn2"], dgu, l)
    dmg, dwo = _bwd_out(dx1, wb["w_out"], s["merged"], l)
    mid = () if midway is None else tuple(midway([dwo, dwfi, dwfo]))
    dz6, dws, dbs, dlg, dlb = _gmlp_bwd(dmg, s["z6"], tm["ws_b"], tm["wst_b"], tm["bs_b"], tm["lg"], tm["lb"], mid)
    dz6, dcw, dcb, dwr, dbr, dwi, dbi, dlam = _lru_bwd(dz6, dmg, s["z6"], s["h0"], s["h1"], *s["lru"])

    def w_map(i, k):
        sh, tl = _in_tile(k)
        return (sh, 0, tl)

    dwin = _dw_in(s["hn1"], dz6, l)
    tail = () if late is None else tuple(late([dwin]))
    dx0, dg1 = _mm_nt_rms_bwd(
        dz6, pl.BlockSpec((None, tmb, 512), lambda i, k: (k // 2, i, k % 2)),
        wb["w_in"], pl.BlockSpec((None, D, 512), w_map),
        N_IN_T, tmb, s["x"], s["g1"], dx1, f"bwd_in_{l}", tail)
    small = dict(norm1_g=dg1[0], gmlp_ln_g=dlg[0], gmlp_ln_b=dlb[0], gmlp_w_s=dws, gmlp_b_s=dbs[:, :, 0], conv_w=dcw, conv_b=dcb[0],
                 lru_w_r=dwr, lru_b_r=dbr, lru_w_i=dwi, lru_b_i=dbi, lru_lambda=dlam, norm2_g=dg2[0])
    return dx0, [dwin, dwo, dwfi, dwfo], small


def _local_step(x, tgt, p, wbs):
    saved = []
    for l in range(2):
        x, s = _forward_layer(l, x, p, wbs[l])
        saved.append(s)
    dx, loss_v, dfg = _loss_head(x, tgt, p["final_g"][None])
    big, smalls = [None, None], [None, None]
    for l in (1, 0):
        dx, big[l], smalls[l] = _backward_layer(l, dx, saved[l])
    small = {k: jnp.stack([smalls[0][k], smalls[1][k]]) for k in LAYER_SMALL}
    small["final_g"] = dfg[0]
    return loss_v, dx, big, small


def _place():
    x, y, c = lax.axis_index("x"), lax.axis_index("y"), lax.axis_index("c")
    return x, y, c, 2 * x + y


def _chip_at(x, y, d):
    px = 1 - x if d & 2 else x
    py = 1 - y if d & 1 else y
    return px, py, 2 * px + py


HBM = pl.BlockSpec(memory_space=pltpu.HBM)
SEM = pl.BlockSpec(memory_space=pltpu.SEMAPHORE)
DATAFLOW = pltpu.SideEffectType.DATAFLOW_SIDE_EFFECTING


def _in_hbm(a):
    return pltpu.with_memory_space_constraint(a, pltpu.HBM)


def _cast_into(wf, l, chip_arr, name):
    _, rows, cols = wf.shape
    rh = rows // 2

    def body(ch_ref, w_ref, o_ref):
        o_ref[...] = w_ref[...].astype(BF16)

    return pl.pallas_call(
        body, name=name, out_shape=jax.ShapeDtypeStruct((4, 2, rh, cols), BF16),
        grid_spec=pltpu.PrefetchScalarGridSpec(
            num_scalar_prefetch=1, grid=(2,),
            in_specs=[pl.BlockSpec((None, None, rh, cols), lambda h, ch: (l, h, 0, 0))],
            out_specs=pl.BlockSpec((None, None, rh, cols), lambda h, ch: (ch[0], h, 0, 0))),
        compiler_params=_cp(("parallel",)),
    )(chip_arr, wf.reshape(2, 2, rh, cols))


def _half_block(ref, chip, half, to, send_sem, recv_sem):
    blk = ref.at[chip, half]
    return pltpu.make_async_remote_copy(src_ref=blk, dst_ref=blk, send_sem=send_sem, recv_sem=recv_sem,
                                        device_id=to, device_id_type=MESH)


def _gather_weights(bufs, tiny):
    nt = len(bufs)
    n_ici = nt * 3

    def body(*refs):
        tiny_ref = refs[nt]
        o_refs, tiny_o = refs[nt + 1:2 * nt + 1], refs[2 * nt + 1]
        send, recv, fsend, frecv, tsend, trecv, lsem = refs[2 * nt + 2:]
        x, y, c, chip = _place()
        local = pltpu.make_async_copy(tiny_ref, tiny_o.at[chip], lsem)
        local.start()

        def tin(d, origin_chip, to):
            return pltpu.make_async_remote_copy(
                src_ref=tiny_ref, dst_ref=tiny_o.at[origin_chip], send_sem=tsend.at[d - 1], recv_sem=trecv.at[d - 1],
                device_id=to, device_id_type=MESH)

        sends = []
        for t in range(nt):
            for d in (1, 2, 3):
                px, py, _ = _chip_at(x, y, d)
                sends.append(_half_block(o_refs[t], chip, c, (px, py, c), send.at[3 * t + d - 1], recv.at[3 * t + d - 1]))
        for d in (1, 2, 3):
            px, py, _ = _chip_at(x, y, d)
            sends.append(tin(d, chip, (px, py, c)))
        for cp in sends:
            cp.start()
        passed = []
        for t in range(nt):
            for d in (1, 2, 3):
                k = 3 * t + d - 1
                _, _, pchip = _chip_at(x, y, d)
                _half_block(o_refs[t], pchip, c, (x, y, c), send.at[k], recv.at[k]).wait_recv()
                f = _half_block(o_refs[t], pchip, c, (x, y, 1 - c), fsend.at[k], frecv.at[k])
                f.start()
                passed.append(f)
        for t in range(nt):
            for d in (1, 2, 3):
                k = 3 * t + d - 1
                _, _, pchip = _chip_at(x, y, d)
                _half_block(o_refs[t], pchip, 1 - c, (x, y, 1 - c), fsend.at[k], frecv.at[k]).wait_recv()
        for d in (1, 2, 3):
            _, _, pchip = _chip_at(x, y, d)
            tin(d, pchip, (x, y, c)).wait_recv()
        for cp in sends + passed:
            cp.wait_send()
        local.wait()

    out_shape = [jax.ShapeDtypeStruct(b.shape, b.dtype) for b in bufs]
    out_shape.append(jax.ShapeDtypeStruct((4,) + tiny.shape, tiny.dtype))
    outs = pl.pallas_call(
        body, name="gather_weights_0", out_shape=out_shape,
        in_specs=[ANY] * (nt + 1), out_specs=[ANY] * (nt + 1),
        scratch_shapes=[pltpu.SemaphoreType.DMA((n_ici,)), pltpu.SemaphoreType.DMA((n_ici,)),
                        pltpu.SemaphoreType.DMA((n_ici,)), pltpu.SemaphoreType.DMA((n_ici,)),
                        pltpu.SemaphoreType.DMA((3,)), pltpu.SemaphoreType.DMA((3,)), pltpu.SemaphoreType.DMA],
        input_output_aliases={t: t for t in range(nt)},
        compiler_params=_cp(has_side_effects=True),
    )(*bufs, tiny)
    return outs[:nt], outs[nt]


def _gather_start(bufs, tag, after=()):
    nt, na = len(bufs), len(after)

    def body(*refs):
        b_refs = refs[:nt]
        send, recv = refs[nt + na], refs[nt + na + 1]
        token = refs[2 * nt + na + 2]
        x, y, c, chip = _place()
        for t in range(nt):
            for d in (1, 2, 3):
                px, py, _ = _chip_at(x, y, d)
                _half_block(b_refs[t], chip, c, (px, py, c), send.at[3 * t + d - 1], recv.at[3 * t + d - 1]).start()
        token[...] = jnp.zeros_like(token)

    outs = pl.pallas_call(
        body, name=f"gather_start_{tag}",
        out_shape=(pltpu.SemaphoreType.DMA((3 * nt,)), pltpu.SemaphoreType.DMA((3 * nt,)),
                   *[pltpu.HBM(b.shape, b.dtype) for b in bufs], jax.ShapeDtypeStruct((8, 128), F32)),
        in_specs=[HBM] * nt + [ANY] * na, out_specs=(SEM, SEM, *[HBM] * nt, pl.BlockSpec(memory_space=pltpu.VMEM)),
        input_output_aliases={t: 2 + t for t in range(nt)},
        compiler_params=pltpu.CompilerParams(has_side_effects=DATAFLOW),
    )(*[_in_hbm(b) for b in bufs], *after)
    return outs[0], outs[1], list(outs[2:2 + nt]), outs[2 + nt]


def _gather_wait(send, recv, bufs, after, tag):
    nt = len(bufs)

    def body(*refs):
        b_refs = refs[:nt]
        send_ref, recv_ref = refs[nt], refs[nt + 1]
        x, y, c, chip = _place()
        for t in range(nt):
            for d in (1, 2, 3):
                k = 3 * t + d - 1
                px, py, pchip = _chip_at(x, y, d)
                _half_block(b_refs[t], chip, c, (px, py, c), send_ref.at[k], recv_ref.at[k]).wait_send()
                _half_block(b_refs[t], pchip, c, (px, py, c), send_ref.at[k], recv_ref.at[k]).wait_recv()

    outs = pl.pallas_call(
        body, name=f"gather_wait_{tag}", out_shape=[pltpu.HBM(b.shape, b.dtype) for b in bufs],
        in_specs=[HBM] * nt + [SEM, SEM, ANY], out_specs=[HBM] * nt,
        input_output_aliases={t: t for t in range(nt)},
        compiler_params=pltpu.CompilerParams(has_side_effects=DATAFLOW),
    )(*bufs, send, recv, after)
    return list(outs)


def _gather_pass_on(bufs, tag):
    nt = len(bufs)

    def body(*refs):
        o_refs = refs[nt:2 * nt]
        fsend, frecv = refs[2 * nt:]
        x, y, c, _ = _place()
        cps = []
        for t in range(nt):
            for d in (1, 2, 3):
                k = 3 * t + d - 1
                _, _, pchip = _chip_at(x, y, d)
                cps.append(_half_block(o_refs[t], pchip, c, (x, y, 1 - c), fsend.at[k], frecv.at[k]))
        for cp in cps:
            cp.start()
        for t in range(nt):
            for d in (1, 2, 3):
                k = 3 * t + d - 1
                _, _, pchip = _chip_at(x, y, d)
                _half_block(o_refs[t], pchip, 1 - c, (x, y, 1 - c), fsend.at[k], frecv.at[k]).wait_recv()
        for cp in cps:
            cp.wait_send()

    return pl.pallas_call(
        body, name=f"gather_pass_on_{tag}", out_shape=[jax.ShapeDtypeStruct(b.shape, b.dtype) for b in bufs],
        in_specs=[ANY] * nt, out_specs=[ANY] * nt,
        scratch_shapes=[pltpu.SemaphoreType.DMA((3 * nt,)), pltpu.SemaphoreType.DMA((3 * nt,))],
        input_output_aliases={t: t for t in range(nt)},
        compiler_params=_cp(has_side_effects=True),
    )(*bufs)


def _to_sibling_halves(gs, l):
    nt = len(gs)

    def body(*refs):
        g_refs, o_refs = refs[:nt], refs[nt:2 * nt]
        send, recv = refs[2 * nt:]
        x, y, c, _ = _place()
        cps = [pltpu.make_async_remote_copy(
            src_ref=g_refs[t].at[k, 1 - c], dst_ref=o_refs[t].at[k], send_sem=send.at[4 * t + k], recv_sem=recv.at[4 * t + k],
            device_id=(x, y, 1 - c), device_id_type=MESH) for t in range(nt) for k in range(4)]
        for cp in cps:
            cp.start()
        for cp in cps:
            cp.wait()

    return pl.pallas_call(
        body, name=f"grads_to_sibling_{l}", out_shape=[jax.ShapeDtypeStruct((4,) + g.shape[2:], g.dtype) for g in gs],
        in_specs=[ANY] * nt, out_specs=[ANY] * nt,
        scratch_shapes=[pltpu.SemaphoreType.DMA((4 * nt,)), pltpu.SemaphoreType.DMA((4 * nt,))],
        compiler_params=_cp(has_side_effects=True),
    )(*gs)


def _chip_copy(c_ref, land_ref, x, y, c, d, send_sem, recv_sem):
    px, py, pchip = _chip_at(x, y, d)
    return pltpu.make_async_remote_copy(src_ref=c_ref.at[pchip], dst_ref=land_ref.at[d - 1], send_sem=send_sem, recv_sem=recv_sem,
                                        device_id=(px, py, c), device_id_type=MESH)


def _exchange_start(srcs, lands, copies, nsem, name):
    ns, n = len(srcs), len(srcs) + len(lands)

    def body(*refs):
        for cp in copies(refs[:ns], refs[ns:n], refs[n], refs[n + 1]):
            cp.start()
        token = refs[2 * n + 2]
        token[...] = jnp.zeros_like(token)

    outs = pl.pallas_call(
        body, name=name,
        out_shape=(pltpu.SemaphoreType.DMA((nsem,)), pltpu.SemaphoreType.DMA((nsem,)),
                   *[pltpu.HBM(a.shape, a.dtype) for a in list(srcs) + list(lands)], jax.ShapeDtypeStruct((8, 128), F32)),
        in_specs=[HBM] * n, out_specs=(SEM, SEM, *[HBM] * n, pl.BlockSpec(memory_space=pltpu.VMEM)),
        input_output_aliases={i: 2 + i for i in range(n)},
        compiler_params=pltpu.CompilerParams(has_side_effects=DATAFLOW),
    )(*[_in_hbm(a) for a in list(srcs) + list(lands)])
    return outs[0], outs[1], list(outs[2:2 + ns]), list(outs[2 + ns:2 + n]), outs[2 + n]


def _exchange_wait(send, recv, srcs, lands, after, copies, name):
    ns, n = len(srcs), len(srcs) + len(lands)

    def body(*refs):
        for cp in copies(refs[:ns], refs[ns:n], refs[n], refs[n + 1]):
            cp.wait_send()
            cp.wait_recv()

    outs = pl.pallas_call(
        body, name=name, out_shape=[pltpu.HBM(a.shape, a.dtype) for a in list(srcs) + list(lands)],
        in_specs=[HBM] * n + [SEM, SEM, ANY], out_specs=[HBM] * n,
        input_output_aliases={i: i for i in range(n)},
        compiler_params=pltpu.CompilerParams(has_side_effects=DATAFLOW),
    )(*srcs, *lands, send, recv, after)
    return list(outs[:ns]), list(outs[ns:])


def _chips_copies(c_refs, land_refs, send, recv):
    x, y, c, _ = _place()
    return [_chip_copy(c_refs[t], land_refs[t], x, y, c, d, send.at[3 * t + d - 1], recv.at[3 * t + d - 1])
            for t in range(len(c_refs)) for d in (1, 2, 3)]


def _sibling_copies(g_refs, land_refs, send, recv):
    x, y, c, _ = _place()
    return [pltpu.make_async_remote_copy(
        src_ref=g_refs[t].at[k, 1 - c], dst_ref=land_refs[t].at[k], send_sem=send.at[4 * t + k], recv_sem=recv.at[4 * t + k],
        device_id=(x, y, 1 - c), device_id_type=MESH) for t in range(len(g_refs)) for k in range(4)]


def _join_halves(fs, l):
    nt = len(fs)

    def body(*refs):
        o_refs = refs[nt:2 * nt]
        send, recv = refs[2 * nt:]
        x, y, c, _ = _place()
        cps = [pltpu.make_async_remote_copy(
            src_ref=o_refs[t].at[c], dst_ref=o_refs[t].at[c], send_sem=send.at[t], recv_sem=recv.at[t],
            device_id=(x, y, 1 - c), device_id_type=MESH) for t in range(nt)]
        for cp in cps:
            cp.start()
        for cp in cps:
            cp.wait()

    return pl.pallas_call(
        body, name=f"grads_join_{l}", out_shape=[jax.ShapeDtypeStruct(a.shape, a.dtype) for a in fs],
        in_specs=[ANY] * nt, out_specs=[ANY] * nt,
        scratch_shapes=[pltpu.SemaphoreType.DMA((nt,)), pltpu.SemaphoreType.DMA((nt,))],
        input_output_aliases={t: t for t in range(nt)},
        compiler_params=_cp(has_side_effects=True),
    )(*fs)


def _add_half(g, r, c_arr, name):
    _, _, rh, cols = g.shape

    def body(c_ref, g_ref, r_ref, o_ref):
        o_ref[...] = (g_ref[...] + r_ref[...]).astype(BF16)

    blk = pl.BlockSpec((None, rh, cols), lambda k, cr: (k, 0, 0))
    return pl.pallas_call(
        body, name=name, out_shape=jax.ShapeDtypeStruct((4, rh, cols), BF16),
        grid_spec=pltpu.PrefetchScalarGridSpec(
            num_scalar_prefetch=1, grid=(4,),
            in_specs=[pl.BlockSpec((None, None, rh, cols), lambda k, cr: (k, cr[0], 0, 0)), blk], out_specs=blk),
        compiler_params=_cp(("parallel",)),
    )(c_arr, g, r)


def _sum_chips(cs, r3, place_arr, name):
    _, rh, cols = cs.shape
    rb = rh // 2

    def body(pl_ref, a_ref, r0_ref, r1_ref, r2_ref, o_ref):
        up = lambda ref: ref[...].astype(F32)
        o_ref[...] = ((up(a_ref) + up(r0_ref)) + up(r1_ref)) + up(r2_ref)

    def slot(d):
        return pl.BlockSpec((None, rb, cols), lambda i, pa: (d, i, 0))

    return pl.pallas_call(
        body, name=name, out_shape=jax.ShapeDtypeStruct((2, rh, cols), F32),
        grid_spec=pltpu.PrefetchScalarGridSpec(
            num_scalar_prefetch=1, grid=(2,),
            in_specs=[pl.BlockSpec((None, rb, cols), lambda i, pa: (pa[0], i, 0)), slot(0), slot(1), slot(2)],
            out_specs=pl.BlockSpec((None, rb, cols), lambda i, pa: (pa[1], i, 0))),
        compiler_params=_cp(("parallel",)),
    )(place_arr, cs, r3, r3, r3)


def _allreduce_small(pack):
    rows = pack.shape[0]
    hr = rows // 2

    def body(p_ref, o_ref, sib, slots, s1, r1, s2, r2, s3, r3):
        x, y, c, chip = _place()
        sibling = (x, y, 1 - c)
        ex = pltpu.make_async_remote_copy(src_ref=p_ref, dst_ref=sib, send_sem=s1, recv_sem=r1,
                                          device_id=sibling, device_id_type=MESH)
        ex.start()
        ex.wait()
        half = pl.ds(pl.multiple_of(c * hr, 16), hr)
        slots[0] = (p_ref[half, :] + sib[half, :]).astype(BF16)
        cps = []
        for d in (1, 2, 3):
            px, py, _ = _chip_at(x, y, d)
            cps.append(pltpu.make_async_remote_copy(
                src_ref=slots.at[0], dst_ref=slots.at[d], send_sem=s2.at[d - 1], recv_sem=r2.at[d - 1],
                device_id=(px, py, c), device_id_type=MESH))
        for cp in cps:
            cp.start()
        for cp in cps:
            cp.wait()
        tot = slots[chip].astype(F32)
        for k in (1, 2, 3):
            tot = tot + slots[jnp.bitwise_xor(chip, k)].astype(F32)
        o_ref[half, :] = tot
        back = pltpu.make_async_remote_copy(src_ref=o_ref.at[half, :], dst_ref=o_ref.at[half, :], send_sem=s3, recv_sem=r3,
                                            device_id=sibling, device_id_type=MESH)
        back.start()
        back.wait()

    vm = pl.BlockSpec(memory_space=pltpu.VMEM)
    return pl.pallas_call(
        body, name="allreduce_small", out_shape=jax.ShapeDtypeStruct((rows, 128), F32),
        in_specs=[vm], out_specs=vm,
        scratch_shapes=[pltpu.VMEM((rows, 128), F32), pltpu.VMEM((4, hr, 128), BF16),
                        pltpu.SemaphoreType.DMA, pltpu.SemaphoreType.DMA, pltpu.SemaphoreType.DMA((3,)), pltpu.SemaphoreType.DMA((3,)),
                        pltpu.SemaphoreType.DMA, pltpu.SemaphoreType.DMA],
        compiler_params=_cp(has_side_effects=True),
    )(pack)


def _adam_math(gv, wv, mv, vv):
    m2 = ADAM_B1 * mv + (1.0 - ADAM_B1) * gv
    v2 = ADAM_B2 * vv + (1.0 - ADAM_B2) * (gv * gv)
    m_hat = m2 / (1.0 - ADAM_B1 ** ADAM_STEP)
    v_hat = v2 / (1.0 - ADAM_B2 ** ADAM_STEP)
    return -ADAM_LR * (m_hat / (jnp.sqrt(v_hat) + ADAM_EPS) + ADAM_WD * wv), m2, v2


def _adam(g, w, m, v, name):
    rows, cols = g.shape
    rb = rows // 4

    def body(g_ref, w_ref, m_ref, v_ref, d_ref, m2_ref, v2_ref):
        d_ref[...], m2_ref[...], v2_ref[...] = _adam_math(g_ref[...], w_ref[...], m_ref[...], v_ref[...])

    blk = pl.BlockSpec((rb, cols), lambda i: (i, 0))
    shp = jax.ShapeDtypeStruct((rows, cols), F32)
    return pl.pallas_call(
        body, name=name, grid=(4,), in_specs=[blk] * 4, out_specs=[blk] * 3, out_shape=[shp] * 3,
        compiler_params=_cp(("parallel",)),
    )(g, w, m, v)


def _adam_layer(g, w, m, v, l, prev, name):
    rows, cols = g.shape
    rb = rows // 4

    def body(g_ref, w_ref, m_ref, v_ref, *rest):
        go_ref, d_ref, m2_ref, v2_ref = rest[-4:]
        gv = g_ref[...]
        go_ref[...] = gv
        d_ref[...], m2_ref[...], v2_ref[...] = _adam_math(gv, w_ref[...], m_ref[...], v_ref[...])

    lay = pl.BlockSpec((None, rb, cols), lambda i: (l, i, 0))
    shp = jax.ShapeDtypeStruct((2, rows, cols), F32)
    prev = () if prev is None else tuple(prev)
    return pl.pallas_call(
        body, name=name, grid=(4,), in_specs=[pl.BlockSpec((rb, cols), lambda i: (i, 0)), lay, lay, lay] + [ANY] * len(prev),
        out_specs=[lay] * 4, out_shape=[shp] * 4,
        input_output_aliases={4 + j: j for j in range(len(prev))},
        compiler_params=_cp(("parallel",)),
    )(g, w, m, v, *prev)


def _rows128(a):
    return a.reshape(-1, 128)


def _pack(arrs, mult):
    parts = [_rows128(a) for a in arrs]
    rows = sum(q.shape[0] for q in parts)
    pad = -rows % mult
    if pad:
        parts.append(jnp.zeros((pad, 128), F32))
    return jnp.concatenate(parts, axis=0)


def _unpack(pack, shapes):
    out, o = [], 0
    for s in shapes:
        n = 1
        for e in s:
            n *= e
        out.append(pack[o:o + n // 128].reshape(s))
        o += n // 128
    return out


WEIGHTS = ['norm1_g', 'w_in', 'gmlp_ln_g', 'gmlp_ln_b', 'gmlp_w_s', 'gmlp_b_s', 'conv_w', 'conv_b', 'lru_w_r', 'lru_b_r', 'lru_w_i',
           'lru_b_i', 'lru_lambda', 'w_out', 'norm2_g', 'w_ffn_in', 'w_ffn_out', 'final_g']
BIG = ['w_in', 'w_out', 'w_ffn_in', 'w_ffn_out']
SMALL = [n for n in WEIGHTS if n not in BIG]
CHIP_SHARDED_SMALL = ['conv_w', 'lru_b_r', 'lru_b_i', 'lru_lambda']


def kernel(x, norm1_g, w_in, gmlp_ln_g, gmlp_ln_b, gmlp_w_s, gmlp_b_s, conv_w, conv_b, lru_w_r, lru_b_r, lru_w_i, lru_b_i, lru_lambda, w_out, norm2_g, w_ffn_in, w_ffn_out, final_g, loss_target, m_norm1_g, m_w_in, m_gmlp_ln_g, m_gmlp_ln_b, m_gmlp_w_s, m_gmlp_b_s, m_conv_w, m_conv_b, m_lru_w_r, m_lru_b_r, m_lru_w_i, m_lru_b_i, m_lru_lambda, m_w_out, m_norm2_g, m_w_ffn_in, m_w_ffn_out, m_final_g, v_norm1_g, v_w_in, v_gmlp_ln_g, v_gmlp_ln_b, v_gmlp_w_s, v_gmlp_b_s, v_conv_w, v_conv_b, v_lru_w_r, v_lru_b_r, v_lru_w_i, v_lru_b_i, v_lru_lambda, v_w_out, v_norm2_g, v_w_ffn_in, v_w_ffn_out, v_final_g):
    a = dict(locals())
    w = {n: a[n] for n in WEIGHTS}
    mom = {n: a["m_" + n] for n in WEIGHTS}
    var = {n: a["v_" + n] for n in WEIGHTS}
    _, _, c, chip = _place()
    c_arr, chip_arr = jnp.reshape(c, (1,)).astype(jnp.int32), jnp.reshape(chip, (1,)).astype(jnp.int32)
    place_arr = jnp.stack([chip, c]).astype(jnp.int32)

    first, rest = BIG[:1], BIG[1:]

    def as_weights(names, full):
        wb = {n: f.reshape(4, 2 * f.shape[2], f.shape[3]) for n, f in zip(names, full)}
        if "w_out" in wb:
            wb["w_out"] = wb["w_out"].reshape(D, D)
            wb["w_ffn_out"] = wb["w_ffn_out"].reshape(DFF, D)
        return wb

    bufs = [{n: _cast_into(w[n], l, chip_arr, f"cast_{n}_{l}") for n in BIG} for l in range(2)]
    tiny = _pack([w[n] for n in CHIP_SHARDED_SMALL], 8)
    w_in0, tiny_full = _gather_weights([bufs[0]["w_in"]], tiny)
    fly0 = _gather_start([bufs[0][n] for n in rest], "0", after=(tiny_full,))
    fly1 = _gather_start([bufs[1][n] for n in BIG], "1", after=(fly0[3],))
    p = {n: w[n] for n in SMALL}
    parts = [_unpack(tiny_full[k], [w[n].shape for n in CHIP_SHARDED_SMALL]) for k in range(4)]
    for i, n in enumerate(CHIP_SHARDED_SMALL):
        p[n] = jnp.concatenate([parts[k][i] for k in range(4)], axis=-1)

    def landed(fly, names, after, tag):
        return as_weights(names, _gather_pass_on(_gather_wait(fly[0], fly[1], fly[2], after, tag), tag))

    xa, saved0 = _forward_layer(0, x[0], p, as_weights(first, w_in0), after=(fly0[3], fly1[3]),
                                rest=lambda merged: landed(fly0, rest, merged, "0"))
    xb, saved1 = _forward_layer(1, xa, p, landed(fly1, BIG, xa, "1"))
    dxb, loss_v, dfg = _loss_head(xb, loss_target[0], p["final_g"][None])
    loss = lax.psum(loss_v[0, 0], ("x", "y", "c"))

    out, flying = {}, {}

    def halves(grads):
        return [g.reshape(4, 2, -1, g.shape[-1]) for g in grads]

    def sibling_start(grads, names, l, tag):
        gs = halves(grads)
        lands = [lax.empty((4,) + g.shape[2:], g.dtype) for g in gs]
        flying["s" + tag] = (names, l) + tuple(
            _exchange_start(gs, lands, _sibling_copies, 4 * len(gs), f"grads_to_sibling_start_{tag}"))
        return (flying["s" + tag][-1],)

    def chips_start(gs, from_sib, names, l, tag):
        cs = [_add_half(g, r, c_arr, f"add_half_{n}_{l}") for n, g, r in zip(names, gs, from_sib)]
        lands = [lax.empty((3,) + a.shape[1:], a.dtype) for a in cs]
        flying[tag] = (names, l) + tuple(_exchange_start(cs, lands, _chips_copies, 3 * len(cs), f"grads_to_chips_start_{tag}"))
        return (flying[tag][-1],)

    def sibling_finish(tag, after):
        names, l, send, recv, gs, lands, _ = flying["s" + tag]
        gs, from_sib = _exchange_wait(send, recv, gs, lands, after, _sibling_copies, f"grads_to_sibling_wait_{tag}")
        return chips_start(gs, from_sib, names, l, tag)

    def reduce_start(grads, names, l, tag):
        gs = halves(grads)
        return chips_start(gs, _to_sibling_halves(gs, tag), names, l, tag)

    def reduce_finish(tag, after):
        names, l, send, recv, cs, lands, _ = flying[tag]
        cs, lands = _exchange_wait(send, recv, cs, lands, after, _chips_copies, f"grads_to_chips_wait_{tag}")
        ts = [_sum_chips(cc, r3, place_arr, f"sum_chips_{n}_{l}") for n, cc, r3 in zip(names, cs, lands)]
        for n, j in zip(names, _join_halves(ts, tag)):
            out[n] = _adam_layer(j.reshape(w[n].shape[1:]), w[n], mom[n], var[n], l, out.get(n), f"adam_{n}_{l}")
        return out[names[-1]][0]

    def late1(grads):
        return sibling_finish("1a", grads[0]) + sibling_start(grads, first, 1, "1b")

    def midway0(grads):
        reduce_finish("1a", grads[0])
        reduce_finish("1b", grads[0])
        return reduce_start(grads, rest, 0, "0a")

    dxa, big1, small1 = _backward_layer(1, dxb, saved1, midway=lambda grads: sibling_start(grads, rest, 1, "1a"), late=late1)
    dx, big0, small0 = _backward_layer(0, dxa, saved0, after=sibling_finish("1b", dxa), midway=midway0,
                                       late=lambda grads: reduce_start(grads, first, 0, "0b"))
    reduce_finish("0b", reduce_finish("0a", dx))
    small = {k: jnp.stack([small0[k], small1[k]]) for k in LAYER_SMALL}
    small["final_g"] = dfg[0]

    full_shapes = [small[n].shape for n in SMALL]
    red = _unpack(_allreduce_small(_pack([small[n] for n in SMALL], 32)), full_shapes)
    g_small = []
    for n, g in zip(SMALL, red):
        if n in CHIP_SHARDED_SMALL:
            g = lax.dynamic_slice_in_dim(g, chip * w[n].shape[-1], w[n].shape[-1], axis=g.ndim - 1)
        g_small.append(g)
    shapes = [w[n].shape for n in SMALL]
    packs = [_pack(lst, 32) for lst in (g_small, [w[n] for n in SMALL], [mom[n] for n in SMALL], [var[n] for n in SMALL])]
    upd = [_unpack(u, shapes) for u in _adam(*packs, "adam_small")]
    for i, n in enumerate(SMALL):
        out[n] = [g_small[i], upd[0][i], upd[1][i], upd[2][i]]

    return (loss, dx[None]) + tuple(out[n][i] for i in range(4) for n in WEIGHTS)
```

```python
import functools

import jax
import jax.numpy as jnp
from jax import lax
from jax.experimental import pallas as pl
from jax.experimental.pallas import tpu as pltpu

F32 = jnp.float32
BF16 = jnp.bfloat16
MESH = pl.DeviceIdType.MESH

D = 1024
NH = 8
HD = 128
CHUNK = 128
N_IN_T = 12
DFF = 2816
DFF_SH = 1408
EPS = 1e-6
LRU_C = 8.0
ADAM_LR, ADAM_B1, ADAM_B2, ADAM_EPS, ADAM_WD, ADAM_STEP = 0.001, 0.9, 0.999, 1e-08, 0.01, 10

TM = 512
TM_BIG = 1024
RT = 128
PADR = 8
VMEM_LIMIT = 56 * 1024 * 1024


def _cp(sem=None, **kw):
    if sem is not None:
        kw["dimension_semantics"] = sem
    return pltpu.CompilerParams(vmem_limit_bytes=VMEM_LIMIT, **kw)


_GC = 0.7978845608028654


def _sigmoid(x):
    return 1.0 / (1.0 + jnp.exp(-x))


def _gelu(x):
    return 0.5 * x * (1.0 + jnp.tanh(_GC * (x + 0.044715 * x * x * x)))


def _gelu_and_grad(x):
    t = jnp.tanh(_GC * (x + 0.044715 * x * x * x))
    g = 0.5 * x * (1.0 + t)
    dg = 0.5 * (1.0 + t) + 0.5 * x * (1.0 - t * t) * _GC * (1.0 + 3 * 0.044715 * x * x)
    return g, dg


def _softplus_neg(lam):
    y = jnp.exp(-jnp.abs(lam))
    u = 1.0 + y
    l1p = jnp.where(u == 1.0, y, jnp.log(u) * y / (u - 1.0))
    return jnp.maximum(-lam, 0.0) + l1p


def _dot(a, b):
    return jnp.dot(a, b, preferred_element_type=F32)


def _dot_nt(a, b):
    return lax.dot_general(a, b, (((1,), (1,)), ((), ())), preferred_element_type=F32)


def _dot_tn(a, b):
    return lax.dot_general(a, b, (((0,), (0,)), ((), ())), preferred_element_type=F32)


def _rms_hat(x):
    r = lax.rsqrt(jnp.mean(x * x, axis=-1, keepdims=True) + EPS)
    return x * r, r


def _rms_bwd(dh, x, g):
    xh, r = _rms_hat(x)
    dxh = dh * g
    dx = r * (dxh - xh * jnp.mean(dxh * xh, axis=-1, keepdims=True))
    return dx, jnp.sum(dh * xh, axis=0, keepdims=True)


NORM_ROWS = 64


def _norm_into(x_ref, g_ref, h_ref):
    g = g_ref[...]

    def step(i, c):
        rows = pl.ds(pl.multiple_of(i * NORM_ROWS, NORM_ROWS), NORM_ROWS)
        xh, _ = _rms_hat(x_ref[rows, :])
        h_ref[rows, :] = (xh * g).astype(BF16)
        return c
    lax.fori_loop(0, x_ref.shape[0] // NORM_ROWS, step, 0)


def _in_tile(j):
    m, hf = j // 2, j % 2
    orig = jnp.where(m < 2, m, jnp.where(m == 2, 4, jnp.where(m < 5, m - 1, 5)))
    t = orig * 2 + hf
    return t // 3, t % 3


ANY = pl.BlockSpec(memory_space=pl.ANY)


def _mm_in(x, g, w_in, l, after=()):
    S = x.shape[0]
    tm = min(2 * TM_BIG, S)

    def body(x_ref, g_ref, w_ref, *rest):
        o_ref, h_ref = rest[-2:]

        @pl.when(pl.program_id(1) == 0)
        def _():
            _norm_into(x_ref, g_ref, h_ref)
        o_ref[...] = _dot(h_ref[...], w_ref[...]).astype(BF16)

    def w_map(i, j):
        sh, tl = _in_tile(j)
        return (sh, 0, tl)

    return pl.pallas_call(
        body, name=f"mm_in_{l}", grid=(S // tm, N_IN_T),
        in_specs=[pl.BlockSpec((tm, D), lambda i, j: (i, 0)), pl.BlockSpec((1, D), lambda i, j: (0, 0)),
                  pl.BlockSpec((None, D, 512), w_map)] + [ANY] * len(after),
        out_specs=[pl.BlockSpec((None, tm, 512), lambda i, j: (j // 2, i, j % 2)), pl.BlockSpec((tm, D), lambda i, j: (i, 0))],
        out_shape=[jax.ShapeDtypeStruct((6, S, D), BF16), jax.ShapeDtypeStruct((S, D), BF16)],
        compiler_params=_cp(("parallel", "arbitrary")),
    )(x, g, w_in, *after)


def _mm_res(a, w, res, l, name):
    S, K = a.shape

    def body(a_ref, w_ref, r_ref, o_ref):
        o_ref[...] = r_ref[...] + _dot(a_ref[...], w_ref[...])

    tm = TM
    return pl.pallas_call(
        body, name=f"{name}_{l}", grid=(S // tm,),
        in_specs=[pl.BlockSpec((tm, K), lambda i: (i, 0)), pl.BlockSpec((K, D), lambda i: (0, 0)),
                  pl.BlockSpec((tm, D), lambda i: (i, 0))],
        out_specs=pl.BlockSpec((tm, D), lambda i: (i, 0)),
        out_shape=jax.ShapeDtypeStruct((S, D), F32),
        compiler_params=_cp(("parallel",)),
    )(a, w, res)


def _mm_ffn_in(x, g, w_fi, l):
    S = x.shape[0]

    def body(x_ref, g_ref, w_ref, gu_ref, ff_ref, h_ref):
        @pl.when(pl.program_id(1) == 0)
        def _():
            _norm_into(x_ref, g_ref, h_ref)
        hv = h_ref[...]
        ga = _dot(hv, w_ref[0])
        gb = _dot(hv, w_ref[1])
        gu_ref[0] = ga.astype(BF16)
        gu_ref[1] = gb.astype(BF16)
        ff_ref[...] = (ga * _sigmoid(ga) * gb).astype(BF16)

    gu, ff, h = pl.pallas_call(
        body, name=f"mm_ffn_in_{l}", grid=(S // TM, 2),
        in_specs=[pl.BlockSpec((TM, D), lambda i, s: (i, 0)), pl.BlockSpec((1, D), lambda i, s: (0, 0)),
                  pl.BlockSpec((2, None, D, DFF_SH), lambda i, s: (0, s, 0, 0))],
        out_specs=[pl.BlockSpec((2, None, TM, DFF_SH), lambda i, s: (0, s, i, 0)),
                   pl.BlockSpec((TM, DFF_SH), lambda i, s: (i, s)),
                   pl.BlockSpec((TM, D), lambda i, s: (i, 0))],
        out_shape=[jax.ShapeDtypeStruct((2, 2, S, DFF_SH), BF16), jax.ShapeDtypeStruct((S, DFF), BF16),
                   jax.ShapeDtypeStruct((S, D), BF16)],
        compiler_params=_cp(("parallel", "arbitrary")),
    )(x, g, w_fi.reshape(2, 2, D, DFF_SH))
    return gu.reshape(4, S, DFF_SH), ff, h


def _gmlp_fwd(z6, ws_b, bs_b, lg, lb):
    S = z6.shape[1]

    def body(z_ref, ws_ref, bs_ref, lg_ref, lb_ref, o_ref, mix):
        gv = _gelu(z_ref[1].astype(F32))
        xc = gv - jnp.mean(gv, axis=-1, keepdims=True)
        rs = lax.rsqrt(jnp.mean(xc * xc, axis=-1, keepdims=True) + EPS)
        vb = (xc * rs * lg_ref[...] + lb_ref[...]).astype(BF16)
        for gi in range(NH):
            cs = slice(gi * HD, (gi + 1) * HD)
            mix[:, cs] = _dot(ws_ref[gi], vb[:, cs])
        o_ref[...] = (_sigmoid(z_ref[2].astype(F32)) * _gelu(z_ref[0].astype(F32)) * (mix[...] + bs_ref[...])).astype(BF16)

    return pl.pallas_call(
        body, name="gmlp_fwd", grid=(S // CHUNK,),
        in_specs=[pl.BlockSpec((3, CHUNK, D), lambda i: (0, i, 0)), pl.BlockSpec((NH, CHUNK, CHUNK), lambda i: (0, 0, 0)),
                  pl.BlockSpec((CHUNK, D), lambda i: (0, 0)), pl.BlockSpec((1, D), lambda i: (0, 0)),
                  pl.BlockSpec((1, D), lambda i: (0, 0))],
        out_specs=pl.BlockSpec((CHUNK, D), lambda i: (i, 0)),
        out_shape=jax.ShapeDtypeStruct((S, D), BF16),
        scratch_shapes=[pltpu.VMEM((CHUNK, D), F32)],
        compiler_params=_cp(("parallel",)),
    )(z6, ws_b, bs_b, lg, lb)


def _row_iota():
    return lax.broadcasted_iota(jnp.int32, (RT, HD), 0)


SUB = 8
UNROLL = 4
GRAD_ROWS = 256


def _scan_up(a, b, carry):
    row = lax.broadcasted_iota(jnp.int32, (SUB, HD), 0)
    masks = [(d, row >= d) for d in (1, 2, 4)]
    c = jnp.broadcast_to(carry, (SUB, HD))
    hs = []
    for j in range(RT // SUB):
        aj, bj = a[SUB * j:SUB * (j + 1)], b[SUB * j:SUB * (j + 1)]
        for d, m in masks:
            bj = bj + aj * jnp.where(m, pltpu.roll(bj, d, 0), 0.0)
            aj = aj * jnp.where(m, pltpu.roll(aj, d, 0), 1.0)
        h = bj + aj * c
        hs.append(h)
        c = jnp.broadcast_to(h[SUB - 1:SUB, :], (SUB, HD))
    return jnp.concatenate(hs, axis=0), hs[-1][SUB - 1:SUB, :]


def _scan_down(a, b, carry):
    row = lax.broadcasted_iota(jnp.int32, (SUB, HD), 0)
    masks = [(d, row < SUB - d) for d in (1, 2, 4)]
    c = jnp.broadcast_to(carry, (SUB, HD))
    hs = []
    for j in reversed(range(RT // SUB)):
        aj, bj = a[SUB * j:SUB * (j + 1)], b[SUB * j:SUB * (j + 1)]
        for d, m in masks:
            bj = bj + aj * jnp.where(m, pltpu.roll(bj, SUB - d, 0), 0.0)
            aj = aj * jnp.where(m, pltpu.roll(aj, SUB - d, 0), 1.0)
        h = bj + aj * c
        hs.append(h)
        c = jnp.broadcast_to(h[0:1, :], (SUB, HD))
    return jnp.concatenate(hs[::-1], axis=0), hs[-1][0:1, :]


def _decay(r, sp_d):
    log_a = -LRU_C * r * sp_d
    a = jnp.exp(log_a)
    return a, jnp.sqrt(jnp.maximum(-jnp.tanh(log_a) * (a * a + 1.0), 0.0))


def _lru_gates(xc, d, wr_ref, br_ref, wi_ref, bi_ref, sp):
    xb = xc.astype(BF16)
    r = _sigmoid(_dot(xb, wr_ref[d]) + br_ref[d:d + 1, :])
    i = _sigmoid(_dot(xb, wi_ref[d]) + bi_ref[d:d + 1, :])
    a, mult = _decay(r, sp[d:d + 1, :])
    return r, i, a, mult


def _shifted(win, k):
    w = RT + 2 * PADR
    v = win if k == 0 else pltpu.roll(win, (-k) % w, 0)
    return v[PADR:PADR + RT]


def _conv_taps(win):
    return [_shifted(win, k) for k in (-1, 0, 1, 2)]


def _fill_padded(dst, src_ref, S):
    zeros = jnp.zeros((PADR, HD), F32)
    dst[0:PADR, :] = zeros
    dst[PADR + S:2 * PADR + S, :] = zeros

    def cp(i, c):
        t0 = pl.multiple_of(i * RT, RT)
        dst[pl.ds(t0 + PADR, RT), :] = src_ref[pl.ds(t0, RT), :].astype(F32)
        return c
    lax.fori_loop(0, S // RT, cp, 0)


def _conv_fwd_all(zxp, xc_s, cw_ref, cb_ref, S):
    def cv(i, c):
        t0 = pl.multiple_of(i * RT, RT)
        xm1, x0, xp1, xp2 = _conv_taps(zxp[pl.ds(t0, RT + 2 * PADR), :])
        xc_s[pl.ds(t0, RT), :] = (cb_ref[...] + xm1 * cw_ref[0:1, :] + x0 * cw_ref[1:2, :]
                                  + xp1 * cw_ref[2:3, :] + xp2 * cw_ref[3:4, :])
        return c
    lax.fori_loop(0, S // RT, cv, 0)


def _lru_specs(S):
    head = lambda h: (0, h)
    return [pl.BlockSpec((4, HD), head), pl.BlockSpec((1, HD), head),
            pl.BlockSpec((2, None, HD, HD), lambda h: (0, h, 0, 0)), pl.BlockSpec((2, HD), head),
            pl.BlockSpec((2, None, HD, HD), lambda h: (0, h, 0, 0)), pl.BlockSpec((2, HD), head),
            pl.BlockSpec((2, HD), head)]


def _lru_fwd(z6, ya, cw, cb, wr, br, wi, bi, lam):
    S = z6.shape[1]
    nt = S // RT

    def body(z_ref, ya_ref, cw_ref, cb_ref, wr_ref, br_ref, wi_ref, bi_ref, lam_ref, mg_ref, h0_ref, h1_ref, zxp, xc_s):
        sp = _softplus_neg(lam_ref[...])
        _fill_padded(zxp, z_ref.at[0], S)
        _conv_fwd_all(zxp, xc_s, cw_ref, cb_ref, S)

        def scans(i, carry):
            cu, cd = carry
            for u in range(UNROLL):
                j = i * UNROLL + u
                ru = pl.ds(pl.multiple_of(j * RT, RT), RT)
                rd = pl.ds(pl.multiple_of((nt - 1 - j) * RT, RT), RT)
                xu, xd = xc_s[ru, :], xc_s[rd, :]
                _, gi, a, mult = _lru_gates(xu, 0, wr_ref, br_ref, wi_ref, bi_ref, sp)
                hu, cu = _scan_up(a, mult * gi * xu, cu)
                h0_ref[ru, :] = hu
                _, gi, a, mult = _lru_gates(xd, 1, wr_ref, br_ref, wi_ref, bi_ref, sp)
                hd, cd = _scan_down(a, mult * gi * xd, cd)
                h1_ref[rd, :] = hd
            return cu, cd
        z1 = jnp.zeros((1, HD), F32)
        lax.fori_loop(0, nt // UNROLL, scans, (z1, z1))

        def merge(i, c):
            rows = pl.ds(pl.multiple_of(i * RT, RT), RT)
            yb = (h0_ref[rows, :] + h1_ref[rows, :]) * _gelu(z_ref[1, rows, :].astype(F32))
            mg_ref[rows, :] = (ya_ref[rows, :].astype(F32) + _sigmoid(z_ref[2, rows, :].astype(F32)) * yb).astype(BF16)
            return c
        lax.fori_loop(0, nt, merge, 0)

    col = pl.BlockSpec((S, HD), lambda h: (0, h))
    return pl.pallas_call(
        body, name="lru_fwd", grid=(NH,),
        in_specs=[pl.BlockSpec((3, S, HD), lambda h: (1, 0, h)), col] + _lru_specs(S),
        out_specs=[col, col, col],
        out_shape=[jax.ShapeDtypeStruct((S, D), BF16), jax.ShapeDtypeStruct((S, D), F32), jax.ShapeDtypeStruct((S, D), F32)],
        scratch_shapes=[pltpu.VMEM((S + 2 * PADR, HD), F32), pltpu.VMEM((S, HD), F32)],
        compiler_params=_cp(("parallel",)),
    )(z6, ya, cw, cb, wr, br, wi, bi, lam)


def _loss_head(x, tgt, g):
    S = x.shape[0]

    def body(x_ref, t_ref, g_ref, dx_ref, loss_ref, dg_ref):
        @pl.when(pl.program_id(0) == 0)
        def _():
            loss_ref[...] = jnp.zeros_like(loss_ref)
            dg_ref[...] = jnp.zeros_like(dg_ref)
        xv = x_ref[...]
        xh, _ = _rms_hat(xv)
        e = xh * g_ref[...] - t_ref[...]
        loss_ref[...] += jnp.sum(e * e) * (0.5 / D)
        dx, dgs = _rms_bwd(e * (1.0 / D), xv, g_ref[...])
        dx_ref[...] = dx
        dg_ref[...] += dgs

    return pl.pallas_call(
        body, name="loss_head", grid=(S // TM,),
        in_specs=[pl.BlockSpec((TM, D), lambda i: (i, 0)), pl.BlockSpec((TM, D), lambda i: (i, 0)),
                  pl.BlockSpec((1, D), lambda i: (0, 0))],
        out_specs=[pl.BlockSpec((TM, D), lambda i: (i, 0)), pl.BlockSpec((1, 128), lambda i: (0, 0)),
                   pl.BlockSpec((1, D), lambda i: (0, 0))],
        out_shape=[jax.ShapeDtypeStruct((S, D), F32), jax.ShapeDtypeStruct((1, 128), F32), jax.ShapeDtypeStruct((1, D), F32)],
        compiler_params=_cp(("arbitrary",)),
    )(x, tgt, g)


def _bwd_ffn_out(dx, w_fo, gu, l, after=()):
    S = dx.shape[0]

    def body(dx_ref, w_ref, gu_ref, *rest):
        o_ref = rest[-1]
        d = _dot_nt(dx_ref[...].astype(BF16), w_ref[...])
        ga, gb = gu_ref[0].astype(F32), gu_ref[1].astype(F32)
        sg = _sigmoid(ga)
        o_ref[0] = (d * gb * sg * (1.0 + ga * (1.0 - sg))).astype(BF16)
        o_ref[1] = (d * ga * sg).astype(BF16)

    pair = pl.BlockSpec((2, None, TM, DFF_SH), lambda i, s: (0, s, i, 0))
    dgu = pl.pallas_call(
        body, name=f"bwd_ffn_out_{l}", grid=(S // TM, 2),
        in_specs=[pl.BlockSpec((TM, D), lambda i, s: (i, 0)), pl.BlockSpec((DFF_SH, D), lambda i, s: (s, 0)), pair]
        + [ANY] * len(after),
        out_specs=pair,
        out_shape=jax.ShapeDtypeStruct((2, 2, S, DFF_SH), BF16),
        compiler_params=_cp(("parallel", "arbitrary")),
    )(dx, w_fo, gu.reshape(2, 2, S, DFF_SH), *after)
    return dgu.reshape(4, S, DFF_SH)


def _mm_tn(a, b, m_blk, tk, name):
    S, M = a.shape

    def body(a_ref, b_ref, o_ref):
        @pl.when(pl.program_id(1) == 0)
        def _():
            o_ref[...] = jnp.zeros_like(o_ref)
        o_ref[...] += _dot_tn(a_ref[...], b_ref[...].astype(BF16))

    return pl.pallas_call(
        body, name=name, grid=(M // m_blk, S // tk),
        in_specs=[pl.BlockSpec((tk, m_blk), lambda m, k: (k, m)), pl.BlockSpec((tk, D), lambda m, k: (k, 0))],
        out_specs=pl.BlockSpec((m_blk, D), lambda m, k: (m, 0)),
        out_shape=jax.ShapeDtypeStruct((M, D), F32),
        compiler_params=_cp(("parallel", "arbitrary")),
    )(a, b)


def _mm_nt_rms_bwd(a, a_spec, w, w_spec, nk, tm, x, g, dres, name, after=()):
    S = x.shape[0]

    def body(a_ref, w_ref, x_ref, g_ref, r_ref, *rest):
        dx_ref, dg_ref, acc = rest[-3:]
        i, k = pl.program_id(0), pl.program_id(1)

        @pl.when(k == 0)
        def _():
            acc[...] = jnp.zeros_like(acc)
        acc[...] += _dot_nt(a_ref[...], w_ref[...])

        @pl.when(jnp.logical_and(i == 0, k == 0))
        def _():
            dg_ref[...] = jnp.zeros_like(dg_ref)

        @pl.when(k == nk - 1)
        def _():
            g = g_ref[...]

            def step(j, dg):
                rows = pl.ds(pl.multiple_of(j * NORM_ROWS, NORM_ROWS), NORM_ROWS)
                dx, dgs = _rms_bwd(acc[rows, :], x_ref[rows, :], g)
                dx_ref[rows, :] = r_ref[rows, :] + dx
                return dg + dgs
            dg_ref[...] += lax.fori_loop(0, tm // NORM_ROWS, step, jnp.zeros((1, D), F32))

    row = pl.BlockSpec((tm, D), lambda i, k: (i, 0))
    vec = pl.BlockSpec((1, D), lambda i, k: (0, 0))
    return pl.pallas_call(
        body, name=name, grid=(S // tm, nk),
        in_specs=[a_spec, w_spec, row, vec, row] + [ANY] * len(after),
        out_specs=[row, vec],
        out_shape=[jax.ShapeDtypeStruct((S, D), F32), jax.ShapeDtypeStruct((1, D), F32)],
        scratch_shapes=[pltpu.VMEM((tm, D), F32)],
        compiler_params=_cp(("arbitrary", "arbitrary")),
    )(a, w, x, g, dres, *after)


def _dw_ffn_in(h, dgu, l):
    S = h.shape[0]

    def body(h_ref, b_ref, o_ref):
        @pl.when(pl.program_id(1) == 0)
        def _():
            o_ref[...] = jnp.zeros_like(o_ref)
        o_ref[...] += _dot_tn(h_ref[...], b_ref[...])

    tk = min(2 * TM_BIG, S)
    return pl.pallas_call(
        body, name=f"dw_ffn_in_{l}", grid=(4, S // tk),
        in_specs=[pl.BlockSpec((tk, D), lambda j, k: (k, 0)), pl.BlockSpec((None, tk, DFF_SH), lambda j, k: (j, k, 0))],
        out_specs=pl.BlockSpec((None, D, DFF_SH), lambda j, k: (j, 0, 0)),
        out_shape=jax.ShapeDtypeStruct((4, D, DFF_SH), F32),
        compiler_params=_cp(("parallel", "arbitrary")),
    )(h, dgu)


_HALF_COMPS = ((0, 1, 3), (4, 2, 5))


def _dw_in(h, dz6, l):
    S = h.shape[0]

    def body(h_ref, d0_ref, d1_ref, d2_ref, o_ref):
        @pl.when(pl.program_id(1) == 0)
        def _():
            o_ref[...] = jnp.zeros_like(o_ref)
        hv = h_ref[...]
        for q, d_ref in enumerate((d0_ref, d1_ref, d2_ref)):
            for hf in range(2):
                col = 1024 * q + 512 * hf
                o_ref[col // 1536, :, col % 1536:col % 1536 + 512] += _dot_tn(hv, d_ref[:, 512 * hf:512 * (hf + 1)])

    tk = min(TM_BIG, S)

    def comp(q):
        return pl.BlockSpec((None, tk, D), lambda p, k: (jnp.where(p == 0, _HALF_COMPS[0][q], _HALF_COMPS[1][q]), k, 0))

    return pl.pallas_call(
        body, name=f"dw_in_{l}", grid=(2, S // tk),
        in_specs=[pl.BlockSpec((tk, D), lambda p, k: (k, 0)), comp(0), comp(1), comp(2)],
        out_specs=pl.BlockSpec((2, D, 1536), lambda p, k: (p, 0, 0)),
        out_shape=jax.ShapeDtypeStruct((4, D, 1536), F32),
        compiler_params=_cp(("parallel", "arbitrary")),
    )(h, dz6, dz6, dz6)


def _bwd_out(dx, w_o, merged, l):
    S = dx.shape[0]

    def body(dx_ref, w_ref, m_ref, dm_ref, dw_ref):
        @pl.when(pl.program_id(0) == 0)
        def _():
            dw_ref[...] = jnp.zeros_like(dw_ref)
        dxb = dx_ref[...].astype(BF16)
        dm_ref[...] = _dot_nt(dxb, w_ref[...]).astype(BF16)
        dw_ref[...] += _dot_tn(m_ref[...], dxb)

    tm = TM
    row = pl.BlockSpec((tm, D), lambda i: (i, 0))
    return pl.pallas_call(
        body, name=f"bwd_out_{l}", grid=(S // tm,),
        in_specs=[row, pl.BlockSpec((D, D), lambda i: (0, 0)), row],
        out_specs=[row, pl.BlockSpec((D, D), lambda i: (0, 0))],
        out_shape=[jax.ShapeDtypeStruct((S, D), BF16), jax.ShapeDtypeStruct((D, D), F32)],
        compiler_params=_cp(("arbitrary",)),
    )(dx, w_o, merged)


def _gmlp_bwd(dm, z6, ws_b, wst_b, bs_b, lg, lb, after=()):
    S = z6.shape[1]

    def body(dm_ref, z_ref, ws_ref, wst_ref, bs_ref, lg_ref, lb_ref, *rest):
        dz_ref, dws_ref, dbs_ref, dlg_ref, dlb_ref, mix, dv = rest[-7:]

        @pl.when(pl.program_id(0) == 0)
        def _():
            dws_ref[...] = jnp.zeros_like(dws_ref)
            dbs_ref[...] = jnp.zeros_like(dbs_ref)
            dlg_ref[...] = jnp.zeros_like(dlg_ref)
            dlb_ref[...] = jnp.zeros_like(dlb_ref)
        gv, dgelu_v = _gelu_and_grad(z_ref[1].astype(F32))
        xc = gv - jnp.mean(gv, axis=-1, keepdims=True)
        rs = lax.rsqrt(jnp.mean(xc * xc, axis=-1, keepdims=True) + EPS)
        vh = xc * rs
        vb = (vh * lg_ref[...] + lb_ref[...]).astype(BF16)
        for gi in range(NH):
            cs = slice(gi * HD, (gi + 1) * HD)
            mix[:, cs] = _dot(ws_ref[gi], vb[:, cs])
        u, dgelu_u = _gelu_and_grad(z_ref[0].astype(F32))
        sa = _sigmoid(z_ref[2].astype(F32))
        mixed = mix[...] + bs_ref[...]
        dyg = dm_ref[...].astype(F32)
        dz_ref[2] = (dyg * u * mixed * sa * (1.0 - sa)).astype(BF16)
        dya = dyg * sa
        dz_ref[0] = (dya * mixed * dgelu_u).astype(BF16)
        dmix = dya * u
        dmb = dmix.astype(BF16)
        for gi in range(NH):
            cs = slice(gi * HD, (gi + 1) * HD)
            dv[:, cs] = _dot(wst_ref[gi], dmb[:, cs])
            dws_ref[gi] += _dot_nt(dmb[:, cs], vb[:, cs])
            dbs_ref[gi] += jnp.broadcast_to(jnp.sum(dmix[:, cs], axis=1, keepdims=True), (CHUNK, HD))
        dvv = dv[...]
        dlg_ref[...] += jnp.sum(dvv * vh, axis=0, keepdims=True)
        dlb_ref[...] += jnp.sum(dvv, axis=0, keepdims=True)
        dvh = dvv * lg_ref[...]
        dgv = rs * (dvh - jnp.mean(dvh, axis=-1, keepdims=True) - vh * jnp.mean(dvh * vh, axis=-1, keepdims=True))
        dz_ref[1] = (dgv * dgelu_v).astype(BF16)

    vec = pl.BlockSpec((1, D), lambda i: (0, 0))
    mat = pl.BlockSpec((NH, CHUNK, CHUNK), lambda i: (0, 0, 0))
    return pl.pallas_call(
        body, name="gmlp_bwd", grid=(S // CHUNK,),
        in_specs=[pl.BlockSpec((CHUNK, D), lambda i: (i, 0)), pl.BlockSpec((3, CHUNK, D), lambda i: (0, i, 0)), mat, mat,
                  pl.BlockSpec((CHUNK, D), lambda i: (0, 0)), vec, vec] + [ANY] * len(after),
        out_specs=[pl.BlockSpec((3, CHUNK, D), lambda i: (0, i, 0)), mat, mat, vec, vec],
        out_shape=[jax.ShapeDtypeStruct((6, S, D), BF16), jax.ShapeDtypeStruct((NH, CHUNK, CHUNK), F32),
                   jax.ShapeDtypeStruct((NH, CHUNK, HD), F32), jax.ShapeDtypeStruct((1, D), F32), jax.ShapeDtypeStruct((1, D), F32)],
        scratch_shapes=[pltpu.VMEM((CHUNK, D), F32), pltpu.VMEM((CHUNK, D), F32)],
        compiler_params=_cp(("arbitrary",)),
    )(dm, z6, ws_b, wst_b, bs_b, lg, lb, *after)


def _lru_bwd(dz6, dm, z6, h0, h1, cw, cb, wr, br, wi, bi, lam):
    S = z6.shape[1]
    nt = S // RT

    def body(dz_in, dm_ref, z_ref, h0_ref, h1_ref, cw_ref, cb_ref, wr_ref, br_ref, wi_ref, bi_ref, lam_ref,
             dz_ref, dcw_ref, dcb_ref, dwr_ref, dbr_ref, dwi_ref, dbi_ref, dlam_ref, zxp, xc_s, dhs_s, dxcp, r_s, lam_s):
        del dz_in
        lam = lam_ref[...]
        sp = _softplus_neg(lam)
        row = _row_iota()
        _fill_padded(zxp, z_ref.at[0], S)
        _conv_fwd_all(zxp, xc_s, cw_ref, cb_ref, S)
        zeros = jnp.zeros((PADR, HD), F32)
        dxcp[0:PADR, :] = zeros
        dxcp[PADR + S:2 * PADR + S, :] = zeros
        dwr_ref[...] = jnp.zeros_like(dwr_ref)
        dwi_ref[...] = jnp.zeros_like(dwi_ref)

        def pre(i, c):
            rows = pl.ds(pl.multiple_of(i * RT, RT), RT)
            hs = h0_ref[rows, :] + h1_ref[rows, :]
            dmv = dm_ref[rows, :].astype(F32)
            sb = _sigmoid(z_ref[2, rows, :].astype(F32))
            gg, dgg = _gelu_and_grad(z_ref[1, rows, :].astype(F32))
            dz_ref[2, rows, :] = (dmv * hs * gg * sb * (1.0 - sb)).astype(BF16)
            dyb = dmv * sb
            dz_ref[1, rows, :] = (dyb * hs * dgg).astype(BF16)
            dhs_s[rows, :] = dyb * gg
            return c
        lax.fori_loop(0, nt, pre, 0)

        def gate_bwd(d, gates, lamv, da, xc):
            r, gi, a, mult = gates
            dmult = lamv * gi * xc
            dgi = lamv * mult * xc
            dlog = (da - dmult * a / mult) * a
            dpr = (dlog * (-LRU_C) * sp[d:d + 1, :]) * r * (1.0 - r)
            dpi = dgi * gi * (1.0 - gi)
            xb, dprb, dpib = xc.astype(BF16), dpr.astype(BF16), dpi.astype(BF16)
            dwr_ref[d] += _dot_tn(xb, dprb)
            dwi_ref[d] += _dot_tn(xb, dpib)
            dxc = lamv * mult * gi + _dot_nt(dprb, wr_ref[d]) + _dot_nt(dpib, wi_ref[d])
            return dxc, (jnp.sum(dlog * r, axis=0, keepdims=True) * (-LRU_C), jnp.sum(dpr, axis=0, keepdims=True),
                         jnp.sum(dpi, axis=0, keepdims=True))

        def rgates(i, c):
            for u in range(UNROLL):
                rows = pl.ds(pl.multiple_of((i * UNROLL + u) * RT, RT), RT)
                xb = xc_s[rows, :].astype(BF16)
                for d in range(2):
                    r_s[d, rows, :] = _sigmoid(_dot(xb, wr_ref[d]) + br_ref[d:d + 1, :])
            return c
        lax.fori_loop(0, nt // UNROLL, rgates, 0)

        def chains(i, carry):
            qn, qp = carry
            for u in range(UNROLL):
                j = i * UNROLL + u
                rd = pl.ds(pl.multiple_of((nt - 1 - j) * RT, RT), RT)
                a, dhs = _decay(r_s[0, rd, :], sp[0:1, :])[0], dhs_s[rd, :]
                q, q_first = _scan_down(a, a * dhs, qn)
                lam_s[0, rd, :] = dhs + jnp.where(row == RT - 1, qn, pltpu.roll(q, RT - 1, 0))
                qn = q_first
                ru = pl.ds(pl.multiple_of(j * RT, RT), RT)
                a, dhs = _decay(r_s[1, ru, :], sp[1:2, :])[0], dhs_s[ru, :]
                q, q_last = _scan_up(a, a * dhs, qp)
                lam_s[1, ru, :] = dhs + jnp.where(row == 0, qp, pltpu.roll(q, 1, 0))
                qp = q_last
            return qn, qp

        z1 = jnp.zeros((1, HD), F32)
        lax.fori_loop(0, nt // UNROLL, chains, (z1, z1))

        ct = min(GRAD_ROWS, S)
        crow = lax.broadcasted_iota(jnp.int32, (ct, HD), 0)

        def tile_grads(i, acc):
            t0 = pl.multiple_of(i * ct, ct)
            rows = pl.ds(t0, ct)
            xc = xc_s[rows, :]
            xb = xc.astype(BF16)
            tp = pl.multiple_of(jnp.maximum(t0 - PADR, 0), PADR)
            prev = jnp.where(t0 > 0, h0_ref[pl.ds(tp, PADR), :][PADR - 1:PADR, :], 0.0)
            tn = pl.multiple_of(jnp.minimum(t0 + ct, S - PADR), PADR)
            nxt = jnp.where(t0 + ct < S, h1_ref[pl.ds(tn, PADR), :][0:1, :], 0.0)
            hside = (jnp.where(crow == 0, prev, pltpu.roll(h0_ref[rows, :], 1, 0)),
                     jnp.where(crow == ct - 1, nxt, pltpu.roll(h1_ref[rows, :], ct - 1, 0)))
            dxc, sums = 0.0, ()
            for d in range(2):
                r = r_s[d, rows, :]
                gi = _sigmoid(_dot(xb, wi_ref[d]) + bi_ref[d:d + 1, :])
                a, mult = _decay(r, sp[d:d + 1, :])
                lamv = lam_s[d, rows, :]
                dxc_d, s_d = gate_bwd(d, (r, gi, a, mult), lamv, lamv * hside[d], xc)
                dxc = dxc + dxc_d
                sums = sums + s_d
            dxcp[pl.ds(t0 + PADR, ct), :] = dxc
            return tuple(x + y for x, y in zip(acc, sums))

        s_sp0, s_br0, s_bi0, s_sp1, s_br1, s_bi1 = lax.fori_loop(0, S // ct, tile_grads, (z1,) * 6)

        dsp = jnp.concatenate([s_sp0, s_sp1], axis=0)
        dlam_ref[...] = -dsp * _sigmoid(-lam)
        dbr_ref[...] = jnp.concatenate([s_br0, s_br1], axis=0)
        dbi_ref[...] = jnp.concatenate([s_bi0, s_bi1], axis=0)

        def conv_bwd(i, carry):
            c0, c1, c2, c3, cb_ = carry
            t0 = pl.multiple_of(i * RT, RT)
            dwin = dxcp[pl.ds(t0, RT + 2 * PADR), :]
            d0 = _shifted(dwin, 0)
            dz_ref[0, pl.ds(t0, RT), :] = (_shifted(dwin, 1) * cw_ref[0:1, :] + d0 * cw_ref[1:2, :]
                                           + _shifted(dwin, -1) * cw_ref[2:3, :] + _shifted(dwin, -2) * cw_ref[3:4, :]).astype(BF16)
            xm1, x0, xp1, xp2 = _conv_taps(zxp[pl.ds(t0, RT + 2 * PADR), :])
            sm = lambda v: jnp.sum(v, axis=0, keepdims=True)
            return c0 + sm(d0 * xm1), c1 + sm(d0 * x0), c2 + sm(d0 * xp1), c3 + sm(d0 * xp2), cb_ + sm(d0)

        c0, c1, c2, c3, cb_ = lax.fori_loop(0, nt, conv_bwd, (z1, z1, z1, z1, z1))
        dcw_ref[...] = jnp.concatenate([c0, c1, c2, c3], axis=0)
        dcb_ref[...] = cb_

    col = pl.BlockSpec((S, HD), lambda h: (0, h))
    head = lambda h: (0, h)
    wspec = pl.BlockSpec((2, None, HD, HD), lambda h: (0, h, 0, 0))
    return pl.pallas_call(
        body, name="lru_bwd", grid=(NH,),
        in_specs=[pl.BlockSpec(memory_space=pl.ANY), col, pl.BlockSpec((3, S, HD), lambda h: (1, 0, h)), col, col] + _lru_specs(S),
        out_specs=[pl.BlockSpec((3, S, HD), lambda h: (1, 0, h)), pl.BlockSpec((4, HD), head), pl.BlockSpec((1, HD), head),
                   wspec, pl.BlockSpec((2, HD), head), wspec, pl.BlockSpec((2, HD), head), pl.BlockSpec((2, HD), head)],
        out_shape=[jax.ShapeDtypeStruct((6, S, D), BF16), jax.ShapeDtypeStruct((4, D), F32), jax.ShapeDtypeStruct((1, D), F32),
                   jax.ShapeDtypeStruct((2, NH, HD, HD), F32), jax.ShapeDtypeStruct((2, D), F32),
                   jax.ShapeDtypeStruct((2, NH, HD, HD), F32), jax.ShapeDtypeStruct((2, D), F32), jax.ShapeDtypeStruct((2, D), F32)],
        scratch_shapes=[pltpu.VMEM((S + 2 * PADR, HD), F32), pltpu.VMEM((S, HD), F32), pltpu.VMEM((S, HD), F32),
                        pltpu.VMEM((S + 2 * PADR, HD), F32), pltpu.VMEM((2, S, HD), F32), pltpu.VMEM((2, S, HD), F32)],
        input_output_aliases={0: 0},
        compiler_params=_cp(("parallel",)),
    )(dz6, dm, z6, h0, h1, cw, cb, wr, br, wi, bi, lam)


LAYER_SMALL = ("norm1_g", "gmlp_ln_g", "gmlp_ln_b", "gmlp_w_s", "gmlp_b_s", "conv_w", "conv_b",
               "lru_w_r", "lru_b_r", "lru_w_i", "lru_b_i", "lru_lambda", "norm2_g")


def _forward_layer(l, x, p, wb, after=(), rest=None):
    g1, g2 = p["norm1_g"][l][None], p["norm2_g"][l][None]
    ws_b = p["gmlp_w_s"][l].astype(BF16)
    tm = dict(ws_b=ws_b, wst_b=jnp.swapaxes(ws_b, 1, 2), bs_b=jnp.repeat(p["gmlp_b_s"][l].T, HD, axis=1),
              lg=p["gmlp_ln_g"][l][None], lb=p["gmlp_ln_b"][l][None])
    lru = (p["conv_w"][l], p["conv_b"][l][None], p["lru_w_r"][l].astype(BF16), p["lru_b_r"][l],
           p["lru_w_i"][l].astype(BF16), p["lru_b_i"][l], p["lru_lambda"][l])
    z6, hn1 = _mm_in(x, g1, wb["w_in"], l, after)
    ya = _gmlp_fwd(z6, tm["ws_b"], tm["bs_b"], tm["lg"], tm["lb"])
    merged, h0, h1 = _lru_fwd(z6, ya, *lru)
    if rest is not None:
        wb = dict(wb, **rest(merged))
    x1 = _mm_res(merged, wb["w_out"], x, l, "mm_out")
    gu, ff, hn2 = _mm_ffn_in(x1, g2, wb["w_ffn_in"], l)
    x2 = _mm_res(ff, wb["w_ffn_out"], x1, l, "mm_ffn_out")
    return x2, dict(x=x, z6=z6, h0=h0, h1=h1, merged=merged, x1=x1, gu=gu, ff=ff, g1=g1, g2=g2, tm=tm, lru=lru,
                    hn1=hn1, hn2=hn2, wb=wb)


def _backward_layer(l, dx, s, after=(), midway=None, late=None):
    S = dx.shape[0]
    tm, wb = s["tm"], s["wb"]
    g2 = s["g2"]
    dgu = _bwd_ffn_out(dx, wb["w_ffn_out"], s["gu"], l, after)
    tmb = min(TM_BIG, S)
    dwfo = _mm_tn(s["ff"], dx, DFF_SH, tmb, f"dw_ffn_out_{l}")
    dx1, dg2 = _mm_nt_rms_bwd(
        dgu, pl.BlockSpec((None, tmb, DFF_SH), lambda i, k: (k, i, 0)),
        wb["w_ffn_in"], pl.BlockSpec((None, D, DFF_SH), lambda i, k: (k, 0, 0)),
        4, tmb, s["x1"], g2, dx, f"bwd_ffn_in_{l}")
    dwfi = _dw_ffn_in(s["hn2"], dgu, l)
    dmg, dwo = _bwd_out(dx1, wb["w_out"], s["merged"], l)
    mid = () if midway is None else tuple(midway([dwo, dwfi, dwfo]))
    dz6, dws, dbs, dlg, dlb = _gmlp_bwd(dmg, s["z6"], tm["ws_b"], tm["wst_b"], tm["bs_b"], tm["lg"], tm["lb"], mid)
    dz6, dcw, dcb, dwr, dbr, dwi, dbi, dlam = _lru_bwd(dz6, dmg, s["z6"], s["h0"], s["h1"], *s["lru"])

    def w_map(i, k):
        sh, tl = _in_tile(k)
        return (sh, 0, tl)

    dwin = _dw_in(s["hn1"], dz6, l)
    tail = () if late is None else tuple(late([dwin]))
    dx0, dg1 = _mm_nt_rms_bwd(
        dz6, pl.BlockSpec((None, tmb, 512), lambda i, k: (k // 2, i, k % 2)),
        wb["w_in"], pl.BlockSpec((None, D, 512), w_map),
        N_IN_T, tmb, s["x"], s["g1"], dx1, f"bwd_in_{l}", tail)
    small = dict(norm1_g=dg1[0], gmlp_ln_g=dlg[0], gmlp_ln_b=dlb[0], gmlp_w_s=dws, gmlp_b_s=dbs[:, :, 0], conv_w=dcw, conv_b=dcb[0],
                 lru_w_r=dwr, lru_b_r=dbr, lru_w_i=dwi, lru_b_i=dbi, lru_lambda=dlam, norm2_g=dg2[0])
    return dx0, [dwin, dwo, dwfi, dwfo], small


def _local_step(x, tgt, p, wbs):
    saved = []
    for l in range(2):
        x, s = _forward_layer(l, x, p, wbs[l])
        saved.append(s)
    dx, loss_v, dfg = _loss_head(x, tgt, p["final_g"][None])
    big, smalls = [None, None], [None, None]
    for l in (1, 0):
        dx, big[l], smalls[l] = _backward_layer(l, dx, saved[l])
    small = {k: jnp.stack([smalls[0][k], smalls[1][k]]) for k in LAYER_SMALL}
    small["final_g"] = dfg[0]
    return loss_v, dx, big, small


def _place():
    x, y, c = lax.axis_index("x"), lax.axis_index("y"), lax.axis_index("c")
    return x, y, c, 2 * x + y


def _chip_at(x, y, d):
    px = 1 - x if d & 2 else x
    py = 1 - y if d & 1 else y
    return px, py, 2 * px + py


HBM = pl.BlockSpec(memory_space=pltpu.HBM)
SEM = pl.BlockSpec(memory_space=pltpu.SEMAPHORE)
DATAFLOW = pltpu.SideEffectType.DATAFLOW_SIDE_EFFECTING


def _in_hbm(a):
    return pltpu.with_memory_space_constraint(a, pltpu.HBM)


def _cast_into(wf, l, chip_arr, name):
    _, rows, cols = wf.shape
    rh = rows // 2

    def body(ch_ref, w_ref, o_ref):
        o_ref[...] = w_ref[...].astype(BF16)

    return pl.pallas_call(
        body, name=name, out_shape=jax.ShapeDtypeStruct((4, 2, rh, cols), BF16),
        grid_spec=pltpu.PrefetchScalarGridSpec(
            num_scalar_prefetch=1, grid=(2,),
            in_specs=[pl.BlockSpec((None, None, rh, cols), lambda h, ch: (l, h, 0, 0))],
            out_specs=pl.BlockSpec((None, None, rh, cols), lambda h, ch: (ch[0], h, 0, 0))),
        compiler_params=_cp(("parallel",)),
    )(chip_arr, wf.reshape(2, 2, rh, cols))


def _half_block(ref, chip, half, to, send_sem, recv_sem):
    blk = ref.at[chip, half]
    return pltpu.make_async_remote_copy(src_ref=blk, dst_ref=blk, send_sem=send_sem, recv_sem=recv_sem,
                                        device_id=to, device_id_type=MESH)


def _gather_weights(bufs, tiny):
    nt = len(bufs)
    n_ici = max(nt * 3, 1)

    def body(*refs):
        tiny_ref = refs[nt]
        o_refs, tiny_o = refs[nt + 1:2 * nt + 1], refs[2 * nt + 1]
        send, recv, fsend, frecv, tsend, trecv, lsem = refs[2 * nt + 2:]
        x, y, c, chip = _place()
        local = pltpu.make_async_copy(tiny_ref, tiny_o.at[chip], lsem)
        local.start()

        def tin(d, origin_chip, to):
            return pltpu.make_async_remote_copy(
                src_ref=tiny_ref, dst_ref=tiny_o.at[origin_chip], send_sem=tsend.at[d - 1], recv_sem=trecv.at[d - 1],
                device_id=to, device_id_type=MESH)

        sends = []
        for t in range(nt):
            for d in (1, 2, 3):
                px, py, _ = _chip_at(x, y, d)
                sends.append(_half_block(o_refs[t], chip, c, (px, py, c), send.at[3 * t + d - 1], recv.at[3 * t + d - 1]))
        for d in (1, 2, 3):
            px, py, _ = _chip_at(x, y, d)
            sends.append(tin(d, chip, (px, py, c)))
        for cp in sends:
            cp.start()
        passed = []
        for t in range(nt):
            for d in (1, 2, 3):
                k = 3 * t + d - 1
                _, _, pchip = _chip_at(x, y, d)
                _half_block(o_refs[t], pchip, c, (x, y, c), send.at[k], recv.at[k]).wait_recv()
                f = _half_block(o_refs[t], pchip, c, (x, y, 1 - c), fsend.at[k], frecv.at[k])
                f.start()
                passed.append(f)
        for t in range(nt):
            for d in (1, 2, 3):
                k = 3 * t + d - 1
                _, _, pchip = _chip_at(x, y, d)
                _half_block(o_refs[t], pchip, 1 - c, (x, y, 1 - c), fsend.at[k], frecv.at[k]).wait_recv()
        for d in (1, 2, 3):
            _, _, pchip = _chip_at(x, y, d)
            tin(d, pchip, (x, y, c)).wait_recv()
        for cp in sends + passed:
            cp.wait_send()
        local.wait()

    out_shape = [jax.ShapeDtypeStruct(b.shape, b.dtype) for b in bufs]
    out_shape.append(jax.ShapeDtypeStruct((4,) + tiny.shape, tiny.dtype))
    outs = pl.pallas_call(
        body, name="gather_weights_0", out_shape=out_shape,
        in_specs=[ANY] * (nt + 1), out_specs=[ANY] * (nt + 1),
        scratch_shapes=[pltpu.SemaphoreType.DMA((n_ici,)), pltpu.SemaphoreType.DMA((n_ici,)),
                        pltpu.SemaphoreType.DMA((n_ici,)), pltpu.SemaphoreType.DMA((n_ici,)),
                        pltpu.SemaphoreType.DMA((3,)), pltpu.SemaphoreType.DMA((3,)), pltpu.SemaphoreType.DMA],
        input_output_aliases={t: t for t in range(nt)},
        compiler_params=_cp(has_side_effects=True),
    )(*bufs, tiny)
    return outs[:nt], outs[nt]


def _gather_start(bufs, tag, after=()):
    nt, na = len(bufs), len(after)

    def body(*refs):
        b_refs = refs[:nt]
        send, recv = refs[nt + na], refs[nt + na + 1]
        token = refs[2 * nt + na + 2]
        x, y, c, chip = _place()
        for t in range(nt):
            for d in (1, 2, 3):
                px, py, _ = _chip_at(x, y, d)
                _half_block(b_refs[t], chip, c, (px, py, c), send.at[3 * t + d - 1], recv.at[3 * t + d - 1]).start()
        token[...] = jnp.zeros_like(token)

    outs = pl.pallas_call(
        body, name=f"gather_start_{tag}",
        out_shape=(pltpu.SemaphoreType.DMA((3 * nt,)), pltpu.SemaphoreType.DMA((3 * nt,)),
                   *[pltpu.HBM(b.shape, b.dtype) for b in bufs], jax.ShapeDtypeStruct((8, 128), F32)),
        in_specs=[HBM] * nt + [ANY] * na, out_specs=(SEM, SEM, *[HBM] * nt, pl.BlockSpec(memory_space=pltpu.VMEM)),
        input_output_aliases={t: 2 + t for t in range(nt)},
        compiler_params=pltpu.CompilerParams(has_side_effects=DATAFLOW),
    )(*[_in_hbm(b) for b in bufs], *after)
    return outs[0], outs[1], list(outs[2:2 + nt]), outs[2 + nt]


def _gather_wait(send, recv, bufs, after, tag):
    nt = len(bufs)

    def body(*refs):
        b_refs = refs[:nt]
        send_ref, recv_ref = refs[nt], refs[nt + 1]
        x, y, c, chip = _place()
        for t in range(nt):
            for d in (1, 2, 3):
                k = 3 * t + d - 1
                px, py, pchip = _chip_at(x, y, d)
                _half_block(b_refs[t], chip, c, (px, py, c), send_ref.at[k], recv_ref.at[k]).wait_send()
                _half_block(b_refs[t], pchip, c, (px, py, c), send_ref.at[k], recv_ref.at[k]).wait_recv()

    outs = pl.pallas_call(
        body, name=f"gather_wait_{tag}", out_shape=[pltpu.HBM(b.shape, b.dtype) for b in bufs],
        in_specs=[HBM] * nt + [SEM, SEM, ANY], out_specs=[HBM] * nt,
        input_output_aliases={t: t for t in range(nt)},
        compiler_params=pltpu.CompilerParams(has_side_effects=DATAFLOW),
    )(*bufs, send, recv, after)
    return list(outs)


def _gather_pass_on(bufs, tag):
    nt = len(bufs)

    def body(*refs):
        o_refs = refs[nt:2 * nt]
        fsend, frecv = refs[2 * nt:]
        x, y, c, _ = _place()
        cps = []
        for t in range(nt):
            for d in (1, 2, 3):
                k = 3 * t + d - 1
                _, _, pchip = _chip_at(x, y, d)
                cps.append(_half_block(o_refs[t], pchip, c, (x, y, 1 - c), fsend.at[k], frecv.at[k]))
        for cp in cps:
            cp.start()
        for t in range(nt):
            for d in (1, 2, 3):
                k = 3 * t + d - 1
                _, _, pchip = _chip_at(x, y, d)
                _half_block(o_refs[t], pchip, 1 - c, (x, y, 1 - c), fsend.at[k], frecv.at[k]).wait_recv()
        for cp in cps:
            cp.wait_send()

    return pl.pallas_call(
        body, name=f"gather_pass_on_{tag}", out_shape=[jax.ShapeDtypeStruct(b.shape, b.dtype) for b in bufs],
        in_specs=[ANY] * nt, out_specs=[ANY] * nt,
        scratch_shapes=[pltpu.SemaphoreType.DMA((3 * nt,)), pltpu.SemaphoreType.DMA((3 * nt,))],
        input_output_aliases={t: t for t in range(nt)},
        compiler_params=_cp(has_side_effects=True),
    )(*bufs)


def _to_sibling_halves(gs, l):
    nt = len(gs)

    def body(*refs):
        g_refs, o_refs = refs[:nt], refs[nt:2 * nt]
        send, recv = refs[2 * nt:]
        x, y, c, _ = _place()
        cps = [pltpu.make_async_remote_copy(
            src_ref=g_refs[t].at[k, 1 - c], dst_ref=o_refs[t].at[k], send_sem=send.at[4 * t + k], recv_sem=recv.at[4 * t + k],
            device_id=(x, y, 1 - c), device_id_type=MESH) for t in range(nt) for k in range(4)]
        for cp in cps:
            cp.start()
        for cp in cps:
            cp.wait()

    return pl.pallas_call(
        body, name=f"grads_to_sibling_{l}", out_shape=[jax.ShapeDtypeStruct((4,) + g.shape[2:], g.dtype) for g in gs],
        in_specs=[ANY] * nt, out_specs=[ANY] * nt,
        scratch_shapes=[pltpu.SemaphoreType.DMA((4 * nt,)), pltpu.SemaphoreType.DMA((4 * nt,))],
        compiler_params=_cp(has_side_effects=True),
    )(*gs)


def _chip_copy(c_ref, land_ref, x, y, c, d, send_sem, recv_sem):
    px, py, pchip = _chip_at(x, y, d)
    return pltpu.make_async_remote_copy(src_ref=c_ref.at[pchip], dst_ref=land_ref.at[d - 1], send_sem=send_sem, recv_sem=recv_sem,
                                        device_id=(px, py, c), device_id_type=MESH)


def _exchange_start(srcs, lands, copies, nsem, name):
    ns, n = len(srcs), len(srcs) + len(lands)

    def body(*refs):
        for cp in copies(refs[:ns], refs[ns:n], refs[n], refs[n + 1]):
            cp.start()
        token = refs[2 * n + 2]
        token[...] = jnp.zeros_like(token)

    outs = pl.pallas_call(
        body, name=name,
        out_shape=(pltpu.SemaphoreType.DMA((nsem,)), pltpu.SemaphoreType.DMA((nsem,)),
                   *[pltpu.HBM(a.shape, a.dtype) for a in list(srcs) + list(lands)], jax.ShapeDtypeStruct((8, 128), F32)),
        in_specs=[HBM] * n, out_specs=(SEM, SEM, *[HBM] * n, pl.BlockSpec(memory_space=pltpu.VMEM)),
        input_output_aliases={i: 2 + i for i in range(n)},
        compiler_params=pltpu.CompilerParams(has_side_effects=DATAFLOW),
    )(*[_in_hbm(a) for a in list(srcs) + list(lands)])
    return outs[0], outs[1], list(outs[2:2 + ns]), list(outs[2 + ns:2 + n]), outs[2 + n]


def _exchange_wait(send, recv, srcs, lands, after, copies, name):
    ns, n = len(srcs), len(srcs) + len(lands)

    def body(*refs):
        for cp in copies(refs[:ns], refs[ns:n], refs[n], refs[n + 1]):
            cp.wait_send()
            cp.wait_recv()

    outs = pl.pallas_call(
        body, name=name, out_shape=[pltpu.HBM(a.shape, a.dtype) for a in list(srcs) + list(lands)],
        in_specs=[HBM] * n + [SEM, SEM, ANY], out_specs=[HBM] * n,
        input_output_aliases={i: i for i in range(n)},
        compiler_params=pltpu.CompilerParams(has_side_effects=DATAFLOW),
    )(*srcs, *lands, send, recv, after)
    return list(outs[:ns]), list(outs[ns:])


def _chips_copies(c_refs, land_refs, send, recv):
    x, y, c, _ = _place()
    return [_chip_copy(c_refs[t], land_refs[t], x, y, c, d, send.at[3 * t + d - 1], recv.at[3 * t + d - 1])
            for t in range(len(c_refs)) for d in (1, 2, 3)]


def _sibling_copies(g_refs, land_refs, send, recv):
    x, y, c, _ = _place()
    return [pltpu.make_async_remote_copy(
        src_ref=g_refs[t].at[k, 1 - c], dst_ref=land_refs[t].at[k], send_sem=send.at[4 * t + k], recv_sem=recv.at[4 * t + k],
        device_id=(x, y, 1 - c), device_id_type=MESH) for t in range(len(g_refs)) for k in range(4)]


def _join_halves(fs, l):
    nt = len(fs)

    def body(*refs):
        o_refs = refs[nt:2 * nt]
        send, recv = refs[2 * nt:]
        x, y, c, _ = _place()
        cps = [pltpu.make_async_remote_copy(
            src_ref=o_refs[t].at[c], dst_ref=o_refs[t].at[c], send_sem=send.at[t], recv_sem=recv.at[t],
            device_id=(x, y, 1 - c), device_id_type=MESH) for t in range(nt)]
        for cp in cps:
            cp.start()
        for cp in cps:
            cp.wait()

    return pl.pallas_call(
        body, name=f"grads_join_{l}", out_shape=[jax.ShapeDtypeStruct(a.shape, a.dtype) for a in fs],
        in_specs=[ANY] * nt, out_specs=[ANY] * nt,
        scratch_shapes=[pltpu.SemaphoreType.DMA((nt,)), pltpu.SemaphoreType.DMA((nt,))],
        input_output_aliases={t: t for t in range(nt)},
        compiler_params=_cp(has_side_effects=True),
    )(*fs)


def _add_half(g, r, c_arr, name):
    _, _, rh, cols = g.shape

    def body(c_ref, g_ref, r_ref, o_ref):
        o_ref[...] = (g_ref[...] + r_ref[...]).astype(BF16)

    blk = pl.BlockSpec((None, rh, cols), lambda k, cr: (k, 0, 0))
    return pl.pallas_call(
        body, name=name, out_shape=jax.ShapeDtypeStruct((4, rh, cols), BF16),
        grid_spec=pltpu.PrefetchScalarGridSpec(
            num_scalar_prefetch=1, grid=(4,),
            in_specs=[pl.BlockSpec((None, None, rh, cols), lambda k, cr: (k, cr[0], 0, 0)), blk], out_specs=blk),
        compiler_params=_cp(("parallel",)),
    )(c_arr, g, r)


def _sum_chips(cs, r3, place_arr, name):
    _, rh, cols = cs.shape
    rb = rh // 2

    def body(pl_ref, a_ref, r0_ref, r1_ref, r2_ref, o_ref):
        up = lambda ref: ref[...].astype(F32)
        o_ref[...] = ((up(a_ref) + up(r0_ref)) + up(r1_ref)) + up(r2_ref)

    def slot(d):
        return pl.BlockSpec((None, rb, cols), lambda i, pa: (d, i, 0))

    return pl.pallas_call(
        body, name=name, out_shape=jax.ShapeDtypeStruct((2, rh, cols), F32),
        grid_spec=pltpu.PrefetchScalarGridSpec(
            num_scalar_prefetch=1, grid=(2,),
            in_specs=[pl.BlockSpec((None, rb, cols), lambda i, pa: (pa[0], i, 0)), slot(0), slot(1), slot(2)],
            out_specs=pl.BlockSpec((None, rb, cols), lambda i, pa: (pa[1], i, 0))),
        compiler_params=_cp(("parallel",)),
    )(place_arr, cs, r3, r3, r3)


def _allreduce_small(pack):
    rows = pack.shape[0]
    hr = rows // 2

    def body(p_ref, o_ref, sib, slots, s1, r1, s2, r2, s3, r3):
        x, y, c, chip = _place()
        sibling = (x, y, 1 - c)
        ex = pltpu.make_async_remote_copy(src_ref=p_ref, dst_ref=sib, send_sem=s1, recv_sem=r1,
                                          device_id=sibling, device_id_type=MESH)
        ex.start()
        ex.wait()
        half = pl.ds(pl.multiple_of(c * hr, 16), hr)
        slots[0] = (p_ref[half, :] + sib[half, :]).astype(BF16)
        cps = []
        for d in (1, 2, 3):
            px, py, _ = _chip_at(x, y, d)
            cps.append(pltpu.make_async_remote_copy(
                src_ref=slots.at[0], dst_ref=slots.at[d], send_sem=s2.at[d - 1], recv_sem=r2.at[d - 1],
                device_id=(px, py, c), device_id_type=MESH))
        for cp in cps:
            cp.start()
        for cp in cps:
            cp.wait()
        tot = slots[chip].astype(F32)
        for k in (1, 2, 3):
            tot = tot + slots[jnp.bitwise_xor(chip, k)].astype(F32)
        o_ref[half, :] = tot
        back = pltpu.make_async_remote_copy(src_ref=o_ref.at[half, :], dst_ref=o_ref.at[half, :], send_sem=s3, recv_sem=r3,
                                            device_id=sibling, device_id_type=MESH)
        back.start()
        back.wait()

    vm = pl.BlockSpec(memory_space=pltpu.VMEM)
    return pl.pallas_call(
        body, name="allreduce_small", out_shape=jax.ShapeDtypeStruct((rows, 128), F32),
        in_specs=[vm], out_specs=vm,
        scratch_shapes=[pltpu.VMEM((rows, 128), F32), pltpu.VMEM((4, hr, 128), BF16),
                        pltpu.SemaphoreType.DMA, pltpu.SemaphoreType.DMA, pltpu.SemaphoreType.DMA((3,)), pltpu.SemaphoreType.DMA((3,)),
                        pltpu.SemaphoreType.DMA, pltpu.SemaphoreType.DMA],
        compiler_params=_cp(has_side_effects=True),
    )(pack)


def _adam_math(gv, wv, mv, vv):
    m2 = ADAM_B1 * mv + (1.0 - ADAM_B1) * gv
    v2 = ADAM_B2 * vv + (1.0 - ADAM_B2) * (gv * gv)
    m_hat = m2 / (1.0 - ADAM_B1 ** ADAM_STEP)
    v_hat = v2 / (1.0 - ADAM_B2 ** ADAM_STEP)
    return -ADAM_LR * (m_hat / (jnp.sqrt(v_hat) + ADAM_EPS) + ADAM_WD * wv), m2, v2


def _adam(g, w, m, v, name):
    rows, cols = g.shape
    rb = rows // 4

    def body(g_ref, w_ref, m_ref, v_ref, d_ref, m2_ref, v2_ref):
        d_ref[...], m2_ref[...], v2_ref[...] = _adam_math(g_ref[...], w_ref[...], m_ref[...], v_ref[...])

    blk = pl.BlockSpec((rb, cols), lambda i: (i, 0))
    shp = jax.ShapeDtypeStruct((rows, cols), F32)
    return pl.pallas_call(
        body, name=name, grid=(4,), in_specs=[blk] * 4, out_specs=[blk] * 3, out_shape=[shp] * 3,
        compiler_params=_cp(("parallel",)),
    )(g, w, m, v)


def _adam_layer(g, w, m, v, l, prev, name):
    rows, cols = g.shape
    rb = rows // 4

    def body(g_ref, w_ref, m_ref, v_ref, *rest):
        go_ref, d_ref, m2_ref, v2_ref = rest[-4:]
        gv = g_ref[...]
        go_ref[...] = gv
        d_ref[...], m2_ref[...], v2_ref[...] = _adam_math(gv, w_ref[...], m_ref[...], v_ref[...])

    lay = pl.BlockSpec((None, rb, cols), lambda i: (l, i, 0))
    shp = jax.ShapeDtypeStruct((2, rows, cols), F32)
    prev = () if prev is None else tuple(prev)
    return pl.pallas_call(
        body, name=name, grid=(4,), in_specs=[pl.BlockSpec((rb, cols), lambda i: (i, 0)), lay, lay, lay] + [ANY] * len(prev),
        out_specs=[lay] * 4, out_shape=[shp] * 4,
        input_output_aliases={4 + j: j for j in range(len(prev))},
        compiler_params=_cp(("parallel",)),
    )(g, w, m, v, *prev)


def _rows128(a):
    return a.reshape(-1, 128)


def _pack(arrs, mult):
    parts = [_rows128(a) for a in arrs]
    rows = sum(q.shape[0] for q in parts)
    pad = -rows % mult
    if pad:
        parts.append(jnp.zeros((pad, 128), F32))
    return jnp.concatenate(parts, axis=0)


def _unpack(pack, shapes):
    out, o = [], 0
    for s in shapes:
        n = 1
        for e in s:
            n *= e
        out.append(pack[o:o + n // 128].reshape(s))
        o += n // 128
    return out


WEIGHTS = ['norm1_g', 'w_in', 'gmlp_ln_g', 'gmlp_ln_b', 'gmlp_w_s', 'gmlp_b_s', 'conv_w', 'conv_b', 'lru_w_r', 'lru_b_r', 'lru_w_i',
           'lru_b_i', 'lru_lambda', 'w_out', 'norm2_g', 'w_ffn_in', 'w_ffn_out', 'final_g']
BIG = ['w_in', 'w_out', 'w_ffn_in', 'w_ffn_out']
SMALL = [n for n in WEIGHTS if n not in BIG]
CHIP_SHARDED_SMALL = ['conv_w', 'lru_b_r', 'lru_b_i', 'lru_lambda']


def kernel(x, norm1_g, w_in, gmlp_ln_g, gmlp_ln_b, gmlp_w_s, gmlp_b_s, conv_w, conv_b, lru_w_r, lru_b_r, lru_w_i, lru_b_i, lru_lambda, w_out, norm2_g, w_ffn_in, w_ffn_out, final_g, loss_target, m_norm1_g, m_w_in, m_gmlp_ln_g, m_gmlp_ln_b, m_gmlp_w_s, m_gmlp_b_s, m_conv_w, m_conv_b, m_lru_w_r, m_lru_b_r, m_lru_w_i, m_lru_b_i, m_lru_lambda, m_w_out, m_norm2_g, m_w_ffn_in, m_w_ffn_out, m_final_g, v_norm1_g, v_w_in, v_gmlp_ln_g, v_gmlp_ln_b, v_gmlp_w_s, v_gmlp_b_s, v_conv_w, v_conv_b, v_lru_w_r, v_lru_b_r, v_lru_w_i, v_lru_b_i, v_lru_lambda, v_w_out, v_norm2_g, v_w_ffn_in, v_w_ffn_out, v_final_g):
    a = dict(locals())
    w = {n: a[n] for n in WEIGHTS}
    mom = {n: a["m_" + n] for n in WEIGHTS}
    var = {n: a["v_" + n] for n in WEIGHTS}
    _, _, c, chip = _place()
    c_arr, chip_arr = jnp.reshape(c, (1,)).astype(jnp.int32), jnp.reshape(chip, (1,)).astype(jnp.int32)
    place_arr = jnp.stack([chip, c]).astype(jnp.int32)

    first, rest = BIG[:1], BIG[1:]

    def as_weights(names, full):
        wb = {n: f.reshape(4, 2 * f.shape[2], f.shape[3]) for n, f in zip(names, full)}
        if "w_out" in wb:
            wb["w_out"] = wb["w_out"].reshape(D, D)
            wb["w_ffn_out"] = wb["w_ffn_out"].reshape(DFF, D)
        return wb

    def cast(n, l):
        return _cast_into(w[n], l, chip_arr, f"cast_{n}_{l}")

    def landed(fly, names, after, tag):
        return as_weights(names, _gather_pass_on(_gather_wait(fly[0], fly[1], fly[2], after, tag), tag))

    fly_in = _gather_start([cast("w_in", 0)], "in")
    tiny = _pack([w[n] for n in CHIP_SHARDED_SMALL], 8)
    _, tiny_full = _gather_weights([], tiny)
    fly0 = _gather_start([cast(n, 0) for n in rest], "0", after=(fly_in[3], tiny_full))
    bufs1 = [cast(n, 1) for n in BIG]
    fly1 = _gather_start(bufs1, "1", after=(fly0[3],))
    p = {n: w[n] for n in SMALL}
    parts = [_unpack(tiny_full[k], [w[n].shape for n in CHIP_SHARDED_SMALL]) for k in range(4)]
    for i, n in enumerate(CHIP_SHARDED_SMALL):
        p[n] = jnp.concatenate([parts[k][i] for k in range(4)], axis=-1)

    xa, saved0 = _forward_layer(0, x[0], p, landed(fly_in, first, fly1[3], "in"), after=(fly0[3], fly1[3]),
                                rest=lambda merged: landed(fly0, rest, merged, "0"))
    xb, saved1 = _forward_layer(1, xa, p, landed(fly1, BIG, xa, "1"))
    dxb, loss_v, dfg = _loss_head(xb, loss_target[0], p["final_g"][None])
    loss = lax.psum(loss_v[0, 0], ("x", "y", "c"))

    out, flying = {}, {}

    def halves(grads):
        return [g.reshape(4, 2, -1, g.shape[-1]) for g in grads]

    def sibling_start(grads, names, l, tag):
        gs = halves(grads)
        lands = [lax.empty((4,) + g.shape[2:], g.dtype) for g in gs]
        flying["s" + tag] = (names, l) + tuple(
            _exchange_start(gs, lands, _sibling_copies, 4 * len(gs), f"grads_to_sibling_start_{tag}"))
        return (flying["s" + tag][-1],)

    def chips_start(gs, from_sib, names, l, tag):
        cs = [_add_half(g, r, c_arr, f"add_half_{n}_{l}") for n, g, r in zip(names, gs, from_sib)]
        lands = [lax.empty((3,) + a.shape[1:], a.dtype) for a in cs]
        flying[tag] = (names, l) + tuple(_exchange_start(cs, lands, _chips_copies, 3 * len(cs), f"grads_to_chips_start_{tag}"))
        return (flying[tag][-1],)

    def sibling_finish(tag, after):
        names, l, send, recv, gs, lands, _ = flying["s" + tag]
        gs, from_sib = _exchange_wait(send, recv, gs, lands, after, _sibling_copies, f"grads_to_sibling_wait_{tag}")
        return chips_start(gs, from_sib, names, l, tag)

    def reduce_start(grads, names, l, tag):
        gs = halves(grads)
        return chips_start(gs, _to_sibling_halves(gs, tag), names, l, tag)

    def reduce_finish(tag, after):
        names, l, send, recv, cs, lands, _ = flying[tag]
        cs, lands = _exchange_wait(send, recv, cs, lands, after, _chips_copies, f"grads_to_chips_wait_{tag}")
        ts = [_sum_chips(cc, r3, place_arr, f"sum_chips_{n}_{l}") for n, cc, r3 in zip(names, cs, lands)]
        for n, j in zip(names, _join_halves(ts, tag)):
            out[n] = _adam_layer(j.reshape(w[n].shape[1:]), w[n], mom[n], var[n], l, out.get(n), f"adam_{n}_{l}")
        return out[names[-1]][0]

    def late1(grads):
        return sibling_finish("1a", grads[0]) + sibling_start(grads, first, 1, "1b")

    def midway0(grads):
        reduce_finish("1a", grads[0])
        reduce_finish("1b", grads[0])
        return reduce_start(grads, rest, 0, "0a")

    dxa, big1, small1 = _backward_layer(1, dxb, saved1, midway=lambda grads: sibling_start(grads, rest, 1, "1a"), late=late1)
    dx, big0, small0 = _backward_layer(0, dxa, saved0, after=sibling_finish("1b", dxa), midway=midway0,
                                       late=lambda grads: reduce_start(grads, first, 0, "0b"))
    reduce_finish("0b", reduce_finish("0a", dx))
    small = {k: jnp.stack([small0[k], small1[k]]) for k in LAYER_SMALL}
    small["final_g"] = dfg[0]

    full_shapes = [small[n].shape for n in SMALL]
    red = _unpack(_allreduce_small(_pack([small[n] for n in SMALL], 32)), full_shapes)
    g_small = []
    for n, g in zip(SMALL, red):
        if n in CHIP_SHARDED_SMALL:
            g = lax.dynamic_slice_in_dim(g, chip * w[n].shape[-1], w[n].shape[-1], axis=g.ndim - 1)
        g_small.append(g)
    shapes = [w[n].shape for n in SMALL]
    packs = [_pack(lst, 32) for lst in (g_small, [w[n] for n in SMALL], [mom[n] for n in SMALL], [var[n] for n in SMALL])]
    upd = [_unpack(u, shapes) for u in _adam(*packs, "adam_small")]
    for i, n in enumerate(SMALL):
        out[n] = [g_small[i], upd[0][i], upd[1][i], upd[2][i]]

    return (loss, dx[None]) + tuple(out[n][i] for i in range(4) for n in WEIGHTS)
```

```python
import functools

import jax
import jax.numpy as jnp
from jax import lax
from jax.experimental import pallas as pl
from jax.experimental.pallas import tpu as pltpu

F32 = jnp.float32
BF16 = jnp.bfloat16
MESH = pl.DeviceIdType.MESH

D = 1024
NH = 8
HD = 128
CHUNK = 128
N_IN_T = 12
DFF = 2816
DFF_SH = 1408
EPS = 1e-6
LRU_C = 8.0
ADAM_LR, ADAM_B1, ADAM_B2, ADAM_EPS, ADAM_WD, ADAM_STEP = 0.001, 0.9, 0.999, 1e-08, 0.01, 10

TM = 512
TM_BIG = 1024
RT = 128
PADR = 8
VMEM_LIMIT = 56 * 1024 * 1024


def _cp(sem=None, **kw):
    if sem is not None:
        kw["dimension_semantics"] = sem
    return pltpu.CompilerParams(vmem_limit_bytes=VMEM_LIMIT, **kw)


_GC = 0.7978845608028654


def _sigmoid(x):
    return 1.0 / (1.0 + jnp.exp(-x))


def _gelu(x):
    return 0.5 * x * (1.0 + jnp.tanh(_GC * (x + 0.044715 * x * x * x)))


def _gelu_and_grad(x):
    t = jnp.tanh(_GC * (x + 0.044715 * x * x * x))
    g = 0.5 * x * (1.0 + t)
    dg = 0.5 * (1.0 + t) + 0.5 * x * (1.0 - t * t) * _GC * (1.0 + 3 * 0.044715 * x * x)
    return g, dg


def _softplus_neg(lam):
    y = jnp.exp(-jnp.abs(lam))
    u = 1.0 + y
    l1p = jnp.where(u == 1.0, y, jnp.log(u) * y / (u - 1.0))
    return jnp.maximum(-lam, 0.0) + l1p


def _dot(a, b):
    return jnp.dot(a, b, preferred_element_type=F32)


def _dot_nt(a, b):
    return lax.dot_general(a, b, (((1,), (1,)), ((), ())), preferred_element_type=F32)


def _dot_tn(a, b):
    return lax.dot_general(a, b, (((0,), (0,)), ((), ())), preferred_element_type=F32)


def _rms_hat(x):
    r = lax.rsqrt(jnp.mean(x * x, axis=-1, keepdims=True) + EPS)
    return x * r, r


def _rms_bwd(dh, x, g):
    xh, r = _rms_hat(x)
    dxh = dh * g
    dx = r * (dxh - xh * jnp.mean(dxh * xh, axis=-1, keepdims=True))
    return dx, jnp.sum(dh * xh, axis=0, keepdims=True)


def _norm_into(x_ref, g_ref, h_ref):
    xh, _ = _rms_hat(x_ref[...])
    h_ref[...] = (xh * g_ref[...]).astype(BF16)


def _in_tile(j):
    m, hf = j // 2, j % 2
    orig = jnp.where(m < 2, m, jnp.where(m == 2, 4, jnp.where(m < 5, m - 1, 5)))
    t = orig * 2 + hf
    return t // 3, t % 3


ANY = pl.BlockSpec(memory_space=pl.ANY)


def _mm_in(x, g, w_in, l, after=()):
    S = x.shape[0]
    tm = min(2 * TM_BIG, S)

    def body(x_ref, g_ref, w_ref, *rest):
        o_ref, h_ref = rest[-2:]

        @pl.when(pl.program_id(1) == 0)
        def _():
            _norm_into(x_ref, g_ref, h_ref)
        o_ref[...] = _dot(h_ref[...], w_ref[...]).astype(BF16)

    def w_map(i, j):
        sh, tl = _in_tile(j)
        return (sh, 0, tl)

    return pl.pallas_call(
        body, name=f"mm_in_{l}", grid=(S // tm, N_IN_T),
        in_specs=[pl.BlockSpec((tm, D), lambda i, j: (i, 0)), pl.BlockSpec((1, D), lambda i, j: (0, 0)),
                  pl.BlockSpec((None, D, 512), w_map)] + [ANY] * len(after),
        out_specs=[pl.BlockSpec((None, tm, 512), lambda i, j: (j // 2, i, j % 2)), pl.BlockSpec((tm, D), lambda i, j: (i, 0))],
        out_shape=[jax.ShapeDtypeStruct((6, S, D), BF16), jax.ShapeDtypeStruct((S, D), BF16)],
        compiler_params=_cp(("parallel", "arbitrary")),
    )(x, g, w_in, *after)


def _mm_res(a, w, res, l, name):
    S, K = a.shape

    def body(a_ref, w_ref, r_ref, o_ref):
        o_ref[...] = r_ref[...] + _dot(a_ref[...], w_ref[...])

    tm = TM
    return pl.pallas_call(
        body, name=f"{name}_{l}", grid=(S // tm,),
        in_specs=[pl.BlockSpec((tm, K), lambda i: (i, 0)), pl.BlockSpec((K, D), lambda i: (0, 0)),
                  pl.BlockSpec((tm, D), lambda i: (i, 0))],
        out_specs=pl.BlockSpec((tm, D), lambda i: (i, 0)),
        out_shape=jax.ShapeDtypeStruct((S, D), F32),
        compiler_params=_cp(("parallel",)),
    )(a, w, res)


def _mm_ffn_in(x, g, w_fi, l):
    S = x.shape[0]

    def body(x_ref, g_ref, w_ref, gu_ref, ff_ref, h_ref):
        @pl.when(pl.program_id(1) == 0)
        def _():
            _norm_into(x_ref, g_ref, h_ref)
        hv = h_ref[...]
        ga = _dot(hv, w_ref[0])
        gb = _dot(hv, w_ref[1])
        gu_ref[0] = ga.astype(BF16)
        gu_ref[1] = gb.astype(BF16)
        ff_ref[...] = (ga * _sigmoid(ga) * gb).astype(BF16)

    gu, ff, h = pl.pallas_call(
        body, name=f"mm_ffn_in_{l}", grid=(S // TM, 2),
        in_specs=[pl.BlockSpec((TM, D), lambda i, s: (i, 0)), pl.BlockSpec((1, D), lambda i, s: (0, 0)),
                  pl.BlockSpec((2, None, D, DFF_SH), lambda i, s: (0, s, 0, 0))],
        out_specs=[pl.BlockSpec((2, None, TM, DFF_SH), lambda i, s: (0, s, i, 0)),
                   pl.BlockSpec((TM, DFF_SH), lambda i, s: (i, s)),
                   pl.BlockSpec((TM, D), lambda i, s: (i, 0))],
        out_shape=[jax.ShapeDtypeStruct((2, 2, S, DFF_SH), BF16), jax.ShapeDtypeStruct((S, DFF), BF16),
                   jax.ShapeDtypeStruct((S, D), BF16)],
        compiler_params=_cp(("parallel", "arbitrary")),
    )(x, g, w_fi.reshape(2, 2, D, DFF_SH))
    return gu.reshape(4, S, DFF_SH), ff, h


def _gmlp_fwd(z6, ws_b, bs_b, lg, lb):
    S = z6.shape[1]

    def body(z_ref, ws_ref, bs_ref, lg_ref, lb_ref, o_ref, mix):
        gv = _gelu(z_ref[1].astype(F32))
        xc = gv - jnp.mean(gv, axis=-1, keepdims=True)
        rs = lax.rsqrt(jnp.mean(xc * xc, axis=-1, keepdims=True) + EPS)
        vb = (xc * rs * lg_ref[...] + lb_ref[...]).astype(BF16)
        for gi in range(NH):
            cs = slice(gi * HD, (gi + 1) * HD)
            mix[:, cs] = _dot(ws_ref[gi], vb[:, cs])
        o_ref[...] = (_sigmoid(z_ref[2].astype(F32)) * _gelu(z_ref[0].astype(F32)) * (mix[...] + bs_ref[...])).astype(BF16)

    return pl.pallas_call(
        body, name="gmlp_fwd", grid=(S // CHUNK,),
        in_specs=[pl.BlockSpec((3, CHUNK, D), lambda i: (0, i, 0)), pl.BlockSpec((NH, CHUNK, CHUNK), lambda i: (0, 0, 0)),
                  pl.BlockSpec((CHUNK, D), lambda i: (0, 0)), pl.BlockSpec((1, D), lambda i: (0, 0)),
                  pl.BlockSpec((1, D), lambda i: (0, 0))],
        out_specs=pl.BlockSpec((CHUNK, D), lambda i: (i, 0)),
        out_shape=jax.ShapeDtypeStruct((S, D), BF16),
        scratch_shapes=[pltpu.VMEM((CHUNK, D), F32)],
        compiler_params=_cp(("parallel",)),
    )(z6, ws_b, bs_b, lg, lb)


def _row_iota():
    return lax.broadcasted_iota(jnp.int32, (RT, HD), 0)


SUB = 8
UNROLL = 4
GRAD_ROWS = 256


def _scan_up(a, b, carry):
    row = lax.broadcasted_iota(jnp.int32, (SUB, HD), 0)
    masks = [(d, row >= d) for d in (1, 2, 4)]
    c = jnp.broadcast_to(carry, (SUB, HD))
    hs = []
    for j in range(RT // SUB):
        aj, bj = a[SUB * j:SUB * (j + 1)], b[SUB * j:SUB * (j + 1)]
        for d, m in masks:
            bj = bj + aj * jnp.where(m, pltpu.roll(bj, d, 0), 0.0)
            aj = aj * jnp.where(m, pltpu.roll(aj, d, 0), 1.0)
        h = bj + aj * c
        hs.append(h)
        c = jnp.broadcast_to(h[SUB - 1:SUB, :], (SUB, HD))
    return jnp.concatenate(hs, axis=0), hs[-1][SUB - 1:SUB, :]


def _scan_down(a, b, carry):
    row = lax.broadcasted_iota(jnp.int32, (SUB, HD), 0)
    masks = [(d, row < SUB - d) for d in (1, 2, 4)]
    c = jnp.broadcast_to(carry, (SUB, HD))
    hs = []
    for j in reversed(range(RT // SUB)):
        aj, bj = a[SUB * j:SUB * (j + 1)], b[SUB * j:SUB * (j + 1)]
        for d, m in masks:
            bj = bj + aj * jnp.where(m, pltpu.roll(bj, SUB - d, 0), 0.0)
            aj = aj * jnp.where(m, pltpu.roll(aj, SUB - d, 0), 1.0)
        h = bj + aj * c
        hs.append(h)
        c = jnp.broadcast_to(h[0:1, :], (SUB, HD))
    return jnp.concatenate(hs[::-1], axis=0), hs[-1][0:1, :]


def _decay(r, sp_d):
    log_a = -LRU_C * r * sp_d
    a = jnp.exp(log_a)
    return a, jnp.sqrt(jnp.maximum(-jnp.tanh(log_a) * (a * a + 1.0), 0.0))


def _lru_gates(xc, d, wr_ref, br_ref, wi_ref, bi_ref, sp):
    xb = xc.astype(BF16)
    r = _sigmoid(_dot(xb, wr_ref[d]) + br_ref[d:d + 1, :])
    i = _sigmoid(_dot(xb, wi_ref[d]) + bi_ref[d:d + 1, :])
    a, mult = _decay(r, sp[d:d + 1, :])
    return r, i, a, mult


def _shifted(win, k):
    w = RT + 2 * PADR
    v = win if k == 0 else pltpu.roll(win, (-k) % w, 0)
    return v[PADR:PADR + RT]


def _conv_taps(win):
    return [_shifted(win, k) for k in (-1, 0, 1, 2)]


def _fill_padded(dst, src_ref, S):
    zeros = jnp.zeros((PADR, HD), F32)
    dst[0:PADR, :] = zeros
    dst[PADR + S:2 * PADR + S, :] = zeros

    def cp(i, c):
        t0 = pl.multiple_of(i * RT, RT)
        dst[pl.ds(t0 + PADR, RT), :] = src_ref[pl.ds(t0, RT), :].astype(F32)
        return c
    lax.fori_loop(0, S // RT, cp, 0)


def _conv_fwd_all(zxp, xc_s, cw_ref, cb_ref, S):
    def cv(i, c):
        t0 = pl.multiple_of(i * RT, RT)
        xm1, x0, xp1, xp2 = _conv_taps(zxp[pl.ds(t0, RT + 2 * PADR), :])
        xc_s[pl.ds(t0, RT), :] = (cb_ref[...] + xm1 * cw_ref[0:1, :] + x0 * cw_ref[1:2, :]
                                  + xp1 * cw_ref[2:3, :] + xp2 * cw_ref[3:4, :])
        return c
    lax.fori_loop(0, S // RT, cv, 0)


def _lru_specs(S):
    head = lambda h: (0, h)
    return [pl.BlockSpec((4, HD), head), pl.BlockSpec((1, HD), head),
            pl.BlockSpec((2, None, HD, HD), lambda h: (0, h, 0, 0)), pl.BlockSpec((2, HD), head),
            pl.BlockSpec((2, None, HD, HD), lambda h: (0, h, 0, 0)), pl.BlockSpec((2, HD), head),
            pl.BlockSpec((2, HD), head)]


def _lru_fwd(z6, ya, cw, cb, wr, br, wi, bi, lam):
    S = z6.shape[1]
    nt = S // RT

    def body(z_ref, ya_ref, cw_ref, cb_ref, wr_ref, br_ref, wi_ref, bi_ref, lam_ref, mg_ref, h0_ref, h1_ref, zxp, xc_s):
        sp = _softplus_neg(lam_ref[...])
        _fill_padded(zxp, z_ref.at[0], S)
        _conv_fwd_all(zxp, xc_s, cw_ref, cb_ref, S)

        def scans(i, carry):
            cu, cd = carry
            for u in range(UNROLL):
                j = i * UNROLL + u
                ru = pl.ds(pl.multiple_of(j * RT, RT), RT)
                rd = pl.ds(pl.multiple_of((nt - 1 - j) * RT, RT), RT)
                xu, xd = xc_s[ru, :], xc_s[rd, :]
                _, gi, a, mult = _lru_gates(xu, 0, wr_ref, br_ref, wi_ref, bi_ref, sp)
                hu, cu = _scan_up(a, mult * gi * xu, cu)
                h0_ref[ru, :] = hu
                _, gi, a, mult = _lru_gates(xd, 1, wr_ref, br_ref, wi_ref, bi_ref, sp)
                hd, cd = _scan_down(a, mult * gi * xd, cd)
                h1_ref[rd, :] = hd
            return cu, cd
        z1 = jnp.zeros((1, HD), F32)
        lax.fori_loop(0, nt // UNROLL, scans, (z1, z1))

        def merge(i, c):
            rows = pl.ds(pl.multiple_of(i * RT, RT), RT)
            yb = (h0_ref[rows, :] + h1_ref[rows, :]) * _gelu(z_ref[1, rows, :].astype(F32))
            mg_ref[rows, :] = (ya_ref[rows, :].astype(F32) + _sigmoid(z_ref[2, rows, :].astype(F32)) * yb).astype(BF16)
            return c
        lax.fori_loop(0, nt, merge, 0)

    col = pl.BlockSpec((S, HD), lambda h: (0, h))
    return pl.pallas_call(
        body, name="lru_fwd", grid=(NH,),
        in_specs=[pl.BlockSpec((3, S, HD), lambda h: (1, 0, h)), col] + _lru_specs(S),
        out_specs=[col, col, col],
        out_shape=[jax.ShapeDtypeStruct((S, D), BF16), jax.ShapeDtypeStruct((S, D), F32), jax.ShapeDtypeStruct((S, D), F32)],
        scratch_shapes=[pltpu.VMEM((S + 2 * PADR, HD), F32), pltpu.VMEM((S, HD), F32)],
        compiler_params=_cp(("parallel",)),
    )(z6, ya, cw, cb, wr, br, wi, bi, lam)


def _loss_head(x, tgt, g):
    S = x.shape[0]

    def body(x_ref, t_ref, g_ref, dx_ref, loss_ref, dg_ref):
        @pl.when(pl.program_id(0) == 0)
        def _():
            loss_ref[...] = jnp.zeros_like(loss_ref)
            dg_ref[...] = jnp.zeros_like(dg_ref)
        xv = x_ref[...]
        xh, _ = _rms_hat(xv)
        e = xh * g_ref[...] - t_ref[...]
        loss_ref[...] += jnp.sum(e * e) * (0.5 / D)
        dx, dgs = _rms_bwd(e * (1.0 / D), xv, g_ref[...])
        dx_ref[...] = dx
        dg_ref[...] += dgs

    return pl.pallas_call(
        body, name="loss_head", grid=(S // TM,),
        in_specs=[pl.BlockSpec((TM, D), lambda i: (i, 0)), pl.BlockSpec((TM, D), lambda i: (i, 0)),
                  pl.BlockSpec((1, D), lambda i: (0, 0))],
        out_specs=[pl.BlockSpec((TM, D), lambda i: (i, 0)), pl.BlockSpec((1, 128), lambda i: (0, 0)),
                   pl.BlockSpec((1, D), lambda i: (0, 0))],
        out_shape=[jax.ShapeDtypeStruct((S, D), F32), jax.ShapeDtypeStruct((1, 128), F32), jax.ShapeDtypeStruct((1, D), F32)],
        compiler_params=_cp(("arbitrary",)),
    )(x, tgt, g)


def _bwd_ffn_out(dx, w_fo, gu, l, after=()):
    S = dx.shape[0]

    def body(dx_ref, w_ref, gu_ref, *rest):
        o_ref = rest[-1]
        d = _dot_nt(dx_ref[...].astype(BF16), w_ref[...])
        ga, gb = gu_ref[0].astype(F32), gu_ref[1].astype(F32)
        sg = _sigmoid(ga)
        o_ref[0] = (d * gb * sg * (1.0 + ga * (1.0 - sg))).astype(BF16)
        o_ref[1] = (d * ga * sg).astype(BF16)

    pair = pl.BlockSpec((2, None, TM, DFF_SH), lambda i, s: (0, s, i, 0))
    dgu = pl.pallas_call(
        body, name=f"bwd_ffn_out_{l}", grid=(S // TM, 2),
        in_specs=[pl.BlockSpec((TM, D), lambda i, s: (i, 0)), pl.BlockSpec((DFF_SH, D), lambda i, s: (s, 0)), pair]
        + [ANY] * len(after),
        out_specs=pair,
        out_shape=jax.ShapeDtypeStruct((2, 2, S, DFF_SH), BF16),
        compiler_params=_cp(("parallel", "arbitrary")),
    )(dx, w_fo, gu.reshape(2, 2, S, DFF_SH), *after)
    return dgu.reshape(4, S, DFF_SH)


def _mm_tn(a, b, m_blk, tk, name):
    S, M = a.shape

    def body(a_ref, b_ref, o_ref):
        @pl.when(pl.program_id(1) == 0)
        def _():
            o_ref[...] = jnp.zeros_like(o_ref)
        o_ref[...] += _dot_tn(a_ref[...], b_ref[...].astype(BF16))

    return pl.pallas_call(
        body, name=name, grid=(M // m_blk, S // tk),
        in_specs=[pl.BlockSpec((tk, m_blk), lambda m, k: (k, m)), pl.BlockSpec((tk, D), lambda m, k: (k, 0))],
        out_specs=pl.BlockSpec((m_blk, D), lambda m, k: (m, 0)),
        out_shape=jax.ShapeDtypeStruct((M, D), F32),
        compiler_params=_cp(("parallel", "arbitrary")),
    )(a, b)


def _mm_nt_rms_bwd(a, a_spec, w, w_spec, nk, tm, x, g, dres, name, after=()):
    S = x.shape[0]

    def body(a_ref, w_ref, x_ref, g_ref, r_ref, *rest):
        dx_ref, dg_ref, acc = rest[-3:]
        i, k = pl.program_id(0), pl.program_id(1)

        @pl.when(k == 0)
        def _():
            acc[...] = jnp.zeros_like(acc)
        acc[...] += _dot_nt(a_ref[...], w_ref[...])

        @pl.when(jnp.logical_and(i == 0, k == 0))
        def _():
            dg_ref[...] = jnp.zeros_like(dg_ref)

        @pl.when(k == nk - 1)
        def _():
            dx, dgs = _rms_bwd(acc[...], x_ref[...], g_ref[...])
            dx_ref[...] = r_ref[...] + dx
            dg_ref[...] += dgs

    row = pl.BlockSpec((tm, D), lambda i, k: (i, 0))
    vec = pl.BlockSpec((1, D), lambda i, k: (0, 0))
    return pl.pallas_call(
        body, name=name, grid=(S // tm, nk),
        in_specs=[a_spec, w_spec, row, vec, row] + [ANY] * len(after),
        out_specs=[row, vec],
        out_shape=[jax.ShapeDtypeStruct((S, D), F32), jax.ShapeDtypeStruct((1, D), F32)],
        scratch_shapes=[pltpu.VMEM((tm, D), F32)],
        compiler_params=_cp(("arbitrary", "arbitrary")),
    )(a, w, x, g, dres, *after)


def _dw_ffn_in(h, dgu, l):
    S = h.shape[0]

    def body(h_ref, b_ref, o_ref):
        @pl.when(pl.program_id(1) == 0)
        def _():
            o_ref[...] = jnp.zeros_like(o_ref)
        o_ref[...] += _dot_tn(h_ref[...], b_ref[...])

    tk = min(2 * TM_BIG, S)
    return pl.pallas_call(
        body, name=f"dw_ffn_in_{l}", grid=(4, S // tk),
        in_specs=[pl.BlockSpec((tk, D), lambda j, k: (k, 0)), pl.BlockSpec((None, tk, DFF_SH), lambda j, k: (j, k, 0))],
        out_specs=pl.BlockSpec((None, D, DFF_SH), lambda j, k: (j, 0, 0)),
        out_shape=jax.ShapeDtypeStruct((4, D, DFF_SH), F32),
        compiler_params=_cp(("parallel", "arbitrary")),
    )(h, dgu)


_HALF_COMPS = ((0, 1, 3), (4, 2, 5))


def _dw_in(h, dz6, l):
    S = h.shape[0]

    def body(h_ref, d0_ref, d1_ref, d2_ref, o_ref):
        @pl.when(pl.program_id(1) == 0)
        def _():
            o_ref[...] = jnp.zeros_like(o_ref)
        hv = h_ref[...]
        for q, d_ref in enumerate((d0_ref, d1_ref, d2_ref)):
            for hf in range(2):
                col = 1024 * q + 512 * hf
                o_ref[col // 1536, :, col % 1536:col % 1536 + 512] += _dot_tn(hv, d_ref[:, 512 * hf:512 * (hf + 1)])

    tk = min(TM_BIG, S)

    def comp(q):
        return pl.BlockSpec((None, tk, D), lambda p, k: (jnp.where(p == 0, _HALF_COMPS[0][q], _HALF_COMPS[1][q]), k, 0))

    return pl.pallas_call(
        body, name=f"dw_in_{l}", grid=(2, S // tk),
        in_specs=[pl.BlockSpec((tk, D), lambda p, k: (k, 0)), comp(0), comp(1), comp(2)],
        out_specs=pl.BlockSpec((2, D, 1536), lambda p, k: (p, 0, 0)),
        out_shape=jax.ShapeDtypeStruct((4, D, 1536), F32),
        compiler_params=_cp(("parallel", "arbitrary")),
    )(h, dz6, dz6, dz6)


def _bwd_out(dx, w_o, merged, l):
    S = dx.shape[0]

    def body(dx_ref, w_ref, m_ref, dm_ref, dw_ref):
        @pl.when(pl.program_id(0) == 0)
        def _():
            dw_ref[...] = jnp.zeros_like(dw_ref)
        dxb = dx_ref[...].astype(BF16)
        dm_ref[...] = _dot_nt(dxb, w_ref[...]).astype(BF16)
        dw_ref[...] += _dot_tn(m_ref[...], dxb)

    tm = TM
    row = pl.BlockSpec((tm, D), lambda i: (i, 0))
    return pl.pallas_call(
        body, name=f"bwd_out_{l}", grid=(S // tm,),
        in_specs=[row, pl.BlockSpec((D, D), lambda i: (0, 0)), row],
        out_specs=[row, pl.BlockSpec((D, D), lambda i: (0, 0))],
        out_shape=[jax.ShapeDtypeStruct((S, D), BF16), jax.ShapeDtypeStruct((D, D), F32)],
        compiler_params=_cp(("arbitrary",)),
    )(dx, w_o, merged)


def _gmlp_bwd(dm, z6, ws_b, wst_b, bs_b, lg, lb, after=()):
    S = z6.shape[1]

    def body(dm_ref, z_ref, ws_ref, wst_ref, bs_ref, lg_ref, lb_ref, *rest):
        dz_ref, dws_ref, dbs_ref, dlg_ref, dlb_ref, mix, dv = rest[-7:]

        @pl.when(pl.program_id(0) == 0)
        def _():
            dws_ref[...] = jnp.zeros_like(dws_ref)
            dbs_ref[...] = jnp.zeros_like(dbs_ref)
            dlg_ref[...] = jnp.zeros_like(dlg_ref)
            dlb_ref[...] = jnp.zeros_like(dlb_ref)
        gv, dgelu_v = _gelu_and_grad(z_ref[1].astype(F32))
        xc = gv - jnp.mean(gv, axis=-1, keepdims=True)
        rs = lax.rsqrt(jnp.mean(xc * xc, axis=-1, keepdims=True) + EPS)
        vh = xc * rs
        vb = (vh * lg_ref[...] + lb_ref[...]).astype(BF16)
        for gi in range(NH):
            cs = slice(gi * HD, (gi + 1) * HD)
            mix[:, cs] = _dot(ws_ref[gi], vb[:, cs])
        u, dgelu_u = _gelu_and_grad(z_ref[0].astype(F32))
        sa = _sigmoid(z_ref[2].astype(F32))
        mixed = mix[...] + bs_ref[...]
        dyg = dm_ref[...].astype(F32)
        dz_ref[2] = (dyg * u * mixed * sa * (1.0 - sa)).astype(BF16)
        dya = dyg * sa
        dz_ref[0] = (dya * mixed * dgelu_u).astype(BF16)
        dmix = dya * u
        dmb = dmix.astype(BF16)
        for gi in range(NH):
            cs = slice(gi * HD, (gi + 1) * HD)
            dv[:, cs] = _dot(wst_ref[gi], dmb[:, cs])
            dws_ref[gi] += _dot_nt(dmb[:, cs], vb[:, cs])
            dbs_ref[gi] += jnp.broadcast_to(jnp.sum(dmix[:, cs], axis=1, keepdims=True), (CHUNK, HD))
        dvv = dv[...]
        dlg_ref[...] += jnp.sum(dvv * vh, axis=0, keepdims=True)
        dlb_ref[...] += jnp.sum(dvv, axis=0, keepdims=True)
        dvh = dvv * lg_ref[...]
        dgv = rs * (dvh - jnp.mean(dvh, axis=-1, keepdims=True) - vh * jnp.mean(dvh * vh, axis=-1, keepdims=True))
        dz_ref[1] = (dgv * dgelu_v).astype(BF16)

    vec = pl.BlockSpec((1, D), lambda i: (0, 0))
    mat = pl.BlockSpec((NH, CHUNK, CHUNK), lambda i: (0, 0, 0))
    return pl.pallas_call(
        body, name="gmlp_bwd", grid=(S // CHUNK,),
        in_specs=[pl.BlockSpec((CHUNK, D), lambda i: (i, 0)), pl.BlockSpec((3, CHUNK, D), lambda i: (0, i, 0)), mat, mat,
                  pl.BlockSpec((CHUNK, D), lambda i: (0, 0)), vec, vec] + [ANY] * len(after),
        out_specs=[pl.BlockSpec((3, CHUNK, D), lambda i: (0, i, 0)), mat, mat, vec, vec],
        out_shape=[jax.ShapeDtypeStruct((6, S, D), BF16), jax.ShapeDtypeStruct((NH, CHUNK, CHUNK), F32),
                   jax.ShapeDtypeStruct((NH, CHUNK, HD), F32), jax.ShapeDtypeStruct((1, D), F32), jax.ShapeDtypeStruct((1, D), F32)],
        scratch_shapes=[pltpu.VMEM((CHUNK, D), F32), pltpu.VMEM((CHUNK, D), F32)],
        compiler_params=_cp(("arbitrary",)),
    )(dm, z6, ws_b, wst_b, bs_b, lg, lb, *after)


def _lru_bwd(dz6, dm, z6, h0, h1, cw, cb, wr, br, wi, bi, lam):
    S = z6.shape[1]
    nt = S // RT

    def body(dz_in, dm_ref, z_ref, h0_ref, h1_ref, cw_ref, cb_ref, wr_ref, br_ref, wi_ref, bi_ref, lam_ref,
             dz_ref, dcw_ref, dcb_ref, dwr_ref, dbr_ref, dwi_ref, dbi_ref, dlam_ref, zxp, xc_s, dhs_s, dxcp, r_s, lam_s):
        del dz_in
        lam = lam_ref[...]
        sp = _softplus_neg(lam)
        row = _row_iota()
        _fill_padded(zxp, z_ref.at[0], S)
        _conv_fwd_all(zxp, xc_s, cw_ref, cb_ref, S)
        zeros = jnp.zeros((PADR, HD), F32)
        dxcp[0:PADR, :] = zeros
        dxcp[PADR + S:2 * PADR + S, :] = zeros
        dwr_ref[...] = jnp.zeros_like(dwr_ref)
        dwi_ref[...] = jnp.zeros_like(dwi_ref)

        def pre(i, c):
            rows = pl.ds(pl.multiple_of(i * RT, RT), RT)
            hs = h0_ref[rows, :] + h1_ref[rows, :]
            dmv = dm_ref[rows, :].astype(F32)
            sb = _sigmoid(z_ref[2, rows, :].astype(F32))
            gg, dgg = _gelu_and_grad(z_ref[1, rows, :].astype(F32))
            dz_ref[2, rows, :] = (dmv * hs * gg * sb * (1.0 - sb)).astype(BF16)
            dyb = dmv * sb
            dz_ref[1, rows, :] = (dyb * hs * dgg).astype(BF16)
            dhs_s[rows, :] = dyb * gg
            return c
        lax.fori_loop(0, nt, pre, 0)

        def gate_bwd(d, gates, lamv, da, xc):
            r, gi, a, mult = gates
            dmult = lamv * gi * xc
            dgi = lamv * mult * xc
            dlog = (da - dmult * a / mult) * a
            dpr = (dlog * (-LRU_C) * sp[d:d + 1, :]) * r * (1.0 - r)
            dpi = dgi * gi * (1.0 - gi)
            xb, dprb, dpib = xc.astype(BF16), dpr.astype(BF16), dpi.astype(BF16)
            dwr_ref[d] += _dot_tn(xb, dprb)
            dwi_ref[d] += _dot_tn(xb, dpib)
            dxc = lamv * mult * gi + _dot_nt(dprb, wr_ref[d]) + _dot_nt(dpib, wi_ref[d])
            return dxc, (jnp.sum(dlog * r, axis=0, keepdims=True) * (-LRU_C), jnp.sum(dpr, axis=0, keepdims=True),
                         jnp.sum(dpi, axis=0, keepdims=True))

        def rgates(i, c):
            for u in range(UNROLL):
                rows = pl.ds(pl.multiple_of((i * UNROLL + u) * RT, RT), RT)
                xb = xc_s[rows, :].astype(BF16)
                for d in range(2):
                    r_s[d, rows, :] = _sigmoid(_dot(xb, wr_ref[d]) + br_ref[d:d + 1, :])
            return c
        lax.fori_loop(0, nt // UNROLL, rgates, 0)

        def chains(i, carry):
            qn, qp = carry
            for u in range(UNROLL):
                j = i * UNROLL + u
                rd = pl.ds(pl.multiple_of((nt - 1 - j) * RT, RT), RT)
                a, dhs = _decay(r_s[0, rd, :], sp[0:1, :])[0], dhs_s[rd, :]
                q, q_first = _scan_down(a, a * dhs, qn)
                lam_s[0, rd, :] = dhs + jnp.where(row == RT - 1, qn, pltpu.roll(q, RT - 1, 0))
                qn = q_first
                ru = pl.ds(pl.multiple_of(j * RT, RT), RT)
                a, dhs = _decay(r_s[1, ru, :], sp[1:2, :])[0], dhs_s[ru, :]
                q, q_last = _scan_up(a, a * dhs, qp)
                lam_s[1, ru, :] = dhs + jnp.where(row == 0, qp, pltpu.roll(q, 1, 0))
                qp = q_last
            return qn, qp

        z1 = jnp.zeros((1, HD), F32)
        lax.fori_loop(0, nt // UNROLL, chains, (z1, z1))

        ct = min(GRAD_ROWS, S)
        crow = lax.broadcasted_iota(jnp.int32, (ct, HD), 0)

        def tile_grads(i, acc):
            t0 = pl.multiple_of(i * ct, ct)
            rows = pl.ds(t0, ct)
            xc = xc_s[rows, :]
            xb = xc.astype(BF16)
            tp = pl.multiple_of(jnp.maximum(t0 - PADR, 0), PADR)
            prev = jnp.where(t0 > 0, h0_ref[pl.ds(tp, PADR), :][PADR - 1:PADR, :], 0.0)
            tn = pl.multiple_of(jnp.minimum(t0 + ct, S - PADR), PADR)
            nxt = jnp.where(t0 + ct < S, h1_ref[pl.ds(tn, PADR), :][0:1, :], 0.0)
            hside = (jnp.where(crow == 0, prev, pltpu.roll(h0_ref[rows, :], 1, 0)),
                     jnp.where(crow == ct - 1, nxt, pltpu.roll(h1_ref[rows, :], ct - 1, 0)))
            dxc, sums = 0.0, ()
            for d in range(2):
                r = r_s[d, rows, :]
                gi = _sigmoid(_dot(xb, wi_ref[d]) + bi_ref[d:d + 1, :])
                a, mult = _decay(r, sp[d:d + 1, :])
                lamv = lam_s[d, rows, :]
                dxc_d, s_d = gate_bwd(d, (r, gi, a, mult), lamv, lamv * hside[d], xc)
                dxc = dxc + dxc_d
                sums = sums + s_d
            dxcp[pl.ds(t0 + PADR, ct), :] = dxc
            return tuple(x + y for x, y in zip(acc, sums))

        s_sp0, s_br0, s_bi0, s_sp1, s_br1, s_bi1 = lax.fori_loop(0, S // ct, tile_grads, (z1,) * 6)

        dsp = jnp.concatenate([s_sp0, s_sp1], axis=0)
        dlam_ref[...] = -dsp * _sigmoid(-lam)
        dbr_ref[...] = jnp.concatenate([s_br0, s_br1], axis=0)
        dbi_ref[...] = jnp.concatenate([s_bi0, s_bi1], axis=0)

        def conv_bwd(i, carry):
            c0, c1, c2, c3, cb_ = carry
            t0 = pl.multiple_of(i * RT, RT)
            dwin = dxcp[pl.ds(t0, RT + 2 * PADR), :]
            d0 = _shifted(dwin, 0)
            dz_ref[0, pl.ds(t0, RT), :] = (_shifted(dwin, 1) * cw_ref[0:1, :] + d0 * cw_ref[1:2, :]
                                           + _shifted(dwin, -1) * cw_ref[2:3, :] + _shifted(dwin, -2) * cw_ref[3:4, :]).astype(BF16)
            xm1, x0, xp1, xp2 = _conv_taps(zxp[pl.ds(t0, RT + 2 * PADR), :])
            sm = lambda v: jnp.sum(v, axis=0, keepdims=True)
            return c0 + sm(d0 * xm1), c1 + sm(d0 * x0), c2 + sm(d0 * xp1), c3 + sm(d0 * xp2), cb_ + sm(d0)

        c0, c1, c2, c3, cb_ = lax.fori_loop(0, nt, conv_bwd, (z1, z1, z1, z1, z1))
        dcw_ref[...] = jnp.concatenate([c0, c1, c2, c3], axis=0)
        dcb_ref[...] = cb_

    col = pl.BlockSpec((S, HD), lambda h: (0, h))
    head = lambda h: (0, h)
    wspec = pl.BlockSpec((2, None, HD, HD), lambda h: (0, h, 0, 0))
    return pl.pallas_call(
        body, name="lru_bwd", grid=(NH,),
        in_specs=[pl.BlockSpec(memory_space=pl.ANY), col, pl.BlockSpec((3, S, HD), lambda h: (1, 0, h)), col, col] + _lru_specs(S),
        out_specs=[pl.BlockSpec((3, S, HD), lambda h: (1, 0, h)), pl.BlockSpec((4, HD), head), pl.BlockSpec((1, HD), head),
                   wspec, pl.BlockSpec((2, HD), head), wspec, pl.BlockSpec((2, HD), head), pl.BlockSpec((2, HD), head)],
        out_shape=[jax.ShapeDtypeStruct((6, S, D), BF16), jax.ShapeDtypeStruct((4, D), F32), jax.ShapeDtypeStruct((1, D), F32),
                   jax.ShapeDtypeStruct((2, NH, HD, HD), F32), jax.ShapeDtypeStruct((2, D), F32),
                   jax.ShapeDtypeStruct((2, NH, HD, HD), F32), jax.ShapeDtypeStruct((2, D), F32), jax.ShapeDtypeStruct((2, D), F32)],
        scratch_shapes=[pltpu.VMEM((S + 2 * PADR, HD), F32), pltpu.VMEM((S, HD), F32), pltpu.VMEM((S, HD), F32),
                        pltpu.VMEM((S + 2 * PADR, HD), F32), pltpu.VMEM((2, S, HD), F32), pltpu.VMEM((2, S, HD), F32)],
        input_output_aliases={0: 0},
        compiler_params=_cp(("parallel",)),
    )(dz6, dm, z6, h0, h1, cw, cb, wr, br, wi, bi, lam)


LAYER_SMALL = ("norm1_g", "gmlp_ln_g", "gmlp_ln_b", "gmlp_w_s", "gmlp_b_s", "conv_w", "conv_b",
               "lru_w_r", "lru_b_r", "lru_w_i", "lru_b_i", "lru_lambda", "norm2_g")


def _forward_layer(l, x, p, wb, after=(), rest=None):
    g1, g2 = p["norm1_g"][l][None], p["norm2_g"][l][None]
    ws_b = p["gmlp_w_s"][l].astype(BF16)
    tm = dict(ws_b=ws_b, wst_b=jnp.swapaxes(ws_b, 1, 2), bs_b=jnp.repeat(p["gmlp_b_s"][l].T, HD, axis=1),
              lg=p["gmlp_ln_g"][l][None], lb=p["gmlp_ln_b"][l][None])
    lru = (p["conv_w"][l], p["conv_b"][l][None], p["lru_w_r"][l].astype(BF16), p["lru_b_r"][l],
           p["lru_w_i"][l].astype(BF16), p["lru_b_i"][l], p["lru_lambda"][l])
    z6, hn1 = _mm_in(x, g1, wb["w_in"], l, after)
    ya = _gmlp_fwd(z6, tm["ws_b"], tm["bs_b"], tm["lg"], tm["lb"])
    merged, h0, h1 = _lru_fwd(z6, ya, *lru)
    if rest is not None:
        wb = dict(wb, **rest(merged))
    x1 = _mm_res(merged, wb["w_out"], x, l, "mm_out")
    gu, ff, hn2 = _mm_ffn_in(x1, g2, wb["w_ffn_in"], l)
    x2 = _mm_res(ff, wb["w_ffn_out"], x1, l, "mm_ffn_out")
    return x2, dict(x=x, z6=z6, h0=h0, h1=h1, merged=merged, x1=x1, gu=gu, ff=ff, g1=g1, g2=g2, tm=tm, lru=lru,
                    hn1=hn1, hn2=hn2, wb=wb)


def _backward_layer(l, dx, s, after=(), midway=None, late=None):
    S = dx.shape[0]
    tm, wb = s["tm"], s["wb"]
    g2 = s["g2"]
    dgu = _bwd_ffn_out(dx, wb["w_ffn_out"], s["gu"], l, after)
    tmb = min(TM_BIG, S)
    dwfo = _mm_tn(s["ff"], dx, DFF_SH, tmb, f"dw_ffn_out_{l}")
    dx1, dg2 = _mm_nt_rms_bwd(
        dgu, pl.BlockSpec((None, tmb, DFF_SH), lambda i, k: (k, i, 0)),
        wb["w_ffn_in"], pl.BlockSpec((None, D, DFF_SH), lambda i, k: (k, 0, 0)),
        4, tmb, s["x1"], g2, dx, f"bwd_ffn_in_{l}")
    dwfi = _dw_ffn_in(s["hn2"], dgu, l)
    dmg, dwo = _bwd_out(dx1, wb["w_out"], s["merged"], l)
    mid = () if midway is None else tuple(midway([dwo, dwfi, dwfo]))
    dz6, dws, dbs, dlg, dlb = _gmlp_bwd(dmg, s["z6"], tm["ws_b"], tm["wst_b"], tm["bs_b"], tm["lg"], tm["lb"], mid)
    dz6, dcw, dcb, dwr, dbr, dwi, dbi, dlam = _lru_bwd(dz6, dmg, s["z6"], s["h0"], s["h1"], *s["lru"])

    def w_map(i, k):
        sh, tl = _in_tile(k)
        return (sh, 0, tl)

    dwin = _dw_in(s["hn1"], dz6, l)
    tail = () if late is None else tuple(late([dwin]))
    dx0, dg1 = _mm_nt_rms_bwd(
        dz6, pl.BlockSpec((None, tmb, 512), lambda i, k: (k // 2, i, k % 2)),
        wb["w_in"], pl.BlockSpec((None, D, 512), w_map),
        N_IN_T, tmb, s["x"], s["g1"], dx1, f"bwd_in_{l}", tail)
    small = dict(norm1_g=dg1[0], gmlp_ln_g=dlg[0], gmlp_ln_b=dlb[0], gmlp_w_s=dws, gmlp_b_s=dbs[:, :, 0], conv_w=dcw, conv_b=dcb[0],
                 lru_w_r=dwr, lru_b_r=dbr, lru_w_i=dwi, lru_b_i=dbi, lru_lambda=dlam, norm2_g=dg2[0])
    return dx0, [dwin, dwo, dwfi, dwfo], small


def _local_step(x, tgt, p, wbs):
    saved = []
    for l in range(2):
        x, s = _forward_layer(l, x, p, wbs[l])
        saved.append(s)
    dx, loss_v, dfg = _loss_head(x, tgt, p["final_g"][None])
    big, smalls = [None, None], [None, None]
    for l in (1, 0):
        dx, big[l], smalls[l] = _backward_layer(l, dx, saved[l])
    small = {k: jnp.stack([smalls[0][k], smalls[1][k]]) for k in LAYER_SMALL}
    small["final_g"] = dfg[0]
    return loss_v, dx, big, small


def _place():
    x, y, c = lax.axis_index("x"), lax.axis_index("y"), lax.axis_index("c")
    return x, y, c, 2 * x + y


def _chip_at(x, y, d):
    px = 1 - x if d & 2 else x
    py = 1 - y if d & 1 else y
    return px, py, 2 * px + py


HBM = pl.BlockSpec(memory_space=pltpu.HBM)
SEM = pl.BlockSpec(memory_space=pltpu.SEMAPHORE)
DATAFLOW = pltpu.SideEffectType.DATAFLOW_SIDE_EFFECTING


def _in_hbm(a):
    return pltpu.with_memory_space_constraint(a, pltpu.HBM)


def _cast_into(wf, l, chip_arr, name):
    _, rows, cols = wf.shape
    rh = rows // 2

    def body(ch_ref, w_ref, o_ref):
        o_ref[...] = w_ref[...].astype(BF16)

    return pl.pallas_call(
        body, name=name, out_shape=jax.ShapeDtypeStruct((4, 2, rh, cols), BF16),
        grid_spec=pltpu.PrefetchScalarGridSpec(
            num_scalar_prefetch=1, grid=(2,),
            in_specs=[pl.BlockSpec((None, None, rh, cols), lambda h, ch: (l, h, 0, 0))],
            out_specs=pl.BlockSpec((None, None, rh, cols), lambda h, ch: (ch[0], h, 0, 0))),
        compiler_params=_cp(("parallel",)),
    )(chip_arr, wf.reshape(2, 2, rh, cols))


def _half_block(ref, chip, half, to, send_sem, recv_sem):
    blk = ref.at[chip, half]
    return pltpu.make_async_remote_copy(src_ref=blk, dst_ref=blk, send_sem=send_sem, recv_sem=recv_sem,
                                        device_id=to, device_id_type=MESH)


def _gather_weights(bufs, tiny):
    nt = len(bufs)
    n_ici = max(nt * 3, 1)

    def body(*refs):
        tiny_ref = refs[nt]
        o_refs, tiny_o = refs[nt + 1:2 * nt + 1], refs[2 * nt + 1]
        send, recv, fsend, frecv, tsend, trecv, lsem = refs[2 * nt + 2:]
        x, y, c, chip = _place()
        local = pltpu.make_async_copy(tiny_ref, tiny_o.at[chip], lsem)
        local.start()

        def tin(d, origin_chip, to):
            return pltpu.make_async_remote_copy(
                src_ref=tiny_ref, dst_ref=tiny_o.at[origin_chip], send_sem=tsend.at[d - 1], recv_sem=trecv.at[d - 1],
                device_id=to, device_id_type=MESH)

        sends = []
        for t in range(nt):
            for d in (1, 2, 3):
                px, py, _ = _chip_at(x, y, d)
                sends.append(_half_block(o_refs[t], chip, c, (px, py, c), send.at[3 * t + d - 1], recv.at[3 * t + d - 1]))
        for d in (1, 2, 3):
            px, py, _ = _chip_at(x, y, d)
            sends.append(tin(d, chip, (px, py, c)))
        for cp in sends:
            cp.start()
        passed = []
        for t in range(nt):
            for d in (1, 2, 3):
                k = 3 * t + d - 1
                _, _, pchip = _chip_at(x, y, d)
                _half_block(o_refs[t], pchip, c, (x, y, c), send.at[k], recv.at[k]).wait_recv()
                f = _half_block(o_refs[t], pchip, c, (x, y, 1 - c), fsend.at[k], frecv.at[k])
                f.start()
                passed.append(f)
        for t in range(nt):
            for d in (1, 2, 3):
                k = 3 * t + d - 1
                _, _, pchip = _chip_at(x, y, d)
                _half_block(o_refs[t], pchip, 1 - c, (x, y, 1 - c), fsend.at[k], frecv.at[k]).wait_recv()
        for d in (1, 2, 3):
            _, _, pchip = _chip_at(x, y, d)
            tin(d, pchip, (x, y, c)).wait_recv()
        for cp in sends + passed:
            cp.wait_send()
        local.wait()

    out_shape = [jax.ShapeDtypeStruct(b.shape, b.dtype) for b in bufs]
    out_shape.append(jax.ShapeDtypeStruct((4,) + tiny.shape, tiny.dtype))
    outs = pl.pallas_call(
        body, name="gather_weights_0", out_shape=out_shape,
        in_specs=[ANY] * (nt + 1), out_specs=[ANY] * (nt + 1),
        scratch_shapes=[pltpu.SemaphoreType.DMA((n_ici,)), pltpu.SemaphoreType.DMA((n_ici,)),
                        pltpu.SemaphoreType.DMA((n_ici,)), pltpu.SemaphoreType.DMA((n_ici,)),
                        pltpu.SemaphoreType.DMA((3,)), pltpu.SemaphoreType.DMA((3,)), pltpu.SemaphoreType.DMA],
        input_output_aliases={t: t for t in range(nt)},
        compiler_params=_cp(has_side_effects=True),
    )(*bufs, tiny)
    return outs[:nt], outs[nt]


def _gather_start(bufs, tag, after=()):
    nt, na = len(bufs), len(after)

    def body(*refs):
        b_refs = refs[:nt]
        send, recv = refs[nt + na], refs[nt + na + 1]
        token = refs[2 * nt + na + 2]
        x, y, c, chip = _place()
        for t in range(nt):
            for d in (1, 2, 3):
                px, py, _ = _chip_at(x, y, d)
                _half_block(b_refs[t], chip, c, (px, py, c), send.at[3 * t + d - 1], recv.at[3 * t + d - 1]).start()
        token[...] = jnp.zeros_like(token)

    outs = pl.pallas_call(
        body, name=f"gather_start_{tag}",
        out_shape=(pltpu.SemaphoreType.DMA((3 * nt,)), pltpu.SemaphoreType.DMA((3 * nt,)),
                   *[pltpu.HBM(b.shape, b.dtype) for b in bufs], jax.ShapeDtypeStruct((8, 128), F32)),
        in_specs=[HBM] * nt + [ANY] * na, out_specs=(SEM, SEM, *[HBM] * nt, pl.BlockSpec(memory_space=pltpu.VMEM)),
        input_output_aliases={t: 2 + t for t in range(nt)},
        compiler_params=pltpu.CompilerParams(has_side_effects=DATAFLOW),
    )(*[_in_hbm(b) for b in bufs], *after)
    return outs[0], outs[1], list(outs[2:2 + nt]), outs[2 + nt]


def _gather_wait(send, recv, bufs, after, tag):
    nt = len(bufs)

    def body(*refs):
        b_refs = refs[:nt]
        send_ref, recv_ref = refs[nt], refs[nt + 1]
        x, y, c, chip = _place()
        for t in range(nt):
            for d in (1, 2, 3):
                k = 3 * t + d - 1
                px, py, pchip = _chip_at(x, y, d)
                _half_block(b_refs[t], chip, c, (px, py, c), send_ref.at[k], recv_ref.at[k]).wait_send()
                _half_block(b_refs[t], pchip, c, (px, py, c), send_ref.at[k], recv_ref.at[k]).wait_recv()

    outs = pl.pallas_call(
        body, name=f"gather_wait_{tag}", out_shape=[pltpu.HBM(b.shape, b.dtype) for b in bufs],
        in_specs=[HBM] * nt + [SEM, SEM, ANY], out_specs=[HBM] * nt,
        input_output_aliases={t: t for t in range(nt)},
        compiler_params=pltpu.CompilerParams(has_side_effects=DATAFLOW),
    )(*bufs, send, recv, after)
    return list(outs)


def _gather_pass_on(bufs, tag):
    nt = len(bufs)

    def body(*refs):
        o_refs = refs[nt:2 * nt]
        fsend, frecv = refs[2 * nt:]
        x, y, c, _ = _place()
        cps = []
        for t in range(nt):
            for d in (1, 2, 3):
                k = 3 * t + d - 1
                _, _, pchip = _chip_at(x, y, d)
                cps.append(_half_block(o_refs[t], pchip, c, (x, y, 1 - c), fsend.at[k], frecv.at[k]))
        for cp in cps:
            cp.start()
        for t in range(nt):
            for d in (1, 2, 3):
                k = 3 * t + d - 1
                _, _, pchip = _chip_at(x, y, d)
                _half_block(o_refs[t], pchip, 1 - c, (x, y, 1 - c), fsend.at[k], frecv.at[k]).wait_recv()
        for cp in cps:
            cp.wait_send()

    return pl.pallas_call(
        body, name=f"gather_pass_on_{tag}", out_shape=[jax.ShapeDtypeStruct(b.shape, b.dtype) for b in bufs],
        in_specs=[ANY] * nt, out_specs=[ANY] * nt,
        scratch_shapes=[pltpu.SemaphoreType.DMA((3 * nt,)), pltpu.SemaphoreType.DMA((3 * nt,))],
        input_output_aliases={t: t for t in range(nt)},
        compiler_params=_cp(has_side_effects=True),
    )(*bufs)


def _to_sibling_halves(gs, l):
    nt = len(gs)

    def body(*refs):
        g_refs, o_refs = refs[:nt], refs[nt:2 * nt]
        send, recv = refs[2 * nt:]
        x, y, c, _ = _place()
        cps = [pltpu.make_async_remote_copy(
            src_ref=g_refs[t].at[k, 1 - c], dst_ref=o_refs[t].at[k], send_sem=send.at[4 * t + k], recv_sem=recv.at[4 * t + k],
            device_id=(x, y, 1 - c), device_id_type=MESH) for t in range(nt) for k in range(4)]
        for cp in cps:
            cp.start()
        for cp in cps:
            cp.wait()

    return pl.pallas_call(
        body, name=f"grads_to_sibling_{l}", out_shape=[jax.ShapeDtypeStruct((4,) + g.shape[2:], g.dtype) for g in gs],
        in_specs=[ANY] * nt, out_specs=[ANY] * nt,
        scratch_shapes=[pltpu.SemaphoreType.DMA((4 * nt,)), pltpu.SemaphoreType.DMA((4 * nt,))],
        compiler_params=_cp(has_side_effects=True),
    )(*gs)


def _chip_copy(c_ref, land_ref, x, y, c, d, send_sem, recv_sem):
    px, py, pchip = _chip_at(x, y, d)
    return pltpu.make_async_remote_copy(src_ref=c_ref.at[pchip], dst_ref=land_ref.at[d - 1], send_sem=send_sem, recv_sem=recv_sem,
                                        device_id=(px, py, c), device_id_type=MESH)


def _exchange_start(srcs, lands, copies, nsem, name):
    ns, n = len(srcs), len(srcs) + len(lands)

    def body(*refs):
        for cp in copies(refs[:ns], refs[ns:n], refs[n], refs[n + 1]):
            cp.start()
        token = refs[2 * n + 2]
        token[...] = jnp.zeros_like(token)

    outs = pl.pallas_call(
        body, name=name,
        out_shape=(pltpu.SemaphoreType.DMA((nsem,)), pltpu.SemaphoreType.DMA((nsem,)),
                   *[pltpu.HBM(a.shape, a.dtype) for a in list(srcs) + list(lands)], jax.ShapeDtypeStruct((8, 128), F32)),
        in_specs=[HBM] * n, out_specs=(SEM, SEM, *[HBM] * n, pl.BlockSpec(memory_space=pltpu.VMEM)),
        input_output_aliases={i: 2 + i for i in range(n)},
        compiler_params=pltpu.CompilerParams(has_side_effects=DATAFLOW),
    )(*[_in_hbm(a) for a in list(srcs) + list(lands)])
    return outs[0], outs[1], list(outs[2:2 + ns]), list(outs[2 + ns:2 + n]), outs[2 + n]


def _exchange_wait(send, recv, srcs, lands, after, copies, name):
    ns, n = len(srcs), len(srcs) + len(lands)

    def body(*refs):
        for cp in copies(refs[:ns], refs[ns:n], refs[n], refs[n + 1]):
            cp.wait_send()
            cp.wait_recv()

    outs = pl.pallas_call(
        body, name=name, out_shape=[pltpu.HBM(a.shape, a.dtype) for a in list(srcs) + list(lands)],
        in_specs=[HBM] * n + [SEM, SEM, ANY], out_specs=[HBM] * n,
        input_output_aliases={i: i for i in range(n)},
        compiler_params=pltpu.CompilerParams(has_side_effects=DATAFLOW),
    )(*srcs, *lands, send, recv, after)
    return list(outs[:ns]), list(outs[ns:])


def _chips_copies(c_refs, land_refs, send, recv):
    x, y, c, _ = _place()
    return [_chip_copy(c_refs[t], land_refs[t], x, y, c, d, send.at[3 * t + d - 1], recv.at[3 * t + d - 1])
            for t in range(len(c_refs)) for d in (1, 2, 3)]


def _sibling_copies(g_refs, land_refs, send, recv):
    x, y, c, _ = _place()
    return [pltpu.make_async_remote_copy(
        src_ref=g_refs[t].at[k, 1 - c], dst_ref=land_refs[t].at[k], send_sem=send.at[4 * t + k], recv_sem=recv.at[4 * t + k],
        device_id=(x, y, 1 - c), device_id_type=MESH) for t in range(len(g_refs)) for k in range(4)]


def _join_halves(fs, l):
    nt = len(fs)

    def body(*refs):
        o_refs = refs[nt:2 * nt]
        send, recv = refs[2 * nt:]
        x, y, c, _ = _place()
        cps = [pltpu.make_async_remote_copy(
            src_ref=o_refs[t].at[c], dst_ref=o_refs[t].at[c], send_sem=send.at[t], recv_sem=recv.at[t],
            device_id=(x, y, 1 - c), device_id_type=MESH) for t in range(nt)]
        for cp in cps:
            cp.start()
        for cp in cps:
            cp.wait()

    return pl.pallas_call(
        body, name=f"grads_join_{l}", out_shape=[jax.ShapeDtypeStruct(a.shape, a.dtype) for a in fs],
        in_specs=[ANY] * nt, out_specs=[ANY] * nt,
        scratch_shapes=[pltpu.SemaphoreType.DMA((nt,)), pltpu.SemaphoreType.DMA((nt,))],
        input_output_aliases={t: t for t in range(nt)},
        compiler_params=_cp(has_side_effects=True),
    )(*fs)


def _add_half(g, r, c_arr, name):
    _, _, rh, cols = g.shape

    def body(c_ref, g_ref, r_ref, o_ref):
        o_ref[...] = (g_ref[...] + r_ref[...]).astype(BF16)

    blk = pl.BlockSpec((None, rh, cols), lambda k, cr: (k, 0, 0))
    return pl.pallas_call(
        body, name=name, out_shape=jax.ShapeDtypeStruct((4, rh, cols), BF16),
        grid_spec=pltpu.PrefetchScalarGridSpec(
            num_scalar_prefetch=1, grid=(4,),
            in_specs=[pl.BlockSpec((None, None, rh, cols), lambda k, cr: (k, cr[0], 0, 0)), blk], out_specs=blk),
        compiler_params=_cp(("parallel",)),
    )(c_arr, g, r)


def _sum_chips(cs, r3, place_arr, name):
    _, rh, cols = cs.shape
    rb = rh // 2

    def body(pl_ref, a_ref, r0_ref, r1_ref, r2_ref, o_ref):
        up = lambda ref: ref[...].astype(F32)
        o_ref[...] = ((up(a_ref) + up(r0_ref)) + up(r1_ref)) + up(r2_ref)

    def slot(d):
        return pl.BlockSpec((None, rb, cols), lambda i, pa: (d, i, 0))

    return pl.pallas_call(
        body, name=name, out_shape=jax.ShapeDtypeStruct((2, rh, cols), F32),
        grid_spec=pltpu.PrefetchScalarGridSpec(
            num_scalar_prefetch=1, grid=(2,),
            in_specs=[pl.BlockSpec((None, rb, cols), lambda i, pa: (pa[0], i, 0)), slot(0), slot(1), slot(2)],
            out_specs=pl.BlockSpec((None, rb, cols), lambda i, pa: (pa[1], i, 0))),
        compiler_params=_cp(("parallel",)),
    )(place_arr, cs, r3, r3, r3)


def _allreduce_small(pack):
    rows = pack.shape[0]
    hr = rows // 2

    def body(p_ref, o_ref, sib, slots, s1, r1, s2, r2, s3, r3):
        x, y, c, chip = _place()
        sibling = (x, y, 1 - c)
        ex = pltpu.make_async_remote_copy(src_ref=p_ref, dst_ref=sib, send_sem=s1, recv_sem=r1,
                                          device_id=sibling, device_id_type=MESH)
        ex.start()
        ex.wait()
        half = pl.ds(pl.multiple_of(c * hr, 16), hr)
        slots[0] = (p_ref[half, :] + sib[half, :]).astype(BF16)
        cps = []
        for d in (1, 2, 3):
            px, py, _ = _chip_at(x, y, d)
            cps.append(pltpu.make_async_remote_copy(
                src_ref=slots.at[0], dst_ref=slots.at[d], send_sem=s2.at[d - 1], recv_sem=r2.at[d - 1],
                device_id=(px, py, c), device_id_type=MESH))
        for cp in cps:
            cp.start()
        for cp in cps:
            cp.wait()
        tot = slots[chip].astype(F32)
        for k in (1, 2, 3):
            tot = tot + slots[jnp.bitwise_xor(chip, k)].astype(F32)
        o_ref[half, :] = tot
        back = pltpu.make_async_remote_copy(src_ref=o_ref.at[half, :], dst_ref=o_ref.at[half, :], send_sem=s3, recv_sem=r3,
                                            device_id=sibling, device_id_type=MESH)
        back.start()
        back.wait()

    vm = pl.BlockSpec(memory_space=pltpu.VMEM)
    return pl.pallas_call(
        body, name="allreduce_small", out_shape=jax.ShapeDtypeStruct((rows, 128), F32),
        in_specs=[vm], out_specs=vm,
        scratch_shapes=[pltpu.VMEM((rows, 128), F32), pltpu.VMEM((4, hr, 128), BF16),
                        pltpu.SemaphoreType.DMA, pltpu.SemaphoreType.DMA, pltpu.SemaphoreType.DMA((3,)), pltpu.SemaphoreType.DMA((3,)),
                        pltpu.SemaphoreType.DMA, pltpu.SemaphoreType.DMA],
        compiler_params=_cp(has_side_effects=True),
    )(pack)


def _adam_math(gv, wv, mv, vv):
    m2 = ADAM_B1 * mv + (1.0 - ADAM_B1) * gv
    v2 = ADAM_B2 * vv + (1.0 - ADAM_B2) * (gv * gv)
    m_hat = m2 / (1.0 - ADAM_B1 ** ADAM_STEP)
    v_hat = v2 / (1.0 - ADAM_B2 ** ADAM_STEP)
    return -ADAM_LR * (m_hat / (jnp.sqrt(v_hat) + ADAM_EPS) + ADAM_WD * wv), m2, v2


def _adam(g, w, m, v, name):
    rows, cols = g.shape
    rb = rows // 4

    def body(g_ref, w_ref, m_ref, v_ref, d_ref, m2_ref, v2_ref):
        d_ref[...], m2_ref[...], v2_ref[...] = _adam_math(g_ref[...], w_ref[...], m_ref[...], v_ref[...])

    blk = pl.BlockSpec((rb, cols), lambda i: (i, 0))
    shp = jax.ShapeDtypeStruct((rows, cols), F32)
    return pl.pallas_call(
        body, name=name, grid=(4,), in_specs=[blk] * 4, out_specs=[blk] * 3, out_shape=[shp] * 3,
        compiler_params=_cp(("parallel",)),
    )(g, w, m, v)


def _adam_layer(g, w, m, v, l, prev, name):
    rows, cols = g.shape
    rb = rows // 4

    def body(g_ref, w_ref, m_ref, v_ref, *rest):
        go_ref, d_ref, m2_ref, v2_ref = rest[-4:]
        gv = g_ref[...]
        go_ref[...] = gv
        d_ref[...], m2_ref[...], v2_ref[...] = _adam_math(gv, w_ref[...], m_ref[...], v_ref[...])

    lay = pl.BlockSpec((None, rb, cols), lambda i: (l, i, 0))
    shp = jax.ShapeDtypeStruct((2, rows, cols), F32)
    prev = () if prev is None else tuple(prev)
    return pl.pallas_call(
        body, name=name, grid=(4,), in_specs=[pl.BlockSpec((rb, cols), lambda i: (i, 0)), lay, lay, lay] + [ANY] * len(prev),
        out_specs=[lay] * 4, out_shape=[shp] * 4,
        input_output_aliases={4 + j: j for j in range(len(prev))},
        compiler_params=_cp(("parallel",)),
    )(g, w, m, v, *prev)


def _rows128(a):
    return a.reshape(-1, 128)


def _pack(arrs, mult):
    parts = [_rows128(a) for a in arrs]
    rows = sum(q.shape[0] for q in parts)
    pad = -rows % mult
    if pad:
        parts.append(jnp.zeros((pad, 128), F32))
    return jnp.concatenate(parts, axis=0)


def _unpack(pack, shapes):
    out, o = [], 0
    for s in shapes:
        n = 1
        for e in s:
            n *= e
        out.append(pack[o:o + n // 128].reshape(s))
        o += n // 128
    return out


WEIGHTS = ['norm1_g', 'w_in', 'gmlp_ln_g', 'gmlp_ln_b', 'gmlp_w_s', 'gmlp_b_s', 'conv_w', 'conv_b', 'lru_w_r', 'lru_b_r', 'lru_w_i',
           'lru_b_i', 'lru_lambda', 'w_out', 'norm2_g', 'w_ffn_in', 'w_ffn_out', 'final_g']
BIG = ['w_in', 'w_out', 'w_ffn_in', 'w_ffn_out']
SMALL = [n for n in WEIGHTS if n not in BIG]
CHIP_SHARDED_SMALL = ['conv_w', 'lru_b_r', 'lru_b_i', 'lru_lambda']


def kernel(x, norm1_g, w_in, gmlp_ln_g, gmlp_ln_b, gmlp_w_s, gmlp_b_s, conv_w, conv_b, lru_w_r, lru_b_r, lru_w_i, lru_b_i, lru_lambda, w_out, norm2_g, w_ffn_in, w_ffn_out, final_g, loss_target, m_norm1_g, m_w_in, m_gmlp_ln_g, m_gmlp_ln_b, m_gmlp_w_s, m_gmlp_b_s, m_conv_w, m_conv_b, m_lru_w_r, m_lru_b_r, m_lru_w_i, m_lru_b_i, m_lru_lambda, m_w_out, m_norm2_g, m_w_ffn_in, m_w_ffn_out, m_final_g, v_norm1_g, v_w_in, v_gmlp_ln_g, v_gmlp_ln_b, v_gmlp_w_s, v_gmlp_b_s, v_conv_w, v_conv_b, v_lru_w_r, v_lru_b_r, v_lru_w_i, v_lru_b_i, v_lru_lambda, v_w_out, v_norm2_g, v_w_ffn_in, v_w_ffn_out, v_final_g):
    a = dict(locals())
    w = {n: a[n] for n in WEIGHTS}
    mom = {n: a["m_" + n] for n in WEIGHTS}
    var = {n: a["v_" + n] for n in WEIGHTS}
    _, _, c, chip = _place()
    c_arr, chip_arr = jnp.reshape(c, (1,)).astype(jnp.int32), jnp.reshape(chip, (1,)).astype(jnp.int32)
    place_arr = jnp.stack([chip, c]).astype(jnp.int32)

    first, rest = BIG[:1], BIG[1:]

    def as_weights(names, full):
        wb = {n: f.reshape(4, 2 * f.shape[2], f.shape[3]) for n, f in zip(names, full)}
        if "w_out" in wb:
            wb["w_out"] = wb["w_out"].reshape(D, D)
            wb["w_ffn_out"] = wb["w_ffn_out"].reshape(DFF, D)
        return wb

    def cast(n, l):
        return _cast_into(w[n], l, chip_arr, f"cast_{n}_{l}")

    def landed(fly, names, after, tag):
        return as_weights(names, _gather_pass_on(_gather_wait(fly[0], fly[1], fly[2], after, tag), tag))

    tiny = _pack([w[n] for n in CHIP_SHARDED_SMALL], 8)
    _, tiny_full = _gather_weights([], tiny)
    fly_in = _gather_start([cast("w_in", 0)], "in", after=(tiny_full,))
    fly0 = _gather_start([cast(n, 0) for n in rest], "0", after=(fly_in[3],))
    bufs1 = [cast(n, 1) for n in BIG]
    fly1 = _gather_start(bufs1, "1", after=(fly0[3],))
    p = {n: w[n] for n in SMALL}
    parts = [_unpack(tiny_full[k], [w[n].shape for n in CHIP_SHARDED_SMALL]) for k in range(4)]
    for i, n in enumerate(CHIP_SHARDED_SMALL):
        p[n] = jnp.concatenate([parts[k][i] for k in range(4)], axis=-1)

    xa, saved0 = _forward_layer(0, x[0], p, landed(fly_in, first, fly1[3], "in"), after=(fly0[3], fly1[3]),
                                rest=lambda merged: landed(fly0, rest, merged, "0"))
    xb, saved1 = _forward_layer(1, xa, p, landed(fly1, BIG, xa, "1"))
    dxb, loss_v, dfg = _loss_head(xb, loss_target[0], p["final_g"][None])
    loss = lax.psum(loss_v[0, 0], ("x", "y", "c"))

    out, flying = {}, {}

    def halves(grads):
        return [g.reshape(4, 2, -1, g.shape[-1]) for g in grads]

    def sibling_start(grads, names, l, tag):
        gs = halves(grads)
        lands = [lax.empty((4,) + g.shape[2:], g.dtype) for g in gs]
        flying["s" + tag] = (names, l) + tuple(
            _exchange_start(gs, lands, _sibling_copies, 4 * len(gs), f"grads_to_sibling_start_{tag}"))
        return (flying["s" + tag][-1],)

    def chips_start(gs, from_sib, names, l, tag):
        cs = [_add_half(g, r, c_arr, f"add_half_{n}_{l}") for n, g, r in zip(names, gs, from_sib)]
        lands = [lax.empty((3,) + a.shape[1:], a.dtype) for a in cs]
        flying[tag] = (names, l) + tuple(_exchange_start(cs, lands, _chips_copies, 3 * len(cs), f"grads_to_chips_start_{tag}"))
        return (flying[tag][-1],)

    def sibling_finish(tag, after):
        names, l, send, recv, gs, lands, _ = flying["s" + tag]
        gs, from_sib = _exchange_wait(send, recv, gs, lands, after, _sibling_copies, f"grads_to_sibling_wait_{tag}")
        return chips_start(gs, from_sib, names, l, tag)

    def reduce_start(grads, names, l, tag):
        gs = halves(grads)
        return chips_start(gs, _to_sibling_halves(gs, tag), names, l, tag)

    def reduce_finish(tag, after):
        names, l, send, recv, cs, lands, _ = flying[tag]
        cs, lands = _exchange_wait(send, recv, cs, lands, after, _chips_copies, f"grads_to_chips_wait_{tag}")
        ts = [_sum_chips(cc, r3, place_arr, f"sum_chips_{n}_{l}") for n, cc, r3 in zip(names, cs, lands)]
        for n, j in zip(names, _join_halves(ts, tag)):
            out[n] = _adam_layer(j.reshape(w[n].shape[1:]), w[n], mom[n], var[n], l, out.get(n), f"adam_{n}_{l}")
        return out[names[-1]][0]

    def late1(grads):
        return sibling_finish("1a", grads[0]) + sibling_start(grads, first, 1, "1b")

    def midway0(grads):
        reduce_finish("1a", grads[0])
        reduce_finish("1b", grads[0])
        return reduce_start(grads, rest, 0, "0a")

    dxa, big1, small1 = _backward_layer(1, dxb, saved1, midway=lambda grads: sibling_start(grads, rest, 1, "1a"), late=late1)
    dx, big0, small0 = _backward_layer(0, dxa, saved0, after=sibling_finish("1b", dxa), midway=midway0,
                                       late=lambda grads: reduce_start(grads, first, 0, "0b"))
    reduce_finish("0b", reduce_finish("0a", dx))
    small = {k: jnp.stack([small0[k], small1[k]]) for k in LAYER_SMALL}
    small["final_g"] = dfg[0]

    full_shapes = [small[n].shape for n in SMALL]
    red = _unpack(_allreduce_small(_pack([small[n] for n in SMALL], 32)), full_shapes)
    g_small = []
    for n, g in zip(SMALL, red):
        if n in CHIP_SHARDED_SMALL:
            g = lax.dynamic_slice_in_dim(g, chip * w[n].shape[-1], w[n].shape[-1], axis=g.ndim - 1)
        g_small.append(g)
    shapes = [w[n].shape for n in SMALL]
    packs = [_pack(lst, 32) for lst in (g_small, [w[n] for n in SMALL], [mom[n] for n in SMALL], [var[n] for n in SMALL])]
    upd = [_unpack(u, shapes) for u in _adam(*packs, "adam_small")]
    for i, n in enumerate(SMALL):
        out[n] = [g_small[i], upd[0][i], upd[1][i], upd[2][i]]

    return (loss, dx[None]) + tuple(out[n][i] for i in range(4) for n in WEIGHTS)
```

```python
import functools

import jax
import jax.numpy as jnp
from jax import lax
from jax.experimental import pallas as pl
from jax.experimental.pallas import tpu as pltpu

F32 = jnp.float32
BF16 = jnp.bfloat16
MESH = pl.DeviceIdType.MESH

D = 1024
NH = 8
HD = 128
CHUNK = 128
N_IN_T = 12
DFF = 2816
DFF_SH = 1408
EPS = 1e-6
LRU_C = 8.0
ADAM_LR, ADAM_B1, ADAM_B2, ADAM_EPS, ADAM_WD, ADAM_STEP = 0.001, 0.9, 0.999, 1e-08, 0.01, 10

TM = 512
TM_BIG = 1024
RT = 128
PADR = 8
VMEM_LIMIT = 56 * 1024 * 1024


def _cp(sem=None, **kw):
    if sem is not None:
        kw["dimension_semantics"] = sem
    return pltpu.CompilerParams(vmem_limit_bytes=VMEM_LIMIT, **kw)


_GC = 0.7978845608028654


def _sigmoid(x):
    return 1.0 / (1.0 + jnp.exp(-x))


def _gelu(x):
    return 0.5 * x * (1.0 + jnp.tanh(_GC * (x + 0.044715 * x * x * x)))


def _gelu_and_grad(x):
    t = jnp.tanh(_GC * (x + 0.044715 * x * x * x))
    g = 0.5 * x * (1.0 + t)
    dg = 0.5 * (1.0 + t) + 0.5 * x * (1.0 - t * t) * _GC * (1.0 + 3 * 0.044715 * x * x)
    return g, dg


def _softplus_neg(lam):
    y = jnp.exp(-jnp.abs(lam))
    u = 1.0 + y
    l1p = jnp.where(u == 1.0, y, jnp.log(u) * y / (u - 1.0))
    return jnp.maximum(-lam, 0.0) + l1p


def _dot(a, b):
    return jnp.dot(a, b, preferred_element_type=F32)


def _dot_nt(a, b):
    return lax.dot_general(a, b, (((1,), (1,)), ((), ())), preferred_element_type=F32)


def _dot_tn(a, b):
    return lax.dot_general(a, b, (((0,), (0,)), ((), ())), preferred_element_type=F32)


def _rms_hat(x):
    r = lax.rsqrt(jnp.mean(x * x, axis=-1, keepdims=True) + EPS)
    return x * r, r


def _rms_bwd(dh, x, g):
    xh, r = _rms_hat(x)
    dxh = dh * g
    dx = r * (dxh - xh * jnp.mean(dxh * xh, axis=-1, keepdims=True))
    return dx, jnp.sum(dh * xh, axis=0, keepdims=True)


def _norm_into(x_ref, g_ref, h_ref):
    xh, _ = _rms_hat(x_ref[...])
    h_ref[...] = (xh * g_ref[...]).astype(BF16)


def _in_tile(j):
    m, hf = j // 2, j % 2
    orig = jnp.where(m < 2, m, jnp.where(m == 2, 4, jnp.where(m < 5, m - 1, 5)))
    t = orig * 2 + hf
    return t // 3, t % 3


ANY = pl.BlockSpec(memory_space=pl.ANY)


def _mm_in(x, g, w_in, l, after=()):
    S = x.shape[0]
    tm = min(2 * TM_BIG, S)

    def body(x_ref, g_ref, w0_ref, w1_ref, *rest):
        o_ref, h_ref = rest[-2:]

        @pl.when(pl.program_id(1) == 0)
        def _():
            _norm_into(x_ref, g_ref, h_ref)
        hv = h_ref[...]
        o_ref[:, 0:512] = _dot(hv, w0_ref[...]).astype(BF16)
        o_ref[:, 512:1024] = _dot(hv, w1_ref[...]).astype(BF16)

    def w_tile(hf):
        def w_map(i, m):
            sh, tl = _in_tile(2 * m + hf)
            return (sh, 0, tl)
        return pl.BlockSpec((None, D, 512), w_map)

    return pl.pallas_call(
        body, name=f"mm_in_{l}", grid=(S // tm, 6),
        in_specs=[pl.BlockSpec((tm, D), lambda i, m: (i, 0)), pl.BlockSpec((1, D), lambda i, m: (0, 0)),
                  w_tile(0), w_tile(1)] + [ANY] * len(after),
        out_specs=[pl.BlockSpec((None, tm, D), lambda i, m: (m, i, 0)), pl.BlockSpec((tm, D), lambda i, m: (i, 0))],
        out_shape=[jax.ShapeDtypeStruct((6, S, D), BF16), jax.ShapeDtypeStruct((S, D), BF16)],
        compiler_params=_cp(("parallel", "arbitrary")),
    )(x, g, w_in, w_in, *after)


def _mm_res(a, w, res, l, name):
    S, K = a.shape

    def body(a_ref, w_ref, r_ref, o_ref):
        o_ref[...] = r_ref[...] + _dot(a_ref[...], w_ref[...])

    tm = TM
    return pl.pallas_call(
        body, name=f"{name}_{l}", grid=(S // tm,),
        in_specs=[pl.BlockSpec((tm, K), lambda i: (i, 0)), pl.BlockSpec((K, D), lambda i: (0, 0)),
                  pl.BlockSpec((tm, D), lambda i: (i, 0))],
        out_specs=pl.BlockSpec((tm, D), lambda i: (i, 0)),
        out_shape=jax.ShapeDtypeStruct((S, D), F32),
        compiler_params=_cp(("parallel",)),
    )(a, w, res)


def _mm_ffn_in(x, g, w_fi, l):
    S = x.shape[0]

    def body(x_ref, g_ref, w_ref, gu_ref, ff_ref, h_ref):
        @pl.when(pl.program_id(1) == 0)
        def _():
            _norm_into(x_ref, g_ref, h_ref)
        hv = h_ref[...]
        ga = _dot(hv, w_ref[0])
        gb = _dot(hv, w_ref[1])
        gu_ref[0] = ga.astype(BF16)
        gu_ref[1] = gb.astype(BF16)
        ff_ref[...] = (ga * _sigmoid(ga) * gb).astype(BF16)

    gu, ff, h = pl.pallas_call(
        body, name=f"mm_ffn_in_{l}", grid=(S // TM, 2),
        in_specs=[pl.BlockSpec((TM, D), lambda i, s: (i, 0)), pl.BlockSpec((1, D), lambda i, s: (0, 0)),
                  pl.BlockSpec((2, None, D, DFF_SH), lambda i, s: (0, s, 0, 0))],
        out_specs=[pl.BlockSpec((2, None, TM, DFF_SH), lambda i, s: (0, s, i, 0)),
                   pl.BlockSpec((TM, DFF_SH), lambda i, s: (i, s)),
                   pl.BlockSpec((TM, D), lambda i, s: (i, 0))],
        out_shape=[jax.ShapeDtypeStruct((2, 2, S, DFF_SH), BF16), jax.ShapeDtypeStruct((S, DFF), BF16),
                   jax.ShapeDtypeStruct((S, D), BF16)],
        compiler_params=_cp(("parallel", "arbitrary")),
    )(x, g, w_fi.reshape(2, 2, D, DFF_SH))
    return gu.reshape(4, S, DFF_SH), ff, h


def _gmlp_fwd(z6, ws_b, bs_b, lg, lb):
    S = z6.shape[1]

    def body(z_ref, ws_ref, bs_ref, lg_ref, lb_ref, o_ref, mix):
        gv = _gelu(z_ref[1].astype(F32))
        xc = gv - jnp.mean(gv, axis=-1, keepdims=True)
        rs = lax.rsqrt(jnp.mean(xc * xc, axis=-1, keepdims=True) + EPS)
        vb = (xc * rs * lg_ref[...] + lb_ref[...]).astype(BF16)
        for gi in range(NH):
            cs = slice(gi * HD, (gi + 1) * HD)
            mix[:, cs] = _dot(ws_ref[gi], vb[:, cs])
        o_ref[...] = (_sigmoid(z_ref[2].astype(F32)) * _gelu(z_ref[0].astype(F32)) * (mix[...] + bs_ref[...])).astype(BF16)

    return pl.pallas_call(
        body, name="gmlp_fwd", grid=(S // CHUNK,),
        in_specs=[pl.BlockSpec((3, CHUNK, D), lambda i: (0, i, 0)), pl.BlockSpec((NH, CHUNK, CHUNK), lambda i: (0, 0, 0)),
                  pl.BlockSpec((CHUNK, D), lambda i: (0, 0)), pl.BlockSpec((1, D), lambda i: (0, 0)),
                  pl.BlockSpec((1, D), lambda i: (0, 0))],
        out_specs=pl.BlockSpec((CHUNK, D), lambda i: (i, 0)),
        out_shape=jax.ShapeDtypeStruct((S, D), BF16),
        scratch_shapes=[pltpu.VMEM((CHUNK, D), F32)],
        compiler_params=_cp(("parallel",)),
    )(z6, ws_b, bs_b, lg, lb)


def _row_iota():
    return lax.broadcasted_iota(jnp.int32, (RT, HD), 0)


SUB = 8
UNROLL = 4
GRAD_ROWS = 256


def _scan_up(a, b, carry):
    row = lax.broadcasted_iota(jnp.int32, (SUB, HD), 0)
    masks = [(d, row >= d) for d in (1, 2, 4)]
    c = jnp.broadcast_to(carry, (SUB, HD))
    hs = []
    for j in range(RT // SUB):
        aj, bj = a[SUB * j:SUB * (j + 1)], b[SUB * j:SUB * (j + 1)]
        for d, m in masks:
            bj = bj + aj * jnp.where(m, pltpu.roll(bj, d, 0), 0.0)
            aj = aj * jnp.where(m, pltpu.roll(aj, d, 0), 1.0)
        h = bj + aj * c
        hs.append(h)
        c = jnp.broadcast_to(h[SUB - 1:SUB, :], (SUB, HD))
    return jnp.concatenate(hs, axis=0), hs[-1][SUB - 1:SUB, :]


def _scan_down(a, b, carry):
    row = lax.broadcasted_iota(jnp.int32, (SUB, HD), 0)
    masks = [(d, row < SUB - d) for d in (1, 2, 4)]
    c = jnp.broadcast_to(carry, (SUB, HD))
    hs = []
    for j in reversed(range(RT // SUB)):
        aj, bj = a[SUB * j:SUB * (j + 1)], b[SUB * j:SUB * (j + 1)]
        for d, m in masks:
            bj = bj + aj * jnp.where(m, pltpu.roll(bj, SUB - d, 0), 0.0)
            aj = aj * jnp.where(m, pltpu.roll(aj, SUB - d, 0), 1.0)
        h = bj + aj * c
        hs.append(h)
        c = jnp.broadcast_to(h[0:1, :], (SUB, HD))
    return jnp.concatenate(hs[::-1], axis=0), hs[-1][0:1, :]


def _decay(r, sp_d):
    log_a = -LRU_C * r * sp_d
    a = jnp.exp(log_a)
    return a, jnp.sqrt(jnp.maximum(-jnp.tanh(log_a) * (a * a + 1.0), 0.0))


def _lru_gates(xc, d, wr_ref, br_ref, wi_ref, bi_ref, sp):
    xb = xc.astype(BF16)
    r = _sigmoid(_dot(xb, wr_ref[d]) + br_ref[d:d + 1, :])
    i = _sigmoid(_dot(xb, wi_ref[d]) + bi_ref[d:d + 1, :])
    a, mult = _decay(r, sp[d:d + 1, :])
    return r, i, a, mult


def _shifted(win, k):
    w = RT + 2 * PADR
    v = win if k == 0 else pltpu.roll(win, (-k) % w, 0)
    return v[PADR:PADR + RT]


def _conv_taps(win):
    return [_shifted(win, k) for k in (-1, 0, 1, 2)]


def _fill_padded(dst, src_ref, S):
    zeros = jnp.zeros((PADR, HD), F32)
    dst[0:PADR, :] = zeros
    dst[PADR + S:2 * PADR + S, :] = zeros

    def cp(i, c):
        t0 = pl.multiple_of(i * RT, RT)
        dst[pl.ds(t0 + PADR, RT), :] = src_ref[pl.ds(t0, RT), :].astype(F32)
        return c
    lax.fori_loop(0, S // RT, cp, 0)


def _conv_fwd_all(zxp, xc_s, cw_ref, cb_ref, S):
    def cv(i, c):
        t0 = pl.multiple_of(i * RT, RT)
        xm1, x0, xp1, xp2 = _conv_taps(zxp[pl.ds(t0, RT + 2 * PADR), :])
        xc_s[pl.ds(t0, RT), :] = (cb_ref[...] + xm1 * cw_ref[0:1, :] + x0 * cw_ref[1:2, :]
                                  + xp1 * cw_ref[2:3, :] + xp2 * cw_ref[3:4, :])
        return c
    lax.fori_loop(0, S // RT, cv, 0)


def _lru_specs(S):
    head = lambda h: (0, h)
    return [pl.BlockSpec((4, HD), head), pl.BlockSpec((1, HD), head),
            pl.BlockSpec((2, None, HD, HD), lambda h: (0, h, 0, 0)), pl.BlockSpec((2, HD), head),
            pl.BlockSpec((2, None, HD, HD), lambda h: (0, h, 0, 0)), pl.BlockSpec((2, HD), head),
            pl.BlockSpec((2, HD), head)]


def _lru_fwd(z6, ya, cw, cb, wr, br, wi, bi, lam):
    S = z6.shape[1]
    nt = S // RT

    def body(z_ref, ya_ref, cw_ref, cb_ref, wr_ref, br_ref, wi_ref, bi_ref, lam_ref, mg_ref, h0_ref, h1_ref, zxp, xc_s):
        sp = _softplus_neg(lam_ref[...])
        _fill_padded(zxp, z_ref.at[0], S)
        _conv_fwd_all(zxp, xc_s, cw_ref, cb_ref, S)

        def scans(i, carry):
            cu, cd = carry
            for u in range(UNROLL):
                j = i * UNROLL + u
                ru = pl.ds(pl.multiple_of(j * RT, RT), RT)
                rd = pl.ds(pl.multiple_of((nt - 1 - j) * RT, RT), RT)
                xu, xd = xc_s[ru, :], xc_s[rd, :]
                _, gi, a, mult = _lru_gates(xu, 0, wr_ref, br_ref, wi_ref, bi_ref, sp)
                hu, cu = _scan_up(a, mult * gi * xu, cu)
                h0_ref[ru, :] = hu
                _, gi, a, mult = _lru_gates(xd, 1, wr_ref, br_ref, wi_ref, bi_ref, sp)
                hd, cd = _scan_down(a, mult * gi * xd, cd)
                h1_ref[rd, :] = hd
            return cu, cd
        z1 = jnp.zeros((1, HD), F32)
        lax.fori_loop(0, nt // UNROLL, scans, (z1, z1))

        def merge(i, c):
            rows = pl.ds(pl.multiple_of(i * RT, RT), RT)
            yb = (h0_ref[rows, :] + h1_ref[rows, :]) * _gelu(z_ref[1, rows, :].astype(F32))
            mg_ref[rows, :] = (ya_ref[rows, :].astype(F32) + _sigmoid(z_ref[2, rows, :].astype(F32)) * yb).astype(BF16)
            return c
        lax.fori_loop(0, nt, merge, 0)

    col = pl.BlockSpec((S, HD), lambda h: (0, h))
    return pl.pallas_call(
        body, name="lru_fwd", grid=(NH,),
        in_specs=[pl.BlockSpec((3, S, HD), lambda h: (1, 0, h)), col] + _lru_specs(S),
        out_specs=[col, col, col],
        out_shape=[jax.ShapeDtypeStruct((S, D), BF16), jax.ShapeDtypeStruct((S, D), F32), jax.ShapeDtypeStruct((S, D), F32)],
        scratch_shapes=[pltpu.VMEM((S + 2 * PADR, HD), F32), pltpu.VMEM((S, HD), F32)],
        compiler_params=_cp(("parallel",)),
    )(z6, ya, cw, cb, wr, br, wi, bi, lam)


def _loss_head(x, tgt, g):
    S = x.shape[0]

    def body(x_ref, t_ref, g_ref, dx_ref, loss_ref, dg_ref):
        @pl.when(pl.program_id(0) == 0)
        def _():
            loss_ref[...] = jnp.zeros_like(loss_ref)
            dg_ref[...] = jnp.zeros_like(dg_ref)
        xv = x_ref[...]
        xh, _ = _rms_hat(xv)
        e = xh * g_ref[...] - t_ref[...]
        loss_ref[...] += jnp.sum(e * e) * (0.5 / D)
        dx, dgs = _rms_bwd(e * (1.0 / D), xv, g_ref[...])
        dx_ref[...] = dx
        dg_ref[...] += dgs

    return pl.pallas_call(
        body, name="loss_head", grid=(S // TM,),
        in_specs=[pl.BlockSpec((TM, D), lambda i: (i, 0)), pl.BlockSpec((TM, D), lambda i: (i, 0)),
                  pl.BlockSpec((1, D), lambda i: (0, 0))],
        out_specs=[pl.BlockSpec((TM, D), lambda i: (i, 0)), pl.BlockSpec((1, 128), lambda i: (0, 0)),
                   pl.BlockSpec((1, D), lambda i: (0, 0))],
        out_shape=[jax.ShapeDtypeStruct((S, D), F32), jax.ShapeDtypeStruct((1, 128), F32), jax.ShapeDtypeStruct((1, D), F32)],
        compiler_params=_cp(("arbitrary",)),
    )(x, tgt, g)


def _bwd_ffn_out(dx, w_fo, gu, l, after=()):
    S = dx.shape[0]

    tm = min(TM_BIG, S) if l else TM

    def body(dx_ref, w_ref, gu_ref, *rest):
        o_ref = rest[-1]
        for r0 in range(0, tm, TM):
            rows = slice(r0, r0 + TM)
            d = _dot_nt(dx_ref[rows, :].astype(BF16), w_ref[...])
            ga, gb = gu_ref[0, rows, :].astype(F32), gu_ref[1, rows, :].astype(F32)
            sg = _sigmoid(ga)
            o_ref[0, rows, :] = (d * gb * sg * (1.0 + ga * (1.0 - sg))).astype(BF16)
            o_ref[1, rows, :] = (d * ga * sg).astype(BF16)

    pair = pl.BlockSpec((2, None, tm, DFF_SH), lambda i, s: (0, s, i, 0))
    dgu = pl.pallas_call(
        body, name=f"bwd_ffn_out_{l}", grid=(S // tm, 2),
        in_specs=[pl.BlockSpec((tm, D), lambda i, s: (i, 0)), pl.BlockSpec((DFF_SH, D), lambda i, s: (s, 0)), pair]
        + [ANY] * len(after),
        out_specs=pair,
        out_shape=jax.ShapeDtypeStruct((2, 2, S, DFF_SH), BF16),
        compiler_params=_cp(("parallel", "arbitrary")),
    )(dx, w_fo, gu.reshape(2, 2, S, DFF_SH), *after)
    return dgu.reshape(4, S, DFF_SH)


def _mm_tn(a, b, m_blk, tk, name):
    S, M = a.shape

    def body(a_ref, b_ref, o_ref):
        @pl.when(pl.program_id(1) == 0)
        def _():
            o_ref[...] = jnp.zeros_like(o_ref)
        o_ref[...] += _dot_tn(a_ref[...], b_ref[...].astype(BF16))

    return pl.pallas_call(
        body, name=name, grid=(M // m_blk, S // tk),
        in_specs=[pl.BlockSpec((tk, m_blk), lambda m, k: (k, m)), pl.BlockSpec((tk, D), lambda m, k: (k, 0))],
        out_specs=pl.BlockSpec((m_blk, D), lambda m, k: (m, 0)),
        out_shape=jax.ShapeDtypeStruct((M, D), F32),
        compiler_params=_cp(("parallel", "arbitrary")),
    )(a, b)


def _mm_nt_rms_bwd(a, a_specs, w, w_specs, nk, tm, x, g, dres, name, after=()):
    S = x.shape[0]
    sub = len(a_specs)

    def body(*refs):
        a_refs, w_refs = refs[:sub], refs[sub:2 * sub]
        x_ref, g_ref, r_ref = refs[2 * sub:2 * sub + 3]
        dx_ref, dg_ref, acc = refs[-3:]
        i, k = pl.program_id(0), pl.program_id(1)
        part = _dot_nt(a_refs[0][...], w_refs[0][...])
        for j in range(1, sub):
            part = part + _dot_nt(a_refs[j][...], w_refs[j][...])

        @pl.when(k == 0)
        def _():
            acc[...] = part

        @pl.when(k > 0)
        def _():
            acc[...] += part

        @pl.when(jnp.logical_and(i == 0, k == 0))
        def _():
            dg_ref[...] = jnp.zeros_like(dg_ref)

        @pl.when(k == nk - 1)
        def _():
            dx, dgs = _rms_bwd(acc[...], x_ref[...], g_ref[...])
            dx_ref[...] = r_ref[...] + dx
            dg_ref[...] += dgs

    row = pl.BlockSpec((tm, D), lambda i, k: (i, 0))
    vec = pl.BlockSpec((1, D), lambda i, k: (0, 0))
    return pl.pallas_call(
        body, name=name, grid=(S // tm, nk),
        in_specs=list(a_specs) + list(w_specs) + [row, vec, row] + [ANY] * len(after),
        out_specs=[row, vec],
        out_shape=[jax.ShapeDtypeStruct((S, D), F32), jax.ShapeDtypeStruct((1, D), F32)],
        scratch_shapes=[pltpu.VMEM((tm, D), F32)],
        compiler_params=_cp(("arbitrary", "arbitrary")),
    )(*[a] * sub, *[w] * sub, x, g, dres, *after)


def _dw_ffn_in(h, dgu, l):
    S = h.shape[0]

    def body(h_ref, b_ref, o_ref):
        @pl.when(pl.program_id(1) == 0)
        def _():
            o_ref[...] = jnp.zeros_like(o_ref)
        o_ref[...] += _dot_tn(h_ref[...], b_ref[...])

    tk = min(2 * TM_BIG, S)
    return pl.pallas_call(
        body, name=f"dw_ffn_in_{l}", grid=(4, S // tk),
        in_specs=[pl.BlockSpec((tk, D), lambda j, k: (k, 0)), pl.BlockSpec((None, tk, DFF_SH), lambda j, k: (j, k, 0))],
        out_specs=pl.BlockSpec((None, D, DFF_SH), lambda j, k: (j, 0, 0)),
        out_shape=jax.ShapeDtypeStruct((4, D, DFF_SH), F32),
        compiler_params=_cp(("parallel", "arbitrary")),
    )(h, dgu)


_HALF_COMPS = ((0, 1, 3), (4, 2, 5))


def _dw_in(h, dz6, l):
    S = h.shape[0]

    def body(h_ref, d0_ref, d1_ref, d2_ref, o_ref):
        @pl.when(pl.program_id(1) == 0)
        def _():
            o_ref[...] = jnp.zeros_like(o_ref)
        hv = h_ref[...]
        for q, d_ref in enumerate((d0_ref, d1_ref, d2_ref)):
            for hf in range(2):
                col = 1024 * q + 512 * hf
                o_ref[col // 1536, :, col % 1536:col % 1536 + 512] += _dot_tn(hv, d_ref[:, 512 * hf:512 * (hf + 1)])

    tk = min(TM_BIG, S)

    def comp(q):
        return pl.BlockSpec((None, tk, D), lambda p, k: (jnp.where(p == 0, _HALF_COMPS[0][q], _HALF_COMPS[1][q]), k, 0))

    return pl.pallas_call(
        body, name=f"dw_in_{l}", grid=(2, S // tk),
        in_specs=[pl.BlockSpec((tk, D), lambda p, k: (k, 0)), comp(0), comp(1), comp(2)],
        out_specs=pl.BlockSpec((2, D, 1536), lambda p, k: (p, 0, 0)),
        out_shape=jax.ShapeDtypeStruct((4, D, 1536), F32),
        compiler_params=_cp(("parallel", "arbitrary")),
    )(h, dz6, dz6, dz6)


def _bwd_out(dx, w_o, merged, l):
    S = dx.shape[0]

    def body(dx_ref, w_ref, m_ref, dm_ref, dw_ref):
        @pl.when(pl.program_id(0) == 0)
        def _():
            dw_ref[...] = jnp.zeros_like(dw_ref)
        dxb = dx_ref[...].astype(BF16)
        dm_ref[...] = _dot_nt(dxb, w_ref[...]).astype(BF16)
        dw_ref[...] += _dot_tn(m_ref[...], dxb)

    tm = TM
    row = pl.BlockSpec((tm, D), lambda i: (i, 0))
    return pl.pallas_call(
        body, name=f"bwd_out_{l}", grid=(S // tm,),
        in_specs=[row, pl.BlockSpec((D, D), lambda i: (0, 0)), row],
        out_specs=[row, pl.BlockSpec((D, D), lambda i: (0, 0))],
        out_shape=[jax.ShapeDtypeStruct((S, D), BF16), jax.ShapeDtypeStruct((D, D), F32)],
        compiler_params=_cp(("arbitrary",)),
    )(dx, w_o, merged)


def _gmlp_bwd(dm, z6, ws_b, wst_b, bs_b, lg, lb, after=()):
    S = z6.shape[1]

    def body(dm_ref, z_ref, ws_ref, wst_ref, bs_ref, lg_ref, lb_ref, *rest):
        dz_ref, dws_ref, dbs_ref, dlg_ref, dlb_ref, mix, dv = rest[-7:]

        @pl.when(pl.program_id(0) == 0)
        def _():
            dws_ref[...] = jnp.zeros_like(dws_ref)
            dbs_ref[...] = jnp.zeros_like(dbs_ref)
            dlg_ref[...] = jnp.zeros_like(dlg_ref)
            dlb_ref[...] = jnp.zeros_like(dlb_ref)
        gv, dgelu_v = _gelu_and_grad(z_ref[1].astype(F32))
        xc = gv - jnp.mean(gv, axis=-1, keepdims=True)
        rs = lax.rsqrt(jnp.mean(xc * xc, axis=-1, keepdims=True) + EPS)
        vh = xc * rs
        vb = (vh * lg_ref[...] + lb_ref[...]).astype(BF16)
        for gi in range(NH):
            cs = slice(gi * HD, (gi + 1) * HD)
            mix[:, cs] = _dot(ws_ref[gi], vb[:, cs])
        u, dgelu_u = _gelu_and_grad(z_ref[0].astype(F32))
        sa = _sigmoid(z_ref[2].astype(F32))
        mixed = mix[...] + bs_ref[...]
        dyg = dm_ref[...].astype(F32)
        dz_ref[2] = (dyg * u * mixed * sa * (1.0 - sa)).astype(BF16)
        dya = dyg * sa
        dz_ref[0] = (dya * mixed * dgelu_u).astype(BF16)
        dmix = dya * u
        dmb = dmix.astype(BF16)
        for gi in range(NH):
            cs = slice(gi * HD, (gi + 1) * HD)
            dv[:, cs] = _dot(wst_ref[gi], dmb[:, cs])
            dws_ref[gi] += _dot_nt(dmb[:, cs], vb[:, cs])
            dbs_ref[gi] += jnp.broadcast_to(jnp.sum(dmix[:, cs], axis=1, keepdims=True), (CHUNK, HD))
        dvv = dv[...]
        dlg_ref[...] += jnp.sum(dvv * vh, axis=0, keepdims=True)
        dlb_ref[...] += jnp.sum(dvv, axis=0, keepdims=True)
        dvh = dvv * lg_ref[...]
        dgv = rs * (dvh - jnp.mean(dvh, axis=-1, keepdims=True) - vh * jnp.mean(dvh * vh, axis=-1, keepdims=True))
        dz_ref[1] = (dgv * dgelu_v).astype(BF16)

    vec = pl.BlockSpec((1, D), lambda i: (0, 0))
    mat = pl.BlockSpec((NH, CHUNK, CHUNK), lambda i: (0, 0, 0))
    return pl.pallas_call(
        body, name="gmlp_bwd", grid=(S // CHUNK,),
        in_specs=[pl.BlockSpec((CHUNK, D), lambda i: (i, 0)), pl.BlockSpec((3, CHUNK, D), lambda i: (0, i, 0)), mat, mat,
                  pl.BlockSpec((CHUNK, D), lambda i: (0, 0)), vec, vec] + [ANY] * len(after),
        out_specs=[pl.BlockSpec((3, CHUNK, D), lambda i: (0, i, 0)), mat, mat, vec, vec],
        out_shape=[jax.ShapeDtypeStruct((6, S, D), BF16), jax.ShapeDtypeStruct((NH, CHUNK, CHUNK), F32),
                   jax.ShapeDtypeStruct((NH, CHUNK, HD), F32), jax.ShapeDtypeStruct((1, D), F32), jax.ShapeDtypeStruct((1, D), F32)],
        scratch_shapes=[pltpu.VMEM((CHUNK, D), F32), pltpu.VMEM((CHUNK, D), F32)],
        compiler_params=_cp(("arbitrary",)),
    )(dm, z6, ws_b, wst_b, bs_b, lg, lb, *after)


def _lru_bwd(dz6, dm, z6, h0, h1, cw, cb, wr, br, wi, bi, lam):
    S = z6.shape[1]
    nt = S // RT

    def body(dz_in, dm_ref, z_ref, h0_ref, h1_ref, cw_ref, cb_ref, wr_ref, br_ref, wi_ref, bi_ref, lam_ref,
             dz_ref, dcw_ref, dcb_ref, dwr_ref, dbr_ref, dwi_ref, dbi_ref, dlam_ref, zxp, xc_s, dhs_s, dxcp, r_s, lam_s):
        del dz_in
        lam = lam_ref[...]
        sp = _softplus_neg(lam)
        row = _row_iota()
        _fill_padded(zxp, z_ref.at[0], S)
        _conv_fwd_all(zxp, xc_s, cw_ref, cb_ref, S)
        zeros = jnp.zeros((PADR, HD), F32)
        dxcp[0:PADR, :] = zeros
        dxcp[PADR + S:2 * PADR + S, :] = zeros
        dwr_ref[...] = jnp.zeros_like(dwr_ref)
        dwi_ref[...] = jnp.zeros_like(dwi_ref)

        def pre(i, c):
            rows = pl.ds(pl.multiple_of(i * RT, RT), RT)
            hs = h0_ref[rows, :] + h1_ref[rows, :]
            dmv = dm_ref[rows, :].astype(F32)
            sb = _sigmoid(z_ref[2, rows, :].astype(F32))
            gg, dgg = _gelu_and_grad(z_ref[1, rows, :].astype(F32))
            dz_ref[2, rows, :] = (dmv * hs * gg * sb * (1.0 - sb)).astype(BF16)
            dyb = dmv * sb
            dz_ref[1, rows, :] = (dyb * hs * dgg).astype(BF16)
            dhs_s[rows, :] = dyb * gg
            return c
        lax.fori_loop(0, nt, pre, 0)

        def gate_bwd(d, gates, lamv, da, xc):
            r, gi, a, mult = gates
            dmult = lamv * gi * xc
            dgi = lamv * mult * xc
            dlog = (da - dmult * a / mult) * a
            dpr = (dlog * (-LRU_C) * sp[d:d + 1, :]) * r * (1.0 - r)
            dpi = dgi * gi * (1.0 - gi)
            xb, dprb, dpib = xc.astype(BF16), dpr.astype(BF16), dpi.astype(BF16)
            dwr_ref[d] += _dot_tn(xb, dprb)
            dwi_ref[d] += _dot_tn(xb, dpib)
            dxc = lamv * mult * gi + _dot_nt(dprb, wr_ref[d]) + _dot_nt(dpib, wi_ref[d])
            return dxc, (jnp.sum(dlog * r, axis=0, keepdims=True) * (-LRU_C), jnp.sum(dpr, axis=0, keepdims=True),
                         jnp.sum(dpi, axis=0, keepdims=True))

        def rgates(i, c):
            for u in range(UNROLL):
                rows = pl.ds(pl.multiple_of((i * UNROLL + u) * RT, RT), RT)
                xb = xc_s[rows, :].astype(BF16)
                for d in range(2):
                    r_s[d, rows, :] = _sigmoid(_dot(xb, wr_ref[d]) + br_ref[d:d + 1, :])
            return c
        lax.fori_loop(0, nt // UNROLL, rgates, 0)

        def chains(i, carry):
            qn, qp = carry
            for u in range(UNROLL):
                j = i * UNROLL + u
                rd = pl.ds(pl.multiple_of((nt - 1 - j) * RT, RT), RT)
                a, dhs = _decay(r_s[0, rd, :], sp[0:1, :])[0], dhs_s[rd, :]
                q, q_first = _scan_down(a, a * dhs, qn)
                lam_s[0, rd, :] = dhs + jnp.where(row == RT - 1, qn, pltpu.roll(q, RT - 1, 0))
                qn = q_first
                ru = pl.ds(pl.multiple_of(j * RT, RT), RT)
                a, dhs = _decay(r_s[1, ru, :], sp[1:2, :])[0], dhs_s[ru, :]
                q, q_last = _scan_up(a, a * dhs, qp)
                lam_s[1, ru, :] = dhs + jnp.where(row == 0, qp, pltpu.roll(q, 1, 0))
                qp = q_last
            return qn, qp

        z1 = jnp.zeros((1, HD), F32)
        lax.fori_loop(0, nt // UNROLL, chains, (z1, z1))

        ct = min(GRAD_ROWS, S)
        crow = lax.broadcasted_iota(jnp.int32, (ct, HD), 0)

        def tile_grads(i, acc):
            t0 = pl.multiple_of(i * ct, ct)
            rows = pl.ds(t0, ct)
            xc = xc_s[rows, :]
            xb = xc.astype(BF16)
            tp = pl.multiple_of(jnp.maximum(t0 - PADR, 0), PADR)
            prev = jnp.where(t0 > 0, h0_ref[pl.ds(tp, PADR), :][PADR - 1:PADR, :], 0.0)
            tn = pl.multiple_of(jnp.minimum(t0 + ct, S - PADR), PADR)
            nxt = jnp.where(t0 + ct < S, h1_ref[pl.ds(tn, PADR), :][0:1, :], 0.0)
            hside = (jnp.where(crow == 0, prev, pltpu.roll(h0_ref[rows, :], 1, 0)),
                     jnp.where(crow == ct - 1, nxt, pltpu.roll(h1_ref[rows, :], ct - 1, 0)))
            dxc, sums = 0.0, ()
            for d in range(2):
                r = r_s[d, rows, :]
                gi = _sigmoid(_dot(xb, wi_ref[d]) + bi_ref[d:d + 1, :])
                a, mult = _decay(r, sp[d:d + 1, :])
                lamv = lam_s[d, rows, :]
                dxc_d, s_d = gate_bwd(d, (r, gi, a, mult), lamv, lamv * hside[d], xc)
                dxc = dxc + dxc_d
                sums = sums + s_d
            dxcp[pl.ds(t0 + PADR, ct), :] = dxc
            return tuple(x + y for x, y in zip(acc, sums))

        s_sp0, s_br0, s_bi0, s_sp1, s_br1, s_bi1 = lax.fori_loop(0, S // ct, tile_grads, (z1,) * 6)

        dsp = jnp.concatenate([s_sp0, s_sp1], axis=0)
        dlam_ref[...] = -dsp * _sigmoid(-lam)
        dbr_ref[...] = jnp.concatenate([s_br0, s_br1], axis=0)
        dbi_ref[...] = jnp.concatenate([s_bi0, s_bi1], axis=0)

        def conv_bwd(i, carry):
            c0, c1, c2, c3, cb_ = carry
            t0 = pl.multiple_of(i * RT, RT)
            dwin = dxcp[pl.ds(t0, RT + 2 * PADR), :]
            d0 = _shifted(dwin, 0)
            dz_ref[0, pl.ds(t0, RT), :] = (_shifted(dwin, 1) * cw_ref[0:1, :] + d0 * cw_ref[1:2, :]
                                           + _shifted(dwin, -1) * cw_ref[2:3, :] + _shifted(dwin, -2) * cw_ref[3:4, :]).astype(BF16)
            xm1, x0, xp1, xp2 = _conv_taps(zxp[pl.ds(t0, RT + 2 * PADR), :])
            sm = lambda v: jnp.sum(v, axis=0, keepdims=True)
            return c0 + sm(d0 * xm1), c1 + sm(d0 * x0), c2 + sm(d0 * xp1), c3 + sm(d0 * xp2), cb_ + sm(d0)

        c0, c1, c2, c3, cb_ = lax.fori_loop(0, nt, conv_bwd, (z1, z1, z1, z1, z1))
        dcw_ref[...] = jnp.concatenate([c0, c1, c2, c3], axis=0)
        dcb_ref[...] = cb_

    col = pl.BlockSpec((S, HD), lambda h: (0, h))
    head = lambda h: (0, h)
    wspec = pl.BlockSpec((2, None, HD, HD), lambda h: (0, h, 0, 0))
    return pl.pallas_call(
        body, name="lru_bwd", grid=(NH,),
        in_specs=[pl.BlockSpec(memory_space=pl.ANY), col, pl.BlockSpec((3, S, HD), lambda h: (1, 0, h)), col, col] + _lru_specs(S),
        out_specs=[pl.BlockSpec((3, S, HD), lambda h: (1, 0, h)), pl.BlockSpec((4, HD), head), pl.BlockSpec((1, HD), head),
                   wspec, pl.BlockSpec((2, HD), head), wspec, pl.BlockSpec((2, HD), head), pl.BlockSpec((2, HD), head)],
        out_shape=[jax.ShapeDtypeStruct((6, S, D), BF16), jax.ShapeDtypeStruct((4, D), F32), jax.ShapeDtypeStruct((1, D), F32),
                   jax.ShapeDtypeStruct((2, NH, HD, HD), F32), jax.ShapeDtypeStruct((2, D), F32),
                   jax.ShapeDtypeStruct((2, NH, HD, HD), F32), jax.ShapeDtypeStruct((2, D), F32), jax.ShapeDtypeStruct((2, D), F32)],
        scratch_shapes=[pltpu.VMEM((S + 2 * PADR, HD), F32), pltpu.VMEM((S, HD), F32), pltpu.VMEM((S, HD), F32),
                        pltpu.VMEM((S + 2 * PADR, HD), F32), pltpu.VMEM((2, S, HD), F32), pltpu.VMEM((2, S, HD), F32)],
        input_output_aliases={0: 0},
        compiler_params=_cp(("parallel",)),
    )(dz6, dm, z6, h0, h1, cw, cb, wr, br, wi, bi, lam)


LAYER_SMALL = ("norm1_g", "gmlp_ln_g", "gmlp_ln_b", "gmlp_w_s", "gmlp_b_s", "conv_w", "conv_b",
               "lru_w_r", "lru_b_r", "lru_w_i", "lru_b_i", "lru_lambda", "norm2_g")


def _forward_layer(l, x, p, wb, after=(), rest=None):
    g1, g2 = p["norm1_g"][l][None], p["norm2_g"][l][None]
    ws_b = p["gmlp_w_s"][l].astype(BF16)
    tm = dict(ws_b=ws_b, wst_b=jnp.swapaxes(ws_b, 1, 2), bs_b=jnp.repeat(p["gmlp_b_s"][l].T, HD, axis=1),
              lg=p["gmlp_ln_g"][l][None], lb=p["gmlp_ln_b"][l][None])
    lru = (p["conv_w"][l], p["conv_b"][l][None], p["lru_w_r"][l].astype(BF16), p["lru_b_r"][l],
           p["lru_w_i"][l].astype(BF16), p["lru_b_i"][l], p["lru_lambda"][l])
    z6, hn1 = _mm_in(x, g1, wb["w_in"], l, after)
    ya = _gmlp_fwd(z6, tm["ws_b"], tm["bs_b"], tm["lg"], tm["lb"])
    merged, h0, h1 = _lru_fwd(z6, ya, *lru)
    if rest is not None:
        wb = dict(wb, **rest(merged))
    x1 = _mm_res(merged, wb["w_out"], x, l, "mm_out")
    gu, ff, hn2 = _mm_ffn_in(x1, g2, wb["w_ffn_in"], l)
    x2 = _mm_res(ff, wb["w_ffn_out"], x1, l, "mm_ffn_out")
    return x2, dict(x=x, z6=z6, h0=h0, h1=h1, merged=merged, x1=x1, gu=gu, ff=ff, g1=g1, g2=g2, tm=tm, lru=lru,
                    hn1=hn1, hn2=hn2, wb=wb)


def _backward_layer(l, dx, s, after=(), midway=None, late=None):
    S = dx.shape[0]
    tm, wb = s["tm"], s["wb"]
    g2 = s["g2"]
    dgu = _bwd_ffn_out(dx, wb["w_ffn_out"], s["gu"], l, after)
    tmb = min(TM_BIG, S)
    dwfo = _mm_tn(s["ff"], dx, DFF_SH, tmb, f"dw_ffn_out_{l}")
    dx1, dg2 = _mm_nt_rms_bwd(
        dgu, [pl.BlockSpec((None, tmb, DFF_SH), lambda i, k: (k, i, 0))],
        wb["w_ffn_in"], [pl.BlockSpec((None, D, DFF_SH), lambda i, k: (k, 0, 0))],
        4, tmb, s["x1"], g2, dx, f"bwd_ffn_in_{l}")
    dwfi = _dw_ffn_in(s["hn2"], dgu, l)
    dmg, dwo = _bwd_out(dx1, wb["w_out"], s["merged"], l)
    mid = () if midway is None else tuple(midway([dwo, dwfi, dwfo]))
    dz6, dws, dbs, dlg, dlb = _gmlp_bwd(dmg, s["z6"], tm["ws_b"], tm["wst_b"], tm["bs_b"], tm["lg"], tm["lb"], mid)
    dz6, dcw, dcb, dwr, dbr, dwi, dbi, dlam = _lru_bwd(dz6, dmg, s["z6"], s["h0"], s["h1"], *s["lru"])

    sub = 3

    def dz_tile(j):
        return pl.BlockSpec((None, tmb, 512), lambda i, k: ((sub * k + j) // 2, i, (sub * k + j) % 2))

    def w_tile(j):
        def w_map(i, k):
            sh, tl = _in_tile(sub * k + j)
            return (sh, 0, tl)
        return pl.BlockSpec((None, D, 512), w_map)

    dwin = _dw_in(s["hn1"], dz6, l)
    tail = () if late is None else tuple(late([dwin]))
    dx0, dg1 = _mm_nt_rms_bwd(
        dz6, [dz_tile(j) for j in range(sub)], wb["w_in"], [w_tile(j) for j in range(sub)],
        N_IN_T // sub, tmb, s["x"], s["g1"], dx1, f"bwd_in_{l}", tail)
    small = dict(norm1_g=dg1[0], gmlp_ln_g=dlg[0], gmlp_ln_b=dlb[0], gmlp_w_s=dws, gmlp_b_s=dbs[:, :, 0], conv_w=dcw, conv_b=dcb[0],
                 lru_w_r=dwr, lru_b_r=dbr, lru_w_i=dwi, lru_b_i=dbi, lru_lambda=dlam, norm2_g=dg2[0])
    return dx0, [dwin, dwo, dwfi, dwfo], small


def _local_step(x, tgt, p, wbs):
    saved = []
    for l in range(2):
        x, s = _forward_layer(l, x, p, wbs[l])
        saved.append(s)
    dx, loss_v, dfg = _loss_head(x, tgt, p["final_g"][None])
    big, smalls = [None, None], [None, None]
    for l in (1, 0):
        dx, big[l], smalls[l] = _backward_layer(l, dx, saved[l])
    small = {k: jnp.stack([smalls[0][k], smalls[1][k]]) for k in LAYER_SMALL}
    small["final_g"] = dfg[0]
    return loss_v, dx, big, small


def _place():
    x, y, c = lax.axis_index("x"), lax.axis_index("y"), lax.axis_index("c")
    return x, y, c, 2 * x + y


def _chip_at(x, y, d):
    px = 1 - x if d & 2 else x
    py = 1 - y if d & 1 else y
    return px, py, 2 * px + py


HBM = pl.BlockSpec(memory_space=pltpu.HBM)
SEM = pl.BlockSpec(memory_space=pltpu.SEMAPHORE)
DATAFLOW = pltpu.SideEffectType.DATAFLOW_SIDE_EFFECTING


def _in_hbm(a):
    return pltpu.with_memory_space_constraint(a, pltpu.HBM)


def _cast_into(wf, l, chip_arr, name):
    _, rows, cols = wf.shape
    rh = rows // 2

    def body(ch_ref, w_ref, o_ref):
        o_ref[...] = w_ref[...].astype(BF16)

    return pl.pallas_call(
        body, name=name, out_shape=jax.ShapeDtypeStruct((4, 2, rh, cols), BF16),
        grid_spec=pltpu.PrefetchScalarGridSpec(
            num_scalar_prefetch=1, grid=(2,),
            in_specs=[pl.BlockSpec((None, None, rh, cols), lambda h, ch: (l, h, 0, 0))],
            out_specs=pl.BlockSpec((None, None, rh, cols), lambda h, ch: (ch[0], h, 0, 0))),
        compiler_params=_cp(("parallel",)),
    )(chip_arr, wf.reshape(2, 2, rh, cols))


def _half_block(ref, chip, half, to, send_sem, recv_sem):
    blk = ref.at[chip, half]
    return pltpu.make_async_remote_copy(src_ref=blk, dst_ref=blk, send_sem=send_sem, recv_sem=recv_sem,
                                        device_id=to, device_id_type=MESH)


def _gather_weights(bufs, tiny):
    nt = len(bufs)
    n_ici = max(nt * 3, 1)

    def body(*refs):
        tiny_ref = refs[nt]
        o_refs, tiny_o = refs[nt + 1:2 * nt + 1], refs[2 * nt + 1]
        send, recv, fsend, frecv, tsend, trecv, lsem = refs[2 * nt + 2:]
        x, y, c, chip = _place()
        local = pltpu.make_async_copy(tiny_ref, tiny_o.at[chip], lsem)
        local.start()

        def tin(d, origin_chip, to):
            return pltpu.make_async_remote_copy(
                src_ref=tiny_ref, dst_ref=tiny_o.at[origin_chip], send_sem=tsend.at[d - 1], recv_sem=trecv.at[d - 1],
                device_id=to, device_id_type=MESH)

        sends = []
        for t in range(nt):
            for d in (1, 2, 3):
                px, py, _ = _chip_at(x, y, d)
                sends.append(_half_block(o_refs[t], chip, c, (px, py, c), send.at[3 * t + d - 1], recv.at[3 * t + d - 1]))
        for d in (1, 2, 3):
            px, py, _ = _chip_at(x, y, d)
            sends.append(tin(d, chip, (px, py, c)))
        for cp in sends:
            cp.start()
        passed = []
        for t in range(nt):
            for d in (1, 2, 3):
                k = 3 * t + d - 1
                _, _, pchip = _chip_at(x, y, d)
                _half_block(o_refs[t], pchip, c, (x, y, c), send.at[k], recv.at[k]).wait_recv()
                f = _half_block(o_refs[t], pchip, c, (x, y, 1 - c), fsend.at[k], frecv.at[k])
                f.start()
                passed.append(f)
        for t in range(nt):
            for d in (1, 2, 3):
                k = 3 * t + d - 1
                _, _, pchip = _chip_at(x, y, d)
                _half_block(o_refs[t], pchip, 1 - c, (x, y, 1 - c), fsend.at[k], frecv.at[k]).wait_recv()
        for d in (1, 2, 3):
            _, _, pchip = _chip_at(x, y, d)
            tin(d, pchip, (x, y, c)).wait_recv()
        for cp in sends + passed:
            cp.wait_send()
        local.wait()

    out_shape = [jax.ShapeDtypeStruct(b.shape, b.dtype) for b in bufs]
    out_shape.append(jax.ShapeDtypeStruct((4,) + tiny.shape, tiny.dtype))
    outs = pl.pallas_call(
        body, name="gather_weights_0", out_shape=out_shape,
        in_specs=[ANY] * (nt + 1), out_specs=[ANY] * (nt + 1),
        scratch_shapes=[pltpu.SemaphoreType.DMA((n_ici,)), pltpu.SemaphoreType.DMA((n_ici,)),
                        pltpu.SemaphoreType.DMA((n_ici,)), pltpu.SemaphoreType.DMA((n_ici,)),
                        pltpu.SemaphoreType.DMA((3,)), pltpu.SemaphoreType.DMA((3,)), pltpu.SemaphoreType.DMA],
        input_output_aliases={t: t for t in range(nt)},
        compiler_params=_cp(has_side_effects=True),
    )(*bufs, tiny)
    return outs[:nt], outs[nt]


def _gather_start(bufs, tag, after=()):
    nt, na = len(bufs), len(after)

    def body(*refs):
        b_refs = refs[:nt]
        send, recv = refs[nt + na], refs[nt + na + 1]
        token = refs[2 * nt + na + 2]
        x, y, c, chip = _place()
        for t in range(nt):
            for d in (1, 2, 3):
                px, py, _ = _chip_at(x, y, d)
                _half_block(b_refs[t], chip, c, (px, py, c), send.at[3 * t + d - 1], recv.at[3 * t + d - 1]).start()
        token[...] = jnp.zeros_like(token)

    outs = pl.pallas_call(
        body, name=f"gather_start_{tag}",
        out_shape=(pltpu.SemaphoreType.DMA((3 * nt,)), pltpu.SemaphoreType.DMA((3 * nt,)),
                   *[pltpu.HBM(b.shape, b.dtype) for b in bufs], jax.ShapeDtypeStruct((8, 128), F32)),
        in_specs=[HBM] * nt + [ANY] * na, out_specs=(SEM, SEM, *[HBM] * nt, pl.BlockSpec(memory_space=pltpu.VMEM)),
        input_output_aliases={t: 2 + t for t in range(nt)},
        compiler_params=pltpu.CompilerParams(has_side_effects=DATAFLOW),
    )(*[_in_hbm(b) for b in bufs], *after)
    return outs[0], outs[1], list(outs[2:2 + nt]), outs[2 + nt]


def _gather_wait(send, recv, bufs, after, tag):
    nt = len(bufs)

    def body(*refs):
        b_refs = refs[:nt]
        send_ref, recv_ref = refs[nt], refs[nt + 1]
        x, y, c, chip = _place()
        for t in range(nt):
            for d in (1, 2, 3):
                k = 3 * t + d - 1
                px, py, pchip = _chip_at(x, y, d)
                _half_block(b_refs[t], chip, c, (px, py, c), send_ref.at[k], recv_ref.at[k]).wait_send()
                _half_block(b_refs[t], pchip, c, (px, py, c), send_ref.at[k], recv_ref.at[k]).wait_recv()

    outs = pl.pallas_call(
        body, name=f"gather_wait_{tag}", out_shape=[pltpu.HBM(b.shape, b.dtype) for b in bufs],
        in_specs=[HBM] * nt + [SEM, SEM, ANY], out_specs=[HBM] * nt,
        input_output_aliases={t: t for t in range(nt)},
        compiler_params=pltpu.CompilerParams(has_side_effects=DATAFLOW),
    )(*bufs, send, recv, after)
    return list(outs)


def _gather_pass_on(bufs, tag):
    nt = len(bufs)

    def body(*refs):
        o_refs = refs[nt:2 * nt]
        fsend, frecv = refs[2 * nt:]
        x, y, c, _ = _place()
        cps = []
        for t in range(nt):
            for d in (1, 2, 3):
                k = 3 * t + d - 1
                _, _, pchip = _chip_at(x, y, d)
                cps.append(_half_block(o_refs[t], pchip, c, (x, y, 1 - c), fsend.at[k], frecv.at[k]))
        for cp in cps:
            cp.start()
        for t in range(nt):
            for d in (1, 2, 3):
                k = 3 * t + d - 1
                _, _, pchip = _chip_at(x, y, d)
                _half_block(o_refs[t], pchip, 1 - c, (x, y, 1 - c), fsend.at[k], frecv.at[k]).wait_recv()
        for cp in cps:
            cp.wait_send()

    return pl.pallas_call(
        body, name=f"gather_pass_on_{tag}", out_shape=[jax.ShapeDtypeStruct(b.shape, b.dtype) for b in bufs],
        in_specs=[ANY] * nt, out_specs=[ANY] * nt,
        scratch_shapes=[pltpu.SemaphoreType.DMA((3 * nt,)), pltpu.SemaphoreType.DMA((3 * nt,))],
        input_output_aliases={t: t for t in range(nt)},
        compiler_params=_cp(has_side_effects=True),
    )(*bufs)


def _to_sibling_halves(gs, l):
    nt = len(gs)

    def body(*refs):
        g_refs, o_refs = refs[:nt], refs[nt:2 * nt]
        send, recv = refs[2 * nt:]
        x, y, c, _ = _place()
        cps = [pltpu.make_async_remote_copy(
            src_ref=g_refs[t].at[k, 1 - c], dst_ref=o_refs[t].at[k], send_sem=send.at[4 * t + k], recv_sem=recv.at[4 * t + k],
            device_id=(x, y, 1 - c), device_id_type=MESH) for t in range(nt) for k in range(4)]
        for cp in cps:
            cp.start()
        for cp in cps:
            cp.wait()

    return pl.pallas_call(
        body, name=f"grads_to_sibling_{l}", out_shape=[jax.ShapeDtypeStruct((4,) + g.shape[2:], g.dtype) for g in gs],
        in_specs=[ANY] * nt, out_specs=[ANY] * nt,
        scratch_shapes=[pltpu.SemaphoreType.DMA((4 * nt,)), pltpu.SemaphoreType.DMA((4 * nt,))],
        compiler_params=_cp(has_side_effects=True),
    )(*gs)


def _chip_copy(c_ref, land_ref, x, y, c, d, send_sem, recv_sem):
    px, py, pchip = _chip_at(x, y, d)
    return pltpu.make_async_remote_copy(src_ref=c_ref.at[pchip], dst_ref=land_ref.at[d - 1], send_sem=send_sem, recv_sem=recv_sem,
                                        device_id=(px, py, c), device_id_type=MESH)


def _exchange_start(srcs, lands, copies, nsem, name):
    ns, n = len(srcs), len(srcs) + len(lands)

    def body(*refs):
        for cp in copies(refs[:ns], refs[ns:n], refs[n], refs[n + 1]):
            cp.start()
        token = refs[2 * n + 2]
        token[...] = jnp.zeros_like(token)

    outs = pl.pallas_call(
        body, name=name,
        out_shape=(pltpu.SemaphoreType.DMA((nsem,)), pltpu.SemaphoreType.DMA((nsem,)),
                   *[pltpu.HBM(a.shape, a.dtype) for a in list(srcs) + list(lands)], jax.ShapeDtypeStruct((8, 128), F32)),
        in_specs=[HBM] * n, out_specs=(SEM, SEM, *[HBM] * n, pl.BlockSpec(memory_space=pltpu.VMEM)),
        input_output_aliases={i: 2 + i for i in range(n)},
        compiler_params=pltpu.CompilerParams(has_side_effects=DATAFLOW),
    )(*[_in_hbm(a) for a in list(srcs) + list(lands)])
    return outs[0], outs[1], list(outs[2:2 + ns]), list(outs[2 + ns:2 + n]), outs[2 + n]


def _exchange_wait(send, recv, srcs, lands, after, copies, name):
    ns, n = len(srcs), len(srcs) + len(lands)

    def body(*refs):
        for cp in copies(refs[:ns], refs[ns:n], refs[n], refs[n + 1]):
            cp.wait_send()
            cp.wait_recv()

    outs = pl.pallas_call(
        body, name=name, out_shape=[pltpu.HBM(a.shape, a.dtype) for a in list(srcs) + list(lands)],
        in_specs=[HBM] * n + [SEM, SEM, ANY], out_specs=[HBM] * n,
        input_output_aliases={i: i for i in range(n)},
        compiler_params=pltpu.CompilerParams(has_side_effects=DATAFLOW),
    )(*srcs, *lands, send, recv, after)
    return list(outs[:ns]), list(outs[ns:])


def _chips_copies(c_refs, land_refs, send, recv):
    x, y, c, _ = _place()
    return [_chip_copy(c_refs[t], land_refs[t], x, y, c, d, send.at[3 * t + d - 1], recv.at[3 * t + d - 1])
            for t in range(len(c_refs)) for d in (1, 2, 3)]


def _sibling_copies(g_refs, land_refs, send, recv):
    x, y, c, _ = _place()
    return [pltpu.make_async_remote_copy(
        src_ref=g_refs[t].at[k, 1 - c], dst_ref=land_refs[t].at[k], send_sem=send.at[4 * t + k], recv_sem=recv.at[4 * t + k],
        device_id=(x, y, 1 - c), device_id_type=MESH) for t in range(len(g_refs)) for k in range(4)]


def _join_halves(fs, l):
    nt = len(fs)

    def body(*refs):
        o_refs = refs[nt:2 * nt]
        send, recv = refs[2 * nt:]
        x, y, c, _ = _place()
        cps = [pltpu.make_async_remote_copy(
            src_ref=o_refs[t].at[c], dst_ref=o_refs[t].at[c], send_sem=send.at[t], recv_sem=recv.at[t],
            device_id=(x, y, 1 - c), device_id_type=MESH) for t in range(nt)]
        for cp in cps:
            cp.start()
        for cp in cps:
            cp.wait()

    return pl.pallas_call(
        body, name=f"grads_join_{l}", out_shape=[jax.ShapeDtypeStruct(a.shape, a.dtype) for a in fs],
        in_specs=[ANY] * nt, out_specs=[ANY] * nt,
        scratch_shapes=[pltpu.SemaphoreType.DMA((nt,)), pltpu.SemaphoreType.DMA((nt,))],
        input_output_aliases={t: t for t in range(nt)},
        compiler_params=_cp(has_side_effects=True),
    )(*fs)


def _add_half(g, r, c_arr, name):
    _, _, rh, cols = g.shape

    def body(c_ref, g_ref, r_ref, o_ref):
        o_ref[...] = (g_ref[...] + r_ref[...]).astype(BF16)

    blk = pl.BlockSpec((None, rh, cols), lambda k, cr: (k, 0, 0))
    return pl.pallas_call(
        body, name=name, out_shape=jax.ShapeDtypeStruct((4, rh, cols), BF16),
        grid_spec=pltpu.PrefetchScalarGridSpec(
            num_scalar_prefetch=1, grid=(4,),
            in_specs=[pl.BlockSpec((None, None, rh, cols), lambda k, cr: (k, cr[0], 0, 0)), blk], out_specs=blk),
        compiler_params=_cp(("parallel",)),
    )(c_arr, g, r)


def _sum_chips(cs, r3, place_arr, name):
    _, rh, cols = cs.shape
    rb = rh // 2

    def body(pl_ref, a_ref, r0_ref, r1_ref, r2_ref, o_ref):
        up = lambda ref: ref[...].astype(F32)
        o_ref[...] = ((up(a_ref) + up(r0_ref)) + up(r1_ref)) + up(r2_ref)

    def slot(d):
        return pl.BlockSpec((None, rb, cols), lambda i, pa: (d, i, 0))

    return pl.pallas_call(
        body, name=name, out_shape=jax.ShapeDtypeStruct((2, rh, cols), F32),
        grid_spec=pltpu.PrefetchScalarGridSpec(
            num_scalar_prefetch=1, grid=(2,),
            in_specs=[pl.BlockSpec((None, rb, cols), lambda i, pa: (pa[0], i, 0)), slot(0), slot(1), slot(2)],
            out_specs=pl.BlockSpec((None, rb, cols), lambda i, pa: (pa[1], i, 0))),
        compiler_params=_cp(("parallel",)),
    )(place_arr, cs, r3, r3, r3)


def _allreduce_small(pack):
    rows = pack.shape[0]
    hr = rows // 2

    def body(p_ref, o_ref, sib, slots, s1, r1, s2, r2, s3, r3):
        x, y, c, chip = _place()
        sibling = (x, y, 1 - c)
        ex = pltpu.make_async_remote_copy(src_ref=p_ref, dst_ref=sib, send_sem=s1, recv_sem=r1,
                                          device_id=sibling, device_id_type=MESH)
        ex.start()
        ex.wait()
        half = pl.ds(pl.multiple_of(c * hr, 16), hr)
        slots[0] = (p_ref[half, :] + sib[half, :]).astype(BF16)
        cps = []
        for d in (1, 2, 3):
            px, py, _ = _chip_at(x, y, d)
            cps.append(pltpu.make_async_remote_copy(
                src_ref=slots.at[0], dst_ref=slots.at[d], send_sem=s2.at[d - 1], recv_sem=r2.at[d - 1],
                device_id=(px, py, c), device_id_type=MESH))
        for cp in cps:
            cp.start()
        for cp in cps:
            cp.wait()
        tot = slots[chip].astype(F32)
        for k in (1, 2, 3):
            tot = tot + slots[jnp.bitwise_xor(chip, k)].astype(F32)
        o_ref[half, :] = tot
        back = pltpu.make_async_remote_copy(src_ref=o_ref.at[half, :], dst_ref=o_ref.at[half, :], send_sem=s3, recv_sem=r3,
                                            device_id=sibling, device_id_type=MESH)
        back.start()
        back.wait()

    vm = pl.BlockSpec(memory_space=pltpu.VMEM)
    return pl.pallas_call(
        body, name="allreduce_small", out_shape=jax.ShapeDtypeStruct((rows, 128), F32),
        in_specs=[vm], out_specs=vm,
        scratch_shapes=[pltpu.VMEM((rows, 128), F32), pltpu.VMEM((4, hr, 128), BF16),
                        pltpu.SemaphoreType.DMA, pltpu.SemaphoreType.DMA, pltpu.SemaphoreType.DMA((3,)), pltpu.SemaphoreType.DMA((3,)),
                        pltpu.SemaphoreType.DMA, pltpu.SemaphoreType.DMA],
        compiler_params=_cp(has_side_effects=True),
    )(pack)


def _adam_math(gv, wv, mv, vv):
    m2 = ADAM_B1 * mv + (1.0 - ADAM_B1) * gv
    v2 = ADAM_B2 * vv + (1.0 - ADAM_B2) * (gv * gv)
    m_hat = m2 / (1.0 - ADAM_B1 ** ADAM_STEP)
    v_hat = v2 / (1.0 - ADAM_B2 ** ADAM_STEP)
    return -ADAM_LR * (m_hat / (jnp.sqrt(v_hat) + ADAM_EPS) + ADAM_WD * wv), m2, v2


def _adam(g, w, m, v, name):
    rows, cols = g.shape
    rb = rows // 4

    def body(g_ref, w_ref, m_ref, v_ref, d_ref, m2_ref, v2_ref):
        d_ref[...], m2_ref[...], v2_ref[...] = _adam_math(g_ref[...], w_ref[...], m_ref[...], v_ref[...])

    blk = pl.BlockSpec((rb, cols), lambda i: (i, 0))
    shp = jax.ShapeDtypeStruct((rows, cols), F32)
    return pl.pallas_call(
        body, name=name, grid=(4,), in_specs=[blk] * 4, out_specs=[blk] * 3, out_shape=[shp] * 3,
        compiler_params=_cp(("parallel",)),
    )(g, w, m, v)


def _adam_layer(g, w, m, v, l, prev, name):
    rows, cols = g.shape
    rb = rows // 4

    def body(g_ref, w_ref, m_ref, v_ref, *rest):
        go_ref, d_ref, m2_ref, v2_ref = rest[-4:]
        gv = g_ref[...]
        go_ref[...] = gv
        d_ref[...], m2_ref[...], v2_ref[...] = _adam_math(gv, w_ref[...], m_ref[...], v_ref[...])

    lay = pl.BlockSpec((None, rb, cols), lambda i: (l, i, 0))
    shp = jax.ShapeDtypeStruct((2, rows, cols), F32)
    prev = () if prev is None else tuple(prev)
    return pl.pallas_call(
        body, name=name, grid=(4,), in_specs=[pl.BlockSpec((rb, cols), lambda i: (i, 0)), lay, lay, lay] + [ANY] * len(prev),
        out_specs=[lay] * 4, out_shape=[shp] * 4,
        input_output_aliases={4 + j: j for j in range(len(prev))},
        compiler_params=_cp(("parallel",)),
    )(g, w, m, v, *prev)


def _rows128(a):
    return a.reshape(-1, 128)


def _pack(arrs, mult):
    parts = [_rows128(a) for a in arrs]
    rows = sum(q.shape[0] for q in parts)
    pad = -rows % mult
    if pad:
        parts.append(jnp.zeros((pad, 128), F32))
    return jnp.concatenate(parts, axis=0)


def _unpack(pack, shapes):
    out, o = [], 0
    for s in shapes:
        n = 1
        for e in s:
            n *= e
        out.append(pack[o:o + n // 128].reshape(s))
        o += n // 128
    return out


WEIGHTS = ['norm1_g', 'w_in', 'gmlp_ln_g', 'gmlp_ln_b', 'gmlp_w_s', 'gmlp_b_s', 'conv_w', 'conv_b', 'lru_w_r', 'lru_b_r', 'lru_w_i',
           'lru_b_i', 'lru_lambda', 'w_out', 'norm2_g', 'w_ffn_in', 'w_ffn_out', 'final_g']
BIG = ['w_in', 'w_out', 'w_ffn_in', 'w_ffn_out']
SMALL = [n for n in WEIGHTS if n not in BIG]
CHIP_SHARDED_SMALL = ['conv_w', 'lru_b_r', 'lru_b_i', 'lru_lambda']


def kernel(x, norm1_g, w_in, gmlp_ln_g, gmlp_ln_b, gmlp_w_s, gmlp_b_s, conv_w, conv_b, lru_w_r, lru_b_r, lru_w_i, lru_b_i, lru_lambda, w_out, norm2_g, w_ffn_in, w_ffn_out, final_g, loss_target, m_norm1_g, m_w_in, m_gmlp_ln_g, m_gmlp_ln_b, m_gmlp_w_s, m_gmlp_b_s, m_conv_w, m_conv_b, m_lru_w_r, m_lru_b_r, m_lru_w_i, m_lru_b_i, m_lru_lambda, m_w_out, m_norm2_g, m_w_ffn_in, m_w_ffn_out, m_final_g, v_norm1_g, v_w_in, v_gmlp_ln_g, v_gmlp_ln_b, v_gmlp_w_s, v_gmlp_b_s, v_conv_w, v_conv_b, v_lru_w_r, v_lru_b_r, v_lru_w_i, v_lru_b_i, v_lru_lambda, v_w_out, v_norm2_g, v_w_ffn_in, v_w_ffn_out, v_final_g):
    a = dict(locals())
    w = {n: a[n] for n in WEIGHTS}
    mom = {n: a["m_" + n] for n in WEIGHTS}
    var = {n: a["v_" + n] for n in WEIGHTS}
    _, _, c, chip = _place()
    c_arr, chip_arr = jnp.reshape(c, (1,)).astype(jnp.int32), jnp.reshape(chip, (1,)).astype(jnp.int32)
    place_arr = jnp.stack([chip, c]).astype(jnp.int32)

    first, rest = BIG[:1], BIG[1:]

    def as_weights(names, full):
        wb = {n: f.reshape(4, 2 * f.shape[2], f.shape[3]) for n, f in zip(names, full)}
        if "w_out" in wb:
            wb["w_out"] = wb["w_out"].reshape(D, D)
            wb["w_ffn_out"] = wb["w_ffn_out"].reshape(DFF, D)
        return wb

    def cast(n, l):
        return _cast_into(w[n], l, chip_arr, f"cast_{n}_{l}")

    def landed(fly, names, after, tag):
        return as_weights(names, _gather_pass_on(_gather_wait(fly[0], fly[1], fly[2], after, tag), tag))

    tiny = _pack([w[n] for n in CHIP_SHARDED_SMALL], 8)
    _, tiny_full = _gather_weights([], tiny)
    fly_in = _gather_start([cast("w_in", 0)], "in", after=(tiny_full,))
    fly0 = _gather_start([cast(n, 0) for n in rest], "0", after=(fly_in[3],))
    bufs1 = [cast(n, 1) for n in BIG]
    fly1 = _gather_start(bufs1, "1", after=(fly0[3],))
    p = {n: w[n] for n in SMALL}
    parts = [_unpack(tiny_full[k], [w[n].shape for n in CHIP_SHARDED_SMALL]) for k in range(4)]
    for i, n in enumerate(CHIP_SHARDED_SMALL):
        p[n] = jnp.concatenate([parts[k][i] for k in range(4)], axis=-1)

    xa, saved0 = _forward_layer(0, x[0], p, landed(fly_in, first, fly1[3], "in"), after=(fly0[3], fly1[3]),
                                rest=lambda merged: landed(fly0, rest, merged, "0"))
    xb, saved1 = _forward_layer(1, xa, p, landed(fly1, BIG, xa, "1"))
    dxb, loss_v, dfg = _loss_head(xb, loss_target[0], p["final_g"][None])
    loss = lax.psum(loss_v[0, 0], ("x", "y", "c"))

    out, flying = {}, {}

    def halves(grads):
        return [g.reshape(4, 2, -1, g.shape[-1]) for g in grads]

    def sibling_start(grads, names, l, tag):
        gs = halves(grads)
        lands = [lax.empty((4,) + g.shape[2:], g.dtype) for g in gs]
        flying["s" + tag] = (names, l) + tuple(
            _exchange_start(gs, lands, _sibling_copies, 4 * len(gs), f"grads_to_sibling_start_{tag}"))
        return (flying["s" + tag][-1],)

    def chips_start(gs, from_sib, names, l, tag):
        cs = [_add_half(g, r, c_arr, f"add_half_{n}_{l}") for n, g, r in zip(names, gs, from_sib)]
        lands = [lax.empty((3,) + a.shape[1:], a.dtype) for a in cs]
        flying[tag] = (names, l) + tuple(_exchange_start(cs, lands, _chips_copies, 3 * len(cs), f"grads_to_chips_start_{tag}"))
        return (flying[tag][-1],)

    def sibling_finish(tag, after):
        names, l, send, recv, gs, lands, _ = flying["s" + tag]
        gs, from_sib = _exchange_wait(send, recv, gs, lands, after, _sibling_copies, f"grads_to_sibling_wait_{tag}")
        return chips_start(gs, from_sib, names, l, tag)

    def reduce_start(grads, names, l, tag):
        gs = halves(grads)
        return chips_start(gs, _to_sibling_halves(gs, tag), names, l, tag)

    def reduce_finish(tag, after):
        names, l, send, recv, cs, lands, _ = flying[tag]
        cs, lands = _exchange_wait(send, recv, cs, lands, after, _chips_copies, f"grads_to_chips_wait_{tag}")
        ts = [_sum_chips(cc, r3, place_arr, f"sum_chips_{n}_{l}") for n, cc, r3 in zip(names, cs, lands)]
        for n, j in zip(names, _join_halves(ts, tag)):
            out[n] = _adam_layer(j.reshape(w[n].shape[1:]), w[n], mom[n], var[n], l, out.get(n), f"adam_{n}_{l}")
        return out[names[-1]][0]

    def late1(grads):
        return sibling_finish("1a", grads[0]) + sibling_start(grads, first, 1, "1b")

    def midway0(grads):
        reduce_finish("1a", grads[0])
        reduce_finish("1b", grads[0])
        return reduce_start(grads, rest, 0, "0a")

    dxa, big1, small1 = _backward_layer(1, dxb, saved1, midway=lambda grads: sibling_start(grads, rest, 1, "1a"), late=late1)
    dx, big0, small0 = _backward_layer(0, dxa, saved0, after=sibling_finish("1b", dxa), midway=midway0,
                                       late=lambda grads: reduce_start(grads, first, 0, "0b"))
    reduce_finish("0b", reduce_finish("0a", dx))
    small = {k: jnp.stack([small0[k], small1[k]]) for k in LAYER_SMALL}
    small["final_g"] = dfg[0]

    full_shapes = [small[n].shape for n in SMALL]
    red = _unpack(_allreduce_small(_pack([small[n] for n in SMALL], 32)), full_shapes)
    g_small = []
    for n, g in zip(SMALL, red):
        if n in CHIP_SHARDED_SMALL:
            g = lax.dynamic_slice_in_dim(g, chip * w[n].shape[-1], w[n].shape[-1], axis=g.ndim - 1)
        g_small.append(g)
    shapes = [w[n].shape for n in SMALL]
    packs = [_pack(lst, 32) for lst in (g_small, [w[n] for n in SMALL], [mom[n] for n in SMALL], [var[n] for n in SMALL])]
    upd = [_unpack(u, shapes) for u in _adam(*packs, "adam_small")]
    for i, n in enumerate(SMALL):
        out[n] = [g_small[i], upd[0][i], upd[1][i], upd[2][i]]

    return (loss, dx[None]) + tuple(out[n][i] for i in range(4) for n in WEIGHTS)
```

```python
import functools

import jax
import jax.numpy as jnp
from jax import lax
from jax.experimental import pallas as pl
from jax.experimental.pallas import tpu as pltpu

F32 = jnp.float32
BF16 = jnp.bfloat16
MESH = pl.DeviceIdType.MESH

D = 1024
NH = 8
HD = 128
CHUNK = 128
N_IN_T = 12
DFF = 2816
DFF_SH = 1408
EPS = 1e-6
LRU_C = 8.0
ADAM_LR, ADAM_B1, ADAM_B2, ADAM_EPS, ADAM_WD, ADAM_STEP = 0.001, 0.9, 0.999, 1e-08, 0.01, 10

TM = 512
TM_BIG = 1024
RT = 128
PADR = 8
VMEM_LIMIT = 56 * 1024 * 1024


def _cp(sem=None, **kw):
    if sem is not None:
        kw["dimension_semantics"] = sem
    return pltpu.CompilerParams(vmem_limit_bytes=VMEM_LIMIT, **kw)


_GC = 0.7978845608028654


def _sigmoid(x):
    return 1.0 / (1.0 + jnp.exp(-x))


def _gelu(x):
    return 0.5 * x * (1.0 + jnp.tanh(_GC * (x + 0.044715 * x * x * x)))


def _gelu_and_grad(x):
    t = jnp.tanh(_GC * (x + 0.044715 * x * x * x))
    g = 0.5 * x * (1.0 + t)
    dg = 0.5 * (1.0 + t) + 0.5 * x * (1.0 - t * t) * _GC * (1.0 + 3 * 0.044715 * x * x)
    return g, dg


def _softplus_neg(lam):
    y = jnp.exp(-jnp.abs(lam))
    u = 1.0 + y
    l1p = jnp.where(u == 1.0, y, jnp.log(u) * y / (u - 1.0))
    return jnp.maximum(-lam, 0.0) + l1p


def _dot(a, b):
    return jnp.dot(a, b, preferred_element_type=F32)


def _dot_nt(a, b):
    return lax.dot_general(a, b, (((1,), (1,)), ((), ())), preferred_element_type=F32)


def _dot_tn(a, b):
    return lax.dot_general(a, b, (((0,), (0,)), ((), ())), preferred_element_type=F32)


def _rms_hat(x):
    r = lax.rsqrt(jnp.mean(x * x, axis=-1, keepdims=True) + EPS)
    return x * r, r


def _rms_bwd(dh, x, g):
    xh, r = _rms_hat(x)
    dxh = dh * g
    dx = r * (dxh - xh * jnp.mean(dxh * xh, axis=-1, keepdims=True))
    return dx, jnp.sum(dh * xh, axis=0, keepdims=True)


def _norm_into(x_ref, g_ref, h_ref):
    xh, _ = _rms_hat(x_ref[...])
    h_ref[...] = (xh * g_ref[...]).astype(BF16)


def _in_tile(j):
    m, hf = j // 2, j % 2
    orig = jnp.where(m < 2, m, jnp.where(m == 2, 4, jnp.where(m < 5, m - 1, 5)))
    t = orig * 2 + hf
    return t // 3, t % 3


ANY = pl.BlockSpec(memory_space=pl.ANY)


def _mm_in(x, g, w_in, l, after=()):
    S = x.shape[0]
    tm = min(2 * TM_BIG, S)

    def body(x_ref, g_ref, w0_ref, w1_ref, *rest):
        o_ref, h_ref = rest[-2:]

        @pl.when(pl.program_id(1) == 0)
        def _():
            _norm_into(x_ref, g_ref, h_ref)
        hv = h_ref[...]
        o_ref[:, 0:512] = _dot(hv, w0_ref[...]).astype(BF16)
        o_ref[:, 512:1024] = _dot(hv, w1_ref[...]).astype(BF16)

    def w_tile(hf):
        def w_map(i, m):
            sh, tl = _in_tile(2 * m + hf)
            return (sh, 0, tl)
        return pl.BlockSpec((None, D, 512), w_map)

    return pl.pallas_call(
        body, name=f"mm_in_{l}", grid=(S // tm, 6),
        in_specs=[pl.BlockSpec((tm, D), lambda i, m: (i, 0)), pl.BlockSpec((1, D), lambda i, m: (0, 0)),
                  w_tile(0), w_tile(1)] + [ANY] * len(after),
        out_specs=[pl.BlockSpec((None, tm, D), lambda i, m: (m, i, 0)), pl.BlockSpec((tm, D), lambda i, m: (i, 0))],
        out_shape=[jax.ShapeDtypeStruct((6, S, D), BF16), jax.ShapeDtypeStruct((S, D), BF16)],
        compiler_params=_cp(("parallel", "arbitrary")),
    )(x, g, w_in, w_in, *after)


def _mm_res(a, w, res, l, name, after=()):
    S, K = a.shape

    def body(a_ref, w_ref, r_ref, *rest):
        rest[-1][...] = r_ref[...] + _dot(a_ref[...], w_ref[...])

    tm = TM
    return pl.pallas_call(
        body, name=f"{name}_{l}", grid=(S // tm,),
        in_specs=[pl.BlockSpec((tm, K), lambda i: (i, 0)), pl.BlockSpec((K, D), lambda i: (0, 0)),
                  pl.BlockSpec((tm, D), lambda i: (i, 0))] + [ANY] * len(after),
        out_specs=pl.BlockSpec((tm, D), lambda i: (i, 0)),
        out_shape=jax.ShapeDtypeStruct((S, D), F32),
        compiler_params=_cp(("parallel",)),
    )(a, w, res, *after)


def _mm_ffn_in(x, g, w_fi, l):
    S = x.shape[0]

    tm = min(TM_BIG, S) if l else TM

    def body(x_ref, g_ref, w_ref, gu_ref, ff_ref, h_ref):
        @pl.when(pl.program_id(1) == 0)
        def _():
            _norm_into(x_ref, g_ref, h_ref)
        for r0 in range(0, tm, TM):
            rows = slice(r0, r0 + TM)
            hv = h_ref[rows, :]
            ga = _dot(hv, w_ref[0])
            gb = _dot(hv, w_ref[1])
            gu_ref[0, rows, :] = ga.astype(BF16)
            gu_ref[1, rows, :] = gb.astype(BF16)
            ff_ref[rows, :] = (ga * _sigmoid(ga) * gb).astype(BF16)

    gu, ff, h = pl.pallas_call(
        body, name=f"mm_ffn_in_{l}", grid=(S // tm, 2),
        in_specs=[pl.BlockSpec((tm, D), lambda i, s: (i, 0)), pl.BlockSpec((1, D), lambda i, s: (0, 0)),
                  pl.BlockSpec((2, None, D, DFF_SH), lambda i, s: (0, s, 0, 0))],
        out_specs=[pl.BlockSpec((2, None, tm, DFF_SH), lambda i, s: (0, s, i, 0)),
                   pl.BlockSpec((tm, DFF_SH), lambda i, s: (i, s)),
                   pl.BlockSpec((tm, D), lambda i, s: (i, 0))],
        out_shape=[jax.ShapeDtypeStruct((2, 2, S, DFF_SH), BF16), jax.ShapeDtypeStruct((S, DFF), BF16),
                   jax.ShapeDtypeStruct((S, D), BF16)],
        compiler_params=_cp(("parallel", "arbitrary")),
    )(x, g, w_fi.reshape(2, 2, D, DFF_SH))
    return gu.reshape(4, S, DFF_SH), ff, h


def _gmlp_fwd(z6, ws_b, bs_b, lg, lb):
    S = z6.shape[1]

    def body(z_ref, ws_ref, bs_ref, lg_ref, lb_ref, o_ref, mix):
        gv = _gelu(z_ref[1].astype(F32))
        xc = gv - jnp.mean(gv, axis=-1, keepdims=True)
        rs = lax.rsqrt(jnp.mean(xc * xc, axis=-1, keepdims=True) + EPS)
        vb = (xc * rs * lg_ref[...] + lb_ref[...]).astype(BF16)
        for gi in range(NH):
            cs = slice(gi * HD, (gi + 1) * HD)
            mix[:, cs] = _dot(ws_ref[gi], vb[:, cs])
        o_ref[...] = (_sigmoid(z_ref[2].astype(F32)) * _gelu(z_ref[0].astype(F32)) * (mix[...] + bs_ref[...])).astype(BF16)

    return pl.pallas_call(
        body, name="gmlp_fwd", grid=(S // CHUNK,),
        in_specs=[pl.BlockSpec((3, CHUNK, D), lambda i: (0, i, 0)), pl.BlockSpec((NH, CHUNK, CHUNK), lambda i: (0, 0, 0)),
                  pl.BlockSpec((CHUNK, D), lambda i: (0, 0)), pl.BlockSpec((1, D), lambda i: (0, 0)),
                  pl.BlockSpec((1, D), lambda i: (0, 0))],
        out_specs=pl.BlockSpec((CHUNK, D), lambda i: (i, 0)),
        out_shape=jax.ShapeDtypeStruct((S, D), BF16),
        scratch_shapes=[pltpu.VMEM((CHUNK, D), F32)],
        compiler_params=_cp(("parallel",)),
    )(z6, ws_b, bs_b, lg, lb)


def _row_iota():
    return lax.broadcasted_iota(jnp.int32, (RT, HD), 0)


SUB = 8
UNROLL = 4
GRAD_ROWS = 256


def _scan_up(a, b, carry):
    row = lax.broadcasted_iota(jnp.int32, (SUB, HD), 0)
    masks = [(d, row >= d) for d in (1, 2, 4)]
    c = jnp.broadcast_to(carry, (SUB, HD))
    hs = []
    for j in range(RT // SUB):
        aj, bj = a[SUB * j:SUB * (j + 1)], b[SUB * j:SUB * (j + 1)]
        for d, m in masks:
            bj = bj + aj * jnp.where(m, pltpu.roll(bj, d, 0), 0.0)
            aj = aj * jnp.where(m, pltpu.roll(aj, d, 0), 1.0)
        h = bj + aj * c
        hs.append(h)
        c = jnp.broadcast_to(h[SUB - 1:SUB, :], (SUB, HD))
    return jnp.concatenate(hs, axis=0), hs[-1][SUB - 1:SUB, :]


def _scan_down(a, b, carry):
    row = lax.broadcasted_iota(jnp.int32, (SUB, HD), 0)
    masks = [(d, row < SUB - d) for d in (1, 2, 4)]
    c = jnp.broadcast_to(carry, (SUB, HD))
    hs = []
    for j in reversed(range(RT // SUB)):
        aj, bj = a[SUB * j:SUB * (j + 1)], b[SUB * j:SUB * (j + 1)]
        for d, m in masks:
            bj = bj + aj * jnp.where(m, pltpu.roll(bj, SUB - d, 0), 0.0)
            aj = aj * jnp.where(m, pltpu.roll(aj, SUB - d, 0), 1.0)
        h = bj + aj * c
        hs.append(h)
        c = jnp.broadcast_to(h[0:1, :], (SUB, HD))
    return jnp.concatenate(hs[::-1], axis=0), hs[-1][0:1, :]


def _decay(r, sp_d):
    log_a = -LRU_C * r * sp_d
    a = jnp.exp(log_a)
    return a, jnp.sqrt(jnp.maximum(-jnp.tanh(log_a) * (a * a + 1.0), 0.0))


def _lru_gates(xc, d, wr_ref, br_ref, wi_ref, bi_ref, sp):
    xb = xc.astype(BF16)
    r = _sigmoid(_dot(xb, wr_ref[d]) + br_ref[d:d + 1, :])
    i = _sigmoid(_dot(xb, wi_ref[d]) + bi_ref[d:d + 1, :])
    a, mult = _decay(r, sp[d:d + 1, :])
    return r, i, a, mult


def _shifted(win, k):
    w = RT + 2 * PADR
    v = win if k == 0 else pltpu.roll(win, (-k) % w, 0)
    return v[PADR:PADR + RT]


def _conv_taps(win):
    return [_shifted(win, k) for k in (-1, 0, 1, 2)]


def _fill_padded(dst, src_ref, S):
    zeros = jnp.zeros((PADR, HD), F32)
    dst[0:PADR, :] = zeros
    dst[PADR + S:2 * PADR + S, :] = zeros

    def cp(i, c):
        t0 = pl.multiple_of(i * RT, RT)
        dst[pl.ds(t0 + PADR, RT), :] = src_ref[pl.ds(t0, RT), :].astype(F32)
        return c
    lax.fori_loop(0, S // RT, cp, 0)


def _conv_fwd_all(zxp, xc_s, cw_ref, cb_ref, S):
    def cv(i, c):
        t0 = pl.multiple_of(i * RT, RT)
        xm1, x0, xp1, xp2 = _conv_taps(zxp[pl.ds(t0, RT + 2 * PADR), :])
        xc_s[pl.ds(t0, RT), :] = (cb_ref[...] + xm1 * cw_ref[0:1, :] + x0 * cw_ref[1:2, :]
                                  + xp1 * cw_ref[2:3, :] + xp2 * cw_ref[3:4, :])
        return c
    lax.fori_loop(0, S // RT, cv, 0)


def _lru_specs(S):
    head = lambda h: (0, h)
    return [pl.BlockSpec((4, HD), head), pl.BlockSpec((1, HD), head),
            pl.BlockSpec((2, None, HD, HD), lambda h: (0, h, 0, 0)), pl.BlockSpec((2, HD), head),
            pl.BlockSpec((2, None, HD, HD), lambda h: (0, h, 0, 0)), pl.BlockSpec((2, HD), head),
            pl.BlockSpec((2, HD), head)]


def _lru_fwd(z6, ya, cw, cb, wr, br, wi, bi, lam):
    S = z6.shape[1]
    nt = S // RT

    def body(z_ref, ya_ref, cw_ref, cb_ref, wr_ref, br_ref, wi_ref, bi_ref, lam_ref, mg_ref, h0_ref, h1_ref, zxp, xc_s):
        sp = _softplus_neg(lam_ref[...])
        _fill_padded(zxp, z_ref.at[0], S)
        _conv_fwd_all(zxp, xc_s, cw_ref, cb_ref, S)

        def scans(i, carry):
            cu, cd = carry
            for u in range(UNROLL):
                j = i * UNROLL + u
                ru = pl.ds(pl.multiple_of(j * RT, RT), RT)
                rd = pl.ds(pl.multiple_of((nt - 1 - j) * RT, RT), RT)
                xu, xd = xc_s[ru, :], xc_s[rd, :]
                _, gi, a, mult = _lru_gates(xu, 0, wr_ref, br_ref, wi_ref, bi_ref, sp)
                hu, cu = _scan_up(a, mult * gi * xu, cu)
                h0_ref[ru, :] = hu
                _, gi, a, mult = _lru_gates(xd, 1, wr_ref, br_ref, wi_ref, bi_ref, sp)
                hd, cd = _scan_down(a, mult * gi * xd, cd)
                h1_ref[rd, :] = hd
            return cu, cd
        z1 = jnp.zeros((1, HD), F32)
        lax.fori_loop(0, nt // UNROLL, scans, (z1, z1))

        def merge(i, c):
            rows = pl.ds(pl.multiple_of(i * RT, RT), RT)
            yb = (h0_ref[rows, :] + h1_ref[rows, :]) * _gelu(z_ref[1, rows, :].astype(F32))
            mg_ref[rows, :] = (ya_ref[rows, :].astype(F32) + _sigmoid(z_ref[2, rows, :].astype(F32)) * yb).astype(BF16)
            return c
        lax.fori_loop(0, nt, merge, 0)

    col = pl.BlockSpec((S, HD), lambda h: (0, h))
    return pl.pallas_call(
        body, name="lru_fwd", grid=(NH,),
        in_specs=[pl.BlockSpec((3, S, HD), lambda h: (1, 0, h)), col] + _lru_specs(S),
        out_specs=[col, col, col],
        out_shape=[jax.ShapeDtypeStruct((S, D), BF16), jax.ShapeDtypeStruct((S, D), F32), jax.ShapeDtypeStruct((S, D), F32)],
        scratch_shapes=[pltpu.VMEM((S + 2 * PADR, HD), F32), pltpu.VMEM((S, HD), F32)],
        compiler_params=_cp(("parallel",)),
    )(z6, ya, cw, cb, wr, br, wi, bi, lam)


def _loss_head(x, tgt, g):
    S = x.shape[0]

    def body(x_ref, t_ref, g_ref, dx_ref, loss_ref, dg_ref):
        @pl.when(pl.program_id(0) == 0)
        def _():
            loss_ref[...] = jnp.zeros_like(loss_ref)
            dg_ref[...] = jnp.zeros_like(dg_ref)
        xv = x_ref[...]
        xh, _ = _rms_hat(xv)
        e = xh * g_ref[...] - t_ref[...]
        loss_ref[...] += jnp.sum(e * e) * (0.5 / D)
        dx, dgs = _rms_bwd(e * (1.0 / D), xv, g_ref[...])
        dx_ref[...] = dx
        dg_ref[...] += dgs

    return pl.pallas_call(
        body, name="loss_head", grid=(S // TM,),
        in_specs=[pl.BlockSpec((TM, D), lambda i: (i, 0)), pl.BlockSpec((TM, D), lambda i: (i, 0)),
                  pl.BlockSpec((1, D), lambda i: (0, 0))],
        out_specs=[pl.BlockSpec((TM, D), lambda i: (i, 0)), pl.BlockSpec((1, 128), lambda i: (0, 0)),
                   pl.BlockSpec((1, D), lambda i: (0, 0))],
        out_shape=[jax.ShapeDtypeStruct((S, D), F32), jax.ShapeDtypeStruct((1, 128), F32), jax.ShapeDtypeStruct((1, D), F32)],
        compiler_params=_cp(("arbitrary",)),
    )(x, tgt, g)


def _bwd_ffn_out(dx, w_fo, gu, l, after=()):
    S = dx.shape[0]

    tm = min(TM_BIG, S)

    def body(dx_ref, w_ref, gu_ref, *rest):
        o_ref = rest[-1]
        for r0 in range(0, tm, TM):
            rows = slice(r0, r0 + TM)
            d = _dot_nt(dx_ref[rows, :].astype(BF16), w_ref[...])
            ga, gb = gu_ref[0, rows, :].astype(F32), gu_ref[1, rows, :].astype(F32)
            sg = _sigmoid(ga)
            o_ref[0, rows, :] = (d * gb * sg * (1.0 + ga * (1.0 - sg))).astype(BF16)
            o_ref[1, rows, :] = (d * ga * sg).astype(BF16)

    pair = pl.BlockSpec((2, None, tm, DFF_SH), lambda i, s: (0, s, i, 0))
    dgu = pl.pallas_call(
        body, name=f"bwd_ffn_out_{l}", grid=(S // tm, 2),
        in_specs=[pl.BlockSpec((tm, D), lambda i, s: (i, 0)), pl.BlockSpec((DFF_SH, D), lambda i, s: (s, 0)), pair]
        + [ANY] * len(after),
        out_specs=pair,
        out_shape=jax.ShapeDtypeStruct((2, 2, S, DFF_SH), BF16),
        compiler_params=_cp(("parallel", "arbitrary")),
    )(dx, w_fo, gu.reshape(2, 2, S, DFF_SH), *after)
    return dgu.reshape(4, S, DFF_SH)


def _mm_tn(a, b, m_blk, tk, name):
    S, M = a.shape

    def body(a_ref, b_ref, o_ref):
        @pl.when(pl.program_id(1) == 0)
        def _():
            o_ref[...] = jnp.zeros_like(o_ref)
        o_ref[...] += _dot_tn(a_ref[...], b_ref[...].astype(BF16))

    return pl.pallas_call(
        body, name=name, grid=(M // m_blk, S // tk),
        in_specs=[pl.BlockSpec((tk, m_blk), lambda m, k: (k, m)), pl.BlockSpec((tk, D), lambda m, k: (k, 0))],
        out_specs=pl.BlockSpec((m_blk, D), lambda m, k: (m, 0)),
        out_shape=jax.ShapeDtypeStruct((M, D), F32),
        compiler_params=_cp(("parallel", "arbitrary")),
    )(a, b)


def _mm_nt_rms_bwd(a, a_specs, w, w_specs, nk, tm, x, g, dres, name, after=()):
    S = x.shape[0]
    sub = len(a_specs)

    def body(*refs):
        a_refs, w_refs = refs[:sub], refs[sub:2 * sub]
        x_ref, g_ref, r_ref = refs[2 * sub:2 * sub + 3]
        dx_ref, dg_ref, acc = refs[-3:]
        i, k = pl.program_id(0), pl.program_id(1)
        part = _dot_nt(a_refs[0][...], w_refs[0][...])
        for j in range(1, sub):
            part = part + _dot_nt(a_refs[j][...], w_refs[j][...])

        @pl.when(k == 0)
        def _():
            acc[...] = jnp.zeros_like(acc)
        acc[...] += part

        @pl.when(jnp.logical_and(i == 0, k == 0))
        def _():
            dg_ref[...] = jnp.zeros_like(dg_ref)

        @pl.when(k == nk - 1)
        def _():
            dx, dgs = _rms_bwd(acc[...], x_ref[...], g_ref[...])
            dx_ref[...] = r_ref[...] + dx
            dg_ref[...] += dgs

    row = pl.BlockSpec((tm, D), lambda i, k: (i, 0))
    vec = pl.BlockSpec((1, D), lambda i, k: (0, 0))
    return pl.pallas_call(
        body, name=name, grid=(S // tm, nk),
        in_specs=list(a_specs) + list(w_specs) + [row, vec, row] + [ANY] * len(after),
        out_specs=[row, vec],
        out_shape=[jax.ShapeDtypeStruct((S, D), F32), jax.ShapeDtypeStruct((1, D), F32)],
        scratch_shapes=[pltpu.VMEM((tm, D), F32)],
        compiler_params=_cp(("arbitrary", "arbitrary")),
    )(*[a] * sub, *[w] * sub, x, g, dres, *after)


def _dw_ffn_in(h, dgu, l):
    S = h.shape[0]

    def body(h_ref, b_ref, o_ref):
        @pl.when(pl.program_id(1) == 0)
        def _():
            o_ref[...] = jnp.zeros_like(o_ref)
        o_ref[...] += _dot_tn(h_ref[...], b_ref[...])

    tk = min(2 * TM_BIG, S)
    return pl.pallas_call(
        body, name=f"dw_ffn_in_{l}", grid=(4, S // tk),
        in_specs=[pl.BlockSpec((tk, D), lambda j, k: (k, 0)), pl.BlockSpec((None, tk, DFF_SH), lambda j, k: (j, k, 0))],
        out_specs=pl.BlockSpec((None, D, DFF_SH), lambda j, k: (j, 0, 0)),
        out_shape=jax.ShapeDtypeStruct((4, D, DFF_SH), F32),
        compiler_params=_cp(("parallel", "arbitrary")),
    )(h, dgu)


_HALF_COMPS = ((0, 1, 3), (4, 2, 5))


def _dw_in(h, dz6, l):
    S = h.shape[0]

    def body(h_ref, d0_ref, d1_ref, d2_ref, o_ref):
        @pl.when(pl.program_id(1) == 0)
        def _():
            o_ref[...] = jnp.zeros_like(o_ref)
        hv = h_ref[...]
        for q, d_ref in enumerate((d0_ref, d1_ref, d2_ref)):
            for hf in range(2):
                col = 1024 * q + 512 * hf
                o_ref[col // 1536, :, col % 1536:col % 1536 + 512] += _dot_tn(hv, d_ref[:, 512 * hf:512 * (hf + 1)])

    tk = min(TM_BIG, S)

    def comp(q):
        return pl.BlockSpec((None, tk, D), lambda p, k: (jnp.where(p == 0, _HALF_COMPS[0][q], _HALF_COMPS[1][q]), k, 0))

    return pl.pallas_call(
        body, name=f"dw_in_{l}", grid=(2, S // tk),
        in_specs=[pl.BlockSpec((tk, D), lambda p, k: (k, 0)), comp(0), comp(1), comp(2)],
        out_specs=pl.BlockSpec((2, D, 1536), lambda p, k: (p, 0, 0)),
        out_shape=jax.ShapeDtypeStruct((4, D, 1536), F32),
        compiler_params=_cp(("parallel", "arbitrary")),
    )(h, dz6, dz6, dz6)


def _bwd_out(dx, w_o, merged, l):
    S = dx.shape[0]

    def body(dx_ref, w_ref, m_ref, dm_ref, dw_ref):
        @pl.when(pl.program_id(0) == 0)
        def _():
            dw_ref[...] = jnp.zeros_like(dw_ref)
        dxb = dx_ref[...].astype(BF16)
        dm_ref[...] = _dot_nt(dxb, w_ref[...]).astype(BF16)
        dw_ref[...] += _dot_tn(m_ref[...], dxb)

    tm = TM
    row = pl.BlockSpec((tm, D), lambda i: (i, 0))
    return pl.pallas_call(
        body, name=f"bwd_out_{l}", grid=(S // tm,),
        in_specs=[row, pl.BlockSpec((D, D), lambda i: (0, 0)), row],
        out_specs=[row, pl.BlockSpec((D, D), lambda i: (0, 0))],
        out_shape=[jax.ShapeDtypeStruct((S, D), BF16), jax.ShapeDtypeStruct((D, D), F32)],
        compiler_params=_cp(("arbitrary",)),
    )(dx, w_o, merged)


def _gmlp_bwd(dm, z6, ws_b, wst_b, bs_b, lg, lb, after=()):
    S = z6.shape[1]

    def body(dm_ref, z_ref, ws_ref, wst_ref, bs_ref, lg_ref, lb_ref, *rest):
        dz_ref, dws_ref, dbs_ref, dlg_ref, dlb_ref, mix, dv = rest[-7:]

        @pl.when(pl.program_id(0) == 0)
        def _():
            dws_ref[...] = jnp.zeros_like(dws_ref)
            dbs_ref[...] = jnp.zeros_like(dbs_ref)
            dlg_ref[...] = jnp.zeros_like(dlg_ref)
            dlb_ref[...] = jnp.zeros_like(dlb_ref)
        gv, dgelu_v = _gelu_and_grad(z_ref[1].astype(F32))
        xc = gv - jnp.mean(gv, axis=-1, keepdims=True)
        rs = lax.rsqrt(jnp.mean(xc * xc, axis=-1, keepdims=True) + EPS)
        vh = xc * rs
        vb = (vh * lg_ref[...] + lb_ref[...]).astype(BF16)
        for gi in range(NH):
            cs = slice(gi * HD, (gi + 1) * HD)
            mix[:, cs] = _dot(ws_ref[gi], vb[:, cs])
        u, dgelu_u = _gelu_and_grad(z_ref[0].astype(F32))
        sa = _sigmoid(z_ref[2].astype(F32))
        mixed = mix[...] + bs_ref[...]
        dyg = dm_ref[...].astype(F32)
        dz_ref[2] = (dyg * u * mixed * sa * (1.0 - sa)).astype(BF16)
        dya = dyg * sa
        dz_ref[0] = (dya * mixed * dgelu_u).astype(BF16)
        dmix = dya * u
        dmb = dmix.astype(BF16)
        for gi in range(NH):
            cs = slice(gi * HD, (gi + 1) * HD)
            dv[:, cs] = _dot(wst_ref[gi], dmb[:, cs])
            dws_ref[gi] += _dot_nt(dmb[:, cs], vb[:, cs])
            dbs_ref[gi] += jnp.broadcast_to(jnp.sum(dmix[:, cs], axis=1, keepdims=True), (CHUNK, HD))
        dvv = dv[...]
        dlg_ref[...] += jnp.sum(dvv * vh, axis=0, keepdims=True)
        dlb_ref[...] += jnp.sum(dvv, axis=0, keepdims=True)
        dvh = dvv * lg_ref[...]
        dgv = rs * (dvh - jnp.mean(dvh, axis=-1, keepdims=True) - vh * jnp.mean(dvh * vh, axis=-1, keepdims=True))
        dz_ref[1] = (dgv * dgelu_v).astype(BF16)

    vec = pl.BlockSpec((1, D), lambda i: (0, 0))
    mat = pl.BlockSpec((NH, CHUNK, CHUNK), lambda i: (0, 0, 0))
    return pl.pallas_call(
        body, name="gmlp_bwd", grid=(S // CHUNK,),
        in_specs=[pl.BlockSpec((CHUNK, D), lambda i: (i, 0)), pl.BlockSpec((3, CHUNK, D), lambda i: (0, i, 0)), mat, mat,
                  pl.BlockSpec((CHUNK, D), lambda i: (0, 0)), vec, vec] + [ANY] * len(after),
        out_specs=[pl.BlockSpec((3, CHUNK, D), lambda i: (0, i, 0)), mat, mat, vec, vec],
        out_shape=[jax.ShapeDtypeStruct((6, S, D), BF16), jax.ShapeDtypeStruct((NH, CHUNK, CHUNK), F32),
                   jax.ShapeDtypeStruct((NH, CHUNK, HD), F32), jax.ShapeDtypeStruct((1, D), F32), jax.ShapeDtypeStruct((1, D), F32)],
        scratch_shapes=[pltpu.VMEM((CHUNK, D), F32), pltpu.VMEM((CHUNK, D), F32)],
        compiler_params=_cp(("arbitrary",)),
    )(dm, z6, ws_b, wst_b, bs_b, lg, lb, *after)


def _lru_bwd(dz6, dm, z6, h0, h1, cw, cb, wr, br, wi, bi, lam):
    S = z6.shape[1]
    nt = S // RT

    def body(dz_in, dm_ref, z_ref, h0_ref, h1_ref, cw_ref, cb_ref, wr_ref, br_ref, wi_ref, bi_ref, lam_ref,
             dz_ref, dcw_ref, dcb_ref, dwr_ref, dbr_ref, dwi_ref, dbi_ref, dlam_ref, zxp, xc_s, dhs_s, dxcp, r_s, lam_s):
        del dz_in
        lam = lam_ref[...]
        sp = _softplus_neg(lam)
        row = _row_iota()
        _fill_padded(zxp, z_ref.at[0], S)
        _conv_fwd_all(zxp, xc_s, cw_ref, cb_ref, S)
        zeros = jnp.zeros((PADR, HD), F32)
        dxcp[0:PADR, :] = zeros
        dxcp[PADR + S:2 * PADR + S, :] = zeros
        dwr_ref[...] = jnp.zeros_like(dwr_ref)
        dwi_ref[...] = jnp.zeros_like(dwi_ref)

        def pre(i, c):
            rows = pl.ds(pl.multiple_of(i * RT, RT), RT)
            hs = h0_ref[rows, :] + h1_ref[rows, :]
            dmv = dm_ref[rows, :].astype(F32)
            sb = _sigmoid(z_ref[2, rows, :].astype(F32))
            gg, dgg = _gelu_and_grad(z_ref[1, rows, :].astype(F32))
            dz_ref[2, rows, :] = (dmv * hs * gg * sb * (1.0 - sb)).astype(BF16)
            dyb = dmv * sb
            dz_ref[1, rows, :] = (dyb * hs * dgg).astype(BF16)
            dhs_s[rows, :] = dyb * gg
            return c
        lax.fori_loop(0, nt, pre, 0)

        def gate_bwd(d, gates, lamv, da, xc):
            r, gi, a, mult = gates
            dmult = lamv * gi * xc
            dgi = lamv * mult * xc
            dlog = (da - dmult * a / mult) * a
            dpr = (dlog * (-LRU_C) * sp[d:d + 1, :]) * r * (1.0 - r)
            dpi = dgi * gi * (1.0 - gi)
            xb, dprb, dpib = xc.astype(BF16), dpr.astype(BF16), dpi.astype(BF16)
            dwr_ref[d] += _dot_tn(xb, dprb)
            dwi_ref[d] += _dot_tn(xb, dpib)
            dxc = lamv * mult * gi + _dot_nt(dprb, wr_ref[d]) + _dot_nt(dpib, wi_ref[d])
            return dxc, (jnp.sum(dlog * r, axis=0, keepdims=True) * (-LRU_C), jnp.sum(dpr, axis=0, keepdims=True),
                         jnp.sum(dpi, axis=0, keepdims=True))

        def rgates(i, c):
            for u in range(UNROLL):
                rows = pl.ds(pl.multiple_of((i * UNROLL + u) * RT, RT), RT)
                xb = xc_s[rows, :].astype(BF16)
                for d in range(2):
                    r_s[d, rows, :] = _sigmoid(_dot(xb, wr_ref[d]) + br_ref[d:d + 1, :])
            return c
        lax.fori_loop(0, nt // UNROLL, rgates, 0)

        def chains(i, carry):
            qn, qp = carry
            for u in range(UNROLL):
                j = i * UNROLL + u
                rd = pl.ds(pl.multiple_of((nt - 1 - j) * RT, RT), RT)
                a, dhs = _decay(r_s[0, rd, :], sp[0:1, :])[0], dhs_s[rd, :]
                q, q_first = _scan_down(a, a * dhs, qn)
                lam_s[0, rd, :] = dhs + jnp.where(row == RT - 1, qn, pltpu.roll(q, RT - 1, 0))
                qn = q_first
                ru = pl.ds(pl.multiple_of(j * RT, RT), RT)
                a, dhs = _decay(r_s[1, ru, :], sp[1:2, :])[0], dhs_s[ru, :]
                q, q_last = _scan_up(a, a * dhs, qp)
                lam_s[1, ru, :] = dhs + jnp.where(row == 0, qp, pltpu.roll(q, 1, 0))
                qp = q_last
            return qn, qp

        z1 = jnp.zeros((1, HD), F32)
        lax.fori_loop(0, nt // UNROLL, chains, (z1, z1))

        ct = min(GRAD_ROWS, S)
        crow = lax.broadcasted_iota(jnp.int32, (ct, HD), 0)

        def tile_grads(i, acc):
            t0 = pl.multiple_of(i * ct, ct)
            rows = pl.ds(t0, ct)
            xc = xc_s[rows, :]
            xb = xc.astype(BF16)
            tp = pl.multiple_of(jnp.maximum(t0 - PADR, 0), PADR)
            prev = jnp.where(t0 > 0, h0_ref[pl.ds(tp, PADR), :][PADR - 1:PADR, :], 0.0)
            tn = pl.multiple_of(jnp.minimum(t0 + ct, S - PADR), PADR)
            nxt = jnp.where(t0 + ct < S, h1_ref[pl.ds(tn, PADR), :][0:1, :], 0.0)
            hside = (jnp.where(crow == 0, prev, pltpu.roll(h0_ref[rows, :], 1, 0)),
                     jnp.where(crow == ct - 1, nxt, pltpu.roll(h1_ref[rows, :], ct - 1, 0)))
            dxc, sums = 0.0, ()
            for d in range(2):
                r = r_s[d, rows, :]
                gi = _sigmoid(_dot(xb, wi_ref[d]) + bi_ref[d:d + 1, :])
                a, mult = _decay(r, sp[d:d + 1, :])
                lamv = lam_s[d, rows, :]
                dxc_d, s_d = gate_bwd(d, (r, gi, a, mult), lamv, lamv * hside[d], xc)
                dxc = dxc + dxc_d
                sums = sums + s_d
            dxcp[pl.ds(t0 + PADR, ct), :] = dxc
            return tuple(x + y for x, y in zip(acc, sums))

        s_sp0, s_br0, s_bi0, s_sp1, s_br1, s_bi1 = lax.fori_loop(0, S // ct, tile_grads, (z1,) * 6)

        dsp = jnp.concatenate([s_sp0, s_sp1], axis=0)
        dlam_ref[...] = -dsp * _sigmoid(-lam)
        dbr_ref[...] = jnp.concatenate([s_br0, s_br1], axis=0)
        dbi_ref[...] = jnp.concatenate([s_bi0, s_bi1], axis=0)

        def conv_bwd(i, carry):
            c0, c1, c2, c3, cb_ = carry
            t0 = pl.multiple_of(i * RT, RT)
            dwin = dxcp[pl.ds(t0, RT + 2 * PADR), :]
            d0 = _shifted(dwin, 0)
            dz_ref[0, pl.ds(t0, RT), :] = (_shifted(dwin, 1) * cw_ref[0:1, :] + d0 * cw_ref[1:2, :]
                                           + _shifted(dwin, -1) * cw_ref[2:3, :] + _shifted(dwin, -2) * cw_ref[3:4, :]).astype(BF16)
            xm1, x0, xp1, xp2 = _conv_taps(zxp[pl.ds(t0, RT + 2 * PADR), :])
            sm = lambda v: jnp.sum(v, axis=0, keepdims=True)
            return c0 + sm(d0 * xm1), c1 + sm(d0 * x0), c2 + sm(d0 * xp1), c3 + sm(d0 * xp2), cb_ + sm(d0)

        c0, c1, c2, c3, cb_ = lax.fori_loop(0, nt, conv_bwd, (z1, z1, z1, z1, z1))
        dcw_ref[...] = jnp.concatenate([c0, c1, c2, c3], axis=0)
        dcb_ref[...] = cb_

    col = pl.BlockSpec((S, HD), lambda h: (0, h))
    head = lambda h: (0, h)
    wspec = pl.BlockSpec((2, None, HD, HD), lambda h: (0, h, 0, 0))
    return pl.pallas_call(
        body, name="lru_bwd", grid=(NH,),
        in_specs=[pl.BlockSpec(memory_space=pl.ANY), col, pl.BlockSpec((3, S, HD), lambda h: (1, 0, h)), col, col] + _lru_specs(S),
        out_specs=[pl.BlockSpec((3, S, HD), lambda h: (1, 0, h)), pl.BlockSpec((4, HD), head), pl.BlockSpec((1, HD), head),
                   wspec, pl.BlockSpec((2, HD), head), wspec, pl.BlockSpec((2, HD), head), pl.BlockSpec((2, HD), head)],
        out_shape=[jax.ShapeDtypeStruct((6, S, D), BF16), jax.ShapeDtypeStruct((4, D), F32), jax.ShapeDtypeStruct((1, D), F32),
                   jax.ShapeDtypeStruct((2, NH, HD, HD), F32), jax.ShapeDtypeStruct((2, D), F32),
                   jax.ShapeDtypeStruct((2, NH, HD, HD), F32), jax.ShapeDtypeStruct((2, D), F32), jax.ShapeDtypeStruct((2, D), F32)],
        scratch_shapes=[pltpu.VMEM((S + 2 * PADR, HD), F32), pltpu.VMEM((S, HD), F32), pltpu.VMEM((S, HD), F32),
                        pltpu.VMEM((S + 2 * PADR, HD), F32), pltpu.VMEM((2, S, HD), F32), pltpu.VMEM((2, S, HD), F32)],
        input_output_aliases={0: 0},
        compiler_params=_cp(("parallel",)),
    )(dz6, dm, z6, h0, h1, cw, cb, wr, br, wi, bi, lam)


LAYER_SMALL = ("norm1_g", "gmlp_ln_g", "gmlp_ln_b", "gmlp_w_s", "gmlp_b_s", "conv_w", "conv_b",
               "lru_w_r", "lru_b_r", "lru_w_i", "lru_b_i", "lru_lambda", "norm2_g")


def _forward_layer(l, x, p, wb, after=(), rest=None, near_end=None):
    g1, g2 = p["norm1_g"][l][None], p["norm2_g"][l][None]
    ws_b = p["gmlp_w_s"][l].astype(BF16)
    tm = dict(ws_b=ws_b, wst_b=jnp.swapaxes(ws_b, 1, 2), bs_b=jnp.repeat(p["gmlp_b_s"][l].T, HD, axis=1),
              lg=p["gmlp_ln_g"][l][None], lb=p["gmlp_ln_b"][l][None])
    lru = (p["conv_w"][l], p["conv_b"][l][None], p["lru_w_r"][l].astype(BF16), p["lru_b_r"][l],
           p["lru_w_i"][l].astype(BF16), p["lru_b_i"][l], p["lru_lambda"][l])
    z6, hn1 = _mm_in(x, g1, wb["w_in"], l, after)
    ya = _gmlp_fwd(z6, tm["ws_b"], tm["bs_b"], tm["lg"], tm["lb"])
    merged, h0, h1 = _lru_fwd(z6, ya, *lru)
    if rest is not None:
        wb = dict(wb, **rest(merged))
    x1 = _mm_res(merged, wb["w_out"], x, l, "mm_out")
    gu, ff, hn2 = _mm_ffn_in(x1, g2, wb["w_ffn_in"], l)
    x2 = _mm_res(ff, wb["w_ffn_out"], x1, l, "mm_ffn_out", () if near_end is None else tuple(near_end(gu)))
    return x2, dict(x=x, z6=z6, h0=h0, h1=h1, merged=merged, x1=x1, gu=gu, ff=ff, g1=g1, g2=g2, tm=tm, lru=lru,
                    hn1=hn1, hn2=hn2, wb=wb)


def _backward_layer(l, dx, s, after=(), midway=None, late=None):
    S = dx.shape[0]
    tm, wb = s["tm"], s["wb"]
    g2 = s["g2"]
    dgu = _bwd_ffn_out(dx, wb["w_ffn_out"], s["gu"], l, after)
    tmb = min(TM_BIG, S)
    dwfo = _mm_tn(s["ff"], dx, DFF_SH, tmb, f"dw_ffn_out_{l}")
    dx1, dg2 = _mm_nt_rms_bwd(
        dgu, [pl.BlockSpec((None, tmb, DFF_SH), lambda i, k: (k, i, 0))],
        wb["w_ffn_in"], [pl.BlockSpec((None, D, DFF_SH), lambda i, k: (k, 0, 0))],
        4, tmb, s["x1"], g2, dx, f"bwd_ffn_in_{l}")
    dwfi = _dw_ffn_in(s["hn2"], dgu, l)
    dmg, dwo = _bwd_out(dx1, wb["w_out"], s["merged"], l)
    mid = () if midway is None else tuple(midway([dwo, dwfi, dwfo]))
    dz6, dws, dbs, dlg, dlb = _gmlp_bwd(dmg, s["z6"], tm["ws_b"], tm["wst_b"], tm["bs_b"], tm["lg"], tm["lb"], mid)
    dz6, dcw, dcb, dwr, dbr, dwi, dbi, dlam = _lru_bwd(dz6, dmg, s["z6"], s["h0"], s["h1"], *s["lru"])

    sub = 3

    def dz_tile(j):
        return pl.BlockSpec((None, tmb, 512), lambda i, k: ((sub * k + j) // 2, i, (sub * k + j) % 2))

    def w_tile(j):
        def w_map(i, k):
            sh, tl = _in_tile(sub * k + j)
            return (sh, 0, tl)
        return pl.BlockSpec((None, D, 512), w_map)

    dwin = _dw_in(s["hn1"], dz6, l)
    tail = () if late is None else tuple(late([dwin]))
    dx0, dg1 = _mm_nt_rms_bwd(
        dz6, [dz_tile(j) for j in range(sub)], wb["w_in"], [w_tile(j) for j in range(sub)],
        N_IN_T // sub, tmb, s["x"], s["g1"], dx1, f"bwd_in_{l}", tail)
    small = dict(norm1_g=dg1[0], gmlp_ln_g=dlg[0], gmlp_ln_b=dlb[0], gmlp_w_s=dws, gmlp_b_s=dbs[:, :, 0], conv_w=dcw, conv_b=dcb[0],
                 lru_w_r=dwr, lru_b_r=dbr, lru_w_i=dwi, lru_b_i=dbi, lru_lambda=dlam, norm2_g=dg2[0])
    return dx0, [dwin, dwo, dwfi, dwfo], small


def _local_step(x, tgt, p, wbs):
    saved = []
    for l in range(2):
        x, s = _forward_layer(l, x, p, wbs[l])
        saved.append(s)
    dx, loss_v, dfg = _loss_head(x, tgt, p["final_g"][None])
    big, smalls = [None, None], [None, None]
    for l in (1, 0):
        dx, big[l], smalls[l] = _backward_layer(l, dx, saved[l])
    small = {k: jnp.stack([smalls[0][k], smalls[1][k]]) for k in LAYER_SMALL}
    small["final_g"] = dfg[0]
    return loss_v, dx, big, small


def _place():
    x, y, c = lax.axis_index("x"), lax.axis_index("y"), lax.axis_index("c")
    return x, y, c, 2 * x + y


def _chip_at(x, y, d):
    px = 1 - x if d & 2 else x
    py = 1 - y if d & 1 else y
    return px, py, 2 * px + py


HBM = pl.BlockSpec(memory_space=pltpu.HBM)
SEM = pl.BlockSpec(memory_space=pltpu.SEMAPHORE)
DATAFLOW = pltpu.SideEffectType.DATAFLOW_SIDE_EFFECTING


def _in_hbm(a):
    return pltpu.with_memory_space_constraint(a, pltpu.HBM)


def _cast_into(wf, l, chip_arr, name):
    _, rows, cols = wf.shape
    rh = rows // 2

    def body(ch_ref, w_ref, o_ref):
        o_ref[...] = w_ref[...].astype(BF16)

    return pl.pallas_call(
        body, name=name, out_shape=jax.ShapeDtypeStruct((4, 2, rh, cols), BF16),
        grid_spec=pltpu.PrefetchScalarGridSpec(
            num_scalar_prefetch=1, grid=(2,),
            in_specs=[pl.BlockSpec((None, None, rh, cols), lambda h, ch: (l, h, 0, 0))],
            out_specs=pl.BlockSpec((None, None, rh, cols), lambda h, ch: (ch[0], h, 0, 0))),
        compiler_params=_cp(("parallel",)),
    )(chip_arr, wf.reshape(2, 2, rh, cols))


def _half_block(ref, chip, half, to, send_sem, recv_sem):
    blk = ref.at[chip, half]
    return pltpu.make_async_remote_copy(src_ref=blk, dst_ref=blk, send_sem=send_sem, recv_sem=recv_sem,
                                        device_id=to, device_id_type=MESH)


def _gather_weights(bufs, tiny):
    nt = len(bufs)
    n_ici = max(nt * 3, 1)

    def body(*refs):
        tiny_ref = refs[nt]
        o_refs, tiny_o = refs[nt + 1:2 * nt + 1], refs[2 * nt + 1]
        send, recv, fsend, frecv, tsend, trecv, lsem = refs[2 * nt + 2:]
        x, y, c, chip = _place()
        local = pltpu.make_async_copy(tiny_ref, tiny_o.at[chip], lsem)
        local.start()

        def tin(d, origin_chip, to):
            return pltpu.make_async_remote_copy(
                src_ref=tiny_ref, dst_ref=tiny_o.at[origin_chip], send_sem=tsend.at[d - 1], recv_sem=trecv.at[d - 1],
                device_id=to, device_id_type=MESH)

        sends = []
        for t in range(nt):
            for d in (1, 2, 3):
                px, py, _ = _chip_at(x, y, d)
                sends.append(_half_block(o_refs[t], chip, c, (px, py, c), send.at[3 * t + d - 1], recv.at[3 * t + d - 1]))
        for d in (1, 2, 3):
            px, py, _ = _chip_at(x, y, d)
            sends.append(tin(d, chip, (px, py, c)))
        for cp in sends:
            cp.start()
        passed = []
        for t in range(nt):
            for d in (1, 2, 3):
                k = 3 * t + d - 1
                _, _, pchip = _chip_at(x, y, d)
                _half_block(o_refs[t], pchip, c, (x, y, c), send.at[k], recv.at[k]).wait_recv()
                f = _half_block(o_refs[t], pchip, c, (x, y, 1 - c), fsend.at[k], frecv.at[k])
                f.start()
                passed.append(f)
        for t in range(nt):
            for d in (1, 2, 3):
                k = 3 * t + d - 1
                _, _, pchip = _chip_at(x, y, d)
                _half_block(o_refs[t], pchip, 1 - c, (x, y, 1 - c), fsend.at[k], frecv.at[k]).wait_recv()
        for d in (1, 2, 3):
            _, _, pchip = _chip_at(x, y, d)
            tin(d, pchip, (x, y, c)).wait_recv()
        for cp in sends + passed:
            cp.wait_send()
        local.wait()

    out_shape = [jax.ShapeDtypeStruct(b.shape, b.dtype) for b in bufs]
    out_shape.append(jax.ShapeDtypeStruct((4,) + tiny.shape, tiny.dtype))
    outs = pl.pallas_call(
        body, name="gather_weights_0", out_shape=out_shape,
        in_specs=[ANY] * (nt + 1), out_specs=[ANY] * (nt + 1),
        scratch_shapes=[pltpu.SemaphoreType.DMA((n_ici,)), pltpu.SemaphoreType.DMA((n_ici,)),
                        pltpu.SemaphoreType.DMA((n_ici,)), pltpu.SemaphoreType.DMA((n_ici,)),
                        pltpu.SemaphoreType.DMA((3,)), pltpu.SemaphoreType.DMA((3,)), pltpu.SemaphoreType.DMA],
        input_output_aliases={t: t for t in range(nt)},
        compiler_params=_cp(has_side_effects=True),
    )(*bufs, tiny)
    return outs[:nt], outs[nt]


def _gather_start(bufs, tag, after=()):
    nt, na = len(bufs), len(after)

    def body(*refs):
        b_refs = refs[:nt]
        send, recv = refs[nt + na], refs[nt + na + 1]
        token = refs[2 * nt + na + 2]
        x, y, c, chip = _place()
        for t in range(nt):
            for d in (1, 2, 3):
                px, py, _ = _chip_at(x, y, d)
                _half_block(b_refs[t], chip, c, (px, py, c), send.at[3 * t + d - 1], recv.at[3 * t + d - 1]).start()
        token[...] = jnp.zeros_like(token)

    outs = pl.pallas_call(
        body, name=f"gather_start_{tag}",
        out_shape=(pltpu.SemaphoreType.DMA((3 * nt,)), pltpu.SemaphoreType.DMA((3 * nt,)),
                   *[pltpu.HBM(b.shape, b.dtype) for b in bufs], jax.ShapeDtypeStruct((8, 128), F32)),
        in_specs=[HBM] * nt + [ANY] * na, out_specs=(SEM, SEM, *[HBM] * nt, pl.BlockSpec(memory_space=pltpu.VMEM)),
        input_output_aliases={t: 2 + t for t in range(nt)},
        compiler_params=pltpu.CompilerParams(has_side_effects=DATAFLOW),
    )(*[_in_hbm(b) for b in bufs], *after)
    return outs[0], outs[1], list(outs[2:2 + nt]), outs[2 + nt]


def _gather_wait(send, recv, bufs, after, tag):
    nt = len(bufs)

    def body(*refs):
        b_refs = refs[:nt]
        send_ref, recv_ref = refs[nt], refs[nt + 1]
        x, y, c, chip = _place()
        for t in range(nt):
            for d in (1, 2, 3):
                k = 3 * t + d - 1
                px, py, pchip = _chip_at(x, y, d)
                _half_block(b_refs[t], chip, c, (px, py, c), send_ref.at[k], recv_ref.at[k]).wait_send()
                _half_block(b_refs[t], pchip, c, (px, py, c), send_ref.at[k], recv_ref.at[k]).wait_recv()

    outs = pl.pallas_call(
        body, name=f"gather_wait_{tag}", out_shape=[pltpu.HBM(b.shape, b.dtype) for b in bufs],
        in_specs=[HBM] * nt + [SEM, SEM, ANY], out_specs=[HBM] * nt,
        input_output_aliases={t: t for t in range(nt)},
        compiler_params=pltpu.CompilerParams(has_side_effects=DATAFLOW),
    )(*bufs, send, recv, after)
    return list(outs)


def _gather_pass_on(bufs, tag):
    nt = len(bufs)

    def body(*refs):
        o_refs = refs[nt:2 * nt]
        fsend, frecv = refs[2 * nt:]
        x, y, c, _ = _place()
        cps = []
        for t in range(nt):
            for d in (1, 2, 3):
                k = 3 * t + d - 1
                _, _, pchip = _chip_at(x, y, d)
                cps.append(_half_block(o_refs[t], pchip, c, (x, y, 1 - c), fsend.at[k], frecv.at[k]))
        for cp in cps:
            cp.start()
        for t in range(nt):
            for d in (1, 2, 3):
                k = 3 * t + d - 1
                _, _, pchip = _chip_at(x, y, d)
                _half_block(o_refs[t], pchip, 1 - c, (x, y, 1 - c), fsend.at[k], frecv.at[k]).wait_recv()
        for cp in cps:
            cp.wait_send()

    return pl.pallas_call(
        body, name=f"gather_pass_on_{tag}", out_shape=[jax.ShapeDtypeStruct(b.shape, b.dtype) for b in bufs],
        in_specs=[ANY] * nt, out_specs=[ANY] * nt,
        scratch_shapes=[pltpu.SemaphoreType.DMA((3 * nt,)), pltpu.SemaphoreType.DMA((3 * nt,))],
        input_output_aliases={t: t for t in range(nt)},
        compiler_params=_cp(has_side_effects=True),
    )(*bufs)


def _to_sibling_halves(gs, l):
    nt = len(gs)

    def body(*refs):
        g_refs, o_refs = refs[:nt], refs[nt:2 * nt]
        send, recv = refs[2 * nt:]
        x, y, c, _ = _place()
        cps = [pltpu.make_async_remote_copy(
            src_ref=g_refs[t].at[k, 1 - c], dst_ref=o_refs[t].at[k], send_sem=send.at[4 * t + k], recv_sem=recv.at[4 * t + k],
            device_id=(x, y, 1 - c), device_id_type=MESH) for t in range(nt) for k in range(4)]
        for cp in cps:
            cp.start()
        for cp in cps:
            cp.wait()

    return pl.pallas_call(
        body, name=f"grads_to_sibling_{l}", out_shape=[jax.ShapeDtypeStruct((4,) + g.shape[2:], g.dtype) for g in gs],
        in_specs=[ANY] * nt, out_specs=[ANY] * nt,
        scratch_shapes=[pltpu.SemaphoreType.DMA((4 * nt,)), pltpu.SemaphoreType.DMA((4 * nt,))],
        compiler_params=_cp(has_side_effects=True),
    )(*gs)


def _chip_copy(c_ref, land_ref, x, y, c, d, send_sem, recv_sem):
    px, py, pchip = _chip_at(x, y, d)
    return pltpu.make_async_remote_copy(src_ref=c_ref.at[pchip], dst_ref=land_ref.at[d - 1], send_sem=send_sem, recv_sem=recv_sem,
                                        device_id=(px, py, c), device_id_type=MESH)


def _exchange_start(srcs, lands, copies, nsem, name):
    ns, n = len(srcs), len(srcs) + len(lands)

    def body(*refs):
        for cp in copies(refs[:ns], refs[ns:n], refs[n], refs[n + 1]):
            cp.start()
        token = refs[2 * n + 2]
        token[...] = jnp.zeros_like(token)

    outs = pl.pallas_call(
        body, name=name,
        out_shape=(pltpu.SemaphoreType.DMA((nsem,)), pltpu.SemaphoreType.DMA((nsem,)),
                   *[pltpu.HBM(a.shape, a.dtype) for a in list(srcs) + list(lands)], jax.ShapeDtypeStruct((8, 128), F32)),
        in_specs=[HBM] * n, out_specs=(SEM, SEM, *[HBM] * n, pl.BlockSpec(memory_space=pltpu.VMEM)),
        input_output_aliases={i: 2 + i for i in range(n)},
        compiler_params=pltpu.CompilerParams(has_side_effects=DATAFLOW),
    )(*[_in_hbm(a) for a in list(srcs) + list(lands)])
    return outs[0], outs[1], list(outs[2:2 + ns]), list(outs[2 + ns:2 + n]), outs[2 + n]


def _exchange_wait(send, recv, srcs, lands, after, copies, name):
    ns, n = len(srcs), len(srcs) + len(lands)

    def body(*refs):
        for cp in copies(refs[:ns], refs[ns:n], refs[n], refs[n + 1]):
            cp.wait_send()
            cp.wait_recv()

    outs = pl.pallas_call(
        body, name=name, out_shape=[pltpu.HBM(a.shape, a.dtype) for a in list(srcs) + list(lands)],
        in_specs=[HBM] * n + [SEM, SEM, ANY], out_specs=[HBM] * n,
        input_output_aliases={i: i for i in range(n)},
        compiler_params=pltpu.CompilerParams(has_side_effects=DATAFLOW),
    )(*srcs, *lands, send, recv, after)
    return list(outs[:ns]), list(outs[ns:])


def _pass_on_copies(b_refs, land_refs, send, recv):
    del land_refs
    x, y, c, _ = _place()
    return [_half_block(b_refs[t], _chip_at(x, y, d)[2], c, (x, y, 1 - c), send.at[3 * t + d - 1], recv.at[3 * t + d - 1])
            for t in range(len(b_refs)) for d in (1, 2, 3)]


def _chips_copies(c_refs, land_refs, send, recv):
    x, y, c, _ = _place()
    return [_chip_copy(c_refs[t], land_refs[t], x, y, c, d, send.at[3 * t + d - 1], recv.at[3 * t + d - 1])
            for t in range(len(c_refs)) for d in (1, 2, 3)]


def _sibling_copies(g_refs, land_refs, send, recv):
    x, y, c, _ = _place()
    return [pltpu.make_async_remote_copy(
        src_ref=g_refs[t].at[k, 1 - c], dst_ref=land_refs[t].at[k], send_sem=send.at[4 * t + k], recv_sem=recv.at[4 * t + k],
        device_id=(x, y, 1 - c), device_id_type=MESH) for t in range(len(g_refs)) for k in range(4)]


def _join_halves(fs, l):
    nt = len(fs)

    def body(*refs):
        o_refs = refs[nt:2 * nt]
        send, recv = refs[2 * nt:]
        x, y, c, _ = _place()
        cps = [pltpu.make_async_remote_copy(
            src_ref=o_refs[t].at[c], dst_ref=o_refs[t].at[c], send_sem=send.at[t], recv_sem=recv.at[t],
            device_id=(x, y, 1 - c), device_id_type=MESH) for t in range(nt)]
        for cp in cps:
            cp.start()
        for cp in cps:
            cp.wait()

    return pl.pallas_call(
        body, name=f"grads_join_{l}", out_shape=[jax.ShapeDtypeStruct(a.shape, a.dtype) for a in fs],
        in_specs=[ANY] * nt, out_specs=[ANY] * nt,
        scratch_shapes=[pltpu.SemaphoreType.DMA((nt,)), pltpu.SemaphoreType.DMA((nt,))],
        input_output_aliases={t: t for t in range(nt)},
        compiler_params=_cp(has_side_effects=True),
    )(*fs)


def _add_half(g, r, c_arr, name):
    _, _, rh, cols = g.shape

    def body(c_ref, g_ref, r_ref, o_ref):
        o_ref[...] = (g_ref[...] + r_ref[...]).astype(BF16)

    blk = pl.BlockSpec((None, rh, cols), lambda k, cr: (k, 0, 0))
    return pl.pallas_call(
        body, name=name, out_shape=jax.ShapeDtypeStruct((4, rh, cols), BF16),
        grid_spec=pltpu.PrefetchScalarGridSpec(
            num_scalar_prefetch=1, grid=(4,),
            in_specs=[pl.BlockSpec((None, None, rh, cols), lambda k, cr: (k, cr[0], 0, 0)), blk], out_specs=blk),
        compiler_params=_cp(("parallel",)),
    )(c_arr, g, r)


def _sum_chips(cs, r3, place_arr, name):
    _, rh, cols = cs.shape
    rb = rh // 2

    def body(pl_ref, a_ref, r0_ref, r1_ref, r2_ref, o_ref):
        up = lambda ref: ref[...].astype(F32)
        o_ref[...] = ((up(a_ref) + up(r0_ref)) + up(r1_ref)) + up(r2_ref)

    def slot(d):
        return pl.BlockSpec((None, rb, cols), lambda i, pa: (d, i, 0))

    return pl.pallas_call(
        body, name=name, out_shape=jax.ShapeDtypeStruct((2, rh, cols), F32),
        grid_spec=pltpu.PrefetchScalarGridSpec(
            num_scalar_prefetch=1, grid=(2,),
            in_specs=[pl.BlockSpec((None, rb, cols), lambda i, pa: (pa[0], i, 0)), slot(0), slot(1), slot(2)],
            out_specs=pl.BlockSpec((None, rb, cols), lambda i, pa: (pa[1], i, 0))),
        compiler_params=_cp(("parallel",)),
    )(place_arr, cs, r3, r3, r3)


def _allreduce_small(pack):
    rows = pack.shape[0]
    hr = rows // 2

    def body(p_ref, o_ref, sib, slots, s1, r1, s2, r2, s3, r3):
        x, y, c, chip = _place()
        sibling = (x, y, 1 - c)
        ex = pltpu.make_async_remote_copy(src_ref=p_ref, dst_ref=sib, send_sem=s1, recv_sem=r1,
                                          device_id=sibling, device_id_type=MESH)
        ex.start()
        ex.wait()
        half = pl.ds(pl.multiple_of(c * hr, 16), hr)
        slots[0] = (p_ref[half, :] + sib[half, :]).astype(BF16)
        cps = []
        for d in (1, 2, 3):
            px, py, _ = _chip_at(x, y, d)
            cps.append(pltpu.make_async_remote_copy(
                src_ref=slots.at[0], dst_ref=slots.at[d], send_sem=s2.at[d - 1], recv_sem=r2.at[d - 1],
                device_id=(px, py, c), device_id_type=MESH))
        for cp in cps:
            cp.start()
        for cp in cps:
            cp.wait()
        tot = slots[chip].astype(F32)
        for k in (1, 2, 3):
            tot = tot + slots[jnp.bitwise_xor(chip, k)].astype(F32)
        o_ref[half, :] = tot
        back = pltpu.make_async_remote_copy(src_ref=o_ref.at[half, :], dst_ref=o_ref.at[half, :], send_sem=s3, recv_sem=r3,
                                            device_id=sibling, device_id_type=MESH)
        back.start()
        back.wait()

    vm = pl.BlockSpec(memory_space=pltpu.VMEM)
    return pl.pallas_call(
        body, name="allreduce_small", out_shape=jax.ShapeDtypeStruct((rows, 128), F32),
        in_specs=[vm], out_specs=vm,
        scratch_shapes=[pltpu.VMEM((rows, 128), F32), pltpu.VMEM((4, hr, 128), BF16),
                        pltpu.SemaphoreType.DMA, pltpu.SemaphoreType.DMA, pltpu.SemaphoreType.DMA((3,)), pltpu.SemaphoreType.DMA((3,)),
                        pltpu.SemaphoreType.DMA, pltpu.SemaphoreType.DMA],
        compiler_params=_cp(has_side_effects=True),
    )(pack)


def _adam_math(gv, wv, mv, vv):
    m2 = ADAM_B1 * mv + (1.0 - ADAM_B1) * gv
    v2 = ADAM_B2 * vv + (1.0 - ADAM_B2) * (gv * gv)
    m_hat = m2 / (1.0 - ADAM_B1 ** ADAM_STEP)
    v_hat = v2 / (1.0 - ADAM_B2 ** ADAM_STEP)
    return -ADAM_LR * (m_hat / (jnp.sqrt(v_hat) + ADAM_EPS) + ADAM_WD * wv), m2, v2


def _adam(g, w, m, v, name):
    rows, cols = g.shape
    rb = rows // 4

    def body(g_ref, w_ref, m_ref, v_ref, d_ref, m2_ref, v2_ref):
        d_ref[...], m2_ref[...], v2_ref[...] = _adam_math(g_ref[...], w_ref[...], m_ref[...], v_ref[...])

    blk = pl.BlockSpec((rb, cols), lambda i: (i, 0))
    shp = jax.ShapeDtypeStruct((rows, cols), F32)
    return pl.pallas_call(
        body, name=name, grid=(4,), in_specs=[blk] * 4, out_specs=[blk] * 3, out_shape=[shp] * 3,
        compiler_params=_cp(("parallel",)),
    )(g, w, m, v)


def _adam_layer(g, w, m, v, l, prev, name):
    rows, cols = g.shape
    rb = rows // 4

    def body(g_ref, w_ref, m_ref, v_ref, *rest):
        go_ref, d_ref, m2_ref, v2_ref = rest[-4:]
        gv = g_ref[...]
        go_ref[...] = gv
        d_ref[...], m2_ref[...], v2_ref[...] = _adam_math(gv, w_ref[...], m_ref[...], v_ref[...])

    lay = pl.BlockSpec((None, rb, cols), lambda i: (l, i, 0))
    shp = jax.ShapeDtypeStruct((2, rows, cols), F32)
    prev = () if prev is None else tuple(prev)
    return pl.pallas_call(
        body, name=name, grid=(4,), in_specs=[pl.BlockSpec((rb, cols), lambda i: (i, 0)), lay, lay, lay] + [ANY] * len(prev),
        out_specs=[lay] * 4, out_shape=[shp] * 4,
        input_output_aliases={4 + j: j for j in range(len(prev))},
        compiler_params=_cp(("parallel",)),
    )(g, w, m, v, *prev)


def _rows128(a):
    return a.reshape(-1, 128)


def _pack(arrs, mult):
    parts = [_rows128(a) for a in arrs]
    rows = sum(q.shape[0] for q in parts)
    pad = -rows % mult
    if pad:
        parts.append(jnp.zeros((pad, 128), F32))
    return jnp.concatenate(parts, axis=0)


def _unpack(pack, shapes):
    out, o = [], 0
    for s in shapes:
        n = 1
        for e in s:
            n *= e
        out.append(pack[o:o + n // 128].reshape(s))
        o += n // 128
    return out


WEIGHTS = ['norm1_g', 'w_in', 'gmlp_ln_g', 'gmlp_ln_b', 'gmlp_w_s', 'gmlp_b_s', 'conv_w', 'conv_b', 'lru_w_r', 'lru_b_r', 'lru_w_i',
           'lru_b_i', 'lru_lambda', 'w_out', 'norm2_g', 'w_ffn_in', 'w_ffn_out', 'final_g']
BIG = ['w_in', 'w_out', 'w_ffn_in', 'w_ffn_out']
SMALL = [n for n in WEIGHTS if n not in BIG]
CHIP_SHARDED_SMALL = ['conv_w', 'lru_b_r', 'lru_b_i', 'lru_lambda']


def kernel(x, norm1_g, w_in, gmlp_ln_g, gmlp_ln_b, gmlp_w_s, gmlp_b_s, conv_w, conv_b, lru_w_r, lru_b_r, lru_w_i, lru_b_i, lru_lambda, w_out, norm2_g, w_ffn_in, w_ffn_out, final_g, loss_target, m_norm1_g, m_w_in, m_gmlp_ln_g, m_gmlp_ln_b, m_gmlp_w_s, m_gmlp_b_s, m_conv_w, m_conv_b, m_lru_w_r, m_lru_b_r, m_lru_w_i, m_lru_b_i, m_lru_lambda, m_w_out, m_norm2_g, m_w_ffn_in, m_w_ffn_out, m_final_g, v_norm1_g, v_w_in, v_gmlp_ln_g, v_gmlp_ln_b, v_gmlp_w_s, v_gmlp_b_s, v_conv_w, v_conv_b, v_lru_w_r, v_lru_b_r, v_lru_w_i, v_lru_b_i, v_lru_lambda, v_w_out, v_norm2_g, v_w_ffn_in, v_w_ffn_out, v_final_g):
    a = dict(locals())
    w = {n: a[n] for n in WEIGHTS}
    mom = {n: a["m_" + n] for n in WEIGHTS}
    var = {n: a["v_" + n] for n in WEIGHTS}
    _, _, c, chip = _place()
    c_arr, chip_arr = jnp.reshape(c, (1,)).astype(jnp.int32), jnp.reshape(chip, (1,)).astype(jnp.int32)
    place_arr = jnp.stack([chip, c]).astype(jnp.int32)

    first, rest = BIG[:1], BIG[1:]

    def as_weights(names, full):
        wb = {n: f.reshape(4, 2 * f.shape[2], f.shape[3]) for n, f in zip(names, full)}
        if "w_out" in wb:
            wb["w_out"] = wb["w_out"].reshape(D, D)
            wb["w_ffn_out"] = wb["w_ffn_out"].reshape(DFF, D)
        return wb

    def cast(n, l):
        return _cast_into(w[n], l, chip_arr, f"cast_{n}_{l}")

    def landed(fly, names, after, tag):
        return as_weights(names, _gather_pass_on(_gather_wait(fly[0], fly[1], fly[2], after, tag), tag))

    tiny = _pack([w[n] for n in CHIP_SHARDED_SMALL], 8)
    _, tiny_full = _gather_weights([], tiny)
    fly_in = _gather_start([cast("w_in", 0)], "in", after=(tiny_full,))
    fly0 = _gather_start([cast(n, 0) for n in rest], "0", after=(fly_in[3],))
    bufs1 = [cast(n, 1) for n in BIG]
    fly1 = _gather_start(bufs1, "1", after=(fly0[3],))
    p = {n: w[n] for n in SMALL}
    parts = [_unpack(tiny_full[k], [w[n].shape for n in CHIP_SHARDED_SMALL]) for k in range(4)]
    for i, n in enumerate(CHIP_SHARDED_SMALL):
        p[n] = jnp.concatenate([parts[k][i] for k in range(4)], axis=-1)

    passing = {}

    def pass_on_1(gu):
        bufs = _gather_wait(fly1[0], fly1[1], fly1[2], gu, "1")
        passing[1] = _exchange_start(bufs, [], _pass_on_copies, 3 * len(bufs), "gather_pass_on_start_1")
        return (passing[1][-1],)

    xa, saved0 = _forward_layer(0, x[0], p, landed(fly_in, first, fly1[3], "in"), after=(fly0[3], fly1[3]),
                                rest=lambda merged: landed(fly0, rest, merged, "0"), near_end=pass_on_1)
    send, recv, bufs1, _, _ = passing[1]
    xb, saved1 = _forward_layer(
        1, xa, p, as_weights(BIG, _exchange_wait(send, recv, bufs1, [], xa, _pass_on_copies, "gather_pass_on_wait_1")[0]))
    dxb, loss_v, dfg = _loss_head(xb, loss_target[0], p["final_g"][None])
    loss = lax.psum(loss_v[0, 0], ("x", "y", "c"))

    out, flying = {}, {}

    def halves(grads):
        return [g.reshape(4, 2, -1, g.shape[-1]) for g in grads]

    def sibling_start(grads, names, l, tag):
        gs = halves(grads)
        lands = [lax.empty((4,) + g.shape[2:], g.dtype) for g in gs]
        flying["s" + tag] = (names, l) + tuple(
            _exchange_start(gs, lands, _sibling_copies, 4 * len(gs), f"grads_to_sibling_start_{tag}"))
        return (flying["s" + tag][-1],)

    def chips_start(gs, from_sib, names, l, tag):
        cs = [_add_half(g, r, c_arr, f"add_half_{n}_{l}") for n, g, r in zip(names, gs, from_sib)]
        lands = [lax.empty((3,) + a.shape[1:], a.dtype) for a in cs]
        flying[tag] = (names, l) + tuple(_exchange_start(cs, lands, _chips_copies, 3 * len(cs), f"grads_to_chips_start_{tag}"))
        return (flying[tag][-1],)

    def sibling_finish(tag, after):
        names, l, send, recv, gs, lands, _ = flying["s" + tag]
        gs, from_sib = _exchange_wait(send, recv, gs, lands, after, _sibling_copies, f"grads_to_sibling_wait_{tag}")
        return chips_start(gs, from_sib, names, l, tag)

    def reduce_start(grads, names, l, tag):
        gs = halves(grads)
        return chips_start(gs, _to_sibling_halves(gs, tag), names, l, tag)

    def reduce_finish(tags, after):
        names, ts = [], []
        for tag in tags:
            names_t, l, send, recv, cs, lands, _ = flying[tag]
            cs, lands = _exchange_wait(send, recv, cs, lands, after, _chips_copies, f"grads_to_chips_wait_{tag}")
            ts += [_sum_chips(cc, r3, place_arr, f"sum_chips_{n}_{l}") for n, cc, r3 in zip(names_t, cs, lands)]
            names += names_t
        for n, j in zip(names, _join_halves(ts, tags[0])):
            out[n] = _adam_layer(j.reshape(w[n].shape[1:]), w[n], mom[n], var[n], l, out.get(n), f"adam_{n}_{l}")

    def late1(grads):
        return sibling_finish("1a", grads[0]) + sibling_start(grads, first, 1, "1b")

    def midway0(grads):
        reduce_finish(("1a", "1b"), grads[0])
        return reduce_start(grads, rest, 0, "0a")

    dxa, big1, small1 = _backward_layer(1, dxb, saved1, midway=lambda grads: sibling_start(grads, rest, 1, "1a"), late=late1)
    dx, big0, small0 = _backward_layer(0, dxa, saved0, after=sibling_finish("1b", dxa), midway=midway0,
                                       late=lambda grads: reduce_start(grads, first, 0, "0b"))
    reduce_finish(("0a", "0b"), dx)
    small = {k: jnp.stack([small0[k], small1[k]]) for k in LAYER_SMALL}
    small["final_g"] = dfg[0]

    full_shapes = [small[n].shape for n in SMALL]
    red = _unpack(_allreduce_small(_pack([small[n] for n in SMALL], 32)), full_shapes)
    g_small = []
    for n, g in zip(SMALL, red):
        if n in CHIP_SHARDED_SMALL:
            g = lax.dynamic_slice_in_dim(g, chip * w[n].shape[-1], w[n].shape[-1], axis=g.ndim - 1)
        g_small.append(g)
    shapes = [w[n].shape for n in SMALL]
    packs = [_pack(lst, 32) for lst in (g_small, [w[n] for n in SMALL], [mom[n] for n in SMALL], [var[n] for n in SMALL])]
    upd = [_unpack(u, shapes) for u in _adam(*packs, "adam_small")]
    for i, n in enumerate(SMALL):
        out[n] = [g_small[i], upd[0][i], upd[1][i], upd[2][i]]

    return (loss, dx[None]) + tuple(out[n][i] for i in range(4) for n in WEIGHTS)
```

```python
import functools

import jax
import jax.numpy as jnp
from jax import lax
from jax.experimental import pallas as pl
from jax.experimental.pallas import tpu as pltpu

F32 = jnp.float32
BF16 = jnp.bfloat16
MESH = pl.DeviceIdType.MESH

D = 1024
NH = 8
HD = 128
CHUNK = 128
N_IN_T = 12
DFF = 2816
DFF_SH = 1408
EPS = 1e-6
LRU_C = 8.0
ADAM_LR, ADAM_B1, ADAM_B2, ADAM_EPS, ADAM_WD, ADAM_STEP = 0.001, 0.9, 0.999, 1e-08, 0.01, 10

TM = 512
TM_BIG = 1024
RT = 128
PADR = 8
VMEM_LIMIT = 56 * 1024 * 1024


def _cp(sem=None, **kw):
    if sem is not None:
        kw["dimension_semantics"] = sem
    return pltpu.CompilerParams(vmem_limit_bytes=VMEM_LIMIT, **kw)


_GC = 0.7978845608028654


def _sigmoid(x):
    return 1.0 / (1.0 + jnp.exp(-x))


def _gelu(x):
    return 0.5 * x * (1.0 + jnp.tanh(_GC * (x + 0.044715 * x * x * x)))


def _gelu_and_grad(x):
    t = jnp.tanh(_GC * (x + 0.044715 * x * x * x))
    g = 0.5 * x * (1.0 + t)
    dg = 0.5 * (1.0 + t) + 0.5 * x * (1.0 - t * t) * _GC * (1.0 + 3 * 0.044715 * x * x)
    return g, dg


def _softplus_neg(lam):
    y = jnp.exp(-jnp.abs(lam))
    u = 1.0 + y
    l1p = jnp.where(u == 1.0, y, jnp.log(u) * y / (u - 1.0))
    return jnp.maximum(-lam, 0.0) + l1p


def _dot(a, b):
    return jnp.dot(a, b, preferred_element_type=F32)


def _dot_nt(a, b):
    return lax.dot_general(a, b, (((1,), (1,)), ((), ())), preferred_element_type=F32)


def _dot_tn(a, b):
    return lax.dot_general(a, b, (((0,), (0,)), ((), ())), preferred_element_type=F32)


def _rms_hat(x):
    r = lax.rsqrt(jnp.mean(x * x, axis=-1, keepdims=True) + EPS)
    return x * r, r


def _rms_bwd(dh, x, g):
    xh, r = _rms_hat(x)
    dxh = dh * g
    dx = r * (dxh - xh * jnp.mean(dxh * xh, axis=-1, keepdims=True))
    return dx, jnp.sum(dh * xh, axis=0, keepdims=True)


def _norm_into(x_ref, g_ref, h_ref):
    xh, _ = _rms_hat(x_ref[...])
    h_ref[...] = (xh * g_ref[...]).astype(BF16)


def _in_tile(j):
    m, hf = j // 2, j % 2
    orig = jnp.where(m < 2, m, jnp.where(m == 2, 4, jnp.where(m < 5, m - 1, 5)))
    t = orig * 2 + hf
    return t // 3, t % 3


ANY = pl.BlockSpec(memory_space=pl.ANY)


def _mm_in(x, g, w_in, l, after=()):
    S = x.shape[0]
    tm = min(2 * TM_BIG, S)

    def body(x_ref, g_ref, w0_ref, w1_ref, *rest):
        o_ref, h_ref = rest[-2:]

        @pl.when(pl.program_id(1) == 0)
        def _():
            _norm_into(x_ref, g_ref, h_ref)
        rp = TM if l else tm
        for r0 in range(0, tm, rp):
            hv = h_ref[r0:r0 + rp, :]
            o_ref[r0:r0 + rp, 0:512] = _dot(hv, w0_ref[...]).astype(BF16)
            o_ref[r0:r0 + rp, 512:1024] = _dot(hv, w1_ref[...]).astype(BF16)

    def w_tile(hf):
        def w_map(i, m):
            sh, tl = _in_tile(2 * m + hf)
            return (sh, 0, tl)
        return pl.BlockSpec((None, D, 512), w_map)

    return pl.pallas_call(
        body, name=f"mm_in_{l}", grid=(S // tm, 6),
        in_specs=[pl.BlockSpec((tm, D), lambda i, m: (i, 0)), pl.BlockSpec((1, D), lambda i, m: (0, 0)),
                  w_tile(0), w_tile(1)] + [ANY] * len(after),
        out_specs=[pl.BlockSpec((None, tm, D), lambda i, m: (m, i, 0)), pl.BlockSpec((tm, D), lambda i, m: (i, 0))],
        out_shape=[jax.ShapeDtypeStruct((6, S, D), BF16), jax.ShapeDtypeStruct((S, D), BF16)],
        compiler_params=_cp(("parallel", "arbitrary")),
    )(x, g, w_in, w_in, *after)


def _mm_res(a, w, res, l, name, after=()):
    S, K = a.shape

    tm = min(TM_BIG, S) if (l and K > D) else TM

    def body(a_ref, w_ref, r_ref, *rest):
        for r0 in range(0, tm, TM):
            rows = slice(r0, r0 + TM)
            rest[-1][rows, :] = r_ref[rows, :] + _dot(a_ref[rows, :], w_ref[...])

    return pl.pallas_call(
        body, name=f"{name}_{l}", grid=(S // tm,),
        in_specs=[pl.BlockSpec((tm, K), lambda i: (i, 0)), pl.BlockSpec((K, D), lambda i: (0, 0)),
                  pl.BlockSpec((tm, D), lambda i: (i, 0))] + [ANY] * len(after),
        out_specs=pl.BlockSpec((tm, D), lambda i: (i, 0)),
        out_shape=jax.ShapeDtypeStruct((S, D), F32),
        compiler_params=_cp(("parallel",)),
    )(a, w, res, *after)


def _mm_ffn_in(x, g, w_fi, l):
    S = x.shape[0]

    tm = min(TM_BIG, S)

    def body(x_ref, g_ref, w_ref, gu_ref, ff_ref, h_ref):
        @pl.when(pl.program_id(1) == 0)
        def _():
            _norm_into(x_ref, g_ref, h_ref)
        for r0 in range(0, tm, TM):
            rows = slice(r0, r0 + TM)
            hv = h_ref[rows, :]
            ga = _dot(hv, w_ref[0])
            gb = _dot(hv, w_ref[1])
            gu_ref[0, rows, :] = ga.astype(BF16)
            gu_ref[1, rows, :] = gb.astype(BF16)
            ff_ref[rows, :] = (ga * _sigmoid(ga) * gb).astype(BF16)

    gu, ff, h = pl.pallas_call(
        body, name=f"mm_ffn_in_{l}", grid=(S // tm, 2),
        in_specs=[pl.BlockSpec((tm, D), lambda i, s: (i, 0)), pl.BlockSpec((1, D), lambda i, s: (0, 0)),
                  pl.BlockSpec((2, None, D, DFF_SH), lambda i, s: (0, s, 0, 0))],
        out_specs=[pl.BlockSpec((2, None, tm, DFF_SH), lambda i, s: (0, s, i, 0)),
                   pl.BlockSpec((tm, DFF_SH), lambda i, s: (i, s)),
                   pl.BlockSpec((tm, D), lambda i, s: (i, 0))],
        out_shape=[jax.ShapeDtypeStruct((2, 2, S, DFF_SH), BF16), jax.ShapeDtypeStruct((S, DFF), BF16),
                   jax.ShapeDtypeStruct((S, D), BF16)],
        compiler_params=_cp(("parallel", "arbitrary")),
    )(x, g, w_fi.reshape(2, 2, D, DFF_SH))
    return gu.reshape(4, S, DFF_SH), ff, h


def _gmlp_fwd(z6, ws_b, bs_b, lg, lb):
    S = z6.shape[1]

    def body(z_ref, ws_ref, bs_ref, lg_ref, lb_ref, o_ref, mix):
        gv = _gelu(z_ref[1].astype(F32))
        xc = gv - jnp.mean(gv, axis=-1, keepdims=True)
        rs = lax.rsqrt(jnp.mean(xc * xc, axis=-1, keepdims=True) + EPS)
        vb = (xc * rs * lg_ref[...] + lb_ref[...]).astype(BF16)
        for gi in range(NH):
            cs = slice(gi * HD, (gi + 1) * HD)
            mix[:, cs] = _dot(ws_ref[gi], vb[:, cs])
        o_ref[...] = (_sigmoid(z_ref[2].astype(F32)) * _gelu(z_ref[0].astype(F32)) * (mix[...] + bs_ref[...])).astype(BF16)

    return pl.pallas_call(
        body, name="gmlp_fwd", grid=(S // CHUNK,),
        in_specs=[pl.BlockSpec((3, CHUNK, D), lambda i: (0, i, 0)), pl.BlockSpec((NH, CHUNK, CHUNK), lambda i: (0, 0, 0)),
                  pl.BlockSpec((CHUNK, D), lambda i: (0, 0)), pl.BlockSpec((1, D), lambda i: (0, 0)),
                  pl.BlockSpec((1, D), lambda i: (0, 0))],
        out_specs=pl.BlockSpec((CHUNK, D), lambda i: (i, 0)),
        out_shape=jax.ShapeDtypeStruct((S, D), BF16),
        scratch_shapes=[pltpu.VMEM((CHUNK, D), F32)],
        compiler_params=_cp(("parallel",)),
    )(z6, ws_b, bs_b, lg, lb)


def _row_iota():
    return lax.broadcasted_iota(jnp.int32, (RT, HD), 0)


SUB = 8
UNROLL = 4
GRAD_ROWS = 256


def _scan_up(a, b, carry):
    row = lax.broadcasted_iota(jnp.int32, (SUB, HD), 0)
    masks = [(d, row >= d) for d in (1, 2, 4)]
    c = jnp.broadcast_to(carry, (SUB, HD))
    hs = []
    for j in range(RT // SUB):
        aj, bj = a[SUB * j:SUB * (j + 1)], b[SUB * j:SUB * (j + 1)]
        for d, m in masks:
            bj = bj + aj * jnp.where(m, pltpu.roll(bj, d, 0), 0.0)
            aj = aj * jnp.where(m, pltpu.roll(aj, d, 0), 1.0)
        h = bj + aj * c
        hs.append(h)
        c = jnp.broadcast_to(h[SUB - 1:SUB, :], (SUB, HD))
    return jnp.concatenate(hs, axis=0), hs[-1][SUB - 1:SUB, :]


def _scan_down(a, b, carry):
    row = lax.broadcasted_iota(jnp.int32, (SUB, HD), 0)
    masks = [(d, row < SUB - d) for d in (1, 2, 4)]
    c = jnp.broadcast_to(carry, (SUB, HD))
    hs = []
    for j in reversed(range(RT // SUB)):
        aj, bj = a[SUB * j:SUB * (j + 1)], b[SUB * j:SUB * (j + 1)]
        for d, m in masks:
            bj = bj + aj * jnp.where(m, pltpu.roll(bj, SUB - d, 0), 0.0)
            aj = aj * jnp.where(m, pltpu.roll(aj, SUB - d, 0), 1.0)
        h = bj + aj * c
        hs.append(h)
        c = jnp.broadcast_to(h[0:1, :], (SUB, HD))
    return jnp.concatenate(hs[::-1], axis=0), hs[-1][0:1, :]


def _decay(r, sp_d):
    log_a = -LRU_C * r * sp_d
    a = jnp.exp(log_a)
    return a, jnp.sqrt(jnp.maximum(-jnp.tanh(log_a) * (a * a + 1.0), 0.0))


def _lru_gates(xc, d, wr_ref, br_ref, wi_ref, bi_ref, sp):
    xb = xc.astype(BF16)
    r = _sigmoid(_dot(xb, wr_ref[d]) + br_ref[d:d + 1, :])
    i = _sigmoid(_dot(xb, wi_ref[d]) + bi_ref[d:d + 1, :])
    a, mult = _decay(r, sp[d:d + 1, :])
    return r, i, a, mult


def _shifted(win, k):
    w = RT + 2 * PADR
    v = win if k == 0 else pltpu.roll(win, (-k) % w, 0)
    return v[PADR:PADR + RT]


def _conv_taps(win):
    return [_shifted(win, k) for k in (-1, 0, 1, 2)]


def _fill_padded(dst, src_ref, S):
    zeros = jnp.zeros((PADR, HD), F32)
    dst[0:PADR, :] = zeros
    dst[PADR + S:2 * PADR + S, :] = zeros

    def cp(i, c):
        t0 = pl.multiple_of(i * RT, RT)
        dst[pl.ds(t0 + PADR, RT), :] = src_ref[pl.ds(t0, RT), :].astype(F32)
        return c
    lax.fori_loop(0, S // RT, cp, 0)


def _conv_fwd_all(zxp, xc_s, cw_ref, cb_ref, S):
    def cv(i, c):
        t0 = pl.multiple_of(i * RT, RT)
        xm1, x0, xp1, xp2 = _conv_taps(zxp[pl.ds(t0, RT + 2 * PADR), :])
        xc_s[pl.ds(t0, RT), :] = (cb_ref[...] + xm1 * cw_ref[0:1, :] + x0 * cw_ref[1:2, :]
                                  + xp1 * cw_ref[2:3, :] + xp2 * cw_ref[3:4, :])
        return c
    lax.fori_loop(0, S // RT, cv, 0)


def _lru_specs(S):
    head = lambda h: (0, h)
    return [pl.BlockSpec((4, HD), head), pl.BlockSpec((1, HD), head),
            pl.BlockSpec((2, None, HD, HD), lambda h: (0, h, 0, 0)), pl.BlockSpec((2, HD), head),
            pl.BlockSpec((2, None, HD, HD), lambda h: (0, h, 0, 0)), pl.BlockSpec((2, HD), head),
            pl.BlockSpec((2, HD), head)]


def _lru_fwd(z6, ya, cw, cb, wr, br, wi, bi, lam):
    S = z6.shape[1]
    nt = S // RT

    def body(z_ref, ya_ref, cw_ref, cb_ref, wr_ref, br_ref, wi_ref, bi_ref, lam_ref, mg_ref, h0_ref, h1_ref, zxp, xc_s):
        sp = _softplus_neg(lam_ref[...])
        _fill_padded(zxp, z_ref.at[0], S)
        _conv_fwd_all(zxp, xc_s, cw_ref, cb_ref, S)

        def scans(i, carry):
            cu, cd = carry
            for u in range(UNROLL):
                j = i * UNROLL + u
                ru = pl.ds(pl.multiple_of(j * RT, RT), RT)
                rd = pl.ds(pl.multiple_of((nt - 1 - j) * RT, RT), RT)
                xu, xd = xc_s[ru, :], xc_s[rd, :]
                _, gi, a, mult = _lru_gates(xu, 0, wr_ref, br_ref, wi_ref, bi_ref, sp)
                hu, cu = _scan_up(a, mult * gi * xu, cu)
                h0_ref[ru, :] = hu
                _, gi, a, mult = _lru_gates(xd, 1, wr_ref, br_ref, wi_ref, bi_ref, sp)
                hd, cd = _scan_down(a, mult * gi * xd, cd)
                h1_ref[rd, :] = hd
            return cu, cd
        z1 = jnp.zeros((1, HD), F32)
        lax.fori_loop(0, nt // UNROLL, scans, (z1, z1))

        def merge(i, c):
            rows = pl.ds(pl.multiple_of(i * RT, RT), RT)
            yb = (h0_ref[rows, :] + h1_ref[rows, :]) * _gelu(z_ref[1, rows, :].astype(F32))
            mg_ref[rows, :] = (ya_ref[rows, :].astype(F32) + _sigmoid(z_ref[2, rows, :].astype(F32)) * yb).astype(BF16)
            return c
        lax.fori_loop(0, nt, merge, 0)

    col = pl.BlockSpec((S, HD), lambda h: (0, h))
    return pl.pallas_call(
        body, name="lru_fwd", grid=(NH,),
        in_specs=[pl.BlockSpec((3, S, HD), lambda h: (1, 0, h)), col] + _lru_specs(S),
        out_specs=[col, col, col],
        out_shape=[jax.ShapeDtypeStruct((S, D), BF16), jax.ShapeDtypeStruct((S, D), F32), jax.ShapeDtypeStruct((S, D), F32)],
        scratch_shapes=[pltpu.VMEM((S + 2 * PADR, HD), F32), pltpu.VMEM((S, HD), F32)],
        compiler_params=_cp(("parallel",)),
    )(z6, ya, cw, cb, wr, br, wi, bi, lam)


def _loss_head(x, tgt, g):
    S = x.shape[0]

    def body(x_ref, t_ref, g_ref, dx_ref, loss_ref, dg_ref):
        @pl.when(pl.program_id(0) == 0)
        def _():
            loss_ref[...] = jnp.zeros_like(loss_ref)
            dg_ref[...] = jnp.zeros_like(dg_ref)
        xv = x_ref[...]
        xh, _ = _rms_hat(xv)
        e = xh * g_ref[...] - t_ref[...]
        loss_ref[...] += jnp.sum(e * e) * (0.5 / D)
        dx, dgs = _rms_bwd(e * (1.0 / D), xv, g_ref[...])
        dx_ref[...] = dx
        dg_ref[...] += dgs

    return pl.pallas_call(
        body, name="loss_head", grid=(S // TM,),
        in_specs=[pl.BlockSpec((TM, D), lambda i: (i, 0)), pl.BlockSpec((TM, D), lambda i: (i, 0)),
                  pl.BlockSpec((1, D), lambda i: (0, 0))],
        out_specs=[pl.BlockSpec((TM, D), lambda i: (i, 0)), pl.BlockSpec((1, 128), lambda i: (0, 0)),
                   pl.BlockSpec((1, D), lambda i: (0, 0))],
        out_shape=[jax.ShapeDtypeStruct((S, D), F32), jax.ShapeDtypeStruct((1, 128), F32), jax.ShapeDtypeStruct((1, D), F32)],
        compiler_params=_cp(("arbitrary",)),
    )(x, tgt, g)


def _bwd_ffn_out(dx, w_fo, gu, l, after=()):
    S = dx.shape[0]

    tm = min(TM_BIG, S)

    def body(dx_ref, w_ref, gu_ref, *rest):
        o_ref = rest[-1]
        for r0 in range(0, tm, TM):
            rows = slice(r0, r0 + TM)
            d = _dot_nt(dx_ref[rows, :].astype(BF16), w_ref[...])
            ga, gb = gu_ref[0, rows, :].astype(F32), gu_ref[1, rows, :].astype(F32)
            sg = _sigmoid(ga)
            o_ref[0, rows, :] = (d * gb * sg * (1.0 + ga * (1.0 - sg))).astype(BF16)
            o_ref[1, rows, :] = (d * ga * sg).astype(BF16)

    pair = pl.BlockSpec((2, None, tm, DFF_SH), lambda i, s: (0, s, i, 0))
    dgu = pl.pallas_call(
        body, name=f"bwd_ffn_out_{l}", grid=(S // tm, 2),
        in_specs=[pl.BlockSpec((tm, D), lambda i, s: (i, 0)), pl.BlockSpec((DFF_SH, D), lambda i, s: (s, 0)), pair]
        + [ANY] * len(after),
        out_specs=pair,
        out_shape=jax.ShapeDtypeStruct((2, 2, S, DFF_SH), BF16),
        compiler_params=_cp(("parallel", "arbitrary")),
    )(dx, w_fo, gu.reshape(2, 2, S, DFF_SH), *after)
    return dgu.reshape(4, S, DFF_SH)


def _mm_tn(a, b, m_blk, tk, name):
    S, M = a.shape

    def body(a_ref, b_ref, o_ref):
        @pl.when(pl.program_id(1) == 0)
        def _():
            o_ref[...] = jnp.zeros_like(o_ref)
        o_ref[...] += _dot_tn(a_ref[...], b_ref[...].astype(BF16))

    return pl.pallas_call(
        body, name=name, grid=(M // m_blk, S // tk),
        in_specs=[pl.BlockSpec((tk, m_blk), lambda m, k: (k, m)), pl.BlockSpec((tk, D), lambda m, k: (k, 0))],
        out_specs=pl.BlockSpec((m_blk, D), lambda m, k: (m, 0)),
        out_shape=jax.ShapeDtypeStruct((M, D), F32),
        compiler_params=_cp(("parallel", "arbitrary")),
    )(a, b)


def _mm_nt_rms_bwd(a, a_specs, w, w_specs, nk, tm, x, g, dres, name, after=(), row_part=None):
    S = x.shape[0]
    sub = len(a_specs)
    row_part = tm if row_part is None else min(row_part, tm)

    def body(*refs):
        a_refs, w_refs = refs[:sub], refs[sub:2 * sub]
        x_ref, g_ref, r_ref = refs[2 * sub:2 * sub + 3]
        dx_ref, dg_ref, acc = refs[-3:]
        i, k = pl.program_id(0), pl.program_id(1)
        @pl.when(k == 0)
        def _():
            acc[...] = jnp.zeros_like(acc)
        for r0 in range(0, tm, row_part):
            rows = slice(r0, r0 + row_part)
            for j in range(sub):
                acc[rows, :] += _dot_nt(a_refs[j][rows, :], w_refs[j][...])

        @pl.when(jnp.logical_and(i == 0, k == 0))
        def _():
            dg_ref[...] = jnp.zeros_like(dg_ref)

        @pl.when(k == nk - 1)
        def _():
            dx, dgs = _rms_bwd(acc[...], x_ref[...], g_ref[...])
            dx_ref[...] = r_ref[...] + dx
            dg_ref[...] += dgs

    row = pl.BlockSpec((tm, D), lambda i, k: (i, 0))
    vec = pl.BlockSpec((1, D), lambda i, k: (0, 0))
    return pl.pallas_call(
        body, name=name, grid=(S // tm, nk),
        in_specs=list(a_specs) + list(w_specs) + [row, vec, row] + [ANY] * len(after),
        out_specs=[row, vec],
        out_shape=[jax.ShapeDtypeStruct((S, D), F32), jax.ShapeDtypeStruct((1, D), F32)],
        scratch_shapes=[pltpu.VMEM((tm, D), F32)],
        compiler_params=_cp(("arbitrary", "arbitrary")),
    )(*[a] * sub, *[w] * sub, x, g, dres, *after)


def _dw_ffn_in(h, dgu, l):
    S = h.shape[0]

    def body(h_ref, b_ref, o_ref):
        @pl.when(pl.program_id(1) == 0)
        def _():
            o_ref[...] = jnp.zeros_like(o_ref)
        o_ref[...] += _dot_tn(h_ref[...], b_ref[...])

    tk = min(2 * TM_BIG, S)
    return pl.pallas_call(
        body, name=f"dw_ffn_in_{l}", grid=(4, S // tk),
        in_specs=[pl.BlockSpec((tk, D), lambda j, k: (k, 0)), pl.BlockSpec((None, tk, DFF_SH), lambda j, k: (j, k, 0))],
        out_specs=pl.BlockSpec((None, D, DFF_SH), lambda j, k: (j, 0, 0)),
        out_shape=jax.ShapeDtypeStruct((4, D, DFF_SH), F32),
        compiler_params=_cp(("parallel", "arbitrary")),
    )(h, dgu)


_HALF_COMPS = ((0, 1, 3), (4, 2, 5))


def _dw_in(h, dz6, l):
    S = h.shape[0]

    def body(h_ref, d0_ref, d1_ref, d2_ref, o_ref):
        @pl.when(pl.program_id(1) == 0)
        def _():
            o_ref[...] = jnp.zeros_like(o_ref)
        hv = h_ref[...]
        for q, d_ref in enumerate((d0_ref, d1_ref, d2_ref)):
            for hf in range(2):
                col = 1024 * q + 512 * hf
                o_ref[col // 1536, :, col % 1536:col % 1536 + 512] += _dot_tn(hv, d_ref[:, 512 * hf:512 * (hf + 1)])

    tk = min(TM_BIG, S)

    def comp(q):
        return pl.BlockSpec((None, tk, D), lambda p, k: (jnp.where(p == 0, _HALF_COMPS[0][q], _HALF_COMPS[1][q]), k, 0))

    return pl.pallas_call(
        body, name=f"dw_in_{l}", grid=(2, S // tk),
        in_specs=[pl.BlockSpec((tk, D), lambda p, k: (k, 0)), comp(0), comp(1), comp(2)],
        out_specs=pl.BlockSpec((2, D, 1536), lambda p, k: (p, 0, 0)),
        out_shape=jax.ShapeDtypeStruct((4, D, 1536), F32),
        compiler_params=_cp(("parallel", "arbitrary")),
    )(h, dz6, dz6, dz6)


def _bwd_out(dx, w_o, merged, l):
    S = dx.shape[0]

    def body(dx_ref, w_ref, m_ref, dm_ref, dw_ref):
        @pl.when(pl.program_id(0) == 0)
        def _():
            dw_ref[...] = jnp.zeros_like(dw_ref)
        dxb = dx_ref[...].astype(BF16)
        for r0 in range(0, tm, TM):
            dm_ref[r0:r0 + TM, :] = _dot_nt(dxb[r0:r0 + TM, :], w_ref[...]).astype(BF16)
        dw_ref[...] += _dot_tn(m_ref[...], dxb)

    tm = min(TM_BIG, S) if l else TM
    row = pl.BlockSpec((tm, D), lambda i: (i, 0))
    return pl.pallas_call(
        body, name=f"bwd_out_{l}", grid=(S // tm,),
        in_specs=[row, pl.BlockSpec((D, D), lambda i: (0, 0)), row],
        out_specs=[row, pl.BlockSpec((D, D), lambda i: (0, 0))],
        out_shape=[jax.ShapeDtypeStruct((S, D), BF16), jax.ShapeDtypeStruct((D, D), F32)],
        compiler_params=_cp(("arbitrary",)),
    )(dx, w_o, merged)


def _gmlp_bwd(dm, z6, ws_b, wst_b, bs_b, lg, lb, after=()):
    S = z6.shape[1]

    def body(dm_ref, z_ref, ws_ref, wst_ref, bs_ref, lg_ref, lb_ref, *rest):
        dz_ref, dws_ref, dbs_ref, dlg_ref, dlb_ref, mix, dv = rest[-7:]

        @pl.when(pl.program_id(0) == 0)
        def _():
            dws_ref[...] = jnp.zeros_like(dws_ref)
            dbs_ref[...] = jnp.zeros_like(dbs_ref)
            dlg_ref[...] = jnp.zeros_like(dlg_ref)
            dlb_ref[...] = jnp.zeros_like(dlb_ref)
        gv, dgelu_v = _gelu_and_grad(z_ref[1].astype(F32))
        xc = gv - jnp.mean(gv, axis=-1, keepdims=True)
        rs = lax.rsqrt(jnp.mean(xc * xc, axis=-1, keepdims=True) + EPS)
        vh = xc * rs
        vb = (vh * lg_ref[...] + lb_ref[...]).astype(BF16)
        for gi in range(NH):
            cs = slice(gi * HD, (gi + 1) * HD)
            mix[:, cs] = _dot(ws_ref[gi], vb[:, cs])
        u, dgelu_u = _gelu_and_grad(z_ref[0].astype(F32))
        sa = _sigmoid(z_ref[2].astype(F32))
        mixed = mix[...] + bs_ref[...]
        dyg = dm_ref[...].astype(F32)
        dz_ref[2] = (dyg * u * mixed * sa * (1.0 - sa)).astype(BF16)
        dya = dyg * sa
        dz_ref[0] = (dya * mixed * dgelu_u).astype(BF16)
        dmix = dya * u
        dmb = dmix.astype(BF16)
        for gi in range(NH):
            cs = slice(gi * HD, (gi + 1) * HD)
            dv[:, cs] = _dot(wst_ref[gi], dmb[:, cs])
            dws_ref[gi] += _dot_nt(dmb[:, cs], vb[:, cs])
            dbs_ref[gi] += jnp.broadcast_to(jnp.sum(dmix[:, cs], axis=1, keepdims=True), (CHUNK, HD))
        dvv = dv[...]
        dlg_ref[...] += jnp.sum(dvv * vh, axis=0, keepdims=True)
        dlb_ref[...] += jnp.sum(dvv, axis=0, keepdims=True)
        dvh = dvv * lg_ref[...]
        dgv = rs * (dvh - jnp.mean(dvh, axis=-1, keepdims=True) - vh * jnp.mean(dvh * vh, axis=-1, keepdims=True))
        dz_ref[1] = (dgv * dgelu_v).astype(BF16)

    vec = pl.BlockSpec((1, D), lambda i: (0, 0))
    mat = pl.BlockSpec((NH, CHUNK, CHUNK), lambda i: (0, 0, 0))
    return pl.pallas_call(
        body, name="gmlp_bwd", grid=(S // CHUNK,),
        in_specs=[pl.BlockSpec((CHUNK, D), lambda i: (i, 0)), pl.BlockSpec((3, CHUNK, D), lambda i: (0, i, 0)), mat, mat,
                  pl.BlockSpec((CHUNK, D), lambda i: (0, 0)), vec, vec] + [ANY] * len(after),
        out_specs=[pl.BlockSpec((3, CHUNK, D), lambda i: (0, i, 0)), mat, mat, vec, vec],
        out_shape=[jax.ShapeDtypeStruct((6, S, D), BF16), jax.ShapeDtypeStruct((NH, CHUNK, CHUNK), F32),
                   jax.ShapeDtypeStruct((NH, CHUNK, HD), F32), jax.ShapeDtypeStruct((1, D), F32), jax.ShapeDtypeStruct((1, D), F32)],
        scratch_shapes=[pltpu.VMEM((CHUNK, D), F32), pltpu.VMEM((CHUNK, D), F32)],
        compiler_params=_cp(("arbitrary",)),
    )(dm, z6, ws_b, wst_b, bs_b, lg, lb, *after)


def _lru_bwd(dz6, dm, z6, h0, h1, cw, cb, wr, br, wi, bi, lam):
    S = z6.shape[1]
    nt = S // RT

    def body(dz_in, dm_ref, z_ref, h0_ref, h1_ref, cw_ref, cb_ref, wr_ref, br_ref, wi_ref, bi_ref, lam_ref,
             dz_ref, dcw_ref, dcb_ref, dwr_ref, dbr_ref, dwi_ref, dbi_ref, dlam_ref, zxp, xc_s, dhs_s, dxcp, r_s, lam_s):
        del dz_in
        lam = lam_ref[...]
        sp = _softplus_neg(lam)
        row = _row_iota()
        _fill_padded(zxp, z_ref.at[0], S)
        _conv_fwd_all(zxp, xc_s, cw_ref, cb_ref, S)
        zeros = jnp.zeros((PADR, HD), F32)
        dxcp[0:PADR, :] = zeros
        dxcp[PADR + S:2 * PADR + S, :] = zeros
        dwr_ref[...] = jnp.zeros_like(dwr_ref)
        dwi_ref[...] = jnp.zeros_like(dwi_ref)

        def pre(i, c):
            rows = pl.ds(pl.multiple_of(i * RT, RT), RT)
            hs = h0_ref[rows, :] + h1_ref[rows, :]
            dmv = dm_ref[rows, :].astype(F32)
            sb = _sigmoid(z_ref[2, rows, :].astype(F32))
            gg, dgg = _gelu_and_grad(z_ref[1, rows, :].astype(F32))
            dz_ref[2, rows, :] = (dmv * hs * gg * sb * (1.0 - sb)).astype(BF16)
            dyb = dmv * sb
            dz_ref[1, rows, :] = (dyb * hs * dgg).astype(BF16)
            dhs_s[rows, :] = dyb * gg
            return c
        lax.fori_loop(0, nt, pre, 0)

        def gate_bwd(d, gates, lamv, da, xc):
            r, gi, a, mult = gates
            dmult = lamv * gi * xc
            dgi = lamv * mult * xc
            dlog = (da - dmult * a / mult) * a
            dpr = (dlog * (-LRU_C) * sp[d:d + 1, :]) * r * (1.0 - r)
            dpi = dgi * gi * (1.0 - gi)
            xb, dprb, dpib = xc.astype(BF16), dpr.astype(BF16), dpi.astype(BF16)
            dwr_ref[d] += _dot_tn(xb, dprb)
            dwi_ref[d] += _dot_tn(xb, dpib)
            dxc = lamv * mult * gi + _dot_nt(dprb, wr_ref[d]) + _dot_nt(dpib, wi_ref[d])
            return dxc, (jnp.sum(dlog * r, axis=0, keepdims=True) * (-LRU_C), jnp.sum(dpr, axis=0, keepdims=True),
                         jnp.sum(dpi, axis=0, keepdims=True))

        def rgates(i, c):
            for u in range(UNROLL):
                rows = pl.ds(pl.multiple_of((i * UNROLL + u) * RT, RT), RT)
                xb = xc_s[rows, :].astype(BF16)
                for d in range(2):
                    r_s[d, rows, :] = _sigmoid(_dot(xb, wr_ref[d]) + br_ref[d:d + 1, :])
            return c
        lax.fori_loop(0, nt // UNROLL, rgates, 0)

        def chains(i, carry):
            qn, qp = carry
            for u in range(UNROLL):
                j = i * UNROLL + u
                rd = pl.ds(pl.multiple_of((nt - 1 - j) * RT, RT), RT)
                a, dhs = _decay(r_s[0, rd, :], sp[0:1, :])[0], dhs_s[rd, :]
                q, q_first = _scan_down(a, a * dhs, qn)
                lam_s[0, rd, :] = dhs + jnp.where(row == RT - 1, qn, pltpu.roll(q, RT - 1, 0))
                qn = q_first
                ru = pl.ds(pl.multiple_of(j * RT, RT), RT)
                a, dhs = _decay(r_s[1, ru, :], sp[1:2, :])[0], dhs_s[ru, :]
                q, q_last = _scan_up(a, a * dhs, qp)
                lam_s[1, ru, :] = dhs + jnp.where(row == 0, qp, pltpu.roll(q, 1, 0))
                qp = q_last
            return qn, qp

        z1 = jnp.zeros((1, HD), F32)
        lax.fori_loop(0, nt // UNROLL, chains, (z1, z1))

        ct = min(GRAD_ROWS, S)
        crow = lax.broadcasted_iota(jnp.int32, (ct, HD), 0)

        def tile_grads(i, acc):
            t0 = pl.multiple_of(i * ct, ct)
            rows = pl.ds(t0, ct)
            xc = xc_s[rows, :]
            xb = xc.astype(BF16)
            tp = pl.multiple_of(jnp.maximum(t0 - PADR, 0), PADR)
            prev = jnp.where(t0 > 0, h0_ref[pl.ds(tp, PADR), :][PADR - 1:PADR, :], 0.0)
            tn = pl.multiple_of(jnp.minimum(t0 + ct, S - PADR), PADR)
            nxt = jnp.where(t0 + ct < S, h1_ref[pl.ds(tn, PADR), :][0:1, :], 0.0)
            hside = (jnp.where(crow == 0, prev, pltpu.roll(h0_ref[rows, :], 1, 0)),
                     jnp.where(crow == ct - 1, nxt, pltpu.roll(h1_ref[rows, :], ct - 1, 0)))
            dxc, sums = 0.0, ()
            for d in range(2):
                r = r_s[d, rows, :]
                gi = _sigmoid(_dot(xb, wi_ref[d]) + bi_ref[d:d + 1, :])
                a, mult = _decay(r, sp[d:d + 1, :])
                lamv = lam_s[d, rows, :]
                dxc_d, s_d = gate_bwd(d, (r, gi, a, mult), lamv, lamv * hside[d], xc)
                dxc = dxc + dxc_d
                sums = sums + s_d
            dxcp[pl.ds(t0 + PADR, ct), :] = dxc
            return tuple(x + y for x, y in zip(acc, sums))

        s_sp0, s_br0, s_bi0, s_sp1, s_br1, s_bi1 = lax.fori_loop(0, S // ct, tile_grads, (z1,) * 6)

        dsp = jnp.concatenate([s_sp0, s_sp1], axis=0)
        dlam_ref[...] = -dsp * _sigmoid(-lam)
        dbr_ref[...] = jnp.concatenate([s_br0, s_br1], axis=0)
        dbi_ref[...] = jnp.concatenate([s_bi0, s_bi1], axis=0)

        def conv_bwd(i, carry):
            c0, c1, c2, c3, cb_ = carry
            t0 = pl.multiple_of(i * RT, RT)
            dwin = dxcp[pl.ds(t0, RT + 2 * PADR), :]
            d0 = _shifted(dwin, 0)
            dz_ref[0, pl.ds(t0, RT), :] = (_shifted(dwin, 1) * cw_ref[0:1, :] + d0 * cw_ref[1:2, :]
                                           + _shifted(dwin, -1) * cw_ref[2:3, :] + _shifted(dwin, -2) * cw_ref[3:4, :]).astype(BF16)
            xm1, x0, xp1, xp2 = _conv_taps(zxp[pl.ds(t0, RT + 2 * PADR), :])
            sm = lambda v: jnp.sum(v, axis=0, keepdims=True)
            return c0 + sm(d0 * xm1), c1 + sm(d0 * x0), c2 + sm(d0 * xp1), c3 + sm(d0 * xp2), cb_ + sm(d0)

        c0, c1, c2, c3, cb_ = lax.fori_loop(0, nt, conv_bwd, (z1, z1, z1, z1, z1))
        dcw_ref[...] = jnp.concatenate([c0, c1, c2, c3], axis=0)
        dcb_ref[...] = cb_

    col = pl.BlockSpec((S, HD), lambda h: (0, h))
    head = lambda h: (0, h)
    wspec = pl.BlockSpec((2, None, HD, HD), lambda h: (0, h, 0, 0))
    return pl.pallas_call(
        body, name="lru_bwd", grid=(NH,),
        in_specs=[pl.BlockSpec(memory_space=pl.ANY), col, pl.BlockSpec((3, S, HD), lambda h: (1, 0, h)), col, col] + _lru_specs(S),
        out_specs=[pl.BlockSpec((3, S, HD), lambda h: (1, 0, h)), pl.BlockSpec((4, HD), head), pl.BlockSpec((1, HD), head),
                   wspec, pl.BlockSpec((2, HD), head), wspec, pl.BlockSpec((2, HD), head), pl.BlockSpec((2, HD), head)],
        out_shape=[jax.ShapeDtypeStruct((6, S, D), BF16), jax.ShapeDtypeStruct((4, D), F32), jax.ShapeDtypeStruct((1, D), F32),
                   jax.ShapeDtypeStruct((2, NH, HD, HD), F32), jax.ShapeDtypeStruct((2, D), F32),
                   jax.ShapeDtypeStruct((2, NH, HD, HD), F32), jax.ShapeDtypeStruct((2, D), F32), jax.ShapeDtypeStruct((2, D), F32)],
        scratch_shapes=[pltpu.VMEM((S + 2 * PADR, HD), F32), pltpu.VMEM((S, HD), F32), pltpu.VMEM((S, HD), F32),
                        pltpu.VMEM((S + 2 * PADR, HD), F32), pltpu.VMEM((2, S, HD), F32), pltpu.VMEM((2, S, HD), F32)],
        input_output_aliases={0: 0},
        compiler_params=_cp(("parallel",)),
    )(dz6, dm, z6, h0, h1, cw, cb, wr, br, wi, bi, lam)


LAYER_SMALL = ("norm1_g", "gmlp_ln_g", "gmlp_ln_b", "gmlp_w_s", "gmlp_b_s", "conv_w", "conv_b",
               "lru_w_r", "lru_b_r", "lru_w_i", "lru_b_i", "lru_lambda", "norm2_g")


def _forward_layer(l, x, p, wb, after=(), rest=None, near_end=None):
    g1, g2 = p["norm1_g"][l][None], p["norm2_g"][l][None]
    ws_b = p["gmlp_w_s"][l].astype(BF16)
    tm = dict(ws_b=ws_b, wst_b=jnp.swapaxes(ws_b, 1, 2), bs_b=jnp.repeat(p["gmlp_b_s"][l].T, HD, axis=1),
              lg=p["gmlp_ln_g"][l][None], lb=p["gmlp_ln_b"][l][None])
    lru = (p["conv_w"][l], p["conv_b"][l][None], p["lru_w_r"][l].astype(BF16), p["lru_b_r"][l],
           p["lru_w_i"][l].astype(BF16), p["lru_b_i"][l], p["lru_lambda"][l])
    z6, hn1 = _mm_in(x, g1, wb["w_in"], l, after)
    ya = _gmlp_fwd(z6, tm["ws_b"], tm["bs_b"], tm["lg"], tm["lb"])
    merged, h0, h1 = _lru_fwd(z6, ya, *lru)
    if rest is not None:
        wb = dict(wb, **rest(merged))
    x1 = _mm_res(merged, wb["w_out"], x, l, "mm_out")
    gu, ff, hn2 = _mm_ffn_in(x1, g2, wb["w_ffn_in"], l)
    x2 = _mm_res(ff, wb["w_ffn_out"], x1, l, "mm_ffn_out", () if near_end is None else tuple(near_end(gu)))
    return x2, dict(x=x, z6=z6, h0=h0, h1=h1, merged=merged, x1=x1, gu=gu, ff=ff, g1=g1, g2=g2, tm=tm, lru=lru,
                    hn1=hn1, hn2=hn2, wb=wb)


def _backward_layer(l, dx, s, after=(), midway=None, late=None):
    S = dx.shape[0]
    tm, wb = s["tm"], s["wb"]
    g2 = s["g2"]
    dgu = _bwd_ffn_out(dx, wb["w_ffn_out"], s["gu"], l, after)
    tmb = min(TM_BIG, S)
    dwfo = _mm_tn(s["ff"], dx, DFF_SH, tmb, f"dw_ffn_out_{l}")
    dx1, dg2 = _mm_nt_rms_bwd(
        dgu, [pl.BlockSpec((None, tmb, DFF_SH), lambda i, k: (k, i, 0))],
        wb["w_ffn_in"], [pl.BlockSpec((None, D, DFF_SH), lambda i, k: (k, 0, 0))],
        4, tmb, s["x1"], g2, dx, f"bwd_ffn_in_{l}")
    dwfi = _dw_ffn_in(s["hn2"], dgu, l)
    dmg, dwo = _bwd_out(dx1, wb["w_out"], s["merged"], l)
    mid = () if midway is None else tuple(midway([dwo, dwfi, dwfo]))
    dz6, dws, dbs, dlg, dlb = _gmlp_bwd(dmg, s["z6"], tm["ws_b"], tm["wst_b"], tm["bs_b"], tm["lg"], tm["lb"], mid)
    dz6, dcw, dcb, dwr, dbr, dwi, dbi, dlam = _lru_bwd(dz6, dmg, s["z6"], s["h0"], s["h1"], *s["lru"])

    sub = 3

    def dz_tile(j):
        return pl.BlockSpec((None, tmb, 512), lambda i, k: ((sub * k + j) // 2, i, (sub * k + j) % 2))

    def w_tile(j):
        def w_map(i, k):
            sh, tl = _in_tile(sub * k + j)
            return (sh, 0, tl)
        return pl.BlockSpec((None, D, 512), w_map)

    dwin = _dw_in(s["hn1"], dz6, l)
    tail = () if late is None else tuple(late([dwin]))
    dx0, dg1 = _mm_nt_rms_bwd(
        dz6, [dz_tile(j) for j in range(sub)], wb["w_in"], [w_tile(j) for j in range(sub)],
        N_IN_T // sub, tmb, s["x"], s["g1"], dx1, f"bwd_in_{l}", tail, row_part=TM if l else None)
    small = dict(norm1_g=dg1[0], gmlp_ln_g=dlg[0], gmlp_ln_b=dlb[0], gmlp_w_s=dws, gmlp_b_s=dbs[:, :, 0], conv_w=dcw, conv_b=dcb[0],
                 lru_w_r=dwr, lru_b_r=dbr, lru_w_i=dwi, lru_b_i=dbi, lru_lambda=dlam, norm2_g=dg2[0])
    return dx0, [dwin, dwo, dwfi, dwfo], small


def _local_step(x, tgt, p, wbs):
    saved = []
    for l in range(2):
        x, s = _forward_layer(l, x, p, wbs[l])
        saved.append(s)
    dx, loss_v, dfg = _loss_head(x, tgt, p["final_g"][None])
    big, smalls = [None, None], [None, None]
    for l in (1, 0):
        dx, big[l], smalls[l] = _backward_layer(l, dx, saved[l])
    small = {k: jnp.stack([smalls[0][k], smalls[1][k]]) for k in LAYER_SMALL}
    small["final_g"] = dfg[0]
    return loss_v, dx, big, small


def _place():
    x, y, c = lax.axis_index("x"), lax.axis_index("y"), lax.axis_index("c")
    return x, y, c, 2 * x + y


def _chip_at(x, y, d):
    px = 1 - x if d & 2 else x
    py = 1 - y if d & 1 else y
    return px, py, 2 * px + py


HBM = pl.BlockSpec(memory_space=pltpu.HBM)
SEM = pl.BlockSpec(memory_space=pltpu.SEMAPHORE)
DATAFLOW = pltpu.SideEffectType.DATAFLOW_SIDE_EFFECTING


def _in_hbm(a):
    return pltpu.with_memory_space_constraint(a, pltpu.HBM)


def _cast_into(wf, l, chip_arr, name):
    _, rows, cols = wf.shape
    rh = rows // 2

    def body(ch_ref, w_ref, o_ref):
        o_ref[...] = w_ref[...].astype(BF16)

    return pl.pallas_call(
        body, name=name, out_shape=jax.ShapeDtypeStruct((4, 2, rh, cols), BF16),
        grid_spec=pltpu.PrefetchScalarGridSpec(
            num_scalar_prefetch=1, grid=(2,),
            in_specs=[pl.BlockSpec((None, None, rh, cols), lambda h, ch: (l, h, 0, 0))],
            out_specs=pl.BlockSpec((None, None, rh, cols), lambda h, ch: (ch[0], h, 0, 0))),
        compiler_params=_cp(("parallel",)),
    )(chip_arr, wf.reshape(2, 2, rh, cols))


def _half_block(ref, chip, half, to, send_sem, recv_sem):
    blk = ref.at[chip, half]
    return pltpu.make_async_remote_copy(src_ref=blk, dst_ref=blk, send_sem=send_sem, recv_sem=recv_sem,
                                        device_id=to, device_id_type=MESH)


def _gather_weights(bufs, tiny):
    nt = len(bufs)
    n_ici = max(nt * 3, 1)

    def body(*refs):
        tiny_ref = refs[nt]
        o_refs, tiny_o = refs[nt + 1:2 * nt + 1], refs[2 * nt + 1]
        send, recv, fsend, frecv, tsend, trecv, lsem = refs[2 * nt + 2:]
        x, y, c, chip = _place()
        local = pltpu.make_async_copy(tiny_ref, tiny_o.at[chip], lsem)
        local.start()

        def tin(d, origin_chip, to):
            return pltpu.make_async_remote_copy(
                src_ref=tiny_ref, dst_ref=tiny_o.at[origin_chip], send_sem=tsend.at[d - 1], recv_sem=trecv.at[d - 1],
                device_id=to, device_id_type=MESH)

        sends = []
        for t in range(nt):
            for d in (1, 2, 3):
                px, py, _ = _chip_at(x, y, d)
                sends.append(_half_block(o_refs[t], chip, c, (px, py, c), send.at[3 * t + d - 1], recv.at[3 * t + d - 1]))
        for d in (1, 2, 3):
            px, py, _ = _chip_at(x, y, d)
            sends.append(tin(d, chip, (px, py, c)))
        for cp in sends:
            cp.start()
        passed = []
        for t in range(nt):
            for d in (1, 2, 3):
                k = 3 * t + d - 1
                _, _, pchip = _chip_at(x, y, d)
                _half_block(o_refs[t], pchip, c, (x, y, c), send.at[k], recv.at[k]).wait_recv()
                f = _half_block(o_refs[t], pchip, c, (x, y, 1 - c), fsend.at[k], frecv.at[k])
                f.start()
                passed.append(f)
        for t in range(nt):
            for d in (1, 2, 3):
                k = 3 * t + d - 1
                _, _, pchip = _chip_at(x, y, d)
                _half_block(o_refs[t], pchip, 1 - c, (x, y, 1 - c), fsend.at[k], frecv.at[k]).wait_recv()
        for d in (1, 2, 3):
            _, _, pchip = _chip_at(x, y, d)
            tin(d, pchip, (x, y, c)).wait_recv()
        for cp in sends + passed:
            cp.wait_send()
        local.wait()

    out_shape = [jax.ShapeDtypeStruct(b.shape, b.dtype) for b in bufs]
    out_shape.append(jax.ShapeDtypeStruct((4,) + tiny.shape, tiny.dtype))
    outs = pl.pallas_call(
        body, name="gather_weights_0", out_shape=out_shape,
        in_specs=[ANY] * (nt + 1), out_specs=[ANY] * (nt + 1),
        scratch_shapes=[pltpu.SemaphoreType.DMA((n_ici,)), pltpu.SemaphoreType.DMA((n_ici,)),
                        pltpu.SemaphoreType.DMA((n_ici,)), pltpu.SemaphoreType.DMA((n_ici,)),
                        pltpu.SemaphoreType.DMA((3,)), pltpu.SemaphoreType.DMA((3,)), pltpu.SemaphoreType.DMA],
        input_output_aliases={t: t for t in range(nt)},
        compiler_params=_cp(has_side_effects=True),
    )(*bufs, tiny)
    return outs[:nt], outs[nt]


def _gather_start(bufs, tag, after=()):
    nt, na = len(bufs), len(after)

    def body(*refs):
        b_refs = refs[:nt]
        send, recv = refs[nt + na], refs[nt + na + 1]
        token = refs[2 * nt + na + 2]
        x, y, c, chip = _place()
        for t in range(nt):
            for d in (1, 2, 3):
                px, py, _ = _chip_at(x, y, d)
                _half_block(b_refs[t], chip, c, (px, py, c), send.at[3 * t + d - 1], recv.at[3 * t + d - 1]).start()
        token[...] = jnp.zeros_like(token)

    outs = pl.pallas_call(
        body, name=f"gather_start_{tag}",
        out_shape=(pltpu.SemaphoreType.DMA((3 * nt,)), pltpu.SemaphoreType.DMA((3 * nt,)),
                   *[pltpu.HBM(b.shape, b.dtype) for b in bufs], jax.ShapeDtypeStruct((8, 128), F32)),
        in_specs=[HBM] * nt + [ANY] * na, out_specs=(SEM, SEM, *[HBM] * nt, pl.BlockSpec(memory_space=pltpu.VMEM)),
        input_output_aliases={t: 2 + t for t in range(nt)},
        compiler_params=pltpu.CompilerParams(has_side_effects=DATAFLOW),
    )(*[_in_hbm(b) for b in bufs], *after)
    return outs[0], outs[1], list(outs[2:2 + nt]), outs[2 + nt]


def _gather_wait(send, recv, bufs, after, tag):
    nt = len(bufs)

    def body(*refs):
        b_refs = refs[:nt]
        send_ref, recv_ref = refs[nt], refs[nt + 1]
        x, y, c, chip = _place()
        for t in range(nt):
            for d in (1, 2, 3):
                k = 3 * t + d - 1
                px, py, pchip = _chip_at(x, y, d)
                _half_block(b_refs[t], chip, c, (px, py, c), send_ref.at[k], recv_ref.at[k]).wait_send()
                _half_block(b_refs[t], pchip, c, (px, py, c), send_ref.at[k], recv_ref.at[k]).wait_recv()

    outs = pl.pallas_call(
        body, name=f"gather_wait_{tag}", out_shape=[pltpu.HBM(b.shape, b.dtype) for b in bufs],
        in_specs=[HBM] * nt + [SEM, SEM, ANY], out_specs=[HBM] * nt,
        input_output_aliases={t: t for t in range(nt)},
        compiler_params=pltpu.CompilerParams(has_side_effects=DATAFLOW),
    )(*bufs, send, recv, after)
    return list(outs)


def _gather_pass_on(bufs, tag):
    nt = len(bufs)

    def body(*refs):
        o_refs = refs[nt:2 * nt]
        fsend, frecv = refs[2 * nt:]
        x, y, c, _ = _place()
        cps = []
        for t in range(nt):
            for d in (1, 2, 3):
                k = 3 * t + d - 1
                _, _, pchip = _chip_at(x, y, d)
                cps.append(_half_block(o_refs[t], pchip, c, (x, y, 1 - c), fsend.at[k], frecv.at[k]))
        for cp in cps:
            cp.start()
        for t in range(nt):
            for d in (1, 2, 3):
                k = 3 * t + d - 1
                _, _, pchip = _chip_at(x, y, d)
                _half_block(o_refs[t], pchip, 1 - c, (x, y, 1 - c), fsend.at[k], frecv.at[k]).wait_recv()
        for cp in cps:
            cp.wait_send()

    return pl.pallas_call(
        body, name=f"gather_pass_on_{tag}", out_shape=[jax.ShapeDtypeStruct(b.shape, b.dtype) for b in bufs],
        in_specs=[ANY] * nt, out_specs=[ANY] * nt,
        scratch_shapes=[pltpu.SemaphoreType.DMA((3 * nt,)), pltpu.SemaphoreType.DMA((3 * nt,))],
        input_output_aliases={t: t for t in range(nt)},
        compiler_params=_cp(has_side_effects=True),
    )(*bufs)


def _to_sibling_halves(gs, l):
    nt = len(gs)

    def body(*refs):
        g_refs, o_refs = refs[:nt], refs[nt:2 * nt]
        send, recv = refs[2 * nt:]
        x, y, c, _ = _place()
        cps = [pltpu.make_async_remote_copy(
            src_ref=g_refs[t].at[k, 1 - c], dst_ref=o_refs[t].at[k], send_sem=send.at[4 * t + k], recv_sem=recv.at[4 * t + k],
            device_id=(x, y, 1 - c), device_id_type=MESH) for t in range(nt) for k in range(4)]
        for cp in cps:
            cp.start()
        for cp in cps:
            cp.wait()

    return pl.pallas_call(
        body, name=f"grads_to_sibling_{l}", out_shape=[jax.ShapeDtypeStruct((4,) + g.shape[2:], g.dtype) for g in gs],
        in_specs=[ANY] * nt, out_specs=[ANY] * nt,
        scratch_shapes=[pltpu.SemaphoreType.DMA((4 * nt,)), pltpu.SemaphoreType.DMA((4 * nt,))],
        compiler_params=_cp(has_side_effects=True),
    )(*gs)


def _chip_copy(c_ref, land_ref, x, y, c, d, send_sem, recv_sem):
    px, py, pchip = _chip_at(x, y, d)
    return pltpu.make_async_remote_copy(src_ref=c_ref.at[pchip], dst_ref=land_ref.at[d - 1], send_sem=send_sem, recv_sem=recv_sem,
                                        device_id=(px, py, c), device_id_type=MESH)


def _exchange_start(srcs, lands, copies, nsem, name):
    ns, n = len(srcs), len(srcs) + len(lands)

    def body(*refs):
        for cp in copies(refs[:ns], refs[ns:n], refs[n], refs[n + 1]):
            cp.start()
        token = refs[2 * n + 2]
        token[...] = jnp.zeros_like(token)

    outs = pl.pallas_call(
        body, name=name,
        out_shape=(pltpu.SemaphoreType.DMA((nsem,)), pltpu.SemaphoreType.DMA((nsem,)),
                   *[pltpu.HBM(a.shape, a.dtype) for a in list(srcs) + list(lands)], jax.ShapeDtypeStruct((8, 128), F32)),
        in_specs=[HBM] * n, out_specs=(SEM, SEM, *[HBM] * n, pl.BlockSpec(memory_space=pltpu.VMEM)),
        input_output_aliases={i: 2 + i for i in range(n)},
        compiler_params=pltpu.CompilerParams(has_side_effects=DATAFLOW),
    )(*[_in_hbm(a) for a in list(srcs) + list(lands)])
    return outs[0], outs[1], list(outs[2:2 + ns]), list(outs[2 + ns:2 + n]), outs[2 + n]


def _exchange_wait(send, recv, srcs, lands, after, copies, name):
    ns, n = len(srcs), len(srcs) + len(lands)

    def body(*refs):
        for cp in copies(refs[:ns], refs[ns:n], refs[n], refs[n + 1]):
            cp.wait_send()
            cp.wait_recv()

    outs = pl.pallas_call(
        body, name=name, out_shape=[pltpu.HBM(a.shape, a.dtype) for a in list(srcs) + list(lands)],
        in_specs=[HBM] * n + [SEM, SEM, ANY], out_specs=[HBM] * n,
        input_output_aliases={i: i for i in range(n)},
        compiler_params=pltpu.CompilerParams(has_side_effects=DATAFLOW),
    )(*srcs, *lands, send, recv, after)
    return list(outs[:ns]), list(outs[ns:])


def _pass_on_copies(b_refs, land_refs, send, recv):
    del land_refs
    x, y, c, _ = _place()
    return [_half_block(b_refs[t], _chip_at(x, y, d)[2], c, (x, y, 1 - c), send.at[3 * t + d - 1], recv.at[3 * t + d - 1])
            for t in range(len(b_refs)) for d in (1, 2, 3)]


def _chips_copies(c_refs, land_refs, send, recv):
    x, y, c, _ = _place()
    return [_chip_copy(c_refs[t], land_refs[t], x, y, c, d, send.at[3 * t + d - 1], recv.at[3 * t + d - 1])
            for t in range(len(c_refs)) for d in (1, 2, 3)]


def _sibling_copies(g_refs, land_refs, send, recv):
    x, y, c, _ = _place()
    return [pltpu.make_async_remote_copy(
        src_ref=g_refs[t].at[k, 1 - c], dst_ref=land_refs[t].at[k], send_sem=send.at[4 * t + k], recv_sem=recv.at[4 * t + k],
        device_id=(x, y, 1 - c), device_id_type=MESH) for t in range(len(g_refs)) for k in range(4)]


def _join_halves(fs, l):
    nt = len(fs)

    def body(*refs):
        o_refs = refs[nt:2 * nt]
        send, recv = refs[2 * nt:]
        x, y, c, _ = _place()
        cps = [pltpu.make_async_remote_copy(
            src_ref=o_refs[t].at[c], dst_ref=o_refs[t].at[c], send_sem=send.at[t], recv_sem=recv.at[t],
            device_id=(x, y, 1 - c), device_id_type=MESH) for t in range(nt)]
        for cp in cps:
            cp.start()
        for cp in cps:
            cp.wait()

    return pl.pallas_call(
        body, name=f"grads_join_{l}", out_shape=[jax.ShapeDtypeStruct(a.shape, a.dtype) for a in fs],
        in_specs=[ANY] * nt, out_specs=[ANY] * nt,
        scratch_shapes=[pltpu.SemaphoreType.DMA((nt,)), pltpu.SemaphoreType.DMA((nt,))],
        input_output_aliases={t: t for t in range(nt)},
        compiler_params=_cp(has_side_effects=True),
    )(*fs)


def _add_half(g, r, c_arr, name):
    _, _, rh, cols = g.shape

    def body(c_ref, g_ref, r_ref, o_ref):
        o_ref[...] = (g_ref[...] + r_ref[...]).astype(BF16)

    blk = pl.BlockSpec((None, rh, cols), lambda k, cr: (k, 0, 0))
    return pl.pallas_call(
        body, name=name, out_shape=jax.ShapeDtypeStruct((4, rh, cols), BF16),
        grid_spec=pltpu.PrefetchScalarGridSpec(
            num_scalar_prefetch=1, grid=(4,),
            in_specs=[pl.BlockSpec((None, None, rh, cols), lambda k, cr: (k, cr[0], 0, 0)), blk], out_specs=blk),
        compiler_params=_cp(("parallel",)),
    )(c_arr, g, r)


def _sum_chips(cs, r3, place_arr, name):
    _, rh, cols = cs.shape
    rb = rh // 2

    def body(pl_ref, a_ref, r0_ref, r1_ref, r2_ref, o_ref):
        up = lambda ref: ref[...].astype(F32)
        o_ref[...] = ((up(a_ref) + up(r0_ref)) + up(r1_ref)) + up(r2_ref)

    def slot(d):
        return pl.BlockSpec((None, rb, cols), lambda i, pa: (d, i, 0))

    return pl.pallas_call(
        body, name=name, out_shape=jax.ShapeDtypeStruct((2, rh, cols), F32),
        grid_spec=pltpu.PrefetchScalarGridSpec(
            num_scalar_prefetch=1, grid=(2,),
            in_specs=[pl.BlockSpec((None, rb, cols), lambda i, pa: (pa[0], i, 0)), slot(0), slot(1), slot(2)],
            out_specs=pl.BlockSpec((None, rb, cols), lambda i, pa: (pa[1], i, 0))),
        compiler_params=_cp(("parallel",)),
    )(place_arr, cs, r3, r3, r3)


def _allreduce_small(pack):
    rows = pack.shape[0]
    hr = rows // 2

    def body(p_ref, o_ref, sib, slots, s1, r1, s2, r2, s3, r3):
        x, y, c, chip = _place()
        sibling = (x, y, 1 - c)
        ex = pltpu.make_async_remote_copy(src_ref=p_ref, dst_ref=sib, send_sem=s1, recv_sem=r1,
                                          device_id=sibling, device_id_type=MESH)
        ex.start()
        ex.wait()
        half = pl.ds(pl.multiple_of(c * hr, 16), hr)
        slots[0] = (p_ref[half, :] + sib[half, :]).astype(BF16)
        cps = []
        for d in (1, 2, 3):
            px, py, _ = _chip_at(x, y, d)
            cps.append(pltpu.make_async_remote_copy(
                src_ref=slots.at[0], dst_ref=slots.at[d], send_sem=s2.at[d - 1], recv_sem=r2.at[d - 1],
                device_id=(px, py, c), device_id_type=MESH))
        for cp in cps:
            cp.start()
        for cp in cps:
            cp.wait()
        tot = slots[chip].astype(F32)
        for k in (1, 2, 3):
            tot = tot + slots[jnp.bitwise_xor(chip, k)].astype(F32)
        o_ref[half, :] = tot
        back = pltpu.make_async_remote_copy(src_ref=o_ref.at[half, :], dst_ref=o_ref.at[half, :], send_sem=s3, recv_sem=r3,
                                            device_id=sibling, device_id_type=MESH)
        back.start()
        back.wait()

    vm = pl.BlockSpec(memory_space=pltpu.VMEM)
    return pl.pallas_call(
        body, name="allreduce_small", out_shape=jax.ShapeDtypeStruct((rows, 128), F32),
        in_specs=[vm], out_specs=vm,
        scratch_shapes=[pltpu.VMEM((rows, 128), F32), pltpu.VMEM((4, hr, 128), BF16),
                        pltpu.SemaphoreType.DMA, pltpu.SemaphoreType.DMA, pltpu.SemaphoreType.DMA((3,)), pltpu.SemaphoreType.DMA((3,)),
                        pltpu.SemaphoreType.DMA, pltpu.SemaphoreType.DMA],
        compiler_params=_cp(has_side_effects=True),
    )(pack)


def _adam_math(gv, wv, mv, vv):
    m2 = ADAM_B1 * mv + (1.0 - ADAM_B1) * gv
    v2 = ADAM_B2 * vv + (1.0 - ADAM_B2) * (gv * gv)
    m_hat = m2 / (1.0 - ADAM_B1 ** ADAM_STEP)
    v_hat = v2 / (1.0 - ADAM_B2 ** ADAM_STEP)
    return -ADAM_LR * (m_hat / (jnp.sqrt(v_hat) + ADAM_EPS) + ADAM_WD * wv), m2, v2


def _adam(g, w, m, v, name):
    rows, cols = g.shape
    rb = rows // 4

    def body(g_ref, w_ref, m_ref, v_ref, d_ref, m2_ref, v2_ref):
        d_ref[...], m2_ref[...], v2_ref[...] = _adam_math(g_ref[...], w_ref[...], m_ref[...], v_ref[...])

    blk = pl.BlockSpec((rb, cols), lambda i: (i, 0))
    shp = jax.ShapeDtypeStruct((rows, cols), F32)
    return pl.pallas_call(
        body, name=name, grid=(4,), in_specs=[blk] * 4, out_specs=[blk] * 3, out_shape=[shp] * 3,
        compiler_params=_cp(("parallel",)),
    )(g, w, m, v)


def _adam_layer(g, w, m, v, l, prev, name):
    rows, cols = g.shape
    rb = rows // 4

    def body(g_ref, w_ref, m_ref, v_ref, *rest):
        go_ref, d_ref, m2_ref, v2_ref = rest[-4:]
        gv = g_ref[...]
        go_ref[...] = gv
        d_ref[...], m2_ref[...], v2_ref[...] = _adam_math(gv, w_ref[...], m_ref[...], v_ref[...])

    lay = pl.BlockSpec((None, rb, cols), lambda i: (l, i, 0))
    shp = jax.ShapeDtypeStruct((2, rows, cols), F32)
    prev = () if prev is None else tuple(prev)
    return pl.pallas_call(
        body, name=name, grid=(4,), in_specs=[pl.BlockSpec((rb, cols), lambda i: (i, 0)), lay, lay, lay] + [ANY] * len(prev),
        out_specs=[lay] * 4, out_shape=[shp] * 4,
        input_output_aliases={4 + j: j for j in range(len(prev))},
        compiler_params=_cp(("parallel",)),
    )(g, w, m, v, *prev)


def _rows128(a):
    return a.reshape(-1, 128)


def _pack(arrs, mult):
    parts = [_rows128(a) for a in arrs]
    rows = sum(q.shape[0] for q in parts)
    pad = -rows % mult
    if pad:
        parts.append(jnp.zeros((pad, 128), F32))
    return jnp.concatenate(parts, axis=0)


def _unpack(pack, shapes):
    out, o = [], 0
    for s in shapes:
        n = 1
        for e in s:
            n *= e
        out.append(pack[o:o + n // 128].reshape(s))
        o += n // 128
    return out


WEIGHTS = ['norm1_g', 'w_in', 'gmlp_ln_g', 'gmlp_ln_b', 'gmlp_w_s', 'gmlp_b_s', 'conv_w', 'conv_b', 'lru_w_r', 'lru_b_r', 'lru_w_i',
           'lru_b_i', 'lru_lambda', 'w_out', 'norm2_g', 'w_ffn_in', 'w_ffn_out', 'final_g']
BIG = ['w_in', 'w_out', 'w_ffn_in', 'w_ffn_out']
SMALL = [n for n in WEIGHTS if n not in BIG]
CHIP_SHARDED_SMALL = ['conv_w', 'lru_b_r', 'lru_b_i', 'lru_lambda']


def kernel(x, norm1_g, w_in, gmlp_ln_g, gmlp_ln_b, gmlp_w_s, gmlp_b_s, conv_w, conv_b, lru_w_r, lru_b_r, lru_w_i, lru_b_i, lru_lambda, w_out, norm2_g, w_ffn_in, w_ffn_out, final_g, loss_target, m_norm1_g, m_w_in, m_gmlp_ln_g, m_gmlp_ln_b, m_gmlp_w_s, m_gmlp_b_s, m_conv_w, m_conv_b, m_lru_w_r, m_lru_b_r, m_lru_w_i, m_lru_b_i, m_lru_lambda, m_w_out, m_norm2_g, m_w_ffn_in, m_w_ffn_out, m_final_g, v_norm1_g, v_w_in, v_gmlp_ln_g, v_gmlp_ln_b, v_gmlp_w_s, v_gmlp_b_s, v_conv_w, v_conv_b, v_lru_w_r, v_lru_b_r, v_lru_w_i, v_lru_b_i, v_lru_lambda, v_w_out, v_norm2_g, v_w_ffn_in, v_w_ffn_out, v_final_g):
    a = dict(locals())
    w = {n: a[n] for n in WEIGHTS}
    mom = {n: a["m_" + n] for n in WEIGHTS}
    var = {n: a["v_" + n] for n in WEIGHTS}
    _, _, c, chip = _place()
    c_arr, chip_arr = jnp.reshape(c, (1,)).astype(jnp.int32), jnp.reshape(chip, (1,)).astype(jnp.int32)
    place_arr = jnp.stack([chip, c]).astype(jnp.int32)

    first, rest = BIG[:1], BIG[1:]

    def as_weights(names, full):
        wb = {n: f.reshape(4, 2 * f.shape[2], f.shape[3]) for n, f in zip(names, full)}
        if "w_out" in wb:
            wb["w_out"] = wb["w_out"].reshape(D, D)
            wb["w_ffn_out"] = wb["w_ffn_out"].reshape(DFF, D)
        return wb

    def cast(n, l):
        return _cast_into(w[n], l, chip_arr, f"cast_{n}_{l}")

    def landed(fly, names, after, tag):
        return as_weights(names, _gather_pass_on(_gather_wait(fly[0], fly[1], fly[2], after, tag), tag))

    tiny = _pack([w[n] for n in CHIP_SHARDED_SMALL], 8)
    _, tiny_full = _gather_weights([], tiny)
    fly_in = _gather_start([cast("w_in", 0)], "in", after=(tiny_full,))
    fly0 = _gather_start([cast(n, 0) for n in rest], "0", after=(fly_in[3],))
    bufs1 = [cast(n, 1) for n in BIG]
    fly1 = _gather_start(bufs1, "1", after=(fly0[3],))
    p = {n: w[n] for n in SMALL}
    parts = [_unpack(tiny_full[k], [w[n].shape for n in CHIP_SHARDED_SMALL]) for k in range(4)]
    for i, n in enumerate(CHIP_SHARDED_SMALL):
        p[n] = jnp.concatenate([parts[k][i] for k in range(4)], axis=-1)

    passing = {}

    def pass_on_1(gu):
        bufs = _gather_wait(fly1[0], fly1[1], fly1[2], gu, "1")
        passing[1] = _exchange_start(bufs, [], _pass_on_copies, 3 * len(bufs), "gather_pass_on_start_1")
        return (passing[1][-1],)

    xa, saved0 = _forward_layer(0, x[0], p, landed(fly_in, first, fly1[3], "in"), after=(fly0[3], fly1[3]),
                                rest=lambda merged: landed(fly0, rest, merged, "0"), near_end=pass_on_1)
    send, recv, bufs1, _, _ = passing[1]
    xb, saved1 = _forward_layer(
        1, xa, p, as_weights(BIG, _exchange_wait(send, recv, bufs1, [], xa, _pass_on_copies, "gather_pass_on_wait_1")[0]))
    dxb, loss_v, dfg = _loss_head(xb, loss_target[0], p["final_g"][None])
    loss = lax.psum(loss_v[0, 0], ("x", "y", "c"))

    out, flying = {}, {}

    def halves(grads):
        return [g.reshape(4, 2, -1, g.shape[-1]) for g in grads]

    def sibling_start(grads, names, l, tag):
        gs = halves(grads)
        lands = [lax.empty((4,) + g.shape[2:], g.dtype) for g in gs]
        flying["s" + tag] = (names, l) + tuple(
            _exchange_start(gs, lands, _sibling_copies, 4 * len(gs), f"grads_to_sibling_start_{tag}"))
        return (flying["s" + tag][-1],)

    def chips_start(gs, from_sib, names, l, tag):
        cs = [_add_half(g, r, c_arr, f"add_half_{n}_{l}") for n, g, r in zip(names, gs, from_sib)]
        lands = [lax.empty((3,) + a.shape[1:], a.dtype) for a in cs]
        flying[tag] = (names, l) + tuple(_exchange_start(cs, lands, _chips_copies, 3 * len(cs), f"grads_to_chips_start_{tag}"))
        return (flying[tag][-1],)

    def sibling_finish(tag, after):
        names, l, send, recv, gs, lands, _ = flying["s" + tag]
        gs, from_sib = _exchange_wait(send, recv, gs, lands, after, _sibling_copies, f"grads_to_sibling_wait_{tag}")
        return chips_start(gs, from_sib, names, l, tag)

    def reduce_start(grads, names, l, tag):
        gs = halves(grads)
        return chips_start(gs, _to_sibling_halves(gs, tag), names, l, tag)

    def reduce_finish(tags, after):
        names, ts = [], []
        for tag in tags:
            names_t, l, send, recv, cs, lands, _ = flying[tag]
            cs, lands = _exchange_wait(send, recv, cs, lands, after, _chips_copies, f"grads_to_chips_wait_{tag}")
            ts += [_sum_chips(cc, r3, place_arr, f"sum_chips_{n}_{l}") for n, cc, r3 in zip(names_t, cs, lands)]
            names += names_t
        for n, j in zip(names, _join_halves(ts, tags[0])):
            out[n] = _adam_layer(j.reshape(w[n].shape[1:]), w[n], mom[n], var[n], l, out.get(n), f"adam_{n}_{l}")

    def late1(grads):
        return sibling_finish("1a", grads[0]) + sibling_start(grads, first, 1, "1b")

    def midway0(grads):
        reduce_finish(("1a", "1b"), grads[0])
        return reduce_start(grads, rest, 0, "0a")

    dxa, big1, small1 = _backward_layer(1, dxb, saved1, midway=lambda grads: sibling_start(grads, rest, 1, "1a"), late=late1)
    dx, big0, small0 = _backward_layer(0, dxa, saved0, after=sibling_finish("1b", dxa), midway=midway0,
                                       late=lambda grads: reduce_start(grads, first, 0, "0b"))
    reduce_finish(("0a", "0b"), dx)
    small = {k: jnp.stack([small0[k], small1[k]]) for k in LAYER_SMALL}
    small["final_g"] = dfg[0]

    full_shapes = [small[n].shape for n in SMALL]
    red = _unpack(_allreduce_small(_pack([small[n] for n in SMALL], 32)), full_shapes)
    g_small = []
    for n, g in zip(SMALL, red):
        if n in CHIP_SHARDED_SMALL:
            g = lax.dynamic_slice_in_dim(g, chip * w[n].shape[-1], w[n].shape[-1], axis=g.ndim - 1)
        g_small.append(g)
    shapes = [w[n].shape for n in SMALL]
    packs = [_pack(lst, 32) for lst in (g_small, [w[n] for n in SMALL], [mom[n] for n in SMALL], [var[n] for n in SMALL])]
    upd = [_unpack(u, shapes) for u in _adam(*packs, "adam_small")]
    for i, n in enumerate(SMALL):
        out[n] = [g_small[i], upd[0][i], upd[1][i], upd[2][i]]

    return (loss, dx[None]) + tuple(out[n][i] for i in range(4) for n in WEIGHTS)
```

```python
import functools

import jax
import jax.numpy as jnp
from jax import lax
from jax.experimental import pallas as pl
from jax.experimental.pallas import tpu as pltpu

F32 = jnp.float32
BF16 = jnp.bfloat16
MESH = pl.DeviceIdType.MESH

D = 1024
NH = 8
HD = 128
CHUNK = 128
N_IN_T = 12
DFF = 2816
DFF_SH = 1408
EPS = 1e-6
LRU_C = 8.0
ADAM_LR, ADAM_B1, ADAM_B2, ADAM_EPS, ADAM_WD, ADAM_STEP = 0.001, 0.9, 0.999, 1e-08, 0.01, 10

TM = 512
TM_BIG = 1024
RT = 128
PADR = 8
VMEM_LIMIT = 56 * 1024 * 1024


def _cp(sem=None, **kw):
    if sem is not None:
        kw["dimension_semantics"] = sem
    return pltpu.CompilerParams(vmem_limit_bytes=VMEM_LIMIT, **kw)


_GC = 0.7978845608028654


def _sigmoid(x):
    return 0.5 * jnp.tanh(0.5 * x) + 0.5


def _gelu(x):
    return 0.5 * x * (1.0 + jnp.tanh(_GC * (x + 0.044715 * x * x * x)))


def _gelu_and_grad(x):
    t = jnp.tanh(_GC * (x + 0.044715 * x * x * x))
    g = 0.5 * x * (1.0 + t)
    dg = 0.5 * (1.0 + t) + 0.5 * x * (1.0 - t * t) * _GC * (1.0 + 3 * 0.044715 * x * x)
    return g, dg


def _softplus_neg(lam):
    y = jnp.exp(-jnp.abs(lam))
    u = 1.0 + y
    l1p = jnp.where(u == 1.0, y, jnp.log(u) * y / (u - 1.0))
    return jnp.maximum(-lam, 0.0) + l1p


def _dot(a, b):
    return jnp.dot(a, b, preferred_element_type=F32)


def _dot_nt(a, b):
    return lax.dot_general(a, b, (((1,), (1,)), ((), ())), preferred_element_type=F32)


def _dot_tn(a, b):
    return lax.dot_general(a, b, (((0,), (0,)), ((), ())), preferred_element_type=F32)


def _rms_hat(x):
    r = lax.rsqrt(jnp.mean(x * x, axis=-1, keepdims=True) + EPS)
    return x * r, r


def _rms_bwd(dh, x, g):
    xh, r = _rms_hat(x)
    dxh = dh * g
    dx = r * (dxh - xh * jnp.mean(dxh * xh, axis=-1, keepdims=True))
    return dx, jnp.sum(dh * xh, axis=0, keepdims=True)


def _norm_into(x_ref, g_ref, h_ref):
    xh, _ = _rms_hat(x_ref[...])
    h_ref[...] = (xh * g_ref[...]).astype(BF16)


def _in_tile(j):
    m, hf = j // 2, j % 2
    orig = jnp.where(m < 2, m, jnp.where(m == 2, 4, jnp.where(m < 5, m - 1, 5)))
    t = orig * 2 + hf
    return t // 3, t % 3


ANY = pl.BlockSpec(memory_space=pl.ANY)


def _mm_in(x, g, w_in, l, after=()):
    S = x.shape[0]
    tm = min(2 * TM_BIG, S)

    def body(x_ref, g_ref, w0_ref, w1_ref, *rest):
        o_ref, h_ref = rest[-2:]

        @pl.when(pl.program_id(1) == 0)
        def _():
            _norm_into(x_ref, g_ref, h_ref)
        rp = min(TM, tm)
        for r0 in range(0, tm, rp):
            hv = h_ref[r0:r0 + rp, :]
            o_ref[r0:r0 + rp, 0:512] = _dot(hv, w0_ref[...]).astype(BF16)
            o_ref[r0:r0 + rp, 512:1024] = _dot(hv, w1_ref[...]).astype(BF16)

    def w_tile(hf):
        def w_map(i, m):
            sh, tl = _in_tile(2 * m + hf)
            return (sh, 0, tl)
        return pl.BlockSpec((None, D, 512), w_map)

    return pl.pallas_call(
        body, name=f"mm_in_{l}", grid=(S // tm, 6),
        in_specs=[pl.BlockSpec((tm, D), lambda i, m: (i, 0)), pl.BlockSpec((1, D), lambda i, m: (0, 0)),
                  w_tile(0), w_tile(1)] + [ANY] * len(after),
        out_specs=[pl.BlockSpec((None, tm, D), lambda i, m: (m, i, 0)), pl.BlockSpec((tm, D), lambda i, m: (i, 0))],
        out_shape=[jax.ShapeDtypeStruct((6, S, D), BF16), jax.ShapeDtypeStruct((S, D), BF16)],
        compiler_params=_cp(("parallel", "arbitrary")),
    )(x, g, w_in, w_in, *after)


def _mm_res(a, w, res, l, name, after=()):
    S, K = a.shape

    tm = TM

    def body(a_ref, w_ref, r_ref, *rest):
        rest[-1][...] = r_ref[...] + _dot(a_ref[...], w_ref[...])

    return pl.pallas_call(
        body, name=f"{name}_{l}", grid=(S // tm,),
        in_specs=[pl.BlockSpec((tm, K), lambda i: (i, 0)), pl.BlockSpec((K, D), lambda i: (0, 0)),
                  pl.BlockSpec((tm, D), lambda i: (i, 0))] + [ANY] * len(after),
        out_specs=pl.BlockSpec((tm, D), lambda i: (i, 0)),
        out_shape=jax.ShapeDtypeStruct((S, D), F32),
        compiler_params=_cp(("parallel",)),
    )(a, w, res, *after)


def _mm_ffn_in(x, g, w_fi, l):
    S = x.shape[0]

    tm = min(TM_BIG, S)

    def body(x_ref, g_ref, w_ref, gu_ref, ff_ref, h_ref):
        @pl.when(pl.program_id(1) == 0)
        def _():
            _norm_into(x_ref, g_ref, h_ref)
        for r0 in range(0, tm, TM):
            rows = slice(r0, r0 + TM)
            hv = h_ref[rows, :]
            ga = _dot(hv, w_ref[0])
            gb = _dot(hv, w_ref[1])
            gu_ref[0, rows, :] = ga.astype(BF16)
            gu_ref[1, rows, :] = gb.astype(BF16)
            ff_ref[rows, :] = (ga * _sigmoid(ga) * gb).astype(BF16)

    gu, ff, h = pl.pallas_call(
        body, name=f"mm_ffn_in_{l}", grid=(S // tm, 2),
        in_specs=[pl.BlockSpec((tm, D), lambda i, s: (i, 0)), pl.BlockSpec((1, D), lambda i, s: (0, 0)),
                  pl.BlockSpec((2, None, D, DFF_SH), lambda i, s: (0, s, 0, 0))],
        out_specs=[pl.BlockSpec((2, None, tm, DFF_SH), lambda i, s: (0, s, i, 0)),
                   pl.BlockSpec((tm, DFF_SH), lambda i, s: (i, s)),
                   pl.BlockSpec((tm, D), lambda i, s: (i, 0))],
        out_shape=[jax.ShapeDtypeStruct((2, 2, S, DFF_SH), BF16), jax.ShapeDtypeStruct((S, DFF), BF16),
                   jax.ShapeDtypeStruct((S, D), BF16)],
        compiler_params=_cp(("parallel", "arbitrary")),
    )(x, g, w_fi.reshape(2, 2, D, DFF_SH))
    return gu.reshape(4, S, DFF_SH), ff, h


def _gmlp_fwd(z6, ws_b, bs_b, lg, lb):
    S = z6.shape[1]

    def body(z_ref, ws_ref, bs_ref, lg_ref, lb_ref, o_ref, mix):
        gv = _gelu(z_ref[1].astype(F32))
        xc = gv - jnp.mean(gv, axis=-1, keepdims=True)
        rs = lax.rsqrt(jnp.mean(xc * xc, axis=-1, keepdims=True) + EPS)
        vb = (xc * rs * lg_ref[...] + lb_ref[...]).astype(BF16)
        for gi in range(NH):
            cs = slice(gi * HD, (gi + 1) * HD)
            mix[:, cs] = _dot(ws_ref[gi], vb[:, cs])
        o_ref[...] = (_sigmoid(z_ref[2].astype(F32)) * _gelu(z_ref[0].astype(F32)) * (mix[...] + bs_ref[...])).astype(BF16)

    return pl.pallas_call(
        body, name="gmlp_fwd", grid=(S // CHUNK,),
        in_specs=[pl.BlockSpec((3, CHUNK, D), lambda i: (0, i, 0)), pl.BlockSpec((NH, CHUNK, CHUNK), lambda i: (0, 0, 0)),
                  pl.BlockSpec((CHUNK, D), lambda i: (0, 0)), pl.BlockSpec((1, D), lambda i: (0, 0)),
                  pl.BlockSpec((1, D), lambda i: (0, 0))],
        out_specs=pl.BlockSpec((CHUNK, D), lambda i: (i, 0)),
        out_shape=jax.ShapeDtypeStruct((S, D), BF16),
        scratch_shapes=[pltpu.VMEM((CHUNK, D), F32)],
        compiler_params=_cp(("parallel",)),
    )(z6, ws_b, bs_b, lg, lb)


def _row_iota():
    return lax.broadcasted_iota(jnp.int32, (RT, HD), 0)


SUB = 8
UNROLL = 4
GRAD_ROWS = 512


def _scan_up(a, b, carry):
    row = lax.broadcasted_iota(jnp.int32, (SUB, HD), 0)
    masks = [(d, row >= d) for d in (1, 2, 4)]
    c = jnp.broadcast_to(carry, (SUB, HD))
    hs = []
    for j in range(RT // SUB):
        aj, bj = a[SUB * j:SUB * (j + 1)], b[SUB * j:SUB * (j + 1)]
        for d, m in masks:
            bj = bj + aj * jnp.where(m, pltpu.roll(bj, d, 0), 0.0)
            aj = aj * jnp.where(m, pltpu.roll(aj, d, 0), 1.0)
        h = bj + aj * c
        hs.append(h)
        c = jnp.broadcast_to(h[SUB - 1:SUB, :], (SUB, HD))
    return jnp.concatenate(hs, axis=0), hs[-1][SUB - 1:SUB, :]


def _scan_down(a, b, carry):
    row = lax.broadcasted_iota(jnp.int32, (SUB, HD), 0)
    masks = [(d, row < SUB - d) for d in (1, 2, 4)]
    c = jnp.broadcast_to(carry, (SUB, HD))
    hs = []
    for j in reversed(range(RT // SUB)):
        aj, bj = a[SUB * j:SUB * (j + 1)], b[SUB * j:SUB * (j + 1)]
        for d, m in masks:
            bj = bj + aj * jnp.where(m, pltpu.roll(bj, SUB - d, 0), 0.0)
            aj = aj * jnp.where(m, pltpu.roll(aj, SUB - d, 0), 1.0)
        h = bj + aj * c
        hs.append(h)
        c = jnp.broadcast_to(h[0:1, :], (SUB, HD))
    return jnp.concatenate(hs[::-1], axis=0), hs[-1][0:1, :]


def _decay(r, sp_d):
    log_a = -LRU_C * r * sp_d
    a = jnp.exp(log_a)
    return a, jnp.sqrt(jnp.maximum(-jnp.tanh(log_a) * (a * a + 1.0), 0.0))


def _lru_gates(xc, d, wr_ref, br_ref, wi_ref, bi_ref, sp):
    xb = xc.astype(BF16)
    r = _sigmoid(_dot(xb, wr_ref[d]) + br_ref[d:d + 1, :])
    i = _sigmoid(_dot(xb, wi_ref[d]) + bi_ref[d:d + 1, :])
    a, mult = _decay(r, sp[d:d + 1, :])
    return r, i, a, mult


def _shifted(win, k):
    w = RT + 2 * PADR
    v = win if k == 0 else pltpu.roll(win, (-k) % w, 0)
    return v[PADR:PADR + RT]


def _conv_taps(win):
    return [_shifted(win, k) for k in (-1, 0, 1, 2)]


def _fill_padded(dst, src_ref, S):
    zeros = jnp.zeros((PADR, HD), F32)
    dst[0:PADR, :] = zeros
    dst[PADR + S:2 * PADR + S, :] = zeros

    def cp(i, c):
        t0 = pl.multiple_of(i * RT, RT)
        dst[pl.ds(t0 + PADR, RT), :] = src_ref[pl.ds(t0, RT), :].astype(F32)
        return c
    lax.fori_loop(0, S // RT, cp, 0)


def _conv_fwd_all(zxp, xc_s, cw_ref, cb_ref, S):
    def cv(i, c):
        t0 = pl.multiple_of(i * RT, RT)
        xm1, x0, xp1, xp2 = _conv_taps(zxp[pl.ds(t0, RT + 2 * PADR), :])
        xc_s[pl.ds(t0, RT), :] = (cb_ref[...] + xm1 * cw_ref[0:1, :] + x0 * cw_ref[1:2, :]
                                  + xp1 * cw_ref[2:3, :] + xp2 * cw_ref[3:4, :])
        return c
    lax.fori_loop(0, S // RT, cv, 0)


def _lru_specs(S):
    head = lambda h: (0, h)
    return [pl.BlockSpec((4, HD), head), pl.BlockSpec((1, HD), head),
            pl.BlockSpec((2, None, HD, HD), lambda h: (0, h, 0, 0)), pl.BlockSpec((2, HD), head),
            pl.BlockSpec((2, None, HD, HD), lambda h: (0, h, 0, 0)), pl.BlockSpec((2, HD), head),
            pl.BlockSpec((2, HD), head)]


def _lru_fwd(z6, ya, cw, cb, wr, br, wi, bi, lam):
    S = z6.shape[1]
    nt = S // RT

    def body(z_ref, ya_ref, cw_ref, cb_ref, wr_ref, br_ref, wi_ref, bi_ref, lam_ref, mg_ref, h0_ref, h1_ref, zxp, xc_s):
        sp = _softplus_neg(lam_ref[...])
        _fill_padded(zxp, z_ref.at[0], S)
        _conv_fwd_all(zxp, xc_s, cw_ref, cb_ref, S)

        def scans(i, carry):
            cu, cd = carry
            for u in range(UNROLL):
                j = i * UNROLL + u
                ru = pl.ds(pl.multiple_of(j * RT, RT), RT)
                rd = pl.ds(pl.multiple_of((nt - 1 - j) * RT, RT), RT)
                xu, xd = xc_s[ru, :], xc_s[rd, :]
                _, gi, a, mult = _lru_gates(xu, 0, wr_ref, br_ref, wi_ref, bi_ref, sp)
                hu, cu = _scan_up(a, mult * gi * xu, cu)
                h0_ref[ru, :] = hu
                _, gi, a, mult = _lru_gates(xd, 1, wr_ref, br_ref, wi_ref, bi_ref, sp)
                hd, cd = _scan_down(a, mult * gi * xd, cd)
                h1_ref[rd, :] = hd
            return cu, cd
        z1 = jnp.zeros((1, HD), F32)
        lax.fori_loop(0, nt // UNROLL, scans, (z1, z1))

        def merge(i, c):
            rows = pl.ds(pl.multiple_of(i * RT, RT), RT)
            yb = (h0_ref[rows, :] + h1_ref[rows, :]) * _gelu(z_ref[1, rows, :].astype(F32))
            mg_ref[rows, :] = (ya_ref[rows, :].astype(F32) + _sigmoid(z_ref[2, rows, :].astype(F32)) * yb).astype(BF16)
            return c
        lax.fori_loop(0, nt, merge, 0)

    col = pl.BlockSpec((S, HD), lambda h: (0, h))
    return pl.pallas_call(
        body, name="lru_fwd", grid=(NH,),
        in_specs=[pl.BlockSpec((3, S, HD), lambda h: (1, 0, h)), col] + _lru_specs(S),
        out_specs=[col, col, col],
        out_shape=[jax.ShapeDtypeStruct((S, D), BF16), jax.ShapeDtypeStruct((S, D), F32), jax.ShapeDtypeStruct((S, D), F32)],
        scratch_shapes=[pltpu.VMEM((S + 2 * PADR, HD), F32), pltpu.VMEM((S, HD), F32)],
        compiler_params=_cp(("parallel",)),
    )(z6, ya, cw, cb, wr, br, wi, bi, lam)


def _loss_head(x, tgt, g):
    S = x.shape[0]

    def body(x_ref, t_ref, g_ref, dx_ref, loss_ref, dg_ref):
        @pl.when(pl.program_id(0) == 0)
        def _():
            loss_ref[...] = jnp.zeros_like(loss_ref)
            dg_ref[...] = jnp.zeros_like(dg_ref)
        xv = x_ref[...]
        xh, _ = _rms_hat(xv)
        e = xh * g_ref[...] - t_ref[...]
        loss_ref[...] += jnp.sum(e * e) * (0.5 / D)
        dx, dgs = _rms_bwd(e * (1.0 / D), xv, g_ref[...])
        dx_ref[...] = dx
        dg_ref[...] += dgs

    return pl.pallas_call(
        body, name="loss_head", grid=(S // TM,),
        in_specs=[pl.BlockSpec((TM, D), lambda i: (i, 0)), pl.BlockSpec((TM, D), lambda i: (i, 0)),
                  pl.BlockSpec((1, D), lambda i: (0, 0))],
        out_specs=[pl.BlockSpec((TM, D), lambda i: (i, 0)), pl.BlockSpec((1, 128), lambda i: (0, 0)),
                   pl.BlockSpec((1, D), lambda i: (0, 0))],
        out_shape=[jax.ShapeDtypeStruct((S, D), F32), jax.ShapeDtypeStruct((1, 128), F32), jax.ShapeDtypeStruct((1, D), F32)],
        compiler_params=_cp(("arbitrary",)),
    )(x, tgt, g)


def _bwd_ffn_out(dx, w_fo, gu, l, after=()):
    S = dx.shape[0]

    tm = min(TM_BIG, S)

    def body(dx_ref, w_ref, gu_ref, *rest):
        o_ref = rest[-1]
        for r0 in range(0, tm, TM):
            rows = slice(r0, r0 + TM)
            d = _dot_nt(dx_ref[rows, :].astype(BF16), w_ref[...])
            ga, gb = gu_ref[0, rows, :].astype(F32), gu_ref[1, rows, :].astype(F32)
            sg = _sigmoid(ga)
            o_ref[0, rows, :] = (d * gb * sg * (1.0 + ga * (1.0 - sg))).astype(BF16)
            o_ref[1, rows, :] = (d * ga * sg).astype(BF16)

    pair = pl.BlockSpec((2, None, tm, DFF_SH), lambda i, s: (0, s, i, 0))
    dgu = pl.pallas_call(
        body, name=f"bwd_ffn_out_{l}", grid=(S // tm, 2),
        in_specs=[pl.BlockSpec((tm, D), lambda i, s: (i, 0)), pl.BlockSpec((DFF_SH, D), lambda i, s: (s, 0)), pair]
        + [ANY] * len(after),
        out_specs=pair,
        out_shape=jax.ShapeDtypeStruct((2, 2, S, DFF_SH), BF16),
        compiler_params=_cp(("parallel", "arbitrary")),
    )(dx, w_fo, gu.reshape(2, 2, S, DFF_SH), *after)
    return dgu.reshape(4, S, DFF_SH)


def _mm_tn(a, b, m_blk, tk, name):
    S, M = a.shape

    def body(a_ref, b_ref, o_ref):
        @pl.when(pl.program_id(1) == 0)
        def _():
            o_ref[...] = jnp.zeros_like(o_ref)
        o_ref[...] += _dot_tn(a_ref[...], b_ref[...].astype(BF16))

    return pl.pallas_call(
        body, name=name, grid=(M // m_blk, S // tk),
        in_specs=[pl.BlockSpec((tk, m_blk), lambda m, k: (k, m)), pl.BlockSpec((tk, D), lambda m, k: (k, 0))],
        out_specs=pl.BlockSpec((m_blk, D), lambda m, k: (m, 0)),
        out_shape=jax.ShapeDtypeStruct((M, D), F32),
        compiler_params=_cp(("parallel", "arbitrary")),
    )(a, b)


def _mm_nt_rms_bwd(a, a_specs, w, w_specs, nk, tm, x, g, dres, name, after=(), row_part=None):
    S = x.shape[0]
    sub = len(a_specs)
    row_part = tm if row_part is None else min(row_part, tm)

    def body(*refs):
        a_refs, w_refs = refs[:sub], refs[sub:2 * sub]
        x_ref, g_ref, r_ref = refs[2 * sub:2 * sub + 3]
        dx_ref, dg_ref, acc = refs[-3:]
        i, k = pl.program_id(0), pl.program_id(1)
        @pl.when(k == 0)
        def _():
            acc[...] = jnp.zeros_like(acc)
        for r0 in range(0, tm, row_part):
            rows = slice(r0, r0 + row_part)
            for j in range(sub):
                acc[rows, :] += _dot_nt(a_refs[j][rows, :], w_refs[j][...])

        @pl.when(jnp.logical_and(i == 0, k == 0))
        def _():
            dg_ref[...] = jnp.zeros_like(dg_ref)

        @pl.when(k == nk - 1)
        def _():
            dx, dgs = _rms_bwd(acc[...], x_ref[...], g_ref[...])
            dx_ref[...] = r_ref[...] + dx
            dg_ref[...] += dgs

    row = pl.BlockSpec((tm, D), lambda i, k: (i, 0))
    vec = pl.BlockSpec((1, D), lambda i, k: (0, 0))
    return pl.pallas_call(
        body, name=name, grid=(S // tm, nk),
        in_specs=list(a_specs) + list(w_specs) + [row, vec, row] + [ANY] * len(after),
        out_specs=[row, vec],
        out_shape=[jax.ShapeDtypeStruct((S, D), F32), jax.ShapeDtypeStruct((1, D), F32)],
        scratch_shapes=[pltpu.VMEM((tm, D), F32)],
        compiler_params=_cp(("arbitrary", "arbitrary")),
    )(*[a] * sub, *[w] * sub, x, g, dres, *after)


def _dw_ffn_in(h, dgu, l):
    S = h.shape[0]

    def body(h_ref, b_ref, o_ref):
        @pl.when(pl.program_id(1) == 0)
        def _():
            o_ref[...] = jnp.zeros_like(o_ref)
        o_ref[...] += _dot_tn(h_ref[...], b_ref[...])

    tk = min(2 * TM_BIG, S)
    return pl.pallas_call(
        body, name=f"dw_ffn_in_{l}", grid=(4, S // tk),
        in_specs=[pl.BlockSpec((tk, D), lambda j, k: (k, 0)), pl.BlockSpec((None, tk, DFF_SH), lambda j, k: (j, k, 0))],
        out_specs=pl.BlockSpec((None, D, DFF_SH), lambda j, k: (j, 0, 0)),
        out_shape=jax.ShapeDtypeStruct((4, D, DFF_SH), F32),
        compiler_params=_cp(("parallel", "arbitrary")),
    )(h, dgu)


_HALF_COMPS = ((0, 1, 3), (4, 2, 5))


def _dw_in(h, dz6, l):
    S = h.shape[0]

    def body(h_ref, d0_ref, d1_ref, d2_ref, o_ref):
        @pl.when(pl.program_id(1) == 0)
        def _():
            o_ref[...] = jnp.zeros_like(o_ref)
        hv = h_ref[...]
        for q, d_ref in enumerate((d0_ref, d1_ref, d2_ref)):
            for hf in range(2):
                col = 1024 * q + 512 * hf
                o_ref[col // 1536, :, col % 1536:col % 1536 + 512] += _dot_tn(hv, d_ref[:, 512 * hf:512 * (hf + 1)])

    tk = min(TM_BIG, S)

    def comp(q):
        return pl.BlockSpec((None, tk, D), lambda p, k: (jnp.where(p == 0, _HALF_COMPS[0][q], _HALF_COMPS[1][q]), k, 0))

    return pl.pallas_call(
        body, name=f"dw_in_{l}", grid=(2, S // tk),
        in_specs=[pl.BlockSpec((tk, D), lambda p, k: (k, 0)), comp(0), comp(1), comp(2)],
        out_specs=pl.BlockSpec((2, D, 1536), lambda p, k: (p, 0, 0)),
        out_shape=jax.ShapeDtypeStruct((4, D, 1536), F32),
        compiler_params=_cp(("parallel", "arbitrary")),
    )(h, dz6, dz6, dz6)


def _bwd_out(dx, w_o, merged, l):
    S = dx.shape[0]

    def body(dx_ref, w_ref, m_ref, dm_ref, dw_ref):
        @pl.when(pl.program_id(0) == 0)
        def _():
            dw_ref[...] = jnp.zeros_like(dw_ref)
        dxb = dx_ref[...].astype(BF16)
        dm_ref[...] = _dot_nt(dxb, w_ref[...]).astype(BF16)
        dw_ref[...] += _dot_tn(m_ref[...], dxb)

    tm = TM
    row = pl.BlockSpec((tm, D), lambda i: (i, 0))
    return pl.pallas_call(
        body, name=f"bwd_out_{l}", grid=(S // tm,),
        in_specs=[row, pl.BlockSpec((D, D), lambda i: (0, 0)), row],
        out_specs=[row, pl.BlockSpec((D, D), lambda i: (0, 0))],
        out_shape=[jax.ShapeDtypeStruct((S, D), BF16), jax.ShapeDtypeStruct((D, D), F32)],
        compiler_params=_cp(("arbitrary",)),
    )(dx, w_o, merged)


def _gmlp_bwd(dm, z6, ws_b, wst_b, bs_b, lg, lb, after=()):
    S = z6.shape[1]

    def body(dm_ref, z_ref, ws_ref, wst_ref, bs_ref, lg_ref, lb_ref, *rest):
        dz_ref, dws_ref, dbs_ref, dlg_ref, dlb_ref, mix, dv = rest[-7:]

        @pl.when(pl.program_id(0) == 0)
        def _():
            dws_ref[...] = jnp.zeros_like(dws_ref)
            dbs_ref[...] = jnp.zeros_like(dbs_ref)
            dlg_ref[...] = jnp.zeros_like(dlg_ref)
            dlb_ref[...] = jnp.zeros_like(dlb_ref)
        gv, dgelu_v = _gelu_and_grad(z_ref[1].astype(F32))
        xc = gv - jnp.mean(gv, axis=-1, keepdims=True)
        rs = lax.rsqrt(jnp.mean(xc * xc, axis=-1, keepdims=True) + EPS)
        vh = xc * rs
        vb = (vh * lg_ref[...] + lb_ref[...]).astype(BF16)
        for gi in range(NH):
            cs = slice(gi * HD, (gi + 1) * HD)
            mix[:, cs] = _dot(ws_ref[gi], vb[:, cs])
        u, dgelu_u = _gelu_and_grad(z_ref[0].astype(F32))
        sa = _sigmoid(z_ref[2].astype(F32))
        mixed = mix[...] + bs_ref[...]
        dyg = dm_ref[...].astype(F32)
        dz_ref[2] = (dyg * u * mixed * sa * (1.0 - sa)).astype(BF16)
        dya = dyg * sa
        dz_ref[0] = (dya * mixed * dgelu_u).astype(BF16)
        dmix = dya * u
        dmb = dmix.astype(BF16)
        for gi in range(NH):
            cs = slice(gi * HD, (gi + 1) * HD)
            dv[:, cs] = _dot(wst_ref[gi], dmb[:, cs])
            dws_ref[gi] += _dot_nt(dmb[:, cs], vb[:, cs])
            dbs_ref[gi] += jnp.broadcast_to(jnp.sum(dmix[:, cs], axis=1, keepdims=True), (CHUNK, HD))
        dvv = dv[...]
        dlg_ref[...] += jnp.sum(dvv * vh, axis=0, keepdims=True)
        dlb_ref[...] += jnp.sum(dvv, axis=0, keepdims=True)
        dvh = dvv * lg_ref[...]
        dgv = rs * (dvh - jnp.mean(dvh, axis=-1, keepdims=True) - vh * jnp.mean(dvh * vh, axis=-1, keepdims=True))
        dz_ref[1] = (dgv * dgelu_v).astype(BF16)

    vec = pl.BlockSpec((1, D), lambda i: (0, 0))
    mat = pl.BlockSpec((NH, CHUNK, CHUNK), lambda i: (0, 0, 0))
    return pl.pallas_call(
        body, name="gmlp_bwd", grid=(S // CHUNK,),
        in_specs=[pl.BlockSpec((CHUNK, D), lambda i: (i, 0)), pl.BlockSpec((3, CHUNK, D), lambda i: (0, i, 0)), mat, mat,
                  pl.BlockSpec((CHUNK, D), lambda i: (0, 0)), vec, vec] + [ANY] * len(after),
        out_specs=[pl.BlockSpec((3, CHUNK, D), lambda i: (0, i, 0)), mat, mat, vec, vec],
        out_shape=[jax.ShapeDtypeStruct((6, S, D), BF16), jax.ShapeDtypeStruct((NH, CHUNK, CHUNK), F32),
                   jax.ShapeDtypeStruct((NH, CHUNK, HD), F32), jax.ShapeDtypeStruct((1, D), F32), jax.ShapeDtypeStruct((1, D), F32)],
        scratch_shapes=[pltpu.VMEM((CHUNK, D), F32), pltpu.VMEM((CHUNK, D), F32)],
        compiler_params=_cp(("arbitrary",)),
    )(dm, z6, ws_b, wst_b, bs_b, lg, lb, *after)


def _lru_bwd(dz6, dm, z6, h0, h1, cw, cb, wr, br, wi, bi, lam, after=()):
    S = z6.shape[1]
    nt = S // RT

    def body(dz_in, dm_ref, z_ref, h0_ref, h1_ref, cw_ref, cb_ref, wr_ref, br_ref, wi_ref, bi_ref, lam_ref, *rest):
        dz_ref, dcw_ref, dcb_ref, dwr_ref, dbr_ref, dwi_ref, dbi_ref, dlam_ref, zxp, xc_s, dhs_s, dxcp, r_s, lam_s = rest[-14:]
        del dz_in
        lam = lam_ref[...]
        sp = _softplus_neg(lam)
        row = _row_iota()
        _fill_padded(zxp, z_ref.at[0], S)
        _conv_fwd_all(zxp, xc_s, cw_ref, cb_ref, S)
        zeros = jnp.zeros((PADR, HD), F32)
        dxcp[0:PADR, :] = zeros
        dxcp[PADR + S:2 * PADR + S, :] = zeros
        dwr_ref[...] = jnp.zeros_like(dwr_ref)
        dwi_ref[...] = jnp.zeros_like(dwi_ref)

        def pre(i, c):
            rows = pl.ds(pl.multiple_of(i * RT, RT), RT)
            hs = h0_ref[rows, :] + h1_ref[rows, :]
            dmv = dm_ref[rows, :].astype(F32)
            sb = _sigmoid(z_ref[2, rows, :].astype(F32))
            gg, dgg = _gelu_and_grad(z_ref[1, rows, :].astype(F32))
            dz_ref[2, rows, :] = (dmv * hs * gg * sb * (1.0 - sb)).astype(BF16)
            dyb = dmv * sb
            dz_ref[1, rows, :] = (dyb * hs * dgg).astype(BF16)
            dhs_s[rows, :] = dyb * gg
            return c
        lax.fori_loop(0, nt, pre, 0)

        def gate_bwd(d, gates, lamv, da, xc):
            r, gi, a, mult = gates
            dmult = lamv * gi * xc
            dgi = lamv * mult * xc
            dlog = (da - dmult * a / mult) * a
            dpr = (dlog * (-LRU_C) * sp[d:d + 1, :]) * r * (1.0 - r)
            dpi = dgi * gi * (1.0 - gi)
            xb, dprb, dpib = xc.astype(BF16), dpr.astype(BF16), dpi.astype(BF16)
            dwr_ref[d] += _dot_tn(xb, dprb)
            dwi_ref[d] += _dot_tn(xb, dpib)
            dxc = lamv * mult * gi + _dot_nt(dprb, wr_ref[d]) + _dot_nt(dpib, wi_ref[d])
            return dxc, (jnp.sum(dlog * r, axis=0, keepdims=True) * (-LRU_C), jnp.sum(dpr, axis=0, keepdims=True),
                         jnp.sum(dpi, axis=0, keepdims=True))

        def rgates(i, c):
            for u in range(UNROLL):
                rows = pl.ds(pl.multiple_of((i * UNROLL + u) * RT, RT), RT)
                xb = xc_s[rows, :].astype(BF16)
                for d in range(2):
                    r_s[d, rows, :] = _sigmoid(_dot(xb, wr_ref[d]) + br_ref[d:d + 1, :])
            return c
        lax.fori_loop(0, nt // UNROLL, rgates, 0)

        def chains(i, carry):
            qn, qp = carry
            for u in range(UNROLL):
                j = i * UNROLL + u
                rd = pl.ds(pl.multiple_of((nt - 1 - j) * RT, RT), RT)
                a, dhs = _decay(r_s[0, rd, :], sp[0:1, :])[0], dhs_s[rd, :]
                q, q_first = _scan_down(a, a * dhs, qn)
                lam_s[0, rd, :] = dhs + jnp.where(row == RT - 1, qn, pltpu.roll(q, RT - 1, 0))
                qn = q_first
                ru = pl.ds(pl.multiple_of(j * RT, RT), RT)
                a, dhs = _decay(r_s[1, ru, :], sp[1:2, :])[0], dhs_s[ru, :]
                q, q_last = _scan_up(a, a * dhs, qp)
                lam_s[1, ru, :] = dhs + jnp.where(row == 0, qp, pltpu.roll(q, 1, 0))
                qp = q_last
            return qn, qp

        z1 = jnp.zeros((1, HD), F32)
        lax.fori_loop(0, nt // UNROLL, chains, (z1, z1))

        ct = min(GRAD_ROWS, S)
        crow = lax.broadcasted_iota(jnp.int32, (ct, HD), 0)

        def tile_grads(i, acc):
            t0 = pl.multiple_of(i * ct, ct)
            rows = pl.ds(t0, ct)
            xc = xc_s[rows, :]
            xb = xc.astype(BF16)
            tp = pl.multiple_of(jnp.maximum(t0 - PADR, 0), PADR)
            prev = jnp.where(t0 > 0, h0_ref[pl.ds(tp, PADR), :][PADR - 1:PADR, :], 0.0)
            tn = pl.multiple_of(jnp.minimum(t0 + ct, S - PADR), PADR)
            nxt = jnp.where(t0 + ct < S, h1_ref[pl.ds(tn, PADR), :][0:1, :], 0.0)
            hside = (jnp.where(crow == 0, prev, pltpu.roll(h0_ref[rows, :], 1, 0)),
                     jnp.where(crow == ct - 1, nxt, pltpu.roll(h1_ref[rows, :], ct - 1, 0)))
            dxc, sums = 0.0, ()
            for d in range(2):
                r = r_s[d, rows, :]
                gi = _sigmoid(_dot(xb, wi_ref[d]) + bi_ref[d:d + 1, :])
                a, mult = _decay(r, sp[d:d + 1, :])
                lamv = lam_s[d, rows, :]
                dxc_d, s_d = gate_bwd(d, (r, gi, a, mult), lamv, lamv * hside[d], xc)
                dxc = dxc + dxc_d
                sums = sums + s_d
            dxcp[pl.ds(t0 + PADR, ct), :] = dxc
            return tuple(x + y for x, y in zip(acc, sums))

        s_sp0, s_br0, s_bi0, s_sp1, s_br1, s_bi1 = lax.fori_loop(0, S // ct, tile_grads, (z1,) * 6)

        dsp = jnp.concatenate([s_sp0, s_sp1], axis=0)
        dlam_ref[...] = -dsp * _sigmoid(-lam)
        dbr_ref[...] = jnp.concatenate([s_br0, s_br1], axis=0)
        dbi_ref[...] = jnp.concatenate([s_bi0, s_bi1], axis=0)

        def conv_bwd(i, carry):
            c0, c1, c2, c3, cb_ = carry
            t0 = pl.multiple_of(i * RT, RT)
            dwin = dxcp[pl.ds(t0, RT + 2 * PADR), :]
            d0 = _shifted(dwin, 0)
            dz_ref[0, pl.ds(t0, RT), :] = (_shifted(dwin, 1) * cw_ref[0:1, :] + d0 * cw_ref[1:2, :]
                                           + _shifted(dwin, -1) * cw_ref[2:3, :] + _shifted(dwin, -2) * cw_ref[3:4, :]).astype(BF16)
            xm1, x0, xp1, xp2 = _conv_taps(zxp[pl.ds(t0, RT + 2 * PADR), :])
            sm = lambda v: jnp.sum(v, axis=0, keepdims=True)
            return c0 + sm(d0 * xm1), c1 + sm(d0 * x0), c2 + sm(d0 * xp1), c3 + sm(d0 * xp2), cb_ + sm(d0)

        c0, c1, c2, c3, cb_ = lax.fori_loop(0, nt, conv_bwd, (z1, z1, z1, z1, z1))
        dcw_ref[...] = jnp.concatenate([c0, c1, c2, c3], axis=0)
        dcb_ref[...] = cb_

    col = pl.BlockSpec((S, HD), lambda h: (0, h))
    head = lambda h: (0, h)
    wspec = pl.BlockSpec((2, None, HD, HD), lambda h: (0, h, 0, 0))
    return pl.pallas_call(
        body, name="lru_bwd", grid=(NH,),
        in_specs=[pl.BlockSpec(memory_space=pl.ANY), col, pl.BlockSpec((3, S, HD), lambda h: (1, 0, h)), col, col] + _lru_specs(S)
        + [ANY] * len(after),
        out_specs=[pl.BlockSpec((3, S, HD), lambda h: (1, 0, h)), pl.BlockSpec((4, HD), head), pl.BlockSpec((1, HD), head),
                   wspec, pl.BlockSpec((2, HD), head), wspec, pl.BlockSpec((2, HD), head), pl.BlockSpec((2, HD), head)],
        out_shape=[jax.ShapeDtypeStruct((6, S, D), BF16), jax.ShapeDtypeStruct((4, D), F32), jax.ShapeDtypeStruct((1, D), F32),
                   jax.ShapeDtypeStruct((2, NH, HD, HD), F32), jax.ShapeDtypeStruct((2, D), F32),
                   jax.ShapeDtypeStruct((2, NH, HD, HD), F32), jax.ShapeDtypeStruct((2, D), F32), jax.ShapeDtypeStruct((2, D), F32)],
        scratch_shapes=[pltpu.VMEM((S + 2 * PADR, HD), F32), pltpu.VMEM((S, HD), F32), pltpu.VMEM((S, HD), F32),
                        pltpu.VMEM((S + 2 * PADR, HD), F32), pltpu.VMEM((2, S, HD), F32), pltpu.VMEM((2, S, HD), F32)],
        input_output_aliases={0: 0},
        compiler_params=_cp(("parallel",)),
    )(dz6, dm, z6, h0, h1, cw, cb, wr, br, wi, bi, lam, *after)


LAYER_SMALL = ("norm1_g", "gmlp_ln_g", "gmlp_ln_b", "gmlp_w_s", "gmlp_b_s", "conv_w", "conv_b",
               "lru_w_r", "lru_b_r", "lru_w_i", "lru_b_i", "lru_lambda", "norm2_g")


def _forward_layer(l, x, p, wb, after=(), rest=None, near_end=None):
    g1, g2 = p["norm1_g"][l][None], p["norm2_g"][l][None]
    ws_b = p["gmlp_w_s"][l].astype(BF16)
    tm = dict(ws_b=ws_b, wst_b=jnp.swapaxes(ws_b, 1, 2), bs_b=jnp.repeat(p["gmlp_b_s"][l].T, HD, axis=1),
              lg=p["gmlp_ln_g"][l][None], lb=p["gmlp_ln_b"][l][None])
    lru = (p["conv_w"][l], p["conv_b"][l][None], p["lru_w_r"][l].astype(BF16), p["lru_b_r"][l],
           p["lru_w_i"][l].astype(BF16), p["lru_b_i"][l], p["lru_lambda"][l])
    z6, hn1 = _mm_in(x, g1, wb["w_in"], l, after)
    ya = _gmlp_fwd(z6, tm["ws_b"], tm["bs_b"], tm["lg"], tm["lb"])
    merged, h0, h1 = _lru_fwd(z6, ya, *lru)
    if rest is not None:
        wb = dict(wb, **rest(merged))
    x1 = _mm_res(merged, wb["w_out"], x, l, "mm_out")
    gu, ff, hn2 = _mm_ffn_in(x1, g2, wb["w_ffn_in"], l)
    x2 = _mm_res(ff, wb["w_ffn_out"], x1, l, "mm_ffn_out", () if near_end is None else tuple(near_end(gu)))
    return x2, dict(x=x, z6=z6, h0=h0, h1=h1, merged=merged, x1=x1, gu=gu, ff=ff, g1=g1, g2=g2, tm=tm, lru=lru,
                    hn1=hn1, hn2=hn2, wb=wb)


def _backward_layer(l, dx, s, after=(), midway=None, midway2=None, late=None):
    S = dx.shape[0]
    tm, wb = s["tm"], s["wb"]
    g2 = s["g2"]
    dgu = _bwd_ffn_out(dx, wb["w_ffn_out"], s["gu"], l, after)
    tmb = min(TM_BIG, S)
    dwfo = _mm_tn(s["ff"], dx, DFF_SH, tmb, f"dw_ffn_out_{l}")
    dx1, dg2 = _mm_nt_rms_bwd(
        dgu, [pl.BlockSpec((None, tmb, DFF_SH), lambda i, k: (k, i, 0))],
        wb["w_ffn_in"], [pl.BlockSpec((None, D, DFF_SH), lambda i, k: (k, 0, 0))],
        4, tmb, s["x1"], g2, dx, f"bwd_ffn_in_{l}")
    dwfi = _dw_ffn_in(s["hn2"], dgu, l)
    dmg, dwo = _bwd_out(dx1, wb["w_out"], s["merged"], l)
    mid = () if midway is None else tuple(midway([dwo, dwfi, dwfo]))
    dz6, dws, dbs, dlg, dlb = _gmlp_bwd(dmg, s["z6"], tm["ws_b"], tm["wst_b"], tm["bs_b"], tm["lg"], tm["lb"], mid)
    mid2 = () if midway2 is None else tuple(midway2(dws))
    dz6, dcw, dcb, dwr, dbr, dwi, dbi, dlam = _lru_bwd(dz6, dmg, s["z6"], s["h0"], s["h1"], *s["lru"], after=mid2)

    sub = 3

    def dz_tile(j):
        return pl.BlockSpec((None, tmb, 512), lambda i, k: ((sub * k + j) // 2, i, (sub * k + j) % 2))

    def w_tile(j):
        def w_map(i, k):
            sh, tl = _in_tile(sub * k + j)
            return (sh, 0, tl)
        return pl.BlockSpec((None, D, 512), w_map)

    dwin = _dw_in(s["hn1"], dz6, l)
    tail = () if late is None else tuple(late([dwin]))
    dx0, dg1 = _mm_nt_rms_bwd(
        dz6, [dz_tile(j) for j in range(sub)], wb["w_in"], [w_tile(j) for j in range(sub)],
        N_IN_T // sub, tmb, s["x"], s["g1"], dx1, f"bwd_in_{l}", tail)
    small = dict(norm1_g=dg1[0], gmlp_ln_g=dlg[0], gmlp_ln_b=dlb[0], gmlp_w_s=dws, gmlp_b_s=dbs[:, :, 0], conv_w=dcw, conv_b=dcb[0],
                 lru_w_r=dwr, lru_b_r=dbr, lru_w_i=dwi, lru_b_i=dbi, lru_lambda=dlam, norm2_g=dg2[0])
    return dx0, [dwin, dwo, dwfi, dwfo], small


def _local_step(x, tgt, p, wbs):
    saved = []
    for l in range(2):
        x, s = _forward_layer(l, x, p, wbs[l])
        saved.append(s)
    dx, loss_v, dfg = _loss_head(x, tgt, p["final_g"][None])
    big, smalls = [None, None], [None, None]
    for l in (1, 0):
        dx, big[l], smalls[l] = _backward_layer(l, dx, saved[l])
    small = {k: jnp.stack([smalls[0][k], smalls[1][k]]) for k in LAYER_SMALL}
    small["final_g"] = dfg[0]
    return loss_v, dx, big, small


def _place():
    x, y, c = lax.axis_index("x"), lax.axis_index("y"), lax.axis_index("c")
    return x, y, c, 2 * x + y


def _chip_at(x, y, d):
    px = 1 - x if d & 2 else x
    py = 1 - y if d & 1 else y
    return px, py, 2 * px + py


HBM = pl.BlockSpec(memory_space=pltpu.HBM)
SEM = pl.BlockSpec(memory_space=pltpu.SEMAPHORE)
DATAFLOW = pltpu.SideEffectType.DATAFLOW_SIDE_EFFECTING


def _in_hbm(a):
    return pltpu.with_memory_space_constraint(a, pltpu.HBM)


def _cast_into(wf, l, chip_arr, name):
    _, rows, cols = wf.shape
    rh = rows // 2

    def body(ch_ref, w_ref, o_ref):
        o_ref[...] = w_ref[...].astype(BF16)

    return pl.pallas_call(
        body, name=name, out_shape=jax.ShapeDtypeStruct((4, 2, rh, cols), BF16),
        grid_spec=pltpu.PrefetchScalarGridSpec(
            num_scalar_prefetch=1, grid=(2,),
            in_specs=[pl.BlockSpec((None, None, rh, cols), lambda h, ch: (l, h, 0, 0))],
            out_specs=pl.BlockSpec((None, None, rh, cols), lambda h, ch: (ch[0], h, 0, 0))),
        compiler_params=_cp(("parallel",)),
    )(chip_arr, wf.reshape(2, 2, rh, cols))


def _half_block(ref, chip, half, to, send_sem, recv_sem):
    blk = ref.at[chip, half]
    return pltpu.make_async_remote_copy(src_ref=blk, dst_ref=blk, send_sem=send_sem, recv_sem=recv_sem,
                                        device_id=to, device_id_type=MESH)


def _gather_weights(bufs, tiny):
    nt = len(bufs)
    n_ici = max(nt * 3, 1)

    def body(*refs):
        tiny_ref = refs[nt]
        o_refs, tiny_o = refs[nt + 1:2 * nt + 1], refs[2 * nt + 1]
        send, recv, fsend, frecv, tsend, trecv, lsem = refs[2 * nt + 2:]
        x, y, c, chip = _place()
        local = pltpu.make_async_copy(tiny_ref, tiny_o.at[chip], lsem)
        local.start()

        def tin(d, origin_chip, to):
            return pltpu.make_async_remote_copy(
                src_ref=tiny_ref, dst_ref=tiny_o.at[origin_chip], send_sem=tsend.at[d - 1], recv_sem=trecv.at[d - 1],
                device_id=to, device_id_type=MESH)

        sends = []
        for t in range(nt):
            for d in (1, 2, 3):
                px, py, _ = _chip_at(x, y, d)
                sends.append(_half_block(o_refs[t], chip, c, (px, py, c), send.at[3 * t + d - 1], recv.at[3 * t + d - 1]))
        for d in (1, 2, 3):
            px, py, _ = _chip_at(x, y, d)
            sends.append(tin(d, chip, (px, py, c)))
        for cp in sends:
            cp.start()
        passed = []
        for t in range(nt):
            for d in (1, 2, 3):
                k = 3 * t + d - 1
                _, _, pchip = _chip_at(x, y, d)
                _half_block(o_refs[t], pchip, c, (x, y, c), send.at[k], recv.at[k]).wait_recv()
                f = _half_block(o_refs[t], pchip, c, (x, y, 1 - c), fsend.at[k], frecv.at[k])
                f.start()
                passed.append(f)
        for t in range(nt):
            for d in (1, 2, 3):
                k = 3 * t + d - 1
                _, _, pchip = _chip_at(x, y, d)
                _half_block(o_refs[t], pchip, 1 - c, (x, y, 1 - c), fsend.at[k], frecv.at[k]).wait_recv()
        for d in (1, 2, 3):
            _, _, pchip = _chip_at(x, y, d)
            tin(d, pchip, (x, y, c)).wait_recv()
        for cp in sends + passed:
            cp.wait_send()
        local.wait()

    out_shape = [jax.ShapeDtypeStruct(b.shape, b.dtype) for b in bufs]
    out_shape.append(jax.ShapeDtypeStruct((4,) + tiny.shape, tiny.dtype))
    outs = pl.pallas_call(
        body, name="gather_weights_0", out_shape=out_shape,
        in_specs=[ANY] * (nt + 1), out_specs=[ANY] * (nt + 1),
        scratch_shapes=[pltpu.SemaphoreType.DMA((n_ici,)), pltpu.SemaphoreType.DMA((n_ici,)),
                        pltpu.SemaphoreType.DMA((n_ici,)), pltpu.SemaphoreType.DMA((n_ici,)),
                        pltpu.SemaphoreType.DMA((3,)), pltpu.SemaphoreType.DMA((3,)), pltpu.SemaphoreType.DMA],
        input_output_aliases={t: t for t in range(nt)},
        compiler_params=_cp(has_side_effects=True),
    )(*bufs, tiny)
    return outs[:nt], outs[nt]


def _gather_start(bufs, tag, after=()):
    nt, na = len(bufs), len(after)

    def body(*refs):
        b_refs = refs[:nt]
        send, recv = refs[nt + na], refs[nt + na + 1]
        token = refs[2 * nt + na + 2]
        x, y, c, chip = _place()
        for t in range(nt):
            for d in (1, 2, 3):
                px, py, _ = _chip_at(x, y, d)
                _half_block(b_refs[t], chip, c, (px, py, c), send.at[3 * t + d - 1], recv.at[3 * t + d - 1]).start()
        token[...] = jnp.zeros_like(token)

    outs = pl.pallas_call(
        body, name=f"gather_start_{tag}",
        out_shape=(pltpu.SemaphoreType.DMA((3 * nt,)), pltpu.SemaphoreType.DMA((3 * nt,)),
                   *[pltpu.HBM(b.shape, b.dtype) for b in bufs], jax.ShapeDtypeStruct((8, 128), F32)),
        in_specs=[HBM] * nt + [ANY] * na, out_specs=(SEM, SEM, *[HBM] * nt, pl.BlockSpec(memory_space=pltpu.VMEM)),
        input_output_aliases={t: 2 + t for t in range(nt)},
        compiler_params=pltpu.CompilerParams(has_side_effects=DATAFLOW),
    )(*[_in_hbm(b) for b in bufs], *after)
    return outs[0], outs[1], list(outs[2:2 + nt]), outs[2 + nt]


def _gather_wait(send, recv, bufs, after, tag):
    nt = len(bufs)

    def body(*refs):
        b_refs = refs[:nt]
        send_ref, recv_ref = refs[nt], refs[nt + 1]
        x, y, c, chip = _place()
        for t in range(nt):
            for d in (1, 2, 3):
                k = 3 * t + d - 1
                px, py, pchip = _chip_at(x, y, d)
                _half_block(b_refs[t], chip, c, (px, py, c), send_ref.at[k], recv_ref.at[k]).wait_send()
                _half_block(b_refs[t], pchip, c, (px, py, c), send_ref.at[k], recv_ref.at[k]).wait_recv()

    outs = pl.pallas_call(
        body, name=f"gather_wait_{tag}", out_shape=[pltpu.HBM(b.shape, b.dtype) for b in bufs],
        in_specs=[HBM] * nt + [SEM, SEM, ANY], out_specs=[HBM] * nt,
        input_output_aliases={t: t for t in range(nt)},
        compiler_params=pltpu.CompilerParams(has_side_effects=DATAFLOW),
    )(*bufs, send, recv, after)
    return list(outs)


def _gather_pass_on(bufs, tag):
    nt = len(bufs)

    def body(*refs):
        o_refs = refs[nt:2 * nt]
        fsend, frecv = refs[2 * nt:]
        x, y, c, _ = _place()
        cps = []
        for t in range(nt):
            for d in (1, 2, 3):
                k = 3 * t + d - 1
                _, _, pchip = _chip_at(x, y, d)
                cps.append(_half_block(o_refs[t], pchip, c, (x, y, 1 - c), fsend.at[k], frecv.at[k]))
        for cp in cps:
            cp.start()
        for t in range(nt):
            for d in (1, 2, 3):
                k = 3 * t + d - 1
                _, _, pchip = _chip_at(x, y, d)
                _half_block(o_refs[t], pchip, 1 - c, (x, y, 1 - c), fsend.at[k], frecv.at[k]).wait_recv()
        for cp in cps:
            cp.wait_send()

    return pl.pallas_call(
        body, name=f"gather_pass_on_{tag}", out_shape=[jax.ShapeDtypeStruct(b.shape, b.dtype) for b in bufs],
        in_specs=[ANY] * nt, out_specs=[ANY] * nt,
        scratch_shapes=[pltpu.SemaphoreType.DMA((3 * nt,)), pltpu.SemaphoreType.DMA((3 * nt,))],
        input_output_aliases={t: t for t in range(nt)},
        compiler_params=_cp(has_side_effects=True),
    )(*bufs)


def _to_sibling_halves(gs, l):
    nt = len(gs)

    def body(*refs):
        g_refs, o_refs = refs[:nt], refs[nt:2 * nt]
        send, recv = refs[2 * nt:]
        x, y, c, _ = _place()
        cps = [pltpu.make_async_remote_copy(
            src_ref=g_refs[t].at[k, 1 - c], dst_ref=o_refs[t].at[k], send_sem=send.at[4 * t + k], recv_sem=recv.at[4 * t + k],
            device_id=(x, y, 1 - c), device_id_type=MESH) for t in range(nt) for k in range(4)]
        for cp in cps:
            cp.start()
        for cp in cps:
            cp.wait()

    return pl.pallas_call(
        body, name=f"grads_to_sibling_{l}", out_shape=[jax.ShapeDtypeStruct((4,) + g.shape[2:], g.dtype) for g in gs],
        in_specs=[ANY] * nt, out_specs=[ANY] * nt,
        scratch_shapes=[pltpu.SemaphoreType.DMA((4 * nt,)), pltpu.SemaphoreType.DMA((4 * nt,))],
        compiler_params=_cp(has_side_effects=True),
    )(*gs)


def _chip_copy(c_ref, land_ref, x, y, c, d, send_sem, recv_sem):
    px, py, pchip = _chip_at(x, y, d)
    return pltpu.make_async_remote_copy(src_ref=c_ref.at[pchip], dst_ref=land_ref.at[d - 1], send_sem=send_sem, recv_sem=recv_sem,
                                        device_id=(px, py, c), device_id_type=MESH)


def _exchange_start(srcs, lands, copies, nsem, name):
    ns, n = len(srcs), len(srcs) + len(lands)

    def body(*refs):
        for cp in copies(refs[:ns], refs[ns:n], refs[n], refs[n + 1]):
            cp.start()
        token = refs[2 * n + 2]
        token[...] = jnp.zeros_like(token)

    outs = pl.pallas_call(
        body, name=name,
        out_shape=(pltpu.SemaphoreType.DMA((nsem,)), pltpu.SemaphoreType.DMA((nsem,)),
                   *[pltpu.HBM(a.shape, a.dtype) for a in list(srcs) + list(lands)], jax.ShapeDtypeStruct((8, 128), F32)),
        in_specs=[HBM] * n, out_specs=(SEM, SEM, *[HBM] * n, pl.BlockSpec(memory_space=pltpu.VMEM)),
        input_output_aliases={i: 2 + i for i in range(n)},
        compiler_params=pltpu.CompilerParams(has_side_effects=DATAFLOW),
    )(*[_in_hbm(a) for a in list(srcs) + list(lands)])
    return outs[0], outs[1], list(outs[2:2 + ns]), list(outs[2 + ns:2 + n]), outs[2 + n]


def _exchange_wait(send, recv, srcs, lands, after, copies, name):
    ns, n = len(srcs), len(srcs) + len(lands)

    def body(*refs):
        for cp in copies(refs[:ns], refs[ns:n], refs[n], refs[n + 1]):
            cp.wait_send()
            cp.wait_recv()

    outs = pl.pallas_call(
        body, name=name, out_shape=[pltpu.HBM(a.shape, a.dtype) for a in list(srcs) + list(lands)],
        in_specs=[HBM] * n + [SEM, SEM, ANY], out_specs=[HBM] * n,
        input_output_aliases={i: i for i in range(n)},
        compiler_params=pltpu.CompilerParams(has_side_effects=DATAFLOW),
    )(*srcs, *lands, send, recv, after)
    return list(outs[:ns]), list(outs[ns:])


def _pass_on_copies(b_refs, land_refs, send, recv):
    del land_refs
    x, y, c, _ = _place()
    return [_half_block(b_refs[t], _chip_at(x, y, d)[2], c, (x, y, 1 - c), send.at[3 * t + d - 1], recv.at[3 * t + d - 1])
            for t in range(len(b_refs)) for d in (1, 2, 3)]


def _chips_copies(c_refs, land_refs, send, recv):
    x, y, c, _ = _place()
    return [_chip_copy(c_refs[t], land_refs[t], x, y, c, d, send.at[3 * t + d - 1], recv.at[3 * t + d - 1])
            for t in range(len(c_refs)) for d in (1, 2, 3)]


def _sibling_copies(g_refs, land_refs, send, recv):
    x, y, c, _ = _place()
    return [pltpu.make_async_remote_copy(
        src_ref=g_refs[t].at[k, 1 - c], dst_ref=land_refs[t].at[k], send_sem=send.at[4 * t + k], recv_sem=recv.at[4 * t + k],
        device_id=(x, y, 1 - c), device_id_type=MESH) for t in range(len(g_refs)) for k in range(4)]


def _join_halves(fs, l):
    nt = len(fs)

    def body(*refs):
        o_refs = refs[nt:2 * nt]
        send, recv = refs[2 * nt:]
        x, y, c, _ = _place()
        cps = [pltpu.make_async_remote_copy(
            src_ref=o_refs[t].at[c], dst_ref=o_refs[t].at[c], send_sem=send.at[t], recv_sem=recv.at[t],
            device_id=(x, y, 1 - c), device_id_type=MESH) for t in range(nt)]
        for cp in cps:
            cp.start()
        for cp in cps:
            cp.wait()

    return pl.pallas_call(
        body, name=f"grads_join_{l}", out_shape=[jax.ShapeDtypeStruct(a.shape, a.dtype) for a in fs],
        in_specs=[ANY] * nt, out_specs=[ANY] * nt,
        scratch_shapes=[pltpu.SemaphoreType.DMA((nt,)), pltpu.SemaphoreType.DMA((nt,))],
        input_output_aliases={t: t for t in range(nt)},
        compiler_params=_cp(has_side_effects=True),
    )(*fs)


def _add_half(g, r, c_arr, name):
    _, _, rh, cols = g.shape

    def body(c_ref, g_ref, r_ref, o_ref):
        o_ref[...] = (g_ref[...] + r_ref[...]).astype(BF16)

    blk = pl.BlockSpec((None, rh, cols), lambda k, cr: (k, 0, 0))
    return pl.pallas_call(
        body, name=name, out_shape=jax.ShapeDtypeStruct((4, rh, cols), BF16),
        grid_spec=pltpu.PrefetchScalarGridSpec(
            num_scalar_prefetch=1, grid=(4,),
            in_specs=[pl.BlockSpec((None, None, rh, cols), lambda k, cr: (k, cr[0], 0, 0)), blk], out_specs=blk),
        compiler_params=_cp(("parallel",)),
    )(c_arr, g, r)


def _sum_chips(cs, r3, place_arr, name):
    _, rh, cols = cs.shape
    rb = rh // 2

    def body(pl_ref, a_ref, r0_ref, r1_ref, r2_ref, o_ref):
        up = lambda ref: ref[...].astype(F32)
        o_ref[...] = ((up(a_ref) + up(r0_ref)) + up(r1_ref)) + up(r2_ref)

    def slot(d):
        return pl.BlockSpec((None, rb, cols), lambda i, pa: (d, i, 0))

    return pl.pallas_call(
        body, name=name, out_shape=jax.ShapeDtypeStruct((2, rh, cols), F32),
        grid_spec=pltpu.PrefetchScalarGridSpec(
            num_scalar_prefetch=1, grid=(2,),
            in_specs=[pl.BlockSpec((None, rb, cols), lambda i, pa: (pa[0], i, 0)), slot(0), slot(1), slot(2)],
            out_specs=pl.BlockSpec((None, rb, cols), lambda i, pa: (pa[1], i, 0))),
        compiler_params=_cp(("parallel",)),
    )(place_arr, cs, r3, r3, r3)


def _allreduce_small(pack):
    rows = pack.shape[0]
    hr = rows // 2

    def body(p_ref, o_ref, sib, slots, s1, r1, s2, r2, s3, r3):
        x, y, c, chip = _place()
        sibling = (x, y, 1 - c)
        ex = pltpu.make_async_remote_copy(src_ref=p_ref, dst_ref=sib, send_sem=s1, recv_sem=r1,
                                          device_id=sibling, device_id_type=MESH)
        ex.start()
        ex.wait()
        half = pl.ds(pl.multiple_of(c * hr, 16), hr)
        slots[0] = (p_ref[half, :] + sib[half, :]).astype(BF16)
        cps = []
        for d in (1, 2, 3):
            px, py, _ = _chip_at(x, y, d)
            cps.append(pltpu.make_async_remote_copy(
                src_ref=slots.at[0], dst_ref=slots.at[d], send_sem=s2.at[d - 1], recv_sem=r2.at[d - 1],
                device_id=(px, py, c), device_id_type=MESH))
        for cp in cps:
            cp.start()
        for cp in cps:
            cp.wait()
        tot = slots[chip].astype(F32)
        for k in (1, 2, 3):
            tot = tot + slots[jnp.bitwise_xor(chip, k)].astype(F32)
        o_ref[half, :] = tot
        back = pltpu.make_async_remote_copy(src_ref=o_ref.at[half, :], dst_ref=o_ref.at[half, :], send_sem=s3, recv_sem=r3,
                                            device_id=sibling, device_id_type=MESH)
        back.start()
        back.wait()

    vm = pl.BlockSpec(memory_space=pltpu.VMEM)
    return pl.pallas_call(
        body, name="allreduce_small", out_shape=jax.ShapeDtypeStruct((rows, 128), F32),
        in_specs=[vm], out_specs=vm,
        scratch_shapes=[pltpu.VMEM((rows, 128), F32), pltpu.VMEM((4, hr, 128), BF16),
                        pltpu.SemaphoreType.DMA, pltpu.SemaphoreType.DMA, pltpu.SemaphoreType.DMA((3,)), pltpu.SemaphoreType.DMA((3,)),
                        pltpu.SemaphoreType.DMA, pltpu.SemaphoreType.DMA],
        compiler_params=_cp(has_side_effects=True),
    )(pack)


def _adam_math(gv, wv, mv, vv):
    m2 = ADAM_B1 * mv + (1.0 - ADAM_B1) * gv
    v2 = ADAM_B2 * vv + (1.0 - ADAM_B2) * (gv * gv)
    m_hat = m2 / (1.0 - ADAM_B1 ** ADAM_STEP)
    v_hat = v2 / (1.0 - ADAM_B2 ** ADAM_STEP)
    return -ADAM_LR * (m_hat / (jnp.sqrt(v_hat) + ADAM_EPS) + ADAM_WD * wv), m2, v2


def _adam(g, w, m, v, name):
    rows, cols = g.shape
    rb = rows // 4

    def body(g_ref, w_ref, m_ref, v_ref, d_ref, m2_ref, v2_ref):
        d_ref[...], m2_ref[...], v2_ref[...] = _adam_math(g_ref[...], w_ref[...], m_ref[...], v_ref[...])

    blk = pl.BlockSpec((rb, cols), lambda i: (i, 0))
    shp = jax.ShapeDtypeStruct((rows, cols), F32)
    return pl.pallas_call(
        body, name=name, grid=(4,), in_specs=[blk] * 4, out_specs=[blk] * 3, out_shape=[shp] * 3,
        compiler_params=_cp(("parallel",)),
    )(g, w, m, v)


def _adam_layer(g, w, m, v, l, prev, name):
    rows, cols = g.shape
    rb = rows // 4

    def body(g_ref, w_ref, m_ref, v_ref, *rest):
        go_ref, d_ref, m2_ref, v2_ref = rest[-4:]
        gv = g_ref[...]
        go_ref[...] = gv
        d_ref[...], m2_ref[...], v2_ref[...] = _adam_math(gv, w_ref[...], m_ref[...], v_ref[...])

    lay = pl.BlockSpec((None, rb, cols), lambda i: (l, i, 0))
    shp = jax.ShapeDtypeStruct((2, rows, cols), F32)
    prev = () if prev is None else tuple(prev)
    return pl.pallas_call(
        body, name=name, grid=(4,), in_specs=[pl.BlockSpec((rb, cols), lambda i: (i, 0)), lay, lay, lay] + [ANY] * len(prev),
        out_specs=[lay] * 4, out_shape=[shp] * 4,
        input_output_aliases={4 + j: j for j in range(len(prev))},
        compiler_params=_cp(("parallel",)),
    )(g, w, m, v, *prev)


def _rows128(a):
    return a.reshape(-1, 128)


def _pack(arrs, mult):
    parts = [_rows128(a) for a in arrs]
    rows = sum(q.shape[0] for q in parts)
    pad = -rows % mult
    if pad:
        parts.append(jnp.zeros((pad, 128), F32))
    return jnp.concatenate(parts, axis=0)


def _unpack(pack, shapes):
    out, o = [], 0
    for s in shapes:
        n = 1
        for e in s:
            n *= e
        out.append(pack[o:o + n // 128].reshape(s))
        o += n // 128
    return out


WEIGHTS = ['norm1_g', 'w_in', 'gmlp_ln_g', 'gmlp_ln_b', 'gmlp_w_s', 'gmlp_b_s', 'conv_w', 'conv_b', 'lru_w_r', 'lru_b_r', 'lru_w_i',
           'lru_b_i', 'lru_lambda', 'w_out', 'norm2_g', 'w_ffn_in', 'w_ffn_out', 'final_g']
BIG = ['w_in', 'w_out', 'w_ffn_in', 'w_ffn_out']
SMALL = [n for n in WEIGHTS if n not in BIG]
CHIP_SHARDED_SMALL = ['conv_w', 'lru_b_r', 'lru_b_i', 'lru_lambda']


def kernel(x, norm1_g, w_in, gmlp_ln_g, gmlp_ln_b, gmlp_w_s, gmlp_b_s, conv_w, conv_b, lru_w_r, lru_b_r, lru_w_i, lru_b_i, lru_lambda, w_out, norm2_g, w_ffn_in, w_ffn_out, final_g, loss_target, m_norm1_g, m_w_in, m_gmlp_ln_g, m_gmlp_ln_b, m_gmlp_w_s, m_gmlp_b_s, m_conv_w, m_conv_b, m_lru_w_r, m_lru_b_r, m_lru_w_i, m_lru_b_i, m_lru_lambda, m_w_out, m_norm2_g, m_w_ffn_in, m_w_ffn_out, m_final_g, v_norm1_g, v_w_in, v_gmlp_ln_g, v_gmlp_ln_b, v_gmlp_w_s, v_gmlp_b_s, v_conv_w, v_conv_b, v_lru_w_r, v_lru_b_r, v_lru_w_i, v_lru_b_i, v_lru_lambda, v_w_out, v_norm2_g, v_w_ffn_in, v_w_ffn_out, v_final_g):
    a = dict(locals())
    w = {n: a[n] for n in WEIGHTS}
    mom = {n: a["m_" + n] for n in WEIGHTS}
    var = {n: a["v_" + n] for n in WEIGHTS}
    _, _, c, chip = _place()
    c_arr, chip_arr = jnp.reshape(c, (1,)).astype(jnp.int32), jnp.reshape(chip, (1,)).astype(jnp.int32)
    place_arr = jnp.stack([chip, c]).astype(jnp.int32)

    first, rest = BIG[:1], BIG[1:]

    def as_weights(names, full):
        wb = {n: f.reshape(4, 2 * f.shape[2], f.shape[3]) for n, f in zip(names, full)}
        if "w_out" in wb:
            wb["w_out"] = wb["w_out"].reshape(D, D)
            wb["w_ffn_out"] = wb["w_ffn_out"].reshape(DFF, D)
        return wb

    def cast(n, l):
        return _cast_into(w[n], l, chip_arr, f"cast_{n}_{l}")

    def landed(fly, names, after, tag):
        return as_weights(names, _gather_pass_on(_gather_wait(fly[0], fly[1], fly[2], after, tag), tag))

    tiny = _pack([w[n] for n in CHIP_SHARDED_SMALL], 8)
    _, tiny_full = _gather_weights([], tiny)
    fly_in = _gather_start([cast("w_in", 0)], "in", after=(tiny_full,))
    fly0 = _gather_start([cast(n, 0) for n in rest], "0", after=(fly_in[3],))
    bufs1 = [cast(n, 1) for n in BIG]
    fly1 = _gather_start(bufs1, "1", after=(fly0[3],))
    p = {n: w[n] for n in SMALL}
    parts = [_unpack(tiny_full[k], [w[n].shape for n in CHIP_SHARDED_SMALL]) for k in range(4)]
    for i, n in enumerate(CHIP_SHARDED_SMALL):
        p[n] = jnp.concatenate([parts[k][i] for k in range(4)], axis=-1)

    passing = {}

    def pass_on_1(gu):
        bufs = _gather_wait(fly1[0], fly1[1], fly1[2], gu, "1")
        passing[1] = _exchange_start(bufs, [], _pass_on_copies, 3 * len(bufs), "gather_pass_on_start_1")
        return (passing[1][-1],)

    xa, saved0 = _forward_layer(0, x[0], p, landed(fly_in, first, fly1[3], "in"), after=(fly0[3], fly1[3]),
                                rest=lambda merged: landed(fly0, rest, merged, "0"), near_end=pass_on_1)
    send, recv, bufs1, _, _ = passing[1]
    xb, saved1 = _forward_layer(
        1, xa, p, as_weights(BIG, _exchange_wait(send, recv, bufs1, [], xa, _pass_on_copies, "gather_pass_on_wait_1")[0]))
    dxb, loss_v, dfg = _loss_head(xb, loss_target[0], p["final_g"][None])
    loss = lax.psum(loss_v[0, 0], ("x", "y", "c"))

    out, flying = {}, {}

    def halves(grads):
        return [g.reshape(4, 2, -1, g.shape[-1]) for g in grads]

    def sibling_start(grads, names, l, tag):
        gs = halves(grads)
        lands = [lax.empty((4,) + g.shape[2:], g.dtype) for g in gs]
        flying["s" + tag] = (names, l) + tuple(
            _exchange_start(gs, lands, _sibling_copies, 4 * len(gs), f"grads_to_sibling_start_{tag}"))
        return (flying["s" + tag][-1],)

    def chips_start(gs, from_sib, names, l, tag):
        cs = [_add_half(g, r, c_arr, f"add_half_{n}_{l}") for n, g, r in zip(names, gs, from_sib)]
        lands = [lax.empty((3,) + a.shape[1:], a.dtype) for a in cs]
        flying[tag] = (names, l) + tuple(_exchange_start(cs, lands, _chips_copies, 3 * len(cs), f"grads_to_chips_start_{tag}"))
        return (flying[tag][-1],)

    def sibling_finish(tag, after):
        names, l, send, recv, gs, lands, _ = flying["s" + tag]
        gs, from_sib = _exchange_wait(send, recv, gs, lands, after, _sibling_copies, f"grads_to_sibling_wait_{tag}")
        return chips_start(gs, from_sib, names, l, tag)

    def reduce_start(grads, names, l, tag):
        gs = halves(grads)
        return chips_start(gs, _to_sibling_halves(gs, tag), names, l, tag)

    def reduce_finish(tags, after):
        names, ts = [], []
        for tag in tags:
            names_t, l, send, recv, cs, lands, _ = flying[tag]
            cs, lands = _exchange_wait(send, recv, cs, lands, after, _chips_copies, f"grads_to_chips_wait_{tag}")
            ts += [_sum_chips(cc, r3, place_arr, f"sum_chips_{n}_{l}") for n, cc, r3 in zip(names_t, cs, lands)]
            names += names_t
        for n, j in zip(names, _join_halves(ts, tags[0])):
            out[n] = _adam_layer(j.reshape(w[n].shape[1:]), w[n], mom[n], var[n], l, out.get(n), f"adam_{n}_{l}")

    def late1(grads):
        return sibling_finish("1a", grads[0]) + sibling_start(grads, first, 1, "1b")

    def midway0(grads):
        reduce_finish(("1a", "1b"), grads[0])
        return sibling_start(grads, rest, 0, "0a")

    dxa, big1, small1 = _backward_layer(1, dxb, saved1, midway=lambda grads: sibling_start(grads, rest, 1, "1a"), late=late1)
    dx, big0, small0 = _backward_layer(0, dxa, saved0, after=sibling_finish("1b", dxa), midway=midway0,
                                       midway2=lambda dws: sibling_finish("0a", dws),
                                       late=lambda grads: reduce_start(grads, first, 0, "0b"))
    reduce_finish(("0a", "0b"), dx)
    small = {k: jnp.stack([small0[k], small1[k]]) for k in LAYER_SMALL}
    small["final_g"] = dfg[0]

    full_shapes = [small[n].shape for n in SMALL]
    red = _unpack(_allreduce_small(_pack([small[n] for n in SMALL], 32)), full_shapes)
    g_small = []
    for n, g in zip(SMALL, red):
        if n in CHIP_SHARDED_SMALL:
            g = lax.dynamic_slice_in_dim(g, chip * w[n].shape[-1], w[n].shape[-1], axis=g.ndim - 1)
        g_small.append(g)
    shapes = [w[n].shape for n in SMALL]
    packs = [_pack(lst, 32) for lst in (g_small, [w[n] for n in SMALL], [mom[n] for n in SMALL], [var[n] for n in SMALL])]
    upd = [_unpack(u, shapes) for u in _adam(*packs, "adam_small")]
    for i, n in enumerate(SMALL):
        out[n] = [g_small[i], upd[0][i], upd[1][i], upd[2][i]]

    return (loss, dx[None]) + tuple(out[n][i] for i in range(4) for n in WEIGHTS)
```

```python
import functools

import jax
import jax.numpy as jnp
from jax import lax
from jax.experimental import pallas as pl
from jax.experimental.pallas import tpu as pltpu

F32 = jnp.float32
BF16 = jnp.bfloat16
MESH = pl.DeviceIdType.MESH

D = 1024
NH = 8
HD = 128
CHUNK = 128
N_IN_T = 12
DFF = 2816
DFF_SH = 1408
EPS = 1e-6
LRU_C = 8.0
ADAM_LR, ADAM_B1, ADAM_B2, ADAM_EPS, ADAM_WD, ADAM_STEP = 0.001, 0.9, 0.999, 1e-08, 0.01, 10

TM = 512
TM_BIG = 1024
RT = 128
PADR = 8
VMEM_LIMIT = 56 * 1024 * 1024


def _cp(sem=None, **kw):
    if sem is not None:
        kw["dimension_semantics"] = sem
    return pltpu.CompilerParams(vmem_limit_bytes=VMEM_LIMIT, **kw)


_GC = 0.7978845608028654


def _sigmoid(x):
    return 0.5 * jnp.tanh(0.5 * x) + 0.5


_GK = 0.044715


def _gelu(x):
    t = jnp.tanh(x * (_GC + (_GC * _GK) * (x * x)))
    return x * (0.5 + 0.5 * t)


def _gelu_and_grad(x):
    x2 = x * x
    t = jnp.tanh(x * (_GC + (_GC * _GK) * x2))
    h = 0.5 + 0.5 * t
    return x * h, h + x * (1.0 - t * t) * (0.5 * _GC + (1.5 * _GC * _GK) * x2)


def _softplus_neg(lam):
    y = jnp.exp(-jnp.abs(lam))
    u = 1.0 + y
    l1p = jnp.where(u == 1.0, y, jnp.log(u) * y / (u - 1.0))
    return jnp.maximum(-lam, 0.0) + l1p


def _dot(a, b):
    return jnp.dot(a, b, preferred_element_type=F32)


def _dot_nt(a, b):
    return lax.dot_general(a, b, (((1,), (1,)), ((), ())), preferred_element_type=F32)


def _dot_tn(a, b):
    return lax.dot_general(a, b, (((0,), (0,)), ((), ())), preferred_element_type=F32)


def _rms_hat(x):
    r = lax.rsqrt(jnp.mean(x * x, axis=-1, keepdims=True) + EPS)
    return x * r, r


def _rms_bwd(dh, x, g):
    xh, r = _rms_hat(x)
    dxh = dh * g
    dx = r * (dxh - xh * jnp.mean(dxh * xh, axis=-1, keepdims=True))
    return dx, jnp.sum(dh * xh, axis=0, keepdims=True)


def _norm_into(x_ref, g_ref, h_ref):
    xh, _ = _rms_hat(x_ref[...])
    h_ref[...] = (xh * g_ref[...]).astype(BF16)


def _in_tile(j):
    m, hf = j // 2, j % 2
    orig = jnp.where(m < 2, m, jnp.where(m == 2, 4, jnp.where(m < 5, m - 1, 5)))
    t = orig * 2 + hf
    return t // 3, t % 3


ANY = pl.BlockSpec(memory_space=pl.ANY)


def _mm_in(x, g, w_in, l, after=()):
    S = x.shape[0]
    tm = min(2 * TM_BIG, S)

    def body(x_ref, g_ref, w0_ref, w1_ref, *rest):
        o_ref, h_ref = rest[-2:]

        @pl.when(pl.program_id(1) == 0)
        def _():
            _norm_into(x_ref, g_ref, h_ref)
        rp = min(TM, tm)
        for r0 in range(0, tm, rp):
            hv = h_ref[r0:r0 + rp, :]
            o_ref[r0:r0 + rp, 0:512] = _dot(hv, w0_ref[...]).astype(BF16)
            o_ref[r0:r0 + rp, 512:1024] = _dot(hv, w1_ref[...]).astype(BF16)

    def w_tile(hf):
        def w_map(i, m):
            sh, tl = _in_tile(2 * m + hf)
            return (sh, 0, tl)
        return pl.BlockSpec((None, D, 512), w_map)

    return pl.pallas_call(
        body, name=f"mm_in_{l}", grid=(S // tm, 6),
        in_specs=[pl.BlockSpec((tm, D), lambda i, m: (i, 0)), pl.BlockSpec((1, D), lambda i, m: (0, 0)),
                  w_tile(0), w_tile(1)] + [ANY] * len(after),
        out_specs=[pl.BlockSpec((None, tm, D), lambda i, m: (m, i, 0)), pl.BlockSpec((tm, D), lambda i, m: (i, 0))],
        out_shape=[jax.ShapeDtypeStruct((6, S, D), BF16), jax.ShapeDtypeStruct((S, D), BF16)],
        compiler_params=_cp(("parallel", "arbitrary")),
    )(x, g, w_in, w_in, *after)


def _mm_res(a, w, res, l, name, after=()):
    S, K = a.shape

    tm = TM

    def body(a_ref, w_ref, r_ref, *rest):
        rest[-1][...] = r_ref[...] + _dot(a_ref[...], w_ref[...])

    return pl.pallas_call(
        body, name=f"{name}_{l}", grid=(S // tm,),
        in_specs=[pl.BlockSpec((tm, K), lambda i: (i, 0)), pl.BlockSpec((K, D), lambda i: (0, 0)),
                  pl.BlockSpec((tm, D), lambda i: (i, 0))] + [ANY] * len(after),
        out_specs=pl.BlockSpec((tm, D), lambda i: (i, 0)),
        out_shape=jax.ShapeDtypeStruct((S, D), F32),
        compiler_params=_cp(("parallel",)),
    )(a, w, res, *after)


def _mm_ffn_in(x, g, w_fi, l):
    S = x.shape[0]

    tm = min(TM_BIG, S)

    def body(x_ref, g_ref, w_ref, gu_ref, ff_ref, h_ref):
        @pl.when(pl.program_id(1) == 0)
        def _():
            _norm_into(x_ref, g_ref, h_ref)
        for r0 in range(0, tm, TM):
            rows = slice(r0, r0 + TM)
            hv = h_ref[rows, :]
            ga = _dot(hv, w_ref[0])
            gb = _dot(hv, w_ref[1])
            gu_ref[0, rows, :] = ga.astype(BF16)
            gu_ref[1, rows, :] = gb.astype(BF16)
            ff_ref[rows, :] = (ga * _sigmoid(ga) * gb).astype(BF16)

    gu, ff, h = pl.pallas_call(
        body, name=f"mm_ffn_in_{l}", grid=(S // tm, 2),
        in_specs=[pl.BlockSpec((tm, D), lambda i, s: (i, 0)), pl.BlockSpec((1, D), lambda i, s: (0, 0)),
                  pl.BlockSpec((2, None, D, DFF_SH), lambda i, s: (0, s, 0, 0))],
        out_specs=[pl.BlockSpec((2, None, tm, DFF_SH), lambda i, s: (0, s, i, 0)),
                   pl.BlockSpec((tm, DFF_SH), lambda i, s: (i, s)),
                   pl.BlockSpec((tm, D), lambda i, s: (i, 0))],
        out_shape=[jax.ShapeDtypeStruct((2, 2, S, DFF_SH), BF16), jax.ShapeDtypeStruct((S, DFF), BF16),
                   jax.ShapeDtypeStruct((S, D), BF16)],
        compiler_params=_cp(("parallel", "arbitrary")),
    )(x, g, w_fi.reshape(2, 2, D, DFF_SH))
    return gu.reshape(4, S, DFF_SH), ff, h


def _gmlp_fwd(z6, ws_b, bs_b, lg, lb):
    S = z6.shape[1]

    def body(z_ref, ws_ref, bs_ref, lg_ref, lb_ref, o_ref, mix):
        gv = _gelu(z_ref[1].astype(F32))
        xc = gv - jnp.mean(gv, axis=-1, keepdims=True)
        rs = lax.rsqrt(jnp.mean(xc * xc, axis=-1, keepdims=True) + EPS)
        vb = (xc * rs * lg_ref[...] + lb_ref[...]).astype(BF16)
        for gi in range(NH):
            cs = slice(gi * HD, (gi + 1) * HD)
            mix[:, cs] = _dot(ws_ref[gi], vb[:, cs])
        o_ref[...] = (_sigmoid(z_ref[2].astype(F32)) * _gelu(z_ref[0].astype(F32)) * (mix[...] + bs_ref[...])).astype(BF16)

    return pl.pallas_call(
        body, name="gmlp_fwd", grid=(S // CHUNK,),
        in_specs=[pl.BlockSpec((3, CHUNK, D), lambda i: (0, i, 0)), pl.BlockSpec((NH, CHUNK, CHUNK), lambda i: (0, 0, 0)),
                  pl.BlockSpec((CHUNK, D), lambda i: (0, 0)), pl.BlockSpec((1, D), lambda i: (0, 0)),
                  pl.BlockSpec((1, D), lambda i: (0, 0))],
        out_specs=pl.BlockSpec((CHUNK, D), lambda i: (i, 0)),
        out_shape=jax.ShapeDtypeStruct((S, D), BF16),
        scratch_shapes=[pltpu.VMEM((CHUNK, D), F32)],
        compiler_params=_cp(("parallel",)),
    )(z6, ws_b, bs_b, lg, lb)


def _row_iota():
    return lax.broadcasted_iota(jnp.int32, (RT, HD), 0)


SUB = 8
UNROLL = 4
GRAD_ROWS = 512


def _scan_up(a, b, carry):
    row = lax.broadcasted_iota(jnp.int32, (SUB, HD), 0)
    masks = [(d, row >= d) for d in (1, 2, 4)]
    c = jnp.broadcast_to(carry, (SUB, HD))
    hs = []
    for j in range(RT // SUB):
        aj, bj = a[SUB * j:SUB * (j + 1)], b[SUB * j:SUB * (j + 1)]
        for d, m in masks:
            bj = bj + aj * jnp.where(m, pltpu.roll(bj, d, 0), 0.0)
            aj = aj * jnp.where(m, pltpu.roll(aj, d, 0), 1.0)
        h = bj + aj * c
        hs.append(h)
        c = jnp.broadcast_to(h[SUB - 1:SUB, :], (SUB, HD))
    return jnp.concatenate(hs, axis=0), hs[-1][SUB - 1:SUB, :]


def _scan_down(a, b, carry):
    row = lax.broadcasted_iota(jnp.int32, (SUB, HD), 0)
    masks = [(d, row < SUB - d) for d in (1, 2, 4)]
    c = jnp.broadcast_to(carry, (SUB, HD))
    hs = []
    for j in reversed(range(RT // SUB)):
        aj, bj = a[SUB * j:SUB * (j + 1)], b[SUB * j:SUB * (j + 1)]
        for d, m in masks:
            bj = bj + aj * jnp.where(m, pltpu.roll(bj, SUB - d, 0), 0.0)
            aj = aj * jnp.where(m, pltpu.roll(aj, SUB - d, 0), 1.0)
        h = bj + aj * c
        hs.append(h)
        c = jnp.broadcast_to(h[0:1, :], (SUB, HD))
    return jnp.concatenate(hs[::-1], axis=0), hs[-1][0:1, :]


def _decay(r, sp_d):
    log_a = -LRU_C * r * sp_d
    a = jnp.exp(log_a)
    return a, jnp.sqrt(jnp.maximum(-jnp.tanh(log_a) * (a * a + 1.0), 0.0))


def _lru_gates(xc, d, wr_ref, br_ref, wi_ref, bi_ref, sp):
    xb = xc.astype(BF16)
    r = _sigmoid(_dot(xb, wr_ref[d]) + br_ref[d:d + 1, :])
    i = _sigmoid(_dot(xb, wi_ref[d]) + bi_ref[d:d + 1, :])
    a, mult = _decay(r, sp[d:d + 1, :])
    return r, i, a, mult


def _shifted(win, k):
    w = RT + 2 * PADR
    v = win if k == 0 else pltpu.roll(win, (-k) % w, 0)
    return v[PADR:PADR + RT]


def _conv_taps(win):
    return [_shifted(win, k) for k in (-1, 0, 1, 2)]


def _fill_padded(dst, src_ref, S):
    zeros = jnp.zeros((PADR, HD), F32)
    dst[0:PADR, :] = zeros
    dst[PADR + S:2 * PADR + S, :] = zeros

    def cp(i, c):
        t0 = pl.multiple_of(i * RT, RT)
        dst[pl.ds(t0 + PADR, RT), :] = src_ref[pl.ds(t0, RT), :].astype(F32)
        return c
    lax.fori_loop(0, S // RT, cp, 0)


def _conv_fwd_all(zxp, xc_s, cw_ref, cb_ref, S):
    def cv(i, c):
        t0 = pl.multiple_of(i * RT, RT)
        xm1, x0, xp1, xp2 = _conv_taps(zxp[pl.ds(t0, RT + 2 * PADR), :])
        xc_s[pl.ds(t0, RT), :] = (cb_ref[...] + xm1 * cw_ref[0:1, :] + x0 * cw_ref[1:2, :]
                                  + xp1 * cw_ref[2:3, :] + xp2 * cw_ref[3:4, :])
        return c
    lax.fori_loop(0, S // RT, cv, 0)


def _lru_specs(S):
    head = lambda h: (0, h)
    return [pl.BlockSpec((4, HD), head), pl.BlockSpec((1, HD), head),
            pl.BlockSpec((2, None, HD, HD), lambda h: (0, h, 0, 0)), pl.BlockSpec((2, HD), head),
            pl.BlockSpec((2, None, HD, HD), lambda h: (0, h, 0, 0)), pl.BlockSpec((2, HD), head),
            pl.BlockSpec((2, HD), head)]


def _lru_fwd(z6, ya, cw, cb, wr, br, wi, bi, lam):
    S = z6.shape[1]
    nt = S // RT

    def body(z_ref, ya_ref, cw_ref, cb_ref, wr_ref, br_ref, wi_ref, bi_ref, lam_ref, mg_ref, h0_ref, h1_ref, zxp, xc_s):
        sp = _softplus_neg(lam_ref[...])
        _fill_padded(zxp, z_ref.at[0], S)
        _conv_fwd_all(zxp, xc_s, cw_ref, cb_ref, S)

        def scans(i, carry):
            cu, cd = carry
            for u in range(UNROLL):
                j = i * UNROLL + u
                ru = pl.ds(pl.multiple_of(j * RT, RT), RT)
                rd = pl.ds(pl.multiple_of((nt - 1 - j) * RT, RT), RT)
                xu, xd = xc_s[ru, :], xc_s[rd, :]
                _, gi, a, mult = _lru_gates(xu, 0, wr_ref, br_ref, wi_ref, bi_ref, sp)
                hu, cu = _scan_up(a, mult * gi * xu, cu)
                h0_ref[ru, :] = hu
                _, gi, a, mult = _lru_gates(xd, 1, wr_ref, br_ref, wi_ref, bi_ref, sp)
                hd, cd = _scan_down(a, mult * gi * xd, cd)
                h1_ref[rd, :] = hd
            return cu, cd
        z1 = jnp.zeros((1, HD), F32)
        lax.fori_loop(0, nt // UNROLL, scans, (z1, z1))

        def merge(i, c):
            rows = pl.ds(pl.multiple_of(i * RT, RT), RT)
            yb = (h0_ref[rows, :] + h1_ref[rows, :]) * _gelu(z_ref[1, rows, :].astype(F32))
            mg_ref[rows, :] = (ya_ref[rows, :].astype(F32) + _sigmoid(z_ref[2, rows, :].astype(F32)) * yb).astype(BF16)
            return c
        lax.fori_loop(0, nt, merge, 0)

    col = pl.BlockSpec((S, HD), lambda h: (0, h))
    return pl.pallas_call(
        body, name="lru_fwd", grid=(NH,),
        in_specs=[pl.BlockSpec((3, S, HD), lambda h: (1, 0, h)), col] + _lru_specs(S),
        out_specs=[col, col, col],
        out_shape=[jax.ShapeDtypeStruct((S, D), BF16), jax.ShapeDtypeStruct((S, D), F32), jax.ShapeDtypeStruct((S, D), F32)],
        scratch_shapes=[pltpu.VMEM((S + 2 * PADR, HD), F32), pltpu.VMEM((S, HD), F32)],
        compiler_params=_cp(("parallel",)),
    )(z6, ya, cw, cb, wr, br, wi, bi, lam)


def _loss_head(x, tgt, g):
    S = x.shape[0]

    def body(x_ref, t_ref, g_ref, dx_ref, loss_ref, dg_ref):
        @pl.when(pl.program_id(0) == 0)
        def _():
            loss_ref[...] = jnp.zeros_like(loss_ref)
            dg_ref[...] = jnp.zeros_like(dg_ref)
        xv = x_ref[...]
        xh, _ = _rms_hat(xv)
        e = xh * g_ref[...] - t_ref[...]
        loss_ref[...] += jnp.sum(e * e) * (0.5 / D)
        dx, dgs = _rms_bwd(e * (1.0 / D), xv, g_ref[...])
        dx_ref[...] = dx
        dg_ref[...] += dgs

    return pl.pallas_call(
        body, name="loss_head", grid=(S // TM,),
        in_specs=[pl.BlockSpec((TM, D), lambda i: (i, 0)), pl.BlockSpec((TM, D), lambda i: (i, 0)),
                  pl.BlockSpec((1, D), lambda i: (0, 0))],
        out_specs=[pl.BlockSpec((TM, D), lambda i: (i, 0)), pl.BlockSpec((1, 128), lambda i: (0, 0)),
                   pl.BlockSpec((1, D), lambda i: (0, 0))],
        out_shape=[jax.ShapeDtypeStruct((S, D), F32), jax.ShapeDtypeStruct((1, 128), F32), jax.ShapeDtypeStruct((1, D), F32)],
        compiler_params=_cp(("arbitrary",)),
    )(x, tgt, g)


def _bwd_ffn_out(dx, w_fo, gu, l, after=()):
    S = dx.shape[0]

    tm = min(TM_BIG, S)

    def body(dx_ref, w_ref, gu_ref, *rest):
        o_ref = rest[-1]
        for r0 in range(0, tm, TM):
            rows = slice(r0, r0 + TM)
            d = _dot_nt(dx_ref[rows, :].astype(BF16), w_ref[...])
            ga, gb = gu_ref[0, rows, :].astype(F32), gu_ref[1, rows, :].astype(F32)
            sg = _sigmoid(ga)
            o_ref[0, rows, :] = (d * gb * sg * (1.0 + ga * (1.0 - sg))).astype(BF16)
            o_ref[1, rows, :] = (d * ga * sg).astype(BF16)

    pair = pl.BlockSpec((2, None, tm, DFF_SH), lambda i, s: (0, s, i, 0))
    dgu = pl.pallas_call(
        body, name=f"bwd_ffn_out_{l}", grid=(S // tm, 2),
        in_specs=[pl.BlockSpec((tm, D), lambda i, s: (i, 0)), pl.BlockSpec((DFF_SH, D), lambda i, s: (s, 0)), pair]
        + [ANY] * len(after),
        out_specs=pair,
        out_shape=jax.ShapeDtypeStruct((2, 2, S, DFF_SH), BF16),
        compiler_params=_cp(("parallel", "arbitrary")),
    )(dx, w_fo, gu.reshape(2, 2, S, DFF_SH), *after)
    return dgu.reshape(4, S, DFF_SH)


def _mm_tn(a, b, m_blk, tk, name):
    S, M = a.shape

    def body(a_ref, b_ref, o_ref):
        @pl.when(pl.program_id(1) == 0)
        def _():
            o_ref[...] = jnp.zeros_like(o_ref)
        o_ref[...] += _dot_tn(a_ref[...], b_ref[...].astype(BF16))

    return pl.pallas_call(
        body, name=name, grid=(M // m_blk, S // tk),
        in_specs=[pl.BlockSpec((tk, m_blk), lambda m, k: (k, m)), pl.BlockSpec((tk, D), lambda m, k: (k, 0))],
        out_specs=pl.BlockSpec((m_blk, D), lambda m, k: (m, 0)),
        out_shape=jax.ShapeDtypeStruct((M, D), F32),
        compiler_params=_cp(("parallel", "arbitrary")),
    )(a, b)


def _mm_nt_rms_bwd(a, a_specs, w, w_specs, nk, tm, x, g, dres, name, after=(), row_part=None):
    S = x.shape[0]
    sub = len(a_specs)
    row_part = tm if row_part is None else min(row_part, tm)

    def body(*refs):
        a_refs, w_refs = refs[:sub], refs[sub:2 * sub]
        x_ref, g_ref, r_ref = refs[2 * sub:2 * sub + 3]
        dx_ref, dg_ref, acc = refs[-3:]
        i, k = pl.program_id(0), pl.program_id(1)
        @pl.when(k == 0)
        def _():
            acc[...] = jnp.zeros_like(acc)
        for r0 in range(0, tm, row_part):
            rows = slice(r0, r0 + row_part)
            for j in range(sub):
                acc[rows, :] += _dot_nt(a_refs[j][rows, :], w_refs[j][...])

        @pl.when(jnp.logical_and(i == 0, k == 0))
        def _():
            dg_ref[...] = jnp.zeros_like(dg_ref)

        @pl.when(k == nk - 1)
        def _():
            dx, dgs = _rms_bwd(acc[...], x_ref[...], g_ref[...])
            dx_ref[...] = r_ref[...] + dx
            dg_ref[...] += dgs

    row = pl.BlockSpec((tm, D), lambda i, k: (i, 0))
    vec = pl.BlockSpec((1, D), lambda i, k: (0, 0))
    return pl.pallas_call(
        body, name=name, grid=(S // tm, nk),
        in_specs=list(a_specs) + list(w_specs) + [row, vec, row] + [ANY] * len(after),
        out_specs=[row, vec],
        out_shape=[jax.ShapeDtypeStruct((S, D), F32), jax.ShapeDtypeStruct((1, D), F32)],
        scratch_shapes=[pltpu.VMEM((tm, D), F32)],
        compiler_params=_cp(("arbitrary", "arbitrary")),
    )(*[a] * sub, *[w] * sub, x, g, dres, *after)


def _dw_ffn_in(h, dgu, l):
    S = h.shape[0]

    def body(h_ref, b_ref, o_ref):
        @pl.when(pl.program_id(1) == 0)
        def _():
            o_ref[...] = jnp.zeros_like(o_ref)
        o_ref[...] += _dot_tn(h_ref[...], b_ref[...])

    tk = min(2 * TM_BIG, S)
    return pl.pallas_call(
        body, name=f"dw_ffn_in_{l}", grid=(4, S // tk),
        in_specs=[pl.BlockSpec((tk, D), lambda j, k: (k, 0)), pl.BlockSpec((None, tk, DFF_SH), lambda j, k: (j, k, 0))],
        out_specs=pl.BlockSpec((None, D, DFF_SH), lambda j, k: (j, 0, 0)),
        out_shape=jax.ShapeDtypeStruct((4, D, DFF_SH), F32),
        compiler_params=_cp(("parallel", "arbitrary")),
    )(h, dgu)


_HALF_COMPS = ((0, 1, 3), (4, 2, 5))


def _dw_in(h, dz6, l):
    S = h.shape[0]

    def body(h_ref, d0_ref, d1_ref, d2_ref, o_ref):
        @pl.when(pl.program_id(1) == 0)
        def _():
            o_ref[...] = jnp.zeros_like(o_ref)
        hv = h_ref[...]
        for q, d_ref in enumerate((d0_ref, d1_ref, d2_ref)):
            for hf in range(2):
                col = 1024 * q + 512 * hf
                o_ref[col // 1536, :, col % 1536:col % 1536 + 512] += _dot_tn(hv, d_ref[:, 512 * hf:512 * (hf + 1)])

    tk = min(TM_BIG, S)

    def comp(q):
        return pl.BlockSpec((None, tk, D), lambda p, k: (jnp.where(p == 0, _HALF_COMPS[0][q], _HALF_COMPS[1][q]), k, 0))

    return pl.pallas_call(
        body, name=f"dw_in_{l}", grid=(2, S // tk),
        in_specs=[pl.BlockSpec((tk, D), lambda p, k: (k, 0)), comp(0), comp(1), comp(2)],
        out_specs=pl.BlockSpec((2, D, 1536), lambda p, k: (p, 0, 0)),
        out_shape=jax.ShapeDtypeStruct((4, D, 1536), F32),
        compiler_params=_cp(("parallel", "arbitrary")),
    )(h, dz6, dz6, dz6)


def _bwd_out(dx, w_o, merged, l):
    S = dx.shape[0]

    def body(dx_ref, w_ref, m_ref, dm_ref, dw_ref):
        @pl.when(pl.program_id(0) == 0)
        def _():
            dw_ref[...] = jnp.zeros_like(dw_ref)
        dxb = dx_ref[...].astype(BF16)
        dm_ref[...] = _dot_nt(dxb, w_ref[...]).astype(BF16)
        dw_ref[...] += _dot_tn(m_ref[...], dxb)

    tm = TM
    row = pl.BlockSpec((tm, D), lambda i: (i, 0))
    return pl.pallas_call(
        body, name=f"bwd_out_{l}", grid=(S // tm,),
        in_specs=[row, pl.BlockSpec((D, D), lambda i: (0, 0)), row],
        out_specs=[row, pl.BlockSpec((D, D), lambda i: (0, 0))],
        out_shape=[jax.ShapeDtypeStruct((S, D), BF16), jax.ShapeDtypeStruct((D, D), F32)],
        compiler_params=_cp(("arbitrary",)),
    )(dx, w_o, merged)


def _gmlp_bwd(dm, z6, ws_b, wst_b, bs_b, lg, lb, after=()):
    S = z6.shape[1]

    def body(dm_ref, z_ref, ws_ref, wst_ref, bs_ref, lg_ref, lb_ref, *rest):
        dz_ref, dws_ref, dbs_ref, dlg_ref, dlb_ref, mix, dv = rest[-7:]

        @pl.when(pl.program_id(0) == 0)
        def _():
            dws_ref[...] = jnp.zeros_like(dws_ref)
            dbs_ref[...] = jnp.zeros_like(dbs_ref)
            dlg_ref[...] = jnp.zeros_like(dlg_ref)
            dlb_ref[...] = jnp.zeros_like(dlb_ref)
        gv, dgelu_v = _gelu_and_grad(z_ref[1].astype(F32))
        xc = gv - jnp.mean(gv, axis=-1, keepdims=True)
        rs = lax.rsqrt(jnp.mean(xc * xc, axis=-1, keepdims=True) + EPS)
        vh = xc * rs
        vb = (vh * lg_ref[...] + lb_ref[...]).astype(BF16)
        for gi in range(NH):
            cs = slice(gi * HD, (gi + 1) * HD)
            mix[:, cs] = _dot(ws_ref[gi], vb[:, cs])
        u, dgelu_u = _gelu_and_grad(z_ref[0].astype(F32))
        sa = _sigmoid(z_ref[2].astype(F32))
        mixed = mix[...] + bs_ref[...]
        dyg = dm_ref[...].astype(F32)
        dz_ref[2] = (dyg * u * mixed * sa * (1.0 - sa)).astype(BF16)
        dya = dyg * sa
        dz_ref[0] = (dya * mixed * dgelu_u).astype(BF16)
        dmix = dya * u
        dmb = dmix.astype(BF16)
        for gi in range(NH):
            cs = slice(gi * HD, (gi + 1) * HD)
            dv[:, cs] = _dot(wst_ref[gi], dmb[:, cs])
            dws_ref[gi] += _dot_nt(dmb[:, cs], vb[:, cs])
            dbs_ref[gi] += jnp.broadcast_to(jnp.sum(dmix[:, cs], axis=1, keepdims=True), (CHUNK, HD))
        dvv = dv[...]
        dlg_ref[...] += jnp.sum(dvv * vh, axis=0, keepdims=True)
        dlb_ref[...] += jnp.sum(dvv, axis=0, keepdims=True)
        dvh = dvv * lg_ref[...]
        dgv = rs * (dvh - jnp.mean(dvh, axis=-1, keepdims=True) - vh * jnp.mean(dvh * vh, axis=-1, keepdims=True))
        dz_ref[1] = (dgv * dgelu_v).astype(BF16)

    vec = pl.BlockSpec((1, D), lambda i: (0, 0))
    mat = pl.BlockSpec((NH, CHUNK, CHUNK), lambda i: (0, 0, 0))
    return pl.pallas_call(
        body, name="gmlp_bwd", grid=(S // CHUNK,),
        in_specs=[pl.BlockSpec((CHUNK, D), lambda i: (i, 0)), pl.BlockSpec((3, CHUNK, D), lambda i: (0, i, 0)), mat, mat,
                  pl.BlockSpec((CHUNK, D), lambda i: (0, 0)), vec, vec] + [ANY] * len(after),
        out_specs=[pl.BlockSpec((3, CHUNK, D), lambda i: (0, i, 0)), mat, mat, vec, vec],
        out_shape=[jax.ShapeDtypeStruct((6, S, D), BF16), jax.ShapeDtypeStruct((NH, CHUNK, CHUNK), F32),
                   jax.ShapeDtypeStruct((NH, CHUNK, HD), F32), jax.ShapeDtypeStruct((1, D), F32), jax.ShapeDtypeStruct((1, D), F32)],
        scratch_shapes=[pltpu.VMEM((CHUNK, D), F32), pltpu.VMEM((CHUNK, D), F32)],
        compiler_params=_cp(("arbitrary",)),
    )(dm, z6, ws_b, wst_b, bs_b, lg, lb, *after)


def _lru_bwd(dz6, dm, z6, h0, h1, cw, cb, wr, br, wi, bi, lam, after=()):
    S = z6.shape[1]
    nt = S // RT

    def body(dz_in, dm_ref, z_ref, h0_ref, h1_ref, cw_ref, cb_ref, wr_ref, br_ref, wi_ref, bi_ref, lam_ref, *rest):
        dz_ref, dcw_ref, dcb_ref, dwr_ref, dbr_ref, dwi_ref, dbi_ref, dlam_ref, zxp, xc_s, dhs_s, dxcp, r_s, lam_s = rest[-14:]
        del dz_in
        lam = lam_ref[...]
        sp = _softplus_neg(lam)
        row = _row_iota()
        _fill_padded(zxp, z_ref.at[0], S)
        _conv_fwd_all(zxp, xc_s, cw_ref, cb_ref, S)
        zeros = jnp.zeros((PADR, HD), F32)
        dxcp[0:PADR, :] = zeros
        dxcp[PADR + S:2 * PADR + S, :] = zeros
        dwr_ref[...] = jnp.zeros_like(dwr_ref)
        dwi_ref[...] = jnp.zeros_like(dwi_ref)

        def pre(i, c):
            rows = pl.ds(pl.multiple_of(i * RT, RT), RT)
            hs = h0_ref[rows, :] + h1_ref[rows, :]
            dmv = dm_ref[rows, :].astype(F32)
            sb = _sigmoid(z_ref[2, rows, :].astype(F32))
            gg, dgg = _gelu_and_grad(z_ref[1, rows, :].astype(F32))
            dz_ref[2, rows, :] = (dmv * hs * gg * sb * (1.0 - sb)).astype(BF16)
            dyb = dmv * sb
            dz_ref[1, rows, :] = (dyb * hs * dgg).astype(BF16)
            dhs_s[rows, :] = dyb * gg
            return c
        lax.fori_loop(0, nt, pre, 0)

        def gate_bwd(d, gates, lamv, da, xc):
            r, gi, a, mult = gates
            dmult = lamv * gi * xc
            dgi = lamv * mult * xc
            dlog = (da - dmult * a / mult) * a
            dpr = (dlog * (-LRU_C) * sp[d:d + 1, :]) * r * (1.0 - r)
            dpi = dgi * gi * (1.0 - gi)
            xb, dprb, dpib = xc.astype(BF16), dpr.astype(BF16), dpi.astype(BF16)
            dwr_ref[d] += _dot_tn(xb, dprb)
            dwi_ref[d] += _dot_tn(xb, dpib)
            dxc = lamv * mult * gi + _dot_nt(dprb, wr_ref[d]) + _dot_nt(dpib, wi_ref[d])
            return dxc, (jnp.sum(dlog * r, axis=0, keepdims=True) * (-LRU_C), jnp.sum(dpr, axis=0, keepdims=True),
                         jnp.sum(dpi, axis=0, keepdims=True))

        def rgates(i, c):
            for u in range(UNROLL):
                rows = pl.ds(pl.multiple_of((i * UNROLL + u) * RT, RT), RT)
                xb = xc_s[rows, :].astype(BF16)
                for d in range(2):
                    r_s[d, rows, :] = _sigmoid(_dot(xb, wr_ref[d]) + br_ref[d:d + 1, :])
            return c
        lax.fori_loop(0, nt // UNROLL, rgates, 0)

        def chains(i, carry):
            qn, qp = carry
            for u in range(UNROLL):
                j = i * UNROLL + u
                rd = pl.ds(pl.multiple_of((nt - 1 - j) * RT, RT), RT)
                a, dhs = _decay(r_s[0, rd, :], sp[0:1, :])[0], dhs_s[rd, :]
                q, q_first = _scan_down(a, a * dhs, qn)
                lam_s[0, rd, :] = dhs + jnp.where(row == RT - 1, qn, pltpu.roll(q, RT - 1, 0))
                qn = q_first
                ru = pl.ds(pl.multiple_of(j * RT, RT), RT)
                a, dhs = _decay(r_s[1, ru, :], sp[1:2, :])[0], dhs_s[ru, :]
                q, q_last = _scan_up(a, a * dhs, qp)
                lam_s[1, ru, :] = dhs + jnp.where(row == 0, qp, pltpu.roll(q, 1, 0))
                qp = q_last
            return qn, qp

        z1 = jnp.zeros((1, HD), F32)
        lax.fori_loop(0, nt // UNROLL, chains, (z1, z1))

        ct = min(GRAD_ROWS, S)
        crow = lax.broadcasted_iota(jnp.int32, (ct, HD), 0)

        def tile_grads(i, acc):
            t0 = pl.multiple_of(i * ct, ct)
            rows = pl.ds(t0, ct)
            xc = xc_s[rows, :]
            xb = xc.astype(BF16)
            tp = pl.multiple_of(jnp.maximum(t0 - PADR, 0), PADR)
            prev = jnp.where(t0 > 0, h0_ref[pl.ds(tp, PADR), :][PADR - 1:PADR, :], 0.0)
            tn = pl.multiple_of(jnp.minimum(t0 + ct, S - PADR), PADR)
            nxt = jnp.where(t0 + ct < S, h1_ref[pl.ds(tn, PADR), :][0:1, :], 0.0)
            hside = (jnp.where(crow == 0, prev, pltpu.roll(h0_ref[rows, :], 1, 0)),
                     jnp.where(crow == ct - 1, nxt, pltpu.roll(h1_ref[rows, :], ct - 1, 0)))
            dxc, sums = 0.0, ()
            for d in range(2):
                r = r_s[d, rows, :]
                gi = _sigmoid(_dot(xb, wi_ref[d]) + bi_ref[d:d + 1, :])
                a, mult = _decay(r, sp[d:d + 1, :])
                lamv = lam_s[d, rows, :]
                dxc_d, s_d = gate_bwd(d, (r, gi, a, mult), lamv, lamv * hside[d], xc)
                dxc = dxc + dxc_d
                sums = sums + s_d
            dxcp[pl.ds(t0 + PADR, ct), :] = dxc
            return tuple(x + y for x, y in zip(acc, sums))

        s_sp0, s_br0, s_bi0, s_sp1, s_br1, s_bi1 = lax.fori_loop(0, S // ct, tile_grads, (z1,) * 6)

        dsp = jnp.concatenate([s_sp0, s_sp1], axis=0)
        dlam_ref[...] = -dsp * _sigmoid(-lam)
        dbr_ref[...] = jnp.concatenate([s_br0, s_br1], axis=0)
        dbi_ref[...] = jnp.concatenate([s_bi0, s_bi1], axis=0)

        def conv_bwd(i, carry):
            c0, c1, c2, c3, cb_ = carry
            t0 = pl.multiple_of(i * RT, RT)
            dwin = dxcp[pl.ds(t0, RT + 2 * PADR), :]
            d0 = _shifted(dwin, 0)
            dz_ref[0, pl.ds(t0, RT), :] = (_shifted(dwin, 1) * cw_ref[0:1, :] + d0 * cw_ref[1:2, :]
                                           + _shifted(dwin, -1) * cw_ref[2:3, :] + _shifted(dwin, -2) * cw_ref[3:4, :]).astype(BF16)
            xm1, x0, xp1, xp2 = _conv_taps(zxp[pl.ds(t0, RT + 2 * PADR), :])
            sm = lambda v: jnp.sum(v, axis=0, keepdims=True)
            return c0 + sm(d0 * xm1), c1 + sm(d0 * x0), c2 + sm(d0 * xp1), c3 + sm(d0 * xp2), cb_ + sm(d0)

        c0, c1, c2, c3, cb_ = lax.fori_loop(0, nt, conv_bwd, (z1, z1, z1, z1, z1))
        dcw_ref[...] = jnp.concatenate([c0, c1, c2, c3], axis=0)
        dcb_ref[...] = cb_

    col = pl.BlockSpec((S, HD), lambda h: (0, h))
    head = lambda h: (0, h)
    wspec = pl.BlockSpec((2, None, HD, HD), lambda h: (0, h, 0, 0))
    return pl.pallas_call(
        body, name="lru_bwd", grid=(NH,),
        in_specs=[pl.BlockSpec(memory_space=pl.ANY), col, pl.BlockSpec((3, S, HD), lambda h: (1, 0, h)), col, col] + _lru_specs(S)
        + [ANY] * len(after),
        out_specs=[pl.BlockSpec((3, S, HD), lambda h: (1, 0, h)), pl.BlockSpec((4, HD), head), pl.BlockSpec((1, HD), head),
                   wspec, pl.BlockSpec((2, HD), head), wspec, pl.BlockSpec((2, HD), head), pl.BlockSpec((2, HD), head)],
        out_shape=[jax.ShapeDtypeStruct((6, S, D), BF16), jax.ShapeDtypeStruct((4, D), F32), jax.ShapeDtypeStruct((1, D), F32),
                   jax.ShapeDtypeStruct((2, NH, HD, HD), F32), jax.ShapeDtypeStruct((2, D), F32),
                   jax.ShapeDtypeStruct((2, NH, HD, HD), F32), jax.ShapeDtypeStruct((2, D), F32), jax.ShapeDtypeStruct((2, D), F32)],
        scratch_shapes=[pltpu.VMEM((S + 2 * PADR, HD), F32), pltpu.VMEM((S, HD), F32), pltpu.VMEM((S, HD), F32),
                        pltpu.VMEM((S + 2 * PADR, HD), F32), pltpu.VMEM((2, S, HD), F32), pltpu.VMEM((2, S, HD), F32)],
        input_output_aliases={0: 0},
        compiler_params=_cp(("parallel",)),
    )(dz6, dm, z6, h0, h1, cw, cb, wr, br, wi, bi, lam, *after)


LAYER_SMALL = ("norm1_g", "gmlp_ln_g", "gmlp_ln_b", "gmlp_w_s", "gmlp_b_s", "conv_w", "conv_b",
               "lru_w_r", "lru_b_r", "lru_w_i", "lru_b_i", "lru_lambda", "norm2_g")


def _forward_layer(l, x, p, wb, after=(), rest=None, near_end=None):
    g1, g2 = p["norm1_g"][l][None], p["norm2_g"][l][None]
    ws_b = p["gmlp_w_s"][l].astype(BF16)
    tm = dict(ws_b=ws_b, wst_b=jnp.swapaxes(ws_b, 1, 2), bs_b=jnp.repeat(p["gmlp_b_s"][l].T, HD, axis=1),
              lg=p["gmlp_ln_g"][l][None], lb=p["gmlp_ln_b"][l][None])
    lru = (p["conv_w"][l], p["conv_b"][l][None], p["lru_w_r"][l].astype(BF16), p["lru_b_r"][l],
           p["lru_w_i"][l].astype(BF16), p["lru_b_i"][l], p["lru_lambda"][l])
    z6, hn1 = _mm_in(x, g1, wb["w_in"], l, after)
    ya = _gmlp_fwd(z6, tm["ws_b"], tm["bs_b"], tm["lg"], tm["lb"])
    merged, h0, h1 = _lru_fwd(z6, ya, *lru)
    if rest is not None:
        wb = dict(wb, **rest(merged))
    x1 = _mm_res(merged, wb["w_out"], x, l, "mm_out")
    gu, ff, hn2 = _mm_ffn_in(x1, g2, wb["w_ffn_in"], l)
    x2 = _mm_res(ff, wb["w_ffn_out"], x1, l, "mm_ffn_out", () if near_end is None else tuple(near_end(gu)))
    return x2, dict(x=x, z6=z6, h0=h0, h1=h1, merged=merged, x1=x1, gu=gu, ff=ff, g1=g1, g2=g2, tm=tm, lru=lru,
                    hn1=hn1, hn2=hn2, wb=wb)


def _backward_layer(l, dx, s, after=(), midway=None, midway2=None, late=None):
    S = dx.shape[0]
    tm, wb = s["tm"], s["wb"]
    g2 = s["g2"]
    dgu = _bwd_ffn_out(dx, wb["w_ffn_out"], s["gu"], l, after)
    tmb = min(TM_BIG, S)
    dwfo = _mm_tn(s["ff"], dx, DFF_SH, tmb, f"dw_ffn_out_{l}")
    dx1, dg2 = _mm_nt_rms_bwd(
        dgu, [pl.BlockSpec((None, tmb, DFF_SH), lambda i, k: (k, i, 0))],
        wb["w_ffn_in"], [pl.BlockSpec((None, D, DFF_SH), lambda i, k: (k, 0, 0))],
        4, tmb, s["x1"], g2, dx, f"bwd_ffn_in_{l}")
    dwfi = _dw_ffn_in(s["hn2"], dgu, l)
    dmg, dwo = _bwd_out(dx1, wb["w_out"], s["merged"], l)
    mid = () if midway is None else tuple(midway([dwo, dwfi, dwfo]))
    dz6, dws, dbs, dlg, dlb = _gmlp_bwd(dmg, s["z6"], tm["ws_b"], tm["wst_b"], tm["bs_b"], tm["lg"], tm["lb"], mid)
    mid2 = () if midway2 is None else tuple(midway2(dws))
    dz6, dcw, dcb, dwr, dbr, dwi, dbi, dlam = _lru_bwd(dz6, dmg, s["z6"], s["h0"], s["h1"], *s["lru"], after=mid2)

    sub = 3

    def dz_tile(j):
        return pl.BlockSpec((None, tmb, 512), lambda i, k: ((sub * k + j) // 2, i, (sub * k + j) % 2))

    def w_tile(j):
        def w_map(i, k):
            sh, tl = _in_tile(sub * k + j)
            return (sh, 0, tl)
        return pl.BlockSpec((None, D, 512), w_map)

    dwin = _dw_in(s["hn1"], dz6, l)
    small = dict(gmlp_ln_g=dlg[0], gmlp_ln_b=dlb[0], gmlp_w_s=dws, gmlp_b_s=dbs[:, :, 0], conv_w=dcw, conv_b=dcb[0],
                 lru_w_r=dwr, lru_b_r=dbr, lru_w_i=dwi, lru_b_i=dbi, lru_lambda=dlam, norm2_g=dg2[0])
    tail = () if late is None else tuple(late([dwin], small))
    dx0, dg1 = _mm_nt_rms_bwd(
        dz6, [dz_tile(j) for j in range(sub)], wb["w_in"], [w_tile(j) for j in range(sub)],
        N_IN_T // sub, tmb, s["x"], s["g1"], dx1, f"bwd_in_{l}", tail)
    return dx0, [dwin, dwo, dwfi, dwfo], dict(small, norm1_g=dg1[0])


def _local_step(x, tgt, p, wbs):
    saved = []
    for l in range(2):
        x, s = _forward_layer(l, x, p, wbs[l])
        saved.append(s)
    dx, loss_v, dfg = _loss_head(x, tgt, p["final_g"][None])
    big, smalls = [None, None], [None, None]
    for l in (1, 0):
        dx, big[l], smalls[l] = _backward_layer(l, dx, saved[l])
    small = {k: jnp.stack([smalls[0][k], smalls[1][k]]) for k in LAYER_SMALL}
    small["final_g"] = dfg[0]
    return loss_v, dx, big, small


def _place():
    x, y, c = lax.axis_index("x"), lax.axis_index("y"), lax.axis_index("c")
    return x, y, c, 2 * x + y


def _chip_at(x, y, d):
    px = 1 - x if d & 2 else x
    py = 1 - y if d & 1 else y
    return px, py, 2 * px + py


HBM = pl.BlockSpec(memory_space=pltpu.HBM)
SEM = pl.BlockSpec(memory_space=pltpu.SEMAPHORE)
DATAFLOW = pltpu.SideEffectType.DATAFLOW_SIDE_EFFECTING


def _in_hbm(a):
    return pltpu.with_memory_space_constraint(a, pltpu.HBM)


def _cast_into(wf, l, chip_arr, name):
    _, rows, cols = wf.shape
    rh = rows // 2

    def body(ch_ref, w_ref, o_ref):
        o_ref[...] = w_ref[...].astype(BF16)

    return pl.pallas_call(
        body, name=name, out_shape=jax.ShapeDtypeStruct((4, 2, rh, cols), BF16),
        grid_spec=pltpu.PrefetchScalarGridSpec(
            num_scalar_prefetch=1, grid=(2,),
            in_specs=[pl.BlockSpec((None, None, rh, cols), lambda h, ch: (l, h, 0, 0))],
            out_specs=pl.BlockSpec((None, None, rh, cols), lambda h, ch: (ch[0], h, 0, 0))),
        compiler_params=_cp(("parallel",)),
    )(chip_arr, wf.reshape(2, 2, rh, cols))


def _half_block(ref, chip, half, to, send_sem, recv_sem):
    blk = ref.at[chip, half]
    return pltpu.make_async_remote_copy(src_ref=blk, dst_ref=blk, send_sem=send_sem, recv_sem=recv_sem,
                                        device_id=to, device_id_type=MESH)


def _gather_weights(bufs, tiny):
    nt = len(bufs)
    n_ici = max(nt * 3, 1)

    def body(*refs):
        tiny_ref = refs[nt]
        o_refs, tiny_o = refs[nt + 1:2 * nt + 1], refs[2 * nt + 1]
        send, recv, fsend, frecv, tsend, trecv, lsem = refs[2 * nt + 2:]
        x, y, c, chip = _place()
        local = pltpu.make_async_copy(tiny_ref, tiny_o.at[chip], lsem)
        local.start()

        def tin(d, origin_chip, to):
            return pltpu.make_async_remote_copy(
                src_ref=tiny_ref, dst_ref=tiny_o.at[origin_chip], send_sem=tsend.at[d - 1], recv_sem=trecv.at[d - 1],
                device_id=to, device_id_type=MESH)

        sends = []
        for t in range(nt):
            for d in (1, 2, 3):
                px, py, _ = _chip_at(x, y, d)
                sends.append(_half_block(o_refs[t], chip, c, (px, py, c), send.at[3 * t + d - 1], recv.at[3 * t + d - 1]))
        for d in (1, 2, 3):
            px, py, _ = _chip_at(x, y, d)
            sends.append(tin(d, chip, (px, py, c)))
        for cp in sends:
            cp.start()
        passed = []
        for t in range(nt):
            for d in (1, 2, 3):
                k = 3 * t + d - 1
                _, _, pchip = _chip_at(x, y, d)
                _half_block(o_refs[t], pchip, c, (x, y, c), send.at[k], recv.at[k]).wait_recv()
                f = _half_block(o_refs[t], pchip, c, (x, y, 1 - c), fsend.at[k], frecv.at[k])
                f.start()
                passed.append(f)
        for t in range(nt):
            for d in (1, 2, 3):
                k = 3 * t + d - 1
                _, _, pchip = _chip_at(x, y, d)
                _half_block(o_refs[t], pchip, 1 - c, (x, y, 1 - c), fsend.at[k], frecv.at[k]).wait_recv()
        for d in (1, 2, 3):
            _, _, pchip = _chip_at(x, y, d)
            tin(d, pchip, (x, y, c)).wait_recv()
        for cp in sends + passed:
            cp.wait_send()
        local.wait()

    out_shape = [jax.ShapeDtypeStruct(b.shape, b.dtype) for b in bufs]
    out_shape.append(jax.ShapeDtypeStruct((4,) + tiny.shape, tiny.dtype))
    outs = pl.pallas_call(
        body, name="gather_weights_0", out_shape=out_shape,
        in_specs=[ANY] * (nt + 1), out_specs=[ANY] * (nt + 1),
        scratch_shapes=[pltpu.SemaphoreType.DMA((n_ici,)), pltpu.SemaphoreType.DMA((n_ici,)),
                        pltpu.SemaphoreType.DMA((n_ici,)), pltpu.SemaphoreType.DMA((n_ici,)),
                        pltpu.SemaphoreType.DMA((3,)), pltpu.SemaphoreType.DMA((3,)), pltpu.SemaphoreType.DMA],
        input_output_aliases={t: t for t in range(nt)},
        compiler_params=_cp(has_side_effects=True),
    )(*bufs, tiny)
    return outs[:nt], outs[nt]


def _gather_start(bufs, tag, after=()):
    nt, na = len(bufs), len(after)

    def body(*refs):
        b_refs = refs[:nt]
        send, recv = refs[nt + na], refs[nt + na + 1]
        token = refs[2 * nt + na + 2]
        x, y, c, chip = _place()
        for t in range(nt):
            for d in (1, 2, 3):
                px, py, _ = _chip_at(x, y, d)
                _half_block(b_refs[t], chip, c, (px, py, c), send.at[3 * t + d - 1], recv.at[3 * t + d - 1]).start()
        token[...] = jnp.zeros_like(token)

    outs = pl.pallas_call(
        body, name=f"gather_start_{tag}",
        out_shape=(pltpu.SemaphoreType.DMA((3 * nt,)), pltpu.SemaphoreType.DMA((3 * nt,)),
                   *[pltpu.HBM(b.shape, b.dtype) for b in bufs], jax.ShapeDtypeStruct((8, 128), F32)),
        in_specs=[HBM] * nt + [ANY] * na, out_specs=(SEM, SEM, *[HBM] * nt, pl.BlockSpec(memory_space=pltpu.VMEM)),
        input_output_aliases={t: 2 + t for t in range(nt)},
        compiler_params=pltpu.CompilerParams(has_side_effects=DATAFLOW),
    )(*[_in_hbm(b) for b in bufs], *after)
    return outs[0], outs[1], list(outs[2:2 + nt]), outs[2 + nt]


def _gather_wait(send, recv, bufs, after, tag):
    nt = len(bufs)

    def body(*refs):
        b_refs = refs[:nt]
        send_ref, recv_ref = refs[nt], refs[nt + 1]
        x, y, c, chip = _place()
        for t in range(nt):
            for d in (1, 2, 3):
                k = 3 * t + d - 1
                px, py, pchip = _chip_at(x, y, d)
                _half_block(b_refs[t], chip, c, (px, py, c), send_ref.at[k], recv_ref.at[k]).wait_send()
                _half_block(b_refs[t], pchip, c, (px, py, c), send_ref.at[k], recv_ref.at[k]).wait_recv()

    outs = pl.pallas_call(
        body, name=f"gather_wait_{tag}", out_shape=[pltpu.HBM(b.shape, b.dtype) for b in bufs],
        in_specs=[HBM] * nt + [SEM, SEM, ANY], out_specs=[HBM] * nt,
        input_output_aliases={t: t for t in range(nt)},
        compiler_params=pltpu.CompilerParams(has_side_effects=DATAFLOW),
    )(*bufs, send, recv, after)
    return list(outs)


def _gather_pass_on(bufs, tag):
    nt = len(bufs)

    def body(*refs):
        o_refs = refs[nt:2 * nt]
        fsend, frecv = refs[2 * nt:]
        x, y, c, _ = _place()
        cps = []
        for t in range(nt):
            for d in (1, 2, 3):
                k = 3 * t + d - 1
                _, _, pchip = _chip_at(x, y, d)
                cps.append(_half_block(o_refs[t], pchip, c, (x, y, 1 - c), fsend.at[k], frecv.at[k]))
        for cp in cps:
            cp.start()
        for t in range(nt):
            for d in (1, 2, 3):
                k = 3 * t + d - 1
                _, _, pchip = _chip_at(x, y, d)
                _half_block(o_refs[t], pchip, 1 - c, (x, y, 1 - c), fsend.at[k], frecv.at[k]).wait_recv()
        for cp in cps:
            cp.wait_send()

    return pl.pallas_call(
        body, name=f"gather_pass_on_{tag}", out_shape=[jax.ShapeDtypeStruct(b.shape, b.dtype) for b in bufs],
        in_specs=[ANY] * nt, out_specs=[ANY] * nt,
        scratch_shapes=[pltpu.SemaphoreType.DMA((3 * nt,)), pltpu.SemaphoreType.DMA((3 * nt,))],
        input_output_aliases={t: t for t in range(nt)},
        compiler_params=_cp(has_side_effects=True),
    )(*bufs)


def _to_sibling_halves(gs, l):
    nt = len(gs)

    def body(*refs):
        g_refs, o_refs = refs[:nt], refs[nt:2 * nt]
        send, recv = refs[2 * nt:]
        x, y, c, _ = _place()
        cps = [pltpu.make_async_remote_copy(
            src_ref=g_refs[t].at[k, 1 - c], dst_ref=o_refs[t].at[k], send_sem=send.at[4 * t + k], recv_sem=recv.at[4 * t + k],
            device_id=(x, y, 1 - c), device_id_type=MESH) for t in range(nt) for k in range(4)]
        for cp in cps:
            cp.start()
        for cp in cps:
            cp.wait()

    return pl.pallas_call(
        body, name=f"grads_to_sibling_{l}", out_shape=[jax.ShapeDtypeStruct((4,) + g.shape[2:], g.dtype) for g in gs],
        in_specs=[ANY] * nt, out_specs=[ANY] * nt,
        scratch_shapes=[pltpu.SemaphoreType.DMA((4 * nt,)), pltpu.SemaphoreType.DMA((4 * nt,))],
        compiler_params=_cp(has_side_effects=True),
    )(*gs)


def _chip_copy(c_ref, land_ref, x, y, c, d, send_sem, recv_sem):
    px, py, pchip = _chip_at(x, y, d)
    return pltpu.make_async_remote_copy(src_ref=c_ref.at[pchip], dst_ref=land_ref.at[d - 1], send_sem=send_sem, recv_sem=recv_sem,
                                        device_id=(px, py, c), device_id_type=MESH)


def _exchange_start(srcs, lands, copies, nsem, name):
    ns, n = len(srcs), len(srcs) + len(lands)

    def body(*refs):
        for cp in copies(refs[:ns], refs[ns:n], refs[n], refs[n + 1]):
            cp.start()
        token = refs[2 * n + 2]
        token[...] = jnp.zeros_like(token)

    outs = pl.pallas_call(
        body, name=name,
        out_shape=(pltpu.SemaphoreType.DMA((nsem,)), pltpu.SemaphoreType.DMA((nsem,)),
                   *[pltpu.HBM(a.shape, a.dtype) for a in list(srcs) + list(lands)], jax.ShapeDtypeStruct((8, 128), F32)),
        in_specs=[HBM] * n, out_specs=(SEM, SEM, *[HBM] * n, pl.BlockSpec(memory_space=pltpu.VMEM)),
        input_output_aliases={i: 2 + i for i in range(n)},
        compiler_params=pltpu.CompilerParams(has_side_effects=DATAFLOW),
    )(*[_in_hbm(a) for a in list(srcs) + list(lands)])
    return outs[0], outs[1], list(outs[2:2 + ns]), list(outs[2 + ns:2 + n]), outs[2 + n]


def _exchange_wait(send, recv, srcs, lands, after, copies, name):
    ns, n = len(srcs), len(srcs) + len(lands)

    def body(*refs):
        for cp in copies(refs[:ns], refs[ns:n], refs[n], refs[n + 1]):
            cp.wait_send()
            cp.wait_recv()

    outs = pl.pallas_call(
        body, name=name, out_shape=[pltpu.HBM(a.shape, a.dtype) for a in list(srcs) + list(lands)],
        in_specs=[HBM] * n + [SEM, SEM, ANY], out_specs=[HBM] * n,
        input_output_aliases={i: i for i in range(n)},
        compiler_params=pltpu.CompilerParams(has_side_effects=DATAFLOW),
    )(*srcs, *lands, send, recv, after)
    return list(outs[:ns]), list(outs[ns:])


def _pass_on_copies(b_refs, land_refs, send, recv):
    del land_refs
    x, y, c, _ = _place()
    return [_half_block(b_refs[t], _chip_at(x, y, d)[2], c, (x, y, 1 - c), send.at[3 * t + d - 1], recv.at[3 * t + d - 1])
            for t in range(len(b_refs)) for d in (1, 2, 3)]


def _chips_copies(c_refs, land_refs, send, recv):
    x, y, c, _ = _place()
    return [_chip_copy(c_refs[t], land_refs[t], x, y, c, d, send.at[3 * t + d - 1], recv.at[3 * t + d - 1])
            for t in range(len(c_refs)) for d in (1, 2, 3)]


def _sibling_copies(g_refs, land_refs, send, recv):
    x, y, c, _ = _place()
    return [pltpu.make_async_remote_copy(
        src_ref=g_refs[t].at[k, 1 - c], dst_ref=land_refs[t].at[k], send_sem=send.at[4 * t + k], recv_sem=recv.at[4 * t + k],
        device_id=(x, y, 1 - c), device_id_type=MESH) for t in range(len(g_refs)) for k in range(4)]


def _join_halves(fs, l):
    nt = len(fs)

    def body(*refs):
        o_refs = refs[nt:2 * nt]
        send, recv = refs[2 * nt:]
        x, y, c, _ = _place()
        cps = [pltpu.make_async_remote_copy(
            src_ref=o_refs[t].at[c], dst_ref=o_refs[t].at[c], send_sem=send.at[t], recv_sem=recv.at[t],
            device_id=(x, y, 1 - c), device_id_type=MESH) for t in range(nt)]
        for cp in cps:
            cp.start()
        for cp in cps:
            cp.wait()

    return pl.pallas_call(
        body, name=f"grads_join_{l}", out_shape=[jax.ShapeDtypeStruct(a.shape, a.dtype) for a in fs],
        in_specs=[ANY] * nt, out_specs=[ANY] * nt,
        scratch_shapes=[pltpu.SemaphoreType.DMA((nt,)), pltpu.SemaphoreType.DMA((nt,))],
        input_output_aliases={t: t for t in range(nt)},
        compiler_params=_cp(has_side_effects=True),
    )(*fs)


def _add_half(g, r, c_arr, name):
    _, _, rh, cols = g.shape

    def body(c_ref, g_ref, r_ref, o_ref):
        o_ref[...] = (g_ref[...] + r_ref[...]).astype(BF16)

    blk = pl.BlockSpec((None, rh, cols), lambda k, cr: (k, 0, 0))
    return pl.pallas_call(
        body, name=name, out_shape=jax.ShapeDtypeStruct((4, rh, cols), BF16),
        grid_spec=pltpu.PrefetchScalarGridSpec(
            num_scalar_prefetch=1, grid=(4,),
            in_specs=[pl.BlockSpec((None, None, rh, cols), lambda k, cr: (k, cr[0], 0, 0)), blk], out_specs=blk),
        compiler_params=_cp(("parallel",)),
    )(c_arr, g, r)


def _sum_chips(cs, r3, place_arr, name):
    _, rh, cols = cs.shape
    rb = rh // 2

    def body(pl_ref, a_ref, r0_ref, r1_ref, r2_ref, o_ref):
        up = lambda ref: ref[...].astype(F32)
        o_ref[...] = ((up(a_ref) + up(r0_ref)) + up(r1_ref)) + up(r2_ref)

    def slot(d):
        return pl.BlockSpec((None, rb, cols), lambda i, pa: (d, i, 0))

    return pl.pallas_call(
        body, name=name, out_shape=jax.ShapeDtypeStruct((2, rh, cols), F32),
        grid_spec=pltpu.PrefetchScalarGridSpec(
            num_scalar_prefetch=1, grid=(2,),
            in_specs=[pl.BlockSpec((None, rb, cols), lambda i, pa: (pa[0], i, 0)), slot(0), slot(1), slot(2)],
            out_specs=pl.BlockSpec((None, rb, cols), lambda i, pa: (pa[1], i, 0))),
        compiler_params=_cp(("parallel",)),
    )(place_arr, cs, r3, r3, r3)


def _allreduce_small(pack):
    rows = pack.shape[0]
    hr = rows // 2

    def body(p_ref, o_ref, sib, slots, s1, r1, s2, r2, s3, r3):
        x, y, c, chip = _place()
        sibling = (x, y, 1 - c)
        ex = pltpu.make_async_remote_copy(src_ref=p_ref, dst_ref=sib, send_sem=s1, recv_sem=r1,
                                          device_id=sibling, device_id_type=MESH)
        ex.start()
        ex.wait()
        half = pl.ds(pl.multiple_of(c * hr, 16), hr)
        slots[0] = (p_ref[half, :] + sib[half, :]).astype(BF16)
        cps = []
        for d in (1, 2, 3):
            px, py, _ = _chip_at(x, y, d)
            cps.append(pltpu.make_async_remote_copy(
                src_ref=slots.at[0], dst_ref=slots.at[d], send_sem=s2.at[d - 1], recv_sem=r2.at[d - 1],
                device_id=(px, py, c), device_id_type=MESH))
        for cp in cps:
            cp.start()
        for cp in cps:
            cp.wait()
        tot = slots[chip].astype(F32)
        for k in (1, 2, 3):
            tot = tot + slots[jnp.bitwise_xor(chip, k)].astype(F32)
        o_ref[half, :] = tot
        back = pltpu.make_async_remote_copy(src_ref=o_ref.at[half, :], dst_ref=o_ref.at[half, :], send_sem=s3, recv_sem=r3,
                                            device_id=sibling, device_id_type=MESH)
        back.start()
        back.wait()

    vm = pl.BlockSpec(memory_space=pltpu.VMEM)
    return pl.pallas_call(
        body, name="allreduce_small", out_shape=jax.ShapeDtypeStruct((rows, 128), F32),
        in_specs=[vm], out_specs=vm,
        scratch_shapes=[pltpu.VMEM((rows, 128), F32), pltpu.VMEM((4, hr, 128), BF16),
                        pltpu.SemaphoreType.DMA, pltpu.SemaphoreType.DMA, pltpu.SemaphoreType.DMA((3,)), pltpu.SemaphoreType.DMA((3,)),
                        pltpu.SemaphoreType.DMA, pltpu.SemaphoreType.DMA],
        compiler_params=_cp(has_side_effects=True),
    )(pack)


def _small_chip_sum(pack):
    rows = pack.shape[0]
    hr = rows // 2

    def body(p_ref, o_ref, sib, s1, r1):
        x, y, c, _ = _place()
        ex = pltpu.make_async_remote_copy(src_ref=p_ref, dst_ref=sib, send_sem=s1, recv_sem=r1,
                                          device_id=(x, y, 1 - c), device_id_type=MESH)
        ex.start()
        ex.wait()
        half = pl.ds(pl.multiple_of(c * hr, 16), hr)
        o_ref[...] = (p_ref[half, :] + sib[half, :]).astype(BF16)

    vm = pl.BlockSpec(memory_space=pltpu.VMEM)
    return pl.pallas_call(
        body, name="small_chip_sum", out_shape=jax.ShapeDtypeStruct((hr, 128), BF16), in_specs=[vm], out_specs=vm,
        scratch_shapes=[pltpu.VMEM((rows, 128), F32), pltpu.SemaphoreType.DMA, pltpu.SemaphoreType.DMA],
        compiler_params=_cp(has_side_effects=True),
    )(pack)


def _small_copies(c_refs, land_refs, send, recv):
    x, y, c, _ = _place()
    cps = []
    for d in (1, 2, 3):
        px, py, _ = _chip_at(x, y, d)
        cps.append(pltpu.make_async_remote_copy(src_ref=c_refs[0], dst_ref=land_refs[0].at[d - 1], send_sem=send.at[d - 1],
                                                recv_sem=recv.at[d - 1], device_id=(px, py, c), device_id_type=MESH))
    return cps


def _small_total(csum, land):
    hr = csum.shape[0]

    def body(c_ref, l_ref, o_ref, slots, s3, r3):
        x, y, c, chip = _place()
        slots[0] = c_ref[...]
        for d in (1, 2, 3):
            slots[d] = l_ref[d - 1]
        tot = slots[chip].astype(F32)
        for k in (1, 2, 3):
            tot = tot + slots[jnp.bitwise_xor(chip, k)].astype(F32)
        half = pl.ds(pl.multiple_of(c * hr, 16), hr)
        o_ref[half, :] = tot
        back = pltpu.make_async_remote_copy(src_ref=o_ref.at[half, :], dst_ref=o_ref.at[half, :], send_sem=s3, recv_sem=r3,
                                            device_id=(x, y, 1 - c), device_id_type=MESH)
        back.start()
        back.wait()

    vm = pl.BlockSpec(memory_space=pltpu.VMEM)
    return pl.pallas_call(
        body, name="small_total", out_shape=jax.ShapeDtypeStruct((2 * hr, 128), F32), in_specs=[vm, vm], out_specs=vm,
        scratch_shapes=[pltpu.VMEM((4, hr, 128), BF16), pltpu.SemaphoreType.DMA, pltpu.SemaphoreType.DMA],
        compiler_params=_cp(has_side_effects=True),
    )(csum, land)


def _adam_math(gv, wv, mv, vv):
    m2 = ADAM_B1 * mv + (1.0 - ADAM_B1) * gv
    v2 = ADAM_B2 * vv + (1.0 - ADAM_B2) * (gv * gv)
    m_hat = m2 / (1.0 - ADAM_B1 ** ADAM_STEP)
    v_hat = v2 / (1.0 - ADAM_B2 ** ADAM_STEP)
    return -ADAM_LR * (m_hat / (jnp.sqrt(v_hat) + ADAM_EPS) + ADAM_WD * wv), m2, v2


def _adam(g, w, m, v, name):
    rows, cols = g.shape
    rb = rows // 4

    def body(g_ref, w_ref, m_ref, v_ref, d_ref, m2_ref, v2_ref):
        d_ref[...], m2_ref[...], v2_ref[...] = _adam_math(g_ref[...], w_ref[...], m_ref[...], v_ref[...])

    blk = pl.BlockSpec((rb, cols), lambda i: (i, 0))
    shp = jax.ShapeDtypeStruct((rows, cols), F32)
    return pl.pallas_call(
        body, name=name, grid=(4,), in_specs=[blk] * 4, out_specs=[blk] * 3, out_shape=[shp] * 3,
        compiler_params=_cp(("parallel",)),
    )(g, w, m, v)


def _adam_layer(g, w, m, v, l, prev, name):
    rows, cols = g.shape
    rb = rows // 4

    def body(g_ref, w_ref, m_ref, v_ref, *rest):
        go_ref, d_ref, m2_ref, v2_ref = rest[-4:]
        gv = g_ref[...]
        go_ref[...] = gv
        d_ref[...], m2_ref[...], v2_ref[...] = _adam_math(gv, w_ref[...], m_ref[...], v_ref[...])

    lay = pl.BlockSpec((None, rb, cols), lambda i: (l, i, 0))
    shp = jax.ShapeDtypeStruct((2, rows, cols), F32)
    prev = () if prev is None else tuple(prev)
    return pl.pallas_call(
        body, name=name, grid=(4,), in_specs=[pl.BlockSpec((rb, cols), lambda i: (i, 0)), lay, lay, lay] + [ANY] * len(prev),
        out_specs=[lay] * 4, out_shape=[shp] * 4,
        input_output_aliases={4 + j: j for j in range(len(prev))},
        compiler_params=_cp(("parallel",)),
    )(g, w, m, v, *prev)


def _rows128(a):
    return a.reshape(-1, 128)


def _pack(arrs, mult):
    parts = [_rows128(a) for a in arrs]
    rows = sum(q.shape[0] for q in parts)
    pad = -rows % mult
    if pad:
        parts.append(jnp.zeros((pad, 128), F32))
    return jnp.concatenate(parts, axis=0)


def _unpack(pack, shapes):
    out, o = [], 0
    for s in shapes:
        n = 1
        for e in s:
            n *= e
        out.append(pack[o:o + n // 128].reshape(s))
        o += n // 128
    return out


WEIGHTS = ['norm1_g', 'w_in', 'gmlp_ln_g', 'gmlp_ln_b', 'gmlp_w_s', 'gmlp_b_s', 'conv_w', 'conv_b', 'lru_w_r', 'lru_b_r', 'lru_w_i',
           'lru_b_i', 'lru_lambda', 'w_out', 'norm2_g', 'w_ffn_in', 'w_ffn_out', 'final_g']
BIG = ['w_in', 'w_out', 'w_ffn_in', 'w_ffn_out']
SMALL = [n for n in WEIGHTS if n not in BIG]
CHIP_SHARDED_SMALL = ['conv_w', 'lru_b_r', 'lru_b_i', 'lru_lambda']


def kernel(x, norm1_g, w_in, gmlp_ln_g, gmlp_ln_b, gmlp_w_s, gmlp_b_s, conv_w, conv_b, lru_w_r, lru_b_r, lru_w_i, lru_b_i, lru_lambda, w_out, norm2_g, w_ffn_in, w_ffn_out, final_g, loss_target, m_norm1_g, m_w_in, m_gmlp_ln_g, m_gmlp_ln_b, m_gmlp_w_s, m_gmlp_b_s, m_conv_w, m_conv_b, m_lru_w_r, m_lru_b_r, m_lru_w_i, m_lru_b_i, m_lru_lambda, m_w_out, m_norm2_g, m_w_ffn_in, m_w_ffn_out, m_final_g, v_norm1_g, v_w_in, v_gmlp_ln_g, v_gmlp_ln_b, v_gmlp_w_s, v_gmlp_b_s, v_conv_w, v_conv_b, v_lru_w_r, v_lru_b_r, v_lru_w_i, v_lru_b_i, v_lru_lambda, v_w_out, v_norm2_g, v_w_ffn_in, v_w_ffn_out, v_final_g):
    a = dict(locals())
    w = {n: a[n] for n in WEIGHTS}
    mom = {n: a["m_" + n] for n in WEIGHTS}
    var = {n: a["v_" + n] for n in WEIGHTS}
    _, _, c, chip = _place()
    c_arr, chip_arr = jnp.reshape(c, (1,)).astype(jnp.int32), jnp.reshape(chip, (1,)).astype(jnp.int32)
    place_arr = jnp.stack([chip, c]).astype(jnp.int32)

    first, rest = BIG[:1], BIG[1:]

    def as_weights(names, full):
        wb = {n: f.reshape(4, 2 * f.shape[2], f.shape[3]) for n, f in zip(names, full)}
        if "w_out" in wb:
            wb["w_out"] = wb["w_out"].reshape(D, D)
            wb["w_ffn_out"] = wb["w_ffn_out"].reshape(DFF, D)
        return wb

    def cast(n, l):
        return _cast_into(w[n], l, chip_arr, f"cast_{n}_{l}")

    def landed(fly, names, after, tag):
        return as_weights(names, _gather_pass_on(_gather_wait(fly[0], fly[1], fly[2], after, tag), tag))

    tiny = _pack([w[n] for n in CHIP_SHARDED_SMALL], 8)
    _, tiny_full = _gather_weights([], tiny)
    fly_in = _gather_start([cast("w_in", 0)], "in", after=(tiny_full,))
    fly0 = _gather_start([cast(n, 0) for n in rest], "0", after=(fly_in[3],))
    bufs1 = [cast(n, 1) for n in BIG]
    fly1 = _gather_start(bufs1, "1", after=(fly0[3],))
    p = {n: w[n] for n in SMALL}
    parts = [_unpack(tiny_full[k], [w[n].shape for n in CHIP_SHARDED_SMALL]) for k in range(4)]
    for i, n in enumerate(CHIP_SHARDED_SMALL):
        p[n] = jnp.concatenate([parts[k][i] for k in range(4)], axis=-1)

    passing = {}

    def pass_on_1(gu):
        bufs = _gather_wait(fly1[0], fly1[1], fly1[2], gu, "1")
        passing[1] = _exchange_start(bufs, [], _pass_on_copies, 3 * len(bufs), "gather_pass_on_start_1")
        return (passing[1][-1],)

    xa, saved0 = _forward_layer(0, x[0], p, landed(fly_in, first, fly1[3], "in"), after=(fly0[3], fly1[3]),
                                rest=lambda merged: landed(fly0, rest, merged, "0"), near_end=pass_on_1)
    send, recv, bufs1, _, _ = passing[1]
    xb, saved1 = _forward_layer(
        1, xa, p, as_weights(BIG, _exchange_wait(send, recv, bufs1, [], xa, _pass_on_copies, "gather_pass_on_wait_1")[0]))
    dxb, loss_v, dfg = _loss_head(xb, loss_target[0], p["final_g"][None])
    loss = lax.psum(loss_v[0, 0], ("x", "y", "c"))

    out, flying = {}, {}

    def halves(grads):
        return [g.reshape(4, 2, -1, g.shape[-1]) for g in grads]

    def sibling_start(grads, names, l, tag):
        gs = halves(grads)
        lands = [lax.empty((4,) + g.shape[2:], g.dtype) for g in gs]
        flying["s" + tag] = (names, l) + tuple(
            _exchange_start(gs, lands, _sibling_copies, 4 * len(gs), f"grads_to_sibling_start_{tag}"))
        return (flying["s" + tag][-1],)

    def chips_start(gs, from_sib, names, l, tag):
        cs = [_add_half(g, r, c_arr, f"add_half_{n}_{l}") for n, g, r in zip(names, gs, from_sib)]
        lands = [lax.empty((3,) + a.shape[1:], a.dtype) for a in cs]
        flying[tag] = (names, l) + tuple(_exchange_start(cs, lands, _chips_copies, 3 * len(cs), f"grads_to_chips_start_{tag}"))
        return (flying[tag][-1],)

    def sibling_finish(tag, after):
        names, l, send, recv, gs, lands, _ = flying["s" + tag]
        gs, from_sib = _exchange_wait(send, recv, gs, lands, after, _sibling_copies, f"grads_to_sibling_wait_{tag}")
        return chips_start(gs, from_sib, names, l, tag)

    def reduce_start(grads, names, l, tag):
        gs = halves(grads)
        return chips_start(gs, _to_sibling_halves(gs, tag), names, l, tag)

    def reduce_finish(tags, after):
        names, ts = [], []
        for tag in tags:
            names_t, l, send, recv, cs, lands, _ = flying[tag]
            cs, lands = _exchange_wait(send, recv, cs, lands, after, _chips_copies, f"grads_to_chips_wait_{tag}")
            ts += [_sum_chips(cc, r3, place_arr, f"sum_chips_{n}_{l}") for n, cc, r3 in zip(names_t, cs, lands)]
            names += names_t
        for n, j in zip(names, _join_halves(ts, tags[0])):
            out[n] = _adam_layer(j.reshape(w[n].shape[1:]), w[n], mom[n], var[n], l, out.get(n), f"adam_{n}_{l}")

    def late1(grads, _):
        return sibling_finish("1a", grads[0]) + sibling_start(grads, first, 1, "1b")

    def midway0(grads):
        reduce_finish(("1a", "1b"), grads[0])
        return sibling_start(grads, rest, 0, "0a")

    def stacked_small(small0):
        small = {k: jnp.stack([small0[k], small1[k]]) for k in LAYER_SMALL}
        return dict(small, final_g=dfg[0])

    def late0(grads, small0):
        toks = reduce_start(grads, first, 0, "0b")
        small = stacked_small(dict(small0, norm1_g=jnp.zeros((D,), F32)))
        csum = _small_chip_sum(_pack([small[n] for n in SMALL], 32))
        flying["small"] = _exchange_start([csum], [lax.empty((3,) + csum.shape, BF16)], _small_copies, 3, "small_to_chips_start")
        return toks + (flying["small"][-1],)

    dxa, big1, small1 = _backward_layer(1, dxb, saved1, midway=lambda grads: sibling_start(grads, rest, 1, "1a"), late=late1)
    dx, big0, small0 = _backward_layer(0, dxa, saved0, after=sibling_finish("1b", dxa), midway=midway0,
                                       midway2=lambda dws: sibling_finish("0a", dws), late=late0)
    reduce_finish(("0a", "0b"), dx)
    small = stacked_small(small0)

    full_shapes = [small[n].shape for n in SMALL]
    send, recv, csum, land, _ = flying["small"]
    csum, land = _exchange_wait(send, recv, csum, land, out[first[0]][0], _small_copies, "small_to_chips_wait")
    red = _unpack(_small_total(csum[0], land[0]), full_shapes)
    norm1_0 = _allreduce_small(_pack([small0["norm1_g"]], 32))[:D // 128].reshape(D)
    red[SMALL.index("norm1_g")] = red[SMALL.index("norm1_g")].at[0].set(norm1_0)
    g_small = []
    for n, g in zip(SMALL, red):
        if n in CHIP_SHARDED_SMALL:
            g = lax.dynamic_slice_in_dim(g, chip * w[n].shape[-1], w[n].shape[-1], axis=g.ndim - 1)
        g_small.append(g)
    shapes = [w[n].shape for n in SMALL]
    packs = [_pack(lst, 32) for lst in (g_small, [w[n] for n in SMALL], [mom[n] for n in SMALL], [var[n] for n in SMALL])]
    upd = [_unpack(u, shapes) for u in _adam(*packs, "adam_small")]
    for i, n in enumerate(SMALL):
        out[n] = [g_small[i], upd[0][i], upd[1][i], upd[2][i]]

    return (loss, dx[None]) + tuple(out[n][i] for i in range(4) for n in WEIGHTS)
```

```python
import functools

import jax
import jax.numpy as jnp
from jax import lax
from jax.experimental import pallas as pl
from jax.experimental.pallas import tpu as pltpu

F32 = jnp.float32
BF16 = jnp.bfloat16
MESH = pl.DeviceIdType.MESH

D = 1024
NH = 8
HD = 128
CHUNK = 128
N_IN_T = 12
DFF = 2816
DFF_SH = 1408
EPS = 1e-6
LRU_C = 8.0
ADAM_LR, ADAM_B1, ADAM_B2, ADAM_EPS, ADAM_WD, ADAM_STEP = 0.001, 0.9, 0.999, 1e-08, 0.01, 10

TM = 512
TM_BIG = 1024
RT = 128
PADR = 8
VMEM_LIMIT = 56 * 1024 * 1024


def _cp(sem=None, **kw):
    if sem is not None:
        kw["dimension_semantics"] = sem
    return pltpu.CompilerParams(vmem_limit_bytes=VMEM_LIMIT, **kw)


_GC = 0.7978845608028654


def _sigmoid(x):
    return 0.5 * jnp.tanh(0.5 * x) + 0.5


_GK = 0.044715


def _gelu(x):
    t = jnp.tanh(x * (_GC + (_GC * _GK) * (x * x)))
    return x * (0.5 + 0.5 * t)


def _gelu_and_grad(x):
    x2 = x * x
    t = jnp.tanh(x * (_GC + (_GC * _GK) * x2))
    h = 0.5 + 0.5 * t
    return x * h, h + x * (1.0 - t * t) * (0.5 * _GC + (1.5 * _GC * _GK) * x2)


def _softplus_neg(lam):
    y = jnp.exp(-jnp.abs(lam))
    u = 1.0 + y
    l1p = jnp.where(u == 1.0, y, jnp.log(u) * y / (u - 1.0))
    return jnp.maximum(-lam, 0.0) + l1p


def _dot(a, b):
    return jnp.dot(a, b, preferred_element_type=F32)


def _dot_nt(a, b):
    return lax.dot_general(a, b, (((1,), (1,)), ((), ())), preferred_element_type=F32)


def _dot_tn(a, b):
    return lax.dot_general(a, b, (((0,), (0,)), ((), ())), preferred_element_type=F32)


def _rms_hat(x):
    r = lax.rsqrt(jnp.mean(x * x, axis=-1, keepdims=True) + EPS)
    return x * r, r


def _rms_bwd(dh, x, g):
    xh, r = _rms_hat(x)
    dxh = dh * g
    dx = r * (dxh - xh * jnp.mean(dxh * xh, axis=-1, keepdims=True))
    return dx, jnp.sum(dh * xh, axis=0, keepdims=True)


def _norm_into(x_ref, g_ref, h_ref):
    xh, _ = _rms_hat(x_ref[...])
    h_ref[...] = (xh * g_ref[...]).astype(BF16)


def _in_tile(j):
    m, hf = j // 2, j % 2
    orig = jnp.where(m < 2, m, jnp.where(m == 2, 4, jnp.where(m < 5, m - 1, 5)))
    t = orig * 2 + hf
    return t // 3, t % 3


ANY = pl.BlockSpec(memory_space=pl.ANY)


def _mm_in(x, g, w_in, l, after=()):
    S = x.shape[0]
    tm = min(2 * TM_BIG, S)

    def body(x_ref, g_ref, w0_ref, w1_ref, *rest):
        o_ref, h_ref = rest[-2:]

        @pl.when(pl.program_id(1) == 0)
        def _():
            _norm_into(x_ref, g_ref, h_ref)
        rp = min(TM, tm)
        for r0 in range(0, tm, rp):
            hv = h_ref[r0:r0 + rp, :]
            o_ref[r0:r0 + rp, 0:512] = _dot(hv, w0_ref[...]).astype(BF16)
            o_ref[r0:r0 + rp, 512:1024] = _dot(hv, w1_ref[...]).astype(BF16)

    def w_tile(hf):
        def w_map(i, m):
            sh, tl = _in_tile(2 * m + hf)
            return (sh, 0, tl)
        return pl.BlockSpec((None, D, 512), w_map)

    return pl.pallas_call(
        body, name=f"mm_in_{l}", grid=(S // tm, 6),
        in_specs=[pl.BlockSpec((tm, D), lambda i, m: (i, 0)), pl.BlockSpec((1, D), lambda i, m: (0, 0)),
                  w_tile(0), w_tile(1)] + [ANY] * len(after),
        out_specs=[pl.BlockSpec((None, tm, D), lambda i, m: (m, i, 0)), pl.BlockSpec((tm, D), lambda i, m: (i, 0))],
        out_shape=[jax.ShapeDtypeStruct((6, S, D), BF16), jax.ShapeDtypeStruct((S, D), BF16)],
        compiler_params=_cp(("parallel", "arbitrary")),
    )(x, g, w_in, w_in, *after)


def _mm_res(a, w, res, l, name, after=()):
    S, K = a.shape

    tm = TM

    def body(a_ref, w_ref, r_ref, *rest):
        rest[-1][...] = r_ref[...] + _dot(a_ref[...], w_ref[...])

    return pl.pallas_call(
        body, name=f"{name}_{l}", grid=(S // tm,),
        in_specs=[pl.BlockSpec((tm, K), lambda i: (i, 0)), pl.BlockSpec((K, D), lambda i: (0, 0)),
                  pl.BlockSpec((tm, D), lambda i: (i, 0))] + [ANY] * len(after),
        out_specs=pl.BlockSpec((tm, D), lambda i: (i, 0)),
        out_shape=jax.ShapeDtypeStruct((S, D), F32),
        compiler_params=_cp(("parallel",)),
    )(a, w, res, *after)


def _mm_ffn_in(x, g, w_fi, l):
    S = x.shape[0]

    tm = min(TM_BIG, S)

    def body(x_ref, g_ref, w_ref, gu_ref, ff_ref, h_ref):
        @pl.when(pl.program_id(1) == 0)
        def _():
            _norm_into(x_ref, g_ref, h_ref)
        for r0 in range(0, tm, TM):
            rows = slice(r0, r0 + TM)
            hv = h_ref[rows, :]
            ga = _dot(hv, w_ref[0])
            gb = _dot(hv, w_ref[1])
            gu_ref[0, rows, :] = ga.astype(BF16)
            gu_ref[1, rows, :] = gb.astype(BF16)
            ff_ref[rows, :] = (ga * _sigmoid(ga) * gb).astype(BF16)

    gu, ff, h = pl.pallas_call(
        body, name=f"mm_ffn_in_{l}", grid=(S // tm, 2),
        in_specs=[pl.BlockSpec((tm, D), lambda i, s: (i, 0)), pl.BlockSpec((1, D), lambda i, s: (0, 0)),
                  pl.BlockSpec((2, None, D, DFF_SH), lambda i, s: (0, s, 0, 0))],
        out_specs=[pl.BlockSpec((2, None, tm, DFF_SH), lambda i, s: (0, s, i, 0)),
                   pl.BlockSpec((tm, DFF_SH), lambda i, s: (i, s)),
                   pl.BlockSpec((tm, D), lambda i, s: (i, 0))],
        out_shape=[jax.ShapeDtypeStruct((2, 2, S, DFF_SH), BF16), jax.ShapeDtypeStruct((S, DFF), BF16),
                   jax.ShapeDtypeStruct((S, D), BF16)],
        compiler_params=_cp(("parallel", "arbitrary")),
    )(x, g, w_fi.reshape(2, 2, D, DFF_SH))
    return gu.reshape(4, S, DFF_SH), ff, h


def _gmlp_fwd(z6, ws_b, bs_b, lg, lb):
    S = z6.shape[1]

    def body(z_ref, ws_ref, bs_ref, lg_ref, lb_ref, o_ref, mix):
        gv = _gelu(z_ref[1].astype(F32))
        xc = gv - jnp.mean(gv, axis=-1, keepdims=True)
        rs = lax.rsqrt(jnp.mean(xc * xc, axis=-1, keepdims=True) + EPS)
        vb = (xc * rs * lg_ref[...] + lb_ref[...]).astype(BF16)
        for gi in range(NH):
            cs = slice(gi * HD, (gi + 1) * HD)
            mix[:, cs] = _dot(ws_ref[gi], vb[:, cs])
        o_ref[...] = (_sigmoid(z_ref[2].astype(F32)) * _gelu(z_ref[0].astype(F32)) * (mix[...] + bs_ref[...])).astype(BF16)

    return pl.pallas_call(
        body, name="gmlp_fwd", grid=(S // CHUNK,),
        in_specs=[pl.BlockSpec((3, CHUNK, D), lambda i: (0, i, 0)), pl.BlockSpec((NH, CHUNK, CHUNK), lambda i: (0, 0, 0)),
                  pl.BlockSpec((CHUNK, D), lambda i: (0, 0)), pl.BlockSpec((1, D), lambda i: (0, 0)),
                  pl.BlockSpec((1, D), lambda i: (0, 0))],
        out_specs=pl.BlockSpec((CHUNK, D), lambda i: (i, 0)),
        out_shape=jax.ShapeDtypeStruct((S, D), BF16),
        scratch_shapes=[pltpu.VMEM((CHUNK, D), F32)],
        compiler_params=_cp(("parallel",)),
    )(z6, ws_b, bs_b, lg, lb)


def _row_iota():
    return lax.broadcasted_iota(jnp.int32, (RT, HD), 0)


SUB = 8
UNROLL = 4
GRAD_ROWS = 512


def _scan_up(a, b, carry):
    row = lax.broadcasted_iota(jnp.int32, (SUB, HD), 0)
    masks = [(d, row >= d) for d in (1, 2, 4)]
    c = jnp.broadcast_to(carry, (SUB, HD))
    hs = []
    for j in range(RT // SUB):
        aj, bj = a[SUB * j:SUB * (j + 1)], b[SUB * j:SUB * (j + 1)]
        for d, m in masks:
            bj = bj + aj * jnp.where(m, pltpu.roll(bj, d, 0), 0.0)
            aj = aj * jnp.where(m, pltpu.roll(aj, d, 0), 1.0)
        h = bj + aj * c
        hs.append(h)
        c = jnp.broadcast_to(h[SUB - 1:SUB, :], (SUB, HD))
    return jnp.concatenate(hs, axis=0), hs[-1][SUB - 1:SUB, :]


def _scan_down(a, b, carry):
    row = lax.broadcasted_iota(jnp.int32, (SUB, HD), 0)
    masks = [(d, row < SUB - d) for d in (1, 2, 4)]
    c = jnp.broadcast_to(carry, (SUB, HD))
    hs = []
    for j in reversed(range(RT // SUB)):
        aj, bj = a[SUB * j:SUB * (j + 1)], b[SUB * j:SUB * (j + 1)]
        for d, m in masks:
            bj = bj + aj * jnp.where(m, pltpu.roll(bj, SUB - d, 0), 0.0)
            aj = aj * jnp.where(m, pltpu.roll(aj, SUB - d, 0), 1.0)
        h = bj + aj * c
        hs.append(h)
        c = jnp.broadcast_to(h[0:1, :], (SUB, HD))
    return jnp.concatenate(hs[::-1], axis=0), hs[-1][0:1, :]


def _decay(r, sp_d):
    log_a = -LRU_C * r * sp_d
    a = jnp.exp(log_a)
    return a, jnp.sqrt(jnp.maximum(-jnp.tanh(log_a) * (a * a + 1.0), 0.0))


def _lru_gates(xc, d, wr_ref, br_ref, wi_ref, bi_ref, sp):
    xb = xc.astype(BF16)
    r = _sigmoid(_dot(xb, wr_ref[d]) + br_ref[d:d + 1, :])
    i = _sigmoid(_dot(xb, wi_ref[d]) + bi_ref[d:d + 1, :])
    a, mult = _decay(r, sp[d:d + 1, :])
    return r, i, a, mult


def _shifted(win, k):
    w = RT + 2 * PADR
    v = win if k == 0 else pltpu.roll(win, (-k) % w, 0)
    return v[PADR:PADR + RT]


def _conv_taps(win):
    return [_shifted(win, k) for k in (-1, 0, 1, 2)]


def _fill_padded(dst, src_ref, S):
    zeros = jnp.zeros((PADR, HD), F32)
    dst[0:PADR, :] = zeros
    dst[PADR + S:2 * PADR + S, :] = zeros

    def cp(i, c):
        t0 = pl.multiple_of(i * RT, RT)
        dst[pl.ds(t0 + PADR, RT), :] = src_ref[pl.ds(t0, RT), :].astype(F32)
        return c
    lax.fori_loop(0, S // RT, cp, 0)


def _conv_fwd_all(zxp, xc_s, cw_ref, cb_ref, S):
    def cv(i, c):
        t0 = pl.multiple_of(i * RT, RT)
        xm1, x0, xp1, xp2 = _conv_taps(zxp[pl.ds(t0, RT + 2 * PADR), :])
        xc_s[pl.ds(t0, RT), :] = (cb_ref[...] + xm1 * cw_ref[0:1, :] + x0 * cw_ref[1:2, :]
                                  + xp1 * cw_ref[2:3, :] + xp2 * cw_ref[3:4, :])
        return c
    lax.fori_loop(0, S // RT, cv, 0)


def _lru_specs(S):
    head = lambda h: (0, h)
    return [pl.BlockSpec((4, HD), head), pl.BlockSpec((1, HD), head),
            pl.BlockSpec((2, None, HD, HD), lambda h: (0, h, 0, 0)), pl.BlockSpec((2, HD), head),
            pl.BlockSpec((2, None, HD, HD), lambda h: (0, h, 0, 0)), pl.BlockSpec((2, HD), head),
            pl.BlockSpec((2, HD), head)]


def _lru_fwd(z6, ya, cw, cb, wr, br, wi, bi, lam):
    S = z6.shape[1]
    nt = S // RT

    def body(z_ref, ya_ref, cw_ref, cb_ref, wr_ref, br_ref, wi_ref, bi_ref, lam_ref, mg_ref, h0_ref, h1_ref, zxp, xc_s):
        sp = _softplus_neg(lam_ref[...])
        _fill_padded(zxp, z_ref.at[0], S)
        _conv_fwd_all(zxp, xc_s, cw_ref, cb_ref, S)

        def scans(i, carry):
            cu, cd = carry
            for u in range(UNROLL):
                j = i * UNROLL + u
                ru = pl.ds(pl.multiple_of(j * RT, RT), RT)
                rd = pl.ds(pl.multiple_of((nt - 1 - j) * RT, RT), RT)
                xu, xd = xc_s[ru, :], xc_s[rd, :]
                _, gi, a, mult = _lru_gates(xu, 0, wr_ref, br_ref, wi_ref, bi_ref, sp)
                hu, cu = _scan_up(a, mult * gi * xu, cu)
                h0_ref[ru, :] = hu
                _, gi, a, mult = _lru_gates(xd, 1, wr_ref, br_ref, wi_ref, bi_ref, sp)
                hd, cd = _scan_down(a, mult * gi * xd, cd)
                h1_ref[rd, :] = hd
            return cu, cd
        z1 = jnp.zeros((1, HD), F32)
        lax.fori_loop(0, nt // UNROLL, scans, (z1, z1))

        def merge(i, c):
            rows = pl.ds(pl.multiple_of(i * RT, RT), RT)
            yb = (h0_ref[rows, :] + h1_ref[rows, :]) * _gelu(z_ref[1, rows, :].astype(F32))
            mg_ref[rows, :] = (ya_ref[rows, :].astype(F32) + _sigmoid(z_ref[2, rows, :].astype(F32)) * yb).astype(BF16)
            return c
        lax.fori_loop(0, nt, merge, 0)

    col = pl.BlockSpec((S, HD), lambda h: (0, h))
    return pl.pallas_call(
        body, name="lru_fwd", grid=(NH,),
        in_specs=[pl.BlockSpec((3, S, HD), lambda h: (1, 0, h)), col] + _lru_specs(S),
        out_specs=[col, col, col],
        out_shape=[jax.ShapeDtypeStruct((S, D), BF16), jax.ShapeDtypeStruct((S, D), F32), jax.ShapeDtypeStruct((S, D), F32)],
        scratch_shapes=[pltpu.VMEM((S + 2 * PADR, HD), F32), pltpu.VMEM((S, HD), F32)],
        compiler_params=_cp(("parallel",)),
    )(z6, ya, cw, cb, wr, br, wi, bi, lam)


def _loss_head(x, tgt, g):
    S = x.shape[0]

    def body(x_ref, t_ref, g_ref, dx_ref, loss_ref, dg_ref):
        @pl.when(pl.program_id(0) == 0)
        def _():
            loss_ref[...] = jnp.zeros_like(loss_ref)
            dg_ref[...] = jnp.zeros_like(dg_ref)
        xv = x_ref[...]
        xh, _ = _rms_hat(xv)
        e = xh * g_ref[...] - t_ref[...]
        loss_ref[...] += jnp.sum(e * e) * (0.5 / D)
        dx, dgs = _rms_bwd(e * (1.0 / D), xv, g_ref[...])
        dx_ref[...] = dx
        dg_ref[...] += dgs

    return pl.pallas_call(
        body, name="loss_head", grid=(S // TM,),
        in_specs=[pl.BlockSpec((TM, D), lambda i: (i, 0)), pl.BlockSpec((TM, D), lambda i: (i, 0)),
                  pl.BlockSpec((1, D), lambda i: (0, 0))],
        out_specs=[pl.BlockSpec((TM, D), lambda i: (i, 0)), pl.BlockSpec((1, 128), lambda i: (0, 0)),
                   pl.BlockSpec((1, D), lambda i: (0, 0))],
        out_shape=[jax.ShapeDtypeStruct((S, D), F32), jax.ShapeDtypeStruct((1, 128), F32), jax.ShapeDtypeStruct((1, D), F32)],
        compiler_params=_cp(("arbitrary",)),
    )(x, tgt, g)


def _bwd_ffn_out(dx, w_fo, gu, l, after=()):
    S = dx.shape[0]

    tm = min(TM_BIG, S)

    def body(dx_ref, w_ref, gu_ref, *rest):
        o_ref = rest[-1]
        for r0 in range(0, tm, TM):
            rows = slice(r0, r0 + TM)
            d = _dot_nt(dx_ref[rows, :].astype(BF16), w_ref[...])
            ga, gb = gu_ref[0, rows, :].astype(F32), gu_ref[1, rows, :].astype(F32)
            sg = _sigmoid(ga)
            o_ref[0, rows, :] = (d * gb * sg * (1.0 + ga * (1.0 - sg))).astype(BF16)
            o_ref[1, rows, :] = (d * ga * sg).astype(BF16)

    pair = pl.BlockSpec((2, None, tm, DFF_SH), lambda i, s: (0, s, i, 0))
    dgu = pl.pallas_call(
        body, name=f"bwd_ffn_out_{l}", grid=(S // tm, 2),
        in_specs=[pl.BlockSpec((tm, D), lambda i, s: (i, 0)), pl.BlockSpec((DFF_SH, D), lambda i, s: (s, 0)), pair]
        + [ANY] * len(after),
        out_specs=pair,
        out_shape=jax.ShapeDtypeStruct((2, 2, S, DFF_SH), BF16),
        compiler_params=_cp(("parallel", "arbitrary")),
    )(dx, w_fo, gu.reshape(2, 2, S, DFF_SH), *after)
    return dgu.reshape(4, S, DFF_SH)


def _mm_tn(a, b, m_blk, tk, name):
    S, M = a.shape

    def body(a_ref, b_ref, o_ref):
        @pl.when(pl.program_id(1) == 0)
        def _():
            o_ref[...] = jnp.zeros_like(o_ref)
        o_ref[...] += _dot_tn(a_ref[...], b_ref[...].astype(BF16))

    return pl.pallas_call(
        body, name=name, grid=(M // m_blk, S // tk),
        in_specs=[pl.BlockSpec((tk, m_blk), lambda m, k: (k, m)), pl.BlockSpec((tk, D), lambda m, k: (k, 0))],
        out_specs=pl.BlockSpec((m_blk, D), lambda m, k: (m, 0)),
        out_shape=jax.ShapeDtypeStruct((M, D), F32),
        compiler_params=_cp(("parallel", "arbitrary")),
    )(a, b)


def _mm_nt_rms_bwd(a, a_specs, w, w_specs, nk, tm, x, g, dres, name, after=(), row_part=None):
    S = x.shape[0]
    sub = len(a_specs)
    row_part = tm if row_part is None else min(row_part, tm)

    def body(*refs):
        a_refs, w_refs = refs[:sub], refs[sub:2 * sub]
        x_ref, g_ref, r_ref = refs[2 * sub:2 * sub + 3]
        dx_ref, dg_ref, acc = refs[-3:]
        i, k = pl.program_id(0), pl.program_id(1)
        @pl.when(k == 0)
        def _():
            acc[...] = jnp.zeros_like(acc)
        for r0 in range(0, tm, row_part):
            rows = slice(r0, r0 + row_part)
            for j in range(sub):
                acc[rows, :] += _dot_nt(a_refs[j][rows, :], w_refs[j][...])

        @pl.when(jnp.logical_and(i == 0, k == 0))
        def _():
            dg_ref[...] = jnp.zeros_like(dg_ref)

        @pl.when(k == nk - 1)
        def _():
            dx, dgs = _rms_bwd(acc[...], x_ref[...], g_ref[...])
            dx_ref[...] = r_ref[...] + dx
            dg_ref[...] += dgs

    row = pl.BlockSpec((tm, D), lambda i, k: (i, 0))
    vec = pl.BlockSpec((1, D), lambda i, k: (0, 0))
    return pl.pallas_call(
        body, name=name, grid=(S // tm, nk),
        in_specs=list(a_specs) + list(w_specs) + [row, vec, row] + [ANY] * len(after),
        out_specs=[row, vec],
        out_shape=[jax.ShapeDtypeStruct((S, D), F32), jax.ShapeDtypeStruct((1, D), F32)],
        scratch_shapes=[pltpu.VMEM((tm, D), F32)],
        compiler_params=_cp(("arbitrary", "arbitrary")),
    )(*[a] * sub, *[w] * sub, x, g, dres, *after)


def _dw_ffn_in(h, dgu, l):
    S = h.shape[0]

    def body(h_ref, b_ref, o_ref):
        @pl.when(pl.program_id(1) == 0)
        def _():
            o_ref[...] = jnp.zeros_like(o_ref)
        o_ref[...] += _dot_tn(h_ref[...], b_ref[...])

    tk = min(2 * TM_BIG, S)
    return pl.pallas_call(
        body, name=f"dw_ffn_in_{l}", grid=(4, S // tk),
        in_specs=[pl.BlockSpec((tk, D), lambda j, k: (k, 0)), pl.BlockSpec((None, tk, DFF_SH), lambda j, k: (j, k, 0))],
        out_specs=pl.BlockSpec((None, D, DFF_SH), lambda j, k: (j, 0, 0)),
        out_shape=jax.ShapeDtypeStruct((4, D, DFF_SH), F32),
        compiler_params=_cp(("parallel", "arbitrary")),
    )(h, dgu)


_HALF_COMPS = ((0, 1, 3), (4, 2, 5))


def _dw_in(h, dz6, l):
    S = h.shape[0]

    def body(h_ref, d0_ref, d1_ref, d2_ref, o_ref):
        @pl.when(pl.program_id(1) == 0)
        def _():
            o_ref[...] = jnp.zeros_like(o_ref)
        hv = h_ref[...]
        for q, d_ref in enumerate((d0_ref, d1_ref, d2_ref)):
            for hf in range(2):
                col = 1024 * q + 512 * hf
                o_ref[col // 1536, :, col % 1536:col % 1536 + 512] += _dot_tn(hv, d_ref[:, 512 * hf:512 * (hf + 1)])

    tk = min(TM_BIG, S)

    def comp(q):
        return pl.BlockSpec((None, tk, D), lambda p, k: (jnp.where(p == 0, _HALF_COMPS[0][q], _HALF_COMPS[1][q]), k, 0))

    return pl.pallas_call(
        body, name=f"dw_in_{l}", grid=(2, S // tk),
        in_specs=[pl.BlockSpec((tk, D), lambda p, k: (k, 0)), comp(0), comp(1), comp(2)],
        out_specs=pl.BlockSpec((2, D, 1536), lambda p, k: (p, 0, 0)),
        out_shape=jax.ShapeDtypeStruct((4, D, 1536), F32),
        compiler_params=_cp(("parallel", "arbitrary")),
    )(h, dz6, dz6, dz6)


def _bwd_out(dx, w_o, merged, l):
    S = dx.shape[0]

    def body(dx_ref, w_ref, m_ref, dm_ref, dw_ref):
        @pl.when(pl.program_id(0) == 0)
        def _():
            dw_ref[...] = jnp.zeros_like(dw_ref)
        dxb = dx_ref[...].astype(BF16)
        dm_ref[...] = _dot_nt(dxb, w_ref[...]).astype(BF16)
        dw_ref[...] += _dot_tn(m_ref[...], dxb)

    tm = TM
    row = pl.BlockSpec((tm, D), lambda i: (i, 0))
    return pl.pallas_call(
        body, name=f"bwd_out_{l}", grid=(S // tm,),
        in_specs=[row, pl.BlockSpec((D, D), lambda i: (0, 0)), row],
        out_specs=[row, pl.BlockSpec((D, D), lambda i: (0, 0))],
        out_shape=[jax.ShapeDtypeStruct((S, D), BF16), jax.ShapeDtypeStruct((D, D), F32)],
        compiler_params=_cp(("arbitrary",)),
    )(dx, w_o, merged)


def _gmlp_bwd(dm, z6, ws_b, wst_b, bs_b, lg, lb, after=()):
    S = z6.shape[1]

    def body(dm_ref, z_ref, ws_ref, wst_ref, bs_ref, lg_ref, lb_ref, *rest):
        dz_ref, dws_ref, dbs_ref, dlg_ref, dlb_ref, mix, dv = rest[-7:]

        @pl.when(pl.program_id(0) == 0)
        def _():
            dws_ref[...] = jnp.zeros_like(dws_ref)
            dbs_ref[...] = jnp.zeros_like(dbs_ref)
            dlg_ref[...] = jnp.zeros_like(dlg_ref)
            dlb_ref[...] = jnp.zeros_like(dlb_ref)
        gv, dgelu_v = _gelu_and_grad(z_ref[1].astype(F32))
        xc = gv - jnp.mean(gv, axis=-1, keepdims=True)
        rs = lax.rsqrt(jnp.mean(xc * xc, axis=-1, keepdims=True) + EPS)
        vh = xc * rs
        vb = (vh * lg_ref[...] + lb_ref[...]).astype(BF16)
        for gi in range(NH):
            cs = slice(gi * HD, (gi + 1) * HD)
            mix[:, cs] = _dot(ws_ref[gi], vb[:, cs])
        u, dgelu_u = _gelu_and_grad(z_ref[0].astype(F32))
        sa = _sigmoid(z_ref[2].astype(F32))
        mixed = mix[...] + bs_ref[...]
        dyg = dm_ref[...].astype(F32)
        dz_ref[2] = (dyg * u * mixed * sa * (1.0 - sa)).astype(BF16)
        dya = dyg * sa
        dz_ref[0] = (dya * mixed * dgelu_u).astype(BF16)
        dmix = dya * u
        dmb = dmix.astype(BF16)
        for gi in range(NH):
            cs = slice(gi * HD, (gi + 1) * HD)
            dv[:, cs] = _dot(wst_ref[gi], dmb[:, cs])
            dws_ref[gi] += _dot_nt(dmb[:, cs], vb[:, cs])
            dbs_ref[gi] += jnp.broadcast_to(jnp.sum(dmix[:, cs], axis=1, keepdims=True), (CHUNK, HD))
        dvv = dv[...]
        dlg_ref[...] += jnp.sum(dvv * vh, axis=0, keepdims=True)
        dlb_ref[...] += jnp.sum(dvv, axis=0, keepdims=True)
        dvh = dvv * lg_ref[...]
        dgv = rs * (dvh - jnp.mean(dvh, axis=-1, keepdims=True) - vh * jnp.mean(dvh * vh, axis=-1, keepdims=True))
        dz_ref[1] = (dgv * dgelu_v).astype(BF16)

    vec = pl.BlockSpec((1, D), lambda i: (0, 0))
    mat = pl.BlockSpec((NH, CHUNK, CHUNK), lambda i: (0, 0, 0))
    return pl.pallas_call(
        body, name="gmlp_bwd", grid=(S // CHUNK,),
        in_specs=[pl.BlockSpec((CHUNK, D), lambda i: (i, 0)), pl.BlockSpec((3, CHUNK, D), lambda i: (0, i, 0)), mat, mat,
                  pl.BlockSpec((CHUNK, D), lambda i: (0, 0)), vec, vec] + [ANY] * len(after),
        out_specs=[pl.BlockSpec((3, CHUNK, D), lambda i: (0, i, 0)), mat, mat, vec, vec],
        out_shape=[jax.ShapeDtypeStruct((6, S, D), BF16), jax.ShapeDtypeStruct((NH, CHUNK, CHUNK), F32),
                   jax.ShapeDtypeStruct((NH, CHUNK, HD), F32), jax.ShapeDtypeStruct((1, D), F32), jax.ShapeDtypeStruct((1, D), F32)],
        scratch_shapes=[pltpu.VMEM((CHUNK, D), F32), pltpu.VMEM((CHUNK, D), F32)],
        compiler_params=_cp(("arbitrary",)),
    )(dm, z6, ws_b, wst_b, bs_b, lg, lb, *after)


def _lru_bwd(dz6, dm, z6, h0, h1, cw, cb, wr, br, wi, bi, lam, after=()):
    S = z6.shape[1]
    nt = S // RT

    def body(dz_in, dm_ref, z_ref, h0_ref, h1_ref, cw_ref, cb_ref, wr_ref, br_ref, wi_ref, bi_ref, lam_ref, *rest):
        dz_ref, dcw_ref, dcb_ref, dwr_ref, dbr_ref, dwi_ref, dbi_ref, dlam_ref, zxp, xc_s, dhs_s, dxcp, r_s, lam_s = rest[-14:]
        del dz_in
        lam = lam_ref[...]
        sp = _softplus_neg(lam)
        row = _row_iota()
        _fill_padded(zxp, z_ref.at[0], S)
        _conv_fwd_all(zxp, xc_s, cw_ref, cb_ref, S)
        zeros = jnp.zeros((PADR, HD), F32)
        dxcp[0:PADR, :] = zeros
        dxcp[PADR + S:2 * PADR + S, :] = zeros
        dwr_ref[...] = jnp.zeros_like(dwr_ref)
        dwi_ref[...] = jnp.zeros_like(dwi_ref)

        def pre(i, c):
            rows = pl.ds(pl.multiple_of(i * RT, RT), RT)
            hs = h0_ref[rows, :] + h1_ref[rows, :]
            dmv = dm_ref[rows, :].astype(F32)
            sb = _sigmoid(z_ref[2, rows, :].astype(F32))
            gg, dgg = _gelu_and_grad(z_ref[1, rows, :].astype(F32))
            dz_ref[2, rows, :] = (dmv * hs * gg * sb * (1.0 - sb)).astype(BF16)
            dyb = dmv * sb
            dz_ref[1, rows, :] = (dyb * hs * dgg).astype(BF16)
            dhs_s[rows, :] = dyb * gg
            return c
        lax.fori_loop(0, nt, pre, 0)

        def gate_bwd(d, gates, lamv, da, xc):
            r, gi, a, mult = gates
            dmult = lamv * gi * xc
            dgi = lamv * mult * xc
            dlog = (da - dmult * a / mult) * a
            dpr = (dlog * (-LRU_C) * sp[d:d + 1, :]) * r * (1.0 - r)
            dpi = dgi * gi * (1.0 - gi)
            xb, dprb, dpib = xc.astype(BF16), dpr.astype(BF16), dpi.astype(BF16)
            dwr_ref[d] += _dot_tn(xb, dprb)
            dwi_ref[d] += _dot_tn(xb, dpib)
            dxc = lamv * mult * gi + _dot_nt(dprb, wr_ref[d]) + _dot_nt(dpib, wi_ref[d])
            return dxc, (jnp.sum(dlog * r, axis=0, keepdims=True) * (-LRU_C), jnp.sum(dpr, axis=0, keepdims=True),
                         jnp.sum(dpi, axis=0, keepdims=True))

        def rgates(i, c):
            for u in range(UNROLL):
                rows = pl.ds(pl.multiple_of((i * UNROLL + u) * RT, RT), RT)
                xb = xc_s[rows, :].astype(BF16)
                for d in range(2):
                    r_s[d, rows, :] = _sigmoid(_dot(xb, wr_ref[d]) + br_ref[d:d + 1, :])
            return c
        lax.fori_loop(0, nt // UNROLL, rgates, 0)

        def chains(i, carry):
            qn, qp = carry
            for u in range(UNROLL):
                j = i * UNROLL + u
                rd = pl.ds(pl.multiple_of((nt - 1 - j) * RT, RT), RT)
                a, dhs = _decay(r_s[0, rd, :], sp[0:1, :])[0], dhs_s[rd, :]
                q, q_first = _scan_down(a, a * dhs, qn)
                lam_s[0, rd, :] = dhs + jnp.where(row == RT - 1, qn, pltpu.roll(q, RT - 1, 0))
                qn = q_first
                ru = pl.ds(pl.multiple_of(j * RT, RT), RT)
                a, dhs = _decay(r_s[1, ru, :], sp[1:2, :])[0], dhs_s[ru, :]
                q, q_last = _scan_up(a, a * dhs, qp)
                lam_s[1, ru, :] = dhs + jnp.where(row == 0, qp, pltpu.roll(q, 1, 0))
                qp = q_last
            return qn, qp

        z1 = jnp.zeros((1, HD), F32)
        lax.fori_loop(0, nt // UNROLL, chains, (z1, z1))

        ct = min(GRAD_ROWS, S)
        crow = lax.broadcasted_iota(jnp.int32, (ct, HD), 0)

        def tile_grads(i, acc):
            t0 = pl.multiple_of(i * ct, ct)
            rows = pl.ds(t0, ct)
            xc = xc_s[rows, :]
            xb = xc.astype(BF16)
            tp = pl.multiple_of(jnp.maximum(t0 - PADR, 0), PADR)
            prev = jnp.where(t0 > 0, h0_ref[pl.ds(tp, PADR), :][PADR - 1:PADR, :], 0.0)
            tn = pl.multiple_of(jnp.minimum(t0 + ct, S - PADR), PADR)
            nxt = jnp.where(t0 + ct < S, h1_ref[pl.ds(tn, PADR), :][0:1, :], 0.0)
            hside = (jnp.where(crow == 0, prev, pltpu.roll(h0_ref[rows, :], 1, 0)),
                     jnp.where(crow == ct - 1, nxt, pltpu.roll(h1_ref[rows, :], ct - 1, 0)))
            dxc, sums = 0.0, ()
            for d in range(2):
                r = r_s[d, rows, :]
                gi = _sigmoid(_dot(xb, wi_ref[d]) + bi_ref[d:d + 1, :])
                a, mult = _decay(r, sp[d:d + 1, :])
                lamv = lam_s[d, rows, :]
                dxc_d, s_d = gate_bwd(d, (r, gi, a, mult), lamv, lamv * hside[d], xc)
                dxc = dxc + dxc_d
                sums = sums + s_d
            dxcp[pl.ds(t0 + PADR, ct), :] = dxc
            return tuple(x + y for x, y in zip(acc, sums))

        s_sp0, s_br0, s_bi0, s_sp1, s_br1, s_bi1 = lax.fori_loop(0, S // ct, tile_grads, (z1,) * 6)

        dsp = jnp.concatenate([s_sp0, s_sp1], axis=0)
        dlam_ref[...] = -dsp * _sigmoid(-lam)
        dbr_ref[...] = jnp.concatenate([s_br0, s_br1], axis=0)
        dbi_ref[...] = jnp.concatenate([s_bi0, s_bi1], axis=0)

        def conv_bwd(i, carry):
            c0, c1, c2, c3, cb_ = carry
            t0 = pl.multiple_of(i * RT, RT)
            dwin = dxcp[pl.ds(t0, RT + 2 * PADR), :]
            d0 = _shifted(dwin, 0)
            dz_ref[0, pl.ds(t0, RT), :] = (_shifted(dwin, 1) * cw_ref[0:1, :] + d0 * cw_ref[1:2, :]
                                           + _shifted(dwin, -1) * cw_ref[2:3, :] + _shifted(dwin, -2) * cw_ref[3:4, :]).astype(BF16)
            xm1, x0, xp1, xp2 = _conv_taps(zxp[pl.ds(t0, RT + 2 * PADR), :])
            sm = lambda v: jnp.sum(v, axis=0, keepdims=True)
            return c0 + sm(d0 * xm1), c1 + sm(d0 * x0), c2 + sm(d0 * xp1), c3 + sm(d0 * xp2), cb_ + sm(d0)

        c0, c1, c2, c3, cb_ = lax.fori_loop(0, nt, conv_bwd, (z1, z1, z1, z1, z1))
        dcw_ref[...] = jnp.concatenate([c0, c1, c2, c3], axis=0)
        dcb_ref[...] = cb_

    col = pl.BlockSpec((S, HD), lambda h: (0, h))
    head = lambda h: (0, h)
    wspec = pl.BlockSpec((2, None, HD, HD), lambda h: (0, h, 0, 0))
    return pl.pallas_call(
        body, name="lru_bwd", grid=(NH,),
        in_specs=[pl.BlockSpec(memory_space=pl.ANY), col, pl.BlockSpec((3, S, HD), lambda h: (1, 0, h)), col, col] + _lru_specs(S)
        + [ANY] * len(after),
        out_specs=[pl.BlockSpec((3, S, HD), lambda h: (1, 0, h)), pl.BlockSpec((4, HD), head), pl.BlockSpec((1, HD), head),
                   wspec, pl.BlockSpec((2, HD), head), wspec, pl.BlockSpec((2, HD), head), pl.BlockSpec((2, HD), head)],
        out_shape=[jax.ShapeDtypeStruct((6, S, D), BF16), jax.ShapeDtypeStruct((4, D), F32), jax.ShapeDtypeStruct((1, D), F32),
                   jax.ShapeDtypeStruct((2, NH, HD, HD), F32), jax.ShapeDtypeStruct((2, D), F32),
                   jax.ShapeDtypeStruct((2, NH, HD, HD), F32), jax.ShapeDtypeStruct((2, D), F32), jax.ShapeDtypeStruct((2, D), F32)],
        scratch_shapes=[pltpu.VMEM((S + 2 * PADR, HD), F32), pltpu.VMEM((S, HD), F32), pltpu.VMEM((S, HD), F32),
                        pltpu.VMEM((S + 2 * PADR, HD), F32), pltpu.VMEM((2, S, HD), F32), pltpu.VMEM((2, S, HD), F32)],
        input_output_aliases={0: 0},
        compiler_params=_cp(("parallel",)),
    )(dz6, dm, z6, h0, h1, cw, cb, wr, br, wi, bi, lam, *after)


LAYER_SMALL = ("norm1_g", "gmlp_ln_g", "gmlp_ln_b", "gmlp_w_s", "gmlp_b_s", "conv_w", "conv_b",
               "lru_w_r", "lru_b_r", "lru_w_i", "lru_b_i", "lru_lambda", "norm2_g")


def _layer_operands(l, p):
    ws_b = p["gmlp_w_s"][l].astype(BF16)
    tm = dict(ws_b=ws_b, wst_b=jnp.swapaxes(ws_b, 1, 2), bs_b=jnp.repeat(p["gmlp_b_s"][l].T, HD, axis=1),
              lg=p["gmlp_ln_g"][l][None], lb=p["gmlp_ln_b"][l][None])
    lru = (p["conv_w"][l], p["conv_b"][l][None], p["lru_w_r"][l].astype(BF16), p["lru_b_r"][l],
           p["lru_w_i"][l].astype(BF16), p["lru_b_i"][l], p["lru_lambda"][l])
    return (p["norm1_g"][l][None], p["norm2_g"][l][None]), tm, lru


def _forward_layer(l, x, p, wb, after=(), rest=None, near_end=None, operands=None):
    (g1, g2), tm, lru = _layer_operands(l, p) if operands is None else operands
    z6, hn1 = _mm_in(x, g1, wb["w_in"], l, after)
    ya = _gmlp_fwd(z6, tm["ws_b"], tm["bs_b"], tm["lg"], tm["lb"])
    merged, h0, h1 = _lru_fwd(z6, ya, *lru)
    if rest is not None:
        wb = dict(wb, **rest(merged))
    x1 = _mm_res(merged, wb["w_out"], x, l, "mm_out")
    gu, ff, hn2 = _mm_ffn_in(x1, g2, wb["w_ffn_in"], l)
    x2 = _mm_res(ff, wb["w_ffn_out"], x1, l, "mm_ffn_out", () if near_end is None else tuple(near_end(gu)))
    return x2, dict(x=x, z6=z6, h0=h0, h1=h1, merged=merged, x1=x1, gu=gu, ff=ff, g1=g1, g2=g2, tm=tm, lru=lru,
                    hn1=hn1, hn2=hn2, wb=wb)


def _backward_layer(l, dx, s, after=(), midway=None, midway2=None, late=None):
    S = dx.shape[0]
    tm, wb = s["tm"], s["wb"]
    g2 = s["g2"]
    dgu = _bwd_ffn_out(dx, wb["w_ffn_out"], s["gu"], l, after)
    tmb = min(TM_BIG, S)
    dwfo = _mm_tn(s["ff"], dx, DFF_SH, tmb, f"dw_ffn_out_{l}")
    dx1, dg2 = _mm_nt_rms_bwd(
        dgu, [pl.BlockSpec((None, tmb, DFF_SH), lambda i, k: (k, i, 0))],
        wb["w_ffn_in"], [pl.BlockSpec((None, D, DFF_SH), lambda i, k: (k, 0, 0))],
        4, tmb, s["x1"], g2, dx, f"bwd_ffn_in_{l}")
    dwfi = _dw_ffn_in(s["hn2"], dgu, l)
    dmg, dwo = _bwd_out(dx1, wb["w_out"], s["merged"], l)
    mid = () if midway is None else tuple(midway([dwo, dwfi, dwfo]))
    dz6, dws, dbs, dlg, dlb = _gmlp_bwd(dmg, s["z6"], tm["ws_b"], tm["wst_b"], tm["bs_b"], tm["lg"], tm["lb"], mid)
    mid2 = () if midway2 is None else tuple(midway2(dws))
    dz6, dcw, dcb, dwr, dbr, dwi, dbi, dlam = _lru_bwd(dz6, dmg, s["z6"], s["h0"], s["h1"], *s["lru"], after=mid2)

    sub = 3

    def dz_tile(j):
        return pl.BlockSpec((None, tmb, 512), lambda i, k: ((sub * k + j) // 2, i, (sub * k + j) % 2))

    def w_tile(j):
        def w_map(i, k):
            sh, tl = _in_tile(sub * k + j)
            return (sh, 0, tl)
        return pl.BlockSpec((None, D, 512), w_map)

    dwin = _dw_in(s["hn1"], dz6, l)
    small = dict(gmlp_ln_g=dlg[0], gmlp_ln_b=dlb[0], gmlp_w_s=dws, gmlp_b_s=dbs[:, :, 0], conv_w=dcw, conv_b=dcb[0],
                 lru_w_r=dwr, lru_b_r=dbr, lru_w_i=dwi, lru_b_i=dbi, lru_lambda=dlam, norm2_g=dg2[0])
    tail = () if late is None else tuple(late([dwin], small))
    dx0, dg1 = _mm_nt_rms_bwd(
        dz6, [dz_tile(j) for j in range(sub)], wb["w_in"], [w_tile(j) for j in range(sub)],
        N_IN_T // sub, tmb, s["x"], s["g1"], dx1, f"bwd_in_{l}", tail)
    return dx0, [dwin, dwo, dwfi, dwfo], dict(small, norm1_g=dg1[0])


def _local_step(x, tgt, p, wbs):
    saved = []
    for l in range(2):
        x, s = _forward_layer(l, x, p, wbs[l])
        saved.append(s)
    dx, loss_v, dfg = _loss_head(x, tgt, p["final_g"][None])
    big, smalls = [None, None], [None, None]
    for l in (1, 0):
        dx, big[l], smalls[l] = _backward_layer(l, dx, saved[l])
    small = {k: jnp.stack([smalls[0][k], smalls[1][k]]) for k in LAYER_SMALL}
    small["final_g"] = dfg[0]
    return loss_v, dx, big, small


def _place():
    x, y, c = lax.axis_index("x"), lax.axis_index("y"), lax.axis_index("c")
    return x, y, c, 2 * x + y


def _chip_at(x, y, d):
    px = 1 - x if d & 2 else x
    py = 1 - y if d & 1 else y
    return px, py, 2 * px + py


HBM = pl.BlockSpec(memory_space=pltpu.HBM)
SEM = pl.BlockSpec(memory_space=pltpu.SEMAPHORE)
DATAFLOW = pltpu.SideEffectType.DATAFLOW_SIDE_EFFECTING


def _in_hbm(a):
    return pltpu.with_memory_space_constraint(a, pltpu.HBM)


def _cast_into(wfs, l, chip_arr, name):
    n = len(wfs)

    def body(ch_ref, *refs):
        for w_ref, o_ref in zip(refs[:n], refs[n:]):
            o_ref[...] = w_ref[...].astype(BF16)

    halves = [(wf.shape[1] // 2, wf.shape[2]) for wf in wfs]
    return pl.pallas_call(
        body, name=name, out_shape=[jax.ShapeDtypeStruct((4, 2, rh, cols), BF16) for rh, cols in halves],
        grid_spec=pltpu.PrefetchScalarGridSpec(
            num_scalar_prefetch=1, grid=(2,),
            in_specs=[pl.BlockSpec((None, None, rh, cols), lambda h, ch: (l, h, 0, 0)) for rh, cols in halves],
            out_specs=[pl.BlockSpec((None, None, rh, cols), lambda h, ch: (ch[0], h, 0, 0)) for rh, cols in halves]),
        compiler_params=_cp(("parallel",)),
    )(chip_arr, *[wf.reshape(2, 2, rh, cols) for wf, (rh, cols) in zip(wfs, halves)])


def _half_block(ref, chip, half, to, send_sem, recv_sem):
    blk = ref.at[chip, half]
    return pltpu.make_async_remote_copy(src_ref=blk, dst_ref=blk, send_sem=send_sem, recv_sem=recv_sem,
                                        device_id=to, device_id_type=MESH)


def _gather_weights(bufs, tiny):
    nt = len(bufs)
    n_ici = max(nt * 3, 1)

    def body(*refs):
        tiny_ref = refs[nt]
        o_refs, tiny_o = refs[nt + 1:2 * nt + 1], refs[2 * nt + 1]
        send, recv, fsend, frecv, tsend, trecv, lsem = refs[2 * nt + 2:]
        x, y, c, chip = _place()
        local = pltpu.make_async_copy(tiny_ref, tiny_o.at[chip], lsem)
        local.start()

        def tin(d, origin_chip, to):
            return pltpu.make_async_remote_copy(
                src_ref=tiny_ref, dst_ref=tiny_o.at[origin_chip], send_sem=tsend.at[d - 1], recv_sem=trecv.at[d - 1],
                device_id=to, device_id_type=MESH)

        sends = []
        for t in range(nt):
            for d in (1, 2, 3):
                px, py, _ = _chip_at(x, y, d)
                sends.append(_half_block(o_refs[t], chip, c, (px, py, c), send.at[3 * t + d - 1], recv.at[3 * t + d - 1]))
        for d in (1, 2, 3):
            px, py, _ = _chip_at(x, y, d)
            sends.append(tin(d, chip, (px, py, c)))
        for cp in sends:
            cp.start()
        passed = []
        for t in range(nt):
            for d in (1, 2, 3):
                k = 3 * t + d - 1
                _, _, pchip = _chip_at(x, y, d)
                _half_block(o_refs[t], pchip, c, (x, y, c), send.at[k], recv.at[k]).wait_recv()
                f = _half_block(o_refs[t], pchip, c, (x, y, 1 - c), fsend.at[k], frecv.at[k])
                f.start()
                passed.append(f)
        for t in range(nt):
            for d in (1, 2, 3):
                k = 3 * t + d - 1
                _, _, pchip = _chip_at(x, y, d)
                _half_block(o_refs[t], pchip, 1 - c, (x, y, 1 - c), fsend.at[k], frecv.at[k]).wait_recv()
        for d in (1, 2, 3):
            _, _, pchip = _chip_at(x, y, d)
            tin(d, pchip, (x, y, c)).wait_recv()
        for cp in sends + passed:
            cp.wait_send()
        local.wait()

    out_shape = [jax.ShapeDtypeStruct(b.shape, b.dtype) for b in bufs]
    out_shape.append(jax.ShapeDtypeStruct((4,) + tiny.shape, tiny.dtype))
    outs = pl.pallas_call(
        body, name="gather_weights_0", out_shape=out_shape,
        in_specs=[ANY] * (nt + 1), out_specs=[ANY] * (nt + 1),
        scratch_shapes=[pltpu.SemaphoreType.DMA((n_ici,)), pltpu.SemaphoreType.DMA((n_ici,)),
                        pltpu.SemaphoreType.DMA((n_ici,)), pltpu.SemaphoreType.DMA((n_ici,)),
                        pltpu.SemaphoreType.DMA((3,)), pltpu.SemaphoreType.DMA((3,)), pltpu.SemaphoreType.DMA],
        input_output_aliases={t: t for t in range(nt)},
        compiler_params=_cp(has_side_effects=True),
    )(*bufs, tiny)
    return outs[:nt], outs[nt]


def _gather_start(bufs, tag, after=()):
    nt, na = len(bufs), len(after)

    def body(*refs):
        b_refs = refs[:nt]
        send, recv = refs[nt + na], refs[nt + na + 1]
        token = refs[2 * nt + na + 2]
        x, y, c, chip = _place()
        for t in range(nt):
            for d in (1, 2, 3):
                px, py, _ = _chip_at(x, y, d)
                _half_block(b_refs[t], chip, c, (px, py, c), send.at[3 * t + d - 1], recv.at[3 * t + d - 1]).start()
        token[...] = jnp.zeros_like(token)

    outs = pl.pallas_call(
        body, name=f"gather_start_{tag}",
        out_shape=(pltpu.SemaphoreType.DMA((3 * nt,)), pltpu.SemaphoreType.DMA((3 * nt,)),
                   *[pltpu.HBM(b.shape, b.dtype) for b in bufs], jax.ShapeDtypeStruct((8, 128), F32)),
        in_specs=[HBM] * nt + [ANY] * na, out_specs=(SEM, SEM, *[HBM] * nt, pl.BlockSpec(memory_space=pltpu.VMEM)),
        input_output_aliases={t: 2 + t for t in range(nt)},
        compiler_params=pltpu.CompilerParams(has_side_effects=DATAFLOW),
    )(*[_in_hbm(b) for b in bufs], *after)
    return outs[0], outs[1], list(outs[2:2 + nt]), outs[2 + nt]


def _gather_wait(send, recv, bufs, after, tag):
    nt = len(bufs)

    def body(*refs):
        b_refs = refs[:nt]
        send_ref, recv_ref = refs[nt], refs[nt + 1]
        x, y, c, chip = _place()
        for t in range(nt):
            for d in (1, 2, 3):
                k = 3 * t + d - 1
                px, py, pchip = _chip_at(x, y, d)
                _half_block(b_refs[t], chip, c, (px, py, c), send_ref.at[k], recv_ref.at[k]).wait_send()
                _half_block(b_refs[t], pchip, c, (px, py, c), send_ref.at[k], recv_ref.at[k]).wait_recv()

    after = tuple(after) if isinstance(after, (tuple, list)) else (after,)
    outs = pl.pallas_call(
        body, name=f"gather_wait_{tag}", out_shape=[pltpu.HBM(b.shape, b.dtype) for b in bufs],
        in_specs=[HBM] * nt + [SEM, SEM] + [ANY] * len(after), out_specs=[HBM] * nt,
        input_output_aliases={t: t for t in range(nt)},
        compiler_params=pltpu.CompilerParams(has_side_effects=DATAFLOW),
    )(*bufs, send, recv, *after)
    return list(outs)


def _gather_pass_on(bufs, tag):
    nt = len(bufs)

    def body(*refs):
        o_refs = refs[nt:2 * nt]
        fsend, frecv = refs[2 * nt:]
        x, y, c, _ = _place()
        cps = []
        for t in range(nt):
            for d in (1, 2, 3):
                k = 3 * t + d - 1
                _, _, pchip = _chip_at(x, y, d)
                cps.append(_half_block(o_refs[t], pchip, c, (x, y, 1 - c), fsend.at[k], frecv.at[k]))
        for cp in cps:
            cp.start()
        for t in range(nt):
            for d in (1, 2, 3):
                k = 3 * t + d - 1
                _, _, pchip = _chip_at(x, y, d)
                _half_block(o_refs[t], pchip, 1 - c, (x, y, 1 - c), fsend.at[k], frecv.at[k]).wait_recv()
        for cp in cps:
            cp.wait_send()

    return pl.pallas_call(
        body, name=f"gather_pass_on_{tag}", out_shape=[jax.ShapeDtypeStruct(b.shape, b.dtype) for b in bufs],
        in_specs=[ANY] * nt, out_specs=[ANY] * nt,
        scratch_shapes=[pltpu.SemaphoreType.DMA((3 * nt,)), pltpu.SemaphoreType.DMA((3 * nt,))],
        input_output_aliases={t: t for t in range(nt)},
        compiler_params=_cp(has_side_effects=True),
    )(*bufs)


def _to_sibling_halves(gs, l):
    nt = len(gs)

    def body(*refs):
        g_refs, o_refs = refs[:nt], refs[nt:2 * nt]
        send, recv = refs[2 * nt:]
        x, y, c, _ = _place()
        cps = [pltpu.make_async_remote_copy(
            src_ref=g_refs[t].at[k, 1 - c], dst_ref=o_refs[t].at[k], send_sem=send.at[4 * t + k], recv_sem=recv.at[4 * t + k],
            device_id=(x, y, 1 - c), device_id_type=MESH) for t in range(nt) for k in range(4)]
        for cp in cps:
            cp.start()
        for cp in cps:
            cp.wait()

    return pl.pallas_call(
        body, name=f"grads_to_sibling_{l}", out_shape=[jax.ShapeDtypeStruct((4,) + g.shape[2:], g.dtype) for g in gs],
        in_specs=[ANY] * nt, out_specs=[ANY] * nt,
        scratch_shapes=[pltpu.SemaphoreType.DMA((4 * nt,)), pltpu.SemaphoreType.DMA((4 * nt,))],
        compiler_params=_cp(has_side_effects=True),
    )(*gs)


def _chip_copy(c_ref, land_ref, x, y, c, d, send_sem, recv_sem):
    px, py, pchip = _chip_at(x, y, d)
    return pltpu.make_async_remote_copy(src_ref=c_ref.at[pchip], dst_ref=land_ref.at[d - 1], send_sem=send_sem, recv_sem=recv_sem,
                                        device_id=(px, py, c), device_id_type=MESH)


def _exchange_start(srcs, lands, copies, nsem, name):
    ns, n = len(srcs), len(srcs) + len(lands)

    def body(*refs):
        for cp in copies(refs[:ns], refs[ns:n], refs[n], refs[n + 1]):
            cp.start()
        token = refs[2 * n + 2]
        token[...] = jnp.zeros_like(token)

    outs = pl.pallas_call(
        body, name=name,
        out_shape=(pltpu.SemaphoreType.DMA((nsem,)), pltpu.SemaphoreType.DMA((nsem,)),
                   *[pltpu.HBM(a.shape, a.dtype) for a in list(srcs) + list(lands)], jax.ShapeDtypeStruct((8, 128), F32)),
        in_specs=[HBM] * n, out_specs=(SEM, SEM, *[HBM] * n, pl.BlockSpec(memory_space=pltpu.VMEM)),
        input_output_aliases={i: 2 + i for i in range(n)},
        compiler_params=pltpu.CompilerParams(has_side_effects=DATAFLOW),
    )(*[_in_hbm(a) for a in list(srcs) + list(lands)])
    return outs[0], outs[1], list(outs[2:2 + ns]), list(outs[2 + ns:2 + n]), outs[2 + n]


def _exchange_wait(send, recv, srcs, lands, after, copies, name):
    ns, n = len(srcs), len(srcs) + len(lands)

    def body(*refs):
        for cp in copies(refs[:ns], refs[ns:n], refs[n], refs[n + 1]):
            cp.wait_send()
            cp.wait_recv()

    outs = pl.pallas_call(
        body, name=name, out_shape=[pltpu.HBM(a.shape, a.dtype) for a in list(srcs) + list(lands)],
        in_specs=[HBM] * n + [SEM, SEM, ANY], out_specs=[HBM] * n,
        input_output_aliases={i: i for i in range(n)},
        compiler_params=pltpu.CompilerParams(has_side_effects=DATAFLOW),
    )(*srcs, *lands, send, recv, after)
    return list(outs[:ns]), list(outs[ns:])


def _pass_on_copies(b_refs, land_refs, send, recv):
    del land_refs
    x, y, c, _ = _place()
    return [_half_block(b_refs[t], _chip_at(x, y, d)[2], c, (x, y, 1 - c), send.at[3 * t + d - 1], recv.at[3 * t + d - 1])
            for t in range(len(b_refs)) for d in (1, 2, 3)]


def _chips_copies(c_refs, land_refs, send, recv):
    x, y, c, _ = _place()
    return [_chip_copy(c_refs[t], land_refs[t], x, y, c, d, send.at[3 * t + d - 1], recv.at[3 * t + d - 1])
            for t in range(len(c_refs)) for d in (1, 2, 3)]


def _sibling_copies(g_refs, land_refs, send, recv):
    x, y, c, _ = _place()
    return [pltpu.make_async_remote_copy(
        src_ref=g_refs[t].at[k, 1 - c], dst_ref=land_refs[t].at[k], send_sem=send.at[4 * t + k], recv_sem=recv.at[4 * t + k],
        device_id=(x, y, 1 - c), device_id_type=MESH) for t in range(len(g_refs)) for k in range(4)]


def _join_halves(fs, l):
    nt = len(fs)

    def body(*refs):
        o_refs = refs[nt:2 * nt]
        send, recv = refs[2 * nt:]
        x, y, c, _ = _place()
        cps = [pltpu.make_async_remote_copy(
            src_ref=o_refs[t].at[c], dst_ref=o_refs[t].at[c], send_sem=send.at[t], recv_sem=recv.at[t],
            device_id=(x, y, 1 - c), device_id_type=MESH) for t in range(nt)]
        for cp in cps:
            cp.start()
        for cp in cps:
            cp.wait()

    return pl.pallas_call(
        body, name=f"grads_join_{l}", out_shape=[jax.ShapeDtypeStruct(a.shape, a.dtype) for a in fs],
        in_specs=[ANY] * nt, out_specs=[ANY] * nt,
        scratch_shapes=[pltpu.SemaphoreType.DMA((nt,)), pltpu.SemaphoreType.DMA((nt,))],
        input_output_aliases={t: t for t in range(nt)},
        compiler_params=_cp(has_side_effects=True),
    )(*fs)


def _add_half(gs, rs, c_arr, name):
    n = len(gs)

    def body(c_ref, *refs):
        for g_ref, r_ref, o_ref in zip(refs[:n], refs[n:2 * n], refs[2 * n:]):
            o_ref[...] = (g_ref[...] + r_ref[...]).astype(BF16)

    def own(g):
        return pl.BlockSpec((None, None) + g.shape[2:], lambda k, cr: (k, cr[0], 0, 0))

    def blk(g):
        return pl.BlockSpec((None,) + g.shape[2:], lambda k, cr: (k, 0, 0))

    return pl.pallas_call(
        body, name=name, out_shape=[jax.ShapeDtypeStruct((4,) + g.shape[2:], BF16) for g in gs],
        grid_spec=pltpu.PrefetchScalarGridSpec(
            num_scalar_prefetch=1, grid=(4,),
            in_specs=[own(g) for g in gs] + [blk(g) for g in gs], out_specs=[blk(g) for g in gs]),
        compiler_params=_cp(("parallel",)),
    )(c_arr, *gs, *rs)


def _sum_chips(css, r3s, place_arr, name):
    n = len(css)

    def body(pl_ref, *refs):
        up = lambda ref: ref[...].astype(F32)
        for t in range(n):
            a_ref, (r0_ref, r1_ref, r2_ref), o_ref = refs[t], refs[n + 3 * t:n + 3 * t + 3], refs[4 * n + t]
            o_ref[...] = ((up(a_ref) + up(r0_ref)) + up(r1_ref)) + up(r2_ref)

    def blk(cs, first):
        _, rh, cols = cs.shape
        return pl.BlockSpec((None, rh // 2, cols), lambda i, pa: (first(pa), i, 0))

    in_specs = [blk(cs, lambda pa: pa[0]) for cs in css]
    for cs in css:
        in_specs += [blk(cs, lambda pa, d=d: d) for d in range(3)]
    return pl.pallas_call(
        body, name=name, out_shape=[jax.ShapeDtypeStruct((2,) + cs.shape[1:], F32) for cs in css],
        grid_spec=pltpu.PrefetchScalarGridSpec(
            num_scalar_prefetch=1, grid=(2,), in_specs=in_specs, out_specs=[blk(cs, lambda pa: pa[1]) for cs in css]),
        compiler_params=_cp(("parallel",)),
    )(place_arr, *css, *[r3 for r3 in r3s for _ in range(3)])


def _allreduce_small(pack):
    rows = pack.shape[0]
    hr = rows // 2

    def body(p_ref, o_ref, sib, slots, s1, r1, s2, r2, s3, r3):
        x, y, c, chip = _place()
        sibling = (x, y, 1 - c)
        ex = pltpu.make_async_remote_copy(src_ref=p_ref, dst_ref=sib, send_sem=s1, recv_sem=r1,
                                          device_id=sibling, device_id_type=MESH)
        ex.start()
        ex.wait()
        half = pl.ds(pl.multiple_of(c * hr, 16), hr)
        slots[0] = (p_ref[half, :] + sib[half, :]).astype(BF16)
        cps = []
        for d in (1, 2, 3):
            px, py, _ = _chip_at(x, y, d)
            cps.append(pltpu.make_async_remote_copy(
                src_ref=slots.at[0], dst_ref=slots.at[d], send_sem=s2.at[d - 1], recv_sem=r2.at[d - 1],
                device_id=(px, py, c), device_id_type=MESH))
        for cp in cps:
            cp.start()
        for cp in cps:
            cp.wait()
        tot = slots[chip].astype(F32)
        for k in (1, 2, 3):
            tot = tot + slots[jnp.bitwise_xor(chip, k)].astype(F32)
        o_ref[half, :] = tot
        back = pltpu.make_async_remote_copy(src_ref=o_ref.at[half, :], dst_ref=o_ref.at[half, :], send_sem=s3, recv_sem=r3,
                                            device_id=sibling, device_id_type=MESH)
        back.start()
        back.wait()

    vm = pl.BlockSpec(memory_space=pltpu.VMEM)
    return pl.pallas_call(
        body, name="allreduce_small", out_shape=jax.ShapeDtypeStruct((rows, 128), F32),
        in_specs=[vm], out_specs=vm,
        scratch_shapes=[pltpu.VMEM((rows, 128), F32), pltpu.VMEM((4, hr, 128), BF16),
                        pltpu.SemaphoreType.DMA, pltpu.SemaphoreType.DMA, pltpu.SemaphoreType.DMA((3,)), pltpu.SemaphoreType.DMA((3,)),
                        pltpu.SemaphoreType.DMA, pltpu.SemaphoreType.DMA],
        compiler_params=_cp(has_side_effects=True),
    )(pack)


def _small_chip_sum(pack):
    rows = pack.shape[0]
    hr = rows // 2

    def body(p_ref, o_ref, sib, s1, r1):
        x, y, c, _ = _place()
        ex = pltpu.make_async_remote_copy(src_ref=p_ref, dst_ref=sib, send_sem=s1, recv_sem=r1,
                                          device_id=(x, y, 1 - c), device_id_type=MESH)
        ex.start()
        ex.wait()
        half = pl.ds(pl.multiple_of(c * hr, 16), hr)
        o_ref[...] = (p_ref[half, :] + sib[half, :]).astype(BF16)

    vm = pl.BlockSpec(memory_space=pltpu.VMEM)
    return pl.pallas_call(
        body, name="small_chip_sum", out_shape=jax.ShapeDtypeStruct((hr, 128), BF16), in_specs=[vm], out_specs=vm,
        scratch_shapes=[pltpu.VMEM((rows, 128), F32), pltpu.SemaphoreType.DMA, pltpu.SemaphoreType.DMA],
        compiler_params=_cp(has_side_effects=True),
    )(pack)


def _small_copies(c_refs, land_refs, send, recv):
    x, y, c, _ = _place()
    cps = []
    for d in (1, 2, 3):
        px, py, _ = _chip_at(x, y, d)
        cps.append(pltpu.make_async_remote_copy(src_ref=c_refs[0], dst_ref=land_refs[0].at[d - 1], send_sem=send.at[d - 1],
                                                recv_sem=recv.at[d - 1], device_id=(px, py, c), device_id_type=MESH))
    return cps


def _small_total(csum, land):
    hr = csum.shape[0]

    def body(c_ref, l_ref, o_ref, slots, s3, r3):
        x, y, c, chip = _place()
        slots[0] = c_ref[...]
        for d in (1, 2, 3):
            slots[d] = l_ref[d - 1]
        tot = slots[chip].astype(F32)
        for k in (1, 2, 3):
            tot = tot + slots[jnp.bitwise_xor(chip, k)].astype(F32)
        half = pl.ds(pl.multiple_of(c * hr, 16), hr)
        o_ref[half, :] = tot
        back = pltpu.make_async_remote_copy(src_ref=o_ref.at[half, :], dst_ref=o_ref.at[half, :], send_sem=s3, recv_sem=r3,
                                            device_id=(x, y, 1 - c), device_id_type=MESH)
        back.start()
        back.wait()

    vm = pl.BlockSpec(memory_space=pltpu.VMEM)
    return pl.pallas_call(
        body, name="small_total", out_shape=jax.ShapeDtypeStruct((2 * hr, 128), F32), in_specs=[vm, vm], out_specs=vm,
        scratch_shapes=[pltpu.VMEM((4, hr, 128), BF16), pltpu.SemaphoreType.DMA, pltpu.SemaphoreType.DMA],
        compiler_params=_cp(has_side_effects=True),
    )(csum, land)


def _adam_math(gv, wv, mv, vv):
    m2 = ADAM_B1 * mv + (1.0 - ADAM_B1) * gv
    v2 = ADAM_B2 * vv + (1.0 - ADAM_B2) * (gv * gv)
    m_hat = m2 / (1.0 - ADAM_B1 ** ADAM_STEP)
    v_hat = v2 / (1.0 - ADAM_B2 ** ADAM_STEP)
    return -ADAM_LR * (m_hat / (jnp.sqrt(v_hat) + ADAM_EPS) + ADAM_WD * wv), m2, v2


def _adam(g, w, m, v, name):
    rows, cols = g.shape
    rb = rows // 4

    def body(g_ref, w_ref, m_ref, v_ref, d_ref, m2_ref, v2_ref):
        d_ref[...], m2_ref[...], v2_ref[...] = _adam_math(g_ref[...], w_ref[...], m_ref[...], v_ref[...])

    blk = pl.BlockSpec((rb, cols), lambda i: (i, 0))
    shp = jax.ShapeDtypeStruct((rows, cols), F32)
    return pl.pallas_call(
        body, name=name, grid=(4,), in_specs=[blk] * 4, out_specs=[blk] * 3, out_shape=[shp] * 3,
        compiler_params=_cp(("parallel",)),
    )(g, w, m, v)


def _adam_layer(gs, ws, ms, vs, l, prevs, name):
    n = len(gs)
    prev = [a for p4 in prevs if p4 is not None for a in p4]

    def body(*refs):
        outs = refs[len(refs) - 4 * n:]
        for t in range(n):
            g_ref, w_ref, m_ref, v_ref = refs[4 * t:4 * t + 4]
            go_ref, d_ref, m2_ref, v2_ref = outs[4 * t:4 * t + 4]
            gv = g_ref[...]
            go_ref[...] = gv
            d_ref[...], m2_ref[...], v2_ref[...] = _adam_math(gv, w_ref[...], m_ref[...], v_ref[...])

    in_specs, out_specs, out_shape, operands, aliases = [], [], [], [], {}
    for t, g in enumerate(gs):
        rows, cols = g.shape
        lay = pl.BlockSpec((None, rows // 4, cols), lambda i: (l, i, 0))
        in_specs += [pl.BlockSpec((rows // 4, cols), lambda i: (i, 0)), lay, lay, lay]
        operands += [g, ws[t], ms[t], vs[t]]
        out_specs += [lay] * 4
        out_shape += [jax.ShapeDtypeStruct((2, rows, cols), F32)] * 4
    k = 4 * n
    for t, p4 in enumerate(prevs):
        if p4 is not None:
            for j in range(4):
                aliases[k] = 4 * t + j
                k += 1
    outs = pl.pallas_call(
        body, name=name, grid=(4,), in_specs=in_specs + [ANY] * len(prev), out_specs=out_specs, out_shape=out_shape,
        input_output_aliases=aliases, compiler_params=_cp(("parallel",)),
    )(*operands, *prev)
    return [list(outs[4 * t:4 * t + 4]) for t in range(n)]


def _rows128(a):
    return a.reshape(-1, 128)


def _pack(arrs, mult):
    parts = [_rows128(a) for a in arrs]
    rows = sum(q.shape[0] for q in parts)
    pad = -rows % mult
    if pad:
        parts.append(jnp.zeros((pad, 128), F32))
    return jnp.concatenate(parts, axis=0)


def _unpack(pack, shapes):
    out, o = [], 0
    for s in shapes:
        n = 1
        for e in s:
            n *= e
        out.append(pack[o:o + n // 128].reshape(s))
        o += n // 128
    return out


WEIGHTS = ['norm1_g', 'w_in', 'gmlp_ln_g', 'gmlp_ln_b', 'gmlp_w_s', 'gmlp_b_s', 'conv_w', 'conv_b', 'lru_w_r', 'lru_b_r', 'lru_w_i',
           'lru_b_i', 'lru_lambda', 'w_out', 'norm2_g', 'w_ffn_in', 'w_ffn_out', 'final_g']
BIG = ['w_in', 'w_out', 'w_ffn_in', 'w_ffn_out']
SMALL = [n for n in WEIGHTS if n not in BIG]
CHIP_SHARDED_SMALL = ['conv_w', 'lru_b_r', 'lru_b_i', 'lru_lambda']


def kernel(x, norm1_g, w_in, gmlp_ln_g, gmlp_ln_b, gmlp_w_s, gmlp_b_s, conv_w, conv_b, lru_w_r, lru_b_r, lru_w_i, lru_b_i, lru_lambda, w_out, norm2_g, w_ffn_in, w_ffn_out, final_g, loss_target, m_norm1_g, m_w_in, m_gmlp_ln_g, m_gmlp_ln_b, m_gmlp_w_s, m_gmlp_b_s, m_conv_w, m_conv_b, m_lru_w_r, m_lru_b_r, m_lru_w_i, m_lru_b_i, m_lru_lambda, m_w_out, m_norm2_g, m_w_ffn_in, m_w_ffn_out, m_final_g, v_norm1_g, v_w_in, v_gmlp_ln_g, v_gmlp_ln_b, v_gmlp_w_s, v_gmlp_b_s, v_conv_w, v_conv_b, v_lru_w_r, v_lru_b_r, v_lru_w_i, v_lru_b_i, v_lru_lambda, v_w_out, v_norm2_g, v_w_ffn_in, v_w_ffn_out, v_final_g):
    a = dict(locals())
    w = {n: a[n] for n in WEIGHTS}
    mom = {n: a["m_" + n] for n in WEIGHTS}
    var = {n: a["v_" + n] for n in WEIGHTS}
    _, _, c, chip = _place()
    c_arr, chip_arr = jnp.reshape(c, (1,)).astype(jnp.int32), jnp.reshape(chip, (1,)).astype(jnp.int32)
    place_arr = jnp.stack([chip, c]).astype(jnp.int32)

    first, rest = BIG[:1], BIG[1:]

    def as_weights(names, full):
        wb = {n: f.reshape(4, 2 * f.shape[2], f.shape[3]) for n, f in zip(names, full)}
        if "w_out" in wb:
            wb["w_out"] = wb["w_out"].reshape(D, D)
            wb["w_ffn_out"] = wb["w_ffn_out"].reshape(DFF, D)
        return wb

    def cast(names, l, tag):
        return _cast_into([w[n] for n in names], l, chip_arr, f"cast_{tag}")

    def landed(fly, names, after, tag):
        return as_weights(names, _gather_pass_on(_gather_wait(fly[0], fly[1], fly[2], after, tag), tag))

    tiny = _pack([w[n] for n in CHIP_SHARDED_SMALL], 8)
    _, tiny_full = _gather_weights([], tiny)
    fly_in = _gather_start(cast(first, 0, "in"), "in", after=(tiny_full,))
    fly0 = _gather_start(cast(rest, 0, "0"), "0", after=(fly_in[3],))
    fly1 = _gather_start(cast(BIG, 1, "1"), "1", after=(fly0[3],))
    p = {n: w[n] for n in SMALL}
    parts = [_unpack(tiny_full[k], [w[n].shape for n in CHIP_SHARDED_SMALL]) for k in range(4)]
    for i, n in enumerate(CHIP_SHARDED_SMALL):
        p[n] = jnp.concatenate([parts[k][i] for k in range(4)], axis=-1)

    operands = [_layer_operands(l, p) for l in range(2)]
    state_packs = [_pack([src[n] for n in SMALL], 32) for src in (w, mom, var)]
    ahead = tuple(jax.tree.leaves(operands)) + tuple(state_packs)

    passing = {}

    def pass_on_1(gu):
        bufs = _gather_wait(fly1[0], fly1[1], fly1[2], gu, "1")
        passing[1] = _exchange_start(bufs, [], _pass_on_copies, 3 * len(bufs), "gather_pass_on_start_1")
        return (passing[1][-1],)

    xa, saved0 = _forward_layer(0, x[0], p, landed(fly_in, first, (fly1[3],) + ahead, "in"), after=(fly0[3], fly1[3]),
                                rest=lambda merged: landed(fly0, rest, merged, "0"), near_end=pass_on_1, operands=operands[0])
    send, recv, bufs1, _, _ = passing[1]
    xb, saved1 = _forward_layer(
        1, xa, p, as_weights(BIG, _exchange_wait(send, recv, bufs1, [], xa, _pass_on_copies, "gather_pass_on_wait_1")[0]),
        operands=operands[1])
    dxb, loss_v, dfg = _loss_head(xb, loss_target[0], p["final_g"][None])
    loss = lax.psum(loss_v[0, 0], ("x", "y", "c"))

    out, flying = {}, {}

    def halves(grads):
        return [g.reshape(4, 2, -1, g.shape[-1]) for g in grads]

    def sibling_start(grads, names, l, tag):
        gs = halves(grads)
        lands = [lax.empty((4,) + g.shape[2:], g.dtype) for g in gs]
        flying["s" + tag] = (names, l) + tuple(
            _exchange_start(gs, lands, _sibling_copies, 4 * len(gs), f"grads_to_sibling_start_{tag}"))
        return (flying["s" + tag][-1],)

    def chips_start(gs, from_sib, names, l, tag):
        cs = _add_half(gs, from_sib, c_arr, f"add_half_{tag}")
        lands = [lax.empty((3,) + a.shape[1:], a.dtype) for a in cs]
        flying[tag] = (names, l) + tuple(_exchange_start(cs, lands, _chips_copies, 3 * len(cs), f"grads_to_chips_start_{tag}"))
        return (flying[tag][-1],)

    def sibling_finish(tag, after):
        names, l, send, recv, gs, lands, _ = flying["s" + tag]
        gs, from_sib = _exchange_wait(send, recv, gs, lands, after, _sibling_copies, f"grads_to_sibling_wait_{tag}")
        return chips_start(gs, from_sib, names, l, tag)

    def reduce_start(grads, names, l, tag):
        gs = halves(grads)
        return chips_start(gs, _to_sibling_halves(gs, tag), names, l, tag)

    def reduce_finish(tags, after):
        groups, ts = [], []
        for tag in tags:
            names, l, send, recv, cs, lands, _ = flying[tag]
            cs, lands = _exchange_wait(send, recv, cs, lands, after, _chips_copies, f"grads_to_chips_wait_{tag}")
            ts += _sum_chips(cs, lands, place_arr, f"sum_chips_{tag}")
            groups.append((tag, names))
        joined = _join_halves(ts, tags[0])
        for tag, names in groups:
            gs, joined = [j.reshape(w[n].shape[1:]) for n, j in zip(names, joined)], joined[len(names):]
            res = _adam_layer(gs, [w[n] for n in names], [mom[n] for n in names], [var[n] for n in names], l,
                              [out.get(n) for n in names], f"adam_{tag}")
            out.update(zip(names, res))

    def late1(grads, _):
        return sibling_finish("1a", grads[0]) + sibling_start(grads, first, 1, "1b")

    def midway0(grads):
        reduce_finish(("1a", "1b"), grads[0])
        return sibling_start(grads, rest, 0, "0a")

    def stacked_small(small0):
        small = {k: jnp.stack([small0[k], small1[k]]) for k in LAYER_SMALL}
        return dict(small, final_g=dfg[0])

    def late0(grads, small0):
        toks = reduce_start(grads, first, 0, "0b")
        small = stacked_small(dict(small0, norm1_g=jnp.zeros((D,), F32)))
        csum = _small_chip_sum(_pack([small[n] for n in SMALL], 32))
        flying["small"] = _exchange_start([csum], [lax.empty((3,) + csum.shape, BF16)], _small_copies, 3, "small_to_chips_start")
        return toks + (flying["small"][-1],)

    dxa, big1, small1 = _backward_layer(1, dxb, saved1, midway=lambda grads: sibling_start(grads, rest, 1, "1a"), late=late1)
    dx, big0, small0 = _backward_layer(0, dxa, saved0, after=sibling_finish("1b", dxa), midway=midway0,
                                       midway2=lambda dws: sibling_finish("0a", dws), late=late0)
    reduce_finish(("0a", "0b"), dx)
    small = stacked_small(small0)

    full_shapes = [small[n].shape for n in SMALL]
    send, recv, csum, land, _ = flying["small"]
    csum, land = _exchange_wait(send, recv, csum, land, out[first[0]][0], _small_copies, "small_to_chips_wait")
    red = _unpack(_small_total(csum[0], land[0]), full_shapes)
    norm1_0 = _allreduce_small(_pack([small0["norm1_g"]], 32))[:D // 128].reshape(D)
    red[SMALL.index("norm1_g")] = red[SMALL.index("norm1_g")].at[0].set(norm1_0)
    g_small = []
    for n, g in zip(SMALL, red):
        if n in CHIP_SHARDED_SMALL:
            g = lax.dynamic_slice_in_dim(g, chip * w[n].shape[-1], w[n].shape[-1], axis=g.ndim - 1)
        g_small.append(g)
    shapes = [w[n].shape for n in SMALL]
    upd = [_unpack(u, shapes) for u in _adam(_pack(g_small, 32), *state_packs, "adam_small")]
    for i, n in enumerate(SMALL):
        out[n] = [g_small[i], upd[0][i], upd[1][i], upd[2][i]]

    return (loss, dx[None]) + tuple(out[n][i] for i in range(4) for n in WEIGHTS)
```

```python
import functools

import jax
import jax.numpy as jnp
from jax import lax
from jax.experimental import pallas as pl
from jax.experimental.pallas import tpu as pltpu

F32 = jnp.float32
BF16 = jnp.bfloat16
MESH = pl.DeviceIdType.MESH

D = 1024
NH = 8
HD = 128
CHUNK = 128
N_IN_T = 12
DFF = 2816
DFF_SH = 1408
EPS = 1e-6
LRU_C = 8.0
ADAM_LR, ADAM_B1, ADAM_B2, ADAM_EPS, ADAM_WD, ADAM_STEP = 0.001, 0.9, 0.999, 1e-08, 0.01, 10

TM = 512
TM_BIG = 1024
RT = 128
PADR = 8
VMEM_LIMIT = 56 * 1024 * 1024


def _cp(sem=None, **kw):
    if sem is not None:
        kw["dimension_semantics"] = sem
    return pltpu.CompilerParams(vmem_limit_bytes=VMEM_LIMIT, **kw)


_GC = 0.7978845608028654


def _sigmoid(x):
    return 0.5 * jnp.tanh(0.5 * x) + 0.5


_GK = 0.044715


def _gelu(x):
    t = jnp.tanh(x * (_GC + (_GC * _GK) * (x * x)))
    return x * (0.5 + 0.5 * t)


def _gelu_and_grad(x):
    x2 = x * x
    t = jnp.tanh(x * (_GC + (_GC * _GK) * x2))
    h = 0.5 + 0.5 * t
    return x * h, h + x * (1.0 - t * t) * (0.5 * _GC + (1.5 * _GC * _GK) * x2)


def _softplus_neg(lam):
    y = jnp.exp(-jnp.abs(lam))
    u = 1.0 + y
    l1p = jnp.where(u == 1.0, y, jnp.log(u) * y / (u - 1.0))
    return jnp.maximum(-lam, 0.0) + l1p


def _dot(a, b):
    return jnp.dot(a, b, preferred_element_type=F32)


def _dot_nt(a, b):
    return lax.dot_general(a, b, (((1,), (1,)), ((), ())), preferred_element_type=F32)


def _dot_tn(a, b):
    return lax.dot_general(a, b, (((0,), (0,)), ((), ())), preferred_element_type=F32)


def _rms_hat(x):
    r = lax.rsqrt(jnp.mean(x * x, axis=-1, keepdims=True) + EPS)
    return x * r, r


def _rms_bwd(dh, x, g):
    xh, r = _rms_hat(x)
    dxh = dh * g
    dx = r * (dxh - xh * jnp.mean(dxh * xh, axis=-1, keepdims=True))
    return dx, jnp.sum(dh * xh, axis=0, keepdims=True)


def _norm_into(x_ref, g_ref, h_ref):
    xh, _ = _rms_hat(x_ref[...])
    h_ref[...] = (xh * g_ref[...]).astype(BF16)


def _in_tile(j):
    m, hf = j // 2, j % 2
    orig = jnp.where(m < 2, m, jnp.where(m == 2, 4, jnp.where(m < 5, m - 1, 5)))
    t = orig * 2 + hf
    return t // 3, t % 3


ANY = pl.BlockSpec(memory_space=pl.ANY)


def _mm_in(x, g, w_in, l, after=()):
    S = x.shape[0]
    tm = min(2 * TM_BIG, S)

    def body(x_ref, g_ref, w0_ref, w1_ref, *rest):
        o_ref, h_ref = rest[-2:]

        @pl.when(pl.program_id(1) == 0)
        def _():
            _norm_into(x_ref, g_ref, h_ref)
        rp = min(TM, tm)
        for r0 in range(0, tm, rp):
            hv = h_ref[r0:r0 + rp, :]
            o_ref[r0:r0 + rp, 0:512] = _dot(hv, w0_ref[...]).astype(BF16)
            o_ref[r0:r0 + rp, 512:1024] = _dot(hv, w1_ref[...]).astype(BF16)

    def w_tile(hf):
        def w_map(i, m):
            sh, tl = _in_tile(2 * m + hf)
            return (sh, 0, tl)
        return pl.BlockSpec((None, D, 512), w_map)

    return pl.pallas_call(
        body, name=f"mm_in_{l}", grid=(S // tm, 6),
        in_specs=[pl.BlockSpec((tm, D), lambda i, m: (i, 0)), pl.BlockSpec((1, D), lambda i, m: (0, 0)),
                  w_tile(0), w_tile(1)] + [ANY] * len(after),
        out_specs=[pl.BlockSpec((None, tm, D), lambda i, m: (m, i, 0)), pl.BlockSpec((tm, D), lambda i, m: (i, 0))],
        out_shape=[jax.ShapeDtypeStruct((6, S, D), BF16), jax.ShapeDtypeStruct((S, D), BF16)],
        compiler_params=_cp(("parallel", "arbitrary")),
    )(x, g, w_in, w_in, *after)


def _mm_res(a, w, res, l, name, after=()):
    S, K = a.shape

    tm = TM

    def body(a_ref, w_ref, r_ref, *rest):
        rest[-1][...] = r_ref[...] + _dot(a_ref[...], w_ref[...])

    return pl.pallas_call(
        body, name=f"{name}_{l}", grid=(S // tm,),
        in_specs=[pl.BlockSpec((tm, K), lambda i: (i, 0)), pl.BlockSpec((K, D), lambda i: (0, 0)),
                  pl.BlockSpec((tm, D), lambda i: (i, 0))] + [ANY] * len(after),
        out_specs=pl.BlockSpec((tm, D), lambda i: (i, 0)),
        out_shape=jax.ShapeDtypeStruct((S, D), F32),
        compiler_params=_cp(("parallel",)),
    )(a, w, res, *after)


def _mm_ffn_in(x, g, w_fi, l):
    S = x.shape[0]

    tm = min(TM_BIG, S)

    def body(x_ref, g_ref, w_ref, gu_ref, ff_ref, h_ref):
        @pl.when(pl.program_id(1) == 0)
        def _():
            _norm_into(x_ref, g_ref, h_ref)
        for r0 in range(0, tm, TM):
            rows = slice(r0, r0 + TM)
            hv = h_ref[rows, :]
            ga = _dot(hv, w_ref[0])
            gb = _dot(hv, w_ref[1])
            sg = _sigmoid(ga)
            silu = ga * sg
            gu_ref[0, rows, :] = (gb * (sg + silu * (1.0 - sg))).astype(BF16)
            gu_ref[1, rows, :] = silu.astype(BF16)
            ff_ref[rows, :] = (silu * gb).astype(BF16)

    gu, ff, h = pl.pallas_call(
        body, name=f"mm_ffn_in_{l}", grid=(S // tm, 2),
        in_specs=[pl.BlockSpec((tm, D), lambda i, s: (i, 0)), pl.BlockSpec((1, D), lambda i, s: (0, 0)),
                  pl.BlockSpec((2, None, D, DFF_SH), lambda i, s: (0, s, 0, 0))],
        out_specs=[pl.BlockSpec((2, None, tm, DFF_SH), lambda i, s: (0, s, i, 0)),
                   pl.BlockSpec((tm, DFF_SH), lambda i, s: (i, s)),
                   pl.BlockSpec((tm, D), lambda i, s: (i, 0))],
        out_shape=[jax.ShapeDtypeStruct((2, 2, S, DFF_SH), BF16), jax.ShapeDtypeStruct((S, DFF), BF16),
                   jax.ShapeDtypeStruct((S, D), BF16)],
        compiler_params=_cp(("parallel", "arbitrary")),
    )(x, g, w_fi.reshape(2, 2, D, DFF_SH))
    return gu.reshape(4, S, DFF_SH), ff, h


def _gmlp_fwd(z6, ws_b, bs_b, lg, lb):
    S = z6.shape[1]

    def body(z_ref, ws_ref, bs_ref, lg_ref, lb_ref, o_ref, mix):
        gv = _gelu(z_ref[1].astype(F32))
        xc = gv - jnp.mean(gv, axis=-1, keepdims=True)
        rs = lax.rsqrt(jnp.mean(xc * xc, axis=-1, keepdims=True) + EPS)
        vb = (xc * rs * lg_ref[...] + lb_ref[...]).astype(BF16)
        for gi in range(NH):
            cs = slice(gi * HD, (gi + 1) * HD)
            mix[:, cs] = _dot(ws_ref[gi], vb[:, cs])
        o_ref[...] = (_sigmoid(z_ref[2].astype(F32)) * _gelu(z_ref[0].astype(F32)) * (mix[...] + bs_ref[...])).astype(BF16)

    return pl.pallas_call(
        body, name="gmlp_fwd", grid=(S // CHUNK,),
        in_specs=[pl.BlockSpec((3, CHUNK, D), lambda i: (0, i, 0)), pl.BlockSpec((NH, CHUNK, CHUNK), lambda i: (0, 0, 0)),
                  pl.BlockSpec((CHUNK, D), lambda i: (0, 0)), pl.BlockSpec((1, D), lambda i: (0, 0)),
                  pl.BlockSpec((1, D), lambda i: (0, 0))],
        out_specs=pl.BlockSpec((CHUNK, D), lambda i: (i, 0)),
        out_shape=jax.ShapeDtypeStruct((S, D), BF16),
        scratch_shapes=[pltpu.VMEM((CHUNK, D), F32)],
        compiler_params=_cp(("parallel",)),
    )(z6, ws_b, bs_b, lg, lb)


def _row_iota():
    return lax.broadcasted_iota(jnp.int32, (RT, HD), 0)


SUB = 8
UNROLL = 4
GRAD_ROWS = 512


def _scan_up(a, b, carry):
    row = lax.broadcasted_iota(jnp.int32, (SUB, HD), 0)
    masks = [(d, row >= d) for d in (1, 2, 4)]
    c = jnp.broadcast_to(carry, (SUB, HD))
    hs = []
    for j in range(RT // SUB):
        aj, bj = a[SUB * j:SUB * (j + 1)], b[SUB * j:SUB * (j + 1)]
        for d, m in masks:
            bj = bj + aj * jnp.where(m, pltpu.roll(bj, d, 0), 0.0)
            aj = aj * jnp.where(m, pltpu.roll(aj, d, 0), 1.0)
        h = bj + aj * c
        hs.append(h)
        c = jnp.broadcast_to(h[SUB - 1:SUB, :], (SUB, HD))
    return jnp.concatenate(hs, axis=0), hs[-1][SUB - 1:SUB, :]


def _scan_down(a, b, carry):
    row = lax.broadcasted_iota(jnp.int32, (SUB, HD), 0)
    masks = [(d, row < SUB - d) for d in (1, 2, 4)]
    c = jnp.broadcast_to(carry, (SUB, HD))
    hs = []
    for j in reversed(range(RT // SUB)):
        aj, bj = a[SUB * j:SUB * (j + 1)], b[SUB * j:SUB * (j + 1)]
        for d, m in masks:
            bj = bj + aj * jnp.where(m, pltpu.roll(bj, SUB - d, 0), 0.0)
            aj = aj * jnp.where(m, pltpu.roll(aj, SUB - d, 0), 1.0)
        h = bj + aj * c
        hs.append(h)
        c = jnp.broadcast_to(h[0:1, :], (SUB, HD))
    return jnp.concatenate(hs[::-1], axis=0), hs[-1][0:1, :]


def _decay(r, sp_d):
    log_a = -LRU_C * r * sp_d
    a = jnp.exp(log_a)
    return a, jnp.sqrt(jnp.maximum(-jnp.tanh(log_a) * (a * a + 1.0), 0.0))


def _lru_gates(xc, d, wr_ref, br_ref, wi_ref, bi_ref, sp):
    xb = xc.astype(BF16)
    r = _sigmoid(_dot(xb, wr_ref[d]) + br_ref[d:d + 1, :])
    i = _sigmoid(_dot(xb, wi_ref[d]) + bi_ref[d:d + 1, :])
    a, mult = _decay(r, sp[d:d + 1, :])
    return r, i, a, mult


def _shifted(win, k):
    w = RT + 2 * PADR
    v = win if k == 0 else pltpu.roll(win, (-k) % w, 0)
    return v[PADR:PADR + RT]


def _conv_taps(win):
    return [_shifted(win, k) for k in (-1, 0, 1, 2)]


def _fill_padded(dst, src_ref, S):
    zeros = jnp.zeros((PADR, HD), F32)
    dst[0:PADR, :] = zeros
    dst[PADR + S:2 * PADR + S, :] = zeros

    def cp(i, c):
        t0 = pl.multiple_of(i * RT, RT)
        dst[pl.ds(t0 + PADR, RT), :] = src_ref[pl.ds(t0, RT), :].astype(F32)
        return c
    lax.fori_loop(0, S // RT, cp, 0)


def _conv_fwd_all(zxp, xc_s, cw_ref, cb_ref, S):
    def cv(i, c):
        t0 = pl.multiple_of(i * RT, RT)
        xm1, x0, xp1, xp2 = _conv_taps(zxp[pl.ds(t0, RT + 2 * PADR), :])
        xc_s[pl.ds(t0, RT), :] = (cb_ref[...] + xm1 * cw_ref[0:1, :] + x0 * cw_ref[1:2, :]
                                  + xp1 * cw_ref[2:3, :] + xp2 * cw_ref[3:4, :])
        return c
    lax.fori_loop(0, S // RT, cv, 0)


def _lru_specs(S):
    head = lambda h: (0, h)
    return [pl.BlockSpec((4, HD), head), pl.BlockSpec((1, HD), head),
            pl.BlockSpec((2, None, HD, HD), lambda h: (0, h, 0, 0)), pl.BlockSpec((2, HD), head),
            pl.BlockSpec((2, None, HD, HD), lambda h: (0, h, 0, 0)), pl.BlockSpec((2, HD), head),
            pl.BlockSpec((2, HD), head)]


def _lru_fwd(z6, ya, cw, cb, wr, br, wi, bi, lam):
    S = z6.shape[1]
    nt = S // RT

    def body(z_ref, ya_ref, cw_ref, cb_ref, wr_ref, br_ref, wi_ref, bi_ref, lam_ref, mg_ref, h0_ref, h1_ref, zxp, xc_s):
        sp = _softplus_neg(lam_ref[...])
        _fill_padded(zxp, z_ref.at[0], S)
        _conv_fwd_all(zxp, xc_s, cw_ref, cb_ref, S)

        def scans(i, carry):
            cu, cd = carry
            for u in range(UNROLL):
                j = i * UNROLL + u
                ru = pl.ds(pl.multiple_of(j * RT, RT), RT)
                rd = pl.ds(pl.multiple_of((nt - 1 - j) * RT, RT), RT)
                xu, xd = xc_s[ru, :], xc_s[rd, :]
                _, gi, a, mult = _lru_gates(xu, 0, wr_ref, br_ref, wi_ref, bi_ref, sp)
                hu, cu = _scan_up(a, mult * gi * xu, cu)
                h0_ref[ru, :] = hu
                _, gi, a, mult = _lru_gates(xd, 1, wr_ref, br_ref, wi_ref, bi_ref, sp)
                hd, cd = _scan_down(a, mult * gi * xd, cd)
                h1_ref[rd, :] = hd
            return cu, cd
        z1 = jnp.zeros((1, HD), F32)
        lax.fori_loop(0, nt // UNROLL, scans, (z1, z1))

        def merge(i, c):
            rows = pl.ds(pl.multiple_of(i * RT, RT), RT)
            yb = (h0_ref[rows, :] + h1_ref[rows, :]) * _gelu(z_ref[1, rows, :].astype(F32))
            mg_ref[rows, :] = (ya_ref[rows, :].astype(F32) + _sigmoid(z_ref[2, rows, :].astype(F32)) * yb).astype(BF16)
            return c
        lax.fori_loop(0, nt, merge, 0)

    col = pl.BlockSpec((S, HD), lambda h: (0, h))
    return pl.pallas_call(
        body, name="lru_fwd", grid=(NH,),
        in_specs=[pl.BlockSpec((3, S, HD), lambda h: (1, 0, h)), col] + _lru_specs(S),
        out_specs=[col, col, col],
        out_shape=[jax.ShapeDtypeStruct((S, D), BF16), jax.ShapeDtypeStruct((S, D), F32), jax.ShapeDtypeStruct((S, D), F32)],
        scratch_shapes=[pltpu.VMEM((S + 2 * PADR, HD), F32), pltpu.VMEM((S, HD), F32)],
        compiler_params=_cp(("parallel",)),
    )(z6, ya, cw, cb, wr, br, wi, bi, lam)


def _loss_head(x, tgt, g):
    S = x.shape[0]

    def body(x_ref, t_ref, g_ref, dx_ref, loss_ref, dg_ref):
        @pl.when(pl.program_id(0) == 0)
        def _():
            loss_ref[...] = jnp.zeros_like(loss_ref)
            dg_ref[...] = jnp.zeros_like(dg_ref)
        xv = x_ref[...]
        xh, _ = _rms_hat(xv)
        e = xh * g_ref[...] - t_ref[...]
        loss_ref[...] += jnp.sum(e * e) * (0.5 / D)
        dx, dgs = _rms_bwd(e * (1.0 / D), xv, g_ref[...])
        dx_ref[...] = dx
        dg_ref[...] += dgs

    return pl.pallas_call(
        body, name="loss_head", grid=(S // TM,),
        in_specs=[pl.BlockSpec((TM, D), lambda i: (i, 0)), pl.BlockSpec((TM, D), lambda i: (i, 0)),
                  pl.BlockSpec((1, D), lambda i: (0, 0))],
        out_specs=[pl.BlockSpec((TM, D), lambda i: (i, 0)), pl.BlockSpec((1, 128), lambda i: (0, 0)),
                   pl.BlockSpec((1, D), lambda i: (0, 0))],
        out_shape=[jax.ShapeDtypeStruct((S, D), F32), jax.ShapeDtypeStruct((1, 128), F32), jax.ShapeDtypeStruct((1, D), F32)],
        compiler_params=_cp(("arbitrary",)),
    )(x, tgt, g)


def _bwd_ffn_out(dx, w_fo, gu, l, after=()):
    S = dx.shape[0]

    tm = min(TM_BIG, S)

    def body(dx_ref, w_ref, gu_ref, *rest):
        o_ref = rest[-1]
        for r0 in range(0, tm, TM):
            rows = slice(r0, r0 + TM)
            d = _dot_nt(dx_ref[rows, :].astype(BF16), w_ref[...])
            o_ref[0, rows, :] = (d * gu_ref[0, rows, :].astype(F32)).astype(BF16)
            o_ref[1, rows, :] = (d * gu_ref[1, rows, :].astype(F32)).astype(BF16)

    pair = pl.BlockSpec((2, None, tm, DFF_SH), lambda i, s: (0, s, i, 0))
    dgu = pl.pallas_call(
        body, name=f"bwd_ffn_out_{l}", grid=(S // tm, 2),
        in_specs=[pl.BlockSpec((tm, D), lambda i, s: (i, 0)), pl.BlockSpec((DFF_SH, D), lambda i, s: (s, 0)), pair]
        + [ANY] * len(after),
        out_specs=pair,
        out_shape=jax.ShapeDtypeStruct((2, 2, S, DFF_SH), BF16),
        compiler_params=_cp(("parallel", "arbitrary")),
    )(dx, w_fo, gu.reshape(2, 2, S, DFF_SH), *after)
    return dgu.reshape(4, S, DFF_SH)


def _mm_tn(a, b, m_blk, tk, name):
    S, M = a.shape

    def body(a_ref, b_ref, o_ref):
        @pl.when(pl.program_id(1) == 0)
        def _():
            o_ref[...] = jnp.zeros_like(o_ref)
        o_ref[...] += _dot_tn(a_ref[...], b_ref[...].astype(BF16))

    return pl.pallas_call(
        body, name=name, grid=(M // m_blk, S // tk),
        in_specs=[pl.BlockSpec((tk, m_blk), lambda m, k: (k, m)), pl.BlockSpec((tk, D), lambda m, k: (k, 0))],
        out_specs=pl.BlockSpec((m_blk, D), lambda m, k: (m, 0)),
        out_shape=jax.ShapeDtypeStruct((M, D), F32),
        compiler_params=_cp(("parallel", "arbitrary")),
    )(a, b)


def _mm_nt_rms_bwd(a, a_specs, w, w_specs, nk, tm, x, g, dres, name, after=(), row_part=None, one_sum=False):
    S = x.shape[0]
    sub = len(a_specs)
    row_part = tm if row_part is None else min(row_part, tm)

    def body(*refs):
        a_refs, w_refs = refs[:sub], refs[sub:2 * sub]
        x_ref, g_ref, r_ref = refs[2 * sub:2 * sub + 3]
        dx_ref, dg_ref, acc = refs[-3:]
        i, k = pl.program_id(0), pl.program_id(1)
        @pl.when(k == 0)
        def _():
            acc[...] = jnp.zeros_like(acc)
        for r0 in range(0, tm, row_part):
            rows = slice(r0, r0 + row_part)
            if one_sum:
                acc[rows, :] += sum(_dot_nt(a_refs[j][rows, :], w_refs[j][...]) for j in range(sub))
            else:
                for j in range(sub):
                    acc[rows, :] += _dot_nt(a_refs[j][rows, :], w_refs[j][...])

        @pl.when(jnp.logical_and(i == 0, k == 0))
        def _():
            dg_ref[...] = jnp.zeros_like(dg_ref)

        @pl.when(k == nk - 1)
        def _():
            dx, dgs = _rms_bwd(acc[...], x_ref[...], g_ref[...])
            dx_ref[...] = r_ref[...] + dx
            dg_ref[...] += dgs

    row = pl.BlockSpec((tm, D), lambda i, k: (i, 0))
    vec = pl.BlockSpec((1, D), lambda i, k: (0, 0))
    return pl.pallas_call(
        body, name=name, grid=(S // tm, nk),
        in_specs=list(a_specs) + list(w_specs) + [row, vec, row] + [ANY] * len(after),
        out_specs=[row, vec],
        out_shape=[jax.ShapeDtypeStruct((S, D), F32), jax.ShapeDtypeStruct((1, D), F32)],
        scratch_shapes=[pltpu.VMEM((tm, D), F32)],
        compiler_params=_cp(("arbitrary", "arbitrary")),
    )(*[a] * sub, *[w] * sub, x, g, dres, *after)


def _dw_ffn_in(h, dgu, l):
    S = h.shape[0]

    def body(h_ref, b_ref, o_ref):
        @pl.when(pl.program_id(1) == 0)
        def _():
            o_ref[...] = jnp.zeros_like(o_ref)
        o_ref[...] += _dot_tn(h_ref[...], b_ref[...])

    tk = min(2 * TM_BIG, S)
    return pl.pallas_call(
        body, name=f"dw_ffn_in_{l}", grid=(4, S // tk),
        in_specs=[pl.BlockSpec((tk, D), lambda j, k: (k, 0)), pl.BlockSpec((None, tk, DFF_SH), lambda j, k: (j, k, 0))],
        out_specs=pl.BlockSpec((None, D, DFF_SH), lambda j, k: (j, 0, 0)),
        out_shape=jax.ShapeDtypeStruct((4, D, DFF_SH), F32),
        compiler_params=_cp(("parallel", "arbitrary")),
    )(h, dgu)


_HALF_COMPS = ((0, 1, 3), (4, 2, 5))


def _dw_in(h, dz6, l):
    S = h.shape[0]

    def body(h_ref, d0_ref, d1_ref, d2_ref, o_ref):
        @pl.when(pl.program_id(1) == 0)
        def _():
            o_ref[...] = jnp.zeros_like(o_ref)
        hv = h_ref[...]
        for q, d_ref in enumerate((d0_ref, d1_ref, d2_ref)):
            for hf in range(2):
                col = 1024 * q + 512 * hf
                o_ref[col // 1536, :, col % 1536:col % 1536 + 512] += _dot_tn(hv, d_ref[:, 512 * hf:512 * (hf + 1)])

    tk = min(TM_BIG, S)

    def comp(q):
        return pl.BlockSpec((None, tk, D), lambda p, k: (jnp.where(p == 0, _HALF_COMPS[0][q], _HALF_COMPS[1][q]), k, 0))

    return pl.pallas_call(
        body, name=f"dw_in_{l}", grid=(2, S // tk),
        in_specs=[pl.BlockSpec((tk, D), lambda p, k: (k, 0)), comp(0), comp(1), comp(2)],
        out_specs=pl.BlockSpec((2, D, 1536), lambda p, k: (p, 0, 0)),
        out_shape=jax.ShapeDtypeStruct((4, D, 1536), F32),
        compiler_params=_cp(("parallel", "arbitrary")),
    )(h, dz6, dz6, dz6)


def _bwd_out(dx, w_o, merged, l):
    S = dx.shape[0]

    def body(dx_ref, w_ref, m_ref, dm_ref, dw_ref):
        @pl.when(pl.program_id(0) == 0)
        def _():
            dw_ref[...] = jnp.zeros_like(dw_ref)
        dxb = dx_ref[...].astype(BF16)
        dm_ref[...] = _dot_nt(dxb, w_ref[...]).astype(BF16)
        dw_ref[...] += _dot_tn(m_ref[...], dxb)

    tm = TM
    row = pl.BlockSpec((tm, D), lambda i: (i, 0))
    return pl.pallas_call(
        body, name=f"bwd_out_{l}", grid=(S // tm,),
        in_specs=[row, pl.BlockSpec((D, D), lambda i: (0, 0)), row],
        out_specs=[row, pl.BlockSpec((D, D), lambda i: (0, 0))],
        out_shape=[jax.ShapeDtypeStruct((S, D), BF16), jax.ShapeDtypeStruct((D, D), F32)],
        compiler_params=_cp(("arbitrary",)),
    )(dx, w_o, merged)


def _gmlp_bwd(dm, z6, ws_b, wst_b, bs_b, lg, lb, after=()):
    S = z6.shape[1]

    def body(dm_ref, z_ref, ws_ref, wst_ref, bs_ref, lg_ref, lb_ref, *rest):
        dz_ref, dws_ref, dbs_ref, dlg_ref, dlb_ref, mix, dv = rest[-7:]

        @pl.when(pl.program_id(0) == 0)
        def _():
            dws_ref[...] = jnp.zeros_like(dws_ref)
            dbs_ref[...] = jnp.zeros_like(dbs_ref)
            dlg_ref[...] = jnp.zeros_like(dlg_ref)
            dlb_ref[...] = jnp.zeros_like(dlb_ref)
        gv, dgelu_v = _gelu_and_grad(z_ref[1].astype(F32))
        xc = gv - jnp.mean(gv, axis=-1, keepdims=True)
        rs = lax.rsqrt(jnp.mean(xc * xc, axis=-1, keepdims=True) + EPS)
        vh = xc * rs
        vb = (vh * lg_ref[...] + lb_ref[...]).astype(BF16)
        for gi in range(NH):
            cs = slice(gi * HD, (gi + 1) * HD)
            mix[:, cs] = _dot(ws_ref[gi], vb[:, cs])
        u, dgelu_u = _gelu_and_grad(z_ref[0].astype(F32))
        sa = _sigmoid(z_ref[2].astype(F32))
        mixed = mix[...] + bs_ref[...]
        dyg = dm_ref[...].astype(F32)
        dz_ref[2] = (dyg * u * mixed * sa * (1.0 - sa)).astype(BF16)
        dya = dyg * sa
        dz_ref[0] = (dya * mixed * dgelu_u).astype(BF16)
        dmix = dya * u
        dmb = dmix.astype(BF16)
        for gi in range(NH):
            cs = slice(gi * HD, (gi + 1) * HD)
            dv[:, cs] = _dot(wst_ref[gi], dmb[:, cs])
            dws_ref[gi] += _dot_nt(dmb[:, cs], vb[:, cs])
            dbs_ref[gi] += jnp.broadcast_to(jnp.sum(dmix[:, cs], axis=1, keepdims=True), (CHUNK, HD))
        dvv = dv[...]
        dlg_ref[...] += jnp.sum(dvv * vh, axis=0, keepdims=True)
        dlb_ref[...] += jnp.sum(dvv, axis=0, keepdims=True)
        dvh = dvv * lg_ref[...]
        dgv = rs * (dvh - jnp.mean(dvh, axis=-1, keepdims=True) - vh * jnp.mean(dvh * vh, axis=-1, keepdims=True))
        dz_ref[1] = (dgv * dgelu_v).astype(BF16)

    vec = pl.BlockSpec((1, D), lambda i: (0, 0))
    mat = pl.BlockSpec((NH, CHUNK, CHUNK), lambda i: (0, 0, 0))
    return pl.pallas_call(
        body, name="gmlp_bwd", grid=(S // CHUNK,),
        in_specs=[pl.BlockSpec((CHUNK, D), lambda i: (i, 0)), pl.BlockSpec((3, CHUNK, D), lambda i: (0, i, 0)), mat, mat,
                  pl.BlockSpec((CHUNK, D), lambda i: (0, 0)), vec, vec] + [ANY] * len(after),
        out_specs=[pl.BlockSpec((3, CHUNK, D), lambda i: (0, i, 0)), mat, mat, vec, vec],
        out_shape=[jax.ShapeDtypeStruct((6, S, D), BF16), jax.ShapeDtypeStruct((NH, CHUNK, CHUNK), F32),
                   jax.ShapeDtypeStruct((NH, CHUNK, HD), F32), jax.ShapeDtypeStruct((1, D), F32), jax.ShapeDtypeStruct((1, D), F32)],
        scratch_shapes=[pltpu.VMEM((CHUNK, D), F32), pltpu.VMEM((CHUNK, D), F32)],
        compiler_params=_cp(("arbitrary",)),
    )(dm, z6, ws_b, wst_b, bs_b, lg, lb, *after)


def _lru_bwd(dz6, dm, z6, h0, h1, cw, cb, wr, br, wi, bi, lam, after=()):
    S = z6.shape[1]
    nt = S // RT

    def body(dz_in, dm_ref, z_ref, h0_ref, h1_ref, cw_ref, cb_ref, wr_ref, br_ref, wi_ref, bi_ref, lam_ref, *rest):
        dz_ref, dcw_ref, dcb_ref, dwr_ref, dbr_ref, dwi_ref, dbi_ref, dlam_ref, zxp, xc_s, dhs_s, dxcp, r_s, lam_s = rest[-14:]
        del dz_in
        lam = lam_ref[...]
        sp = _softplus_neg(lam)
        row = _row_iota()
        _fill_padded(zxp, z_ref.at[0], S)
        _conv_fwd_all(zxp, xc_s, cw_ref, cb_ref, S)
        zeros = jnp.zeros((PADR, HD), F32)
        dxcp[0:PADR, :] = zeros
        dxcp[PADR + S:2 * PADR + S, :] = zeros
        dwr_ref[...] = jnp.zeros_like(dwr_ref)
        dwi_ref[...] = jnp.zeros_like(dwi_ref)

        def pre(i, c):
            rows = pl.ds(pl.multiple_of(i * RT, RT), RT)
            hs = h0_ref[rows, :] + h1_ref[rows, :]
            dmv = dm_ref[rows, :].astype(F32)
            sb = _sigmoid(z_ref[2, rows, :].astype(F32))
            gg, dgg = _gelu_and_grad(z_ref[1, rows, :].astype(F32))
            dz_ref[2, rows, :] = (dmv * hs * gg * sb * (1.0 - sb)).astype(BF16)
            dyb = dmv * sb
            dz_ref[1, rows, :] = (dyb * hs * dgg).astype(BF16)
            dhs_s[rows, :] = dyb * gg
            return c
        lax.fori_loop(0, nt, pre, 0)

        def gate_bwd(d, gates, lamv, da, xc):
            r, gi, a, mult = gates
            dmult = lamv * gi * xc
            dgi = lamv * mult * xc
            dlog = (da - dmult * a / mult) * a
            dpr = (dlog * (-LRU_C) * sp[d:d + 1, :]) * r * (1.0 - r)
            dpi = dgi * gi * (1.0 - gi)
            xb, dprb, dpib = xc.astype(BF16), dpr.astype(BF16), dpi.astype(BF16)
            dwr_ref[d] += _dot_tn(xb, dprb)
            dwi_ref[d] += _dot_tn(xb, dpib)
            dxc = lamv * mult * gi + _dot_nt(dprb, wr_ref[d]) + _dot_nt(dpib, wi_ref[d])
            return dxc, (jnp.sum(dlog * r, axis=0, keepdims=True) * (-LRU_C), jnp.sum(dpr, axis=0, keepdims=True),
                         jnp.sum(dpi, axis=0, keepdims=True))

        def rgates(i, c):
            for u in range(UNROLL):
                rows = pl.ds(pl.multiple_of((i * UNROLL + u) * RT, RT), RT)
                xb = xc_s[rows, :].astype(BF16)
                for d in range(2):
                    r_s[d, rows, :] = _sigmoid(_dot(xb, wr_ref[d]) + br_ref[d:d + 1, :])
            return c
        lax.fori_loop(0, nt // UNROLL, rgates, 0)

        def chains(i, carry):
            qn, qp = carry
            for u in range(UNROLL):
                j = i * UNROLL + u
                rd = pl.ds(pl.multiple_of((nt - 1 - j) * RT, RT), RT)
                a, dhs = _decay(r_s[0, rd, :], sp[0:1, :])[0], dhs_s[rd, :]
                q, q_first = _scan_down(a, a * dhs, qn)
                lam_s[0, rd, :] = dhs + jnp.where(row == RT - 1, qn, pltpu.roll(q, RT - 1, 0))
                qn = q_first
                ru = pl.ds(pl.multiple_of(j * RT, RT), RT)
                a, dhs = _decay(r_s[1, ru, :], sp[1:2, :])[0], dhs_s[ru, :]
                q, q_last = _scan_up(a, a * dhs, qp)
                lam_s[1, ru, :] = dhs + jnp.where(row == 0, qp, pltpu.roll(q, 1, 0))
                qp = q_last
            return qn, qp

        z1 = jnp.zeros((1, HD), F32)
        lax.fori_loop(0, nt // UNROLL, chains, (z1, z1))

        ct = min(GRAD_ROWS, S)
        crow = lax.broadcasted_iota(jnp.int32, (ct, HD), 0)

        def tile_grads(i, acc):
            t0 = pl.multiple_of(i * ct, ct)
            rows = pl.ds(t0, ct)
            xc = xc_s[rows, :]
            xb = xc.astype(BF16)
            tp = pl.multiple_of(jnp.maximum(t0 - PADR, 0), PADR)
            prev = jnp.where(t0 > 0, h0_ref[pl.ds(tp, PADR), :][PADR - 1:PADR, :], 0.0)
            tn = pl.multiple_of(jnp.minimum(t0 + ct, S - PADR), PADR)
            nxt = jnp.where(t0 + ct < S, h1_ref[pl.ds(tn, PADR), :][0:1, :], 0.0)
            hside = (jnp.where(crow == 0, prev, pltpu.roll(h0_ref[rows, :], 1, 0)),
                     jnp.where(crow == ct - 1, nxt, pltpu.roll(h1_ref[rows, :], ct - 1, 0)))
            dxc, sums = 0.0, ()
            for d in range(2):
                r = r_s[d, rows, :]
                gi = _sigmoid(_dot(xb, wi_ref[d]) + bi_ref[d:d + 1, :])
                a, mult = _decay(r, sp[d:d + 1, :])
                lamv = lam_s[d, rows, :]
                dxc_d, s_d = gate_bwd(d, (r, gi, a, mult), lamv, lamv * hside[d], xc)
                dxc = dxc + dxc_d
                sums = sums + s_d
            dxcp[pl.ds(t0 + PADR, ct), :] = dxc
            return tuple(x + y for x, y in zip(acc, sums))

        s_sp0, s_br0, s_bi0, s_sp1, s_br1, s_bi1 = lax.fori_loop(0, S // ct, tile_grads, (z1,) * 6)

        dsp = jnp.concatenate([s_sp0, s_sp1], axis=0)
        dlam_ref[...] = -dsp * _sigmoid(-lam)
        dbr_ref[...] = jnp.concatenate([s_br0, s_br1], axis=0)
        dbi_ref[...] = jnp.concatenate([s_bi0, s_bi1], axis=0)

        def conv_bwd(i, carry):
            c0, c1, c2, c3, cb_ = carry
            t0 = pl.multiple_of(i * RT, RT)
            dwin = dxcp[pl.ds(t0, RT + 2 * PADR), :]
            d0 = _shifted(dwin, 0)
            dz_ref[0, pl.ds(t0, RT), :] = (_shifted(dwin, 1) * cw_ref[0:1, :] + d0 * cw_ref[1:2, :]
                                           + _shifted(dwin, -1) * cw_ref[2:3, :] + _shifted(dwin, -2) * cw_ref[3:4, :]).astype(BF16)
            xm1, x0, xp1, xp2 = _conv_taps(zxp[pl.ds(t0, RT + 2 * PADR), :])
            sm = lambda v: jnp.sum(v, axis=0, keepdims=True)
            return c0 + sm(d0 * xm1), c1 + sm(d0 * x0), c2 + sm(d0 * xp1), c3 + sm(d0 * xp2), cb_ + sm(d0)

        c0, c1, c2, c3, cb_ = lax.fori_loop(0, nt, conv_bwd, (z1, z1, z1, z1, z1))
        dcw_ref[...] = jnp.concatenate([c0, c1, c2, c3], axis=0)
        dcb_ref[...] = cb_

    col = pl.BlockSpec((S, HD), lambda h: (0, h))
    head = lambda h: (0, h)
    wspec = pl.BlockSpec((2, None, HD, HD), lambda h: (0, h, 0, 0))
    return pl.pallas_call(
        body, name="lru_bwd", grid=(NH,),
        in_specs=[pl.BlockSpec(memory_space=pl.ANY), col, pl.BlockSpec((3, S, HD), lambda h: (1, 0, h)), col, col] + _lru_specs(S)
        + [ANY] * len(after),
        out_specs=[pl.BlockSpec((3, S, HD), lambda h: (1, 0, h)), pl.BlockSpec((4, HD), head), pl.BlockSpec((1, HD), head),
                   wspec, pl.BlockSpec((2, HD), head), wspec, pl.BlockSpec((2, HD), head), pl.BlockSpec((2, HD), head)],
        out_shape=[jax.ShapeDtypeStruct((6, S, D), BF16), jax.ShapeDtypeStruct((4, D), F32), jax.ShapeDtypeStruct((1, D), F32),
                   jax.ShapeDtypeStruct((2, NH, HD, HD), F32), jax.ShapeDtypeStruct((2, D), F32),
                   jax.ShapeDtypeStruct((2, NH, HD, HD), F32), jax.ShapeDtypeStruct((2, D), F32), jax.ShapeDtypeStruct((2, D), F32)],
        scratch_shapes=[pltpu.VMEM((S + 2 * PADR, HD), F32), pltpu.VMEM((S, HD), F32), pltpu.VMEM((S, HD), F32),
                        pltpu.VMEM((S + 2 * PADR, HD), F32), pltpu.VMEM((2, S, HD), F32), pltpu.VMEM((2, S, HD), F32)],
        input_output_aliases={0: 0},
        compiler_params=_cp(("parallel",)),
    )(dz6, dm, z6, h0, h1, cw, cb, wr, br, wi, bi, lam, *after)


LAYER_SMALL = ("norm1_g", "gmlp_ln_g", "gmlp_ln_b", "gmlp_w_s", "gmlp_b_s", "conv_w", "conv_b",
               "lru_w_r", "lru_b_r", "lru_w_i", "lru_b_i", "lru_lambda", "norm2_g")


def _layer_operands(l, p):
    ws_b = p["gmlp_w_s"][l].astype(BF16)
    tm = dict(ws_b=ws_b, wst_b=jnp.swapaxes(ws_b, 1, 2), bs_b=jnp.repeat(p["gmlp_b_s"][l].T, HD, axis=1),
              lg=p["gmlp_ln_g"][l][None], lb=p["gmlp_ln_b"][l][None])
    lru = (p["conv_w"][l], p["conv_b"][l][None], p["lru_w_r"][l].astype(BF16), p["lru_b_r"][l],
           p["lru_w_i"][l].astype(BF16), p["lru_b_i"][l], p["lru_lambda"][l])
    return (p["norm1_g"][l][None], p["norm2_g"][l][None]), tm, lru


def _forward_layer(l, x, p, wb, after=(), rest=None, near_end=None, operands=None):
    (g1, g2), tm, lru = _layer_operands(l, p) if operands is None else operands
    z6, hn1 = _mm_in(x, g1, wb["w_in"], l, after)
    ya = _gmlp_fwd(z6, tm["ws_b"], tm["bs_b"], tm["lg"], tm["lb"])
    merged, h0, h1 = _lru_fwd(z6, ya, *lru)
    if rest is not None:
        wb = dict(wb, **rest(merged))
    x1 = _mm_res(merged, wb["w_out"], x, l, "mm_out")
    gu, ff, hn2 = _mm_ffn_in(x1, g2, wb["w_ffn_in"], l)
    x2 = _mm_res(ff, wb["w_ffn_out"], x1, l, "mm_ffn_out", () if near_end is None else tuple(near_end(gu)))
    return x2, dict(x=x, z6=z6, h0=h0, h1=h1, merged=merged, x1=x1, gu=gu, ff=ff, g1=g1, g2=g2, tm=tm, lru=lru,
                    hn1=hn1, hn2=hn2, wb=wb)


def _backward_layer(l, dx, s, after=(), midway=None, midway2=None, late=None):
    S = dx.shape[0]
    tm, wb = s["tm"], s["wb"]
    g2 = s["g2"]
    dgu = _bwd_ffn_out(dx, wb["w_ffn_out"], s["gu"], l, after)
    tmb = min(TM_BIG, S)
    dwfo = _mm_tn(s["ff"], dx, DFF_SH, tmb, f"dw_ffn_out_{l}")
    dx1, dg2 = _mm_nt_rms_bwd(
        dgu, [pl.BlockSpec((None, tmb, DFF_SH), lambda i, k: (k, i, 0))],
        wb["w_ffn_in"], [pl.BlockSpec((None, D, DFF_SH), lambda i, k: (k, 0, 0))],
        4, tmb, s["x1"], g2, dx, f"bwd_ffn_in_{l}", row_part=TM if l else None)
    dwfi = _dw_ffn_in(s["hn2"], dgu, l)
    dmg, dwo = _bwd_out(dx1, wb["w_out"], s["merged"], l)
    mid = () if midway is None else tuple(midway([dwo, dwfi, dwfo]))
    dz6, dws, dbs, dlg, dlb = _gmlp_bwd(dmg, s["z6"], tm["ws_b"], tm["wst_b"], tm["bs_b"], tm["lg"], tm["lb"], mid)
    mid2 = () if midway2 is None else tuple(midway2(dws))
    dz6, dcw, dcb, dwr, dbr, dwi, dbi, dlam = _lru_bwd(dz6, dmg, s["z6"], s["h0"], s["h1"], *s["lru"], after=mid2)

    sub = 3

    def dz_tile(j):
        return pl.BlockSpec((None, tmb, 512), lambda i, k: ((sub * k + j) // 2, i, (sub * k + j) % 2))

    def w_tile(j):
        def w_map(i, k):
            sh, tl = _in_tile(sub * k + j)
            return (sh, 0, tl)
        return pl.BlockSpec((None, D, 512), w_map)

    dwin = _dw_in(s["hn1"], dz6, l)
    small = dict(gmlp_ln_g=dlg[0], gmlp_ln_b=dlb[0], gmlp_w_s=dws, gmlp_b_s=dbs[:, :, 0], conv_w=dcw, conv_b=dcb[0],
                 lru_w_r=dwr, lru_b_r=dbr, lru_w_i=dwi, lru_b_i=dbi, lru_lambda=dlam, norm2_g=dg2[0])
    tail = () if late is None else tuple(late([dwin], small))
    dx0, dg1 = _mm_nt_rms_bwd(
        dz6, [dz_tile(j) for j in range(sub)], wb["w_in"], [w_tile(j) for j in range(sub)],
        N_IN_T // sub, tmb, s["x"], s["g1"], dx1, f"bwd_in_{l}", tail, one_sum=bool(l))
    return dx0, [dwin, dwo, dwfi, dwfo], dict(small, norm1_g=dg1[0])


def _local_step(x, tgt, p, wbs):
    saved = []
    for l in range(2):
        x, s = _forward_layer(l, x, p, wbs[l])
        saved.append(s)
    dx, loss_v, dfg = _loss_head(x, tgt, p["final_g"][None])
    big, smalls = [None, None], [None, None]
    for l in (1, 0):
        dx, big[l], smalls[l] = _backward_layer(l, dx, saved[l])
    small = {k: jnp.stack([smalls[0][k], smalls[1][k]]) for k in LAYER_SMALL}
    small["final_g"] = dfg[0]
    return loss_v, dx, big, small


def _place():
    x, y, c = lax.axis_index("x"), lax.axis_index("y"), lax.axis_index("c")
    return x, y, c, 2 * x + y


def _chip_at(x, y, d):
    px = 1 - x if d & 2 else x
    py = 1 - y if d & 1 else y
    return px, py, 2 * px + py


HBM = pl.BlockSpec(memory_space=pltpu.HBM)
SEM = pl.BlockSpec(memory_space=pltpu.SEMAPHORE)
DATAFLOW = pltpu.SideEffectType.DATAFLOW_SIDE_EFFECTING


def _in_hbm(a):
    return pltpu.with_memory_space_constraint(a, pltpu.HBM)


def _cast_into(wfs, l, chip_arr, name):
    n = len(wfs)

    def body(ch_ref, *refs):
        for w_ref, o_ref in zip(refs[:n], refs[n:]):
            o_ref[...] = w_ref[...].astype(BF16)

    halves = [(wf.shape[1] // 2, wf.shape[2]) for wf in wfs]
    return pl.pallas_call(
        body, name=name, out_shape=[jax.ShapeDtypeStruct((4, 2, rh, cols), BF16) for rh, cols in halves],
        grid_spec=pltpu.PrefetchScalarGridSpec(
            num_scalar_prefetch=1, grid=(2,),
            in_specs=[pl.BlockSpec((None, None, rh, cols), lambda h, ch: (l, h, 0, 0)) for rh, cols in halves],
            out_specs=[pl.BlockSpec((None, None, rh, cols), lambda h, ch: (ch[0], h, 0, 0)) for rh, cols in halves]),
        compiler_params=_cp(("parallel",)),
    )(chip_arr, *[wf.reshape(2, 2, rh, cols) for wf, (rh, cols) in zip(wfs, halves)])


def _half_block(ref, chip, half, to, send_sem, recv_sem):
    blk = ref.at[chip, half]
    return pltpu.make_async_remote_copy(src_ref=blk, dst_ref=blk, send_sem=send_sem, recv_sem=recv_sem,
                                        device_id=to, device_id_type=MESH)


def _gather_weights(bufs, tiny):
    nt = len(bufs)
    n_ici = max(nt * 3, 1)

    def body(*refs):
        tiny_ref = refs[nt]
        o_refs, tiny_o = refs[nt + 1:2 * nt + 1], refs[2 * nt + 1]
        send, recv, fsend, frecv, tsend, trecv, lsem = refs[2 * nt + 2:]
        x, y, c, chip = _place()
        local = pltpu.make_async_copy(tiny_ref, tiny_o.at[chip], lsem)
        local.start()

        def tin(d, origin_chip, to):
            return pltpu.make_async_remote_copy(
                src_ref=tiny_ref, dst_ref=tiny_o.at[origin_chip], send_sem=tsend.at[d - 1], recv_sem=trecv.at[d - 1],
                device_id=to, device_id_type=MESH)

        sends = []
        for t in range(nt):
            for d in (1, 2, 3):
                px, py, _ = _chip_at(x, y, d)
                sends.append(_half_block(o_refs[t], chip, c, (px, py, c), send.at[3 * t + d - 1], recv.at[3 * t + d - 1]))
        for d in (1, 2, 3):
            px, py, _ = _chip_at(x, y, d)
            sends.append(tin(d, chip, (px, py, c)))
        for cp in sends:
            cp.start()
        passed = []
        for t in range(nt):
            for d in (1, 2, 3):
                k = 3 * t + d - 1
                _, _, pchip = _chip_at(x, y, d)
                _half_block(o_refs[t], pchip, c, (x, y, c), send.at[k], recv.at[k]).wait_recv()
                f = _half_block(o_refs[t], pchip, c, (x, y, 1 - c), fsend.at[k], frecv.at[k])
                f.start()
                passed.append(f)
        for t in range(nt):
            for d in (1, 2, 3):
                k = 3 * t + d - 1
                _, _, pchip = _chip_at(x, y, d)
                _half_block(o_refs[t], pchip, 1 - c, (x, y, 1 - c), fsend.at[k], frecv.at[k]).wait_recv()
        for d in (1, 2, 3):
            _, _, pchip = _chip_at(x, y, d)
            tin(d, pchip, (x, y, c)).wait_recv()
        for cp in sends + passed:
            cp.wait_send()
        local.wait()

    out_shape = [jax.ShapeDtypeStruct(b.shape, b.dtype) for b in bufs]
    out_shape.append(jax.ShapeDtypeStruct((4,) + tiny.shape, tiny.dtype))
    outs = pl.pallas_call(
        body, name="gather_weights_0", out_shape=out_shape,
        in_specs=[ANY] * (nt + 1), out_specs=[ANY] * (nt + 1),
        scratch_shapes=[pltpu.SemaphoreType.DMA((n_ici,)), pltpu.SemaphoreType.DMA((n_ici,)),
                        pltpu.SemaphoreType.DMA((n_ici,)), pltpu.SemaphoreType.DMA((n_ici,)),
                        pltpu.SemaphoreType.DMA((3,)), pltpu.SemaphoreType.DMA((3,)), pltpu.SemaphoreType.DMA],
        input_output_aliases={t: t for t in range(nt)},
        compiler_params=_cp(has_side_effects=True),
    )(*bufs, tiny)
    return outs[:nt], outs[nt]


def _gather_start(bufs, tag, after=()):
    nt, na = len(bufs), len(after)

    def body(*refs):
        b_refs = refs[:nt]
        send, recv = refs[nt + na], refs[nt + na + 1]
        token = refs[2 * nt + na + 2]
        x, y, c, chip = _place()
        for t in range(nt):
            for d in (1, 2, 3):
                px, py, _ = _chip_at(x, y, d)
                _half_block(b_refs[t], chip, c, (px, py, c), send.at[3 * t + d - 1], recv.at[3 * t + d - 1]).start()
        token[...] = jnp.zeros_like(token)

    outs = pl.pallas_call(
        body, name=f"gather_start_{tag}",
        out_shape=(pltpu.SemaphoreType.DMA((3 * nt,)), pltpu.SemaphoreType.DMA((3 * nt,)),
                   *[pltpu.HBM(b.shape, b.dtype) for b in bufs], jax.ShapeDtypeStruct((8, 128), F32)),
        in_specs=[HBM] * nt + [ANY] * na, out_specs=(SEM, SEM, *[HBM] * nt, pl.BlockSpec(memory_space=pltpu.VMEM)),
        input_output_aliases={t: 2 + t for t in range(nt)},
        compiler_params=pltpu.CompilerParams(has_side_effects=DATAFLOW),
    )(*[_in_hbm(b) for b in bufs], *after)
    return outs[0], outs[1], list(outs[2:2 + nt]), outs[2 + nt]


def _gather_wait(send, recv, bufs, after, tag):
    nt = len(bufs)

    def body(*refs):
        b_refs = refs[:nt]
        send_ref, recv_ref = refs[nt], refs[nt + 1]
        x, y, c, chip = _place()
        for t in range(nt):
            for d in (1, 2, 3):
                k = 3 * t + d - 1
                px, py, pchip = _chip_at(x, y, d)
                _half_block(b_refs[t], chip, c, (px, py, c), send_ref.at[k], recv_ref.at[k]).wait_send()
                _half_block(b_refs[t], pchip, c, (px, py, c), send_ref.at[k], recv_ref.at[k]).wait_recv()

    after = tuple(after) if isinstance(after, (tuple, list)) else (after,)
    outs = pl.pallas_call(
        body, name=f"gather_wait_{tag}", out_shape=[pltpu.HBM(b.shape, b.dtype) for b in bufs],
        in_specs=[HBM] * nt + [SEM, SEM] + [ANY] * len(after), out_specs=[HBM] * nt,
        input_output_aliases={t: t for t in range(nt)},
        compiler_params=pltpu.CompilerParams(has_side_effects=DATAFLOW),
    )(*bufs, send, recv, *after)
    return list(outs)


def _gather_pass_on(bufs, tag):
    nt = len(bufs)

    def body(*refs):
        o_refs = refs[nt:2 * nt]
        fsend, frecv = refs[2 * nt:]
        x, y, c, _ = _place()
        cps = []
        for t in range(nt):
            for d in (1, 2, 3):
                k = 3 * t + d - 1
                _, _, pchip = _chip_at(x, y, d)
                cps.append(_half_block(o_refs[t], pchip, c, (x, y, 1 - c), fsend.at[k], frecv.at[k]))
        for cp in cps:
            cp.start()
        for t in range(nt):
            for d in (1, 2, 3):
                k = 3 * t + d - 1
                _, _, pchip = _chip_at(x, y, d)
                _half_block(o_refs[t], pchip, 1 - c, (x, y, 1 - c), fsend.at[k], frecv.at[k]).wait_recv()
        for cp in cps:
            cp.wait_send()

    return pl.pallas_call(
        body, name=f"gather_pass_on_{tag}", out_shape=[jax.ShapeDtypeStruct(b.shape, b.dtype) for b in bufs],
        in_specs=[ANY] * nt, out_specs=[ANY] * nt,
        scratch_shapes=[pltpu.SemaphoreType.DMA((3 * nt,)), pltpu.SemaphoreType.DMA((3 * nt,))],
        input_output_aliases={t: t for t in range(nt)},
        compiler_params=_cp(has_side_effects=True),
    )(*bufs)


def _to_sibling_halves(gs, l):
    nt = len(gs)

    def body(*refs):
        g_refs, o_refs = refs[:nt], refs[nt:2 * nt]
        send, recv = refs[2 * nt:]
        x, y, c, _ = _place()
        cps = [pltpu.make_async_remote_copy(
            src_ref=g_refs[t].at[k, 1 - c], dst_ref=o_refs[t].at[k], send_sem=send.at[4 * t + k], recv_sem=recv.at[4 * t + k],
            device_id=(x, y, 1 - c), device_id_type=MESH) for t in range(nt) for k in range(4)]
        for cp in cps:
            cp.start()
        for cp in cps:
            cp.wait()

    return pl.pallas_call(
        body, name=f"grads_to_sibling_{l}", out_shape=[jax.ShapeDtypeStruct((4,) + g.shape[2:], g.dtype) for g in gs],
        in_specs=[ANY] * nt, out_specs=[ANY] * nt,
        scratch_shapes=[pltpu.SemaphoreType.DMA((4 * nt,)), pltpu.SemaphoreType.DMA((4 * nt,))],
        compiler_params=_cp(has_side_effects=True),
    )(*gs)


def _chip_copy(c_ref, land_ref, x, y, c, d, send_sem, recv_sem):
    px, py, pchip = _chip_at(x, y, d)
    return pltpu.make_async_remote_copy(src_ref=c_ref.at[pchip], dst_ref=land_ref.at[d - 1], send_sem=send_sem, recv_sem=recv_sem,
                                        device_id=(px, py, c), device_id_type=MESH)


def _exchange_start(srcs, lands, copies, nsem, name):
    ns, n = len(srcs), len(srcs) + len(lands)

    def body(*refs):
        for cp in copies(refs[:ns], refs[ns:n], refs[n], refs[n + 1]):
            cp.start()
        token = refs[2 * n + 2]
        token[...] = jnp.zeros_like(token)

    outs = pl.pallas_call(
        body, name=name,
        out_shape=(pltpu.SemaphoreType.DMA((nsem,)), pltpu.SemaphoreType.DMA((nsem,)),
                   *[pltpu.HBM(a.shape, a.dtype) for a in list(srcs) + list(lands)], jax.ShapeDtypeStruct((8, 128), F32)),
        in_specs=[HBM] * n, out_specs=(SEM, SEM, *[HBM] * n, pl.BlockSpec(memory_space=pltpu.VMEM)),
        input_output_aliases={i: 2 + i for i in range(n)},
        compiler_params=pltpu.CompilerParams(has_side_effects=DATAFLOW),
    )(*[_in_hbm(a) for a in list(srcs) + list(lands)])
    return outs[0], outs[1], list(outs[2:2 + ns]), list(outs[2 + ns:2 + n]), outs[2 + n]


def _exchange_wait(send, recv, srcs, lands, after, copies, name):
    ns, n = len(srcs), len(srcs) + len(lands)

    def body(*refs):
        for cp in copies(refs[:ns], refs[ns:n], refs[n], refs[n + 1]):
            cp.wait_send()
            cp.wait_recv()

    outs = pl.pallas_call(
        body, name=name, out_shape=[pltpu.HBM(a.shape, a.dtype) for a in list(srcs) + list(lands)],
        in_specs=[HBM] * n + [SEM, SEM, ANY], out_specs=[HBM] * n,
        input_output_aliases={i: i for i in range(n)},
        compiler_params=pltpu.CompilerParams(has_side_effects=DATAFLOW),
    )(*srcs, *lands, send, recv, after)
    return list(outs[:ns]), list(outs[ns:])


def _pass_on_copies(b_refs, land_refs, send, recv):
    del land_refs
    x, y, c, _ = _place()
    return [_half_block(b_refs[t], _chip_at(x, y, d)[2], c, (x, y, 1 - c), send.at[3 * t + d - 1], recv.at[3 * t + d - 1])
            for t in range(len(b_refs)) for d in (1, 2, 3)]


def _chips_copies(c_refs, land_refs, send, recv):
    x, y, c, _ = _place()
    return [_chip_copy(c_refs[t], land_refs[t], x, y, c, d, send.at[3 * t + d - 1], recv.at[3 * t + d - 1])
            for t in range(len(c_refs)) for d in (1, 2, 3)]


def _sibling_copies(g_refs, land_refs, send, recv):
    x, y, c, _ = _place()
    return [pltpu.make_async_remote_copy(
        src_ref=g_refs[t].at[k, 1 - c], dst_ref=land_refs[t].at[k], send_sem=send.at[4 * t + k], recv_sem=recv.at[4 * t + k],
        device_id=(x, y, 1 - c), device_id_type=MESH) for t in range(len(g_refs)) for k in range(4)]


def _join_halves(fs, l):
    nt = len(fs)

    def body(*refs):
        o_refs = refs[nt:2 * nt]
        send, recv = refs[2 * nt:]
        x, y, c, _ = _place()
        cps = [pltpu.make_async_remote_copy(
            src_ref=o_refs[t].at[c], dst_ref=o_refs[t].at[c], send_sem=send.at[t], recv_sem=recv.at[t],
            device_id=(x, y, 1 - c), device_id_type=MESH) for t in range(nt)]
        for cp in cps:
            cp.start()
        for cp in cps:
            cp.wait()

    return pl.pallas_call(
        body, name=f"grads_join_{l}", out_shape=[jax.ShapeDtypeStruct(a.shape, a.dtype) for a in fs],
        in_specs=[ANY] * nt, out_specs=[ANY] * nt,
        scratch_shapes=[pltpu.SemaphoreType.DMA((nt,)), pltpu.SemaphoreType.DMA((nt,))],
        input_output_aliases={t: t for t in range(nt)},
        compiler_params=_cp(has_side_effects=True),
    )(*fs)


def _add_half(gs, rs, c_arr, name):
    n = len(gs)

    def body(c_ref, *refs):
        for g_ref, r_ref, o_ref in zip(refs[:n], refs[n:2 * n], refs[2 * n:]):
            o_ref[...] = (g_ref[...] + r_ref[...]).astype(BF16)

    def own(g):
        return pl.BlockSpec((None, None) + g.shape[2:], lambda k, cr: (k, cr[0], 0, 0))

    def blk(g):
        return pl.BlockSpec((None,) + g.shape[2:], lambda k, cr: (k, 0, 0))

    return pl.pallas_call(
        body, name=name, out_shape=[jax.ShapeDtypeStruct((4,) + g.shape[2:], BF16) for g in gs],
        grid_spec=pltpu.PrefetchScalarGridSpec(
            num_scalar_prefetch=1, grid=(4,),
            in_specs=[own(g) for g in gs] + [blk(g) for g in gs], out_specs=[blk(g) for g in gs]),
        compiler_params=_cp(("parallel",)),
    )(c_arr, *gs, *rs)


def _sum_chips(css, r3s, place_arr, name):
    n = len(css)

    def body(pl_ref, *refs):
        up = lambda ref: ref[...].astype(F32)
        for t in range(n):
            a_ref, (r0_ref, r1_ref, r2_ref), o_ref = refs[t], refs[n + 3 * t:n + 3 * t + 3], refs[4 * n + t]
            o_ref[...] = ((up(a_ref) + up(r0_ref)) + up(r1_ref)) + up(r2_ref)

    def blk(cs, first):
        _, rh, cols = cs.shape
        return pl.BlockSpec((None, rh // 2, cols), lambda i, pa: (first(pa), i, 0))

    in_specs = [blk(cs, lambda pa: pa[0]) for cs in css]
    for cs in css:
        in_specs += [blk(cs, lambda pa, d=d: d) for d in range(3)]
    return pl.pallas_call(
        body, name=name, out_shape=[jax.ShapeDtypeStruct((2,) + cs.shape[1:], F32) for cs in css],
        grid_spec=pltpu.PrefetchScalarGridSpec(
            num_scalar_prefetch=1, grid=(2,), in_specs=in_specs, out_specs=[blk(cs, lambda pa: pa[1]) for cs in css]),
        compiler_params=_cp(("parallel",)),
    )(place_arr, *css, *[r3 for r3 in r3s for _ in range(3)])


def _allreduce_small(pack):
    rows = pack.shape[0]
    hr = rows // 2

    def body(p_ref, o_ref, sib, slots, s1, r1, s2, r2, s3, r3):
        x, y, c, chip = _place()
        sibling = (x, y, 1 - c)
        ex = pltpu.make_async_remote_copy(src_ref=p_ref, dst_ref=sib, send_sem=s1, recv_sem=r1,
                                          device_id=sibling, device_id_type=MESH)
        ex.start()
        ex.wait()
        half = pl.ds(pl.multiple_of(c * hr, 16), hr)
        slots[0] = (p_ref[half, :] + sib[half, :]).astype(BF16)
        cps = []
        for d in (1, 2, 3):
            px, py, _ = _chip_at(x, y, d)
            cps.append(pltpu.make_async_remote_copy(
                src_ref=slots.at[0], dst_ref=slots.at[d], send_sem=s2.at[d - 1], recv_sem=r2.at[d - 1],
                device_id=(px, py, c), device_id_type=MESH))
        for cp in cps:
            cp.start()
        for cp in cps:
            cp.wait()
        tot = slots[chip].astype(F32)
        for k in (1, 2, 3):
            tot = tot + slots[jnp.bitwise_xor(chip, k)].astype(F32)
        o_ref[half, :] = tot
        back = pltpu.make_async_remote_copy(src_ref=o_ref.at[half, :], dst_ref=o_ref.at[half, :], send_sem=s3, recv_sem=r3,
                                            device_id=sibling, device_id_type=MESH)
        back.start()
        back.wait()

    vm = pl.BlockSpec(memory_space=pltpu.VMEM)
    return pl.pallas_call(
        body, name="allreduce_small", out_shape=jax.ShapeDtypeStruct((rows, 128), F32),
        in_specs=[vm], out_specs=vm,
        scratch_shapes=[pltpu.VMEM((rows, 128), F32), pltpu.VMEM((4, hr, 128), BF16),
                        pltpu.SemaphoreType.DMA, pltpu.SemaphoreType.DMA, pltpu.SemaphoreType.DMA((3,)), pltpu.SemaphoreType.DMA((3,)),
                        pltpu.SemaphoreType.DMA, pltpu.SemaphoreType.DMA],
        compiler_params=_cp(has_side_effects=True),
    )(pack)


def _small_chip_sum(pack):
    rows = pack.shape[0]
    hr = rows // 2

    def body(p_ref, o_ref, sib, s1, r1):
        x, y, c, _ = _place()
        ex = pltpu.make_async_remote_copy(src_ref=p_ref, dst_ref=sib, send_sem=s1, recv_sem=r1,
                                          device_id=(x, y, 1 - c), device_id_type=MESH)
        ex.start()
        ex.wait()
        half = pl.ds(pl.multiple_of(c * hr, 16), hr)
        o_ref[...] = (p_ref[half, :] + sib[half, :]).astype(BF16)

    vm = pl.BlockSpec(memory_space=pltpu.VMEM)
    return pl.pallas_call(
        body, name="small_chip_sum", out_shape=jax.ShapeDtypeStruct((hr, 128), BF16), in_specs=[vm], out_specs=vm,
        scratch_shapes=[pltpu.VMEM((rows, 128), F32), pltpu.SemaphoreType.DMA, pltpu.SemaphoreType.DMA],
        compiler_params=_cp(has_side_effects=True),
    )(pack)


def _small_copies(c_refs, land_refs, send, recv):
    x, y, c, _ = _place()
    cps = []
    for d in (1, 2, 3):
        px, py, _ = _chip_at(x, y, d)
        cps.append(pltpu.make_async_remote_copy(src_ref=c_refs[0], dst_ref=land_refs[0].at[d - 1], send_sem=send.at[d - 1],
                                                recv_sem=recv.at[d - 1], device_id=(px, py, c), device_id_type=MESH))
    return cps


def _small_total(csum, land):
    hr = csum.shape[0]

    def body(c_ref, l_ref, o_ref, slots, s3, r3):
        x, y, c, chip = _place()
        slots[0] = c_ref[...]
        for d in (1, 2, 3):
            slots[d] = l_ref[d - 1]
        tot = slots[chip].astype(F32)
        for k in (1, 2, 3):
            tot = tot + slots[jnp.bitwise_xor(chip, k)].astype(F32)
        half = pl.ds(pl.multiple_of(c * hr, 16), hr)
        o_ref[half, :] = tot
        back = pltpu.make_async_remote_copy(src_ref=o_ref.at[half, :], dst_ref=o_ref.at[half, :], send_sem=s3, recv_sem=r3,
                                            device_id=(x, y, 1 - c), device_id_type=MESH)
        back.start()
        back.wait()

    vm = pl.BlockSpec(memory_space=pltpu.VMEM)
    return pl.pallas_call(
        body, name="small_total", out_shape=jax.ShapeDtypeStruct((2 * hr, 128), F32), in_specs=[vm, vm], out_specs=vm,
        scratch_shapes=[pltpu.VMEM((4, hr, 128), BF16), pltpu.SemaphoreType.DMA, pltpu.SemaphoreType.DMA],
        compiler_params=_cp(has_side_effects=True),
    )(csum, land)


def _adam_math(gv, wv, mv, vv):
    m2 = ADAM_B1 * mv + (1.0 - ADAM_B1) * gv
    v2 = ADAM_B2 * vv + (1.0 - ADAM_B2) * (gv * gv)
    m_hat = m2 / (1.0 - ADAM_B1 ** ADAM_STEP)
    v_hat = v2 / (1.0 - ADAM_B2 ** ADAM_STEP)
    return -ADAM_LR * (m_hat / (jnp.sqrt(v_hat) + ADAM_EPS) + ADAM_WD * wv), m2, v2


def _adam(g, w, m, v, name):
    rows, cols = g.shape
    rb = rows // 4

    def body(g_ref, w_ref, m_ref, v_ref, d_ref, m2_ref, v2_ref):
        d_ref[...], m2_ref[...], v2_ref[...] = _adam_math(g_ref[...], w_ref[...], m_ref[...], v_ref[...])

    blk = pl.BlockSpec((rb, cols), lambda i: (i, 0))
    shp = jax.ShapeDtypeStruct((rows, cols), F32)
    return pl.pallas_call(
        body, name=name, grid=(4,), in_specs=[blk] * 4, out_specs=[blk] * 3, out_shape=[shp] * 3,
        compiler_params=_cp(("parallel",)),
    )(g, w, m, v)


def _adam_layer(gs, ws, ms, vs, l, prevs, name):
    n = len(gs)
    prev = [a for p4 in prevs if p4 is not None for a in p4]

    def body(*refs):
        outs = refs[len(refs) - 4 * n:]
        for t in range(n):
            g_ref, w_ref, m_ref, v_ref = refs[4 * t:4 * t + 4]
            go_ref, d_ref, m2_ref, v2_ref = outs[4 * t:4 * t + 4]
            gv = g_ref[...]
            go_ref[...] = gv
            d_ref[...], m2_ref[...], v2_ref[...] = _adam_math(gv, w_ref[...], m_ref[...], v_ref[...])

    in_specs, out_specs, out_shape, operands, aliases = [], [], [], [], {}
    for t, g in enumerate(gs):
        rows, cols = g.shape
        lay = pl.BlockSpec((None, rows // 4, cols), lambda i: (l, i, 0))
        in_specs += [pl.BlockSpec((rows // 4, cols), lambda i: (i, 0)), lay, lay, lay]
        operands += [g, ws[t], ms[t], vs[t]]
        out_specs += [lay] * 4
        out_shape += [jax.ShapeDtypeStruct((2, rows, cols), F32)] * 4
    k = 4 * n
    for t, p4 in enumerate(prevs):
        if p4 is not None:
            for j in range(4):
                aliases[k] = 4 * t + j
                k += 1
    outs = pl.pallas_call(
        body, name=name, grid=(4,), in_specs=in_specs + [ANY] * len(prev), out_specs=out_specs, out_shape=out_shape,
        input_output_aliases=aliases, compiler_params=_cp(("parallel",)),
    )(*operands, *prev)
    return [list(outs[4 * t:4 * t + 4]) for t in range(n)]


def _rows128(a):
    return a.reshape(-1, 128)


def _pack(arrs, mult):
    parts = [_rows128(a) for a in arrs]
    rows = sum(q.shape[0] for q in parts)
    pad = -rows % mult
    if pad:
        parts.append(jnp.zeros((pad, 128), F32))
    return jnp.concatenate(parts, axis=0)


def _unpack(pack, shapes):
    out, o = [], 0
    for s in shapes:
        n = 1
        for e in s:
            n *= e
        out.append(pack[o:o + n // 128].reshape(s))
        o += n // 128
    return out


WEIGHTS = ['norm1_g', 'w_in', 'gmlp_ln_g', 'gmlp_ln_b', 'gmlp_w_s', 'gmlp_b_s', 'conv_w', 'conv_b', 'lru_w_r', 'lru_b_r', 'lru_w_i',
           'lru_b_i', 'lru_lambda', 'w_out', 'norm2_g', 'w_ffn_in', 'w_ffn_out', 'final_g']
BIG = ['w_in', 'w_out', 'w_ffn_in', 'w_ffn_out']
SMALL = [n for n in WEIGHTS if n not in BIG]
CHIP_SHARDED_SMALL = ['conv_w', 'lru_b_r', 'lru_b_i', 'lru_lambda']


def kernel(x, norm1_g, w_in, gmlp_ln_g, gmlp_ln_b, gmlp_w_s, gmlp_b_s, conv_w, conv_b, lru_w_r, lru_b_r, lru_w_i, lru_b_i, lru_lambda, w_out, norm2_g, w_ffn_in, w_ffn_out, final_g, loss_target, m_norm1_g, m_w_in, m_gmlp_ln_g, m_gmlp_ln_b, m_gmlp_w_s, m_gmlp_b_s, m_conv_w, m_conv_b, m_lru_w_r, m_lru_b_r, m_lru_w_i, m_lru_b_i, m_lru_lambda, m_w_out, m_norm2_g, m_w_ffn_in, m_w_ffn_out, m_final_g, v_norm1_g, v_w_in, v_gmlp_ln_g, v_gmlp_ln_b, v_gmlp_w_s, v_gmlp_b_s, v_conv_w, v_conv_b, v_lru_w_r, v_lru_b_r, v_lru_w_i, v_lru_b_i, v_lru_lambda, v_w_out, v_norm2_g, v_w_ffn_in, v_w_ffn_out, v_final_g):
    a = dict(locals())
    w = {n: a[n] for n in WEIGHTS}
    mom = {n: a["m_" + n] for n in WEIGHTS}
    var = {n: a["v_" + n] for n in WEIGHTS}
    _, _, c, chip = _place()
    c_arr, chip_arr = jnp.reshape(c, (1,)).astype(jnp.int32), jnp.reshape(chip, (1,)).astype(jnp.int32)
    place_arr = jnp.stack([chip, c]).astype(jnp.int32)

    first, rest = BIG[:1], BIG[1:]

    def as_weights(names, full):
        wb = {n: f.reshape(4, 2 * f.shape[2], f.shape[3]) for n, f in zip(names, full)}
        if "w_out" in wb:
            wb["w_out"] = wb["w_out"].reshape(D, D)
            wb["w_ffn_out"] = wb["w_ffn_out"].reshape(DFF, D)
        return wb

    def cast(names, l, tag):
        return _cast_into([w[n] for n in names], l, chip_arr, f"cast_{tag}")

    def landed(fly, names, after, tag):
        return as_weights(names, _gather_pass_on(_gather_wait(fly[0], fly[1], fly[2], after, tag), tag))

    tiny = _pack([w[n] for n in CHIP_SHARDED_SMALL], 8)
    _, tiny_full = _gather_weights([], tiny)
    fly_in = _gather_start(cast(first, 0, "in"), "in", after=(tiny_full,))
    fly0 = _gather_start(cast(rest, 0, "0"), "0", after=(fly_in[3],))
    fly1 = _gather_start(cast(BIG, 1, "1"), "1", after=(fly0[3],))
    p = {n: w[n] for n in SMALL}
    parts = [_unpack(tiny_full[k], [w[n].shape for n in CHIP_SHARDED_SMALL]) for k in range(4)]
    for i, n in enumerate(CHIP_SHARDED_SMALL):
        p[n] = jnp.concatenate([parts[k][i] for k in range(4)], axis=-1)

    operands = [_layer_operands(l, p) for l in range(2)]
    state_packs = [_pack([src[n] for n in SMALL], 32) for src in (w, mom, var)]
    ahead = tuple(jax.tree.leaves(operands)) + tuple(state_packs)

    passing = {}

    def pass_on_1(gu):
        bufs = _gather_wait(fly1[0], fly1[1], fly1[2], gu, "1")
        passing[1] = _exchange_start(bufs, [], _pass_on_copies, 3 * len(bufs), "gather_pass_on_start_1")
        return (passing[1][-1],)

    xa, saved0 = _forward_layer(0, x[0], p, landed(fly_in, first, (fly1[3],) + ahead, "in"), after=(fly0[3], fly1[3]),
                                rest=lambda merged: landed(fly0, rest, merged, "0"), near_end=pass_on_1, operands=operands[0])
    send, recv, bufs1, _, _ = passing[1]
    xb, saved1 = _forward_layer(
        1, xa, p, as_weights(BIG, _exchange_wait(send, recv, bufs1, [], xa, _pass_on_copies, "gather_pass_on_wait_1")[0]),
        operands=operands[1])
    dxb, loss_v, dfg = _loss_head(xb, loss_target[0], p["final_g"][None])
    loss = lax.psum(loss_v[0, 0], ("x", "y", "c"))

    out, flying = {}, {}

    def halves(grads):
        return [g.reshape(4, 2, -1, g.shape[-1]) for g in grads]

    def sibling_start(grads, names, l, tag):
        gs = halves(grads)
        lands = [lax.empty((4,) + g.shape[2:], g.dtype) for g in gs]
        flying["s" + tag] = (names, l) + tuple(
            _exchange_start(gs, lands, _sibling_copies, 4 * len(gs), f"grads_to_sibling_start_{tag}"))
        return (flying["s" + tag][-1],)

    def chips_start(gs, from_sib, names, l, tag):
        cs = _add_half(gs, from_sib, c_arr, f"add_half_{tag}")
        lands = [lax.empty((3,) + a.shape[1:], a.dtype) for a in cs]
        flying[tag] = (names, l) + tuple(_exchange_start(cs, lands, _chips_copies, 3 * len(cs), f"grads_to_chips_start_{tag}"))
        return (flying[tag][-1],)

    def sibling_finish(tag, after):
        names, l, send, recv, gs, lands, _ = flying["s" + tag]
        gs, from_sib = _exchange_wait(send, recv, gs, lands, after, _sibling_copies, f"grads_to_sibling_wait_{tag}")
        return chips_start(gs, from_sib, names, l, tag)

    def reduce_start(grads, names, l, tag):
        gs = halves(grads)
        return chips_start(gs, _to_sibling_halves(gs, tag), names, l, tag)

    def reduce_finish(tags, after):
        groups, ts = [], []
        for tag in tags:
            names, l, send, recv, cs, lands, _ = flying[tag]
            cs, lands = _exchange_wait(send, recv, cs, lands, after, _chips_copies, f"grads_to_chips_wait_{tag}")
            ts += _sum_chips(cs, lands, place_arr, f"sum_chips_{tag}")
            groups.append((tag, names))
        joined = _join_halves(ts, tags[0])
        for tag, names in groups:
            gs, joined = [j.reshape(w[n].shape[1:]) for n, j in zip(names, joined)], joined[len(names):]
            res = _adam_layer(gs, [w[n] for n in names], [mom[n] for n in names], [var[n] for n in names], l,
                              [out.get(n) for n in names], f"adam_{tag}")
            out.update(zip(names, res))

    def late1(grads, _):
        return sibling_finish("1a", grads[0]) + sibling_start(grads, first, 1, "1b")

    def midway0(grads):
        reduce_finish(("1a", "1b"), grads[0])
        return sibling_start(grads, rest, 0, "0a")

    def stacked_small(small0):
        small = {k: jnp.stack([small0[k], small1[k]]) for k in LAYER_SMALL}
        return dict(small, final_g=dfg[0])

    def late0(grads, small0):
        toks = reduce_start(grads, first, 0, "0b")
        small = stacked_small(dict(small0, norm1_g=jnp.zeros((D,), F32)))
        csum = _small_chip_sum(_pack([small[n] for n in SMALL], 32))
        flying["small"] = _exchange_start([csum], [lax.empty((3,) + csum.shape, BF16)], _small_copies, 3, "small_to_chips_start")
        return toks + (flying["small"][-1],)

    dxa, big1, small1 = _backward_layer(1, dxb, saved1, midway=lambda grads: sibling_start(grads, rest, 1, "1a"), late=late1)
    dx, big0, small0 = _backward_layer(0, dxa, saved0, after=sibling_finish("1b", dxa), midway=midway0,
                                       midway2=lambda dws: sibling_finish("0a", dws), late=late0)
    reduce_finish(("0a", "0b"), dx)
    small = stacked_small(small0)

    full_shapes = [small[n].shape for n in SMALL]
    send, recv, csum, land, _ = flying["small"]
    csum, land = _exchange_wait(send, recv, csum, land, out[first[0]][0], _small_copies, "small_to_chips_wait")
    red = _unpack(_small_total(csum[0], land[0]), full_shapes)
    norm1_0 = _allreduce_small(_pack([small0["norm1_g"]], 32))[:D // 128].reshape(D)
    red[SMALL.index("norm1_g")] = red[SMALL.index("norm1_g")].at[0].set(norm1_0)
    g_small = []
    for n, g in zip(SMALL, red):
        if n in CHIP_SHARDED_SMALL:
            g = lax.dynamic_slice_in_dim(g, chip * w[n].shape[-1], w[n].shape[-1], axis=g.ndim - 1)
        g_small.append(g)
    shapes = [w[n].shape for n in SMALL]
    upd = [_unpack(u, shapes) for u in _adam(_pack(g_small, 32), *state_packs, "adam_small")]
    for i, n in enumerate(SMALL):
        out[n] = [g_small[i], upd[0][i], upd[1][i], upd[2][i]]

    return (loss, dx[None]) + tuple(out[n][i] for i in range(4) for n in WEIGHTS)
```

```python
import functools

import jax
import jax.numpy as jnp
from jax import lax
from jax.experimental import pallas as pl
from jax.experimental.pallas import tpu as pltpu

F32 = jnp.float32
BF16 = jnp.bfloat16
MESH = pl.DeviceIdType.MESH

D = 1024
NH = 8
HD = 128
CHUNK = 128
N_IN_T = 12
DFF = 2816
DFF_SH = 1408
EPS = 1e-6
LRU_C = 8.0
ADAM_LR, ADAM_B1, ADAM_B2, ADAM_EPS, ADAM_WD, ADAM_STEP = 0.001, 0.9, 0.999, 1e-08, 0.01, 10

TM = 512
TM_BIG = 1024
RT = 128
PADR = 8
VMEM_LIMIT = 56 * 1024 * 1024


def _cp(sem=None, **kw):
    if sem is not None:
        kw["dimension_semantics"] = sem
    return pltpu.CompilerParams(vmem_limit_bytes=VMEM_LIMIT, **kw)


_GC = 0.7978845608028654


def _sigmoid(x):
    return 0.5 * jnp.tanh(0.5 * x) + 0.5


_GK = 0.044715


def _gelu(x):
    t = jnp.tanh(x * (_GC + (_GC * _GK) * (x * x)))
    return x * (0.5 + 0.5 * t)


def _gelu_and_grad(x):
    x2 = x * x
    t = jnp.tanh(x * (_GC + (_GC * _GK) * x2))
    h = 0.5 + 0.5 * t
    return x * h, h + x * (1.0 - t * t) * (0.5 * _GC + (1.5 * _GC * _GK) * x2)


def _softplus_neg(lam):
    y = jnp.exp(-jnp.abs(lam))
    u = 1.0 + y
    l1p = jnp.where(u == 1.0, y, jnp.log(u) * y / (u - 1.0))
    return jnp.maximum(-lam, 0.0) + l1p


def _dot(a, b):
    return jnp.dot(a, b, preferred_element_type=F32)


def _dot_nt(a, b):
    return lax.dot_general(a, b, (((1,), (1,)), ((), ())), preferred_element_type=F32)


def _dot_tn(a, b):
    return lax.dot_general(a, b, (((0,), (0,)), ((), ())), preferred_element_type=F32)


def _rms_hat(x):
    r = lax.rsqrt(jnp.mean(x * x, axis=-1, keepdims=True) + EPS)
    return x * r, r


def _rms_bwd(dh, x, g):
    xh, r = _rms_hat(x)
    dxh = dh * g
    dx = r * (dxh - xh * jnp.mean(dxh * xh, axis=-1, keepdims=True))
    return dx, jnp.sum(dh * xh, axis=0, keepdims=True)


def _norm_into(x_ref, g_ref, h_ref):
    xh, _ = _rms_hat(x_ref[...])
    h_ref[...] = (xh * g_ref[...]).astype(BF16)


def _in_tile(j):
    m, hf = j // 2, j % 2
    orig = jnp.where(m < 2, m, jnp.where(m == 2, 4, jnp.where(m < 5, m - 1, 5)))
    t = orig * 2 + hf
    return t // 3, t % 3


ANY = pl.BlockSpec(memory_space=pl.ANY)


def _mm_in(x, g, w_in, l, after=()):
    S = x.shape[0]
    tm = min(2 * TM_BIG, S)

    def body(x_ref, g_ref, w0_ref, w1_ref, *rest):
        o_ref, h_ref = rest[-2:]

        @pl.when(pl.program_id(1) == 0)
        def _():
            _norm_into(x_ref, g_ref, h_ref)
        rp = min(TM, tm)
        for r0 in range(0, tm, rp):
            hv = h_ref[r0:r0 + rp, :]
            o_ref[r0:r0 + rp, 0:512] = _dot(hv, w0_ref[...]).astype(BF16)
            o_ref[r0:r0 + rp, 512:1024] = _dot(hv, w1_ref[...]).astype(BF16)

    def w_tile(hf):
        def w_map(i, m):
            sh, tl = _in_tile(2 * m + hf)
            return (sh, 0, tl)
        return pl.BlockSpec((None, D, 512), w_map)

    return pl.pallas_call(
        body, name=f"mm_in_{l}", grid=(S // tm, 6),
        in_specs=[pl.BlockSpec((tm, D), lambda i, m: (i, 0)), pl.BlockSpec((1, D), lambda i, m: (0, 0)),
                  w_tile(0), w_tile(1)] + [ANY] * len(after),
        out_specs=[pl.BlockSpec((None, tm, D), lambda i, m: (m, i, 0)), pl.BlockSpec((tm, D), lambda i, m: (i, 0))],
        out_shape=[jax.ShapeDtypeStruct((6, S, D), BF16), jax.ShapeDtypeStruct((S, D), BF16)],
        compiler_params=_cp(("parallel", "arbitrary")),
    )(x, g, w_in, w_in, *after)


def _mm_res(a, w, res, l, name, after=()):
    S, K = a.shape

    tm = TM

    def body(a_ref, w_ref, r_ref, *rest):
        rest[-1][...] = r_ref[...] + _dot(a_ref[...], w_ref[...])

    return pl.pallas_call(
        body, name=f"{name}_{l}", grid=(S // tm,),
        in_specs=[pl.BlockSpec((tm, K), lambda i: (i, 0)), pl.BlockSpec((K, D), lambda i: (0, 0)),
                  pl.BlockSpec((tm, D), lambda i: (i, 0))] + [ANY] * len(after),
        out_specs=pl.BlockSpec((tm, D), lambda i: (i, 0)),
        out_shape=jax.ShapeDtypeStruct((S, D), F32),
        compiler_params=_cp(("parallel",)),
    )(a, w, res, *after)


def _mm_ffn_in(x, g, w_fi, l):
    S = x.shape[0]

    tm = min(TM_BIG, S)

    def body(x_ref, g_ref, w_ref, gu_ref, ff_ref, h_ref):
        @pl.when(pl.program_id(1) == 0)
        def _():
            _norm_into(x_ref, g_ref, h_ref)
        for r0 in range(0, tm, TM):
            rows = slice(r0, r0 + TM)
            hv = h_ref[rows, :]
            ga = _dot(hv, w_ref[0])
            gb = _dot(hv, w_ref[1])
            sg = _sigmoid(ga)
            silu = ga * sg
            gu_ref[0, rows, :] = (gb * (sg + silu * (1.0 - sg))).astype(BF16)
            gu_ref[1, rows, :] = silu.astype(BF16)
            ff_ref[rows, :] = (silu * gb).astype(BF16)

    gu, ff, h = pl.pallas_call(
        body, name=f"mm_ffn_in_{l}", grid=(S // tm, 2),
        in_specs=[pl.BlockSpec((tm, D), lambda i, s: (i, 0)), pl.BlockSpec((1, D), lambda i, s: (0, 0)),
                  pl.BlockSpec((2, None, D, DFF_SH), lambda i, s: (0, s, 0, 0))],
        out_specs=[pl.BlockSpec((2, None, tm, DFF_SH), lambda i, s: (0, s, i, 0)),
                   pl.BlockSpec((tm, DFF_SH), lambda i, s: (i, s)),
                   pl.BlockSpec((tm, D), lambda i, s: (i, 0))],
        out_shape=[jax.ShapeDtypeStruct((2, 2, S, DFF_SH), BF16), jax.ShapeDtypeStruct((S, DFF), BF16),
                   jax.ShapeDtypeStruct((S, D), BF16)],
        compiler_params=_cp(("parallel", "arbitrary")),
    )(x, g, w_fi.reshape(2, 2, D, DFF_SH))
    return gu.reshape(4, S, DFF_SH), ff, h


def _gmlp_fwd(z6, ws_b, bs_b, lg, lb):
    S = z6.shape[1]

    def body(z_ref, ws_ref, bs_ref, lg_ref, lb_ref, o_ref, mix):
        gv = _gelu(z_ref[1].astype(F32))
        xc = gv - jnp.mean(gv, axis=-1, keepdims=True)
        rs = lax.rsqrt(jnp.mean(xc * xc, axis=-1, keepdims=True) + EPS)
        vb = (xc * rs * lg_ref[...] + lb_ref[...]).astype(BF16)
        for gi in range(NH):
            cs = slice(gi * HD, (gi + 1) * HD)
            mix[:, cs] = _dot(ws_ref[gi], vb[:, cs])
        o_ref[...] = (_sigmoid(z_ref[2].astype(F32)) * _gelu(z_ref[0].astype(F32)) * (mix[...] + bs_ref[...])).astype(BF16)

    return pl.pallas_call(
        body, name="gmlp_fwd", grid=(S // CHUNK,),
        in_specs=[pl.BlockSpec((3, CHUNK, D), lambda i: (0, i, 0)), pl.BlockSpec((NH, CHUNK, CHUNK), lambda i: (0, 0, 0)),
                  pl.BlockSpec((CHUNK, D), lambda i: (0, 0)), pl.BlockSpec((1, D), lambda i: (0, 0)),
                  pl.BlockSpec((1, D), lambda i: (0, 0))],
        out_specs=pl.BlockSpec((CHUNK, D), lambda i: (i, 0)),
        out_shape=jax.ShapeDtypeStruct((S, D), BF16),
        scratch_shapes=[pltpu.VMEM((CHUNK, D), F32)],
        compiler_params=_cp(("parallel",)),
    )(z6, ws_b, bs_b, lg, lb)


def _row_iota():
    return lax.broadcasted_iota(jnp.int32, (RT, HD), 0)


SUB = 8
UNROLL = 4
GRAD_ROWS = 512


def _scan_up(a, b, carry):
    row = lax.broadcasted_iota(jnp.int32, (SUB, HD), 0)
    masks = [(d, row >= d) for d in (1, 2, 4)]
    c = jnp.broadcast_to(carry, (SUB, HD))
    hs = []
    for j in range(RT // SUB):
        aj, bj = a[SUB * j:SUB * (j + 1)], b[SUB * j:SUB * (j + 1)]
        for d, m in masks:
            bj = bj + aj * jnp.where(m, pltpu.roll(bj, d, 0), 0.0)
            aj = aj * jnp.where(m, pltpu.roll(aj, d, 0), 1.0)
        h = bj + aj * c
        hs.append(h)
        c = jnp.broadcast_to(h[SUB - 1:SUB, :], (SUB, HD))
    return jnp.concatenate(hs, axis=0), hs[-1][SUB - 1:SUB, :]


def _scan_down(a, b, carry):
    row = lax.broadcasted_iota(jnp.int32, (SUB, HD), 0)
    masks = [(d, row < SUB - d) for d in (1, 2, 4)]
    c = jnp.broadcast_to(carry, (SUB, HD))
    hs = []
    for j in reversed(range(RT // SUB)):
        aj, bj = a[SUB * j:SUB * (j + 1)], b[SUB * j:SUB * (j + 1)]
        for d, m in masks:
            bj = bj + aj * jnp.where(m, pltpu.roll(bj, SUB - d, 0), 0.0)
            aj = aj * jnp.where(m, pltpu.roll(aj, SUB - d, 0), 1.0)
        h = bj + aj * c
        hs.append(h)
        c = jnp.broadcast_to(h[0:1, :], (SUB, HD))
    return jnp.concatenate(hs[::-1], axis=0), hs[-1][0:1, :]


def _decay(r, sp_d):
    log_a = -LRU_C * r * sp_d
    a = jnp.exp(log_a)
    return a, jnp.sqrt(jnp.maximum(-jnp.tanh(log_a) * (a * a + 1.0), 0.0))


def _lru_gates(xc, d, wr_ref, br_ref, wi_ref, bi_ref, sp):
    xb = xc.astype(BF16)
    r = _sigmoid(_dot(xb, wr_ref[d]) + br_ref[d:d + 1, :])
    i = _sigmoid(_dot(xb, wi_ref[d]) + bi_ref[d:d + 1, :])
    a, mult = _decay(r, sp[d:d + 1, :])
    return r, i, a, mult


def _shifted(win, k):
    w = RT + 2 * PADR
    v = win if k == 0 else pltpu.roll(win, (-k) % w, 0)
    return v[PADR:PADR + RT]


def _conv_taps(win):
    return [_shifted(win, k) for k in (-1, 0, 1, 2)]


def _fill_padded(dst, src_ref, S):
    zeros = jnp.zeros((PADR, HD), F32)
    dst[0:PADR, :] = zeros
    dst[PADR + S:2 * PADR + S, :] = zeros

    def cp(i, c):
        t0 = pl.multiple_of(i * RT, RT)
        dst[pl.ds(t0 + PADR, RT), :] = src_ref[pl.ds(t0, RT), :].astype(F32)
        return c
    lax.fori_loop(0, S // RT, cp, 0)


def _conv_fwd_all(zxp, xc_s, cw_ref, cb_ref, S):
    def cv(i, c):
        t0 = pl.multiple_of(i * RT, RT)
        xm1, x0, xp1, xp2 = _conv_taps(zxp[pl.ds(t0, RT + 2 * PADR), :])
        xc_s[pl.ds(t0, RT), :] = (cb_ref[...] + xm1 * cw_ref[0:1, :] + x0 * cw_ref[1:2, :]
                                  + xp1 * cw_ref[2:3, :] + xp2 * cw_ref[3:4, :])
        return c
    lax.fori_loop(0, S // RT, cv, 0)


def _lru_specs(S):
    head = lambda h: (0, h)
    return [pl.BlockSpec((4, HD), head), pl.BlockSpec((1, HD), head),
            pl.BlockSpec((2, None, HD, HD), lambda h: (0, h, 0, 0)), pl.BlockSpec((2, HD), head),
            pl.BlockSpec((2, None, HD, HD), lambda h: (0, h, 0, 0)), pl.BlockSpec((2, HD), head),
            pl.BlockSpec((2, HD), head)]


def _lru_fwd(z6, ya, cw, cb, wr, br, wi, bi, lam):
    S = z6.shape[1]
    nt = S // RT

    def body(z_ref, ya_ref, cw_ref, cb_ref, wr_ref, br_ref, wi_ref, bi_ref, lam_ref, mg_ref, h0_ref, h1_ref, zxp, xc_s):
        sp = _softplus_neg(lam_ref[...])
        _fill_padded(zxp, z_ref.at[0], S)
        _conv_fwd_all(zxp, xc_s, cw_ref, cb_ref, S)

        def scans(i, carry):
            cu, cd = carry
            for u in range(UNROLL):
                j = i * UNROLL + u
                ru = pl.ds(pl.multiple_of(j * RT, RT), RT)
                rd = pl.ds(pl.multiple_of((nt - 1 - j) * RT, RT), RT)
                xu, xd = xc_s[ru, :], xc_s[rd, :]
                _, gi, a, mult = _lru_gates(xu, 0, wr_ref, br_ref, wi_ref, bi_ref, sp)
                hu, cu = _scan_up(a, mult * gi * xu, cu)
                h0_ref[ru, :] = hu
                _, gi, a, mult = _lru_gates(xd, 1, wr_ref, br_ref, wi_ref, bi_ref, sp)
                hd, cd = _scan_down(a, mult * gi * xd, cd)
                h1_ref[rd, :] = hd
            return cu, cd
        z1 = jnp.zeros((1, HD), F32)
        lax.fori_loop(0, nt // UNROLL, scans, (z1, z1))

        def merge(i, c):
            rows = pl.ds(pl.multiple_of(i * RT, RT), RT)
            yb = (h0_ref[rows, :] + h1_ref[rows, :]) * _gelu(z_ref[1, rows, :].astype(F32))
            mg_ref[rows, :] = (ya_ref[rows, :].astype(F32) + _sigmoid(z_ref[2, rows, :].astype(F32)) * yb).astype(BF16)
            return c
        lax.fori_loop(0, nt, merge, 0)

    col = pl.BlockSpec((S, HD), lambda h: (0, h))
    return pl.pallas_call(
        body, name="lru_fwd", grid=(NH,),
        in_specs=[pl.BlockSpec((3, S, HD), lambda h: (1, 0, h)), col] + _lru_specs(S),
        out_specs=[col, col, col],
        out_shape=[jax.ShapeDtypeStruct((S, D), BF16), jax.ShapeDtypeStruct((S, D), F32), jax.ShapeDtypeStruct((S, D), F32)],
        scratch_shapes=[pltpu.VMEM((S + 2 * PADR, HD), F32), pltpu.VMEM((S, HD), F32)],
        compiler_params=_cp(("parallel",)),
    )(z6, ya, cw, cb, wr, br, wi, bi, lam)


def _loss_head(x, tgt, g):
    S = x.shape[0]

    def body(x_ref, t_ref, g_ref, dx_ref, loss_ref, dg_ref):
        @pl.when(pl.program_id(0) == 0)
        def _():
            loss_ref[...] = jnp.zeros_like(loss_ref)
            dg_ref[...] = jnp.zeros_like(dg_ref)
        xv = x_ref[...]
        xh, _ = _rms_hat(xv)
        e = xh * g_ref[...] - t_ref[...]
        loss_ref[...] += jnp.sum(e * e) * (0.5 / D)
        dx, dgs = _rms_bwd(e * (1.0 / D), xv, g_ref[...])
        dx_ref[...] = dx
        dg_ref[...] += dgs

    return pl.pallas_call(
        body, name="loss_head", grid=(S // TM,),
        in_specs=[pl.BlockSpec((TM, D), lambda i: (i, 0)), pl.BlockSpec((TM, D), lambda i: (i, 0)),
                  pl.BlockSpec((1, D), lambda i: (0, 0))],
        out_specs=[pl.BlockSpec((TM, D), lambda i: (i, 0)), pl.BlockSpec((1, 128), lambda i: (0, 0)),
                   pl.BlockSpec((1, D), lambda i: (0, 0))],
        out_shape=[jax.ShapeDtypeStruct((S, D), F32), jax.ShapeDtypeStruct((1, 128), F32), jax.ShapeDtypeStruct((1, D), F32)],
        compiler_params=_cp(("arbitrary",)),
    )(x, tgt, g)


def _bwd_ffn_out(dx, w_fo, gu, l, after=()):
    S = dx.shape[0]

    tm = min(TM_BIG, S)

    def body(dx_ref, w_ref, gu_ref, *rest):
        o_ref = rest[-1]
        for r0 in range(0, tm, TM):
            rows = slice(r0, r0 + TM)
            d = _dot_nt(dx_ref[rows, :].astype(BF16), w_ref[...])
            o_ref[0, rows, :] = (d * gu_ref[0, rows, :].astype(F32)).astype(BF16)
            o_ref[1, rows, :] = (d * gu_ref[1, rows, :].astype(F32)).astype(BF16)

    pair = pl.BlockSpec((2, None, tm, DFF_SH), lambda i, s: (0, s, i, 0))
    dgu = pl.pallas_call(
        body, name=f"bwd_ffn_out_{l}", grid=(S // tm, 2),
        in_specs=[pl.BlockSpec((tm, D), lambda i, s: (i, 0)), pl.BlockSpec((DFF_SH, D), lambda i, s: (s, 0)), pair]
        + [ANY] * len(after),
        out_specs=pair,
        out_shape=jax.ShapeDtypeStruct((2, 2, S, DFF_SH), BF16),
        compiler_params=_cp(("parallel", "arbitrary")),
    )(dx, w_fo, gu.reshape(2, 2, S, DFF_SH), *after)
    return dgu.reshape(4, S, DFF_SH)


def _mm_tn(a, b, m_blk, tk, name):
    S, M = a.shape

    def body(a_ref, b_ref, o_ref):
        @pl.when(pl.program_id(1) == 0)
        def _():
            o_ref[...] = jnp.zeros_like(o_ref)
        o_ref[...] += _dot_tn(a_ref[...], b_ref[...].astype(BF16))

    return pl.pallas_call(
        body, name=name, grid=(M // m_blk, S // tk),
        in_specs=[pl.BlockSpec((tk, m_blk), lambda m, k: (k, m)), pl.BlockSpec((tk, D), lambda m, k: (k, 0))],
        out_specs=pl.BlockSpec((m_blk, D), lambda m, k: (m, 0)),
        out_shape=jax.ShapeDtypeStruct((M, D), F32),
        compiler_params=_cp(("parallel", "arbitrary")),
    )(a, b)


def _mm_nt_rms_bwd(a, a_specs, w, w_specs, nk, tm, x, g, dres, name, after=()):
    S = x.shape[0]
    sub = len(a_specs)

    def body(*refs):
        a_refs, w_refs = refs[:sub], refs[sub:2 * sub]
        x_ref, g_ref, r_ref = refs[2 * sub:2 * sub + 3]
        dx_ref, dg_ref, acc = refs[-3:]
        i, k = pl.program_id(0), pl.program_id(1)
        @pl.when(k == 0)
        def _():
            acc[...] = jnp.zeros_like(acc)
        for j in range(sub):
            acc[...] += _dot_nt(a_refs[j][...], w_refs[j][...])

        @pl.when(jnp.logical_and(i == 0, k == 0))
        def _():
            dg_ref[...] = jnp.zeros_like(dg_ref)

        @pl.when(k == nk - 1)
        def _():
            dx, dgs = _rms_bwd(acc[...], x_ref[...], g_ref[...])
            dx_ref[...] = r_ref[...] + dx
            dg_ref[...] += dgs

    row = pl.BlockSpec((tm, D), lambda i, k: (i, 0))
    vec = pl.BlockSpec((1, D), lambda i, k: (0, 0))
    return pl.pallas_call(
        body, name=name, grid=(S // tm, nk),
        in_specs=list(a_specs) + list(w_specs) + [row, vec, row] + [ANY] * len(after),
        out_specs=[row, vec],
        out_shape=[jax.ShapeDtypeStruct((S, D), F32), jax.ShapeDtypeStruct((1, D), F32)],
        scratch_shapes=[pltpu.VMEM((tm, D), F32)],
        compiler_params=_cp(("arbitrary", "arbitrary")),
    )(*[a] * sub, *[w] * sub, x, g, dres, *after)


def _dw_ffn_in(h, dgu, l):
    S = h.shape[0]

    def body(h_ref, b_ref, o_ref):
        @pl.when(pl.program_id(1) == 0)
        def _():
            o_ref[...] = jnp.zeros_like(o_ref)
        o_ref[...] += _dot_tn(h_ref[...], b_ref[...])

    tk = min(2 * TM_BIG, S)
    return pl.pallas_call(
        body, name=f"dw_ffn_in_{l}", grid=(4, S // tk),
        in_specs=[pl.BlockSpec((tk, D), lambda j, k: (k, 0)), pl.BlockSpec((None, tk, DFF_SH), lambda j, k: (j, k, 0))],
        out_specs=pl.BlockSpec((None, D, DFF_SH), lambda j, k: (j, 0, 0)),
        out_shape=jax.ShapeDtypeStruct((4, D, DFF_SH), F32),
        compiler_params=_cp(("parallel", "arbitrary")),
    )(h, dgu)


_HALF_COMPS = ((0, 1, 3), (4, 2, 5))


def _dw_in(h, dz6, l):
    S = h.shape[0]

    def body(h_ref, d0_ref, d1_ref, d2_ref, o_ref):
        @pl.when(pl.program_id(1) == 0)
        def _():
            o_ref[...] = jnp.zeros_like(o_ref)
        hv = h_ref[...]
        for q, d_ref in enumerate((d0_ref, d1_ref, d2_ref)):
            for hf in range(2):
                col = 1024 * q + 512 * hf
                o_ref[col // 1536, :, col % 1536:col % 1536 + 512] += _dot_tn(hv, d_ref[:, 512 * hf:512 * (hf + 1)])

    tk = min(TM_BIG, S)

    def comp(q):
        return pl.BlockSpec((None, tk, D), lambda p, k: (jnp.where(p == 0, _HALF_COMPS[0][q], _HALF_COMPS[1][q]), k, 0))

    return pl.pallas_call(
        body, name=f"dw_in_{l}", grid=(2, S // tk),
        in_specs=[pl.BlockSpec((tk, D), lambda p, k: (k, 0)), comp(0), comp(1), comp(2)],
        out_specs=pl.BlockSpec((2, D, 1536), lambda p, k: (p, 0, 0)),
        out_shape=jax.ShapeDtypeStruct((4, D, 1536), F32),
        compiler_params=_cp(("parallel", "arbitrary")),
    )(h, dz6, dz6, dz6)


def _bwd_out(dx, w_o, merged, l):
    S = dx.shape[0]

    def body(dx_ref, w_ref, m_ref, dm_ref, dw_ref):
        @pl.when(pl.program_id(0) == 0)
        def _():
            dw_ref[...] = jnp.zeros_like(dw_ref)
        dxb = dx_ref[...].astype(BF16)
        dm_ref[...] = _dot_nt(dxb, w_ref[...]).astype(BF16)
        dw_ref[...] += _dot_tn(m_ref[...], dxb)

    tm = TM
    row = pl.BlockSpec((tm, D), lambda i: (i, 0))
    return pl.pallas_call(
        body, name=f"bwd_out_{l}", grid=(S // tm,),
        in_specs=[row, pl.BlockSpec((D, D), lambda i: (0, 0)), row],
        out_specs=[row, pl.BlockSpec((D, D), lambda i: (0, 0))],
        out_shape=[jax.ShapeDtypeStruct((S, D), BF16), jax.ShapeDtypeStruct((D, D), F32)],
        compiler_params=_cp(("arbitrary",)),
    )(dx, w_o, merged)


def _gmlp_bwd(dm, z6, ws_b, wst_b, bs_b, lg, lb, after=()):
    S = z6.shape[1]

    def body(dm_ref, z_ref, ws_ref, wst_ref, bs_ref, lg_ref, lb_ref, *rest):
        dz_ref, dws_ref, dbs_ref, dlg_ref, dlb_ref, mix, dv = rest[-7:]

        @pl.when(pl.program_id(0) == 0)
        def _():
            dws_ref[...] = jnp.zeros_like(dws_ref)
            dbs_ref[...] = jnp.zeros_like(dbs_ref)
            dlg_ref[...] = jnp.zeros_like(dlg_ref)
            dlb_ref[...] = jnp.zeros_like(dlb_ref)
        gv, dgelu_v = _gelu_and_grad(z_ref[1].astype(F32))
        xc = gv - jnp.mean(gv, axis=-1, keepdims=True)
        rs = lax.rsqrt(jnp.mean(xc * xc, axis=-1, keepdims=True) + EPS)
        vh = xc * rs
        vb = (vh * lg_ref[...] + lb_ref[...]).astype(BF16)
        for gi in range(NH):
            cs = slice(gi * HD, (gi + 1) * HD)
            mix[:, cs] = _dot(ws_ref[gi], vb[:, cs])
        u, dgelu_u = _gelu_and_grad(z_ref[0].astype(F32))
        sa = _sigmoid(z_ref[2].astype(F32))
        mixed = mix[...] + bs_ref[...]
        dyg = dm_ref[...].astype(F32)
        dz_ref[2] = (dyg * u * mixed * sa * (1.0 - sa)).astype(BF16)
        dya = dyg * sa
        dz_ref[0] = (dya * mixed * dgelu_u).astype(BF16)
        dmix = dya * u
        dmb = dmix.astype(BF16)
        for gi in range(NH):
            cs = slice(gi * HD, (gi + 1) * HD)
            dv[:, cs] = _dot(wst_ref[gi], dmb[:, cs])
            dws_ref[gi] += _dot_nt(dmb[:, cs], vb[:, cs])
            dbs_ref[gi] += jnp.broadcast_to(jnp.sum(dmix[:, cs], axis=1, keepdims=True), (CHUNK, HD))
        dvv = dv[...]
        dlg_ref[...] += jnp.sum(dvv * vh, axis=0, keepdims=True)
        dlb_ref[...] += jnp.sum(dvv, axis=0, keepdims=True)
        dvh = dvv * lg_ref[...]
        dgv = rs * (dvh - jnp.mean(dvh, axis=-1, keepdims=True) - vh * jnp.mean(dvh * vh, axis=-1, keepdims=True))
        dz_ref[1] = (dgv * dgelu_v).astype(BF16)

    vec = pl.BlockSpec((1, D), lambda i: (0, 0))
    mat = pl.BlockSpec((NH, CHUNK, CHUNK), lambda i: (0, 0, 0))
    return pl.pallas_call(
        body, name="gmlp_bwd", grid=(S // CHUNK,),
        in_specs=[pl.BlockSpec((CHUNK, D), lambda i: (i, 0)), pl.BlockSpec((3, CHUNK, D), lambda i: (0, i, 0)), mat, mat,
                  pl.BlockSpec((CHUNK, D), lambda i: (0, 0)), vec, vec] + [ANY] * len(after),
        out_specs=[pl.BlockSpec((3, CHUNK, D), lambda i: (0, i, 0)), mat, mat, vec, vec],
        out_shape=[jax.ShapeDtypeStruct((6, S, D), BF16), jax.ShapeDtypeStruct((NH, CHUNK, CHUNK), F32),
                   jax.ShapeDtypeStruct((NH, CHUNK, HD), F32), jax.ShapeDtypeStruct((1, D), F32), jax.ShapeDtypeStruct((1, D), F32)],
        scratch_shapes=[pltpu.VMEM((CHUNK, D), F32), pltpu.VMEM((CHUNK, D), F32)],
        compiler_params=_cp(("arbitrary",)),
    )(dm, z6, ws_b, wst_b, bs_b, lg, lb, *after)


def _lru_bwd(dz6, dm, z6, h0, h1, cw, cb, wr, br, wi, bi, lam, after=()):
    S = z6.shape[1]
    nt = S // RT

    def body(dz_in, dm_ref, z_ref, h0_ref, h1_ref, cw_ref, cb_ref, wr_ref, br_ref, wi_ref, bi_ref, lam_ref, *rest):
        dz_ref, dcw_ref, dcb_ref, dwr_ref, dbr_ref, dwi_ref, dbi_ref, dlam_ref, zxp, xc_s, dhs_s, dxcp, r_s, lam_s = rest[-14:]
        del dz_in
        lam = lam_ref[...]
        sp = _softplus_neg(lam)
        row = _row_iota()
        _fill_padded(zxp, z_ref.at[0], S)
        _conv_fwd_all(zxp, xc_s, cw_ref, cb_ref, S)
        zeros = jnp.zeros((PADR, HD), F32)
        dxcp[0:PADR, :] = zeros
        dxcp[PADR + S:2 * PADR + S, :] = zeros
        dwr_ref[...] = jnp.zeros_like(dwr_ref)
        dwi_ref[...] = jnp.zeros_like(dwi_ref)

        def pre(i, c):
            rows = pl.ds(pl.multiple_of(i * RT, RT), RT)
            hs = h0_ref[rows, :] + h1_ref[rows, :]
            dmv = dm_ref[rows, :].astype(F32)
            sb = _sigmoid(z_ref[2, rows, :].astype(F32))
            gg, dgg = _gelu_and_grad(z_ref[1, rows, :].astype(F32))
            dz_ref[2, rows, :] = (dmv * hs * gg * sb * (1.0 - sb)).astype(BF16)
            dyb = dmv * sb
            dz_ref[1, rows, :] = (dyb * hs * dgg).astype(BF16)
            dhs_s[rows, :] = dyb * gg
            return c
        lax.fori_loop(0, nt, pre, 0)

        def gate_bwd(d, gates, lamv, da, xc):
            r, gi, a, mult = gates
            dmult = lamv * gi * xc
            dgi = lamv * mult * xc
            dlog = (da - dmult * a / mult) * a
            dpr = (dlog * (-LRU_C) * sp[d:d + 1, :]) * r * (1.0 - r)
            dpi = dgi * gi * (1.0 - gi)
            xb, dprb, dpib = xc.astype(BF16), dpr.astype(BF16), dpi.astype(BF16)
            dwr_ref[d] += _dot_tn(xb, dprb)
            dwi_ref[d] += _dot_tn(xb, dpib)
            dxc = lamv * mult * gi + _dot_nt(dprb, wr_ref[d]) + _dot_nt(dpib, wi_ref[d])
            return dxc, (jnp.sum(dlog * r, axis=0, keepdims=True) * (-LRU_C), jnp.sum(dpr, axis=0, keepdims=True),
                         jnp.sum(dpi, axis=0, keepdims=True))

        def rgates(i, c):
            for u in range(UNROLL):
                rows = pl.ds(pl.multiple_of((i * UNROLL + u) * RT, RT), RT)
                xb = xc_s[rows, :].astype(BF16)
                for d in range(2):
                    r_s[d, rows, :] = _sigmoid(_dot(xb, wr_ref[d]) + br_ref[d:d + 1, :])
            return c
        lax.fori_loop(0, nt // UNROLL, rgates, 0)

        def chains(i, carry):
            qn, qp = carry
            for u in range(UNROLL):
                j = i * UNROLL + u
                rd = pl.ds(pl.multiple_of((nt - 1 - j) * RT, RT), RT)
                a, dhs = _decay(r_s[0, rd, :], sp[0:1, :])[0], dhs_s[rd, :]
                q, q_first = _scan_down(a, a * dhs, qn)
                lam_s[0, rd, :] = dhs + jnp.where(row == RT - 1, qn, pltpu.roll(q, RT - 1, 0))
                qn = q_first
                ru = pl.ds(pl.multiple_of(j * RT, RT), RT)
                a, dhs = _decay(r_s[1, ru, :], sp[1:2, :])[0], dhs_s[ru, :]
                q, q_last = _scan_up(a, a * dhs, qp)
                lam_s[1, ru, :] = dhs + jnp.where(row == 0, qp, pltpu.roll(q, 1, 0))
                qp = q_last
            return qn, qp

        z1 = jnp.zeros((1, HD), F32)
        lax.fori_loop(0, nt // UNROLL, chains, (z1, z1))

        ct = min(GRAD_ROWS, S)
        crow = lax.broadcasted_iota(jnp.int32, (ct, HD), 0)

        def tile_grads(i, acc):
            t0 = pl.multiple_of(i * ct, ct)
            rows = pl.ds(t0, ct)
            xc = xc_s[rows, :]
            xb = xc.astype(BF16)
            tp = pl.multiple_of(jnp.maximum(t0 - PADR, 0), PADR)
            prev = jnp.where(t0 > 0, h0_ref[pl.ds(tp, PADR), :][PADR - 1:PADR, :], 0.0)
            tn = pl.multiple_of(jnp.minimum(t0 + ct, S - PADR), PADR)
            nxt = jnp.where(t0 + ct < S, h1_ref[pl.ds(tn, PADR), :][0:1, :], 0.0)
            hside = (jnp.where(crow == 0, prev, pltpu.roll(h0_ref[rows, :], 1, 0)),
                     jnp.where(crow == ct - 1, nxt, pltpu.roll(h1_ref[rows, :], ct - 1, 0)))
            dxc, sums = 0.0, ()
            for d in range(2):
                r = r_s[d, rows, :]
                gi = _sigmoid(_dot(xb, wi_ref[d]) + bi_ref[d:d + 1, :])
                a, mult = _decay(r, sp[d:d + 1, :])
                lamv = lam_s[d, rows, :]
                dxc_d, s_d = gate_bwd(d, (r, gi, a, mult), lamv, lamv * hside[d], xc)
                dxc = dxc + dxc_d
                sums = sums + s_d
            dxcp[pl.ds(t0 + PADR, ct), :] = dxc
            return tuple(x + y for x, y in zip(acc, sums))

        s_sp0, s_br0, s_bi0, s_sp1, s_br1, s_bi1 = lax.fori_loop(0, S // ct, tile_grads, (z1,) * 6)

        dsp = jnp.concatenate([s_sp0, s_sp1], axis=0)
        dlam_ref[...] = -dsp * _sigmoid(-lam)
        dbr_ref[...] = jnp.concatenate([s_br0, s_br1], axis=0)
        dbi_ref[...] = jnp.concatenate([s_bi0, s_bi1], axis=0)

        def conv_bwd(i, carry):
            c0, c1, c2, c3, cb_ = carry
            t0 = pl.multiple_of(i * RT, RT)
            dwin = dxcp[pl.ds(t0, RT + 2 * PADR), :]
            d0 = _shifted(dwin, 0)
            dz_ref[0, pl.ds(t0, RT), :] = (_shifted(dwin, 1) * cw_ref[0:1, :] + d0 * cw_ref[1:2, :]
                                           + _shifted(dwin, -1) * cw_ref[2:3, :] + _shifted(dwin, -2) * cw_ref[3:4, :]).astype(BF16)
            xm1, x0, xp1, xp2 = _conv_taps(zxp[pl.ds(t0, RT + 2 * PADR), :])
            sm = lambda v: jnp.sum(v, axis=0, keepdims=True)
            return c0 + sm(d0 * xm1), c1 + sm(d0 * x0), c2 + sm(d0 * xp1), c3 + sm(d0 * xp2), cb_ + sm(d0)

        c0, c1, c2, c3, cb_ = lax.fori_loop(0, nt, conv_bwd, (z1, z1, z1, z1, z1))
        dcw_ref[...] = jnp.concatenate([c0, c1, c2, c3], axis=0)
        dcb_ref[...] = cb_

    col = pl.BlockSpec((S, HD), lambda h: (0, h))
    head = lambda h: (0, h)
    wspec = pl.BlockSpec((2, None, HD, HD), lambda h: (0, h, 0, 0))
    return pl.pallas_call(
        body, name="lru_bwd", grid=(NH,),
        in_specs=[pl.BlockSpec(memory_space=pl.ANY), col, pl.BlockSpec((3, S, HD), lambda h: (1, 0, h)), col, col] + _lru_specs(S)
        + [ANY] * len(after),
        out_specs=[pl.BlockSpec((3, S, HD), lambda h: (1, 0, h)), pl.BlockSpec((4, HD), head), pl.BlockSpec((1, HD), head),
                   wspec, pl.BlockSpec((2, HD), head), wspec, pl.BlockSpec((2, HD), head), pl.BlockSpec((2, HD), head)],
        out_shape=[jax.ShapeDtypeStruct((6, S, D), BF16), jax.ShapeDtypeStruct((4, D), F32), jax.ShapeDtypeStruct((1, D), F32),
                   jax.ShapeDtypeStruct((2, NH, HD, HD), F32), jax.ShapeDtypeStruct((2, D), F32),
                   jax.ShapeDtypeStruct((2, NH, HD, HD), F32), jax.ShapeDtypeStruct((2, D), F32), jax.ShapeDtypeStruct((2, D), F32)],
        scratch_shapes=[pltpu.VMEM((S + 2 * PADR, HD), F32), pltpu.VMEM((S, HD), F32), pltpu.VMEM((S, HD), F32),
                        pltpu.VMEM((S + 2 * PADR, HD), F32), pltpu.VMEM((2, S, HD), F32), pltpu.VMEM((2, S, HD), F32)],
        input_output_aliases={0: 0},
        compiler_params=_cp(("parallel",)),
    )(dz6, dm, z6, h0, h1, cw, cb, wr, br, wi, bi, lam, *after)


LAYER_SMALL = ("norm1_g", "gmlp_ln_g", "gmlp_ln_b", "gmlp_w_s", "gmlp_b_s", "conv_w", "conv_b",
               "lru_w_r", "lru_b_r", "lru_w_i", "lru_b_i", "lru_lambda", "norm2_g")


def _layer_operands(l, p):
    ws_b = p["gmlp_w_s"][l].astype(BF16)
    tm = dict(ws_b=ws_b, wst_b=jnp.swapaxes(ws_b, 1, 2), bs_b=jnp.repeat(p["gmlp_b_s"][l].T, HD, axis=1),
              lg=p["gmlp_ln_g"][l][None], lb=p["gmlp_ln_b"][l][None])
    lru = (p["conv_w"][l], p["conv_b"][l][None], p["lru_w_r"][l].astype(BF16), p["lru_b_r"][l],
           p["lru_w_i"][l].astype(BF16), p["lru_b_i"][l], p["lru_lambda"][l])
    return (p["norm1_g"][l][None], p["norm2_g"][l][None]), tm, lru


def _forward_layer(l, x, p, wb, after=(), rest=None, near_end=None, operands=None):
    (g1, g2), tm, lru = _layer_operands(l, p) if operands is None else operands
    z6, hn1 = _mm_in(x, g1, wb["w_in"], l, after)
    ya = _gmlp_fwd(z6, tm["ws_b"], tm["bs_b"], tm["lg"], tm["lb"])
    merged, h0, h1 = _lru_fwd(z6, ya, *lru)
    if rest is not None:
        wb = dict(wb, **rest(merged))
    x1 = _mm_res(merged, wb["w_out"], x, l, "mm_out")
    gu, ff, hn2 = _mm_ffn_in(x1, g2, wb["w_ffn_in"], l)
    x2 = _mm_res(ff, wb["w_ffn_out"], x1, l, "mm_ffn_out", () if near_end is None else tuple(near_end(gu)))
    return x2, dict(x=x, z6=z6, h0=h0, h1=h1, merged=merged, x1=x1, gu=gu, ff=ff, g1=g1, g2=g2, tm=tm, lru=lru,
                    hn1=hn1, hn2=hn2, wb=wb)


def _backward_layer(l, dx, s, after=(), midway=None, midway2=None, late=None):
    S = dx.shape[0]
    tm, wb = s["tm"], s["wb"]
    g2 = s["g2"]
    dgu = _bwd_ffn_out(dx, wb["w_ffn_out"], s["gu"], l, after)
    tmb = min(TM_BIG, S)
    dwfo = _mm_tn(s["ff"], dx, DFF_SH, tmb, f"dw_ffn_out_{l}")
    dx1, dg2 = _mm_nt_rms_bwd(
        dgu, [pl.BlockSpec((None, tmb, DFF_SH), lambda i, k: (k, i, 0))],
        wb["w_ffn_in"], [pl.BlockSpec((None, D, DFF_SH), lambda i, k: (k, 0, 0))],
        4, tmb, s["x1"], g2, dx, f"bwd_ffn_in_{l}")
    dwfi = _dw_ffn_in(s["hn2"], dgu, l)
    dmg, dwo = _bwd_out(dx1, wb["w_out"], s["merged"], l)
    mid = () if midway is None else tuple(midway([dwo, dwfi, dwfo]))
    dz6, dws, dbs, dlg, dlb = _gmlp_bwd(dmg, s["z6"], tm["ws_b"], tm["wst_b"], tm["bs_b"], tm["lg"], tm["lb"], mid)
    mid2 = () if midway2 is None else tuple(midway2(dws))
    dz6, dcw, dcb, dwr, dbr, dwi, dbi, dlam = _lru_bwd(dz6, dmg, s["z6"], s["h0"], s["h1"], *s["lru"], after=mid2)

    sub = 3

    def dz_tile(j):
        return pl.BlockSpec((None, tmb, 512), lambda i, k: ((sub * k + j) // 2, i, (sub * k + j) % 2))

    def w_tile(j):
        def w_map(i, k):
            sh, tl = _in_tile(sub * k + j)
            return (sh, 0, tl)
        return pl.BlockSpec((None, D, 512), w_map)

    dwin = _dw_in(s["hn1"], dz6, l)
    small = dict(gmlp_ln_g=dlg[0], gmlp_ln_b=dlb[0], gmlp_w_s=dws, gmlp_b_s=dbs[:, :, 0], conv_w=dcw, conv_b=dcb[0],
                 lru_w_r=dwr, lru_b_r=dbr, lru_w_i=dwi, lru_b_i=dbi, lru_lambda=dlam, norm2_g=dg2[0])
    tail = () if late is None else tuple(late([dwin], small))
    dx0, dg1 = _mm_nt_rms_bwd(
        dz6, [dz_tile(j) for j in range(sub)], wb["w_in"], [w_tile(j) for j in range(sub)],
        N_IN_T // sub, tmb, s["x"], s["g1"], dx1, f"bwd_in_{l}", tail)
    return dx0, [dwin, dwo, dwfi, dwfo], dict(small, norm1_g=dg1[0])


def _local_step(x, tgt, p, wbs):
    saved = []
    for l in range(2):
        x, s = _forward_layer(l, x, p, wbs[l])
        saved.append(s)
    dx, loss_v, dfg = _loss_head(x, tgt, p["final_g"][None])
    big, smalls = [None, None], [None, None]
    for l in (1, 0):
        dx, big[l], smalls[l] = _backward_layer(l, dx, saved[l])
    small = {k: jnp.stack([smalls[0][k], smalls[1][k]]) for k in LAYER_SMALL}
    small["final_g"] = dfg[0]
    return loss_v, dx, big, small


def _place():
    x, y, c = lax.axis_index("x"), lax.axis_index("y"), lax.axis_index("c")
    return x, y, c, 2 * x + y


def _chip_at(x, y, d):
    px = 1 - x if d & 2 else x
    py = 1 - y if d & 1 else y
    return px, py, 2 * px + py


HBM = pl.BlockSpec(memory_space=pltpu.HBM)
SEM = pl.BlockSpec(memory_space=pltpu.SEMAPHORE)
DATAFLOW = pltpu.SideEffectType.DATAFLOW_SIDE_EFFECTING


def _in_hbm(a):
    return pltpu.with_memory_space_constraint(a, pltpu.HBM)


def _cast_into(wfs, l, chip_arr, name):
    n = len(wfs)

    def body(ch_ref, *refs):
        for w_ref, o_ref in zip(refs[:n], refs[n:]):
            o_ref[...] = w_ref[...].astype(BF16)

    halves = [(wf.shape[1] // 2, wf.shape[2]) for wf in wfs]
    return pl.pallas_call(
        body, name=name, out_shape=[jax.ShapeDtypeStruct((4, 2, rh, cols), BF16) for rh, cols in halves],
        grid_spec=pltpu.PrefetchScalarGridSpec(
            num_scalar_prefetch=1, grid=(2,),
            in_specs=[pl.BlockSpec((None, None, rh, cols), lambda h, ch: (l, h, 0, 0)) for rh, cols in halves],
            out_specs=[pl.BlockSpec((None, None, rh, cols), lambda h, ch: (ch[0], h, 0, 0)) for rh, cols in halves]),
        compiler_params=_cp(("parallel",)),
    )(chip_arr, *[wf.reshape(2, 2, rh, cols) for wf, (rh, cols) in zip(wfs, halves)])


def _half_block(ref, chip, half, to, send_sem, recv_sem):
    blk = ref.at[chip, half]
    return pltpu.make_async_remote_copy(src_ref=blk, dst_ref=blk, send_sem=send_sem, recv_sem=recv_sem,
                                        device_id=to, device_id_type=MESH)


def _gather_weights(bufs, tiny):
    nt = len(bufs)
    n_ici = max(nt * 3, 1)

    def body(*refs):
        tiny_ref = refs[nt]
        o_refs, tiny_o = refs[nt + 1:2 * nt + 1], refs[2 * nt + 1]
        send, recv, fsend, frecv, tsend, trecv, lsem = refs[2 * nt + 2:]
        x, y, c, chip = _place()
        local = pltpu.make_async_copy(tiny_ref, tiny_o.at[chip], lsem)
        local.start()

        def tin(d, origin_chip, to):
            return pltpu.make_async_remote_copy(
                src_ref=tiny_ref, dst_ref=tiny_o.at[origin_chip], send_sem=tsend.at[d - 1], recv_sem=trecv.at[d - 1],
                device_id=to, device_id_type=MESH)

        sends = []
        for t in range(nt):
            for d in (1, 2, 3):
                px, py, _ = _chip_at(x, y, d)
                sends.append(_half_block(o_refs[t], chip, c, (px, py, c), send.at[3 * t + d - 1], recv.at[3 * t + d - 1]))
        for d in (1, 2, 3):
            px, py, _ = _chip_at(x, y, d)
            sends.append(tin(d, chip, (px, py, c)))
        for cp in sends:
            cp.start()
        passed = []
        for t in range(nt):
            for d in (1, 2, 3):
                k = 3 * t + d - 1
                _, _, pchip = _chip_at(x, y, d)
                _half_block(o_refs[t], pchip, c, (x, y, c), send.at[k], recv.at[k]).wait_recv()
                f = _half_block(o_refs[t], pchip, c, (x, y, 1 - c), fsend.at[k], frecv.at[k])
                f.start()
                passed.append(f)
        for t in range(nt):
            for d in (1, 2, 3):
                k = 3 * t + d - 1
                _, _, pchip = _chip_at(x, y, d)
                _half_block(o_refs[t], pchip, 1 - c, (x, y, 1 - c), fsend.at[k], frecv.at[k]).wait_recv()
        for d in (1, 2, 3):
            _, _, pchip = _chip_at(x, y, d)
            tin(d, pchip, (x, y, c)).wait_recv()
        for cp in sends + passed:
            cp.wait_send()
        local.wait()

    out_shape = [jax.ShapeDtypeStruct(b.shape, b.dtype) for b in bufs]
    out_shape.append(jax.ShapeDtypeStruct((4,) + tiny.shape, tiny.dtype))
    outs = pl.pallas_call(
        body, name="gather_weights_0", out_shape=out_shape,
        in_specs=[ANY] * (nt + 1), out_specs=[ANY] * (nt + 1),
        scratch_shapes=[pltpu.SemaphoreType.DMA((n_ici,)), pltpu.SemaphoreType.DMA((n_ici,)),
                        pltpu.SemaphoreType.DMA((n_ici,)), pltpu.SemaphoreType.DMA((n_ici,)),
                        pltpu.SemaphoreType.DMA((3,)), pltpu.SemaphoreType.DMA((3,)), pltpu.SemaphoreType.DMA],
        input_output_aliases={t: t for t in range(nt)},
        compiler_params=_cp(has_side_effects=True),
    )(*bufs, tiny)
    return outs[:nt], outs[nt]


def _gather_start(bufs, tag, after=()):
    nt, na = len(bufs), len(after)

    def body(*refs):
        b_refs = refs[:nt]
        send, recv = refs[nt + na], refs[nt + na + 1]
        token = refs[2 * nt + na + 2]
        x, y, c, chip = _place()
        for t in range(nt):
            for d in (1, 2, 3):
                px, py, _ = _chip_at(x, y, d)
                _half_block(b_refs[t], chip, c, (px, py, c), send.at[3 * t + d - 1], recv.at[3 * t + d - 1]).start()
        token[...] = jnp.zeros_like(token)

    outs = pl.pallas_call(
        body, name=f"gather_start_{tag}",
        out_shape=(pltpu.SemaphoreType.DMA((3 * nt,)), pltpu.SemaphoreType.DMA((3 * nt,)),
                   *[pltpu.HBM(b.shape, b.dtype) for b in bufs], jax.ShapeDtypeStruct((8, 128), F32)),
        in_specs=[HBM] * nt + [ANY] * na, out_specs=(SEM, SEM, *[HBM] * nt, pl.BlockSpec(memory_space=pltpu.VMEM)),
        input_output_aliases={t: 2 + t for t in range(nt)},
        compiler_params=pltpu.CompilerParams(has_side_effects=DATAFLOW),
    )(*[_in_hbm(b) for b in bufs], *after)
    return outs[0], outs[1], list(outs[2:2 + nt]), outs[2 + nt]


def _gather_wait(send, recv, bufs, after, tag):
    nt = len(bufs)

    def body(*refs):
        b_refs = refs[:nt]
        send_ref, recv_ref = refs[nt], refs[nt + 1]
        x, y, c, chip = _place()
        for t in range(nt):
            for d in (1, 2, 3):
                k = 3 * t + d - 1
                px, py, pchip = _chip_at(x, y, d)
                _half_block(b_refs[t], chip, c, (px, py, c), send_ref.at[k], recv_ref.at[k]).wait_send()
                _half_block(b_refs[t], pchip, c, (px, py, c), send_ref.at[k], recv_ref.at[k]).wait_recv()

    after = tuple(after) if isinstance(after, (tuple, list)) else (after,)
    outs = pl.pallas_call(
        body, name=f"gather_wait_{tag}", out_shape=[pltpu.HBM(b.shape, b.dtype) for b in bufs],
        in_specs=[HBM] * nt + [SEM, SEM] + [ANY] * len(after), out_specs=[HBM] * nt,
        input_output_aliases={t: t for t in range(nt)},
        compiler_params=pltpu.CompilerParams(has_side_effects=DATAFLOW),
    )(*bufs, send, recv, *after)
    return list(outs)


def _gather_pass_on(bufs, tag):
    nt = len(bufs)

    def body(*refs):
        o_refs = refs[nt:2 * nt]
        fsend, frecv = refs[2 * nt:]
        x, y, c, _ = _place()
        cps = []
        for t in range(nt):
            for d in (1, 2, 3):
                k = 3 * t + d - 1
                _, _, pchip = _chip_at(x, y, d)
                cps.append(_half_block(o_refs[t], pchip, c, (x, y, 1 - c), fsend.at[k], frecv.at[k]))
        for cp in cps:
            cp.start()
        for t in range(nt):
            for d in (1, 2, 3):
                k = 3 * t + d - 1
                _, _, pchip = _chip_at(x, y, d)
                _half_block(o_refs[t], pchip, 1 - c, (x, y, 1 - c), fsend.at[k], frecv.at[k]).wait_recv()
        for cp in cps:
            cp.wait_send()

    return pl.pallas_call(
        body, name=f"gather_pass_on_{tag}", out_shape=[jax.ShapeDtypeStruct(b.shape, b.dtype) for b in bufs],
        in_specs=[ANY] * nt, out_specs=[ANY] * nt,
        scratch_shapes=[pltpu.SemaphoreType.DMA((3 * nt,)), pltpu.SemaphoreType.DMA((3 * nt,))],
        input_output_aliases={t: t for t in range(nt)},
        compiler_params=_cp(has_side_effects=True),
    )(*bufs)


def _to_sibling_halves(gs, l):
    nt = len(gs)

    def body(*refs):
        g_refs, o_refs = refs[:nt], refs[nt:2 * nt]
        send, recv = refs[2 * nt:]
        x, y, c, _ = _place()
        cps = [pltpu.make_async_remote_copy(
            src_ref=g_refs[t].at[k, 1 - c], dst_ref=o_refs[t].at[k], send_sem=send.at[4 * t + k], recv_sem=recv.at[4 * t + k],
            device_id=(x, y, 1 - c), device_id_type=MESH) for t in range(nt) for k in range(4)]
        for cp in cps:
            cp.start()
        for cp in cps:
            cp.wait()

    return pl.pallas_call(
        body, name=f"grads_to_sibling_{l}", out_shape=[jax.ShapeDtypeStruct((4,) + g.shape[2:], g.dtype) for g in gs],
        in_specs=[ANY] * nt, out_specs=[ANY] * nt,
        scratch_shapes=[pltpu.SemaphoreType.DMA((4 * nt,)), pltpu.SemaphoreType.DMA((4 * nt,))],
        compiler_params=_cp(has_side_effects=True),
    )(*gs)


def _chip_copy(c_ref, land_ref, x, y, c, d, send_sem, recv_sem):
    px, py, pchip = _chip_at(x, y, d)
    return pltpu.make_async_remote_copy(src_ref=c_ref.at[pchip], dst_ref=land_ref.at[d - 1], send_sem=send_sem, recv_sem=recv_sem,
                                        device_id=(px, py, c), device_id_type=MESH)


def _exchange_start(srcs, lands, copies, nsem, name):
    ns, n = len(srcs), len(srcs) + len(lands)

    def body(*refs):
        for cp in copies(refs[:ns], refs[ns:n], refs[n], refs[n + 1]):
            cp.start()
        token = refs[2 * n + 2]
        token[...] = jnp.zeros_like(token)

    outs = pl.pallas_call(
        body, name=name,
        out_shape=(pltpu.SemaphoreType.DMA((nsem,)), pltpu.SemaphoreType.DMA((nsem,)),
                   *[pltpu.HBM(a.shape, a.dtype) for a in list(srcs) + list(lands)], jax.ShapeDtypeStruct((8, 128), F32)),
        in_specs=[HBM] * n, out_specs=(SEM, SEM, *[HBM] * n, pl.BlockSpec(memory_space=pltpu.VMEM)),
        input_output_aliases={i: 2 + i for i in range(n)},
        compiler_params=pltpu.CompilerParams(has_side_effects=DATAFLOW),
    )(*[_in_hbm(a) for a in list(srcs) + list(lands)])
    return outs[0], outs[1], list(outs[2:2 + ns]), list(outs[2 + ns:2 + n]), outs[2 + n]


def _exchange_wait(send, recv, srcs, lands, after, copies, name):
    ns, n = len(srcs), len(srcs) + len(lands)

    def body(*refs):
        for cp in copies(refs[:ns], refs[ns:n], refs[n], refs[n + 1]):
            cp.wait_send()
            cp.wait_recv()

    outs = pl.pallas_call(
        body, name=name, out_shape=[pltpu.HBM(a.shape, a.dtype) for a in list(srcs) + list(lands)],
        in_specs=[HBM] * n + [SEM, SEM, ANY], out_specs=[HBM] * n,
        input_output_aliases={i: i for i in range(n)},
        compiler_params=pltpu.CompilerParams(has_side_effects=DATAFLOW),
    )(*srcs, *lands, send, recv, after)
    return list(outs[:ns]), list(outs[ns:])


def _pass_on_copies(b_refs, land_refs, send, recv):
    del land_refs
    x, y, c, _ = _place()
    return [_half_block(b_refs[t], _chip_at(x, y, d)[2], c, (x, y, 1 - c), send.at[3 * t + d - 1], recv.at[3 * t + d - 1])
            for t in range(len(b_refs)) for d in (1, 2, 3)]


def _chips_copies(c_refs, land_refs, send, recv):
    x, y, c, _ = _place()
    return [_chip_copy(c_refs[t], land_refs[t], x, y, c, d, send.at[3 * t + d - 1], recv.at[3 * t + d - 1])
            for t in range(len(c_refs)) for d in (1, 2, 3)]


def _sibling_copies(g_refs, land_refs, send, recv):
    x, y, c, _ = _place()
    return [pltpu.make_async_remote_copy(
        src_ref=g_refs[t].at[k, 1 - c], dst_ref=land_refs[t].at[k], send_sem=send.at[4 * t + k], recv_sem=recv.at[4 * t + k],
        device_id=(x, y, 1 - c), device_id_type=MESH) for t in range(len(g_refs)) for k in range(4)]


def _join_copies(f_refs, land_refs, send, recv):
    del land_refs
    x, y, c, _ = _place()
    return [pltpu.make_async_remote_copy(
        src_ref=f_refs[t].at[c], dst_ref=f_refs[t].at[c], send_sem=send.at[t], recv_sem=recv.at[t],
        device_id=(x, y, 1 - c), device_id_type=MESH) for t in range(len(f_refs))]


def _add_half(gs, rs, c_arr, name):
    n = len(gs)

    def body(c_ref, *refs):
        for g_ref, r_ref, o_ref in zip(refs[:n], refs[n:2 * n], refs[2 * n:]):
            o_ref[...] = (g_ref[...] + r_ref[...]).astype(BF16)

    def own(g):
        return pl.BlockSpec((None, None) + g.shape[2:], lambda k, cr: (k, cr[0], 0, 0))

    def blk(g):
        return pl.BlockSpec((None,) + g.shape[2:], lambda k, cr: (k, 0, 0))

    return pl.pallas_call(
        body, name=name, out_shape=[jax.ShapeDtypeStruct((4,) + g.shape[2:], BF16) for g in gs],
        grid_spec=pltpu.PrefetchScalarGridSpec(
            num_scalar_prefetch=1, grid=(4,),
            in_specs=[own(g) for g in gs] + [blk(g) for g in gs], out_specs=[blk(g) for g in gs]),
        compiler_params=_cp(("parallel",)),
    )(c_arr, *gs, *rs)


def _sum_chips(css, r3s, place_arr, name):
    n = len(css)

    def body(pl_ref, *refs):
        up = lambda ref: ref[...].astype(F32)
        for t in range(n):
            a_ref, (r0_ref, r1_ref, r2_ref), o_ref = refs[t], refs[n + 3 * t:n + 3 * t + 3], refs[4 * n + t]
            o_ref[...] = ((up(a_ref) + up(r0_ref)) + up(r1_ref)) + up(r2_ref)

    def blk(cs, first):
        _, rh, cols = cs.shape
        return pl.BlockSpec((None, rh // 2, cols), lambda i, pa: (first(pa), i, 0))

    in_specs = [blk(cs, lambda pa: pa[0]) for cs in css]
    for cs in css:
        in_specs += [blk(cs, lambda pa, d=d: d) for d in range(3)]
    return pl.pallas_call(
        body, name=name, out_shape=[jax.ShapeDtypeStruct((2,) + cs.shape[1:], F32) for cs in css],
        grid_spec=pltpu.PrefetchScalarGridSpec(
            num_scalar_prefetch=1, grid=(2,), in_specs=in_specs, out_specs=[blk(cs, lambda pa: pa[1]) for cs in css]),
        compiler_params=_cp(("parallel",)),
    )(place_arr, *css, *[r3 for r3 in r3s for _ in range(3)])


def _allreduce_small(pack):
    rows = pack.shape[0]
    hr = rows // 2

    def body(p_ref, o_ref, sib, slots, s1, r1, s2, r2, s3, r3):
        x, y, c, chip = _place()
        sibling = (x, y, 1 - c)
        ex = pltpu.make_async_remote_copy(src_ref=p_ref, dst_ref=sib, send_sem=s1, recv_sem=r1,
                                          device_id=sibling, device_id_type=MESH)
        ex.start()
        ex.wait()
        half = pl.ds(pl.multiple_of(c * hr, 16), hr)
        slots[0] = (p_ref[half, :] + sib[half, :]).astype(BF16)
        cps = []
        for d in (1, 2, 3):
            px, py, _ = _chip_at(x, y, d)
            cps.append(pltpu.make_async_remote_copy(
                src_ref=slots.at[0], dst_ref=slots.at[d], send_sem=s2.at[d - 1], recv_sem=r2.at[d - 1],
                device_id=(px, py, c), device_id_type=MESH))
        for cp in cps:
            cp.start()
        for cp in cps:
            cp.wait()
        tot = slots[chip].astype(F32)
        for k in (1, 2, 3):
            tot = tot + slots[jnp.bitwise_xor(chip, k)].astype(F32)
        o_ref[half, :] = tot
        back = pltpu.make_async_remote_copy(src_ref=o_ref.at[half, :], dst_ref=o_ref.at[half, :], send_sem=s3, recv_sem=r3,
                                            device_id=sibling, device_id_type=MESH)
        back.start()
        back.wait()

    vm = pl.BlockSpec(memory_space=pltpu.VMEM)
    return pl.pallas_call(
        body, name="allreduce_small", out_shape=jax.ShapeDtypeStruct((rows, 128), F32),
        in_specs=[vm], out_specs=vm,
        scratch_shapes=[pltpu.VMEM((rows, 128), F32), pltpu.VMEM((4, hr, 128), BF16),
                        pltpu.SemaphoreType.DMA, pltpu.SemaphoreType.DMA, pltpu.SemaphoreType.DMA((3,)), pltpu.SemaphoreType.DMA((3,)),
                        pltpu.SemaphoreType.DMA, pltpu.SemaphoreType.DMA],
        compiler_params=_cp(has_side_effects=True),
    )(pack)


def _small_chip_sum(pack):
    rows = pack.shape[0]
    hr = rows // 2

    def body(p_ref, o_ref, sib, s1, r1):
        x, y, c, _ = _place()
        ex = pltpu.make_async_remote_copy(src_ref=p_ref, dst_ref=sib, send_sem=s1, recv_sem=r1,
                                          device_id=(x, y, 1 - c), device_id_type=MESH)
        ex.start()
        ex.wait()
        half = pl.ds(pl.multiple_of(c * hr, 16), hr)
        o_ref[...] = (p_ref[half, :] + sib[half, :]).astype(BF16)

    vm = pl.BlockSpec(memory_space=pltpu.VMEM)
    return pl.pallas_call(
        body, name="small_chip_sum", out_shape=jax.ShapeDtypeStruct((hr, 128), BF16), in_specs=[vm], out_specs=vm,
        scratch_shapes=[pltpu.VMEM((rows, 128), F32), pltpu.SemaphoreType.DMA, pltpu.SemaphoreType.DMA],
        compiler_params=_cp(has_side_effects=True),
    )(pack)


def _small_copies(c_refs, land_refs, send, recv):
    x, y, c, _ = _place()
    cps = []
    for d in (1, 2, 3):
        px, py, _ = _chip_at(x, y, d)
        cps.append(pltpu.make_async_remote_copy(src_ref=c_refs[0], dst_ref=land_refs[0].at[d - 1], send_sem=send.at[d - 1],
                                                recv_sem=recv.at[d - 1], device_id=(px, py, c), device_id_type=MESH))
    return cps


def _small_total(csum, land):
    hr = csum.shape[0]

    def body(c_ref, l_ref, o_ref, slots, s3, r3):
        x, y, c, chip = _place()
        slots[0] = c_ref[...]
        for d in (1, 2, 3):
            slots[d] = l_ref[d - 1]
        tot = slots[chip].astype(F32)
        for k in (1, 2, 3):
            tot = tot + slots[jnp.bitwise_xor(chip, k)].astype(F32)
        half = pl.ds(pl.multiple_of(c * hr, 16), hr)
        o_ref[half, :] = tot
        back = pltpu.make_async_remote_copy(src_ref=o_ref.at[half, :], dst_ref=o_ref.at[half, :], send_sem=s3, recv_sem=r3,
                                            device_id=(x, y, 1 - c), device_id_type=MESH)
        back.start()
        back.wait()

    vm = pl.BlockSpec(memory_space=pltpu.VMEM)
    return pl.pallas_call(
        body, name="small_total", out_shape=jax.ShapeDtypeStruct((2 * hr, 128), F32), in_specs=[vm, vm], out_specs=vm,
        scratch_shapes=[pltpu.VMEM((4, hr, 128), BF16), pltpu.SemaphoreType.DMA, pltpu.SemaphoreType.DMA],
        compiler_params=_cp(has_side_effects=True),
    )(csum, land)


def _adam_math(gv, wv, mv, vv):
    m2 = ADAM_B1 * mv + (1.0 - ADAM_B1) * gv
    v2 = ADAM_B2 * vv + (1.0 - ADAM_B2) * (gv * gv)
    m_hat = m2 / (1.0 - ADAM_B1 ** ADAM_STEP)
    v_hat = v2 / (1.0 - ADAM_B2 ** ADAM_STEP)
    return -ADAM_LR * (m_hat / (jnp.sqrt(v_hat) + ADAM_EPS) + ADAM_WD * wv), m2, v2


def _adam(g, w, m, v, name):
    rows, cols = g.shape
    rb = rows // 4

    def body(g_ref, w_ref, m_ref, v_ref, d_ref, m2_ref, v2_ref):
        d_ref[...], m2_ref[...], v2_ref[...] = _adam_math(g_ref[...], w_ref[...], m_ref[...], v_ref[...])

    blk = pl.BlockSpec((rb, cols), lambda i: (i, 0))
    shp = jax.ShapeDtypeStruct((rows, cols), F32)
    return pl.pallas_call(
        body, name=name, grid=(4,), in_specs=[blk] * 4, out_specs=[blk] * 3, out_shape=[shp] * 3,
        compiler_params=_cp(("parallel",)),
    )(g, w, m, v)


def _adam_layer(gs, ws, ms, vs, l, prevs, name):
    n = len(gs)
    prev = [a for p4 in prevs if p4 is not None for a in p4]

    def body(*refs):
        outs = refs[len(refs) - 4 * n:]
        for t in range(n):
            g_ref, w_ref, m_ref, v_ref = refs[4 * t:4 * t + 4]
            go_ref, d_ref, m2_ref, v2_ref = outs[4 * t:4 * t + 4]
            gv = g_ref[...]
            go_ref[...] = gv
            d_ref[...], m2_ref[...], v2_ref[...] = _adam_math(gv, w_ref[...], m_ref[...], v_ref[...])

    in_specs, out_specs, out_shape, operands, aliases = [], [], [], [], {}
    for t, g in enumerate(gs):
        rows, cols = g.shape
        lay = pl.BlockSpec((None, rows // 4, cols), lambda i: (l, i, 0))
        in_specs += [pl.BlockSpec((rows // 4, cols), lambda i: (i, 0)), lay, lay, lay]
        operands += [g, ws[t], ms[t], vs[t]]
        out_specs += [lay] * 4
        out_shape += [jax.ShapeDtypeStruct((2, rows, cols), F32)] * 4
    k = 4 * n
    for t, p4 in enumerate(prevs):
        if p4 is not None:
            for j in range(4):
                aliases[k] = 4 * t + j
                k += 1
    outs = pl.pallas_call(
        body, name=name, grid=(4,), in_specs=in_specs + [ANY] * len(prev), out_specs=out_specs, out_shape=out_shape,
        input_output_aliases=aliases, compiler_params=_cp(("parallel",)),
    )(*operands, *prev)
    return [list(outs[4 * t:4 * t + 4]) for t in range(n)]


def _rows128(a):
    return a.reshape(-1, 128)


def _pack(arrs, mult):
    parts = [_rows128(a) for a in arrs]
    rows = sum(q.shape[0] for q in parts)
    pad = -rows % mult
    if pad:
        parts.append(jnp.zeros((pad, 128), F32))
    return jnp.concatenate(parts, axis=0)


def _unpack(pack, shapes):
    out, o = [], 0
    for s in shapes:
        n = 1
        for e in s:
            n *= e
        out.append(pack[o:o + n // 128].reshape(s))
        o += n // 128
    return out


WEIGHTS = ['norm1_g', 'w_in', 'gmlp_ln_g', 'gmlp_ln_b', 'gmlp_w_s', 'gmlp_b_s', 'conv_w', 'conv_b', 'lru_w_r', 'lru_b_r', 'lru_w_i',
           'lru_b_i', 'lru_lambda', 'w_out', 'norm2_g', 'w_ffn_in', 'w_ffn_out', 'final_g']
BIG = ['w_in', 'w_out', 'w_ffn_in', 'w_ffn_out']
SMALL = [n for n in WEIGHTS if n not in BIG]
CHIP_SHARDED_SMALL = ['conv_w', 'lru_b_r', 'lru_b_i', 'lru_lambda']


def kernel(x, norm1_g, w_in, gmlp_ln_g, gmlp_ln_b, gmlp_w_s, gmlp_b_s, conv_w, conv_b, lru_w_r, lru_b_r, lru_w_i, lru_b_i, lru_lambda, w_out, norm2_g, w_ffn_in, w_ffn_out, final_g, loss_target, m_norm1_g, m_w_in, m_gmlp_ln_g, m_gmlp_ln_b, m_gmlp_w_s, m_gmlp_b_s, m_conv_w, m_conv_b, m_lru_w_r, m_lru_b_r, m_lru_w_i, m_lru_b_i, m_lru_lambda, m_w_out, m_norm2_g, m_w_ffn_in, m_w_ffn_out, m_final_g, v_norm1_g, v_w_in, v_gmlp_ln_g, v_gmlp_ln_b, v_gmlp_w_s, v_gmlp_b_s, v_conv_w, v_conv_b, v_lru_w_r, v_lru_b_r, v_lru_w_i, v_lru_b_i, v_lru_lambda, v_w_out, v_norm2_g, v_w_ffn_in, v_w_ffn_out, v_final_g):
    a = dict(locals())
    w = {n: a[n] for n in WEIGHTS}
    mom = {n: a["m_" + n] for n in WEIGHTS}
    var = {n: a["v_" + n] for n in WEIGHTS}
    _, _, c, chip = _place()
    c_arr, chip_arr = jnp.reshape(c, (1,)).astype(jnp.int32), jnp.reshape(chip, (1,)).astype(jnp.int32)
    place_arr = jnp.stack([chip, c]).astype(jnp.int32)

    first, rest = BIG[:1], BIG[1:]

    def as_weights(names, full):
        wb = {n: f.reshape(4, 2 * f.shape[2], f.shape[3]) for n, f in zip(names, full)}
        if "w_out" in wb:
            wb["w_out"] = wb["w_out"].reshape(D, D)
            wb["w_ffn_out"] = wb["w_ffn_out"].reshape(DFF, D)
        return wb

    def cast(names, l, tag):
        return _cast_into([w[n] for n in names], l, chip_arr, f"cast_{tag}")

    def landed(fly, names, after, tag):
        return as_weights(names, _gather_pass_on(_gather_wait(fly[0], fly[1], fly[2], after, tag), tag))

    tiny = _pack([w[n] for n in CHIP_SHARDED_SMALL], 8)
    _, tiny_full = _gather_weights([], tiny)
    fly_in = _gather_start(cast(first, 0, "in"), "in", after=(tiny_full,))
    fly0 = _gather_start(cast(rest, 0, "0"), "0", after=(fly_in[3],))
    fly1 = _gather_start(cast(BIG, 1, "1"), "1", after=(fly0[3],))
    p = {n: w[n] for n in SMALL}
    parts = [_unpack(tiny_full[k], [w[n].shape for n in CHIP_SHARDED_SMALL]) for k in range(4)]
    for i, n in enumerate(CHIP_SHARDED_SMALL):
        p[n] = jnp.concatenate([parts[k][i] for k in range(4)], axis=-1)

    operands = [_layer_operands(l, p) for l in range(2)]
    state_packs = [_pack([src[n] for n in SMALL], 32) for src in (w, mom, var)]
    ahead = tuple(jax.tree.leaves(operands)) + tuple(state_packs)

    passing = {}

    def pass_on_1(gu):
        bufs = _gather_wait(fly1[0], fly1[1], fly1[2], gu, "1")
        passing[1] = _exchange_start(bufs, [], _pass_on_copies, 3 * len(bufs), "gather_pass_on_start_1")
        return (passing[1][-1],)

    xa, saved0 = _forward_layer(0, x[0], p, landed(fly_in, first, (fly1[3],) + ahead, "in"), after=(fly0[3], fly1[3]),
                                rest=lambda merged: landed(fly0, rest, merged, "0"), near_end=pass_on_1, operands=operands[0])
    send, recv, bufs1, _, _ = passing[1]
    xb, saved1 = _forward_layer(
        1, xa, p, as_weights(BIG, _exchange_wait(send, recv, bufs1, [], xa, _pass_on_copies, "gather_pass_on_wait_1")[0]),
        operands=operands[1])
    dxb, loss_v, dfg = _loss_head(xb, loss_target[0], p["final_g"][None])
    loss = lax.psum(loss_v[0, 0], ("x", "y", "c"))

    out, flying = {}, {}

    def halves(grads):
        return [g.reshape(4, 2, -1, g.shape[-1]) for g in grads]

    def sibling_start(grads, names, l, tag):
        gs = halves(grads)
        lands = [lax.empty((4,) + g.shape[2:], g.dtype) for g in gs]
        flying["s" + tag] = (names, l) + tuple(
            _exchange_start(gs, lands, _sibling_copies, 4 * len(gs), f"grads_to_sibling_start_{tag}"))
        return (flying["s" + tag][-1],)

    def chips_start(gs, from_sib, names, l, tag):
        cs = _add_half(gs, from_sib, c_arr, f"add_half_{tag}")
        lands = [lax.empty((3,) + a.shape[1:], a.dtype) for a in cs]
        flying[tag] = (names, l) + tuple(_exchange_start(cs, lands, _chips_copies, 3 * len(cs), f"grads_to_chips_start_{tag}"))
        return (flying[tag][-1],)

    def sibling_finish(tag, after):
        names, l, send, recv, gs, lands, _ = flying["s" + tag]
        gs, from_sib = _exchange_wait(send, recv, gs, lands, after, _sibling_copies, f"grads_to_sibling_wait_{tag}")
        return chips_start(gs, from_sib, names, l, tag)

    def reduce_start(grads, names, l, tag):
        gs = halves(grads)
        return chips_start(gs, _to_sibling_halves(gs, tag), names, l, tag)

    def reduce_sums(tags, after):
        groups, ts = [], []
        for tag in tags:
            names, l, send, recv, cs, lands, _ = flying[tag]
            cs, lands = _exchange_wait(send, recv, cs, lands, after, _chips_copies, f"grads_to_chips_wait_{tag}")
            ts += _sum_chips(cs, lands, place_arr, f"sum_chips_{tag}")
            groups.append((tag, names))
        flying["j" + tags[0]] = (groups, l) + tuple(_exchange_start(ts, [], _join_copies, len(ts), f"grads_join_start_{tags[0]}"))
        return (flying["j" + tags[0]][-1],)

    def reduce_adam(tag0, after):
        groups, l, send, recv, ts, _, _ = flying["j" + tag0]
        joined = _exchange_wait(send, recv, ts, [], after, _join_copies, f"grads_join_wait_{tag0}")[0]
        for tag, names in groups:
            gs, joined = [j.reshape(w[n].shape[1:]) for n, j in zip(names, joined)], joined[len(names):]
            res = _adam_layer(gs, [w[n] for n in names], [mom[n] for n in names], [var[n] for n in names], l,
                              [out.get(n) for n in names], f"adam_{tag}")
            out.update(zip(names, res))

    def late1(grads, _):
        return sibling_finish("1a", grads[0]) + sibling_start(grads, first, 1, "1b")

    def midway0(grads):
        return reduce_sums(("1a", "1b"), grads[0]) + sibling_start(grads, rest, 0, "0a")

    def midway2_0(dws):
        reduce_adam("1a", dws)
        return sibling_finish("0a", dws)

    def stacked_small(small0):
        small = {k: jnp.stack([small0[k], small1[k]]) for k in LAYER_SMALL}
        return dict(small, final_g=dfg[0])

    def late0(grads, small0):
        toks = reduce_start(grads, first, 0, "0b")
        small = stacked_small(dict(small0, norm1_g=jnp.zeros((D,), F32)))
        csum = _small_chip_sum(_pack([small[n] for n in SMALL], 32))
        flying["small"] = _exchange_start([csum], [lax.empty((3,) + csum.shape, BF16)], _small_copies, 3, "small_to_chips_start")
        return toks + (flying["small"][-1],)

    dxa, big1, small1 = _backward_layer(1, dxb, saved1, midway=lambda grads: sibling_start(grads, rest, 1, "1a"), late=late1)
    dx, big0, small0 = _backward_layer(0, dxa, saved0, after=sibling_finish("1b", dxa), midway=midway0,
                                       midway2=midway2_0, late=late0)
    join_tok = reduce_sums(("0a", "0b"), dx)
    small = stacked_small(small0)

    full_shapes = [small[n].shape for n in SMALL]
    send, recv, csum, land, _ = flying["small"]
    csum, land = _exchange_wait(send, recv, csum, land, join_tok[0], _small_copies, "small_to_chips_wait")
    red = _unpack(_small_total(csum[0], land[0]), full_shapes)
    norm1_0 = _allreduce_small(_pack([small0["norm1_g"]], 32))[:D // 128].reshape(D)
    reduce_adam("0a", norm1_0)
    red[SMALL.index("norm1_g")] = red[SMALL.index("norm1_g")].at[0].set(norm1_0)
    g_small = []
    for n, g in zip(SMALL, red):
        if n in CHIP_SHARDED_SMALL:
            g = lax.dynamic_slice_in_dim(g, chip * w[n].shape[-1], w[n].shape[-1], axis=g.ndim - 1)
        g_small.append(g)
    shapes = [w[n].shape for n in SMALL]
    upd = [_unpack(u, shapes) for u in _adam(_pack(g_small, 32), *state_packs, "adam_small")]
    for i, n in enumerate(SMALL):
        out[n] = [g_small[i], upd[0][i], upd[1][i], upd[2][i]]

    return (loss, dx[None]) + tuple(out[n][i] for i in range(4) for n in WEIGHTS)
```

```python
import functools

import jax
import jax.numpy as jnp
from jax import lax
from jax.experimental import pallas as pl
from jax.experimental.pallas import tpu as pltpu

F32 = jnp.float32
BF16 = jnp.bfloat16
MESH = pl.DeviceIdType.MESH

D = 1024
NH = 8
HD = 128
CHUNK = 128
N_IN_T = 12
DFF = 2816
DFF_SH = 1408
EPS = 1e-6
LRU_C = 8.0
ADAM_LR, ADAM_B1, ADAM_B2, ADAM_EPS, ADAM_WD, ADAM_STEP = 0.001, 0.9, 0.999, 1e-08, 0.01, 10

TM = 512
TM_BIG = 1024
RT = 128
PADR = 8
VMEM_LIMIT = 56 * 1024 * 1024


def _cp(sem=None, **kw):
    if sem is not None:
        kw["dimension_semantics"] = sem
    return pltpu.CompilerParams(vmem_limit_bytes=VMEM_LIMIT, **kw)


_GC = 0.7978845608028654


def _sigmoid(x):
    return 0.5 * jnp.tanh(0.5 * x) + 0.5


_GK = 0.044715


def _gelu(x):
    t = jnp.tanh(x * (_GC + (_GC * _GK) * (x * x)))
    return x * (0.5 + 0.5 * t)


def _gelu_and_grad(x):
    x2 = x * x
    t = jnp.tanh(x * (_GC + (_GC * _GK) * x2))
    h = 0.5 + 0.5 * t
    return x * h, h + x * (1.0 - t * t) * (0.5 * _GC + (1.5 * _GC * _GK) * x2)


def _softplus_neg(lam):
    y = jnp.exp(-jnp.abs(lam))
    u = 1.0 + y
    l1p = jnp.where(u == 1.0, y, jnp.log(u) * y / (u - 1.0))
    return jnp.maximum(-lam, 0.0) + l1p


def _dot(a, b):
    return jnp.dot(a, b, preferred_element_type=F32)


def _dot_nt(a, b):
    return lax.dot_general(a, b, (((1,), (1,)), ((), ())), preferred_element_type=F32)


def _dot_tn(a, b):
    return lax.dot_general(a, b, (((0,), (0,)), ((), ())), preferred_element_type=F32)


def _rms_hat(x):
    r = lax.rsqrt(jnp.mean(x * x, axis=-1, keepdims=True) + EPS)
    return x * r, r


def _rms_bwd(dh, x, g):
    xh, r = _rms_hat(x)
    dxh = dh * g
    dx = r * (dxh - xh * jnp.mean(dxh * xh, axis=-1, keepdims=True))
    return dx, jnp.sum(dh * xh, axis=0, keepdims=True)


def _norm_into(x_ref, g_ref, h_ref):
    xh, _ = _rms_hat(x_ref[...])
    h_ref[...] = (xh * g_ref[...]).astype(BF16)


def _in_tile(j):
    m, hf = j // 2, j % 2
    orig = jnp.where(m < 2, m, jnp.where(m == 2, 4, jnp.where(m < 5, m - 1, 5)))
    t = orig * 2 + hf
    return t // 3, t % 3


ANY = pl.BlockSpec(memory_space=pl.ANY)


def _mm_in(x, g, w_in, l, after=()):
    S = x.shape[0]
    tm = min(2 * TM_BIG, S)

    def body(x_ref, g_ref, w0_ref, w1_ref, *rest):
        o_ref, h_ref = rest[-2:]

        @pl.when(pl.program_id(1) == 0)
        def _():
            _norm_into(x_ref, g_ref, h_ref)
        rp = min(TM, tm)
        for r0 in range(0, tm, rp):
            hv = h_ref[r0:r0 + rp, :]
            o_ref[r0:r0 + rp, 0:512] = _dot(hv, w0_ref[...]).astype(BF16)
            o_ref[r0:r0 + rp, 512:1024] = _dot(hv, w1_ref[...]).astype(BF16)

    def w_tile(hf):
        def w_map(i, m):
            sh, tl = _in_tile(2 * m + hf)
            return (sh, 0, tl)
        return pl.BlockSpec((None, D, 512), w_map)

    return pl.pallas_call(
        body, name=f"mm_in_{l}", grid=(S // tm, 6),
        in_specs=[pl.BlockSpec((tm, D), lambda i, m: (i, 0)), pl.BlockSpec((1, D), lambda i, m: (0, 0)),
                  w_tile(0), w_tile(1)] + [ANY] * len(after),
        out_specs=[pl.BlockSpec((None, tm, D), lambda i, m: (m, i, 0)), pl.BlockSpec((tm, D), lambda i, m: (i, 0))],
        out_shape=[jax.ShapeDtypeStruct((6, S, D), BF16), jax.ShapeDtypeStruct((S, D), BF16)],
        compiler_params=_cp(("parallel", "arbitrary")),
    )(x, g, w_in, w_in, *after)


def _mm_res(a, w, res, l, name, after=()):
    S, K = a.shape

    tm = TM

    def body(a_ref, w_ref, r_ref, *rest):
        rest[-1][...] = r_ref[...] + _dot(a_ref[...], w_ref[...])

    return pl.pallas_call(
        body, name=f"{name}_{l}", grid=(S // tm,),
        in_specs=[pl.BlockSpec((tm, K), lambda i: (i, 0)), pl.BlockSpec((K, D), lambda i: (0, 0)),
                  pl.BlockSpec((tm, D), lambda i: (i, 0))] + [ANY] * len(after),
        out_specs=pl.BlockSpec((tm, D), lambda i: (i, 0)),
        out_shape=jax.ShapeDtypeStruct((S, D), F32),
        compiler_params=_cp(("parallel",)),
    )(a, w, res, *after)


def _mm_ffn_in(x, g, w_fi, l):
    S = x.shape[0]

    tm = min(TM_BIG, S)

    def body(x_ref, g_ref, w_ref, gu_ref, ff_ref, h_ref):
        @pl.when(pl.program_id(1) == 0)
        def _():
            _norm_into(x_ref, g_ref, h_ref)
        for r0 in range(0, tm, TM):
            rows = slice(r0, r0 + TM)
            hv = h_ref[rows, :]
            ga = _dot(hv, w_ref[0])
            gb = _dot(hv, w_ref[1])
            sg = _sigmoid(ga)
            silu = ga * sg
            gu_ref[0, rows, :] = (gb * (sg + silu * (1.0 - sg))).astype(BF16)
            gu_ref[1, rows, :] = silu.astype(BF16)
            ff_ref[rows, :] = (silu * gb).astype(BF16)

    gu, ff, h = pl.pallas_call(
        body, name=f"mm_ffn_in_{l}", grid=(S // tm, 2),
        in_specs=[pl.BlockSpec((tm, D), lambda i, s: (i, 0)), pl.BlockSpec((1, D), lambda i, s: (0, 0)),
                  pl.BlockSpec((2, None, D, DFF_SH), lambda i, s: (0, s, 0, 0))],
        out_specs=[pl.BlockSpec((2, None, tm, DFF_SH), lambda i, s: (0, s, i, 0)),
                   pl.BlockSpec((tm, DFF_SH), lambda i, s: (i, s)),
                   pl.BlockSpec((tm, D), lambda i, s: (i, 0))],
        out_shape=[jax.ShapeDtypeStruct((2, 2, S, DFF_SH), BF16), jax.ShapeDtypeStruct((S, DFF), BF16),
                   jax.ShapeDtypeStruct((S, D), BF16)],
        compiler_params=_cp(("parallel", "arbitrary")),
    )(x, g, w_fi.reshape(2, 2, D, DFF_SH))
    return gu.reshape(4, S, DFF_SH), ff, h


def _gmlp_fwd(z6, ws_b, bs_b, lg, lb):
    S = z6.shape[1]

    def body(z_ref, ws_ref, bs_ref, lg_ref, lb_ref, o_ref, mix):
        gv = _gelu(z_ref[1].astype(F32))
        xc = gv - jnp.mean(gv, axis=-1, keepdims=True)
        rs = lax.rsqrt(jnp.mean(xc * xc, axis=-1, keepdims=True) + EPS)
        vb = (xc * rs * lg_ref[...] + lb_ref[...]).astype(BF16)
        for gi in range(NH):
            cs = slice(gi * HD, (gi + 1) * HD)
            mix[:, cs] = _dot(ws_ref[gi], vb[:, cs])
        o_ref[...] = (_sigmoid(z_ref[2].astype(F32)) * _gelu(z_ref[0].astype(F32)) * (mix[...] + bs_ref[...])).astype(BF16)

    return pl.pallas_call(
        body, name="gmlp_fwd", grid=(S // CHUNK,),
        in_specs=[pl.BlockSpec((3, CHUNK, D), lambda i: (0, i, 0)), pl.BlockSpec((NH, CHUNK, CHUNK), lambda i: (0, 0, 0)),
                  pl.BlockSpec((CHUNK, D), lambda i: (0, 0)), pl.BlockSpec((1, D), lambda i: (0, 0)),
                  pl.BlockSpec((1, D), lambda i: (0, 0))],
        out_specs=pl.BlockSpec((CHUNK, D), lambda i: (i, 0)),
        out_shape=jax.ShapeDtypeStruct((S, D), BF16),
        scratch_shapes=[pltpu.VMEM((CHUNK, D), F32)],
        compiler_params=_cp(("parallel",)),
    )(z6, ws_b, bs_b, lg, lb)


def _row_iota():
    return lax.broadcasted_iota(jnp.int32, (RT, HD), 0)


SUB = 8
UNROLL = 8
GRAD_ROWS = 512


def _scan_up(a, b, carry):
    row = lax.broadcasted_iota(jnp.int32, (SUB, HD), 0)
    masks = [(d, row >= d) for d in (1, 2, 4)]
    c = jnp.broadcast_to(carry, (SUB, HD))
    hs = []
    for j in range(RT // SUB):
        aj, bj = a[SUB * j:SUB * (j + 1)], b[SUB * j:SUB * (j + 1)]
        for d, m in masks:
            bj = bj + aj * jnp.where(m, pltpu.roll(bj, d, 0), 0.0)
            aj = aj * jnp.where(m, pltpu.roll(aj, d, 0), 1.0)
        h = bj + aj * c
        hs.append(h)
        c = jnp.broadcast_to(h[SUB - 1:SUB, :], (SUB, HD))
    return jnp.concatenate(hs, axis=0), hs[-1][SUB - 1:SUB, :]


def _scan_down(a, b, carry):
    row = lax.broadcasted_iota(jnp.int32, (SUB, HD), 0)
    masks = [(d, row < SUB - d) for d in (1, 2, 4)]
    c = jnp.broadcast_to(carry, (SUB, HD))
    hs = []
    for j in reversed(range(RT // SUB)):
        aj, bj = a[SUB * j:SUB * (j + 1)], b[SUB * j:SUB * (j + 1)]
        for d, m in masks:
            bj = bj + aj * jnp.where(m, pltpu.roll(bj, SUB - d, 0), 0.0)
            aj = aj * jnp.where(m, pltpu.roll(aj, SUB - d, 0), 1.0)
        h = bj + aj * c
        hs.append(h)
        c = jnp.broadcast_to(h[0:1, :], (SUB, HD))
    return jnp.concatenate(hs[::-1], axis=0), hs[-1][0:1, :]


def _decay(r, sp_d):
    log_a = -LRU_C * r * sp_d
    a = jnp.exp(log_a)
    return a, jnp.sqrt(jnp.maximum(-jnp.tanh(log_a) * (a * a + 1.0), 0.0))


def _lru_gates(xc, d, wr_ref, br_ref, wi_ref, bi_ref, sp):
    xb = xc.astype(BF16)
    r = _sigmoid(_dot(xb, wr_ref[d]) + br_ref[d:d + 1, :])
    i = _sigmoid(_dot(xb, wi_ref[d]) + bi_ref[d:d + 1, :])
    a, mult = _decay(r, sp[d:d + 1, :])
    return r, i, a, mult


def _shifted(win, k):
    w = RT + 2 * PADR
    v = win if k == 0 else pltpu.roll(win, (-k) % w, 0)
    return v[PADR:PADR + RT]


def _conv_taps(win):
    return [_shifted(win, k) for k in (-1, 0, 1, 2)]


def _fill_padded(dst, src_ref, S):
    zeros = jnp.zeros((PADR, HD), F32)
    dst[0:PADR, :] = zeros
    dst[PADR + S:2 * PADR + S, :] = zeros

    def cp(i, c):
        t0 = pl.multiple_of(i * RT, RT)
        dst[pl.ds(t0 + PADR, RT), :] = src_ref[pl.ds(t0, RT), :].astype(F32)
        return c
    lax.fori_loop(0, S // RT, cp, 0)


def _conv_fwd_all(zxp, xc_s, cw_ref, cb_ref, S):
    def cv(i, c):
        t0 = pl.multiple_of(i * RT, RT)
        xm1, x0, xp1, xp2 = _conv_taps(zxp[pl.ds(t0, RT + 2 * PADR), :])
        xc_s[pl.ds(t0, RT), :] = (cb_ref[...] + xm1 * cw_ref[0:1, :] + x0 * cw_ref[1:2, :]
                                  + xp1 * cw_ref[2:3, :] + xp2 * cw_ref[3:4, :])
        return c
    lax.fori_loop(0, S // RT, cv, 0)


def _lru_specs(S):
    head = lambda h: (0, h)
    return [pl.BlockSpec((4, HD), head), pl.BlockSpec((1, HD), head),
            pl.BlockSpec((2, None, HD, HD), lambda h: (0, h, 0, 0)), pl.BlockSpec((2, HD), head),
            pl.BlockSpec((2, None, HD, HD), lambda h: (0, h, 0, 0)), pl.BlockSpec((2, HD), head),
            pl.BlockSpec((2, HD), head)]


def _lru_fwd(z6, ya, cw, cb, wr, br, wi, bi, lam):
    S = z6.shape[1]
    nt = S // RT

    def body(z_ref, ya_ref, cw_ref, cb_ref, wr_ref, br_ref, wi_ref, bi_ref, lam_ref, mg_ref, h0_ref, h1_ref, zxp, xc_s):
        sp = _softplus_neg(lam_ref[...])
        _fill_padded(zxp, z_ref.at[0], S)
        _conv_fwd_all(zxp, xc_s, cw_ref, cb_ref, S)

        def scans(i, carry):
            cu, cd = carry
            for u in range(UNROLL):
                j = i * UNROLL + u
                ru = pl.ds(pl.multiple_of(j * RT, RT), RT)
                rd = pl.ds(pl.multiple_of((nt - 1 - j) * RT, RT), RT)
                xu, xd = xc_s[ru, :], xc_s[rd, :]
                _, gi, a, mult = _lru_gates(xu, 0, wr_ref, br_ref, wi_ref, bi_ref, sp)
                hu, cu = _scan_up(a, mult * gi * xu, cu)
                h0_ref[ru, :] = hu
                _, gi, a, mult = _lru_gates(xd, 1, wr_ref, br_ref, wi_ref, bi_ref, sp)
                hd, cd = _scan_down(a, mult * gi * xd, cd)
                h1_ref[rd, :] = hd
            return cu, cd
        z1 = jnp.zeros((1, HD), F32)
        lax.fori_loop(0, nt // UNROLL, scans, (z1, z1))

        def merge(i, c):
            rows = pl.ds(pl.multiple_of(i * RT, RT), RT)
            yb = (h0_ref[rows, :] + h1_ref[rows, :]) * _gelu(z_ref[1, rows, :].astype(F32))
            mg_ref[rows, :] = (ya_ref[rows, :].astype(F32) + _sigmoid(z_ref[2, rows, :].astype(F32)) * yb).astype(BF16)
            return c
        lax.fori_loop(0, nt, merge, 0)

    col = pl.BlockSpec((S, HD), lambda h: (0, h))
    return pl.pallas_call(
        body, name="lru_fwd", grid=(NH,),
        in_specs=[pl.BlockSpec((3, S, HD), lambda h: (1, 0, h)), col] + _lru_specs(S),
        out_specs=[col, col, col],
        out_shape=[jax.ShapeDtypeStruct((S, D), BF16), jax.ShapeDtypeStruct((S, D), F32), jax.ShapeDtypeStruct((S, D), F32)],
        scratch_shapes=[pltpu.VMEM((S + 2 * PADR, HD), F32), pltpu.VMEM((S, HD), F32)],
        compiler_params=_cp(("parallel",)),
    )(z6, ya, cw, cb, wr, br, wi, bi, lam)


def _mm_res_loss(a, w, res, tgt, g):
    S, K = a.shape

    def body(a_ref, w_ref, r_ref, t_ref, g_ref, dx_ref, loss_ref, dg_ref):
        @pl.when(pl.program_id(0) == 0)
        def _():
            loss_ref[...] = jnp.zeros_like(loss_ref)
            dg_ref[...] = jnp.zeros_like(dg_ref)
        xv = r_ref[...] + _dot(a_ref[...], w_ref[...])
        xh, _ = _rms_hat(xv)
        e = xh * g_ref[...] - t_ref[...]
        loss_ref[...] += jnp.sum(e * e) * (0.5 / D)
        dx, dgs = _rms_bwd(e * (1.0 / D), xv, g_ref[...])
        dx_ref[...] = dx
        dg_ref[...] += dgs

    row = pl.BlockSpec((TM, D), lambda i: (i, 0))
    vec = pl.BlockSpec((1, D), lambda i: (0, 0))
    return pl.pallas_call(
        body, name="mm_ffn_out_loss", grid=(S // TM,),
        in_specs=[pl.BlockSpec((TM, K), lambda i: (i, 0)), pl.BlockSpec((K, D), lambda i: (0, 0)), row, row, vec],
        out_specs=[row, pl.BlockSpec((1, 128), lambda i: (0, 0)), vec],
        out_shape=[jax.ShapeDtypeStruct((S, D), F32), jax.ShapeDtypeStruct((1, 128), F32), jax.ShapeDtypeStruct((1, D), F32)],
        compiler_params=_cp(("arbitrary",)),
    )(a, w, res, tgt, g)


def _bwd_ffn_out(dx, w_fo, gu, l, after=()):
    S = dx.shape[0]

    tm = min(TM_BIG, S)

    def body(dx_ref, w_ref, gu_ref, *rest):
        o_ref = rest[-1]
        for r0 in range(0, tm, TM):
            rows = slice(r0, r0 + TM)
            d = _dot_nt(dx_ref[rows, :].astype(BF16), w_ref[...])
            o_ref[0, rows, :] = (d * gu_ref[0, rows, :].astype(F32)).astype(BF16)
            o_ref[1, rows, :] = (d * gu_ref[1, rows, :].astype(F32)).astype(BF16)

    pair = pl.BlockSpec((2, None, tm, DFF_SH), lambda i, s: (0, s, i, 0))
    dgu = pl.pallas_call(
        body, name=f"bwd_ffn_out_{l}", grid=(S // tm, 2),
        in_specs=[pl.BlockSpec((tm, D), lambda i, s: (i, 0)), pl.BlockSpec((DFF_SH, D), lambda i, s: (s, 0)), pair]
        + [ANY] * len(after),
        out_specs=pair,
        out_shape=jax.ShapeDtypeStruct((2, 2, S, DFF_SH), BF16),
        compiler_params=_cp(("parallel", "arbitrary")),
    )(dx, w_fo, gu.reshape(2, 2, S, DFF_SH), *after)
    return dgu.reshape(4, S, DFF_SH)


def _mm_tn(a, b, m_blk, tk, name):
    S, M = a.shape

    def body(a_ref, b_ref, o_ref):
        @pl.when(pl.program_id(1) == 0)
        def _():
            o_ref[...] = jnp.zeros_like(o_ref)
        o_ref[...] += _dot_tn(a_ref[...], b_ref[...].astype(BF16))

    return pl.pallas_call(
        body, name=name, grid=(M // m_blk, S // tk),
        in_specs=[pl.BlockSpec((tk, m_blk), lambda m, k: (k, m)), pl.BlockSpec((tk, D), lambda m, k: (k, 0))],
        out_specs=pl.BlockSpec((m_blk, D), lambda m, k: (m, 0)),
        out_shape=jax.ShapeDtypeStruct((M, D), F32),
        compiler_params=_cp(("parallel", "arbitrary")),
    )(a, b)


def _mm_nt_rms_bwd(a, a_specs, w, w_specs, nk, tm, x, g, dres, name, after=()):
    S = x.shape[0]
    sub = len(a_specs)

    def body(*refs):
        a_refs, w_refs = refs[:sub], refs[sub:2 * sub]
        x_ref, g_ref, r_ref = refs[2 * sub:2 * sub + 3]
        dx_ref, dg_ref, acc = refs[-3:]
        i, k = pl.program_id(0), pl.program_id(1)
        @pl.when(k == 0)
        def _():
            acc[...] = jnp.zeros_like(acc)
        for j in range(sub):
            acc[...] += _dot_nt(a_refs[j][...], w_refs[j][...])

        @pl.when(jnp.logical_and(i == 0, k == 0))
        def _():
            dg_ref[...] = jnp.zeros_like(dg_ref)

        @pl.when(k == nk - 1)
        def _():
            dx, dgs = _rms_bwd(acc[...], x_ref[...], g_ref[...])
            dx_ref[...] = r_ref[...] + dx
            dg_ref[...] += dgs

    row = pl.BlockSpec((tm, D), lambda i, k: (i, 0))
    vec = pl.BlockSpec((1, D), lambda i, k: (0, 0))
    return pl.pallas_call(
        body, name=name, grid=(S // tm, nk),
        in_specs=list(a_specs) + list(w_specs) + [row, vec, row] + [ANY] * len(after),
        out_specs=[row, vec],
        out_shape=[jax.ShapeDtypeStruct((S, D), F32), jax.ShapeDtypeStruct((1, D), F32)],
        scratch_shapes=[pltpu.VMEM((tm, D), F32)],
        compiler_params=_cp(("arbitrary", "arbitrary")),
    )(*[a] * sub, *[w] * sub, x, g, dres, *after)


def _dw_ffn_in(h, dgu, l):
    S = h.shape[0]

    def body(h_ref, b_ref, o_ref):
        @pl.when(pl.program_id(1) == 0)
        def _():
            o_ref[...] = jnp.zeros_like(o_ref)
        o_ref[...] += _dot_tn(h_ref[...], b_ref[...])

    tk = min(2 * TM_BIG, S)
    return pl.pallas_call(
        body, name=f"dw_ffn_in_{l}", grid=(4, S // tk),
        in_specs=[pl.BlockSpec((tk, D), lambda j, k: (k, 0)), pl.BlockSpec((None, tk, DFF_SH), lambda j, k: (j, k, 0))],
        out_specs=pl.BlockSpec((None, D, DFF_SH), lambda j, k: (j, 0, 0)),
        out_shape=jax.ShapeDtypeStruct((4, D, DFF_SH), F32),
        compiler_params=_cp(("parallel", "arbitrary")),
    )(h, dgu)


_HALF_COMPS = ((0, 1, 3), (4, 2, 5))


def _dw_in(h, dz6, l):
    S = h.shape[0]

    def body(h_ref, d0_ref, d1_ref, d2_ref, o_ref):
        @pl.when(pl.program_id(1) == 0)
        def _():
            o_ref[...] = jnp.zeros_like(o_ref)
        hv = h_ref[...]
        for q, d_ref in enumerate((d0_ref, d1_ref, d2_ref)):
            for hf in range(2):
                col = 1024 * q + 512 * hf
                o_ref[col // 1536, :, col % 1536:col % 1536 + 512] += _dot_tn(hv, d_ref[:, 512 * hf:512 * (hf + 1)])

    tk = min(TM_BIG, S)

    def comp(q):
        return pl.BlockSpec((None, tk, D), lambda p, k: (jnp.where(p == 0, _HALF_COMPS[0][q], _HALF_COMPS[1][q]), k, 0))

    return pl.pallas_call(
        body, name=f"dw_in_{l}", grid=(2, S // tk),
        in_specs=[pl.BlockSpec((tk, D), lambda p, k: (k, 0)), comp(0), comp(1), comp(2)],
        out_specs=pl.BlockSpec((2, D, 1536), lambda p, k: (p, 0, 0)),
        out_shape=jax.ShapeDtypeStruct((4, D, 1536), F32),
        compiler_params=_cp(("parallel", "arbitrary")),
    )(h, dz6, dz6, dz6)


def _bwd_out(dx, w_o, merged, l):
    S = dx.shape[0]

    def body(dx_ref, w_ref, m_ref, dm_ref, dw_ref):
        @pl.when(pl.program_id(0) == 0)
        def _():
            dw_ref[...] = jnp.zeros_like(dw_ref)
        dxb = dx_ref[...].astype(BF16)
        dm_ref[...] = _dot_nt(dxb, w_ref[...]).astype(BF16)
        dw_ref[...] += _dot_tn(m_ref[...], dxb)

    tm = TM
    row = pl.BlockSpec((tm, D), lambda i: (i, 0))
    return pl.pallas_call(
        body, name=f"bwd_out_{l}", grid=(S // tm,),
        in_specs=[row, pl.BlockSpec((D, D), lambda i: (0, 0)), row],
        out_specs=[row, pl.BlockSpec((D, D), lambda i: (0, 0))],
        out_shape=[jax.ShapeDtypeStruct((S, D), BF16), jax.ShapeDtypeStruct((D, D), F32)],
        compiler_params=_cp(("arbitrary",)),
    )(dx, w_o, merged)


def _gmlp_bwd(dm, z6, ws_b, wst_b, bs_b, lg, lb, after=()):
    S = z6.shape[1]

    def body(dm_ref, z_ref, ws_ref, wst_ref, bs_ref, lg_ref, lb_ref, *rest):
        dz_ref, dws_ref, dbs_ref, dlg_ref, dlb_ref, mix, dv = rest[-7:]

        @pl.when(pl.program_id(0) == 0)
        def _():
            dws_ref[...] = jnp.zeros_like(dws_ref)
            dbs_ref[...] = jnp.zeros_like(dbs_ref)
            dlg_ref[...] = jnp.zeros_like(dlg_ref)
            dlb_ref[...] = jnp.zeros_like(dlb_ref)
        gv, dgelu_v = _gelu_and_grad(z_ref[1].astype(F32))
        xc = gv - jnp.mean(gv, axis=-1, keepdims=True)
        rs = lax.rsqrt(jnp.mean(xc * xc, axis=-1, keepdims=True) + EPS)
        vh = xc * rs
        vb = (vh * lg_ref[...] + lb_ref[...]).astype(BF16)
        for gi in range(NH):
            cs = slice(gi * HD, (gi + 1) * HD)
            mix[:, cs] = _dot(ws_ref[gi], vb[:, cs])
        u, dgelu_u = _gelu_and_grad(z_ref[0].astype(F32))
        sa = _sigmoid(z_ref[2].astype(F32))
        mixed = mix[...] + bs_ref[...]
        dyg = dm_ref[...].astype(F32)
        dz_ref[2] = (dyg * u * mixed * sa * (1.0 - sa)).astype(BF16)
        dya = dyg * sa
        dz_ref[0] = (dya * mixed * dgelu_u).astype(BF16)
        dmix = dya * u
        dmb = dmix.astype(BF16)
        for gi in range(NH):
            cs = slice(gi * HD, (gi + 1) * HD)
            dv[:, cs] = _dot(wst_ref[gi], dmb[:, cs])
            dws_ref[gi] += _dot_nt(dmb[:, cs], vb[:, cs])
            dbs_ref[gi] += jnp.broadcast_to(jnp.sum(dmix[:, cs], axis=1, keepdims=True), (CHUNK, HD))
        dvv = dv[...]
        dlg_ref[...] += jnp.sum(dvv * vh, axis=0, keepdims=True)
        dlb_ref[...] += jnp.sum(dvv, axis=0, keepdims=True)
        dvh = dvv * lg_ref[...]
        dgv = rs * (dvh - jnp.mean(dvh, axis=-1, keepdims=True) - vh * jnp.mean(dvh * vh, axis=-1, keepdims=True))
        dz_ref[1] = (dgv * dgelu_v).astype(BF16)

    vec = pl.BlockSpec((1, D), lambda i: (0, 0))
    mat = pl.BlockSpec((NH, CHUNK, CHUNK), lambda i: (0, 0, 0))
    return pl.pallas_call(
        body, name="gmlp_bwd", grid=(S // CHUNK,),
        in_specs=[pl.BlockSpec((CHUNK, D), lambda i: (i, 0)), pl.BlockSpec((3, CHUNK, D), lambda i: (0, i, 0)), mat, mat,
                  pl.BlockSpec((CHUNK, D), lambda i: (0, 0)), vec, vec] + [ANY] * len(after),
        out_specs=[pl.BlockSpec((3, CHUNK, D), lambda i: (0, i, 0)), mat, mat, vec, vec],
        out_shape=[jax.ShapeDtypeStruct((6, S, D), BF16), jax.ShapeDtypeStruct((NH, CHUNK, CHUNK), F32),
                   jax.ShapeDtypeStruct((NH, CHUNK, HD), F32), jax.ShapeDtypeStruct((1, D), F32), jax.ShapeDtypeStruct((1, D), F32)],
        scratch_shapes=[pltpu.VMEM((CHUNK, D), F32), pltpu.VMEM((CHUNK, D), F32)],
        compiler_params=_cp(("arbitrary",)),
    )(dm, z6, ws_b, wst_b, bs_b, lg, lb, *after)


def _lru_bwd(dz6, dm, z6, h0, h1, cw, cb, wr, br, wi, bi, lam, after=()):
    S = z6.shape[1]
    nt = S // RT

    def body(dz_in, dm_ref, z_ref, h0_ref, h1_ref, cw_ref, cb_ref, wr_ref, br_ref, wi_ref, bi_ref, lam_ref, *rest):
        dz_ref, dcw_ref, dcb_ref, dwr_ref, dbr_ref, dwi_ref, dbi_ref, dlam_ref, zxp, xc_s, dhs_s, dxcp, r_s, lam_s = rest[-14:]
        del dz_in
        lam = lam_ref[...]
        sp = _softplus_neg(lam)
        row = _row_iota()
        _fill_padded(zxp, z_ref.at[0], S)
        _conv_fwd_all(zxp, xc_s, cw_ref, cb_ref, S)
        zeros = jnp.zeros((PADR, HD), F32)
        dxcp[0:PADR, :] = zeros
        dxcp[PADR + S:2 * PADR + S, :] = zeros
        dwr_ref[...] = jnp.zeros_like(dwr_ref)
        dwi_ref[...] = jnp.zeros_like(dwi_ref)

        def pre(i, c):
            rows = pl.ds(pl.multiple_of(i * RT, RT), RT)
            hs = h0_ref[rows, :] + h1_ref[rows, :]
            dmv = dm_ref[rows, :].astype(F32)
            sb = _sigmoid(z_ref[2, rows, :].astype(F32))
            gg, dgg = _gelu_and_grad(z_ref[1, rows, :].astype(F32))
            dz_ref[2, rows, :] = (dmv * hs * gg * sb * (1.0 - sb)).astype(BF16)
            dyb = dmv * sb
            dz_ref[1, rows, :] = (dyb * hs * dgg).astype(BF16)
            dhs_s[rows, :] = dyb * gg
            return c
        lax.fori_loop(0, nt, pre, 0)

        def gate_bwd(d, gates, lamv, da, xc):
            r, gi, a, mult = gates
            dmult = lamv * gi * xc
            dgi = lamv * mult * xc
            dlog = (da - dmult * a / mult) * a
            dpr = (dlog * (-LRU_C) * sp[d:d + 1, :]) * r * (1.0 - r)
            dpi = dgi * gi * (1.0 - gi)
            xb, dprb, dpib = xc.astype(BF16), dpr.astype(BF16), dpi.astype(BF16)
            dwr_ref[d] += _dot_tn(xb, dprb)
            dwi_ref[d] += _dot_tn(xb, dpib)
            dxc = lamv * mult * gi + _dot_nt(dprb, wr_ref[d]) + _dot_nt(dpib, wi_ref[d])
            return dxc, (jnp.sum(dlog * r, axis=0, keepdims=True) * (-LRU_C), jnp.sum(dpr, axis=0, keepdims=True),
                         jnp.sum(dpi, axis=0, keepdims=True))

        def rgates(i, c):
            for u in range(UNROLL):
                rows = pl.ds(pl.multiple_of((i * UNROLL + u) * RT, RT), RT)
                xb = xc_s[rows, :].astype(BF16)
                for d in range(2):
                    r_s[d, rows, :] = _sigmoid(_dot(xb, wr_ref[d]) + br_ref[d:d + 1, :])
            return c
        lax.fori_loop(0, nt // UNROLL, rgates, 0)

        def chains(i, carry):
            qn, qp = carry
            for u in range(UNROLL):
                j = i * UNROLL + u
                rd = pl.ds(pl.multiple_of((nt - 1 - j) * RT, RT), RT)
                a, dhs = _decay(r_s[0, rd, :], sp[0:1, :])[0], dhs_s[rd, :]
                q, q_first = _scan_down(a, a * dhs, qn)
                lam_s[0, rd, :] = dhs + jnp.where(row == RT - 1, qn, pltpu.roll(q, RT - 1, 0))
                qn = q_first
                ru = pl.ds(pl.multiple_of(j * RT, RT), RT)
                a, dhs = _decay(r_s[1, ru, :], sp[1:2, :])[0], dhs_s[ru, :]
                q, q_last = _scan_up(a, a * dhs, qp)
                lam_s[1, ru, :] = dhs + jnp.where(row == 0, qp, pltpu.roll(q, 1, 0))
                qp = q_last
            return qn, qp

        z1 = jnp.zeros((1, HD), F32)
        lax.fori_loop(0, nt // UNROLL, chains, (z1, z1))

        ct = min(GRAD_ROWS, S)
        crow = lax.broadcasted_iota(jnp.int32, (ct, HD), 0)

        def tile_grads(i, acc):
            t0 = pl.multiple_of(i * ct, ct)
            rows = pl.ds(t0, ct)
            xc = xc_s[rows, :]
            xb = xc.astype(BF16)
            tp = pl.multiple_of(jnp.maximum(t0 - PADR, 0), PADR)
            prev = jnp.where(t0 > 0, h0_ref[pl.ds(tp, PADR), :][PADR - 1:PADR, :], 0.0)
            tn = pl.multiple_of(jnp.minimum(t0 + ct, S - PADR), PADR)
            nxt = jnp.where(t0 + ct < S, h1_ref[pl.ds(tn, PADR), :][0:1, :], 0.0)
            hside = (jnp.where(crow == 0, prev, pltpu.roll(h0_ref[rows, :], 1, 0)),
                     jnp.where(crow == ct - 1, nxt, pltpu.roll(h1_ref[rows, :], ct - 1, 0)))
            dxc, sums = 0.0, ()
            for d in range(2):
                r = r_s[d, rows, :]
                gi = _sigmoid(_dot(xb, wi_ref[d]) + bi_ref[d:d + 1, :])
                a, mult = _decay(r, sp[d:d + 1, :])
                lamv = lam_s[d, rows, :]
                dxc_d, s_d = gate_bwd(d, (r, gi, a, mult), lamv, lamv * hside[d], xc)
                dxc = dxc + dxc_d
                sums = sums + s_d
            dxcp[pl.ds(t0 + PADR, ct), :] = dxc
            return tuple(x + y for x, y in zip(acc, sums))

        s_sp0, s_br0, s_bi0, s_sp1, s_br1, s_bi1 = lax.fori_loop(0, S // ct, tile_grads, (z1,) * 6)

        dsp = jnp.concatenate([s_sp0, s_sp1], axis=0)
        dlam_ref[...] = -dsp * _sigmoid(-lam)
        dbr_ref[...] = jnp.concatenate([s_br0, s_br1], axis=0)
        dbi_ref[...] = jnp.concatenate([s_bi0, s_bi1], axis=0)

        def conv_bwd(i, carry):
            c0, c1, c2, c3, cb_ = carry
            t0 = pl.multiple_of(i * RT, RT)
            dwin = dxcp[pl.ds(t0, RT + 2 * PADR), :]
            d0 = _shifted(dwin, 0)
            dz_ref[0, pl.ds(t0, RT), :] = (_shifted(dwin, 1) * cw_ref[0:1, :] + d0 * cw_ref[1:2, :]
                                           + _shifted(dwin, -1) * cw_ref[2:3, :] + _shifted(dwin, -2) * cw_ref[3:4, :]).astype(BF16)
            xm1, x0, xp1, xp2 = _conv_taps(zxp[pl.ds(t0, RT + 2 * PADR), :])
            sm = lambda v: jnp.sum(v, axis=0, keepdims=True)
            return c0 + sm(d0 * xm1), c1 + sm(d0 * x0), c2 + sm(d0 * xp1), c3 + sm(d0 * xp2), cb_ + sm(d0)

        c0, c1, c2, c3, cb_ = lax.fori_loop(0, nt, conv_bwd, (z1, z1, z1, z1, z1))
        dcw_ref[...] = jnp.concatenate([c0, c1, c2, c3], axis=0)
        dcb_ref[...] = cb_

    col = pl.BlockSpec((S, HD), lambda h: (0, h))
    head = lambda h: (0, h)
    wspec = pl.BlockSpec((2, None, HD, HD), lambda h: (0, h, 0, 0))
    return pl.pallas_call(
        body, name="lru_bwd", grid=(NH,),
        in_specs=[pl.BlockSpec(memory_space=pl.ANY), col, pl.BlockSpec((3, S, HD), lambda h: (1, 0, h)), col, col] + _lru_specs(S)
        + [ANY] * len(after),
        out_specs=[pl.BlockSpec((3, S, HD), lambda h: (1, 0, h)), pl.BlockSpec((4, HD), head), pl.BlockSpec((1, HD), head),
                   wspec, pl.BlockSpec((2, HD), head), wspec, pl.BlockSpec((2, HD), head), pl.BlockSpec((2, HD), head)],
        out_shape=[jax.ShapeDtypeStruct((6, S, D), BF16), jax.ShapeDtypeStruct((4, D), F32), jax.ShapeDtypeStruct((1, D), F32),
                   jax.ShapeDtypeStruct((2, NH, HD, HD), F32), jax.ShapeDtypeStruct((2, D), F32),
                   jax.ShapeDtypeStruct((2, NH, HD, HD), F32), jax.ShapeDtypeStruct((2, D), F32), jax.ShapeDtypeStruct((2, D), F32)],
        scratch_shapes=[pltpu.VMEM((S + 2 * PADR, HD), F32), pltpu.VMEM((S, HD), F32), pltpu.VMEM((S, HD), F32),
                        pltpu.VMEM((S + 2 * PADR, HD), F32), pltpu.VMEM((2, S, HD), F32), pltpu.VMEM((2, S, HD), F32)],
        input_output_aliases={0: 0},
        compiler_params=_cp(("parallel",)),
    )(dz6, dm, z6, h0, h1, cw, cb, wr, br, wi, bi, lam, *after)


LAYER_SMALL = ("norm1_g", "gmlp_ln_g", "gmlp_ln_b", "gmlp_w_s", "gmlp_b_s", "conv_w", "conv_b",
               "lru_w_r", "lru_b_r", "lru_w_i", "lru_b_i", "lru_lambda", "norm2_g")


def _layer_operands(l, p):
    ws_b = p["gmlp_w_s"][l].astype(BF16)
    tm = dict(ws_b=ws_b, wst_b=jnp.swapaxes(ws_b, 1, 2), bs_b=jnp.repeat(p["gmlp_b_s"][l].T, HD, axis=1),
              lg=p["gmlp_ln_g"][l][None], lb=p["gmlp_ln_b"][l][None])
    lru = (p["conv_w"][l], p["conv_b"][l][None], p["lru_w_r"][l].astype(BF16), p["lru_b_r"][l],
           p["lru_w_i"][l].astype(BF16), p["lru_b_i"][l], p["lru_lambda"][l])
    return (p["norm1_g"][l][None], p["norm2_g"][l][None]), tm, lru


def _forward_layer(l, x, p, wb, after=(), rest=None, near_end=None, operands=None, loss=None):
    (g1, g2), tm, lru = _layer_operands(l, p) if operands is None else operands
    z6, hn1 = _mm_in(x, g1, wb["w_in"], l, after)
    ya = _gmlp_fwd(z6, tm["ws_b"], tm["bs_b"], tm["lg"], tm["lb"])
    merged, h0, h1 = _lru_fwd(z6, ya, *lru)
    if rest is not None:
        wb = dict(wb, **rest(merged))
    x1 = _mm_res(merged, wb["w_out"], x, l, "mm_out")
    gu, ff, hn2 = _mm_ffn_in(x1, g2, wb["w_ffn_in"], l)
    if loss is None:
        x2 = _mm_res(ff, wb["w_ffn_out"], x1, l, "mm_ffn_out", () if near_end is None else tuple(near_end(gu)))
    else:
        x2 = _mm_res_loss(ff, wb["w_ffn_out"], x1, *loss)
    return x2, dict(x=x, z6=z6, h0=h0, h1=h1, merged=merged, x1=x1, gu=gu, ff=ff, g1=g1, g2=g2, tm=tm, lru=lru,
                    hn1=hn1, hn2=hn2, wb=wb)


def _backward_layer(l, dx, s, after=(), midway=None, midway2=None, late=None):
    S = dx.shape[0]
    tm, wb = s["tm"], s["wb"]
    g2 = s["g2"]
    dgu = _bwd_ffn_out(dx, wb["w_ffn_out"], s["gu"], l, after)
    tmb = min(TM_BIG, S)
    dwfo = _mm_tn(s["ff"], dx, DFF_SH, tmb, f"dw_ffn_out_{l}")
    dx1, dg2 = _mm_nt_rms_bwd(
        dgu, [pl.BlockSpec((None, tmb, DFF_SH), lambda i, k: (k, i, 0))],
        wb["w_ffn_in"], [pl.BlockSpec((None, D, DFF_SH), lambda i, k: (k, 0, 0))],
        4, tmb, s["x1"], g2, dx, f"bwd_ffn_in_{l}")
    dwfi = _dw_ffn_in(s["hn2"], dgu, l)
    dmg, dwo = _bwd_out(dx1, wb["w_out"], s["merged"], l)
    mid = () if midway is None else tuple(midway([dwo, dwfi, dwfo]))
    dz6, dws, dbs, dlg, dlb = _gmlp_bwd(dmg, s["z6"], tm["ws_b"], tm["wst_b"], tm["bs_b"], tm["lg"], tm["lb"], mid)
    mid2 = () if midway2 is None else tuple(midway2(dws))
    dz6, dcw, dcb, dwr, dbr, dwi, dbi, dlam = _lru_bwd(dz6, dmg, s["z6"], s["h0"], s["h1"], *s["lru"], after=mid2)

    sub = 3

    def dz_tile(j):
        return pl.BlockSpec((None, tmb, 512), lambda i, k: ((sub * k + j) // 2, i, (sub * k + j) % 2))

    def w_tile(j):
        def w_map(i, k):
            sh, tl = _in_tile(sub * k + j)
            return (sh, 0, tl)
        return pl.BlockSpec((None, D, 512), w_map)

    dwin = _dw_in(s["hn1"], dz6, l)
    small = dict(gmlp_ln_g=dlg[0], gmlp_ln_b=dlb[0], gmlp_w_s=dws, gmlp_b_s=dbs[:, :, 0], conv_w=dcw, conv_b=dcb[0],
                 lru_w_r=dwr, lru_b_r=dbr, lru_w_i=dwi, lru_b_i=dbi, lru_lambda=dlam, norm2_g=dg2[0])
    tail = () if late is None else tuple(late([dwin], small))
    dx0, dg1 = _mm_nt_rms_bwd(
        dz6, [dz_tile(j) for j in range(sub)], wb["w_in"], [w_tile(j) for j in range(sub)],
        N_IN_T // sub, tmb, s["x"], s["g1"], dx1, f"bwd_in_{l}", tail)
    return dx0, [dwin, dwo, dwfi, dwfo], dict(small, norm1_g=dg1[0])


def _local_step(x, tgt, p, wbs):
    saved = []
    for l in range(2):
        x, s = _forward_layer(l, x, p, wbs[l], loss=(tgt, p["final_g"][None]) if l else None)
        saved.append(s)
    dx, loss_v, dfg = x
    big, smalls = [None, None], [None, None]
    for l in (1, 0):
        dx, big[l], smalls[l] = _backward_layer(l, dx, saved[l])
    small = {k: jnp.stack([smalls[0][k], smalls[1][k]]) for k in LAYER_SMALL}
    small["final_g"] = dfg[0]
    return loss_v, dx, big, small


def _place():
    x, y, c = lax.axis_index("x"), lax.axis_index("y"), lax.axis_index("c")
    return x, y, c, 2 * x + y


def _chip_at(x, y, d):
    px = 1 - x if d & 2 else x
    py = 1 - y if d & 1 else y
    return px, py, 2 * px + py


HBM = pl.BlockSpec(memory_space=pltpu.HBM)
SEM = pl.BlockSpec(memory_space=pltpu.SEMAPHORE)
DATAFLOW = pltpu.SideEffectType.DATAFLOW_SIDE_EFFECTING


def _in_hbm(a):
    return pltpu.with_memory_space_constraint(a, pltpu.HBM)


def _cast_into(wfs, l, chip_arr, name):
    n = len(wfs)

    def body(ch_ref, *refs):
        for w_ref, o_ref in zip(refs[:n], refs[n:]):
            o_ref[...] = w_ref[...].astype(BF16)

    halves = [(wf.shape[1] // 2, wf.shape[2]) for wf in wfs]
    return pl.pallas_call(
        body, name=name, out_shape=[jax.ShapeDtypeStruct((4, 2, rh, cols), BF16) for rh, cols in halves],
        grid_spec=pltpu.PrefetchScalarGridSpec(
            num_scalar_prefetch=1, grid=(2,),
            in_specs=[pl.BlockSpec((None, None, rh, cols), lambda h, ch: (l, h, 0, 0)) for rh, cols in halves],
            out_specs=[pl.BlockSpec((None, None, rh, cols), lambda h, ch: (ch[0], h, 0, 0)) for rh, cols in halves]),
        compiler_params=_cp(("parallel",)),
    )(chip_arr, *[wf.reshape(2, 2, rh, cols) for wf, (rh, cols) in zip(wfs, halves)])


def _half_block(ref, chip, half, to, send_sem, recv_sem):
    blk = ref.at[chip, half]
    return pltpu.make_async_remote_copy(src_ref=blk, dst_ref=blk, send_sem=send_sem, recv_sem=recv_sem,
                                        device_id=to, device_id_type=MESH)


def _gather_weights(bufs, tiny):
    nt = len(bufs)
    n_ici = max(nt * 3, 1)

    def body(*refs):
        tiny_ref = refs[nt]
        o_refs, tiny_o = refs[nt + 1:2 * nt + 1], refs[2 * nt + 1]
        send, recv, fsend, frecv, tsend, trecv, lsem = refs[2 * nt + 2:]
        x, y, c, chip = _place()
        local = pltpu.make_async_copy(tiny_ref, tiny_o.at[chip], lsem)
        local.start()

        def tin(d, origin_chip, to):
            return pltpu.make_async_remote_copy(
                src_ref=tiny_ref, dst_ref=tiny_o.at[origin_chip], send_sem=tsend.at[d - 1], recv_sem=trecv.at[d - 1],
                device_id=to, device_id_type=MESH)

        sends = []
        for t in range(nt):
            for d in (1, 2, 3):
                px, py, _ = _chip_at(x, y, d)
                sends.append(_half_block(o_refs[t], chip, c, (px, py, c), send.at[3 * t + d - 1], recv.at[3 * t + d - 1]))
        for d in (1, 2, 3):
            px, py, _ = _chip_at(x, y, d)
            sends.append(tin(d, chip, (px, py, c)))
        for cp in sends:
            cp.start()
        passed = []
        for t in range(nt):
            for d in (1, 2, 3):
                k = 3 * t + d - 1
                _, _, pchip = _chip_at(x, y, d)
                _half_block(o_refs[t], pchip, c, (x, y, c), send.at[k], recv.at[k]).wait_recv()
                f = _half_block(o_refs[t], pchip, c, (x, y, 1 - c), fsend.at[k], frecv.at[k])
                f.start()
                passed.append(f)
        for t in range(nt):
            for d in (1, 2, 3):
                k = 3 * t + d - 1
                _, _, pchip = _chip_at(x, y, d)
                _half_block(o_refs[t], pchip, 1 - c, (x, y, 1 - c), fsend.at[k], frecv.at[k]).wait_recv()
        for d in (1, 2, 3):
            _, _, pchip = _chip_at(x, y, d)
            tin(d, pchip, (x, y, c)).wait_recv()
        for cp in sends + passed:
            cp.wait_send()
        local.wait()

    out_shape = [jax.ShapeDtypeStruct(b.shape, b.dtype) for b in bufs]
    out_shape.append(jax.ShapeDtypeStruct((4,) + tiny.shape, tiny.dtype))
    outs = pl.pallas_call(
        body, name="gather_weights_0", out_shape=out_shape,
        in_specs=[ANY] * (nt + 1), out_specs=[ANY] * (nt + 1),
        scratch_shapes=[pltpu.SemaphoreType.DMA((n_ici,)), pltpu.SemaphoreType.DMA((n_ici,)),
                        pltpu.SemaphoreType.DMA((n_ici,)), pltpu.SemaphoreType.DMA((n_ici,)),
                        pltpu.SemaphoreType.DMA((3,)), pltpu.SemaphoreType.DMA((3,)), pltpu.SemaphoreType.DMA],
        input_output_aliases={t: t for t in range(nt)},
        compiler_params=_cp(has_side_effects=True),
    )(*bufs, tiny)
    return outs[:nt], outs[nt]


def _gather_start(bufs, tag, after=()):
    nt, na = len(bufs), len(after)

    def body(*refs):
        b_refs = refs[:nt]
        send, recv = refs[nt + na], refs[nt + na + 1]
        token = refs[2 * nt + na + 2]
        x, y, c, chip = _place()
        for t in range(nt):
            for d in (1, 2, 3):
                px, py, _ = _chip_at(x, y, d)
                _half_block(b_refs[t], chip, c, (px, py, c), send.at[3 * t + d - 1], recv.at[3 * t + d - 1]).start()
        token[...] = jnp.zeros_like(token)

    outs = pl.pallas_call(
        body, name=f"gather_start_{tag}",
        out_shape=(pltpu.SemaphoreType.DMA((3 * nt,)), pltpu.SemaphoreType.DMA((3 * nt,)),
                   *[pltpu.HBM(b.shape, b.dtype) for b in bufs], jax.ShapeDtypeStruct((8, 128), F32)),
        in_specs=[HBM] * nt + [ANY] * na, out_specs=(SEM, SEM, *[HBM] * nt, pl.BlockSpec(memory_space=pltpu.VMEM)),
        input_output_aliases={t: 2 + t for t in range(nt)},
        compiler_params=pltpu.CompilerParams(has_side_effects=DATAFLOW),
    )(*[_in_hbm(b) for b in bufs], *after)
    return outs[0], outs[1], list(outs[2:2 + nt]), outs[2 + nt]


def _gather_wait(send, recv, bufs, after, tag):
    nt = len(bufs)

    def body(*refs):
        b_refs = refs[:nt]
        send_ref, recv_ref = refs[nt], refs[nt + 1]
        x, y, c, chip = _place()
        for t in range(nt):
            for d in (1, 2, 3):
                k = 3 * t + d - 1
                px, py, pchip = _chip_at(x, y, d)
                _half_block(b_refs[t], chip, c, (px, py, c), send_ref.at[k], recv_ref.at[k]).wait_send()
                _half_block(b_refs[t], pchip, c, (px, py, c), send_ref.at[k], recv_ref.at[k]).wait_recv()

    after = tuple(after) if isinstance(after, (tuple, list)) else (after,)
    outs = pl.pallas_call(
        body, name=f"gather_wait_{tag}", out_shape=[pltpu.HBM(b.shape, b.dtype) for b in bufs],
        in_specs=[HBM] * nt + [SEM, SEM] + [ANY] * len(after), out_specs=[HBM] * nt,
        input_output_aliases={t: t for t in range(nt)},
        compiler_params=pltpu.CompilerParams(has_side_effects=DATAFLOW),
    )(*bufs, send, recv, *after)
    return list(outs)


def _gather_pass_on(bufs, tag):
    nt = len(bufs)

    def body(*refs):
        o_refs = refs[nt:2 * nt]
        fsend, frecv = refs[2 * nt:]
        x, y, c, _ = _place()
        cps = []
        for t in range(nt):
            for d in (1, 2, 3):
                k = 3 * t + d - 1
                _, _, pchip = _chip_at(x, y, d)
                cps.append(_half_block(o_refs[t], pchip, c, (x, y, 1 - c), fsend.at[k], frecv.at[k]))
        for cp in cps:
            cp.start()
        for t in range(nt):
            for d in (1, 2, 3):
                k = 3 * t + d - 1
                _, _, pchip = _chip_at(x, y, d)
                _half_block(o_refs[t], pchip, 1 - c, (x, y, 1 - c), fsend.at[k], frecv.at[k]).wait_recv()
        for cp in cps:
            cp.wait_send()

    return pl.pallas_call(
        body, name=f"gather_pass_on_{tag}", out_shape=[jax.ShapeDtypeStruct(b.shape, b.dtype) for b in bufs],
        in_specs=[ANY] * nt, out_specs=[ANY] * nt,
        scratch_shapes=[pltpu.SemaphoreType.DMA((3 * nt,)), pltpu.SemaphoreType.DMA((3 * nt,))],
        input_output_aliases={t: t for t in range(nt)},
        compiler_params=_cp(has_side_effects=True),
    )(*bufs)


def _to_sibling_halves(gs, l):
    nt = len(gs)

    def body(*refs):
        g_refs, o_refs = refs[:nt], refs[nt:2 * nt]
        send, recv = refs[2 * nt:]
        x, y, c, _ = _place()
        cps = [pltpu.make_async_remote_copy(
            src_ref=g_refs[t].at[k, 1 - c], dst_ref=o_refs[t].at[k], send_sem=send.at[4 * t + k], recv_sem=recv.at[4 * t + k],
            device_id=(x, y, 1 - c), device_id_type=MESH) for t in range(nt) for k in range(4)]
        for cp in cps:
            cp.start()
        for cp in cps:
            cp.wait()

    return pl.pallas_call(
        body, name=f"grads_to_sibling_{l}", out_shape=[jax.ShapeDtypeStruct((4,) + g.shape[2:], g.dtype) for g in gs],
        in_specs=[ANY] * nt, out_specs=[ANY] * nt,
        scratch_shapes=[pltpu.SemaphoreType.DMA((4 * nt,)), pltpu.SemaphoreType.DMA((4 * nt,))],
        compiler_params=_cp(has_side_effects=True),
    )(*gs)


def _chip_copy(c_ref, land_ref, x, y, c, d, send_sem, recv_sem):
    px, py, pchip = _chip_at(x, y, d)
    return pltpu.make_async_remote_copy(src_ref=c_ref.at[pchip], dst_ref=land_ref.at[d - 1], send_sem=send_sem, recv_sem=recv_sem,
                                        device_id=(px, py, c), device_id_type=MESH)


def _exchange_start(srcs, lands, copies, nsem, name):
    ns, n = len(srcs), len(srcs) + len(lands)

    def body(*refs):
        for cp in copies(refs[:ns], refs[ns:n], refs[n], refs[n + 1]):
            cp.start()
        token = refs[2 * n + 2]
        token[...] = jnp.zeros_like(token)

    outs = pl.pallas_call(
        body, name=name,
        out_shape=(pltpu.SemaphoreType.DMA((nsem,)), pltpu.SemaphoreType.DMA((nsem,)),
                   *[pltpu.HBM(a.shape, a.dtype) for a in list(srcs) + list(lands)], jax.ShapeDtypeStruct((8, 128), F32)),
        in_specs=[HBM] * n, out_specs=(SEM, SEM, *[HBM] * n, pl.BlockSpec(memory_space=pltpu.VMEM)),
        input_output_aliases={i: 2 + i for i in range(n)},
        compiler_params=pltpu.CompilerParams(has_side_effects=DATAFLOW),
    )(*[_in_hbm(a) for a in list(srcs) + list(lands)])
    return outs[0], outs[1], list(outs[2:2 + ns]), list(outs[2 + ns:2 + n]), outs[2 + n]


def _exchange_wait(send, recv, srcs, lands, after, copies, name):
    ns, n = len(srcs), len(srcs) + len(lands)

    def body(*refs):
        for cp in copies(refs[:ns], refs[ns:n], refs[n], refs[n + 1]):
            cp.wait_send()
            cp.wait_recv()

    outs = pl.pallas_call(
        body, name=name, out_shape=[pltpu.HBM(a.shape, a.dtype) for a in list(srcs) + list(lands)],
        in_specs=[HBM] * n + [SEM, SEM, ANY], out_specs=[HBM] * n,
        input_output_aliases={i: i for i in range(n)},
        compiler_params=pltpu.CompilerParams(has_side_effects=DATAFLOW),
    )(*srcs, *lands, send, recv, after)
    return list(outs[:ns]), list(outs[ns:])


def _pass_on_copies(b_refs, land_refs, send, recv):
    del land_refs
    x, y, c, _ = _place()
    return [_half_block(b_refs[t], _chip_at(x, y, d)[2], c, (x, y, 1 - c), send.at[3 * t + d - 1], recv.at[3 * t + d - 1])
            for t in range(len(b_refs)) for d in (1, 2, 3)]


def _chips_copies(c_refs, land_refs, send, recv):
    x, y, c, _ = _place()
    return [_chip_copy(c_refs[t], land_refs[t], x, y, c, d, send.at[3 * t + d - 1], recv.at[3 * t + d - 1])
            for t in range(len(c_refs)) for d in (1, 2, 3)]


def _sibling_copies(g_refs, land_refs, send, recv):
    x, y, c, _ = _place()
    return [pltpu.make_async_remote_copy(
        src_ref=g_refs[t].at[k, 1 - c], dst_ref=land_refs[t].at[k], send_sem=send.at[4 * t + k], recv_sem=recv.at[4 * t + k],
        device_id=(x, y, 1 - c), device_id_type=MESH) for t in range(len(g_refs)) for k in range(4)]


def _join_copies(f_refs, land_refs, send, recv):
    del land_refs
    x, y, c, _ = _place()
    return [pltpu.make_async_remote_copy(
        src_ref=f_refs[t].at[c], dst_ref=f_refs[t].at[c], send_sem=send.at[t], recv_sem=recv.at[t],
        device_id=(x, y, 1 - c), device_id_type=MESH) for t in range(len(f_refs))]


def _add_half(gs, rs, c_arr, name):
    n = len(gs)

    def body(c_ref, *refs):
        for g_ref, r_ref, o_ref in zip(refs[:n], refs[n:2 * n], refs[2 * n:]):
            o_ref[...] = (g_ref[...] + r_ref[...]).astype(BF16)

    def own(g):
        return pl.BlockSpec((None, None) + g.shape[2:], lambda k, cr: (k, cr[0], 0, 0))

    def blk(g):
        return pl.BlockSpec((None,) + g.shape[2:], lambda k, cr: (k, 0, 0))

    return pl.pallas_call(
        body, name=name, out_shape=[jax.ShapeDtypeStruct((4,) + g.shape[2:], BF16) for g in gs],
        grid_spec=pltpu.PrefetchScalarGridSpec(
            num_scalar_prefetch=1, grid=(4,),
            in_specs=[own(g) for g in gs] + [blk(g) for g in gs], out_specs=[blk(g) for g in gs]),
        compiler_params=_cp(("parallel",)),
    )(c_arr, *gs, *rs)


def _sum_chips(css, r3s, place_arr, name):
    n = len(css)

    def body(pl_ref, *refs):
        up = lambda ref: ref[...].astype(F32)
        for t in range(n):
            a_ref, (r0_ref, r1_ref, r2_ref), o_ref = refs[t], refs[n + 3 * t:n + 3 * t + 3], refs[4 * n + t]
            o_ref[...] = ((up(a_ref) + up(r0_ref)) + up(r1_ref)) + up(r2_ref)

    def blk(cs, first):
        _, rh, cols = cs.shape
        return pl.BlockSpec((None, rh // 2, cols), lambda i, pa: (first(pa), i, 0))

    in_specs = [blk(cs, lambda pa: pa[0]) for cs in css]
    for cs in css:
        in_specs += [blk(cs, lambda pa, d=d: d) for d in range(3)]
    return pl.pallas_call(
        body, name=name, out_shape=[jax.ShapeDtypeStruct((2,) + cs.shape[1:], F32) for cs in css],
        grid_spec=pltpu.PrefetchScalarGridSpec(
            num_scalar_prefetch=1, grid=(2,), in_specs=in_specs, out_specs=[blk(cs, lambda pa: pa[1]) for cs in css]),
        compiler_params=_cp(("parallel",)),
    )(place_arr, *css, *[r3 for r3 in r3s for _ in range(3)])


def _allreduce_small(pack):
    rows = pack.shape[0]
    hr = rows // 2

    def body(p_ref, o_ref, sib, slots, s1, r1, s2, r2, s3, r3):
        x, y, c, chip = _place()
        sibling = (x, y, 1 - c)
        ex = pltpu.make_async_remote_copy(src_ref=p_ref, dst_ref=sib, send_sem=s1, recv_sem=r1,
                                          device_id=sibling, device_id_type=MESH)
        ex.start()
        ex.wait()
        half = pl.ds(pl.multiple_of(c * hr, 16), hr)
        slots[0] = (p_ref[half, :] + sib[half, :]).astype(BF16)
        cps = []
        for d in (1, 2, 3):
            px, py, _ = _chip_at(x, y, d)
            cps.append(pltpu.make_async_remote_copy(
                src_ref=slots.at[0], dst_ref=slots.at[d], send_sem=s2.at[d - 1], recv_sem=r2.at[d - 1],
                device_id=(px, py, c), device_id_type=MESH))
        for cp in cps:
            cp.start()
        for cp in cps:
            cp.wait()
        tot = slots[chip].astype(F32)
        for k in (1, 2, 3):
            tot = tot + slots[jnp.bitwise_xor(chip, k)].astype(F32)
        o_ref[half, :] = tot
        back = pltpu.make_async_remote_copy(src_ref=o_ref.at[half, :], dst_ref=o_ref.at[half, :], send_sem=s3, recv_sem=r3,
                                            device_id=sibling, device_id_type=MESH)
        back.start()
        back.wait()

    vm = pl.BlockSpec(memory_space=pltpu.VMEM)
    return pl.pallas_call(
        body, name="allreduce_small", out_shape=jax.ShapeDtypeStruct((rows, 128), F32),
        in_specs=[vm], out_specs=vm,
        scratch_shapes=[pltpu.VMEM((rows, 128), F32), pltpu.VMEM((4, hr, 128), BF16),
                        pltpu.SemaphoreType.DMA, pltpu.SemaphoreType.DMA, pltpu.SemaphoreType.DMA((3,)), pltpu.SemaphoreType.DMA((3,)),
                        pltpu.SemaphoreType.DMA, pltpu.SemaphoreType.DMA],
        compiler_params=_cp(has_side_effects=True),
    )(pack)


def _small_chip_sum(pack):
    rows = pack.shape[0]
    hr = rows // 2

    def body(p_ref, o_ref, sib, s1, r1):
        x, y, c, _ = _place()
        ex = pltpu.make_async_remote_copy(src_ref=p_ref, dst_ref=sib, send_sem=s1, recv_sem=r1,
                                          device_id=(x, y, 1 - c), device_id_type=MESH)
        ex.start()
        ex.wait()
        half = pl.ds(pl.multiple_of(c * hr, 16), hr)
        o_ref[...] = (p_ref[half, :] + sib[half, :]).astype(BF16)

    vm = pl.BlockSpec(memory_space=pltpu.VMEM)
    return pl.pallas_call(
        body, name="small_chip_sum", out_shape=jax.ShapeDtypeStruct((hr, 128), BF16), in_specs=[vm], out_specs=vm,
        scratch_shapes=[pltpu.VMEM((rows, 128), F32), pltpu.SemaphoreType.DMA, pltpu.SemaphoreType.DMA],
        compiler_params=_cp(has_side_effects=True),
    )(pack)


def _small_copies(c_refs, land_refs, send, recv):
    x, y, c, _ = _place()
    cps = []
    for d in (1, 2, 3):
        px, py, _ = _chip_at(x, y, d)
        cps.append(pltpu.make_async_remote_copy(src_ref=c_refs[0], dst_ref=land_refs[0].at[d - 1], send_sem=send.at[d - 1],
                                                recv_sem=recv.at[d - 1], device_id=(px, py, c), device_id_type=MESH))
    return cps


def _small_total(csum, land):
    hr = csum.shape[0]

    def body(c_ref, l_ref, o_ref, slots, s3, r3):
        x, y, c, chip = _place()
        slots[0] = c_ref[...]
        for d in (1, 2, 3):
            slots[d] = l_ref[d - 1]
        tot = slots[chip].astype(F32)
        for k in (1, 2, 3):
            tot = tot + slots[jnp.bitwise_xor(chip, k)].astype(F32)
        half = pl.ds(pl.multiple_of(c * hr, 16), hr)
        o_ref[half, :] = tot
        back = pltpu.make_async_remote_copy(src_ref=o_ref.at[half, :], dst_ref=o_ref.at[half, :], send_sem=s3, recv_sem=r3,
                                            device_id=(x, y, 1 - c), device_id_type=MESH)
        back.start()
        back.wait()

    vm = pl.BlockSpec(memory_space=pltpu.VMEM)
    return pl.pallas_call(
        body, name="small_total", out_shape=jax.ShapeDtypeStruct((2 * hr, 128), F32), in_specs=[vm, vm], out_specs=vm,
        scratch_shapes=[pltpu.VMEM((4, hr, 128), BF16), pltpu.SemaphoreType.DMA, pltpu.SemaphoreType.DMA],
        compiler_params=_cp(has_side_effects=True),
    )(csum, land)


def _adam_math(gv, wv, mv, vv):
    m2 = ADAM_B1 * mv + (1.0 - ADAM_B1) * gv
    v2 = ADAM_B2 * vv + (1.0 - ADAM_B2) * (gv * gv)
    m_hat = m2 / (1.0 - ADAM_B1 ** ADAM_STEP)
    v_hat = v2 / (1.0 - ADAM_B2 ** ADAM_STEP)
    return -ADAM_LR * (m_hat / (jnp.sqrt(v_hat) + ADAM_EPS) + ADAM_WD * wv), m2, v2


def _adam(g, w, m, v, name):
    rows, cols = g.shape
    rb = rows // 4

    def body(g_ref, w_ref, m_ref, v_ref, d_ref, m2_ref, v2_ref):
        d_ref[...], m2_ref[...], v2_ref[...] = _adam_math(g_ref[...], w_ref[...], m_ref[...], v_ref[...])

    blk = pl.BlockSpec((rb, cols), lambda i: (i, 0))
    shp = jax.ShapeDtypeStruct((rows, cols), F32)
    return pl.pallas_call(
        body, name=name, grid=(4,), in_specs=[blk] * 4, out_specs=[blk] * 3, out_shape=[shp] * 3,
        compiler_params=_cp(("parallel",)),
    )(g, w, m, v)


def _adam_layer(gs, ws, ms, vs, l, prevs, name):
    n = len(gs)
    prev = [a for p4 in prevs if p4 is not None for a in p4]

    def body(*refs):
        outs = refs[len(refs) - 4 * n:]
        for t in range(n):
            g_ref, w_ref, m_ref, v_ref = refs[4 * t:4 * t + 4]
            go_ref, d_ref, m2_ref, v2_ref = outs[4 * t:4 * t + 4]
            gv = g_ref[...]
            go_ref[...] = gv
            d_ref[...], m2_ref[...], v2_ref[...] = _adam_math(gv, w_ref[...], m_ref[...], v_ref[...])

    in_specs, out_specs, out_shape, operands, aliases = [], [], [], [], {}
    for t, g in enumerate(gs):
        rows, cols = g.shape
        lay = pl.BlockSpec((None, rows // 4, cols), lambda i: (l, i, 0))
        in_specs += [pl.BlockSpec((rows // 4, cols), lambda i: (i, 0)), lay, lay, lay]
        operands += [g, ws[t], ms[t], vs[t]]
        out_specs += [lay] * 4
        out_shape += [jax.ShapeDtypeStruct((2, rows, cols), F32)] * 4
    k = 4 * n
    for t, p4 in enumerate(prevs):
        if p4 is not None:
            for j in range(4):
                aliases[k] = 4 * t + j
                k += 1
    outs = pl.pallas_call(
        body, name=name, grid=(4,), in_specs=in_specs + [ANY] * len(prev), out_specs=out_specs, out_shape=out_shape,
        input_output_aliases=aliases, compiler_params=_cp(("parallel",)),
    )(*operands, *prev)
    return [list(outs[4 * t:4 * t + 4]) for t in range(n)]


def _rows128(a):
    return a.reshape(-1, 128)


def _pack(arrs, mult):
    parts = [_rows128(a) for a in arrs]
    rows = sum(q.shape[0] for q in parts)
    pad = -rows % mult
    if pad:
        parts.append(jnp.zeros((pad, 128), F32))
    return jnp.concatenate(parts, axis=0)


def _unpack(pack, shapes):
    out, o = [], 0
    for s in shapes:
        n = 1
        for e in s:
            n *= e
        out.append(pack[o:o + n // 128].reshape(s))
        o += n // 128
    return out


WEIGHTS = ['norm1_g', 'w_in', 'gmlp_ln_g', 'gmlp_ln_b', 'gmlp_w_s', 'gmlp_b_s', 'conv_w', 'conv_b', 'lru_w_r', 'lru_b_r', 'lru_w_i',
           'lru_b_i', 'lru_lambda', 'w_out', 'norm2_g', 'w_ffn_in', 'w_ffn_out', 'final_g']
BIG = ['w_in', 'w_out', 'w_ffn_in', 'w_ffn_out']
SMALL = [n for n in WEIGHTS if n not in BIG]
CHIP_SHARDED_SMALL = ['conv_w', 'lru_b_r', 'lru_b_i', 'lru_lambda']


def kernel(x, norm1_g, w_in, gmlp_ln_g, gmlp_ln_b, gmlp_w_s, gmlp_b_s, conv_w, conv_b, lru_w_r, lru_b_r, lru_w_i, lru_b_i, lru_lambda, w_out, norm2_g, w_ffn_in, w_ffn_out, final_g, loss_target, m_norm1_g, m_w_in, m_gmlp_ln_g, m_gmlp_ln_b, m_gmlp_w_s, m_gmlp_b_s, m_conv_w, m_conv_b, m_lru_w_r, m_lru_b_r, m_lru_w_i, m_lru_b_i, m_lru_lambda, m_w_out, m_norm2_g, m_w_ffn_in, m_w_ffn_out, m_final_g, v_norm1_g, v_w_in, v_gmlp_ln_g, v_gmlp_ln_b, v_gmlp_w_s, v_gmlp_b_s, v_conv_w, v_conv_b, v_lru_w_r, v_lru_b_r, v_lru_w_i, v_lru_b_i, v_lru_lambda, v_w_out, v_norm2_g, v_w_ffn_in, v_w_ffn_out, v_final_g):
    a = dict(locals())
    w = {n: a[n] for n in WEIGHTS}
    mom = {n: a["m_" + n] for n in WEIGHTS}
    var = {n: a["v_" + n] for n in WEIGHTS}
    _, _, c, chip = _place()
    c_arr, chip_arr = jnp.reshape(c, (1,)).astype(jnp.int32), jnp.reshape(chip, (1,)).astype(jnp.int32)
    place_arr = jnp.stack([chip, c]).astype(jnp.int32)

    first, rest = BIG[:1], BIG[1:]

    def as_weights(names, full):
        wb = {n: f.reshape(4, 2 * f.shape[2], f.shape[3]) for n, f in zip(names, full)}
        if "w_out" in wb:
            wb["w_out"] = wb["w_out"].reshape(D, D)
            wb["w_ffn_out"] = wb["w_ffn_out"].reshape(DFF, D)
        return wb

    def cast(names, l, tag):
        return _cast_into([w[n] for n in names], l, chip_arr, f"cast_{tag}")

    def landed(fly, names, after, tag):
        return as_weights(names, _gather_pass_on(_gather_wait(fly[0], fly[1], fly[2], after, tag), tag))

    tiny = _pack([w[n] for n in CHIP_SHARDED_SMALL], 8)
    _, tiny_full = _gather_weights([], tiny)
    fly_in = _gather_start(cast(first, 0, "in"), "in", after=(tiny_full,))
    fly0 = _gather_start(cast(rest, 0, "0"), "0", after=(fly_in[3],))
    fly1 = _gather_start(cast(BIG, 1, "1"), "1", after=(fly0[3],))
    p = {n: w[n] for n in SMALL}
    parts = [_unpack(tiny_full[k], [w[n].shape for n in CHIP_SHARDED_SMALL]) for k in range(4)]
    for i, n in enumerate(CHIP_SHARDED_SMALL):
        p[n] = jnp.concatenate([parts[k][i] for k in range(4)], axis=-1)

    operands = [_layer_operands(l, p) for l in range(2)]
    state_packs = [_pack([src[n] for n in SMALL], 32) for src in (w, mom, var)]
    ahead = tuple(jax.tree.leaves(operands)) + tuple(state_packs)

    passing = {}

    def pass_on_1(gu):
        bufs = _gather_wait(fly1[0], fly1[1], fly1[2], gu, "1")
        passing[1] = _exchange_start(bufs, [], _pass_on_copies, 3 * len(bufs), "gather_pass_on_start_1")
        return (passing[1][-1],)

    xa, saved0 = _forward_layer(0, x[0], p, landed(fly_in, first, (fly1[3],) + ahead, "in"), after=(fly0[3], fly1[3]),
                                rest=lambda merged: landed(fly0, rest, merged, "0"), near_end=pass_on_1, operands=operands[0])
    send, recv, bufs1, _, _ = passing[1]
    xb, saved1 = _forward_layer(
        1, xa, p, as_weights(BIG, _exchange_wait(send, recv, bufs1, [], xa, _pass_on_copies, "gather_pass_on_wait_1")[0]),
        operands=operands[1], loss=(loss_target[0], p["final_g"][None]))
    dxb, loss_v, dfg = xb
    loss = lax.psum(loss_v[0, 0], ("x", "y", "c"))

    out, flying = {}, {}

    def halves(grads):
        return [g.reshape(4, 2, -1, g.shape[-1]) for g in grads]

    def sibling_start(grads, names, l, tag):
        gs = halves(grads)
        lands = [lax.empty((4,) + g.shape[2:], g.dtype) for g in gs]
        flying["s" + tag] = (names, l) + tuple(
            _exchange_start(gs, lands, _sibling_copies, 4 * len(gs), f"grads_to_sibling_start_{tag}"))
        return (flying["s" + tag][-1],)

    def chips_start(gs, from_sib, names, l, tag):
        cs = _add_half(gs, from_sib, c_arr, f"add_half_{tag}")
        lands = [lax.empty((3,) + a.shape[1:], a.dtype) for a in cs]
        flying[tag] = (names, l) + tuple(_exchange_start(cs, lands, _chips_copies, 3 * len(cs), f"grads_to_chips_start_{tag}"))
        return (flying[tag][-1],)

    def sibling_finish(tag, after):
        names, l, send, recv, gs, lands, _ = flying["s" + tag]
        gs, from_sib = _exchange_wait(send, recv, gs, lands, after, _sibling_copies, f"grads_to_sibling_wait_{tag}")
        return chips_start(gs, from_sib, names, l, tag)

    def reduce_start(grads, names, l, tag):
        gs = halves(grads)
        return chips_start(gs, _to_sibling_halves(gs, tag), names, l, tag)

    def reduce_sums(tags, after):
        groups, ts = [], []
        for tag in tags:
            names, l, send, recv, cs, lands, _ = flying[tag]
            cs, lands = _exchange_wait(send, recv, cs, lands, after, _chips_copies, f"grads_to_chips_wait_{tag}")
            ts += _sum_chips(cs, lands, place_arr, f"sum_chips_{tag}")
            groups.append((tag, names))
        flying["j" + tags[0]] = (groups, l) + tuple(_exchange_start(ts, [], _join_copies, len(ts), f"grads_join_start_{tags[0]}"))
        return (flying["j" + tags[0]][-1],)

    def reduce_adam(tag0, after):
        groups, l, send, recv, ts, _, _ = flying["j" + tag0]
        joined = _exchange_wait(send, recv, ts, [], after, _join_copies, f"grads_join_wait_{tag0}")[0]
        for tag, names in groups:
            gs, joined = [j.reshape(w[n].shape[1:]) for n, j in zip(names, joined)], joined[len(names):]
            res = _adam_layer(gs, [w[n] for n in names], [mom[n] for n in names], [var[n] for n in names], l,
                              [out.get(n) for n in names], f"adam_{tag}")
            out.update(zip(names, res))

    def late1(grads, _):
        return sibling_finish("1a", grads[0]) + sibling_start(grads, first, 1, "1b")

    def midway0(grads):
        return reduce_sums(("1a", "1b"), grads[0]) + sibling_start(grads, rest, 0, "0a")

    def midway2_0(dws):
        reduce_adam("1a", dws)
        return sibling_finish("0a", dws)

    def stacked_small(small0):
        small = {k: jnp.stack([small0[k], small1[k]]) for k in LAYER_SMALL}
        return dict(small, final_g=dfg[0])

    def late0(grads, small0):
        toks = reduce_start(grads, first, 0, "0b")
        small = stacked_small(dict(small0, norm1_g=jnp.zeros((D,), F32)))
        csum = _small_chip_sum(_pack([small[n] for n in SMALL], 32))
        flying["small"] = _exchange_start([csum], [lax.empty((3,) + csum.shape, BF16)], _small_copies, 3, "small_to_chips_start")
        return toks + (flying["small"][-1],)

    dxa, big1, small1 = _backward_layer(1, dxb, saved1, midway=lambda grads: sibling_start(grads, rest, 1, "1a"), late=late1)
    dx, big0, small0 = _backward_layer(0, dxa, saved0, after=sibling_finish("1b", dxa), midway=midway0,
                                       midway2=midway2_0, late=late0)
    join_tok = reduce_sums(("0a", "0b"), dx)
    small = stacked_small(small0)

    full_shapes = [small[n].shape for n in SMALL]
    send, recv, csum, land, _ = flying["small"]
    csum, land = _exchange_wait(send, recv, csum, land, join_tok[0], _small_copies, "small_to_chips_wait")
    red = _unpack(_small_total(csum[0], land[0]), full_shapes)
    norm1_0 = _allreduce_small(_pack([small0["norm1_g"]], 32))[:D // 128].reshape(D)
    reduce_adam("0a", norm1_0)
    red[SMALL.index("norm1_g")] = red[SMALL.index("norm1_g")].at[0].set(norm1_0)
    g_small = []
    for n, g in zip(SMALL, red):
        if n in CHIP_SHARDED_SMALL:
            g = lax.dynamic_slice_in_dim(g, chip * w[n].shape[-1], w[n].shape[-1], axis=g.ndim - 1)
        g_small.append(g)
    shapes = [w[n].shape for n in SMALL]
    upd = [_unpack(u, shapes) for u in _adam(_pack(g_small, 32), *state_packs, "adam_small")]
    for i, n in enumerate(SMALL):
        out[n] = [g_small[i], upd[0][i], upd[1][i], upd[2][i]]

    return (loss, dx[None]) + tuple(out[n][i] for i in range(4) for n in WEIGHTS)
```

```python
import functools

import jax
import jax.numpy as jnp
from jax import lax
from jax.experimental import pallas as pl
from jax.experimental.pallas import tpu as pltpu

F32 = jnp.float32
BF16 = jnp.bfloat16
MESH = pl.DeviceIdType.MESH

D = 1024
NH = 8
HD = 128
CHUNK = 128
N_IN_T = 12
DFF = 2816
DFF_SH = 1408
EPS = 1e-6
LRU_C = 8.0
ADAM_LR, ADAM_B1, ADAM_B2, ADAM_EPS, ADAM_WD, ADAM_STEP = 0.001, 0.9, 0.999, 1e-08, 0.01, 10

TM = 512
TM_BIG = 1024
RT = 128
PADR = 8
VMEM_LIMIT = 56 * 1024 * 1024


def _cp(sem=None, **kw):
    if sem is not None:
        kw["dimension_semantics"] = sem
    return pltpu.CompilerParams(vmem_limit_bytes=VMEM_LIMIT, **kw)


_GC = 0.7978845608028654


def _sigmoid(x):
    return 0.5 * jnp.tanh(0.5 * x) + 0.5


_GK = 0.044715


def _gelu(x):
    t = jnp.tanh(x * (_GC + (_GC * _GK) * (x * x)))
    return x * (0.5 + 0.5 * t)


def _gelu_and_grad(x):
    x2 = x * x
    t = jnp.tanh(x * (_GC + (_GC * _GK) * x2))
    h = 0.5 + 0.5 * t
    return x * h, h + x * (1.0 - t * t) * (0.5 * _GC + (1.5 * _GC * _GK) * x2)


def _softplus_neg(lam):
    y = jnp.exp(-jnp.abs(lam))
    u = 1.0 + y
    l1p = jnp.where(u == 1.0, y, jnp.log(u) * y / (u - 1.0))
    return jnp.maximum(-lam, 0.0) + l1p


def _dot(a, b):
    return jnp.dot(a, b, preferred_element_type=F32)


def _dot_nt(a, b):
    return lax.dot_general(a, b, (((1,), (1,)), ((), ())), preferred_element_type=F32)


def _dot_tn(a, b):
    return lax.dot_general(a, b, (((0,), (0,)), ((), ())), preferred_element_type=F32)


def _rms_hat(x):
    r = lax.rsqrt(jnp.mean(x * x, axis=-1, keepdims=True) + EPS)
    return x * r, r


def _rms_bwd(dh, x, g):
    xh, r = _rms_hat(x)
    dxh = dh * g
    dx = r * (dxh - xh * jnp.mean(dxh * xh, axis=-1, keepdims=True))
    return dx, jnp.sum(dh * xh, axis=0, keepdims=True)


def _norm_into(x_ref, g_ref, h_ref):
    xh, _ = _rms_hat(x_ref[...])
    h_ref[...] = (xh * g_ref[...]).astype(BF16)


def _in_tile(j):
    m, hf = j // 2, j % 2
    orig = jnp.where(m < 2, m, jnp.where(m == 2, 4, jnp.where(m < 5, m - 1, 5)))
    t = orig * 2 + hf
    return t // 3, t % 3


ANY = pl.BlockSpec(memory_space=pl.ANY)


def _mm_in(x, g, w_in, l, after=()):
    S = x.shape[0]
    tm = min(2 * TM_BIG, S)

    def body(x_ref, g_ref, w0_ref, w1_ref, *rest):
        o_ref, h_ref = rest[-2:]

        @pl.when(pl.program_id(1) == 0)
        def _():
            _norm_into(x_ref, g_ref, h_ref)
        rp = min(TM, tm)
        for r0 in range(0, tm, rp):
            hv = h_ref[r0:r0 + rp, :]
            o_ref[r0:r0 + rp, 0:512] = _dot(hv, w0_ref[...]).astype(BF16)
            o_ref[r0:r0 + rp, 512:1024] = _dot(hv, w1_ref[...]).astype(BF16)

    def w_tile(hf):
        def w_map(i, m):
            sh, tl = _in_tile(2 * m + hf)
            return (sh, 0, tl)
        return pl.BlockSpec((None, D, 512), w_map)

    return pl.pallas_call(
        body, name=f"mm_in_{l}", grid=(S // tm, 6),
        in_specs=[pl.BlockSpec((tm, D), lambda i, m: (i, 0)), pl.BlockSpec((1, D), lambda i, m: (0, 0)),
                  w_tile(0), w_tile(1)] + [ANY] * len(after),
        out_specs=[pl.BlockSpec((None, tm, D), lambda i, m: (m, i, 0)), pl.BlockSpec((tm, D), lambda i, m: (i, 0))],
        out_shape=[jax.ShapeDtypeStruct((6, S, D), BF16), jax.ShapeDtypeStruct((S, D), BF16)],
        compiler_params=_cp(("parallel", "arbitrary")),
    )(x, g, w_in, w_in, *after)


def _mm_res(a, w, res, l, name, after=()):
    S, K = a.shape

    tm = TM

    def body(a_ref, w_ref, r_ref, *rest):
        rest[-1][...] = r_ref[...] + _dot(a_ref[...], w_ref[...])

    return pl.pallas_call(
        body, name=f"{name}_{l}", grid=(S // tm,),
        in_specs=[pl.BlockSpec((tm, K), lambda i: (i, 0)), pl.BlockSpec((K, D), lambda i: (0, 0)),
                  pl.BlockSpec((tm, D), lambda i: (i, 0))] + [ANY] * len(after),
        out_specs=pl.BlockSpec((tm, D), lambda i: (i, 0)),
        out_shape=jax.ShapeDtypeStruct((S, D), F32),
        compiler_params=_cp(("parallel",)),
    )(a, w, res, *after)


def _mm_ffn_in(x, g, w_fi, l):
    S = x.shape[0]

    tm = min(TM_BIG, S)

    def body(x_ref, g_ref, w_ref, gu_ref, ff_ref, h_ref):
        @pl.when(pl.program_id(1) == 0)
        def _():
            _norm_into(x_ref, g_ref, h_ref)
        for r0 in range(0, tm, TM):
            rows = slice(r0, r0 + TM)
            hv = h_ref[rows, :]
            ga = _dot(hv, w_ref[0])
            gb = _dot(hv, w_ref[1])
            sg = _sigmoid(ga)
            silu = ga * sg
            gu_ref[0, rows, :] = (gb * (sg + silu * (1.0 - sg))).astype(BF16)
            gu_ref[1, rows, :] = silu.astype(BF16)
            ff_ref[rows, :] = (silu * gb).astype(BF16)

    gu, ff, h = pl.pallas_call(
        body, name=f"mm_ffn_in_{l}", grid=(S // tm, 2),
        in_specs=[pl.BlockSpec((tm, D), lambda i, s: (i, 0)), pl.BlockSpec((1, D), lambda i, s: (0, 0)),
                  pl.BlockSpec((2, None, D, DFF_SH), lambda i, s: (0, s, 0, 0))],
        out_specs=[pl.BlockSpec((2, None, tm, DFF_SH), lambda i, s: (0, s, i, 0)),
                   pl.BlockSpec((tm, DFF_SH), lambda i, s: (i, s)),
                   pl.BlockSpec((tm, D), lambda i, s: (i, 0))],
        out_shape=[jax.ShapeDtypeStruct((2, 2, S, DFF_SH), BF16), jax.ShapeDtypeStruct((S, DFF), BF16),
                   jax.ShapeDtypeStruct((S, D), BF16)],
        compiler_params=_cp(("parallel", "arbitrary")),
    )(x, g, w_fi.reshape(2, 2, D, DFF_SH))
    return gu.reshape(4, S, DFF_SH), ff, h


def _gmlp_fwd(z6, ws_b, bs_b, lg, lb):
    S = z6.shape[1]

    def body(z_ref, ws_ref, bs_ref, lg_ref, lb_ref, o_ref, mix):
        gv = _gelu(z_ref[1].astype(F32))
        xc = gv - jnp.mean(gv, axis=-1, keepdims=True)
        rs = lax.rsqrt(jnp.mean(xc * xc, axis=-1, keepdims=True) + EPS)
        vb = (xc * rs * lg_ref[...] + lb_ref[...]).astype(BF16)
        for gi in range(NH):
            cs = slice(gi * HD, (gi + 1) * HD)
            mix[:, cs] = _dot(ws_ref[gi], vb[:, cs])
        o_ref[...] = (_sigmoid(z_ref[2].astype(F32)) * _gelu(z_ref[0].astype(F32)) * (mix[...] + bs_ref[...])).astype(BF16)

    return pl.pallas_call(
        body, name="gmlp_fwd", grid=(S // CHUNK,),
        in_specs=[pl.BlockSpec((3, CHUNK, D), lambda i: (0, i, 0)), pl.BlockSpec((NH, CHUNK, CHUNK), lambda i: (0, 0, 0)),
                  pl.BlockSpec((CHUNK, D), lambda i: (0, 0)), pl.BlockSpec((1, D), lambda i: (0, 0)),
                  pl.BlockSpec((1, D), lambda i: (0, 0))],
        out_specs=pl.BlockSpec((CHUNK, D), lambda i: (i, 0)),
        out_shape=jax.ShapeDtypeStruct((S, D), BF16),
        scratch_shapes=[pltpu.VMEM((CHUNK, D), F32)],
        compiler_params=_cp(("parallel",)),
    )(z6, ws_b, bs_b, lg, lb)


def _row_iota():
    return lax.broadcasted_iota(jnp.int32, (RT, HD), 0)


SUB = 8
UNROLL = 8
GRAD_ROWS = 512


def _scan_up(a, b, carry):
    row = lax.broadcasted_iota(jnp.int32, (SUB, HD), 0)
    masks = [(d, row >= d) for d in (1, 2, 4)]
    c = jnp.broadcast_to(carry, (SUB, HD))
    hs = []
    for j in range(RT // SUB):
        aj, bj = a[SUB * j:SUB * (j + 1)], b[SUB * j:SUB * (j + 1)]
        for d, m in masks:
            bj = bj + aj * jnp.where(m, pltpu.roll(bj, d, 0), 0.0)
            aj = aj * jnp.where(m, pltpu.roll(aj, d, 0), 1.0)
        h = bj + aj * c
        hs.append(h)
        c = jnp.broadcast_to(h[SUB - 1:SUB, :], (SUB, HD))
    return jnp.concatenate(hs, axis=0), hs[-1][SUB - 1:SUB, :]


def _scan_down(a, b, carry):
    row = lax.broadcasted_iota(jnp.int32, (SUB, HD), 0)
    masks = [(d, row < SUB - d) for d in (1, 2, 4)]
    c = jnp.broadcast_to(carry, (SUB, HD))
    hs = []
    for j in reversed(range(RT // SUB)):
        aj, bj = a[SUB * j:SUB * (j + 1)], b[SUB * j:SUB * (j + 1)]
        for d, m in masks:
            bj = bj + aj * jnp.where(m, pltpu.roll(bj, SUB - d, 0), 0.0)
            aj = aj * jnp.where(m, pltpu.roll(aj, SUB - d, 0), 1.0)
        h = bj + aj * c
        hs.append(h)
        c = jnp.broadcast_to(h[0:1, :], (SUB, HD))
    return jnp.concatenate(hs[::-1], axis=0), hs[-1][0:1, :]


def _decay(r, sp_d):
    log_a = -LRU_C * r * sp_d
    a = jnp.exp(log_a)
    return a, jnp.sqrt(jnp.maximum(-jnp.tanh(log_a) * (a * a + 1.0), 0.0))


def _decay_bwd(r, sp_d):
    log_a = -LRU_C * r * sp_d
    a = jnp.exp(log_a)
    m2 = jnp.maximum(-jnp.tanh(log_a) * (a * a + 1.0), 0.0)
    inv = lax.rsqrt(m2)
    return a, m2 * inv, inv


def _lru_gates(xc, d, wr_ref, br_ref, wi_ref, bi_ref, sp):
    xb = xc.astype(BF16)
    r = _sigmoid(_dot(xb, wr_ref[d]) + br_ref[d:d + 1, :])
    i = _sigmoid(_dot(xb, wi_ref[d]) + bi_ref[d:d + 1, :])
    a, mult = _decay(r, sp[d:d + 1, :])
    return r, i, a, mult


def _shifted(win, k):
    w = RT + 2 * PADR
    v = win if k == 0 else pltpu.roll(win, (-k) % w, 0)
    return v[PADR:PADR + RT]


def _conv_taps(win):
    return [_shifted(win, k) for k in (-1, 0, 1, 2)]


def _fill_padded(dst, src_ref, S):
    zeros = jnp.zeros((PADR, HD), F32)
    dst[0:PADR, :] = zeros
    dst[PADR + S:2 * PADR + S, :] = zeros

    def cp(i, c):
        t0 = pl.multiple_of(i * RT, RT)
        dst[pl.ds(t0 + PADR, RT), :] = src_ref[pl.ds(t0, RT), :].astype(F32)
        return c
    lax.fori_loop(0, S // RT, cp, 0)


def _conv_fwd_all(zxp, xc_s, cw_ref, cb_ref, S):
    def cv(i, c):
        t0 = pl.multiple_of(i * RT, RT)
        xm1, x0, xp1, xp2 = _conv_taps(zxp[pl.ds(t0, RT + 2 * PADR), :])
        xc_s[pl.ds(t0, RT), :] = (cb_ref[...] + xm1 * cw_ref[0:1, :] + x0 * cw_ref[1:2, :]
                                  + xp1 * cw_ref[2:3, :] + xp2 * cw_ref[3:4, :])
        return c
    lax.fori_loop(0, S // RT, cv, 0)


def _lru_specs(S):
    head = lambda h: (0, h)
    return [pl.BlockSpec((4, HD), head), pl.BlockSpec((1, HD), head),
            pl.BlockSpec((2, None, HD, HD), lambda h: (0, h, 0, 0)), pl.BlockSpec((2, HD), head),
            pl.BlockSpec((2, None, HD, HD), lambda h: (0, h, 0, 0)), pl.BlockSpec((2, HD), head),
            pl.BlockSpec((2, HD), head)]


def _lru_fwd(z6, ya, cw, cb, wr, br, wi, bi, lam):
    S = z6.shape[1]
    nt = S // RT

    def body(z_ref, ya_ref, cw_ref, cb_ref, wr_ref, br_ref, wi_ref, bi_ref, lam_ref, mg_ref, h0_ref, h1_ref, zxp, xc_s):
        sp = _softplus_neg(lam_ref[...])
        _fill_padded(zxp, z_ref.at[0], S)
        _conv_fwd_all(zxp, xc_s, cw_ref, cb_ref, S)

        def scans(i, carry):
            cu, cd = carry
            for u in range(UNROLL):
                j = i * UNROLL + u
                ru = pl.ds(pl.multiple_of(j * RT, RT), RT)
                rd = pl.ds(pl.multiple_of((nt - 1 - j) * RT, RT), RT)
                xu, xd = xc_s[ru, :], xc_s[rd, :]
                _, gi, a, mult = _lru_gates(xu, 0, wr_ref, br_ref, wi_ref, bi_ref, sp)
                hu, cu = _scan_up(a, mult * gi * xu, cu)
                h0_ref[ru, :] = hu
                _, gi, a, mult = _lru_gates(xd, 1, wr_ref, br_ref, wi_ref, bi_ref, sp)
                hd, cd = _scan_down(a, mult * gi * xd, cd)
                h1_ref[rd, :] = hd
            return cu, cd
        z1 = jnp.zeros((1, HD), F32)
        lax.fori_loop(0, nt // UNROLL, scans, (z1, z1))

        def merge(i, c):
            rows = pl.ds(pl.multiple_of(i * RT, RT), RT)
            yb = (h0_ref[rows, :] + h1_ref[rows, :]) * _gelu(z_ref[1, rows, :].astype(F32))
            mg_ref[rows, :] = (ya_ref[rows, :].astype(F32) + _sigmoid(z_ref[2, rows, :].astype(F32)) * yb).astype(BF16)
            return c
        lax.fori_loop(0, nt, merge, 0)

    col = pl.BlockSpec((S, HD), lambda h: (0, h))
    return pl.pallas_call(
        body, name="lru_fwd", grid=(NH,),
        in_specs=[pl.BlockSpec((3, S, HD), lambda h: (1, 0, h)), col] + _lru_specs(S),
        out_specs=[col, col, col],
        out_shape=[jax.ShapeDtypeStruct((S, D), BF16), jax.ShapeDtypeStruct((S, D), F32), jax.ShapeDtypeStruct((S, D), F32)],
        scratch_shapes=[pltpu.VMEM((S + 2 * PADR, HD), F32), pltpu.VMEM((S, HD), F32)],
        compiler_params=_cp(("parallel",)),
    )(z6, ya, cw, cb, wr, br, wi, bi, lam)


def _mm_res_loss(a, w, res, tgt, g):
    S, K = a.shape

    def body(a_ref, w_ref, r_ref, t_ref, g_ref, dx_ref, loss_ref, dg_ref):
        @pl.when(pl.program_id(0) == 0)
        def _():
            loss_ref[...] = jnp.zeros_like(loss_ref)
            dg_ref[...] = jnp.zeros_like(dg_ref)
        xv = r_ref[...] + _dot(a_ref[...], w_ref[...])
        xh, _ = _rms_hat(xv)
        e = xh * g_ref[...] - t_ref[...]
        loss_ref[...] += jnp.sum(e * e) * (0.5 / D)
        dx, dgs = _rms_bwd(e * (1.0 / D), xv, g_ref[...])
        dx_ref[...] = dx
        dg_ref[...] += dgs

    row = pl.BlockSpec((TM, D), lambda i: (i, 0))
    vec = pl.BlockSpec((1, D), lambda i: (0, 0))
    return pl.pallas_call(
        body, name="mm_ffn_out_loss", grid=(S // TM,),
        in_specs=[pl.BlockSpec((TM, K), lambda i: (i, 0)), pl.BlockSpec((K, D), lambda i: (0, 0)), row, row, vec],
        out_specs=[row, pl.BlockSpec((1, 128), lambda i: (0, 0)), vec],
        out_shape=[jax.ShapeDtypeStruct((S, D), F32), jax.ShapeDtypeStruct((1, 128), F32), jax.ShapeDtypeStruct((1, D), F32)],
        compiler_params=_cp(("arbitrary",)),
    )(a, w, res, tgt, g)


def _bwd_ffn_out(dx, w_fo, gu, l, after=()):
    S = dx.shape[0]

    tm = min(TM_BIG, S)

    def body(dx_ref, w_ref, gu_ref, *rest):
        o_ref = rest[-1]
        for r0 in range(0, tm, TM):
            rows = slice(r0, r0 + TM)
            d = _dot_nt(dx_ref[rows, :].astype(BF16), w_ref[...])
            o_ref[0, rows, :] = (d * gu_ref[0, rows, :].astype(F32)).astype(BF16)
            o_ref[1, rows, :] = (d * gu_ref[1, rows, :].astype(F32)).astype(BF16)

    pair = pl.BlockSpec((2, None, tm, DFF_SH), lambda i, s: (0, s, i, 0))
    dgu = pl.pallas_call(
        body, name=f"bwd_ffn_out_{l}", grid=(S // tm, 2),
        in_specs=[pl.BlockSpec((tm, D), lambda i, s: (i, 0)), pl.BlockSpec((DFF_SH, D), lambda i, s: (s, 0)), pair]
        + [ANY] * len(after),
        out_specs=pair,
        out_shape=jax.ShapeDtypeStruct((2, 2, S, DFF_SH), BF16),
        compiler_params=_cp(("parallel", "arbitrary")),
    )(dx, w_fo, gu.reshape(2, 2, S, DFF_SH), *after)
    return dgu.reshape(4, S, DFF_SH)


def _mm_tn(a, b, m_blk, tk, name):
    S, M = a.shape

    def body(a_ref, b_ref, o_ref):
        @pl.when(pl.program_id(1) == 0)
        def _():
            o_ref[...] = jnp.zeros_like(o_ref)
        o_ref[...] += _dot_tn(a_ref[...], b_ref[...].astype(BF16))

    return pl.pallas_call(
        body, name=name, grid=(M // m_blk, S // tk),
        in_specs=[pl.BlockSpec((tk, m_blk), lambda m, k: (k, m)), pl.BlockSpec((tk, D), lambda m, k: (k, 0))],
        out_specs=pl.BlockSpec((m_blk, D), lambda m, k: (m, 0)),
        out_shape=jax.ShapeDtypeStruct((M, D), F32),
        compiler_params=_cp(("parallel", "arbitrary")),
    )(a, b)


def _mm_nt_rms_bwd(a, a_specs, w, w_specs, nk, tm, x, g, dres, name, after=()):
    S = x.shape[0]
    sub = len(a_specs)

    def body(*refs):
        a_refs, w_refs = refs[:sub], refs[sub:2 * sub]
        x_ref, g_ref, r_ref = refs[2 * sub:2 * sub + 3]
        dx_ref, dg_ref, acc = refs[-3:]
        i, k = pl.program_id(0), pl.program_id(1)
        @pl.when(k == 0)
        def _():
            acc[...] = jnp.zeros_like(acc)
        for j in range(sub):
            acc[...] += _dot_nt(a_refs[j][...], w_refs[j][...])

        @pl.when(jnp.logical_and(i == 0, k == 0))
        def _():
            dg_ref[...] = jnp.zeros_like(dg_ref)

        @pl.when(k == nk - 1)
        def _():
            dx, dgs = _rms_bwd(acc[...], x_ref[...], g_ref[...])
            dx_ref[...] = r_ref[...] + dx
            dg_ref[...] += dgs

    row = pl.BlockSpec((tm, D), lambda i, k: (i, 0))
    vec = pl.BlockSpec((1, D), lambda i, k: (0, 0))
    return pl.pallas_call(
        body, name=name, grid=(S // tm, nk),
        in_specs=list(a_specs) + list(w_specs) + [row, vec, row] + [ANY] * len(after),
        out_specs=[row, vec],
        out_shape=[jax.ShapeDtypeStruct((S, D), F32), jax.ShapeDtypeStruct((1, D), F32)],
        scratch_shapes=[pltpu.VMEM((tm, D), F32)],
        compiler_params=_cp(("arbitrary", "arbitrary")),
    )(*[a] * sub, *[w] * sub, x, g, dres, *after)


def _dw_ffn_in(h, dgu, l):
    S = h.shape[0]

    def body(h_ref, b_ref, o_ref):
        @pl.when(pl.program_id(1) == 0)
        def _():
            o_ref[...] = jnp.zeros_like(o_ref)
        o_ref[...] += _dot_tn(h_ref[...], b_ref[...])

    tk = min(2 * TM_BIG, S)
    return pl.pallas_call(
        body, name=f"dw_ffn_in_{l}", grid=(4, S // tk),
        in_specs=[pl.BlockSpec((tk, D), lambda j, k: (k, 0)), pl.BlockSpec((None, tk, DFF_SH), lambda j, k: (j, k, 0))],
        out_specs=pl.BlockSpec((None, D, DFF_SH), lambda j, k: (j, 0, 0)),
        out_shape=jax.ShapeDtypeStruct((4, D, DFF_SH), F32),
        compiler_params=_cp(("parallel", "arbitrary")),
    )(h, dgu)


_HALF_COMPS = ((0, 1, 3), (4, 2, 5))


def _dw_in(h, dz6, l):
    S = h.shape[0]

    def body(h_ref, d0_ref, d1_ref, d2_ref, o_ref):
        @pl.when(pl.program_id(1) == 0)
        def _():
            o_ref[...] = jnp.zeros_like(o_ref)
        hv = h_ref[...]
        for q, d_ref in enumerate((d0_ref, d1_ref, d2_ref)):
            for hf in range(2):
                col = 1024 * q + 512 * hf
                o_ref[col // 1536, :, col % 1536:col % 1536 + 512] += _dot_tn(hv, d_ref[:, 512 * hf:512 * (hf + 1)])

    tk = min(TM_BIG, S)

    def comp(q):
        return pl.BlockSpec((None, tk, D), lambda p, k: (jnp.where(p == 0, _HALF_COMPS[0][q], _HALF_COMPS[1][q]), k, 0))

    return pl.pallas_call(
        body, name=f"dw_in_{l}", grid=(2, S // tk),
        in_specs=[pl.BlockSpec((tk, D), lambda p, k: (k, 0)), comp(0), comp(1), comp(2)],
        out_specs=pl.BlockSpec((2, D, 1536), lambda p, k: (p, 0, 0)),
        out_shape=jax.ShapeDtypeStruct((4, D, 1536), F32),
        compiler_params=_cp(("parallel", "arbitrary")),
    )(h, dz6, dz6, dz6)


def _bwd_out(dx, w_o, merged, l):
    S = dx.shape[0]

    def body(dx_ref, w_ref, m_ref, dm_ref, dw_ref):
        @pl.when(pl.program_id(0) == 0)
        def _():
            dw_ref[...] = jnp.zeros_like(dw_ref)
        dxb = dx_ref[...].astype(BF16)
        dm_ref[...] = _dot_nt(dxb, w_ref[...]).astype(BF16)
        dw_ref[...] += _dot_tn(m_ref[...], dxb)

    tm = TM
    row = pl.BlockSpec((tm, D), lambda i: (i, 0))
    return pl.pallas_call(
        body, name=f"bwd_out_{l}", grid=(S // tm,),
        in_specs=[row, pl.BlockSpec((D, D), lambda i: (0, 0)), row],
        out_specs=[row, pl.BlockSpec((D, D), lambda i: (0, 0))],
        out_shape=[jax.ShapeDtypeStruct((S, D), BF16), jax.ShapeDtypeStruct((D, D), F32)],
        compiler_params=_cp(("arbitrary",)),
    )(dx, w_o, merged)


def _gmlp_bwd(dm, z6, ws_b, wst_b, bs_b, lg, lb, after=()):
    S = z6.shape[1]

    def body(dm_ref, z_ref, ws_ref, wst_ref, bs_ref, lg_ref, lb_ref, *rest):
        dz_ref, dws_ref, dbs_ref, dlg_ref, dlb_ref, mix, dv = rest[-7:]

        @pl.when(pl.program_id(0) == 0)
        def _():
            dws_ref[...] = jnp.zeros_like(dws_ref)
            dbs_ref[...] = jnp.zeros_like(dbs_ref)
            dlg_ref[...] = jnp.zeros_like(dlg_ref)
            dlb_ref[...] = jnp.zeros_like(dlb_ref)
        gv, dgelu_v = _gelu_and_grad(z_ref[1].astype(F32))
        xc = gv - jnp.mean(gv, axis=-1, keepdims=True)
        rs = lax.rsqrt(jnp.mean(xc * xc, axis=-1, keepdims=True) + EPS)
        vh = xc * rs
        vb = (vh * lg_ref[...] + lb_ref[...]).astype(BF16)
        for gi in range(NH):
            cs = slice(gi * HD, (gi + 1) * HD)
            mix[:, cs] = _dot(ws_ref[gi], vb[:, cs])
        u, dgelu_u = _gelu_and_grad(z_ref[0].astype(F32))
        sa = _sigmoid(z_ref[2].astype(F32))
        mixed = mix[...] + bs_ref[...]
        dyg = dm_ref[...].astype(F32)
        dz_ref[2] = (dyg * u * mixed * sa * (1.0 - sa)).astype(BF16)
        dya = dyg * sa
        dz_ref[0] = (dya * mixed * dgelu_u).astype(BF16)
        dmix = dya * u
        dmb = dmix.astype(BF16)
        for gi in range(NH):
            cs = slice(gi * HD, (gi + 1) * HD)
            dv[:, cs] = _dot(wst_ref[gi], dmb[:, cs])
            dws_ref[gi] += _dot_nt(dmb[:, cs], vb[:, cs])
            dbs_ref[gi] += jnp.broadcast_to(jnp.sum(dmix[:, cs], axis=1, keepdims=True), (CHUNK, HD))
        dvv = dv[...]
        dlg_ref[...] += jnp.sum(dvv * vh, axis=0, keepdims=True)
        dlb_ref[...] += jnp.sum(dvv, axis=0, keepdims=True)
        dvh = dvv * lg_ref[...]
        dgv = rs * (dvh - jnp.mean(dvh, axis=-1, keepdims=True) - vh * jnp.mean(dvh * vh, axis=-1, keepdims=True))
        dz_ref[1] = (dgv * dgelu_v).astype(BF16)

    vec = pl.BlockSpec((1, D), lambda i: (0, 0))
    mat = pl.BlockSpec((NH, CHUNK, CHUNK), lambda i: (0, 0, 0))
    return pl.pallas_call(
        body, name="gmlp_bwd", grid=(S // CHUNK,),
        in_specs=[pl.BlockSpec((CHUNK, D), lambda i: (i, 0)), pl.BlockSpec((3, CHUNK, D), lambda i: (0, i, 0)), mat, mat,
                  pl.BlockSpec((CHUNK, D), lambda i: (0, 0)), vec, vec] + [ANY] * len(after),
        out_specs=[pl.BlockSpec((3, CHUNK, D), lambda i: (0, i, 0)), mat, mat, vec, vec],
        out_shape=[jax.ShapeDtypeStruct((6, S, D), BF16), jax.ShapeDtypeStruct((NH, CHUNK, CHUNK), F32),
                   jax.ShapeDtypeStruct((NH, CHUNK, HD), F32), jax.ShapeDtypeStruct((1, D), F32), jax.ShapeDtypeStruct((1, D), F32)],
        scratch_shapes=[pltpu.VMEM((CHUNK, D), F32), pltpu.VMEM((CHUNK, D), F32)],
        compiler_params=_cp(("arbitrary",)),
    )(dm, z6, ws_b, wst_b, bs_b, lg, lb, *after)


def _lru_bwd(dz6, dm, z6, h0, h1, cw, cb, wr, br, wi, bi, lam, after=()):
    S = z6.shape[1]
    nt = S // RT

    def body(dz_in, dm_ref, z_ref, h0_ref, h1_ref, cw_ref, cb_ref, wr_ref, br_ref, wi_ref, bi_ref, lam_ref, *rest):
        dz_ref, dcw_ref, dcb_ref, dwr_ref, dbr_ref, dwi_ref, dbi_ref, dlam_ref, zxp, xc_s, dhs_s, dxcp, r_s, lam_s = rest[-14:]
        del dz_in
        lam = lam_ref[...]
        sp = _softplus_neg(lam)
        row = _row_iota()
        _fill_padded(zxp, z_ref.at[0], S)
        _conv_fwd_all(zxp, xc_s, cw_ref, cb_ref, S)
        zeros = jnp.zeros((PADR, HD), F32)
        dxcp[0:PADR, :] = zeros
        dxcp[PADR + S:2 * PADR + S, :] = zeros
        dwr_ref[...] = jnp.zeros_like(dwr_ref)
        dwi_ref[...] = jnp.zeros_like(dwi_ref)

        def pre(i, c):
            rows = pl.ds(pl.multiple_of(i * RT, RT), RT)
            hs = h0_ref[rows, :] + h1_ref[rows, :]
            dmv = dm_ref[rows, :].astype(F32)
            sb = _sigmoid(z_ref[2, rows, :].astype(F32))
            gg, dgg = _gelu_and_grad(z_ref[1, rows, :].astype(F32))
            dz_ref[2, rows, :] = (dmv * hs * gg * sb * (1.0 - sb)).astype(BF16)
            dyb = dmv * sb
            dz_ref[1, rows, :] = (dyb * hs * dgg).astype(BF16)
            dhs_s[rows, :] = dyb * gg
            return c
        lax.fori_loop(0, nt, pre, 0)

        def gate_bwd(d, gates, lamv, da, xc):
            r, gi, a, mult, inv_mult = gates
            lx, lm = lamv * xc, lamv * mult
            dlog_r = (da - (lx * gi) * (a * inv_mult)) * a * r
            dpr = dlog_r * (1.0 - r) * (-LRU_C * sp[d:d + 1, :])
            dpi = (lx * mult) * gi * (1.0 - gi)
            xb, dprb, dpib = xc.astype(BF16), dpr.astype(BF16), dpi.astype(BF16)
            dwr_ref[d] += _dot_tn(xb, dprb)
            dwi_ref[d] += _dot_tn(xb, dpib)
            dxc = lm * gi + _dot_nt(dprb, wr_ref[d]) + _dot_nt(dpib, wi_ref[d])
            return dxc, (jnp.sum(dlog_r, axis=0, keepdims=True) * (-LRU_C), jnp.sum(dpr, axis=0, keepdims=True),
                         jnp.sum(dpi, axis=0, keepdims=True))

        def rgates(i, c):
            for u in range(UNROLL):
                rows = pl.ds(pl.multiple_of((i * UNROLL + u) * RT, RT), RT)
                xb = xc_s[rows, :].astype(BF16)
                for d in range(2):
                    r_s[d, rows, :] = _sigmoid(_dot(xb, wr_ref[d]) + br_ref[d:d + 1, :])
            return c
        lax.fori_loop(0, nt // UNROLL, rgates, 0)

        def chains(i, carry):
            qn, qp = carry
            for u in range(UNROLL):
                j = i * UNROLL + u
                rd = pl.ds(pl.multiple_of((nt - 1 - j) * RT, RT), RT)
                a, dhs = _decay(r_s[0, rd, :], sp[0:1, :])[0], dhs_s[rd, :]
                q, q_first = _scan_down(a, a * dhs, qn)
                lam_s[0, rd, :] = dhs + jnp.where(row == RT - 1, qn, pltpu.roll(q, RT - 1, 0))
                qn = q_first
                ru = pl.ds(pl.multiple_of(j * RT, RT), RT)
                a, dhs = _decay(r_s[1, ru, :], sp[1:2, :])[0], dhs_s[ru, :]
                q, q_last = _scan_up(a, a * dhs, qp)
                lam_s[1, ru, :] = dhs + jnp.where(row == 0, qp, pltpu.roll(q, 1, 0))
                qp = q_last
            return qn, qp

        z1 = jnp.zeros((1, HD), F32)
        lax.fori_loop(0, nt // UNROLL, chains, (z1, z1))

        ct = min(GRAD_ROWS, S)
        crow = lax.broadcasted_iota(jnp.int32, (ct, HD), 0)

        def tile_grads(i, acc):
            t0 = pl.multiple_of(i * ct, ct)
            rows = pl.ds(t0, ct)
            xc = xc_s[rows, :]
            xb = xc.astype(BF16)
            tp = pl.multiple_of(jnp.maximum(t0 - PADR, 0), PADR)
            prev = jnp.where(t0 > 0, h0_ref[pl.ds(tp, PADR), :][PADR - 1:PADR, :], 0.0)
            tn = pl.multiple_of(jnp.minimum(t0 + ct, S - PADR), PADR)
            nxt = jnp.where(t0 + ct < S, h1_ref[pl.ds(tn, PADR), :][0:1, :], 0.0)
            hside = (jnp.where(crow == 0, prev, pltpu.roll(h0_ref[rows, :], 1, 0)),
                     jnp.where(crow == ct - 1, nxt, pltpu.roll(h1_ref[rows, :], ct - 1, 0)))
            dxc, sums = 0.0, ()
            for d in range(2):
                r = r_s[d, rows, :]
                gi = _sigmoid(_dot(xb, wi_ref[d]) + bi_ref[d:d + 1, :])
                lamv = lam_s[d, rows, :]
                dxc_d, s_d = gate_bwd(d, (r, gi) + _decay_bwd(r, sp[d:d + 1, :]), lamv, lamv * hside[d], xc)
                dxc = dxc + dxc_d
                sums = sums + s_d
            dxcp[pl.ds(t0 + PADR, ct), :] = dxc
            return tuple(x + y for x, y in zip(acc, sums))

        s_sp0, s_br0, s_bi0, s_sp1, s_br1, s_bi1 = lax.fori_loop(0, S // ct, tile_grads, (z1,) * 6)

        dsp = jnp.concatenate([s_sp0, s_sp1], axis=0)
        dlam_ref[...] = -dsp * _sigmoid(-lam)
        dbr_ref[...] = jnp.concatenate([s_br0, s_br1], axis=0)
        dbi_ref[...] = jnp.concatenate([s_bi0, s_bi1], axis=0)

        def conv_bwd(i, carry):
            c0, c1, c2, c3, cb_ = carry
            t0 = pl.multiple_of(i * RT, RT)
            dwin = dxcp[pl.ds(t0, RT + 2 * PADR), :]
            d0 = _shifted(dwin, 0)
            dz_ref[0, pl.ds(t0, RT), :] = (_shifted(dwin, 1) * cw_ref[0:1, :] + d0 * cw_ref[1:2, :]
                                           + _shifted(dwin, -1) * cw_ref[2:3, :] + _shifted(dwin, -2) * cw_ref[3:4, :]).astype(BF16)
            xm1, x0, xp1, xp2 = _conv_taps(zxp[pl.ds(t0, RT + 2 * PADR), :])
            sm = lambda v: jnp.sum(v, axis=0, keepdims=True)
            return c0 + sm(d0 * xm1), c1 + sm(d0 * x0), c2 + sm(d0 * xp1), c3 + sm(d0 * xp2), cb_ + sm(d0)

        c0, c1, c2, c3, cb_ = lax.fori_loop(0, nt, conv_bwd, (z1, z1, z1, z1, z1))
        dcw_ref[...] = jnp.concatenate([c0, c1, c2, c3], axis=0)
        dcb_ref[...] = cb_

    col = pl.BlockSpec((S, HD), lambda h: (0, h))
    head = lambda h: (0, h)
    wspec = pl.BlockSpec((2, None, HD, HD), lambda h: (0, h, 0, 0))
    return pl.pallas_call(
        body, name="lru_bwd", grid=(NH,),
        in_specs=[pl.BlockSpec(memory_space=pl.ANY), col, pl.BlockSpec((3, S, HD), lambda h: (1, 0, h)), col, col] + _lru_specs(S)
        + [ANY] * len(after),
        out_specs=[pl.BlockSpec((3, S, HD), lambda h: (1, 0, h)), pl.BlockSpec((4, HD), head), pl.BlockSpec((1, HD), head),
                   wspec, pl.BlockSpec((2, HD), head), wspec, pl.BlockSpec((2, HD), head), pl.BlockSpec((2, HD), head)],
        out_shape=[jax.ShapeDtypeStruct((6, S, D), BF16), jax.ShapeDtypeStruct((4, D), F32), jax.ShapeDtypeStruct((1, D), F32),
                   jax.ShapeDtypeStruct((2, NH, HD, HD), F32), jax.ShapeDtypeStruct((2, D), F32),
                   jax.ShapeDtypeStruct((2, NH, HD, HD), F32), jax.ShapeDtypeStruct((2, D), F32), jax.ShapeDtypeStruct((2, D), F32)],
        scratch_shapes=[pltpu.VMEM((S + 2 * PADR, HD), F32), pltpu.VMEM((S, HD), F32), pltpu.VMEM((S, HD), F32),
                        pltpu.VMEM((S + 2 * PADR, HD), F32), pltpu.VMEM((2, S, HD), F32), pltpu.VMEM((2, S, HD), F32)],
        input_output_aliases={0: 0},
        compiler_params=_cp(("parallel",)),
    )(dz6, dm, z6, h0, h1, cw, cb, wr, br, wi, bi, lam, *after)


LAYER_SMALL = ("norm1_g", "gmlp_ln_g", "gmlp_ln_b", "gmlp_w_s", "gmlp_b_s", "conv_w", "conv_b",
               "lru_w_r", "lru_b_r", "lru_w_i", "lru_b_i", "lru_lambda", "norm2_g")


def _layer_operands(l, p):
    ws_b = p["gmlp_w_s"][l].astype(BF16)
    tm = dict(ws_b=ws_b, wst_b=jnp.swapaxes(ws_b, 1, 2), bs_b=jnp.repeat(p["gmlp_b_s"][l].T, HD, axis=1),
              lg=p["gmlp_ln_g"][l][None], lb=p["gmlp_ln_b"][l][None])
    lru = (p["conv_w"][l], p["conv_b"][l][None], p["lru_w_r"][l].astype(BF16), p["lru_b_r"][l],
           p["lru_w_i"][l].astype(BF16), p["lru_b_i"][l], p["lru_lambda"][l])
    return (p["norm1_g"][l][None], p["norm2_g"][l][None]), tm, lru


def _forward_layer(l, x, p, wb, after=(), rest=None, near_end=None, operands=None, loss=None):
    (g1, g2), tm, lru = _layer_operands(l, p) if operands is None else operands
    z6, hn1 = _mm_in(x, g1, wb["w_in"], l, after)
    ya = _gmlp_fwd(z6, tm["ws_b"], tm["bs_b"], tm["lg"], tm["lb"])
    merged, h0, h1 = _lru_fwd(z6, ya, *lru)
    if rest is not None:
        wb = dict(wb, **rest(merged))
    x1 = _mm_res(merged, wb["w_out"], x, l, "mm_out")
    gu, ff, hn2 = _mm_ffn_in(x1, g2, wb["w_ffn_in"], l)
    if loss is None:
        x2 = _mm_res(ff, wb["w_ffn_out"], x1, l, "mm_ffn_out", () if near_end is None else tuple(near_end(gu)))
    else:
        x2 = _mm_res_loss(ff, wb["w_ffn_out"], x1, *loss)
    return x2, dict(x=x, z6=z6, h0=h0, h1=h1, merged=merged, x1=x1, gu=gu, ff=ff, g1=g1, g2=g2, tm=tm, lru=lru,
                    hn1=hn1, hn2=hn2, wb=wb)


def _backward_layer(l, dx, s, after=(), midway=None, midway2=None, late=None):
    S = dx.shape[0]
    tm, wb = s["tm"], s["wb"]
    g2 = s["g2"]
    dgu = _bwd_ffn_out(dx, wb["w_ffn_out"], s["gu"], l, after)
    tmb = min(TM_BIG, S)
    dwfo = _mm_tn(s["ff"], dx, DFF_SH, tmb, f"dw_ffn_out_{l}")
    dx1, dg2 = _mm_nt_rms_bwd(
        dgu, [pl.BlockSpec((None, tmb, DFF_SH), lambda i, k: (k, i, 0))],
        wb["w_ffn_in"], [pl.BlockSpec((None, D, DFF_SH), lambda i, k: (k, 0, 0))],
        4, tmb, s["x1"], g2, dx, f"bwd_ffn_in_{l}")
    dwfi = _dw_ffn_in(s["hn2"], dgu, l)
    dmg, dwo = _bwd_out(dx1, wb["w_out"], s["merged"], l)
    mid = () if midway is None else tuple(midway([dwo, dwfi, dwfo]))
    dz6, dws, dbs, dlg, dlb = _gmlp_bwd(dmg, s["z6"], tm["ws_b"], tm["wst_b"], tm["bs_b"], tm["lg"], tm["lb"], mid)
    mid2 = () if midway2 is None else tuple(midway2(dws))
    dz6, dcw, dcb, dwr, dbr, dwi, dbi, dlam = _lru_bwd(dz6, dmg, s["z6"], s["h0"], s["h1"], *s["lru"], after=mid2)

    sub = 3

    def dz_tile(j):
        return pl.BlockSpec((None, tmb, 512), lambda i, k: ((sub * k + j) // 2, i, (sub * k + j) % 2))

    def w_tile(j):
        def w_map(i, k):
            sh, tl = _in_tile(sub * k + j)
            return (sh, 0, tl)
        return pl.BlockSpec((None, D, 512), w_map)

    dwin = _dw_in(s["hn1"], dz6, l)
    small = dict(gmlp_ln_g=dlg[0], gmlp_ln_b=dlb[0], gmlp_w_s=dws, gmlp_b_s=dbs[:, :, 0], conv_w=dcw, conv_b=dcb[0],
                 lru_w_r=dwr, lru_b_r=dbr, lru_w_i=dwi, lru_b_i=dbi, lru_lambda=dlam, norm2_g=dg2[0])
    tail = () if late is None else tuple(late([dwin], small))
    dx0, dg1 = _mm_nt_rms_bwd(
        dz6, [dz_tile(j) for j in range(sub)], wb["w_in"], [w_tile(j) for j in range(sub)],
        N_IN_T // sub, tmb, s["x"], s["g1"], dx1, f"bwd_in_{l}", tail)
    return dx0, [dwin, dwo, dwfi, dwfo], dict(small, norm1_g=dg1[0])


def _local_step(x, tgt, p, wbs):
    saved = []
    for l in range(2):
        x, s = _forward_layer(l, x, p, wbs[l], loss=(tgt, p["final_g"][None]) if l else None)
        saved.append(s)
    dx, loss_v, dfg = x
    big, smalls = [None, None], [None, None]
    for l in (1, 0):
        dx, big[l], smalls[l] = _backward_layer(l, dx, saved[l])
    small = {k: jnp.stack([smalls[0][k], smalls[1][k]]) for k in LAYER_SMALL}
    small["final_g"] = dfg[0]
    return loss_v, dx, big, small


def _place():
    x, y, c = lax.axis_index("x"), lax.axis_index("y"), lax.axis_index("c")
    return x, y, c, 2 * x + y


def _chip_at(x, y, d):
    px = 1 - x if d & 2 else x
    py = 1 - y if d & 1 else y
    return px, py, 2 * px + py


HBM = pl.BlockSpec(memory_space=pltpu.HBM)
SEM = pl.BlockSpec(memory_space=pltpu.SEMAPHORE)
DATAFLOW = pltpu.SideEffectType.DATAFLOW_SIDE_EFFECTING


def _in_hbm(a):
    return pltpu.with_memory_space_constraint(a, pltpu.HBM)


def _cast_into(wfs, l, chip_arr, name):
    n = len(wfs)

    def body(ch_ref, *refs):
        for w_ref, o_ref in zip(refs[:n], refs[n:]):
            o_ref[...] = w_ref[...].astype(BF16)

    halves = [(wf.shape[1] // 2, wf.shape[2]) for wf in wfs]
    return pl.pallas_call(
        body, name=name, out_shape=[jax.ShapeDtypeStruct((4, 2, rh, cols), BF16) for rh, cols in halves],
        grid_spec=pltpu.PrefetchScalarGridSpec(
            num_scalar_prefetch=1, grid=(2,),
            in_specs=[pl.BlockSpec((None, None, rh, cols), lambda h, ch: (l, h, 0, 0)) for rh, cols in halves],
            out_specs=[pl.BlockSpec((None, None, rh, cols), lambda h, ch: (ch[0], h, 0, 0)) for rh, cols in halves]),
        compiler_params=_cp(("parallel",)),
    )(chip_arr, *[wf.reshape(2, 2, rh, cols) for wf, (rh, cols) in zip(wfs, halves)])


def _half_block(ref, chip, half, to, send_sem, recv_sem):
    blk = ref.at[chip, half]
    return pltpu.make_async_remote_copy(src_ref=blk, dst_ref=blk, send_sem=send_sem, recv_sem=recv_sem,
                                        device_id=to, device_id_type=MESH)


def _gather_weights(bufs, tiny):
    nt = len(bufs)
    n_ici = max(nt * 3, 1)

    def body(*refs):
        tiny_ref = refs[nt]
        o_refs, tiny_o = refs[nt + 1:2 * nt + 1], refs[2 * nt + 1]
        send, recv, fsend, frecv, tsend, trecv, lsem = refs[2 * nt + 2:]
        x, y, c, chip = _place()
        local = pltpu.make_async_copy(tiny_ref, tiny_o.at[chip], lsem)
        local.start()

        def tin(d, origin_chip, to):
            return pltpu.make_async_remote_copy(
                src_ref=tiny_ref, dst_ref=tiny_o.at[origin_chip], send_sem=tsend.at[d - 1], recv_sem=trecv.at[d - 1],
                device_id=to, device_id_type=MESH)

        sends = []
        for t in range(nt):
            for d in (1, 2, 3):
                px, py, _ = _chip_at(x, y, d)
                sends.append(_half_block(o_refs[t], chip, c, (px, py, c), send.at[3 * t + d - 1], recv.at[3 * t + d - 1]))
        for d in (1, 2, 3):
            px, py, _ = _chip_at(x, y, d)
            sends.append(tin(d, chip, (px, py, c)))
        for cp in sends:
            cp.start()
        passed = []
        for t in range(nt):
            for d in (1, 2, 3):
                k = 3 * t + d - 1
                _, _, pchip = _chip_at(x, y, d)
                _half_block(o_refs[t], pchip, c, (x, y, c), send.at[k], recv.at[k]).wait_recv()
                f = _half_block(o_refs[t], pchip, c, (x, y, 1 - c), fsend.at[k], frecv.at[k])
                f.start()
                passed.append(f)
        for t in range(nt):
            for d in (1, 2, 3):
                k = 3 * t + d - 1
                _, _, pchip = _chip_at(x, y, d)
                _half_block(o_refs[t], pchip, 1 - c, (x, y, 1 - c), fsend.at[k], frecv.at[k]).wait_recv()
        for d in (1, 2, 3):
            _, _, pchip = _chip_at(x, y, d)
            tin(d, pchip, (x, y, c)).wait_recv()
        for cp in sends + passed:
            cp.wait_send()
        local.wait()

    out_shape = [jax.ShapeDtypeStruct(b.shape, b.dtype) for b in bufs]
    out_shape.append(jax.ShapeDtypeStruct((4,) + tiny.shape, tiny.dtype))
    outs = pl.pallas_call(
        body, name="gather_weights_0", out_shape=out_shape,
        in_specs=[ANY] * (nt + 1), out_specs=[ANY] * (nt + 1),
        scratch_shapes=[pltpu.SemaphoreType.DMA((n_ici,)), pltpu.SemaphoreType.DMA((n_ici,)),
                        pltpu.SemaphoreType.DMA((n_ici,)), pltpu.SemaphoreType.DMA((n_ici,)),
                        pltpu.SemaphoreType.DMA((3,)), pltpu.SemaphoreType.DMA((3,)), pltpu.SemaphoreType.DMA],
        input_output_aliases={t: t for t in range(nt)},
        compiler_params=_cp(has_side_effects=True),
    )(*bufs, tiny)
    return outs[:nt], outs[nt]


def _gather_start(bufs, tag, after=()):
    nt, na = len(bufs), len(after)

    def body(*refs):
        b_refs = refs[:nt]
        send, recv = refs[nt + na], refs[nt + na + 1]
        token = refs[2 * nt + na + 2]
        x, y, c, chip = _place()
        for t in range(nt):
            for d in (1, 2, 3):
                px, py, _ = _chip_at(x, y, d)
                _half_block(b_refs[t], chip, c, (px, py, c), send.at[3 * t + d - 1], recv.at[3 * t + d - 1]).start()
        token[...] = jnp.zeros_like(token)

    outs = pl.pallas_call(
        body, name=f"gather_start_{tag}",
        out_shape=(pltpu.SemaphoreType.DMA((3 * nt,)), pltpu.SemaphoreType.DMA((3 * nt,)),
                   *[pltpu.HBM(b.shape, b.dtype) for b in bufs], jax.ShapeDtypeStruct((8, 128), F32)),
        in_specs=[HBM] * nt + [ANY] * na, out_specs=(SEM, SEM, *[HBM] * nt, pl.BlockSpec(memory_space=pltpu.VMEM)),
        input_output_aliases={t: 2 + t for t in range(nt)},
        compiler_params=pltpu.CompilerParams(has_side_effects=DATAFLOW),
    )(*[_in_hbm(b) for b in bufs], *after)
    return outs[0], outs[1], list(outs[2:2 + nt]), outs[2 + nt]


def _gather_wait(send, recv, bufs, after, tag):
    nt = len(bufs)

    def body(*refs):
        b_refs = refs[:nt]
        send_ref, recv_ref = refs[nt], refs[nt + 1]
        x, y, c, chip = _place()
        for t in range(nt):
            for d in (1, 2, 3):
                k = 3 * t + d - 1
                px, py, pchip = _chip_at(x, y, d)
                _half_block(b_refs[t], chip, c, (px, py, c), send_ref.at[k], recv_ref.at[k]).wait_send()
                _half_block(b_refs[t], pchip, c, (px, py, c), send_ref.at[k], recv_ref.at[k]).wait_recv()

    after = tuple(after) if isinstance(after, (tuple, list)) else (after,)
    outs = pl.pallas_call(
        body, name=f"gather_wait_{tag}", out_shape=[pltpu.HBM(b.shape, b.dtype) for b in bufs],
        in_specs=[HBM] * nt + [SEM, SEM] + [ANY] * len(after), out_specs=[HBM] * nt,
        input_output_aliases={t: t for t in range(nt)},
        compiler_params=pltpu.CompilerParams(has_side_effects=DATAFLOW),
    )(*bufs, send, recv, *after)
    return list(outs)


def _gather_pass_on(bufs, tag):
    nt = len(bufs)

    def body(*refs):
        o_refs = refs[nt:2 * nt]
        fsend, frecv = refs[2 * nt:]
        x, y, c, _ = _place()
        cps = []
        for t in range(nt):
            for d in (1, 2, 3):
                k = 3 * t + d - 1
                _, _, pchip = _chip_at(x, y, d)
                cps.append(_half_block(o_refs[t], pchip, c, (x, y, 1 - c), fsend.at[k], frecv.at[k]))
        for cp in cps:
            cp.start()
        for t in range(nt):
            for d in (1, 2, 3):
                k = 3 * t + d - 1
                _, _, pchip = _chip_at(x, y, d)
                _half_block(o_refs[t], pchip, 1 - c, (x, y, 1 - c), fsend.at[k], frecv.at[k]).wait_recv()
        for cp in cps:
            cp.wait_send()

    return pl.pallas_call(
        body, name=f"gather_pass_on_{tag}", out_shape=[jax.ShapeDtypeStruct(b.shape, b.dtype) for b in bufs],
        in_specs=[ANY] * nt, out_specs=[ANY] * nt,
        scratch_shapes=[pltpu.SemaphoreType.DMA((3 * nt,)), pltpu.SemaphoreType.DMA((3 * nt,))],
        input_output_aliases={t: t for t in range(nt)},
        compiler_params=_cp(has_side_effects=True),
    )(*bufs)


def _chip_copy(c_ref, land_ref, x, y, c, d, send_sem, recv_sem):
    px, py, pchip = _chip_at(x, y, d)
    return pltpu.make_async_remote_copy(src_ref=c_ref.at[pchip], dst_ref=land_ref.at[d - 1], send_sem=send_sem, recv_sem=recv_sem,
                                        device_id=(px, py, c), device_id_type=MESH)


def _exchange_start(srcs, lands, copies, nsem, name):
    ns, n = len(srcs), len(srcs) + len(lands)

    def body(*refs):
        for cp in copies(refs[:ns], refs[ns:n], refs[n], refs[n + 1]):
            cp.start()
        token = refs[2 * n + 2]
        token[...] = jnp.zeros_like(token)

    outs = pl.pallas_call(
        body, name=name,
        out_shape=(pltpu.SemaphoreType.DMA((nsem,)), pltpu.SemaphoreType.DMA((nsem,)),
                   *[pltpu.HBM(a.shape, a.dtype) for a in list(srcs) + list(lands)], jax.ShapeDtypeStruct((8, 128), F32)),
        in_specs=[HBM] * n, out_specs=(SEM, SEM, *[HBM] * n, pl.BlockSpec(memory_space=pltpu.VMEM)),
        input_output_aliases={i: 2 + i for i in range(n)},
        compiler_params=pltpu.CompilerParams(has_side_effects=DATAFLOW),
    )(*[_in_hbm(a) for a in list(srcs) + list(lands)])
    return outs[0], outs[1], list(outs[2:2 + ns]), list(outs[2 + ns:2 + n]), outs[2 + n]


def _exchange_wait(send, recv, srcs, lands, after, copies, name):
    ns, n = len(srcs), len(srcs) + len(lands)

    def body(*refs):
        for cp in copies(refs[:ns], refs[ns:n], refs[n], refs[n + 1]):
            cp.wait_send()
            cp.wait_recv()

    outs = pl.pallas_call(
        body, name=name, out_shape=[pltpu.HBM(a.shape, a.dtype) for a in list(srcs) + list(lands)],
        in_specs=[HBM] * n + [SEM, SEM, ANY], out_specs=[HBM] * n,
        input_output_aliases={i: i for i in range(n)},
        compiler_params=pltpu.CompilerParams(has_side_effects=DATAFLOW),
    )(*srcs, *lands, send, recv, after)
    return list(outs[:ns]), list(outs[ns:])


def _pass_on_copies(b_refs, land_refs, send, recv):
    del land_refs
    x, y, c, _ = _place()
    return [_half_block(b_refs[t], _chip_at(x, y, d)[2], c, (x, y, 1 - c), send.at[3 * t + d - 1], recv.at[3 * t + d - 1])
            for t in range(len(b_refs)) for d in (1, 2, 3)]


def _chips_copies(c_refs, land_refs, send, recv):
    x, y, c, _ = _place()
    return [_chip_copy(c_refs[t], land_refs[t], x, y, c, d, send.at[3 * t + d - 1], recv.at[3 * t + d - 1])
            for t in range(len(c_refs)) for d in (1, 2, 3)]


def _sibling_copies(g_refs, land_refs, send, recv):
    x, y, c, _ = _place()
    return [pltpu.make_async_remote_copy(
        src_ref=g_refs[t].at[k, 1 - c], dst_ref=land_refs[t].at[k], send_sem=send.at[4 * t + k], recv_sem=recv.at[4 * t + k],
        device_id=(x, y, 1 - c), device_id_type=MESH) for t in range(len(g_refs)) for k in range(4)]


def _join_copies(f_refs, land_refs, send, recv):
    del land_refs
    x, y, c, _ = _place()
    return [pltpu.make_async_remote_copy(
        src_ref=f_refs[t].at[c], dst_ref=f_refs[t].at[c], send_sem=send.at[t], recv_sem=recv.at[t],
        device_id=(x, y, 1 - c), device_id_type=MESH) for t in range(len(f_refs))]


def _add_half(gs, rs, c_arr, name):
    n = len(gs)

    def body(c_ref, *refs):
        for g_ref, r_ref, o_ref in zip(refs[:n], refs[n:2 * n], refs[2 * n:]):
            o_ref[...] = (g_ref[...] + r_ref[...]).astype(BF16)

    def own(g):
        return pl.BlockSpec((None, None) + g.shape[2:], lambda k, cr: (k, cr[0], 0, 0))

    def blk(g):
        return pl.BlockSpec((None,) + g.shape[2:], lambda k, cr: (k, 0, 0))

    return pl.pallas_call(
        body, name=name, out_shape=[jax.ShapeDtypeStruct((4,) + g.shape[2:], BF16) for g in gs],
        grid_spec=pltpu.PrefetchScalarGridSpec(
            num_scalar_prefetch=1, grid=(4,),
            in_specs=[own(g) for g in gs] + [blk(g) for g in gs], out_specs=[blk(g) for g in gs]),
        compiler_params=_cp(("parallel",)),
    )(c_arr, *gs, *rs)


def _sum_chips(css, r3s, place_arr, name):
    n = len(css)

    def body(pl_ref, *refs):
        up = lambda ref: ref[...].astype(F32)
        for t in range(n):
            a_ref, (r0_ref, r1_ref, r2_ref), o_ref = refs[t], refs[n + 3 * t:n + 3 * t + 3], refs[4 * n + t]
            o_ref[...] = ((up(a_ref) + up(r0_ref)) + up(r1_ref)) + up(r2_ref)

    def blk(cs, first):
        _, rh, cols = cs.shape
        return pl.BlockSpec((None, rh // 2, cols), lambda i, pa: (first(pa), i, 0))

    in_specs = [blk(cs, lambda pa: pa[0]) for cs in css]
    for cs in css:
        in_specs += [blk(cs, lambda pa, d=d: d) for d in range(3)]
    return pl.pallas_call(
        body, name=name, out_shape=[jax.ShapeDtypeStruct((2,) + cs.shape[1:], F32) for cs in css],
        grid_spec=pltpu.PrefetchScalarGridSpec(
            num_scalar_prefetch=1, grid=(2,), in_specs=in_specs, out_specs=[blk(cs, lambda pa: pa[1]) for cs in css]),
        compiler_params=_cp(("parallel",)),
    )(place_arr, *css, *[r3 for r3 in r3s for _ in range(3)])


def _allreduce_small(pack):
    rows = pack.shape[0]
    hr = rows // 2

    def body(p_ref, o_ref, sib, slots, s1, r1, s2, r2, s3, r3):
        x, y, c, chip = _place()
        sibling = (x, y, 1 - c)
        ex = pltpu.make_async_remote_copy(src_ref=p_ref, dst_ref=sib, send_sem=s1, recv_sem=r1,
                                          device_id=sibling, device_id_type=MESH)
        ex.start()
        ex.wait()
        half = pl.ds(pl.multiple_of(c * hr, 16), hr)
        slots[0] = (p_ref[half, :] + sib[half, :]).astype(BF16)
        cps = []
        for d in (1, 2, 3):
            px, py, _ = _chip_at(x, y, d)
            cps.append(pltpu.make_async_remote_copy(
                src_ref=slots.at[0], dst_ref=slots.at[d], send_sem=s2.at[d - 1], recv_sem=r2.at[d - 1],
                device_id=(px, py, c), device_id_type=MESH))
        for cp in cps:
            cp.start()
        for cp in cps:
            cp.wait()
        tot = slots[chip].astype(F32)
        for k in (1, 2, 3):
            tot = tot + slots[jnp.bitwise_xor(chip, k)].astype(F32)
        o_ref[half, :] = tot
        back = pltpu.make_async_remote_copy(src_ref=o_ref.at[half, :], dst_ref=o_ref.at[half, :], send_sem=s3, recv_sem=r3,
                                            device_id=sibling, device_id_type=MESH)
        back.start()
        back.wait()

    vm = pl.BlockSpec(memory_space=pltpu.VMEM)
    return pl.pallas_call(
        body, name="allreduce_small", out_shape=jax.ShapeDtypeStruct((rows, 128), F32),
        in_specs=[vm], out_specs=vm,
        scratch_shapes=[pltpu.VMEM((rows, 128), F32), pltpu.VMEM((4, hr, 128), BF16),
                        pltpu.SemaphoreType.DMA, pltpu.SemaphoreType.DMA, pltpu.SemaphoreType.DMA((3,)), pltpu.SemaphoreType.DMA((3,)),
                        pltpu.SemaphoreType.DMA, pltpu.SemaphoreType.DMA],
        compiler_params=_cp(has_side_effects=True),
    )(pack)


def _small_chip_sum(pack, after=()):
    rows = pack.shape[0]
    hr = rows // 2

    def body(p_ref, *rest):
        o_ref, sib, s1, r1 = rest[-4:]
        x, y, c, _ = _place()
        ex = pltpu.make_async_remote_copy(src_ref=p_ref, dst_ref=sib, send_sem=s1, recv_sem=r1,
                                          device_id=(x, y, 1 - c), device_id_type=MESH)
        ex.start()
        ex.wait()
        half = pl.ds(pl.multiple_of(c * hr, 16), hr)
        o_ref[...] = (p_ref[half, :] + sib[half, :]).astype(BF16)

    vm = pl.BlockSpec(memory_space=pltpu.VMEM)
    return pl.pallas_call(
        body, name="small_chip_sum", out_shape=jax.ShapeDtypeStruct((hr, 128), BF16),
        in_specs=[vm] + [ANY] * len(after), out_specs=vm,
        scratch_shapes=[pltpu.VMEM((rows, 128), F32), pltpu.SemaphoreType.DMA, pltpu.SemaphoreType.DMA],
        compiler_params=_cp(has_side_effects=True),
    )(pack, *after)


def _small_copies(c_refs, land_refs, send, recv):
    x, y, c, _ = _place()
    cps = []
    for d in (1, 2, 3):
        px, py, _ = _chip_at(x, y, d)
        cps.append(pltpu.make_async_remote_copy(src_ref=c_refs[0], dst_ref=land_refs[0].at[d - 1], send_sem=send.at[d - 1],
                                                recv_sem=recv.at[d - 1], device_id=(px, py, c), device_id_type=MESH))
    return cps


def _small_total(csum, land):
    hr = csum.shape[0]

    def body(c_ref, l_ref, o_ref, slots, s3, r3):
        x, y, c, chip = _place()
        slots[0] = c_ref[...]
        for d in (1, 2, 3):
            slots[d] = l_ref[d - 1]
        tot = slots[chip].astype(F32)
        for k in (1, 2, 3):
            tot = tot + slots[jnp.bitwise_xor(chip, k)].astype(F32)
        half = pl.ds(pl.multiple_of(c * hr, 16), hr)
        o_ref[half, :] = tot
        back = pltpu.make_async_remote_copy(src_ref=o_ref.at[half, :], dst_ref=o_ref.at[half, :], send_sem=s3, recv_sem=r3,
                                            device_id=(x, y, 1 - c), device_id_type=MESH)
        back.start()
        back.wait()

    vm = pl.BlockSpec(memory_space=pltpu.VMEM)
    return pl.pallas_call(
        body, name="small_total", out_shape=jax.ShapeDtypeStruct((2 * hr, 128), F32), in_specs=[vm, vm], out_specs=vm,
        scratch_shapes=[pltpu.VMEM((4, hr, 128), BF16), pltpu.SemaphoreType.DMA, pltpu.SemaphoreType.DMA],
        compiler_params=_cp(has_side_effects=True),
    )(csum, land)


def _adam_math(gv, wv, mv, vv):
    m2 = ADAM_B1 * mv + (1.0 - ADAM_B1) * gv
    v2 = ADAM_B2 * vv + (1.0 - ADAM_B2) * (gv * gv)
    m_hat = m2 / (1.0 - ADAM_B1 ** ADAM_STEP)
    v_hat = v2 / (1.0 - ADAM_B2 ** ADAM_STEP)
    return -ADAM_LR * (m_hat / (jnp.sqrt(v_hat) + ADAM_EPS) + ADAM_WD * wv), m2, v2


def _adam(g, w, m, v, name):
    rows, cols = g.shape
    rb = rows // 4

    def body(g_ref, w_ref, m_ref, v_ref, d_ref, m2_ref, v2_ref):
        d_ref[...], m2_ref[...], v2_ref[...] = _adam_math(g_ref[...], w_ref[...], m_ref[...], v_ref[...])

    blk = pl.BlockSpec((rb, cols), lambda i: (i, 0))
    shp = jax.ShapeDtypeStruct((rows, cols), F32)
    return pl.pallas_call(
        body, name=name, grid=(4,), in_specs=[blk] * 4, out_specs=[blk] * 3, out_shape=[shp] * 3,
        compiler_params=_cp(("parallel",)),
    )(g, w, m, v)


def _adam_layer(gs, ws, ms, vs, l, prevs, name):
    n = len(gs)
    prev = [a for p4 in prevs if p4 is not None for a in p4]

    def body(*refs):
        outs = refs[len(refs) - 4 * n:]
        for t in range(n):
            g_ref, w_ref, m_ref, v_ref = refs[4 * t:4 * t + 4]
            go_ref, d_ref, m2_ref, v2_ref = outs[4 * t:4 * t + 4]
            gv = g_ref[...]
            go_ref[...] = gv
            d_ref[...], m2_ref[...], v2_ref[...] = _adam_math(gv, w_ref[...], m_ref[...], v_ref[...])

    in_specs, out_specs, out_shape, operands, aliases = [], [], [], [], {}
    for t, g in enumerate(gs):
        rows, cols = g.shape
        lay = pl.BlockSpec((None, rows // 4, cols), lambda i: (l, i, 0))
        in_specs += [pl.BlockSpec((rows // 4, cols), lambda i: (i, 0)), lay, lay, lay]
        operands += [g, ws[t], ms[t], vs[t]]
        out_specs += [lay] * 4
        out_shape += [jax.ShapeDtypeStruct((2, rows, cols), F32)] * 4
    k = 4 * n
    for t, p4 in enumerate(prevs):
        if p4 is not None:
            for j in range(4):
                aliases[k] = 4 * t + j
                k += 1
    outs = pl.pallas_call(
        body, name=name, grid=(4,), in_specs=in_specs + [ANY] * len(prev), out_specs=out_specs, out_shape=out_shape,
        input_output_aliases=aliases, compiler_params=_cp(("parallel",)),
    )(*operands, *prev)
    return [list(outs[4 * t:4 * t + 4]) for t in range(n)]


def _rows128(a):
    return a.reshape(-1, 128)


def _pack(arrs, mult):
    parts = [_rows128(a) for a in arrs]
    rows = sum(q.shape[0] for q in parts)
    pad = -rows % mult
    if pad:
        parts.append(jnp.zeros((pad, 128), F32))
    return jnp.concatenate(parts, axis=0)


def _unpack(pack, shapes):
    out, o = [], 0
    for s in shapes:
        n = 1
        for e in s:
            n *= e
        out.append(pack[o:o + n // 128].reshape(s))
        o += n // 128
    return out


WEIGHTS = ['norm1_g', 'w_in', 'gmlp_ln_g', 'gmlp_ln_b', 'gmlp_w_s', 'gmlp_b_s', 'conv_w', 'conv_b', 'lru_w_r', 'lru_b_r', 'lru_w_i',
           'lru_b_i', 'lru_lambda', 'w_out', 'norm2_g', 'w_ffn_in', 'w_ffn_out', 'final_g']
BIG = ['w_in', 'w_out', 'w_ffn_in', 'w_ffn_out']
SMALL = [n for n in WEIGHTS if n not in BIG]
CHIP_SHARDED_SMALL = ['conv_w', 'lru_b_r', 'lru_b_i', 'lru_lambda']


def kernel(x, norm1_g, w_in, gmlp_ln_g, gmlp_ln_b, gmlp_w_s, gmlp_b_s, conv_w, conv_b, lru_w_r, lru_b_r, lru_w_i, lru_b_i, lru_lambda, w_out, norm2_g, w_ffn_in, w_ffn_out, final_g, loss_target, m_norm1_g, m_w_in, m_gmlp_ln_g, m_gmlp_ln_b, m_gmlp_w_s, m_gmlp_b_s, m_conv_w, m_conv_b, m_lru_w_r, m_lru_b_r, m_lru_w_i, m_lru_b_i, m_lru_lambda, m_w_out, m_norm2_g, m_w_ffn_in, m_w_ffn_out, m_final_g, v_norm1_g, v_w_in, v_gmlp_ln_g, v_gmlp_ln_b, v_gmlp_w_s, v_gmlp_b_s, v_conv_w, v_conv_b, v_lru_w_r, v_lru_b_r, v_lru_w_i, v_lru_b_i, v_lru_lambda, v_w_out, v_norm2_g, v_w_ffn_in, v_w_ffn_out, v_final_g):
    a = dict(locals())
    w = {n: a[n] for n in WEIGHTS}
    mom = {n: a["m_" + n] for n in WEIGHTS}
    var = {n: a["v_" + n] for n in WEIGHTS}
    _, _, c, chip = _place()
    c_arr, chip_arr = jnp.reshape(c, (1,)).astype(jnp.int32), jnp.reshape(chip, (1,)).astype(jnp.int32)
    place_arr = jnp.stack([chip, c]).astype(jnp.int32)

    first, rest = BIG[:1], BIG[1:]

    def as_weights(names, full):
        wb = {n: f.reshape(4, 2 * f.shape[2], f.shape[3]) for n, f in zip(names, full)}
        if "w_out" in wb:
            wb["w_out"] = wb["w_out"].reshape(D, D)
            wb["w_ffn_out"] = wb["w_ffn_out"].reshape(DFF, D)
        return wb

    def cast(names, l, tag):
        return _cast_into([w[n] for n in names], l, chip_arr, f"cast_{tag}")

    def landed(fly, names, after, tag):
        return as_weights(names, _gather_pass_on(_gather_wait(fly[0], fly[1], fly[2], after, tag), tag))

    tiny = _pack([w[n] for n in CHIP_SHARDED_SMALL], 8)
    _, tiny_full = _gather_weights([], tiny)
    fly_in = _gather_start(cast(first, 0, "in"), "in", after=(tiny_full,))
    fly0 = _gather_start(cast(rest, 0, "0"), "0", after=(fly_in[3],))
    fly1 = _gather_start(cast(BIG, 1, "1"), "1", after=(fly0[3],))
    p = {n: w[n] for n in SMALL}
    parts = [_unpack(tiny_full[k], [w[n].shape for n in CHIP_SHARDED_SMALL]) for k in range(4)]
    for i, n in enumerate(CHIP_SHARDED_SMALL):
        p[n] = jnp.concatenate([parts[k][i] for k in range(4)], axis=-1)

    operands = [_layer_operands(l, p) for l in range(2)]
    state_packs = [_pack([src[n] for n in SMALL], 32) for src in (w, mom, var)]
    ahead = tuple(jax.tree.leaves(operands)) + tuple(state_packs)

    passing = {}

    def pass_on_1(gu):
        bufs = _gather_wait(fly1[0], fly1[1], fly1[2], gu, "1")
        passing[1] = _exchange_start(bufs, [], _pass_on_copies, 3 * len(bufs), "gather_pass_on_start_1")
        return (passing[1][-1],)

    xa, saved0 = _forward_layer(0, x[0], p, landed(fly_in, first, (fly1[3],) + ahead, "in"), after=(fly0[3], fly1[3]),
                                rest=lambda merged: landed(fly0, rest, merged, "0"), near_end=pass_on_1, operands=operands[0])
    send, recv, bufs1, _, _ = passing[1]
    xb, saved1 = _forward_layer(
        1, xa, p, as_weights(BIG, _exchange_wait(send, recv, bufs1, [], xa, _pass_on_copies, "gather_pass_on_wait_1")[0]),
        operands=operands[1], loss=(loss_target[0], p["final_g"][None]))
    dxb, loss_v, dfg = xb
    loss = lax.psum(loss_v[0, 0], ("x", "y", "c"))

    out, flying = {}, {}

    def halves(grads):
        return [g.reshape(4, 2, -1, g.shape[-1]) for g in grads]

    def sibling_start(grads, names, l, tag):
        gs = halves(grads)
        lands = [lax.empty((4,) + g.shape[2:], g.dtype) for g in gs]
        flying["s" + tag] = (names, l) + tuple(
            _exchange_start(gs, lands, _sibling_copies, 4 * len(gs), f"grads_to_sibling_start_{tag}"))
        return (flying["s" + tag][-1],)

    def chips_start(gs, from_sib, names, l, tag):
        cs = _add_half(gs, from_sib, c_arr, f"add_half_{tag}")
        lands = [lax.empty((3,) + a.shape[1:], a.dtype) for a in cs]
        flying[tag] = (names, l) + tuple(_exchange_start(cs, lands, _chips_copies, 3 * len(cs), f"grads_to_chips_start_{tag}"))
        return (flying[tag][-1],)

    def sibling_finish(tag, after):
        names, l, send, recv, gs, lands, _ = flying["s" + tag]
        gs, from_sib = _exchange_wait(send, recv, gs, lands, after, _sibling_copies, f"grads_to_sibling_wait_{tag}")
        return chips_start(gs, from_sib, names, l, tag)

    def reduce_sums(tags, after):
        groups, ts = [], []
        for tag in tags:
            names, l, send, recv, cs, lands, _ = flying[tag]
            cs, lands = _exchange_wait(send, recv, cs, lands, after, _chips_copies, f"grads_to_chips_wait_{tag}")
            ts += _sum_chips(cs, lands, place_arr, f"sum_chips_{tag}")
            groups.append((tag, names))
        flying["j" + tags[0]] = (groups, l) + tuple(_exchange_start(ts, [], _join_copies, len(ts), f"grads_join_start_{tags[0]}"))
        return (flying["j" + tags[0]][-1],)

    def reduce_adam(tag0, after):
        groups, l, send, recv, ts, _, _ = flying["j" + tag0]
        joined = _exchange_wait(send, recv, ts, [], after, _join_copies, f"grads_join_wait_{tag0}")[0]
        for tag, names in groups:
            gs, joined = [j.reshape(w[n].shape[1:]) for n, j in zip(names, joined)], joined[len(names):]
            res = _adam_layer(gs, [w[n] for n in names], [mom[n] for n in names], [var[n] for n in names], l,
                              [out.get(n) for n in names], f"adam_{tag}")
            out.update(zip(names, res))

    def late1(grads, _):
        return sibling_finish("1a", grads[0]) + sibling_start(grads, first, 1, "1b")

    def midway0(grads):
        return reduce_sums(("1a", "1b"), grads[0]) + sibling_start(grads, rest, 0, "0a")

    def midway2_0(dws):
        reduce_adam("1a", dws)
        return sibling_finish("0a", dws)

    def stacked_small(small0):
        small = {k: jnp.stack([small0[k], small1[k]]) for k in LAYER_SMALL}
        return dict(small, final_g=dfg[0])

    def late0(grads, small0):
        tok = sibling_start(grads, first, 0, "0b")
        small = stacked_small(dict(small0, norm1_g=jnp.zeros((D,), F32)))
        csum = _small_chip_sum(_pack([small[n] for n in SMALL], 32), tok)
        toks = sibling_finish("0b", csum)
        flying["small"] = _exchange_start([csum], [lax.empty((3,) + csum.shape, BF16)], _small_copies, 3, "small_to_chips_start")
        return toks + (flying["small"][-1],)

    dxa, big1, small1 = _backward_layer(1, dxb, saved1, midway=lambda grads: sibling_start(grads, rest, 1, "1a"), late=late1)
    dx, big0, small0 = _backward_layer(0, dxa, saved0, after=sibling_finish("1b", dxa), midway=midway0,
                                       midway2=midway2_0, late=late0)
    join_tok = reduce_sums(("0a", "0b"), dx)
    small = stacked_small(small0)

    full_shapes = [small[n].shape for n in SMALL]
    send, recv, csum, land, _ = flying["small"]
    csum, land = _exchange_wait(send, recv, csum, land, join_tok[0], _small_copies, "small_to_chips_wait")
    red = _unpack(_small_total(csum[0], land[0]), full_shapes)
    norm1_0 = _allreduce_small(_pack([small0["norm1_g"]], 32))[:D // 128].reshape(D)
    reduce_adam("0a", norm1_0)
    red[SMALL.index("norm1_g")] = red[SMALL.index("norm1_g")].at[0].set(norm1_0)
    g_small = []
    for n, g in zip(SMALL, red):
        if n in CHIP_SHARDED_SMALL:
            g = lax.dynamic_slice_in_dim(g, chip * w[n].shape[-1], w[n].shape[-1], axis=g.ndim - 1)
        g_small.append(g)
    shapes = [w[n].shape for n in SMALL]
    upd = [_unpack(u, shapes) for u in _adam(_pack(g_small, 32), *state_packs, "adam_small")]
    for i, n in enumerate(SMALL):
        out[n] = [g_small[i], upd[0][i], upd[1][i], upd[2][i]]

    return (loss, dx[None]) + tuple(out[n][i] for i in range(4) for n in WEIGHTS)
```

```python
import functools

import jax
import jax.numpy as jnp
from jax import lax
from jax.experimental import pallas as pl
from jax.experimental.pallas import tpu as pltpu

F32 = jnp.float32
BF16 = jnp.bfloat16
MESH = pl.DeviceIdType.MESH

D = 1024
NH = 8
HD = 128
CHUNK = 128
N_IN_T = 12
DFF = 2816
DFF_SH = 1408
EPS = 1e-6
LRU_C = 8.0
ADAM_LR, ADAM_B1, ADAM_B2, ADAM_EPS, ADAM_WD, ADAM_STEP = 0.001, 0.9, 0.999, 1e-08, 0.01, 10

TM = 512
TM_BIG = 1024
RT = 128
PADR = 8
VMEM_LIMIT = 56 * 1024 * 1024


def _cp(sem=None, **kw):
    if sem is not None:
        kw["dimension_semantics"] = sem
    return pltpu.CompilerParams(vmem_limit_bytes=VMEM_LIMIT, **kw)


_GC = 0.7978845608028654


def _sigmoid(x):
    return 0.5 * jnp.tanh(0.5 * x) + 0.5


_GK = 0.044715


def _gelu(x):
    t = jnp.tanh(x * (_GC + (_GC * _GK) * (x * x)))
    return x * (0.5 + 0.5 * t)


def _gelu_and_grad(x):
    x2 = x * x
    t = jnp.tanh(x * (_GC + (_GC * _GK) * x2))
    h = 0.5 + 0.5 * t
    return x * h, h + x * (1.0 - t * t) * (0.5 * _GC + (1.5 * _GC * _GK) * x2)


def _softplus_neg(lam):
    y = jnp.exp(-jnp.abs(lam))
    u = 1.0 + y
    l1p = jnp.where(u == 1.0, y, jnp.log(u) * y / (u - 1.0))
    return jnp.maximum(-lam, 0.0) + l1p


def _dot(a, b):
    return jnp.dot(a, b, preferred_element_type=F32)


def _dot_nt(a, b):
    return lax.dot_general(a, b, (((1,), (1,)), ((), ())), preferred_element_type=F32)


def _dot_tn(a, b):
    return lax.dot_general(a, b, (((0,), (0,)), ((), ())), preferred_element_type=F32)


def _rms_hat(x):
    r = lax.rsqrt(jnp.mean(x * x, axis=-1, keepdims=True) + EPS)
    return x * r, r


def _rms_bwd(dh, x, g):
    xh, r = _rms_hat(x)
    dxh = dh * g
    dx = r * (dxh - xh * jnp.mean(dxh * xh, axis=-1, keepdims=True))
    return dx, jnp.sum(dh * xh, axis=0, keepdims=True)


def _norm_into(x_ref, g_ref, h_ref):
    xh, _ = _rms_hat(x_ref[...])
    h_ref[...] = (xh * g_ref[...]).astype(BF16)


def _in_tile(j):
    m, hf = j // 2, j % 2
    orig = jnp.where(m < 2, m, jnp.where(m == 2, 4, jnp.where(m < 5, m - 1, 5)))
    t = orig * 2 + hf
    return t // 3, t % 3


ANY = pl.BlockSpec(memory_space=pl.ANY)


def _mm_in(x, g, w_in, l, after=()):
    S = x.shape[0]
    tm = min(2 * TM_BIG, S)

    def body(x_ref, g_ref, w0_ref, w1_ref, *rest):
        o_ref, h_ref = rest[-2:]

        @pl.when(pl.program_id(1) == 0)
        def _():
            _norm_into(x_ref, g_ref, h_ref)
        rp = min(TM, tm)
        for r0 in range(0, tm, rp):
            hv = h_ref[r0:r0 + rp, :]
            o_ref[r0:r0 + rp, 0:512] = _dot(hv, w0_ref[...]).astype(BF16)
            o_ref[r0:r0 + rp, 512:1024] = _dot(hv, w1_ref[...]).astype(BF16)

    def w_tile(hf):
        def w_map(i, m):
            sh, tl = _in_tile(2 * m + hf)
            return (sh, 0, tl)
        return pl.BlockSpec((None, D, 512), w_map)

    return pl.pallas_call(
        body, name=f"mm_in_{l}", grid=(S // tm, 6),
        in_specs=[pl.BlockSpec((tm, D), lambda i, m: (i, 0)), pl.BlockSpec((1, D), lambda i, m: (0, 0)),
                  w_tile(0), w_tile(1)] + [ANY] * len(after),
        out_specs=[pl.BlockSpec((None, tm, D), lambda i, m: (m, i, 0)), pl.BlockSpec((tm, D), lambda i, m: (i, 0))],
        out_shape=[jax.ShapeDtypeStruct((6, S, D), BF16), jax.ShapeDtypeStruct((S, D), BF16)],
        compiler_params=_cp(("parallel", "arbitrary")),
    )(x, g, w_in, w_in, *after)


def _mm_res(a, w, res, l, name, after=()):
    S, K = a.shape

    tm = TM

    def body(a_ref, w_ref, r_ref, *rest):
        rest[-1][...] = r_ref[...] + _dot(a_ref[...], w_ref[...])

    return pl.pallas_call(
        body, name=f"{name}_{l}", grid=(S // tm,),
        in_specs=[pl.BlockSpec((tm, K), lambda i: (i, 0)), pl.BlockSpec((K, D), lambda i: (0, 0)),
                  pl.BlockSpec((tm, D), lambda i: (i, 0))] + [ANY] * len(after),
        out_specs=pl.BlockSpec((tm, D), lambda i: (i, 0)),
        out_shape=jax.ShapeDtypeStruct((S, D), F32),
        compiler_params=_cp(("parallel",)),
    )(a, w, res, *after)


def _mm_ffn_in(x, g, w_fi, l):
    S = x.shape[0]

    tm = min(TM_BIG, S)

    def body(x_ref, g_ref, w_ref, gu_ref, ff_ref, h_ref):
        @pl.when(pl.program_id(1) == 0)
        def _():
            _norm_into(x_ref, g_ref, h_ref)
        for r0 in range(0, tm, TM):
            rows = slice(r0, r0 + TM)
            hv = h_ref[rows, :]
            ga = _dot(hv, w_ref[0])
            gb = _dot(hv, w_ref[1])
            sg = _sigmoid(ga)
            silu = ga * sg
            gu_ref[0, rows, :] = (gb * (sg + silu * (1.0 - sg))).astype(BF16)
            gu_ref[1, rows, :] = silu.astype(BF16)
            ff_ref[rows, :] = (silu * gb).astype(BF16)

    gu, ff, h = pl.pallas_call(
        body, name=f"mm_ffn_in_{l}", grid=(S // tm, 2),
        in_specs=[pl.BlockSpec((tm, D), lambda i, s: (i, 0)), pl.BlockSpec((1, D), lambda i, s: (0, 0)),
                  pl.BlockSpec((2, None, D, DFF_SH), lambda i, s: (0, s, 0, 0))],
        out_specs=[pl.BlockSpec((2, None, tm, DFF_SH), lambda i, s: (0, s, i, 0)),
                   pl.BlockSpec((tm, DFF_SH), lambda i, s: (i, s)),
                   pl.BlockSpec((tm, D), lambda i, s: (i, 0))],
        out_shape=[jax.ShapeDtypeStruct((2, 2, S, DFF_SH), BF16), jax.ShapeDtypeStruct((S, DFF), BF16),
                   jax.ShapeDtypeStruct((S, D), BF16)],
        compiler_params=_cp(("parallel", "arbitrary")),
    )(x, g, w_fi.reshape(2, 2, D, DFF_SH))
    return gu.reshape(4, S, DFF_SH), ff, h


def _gmlp_fwd(z6, ws_b, bs_b, lg, lb):
    S = z6.shape[1]

    def body(z_ref, ws_ref, bs_ref, lg_ref, lb_ref, o_ref, mix):
        gv = _gelu(z_ref[1].astype(F32))
        xc = gv - jnp.mean(gv, axis=-1, keepdims=True)
        rs = lax.rsqrt(jnp.mean(xc * xc, axis=-1, keepdims=True) + EPS)
        vb = (xc * rs * lg_ref[...] + lb_ref[...]).astype(BF16)
        for gi in range(NH):
            cs = slice(gi * HD, (gi + 1) * HD)
            mix[:, cs] = _dot(ws_ref[gi], vb[:, cs])
        o_ref[...] = (_sigmoid(z_ref[2].astype(F32)) * _gelu(z_ref[0].astype(F32)) * (mix[...] + bs_ref[...])).astype(BF16)

    return pl.pallas_call(
        body, name="gmlp_fwd", grid=(S // CHUNK,),
        in_specs=[pl.BlockSpec((3, CHUNK, D), lambda i: (0, i, 0)), pl.BlockSpec((NH, CHUNK, CHUNK), lambda i: (0, 0, 0)),
                  pl.BlockSpec((CHUNK, D), lambda i: (0, 0)), pl.BlockSpec((1, D), lambda i: (0, 0)),
                  pl.BlockSpec((1, D), lambda i: (0, 0))],
        out_specs=pl.BlockSpec((CHUNK, D), lambda i: (i, 0)),
        out_shape=jax.ShapeDtypeStruct((S, D), BF16),
        scratch_shapes=[pltpu.VMEM((CHUNK, D), F32)],
        compiler_params=_cp(("parallel",)),
    )(z6, ws_b, bs_b, lg, lb)


def _row_iota():
    return lax.broadcasted_iota(jnp.int32, (RT, HD), 0)


SUB = 8
UNROLL = 8
GRAD_ROWS = 512


def _scan_up(a, b, carry):
    row = lax.broadcasted_iota(jnp.int32, (SUB, HD), 0)
    masks = [(d, row >= d) for d in (1, 2, 4)]
    c = jnp.broadcast_to(carry, (SUB, HD))
    hs = []
    for j in range(RT // SUB):
        aj, bj = a[SUB * j:SUB * (j + 1)], b[SUB * j:SUB * (j + 1)]
        for d, m in masks:
            bj = bj + aj * jnp.where(m, pltpu.roll(bj, d, 0), 0.0)
            aj = aj * jnp.where(m, pltpu.roll(aj, d, 0), 1.0)
        h = bj + aj * c
        hs.append(h)
        c = jnp.broadcast_to(h[SUB - 1:SUB, :], (SUB, HD))
    return jnp.concatenate(hs, axis=0), hs[-1][SUB - 1:SUB, :]


def _scan_down(a, b, carry):
    row = lax.broadcasted_iota(jnp.int32, (SUB, HD), 0)
    masks = [(d, row < SUB - d) for d in (1, 2, 4)]
    c = jnp.broadcast_to(carry, (SUB, HD))
    hs = []
    for j in reversed(range(RT // SUB)):
        aj, bj = a[SUB * j:SUB * (j + 1)], b[SUB * j:SUB * (j + 1)]
        for d, m in masks:
            bj = bj + aj * jnp.where(m, pltpu.roll(bj, SUB - d, 0), 0.0)
            aj = aj * jnp.where(m, pltpu.roll(aj, SUB - d, 0), 1.0)
        h = bj + aj * c
        hs.append(h)
        c = jnp.broadcast_to(h[0:1, :], (SUB, HD))
    return jnp.concatenate(hs[::-1], axis=0), hs[-1][0:1, :]


def _decay(r, sp_d):
    log_a = -LRU_C * r * sp_d
    a = jnp.exp(log_a)
    return a, jnp.sqrt(jnp.maximum(-jnp.tanh(log_a) * (a * a + 1.0), 0.0))


def _decay_bwd(r, sp_d):
    log_a = -LRU_C * r * sp_d
    a = jnp.exp(log_a)
    m2 = jnp.maximum(-jnp.tanh(log_a) * (a * a + 1.0), 0.0)
    inv = lax.rsqrt(m2)
    return a, m2 * inv, inv


def _lru_gates(xc, d, wr_ref, br_ref, wi_ref, bi_ref, sp):
    xb = xc.astype(BF16)
    r = _sigmoid(_dot(xb, wr_ref[d]) + br_ref[d:d + 1, :])
    i = _sigmoid(_dot(xb, wi_ref[d]) + bi_ref[d:d + 1, :])
    a, mult = _decay(r, sp[d:d + 1, :])
    return r, i, a, mult


def _shifted(win, k):
    w = RT + 2 * PADR
    v = win if k == 0 else pltpu.roll(win, (-k) % w, 0)
    return v[PADR:PADR + RT]


def _conv_taps(win):
    return [_shifted(win, k) for k in (-1, 0, 1, 2)]


def _fill_padded(dst, src_ref, S):
    zeros = jnp.zeros((PADR, HD), F32)
    dst[0:PADR, :] = zeros
    dst[PADR + S:2 * PADR + S, :] = zeros

    def cp(i, c):
        t0 = pl.multiple_of(i * RT, RT)
        dst[pl.ds(t0 + PADR, RT), :] = src_ref[pl.ds(t0, RT), :].astype(F32)
        return c
    lax.fori_loop(0, S // RT, cp, 0)


def _conv_fwd_all(zxp, xc_s, cw_ref, cb_ref, S):
    def cv(i, c):
        t0 = pl.multiple_of(i * RT, RT)
        xm1, x0, xp1, xp2 = _conv_taps(zxp[pl.ds(t0, RT + 2 * PADR), :])
        xc_s[pl.ds(t0, RT), :] = (cb_ref[...] + xm1 * cw_ref[0:1, :] + x0 * cw_ref[1:2, :]
                                  + xp1 * cw_ref[2:3, :] + xp2 * cw_ref[3:4, :])
        return c
    lax.fori_loop(0, S // RT, cv, 0)


def _lru_specs(S):
    head = lambda h: (0, h)
    return [pl.BlockSpec((4, HD), head), pl.BlockSpec((1, HD), head),
            pl.BlockSpec((2, None, HD, HD), lambda h: (0, h, 0, 0)), pl.BlockSpec((2, HD), head),
            pl.BlockSpec((2, None, HD, HD), lambda h: (0, h, 0, 0)), pl.BlockSpec((2, HD), head),
            pl.BlockSpec((2, HD), head)]


def _lru_fwd(z6, ya, cw, cb, wr, br, wi, bi, lam):
    S = z6.shape[1]
    nt = S // RT

    def body(z_ref, ya_ref, cw_ref, cb_ref, wr_ref, br_ref, wi_ref, bi_ref, lam_ref, mg_ref, h0_ref, h1_ref, zxp, xc_s):
        sp = _softplus_neg(lam_ref[...])
        _fill_padded(zxp, z_ref.at[0], S)
        _conv_fwd_all(zxp, xc_s, cw_ref, cb_ref, S)

        def scans(i, carry):
            cu, cd = carry
            for u in range(UNROLL):
                j = i * UNROLL + u
                ru = pl.ds(pl.multiple_of(j * RT, RT), RT)
                rd = pl.ds(pl.multiple_of((nt - 1 - j) * RT, RT), RT)
                xu, xd = xc_s[ru, :], xc_s[rd, :]
                _, gi, a, mult = _lru_gates(xu, 0, wr_ref, br_ref, wi_ref, bi_ref, sp)
                hu, cu = _scan_up(a, mult * gi * xu, cu)
                h0_ref[ru, :] = hu
                _, gi, a, mult = _lru_gates(xd, 1, wr_ref, br_ref, wi_ref, bi_ref, sp)
                hd, cd = _scan_down(a, mult * gi * xd, cd)
                h1_ref[rd, :] = hd
            return cu, cd
        z1 = jnp.zeros((1, HD), F32)
        lax.fori_loop(0, nt // UNROLL, scans, (z1, z1))

        def merge(i, c):
            rows = pl.ds(pl.multiple_of(i * RT, RT), RT)
            yb = (h0_ref[rows, :] + h1_ref[rows, :]) * _gelu(z_ref[1, rows, :].astype(F32))
            mg_ref[rows, :] = (ya_ref[rows, :].astype(F32) + _sigmoid(z_ref[2, rows, :].astype(F32)) * yb).astype(BF16)
            return c
        lax.fori_loop(0, nt, merge, 0)

    col = pl.BlockSpec((S, HD), lambda h: (0, h))
    return pl.pallas_call(
        body, name="lru_fwd", grid=(NH,),
        in_specs=[pl.BlockSpec((3, S, HD), lambda h: (1, 0, h)), col] + _lru_specs(S),
        out_specs=[col, col, col],
        out_shape=[jax.ShapeDtypeStruct((S, D), BF16), jax.ShapeDtypeStruct((S, D), F32), jax.ShapeDtypeStruct((S, D), F32)],
        scratch_shapes=[pltpu.VMEM((S + 2 * PADR, HD), F32), pltpu.VMEM((S, HD), F32)],
        compiler_params=_cp(("parallel",)),
    )(z6, ya, cw, cb, wr, br, wi, bi, lam)


def _mm_res_loss(a, w, res, tgt, g):
    S, K = a.shape

    def body(a_ref, w_ref, r_ref, t_ref, g_ref, dx_ref, loss_ref, dg_ref):
        @pl.when(pl.program_id(0) == 0)
        def _():
            loss_ref[...] = jnp.zeros_like(loss_ref)
            dg_ref[...] = jnp.zeros_like(dg_ref)
        xv = r_ref[...] + _dot(a_ref[...], w_ref[...])
        xh, _ = _rms_hat(xv)
        e = xh * g_ref[...] - t_ref[...]
        loss_ref[...] += jnp.sum(e * e) * (0.5 / D)
        dx, dgs = _rms_bwd(e * (1.0 / D), xv, g_ref[...])
        dx_ref[...] = dx
        dg_ref[...] += dgs

    row = pl.BlockSpec((TM, D), lambda i: (i, 0))
    vec = pl.BlockSpec((1, D), lambda i: (0, 0))
    return pl.pallas_call(
        body, name="mm_ffn_out_loss", grid=(S // TM,),
        in_specs=[pl.BlockSpec((TM, K), lambda i: (i, 0)), pl.BlockSpec((K, D), lambda i: (0, 0)), row, row, vec],
        out_specs=[row, pl.BlockSpec((1, 128), lambda i: (0, 0)), vec],
        out_shape=[jax.ShapeDtypeStruct((S, D), F32), jax.ShapeDtypeStruct((1, 128), F32), jax.ShapeDtypeStruct((1, D), F32)],
        compiler_params=_cp(("arbitrary",)),
    )(a, w, res, tgt, g)


def _bwd_ffn_out(dx, w_fo, gu, l, after=()):
    S = dx.shape[0]

    tm = min(TM_BIG, S)

    def body(dx_ref, w_ref, gu_ref, *rest):
        o_ref = rest[-1]
        for r0 in range(0, tm, TM):
            rows = slice(r0, r0 + TM)
            d = _dot_nt(dx_ref[rows, :].astype(BF16), w_ref[...])
            o_ref[0, rows, :] = (d * gu_ref[0, rows, :].astype(F32)).astype(BF16)
            o_ref[1, rows, :] = (d * gu_ref[1, rows, :].astype(F32)).astype(BF16)

    pair = pl.BlockSpec((2, None, tm, DFF_SH), lambda i, s: (0, s, i, 0))
    dgu = pl.pallas_call(
        body, name=f"bwd_ffn_out_{l}", grid=(S // tm, 2),
        in_specs=[pl.BlockSpec((tm, D), lambda i, s: (i, 0)), pl.BlockSpec((DFF_SH, D), lambda i, s: (s, 0)), pair]
        + [ANY] * len(after),
        out_specs=pair,
        out_shape=jax.ShapeDtypeStruct((2, 2, S, DFF_SH), BF16),
        compiler_params=_cp(("parallel", "arbitrary")),
    )(dx, w_fo, gu.reshape(2, 2, S, DFF_SH), *after)
    return dgu.reshape(4, S, DFF_SH)


def _mm_tn(a, b, m_blk, tk, name):
    S, M = a.shape

    def body(a_ref, b_ref, o_ref):
        @pl.when(pl.program_id(1) == 0)
        def _():
            o_ref[...] = jnp.zeros_like(o_ref)
        o_ref[...] += _dot_tn(a_ref[...], b_ref[...].astype(BF16))

    return pl.pallas_call(
        body, name=name, grid=(M // m_blk, S // tk),
        in_specs=[pl.BlockSpec((tk, m_blk), lambda m, k: (k, m)), pl.BlockSpec((tk, D), lambda m, k: (k, 0))],
        out_specs=pl.BlockSpec((m_blk, D), lambda m, k: (m, 0)),
        out_shape=jax.ShapeDtypeStruct((M, D), F32),
        compiler_params=_cp(("parallel", "arbitrary")),
    )(a, b)


def _mm_nt_rms_bwd(a, a_specs, w, w_specs, nk, tm, x, g, dres, name, after=()):
    S = x.shape[0]
    sub = len(a_specs)

    def body(*refs):
        a_refs, w_refs = refs[:sub], refs[sub:2 * sub]
        x_ref, g_ref, r_ref = refs[2 * sub:2 * sub + 3]
        dx_ref, dg_ref, acc = refs[-3:]
        i, k = pl.program_id(0), pl.program_id(1)
        @pl.when(k == 0)
        def _():
            acc[...] = jnp.zeros_like(acc)
        for j in range(sub):
            acc[...] += _dot_nt(a_refs[j][...], w_refs[j][...])

        @pl.when(jnp.logical_and(i == 0, k == 0))
        def _():
            dg_ref[...] = jnp.zeros_like(dg_ref)

        @pl.when(k == nk - 1)
        def _():
            dx, dgs = _rms_bwd(acc[...], x_ref[...], g_ref[...])
            dx_ref[...] = r_ref[...] + dx
            dg_ref[...] += dgs

    row = pl.BlockSpec((tm, D), lambda i, k: (i, 0))
    vec = pl.BlockSpec((1, D), lambda i, k: (0, 0))
    return pl.pallas_call(
        body, name=name, grid=(S // tm, nk),
        in_specs=list(a_specs) + list(w_specs) + [row, vec, row] + [ANY] * len(after),
        out_specs=[row, vec],
        out_shape=[jax.ShapeDtypeStruct((S, D), F32), jax.ShapeDtypeStruct((1, D), F32)],
        scratch_shapes=[pltpu.VMEM((tm, D), F32)],
        compiler_params=_cp(("arbitrary", "arbitrary")),
    )(*[a] * sub, *[w] * sub, x, g, dres, *after)


def _dw_ffn_in(h, dgu, l):
    S = h.shape[0]

    def body(h_ref, b_ref, o_ref):
        @pl.when(pl.program_id(1) == 0)
        def _():
            o_ref[...] = jnp.zeros_like(o_ref)
        o_ref[...] += _dot_tn(h_ref[...], b_ref[...])

    tk = min(2 * TM_BIG, S)
    return pl.pallas_call(
        body, name=f"dw_ffn_in_{l}", grid=(4, S // tk),
        in_specs=[pl.BlockSpec((tk, D), lambda j, k: (k, 0)), pl.BlockSpec((None, tk, DFF_SH), lambda j, k: (j, k, 0))],
        out_specs=pl.BlockSpec((None, D, DFF_SH), lambda j, k: (j, 0, 0)),
        out_shape=jax.ShapeDtypeStruct((4, D, DFF_SH), F32),
        compiler_params=_cp(("parallel", "arbitrary")),
    )(h, dgu)


_HALF_COMPS = ((0, 1, 3), (4, 2, 5))


def _dw_in(h, dz6, l, after=()):
    S = h.shape[0]

    def body(h_ref, d0_ref, d1_ref, d2_ref, *rest):
        o_ref = rest[-1]

        @pl.when(pl.program_id(1) == 0)
        def _():
            o_ref[...] = jnp.zeros_like(o_ref)
        hv = h_ref[...]
        for q, d_ref in enumerate((d0_ref, d1_ref, d2_ref)):
            for hf in range(2):
                col = 1024 * q + 512 * hf
                o_ref[col // 1536, :, col % 1536:col % 1536 + 512] += _dot_tn(hv, d_ref[:, 512 * hf:512 * (hf + 1)])

    tk = min(TM_BIG, S)

    def comp(q):
        return pl.BlockSpec((None, tk, D), lambda p, k: (jnp.where(p == 0, _HALF_COMPS[0][q], _HALF_COMPS[1][q]), k, 0))

    return pl.pallas_call(
        body, name=f"dw_in_{l}", grid=(2, S // tk),
        in_specs=[pl.BlockSpec((tk, D), lambda p, k: (k, 0)), comp(0), comp(1), comp(2)] + [ANY] * len(after),
        out_specs=pl.BlockSpec((2, D, 1536), lambda p, k: (p, 0, 0)),
        out_shape=jax.ShapeDtypeStruct((4, D, 1536), F32),
        compiler_params=_cp(("parallel", "arbitrary")),
    )(h, dz6, dz6, dz6, *after)


def _bwd_out(dx, w_o, merged, l):
    S = dx.shape[0]

    def body(dx_ref, w_ref, m_ref, dm_ref, dw_ref):
        @pl.when(pl.program_id(0) == 0)
        def _():
            dw_ref[...] = jnp.zeros_like(dw_ref)
        dxb = dx_ref[...].astype(BF16)
        dm_ref[...] = _dot_nt(dxb, w_ref[...]).astype(BF16)
        dw_ref[...] += _dot_tn(m_ref[...], dxb)

    tm = TM
    row = pl.BlockSpec((tm, D), lambda i: (i, 0))
    return pl.pallas_call(
        body, name=f"bwd_out_{l}", grid=(S // tm,),
        in_specs=[row, pl.BlockSpec((D, D), lambda i: (0, 0)), row],
        out_specs=[row, pl.BlockSpec((D, D), lambda i: (0, 0))],
        out_shape=[jax.ShapeDtypeStruct((S, D), BF16), jax.ShapeDtypeStruct((D, D), F32)],
        compiler_params=_cp(("arbitrary",)),
    )(dx, w_o, merged)


def _gmlp_bwd(dm, z6, ws_b, wst_b, bs_b, lg, lb, after=()):
    S = z6.shape[1]

    def body(dm_ref, z_ref, ws_ref, wst_ref, bs_ref, lg_ref, lb_ref, *rest):
        dz_ref, dws_ref, dbs_ref, dlg_ref, dlb_ref, mix, dv = rest[-7:]

        @pl.when(pl.program_id(0) == 0)
        def _():
            dws_ref[...] = jnp.zeros_like(dws_ref)
            dbs_ref[...] = jnp.zeros_like(dbs_ref)
            dlg_ref[...] = jnp.zeros_like(dlg_ref)
            dlb_ref[...] = jnp.zeros_like(dlb_ref)
        gv, dgelu_v = _gelu_and_grad(z_ref[1].astype(F32))
        xc = gv - jnp.mean(gv, axis=-1, keepdims=True)
        rs = lax.rsqrt(jnp.mean(xc * xc, axis=-1, keepdims=True) + EPS)
        vh = xc * rs
        vb = (vh * lg_ref[...] + lb_ref[...]).astype(BF16)
        for gi in range(NH):
            cs = slice(gi * HD, (gi + 1) * HD)
            mix[:, cs] = _dot(ws_ref[gi], vb[:, cs])
        u, dgelu_u = _gelu_and_grad(z_ref[0].astype(F32))
        sa = _sigmoid(z_ref[2].astype(F32))
        mixed = mix[...] + bs_ref[...]
        dyg = dm_ref[...].astype(F32)
        dz_ref[2] = (dyg * u * mixed * sa * (1.0 - sa)).astype(BF16)
        dya = dyg * sa
        dz_ref[0] = (dya * mixed * dgelu_u).astype(BF16)
        dmix = dya * u
        dmb = dmix.astype(BF16)
        for gi in range(NH):
            cs = slice(gi * HD, (gi + 1) * HD)
            dv[:, cs] = _dot(wst_ref[gi], dmb[:, cs])
            dws_ref[gi] += _dot_nt(dmb[:, cs], vb[:, cs])
            dbs_ref[gi] += jnp.broadcast_to(jnp.sum(dmix[:, cs], axis=1, keepdims=True), (CHUNK, HD))
        dvv = dv[...]
        dlg_ref[...] += jnp.sum(dvv * vh, axis=0, keepdims=True)
        dlb_ref[...] += jnp.sum(dvv, axis=0, keepdims=True)
        dvh = dvv * lg_ref[...]
        dgv = rs * (dvh - jnp.mean(dvh, axis=-1, keepdims=True) - vh * jnp.mean(dvh * vh, axis=-1, keepdims=True))
        dz_ref[1] = (dgv * dgelu_v).astype(BF16)

    vec = pl.BlockSpec((1, D), lambda i: (0, 0))
    mat = pl.BlockSpec((NH, CHUNK, CHUNK), lambda i: (0, 0, 0))
    return pl.pallas_call(
        body, name="gmlp_bwd", grid=(S // CHUNK,),
        in_specs=[pl.BlockSpec((CHUNK, D), lambda i: (i, 0)), pl.BlockSpec((3, CHUNK, D), lambda i: (0, i, 0)), mat, mat,
                  pl.BlockSpec((CHUNK, D), lambda i: (0, 0)), vec, vec] + [ANY] * len(after),
        out_specs=[pl.BlockSpec((3, CHUNK, D), lambda i: (0, i, 0)), mat, mat, vec, vec],
        out_shape=[jax.ShapeDtypeStruct((6, S, D), BF16), jax.ShapeDtypeStruct((NH, CHUNK, CHUNK), F32),
                   jax.ShapeDtypeStruct((NH, CHUNK, HD), F32), jax.ShapeDtypeStruct((1, D), F32), jax.ShapeDtypeStruct((1, D), F32)],
        scratch_shapes=[pltpu.VMEM((CHUNK, D), F32), pltpu.VMEM((CHUNK, D), F32)],
        compiler_params=_cp(("arbitrary",)),
    )(dm, z6, ws_b, wst_b, bs_b, lg, lb, *after)


def _lru_bwd(dz6, dm, z6, h0, h1, cw, cb, wr, br, wi, bi, lam, after=()):
    S = z6.shape[1]
    nt = S // RT

    def body(dz_in, dm_ref, z_ref, h0_ref, h1_ref, cw_ref, cb_ref, wr_ref, br_ref, wi_ref, bi_ref, lam_ref, *rest):
        dz_ref, dcw_ref, dcb_ref, dwr_ref, dbr_ref, dwi_ref, dbi_ref, dlam_ref, zxp, xc_s, dhs_s, dxcp, r_s, lam_s = rest[-14:]
        del dz_in
        lam = lam_ref[...]
        sp = _softplus_neg(lam)
        row = _row_iota()
        _fill_padded(zxp, z_ref.at[0], S)
        _conv_fwd_all(zxp, xc_s, cw_ref, cb_ref, S)
        zeros = jnp.zeros((PADR, HD), F32)
        dxcp[0:PADR, :] = zeros
        dxcp[PADR + S:2 * PADR + S, :] = zeros
        dwr_ref[...] = jnp.zeros_like(dwr_ref)
        dwi_ref[...] = jnp.zeros_like(dwi_ref)

        def pre(i, c):
            rows = pl.ds(pl.multiple_of(i * RT, RT), RT)
            hs = h0_ref[rows, :] + h1_ref[rows, :]
            dmv = dm_ref[rows, :].astype(F32)
            sb = _sigmoid(z_ref[2, rows, :].astype(F32))
            gg, dgg = _gelu_and_grad(z_ref[1, rows, :].astype(F32))
            dz_ref[2, rows, :] = (dmv * hs * gg * sb * (1.0 - sb)).astype(BF16)
            dyb = dmv * sb
            dz_ref[1, rows, :] = (dyb * hs * dgg).astype(BF16)
            dhs_s[rows, :] = dyb * gg
            return c
        lax.fori_loop(0, nt, pre, 0)

        def gate_bwd(d, gates, lamv, da, xc):
            r, gi, a, mult, inv_mult = gates
            lx, lm = lamv * xc, lamv * mult
            dlog_r = (da - (lx * gi) * (a * inv_mult)) * a * r
            dpr = dlog_r * (1.0 - r) * (-LRU_C * sp[d:d + 1, :])
            dpi = (lx * mult) * gi * (1.0 - gi)
            xb, dprb, dpib = xc.astype(BF16), dpr.astype(BF16), dpi.astype(BF16)
            dwr_ref[d] += _dot_tn(xb, dprb)
            dwi_ref[d] += _dot_tn(xb, dpib)
            dxc = lm * gi + _dot_nt(dprb, wr_ref[d]) + _dot_nt(dpib, wi_ref[d])
            return dxc, (jnp.sum(dlog_r, axis=0, keepdims=True) * (-LRU_C), jnp.sum(dpr, axis=0, keepdims=True),
                         jnp.sum(dpi, axis=0, keepdims=True))

        def rgates(i, c):
            for u in range(UNROLL):
                rows = pl.ds(pl.multiple_of((i * UNROLL + u) * RT, RT), RT)
                xb = xc_s[rows, :].astype(BF16)
                for d in range(2):
                    r_s[d, rows, :] = _sigmoid(_dot(xb, wr_ref[d]) + br_ref[d:d + 1, :])
            return c
        lax.fori_loop(0, nt // UNROLL, rgates, 0)

        def chains(i, carry):
            qn, qp = carry
            for u in range(UNROLL):
                j = i * UNROLL + u
                rd = pl.ds(pl.multiple_of((nt - 1 - j) * RT, RT), RT)
                a, dhs = _decay(r_s[0, rd, :], sp[0:1, :])[0], dhs_s[rd, :]
                q, q_first = _scan_down(a, a * dhs, qn)
                lam_s[0, rd, :] = dhs + jnp.where(row == RT - 1, qn, pltpu.roll(q, RT - 1, 0))
                qn = q_first
                ru = pl.ds(pl.multiple_of(j * RT, RT), RT)
                a, dhs = _decay(r_s[1, ru, :], sp[1:2, :])[0], dhs_s[ru, :]
                q, q_last = _scan_up(a, a * dhs, qp)
                lam_s[1, ru, :] = dhs + jnp.where(row == 0, qp, pltpu.roll(q, 1, 0))
                qp = q_last
            return qn, qp

        z1 = jnp.zeros((1, HD), F32)
        lax.fori_loop(0, nt // UNROLL, chains, (z1, z1))

        ct = min(GRAD_ROWS, S)
        crow = lax.broadcasted_iota(jnp.int32, (ct, HD), 0)

        def tile_grads(i, acc):
            t0 = pl.multiple_of(i * ct, ct)
            rows = pl.ds(t0, ct)
            xc = xc_s[rows, :]
            xb = xc.astype(BF16)
            tp = pl.multiple_of(jnp.maximum(t0 - PADR, 0), PADR)
            prev = jnp.where(t0 > 0, h0_ref[pl.ds(tp, PADR), :][PADR - 1:PADR, :], 0.0)
            tn = pl.multiple_of(jnp.minimum(t0 + ct, S - PADR), PADR)
            nxt = jnp.where(t0 + ct < S, h1_ref[pl.ds(tn, PADR), :][0:1, :], 0.0)
            hside = (jnp.where(crow == 0, prev, pltpu.roll(h0_ref[rows, :], 1, 0)),
                     jnp.where(crow == ct - 1, nxt, pltpu.roll(h1_ref[rows, :], ct - 1, 0)))
            dxc, sums = 0.0, ()
            for d in range(2):
                r = r_s[d, rows, :]
                gi = _sigmoid(_dot(xb, wi_ref[d]) + bi_ref[d:d + 1, :])
                lamv = lam_s[d, rows, :]
                dxc_d, s_d = gate_bwd(d, (r, gi) + _decay_bwd(r, sp[d:d + 1, :]), lamv, lamv * hside[d], xc)
                dxc = dxc + dxc_d
                sums = sums + s_d
            dxcp[pl.ds(t0 + PADR, ct), :] = dxc
            return tuple(x + y for x, y in zip(acc, sums))

        s_sp0, s_br0, s_bi0, s_sp1, s_br1, s_bi1 = lax.fori_loop(0, S // ct, tile_grads, (z1,) * 6)

        dsp = jnp.concatenate([s_sp0, s_sp1], axis=0)
        dlam_ref[...] = -dsp * _sigmoid(-lam)
        dbr_ref[...] = jnp.concatenate([s_br0, s_br1], axis=0)
        dbi_ref[...] = jnp.concatenate([s_bi0, s_bi1], axis=0)

        def conv_bwd(i, carry):
            c0, c1, c2, c3, cb_ = carry
            t0 = pl.multiple_of(i * RT, RT)
            dwin = dxcp[pl.ds(t0, RT + 2 * PADR), :]
            d0 = _shifted(dwin, 0)
            dz_ref[0, pl.ds(t0, RT), :] = (_shifted(dwin, 1) * cw_ref[0:1, :] + d0 * cw_ref[1:2, :]
                                           + _shifted(dwin, -1) * cw_ref[2:3, :] + _shifted(dwin, -2) * cw_ref[3:4, :]).astype(BF16)
            xm1, x0, xp1, xp2 = _conv_taps(zxp[pl.ds(t0, RT + 2 * PADR), :])
            sm = lambda v: jnp.sum(v, axis=0, keepdims=True)
            return c0 + sm(d0 * xm1), c1 + sm(d0 * x0), c2 + sm(d0 * xp1), c3 + sm(d0 * xp2), cb_ + sm(d0)

        c0, c1, c2, c3, cb_ = lax.fori_loop(0, nt, conv_bwd, (z1, z1, z1, z1, z1))
        dcw_ref[...] = jnp.concatenate([c0, c1, c2, c3], axis=0)
        dcb_ref[...] = cb_

    col = pl.BlockSpec((S, HD), lambda h: (0, h))
    head = lambda h: (0, h)
    wspec = pl.BlockSpec((2, None, HD, HD), lambda h: (0, h, 0, 0))
    return pl.pallas_call(
        body, name="lru_bwd", grid=(NH,),
        in_specs=[pl.BlockSpec(memory_space=pl.ANY), col, pl.BlockSpec((3, S, HD), lambda h: (1, 0, h)), col, col] + _lru_specs(S)
        + [ANY] * len(after),
        out_specs=[pl.BlockSpec((3, S, HD), lambda h: (1, 0, h)), pl.BlockSpec((4, HD), head), pl.BlockSpec((1, HD), head),
                   wspec, pl.BlockSpec((2, HD), head), wspec, pl.BlockSpec((2, HD), head), pl.BlockSpec((2, HD), head)],
        out_shape=[jax.ShapeDtypeStruct((6, S, D), BF16), jax.ShapeDtypeStruct((4, D), F32), jax.ShapeDtypeStruct((1, D), F32),
                   jax.ShapeDtypeStruct((2, NH, HD, HD), F32), jax.ShapeDtypeStruct((2, D), F32),
                   jax.ShapeDtypeStruct((2, NH, HD, HD), F32), jax.ShapeDtypeStruct((2, D), F32), jax.ShapeDtypeStruct((2, D), F32)],
        scratch_shapes=[pltpu.VMEM((S + 2 * PADR, HD), F32), pltpu.VMEM((S, HD), F32), pltpu.VMEM((S, HD), F32),
                        pltpu.VMEM((S + 2 * PADR, HD), F32), pltpu.VMEM((2, S, HD), F32), pltpu.VMEM((2, S, HD), F32)],
        input_output_aliases={0: 0},
        compiler_params=_cp(("parallel",)),
    )(dz6, dm, z6, h0, h1, cw, cb, wr, br, wi, bi, lam, *after)


LAYER_SMALL = ("norm1_g", "gmlp_ln_g", "gmlp_ln_b", "gmlp_w_s", "gmlp_b_s", "conv_w", "conv_b",
               "lru_w_r", "lru_b_r", "lru_w_i", "lru_b_i", "lru_lambda", "norm2_g")


def _layer_operands(l, p):
    ws_b = p["gmlp_w_s"][l].astype(BF16)
    tm = dict(ws_b=ws_b, wst_b=jnp.swapaxes(ws_b, 1, 2), bs_b=jnp.repeat(p["gmlp_b_s"][l].T, HD, axis=1),
              lg=p["gmlp_ln_g"][l][None], lb=p["gmlp_ln_b"][l][None])
    lru = (p["conv_w"][l], p["conv_b"][l][None], p["lru_w_r"][l].astype(BF16), p["lru_b_r"][l],
           p["lru_w_i"][l].astype(BF16), p["lru_b_i"][l], p["lru_lambda"][l])
    return (p["norm1_g"][l][None], p["norm2_g"][l][None]), tm, lru


def _forward_layer(l, x, p, wb, after=(), rest=None, near_end=None, operands=None, loss=None):
    (g1, g2), tm, lru = _layer_operands(l, p) if operands is None else operands
    z6, hn1 = _mm_in(x, g1, wb["w_in"], l, after)
    ya = _gmlp_fwd(z6, tm["ws_b"], tm["bs_b"], tm["lg"], tm["lb"])
    merged, h0, h1 = _lru_fwd(z6, ya, *lru)
    if rest is not None:
        wb = dict(wb, **rest(merged))
    x1 = _mm_res(merged, wb["w_out"], x, l, "mm_out")
    gu, ff, hn2 = _mm_ffn_in(x1, g2, wb["w_ffn_in"], l)
    if loss is None:
        x2 = _mm_res(ff, wb["w_ffn_out"], x1, l, "mm_ffn_out", () if near_end is None else tuple(near_end(gu)))
    else:
        x2 = _mm_res_loss(ff, wb["w_ffn_out"], x1, *loss)
    return x2, dict(x=x, z6=z6, h0=h0, h1=h1, merged=merged, x1=x1, gu=gu, ff=ff, g1=g1, g2=g2, tm=tm, lru=lru,
                    hn1=hn1, hn2=hn2, wb=wb)


def _backward_layer(l, dx, s, after=(), midway=None, midway2=None, midway3=None, late=None):
    S = dx.shape[0]
    tm, wb = s["tm"], s["wb"]
    g2 = s["g2"]
    dgu = _bwd_ffn_out(dx, wb["w_ffn_out"], s["gu"], l, after)
    tmb = min(TM_BIG, S)
    dwfo = _mm_tn(s["ff"], dx, DFF_SH, tmb, f"dw_ffn_out_{l}")
    dx1, dg2 = _mm_nt_rms_bwd(
        dgu, [pl.BlockSpec((None, tmb, DFF_SH), lambda i, k: (k, i, 0))],
        wb["w_ffn_in"], [pl.BlockSpec((None, D, DFF_SH), lambda i, k: (k, 0, 0))],
        4, tmb, s["x1"], g2, dx, f"bwd_ffn_in_{l}")
    dwfi = _dw_ffn_in(s["hn2"], dgu, l)
    dmg, dwo = _bwd_out(dx1, wb["w_out"], s["merged"], l)
    mid = () if midway is None else tuple(midway([dwo, dwfi, dwfo]))
    dz6, dws, dbs, dlg, dlb = _gmlp_bwd(dmg, s["z6"], tm["ws_b"], tm["wst_b"], tm["bs_b"], tm["lg"], tm["lb"], mid)
    mid2 = () if midway2 is None else tuple(midway2(dws))
    dz6, dcw, dcb, dwr, dbr, dwi, dbi, dlam = _lru_bwd(dz6, dmg, s["z6"], s["h0"], s["h1"], *s["lru"], after=mid2)

    sub = 3

    def dz_tile(j):
        return pl.BlockSpec((None, tmb, 512), lambda i, k: ((sub * k + j) // 2, i, (sub * k + j) % 2))

    def w_tile(j):
        def w_map(i, k):
            sh, tl = _in_tile(sub * k + j)
            return (sh, 0, tl)
        return pl.BlockSpec((None, D, 512), w_map)

    small = dict(gmlp_ln_g=dlg[0], gmlp_ln_b=dlb[0], gmlp_w_s=dws, gmlp_b_s=dbs[:, :, 0], conv_w=dcw, conv_b=dcb[0],
                 lru_w_r=dwr, lru_b_r=dbr, lru_w_i=dwi, lru_b_i=dbi, lru_lambda=dlam, norm2_g=dg2[0])
    mid3 = () if midway3 is None else tuple(midway3(small))
    dwin = _dw_in(s["hn1"], dz6, l, mid3)
    tail = () if late is None else tuple(late([dwin]))
    dx0, dg1 = _mm_nt_rms_bwd(
        dz6, [dz_tile(j) for j in range(sub)], wb["w_in"], [w_tile(j) for j in range(sub)],
        N_IN_T // sub, tmb, s["x"], s["g1"], dx1, f"bwd_in_{l}", tail)
    return dx0, [dwin, dwo, dwfi, dwfo], dict(small, norm1_g=dg1[0])


def _local_step(x, tgt, p, wbs):
    saved = []
    for l in range(2):
        x, s = _forward_layer(l, x, p, wbs[l], loss=(tgt, p["final_g"][None]) if l else None)
        saved.append(s)
    dx, loss_v, dfg = x
    big, smalls = [None, None], [None, None]
    for l in (1, 0):
        dx, big[l], smalls[l] = _backward_layer(l, dx, saved[l])
    small = {k: jnp.stack([smalls[0][k], smalls[1][k]]) for k in LAYER_SMALL}
    small["final_g"] = dfg[0]
    return loss_v, dx, big, small


def _place():
    x, y, c = lax.axis_index("x"), lax.axis_index("y"), lax.axis_index("c")
    return x, y, c, 2 * x + y


def _chip_at(x, y, d):
    px = 1 - x if d & 2 else x
    py = 1 - y if d & 1 else y
    return px, py, 2 * px + py


HBM = pl.BlockSpec(memory_space=pltpu.HBM)
SEM = pl.BlockSpec(memory_space=pltpu.SEMAPHORE)
DATAFLOW = pltpu.SideEffectType.DATAFLOW_SIDE_EFFECTING


def _in_hbm(a):
    return pltpu.with_memory_space_constraint(a, pltpu.HBM)


def _cast_into(wfs, l, chip_arr, name):
    n = len(wfs)

    def body(ch_ref, *refs):
        for w_ref, o_ref in zip(refs[:n], refs[n:]):
            o_ref[...] = w_ref[...].astype(BF16)

    halves = [(wf.shape[1] // 2, wf.shape[2]) for wf in wfs]
    return pl.pallas_call(
        body, name=name, out_shape=[jax.ShapeDtypeStruct((4, 2, rh, cols), BF16) for rh, cols in halves],
        grid_spec=pltpu.PrefetchScalarGridSpec(
            num_scalar_prefetch=1, grid=(2,),
            in_specs=[pl.BlockSpec((None, None, rh, cols), lambda h, ch: (l, h, 0, 0)) for rh, cols in halves],
            out_specs=[pl.BlockSpec((None, None, rh, cols), lambda h, ch: (ch[0], h, 0, 0)) for rh, cols in halves]),
        compiler_params=_cp(("parallel",)),
    )(chip_arr, *[wf.reshape(2, 2, rh, cols) for wf, (rh, cols) in zip(wfs, halves)])


def _half_block(ref, chip, half, to, send_sem, recv_sem):
    blk = ref.at[chip, half]
    return pltpu.make_async_remote_copy(src_ref=blk, dst_ref=blk, send_sem=send_sem, recv_sem=recv_sem,
                                        device_id=to, device_id_type=MESH)


def _gather_weights(bufs, tiny):
    nt = len(bufs)
    n_ici = max(nt * 3, 1)

    def body(*refs):
        tiny_ref = refs[nt]
        o_refs, tiny_o = refs[nt + 1:2 * nt + 1], refs[2 * nt + 1]
        send, recv, fsend, frecv, tsend, trecv, lsem = refs[2 * nt + 2:]
        x, y, c, chip = _place()
        local = pltpu.make_async_copy(tiny_ref, tiny_o.at[chip], lsem)
        local.start()

        def tin(d, origin_chip, to):
            return pltpu.make_async_remote_copy(
                src_ref=tiny_ref, dst_ref=tiny_o.at[origin_chip], send_sem=tsend.at[d - 1], recv_sem=trecv.at[d - 1],
                device_id=to, device_id_type=MESH)

        sends = []
        for t in range(nt):
            for d in (1, 2, 3):
                px, py, _ = _chip_at(x, y, d)
                sends.append(_half_block(o_refs[t], chip, c, (px, py, c), send.at[3 * t + d - 1], recv.at[3 * t + d - 1]))
        for d in (1, 2, 3):
            px, py, _ = _chip_at(x, y, d)
            sends.append(tin(d, chip, (px, py, c)))
        for cp in sends:
            cp.start()
        passed = []
        for t in range(nt):
            for d in (1, 2, 3):
                k = 3 * t + d - 1
                _, _, pchip = _chip_at(x, y, d)
                _half_block(o_refs[t], pchip, c, (x, y, c), send.at[k], recv.at[k]).wait_recv()
                f = _half_block(o_refs[t], pchip, c, (x, y, 1 - c), fsend.at[k], frecv.at[k])
                f.start()
                passed.append(f)
        for t in range(nt):
            for d in (1, 2, 3):
                k = 3 * t + d - 1
                _, _, pchip = _chip_at(x, y, d)
                _half_block(o_refs[t], pchip, 1 - c, (x, y, 1 - c), fsend.at[k], frecv.at[k]).wait_recv()
        for d in (1, 2, 3):
            _, _, pchip = _chip_at(x, y, d)
            tin(d, pchip, (x, y, c)).wait_recv()
        for cp in sends + passed:
            cp.wait_send()
        local.wait()

    out_shape = [jax.ShapeDtypeStruct(b.shape, b.dtype) for b in bufs]
    out_shape.append(jax.ShapeDtypeStruct((4,) + tiny.shape, tiny.dtype))
    outs = pl.pallas_call(
        body, name="gather_weights_0", out_shape=out_shape,
        in_specs=[ANY] * (nt + 1), out_specs=[ANY] * (nt + 1),
        scratch_shapes=[pltpu.SemaphoreType.DMA((n_ici,)), pltpu.SemaphoreType.DMA((n_ici,)),
                        pltpu.SemaphoreType.DMA((n_ici,)), pltpu.SemaphoreType.DMA((n_ici,)),
                        pltpu.SemaphoreType.DMA((3,)), pltpu.SemaphoreType.DMA((3,)), pltpu.SemaphoreType.DMA],
        input_output_aliases={t: t for t in range(nt)},
        compiler_params=_cp(has_side_effects=True),
    )(*bufs, tiny)
    return outs[:nt], outs[nt]


def _gather_start(bufs, tag, after=()):
    nt, na = len(bufs), len(after)

    def body(*refs):
        b_refs = refs[:nt]
        send, recv = refs[nt + na], refs[nt + na + 1]
        token = refs[2 * nt + na + 2]
        x, y, c, chip = _place()
        for t in range(nt):
            for d in (1, 2, 3):
                px, py, _ = _chip_at(x, y, d)
                _half_block(b_refs[t], chip, c, (px, py, c), send.at[3 * t + d - 1], recv.at[3 * t + d - 1]).start()
        token[...] = jnp.zeros_like(token)

    outs = pl.pallas_call(
        body, name=f"gather_start_{tag}",
        out_shape=(pltpu.SemaphoreType.DMA((3 * nt,)), pltpu.SemaphoreType.DMA((3 * nt,)),
                   *[pltpu.HBM(b.shape, b.dtype) for b in bufs], jax.ShapeDtypeStruct((8, 128), F32)),
        in_specs=[HBM] * nt + [ANY] * na, out_specs=(SEM, SEM, *[HBM] * nt, pl.BlockSpec(memory_space=pltpu.VMEM)),
        input_output_aliases={t: 2 + t for t in range(nt)},
        compiler_params=pltpu.CompilerParams(has_side_effects=DATAFLOW),
    )(*[_in_hbm(b) for b in bufs], *after)
    return outs[0], outs[1], list(outs[2:2 + nt]), outs[2 + nt]


def _gather_wait(send, recv, bufs, after, tag):
    nt = len(bufs)

    def body(*refs):
        b_refs = refs[:nt]
        send_ref, recv_ref = refs[nt], refs[nt + 1]
        x, y, c, chip = _place()
        for t in range(nt):
            for d in (1, 2, 3):
                k = 3 * t + d - 1
                px, py, pchip = _chip_at(x, y, d)
                _half_block(b_refs[t], chip, c, (px, py, c), send_ref.at[k], recv_ref.at[k]).wait_send()
                _half_block(b_refs[t], pchip, c, (px, py, c), send_ref.at[k], recv_ref.at[k]).wait_recv()

    after = tuple(after) if isinstance(after, (tuple, list)) else (after,)
    outs = pl.pallas_call(
        body, name=f"gather_wait_{tag}", out_shape=[pltpu.HBM(b.shape, b.dtype) for b in bufs],
        in_specs=[HBM] * nt + [SEM, SEM] + [ANY] * len(after), out_specs=[HBM] * nt,
        input_output_aliases={t: t for t in range(nt)},
        compiler_params=pltpu.CompilerParams(has_side_effects=DATAFLOW),
    )(*bufs, send, recv, *after)
    return list(outs)


def _gather_pass_on(bufs, tag):
    nt = len(bufs)

    def body(*refs):
        o_refs = refs[nt:2 * nt]
        fsend, frecv = refs[2 * nt:]
        x, y, c, _ = _place()
        cps = []
        for t in range(nt):
            for d in (1, 2, 3):
                k = 3 * t + d - 1
                _, _, pchip = _chip_at(x, y, d)
                cps.append(_half_block(o_refs[t], pchip, c, (x, y, 1 - c), fsend.at[k], frecv.at[k]))
        for cp in cps:
            cp.start()
        for t in range(nt):
            for d in (1, 2, 3):
                k = 3 * t + d - 1
                _, _, pchip = _chip_at(x, y, d)
                _half_block(o_refs[t], pchip, 1 - c, (x, y, 1 - c), fsend.at[k], frecv.at[k]).wait_recv()
        for cp in cps:
            cp.wait_send()

    return pl.pallas_call(
        body, name=f"gather_pass_on_{tag}", out_shape=[jax.ShapeDtypeStruct(b.shape, b.dtype) for b in bufs],
        in_specs=[ANY] * nt, out_specs=[ANY] * nt,
        scratch_shapes=[pltpu.SemaphoreType.DMA((3 * nt,)), pltpu.SemaphoreType.DMA((3 * nt,))],
        input_output_aliases={t: t for t in range(nt)},
        compiler_params=_cp(has_side_effects=True),
    )(*bufs)


def _chip_copy(c_ref, land_ref, x, y, c, d, send_sem, recv_sem):
    px, py, pchip = _chip_at(x, y, d)
    return pltpu.make_async_remote_copy(src_ref=c_ref.at[pchip], dst_ref=land_ref.at[d - 1], send_sem=send_sem, recv_sem=recv_sem,
                                        device_id=(px, py, c), device_id_type=MESH)


def _exchange_start(srcs, lands, copies, nsem, name):
    ns, n = len(srcs), len(srcs) + len(lands)

    def body(*refs):
        for cp in copies(refs[:ns], refs[ns:n], refs[n], refs[n + 1]):
            cp.start()
        token = refs[2 * n + 2]
        token[...] = jnp.zeros_like(token)

    outs = pl.pallas_call(
        body, name=name,
        out_shape=(pltpu.SemaphoreType.DMA((nsem,)), pltpu.SemaphoreType.DMA((nsem,)),
                   *[pltpu.HBM(a.shape, a.dtype) for a in list(srcs) + list(lands)], jax.ShapeDtypeStruct((8, 128), F32)),
        in_specs=[HBM] * n, out_specs=(SEM, SEM, *[HBM] * n, pl.BlockSpec(memory_space=pltpu.VMEM)),
        input_output_aliases={i: 2 + i for i in range(n)},
        compiler_params=pltpu.CompilerParams(has_side_effects=DATAFLOW),
    )(*[_in_hbm(a) for a in list(srcs) + list(lands)])
    return outs[0], outs[1], list(outs[2:2 + ns]), list(outs[2 + ns:2 + n]), outs[2 + n]


def _exchange_wait(send, recv, srcs, lands, after, copies, name):
    ns, n = len(srcs), len(srcs) + len(lands)

    def body(*refs):
        for cp in copies(refs[:ns], refs[ns:n], refs[n], refs[n + 1]):
            cp.wait_send()
            cp.wait_recv()

    outs = pl.pallas_call(
        body, name=name, out_shape=[pltpu.HBM(a.shape, a.dtype) for a in list(srcs) + list(lands)],
        in_specs=[HBM] * n + [SEM, SEM, ANY], out_specs=[HBM] * n,
        input_output_aliases={i: i for i in range(n)},
        compiler_params=pltpu.CompilerParams(has_side_effects=DATAFLOW),
    )(*srcs, *lands, send, recv, after)
    return list(outs[:ns]), list(outs[ns:])


def _pass_on_copies(b_refs, land_refs, send, recv):
    del land_refs
    x, y, c, _ = _place()
    return [_half_block(b_refs[t], _chip_at(x, y, d)[2], c, (x, y, 1 - c), send.at[3 * t + d - 1], recv.at[3 * t + d - 1])
            for t in range(len(b_refs)) for d in (1, 2, 3)]


def _chips_copies(c_refs, land_refs, send, recv):
    x, y, c, _ = _place()
    return [_chip_copy(c_refs[t], land_refs[t], x, y, c, d, send.at[3 * t + d - 1], recv.at[3 * t + d - 1])
            for t in range(len(c_refs)) for d in (1, 2, 3)]


def _sibling_copies(g_refs, land_refs, send, recv):
    x, y, c, _ = _place()
    return [pltpu.make_async_remote_copy(
        src_ref=g_refs[t].at[k, 1 - c], dst_ref=land_refs[t].at[k], send_sem=send.at[4 * t + k], recv_sem=recv.at[4 * t + k],
        device_id=(x, y, 1 - c), device_id_type=MESH) for t in range(len(g_refs)) for k in range(4)]


def _join_copies(f_refs, land_refs, send, recv):
    del land_refs
    x, y, c, _ = _place()
    return [pltpu.make_async_remote_copy(
        src_ref=f_refs[t].at[c], dst_ref=f_refs[t].at[c], send_sem=send.at[t], recv_sem=recv.at[t],
        device_id=(x, y, 1 - c), device_id_type=MESH) for t in range(len(f_refs))]


def _add_half(gs, rs, c_arr, name):
    n = len(gs)

    def body(c_ref, *refs):
        for g_ref, r_ref, o_ref in zip(refs[:n], refs[n:2 * n], refs[2 * n:]):
            o_ref[...] = (g_ref[...] + r_ref[...]).astype(BF16)

    def own(g):
        return pl.BlockSpec((None, None) + g.shape[2:], lambda k, cr: (k, cr[0], 0, 0))

    def blk(g):
        return pl.BlockSpec((None,) + g.shape[2:], lambda k, cr: (k, 0, 0))

    return pl.pallas_call(
        body, name=name, out_shape=[jax.ShapeDtypeStruct((4,) + g.shape[2:], BF16) for g in gs],
        grid_spec=pltpu.PrefetchScalarGridSpec(
            num_scalar_prefetch=1, grid=(4,),
            in_specs=[own(g) for g in gs] + [blk(g) for g in gs], out_specs=[blk(g) for g in gs]),
        compiler_params=_cp(("parallel",)),
    )(c_arr, *gs, *rs)


def _sum_chips(css, r3s, place_arr, name):
    n = len(css)

    def body(pl_ref, *refs):
        up = lambda ref: ref[...].astype(F32)
        for t in range(n):
            a_ref, (r0_ref, r1_ref, r2_ref), o_ref = refs[t], refs[n + 3 * t:n + 3 * t + 3], refs[4 * n + t]
            o_ref[...] = ((up(a_ref) + up(r0_ref)) + up(r1_ref)) + up(r2_ref)

    def blk(cs, first):
        _, rh, cols = cs.shape
        return pl.BlockSpec((None, rh // 2, cols), lambda i, pa: (first(pa), i, 0))

    in_specs = [blk(cs, lambda pa: pa[0]) for cs in css]
    for cs in css:
        in_specs += [blk(cs, lambda pa, d=d: d) for d in range(3)]
    return pl.pallas_call(
        body, name=name, out_shape=[jax.ShapeDtypeStruct((2,) + cs.shape[1:], F32) for cs in css],
        grid_spec=pltpu.PrefetchScalarGridSpec(
            num_scalar_prefetch=1, grid=(2,), in_specs=in_specs, out_specs=[blk(cs, lambda pa: pa[1]) for cs in css]),
        compiler_params=_cp(("parallel",)),
    )(place_arr, *css, *[r3 for r3 in r3s for _ in range(3)])


def _allreduce_small(pack):
    rows = pack.shape[0]
    hr = rows // 2

    def body(p_ref, o_ref, sib, slots, s1, r1, s2, r2, s3, r3):
        x, y, c, chip = _place()
        sibling = (x, y, 1 - c)
        ex = pltpu.make_async_remote_copy(src_ref=p_ref, dst_ref=sib, send_sem=s1, recv_sem=r1,
                                          device_id=sibling, device_id_type=MESH)
        ex.start()
        ex.wait()
        half = pl.ds(pl.multiple_of(c * hr, 16), hr)
        slots[0] = (p_ref[half, :] + sib[half, :]).astype(BF16)
        cps = []
        for d in (1, 2, 3):
            px, py, _ = _chip_at(x, y, d)
            cps.append(pltpu.make_async_remote_copy(
                src_ref=slots.at[0], dst_ref=slots.at[d], send_sem=s2.at[d - 1], recv_sem=r2.at[d - 1],
                device_id=(px, py, c), device_id_type=MESH))
        for cp in cps:
            cp.start()
        for cp in cps:
            cp.wait()
        tot = slots[chip].astype(F32)
        for k in (1, 2, 3):
            tot = tot + slots[jnp.bitwise_xor(chip, k)].astype(F32)
        o_ref[half, :] = tot
        back = pltpu.make_async_remote_copy(src_ref=o_ref.at[half, :], dst_ref=o_ref.at[half, :], send_sem=s3, recv_sem=r3,
                                            device_id=sibling, device_id_type=MESH)
        back.start()
        back.wait()

    vm = pl.BlockSpec(memory_space=pltpu.VMEM)
    return pl.pallas_call(
        body, name="allreduce_small", out_shape=jax.ShapeDtypeStruct((rows, 128), F32),
        in_specs=[vm], out_specs=vm,
        scratch_shapes=[pltpu.VMEM((rows, 128), F32), pltpu.VMEM((4, hr, 128), BF16),
                        pltpu.SemaphoreType.DMA, pltpu.SemaphoreType.DMA, pltpu.SemaphoreType.DMA((3,)), pltpu.SemaphoreType.DMA((3,)),
                        pltpu.SemaphoreType.DMA, pltpu.SemaphoreType.DMA],
        compiler_params=_cp(has_side_effects=True),
    )(pack)


def _small_chip_sum(pack, after=()):
    rows = pack.shape[0]
    hr = rows // 2

    def body(p_ref, *rest):
        o_ref, sib, s1, r1 = rest[-4:]
        x, y, c, _ = _place()
        ex = pltpu.make_async_remote_copy(src_ref=p_ref, dst_ref=sib, send_sem=s1, recv_sem=r1,
                                          device_id=(x, y, 1 - c), device_id_type=MESH)
        ex.start()
        ex.wait()
        half = pl.ds(pl.multiple_of(c * hr, 16), hr)
        o_ref[...] = (p_ref[half, :] + sib[half, :]).astype(BF16)

    vm = pl.BlockSpec(memory_space=pltpu.VMEM)
    return pl.pallas_call(
        body, name="small_chip_sum", out_shape=jax.ShapeDtypeStruct((hr, 128), BF16),
        in_specs=[vm] + [ANY] * len(after), out_specs=vm,
        scratch_shapes=[pltpu.VMEM((rows, 128), F32), pltpu.SemaphoreType.DMA, pltpu.SemaphoreType.DMA],
        compiler_params=_cp(has_side_effects=True),
    )(pack, *after)


def _small_copies(c_refs, land_refs, send, recv):
    x, y, c, _ = _place()
    cps = []
    for d in (1, 2, 3):
        px, py, _ = _chip_at(x, y, d)
        cps.append(pltpu.make_async_remote_copy(src_ref=c_refs[0], dst_ref=land_refs[0].at[d - 1], send_sem=send.at[d - 1],
                                                recv_sem=recv.at[d - 1], device_id=(px, py, c), device_id_type=MESH))
    return cps


def _small_total(csum, land):
    hr = csum.shape[0]

    def body(c_ref, l_ref, o_ref, slots, s3, r3):
        x, y, c, chip = _place()
        slots[0] = c_ref[...]
        for d in (1, 2, 3):
            slots[d] = l_ref[d - 1]
        tot = slots[chip].astype(F32)
        for k in (1, 2, 3):
            tot = tot + slots[jnp.bitwise_xor(chip, k)].astype(F32)
        half = pl.ds(pl.multiple_of(c * hr, 16), hr)
        o_ref[half, :] = tot
        back = pltpu.make_async_remote_copy(src_ref=o_ref.at[half, :], dst_ref=o_ref.at[half, :], send_sem=s3, recv_sem=r3,
                                            device_id=(x, y, 1 - c), device_id_type=MESH)
        back.start()
        back.wait()

    vm = pl.BlockSpec(memory_space=pltpu.VMEM)
    return pl.pallas_call(
        body, name="small_total", out_shape=jax.ShapeDtypeStruct((2 * hr, 128), F32), in_specs=[vm, vm], out_specs=vm,
        scratch_shapes=[pltpu.VMEM((4, hr, 128), BF16), pltpu.SemaphoreType.DMA, pltpu.SemaphoreType.DMA],
        compiler_params=_cp(has_side_effects=True),
    )(csum, land)


def _adam_math(gv, wv, mv, vv):
    m2 = ADAM_B1 * mv + (1.0 - ADAM_B1) * gv
    v2 = ADAM_B2 * vv + (1.0 - ADAM_B2) * (gv * gv)
    m_hat = m2 / (1.0 - ADAM_B1 ** ADAM_STEP)
    v_hat = v2 / (1.0 - ADAM_B2 ** ADAM_STEP)
    return -ADAM_LR * (m_hat / (jnp.sqrt(v_hat) + ADAM_EPS) + ADAM_WD * wv), m2, v2


def _adam(g, w, m, v, name):
    rows, cols = g.shape
    rb = rows // 4

    def body(g_ref, w_ref, m_ref, v_ref, d_ref, m2_ref, v2_ref):
        d_ref[...], m2_ref[...], v2_ref[...] = _adam_math(g_ref[...], w_ref[...], m_ref[...], v_ref[...])

    blk = pl.BlockSpec((rb, cols), lambda i: (i, 0))
    shp = jax.ShapeDtypeStruct((rows, cols), F32)
    return pl.pallas_call(
        body, name=name, grid=(4,), in_specs=[blk] * 4, out_specs=[blk] * 3, out_shape=[shp] * 3,
        compiler_params=_cp(("parallel",)),
    )(g, w, m, v)


def _adam_layer(gs, ws, ms, vs, l, prevs, name):
    n = len(gs)
    prev = [a for p4 in prevs if p4 is not None for a in p4]

    def body(*refs):
        outs = refs[len(refs) - 4 * n:]
        for t in range(n):
            g_ref, w_ref, m_ref, v_ref = refs[4 * t:4 * t + 4]
            go_ref, d_ref, m2_ref, v2_ref = outs[4 * t:4 * t + 4]
            gv = g_ref[...]
            go_ref[...] = gv
            d_ref[...], m2_ref[...], v2_ref[...] = _adam_math(gv, w_ref[...], m_ref[...], v_ref[...])

    in_specs, out_specs, out_shape, operands, aliases = [], [], [], [], {}
    for t, g in enumerate(gs):
        rows, cols = g.shape
        lay = pl.BlockSpec((None, rows // 4, cols), lambda i: (l, i, 0))
        in_specs += [pl.BlockSpec((rows // 4, cols), lambda i: (i, 0)), lay, lay, lay]
        operands += [g, ws[t], ms[t], vs[t]]
        out_specs += [lay] * 4
        out_shape += [jax.ShapeDtypeStruct((2, rows, cols), F32)] * 4
    k = 4 * n
    for t, p4 in enumerate(prevs):
        if p4 is not None:
            for j in range(4):
                aliases[k] = 4 * t + j
                k += 1
    outs = pl.pallas_call(
        body, name=name, grid=(4,), in_specs=in_specs + [ANY] * len(prev), out_specs=out_specs, out_shape=out_shape,
        input_output_aliases=aliases, compiler_params=_cp(("parallel",)),
    )(*operands, *prev)
    return [list(outs[4 * t:4 * t + 4]) for t in range(n)]


def _rows128(a):
    return a.reshape(-1, 128)


def _pack(arrs, mult):
    parts = [_rows128(a) for a in arrs]
    rows = sum(q.shape[0] for q in parts)
    pad = -rows % mult
    if pad:
        parts.append(jnp.zeros((pad, 128), F32))
    return jnp.concatenate(parts, axis=0)


def _unpack(pack, shapes):
    out, o = [], 0
    for s in shapes:
        n = 1
        for e in s:
            n *= e
        out.append(pack[o:o + n // 128].reshape(s))
        o += n // 128
    return out


WEIGHTS = ['norm1_g', 'w_in', 'gmlp_ln_g', 'gmlp_ln_b', 'gmlp_w_s', 'gmlp_b_s', 'conv_w', 'conv_b', 'lru_w_r', 'lru_b_r', 'lru_w_i',
           'lru_b_i', 'lru_lambda', 'w_out', 'norm2_g', 'w_ffn_in', 'w_ffn_out', 'final_g']
BIG = ['w_in', 'w_out', 'w_ffn_in', 'w_ffn_out']
SMALL = [n for n in WEIGHTS if n not in BIG]
CHIP_SHARDED_SMALL = ['conv_w', 'lru_b_r', 'lru_b_i', 'lru_lambda']


def kernel(x, norm1_g, w_in, gmlp_ln_g, gmlp_ln_b, gmlp_w_s, gmlp_b_s, conv_w, conv_b, lru_w_r, lru_b_r, lru_w_i, lru_b_i, lru_lambda, w_out, norm2_g, w_ffn_in, w_ffn_out, final_g, loss_target, m_norm1_g, m_w_in, m_gmlp_ln_g, m_gmlp_ln_b, m_gmlp_w_s, m_gmlp_b_s, m_conv_w, m_conv_b, m_lru_w_r, m_lru_b_r, m_lru_w_i, m_lru_b_i, m_lru_lambda, m_w_out, m_norm2_g, m_w_ffn_in, m_w_ffn_out, m_final_g, v_norm1_g, v_w_in, v_gmlp_ln_g, v_gmlp_ln_b, v_gmlp_w_s, v_gmlp_b_s, v_conv_w, v_conv_b, v_lru_w_r, v_lru_b_r, v_lru_w_i, v_lru_b_i, v_lru_lambda, v_w_out, v_norm2_g, v_w_ffn_in, v_w_ffn_out, v_final_g):
    a = dict(locals())
    w = {n: a[n] for n in WEIGHTS}
    mom = {n: a["m_" + n] for n in WEIGHTS}
    var = {n: a["v_" + n] for n in WEIGHTS}
    _, _, c, chip = _place()
    c_arr, chip_arr = jnp.reshape(c, (1,)).astype(jnp.int32), jnp.reshape(chip, (1,)).astype(jnp.int32)
    place_arr = jnp.stack([chip, c]).astype(jnp.int32)

    first, rest = BIG[:1], BIG[1:]

    def as_weights(names, full):
        wb = {n: f.reshape(4, 2 * f.shape[2], f.shape[3]) for n, f in zip(names, full)}
        if "w_out" in wb:
            wb["w_out"] = wb["w_out"].reshape(D, D)
            wb["w_ffn_out"] = wb["w_ffn_out"].reshape(DFF, D)
        return wb

    def cast(names, l, tag):
        return _cast_into([w[n] for n in names], l, chip_arr, f"cast_{tag}")

    def landed(fly, names, after, tag):
        return as_weights(names, _gather_pass_on(_gather_wait(fly[0], fly[1], fly[2], after, tag), tag))

    tiny = _pack([w[n] for n in CHIP_SHARDED_SMALL], 8)
    _, tiny_full = _gather_weights([], tiny)
    fly_in = _gather_start(cast(first, 0, "in"), "in", after=(tiny_full,))
    fly0 = _gather_start(cast(rest, 0, "0"), "0", after=(fly_in[3],))
    fly1 = _gather_start(cast(BIG, 1, "1"), "1", after=(fly0[3],))
    p = {n: w[n] for n in SMALL}
    parts = [_unpack(tiny_full[k], [w[n].shape for n in CHIP_SHARDED_SMALL]) for k in range(4)]
    for i, n in enumerate(CHIP_SHARDED_SMALL):
        p[n] = jnp.concatenate([parts[k][i] for k in range(4)], axis=-1)

    operands = [_layer_operands(l, p) for l in range(2)]
    state_packs = [_pack([src[n] for n in SMALL], 32) for src in (w, mom, var)]
    ahead = tuple(jax.tree.leaves(operands)) + tuple(state_packs)

    passing = {}

    def pass_on_1(gu):
        bufs = _gather_wait(fly1[0], fly1[1], fly1[2], gu, "1")
        passing[1] = _exchange_start(bufs, [], _pass_on_copies, 3 * len(bufs), "gather_pass_on_start_1")
        return (passing[1][-1],)

    xa, saved0 = _forward_layer(0, x[0], p, landed(fly_in, first, (fly1[3],) + ahead, "in"), after=(fly0[3], fly1[3]),
                                rest=lambda merged: landed(fly0, rest, merged, "0"), near_end=pass_on_1, operands=operands[0])
    send, recv, bufs1, _, _ = passing[1]
    xb, saved1 = _forward_layer(
        1, xa, p, as_weights(BIG, _exchange_wait(send, recv, bufs1, [], xa, _pass_on_copies, "gather_pass_on_wait_1")[0]),
        operands=operands[1], loss=(loss_target[0], p["final_g"][None]))
    dxb, loss_v, dfg = xb
    loss = lax.psum(loss_v[0, 0], ("x", "y", "c"))

    out, flying = {}, {}

    def halves(grads):
        return [g.reshape(4, 2, -1, g.shape[-1]) for g in grads]

    def sibling_start(grads, names, l, tag):
        gs = halves(grads)
        lands = [lax.empty((4,) + g.shape[2:], g.dtype) for g in gs]
        flying["s" + tag] = (names, l) + tuple(
            _exchange_start(gs, lands, _sibling_copies, 4 * len(gs), f"grads_to_sibling_start_{tag}"))
        return (flying["s" + tag][-1],)

    def chips_start(gs, from_sib, names, l, tag):
        cs = _add_half(gs, from_sib, c_arr, f"add_half_{tag}")
        lands = [lax.empty((3,) + a.shape[1:], a.dtype) for a in cs]
        flying[tag] = (names, l) + tuple(_exchange_start(cs, lands, _chips_copies, 3 * len(cs), f"grads_to_chips_start_{tag}"))
        return (flying[tag][-1],)

    def sibling_finish(tag, after):
        names, l, send, recv, gs, lands, _ = flying["s" + tag]
        gs, from_sib = _exchange_wait(send, recv, gs, lands, after, _sibling_copies, f"grads_to_sibling_wait_{tag}")
        return chips_start(gs, from_sib, names, l, tag)

    def reduce_sums(tags, after):
        groups, ts = [], []
        for tag in tags:
            names, l, send, recv, cs, lands, _ = flying[tag]
            cs, lands = _exchange_wait(send, recv, cs, lands, after, _chips_copies, f"grads_to_chips_wait_{tag}")
            ts += _sum_chips(cs, lands, place_arr, f"sum_chips_{tag}")
            groups.append((tag, names))
        flying["j" + tags[0]] = (groups, l) + tuple(_exchange_start(ts, [], _join_copies, len(ts), f"grads_join_start_{tags[0]}"))
        return (flying["j" + tags[0]][-1],)

    def reduce_adam(tag0, after):
        groups, l, send, recv, ts, _, _ = flying["j" + tag0]
        joined = _exchange_wait(send, recv, ts, [], after, _join_copies, f"grads_join_wait_{tag0}")[0]
        for tag, names in groups:
            gs, joined = [j.reshape(w[n].shape[1:]) for n, j in zip(names, joined)], joined[len(names):]
            res = _adam_layer(gs, [w[n] for n in names], [mom[n] for n in names], [var[n] for n in names], l,
                              [out.get(n) for n in names], f"adam_{tag}")
            out.update(zip(names, res))

    def late1(grads):
        return sibling_finish("1a", grads[0]) + sibling_start(grads, first, 1, "1b")

    def midway0(grads):
        return reduce_sums(("1a", "1b"), grads[0]) + sibling_start(grads, rest, 0, "0a")

    def stacked_small(small0):
        small = {k: jnp.stack([small0[k], small1[k]]) for k in LAYER_SMALL}
        return dict(small, final_g=dfg[0])

    def midway3_0(small0):
        small = stacked_small(dict(small0, norm1_g=jnp.zeros((D,), F32)))
        csum = _small_chip_sum(_pack([small[n] for n in SMALL], 32))
        flying["small"] = _exchange_start([csum], [lax.empty((3,) + csum.shape, BF16)], _small_copies, 3, "small_to_chips_start")
        return (flying["small"][-1],)

    def late0(grads):
        tok = sibling_start(grads, first, 0, "0b")
        reduce_adam("1a", tok[0])
        return sibling_finish("0b", out[first[0]][0])

    dxa, big1, small1 = _backward_layer(1, dxb, saved1, midway=lambda grads: sibling_start(grads, rest, 1, "1a"), late=late1)
    dx, big0, small0 = _backward_layer(0, dxa, saved0, after=sibling_finish("1b", dxa), midway=midway0,
                                       midway2=lambda dws: sibling_finish("0a", dws), midway3=midway3_0, late=late0)
    join_tok = reduce_sums(("0a", "0b"), dx)
    small = stacked_small(small0)

    full_shapes = [small[n].shape for n in SMALL]
    send, recv, csum, land, _ = flying["small"]
    csum, land = _exchange_wait(send, recv, csum, land, join_tok[0], _small_copies, "small_to_chips_wait")
    red = _unpack(_small_total(csum[0], land[0]), full_shapes)
    norm1_0 = _allreduce_small(_pack([small0["norm1_g"]], 32))[:D // 128].reshape(D)
    reduce_adam("0a", norm1_0)
    red[SMALL.index("norm1_g")] = red[SMALL.index("norm1_g")].at[0].set(norm1_0)
    g_small = []
    for n, g in zip(SMALL, red):
        if n in CHIP_SHARDED_SMALL:
            g = lax.dynamic_slice_in_dim(g, chip * w[n].shape[-1], w[n].shape[-1], axis=g.ndim - 1)
        g_small.append(g)
    shapes = [w[n].shape for n in SMALL]
    upd = [_unpack(u, shapes) for u in _adam(_pack(g_small, 32), *state_packs, "adam_small")]
    for i, n in enumerate(SMALL):
        out[n] = [g_small[i], upd[0][i], upd[1][i], upd[2][i]]

    return (loss, dx[None]) + tuple(out[n][i] for i in range(4) for n in WEIGHTS)
```

```python
import functools

import jax
import jax.numpy as jnp
from jax import lax
from jax.experimental import pallas as pl
from jax.experimental.pallas import tpu as pltpu

F32 = jnp.float32
BF16 = jnp.bfloat16
MESH = pl.DeviceIdType.MESH

D = 1024
NH = 8
HD = 128
CHUNK = 128
GMLP_ROWS = 256
N_IN_T = 12
DFF = 2816
DFF_SH = 1408
EPS = 1e-6
LRU_C = 8.0
ADAM_LR, ADAM_B1, ADAM_B2, ADAM_EPS, ADAM_WD, ADAM_STEP = 0.001, 0.9, 0.999, 1e-08, 0.01, 10

TM = 512
TM_BIG = 1024
RT = 128
PADR = 8
VMEM_LIMIT = 56 * 1024 * 1024


def _cp(sem=None, **kw):
    if sem is not None:
        kw["dimension_semantics"] = sem
    return pltpu.CompilerParams(vmem_limit_bytes=VMEM_LIMIT, **kw)


_GC = 0.7978845608028654


def _sigmoid(x):
    return 0.5 * jnp.tanh(0.5 * x) + 0.5


_GK = 0.044715


def _gelu(x):
    t = jnp.tanh(x * (_GC + (_GC * _GK) * (x * x)))
    return x * (0.5 + 0.5 * t)


def _gelu_and_grad(x):
    x2 = x * x
    t = jnp.tanh(x * (_GC + (_GC * _GK) * x2))
    h = 0.5 + 0.5 * t
    return x * h, h + x * (1.0 - t * t) * (0.5 * _GC + (1.5 * _GC * _GK) * x2)


def _softplus_neg(lam):
    y = jnp.exp(-jnp.abs(lam))
    u = 1.0 + y
    l1p = jnp.where(u == 1.0, y, jnp.log(u) * y / (u - 1.0))
    return jnp.maximum(-lam, 0.0) + l1p


def _dot(a, b):
    return jnp.dot(a, b, preferred_element_type=F32)


def _dot_nt(a, b):
    return lax.dot_general(a, b, (((1,), (1,)), ((), ())), preferred_element_type=F32)


def _dot_tn(a, b):
    return lax.dot_general(a, b, (((0,), (0,)), ((), ())), preferred_element_type=F32)


def _rms_hat(x):
    r = lax.rsqrt(jnp.mean(x * x, axis=-1, keepdims=True) + EPS)
    return x * r, r


def _rms_bwd(dh, x, g):
    xh, r = _rms_hat(x)
    dxh = dh * g
    dx = r * (dxh - xh * jnp.mean(dxh * xh, axis=-1, keepdims=True))
    return dx, jnp.sum(dh * xh, axis=0, keepdims=True)


def _norm_into(x_ref, g_ref, h_ref):
    xh, _ = _rms_hat(x_ref[...])
    h_ref[...] = (xh * g_ref[...]).astype(BF16)


def _in_tile(j):
    m, hf = j // 2, j % 2
    orig = jnp.where(m < 2, m, jnp.where(m == 2, 4, jnp.where(m < 5, m - 1, 5)))
    t = orig * 2 + hf
    return t // 3, t % 3


ANY = pl.BlockSpec(memory_space=pl.ANY)


def _mm_in(x, g, w_in, l, after=()):
    S = x.shape[0]
    tm = min(2 * TM_BIG, S)

    def body(x_ref, g_ref, w0_ref, w1_ref, *rest):
        o_ref, h_ref = rest[-2:]

        @pl.when(pl.program_id(1) == 0)
        def _():
            _norm_into(x_ref, g_ref, h_ref)
        rp = min(TM, tm)
        for r0 in range(0, tm, rp):
            hv = h_ref[r0:r0 + rp, :]
            o_ref[r0:r0 + rp, 0:512] = _dot(hv, w0_ref[...]).astype(BF16)
            o_ref[r0:r0 + rp, 512:1024] = _dot(hv, w1_ref[...]).astype(BF16)

    def w_tile(hf):
        def w_map(i, m):
            sh, tl = _in_tile(2 * m + hf)
            return (sh, 0, tl)
        return pl.BlockSpec((None, D, 512), w_map)

    return pl.pallas_call(
        body, name=f"mm_in_{l}", grid=(S // tm, 6),
        in_specs=[pl.BlockSpec((tm, D), lambda i, m: (i, 0)), pl.BlockSpec((1, D), lambda i, m: (0, 0)),
                  w_tile(0), w_tile(1)] + [ANY] * len(after),
        out_specs=[pl.BlockSpec((None, tm, D), lambda i, m: (m, i, 0)), pl.BlockSpec((tm, D), lambda i, m: (i, 0))],
        out_shape=[jax.ShapeDtypeStruct((6, S, D), BF16), jax.ShapeDtypeStruct((S, D), BF16)],
        compiler_params=_cp(("parallel", "arbitrary")),
    )(x, g, w_in, w_in, *after)


def _mm_res(a, w, res, l, name, after=()):
    S, K = a.shape

    tm = TM

    def body(a_ref, w_ref, r_ref, *rest):
        rest[-1][...] = r_ref[...] + _dot(a_ref[...], w_ref[...])

    return pl.pallas_call(
        body, name=f"{name}_{l}", grid=(S // tm,),
        in_specs=[pl.BlockSpec((tm, K), lambda i: (i, 0)), pl.BlockSpec((K, D), lambda i: (0, 0)),
                  pl.BlockSpec((tm, D), lambda i: (i, 0))] + [ANY] * len(after),
        out_specs=pl.BlockSpec((tm, D), lambda i: (i, 0)),
        out_shape=jax.ShapeDtypeStruct((S, D), F32),
        compiler_params=_cp(("parallel",)),
    )(a, w, res, *after)


def _mm_ffn_in(x, g, w_fi, l):
    S = x.shape[0]

    tm = min(TM_BIG, S)

    def body(x_ref, g_ref, w_ref, gu_ref, ff_ref, h_ref):
        @pl.when(pl.program_id(1) == 0)
        def _():
            _norm_into(x_ref, g_ref, h_ref)
        for r0 in range(0, tm, TM):
            rows = slice(r0, r0 + TM)
            hv = h_ref[rows, :]
            ga = _dot(hv, w_ref[0])
            gb = _dot(hv, w_ref[1])
            sg = _sigmoid(ga)
            silu = ga * sg
            gu_ref[0, rows, :] = (gb * (sg + silu * (1.0 - sg))).astype(BF16)
            gu_ref[1, rows, :] = silu.astype(BF16)
            ff_ref[rows, :] = (silu * gb).astype(BF16)

    gu, ff, h = pl.pallas_call(
        body, name=f"mm_ffn_in_{l}", grid=(S // tm, 2),
        in_specs=[pl.BlockSpec((tm, D), lambda i, s: (i, 0)), pl.BlockSpec((1, D), lambda i, s: (0, 0)),
                  pl.BlockSpec((2, None, D, DFF_SH), lambda i, s: (0, s, 0, 0))],
        out_specs=[pl.BlockSpec((2, None, tm, DFF_SH), lambda i, s: (0, s, i, 0)),
                   pl.BlockSpec((tm, DFF_SH), lambda i, s: (i, s)),
                   pl.BlockSpec((tm, D), lambda i, s: (i, 0))],
        out_shape=[jax.ShapeDtypeStruct((2, 2, S, DFF_SH), BF16), jax.ShapeDtypeStruct((S, DFF), BF16),
                   jax.ShapeDtypeStruct((S, D), BF16)],
        compiler_params=_cp(("parallel", "arbitrary")),
    )(x, g, w_fi.reshape(2, 2, D, DFF_SH))
    return gu.reshape(4, S, DFF_SH), ff, h


def _gmlp_fwd(z6, ws_b, bs_b, lg, lb):
    S = z6.shape[1]

    ts = min(GMLP_ROWS, S)

    def body(z_ref, ws_ref, bs_ref, lg_ref, lb_ref, o_ref, mix):
        for r0 in range(0, ts, CHUNK):
            rows = slice(r0, r0 + CHUNK)
            gv = _gelu(z_ref[1, rows, :].astype(F32))
            xc = gv - jnp.mean(gv, axis=-1, keepdims=True)
            rs = lax.rsqrt(jnp.mean(xc * xc, axis=-1, keepdims=True) + EPS)
            vb = (xc * rs * lg_ref[...] + lb_ref[...]).astype(BF16)
            for gi in range(NH):
                cs = slice(gi * HD, (gi + 1) * HD)
                mix[rows, cs] = _dot(ws_ref[gi], vb[:, cs])
            o_ref[rows, :] = (_sigmoid(z_ref[2, rows, :].astype(F32)) * _gelu(z_ref[0, rows, :].astype(F32))
                              * (mix[rows, :] + bs_ref[...])).astype(BF16)

    return pl.pallas_call(
        body, name="gmlp_fwd", grid=(S // ts,),
        in_specs=[pl.BlockSpec((3, ts, D), lambda i: (0, i, 0)), pl.BlockSpec((NH, CHUNK, CHUNK), lambda i: (0, 0, 0)),
                  pl.BlockSpec((CHUNK, D), lambda i: (0, 0)), pl.BlockSpec((1, D), lambda i: (0, 0)),
                  pl.BlockSpec((1, D), lambda i: (0, 0))],
        out_specs=pl.BlockSpec((ts, D), lambda i: (i, 0)),
        out_shape=jax.ShapeDtypeStruct((S, D), BF16),
        scratch_shapes=[pltpu.VMEM((ts, D), F32)],
        compiler_params=_cp(("parallel",)),
    )(z6, ws_b, bs_b, lg, lb)


def _row_iota():
    return lax.broadcasted_iota(jnp.int32, (RT, HD), 0)


SUB = 8
UNROLL = 8
GRAD_ROWS = 512


def _scan_up(a, b, carry):
    row = lax.broadcasted_iota(jnp.int32, (SUB, HD), 0)
    masks = [(d, row >= d) for d in (1, 2, 4)]
    c = jnp.broadcast_to(carry, (SUB, HD))
    hs = []
    for j in range(RT // SUB):
        aj, bj = a[SUB * j:SUB * (j + 1)], b[SUB * j:SUB * (j + 1)]
        for d, m in masks:
            bj = bj + aj * jnp.where(m, pltpu.roll(bj, d, 0), 0.0)
            aj = aj * jnp.where(m, pltpu.roll(aj, d, 0), 1.0)
        h = bj + aj * c
        hs.append(h)
        c = jnp.broadcast_to(h[SUB - 1:SUB, :], (SUB, HD))
    return jnp.concatenate(hs, axis=0), hs[-1][SUB - 1:SUB, :]


def _scan_down(a, b, carry):
    row = lax.broadcasted_iota(jnp.int32, (SUB, HD), 0)
    masks = [(d, row < SUB - d) for d in (1, 2, 4)]
    c = jnp.broadcast_to(carry, (SUB, HD))
    hs = []
    for j in reversed(range(RT // SUB)):
        aj, bj = a[SUB * j:SUB * (j + 1)], b[SUB * j:SUB * (j + 1)]
        for d, m in masks:
            bj = bj + aj * jnp.where(m, pltpu.roll(bj, SUB - d, 0), 0.0)
            aj = aj * jnp.where(m, pltpu.roll(aj, SUB - d, 0), 1.0)
        h = bj + aj * c
        hs.append(h)
        c = jnp.broadcast_to(h[0:1, :], (SUB, HD))
    return jnp.concatenate(hs[::-1], axis=0), hs[-1][0:1, :]


def _decay(r, sp_d):
    log_a = -LRU_C * r * sp_d
    a = jnp.exp(log_a)
    return a, jnp.sqrt(jnp.maximum(-jnp.tanh(log_a) * (a * a + 1.0), 0.0))


def _decay_bwd(r, sp_d):
    log_a = -LRU_C * r * sp_d
    a = jnp.exp(log_a)
    m2 = jnp.maximum(-jnp.tanh(log_a) * (a * a + 1.0), 0.0)
    inv = lax.rsqrt(m2)
    return a, m2 * inv, inv


def _lru_gates(xc, d, wr_ref, br_ref, wi_ref, bi_ref, sp):
    xb = xc.astype(BF16)
    r = _sigmoid(_dot(xb, wr_ref[d]) + br_ref[d:d + 1, :])
    i = _sigmoid(_dot(xb, wi_ref[d]) + bi_ref[d:d + 1, :])
    a, mult = _decay(r, sp[d:d + 1, :])
    return r, i, a, mult


def _shifted(win, k):
    w = RT + 2 * PADR
    v = win if k == 0 else pltpu.roll(win, (-k) % w, 0)
    return v[PADR:PADR + RT]


def _conv_taps(win):
    return [_shifted(win, k) for k in (-1, 0, 1, 2)]


def _fill_padded(dst, src_ref, S):
    zeros = jnp.zeros((PADR, HD), F32)
    dst[0:PADR, :] = zeros
    dst[PADR + S:2 * PADR + S, :] = zeros

    def cp(i, c):
        t0 = pl.multiple_of(i * RT, RT)
        dst[pl.ds(t0 + PADR, RT), :] = src_ref[pl.ds(t0, RT), :].astype(F32)
        return c
    lax.fori_loop(0, S // RT, cp, 0)


def _conv_fwd_all(zxp, xc_s, cw_ref, cb_ref, S):
    def cv(i, c):
        t0 = pl.multiple_of(i * RT, RT)
        xm1, x0, xp1, xp2 = _conv_taps(zxp[pl.ds(t0, RT + 2 * PADR), :])
        xc_s[pl.ds(t0, RT), :] = (cb_ref[...] + xm1 * cw_ref[0:1, :] + x0 * cw_ref[1:2, :]
                                  + xp1 * cw_ref[2:3, :] + xp2 * cw_ref[3:4, :])
        return c
    lax.fori_loop(0, S // RT, cv, 0)


def _lru_specs(S):
    head = lambda h: (0, h)
    return [pl.BlockSpec((4, HD), head), pl.BlockSpec((1, HD), head),
            pl.BlockSpec((2, None, HD, HD), lambda h: (0, h, 0, 0)), pl.BlockSpec((2, HD), head),
            pl.BlockSpec((2, None, HD, HD), lambda h: (0, h, 0, 0)), pl.BlockSpec((2, HD), head),
            pl.BlockSpec((2, HD), head)]


def _lru_fwd(z6, ya, cw, cb, wr, br, wi, bi, lam):
    S = z6.shape[1]
    nt = S // RT

    def body(z_ref, ya_ref, cw_ref, cb_ref, wr_ref, br_ref, wi_ref, bi_ref, lam_ref, mg_ref, h0_ref, h1_ref, zxp, xc_s):
        sp = _softplus_neg(lam_ref[...])
        _fill_padded(zxp, z_ref.at[0], S)
        _conv_fwd_all(zxp, xc_s, cw_ref, cb_ref, S)

        def scans(i, carry):
            cu, cd = carry
            for u in range(UNROLL):
                j = i * UNROLL + u
                ru = pl.ds(pl.multiple_of(j * RT, RT), RT)
                rd = pl.ds(pl.multiple_of((nt - 1 - j) * RT, RT), RT)
                xu, xd = xc_s[ru, :], xc_s[rd, :]
                _, gi, a, mult = _lru_gates(xu, 0, wr_ref, br_ref, wi_ref, bi_ref, sp)
                hu, cu = _scan_up(a, mult * gi * xu, cu)
                h0_ref[ru, :] = hu
                _, gi, a, mult = _lru_gates(xd, 1, wr_ref, br_ref, wi_ref, bi_ref, sp)
                hd, cd = _scan_down(a, mult * gi * xd, cd)
                h1_ref[rd, :] = hd
            return cu, cd
        z1 = jnp.zeros((1, HD), F32)
        lax.fori_loop(0, nt // UNROLL, scans, (z1, z1))

        def merge(i, c):
            rows = pl.ds(pl.multiple_of(i * RT, RT), RT)
            yb = (h0_ref[rows, :] + h1_ref[rows, :]) * _gelu(z_ref[1, rows, :].astype(F32))
            mg_ref[rows, :] = (ya_ref[rows, :].astype(F32) + _sigmoid(z_ref[2, rows, :].astype(F32)) * yb).astype(BF16)
            return c
        lax.fori_loop(0, nt, merge, 0)

    col = pl.BlockSpec((S, HD), lambda h: (0, h))
    return pl.pallas_call(
        body, name="lru_fwd", grid=(NH,),
        in_specs=[pl.BlockSpec((3, S, HD), lambda h: (1, 0, h)), col] + _lru_specs(S),
        out_specs=[col, col, col],
        out_shape=[jax.ShapeDtypeStruct((S, D), BF16), jax.ShapeDtypeStruct((S, D), F32), jax.ShapeDtypeStruct((S, D), F32)],
        scratch_shapes=[pltpu.VMEM((S + 2 * PADR, HD), F32), pltpu.VMEM((S, HD), F32)],
        compiler_params=_cp(("parallel",)),
    )(z6, ya, cw, cb, wr, br, wi, bi, lam)


def _mm_res_loss(a, w, res, tgt, g):
    S, K = a.shape

    def body(a_ref, w_ref, r_ref, t_ref, g_ref, dx_ref, loss_ref, dg_ref):
        @pl.when(pl.program_id(0) == 0)
        def _():
            loss_ref[...] = jnp.zeros_like(loss_ref)
            dg_ref[...] = jnp.zeros_like(dg_ref)
        xv = r_ref[...] + _dot(a_ref[...], w_ref[...])
        xh, _ = _rms_hat(xv)
        e = xh * g_ref[...] - t_ref[...]
        loss_ref[...] += jnp.sum(e * e) * (0.5 / D)
        dx, dgs = _rms_bwd(e * (1.0 / D), xv, g_ref[...])
        dx_ref[...] = dx
        dg_ref[...] += dgs

    row = pl.BlockSpec((TM, D), lambda i: (i, 0))
    vec = pl.BlockSpec((1, D), lambda i: (0, 0))
    return pl.pallas_call(
        body, name="mm_ffn_out_loss", grid=(S // TM,),
        in_specs=[pl.BlockSpec((TM, K), lambda i: (i, 0)), pl.BlockSpec((K, D), lambda i: (0, 0)), row, row, vec],
        out_specs=[row, pl.BlockSpec((1, 128), lambda i: (0, 0)), vec],
        out_shape=[jax.ShapeDtypeStruct((S, D), F32), jax.ShapeDtypeStruct((1, 128), F32), jax.ShapeDtypeStruct((1, D), F32)],
        compiler_params=_cp(("arbitrary",)),
    )(a, w, res, tgt, g)


def _bwd_ffn_out(dx, w_fo, gu, l, after=()):
    S = dx.shape[0]

    tm = min(TM_BIG, S)

    def body(dx_ref, w_ref, gu_ref, *rest):
        o_ref = rest[-1]
        for r0 in range(0, tm, TM):
            rows = slice(r0, r0 + TM)
            d = _dot_nt(dx_ref[rows, :].astype(BF16), w_ref[...])
            o_ref[0, rows, :] = (d * gu_ref[0, rows, :].astype(F32)).astype(BF16)
            o_ref[1, rows, :] = (d * gu_ref[1, rows, :].astype(F32)).astype(BF16)

    pair = pl.BlockSpec((2, None, tm, DFF_SH), lambda i, s: (0, s, i, 0))
    dgu = pl.pallas_call(
        body, name=f"bwd_ffn_out_{l}", grid=(S // tm, 2),
        in_specs=[pl.BlockSpec((tm, D), lambda i, s: (i, 0)), pl.BlockSpec((DFF_SH, D), lambda i, s: (s, 0)), pair]
        + [ANY] * len(after),
        out_specs=pair,
        out_shape=jax.ShapeDtypeStruct((2, 2, S, DFF_SH), BF16),
        compiler_params=_cp(("parallel", "arbitrary")),
    )(dx, w_fo, gu.reshape(2, 2, S, DFF_SH), *after)
    return dgu.reshape(4, S, DFF_SH)


def _mm_tn(a, b, m_blk, tk, name):
    S, M = a.shape

    def body(a_ref, b_ref, o_ref):
        @pl.when(pl.program_id(1) == 0)
        def _():
            o_ref[...] = jnp.zeros_like(o_ref)
        o_ref[...] += _dot_tn(a_ref[...], b_ref[...].astype(BF16))

    return pl.pallas_call(
        body, name=name, grid=(M // m_blk, S // tk),
        in_specs=[pl.BlockSpec((tk, m_blk), lambda m, k: (k, m)), pl.BlockSpec((tk, D), lambda m, k: (k, 0))],
        out_specs=pl.BlockSpec((m_blk, D), lambda m, k: (m, 0)),
        out_shape=jax.ShapeDtypeStruct((M, D), F32),
        compiler_params=_cp(("parallel", "arbitrary")),
    )(a, b)


def _mm_nt_rms_bwd(a, a_specs, w, w_specs, nk, tm, x, g, dres, name, after=()):
    S = x.shape[0]
    sub = len(a_specs)

    def body(*refs):
        a_refs, w_refs = refs[:sub], refs[sub:2 * sub]
        x_ref, g_ref, r_ref = refs[2 * sub:2 * sub + 3]
        dx_ref, dg_ref, acc = refs[-3:]
        i, k = pl.program_id(0), pl.program_id(1)
        @pl.when(k == 0)
        def _():
            acc[...] = jnp.zeros_like(acc)
        for j in range(sub):
            acc[...] += _dot_nt(a_refs[j][...], w_refs[j][...])

        @pl.when(jnp.logical_and(i == 0, k == 0))
        def _():
            dg_ref[...] = jnp.zeros_like(dg_ref)

        @pl.when(k == nk - 1)
        def _():
            dx, dgs = _rms_bwd(acc[...], x_ref[...], g_ref[...])
            dx_ref[...] = r_ref[...] + dx
            dg_ref[...] += dgs

    row = pl.BlockSpec((tm, D), lambda i, k: (i, 0))
    vec = pl.BlockSpec((1, D), lambda i, k: (0, 0))
    return pl.pallas_call(
        body, name=name, grid=(S // tm, nk),
        in_specs=list(a_specs) + list(w_specs) + [row, vec, row] + [ANY] * len(after),
        out_specs=[row, vec],
        out_shape=[jax.ShapeDtypeStruct((S, D), F32), jax.ShapeDtypeStruct((1, D), F32)],
        scratch_shapes=[pltpu.VMEM((tm, D), F32)],
        compiler_params=_cp(("arbitrary", "arbitrary")),
    )(*[a] * sub, *[w] * sub, x, g, dres, *after)


def _dw_ffn_in(h, dgu, l):
    S = h.shape[0]

    def body(h_ref, b_ref, o_ref):
        @pl.when(pl.program_id(1) == 0)
        def _():
            o_ref[...] = jnp.zeros_like(o_ref)
        o_ref[...] += _dot_tn(h_ref[...], b_ref[...])

    tk = min(2 * TM_BIG, S)
    return pl.pallas_call(
        body, name=f"dw_ffn_in_{l}", grid=(4, S // tk),
        in_specs=[pl.BlockSpec((tk, D), lambda j, k: (k, 0)), pl.BlockSpec((None, tk, DFF_SH), lambda j, k: (j, k, 0))],
        out_specs=pl.BlockSpec((None, D, DFF_SH), lambda j, k: (j, 0, 0)),
        out_shape=jax.ShapeDtypeStruct((4, D, DFF_SH), F32),
        compiler_params=_cp(("parallel", "arbitrary")),
    )(h, dgu)


_HALF_COMPS = ((0, 1, 3), (4, 2, 5))


def _dw_in(h, dz6, l, after=()):
    S = h.shape[0]

    def body(h_ref, d0_ref, d1_ref, d2_ref, *rest):
        o_ref = rest[-1]

        @pl.when(pl.program_id(1) == 0)
        def _():
            o_ref[...] = jnp.zeros_like(o_ref)
        hv = h_ref[...]
        for q, d_ref in enumerate((d0_ref, d1_ref, d2_ref)):
            for hf in range(2):
                col = 1024 * q + 512 * hf
                o_ref[col // 1536, :, col % 1536:col % 1536 + 512] += _dot_tn(hv, d_ref[:, 512 * hf:512 * (hf + 1)])

    tk = min(TM_BIG, S)

    def comp(q):
        return pl.BlockSpec((None, tk, D), lambda p, k: (jnp.where(p == 0, _HALF_COMPS[0][q], _HALF_COMPS[1][q]), k, 0))

    return pl.pallas_call(
        body, name=f"dw_in_{l}", grid=(2, S // tk),
        in_specs=[pl.BlockSpec((tk, D), lambda p, k: (k, 0)), comp(0), comp(1), comp(2)] + [ANY] * len(after),
        out_specs=pl.BlockSpec((2, D, 1536), lambda p, k: (p, 0, 0)),
        out_shape=jax.ShapeDtypeStruct((4, D, 1536), F32),
        compiler_params=_cp(("parallel", "arbitrary")),
    )(h, dz6, dz6, dz6, *after)


def _bwd_out(dx, w_o, merged, l):
    S = dx.shape[0]

    def body(dx_ref, w_ref, m_ref, dm_ref, dw_ref):
        @pl.when(pl.program_id(0) == 0)
        def _():
            dw_ref[...] = jnp.zeros_like(dw_ref)
        dxb = dx_ref[...].astype(BF16)
        dm_ref[...] = _dot_nt(dxb, w_ref[...]).astype(BF16)
        dw_ref[...] += _dot_tn(m_ref[...], dxb)

    tm = TM
    row = pl.BlockSpec((tm, D), lambda i: (i, 0))
    return pl.pallas_call(
        body, name=f"bwd_out_{l}", grid=(S // tm,),
        in_specs=[row, pl.BlockSpec((D, D), lambda i: (0, 0)), row],
        out_specs=[row, pl.BlockSpec((D, D), lambda i: (0, 0))],
        out_shape=[jax.ShapeDtypeStruct((S, D), BF16), jax.ShapeDtypeStruct((D, D), F32)],
        compiler_params=_cp(("arbitrary",)),
    )(dx, w_o, merged)


def _gmlp_bwd(dm, z6, ws_b, wst_b, bs_b, lg, lb, after=()):
    S = z6.shape[1]
    ts = min(GMLP_ROWS, S)

    def body(dm_ref, z_ref, ws_ref, wst_ref, bs_ref, lg_ref, lb_ref, *rest):
        dz_ref, dws_ref, dbs_ref, dlg_ref, dlb_ref, mix, dv = rest[-7:]

        @pl.when(pl.program_id(0) == 0)
        def _():
            dws_ref[...] = jnp.zeros_like(dws_ref)
            dbs_ref[...] = jnp.zeros_like(dbs_ref)
            dlg_ref[...] = jnp.zeros_like(dlg_ref)
            dlb_ref[...] = jnp.zeros_like(dlb_ref)
        for r0 in range(0, ts, CHUNK):
            rows = slice(r0, r0 + CHUNK)
            gv, dgelu_v = _gelu_and_grad(z_ref[1, rows, :].astype(F32))
            xc = gv - jnp.mean(gv, axis=-1, keepdims=True)
            rs = lax.rsqrt(jnp.mean(xc * xc, axis=-1, keepdims=True) + EPS)
            vh = xc * rs
            vb = (vh * lg_ref[...] + lb_ref[...]).astype(BF16)
            for gi in range(NH):
                cs = slice(gi * HD, (gi + 1) * HD)
                mix[rows, cs] = _dot(ws_ref[gi], vb[:, cs])
            u, dgelu_u = _gelu_and_grad(z_ref[0, rows, :].astype(F32))
            sa = _sigmoid(z_ref[2, rows, :].astype(F32))
            dya = dm_ref[rows, :].astype(F32) * sa
            dym = dya * (mix[rows, :] + bs_ref[...])
            dz_ref[2, rows, :] = (dym * u * (1.0 - sa)).astype(BF16)
            dz_ref[0, rows, :] = (dym * dgelu_u).astype(BF16)
            dmix = dya * u
            dmb = dmix.astype(BF16)
            for gi in range(NH):
                cs = slice(gi * HD, (gi + 1) * HD)
                dv[rows, cs] = _dot(wst_ref[gi], dmb[:, cs])
                dws_ref[gi] += _dot_nt(dmb[:, cs], vb[:, cs])
                dbs_ref[gi] += jnp.broadcast_to(jnp.sum(dmix[:, cs], axis=1, keepdims=True), (CHUNK, HD))
            dvv = dv[rows, :]
            dlg_ref[...] += jnp.sum(dvv * vh, axis=0, keepdims=True)
            dlb_ref[...] += jnp.sum(dvv, axis=0, keepdims=True)
            dvh = dvv * lg_ref[...]
            dgv = rs * (dvh - jnp.mean(dvh, axis=-1, keepdims=True) - vh * jnp.mean(dvh * vh, axis=-1, keepdims=True))
            dz_ref[1, rows, :] = (dgv * dgelu_v).astype(BF16)

    vec = pl.BlockSpec((1, D), lambda i: (0, 0))
    mat = pl.BlockSpec((NH, CHUNK, CHUNK), lambda i: (0, 0, 0))
    return pl.pallas_call(
        body, name="gmlp_bwd", grid=(S // ts,),
        in_specs=[pl.BlockSpec((ts, D), lambda i: (i, 0)), pl.BlockSpec((3, ts, D), lambda i: (0, i, 0)), mat, mat,
                  pl.BlockSpec((CHUNK, D), lambda i: (0, 0)), vec, vec] + [ANY] * len(after),
        out_specs=[pl.BlockSpec((3, ts, D), lambda i: (0, i, 0)), mat, mat, vec, vec],
        out_shape=[jax.ShapeDtypeStruct((6, S, D), BF16), jax.ShapeDtypeStruct((NH, CHUNK, CHUNK), F32),
                   jax.ShapeDtypeStruct((NH, CHUNK, HD), F32), jax.ShapeDtypeStruct((1, D), F32), jax.ShapeDtypeStruct((1, D), F32)],
        scratch_shapes=[pltpu.VMEM((ts, D), F32), pltpu.VMEM((ts, D), F32)],
        compiler_params=_cp(("arbitrary",)),
    )(dm, z6, ws_b, wst_b, bs_b, lg, lb, *after)


def _lru_bwd(dz6, dm, z6, h0, h1, cw, cb, wr, br, wi, bi, lam, after=()):
    S = z6.shape[1]
    nt = S // RT

    def body(dz_in, dm_ref, z_ref, h0_ref, h1_ref, cw_ref, cb_ref, wr_ref, br_ref, wi_ref, bi_ref, lam_ref, *rest):
        dz_ref, dcw_ref, dcb_ref, dwr_ref, dbr_ref, dwi_ref, dbi_ref, dlam_ref, zxp, xc_s, dhs_s, dxcp, r_s, lam_s = rest[-14:]
        del dz_in
        lam = lam_ref[...]
        sp = _softplus_neg(lam)
        row = _row_iota()
        _fill_padded(zxp, z_ref.at[0], S)
        _conv_fwd_all(zxp, xc_s, cw_ref, cb_ref, S)
        zeros = jnp.zeros((PADR, HD), F32)
        dxcp[0:PADR, :] = zeros
        dxcp[PADR + S:2 * PADR + S, :] = zeros
        dwr_ref[...] = jnp.zeros_like(dwr_ref)
        dwi_ref[...] = jnp.zeros_like(dwi_ref)

        def pre(i, c):
            rows = pl.ds(pl.multiple_of(i * RT, RT), RT)
            hs = h0_ref[rows, :] + h1_ref[rows, :]
            dmv = dm_ref[rows, :].astype(F32)
            sb = _sigmoid(z_ref[2, rows, :].astype(F32))
            gg, dgg = _gelu_and_grad(z_ref[1, rows, :].astype(F32))
            dz_ref[2, rows, :] = (dmv * hs * gg * sb * (1.0 - sb)).astype(BF16)
            dyb = dmv * sb
            dz_ref[1, rows, :] = (dyb * hs * dgg).astype(BF16)
            dhs_s[rows, :] = dyb * gg
            return c
        lax.fori_loop(0, nt, pre, 0)

        def gate_bwd(d, gates, lamv, da, xc):
            r, gi, a, mult, inv_mult = gates
            lx, lm = lamv * xc, lamv * mult
            dlog_r = (da - (lx * gi) * (a * inv_mult)) * a * r
            dpr = dlog_r * (1.0 - r) * (-LRU_C * sp[d:d + 1, :])
            dpi = (lx * mult) * gi * (1.0 - gi)
            xb, dprb, dpib = xc.astype(BF16), dpr.astype(BF16), dpi.astype(BF16)
            dwr_ref[d] += _dot_tn(xb, dprb)
            dwi_ref[d] += _dot_tn(xb, dpib)
            dxc = lm * gi + _dot_nt(dprb, wr_ref[d]) + _dot_nt(dpib, wi_ref[d])
            return dxc, (jnp.sum(dlog_r, axis=0, keepdims=True) * (-LRU_C), jnp.sum(dpr, axis=0, keepdims=True),
                         jnp.sum(dpi, axis=0, keepdims=True))

        def rgates(i, c):
            for u in range(UNROLL):
                rows = pl.ds(pl.multiple_of((i * UNROLL + u) * RT, RT), RT)
                xb = xc_s[rows, :].astype(BF16)
                for d in range(2):
                    r_s[d, rows, :] = _sigmoid(_dot(xb, wr_ref[d]) + br_ref[d:d + 1, :])
            return c
        lax.fori_loop(0, nt // UNROLL, rgates, 0)

        def chains(i, carry):
            qn, qp = carry
            for u in range(UNROLL):
                j = i * UNROLL + u
                rd = pl.ds(pl.multiple_of((nt - 1 - j) * RT, RT), RT)
                a, dhs = _decay(r_s[0, rd, :], sp[0:1, :])[0], dhs_s[rd, :]
                q, q_first = _scan_down(a, a * dhs, qn)
                lam_s[0, rd, :] = dhs + jnp.where(row == RT - 1, qn, pltpu.roll(q, RT - 1, 0))
                qn = q_first
                ru = pl.ds(pl.multiple_of(j * RT, RT), RT)
                a, dhs = _decay(r_s[1, ru, :], sp[1:2, :])[0], dhs_s[ru, :]
                q, q_last = _scan_up(a, a * dhs, qp)
                lam_s[1, ru, :] = dhs + jnp.where(row == 0, qp, pltpu.roll(q, 1, 0))
                qp = q_last
            return qn, qp

        z1 = jnp.zeros((1, HD), F32)
        lax.fori_loop(0, nt // UNROLL, chains, (z1, z1))

        ct = min(GRAD_ROWS, S)
        crow = lax.broadcasted_iota(jnp.int32, (ct, HD), 0)

        def tile_grads(i, acc):
            t0 = pl.multiple_of(i * ct, ct)
            rows = pl.ds(t0, ct)
            xc = xc_s[rows, :]
            xb = xc.astype(BF16)
            tp = pl.multiple_of(jnp.maximum(t0 - PADR, 0), PADR)
            prev = jnp.where(t0 > 0, h0_ref[pl.ds(tp, PADR), :][PADR - 1:PADR, :], 0.0)
            tn = pl.multiple_of(jnp.minimum(t0 + ct, S - PADR), PADR)
            nxt = jnp.where(t0 + ct < S, h1_ref[pl.ds(tn, PADR), :][0:1, :], 0.0)
            hside = (jnp.where(crow == 0, prev, pltpu.roll(h0_ref[rows, :], 1, 0)),
                     jnp.where(crow == ct - 1, nxt, pltpu.roll(h1_ref[rows, :], ct - 1, 0)))
            dxc, sums = 0.0, ()
            for d in range(2):
                r = r_s[d, rows, :]
                gi = _sigmoid(_dot(xb, wi_ref[d]) + bi_ref[d:d + 1, :])
                lamv = lam_s[d, rows, :]
                dxc_d, s_d = gate_bwd(d, (r, gi) + _decay_bwd(r, sp[d:d + 1, :]), lamv, lamv * hside[d], xc)
                dxc = dxc + dxc_d
                sums = sums + s_d
            dxcp[pl.ds(t0 + PADR, ct), :] = dxc
            return tuple(x + y for x, y in zip(acc, sums))

        s_sp0, s_br0, s_bi0, s_sp1, s_br1, s_bi1 = lax.fori_loop(0, S // ct, tile_grads, (z1,) * 6)

        dsp = jnp.concatenate([s_sp0, s_sp1], axis=0)
        dlam_ref[...] = -dsp * _sigmoid(-lam)
        dbr_ref[...] = jnp.concatenate([s_br0, s_br1], axis=0)
        dbi_ref[...] = jnp.concatenate([s_bi0, s_bi1], axis=0)

        def conv_bwd(i, carry):
            c0, c1, c2, c3, cb_ = carry
            t0 = pl.multiple_of(i * RT, RT)
            dwin = dxcp[pl.ds(t0, RT + 2 * PADR), :]
            d0 = _shifted(dwin, 0)
            dz_ref[0, pl.ds(t0, RT), :] = (_shifted(dwin, 1) * cw_ref[0:1, :] + d0 * cw_ref[1:2, :]
                                           + _shifted(dwin, -1) * cw_ref[2:3, :] + _shifted(dwin, -2) * cw_ref[3:4, :]).astype(BF16)
            xm1, x0, xp1, xp2 = _conv_taps(zxp[pl.ds(t0, RT + 2 * PADR), :])
            sm = lambda v: jnp.sum(v, axis=0, keepdims=True)
            return c0 + sm(d0 * xm1), c1 + sm(d0 * x0), c2 + sm(d0 * xp1), c3 + sm(d0 * xp2), cb_ + sm(d0)

        c0, c1, c2, c3, cb_ = lax.fori_loop(0, nt, conv_bwd, (z1, z1, z1, z1, z1))
        dcw_ref[...] = jnp.concatenate([c0, c1, c2, c3], axis=0)
        dcb_ref[...] = cb_

    col = pl.BlockSpec((S, HD), lambda h: (0, h))
    head = lambda h: (0, h)
    wspec = pl.BlockSpec((2, None, HD, HD), lambda h: (0, h, 0, 0))
    return pl.pallas_call(
        body, name="lru_bwd", grid=(NH,),
        in_specs=[pl.BlockSpec(memory_space=pl.ANY), col, pl.BlockSpec((3, S, HD), lambda h: (1, 0, h)), col, col] + _lru_specs(S)
        + [ANY] * len(after),
        out_specs=[pl.BlockSpec((3, S, HD), lambda h: (1, 0, h)), pl.BlockSpec((4, HD), head), pl.BlockSpec((1, HD), head),
                   wspec, pl.BlockSpec((2, HD), head), wspec, pl.BlockSpec((2, HD), head), pl.BlockSpec((2, HD), head)],
        out_shape=[jax.ShapeDtypeStruct((6, S, D), BF16), jax.ShapeDtypeStruct((4, D), F32), jax.ShapeDtypeStruct((1, D), F32),
                   jax.ShapeDtypeStruct((2, NH, HD, HD), F32), jax.ShapeDtypeStruct((2, D), F32),
                   jax.ShapeDtypeStruct((2, NH, HD, HD), F32), jax.ShapeDtypeStruct((2, D), F32), jax.ShapeDtypeStruct((2, D), F32)],
        scratch_shapes=[pltpu.VMEM((S + 2 * PADR, HD), F32), pltpu.VMEM((S, HD), F32), pltpu.VMEM((S, HD), F32),
                        pltpu.VMEM((S + 2 * PADR, HD), F32), pltpu.VMEM((2, S, HD), F32), pltpu.VMEM((2, S, HD), F32)],
        input_output_aliases={0: 0},
        compiler_params=_cp(("parallel",)),
    )(dz6, dm, z6, h0, h1, cw, cb, wr, br, wi, bi, lam, *after)


LAYER_SMALL = ("norm1_g", "gmlp_ln_g", "gmlp_ln_b", "gmlp_w_s", "gmlp_b_s", "conv_w", "conv_b",
               "lru_w_r", "lru_b_r", "lru_w_i", "lru_b_i", "lru_lambda", "norm2_g")


def _layer_operands(l, p):
    ws_b = p["gmlp_w_s"][l].astype(BF16)
    tm = dict(ws_b=ws_b, wst_b=jnp.swapaxes(ws_b, 1, 2), bs_b=jnp.repeat(p["gmlp_b_s"][l].T, HD, axis=1),
              lg=p["gmlp_ln_g"][l][None], lb=p["gmlp_ln_b"][l][None])
    lru = (p["conv_w"][l], p["conv_b"][l][None], p["lru_w_r"][l].astype(BF16), p["lru_b_r"][l],
           p["lru_w_i"][l].astype(BF16), p["lru_b_i"][l], p["lru_lambda"][l])
    return (p["norm1_g"][l][None], p["norm2_g"][l][None]), tm, lru


def _forward_layer(l, x, p, wb, after=(), rest=None, near_end=None, operands=None, loss=None):
    (g1, g2), tm, lru = _layer_operands(l, p) if operands is None else operands
    z6, hn1 = _mm_in(x, g1, wb["w_in"], l, after)
    ya = _gmlp_fwd(z6, tm["ws_b"], tm["bs_b"], tm["lg"], tm["lb"])
    merged, h0, h1 = _lru_fwd(z6, ya, *lru)
    if rest is not None:
        wb = dict(wb, **rest(merged))
    x1 = _mm_res(merged, wb["w_out"], x, l, "mm_out")
    gu, ff, hn2 = _mm_ffn_in(x1, g2, wb["w_ffn_in"], l)
    if loss is None:
        x2 = _mm_res(ff, wb["w_ffn_out"], x1, l, "mm_ffn_out", () if near_end is None else tuple(near_end(gu)))
    else:
        x2 = _mm_res_loss(ff, wb["w_ffn_out"], x1, *loss)
    return x2, dict(x=x, z6=z6, h0=h0, h1=h1, merged=merged, x1=x1, gu=gu, ff=ff, g1=g1, g2=g2, tm=tm, lru=lru,
                    hn1=hn1, hn2=hn2, wb=wb)


def _backward_layer(l, dx, s, after=(), midway=None, midway2=None, midway3=None, late=None):
    S = dx.shape[0]
    tm, wb = s["tm"], s["wb"]
    g2 = s["g2"]
    dgu = _bwd_ffn_out(dx, wb["w_ffn_out"], s["gu"], l, after)
    tmb = min(TM_BIG, S)
    dwfo = _mm_tn(s["ff"], dx, DFF_SH, tmb, f"dw_ffn_out_{l}")
    dx1, dg2 = _mm_nt_rms_bwd(
        dgu, [pl.BlockSpec((None, tmb, DFF_SH), lambda i, k: (k, i, 0))],
        wb["w_ffn_in"], [pl.BlockSpec((None, D, DFF_SH), lambda i, k: (k, 0, 0))],
        4, tmb, s["x1"], g2, dx, f"bwd_ffn_in_{l}")
    dwfi = _dw_ffn_in(s["hn2"], dgu, l)
    dmg, dwo = _bwd_out(dx1, wb["w_out"], s["merged"], l)
    mid = () if midway is None else tuple(midway([dwo, dwfi, dwfo]))
    dz6, dws, dbs, dlg, dlb = _gmlp_bwd(dmg, s["z6"], tm["ws_b"], tm["wst_b"], tm["bs_b"], tm["lg"], tm["lb"], mid)
    mid2 = () if midway2 is None else tuple(midway2(dws))
    dz6, dcw, dcb, dwr, dbr, dwi, dbi, dlam = _lru_bwd(dz6, dmg, s["z6"], s["h0"], s["h1"], *s["lru"], after=mid2)

    sub = 3

    def dz_tile(j):
        return pl.BlockSpec((None, tmb, 512), lambda i, k: ((sub * k + j) // 2, i, (sub * k + j) % 2))

    def w_tile(j):
        def w_map(i, k):
            sh, tl = _in_tile(sub * k + j)
            return (sh, 0, tl)
        return pl.BlockSpec((None, D, 512), w_map)

    small = dict(gmlp_ln_g=dlg[0], gmlp_ln_b=dlb[0], gmlp_w_s=dws, gmlp_b_s=dbs[:, :, 0], conv_w=dcw, conv_b=dcb[0],
                 lru_w_r=dwr, lru_b_r=dbr, lru_w_i=dwi, lru_b_i=dbi, lru_lambda=dlam, norm2_g=dg2[0])
    mid3 = () if midway3 is None else tuple(midway3(small))
    dwin = _dw_in(s["hn1"], dz6, l, mid3)
    tail = () if late is None else tuple(late([dwin]))
    dx0, dg1 = _mm_nt_rms_bwd(
        dz6, [dz_tile(j) for j in range(sub)], wb["w_in"], [w_tile(j) for j in range(sub)],
        N_IN_T // sub, tmb, s["x"], s["g1"], dx1, f"bwd_in_{l}", tail)
    return dx0, [dwin, dwo, dwfi, dwfo], dict(small, norm1_g=dg1[0])


def _local_step(x, tgt, p, wbs):
    saved = []
    for l in range(2):
        x, s = _forward_layer(l, x, p, wbs[l], loss=(tgt, p["final_g"][None]) if l else None)
        saved.append(s)
    dx, loss_v, dfg = x
    big, smalls = [None, None], [None, None]
    for l in (1, 0):
        dx, big[l], smalls[l] = _backward_layer(l, dx, saved[l])
    small = {k: jnp.stack([smalls[0][k], smalls[1][k]]) for k in LAYER_SMALL}
    small["final_g"] = dfg[0]
    return loss_v, dx, big, small


def _place():
    x, y, c = lax.axis_index("x"), lax.axis_index("y"), lax.axis_index("c")
    return x, y, c, 2 * x + y


def _chip_at(x, y, d):
    px = 1 - x if d & 2 else x
    py = 1 - y if d & 1 else y
    return px, py, 2 * px + py


HBM = pl.BlockSpec(memory_space=pltpu.HBM)
SEM = pl.BlockSpec(memory_space=pltpu.SEMAPHORE)
DATAFLOW = pltpu.SideEffectType.DATAFLOW_SIDE_EFFECTING


def _in_hbm(a):
    return pltpu.with_memory_space_constraint(a, pltpu.HBM)


def _cast_into(wfs, l, chip_arr, name):
    n = len(wfs)

    def body(ch_ref, *refs):
        for w_ref, o_ref in zip(refs[:n], refs[n:]):
            o_ref[...] = w_ref[...].astype(BF16)

    halves = [(wf.shape[1] // 2, wf.shape[2]) for wf in wfs]
    return pl.pallas_call(
        body, name=name, out_shape=[jax.ShapeDtypeStruct((4, 2, rh, cols), BF16) for rh, cols in halves],
        grid_spec=pltpu.PrefetchScalarGridSpec(
            num_scalar_prefetch=1, grid=(2,),
            in_specs=[pl.BlockSpec((None, None, rh, cols), lambda h, ch: (l, h, 0, 0)) for rh, cols in halves],
            out_specs=[pl.BlockSpec((None, None, rh, cols), lambda h, ch: (ch[0], h, 0, 0)) for rh, cols in halves]),
        compiler_params=_cp(("parallel",)),
    )(chip_arr, *[wf.reshape(2, 2, rh, cols) for wf, (rh, cols) in zip(wfs, halves)])


def _half_block(ref, chip, half, to, send_sem, recv_sem):
    blk = ref.at[chip, half]
    return pltpu.make_async_remote_copy(src_ref=blk, dst_ref=blk, send_sem=send_sem, recv_sem=recv_sem,
                                        device_id=to, device_id_type=MESH)


def _gather_weights(bufs, tiny):
    nt = len(bufs)
    n_ici = max(nt * 3, 1)

    def body(*refs):
        tiny_ref = refs[nt]
        o_refs, tiny_o = refs[nt + 1:2 * nt + 1], refs[2 * nt + 1]
        send, recv, fsend, frecv, tsend, trecv, lsem = refs[2 * nt + 2:]
        x, y, c, chip = _place()
        local = pltpu.make_async_copy(tiny_ref, tiny_o.at[chip], lsem)
        local.start()

        def tin(d, origin_chip, to):
            return pltpu.make_async_remote_copy(
                src_ref=tiny_ref, dst_ref=tiny_o.at[origin_chip], send_sem=tsend.at[d - 1], recv_sem=trecv.at[d - 1],
                device_id=to, device_id_type=MESH)

        sends = []
        for t in range(nt):
            for d in (1, 2, 3):
                px, py, _ = _chip_at(x, y, d)
                sends.append(_half_block(o_refs[t], chip, c, (px, py, c), send.at[3 * t + d - 1], recv.at[3 * t + d - 1]))
        for d in (1, 2, 3):
            px, py, _ = _chip_at(x, y, d)
            sends.append(tin(d, chip, (px, py, c)))
        for cp in sends:
            cp.start()
        passed = []
        for t in range(nt):
            for d in (1, 2, 3):
                k = 3 * t + d - 1
                _, _, pchip = _chip_at(x, y, d)
                _half_block(o_refs[t], pchip, c, (x, y, c), send.at[k], recv.at[k]).wait_recv()
                f = _half_block(o_refs[t], pchip, c, (x, y, 1 - c), fsend.at[k], frecv.at[k])
                f.start()
                passed.append(f)
        for t in range(nt):
            for d in (1, 2, 3):
                k = 3 * t + d - 1
                _, _, pchip = _chip_at(x, y, d)
                _half_block(o_refs[t], pchip, 1 - c, (x, y, 1 - c), fsend.at[k], frecv.at[k]).wait_recv()
        for d in (1, 2, 3):
            _, _, pchip = _chip_at(x, y, d)
            tin(d, pchip, (x, y, c)).wait_recv()
        for cp in sends + passed:
            cp.wait_send()
        local.wait()

    out_shape = [jax.ShapeDtypeStruct(b.shape, b.dtype) for b in bufs]
    out_shape.append(jax.ShapeDtypeStruct((4,) + tiny.shape, tiny.dtype))
    outs = pl.pallas_call(
        body, name="gather_weights_0", out_shape=out_shape,
        in_specs=[ANY] * (nt + 1), out_specs=[ANY] * (nt + 1),
        scratch_shapes=[pltpu.SemaphoreType.DMA((n_ici,)), pltpu.SemaphoreType.DMA((n_ici,)),
                        pltpu.SemaphoreType.DMA((n_ici,)), pltpu.SemaphoreType.DMA((n_ici,)),
                        pltpu.SemaphoreType.DMA((3,)), pltpu.SemaphoreType.DMA((3,)), pltpu.SemaphoreType.DMA],
        input_output_aliases={t: t for t in range(nt)},
        compiler_params=_cp(has_side_effects=True),
    )(*bufs, tiny)
    return outs[:nt], outs[nt]


def _gather_start(bufs, tag, after=()):
    nt, na = len(bufs), len(after)

    def body(*refs):
        b_refs = refs[:nt]
        send, recv = refs[nt + na], refs[nt + na + 1]
        token = refs[2 * nt + na + 2]
        x, y, c, chip = _place()
        for t in range(nt):
            for d in (1, 2, 3):
                px, py, _ = _chip_at(x, y, d)
                _half_block(b_refs[t], chip, c, (px, py, c), send.at[3 * t + d - 1], recv.at[3 * t + d - 1]).start()
        token[...] = jnp.zeros_like(token)

    outs = pl.pallas_call(
        body, name=f"gather_start_{tag}",
        out_shape=(pltpu.SemaphoreType.DMA((3 * nt,)), pltpu.SemaphoreType.DMA((3 * nt,)),
                   *[pltpu.HBM(b.shape, b.dtype) for b in bufs], jax.ShapeDtypeStruct((8, 128), F32)),
        in_specs=[HBM] * nt + [ANY] * na, out_specs=(SEM, SEM, *[HBM] * nt, pl.BlockSpec(memory_space=pltpu.VMEM)),
        input_output_aliases={t: 2 + t for t in range(nt)},
        compiler_params=pltpu.CompilerParams(has_side_effects=DATAFLOW),
    )(*[_in_hbm(b) for b in bufs], *after)
    return outs[0], outs[1], list(outs[2:2 + nt]), outs[2 + nt]


def _gather_wait(send, recv, bufs, after, tag):
    nt = len(bufs)

    def body(*refs):
        b_refs = refs[:nt]
        send_ref, recv_ref = refs[nt], refs[nt + 1]
        x, y, c, chip = _place()
        for t in range(nt):
            for d in (1, 2, 3):
                k = 3 * t + d - 1
                px, py, pchip = _chip_at(x, y, d)
                _half_block(b_refs[t], chip, c, (px, py, c), send_ref.at[k], recv_ref.at[k]).wait_send()
                _half_block(b_refs[t], pchip, c, (px, py, c), send_ref.at[k], recv_ref.at[k]).wait_recv()

    after = tuple(after) if isinstance(after, (tuple, list)) else (after,)
    outs = pl.pallas_call(
        body, name=f"gather_wait_{tag}", out_shape=[pltpu.HBM(b.shape, b.dtype) for b in bufs],
        in_specs=[HBM] * nt + [SEM, SEM] + [ANY] * len(after), out_specs=[HBM] * nt,
        input_output_aliases={t: t for t in range(nt)},
        compiler_params=pltpu.CompilerParams(has_side_effects=DATAFLOW),
    )(*bufs, send, recv, *after)
    return list(outs)


def _gather_pass_on(bufs, tag):
    nt = len(bufs)

    def body(*refs):
        o_refs = refs[nt:2 * nt]
        fsend, frecv = refs[2 * nt:]
        x, y, c, _ = _place()
        cps = []
        for t in range(nt):
            for d in (1, 2, 3):
                k = 3 * t + d - 1
                _, _, pchip = _chip_at(x, y, d)
                cps.append(_half_block(o_refs[t], pchip, c, (x, y, 1 - c), fsend.at[k], frecv.at[k]))
        for cp in cps:
            cp.start()
        for t in range(nt):
            for d in (1, 2, 3):
                k = 3 * t + d - 1
                _, _, pchip = _chip_at(x, y, d)
                _half_block(o_refs[t], pchip, 1 - c, (x, y, 1 - c), fsend.at[k], frecv.at[k]).wait_recv()
        for cp in cps:
            cp.wait_send()

    return pl.pallas_call(
        body, name=f"gather_pass_on_{tag}", out_shape=[jax.ShapeDtypeStruct(b.shape, b.dtype) for b in bufs],
        in_specs=[ANY] * nt, out_specs=[ANY] * nt,
        scratch_shapes=[pltpu.SemaphoreType.DMA((3 * nt,)), pltpu.SemaphoreType.DMA((3 * nt,))],
        input_output_aliases={t: t for t in range(nt)},
        compiler_params=_cp(has_side_effects=True),
    )(*bufs)


def _chip_copy(c_ref, land_ref, x, y, c, d, send_sem, recv_sem):
    px, py, pchip = _chip_at(x, y, d)
    return pltpu.make_async_remote_copy(src_ref=c_ref.at[pchip], dst_ref=land_ref.at[d - 1], send_sem=send_sem, recv_sem=recv_sem,
                                        device_id=(px, py, c), device_id_type=MESH)


def _exchange_start(srcs, lands, copies, nsem, name):
    ns, n = len(srcs), len(srcs) + len(lands)

    def body(*refs):
        for cp in copies(refs[:ns], refs[ns:n], refs[n], refs[n + 1]):
            cp.start()
        token = refs[2 * n + 2]
        token[...] = jnp.zeros_like(token)

    outs = pl.pallas_call(
        body, name=name,
        out_shape=(pltpu.SemaphoreType.DMA((nsem,)), pltpu.SemaphoreType.DMA((nsem,)),
                   *[pltpu.HBM(a.shape, a.dtype) for a in list(srcs) + list(lands)], jax.ShapeDtypeStruct((8, 128), F32)),
        in_specs=[HBM] * n, out_specs=(SEM, SEM, *[HBM] * n, pl.BlockSpec(memory_space=pltpu.VMEM)),
        input_output_aliases={i: 2 + i for i in range(n)},
        compiler_params=pltpu.CompilerParams(has_side_effects=DATAFLOW),
    )(*[_in_hbm(a) for a in list(srcs) + list(lands)])
    return outs[0], outs[1], list(outs[2:2 + ns]), list(outs[2 + ns:2 + n]), outs[2 + n]


def _exchange_wait(send, recv, srcs, lands, after, copies, name):
    ns, n = len(srcs), len(srcs) + len(lands)

    def body(*refs):
        for cp in copies(refs[:ns], refs[ns:n], refs[n], refs[n + 1]):
            cp.wait_send()
            cp.wait_recv()

    outs = pl.pallas_call(
        body, name=name, out_shape=[pltpu.HBM(a.shape, a.dtype) for a in list(srcs) + list(lands)],
        in_specs=[HBM] * n + [SEM, SEM, ANY], out_specs=[HBM] * n,
        input_output_aliases={i: i for i in range(n)},
        compiler_params=pltpu.CompilerParams(has_side_effects=DATAFLOW),
    )(*srcs, *lands, send, recv, after)
    return list(outs[:ns]), list(outs[ns:])


def _pass_on_copies(b_refs, land_refs, send, recv):
    del land_refs
    x, y, c, _ = _place()
    return [_half_block(b_refs[t], _chip_at(x, y, d)[2], c, (x, y, 1 - c), send.at[3 * t + d - 1], recv.at[3 * t + d - 1])
            for t in range(len(b_refs)) for d in (1, 2, 3)]


def _chips_copies(c_refs, land_refs, send, recv):
    x, y, c, _ = _place()
    return [_chip_copy(c_refs[t], land_refs[t], x, y, c, d, send.at[3 * t + d - 1], recv.at[3 * t + d - 1])
            for t in range(len(c_refs)) for d in (1, 2, 3)]


def _sibling_copies(g_refs, land_refs, send, recv):
    x, y, c, _ = _place()
    return [pltpu.make_async_remote_copy(
        src_ref=g_refs[t].at[k, 1 - c], dst_ref=land_refs[t].at[k], send_sem=send.at[4 * t + k], recv_sem=recv.at[4 * t + k],
        device_id=(x, y, 1 - c), device_id_type=MESH) for t in range(len(g_refs)) for k in range(4)]


def _join_copies(f_refs, land_refs, send, recv):
    del land_refs
    x, y, c, _ = _place()
    return [pltpu.make_async_remote_copy(
        src_ref=f_refs[t].at[c], dst_ref=f_refs[t].at[c], send_sem=send.at[t], recv_sem=recv.at[t],
        device_id=(x, y, 1 - c), device_id_type=MESH) for t in range(len(f_refs))]


def _add_half(gs, rs, c_arr, name):
    n = len(gs)

    def body(c_ref, *refs):
        for g_ref, r_ref, o_ref in zip(refs[:n], refs[n:2 * n], refs[2 * n:]):
            o_ref[...] = (g_ref[...] + r_ref[...]).astype(BF16)

    def own(g):
        return pl.BlockSpec((None, None) + g.shape[2:], lambda k, cr: (k, cr[0], 0, 0))

    def blk(g):
        return pl.BlockSpec((None,) + g.shape[2:], lambda k, cr: (k, 0, 0))

    return pl.pallas_call(
        body, name=name, out_shape=[jax.ShapeDtypeStruct((4,) + g.shape[2:], BF16) for g in gs],
        grid_spec=pltpu.PrefetchScalarGridSpec(
            num_scalar_prefetch=1, grid=(4,),
            in_specs=[own(g) for g in gs] + [blk(g) for g in gs], out_specs=[blk(g) for g in gs]),
        compiler_params=_cp(("parallel",)),
    )(c_arr, *gs, *rs)


def _sum_chips(css, r3s, place_arr, name):
    n = len(css)

    def body(pl_ref, *refs):
        up = lambda ref: ref[...].astype(F32)
        for t in range(n):
            a_ref, (r0_ref, r1_ref, r2_ref), o_ref = refs[t], refs[n + 3 * t:n + 3 * t + 3], refs[4 * n + t]
            o_ref[...] = ((up(a_ref) + up(r0_ref)) + up(r1_ref)) + up(r2_ref)

    def blk(cs, first):
        _, rh, cols = cs.shape
        return pl.BlockSpec((None, rh // 2, cols), lambda i, pa: (first(pa), i, 0))

    in_specs = [blk(cs, lambda pa: pa[0]) for cs in css]
    for cs in css:
        in_specs += [blk(cs, lambda pa, d=d: d) for d in range(3)]
    return pl.pallas_call(
        body, name=name, out_shape=[jax.ShapeDtypeStruct((2,) + cs.shape[1:], F32) for cs in css],
        grid_spec=pltpu.PrefetchScalarGridSpec(
            num_scalar_prefetch=1, grid=(2,), in_specs=in_specs, out_specs=[blk(cs, lambda pa: pa[1]) for cs in css]),
        compiler_params=_cp(("parallel",)),
    )(place_arr, *css, *[r3 for r3 in r3s for _ in range(3)])


def _allreduce_small(pack):
    rows = pack.shape[0]
    hr = rows // 2

    def body(p_ref, o_ref, sib, slots, s1, r1, s2, r2, s3, r3):
        x, y, c, chip = _place()
        sibling = (x, y, 1 - c)
        ex = pltpu.make_async_remote_copy(src_ref=p_ref, dst_ref=sib, send_sem=s1, recv_sem=r1,
                                          device_id=sibling, device_id_type=MESH)
        ex.start()
        ex.wait()
        half = pl.ds(pl.multiple_of(c * hr, 16), hr)
        slots[0] = (p_ref[half, :] + sib[half, :]).astype(BF16)
        cps = []
        for d in (1, 2, 3):
            px, py, _ = _chip_at(x, y, d)
            cps.append(pltpu.make_async_remote_copy(
                src_ref=slots.at[0], dst_ref=slots.at[d], send_sem=s2.at[d - 1], recv_sem=r2.at[d - 1],
                device_id=(px, py, c), device_id_type=MESH))
        for cp in cps:
            cp.start()
        for cp in cps:
            cp.wait()
        tot = slots[chip].astype(F32)
        for k in (1, 2, 3):
            tot = tot + slots[jnp.bitwise_xor(chip, k)].astype(F32)
        o_ref[half, :] = tot
        back = pltpu.make_async_remote_copy(src_ref=o_ref.at[half, :], dst_ref=o_ref.at[half, :], send_sem=s3, recv_sem=r3,
                                            device_id=sibling, device_id_type=MESH)
        back.start()
        back.wait()

    vm = pl.BlockSpec(memory_space=pltpu.VMEM)
    return pl.pallas_call(
        body, name="allreduce_small", out_shape=jax.ShapeDtypeStruct((rows, 128), F32),
        in_specs=[vm], out_specs=vm,
        scratch_shapes=[pltpu.VMEM((rows, 128), F32), pltpu.VMEM((4, hr, 128), BF16),
                        pltpu.SemaphoreType.DMA, pltpu.SemaphoreType.DMA, pltpu.SemaphoreType.DMA((3,)), pltpu.SemaphoreType.DMA((3,)),
                        pltpu.SemaphoreType.DMA, pltpu.SemaphoreType.DMA],
        compiler_params=_cp(has_side_effects=True),
    )(pack)


def _small_chip_sum(pack, after=()):
    rows = pack.shape[0]
    hr = rows // 2

    def body(p_ref, *rest):
        o_ref, sib, s1, r1 = rest[-4:]
        x, y, c, _ = _place()
        ex = pltpu.make_async_remote_copy(src_ref=p_ref, dst_ref=sib, send_sem=s1, recv_sem=r1,
                                          device_id=(x, y, 1 - c), device_id_type=MESH)
        ex.start()
        ex.wait()
        half = pl.ds(pl.multiple_of(c * hr, 16), hr)
        o_ref[...] = (p_ref[half, :] + sib[half, :]).astype(BF16)

    vm = pl.BlockSpec(memory_space=pltpu.VMEM)
    return pl.pallas_call(
        body, name="small_chip_sum", out_shape=jax.ShapeDtypeStruct((hr, 128), BF16),
        in_specs=[vm] + [ANY] * len(after), out_specs=vm,
        scratch_shapes=[pltpu.VMEM((rows, 128), F32), pltpu.SemaphoreType.DMA, pltpu.SemaphoreType.DMA],
        compiler_params=_cp(has_side_effects=True),
    )(pack, *after)


def _small_copies(c_refs, land_refs, send, recv):
    x, y, c, _ = _place()
    cps = []
    for d in (1, 2, 3):
        px, py, _ = _chip_at(x, y, d)
        cps.append(pltpu.make_async_remote_copy(src_ref=c_refs[0], dst_ref=land_refs[0].at[d - 1], send_sem=send.at[d - 1],
                                                recv_sem=recv.at[d - 1], device_id=(px, py, c), device_id_type=MESH))
    return cps


def _small_total(csum, land):
    hr = csum.shape[0]

    def body(c_ref, l_ref, o_ref, slots, s3, r3):
        x, y, c, chip = _place()
        slots[0] = c_ref[...]
        for d in (1, 2, 3):
            slots[d] = l_ref[d - 1]
        tot = slots[chip].astype(F32)
        for k in (1, 2, 3):
            tot = tot + slots[jnp.bitwise_xor(chip, k)].astype(F32)
        half = pl.ds(pl.multiple_of(c * hr, 16), hr)
        o_ref[half, :] = tot
        back = pltpu.make_async_remote_copy(src_ref=o_ref.at[half, :], dst_ref=o_ref.at[half, :], send_sem=s3, recv_sem=r3,
                                            device_id=(x, y, 1 - c), device_id_type=MESH)
        back.start()
        back.wait()

    vm = pl.BlockSpec(memory_space=pltpu.VMEM)
    return pl.pallas_call(
        body, name="small_total", out_shape=jax.ShapeDtypeStruct((2 * hr, 128), F32), in_specs=[vm, vm], out_specs=vm,
        scratch_shapes=[pltpu.VMEM((4, hr, 128), BF16), pltpu.SemaphoreType.DMA, pltpu.SemaphoreType.DMA],
        compiler_params=_cp(has_side_effects=True),
    )(csum, land)


def _adam_math(gv, wv, mv, vv):
    m2 = ADAM_B1 * mv + (1.0 - ADAM_B1) * gv
    v2 = ADAM_B2 * vv + (1.0 - ADAM_B2) * (gv * gv)
    m_hat = m2 / (1.0 - ADAM_B1 ** ADAM_STEP)
    v_hat = v2 / (1.0 - ADAM_B2 ** ADAM_STEP)
    return -ADAM_LR * (m_hat / (jnp.sqrt(v_hat) + ADAM_EPS) + ADAM_WD * wv), m2, v2


def _adam(g, w, m, v, name):
    rows, cols = g.shape
    rb = rows // 4

    def body(g_ref, w_ref, m_ref, v_ref, d_ref, m2_ref, v2_ref):
        d_ref[...], m2_ref[...], v2_ref[...] = _adam_math(g_ref[...], w_ref[...], m_ref[...], v_ref[...])

    blk = pl.BlockSpec((rb, cols), lambda i: (i, 0))
    shp = jax.ShapeDtypeStruct((rows, cols), F32)
    return pl.pallas_call(
        body, name=name, grid=(4,), in_specs=[blk] * 4, out_specs=[blk] * 3, out_shape=[shp] * 3,
        compiler_params=_cp(("parallel",)),
    )(g, w, m, v)


def _adam_layer(gs, ws, ms, vs, l, prevs, name):
    n = len(gs)
    prev = [a for p4 in prevs if p4 is not None for a in p4]

    def body(*refs):
        outs = refs[len(refs) - 4 * n:]
        for t in range(n):
            g_ref, w_ref, m_ref, v_ref = refs[4 * t:4 * t + 4]
            go_ref, d_ref, m2_ref, v2_ref = outs[4 * t:4 * t + 4]
            gv = g_ref[...]
            go_ref[...] = gv
            d_ref[...], m2_ref[...], v2_ref[...] = _adam_math(gv, w_ref[...], m_ref[...], v_ref[...])

    in_specs, out_specs, out_shape, operands, aliases = [], [], [], [], {}
    for t, g in enumerate(gs):
        rows, cols = g.shape
        lay = pl.BlockSpec((None, rows // 4, cols), lambda i: (l, i, 0))
        in_specs += [pl.BlockSpec((rows // 4, cols), lambda i: (i, 0)), lay, lay, lay]
        operands += [g, ws[t], ms[t], vs[t]]
        out_specs += [lay] * 4
        out_shape += [jax.ShapeDtypeStruct((2, rows, cols), F32)] * 4
    k = 4 * n
    for t, p4 in enumerate(prevs):
        if p4 is not None:
            for j in range(4):
                aliases[k] = 4 * t + j
                k += 1
    outs = pl.pallas_call(
        body, name=name, grid=(4,), in_specs=in_specs + [ANY] * len(prev), out_specs=out_specs, out_shape=out_shape,
        input_output_aliases=aliases, compiler_params=_cp(("parallel",)),
    )(*operands, *prev)
    return [list(outs[4 * t:4 * t + 4]) for t in range(n)]


def _rows128(a):
    return a.reshape(-1, 128)


def _pack(arrs, mult):
    parts = [_rows128(a) for a in arrs]
    rows = sum(q.shape[0] for q in parts)
    pad = -rows % mult
    if pad:
        parts.append(jnp.zeros((pad, 128), F32))
    return jnp.concatenate(parts, axis=0)


def _unpack(pack, shapes):
    out, o = [], 0
    for s in shapes:
        n = 1
        for e in s:
            n *= e
        out.append(pack[o:o + n // 128].reshape(s))
        o += n // 128
    return out


WEIGHTS = ['norm1_g', 'w_in', 'gmlp_ln_g', 'gmlp_ln_b', 'gmlp_w_s', 'gmlp_b_s', 'conv_w', 'conv_b', 'lru_w_r', 'lru_b_r', 'lru_w_i',
           'lru_b_i', 'lru_lambda', 'w_out', 'norm2_g', 'w_ffn_in', 'w_ffn_out', 'final_g']
BIG = ['w_in', 'w_out', 'w_ffn_in', 'w_ffn_out']
SMALL = [n for n in WEIGHTS if n not in BIG]
CHIP_SHARDED_SMALL = ['conv_w', 'lru_b_r', 'lru_b_i', 'lru_lambda']


def kernel(x, norm1_g, w_in, gmlp_ln_g, gmlp_ln_b, gmlp_w_s, gmlp_b_s, conv_w, conv_b, lru_w_r, lru_b_r, lru_w_i, lru_b_i, lru_lambda, w_out, norm2_g, w_ffn_in, w_ffn_out, final_g, loss_target, m_norm1_g, m_w_in, m_gmlp_ln_g, m_gmlp_ln_b, m_gmlp_w_s, m_gmlp_b_s, m_conv_w, m_conv_b, m_lru_w_r, m_lru_b_r, m_lru_w_i, m_lru_b_i, m_lru_lambda, m_w_out, m_norm2_g, m_w_ffn_in, m_w_ffn_out, m_final_g, v_norm1_g, v_w_in, v_gmlp_ln_g, v_gmlp_ln_b, v_gmlp_w_s, v_gmlp_b_s, v_conv_w, v_conv_b, v_lru_w_r, v_lru_b_r, v_lru_w_i, v_lru_b_i, v_lru_lambda, v_w_out, v_norm2_g, v_w_ffn_in, v_w_ffn_out, v_final_g):
    a = dict(locals())
    w = {n: a[n] for n in WEIGHTS}
    mom = {n: a["m_" + n] for n in WEIGHTS}
    var = {n: a["v_" + n] for n in WEIGHTS}
    _, _, c, chip = _place()
    c_arr, chip_arr = jnp.reshape(c, (1,)).astype(jnp.int32), jnp.reshape(chip, (1,)).astype(jnp.int32)
    place_arr = jnp.stack([chip, c]).astype(jnp.int32)

    first, rest = BIG[:1], BIG[1:]

    def as_weights(names, full):
        wb = {n: f.reshape(4, 2 * f.shape[2], f.shape[3]) for n, f in zip(names, full)}
        if "w_out" in wb:
            wb["w_out"] = wb["w_out"].reshape(D, D)
            wb["w_ffn_out"] = wb["w_ffn_out"].reshape(DFF, D)
        return wb

    def cast(names, l, tag):
        return _cast_into([w[n] for n in names], l, chip_arr, f"cast_{tag}")

    def landed(fly, names, after, tag):
        return as_weights(names, _gather_pass_on(_gather_wait(fly[0], fly[1], fly[2], after, tag), tag))

    tiny = _pack([w[n] for n in CHIP_SHARDED_SMALL], 8)
    _, tiny_full = _gather_weights([], tiny)
    fly_in = _gather_start(cast(first, 0, "in"), "in", after=(tiny_full,))
    fly0 = _gather_start(cast(rest, 0, "0"), "0", after=(fly_in[3],))
    fly1 = _gather_start(cast(BIG, 1, "1"), "1", after=(fly0[3],))
    p = {n: w[n] for n in SMALL}
    parts = [_unpack(tiny_full[k], [w[n].shape for n in CHIP_SHARDED_SMALL]) for k in range(4)]
    for i, n in enumerate(CHIP_SHARDED_SMALL):
        p[n] = jnp.concatenate([parts[k][i] for k in range(4)], axis=-1)

    operands = [_layer_operands(l, p) for l in range(2)]
    state_packs = [_pack([src[n] for n in SMALL], 32) for src in (w, mom, var)]
    ahead = tuple(jax.tree.leaves(operands)) + tuple(state_packs)

    passing = {}

    def pass_on_1(gu):
        bufs = _gather_wait(fly1[0], fly1[1], fly1[2], gu, "1")
        passing[1] = _exchange_start(bufs, [], _pass_on_copies, 3 * len(bufs), "gather_pass_on_start_1")
        return (passing[1][-1],)

    xa, saved0 = _forward_layer(0, x[0], p, landed(fly_in, first, (fly1[3],) + ahead, "in"), after=(fly0[3], fly1[3]),
                                rest=lambda merged: landed(fly0, rest, merged, "0"), near_end=pass_on_1, operands=operands[0])
    send, recv, bufs1, _, _ = passing[1]
    xb, saved1 = _forward_layer(
        1, xa, p, as_weights(BIG, _exchange_wait(send, recv, bufs1, [], xa, _pass_on_copies, "gather_pass_on_wait_1")[0]),
        operands=operands[1], loss=(loss_target[0], p["final_g"][None]))
    dxb, loss_v, dfg = xb
    loss = lax.psum(loss_v[0, 0], ("x", "y", "c"))

    out, flying = {}, {}

    def halves(grads):
        return [g.reshape(4, 2, -1, g.shape[-1]) for g in grads]

    def sibling_start(grads, names, l, tag):
        gs = halves(grads)
        lands = [lax.empty((4,) + g.shape[2:], g.dtype) for g in gs]
        flying["s" + tag] = (names, l) + tuple(
            _exchange_start(gs, lands, _sibling_copies, 4 * len(gs), f"grads_to_sibling_start_{tag}"))
        return (flying["s" + tag][-1],)

    def chips_start(gs, from_sib, names, l, tag):
        cs = _add_half(gs, from_sib, c_arr, f"add_half_{tag}")
        lands = [lax.empty((3,) + a.shape[1:], a.dtype) for a in cs]
        flying[tag] = (names, l) + tuple(_exchange_start(cs, lands, _chips_copies, 3 * len(cs), f"grads_to_chips_start_{tag}"))
        return (flying[tag][-1],)

    def sibling_finish(tag, after):
        names, l, send, recv, gs, lands, _ = flying["s" + tag]
        gs, from_sib = _exchange_wait(send, recv, gs, lands, after, _sibling_copies, f"grads_to_sibling_wait_{tag}")
        return chips_start(gs, from_sib, names, l, tag)

    def reduce_sums(tags, after):
        groups, ts = [], []
        for tag in tags:
            names, l, send, recv, cs, lands, _ = flying[tag]
            cs, lands = _exchange_wait(send, recv, cs, lands, after, _chips_copies, f"grads_to_chips_wait_{tag}")
            ts += _sum_chips(cs, lands, place_arr, f"sum_chips_{tag}")
            groups.append((tag, names))
        flying["j" + tags[0]] = (groups, l) + tuple(_exchange_start(ts, [], _join_copies, len(ts), f"grads_join_start_{tags[0]}"))
        return (flying["j" + tags[0]][-1],)

    def reduce_adam(tag0, after):
        groups, l, send, recv, ts, _, _ = flying["j" + tag0]
        joined = _exchange_wait(send, recv, ts, [], after, _join_copies, f"grads_join_wait_{tag0}")[0]
        for tag, names in groups:
            gs, joined = [j.reshape(w[n].shape[1:]) for n, j in zip(names, joined)], joined[len(names):]
            res = _adam_layer(gs, [w[n] for n in names], [mom[n] for n in names], [var[n] for n in names], l,
                              [out.get(n) for n in names], f"adam_{tag}")
            out.update(zip(names, res))

    def late1(grads):
        return sibling_finish("1a", grads[0]) + sibling_start(grads, first, 1, "1b")

    def midway0(grads):
        return reduce_sums(("1a", "1b"), grads[0]) + sibling_start(grads, rest, 0, "0a")

    def stacked_small(small0):
        small = {k: jnp.stack([small0[k], small1[k]]) for k in LAYER_SMALL}
        return dict(small, final_g=dfg[0])

    def midway3_0(small0):
        small = stacked_small(dict(small0, norm1_g=jnp.zeros((D,), F32)))
        csum = _small_chip_sum(_pack([small[n] for n in SMALL], 32))
        flying["small"] = _exchange_start([csum], [lax.empty((3,) + csum.shape, BF16)], _small_copies, 3, "small_to_chips_start")
        return (flying["small"][-1],)

    def late0(grads):
        tok = sibling_start(grads, first, 0, "0b")
        reduce_adam("1a", tok[0])
        return sibling_finish("0b", out[first[0]][0])

    dxa, big1, small1 = _backward_layer(1, dxb, saved1, midway=lambda grads: sibling_start(grads, rest, 1, "1a"), late=late1)
    dx, big0, small0 = _backward_layer(0, dxa, saved0, after=sibling_finish("1b", dxa), midway=midway0,
                                       midway2=lambda dws: sibling_finish("0a", dws), midway3=midway3_0, late=late0)
    join_tok = reduce_sums(("0a", "0b"), dx)
    small = stacked_small(small0)

    full_shapes = [small[n].shape for n in SMALL]
    send, recv, csum, land, _ = flying["small"]
    csum, land = _exchange_wait(send, recv, csum, land, join_tok[0], _small_copies, "small_to_chips_wait")
    red = _unpack(_small_total(csum[0], land[0]), full_shapes)
    norm1_0 = _allreduce_small(_pack([small0["norm1_g"]], 32))[:D // 128].reshape(D)
    reduce_adam("0a", norm1_0)
    red[SMALL.index("norm1_g")] = red[SMALL.index("norm1_g")].at[0].set(norm1_0)
    g_small = []
    for n, g in zip(SMALL, red):
        if n in CHIP_SHARDED_SMALL:
            g = lax.dynamic_slice_in_dim(g, chip * w[n].shape[-1], w[n].shape[-1], axis=g.ndim - 1)
        g_small.append(g)
    shapes = [w[n].shape for n in SMALL]
    upd = [_unpack(u, shapes) for u in _adam(_pack(g_small, 32), *state_packs, "adam_small")]
    for i, n in enumerate(SMALL):
        out[n] = [g_small[i], upd[0][i], upd[1][i], upd[2][i]]

    return (loss, dx[None]) + tuple(out[n][i] for i in range(4) for n in WEIGHTS)
```

```python
import functools

import jax
import jax.numpy as jnp
from jax import lax
from jax.experimental import pallas as pl
from jax.experimental.pallas import tpu as pltpu

F32 = jnp.float32
BF16 = jnp.bfloat16
MESH = pl.DeviceIdType.MESH

D = 1024
NH = 8
HD = 128
CHUNK = 128
GMLP_ROWS = 512
N_IN_T = 12
DFF = 2816
DFF_SH = 1408
EPS = 1e-6
LRU_C = 8.0
ADAM_LR, ADAM_B1, ADAM_B2, ADAM_EPS, ADAM_WD, ADAM_STEP = 0.001, 0.9, 0.999, 1e-08, 0.01, 10

TM = 512
TM_BIG = 1024
RT = 128
PADR = 8
VMEM_LIMIT = 56 * 1024 * 1024


def _cp(sem=None, **kw):
    if sem is not None:
        kw["dimension_semantics"] = sem
    return pltpu.CompilerParams(vmem_limit_bytes=VMEM_LIMIT, **kw)


_GC = 0.7978845608028654


def _sigmoid(x):
    return 0.5 * jnp.tanh(0.5 * x) + 0.5


_GK = 0.044715


def _gelu(x):
    t = jnp.tanh(x * (_GC + (_GC * _GK) * (x * x)))
    return x * (0.5 + 0.5 * t)


def _gelu_and_grad(x):
    x2 = x * x
    t = jnp.tanh(x * (_GC + (_GC * _GK) * x2))
    h = 0.5 + 0.5 * t
    return x * h, h + x * (1.0 - t * t) * (0.5 * _GC + (1.5 * _GC * _GK) * x2)


def _softplus_neg(lam):
    y = jnp.exp(-jnp.abs(lam))
    u = 1.0 + y
    l1p = jnp.where(u == 1.0, y, jnp.log(u) * y / (u - 1.0))
    return jnp.maximum(-lam, 0.0) + l1p


def _dot(a, b):
    return jnp.dot(a, b, preferred_element_type=F32)


def _dot_nt(a, b):
    return lax.dot_general(a, b, (((1,), (1,)), ((), ())), preferred_element_type=F32)


def _dot_tn(a, b):
    return lax.dot_general(a, b, (((0,), (0,)), ((), ())), preferred_element_type=F32)


def _rms_hat(x):
    r = lax.rsqrt(jnp.mean(x * x, axis=-1, keepdims=True) + EPS)
    return x * r, r


def _rms_bwd(dh, x, g):
    xh, r = _rms_hat(x)
    dxh = dh * g
    dx = r * (dxh - xh * jnp.mean(dxh * xh, axis=-1, keepdims=True))
    return dx, jnp.sum(dh * xh, axis=0, keepdims=True)


def _norm_into(x_ref, g_ref, h_ref):
    xh, _ = _rms_hat(x_ref[...])
    h_ref[...] = (xh * g_ref[...]).astype(BF16)


def _in_tile(j):
    m, hf = j // 2, j % 2
    orig = jnp.where(m < 2, m, jnp.where(m == 2, 4, jnp.where(m < 5, m - 1, 5)))
    t = orig * 2 + hf
    return t // 3, t % 3


ANY = pl.BlockSpec(memory_space=pl.ANY)


def _mm_in(x, g, w_in, l, after=()):
    S = x.shape[0]
    tm = min(2 * TM_BIG, S)

    def body(x_ref, g_ref, w0_ref, w1_ref, *rest):
        o_ref, h_ref = rest[-2:]

        @pl.when(pl.program_id(1) == 0)
        def _():
            _norm_into(x_ref, g_ref, h_ref)
        rp = min(TM, tm)
        for r0 in range(0, tm, rp):
            hv = h_ref[r0:r0 + rp, :]
            o_ref[r0:r0 + rp, 0:512] = _dot(hv, w0_ref[...]).astype(BF16)
            o_ref[r0:r0 + rp, 512:1024] = _dot(hv, w1_ref[...]).astype(BF16)

    def w_tile(hf):
        def w_map(i, m):
            sh, tl = _in_tile(2 * m + hf)
            return (sh, 0, tl)
        return pl.BlockSpec((None, D, 512), w_map)

    return pl.pallas_call(
        body, name=f"mm_in_{l}", grid=(S // tm, 6),
        in_specs=[pl.BlockSpec((tm, D), lambda i, m: (i, 0)), pl.BlockSpec((1, D), lambda i, m: (0, 0)),
                  w_tile(0), w_tile(1)] + [ANY] * len(after),
        out_specs=[pl.BlockSpec((None, tm, D), lambda i, m: (m, i, 0)), pl.BlockSpec((tm, D), lambda i, m: (i, 0))],
        out_shape=[jax.ShapeDtypeStruct((6, S, D), BF16), jax.ShapeDtypeStruct((S, D), BF16)],
        compiler_params=_cp(("parallel", "arbitrary")),
    )(x, g, w_in, w_in, *after)


def _mm_res(a, w, res, l, name, after=()):
    S, K = a.shape

    tm = TM

    def body(a_ref, w_ref, r_ref, *rest):
        rest[-1][...] = r_ref[...] + _dot(a_ref[...], w_ref[...])

    return pl.pallas_call(
        body, name=f"{name}_{l}", grid=(S // tm,),
        in_specs=[pl.BlockSpec((tm, K), lambda i: (i, 0)), pl.BlockSpec((K, D), lambda i: (0, 0)),
                  pl.BlockSpec((tm, D), lambda i: (i, 0))] + [ANY] * len(after),
        out_specs=pl.BlockSpec((tm, D), lambda i: (i, 0)),
        out_shape=jax.ShapeDtypeStruct((S, D), F32),
        compiler_params=_cp(("parallel",)),
    )(a, w, res, *after)


def _mm_ffn_in(x, g, w_fi, l):
    S = x.shape[0]

    tm = min(TM_BIG, S)

    def body(x_ref, g_ref, w_ref, gu_ref, ff_ref, h_ref):
        @pl.when(pl.program_id(1) == 0)
        def _():
            _norm_into(x_ref, g_ref, h_ref)
        for r0 in range(0, tm, TM):
            rows = slice(r0, r0 + TM)
            hv = h_ref[rows, :]
            ga = _dot(hv, w_ref[0])
            gb = _dot(hv, w_ref[1])
            sg = _sigmoid(ga)
            silu = ga * sg
            gu_ref[0, rows, :] = (gb * (sg + silu * (1.0 - sg))).astype(BF16)
            gu_ref[1, rows, :] = silu.astype(BF16)
            ff_ref[rows, :] = (silu * gb).astype(BF16)

    gu, ff, h = pl.pallas_call(
        body, name=f"mm_ffn_in_{l}", grid=(S // tm, 2),
        in_specs=[pl.BlockSpec((tm, D), lambda i, s: (i, 0)), pl.BlockSpec((1, D), lambda i, s: (0, 0)),
                  pl.BlockSpec((2, None, D, DFF_SH), lambda i, s: (0, s, 0, 0))],
        out_specs=[pl.BlockSpec((2, None, tm, DFF_SH), lambda i, s: (0, s, i, 0)),
                   pl.BlockSpec((tm, DFF_SH), lambda i, s: (i, s)),
                   pl.BlockSpec((tm, D), lambda i, s: (i, 0))],
        out_shape=[jax.ShapeDtypeStruct((2, 2, S, DFF_SH), BF16), jax.ShapeDtypeStruct((S, DFF), BF16),
                   jax.ShapeDtypeStruct((S, D), BF16)],
        compiler_params=_cp(("parallel", "arbitrary")),
    )(x, g, w_fi.reshape(2, 2, D, DFF_SH))
    return gu.reshape(4, S, DFF_SH), ff, h


def _gmlp_fwd(z6, ws_b, bs_b, lg, lb):
    S = z6.shape[1]

    ts = min(GMLP_ROWS, S)

    def body(z_ref, ws_ref, bs_ref, lg_ref, lb_ref, o_ref, mix):
        for r0 in range(0, ts, CHUNK):
            rows = slice(r0, r0 + CHUNK)
            gv = _gelu(z_ref[1, rows, :].astype(F32))
            xc = gv - jnp.mean(gv, axis=-1, keepdims=True)
            rs = lax.rsqrt(jnp.mean(xc * xc, axis=-1, keepdims=True) + EPS)
            vb = (xc * rs * lg_ref[...] + lb_ref[...]).astype(BF16)
            for gi in range(NH):
                cs = slice(gi * HD, (gi + 1) * HD)
                mix[rows, cs] = _dot(ws_ref[gi], vb[:, cs])
            o_ref[rows, :] = (_sigmoid(z_ref[2, rows, :].astype(F32)) * _gelu(z_ref[0, rows, :].astype(F32))
                              * (mix[rows, :] + bs_ref[...])).astype(BF16)

    return pl.pallas_call(
        body, name="gmlp_fwd", grid=(S // ts,),
        in_specs=[pl.BlockSpec((3, ts, D), lambda i: (0, i, 0)), pl.BlockSpec((NH, CHUNK, CHUNK), lambda i: (0, 0, 0)),
                  pl.BlockSpec((CHUNK, D), lambda i: (0, 0)), pl.BlockSpec((1, D), lambda i: (0, 0)),
                  pl.BlockSpec((1, D), lambda i: (0, 0))],
        out_specs=pl.BlockSpec((ts, D), lambda i: (i, 0)),
        out_shape=jax.ShapeDtypeStruct((S, D), BF16),
        scratch_shapes=[pltpu.VMEM((ts, D), F32)],
        compiler_params=_cp(("parallel",)),
    )(z6, ws_b, bs_b, lg, lb)


def _row_iota():
    return lax.broadcasted_iota(jnp.int32, (RT, HD), 0)


SUB = 8
UNROLL = 8
GRAD_ROWS = 512


def _scan_up(a, b, carry):
    row = lax.broadcasted_iota(jnp.int32, (SUB, HD), 0)
    masks = [(d, row >= d) for d in (1, 2, 4)]
    c = jnp.broadcast_to(carry, (SUB, HD))
    hs = []
    for j in range(RT // SUB):
        aj, bj = a[SUB * j:SUB * (j + 1)], b[SUB * j:SUB * (j + 1)]
        for d, m in masks:
            bj = bj + aj * jnp.where(m, pltpu.roll(bj, d, 0), 0.0)
            aj = aj * jnp.where(m, pltpu.roll(aj, d, 0), 1.0)
        h = bj + aj * c
        hs.append(h)
        c = jnp.broadcast_to(h[SUB - 1:SUB, :], (SUB, HD))
    return jnp.concatenate(hs, axis=0), hs[-1][SUB - 1:SUB, :]


def _scan_down(a, b, carry):
    row = lax.broadcasted_iota(jnp.int32, (SUB, HD), 0)
    masks = [(d, row < SUB - d) for d in (1, 2, 4)]
    c = jnp.broadcast_to(carry, (SUB, HD))
    hs = []
    for j in reversed(range(RT // SUB)):
        aj, bj = a[SUB * j:SUB * (j + 1)], b[SUB * j:SUB * (j + 1)]
        for d, m in masks:
            bj = bj + aj * jnp.where(m, pltpu.roll(bj, SUB - d, 0), 0.0)
            aj = aj * jnp.where(m, pltpu.roll(aj, SUB - d, 0), 1.0)
        h = bj + aj * c
        hs.append(h)
        c = jnp.broadcast_to(h[0:1, :], (SUB, HD))
    return jnp.concatenate(hs[::-1], axis=0), hs[-1][0:1, :]


def _decay(r, sp_d):
    log_a = -LRU_C * r * sp_d
    a = jnp.exp(log_a)
    return a, jnp.sqrt(jnp.maximum(-jnp.tanh(log_a) * (a * a + 1.0), 0.0))


def _decay_bwd(r, sp_d):
    log_a = -LRU_C * r * sp_d
    a = jnp.exp(log_a)
    m2 = jnp.maximum(-jnp.tanh(log_a) * (a * a + 1.0), 0.0)
    inv = lax.rsqrt(m2)
    return a, m2 * inv, inv


def _lru_gates(xc, d, wr_ref, br_ref, wi_ref, bi_ref, sp):
    xb = xc.astype(BF16)
    r = _sigmoid(_dot(xb, wr_ref[d]) + br_ref[d:d + 1, :])
    i = _sigmoid(_dot(xb, wi_ref[d]) + bi_ref[d:d + 1, :])
    a, mult = _decay(r, sp[d:d + 1, :])
    return r, i, a, mult


def _shifted(win, k):
    w = RT + 2 * PADR
    v = win if k == 0 else pltpu.roll(win, (-k) % w, 0)
    return v[PADR:PADR + RT]


def _conv_taps(win):
    return [_shifted(win, k) for k in (-1, 0, 1, 2)]


def _fill_padded(dst, src_ref, S):
    zeros = jnp.zeros((PADR, HD), F32)
    dst[0:PADR, :] = zeros
    dst[PADR + S:2 * PADR + S, :] = zeros

    def cp(i, c):
        t0 = pl.multiple_of(i * RT, RT)
        dst[pl.ds(t0 + PADR, RT), :] = src_ref[pl.ds(t0, RT), :].astype(F32)
        return c
    lax.fori_loop(0, S // RT, cp, 0)


def _conv_fwd_all(zxp, xc_s, cw_ref, cb_ref, S):
    def cv(i, c):
        t0 = pl.multiple_of(i * RT, RT)
        xm1, x0, xp1, xp2 = _conv_taps(zxp[pl.ds(t0, RT + 2 * PADR), :])
        xc_s[pl.ds(t0, RT), :] = (cb_ref[...] + xm1 * cw_ref[0:1, :] + x0 * cw_ref[1:2, :]
                                  + xp1 * cw_ref[2:3, :] + xp2 * cw_ref[3:4, :])
        return c
    lax.fori_loop(0, S // RT, cv, 0)


def _lru_specs(S):
    head = lambda h: (0, h)
    return [pl.BlockSpec((4, HD), head), pl.BlockSpec((1, HD), head),
            pl.BlockSpec((2, None, HD, HD), lambda h: (0, h, 0, 0)), pl.BlockSpec((2, HD), head),
            pl.BlockSpec((2, None, HD, HD), lambda h: (0, h, 0, 0)), pl.BlockSpec((2, HD), head),
            pl.BlockSpec((2, HD), head)]


def _lru_fwd(z6, ya, cw, cb, wr, br, wi, bi, lam):
    S = z6.shape[1]
    nt = S // RT

    def body(z_ref, ya_ref, cw_ref, cb_ref, wr_ref, br_ref, wi_ref, bi_ref, lam_ref, mg_ref, h0_ref, h1_ref, zxp, xc_s):
        sp = _softplus_neg(lam_ref[...])
        _fill_padded(zxp, z_ref.at[0], S)
        _conv_fwd_all(zxp, xc_s, cw_ref, cb_ref, S)

        def scans(i, carry):
            cu, cd = carry
            for u in range(UNROLL):
                j = i * UNROLL + u
                ru = pl.ds(pl.multiple_of(j * RT, RT), RT)
                rd = pl.ds(pl.multiple_of((nt - 1 - j) * RT, RT), RT)
                xu, xd = xc_s[ru, :], xc_s[rd, :]
                _, gi, a, mult = _lru_gates(xu, 0, wr_ref, br_ref, wi_ref, bi_ref, sp)
                hu, cu = _scan_up(a, mult * gi * xu, cu)
                h0_ref[ru, :] = hu
                _, gi, a, mult = _lru_gates(xd, 1, wr_ref, br_ref, wi_ref, bi_ref, sp)
                hd, cd = _scan_down(a, mult * gi * xd, cd)
                h1_ref[rd, :] = hd
            return cu, cd
        z1 = jnp.zeros((1, HD), F32)
        lax.fori_loop(0, nt // UNROLL, scans, (z1, z1))

        def merge(i, c):
            rows = pl.ds(pl.multiple_of(i * RT, RT), RT)
            yb = (h0_ref[rows, :] + h1_ref[rows, :]) * _gelu(z_ref[1, rows, :].astype(F32))
            mg_ref[rows, :] = (ya_ref[rows, :].astype(F32) + _sigmoid(z_ref[2, rows, :].astype(F32)) * yb).astype(BF16)
            return c
        lax.fori_loop(0, nt, merge, 0)

    col = pl.BlockSpec((S, HD), lambda h: (0, h))
    return pl.pallas_call(
        body, name="lru_fwd", grid=(NH,),
        in_specs=[pl.BlockSpec((3, S, HD), lambda h: (1, 0, h)), col] + _lru_specs(S),
        out_specs=[col, col, col],
        out_shape=[jax.ShapeDtypeStruct((S, D), BF16), jax.ShapeDtypeStruct((S, D), F32), jax.ShapeDtypeStruct((S, D), F32)],
        scratch_shapes=[pltpu.VMEM((S + 2 * PADR, HD), F32), pltpu.VMEM((S, HD), F32)],
        compiler_params=_cp(("parallel",)),
    )(z6, ya, cw, cb, wr, br, wi, bi, lam)


def _mm_res_loss(a, w, res, tgt, g):
    S, K = a.shape

    def body(a_ref, w_ref, r_ref, t_ref, g_ref, dx_ref, loss_ref, dg_ref):
        @pl.when(pl.program_id(0) == 0)
        def _():
            loss_ref[...] = jnp.zeros_like(loss_ref)
            dg_ref[...] = jnp.zeros_like(dg_ref)
        xv = r_ref[...] + _dot(a_ref[...], w_ref[...])
        xh, _ = _rms_hat(xv)
        e = xh * g_ref[...] - t_ref[...]
        loss_ref[...] += jnp.sum(e * e) * (0.5 / D)
        dx, dgs = _rms_bwd(e * (1.0 / D), xv, g_ref[...])
        dx_ref[...] = dx
        dg_ref[...] += dgs

    row = pl.BlockSpec((TM, D), lambda i: (i, 0))
    vec = pl.BlockSpec((1, D), lambda i: (0, 0))
    return pl.pallas_call(
        body, name="mm_ffn_out_loss", grid=(S // TM,),
        in_specs=[pl.BlockSpec((TM, K), lambda i: (i, 0)), pl.BlockSpec((K, D), lambda i: (0, 0)), row, row, vec],
        out_specs=[row, pl.BlockSpec((1, 128), lambda i: (0, 0)), vec],
        out_shape=[jax.ShapeDtypeStruct((S, D), F32), jax.ShapeDtypeStruct((1, 128), F32), jax.ShapeDtypeStruct((1, D), F32)],
        compiler_params=_cp(("arbitrary",)),
    )(a, w, res, tgt, g)


def _bwd_ffn_out(dx, w_fo, gu, l, after=()):
    S = dx.shape[0]

    tm = min(TM_BIG, S)

    def body(dx_ref, w_ref, gu_ref, *rest):
        o_ref = rest[-1]
        for r0 in range(0, tm, TM):
            rows = slice(r0, r0 + TM)
            d = _dot_nt(dx_ref[rows, :].astype(BF16), w_ref[...])
            o_ref[0, rows, :] = (d * gu_ref[0, rows, :].astype(F32)).astype(BF16)
            o_ref[1, rows, :] = (d * gu_ref[1, rows, :].astype(F32)).astype(BF16)

    pair = pl.BlockSpec((2, None, tm, DFF_SH), lambda i, s: (0, s, i, 0))
    dgu = pl.pallas_call(
        body, name=f"bwd_ffn_out_{l}", grid=(S // tm, 2),
        in_specs=[pl.BlockSpec((tm, D), lambda i, s: (i, 0)), pl.BlockSpec((DFF_SH, D), lambda i, s: (s, 0)), pair]
        + [ANY] * len(after),
        out_specs=pair,
        out_shape=jax.ShapeDtypeStruct((2, 2, S, DFF_SH), BF16),
        compiler_params=_cp(("parallel", "arbitrary")),
    )(dx, w_fo, gu.reshape(2, 2, S, DFF_SH), *after)
    return dgu.reshape(4, S, DFF_SH)


def _mm_tn(a, b, m_blk, tk, name):
    S, M = a.shape

    def body(a_ref, b_ref, o_ref):
        @pl.when(pl.program_id(1) == 0)
        def _():
            o_ref[...] = jnp.zeros_like(o_ref)
        o_ref[...] += _dot_tn(a_ref[...], b_ref[...].astype(BF16))

    return pl.pallas_call(
        body, name=name, grid=(M // m_blk, S // tk),
        in_specs=[pl.BlockSpec((tk, m_blk), lambda m, k: (k, m)), pl.BlockSpec((tk, D), lambda m, k: (k, 0))],
        out_specs=pl.BlockSpec((m_blk, D), lambda m, k: (m, 0)),
        out_shape=jax.ShapeDtypeStruct((M, D), F32),
        compiler_params=_cp(("parallel", "arbitrary")),
    )(a, b)


def _mm_nt_rms_bwd(a, a_specs, w, w_specs, nk, tm, x, g, dres, name, after=()):
    S = x.shape[0]
    sub = len(a_specs)

    def body(*refs):
        a_refs, w_refs = refs[:sub], refs[sub:2 * sub]
        x_ref, g_ref, r_ref = refs[2 * sub:2 * sub + 3]
        dx_ref, dg_ref, acc = refs[-3:]
        i, k = pl.program_id(0), pl.program_id(1)
        @pl.when(k == 0)
        def _():
            acc[...] = jnp.zeros_like(acc)
        for j in range(sub):
            acc[...] += _dot_nt(a_refs[j][...], w_refs[j][...])

        @pl.when(jnp.logical_and(i == 0, k == 0))
        def _():
            dg_ref[...] = jnp.zeros_like(dg_ref)

        @pl.when(k == nk - 1)
        def _():
            dx, dgs = _rms_bwd(acc[...], x_ref[...], g_ref[...])
            dx_ref[...] = r_ref[...] + dx
            dg_ref[...] += dgs

    row = pl.BlockSpec((tm, D), lambda i, k: (i, 0))
    vec = pl.BlockSpec((1, D), lambda i, k: (0, 0))
    return pl.pallas_call(
        body, name=name, grid=(S // tm, nk),
        in_specs=list(a_specs) + list(w_specs) + [row, vec, row] + [ANY] * len(after),
        out_specs=[row, vec],
        out_shape=[jax.ShapeDtypeStruct((S, D), F32), jax.ShapeDtypeStruct((1, D), F32)],
        scratch_shapes=[pltpu.VMEM((tm, D), F32)],
        compiler_params=_cp(("arbitrary", "arbitrary")),
    )(*[a] * sub, *[w] * sub, x, g, dres, *after)


def _dw_ffn_in(h, dgu, l):
    S = h.shape[0]

    def body(h_ref, b_ref, o_ref):
        @pl.when(pl.program_id(1) == 0)
        def _():
            o_ref[...] = jnp.zeros_like(o_ref)
        o_ref[...] += _dot_tn(h_ref[...], b_ref[...])

    tk = min(2 * TM_BIG, S)
    return pl.pallas_call(
        body, name=f"dw_ffn_in_{l}", grid=(4, S // tk),
        in_specs=[pl.BlockSpec((tk, D), lambda j, k: (k, 0)), pl.BlockSpec((None, tk, DFF_SH), lambda j, k: (j, k, 0))],
        out_specs=pl.BlockSpec((None, D, DFF_SH), lambda j, k: (j, 0, 0)),
        out_shape=jax.ShapeDtypeStruct((4, D, DFF_SH), F32),
        compiler_params=_cp(("parallel", "arbitrary")),
    )(h, dgu)


_HALF_COMPS = ((0, 1, 3), (4, 2, 5))


def _dw_in(h, dz6, l, after=()):
    S = h.shape[0]

    def body(h_ref, d0_ref, d1_ref, d2_ref, *rest):
        o_ref = rest[-1]

        @pl.when(pl.program_id(1) == 0)
        def _():
            o_ref[...] = jnp.zeros_like(o_ref)
        hv = h_ref[...]
        for q, d_ref in enumerate((d0_ref, d1_ref, d2_ref)):
            for hf in range(2):
                col = 1024 * q + 512 * hf
                o_ref[col // 1536, :, col % 1536:col % 1536 + 512] += _dot_tn(hv, d_ref[:, 512 * hf:512 * (hf + 1)])

    tk = min(TM_BIG, S)

    def comp(q):
        return pl.BlockSpec((None, tk, D), lambda p, k: (jnp.where(p == 0, _HALF_COMPS[0][q], _HALF_COMPS[1][q]), k, 0))

    return pl.pallas_call(
        body, name=f"dw_in_{l}", grid=(2, S // tk),
        in_specs=[pl.BlockSpec((tk, D), lambda p, k: (k, 0)), comp(0), comp(1), comp(2)] + [ANY] * len(after),
        out_specs=pl.BlockSpec((2, D, 1536), lambda p, k: (p, 0, 0)),
        out_shape=jax.ShapeDtypeStruct((4, D, 1536), F32),
        compiler_params=_cp(("parallel", "arbitrary")),
    )(h, dz6, dz6, dz6, *after)


def _bwd_out(dx, w_o, merged, l):
    S = dx.shape[0]

    def body(dx_ref, w_ref, m_ref, dm_ref, dw_ref):
        @pl.when(pl.program_id(0) == 0)
        def _():
            dw_ref[...] = jnp.zeros_like(dw_ref)
        dxb = dx_ref[...].astype(BF16)
        dm_ref[...] = _dot_nt(dxb, w_ref[...]).astype(BF16)
        dw_ref[...] += _dot_tn(m_ref[...], dxb)

    tm = TM
    row = pl.BlockSpec((tm, D), lambda i: (i, 0))
    return pl.pallas_call(
        body, name=f"bwd_out_{l}", grid=(S // tm,),
        in_specs=[row, pl.BlockSpec((D, D), lambda i: (0, 0)), row],
        out_specs=[row, pl.BlockSpec((D, D), lambda i: (0, 0))],
        out_shape=[jax.ShapeDtypeStruct((S, D), BF16), jax.ShapeDtypeStruct((D, D), F32)],
        compiler_params=_cp(("arbitrary",)),
    )(dx, w_o, merged)


def _gmlp_bwd(dm, z6, ws_b, wst_b, bs_b, lg, lb, after=()):
    S = z6.shape[1]
    ts = min(GMLP_ROWS, S)

    def body(dm_ref, z_ref, ws_ref, wst_ref, bs_ref, lg_ref, lb_ref, *rest):
        dz_ref, dws_ref, dbs_ref, dlg_ref, dlb_ref, mix, dv = rest[-7:]

        @pl.when(pl.program_id(0) == 0)
        def _():
            dws_ref[...] = jnp.zeros_like(dws_ref)
            dbs_ref[...] = jnp.zeros_like(dbs_ref)
            dlg_ref[...] = jnp.zeros_like(dlg_ref)
            dlb_ref[...] = jnp.zeros_like(dlb_ref)
        for r0 in range(0, ts, CHUNK):
            rows = slice(r0, r0 + CHUNK)
            gv, dgelu_v = _gelu_and_grad(z_ref[1, rows, :].astype(F32))
            xc = gv - jnp.mean(gv, axis=-1, keepdims=True)
            rs = lax.rsqrt(jnp.mean(xc * xc, axis=-1, keepdims=True) + EPS)
            vh = xc * rs
            vb = (vh * lg_ref[...] + lb_ref[...]).astype(BF16)
            for gi in range(NH):
                cs = slice(gi * HD, (gi + 1) * HD)
                mix[rows, cs] = _dot(ws_ref[gi], vb[:, cs])
            u, dgelu_u = _gelu_and_grad(z_ref[0, rows, :].astype(F32))
            sa = _sigmoid(z_ref[2, rows, :].astype(F32))
            dya = dm_ref[rows, :].astype(F32) * sa
            dym = dya * (mix[rows, :] + bs_ref[...])
            dz_ref[2, rows, :] = (dym * u * (1.0 - sa)).astype(BF16)
            dz_ref[0, rows, :] = (dym * dgelu_u).astype(BF16)
            dmix = dya * u
            dmb = dmix.astype(BF16)
            for gi in range(NH):
                cs = slice(gi * HD, (gi + 1) * HD)
                dv[rows, cs] = _dot(wst_ref[gi], dmb[:, cs])
                dws_ref[gi] += _dot_nt(dmb[:, cs], vb[:, cs])
                dbs_ref[gi] += jnp.broadcast_to(jnp.sum(dmix[:, cs], axis=1, keepdims=True), (CHUNK, HD))
            dvv = dv[rows, :]
            dlg_ref[...] += jnp.sum(dvv * vh, axis=0, keepdims=True)
            dlb_ref[...] += jnp.sum(dvv, axis=0, keepdims=True)
            dvh = dvv * lg_ref[...]
            dgv = rs * (dvh - jnp.mean(dvh, axis=-1, keepdims=True) - vh * jnp.mean(dvh * vh, axis=-1, keepdims=True))
            dz_ref[1, rows, :] = (dgv * dgelu_v).astype(BF16)

    vec = pl.BlockSpec((1, D), lambda i: (0, 0))
    mat = pl.BlockSpec((NH, CHUNK, CHUNK), lambda i: (0, 0, 0))
    return pl.pallas_call(
        body, name="gmlp_bwd", grid=(S // ts,),
        in_specs=[pl.BlockSpec((ts, D), lambda i: (i, 0)), pl.BlockSpec((3, ts, D), lambda i: (0, i, 0)), mat, mat,
                  pl.BlockSpec((CHUNK, D), lambda i: (0, 0)), vec, vec] + [ANY] * len(after),
        out_specs=[pl.BlockSpec((3, ts, D), lambda i: (0, i, 0)), mat, mat, vec, vec],
        out_shape=[jax.ShapeDtypeStruct((6, S, D), BF16), jax.ShapeDtypeStruct((NH, CHUNK, CHUNK), F32),
                   jax.ShapeDtypeStruct((NH, CHUNK, HD), F32), jax.ShapeDtypeStruct((1, D), F32), jax.ShapeDtypeStruct((1, D), F32)],
        scratch_shapes=[pltpu.VMEM((ts, D), F32), pltpu.VMEM((ts, D), F32)],
        compiler_params=_cp(("arbitrary",)),
    )(dm, z6, ws_b, wst_b, bs_b, lg, lb, *after)


def _lru_bwd(dz6, dm, z6, h0, h1, cw, cb, wr, br, wi, bi, lam, after=()):
    S = z6.shape[1]
    nt = S // RT

    def body(dz_in, dm_ref, z_ref, h0_ref, h1_ref, cw_ref, cb_ref, wr_ref, br_ref, wi_ref, bi_ref, lam_ref, *rest):
        dz_ref, dcw_ref, dcb_ref, dwr_ref, dbr_ref, dwi_ref, dbi_ref, dlam_ref, zxp, xc_s, dhs_s, dxcp, r_s, lam_s = rest[-14:]
        del dz_in
        lam = lam_ref[...]
        sp = _softplus_neg(lam)
        row = _row_iota()
        _fill_padded(zxp, z_ref.at[0], S)
        _conv_fwd_all(zxp, xc_s, cw_ref, cb_ref, S)
        zeros = jnp.zeros((PADR, HD), F32)
        dxcp[0:PADR, :] = zeros
        dxcp[PADR + S:2 * PADR + S, :] = zeros
        dwr_ref[...] = jnp.zeros_like(dwr_ref)
        dwi_ref[...] = jnp.zeros_like(dwi_ref)

        def pre(i, c):
            rows = pl.ds(pl.multiple_of(i * RT, RT), RT)
            hs = h0_ref[rows, :] + h1_ref[rows, :]
            dmv = dm_ref[rows, :].astype(F32)
            sb = _sigmoid(z_ref[2, rows, :].astype(F32))
            gg, dgg = _gelu_and_grad(z_ref[1, rows, :].astype(F32))
            dz_ref[2, rows, :] = (dmv * hs * gg * sb * (1.0 - sb)).astype(BF16)
            dyb = dmv * sb
            dz_ref[1, rows, :] = (dyb * hs * dgg).astype(BF16)
            dhs_s[rows, :] = dyb * gg
            return c
        lax.fori_loop(0, nt, pre, 0)

        def gate_bwd(d, gates, lamv, da, xc):
            r, gi, a, mult, inv_mult = gates
            lx, lm = lamv * xc, lamv * mult
            dlog_r = (da - (lx * gi) * (a * inv_mult)) * a * r
            dpr = dlog_r * (1.0 - r) * (-LRU_C * sp[d:d + 1, :])
            dpi = (lx * mult) * gi * (1.0 - gi)
            xb, dprb, dpib = xc.astype(BF16), dpr.astype(BF16), dpi.astype(BF16)
            dwr_ref[d] += _dot_tn(xb, dprb)
            dwi_ref[d] += _dot_tn(xb, dpib)
            dxc = lm * gi + _dot_nt(dprb, wr_ref[d]) + _dot_nt(dpib, wi_ref[d])
            return dxc, (jnp.sum(dlog_r, axis=0, keepdims=True) * (-LRU_C), jnp.sum(dpr, axis=0, keepdims=True),
                         jnp.sum(dpi, axis=0, keepdims=True))

        def rgates(i, c):
            for u in range(UNROLL):
                rows = pl.ds(pl.multiple_of((i * UNROLL + u) * RT, RT), RT)
                xb = xc_s[rows, :].astype(BF16)
                for d in range(2):
                    r_s[d, rows, :] = _sigmoid(_dot(xb, wr_ref[d]) + br_ref[d:d + 1, :])
            return c
        lax.fori_loop(0, nt // UNROLL, rgates, 0)

        def chains(i, carry):
            qn, qp = carry
            for u in range(UNROLL):
                j = i * UNROLL + u
                rd = pl.ds(pl.multiple_of((nt - 1 - j) * RT, RT), RT)
                a, dhs = _decay(r_s[0, rd, :], sp[0:1, :])[0], dhs_s[rd, :]
                q, q_first = _scan_down(a, a * dhs, qn)
                lam_s[0, rd, :] = dhs + jnp.where(row == RT - 1, qn, pltpu.roll(q, RT - 1, 0))
                qn = q_first
                ru = pl.ds(pl.multiple_of(j * RT, RT), RT)
                a, dhs = _decay(r_s[1, ru, :], sp[1:2, :])[0], dhs_s[ru, :]
                q, q_last = _scan_up(a, a * dhs, qp)
                lam_s[1, ru, :] = dhs + jnp.where(row == 0, qp, pltpu.roll(q, 1, 0))
                qp = q_last
            return qn, qp

        z1 = jnp.zeros((1, HD), F32)
        lax.fori_loop(0, nt // UNROLL, chains, (z1, z1))

        ct = min(GRAD_ROWS, S)
        crow = lax.broadcasted_iota(jnp.int32, (ct, HD), 0)

        def tile_grads(i, acc):
            t0 = pl.multiple_of(i * ct, ct)
            rows = pl.ds(t0, ct)
            xc = xc_s[rows, :]
            xb = xc.astype(BF16)
            tp = pl.multiple_of(jnp.maximum(t0 - PADR, 0), PADR)
            prev = jnp.where(t0 > 0, h0_ref[pl.ds(tp, PADR), :][PADR - 1:PADR, :], 0.0)
            tn = pl.multiple_of(jnp.minimum(t0 + ct, S - PADR), PADR)
            nxt = jnp.where(t0 + ct < S, h1_ref[pl.ds(tn, PADR), :][0:1, :], 0.0)
            hside = (jnp.where(crow == 0, prev, pltpu.roll(h0_ref[rows, :], 1, 0)),
                     jnp.where(crow == ct - 1, nxt, pltpu.roll(h1_ref[rows, :], ct - 1, 0)))
            dxc, sums = 0.0, ()
            for d in range(2):
                r = r_s[d, rows, :]
                gi = _sigmoid(_dot(xb, wi_ref[d]) + bi_ref[d:d + 1, :])
                lamv = lam_s[d, rows, :]
                dxc_d, s_d = gate_bwd(d, (r, gi) + _decay_bwd(r, sp[d:d + 1, :]), lamv, lamv * hside[d], xc)
                dxc = dxc + dxc_d
                sums = sums + s_d
            dxcp[pl.ds(t0 + PADR, ct), :] = dxc
            return tuple(x + y for x, y in zip(acc, sums))

        s_sp0, s_br0, s_bi0, s_sp1, s_br1, s_bi1 = lax.fori_loop(0, S // ct, tile_grads, (z1,) * 6)

        dsp = jnp.concatenate([s_sp0, s_sp1], axis=0)
        dlam_ref[...] = -dsp * _sigmoid(-lam)
        dbr_ref[...] = jnp.concatenate([s_br0, s_br1], axis=0)
        dbi_ref[...] = jnp.concatenate([s_bi0, s_bi1], axis=0)

        def conv_bwd(i, carry):
            c0, c1, c2, c3, cb_ = carry
            t0 = pl.multiple_of(i * RT, RT)
            dwin = dxcp[pl.ds(t0, RT + 2 * PADR), :]
            d0 = _shifted(dwin, 0)
            dz_ref[0, pl.ds(t0, RT), :] = (_shifted(dwin, 1) * cw_ref[0:1, :] + d0 * cw_ref[1:2, :]
                                           + _shifted(dwin, -1) * cw_ref[2:3, :] + _shifted(dwin, -2) * cw_ref[3:4, :]).astype(BF16)
            xm1, x0, xp1, xp2 = _conv_taps(zxp[pl.ds(t0, RT + 2 * PADR), :])
            sm = lambda v: jnp.sum(v, axis=0, keepdims=True)
            return c0 + sm(d0 * xm1), c1 + sm(d0 * x0), c2 + sm(d0 * xp1), c3 + sm(d0 * xp2), cb_ + sm(d0)

        c0, c1, c2, c3, cb_ = lax.fori_loop(0, nt, conv_bwd, (z1, z1, z1, z1, z1))
        dcw_ref[...] = jnp.concatenate([c0, c1, c2, c3], axis=0)
        dcb_ref[...] = cb_

    col = pl.BlockSpec((S, HD), lambda h: (0, h))
    head = lambda h: (0, h)
    wspec = pl.BlockSpec((2, None, HD, HD), lambda h: (0, h, 0, 0))
    return pl.pallas_call(
        body, name="lru_bwd", grid=(NH,),
        in_specs=[pl.BlockSpec(memory_space=pl.ANY), col, pl.BlockSpec((3, S, HD), lambda h: (1, 0, h)), col, col] + _lru_specs(S)
        + [ANY] * len(after),
        out_specs=[pl.BlockSpec((3, S, HD), lambda h: (1, 0, h)), pl.BlockSpec((4, HD), head), pl.BlockSpec((1, HD), head),
                   wspec, pl.BlockSpec((2, HD), head), wspec, pl.BlockSpec((2, HD), head), pl.BlockSpec((2, HD), head)],
        out_shape=[jax.ShapeDtypeStruct((6, S, D), BF16), jax.ShapeDtypeStruct((4, D), F32), jax.ShapeDtypeStruct((1, D), F32),
                   jax.ShapeDtypeStruct((2, NH, HD, HD), F32), jax.ShapeDtypeStruct((2, D), F32),
                   jax.ShapeDtypeStruct((2, NH, HD, HD), F32), jax.ShapeDtypeStruct((2, D), F32), jax.ShapeDtypeStruct((2, D), F32)],
        scratch_shapes=[pltpu.VMEM((S + 2 * PADR, HD), F32), pltpu.VMEM((S, HD), F32), pltpu.VMEM((S, HD), F32),
                        pltpu.VMEM((S + 2 * PADR, HD), F32), pltpu.VMEM((2, S, HD), F32), pltpu.VMEM((2, S, HD), F32)],
        input_output_aliases={0: 0},
        compiler_params=_cp(("parallel",)),
    )(dz6, dm, z6, h0, h1, cw, cb, wr, br, wi, bi, lam, *after)


LAYER_SMALL = ("norm1_g", "gmlp_ln_g", "gmlp_ln_b", "gmlp_w_s", "gmlp_b_s", "conv_w", "conv_b",
               "lru_w_r", "lru_b_r", "lru_w_i", "lru_b_i", "lru_lambda", "norm2_g")


def _layer_operands(l, p):
    ws_b = p["gmlp_w_s"][l].astype(BF16)
    tm = dict(ws_b=ws_b, wst_b=jnp.swapaxes(ws_b, 1, 2), bs_b=jnp.repeat(p["gmlp_b_s"][l].T, HD, axis=1),
              lg=p["gmlp_ln_g"][l][None], lb=p["gmlp_ln_b"][l][None])
    lru = (p["conv_w"][l], p["conv_b"][l][None], p["lru_w_r"][l].astype(BF16), p["lru_b_r"][l],
           p["lru_w_i"][l].astype(BF16), p["lru_b_i"][l], p["lru_lambda"][l])
    return (p["norm1_g"][l][None], p["norm2_g"][l][None]), tm, lru


def _forward_layer(l, x, p, wb, after=(), rest=None, near_end=None, operands=None, loss=None):
    (g1, g2), tm, lru = _layer_operands(l, p) if operands is None else operands
    z6, hn1 = _mm_in(x, g1, wb["w_in"], l, after)
    ya = _gmlp_fwd(z6, tm["ws_b"], tm["bs_b"], tm["lg"], tm["lb"])
    merged, h0, h1 = _lru_fwd(z6, ya, *lru)
    if rest is not None:
        wb = dict(wb, **rest(merged))
    x1 = _mm_res(merged, wb["w_out"], x, l, "mm_out")
    gu, ff, hn2 = _mm_ffn_in(x1, g2, wb["w_ffn_in"], l)
    if loss is None:
        x2 = _mm_res(ff, wb["w_ffn_out"], x1, l, "mm_ffn_out", () if near_end is None else tuple(near_end(gu)))
    else:
        x2 = _mm_res_loss(ff, wb["w_ffn_out"], x1, *loss)
    return x2, dict(x=x, z6=z6, h0=h0, h1=h1, merged=merged, x1=x1, gu=gu, ff=ff, g1=g1, g2=g2, tm=tm, lru=lru,
                    hn1=hn1, hn2=hn2, wb=wb)


def _backward_layer(l, dx, s, after=(), midway=None, midway2=None, midway3=None, late=None):
    S = dx.shape[0]
    tm, wb = s["tm"], s["wb"]
    g2 = s["g2"]
    dgu = _bwd_ffn_out(dx, wb["w_ffn_out"], s["gu"], l, after)
    tmb = min(TM_BIG, S)
    dwfo = _mm_tn(s["ff"], dx, DFF_SH, tmb, f"dw_ffn_out_{l}")
    dx1, dg2 = _mm_nt_rms_bwd(
        dgu, [pl.BlockSpec((None, tmb, DFF_SH), lambda i, k: (k, i, 0))],
        wb["w_ffn_in"], [pl.BlockSpec((None, D, DFF_SH), lambda i, k: (k, 0, 0))],
        4, tmb, s["x1"], g2, dx, f"bwd_ffn_in_{l}")
    dwfi = _dw_ffn_in(s["hn2"], dgu, l)
    dmg, dwo = _bwd_out(dx1, wb["w_out"], s["merged"], l)
    mid = () if midway is None else tuple(midway([dwo, dwfi, dwfo]))
    dz6, dws, dbs, dlg, dlb = _gmlp_bwd(dmg, s["z6"], tm["ws_b"], tm["wst_b"], tm["bs_b"], tm["lg"], tm["lb"], mid)
    mid2 = () if midway2 is None else tuple(midway2(dws))
    dz6, dcw, dcb, dwr, dbr, dwi, dbi, dlam = _lru_bwd(dz6, dmg, s["z6"], s["h0"], s["h1"], *s["lru"], after=mid2)

    sub = 3

    def dz_tile(j):
        return pl.BlockSpec((None, tmb, 512), lambda i, k: ((sub * k + j) // 2, i, (sub * k + j) % 2))

    def w_tile(j):
        def w_map(i, k):
            sh, tl = _in_tile(sub * k + j)
            return (sh, 0, tl)
        return pl.BlockSpec((None, D, 512), w_map)

    small = dict(gmlp_ln_g=dlg[0], gmlp_ln_b=dlb[0], gmlp_w_s=dws, gmlp_b_s=dbs[:, :, 0], conv_w=dcw, conv_b=dcb[0],
                 lru_w_r=dwr, lru_b_r=dbr, lru_w_i=dwi, lru_b_i=dbi, lru_lambda=dlam, norm2_g=dg2[0])
    mid3 = () if midway3 is None else tuple(midway3(small))
    dwin = _dw_in(s["hn1"], dz6, l, mid3)
    tail = () if late is None else tuple(late([dwin]))
    dx0, dg1 = _mm_nt_rms_bwd(
        dz6, [dz_tile(j) for j in range(sub)], wb["w_in"], [w_tile(j) for j in range(sub)],
        N_IN_T // sub, tmb, s["x"], s["g1"], dx1, f"bwd_in_{l}", tail)
    return dx0, [dwin, dwo, dwfi, dwfo], dict(small, norm1_g=dg1[0])


def _local_step(x, tgt, p, wbs):
    saved = []
    for l in range(2):
        x, s = _forward_layer(l, x, p, wbs[l], loss=(tgt, p["final_g"][None]) if l else None)
        saved.append(s)
    dx, loss_v, dfg = x
    big, smalls = [None, None], [None, None]
    for l in (1, 0):
        dx, big[l], smalls[l] = _backward_layer(l, dx, saved[l])
    small = {k: jnp.stack([smalls[0][k], smalls[1][k]]) for k in LAYER_SMALL}
    small["final_g"] = dfg[0]
    return loss_v, dx, big, small


def _place():
    x, y, c = lax.axis_index("x"), lax.axis_index("y"), lax.axis_index("c")
    return x, y, c, 2 * x + y


def _chip_at(x, y, d):
    px = 1 - x if d & 2 else x
    py = 1 - y if d & 1 else y
    return px, py, 2 * px + py


HBM = pl.BlockSpec(memory_space=pltpu.HBM)
SEM = pl.BlockSpec(memory_space=pltpu.SEMAPHORE)
DATAFLOW = pltpu.SideEffectType.DATAFLOW_SIDE_EFFECTING


def _in_hbm(a):
    return pltpu.with_memory_space_constraint(a, pltpu.HBM)


def _cast_into(wfs, l, chip_arr, name):
    n = len(wfs)

    def body(ch_ref, *refs):
        for w_ref, o_ref in zip(refs[:n], refs[n:]):
            o_ref[...] = w_ref[...].astype(BF16)

    halves = [(wf.shape[1] // 2, wf.shape[2]) for wf in wfs]
    return pl.pallas_call(
        body, name=name, out_shape=[jax.ShapeDtypeStruct((4, 2, rh, cols), BF16) for rh, cols in halves],
        grid_spec=pltpu.PrefetchScalarGridSpec(
            num_scalar_prefetch=1, grid=(2,),
            in_specs=[pl.BlockSpec((None, None, rh, cols), lambda h, ch: (l, h, 0, 0)) for rh, cols in halves],
            out_specs=[pl.BlockSpec((None, None, rh, cols), lambda h, ch: (ch[0], h, 0, 0)) for rh, cols in halves]),
        compiler_params=_cp(("parallel",)),
    )(chip_arr, *[wf.reshape(2, 2, rh, cols) for wf, (rh, cols) in zip(wfs, halves)])


def _half_block(ref, chip, half, to, send_sem, recv_sem):
    blk = ref.at[chip, half]
    return pltpu.make_async_remote_copy(src_ref=blk, dst_ref=blk, send_sem=send_sem, recv_sem=recv_sem,
                                        device_id=to, device_id_type=MESH)


def _gather_weights(bufs, tiny):
    nt = len(bufs)
    n_ici = max(nt * 3, 1)

    def body(*refs):
        tiny_ref = refs[nt]
        o_refs, tiny_o = refs[nt + 1:2 * nt + 1], refs[2 * nt + 1]
        send, recv, fsend, frecv, tsend, trecv, lsem = refs[2 * nt + 2:]
        x, y, c, chip = _place()
        local = pltpu.make_async_copy(tiny_ref, tiny_o.at[chip], lsem)
        local.start()

        def tin(d, origin_chip, to):
            return pltpu.make_async_remote_copy(
                src_ref=tiny_ref, dst_ref=tiny_o.at[origin_chip], send_sem=tsend.at[d - 1], recv_sem=trecv.at[d - 1],
                device_id=to, device_id_type=MESH)

        sends = []
        for t in range(nt):
            for d in (1, 2, 3):
                px, py, _ = _chip_at(x, y, d)
                sends.append(_half_block(o_refs[t], chip, c, (px, py, c), send.at[3 * t + d - 1], recv.at[3 * t + d - 1]))
        for d in (1, 2, 3):
            px, py, _ = _chip_at(x, y, d)
            sends.append(tin(d, chip, (px, py, c)))
        for cp in sends:
            cp.start()
        passed = []
        for t in range(nt):
            for d in (1, 2, 3):
                k = 3 * t + d - 1
                _, _, pchip = _chip_at(x, y, d)
                _half_block(o_refs[t], pchip, c, (x, y, c), send.at[k], recv.at[k]).wait_recv()
                f = _half_block(o_refs[t], pchip, c, (x, y, 1 - c), fsend.at[k], frecv.at[k])
                f.start()
                passed.append(f)
        for t in range(nt):
            for d in (1, 2, 3):
                k = 3 * t + d - 1
                _, _, pchip = _chip_at(x, y, d)
                _half_block(o_refs[t], pchip, 1 - c, (x, y, 1 - c), fsend.at[k], frecv.at[k]).wait_recv()
        for d in (1, 2, 3):
            _, _, pchip = _chip_at(x, y, d)
            tin(d, pchip, (x, y, c)).wait_recv()
        for cp in sends + passed:
            cp.wait_send()
        local.wait()

    out_shape = [jax.ShapeDtypeStruct(b.shape, b.dtype) for b in bufs]
    out_shape.append(jax.ShapeDtypeStruct((4,) + tiny.shape, tiny.dtype))
    outs = pl.pallas_call(
        body, name="gather_weights_0", out_shape=out_shape,
        in_specs=[ANY] * (nt + 1), out_specs=[ANY] * (nt + 1),
        scratch_shapes=[pltpu.SemaphoreType.DMA((n_ici,)), pltpu.SemaphoreType.DMA((n_ici,)),
                        pltpu.SemaphoreType.DMA((n_ici,)), pltpu.SemaphoreType.DMA((n_ici,)),
                        pltpu.SemaphoreType.DMA((3,)), pltpu.SemaphoreType.DMA((3,)), pltpu.SemaphoreType.DMA],
        input_output_aliases={t: t for t in range(nt)},
        compiler_params=_cp(has_side_effects=True),
    )(*bufs, tiny)
    return outs[:nt], outs[nt]


def _gather_start(bufs, tag, after=()):
    nt, na = len(bufs), len(after)

    def body(*refs):
        b_refs = refs[:nt]
        send, recv = refs[nt + na], refs[nt + na + 1]
        token = refs[2 * nt + na + 2]
        x, y, c, chip = _place()
        for t in range(nt):
            for d in (1, 2, 3):
                px, py, _ = _chip_at(x, y, d)
                _half_block(b_refs[t], chip, c, (px, py, c), send.at[3 * t + d - 1], recv.at[3 * t + d - 1]).start()
        token[...] = jnp.zeros_like(token)

    outs = pl.pallas_call(
        body, name=f"gather_start_{tag}",
        out_shape=(pltpu.SemaphoreType.DMA((3 * nt,)), pltpu.SemaphoreType.DMA((3 * nt,)),
                   *[pltpu.HBM(b.shape, b.dtype) for b in bufs], jax.ShapeDtypeStruct((8, 128), F32)),
        in_specs=[HBM] * nt + [ANY] * na, out_specs=(SEM, SEM, *[HBM] * nt, pl.BlockSpec(memory_space=pltpu.VMEM)),
        input_output_aliases={t: 2 + t for t in range(nt)},
        compiler_params=pltpu.CompilerParams(has_side_effects=DATAFLOW),
    )(*[_in_hbm(b) for b in bufs], *after)
    return outs[0], outs[1], list(outs[2:2 + nt]), outs[2 + nt]


def _gather_wait(send, recv, bufs, after, tag):
    nt = len(bufs)

    def body(*refs):
        b_refs = refs[:nt]
        send_ref, recv_ref = refs[nt], refs[nt + 1]
        x, y, c, chip = _place()
        for t in range(nt):
            for d in (1, 2, 3):
                k = 3 * t + d - 1
                px, py, pchip = _chip_at(x, y, d)
                _half_block(b_refs[t], chip, c, (px, py, c), send_ref.at[k], recv_ref.at[k]).wait_send()
                _half_block(b_refs[t], pchip, c, (px, py, c), send_ref.at[k], recv_ref.at[k]).wait_recv()

    after = tuple(after) if isinstance(after, (tuple, list)) else (after,)
    outs = pl.pallas_call(
        body, name=f"gather_wait_{tag}", out_shape=[pltpu.HBM(b.shape, b.dtype) for b in bufs],
        in_specs=[HBM] * nt + [SEM, SEM] + [ANY] * len(after), out_specs=[HBM] * nt,
        input_output_aliases={t: t for t in range(nt)},
        compiler_params=pltpu.CompilerParams(has_side_effects=DATAFLOW),
    )(*bufs, send, recv, *after)
    return list(outs)


def _gather_pass_on(bufs, tag):
    nt = len(bufs)

    def body(*refs):
        o_refs = refs[nt:2 * nt]
        fsend, frecv = refs[2 * nt:]
        x, y, c, _ = _place()
        cps = []
        for t in range(nt):
            for d in (1, 2, 3):
                k = 3 * t + d - 1
                _, _, pchip = _chip_at(x, y, d)
                cps.append(_half_block(o_refs[t], pchip, c, (x, y, 1 - c), fsend.at[k], frecv.at[k]))
        for cp in cps:
            cp.start()
        for t in range(nt):
            for d in (1, 2, 3):
                k = 3 * t + d - 1
                _, _, pchip = _chip_at(x, y, d)
                _half_block(o_refs[t], pchip, 1 - c, (x, y, 1 - c), fsend.at[k], frecv.at[k]).wait_recv()
        for cp in cps:
            cp.wait_send()

    return pl.pallas_call(
        body, name=f"gather_pass_on_{tag}", out_shape=[jax.ShapeDtypeStruct(b.shape, b.dtype) for b in bufs],
        in_specs=[ANY] * nt, out_specs=[ANY] * nt,
        scratch_shapes=[pltpu.SemaphoreType.DMA((3 * nt,)), pltpu.SemaphoreType.DMA((3 * nt,))],
        input_output_aliases={t: t for t in range(nt)},
        compiler_params=_cp(has_side_effects=True),
    )(*bufs)


def _chip_copy(c_ref, land_ref, x, y, c, d, send_sem, recv_sem):
    px, py, pchip = _chip_at(x, y, d)
    return pltpu.make_async_remote_copy(src_ref=c_ref.at[pchip], dst_ref=land_ref.at[d - 1], send_sem=send_sem, recv_sem=recv_sem,
                                        device_id=(px, py, c), device_id_type=MESH)


def _exchange_start(srcs, lands, copies, nsem, name):
    ns, n = len(srcs), len(srcs) + len(lands)

    def body(*refs):
        for cp in copies(refs[:ns], refs[ns:n], refs[n], refs[n + 1]):
            cp.start()
        token = refs[2 * n + 2]
        token[...] = jnp.zeros_like(token)

    outs = pl.pallas_call(
        body, name=name,
        out_shape=(pltpu.SemaphoreType.DMA((nsem,)), pltpu.SemaphoreType.DMA((nsem,)),
                   *[pltpu.HBM(a.shape, a.dtype) for a in list(srcs) + list(lands)], jax.ShapeDtypeStruct((8, 128), F32)),
        in_specs=[HBM] * n, out_specs=(SEM, SEM, *[HBM] * n, pl.BlockSpec(memory_space=pltpu.VMEM)),
        input_output_aliases={i: 2 + i for i in range(n)},
        compiler_params=pltpu.CompilerParams(has_side_effects=DATAFLOW),
    )(*[_in_hbm(a) for a in list(srcs) + list(lands)])
    return outs[0], outs[1], list(outs[2:2 + ns]), list(outs[2 + ns:2 + n]), outs[2 + n]


def _exchange_wait(send, recv, srcs, lands, after, copies, name):
    ns, n = len(srcs), len(srcs) + len(lands)

    def body(*refs):
        for cp in copies(refs[:ns], refs[ns:n], refs[n], refs[n + 1]):
            cp.wait_send()
            cp.wait_recv()

    outs = pl.pallas_call(
        body, name=name, out_shape=[pltpu.HBM(a.shape, a.dtype) for a in list(srcs) + list(lands)],
        in_specs=[HBM] * n + [SEM, SEM, ANY], out_specs=[HBM] * n,
        input_output_aliases={i: i for i in range(n)},
        compiler_params=pltpu.CompilerParams(has_side_effects=DATAFLOW),
    )(*srcs, *lands, send, recv, after)
    return list(outs[:ns]), list(outs[ns:])


def _pass_on_copies(b_refs, land_refs, send, recv):
    del land_refs
    x, y, c, _ = _place()
    return [_half_block(b_refs[t], _chip_at(x, y, d)[2], c, (x, y, 1 - c), send.at[3 * t + d - 1], recv.at[3 * t + d - 1])
            for t in range(len(b_refs)) for d in (1, 2, 3)]


def _chips_copies(c_refs, land_refs, send, recv):
    x, y, c, _ = _place()
    return [_chip_copy(c_refs[t], land_refs[t], x, y, c, d, send.at[3 * t + d - 1], recv.at[3 * t + d - 1])
            for t in range(len(c_refs)) for d in (1, 2, 3)]


def _sibling_copies(g_refs, land_refs, send, recv):
    x, y, c, _ = _place()
    return [pltpu.make_async_remote_copy(
        src_ref=g_refs[t].at[k, 1 - c], dst_ref=land_refs[t].at[k], send_sem=send.at[4 * t + k], recv_sem=recv.at[4 * t + k],
        device_id=(x, y, 1 - c), device_id_type=MESH) for t in range(len(g_refs)) for k in range(4)]


def _join_copies(f_refs, land_refs, send, recv):
    del land_refs
    x, y, c, _ = _place()
    return [pltpu.make_async_remote_copy(
        src_ref=f_refs[t].at[c], dst_ref=f_refs[t].at[c], send_sem=send.at[t], recv_sem=recv.at[t],
        device_id=(x, y, 1 - c), device_id_type=MESH) for t in range(len(f_refs))]


def _add_half(gs, rs, c_arr, name):
    n = len(gs)

    def body(c_ref, *refs):
        for g_ref, r_ref, o_ref in zip(refs[:n], refs[n:2 * n], refs[2 * n:]):
            o_ref[...] = (g_ref[...] + r_ref[...]).astype(BF16)

    def own(g):
        return pl.BlockSpec((None, None) + g.shape[2:], lambda k, cr: (k, cr[0], 0, 0))

    def blk(g):
        return pl.BlockSpec((None,) + g.shape[2:], lambda k, cr: (k, 0, 0))

    return pl.pallas_call(
        body, name=name, out_shape=[jax.ShapeDtypeStruct((4,) + g.shape[2:], BF16) for g in gs],
        grid_spec=pltpu.PrefetchScalarGridSpec(
            num_scalar_prefetch=1, grid=(4,),
            in_specs=[own(g) for g in gs] + [blk(g) for g in gs], out_specs=[blk(g) for g in gs]),
        compiler_params=_cp(("parallel",)),
    )(c_arr, *gs, *rs)


def _sum_chips(css, r3s, place_arr, name):
    n = len(css)

    def body(pl_ref, *refs):
        up = lambda ref: ref[...].astype(F32)
        for t in range(n):
            a_ref, (r0_ref, r1_ref, r2_ref), o_ref = refs[t], refs[n + 3 * t:n + 3 * t + 3], refs[4 * n + t]
            o_ref[...] = ((up(a_ref) + up(r0_ref)) + up(r1_ref)) + up(r2_ref)

    def blk(cs, first):
        _, rh, cols = cs.shape
        return pl.BlockSpec((None, rh // 2, cols), lambda i, pa: (first(pa), i, 0))

    in_specs = [blk(cs, lambda pa: pa[0]) for cs in css]
    for cs in css:
        in_specs += [blk(cs, lambda pa, d=d: d) for d in range(3)]
    return pl.pallas_call(
        body, name=name, out_shape=[jax.ShapeDtypeStruct((2,) + cs.shape[1:], F32) for cs in css],
        grid_spec=pltpu.PrefetchScalarGridSpec(
            num_scalar_prefetch=1, grid=(2,), in_specs=in_specs, out_specs=[blk(cs, lambda pa: pa[1]) for cs in css]),
        compiler_params=_cp(("parallel",)),
    )(place_arr, *css, *[r3 for r3 in r3s for _ in range(3)])


def _allreduce_small(pack):
    rows = pack.shape[0]
    hr = rows // 2

    def body(p_ref, o_ref, sib, slots, s1, r1, s2, r2, s3, r3):
        x, y, c, chip = _place()
        sibling = (x, y, 1 - c)
        ex = pltpu.make_async_remote_copy(src_ref=p_ref, dst_ref=sib, send_sem=s1, recv_sem=r1,
                                          device_id=sibling, device_id_type=MESH)
        ex.start()
        ex.wait()
        half = pl.ds(pl.multiple_of(c * hr, 16), hr)
        slots[0] = (p_ref[half, :] + sib[half, :]).astype(BF16)
        cps = []
        for d in (1, 2, 3):
            px, py, _ = _chip_at(x, y, d)
            cps.append(pltpu.make_async_remote_copy(
                src_ref=slots.at[0], dst_ref=slots.at[d], send_sem=s2.at[d - 1], recv_sem=r2.at[d - 1],
                device_id=(px, py, c), device_id_type=MESH))
        for cp in cps:
            cp.start()
        for cp in cps:
            cp.wait()
        tot = slots[chip].astype(F32)
        for k in (1, 2, 3):
            tot = tot + slots[jnp.bitwise_xor(chip, k)].astype(F32)
        o_ref[half, :] = tot
        back = pltpu.make_async_remote_copy(src_ref=o_ref.at[half, :], dst_ref=o_ref.at[half, :], send_sem=s3, recv_sem=r3,
                                            device_id=sibling, device_id_type=MESH)
        back.start()
        back.wait()

    vm = pl.BlockSpec(memory_space=pltpu.VMEM)
    return pl.pallas_call(
        body, name="allreduce_small", out_shape=jax.ShapeDtypeStruct((rows, 128), F32),
        in_specs=[vm], out_specs=vm,
        scratch_shapes=[pltpu.VMEM((rows, 128), F32), pltpu.VMEM((4, hr, 128), BF16),
                        pltpu.SemaphoreType.DMA, pltpu.SemaphoreType.DMA, pltpu.SemaphoreType.DMA((3,)), pltpu.SemaphoreType.DMA((3,)),
                        pltpu.SemaphoreType.DMA, pltpu.SemaphoreType.DMA],
        compiler_params=_cp(has_side_effects=True),
    )(pack)


def _small_chip_sum(pack, after=()):
    rows = pack.shape[0]
    hr = rows // 2

    def body(p_ref, *rest):
        o_ref, sib, s1, r1 = rest[-4:]
        x, y, c, _ = _place()
        ex = pltpu.make_async_remote_copy(src_ref=p_ref, dst_ref=sib, send_sem=s1, recv_sem=r1,
                                          device_id=(x, y, 1 - c), device_id_type=MESH)
        ex.start()
        ex.wait()
        half = pl.ds(pl.multiple_of(c * hr, 16), hr)
        o_ref[...] = (p_ref[half, :] + sib[half, :]).astype(BF16)

    vm = pl.BlockSpec(memory_space=pltpu.VMEM)
    return pl.pallas_call(
        body, name="small_chip_sum", out_shape=jax.ShapeDtypeStruct((hr, 128), BF16),
        in_specs=[vm] + [ANY] * len(after), out_specs=vm,
        scratch_shapes=[pltpu.VMEM((rows, 128), F32), pltpu.SemaphoreType.DMA, pltpu.SemaphoreType.DMA],
        compiler_params=_cp(has_side_effects=True),
    )(pack, *after)


def _small_copies(c_refs, land_refs, send, recv):
    x, y, c, _ = _place()
    cps = []
    for d in (1, 2, 3):
        px, py, _ = _chip_at(x, y, d)
        cps.append(pltpu.make_async_remote_copy(src_ref=c_refs[0], dst_ref=land_refs[0].at[d - 1], send_sem=send.at[d - 1],
                                                recv_sem=recv.at[d - 1], device_id=(px, py, c), device_id_type=MESH))
    return cps


def _small_total(csum, land):
    hr = csum.shape[0]

    def body(c_ref, l_ref, o_ref, slots, s3, r3):
        x, y, c, chip = _place()
        slots[0] = c_ref[...]
        for d in (1, 2, 3):
            slots[d] = l_ref[d - 1]
        tot = slots[chip].astype(F32)
        for k in (1, 2, 3):
            tot = tot + slots[jnp.bitwise_xor(chip, k)].astype(F32)
        half = pl.ds(pl.multiple_of(c * hr, 16), hr)
        o_ref[half, :] = tot
        back = pltpu.make_async_remote_copy(src_ref=o_ref.at[half, :], dst_ref=o_ref.at[half, :], send_sem=s3, recv_sem=r3,
                                            device_id=(x, y, 1 - c), device_id_type=MESH)
        back.start()
        back.wait()

    vm = pl.BlockSpec(memory_space=pltpu.VMEM)
    return pl.pallas_call(
        body, name="small_total", out_shape=jax.ShapeDtypeStruct((2 * hr, 128), F32), in_specs=[vm, vm], out_specs=vm,
        scratch_shapes=[pltpu.VMEM((4, hr, 128), BF16), pltpu.SemaphoreType.DMA, pltpu.SemaphoreType.DMA],
        compiler_params=_cp(has_side_effects=True),
    )(csum, land)


def _adam_math(gv, wv, mv, vv):
    m2 = ADAM_B1 * mv + (1.0 - ADAM_B1) * gv
    v2 = ADAM_B2 * vv + (1.0 - ADAM_B2) * (gv * gv)
    m_hat = m2 / (1.0 - ADAM_B1 ** ADAM_STEP)
    v_hat = v2 / (1.0 - ADAM_B2 ** ADAM_STEP)
    return -ADAM_LR * (m_hat / (jnp.sqrt(v_hat) + ADAM_EPS) + ADAM_WD * wv), m2, v2


def _adam(g, w, m, v, name):
    rows, cols = g.shape
    rb = rows // 4

    def body(g_ref, w_ref, m_ref, v_ref, d_ref, m2_ref, v2_ref):
        d_ref[...], m2_ref[...], v2_ref[...] = _adam_math(g_ref[...], w_ref[...], m_ref[...], v_ref[...])

    blk = pl.BlockSpec((rb, cols), lambda i: (i, 0))
    shp = jax.ShapeDtypeStruct((rows, cols), F32)
    return pl.pallas_call(
        body, name=name, grid=(4,), in_specs=[blk] * 4, out_specs=[blk] * 3, out_shape=[shp] * 3,
        compiler_params=_cp(("parallel",)),
    )(g, w, m, v)


def _adam_layer(gs, ws, ms, vs, l, prevs, name):
    n = len(gs)
    prev = [a for p4 in prevs if p4 is not None for a in p4]

    def body(*refs):
        outs = refs[len(refs) - 4 * n:]
        for t in range(n):
            g_ref, w_ref, m_ref, v_ref = refs[4 * t:4 * t + 4]
            go_ref, d_ref, m2_ref, v2_ref = outs[4 * t:4 * t + 4]
            gv = g_ref[...]
            go_ref[...] = gv
            d_ref[...], m2_ref[...], v2_ref[...] = _adam_math(gv, w_ref[...], m_ref[...], v_ref[...])

    in_specs, out_specs, out_shape, operands, aliases = [], [], [], [], {}
    for t, g in enumerate(gs):
        rows, cols = g.shape
        lay = pl.BlockSpec((None, rows // 4, cols), lambda i: (l, i, 0))
        in_specs += [pl.BlockSpec((rows // 4, cols), lambda i: (i, 0)), lay, lay, lay]
        operands += [g, ws[t], ms[t], vs[t]]
        out_specs += [lay] * 4
        out_shape += [jax.ShapeDtypeStruct((2, rows, cols), F32)] * 4
    k = 4 * n
    for t, p4 in enumerate(prevs):
        if p4 is not None:
            for j in range(4):
                aliases[k] = 4 * t + j
                k += 1
    outs = pl.pallas_call(
        body, name=name, grid=(4,), in_specs=in_specs + [ANY] * len(prev), out_specs=out_specs, out_shape=out_shape,
        input_output_aliases=aliases, compiler_params=_cp(("parallel",)),
    )(*operands, *prev)
    return [list(outs[4 * t:4 * t + 4]) for t in range(n)]


def _rows128(a):
    return a.reshape(-1, 128)


def _pack(arrs, mult):
    parts = [_rows128(a) for a in arrs]
    rows = sum(q.shape[0] for q in parts)
    pad = -rows % mult
    if pad:
        parts.append(jnp.zeros((pad, 128), F32))
    return jnp.concatenate(parts, axis=0)


def _unpack(pack, shapes):
    out, o = [], 0
    for s in shapes:
        n = 1
        for e in s:
            n *= e
        out.append(pack[o:o + n // 128].reshape(s))
        o += n // 128
    return out


WEIGHTS = ['norm1_g', 'w_in', 'gmlp_ln_g', 'gmlp_ln_b', 'gmlp_w_s', 'gmlp_b_s', 'conv_w', 'conv_b', 'lru_w_r', 'lru_b_r', 'lru_w_i',
           'lru_b_i', 'lru_lambda', 'w_out', 'norm2_g', 'w_ffn_in', 'w_ffn_out', 'final_g']
BIG = ['w_in', 'w_out', 'w_ffn_in', 'w_ffn_out']
SMALL = [n for n in WEIGHTS if n not in BIG]
CHIP_SHARDED_SMALL = ['conv_w', 'lru_b_r', 'lru_b_i', 'lru_lambda']


def kernel(x, norm1_g, w_in, gmlp_ln_g, gmlp_ln_b, gmlp_w_s, gmlp_b_s, conv_w, conv_b, lru_w_r, lru_b_r, lru_w_i, lru_b_i, lru_lambda, w_out, norm2_g, w_ffn_in, w_ffn_out, final_g, loss_target, m_norm1_g, m_w_in, m_gmlp_ln_g, m_gmlp_ln_b, m_gmlp_w_s, m_gmlp_b_s, m_conv_w, m_conv_b, m_lru_w_r, m_lru_b_r, m_lru_w_i, m_lru_b_i, m_lru_lambda, m_w_out, m_norm2_g, m_w_ffn_in, m_w_ffn_out, m_final_g, v_norm1_g, v_w_in, v_gmlp_ln_g, v_gmlp_ln_b, v_gmlp_w_s, v_gmlp_b_s, v_conv_w, v_conv_b, v_lru_w_r, v_lru_b_r, v_lru_w_i, v_lru_b_i, v_lru_lambda, v_w_out, v_norm2_g, v_w_ffn_in, v_w_ffn_out, v_final_g):
    a = dict(locals())
    w = {n: a[n] for n in WEIGHTS}
    mom = {n: a["m_" + n] for n in WEIGHTS}
    var = {n: a["v_" + n] for n in WEIGHTS}
    _, _, c, chip = _place()
    c_arr, chip_arr = jnp.reshape(c, (1,)).astype(jnp.int32), jnp.reshape(chip, (1,)).astype(jnp.int32)
    place_arr = jnp.stack([chip, c]).astype(jnp.int32)

    first, rest = BIG[:1], BIG[1:]

    def as_weights(names, full):
        wb = {n: f.reshape(4, 2 * f.shape[2], f.shape[3]) for n, f in zip(names, full)}
        if "w_out" in wb:
            wb["w_out"] = wb["w_out"].reshape(D, D)
            wb["w_ffn_out"] = wb["w_ffn_out"].reshape(DFF, D)
        return wb

    def cast(names, l, tag):
        return _cast_into([w[n] for n in names], l, chip_arr, f"cast_{tag}")

    def landed(fly, names, after, tag):
        return as_weights(names, _gather_pass_on(_gather_wait(fly[0], fly[1], fly[2], after, tag), tag))

    tiny = _pack([w[n] for n in CHIP_SHARDED_SMALL], 8)
    _, tiny_full = _gather_weights([], tiny)
    fly_in = _gather_start(cast(first, 0, "in"), "in", after=(tiny_full,))
    fly0 = _gather_start(cast(rest, 0, "0"), "0", after=(fly_in[3],))
    fly1 = _gather_start(cast(BIG, 1, "1"), "1", after=(fly0[3],))
    p = {n: w[n] for n in SMALL}
    parts = [_unpack(tiny_full[k], [w[n].shape for n in CHIP_SHARDED_SMALL]) for k in range(4)]
    for i, n in enumerate(CHIP_SHARDED_SMALL):
        p[n] = jnp.concatenate([parts[k][i] for k in range(4)], axis=-1)

    operands = [_layer_operands(l, p) for l in range(2)]
    state_packs = [_pack([src[n] for n in SMALL], 32) for src in (w, mom, var)]
    ahead = tuple(jax.tree.leaves(operands)) + tuple(state_packs)

    passing = {}

    def pass_on_1(gu):
        bufs = _gather_wait(fly1[0], fly1[1], fly1[2], gu, "1")
        passing[1] = _exchange_start(bufs, [], _pass_on_copies, 3 * len(bufs), "gather_pass_on_start_1")
        return (passing[1][-1],)

    xa, saved0 = _forward_layer(0, x[0], p, landed(fly_in, first, (fly1[3],) + ahead, "in"), after=(fly0[3], fly1[3]),
                                rest=lambda merged: landed(fly0, rest, merged, "0"), near_end=pass_on_1, operands=operands[0])
    send, recv, bufs1, _, _ = passing[1]
    xb, saved1 = _forward_layer(
        1, xa, p, as_weights(BIG, _exchange_wait(send, recv, bufs1, [], xa, _pass_on_copies, "gather_pass_on_wait_1")[0]),
        operands=operands[1], loss=(loss_target[0], p["final_g"][None]))
    dxb, loss_v, dfg = xb
    loss = lax.psum(loss_v[0, 0], ("x", "y", "c"))

    out, flying = {}, {}

    def halves(grads):
        return [g.reshape(4, 2, -1, g.shape[-1]) for g in grads]

    def sibling_start(grads, names, l, tag):
        gs = halves(grads)
        lands = [lax.empty((4,) + g.shape[2:], g.dtype) for g in gs]
        flying["s" + tag] = (names, l) + tuple(
            _exchange_start(gs, lands, _sibling_copies, 4 * len(gs), f"grads_to_sibling_start_{tag}"))
        return (flying["s" + tag][-1],)

    def chips_start(gs, from_sib, names, l, tag):
        cs = _add_half(gs, from_sib, c_arr, f"add_half_{tag}")
        lands = [lax.empty((3,) + a.shape[1:], a.dtype) for a in cs]
        flying[tag] = (names, l) + tuple(_exchange_start(cs, lands, _chips_copies, 3 * len(cs), f"grads_to_chips_start_{tag}"))
        return (flying[tag][-1],)

    def sibling_finish(tag, after):
        names, l, send, recv, gs, lands, _ = flying["s" + tag]
        gs, from_sib = _exchange_wait(send, recv, gs, lands, after, _sibling_copies, f"grads_to_sibling_wait_{tag}")
        return chips_start(gs, from_sib, names, l, tag)

    def reduce_sums(tags, after):
        groups, ts = [], []
        for tag in tags:
            names, l, send, recv, cs, lands, _ = flying[tag]
            cs, lands = _exchange_wait(send, recv, cs, lands, after, _chips_copies, f"grads_to_chips_wait_{tag}")
            ts += _sum_chips(cs, lands, place_arr, f"sum_chips_{tag}")
            groups.append((tag, names))
        flying["j" + tags[0]] = (groups, l) + tuple(_exchange_start(ts, [], _join_copies, len(ts), f"grads_join_start_{tags[0]}"))
        return (flying["j" + tags[0]][-1],)

    def reduce_adam(tag0, after):
        groups, l, send, recv, ts, _, _ = flying["j" + tag0]
        joined = _exchange_wait(send, recv, ts, [], after, _join_copies, f"grads_join_wait_{tag0}")[0]
        for tag, names in groups:
            gs, joined = [j.reshape(w[n].shape[1:]) for n, j in zip(names, joined)], joined[len(names):]
            res = _adam_layer(gs, [w[n] for n in names], [mom[n] for n in names], [var[n] for n in names], l,
                              [out.get(n) for n in names], f"adam_{tag}")
            out.update(zip(names, res))

    def late1(grads):
        return sibling_finish("1a", grads[0]) + sibling_start(grads, first, 1, "1b")

    def midway0(grads):
        return reduce_sums(("1a", "1b"), grads[0]) + sibling_start(grads, rest, 0, "0a")

    def stacked_small(small0):
        small = {k: jnp.stack([small0[k], small1[k]]) for k in LAYER_SMALL}
        return dict(small, final_g=dfg[0])

    def midway3_0(small0):
        small = stacked_small(dict(small0, norm1_g=jnp.zeros((D,), F32)))
        csum = _small_chip_sum(_pack([small[n] for n in SMALL], 32))
        flying["small"] = _exchange_start([csum], [lax.empty((3,) + csum.shape, BF16)], _small_copies, 3, "small_to_chips_start")
        return (flying["small"][-1],)

    def late0(grads):
        tok = sibling_start(grads, first, 0, "0b")
        reduce_adam("1a", tok[0])
        return sibling_finish("0b", out[first[0]][0])

    dxa, big1, small1 = _backward_layer(1, dxb, saved1, midway=lambda grads: sibling_start(grads, rest, 1, "1a"), late=late1)
    dx, big0, small0 = _backward_layer(0, dxa, saved0, after=sibling_finish("1b", dxa), midway=midway0,
                                       midway2=lambda dws: sibling_finish("0a", dws), midway3=midway3_0, late=late0)
    join_tok = reduce_sums(("0a", "0b"), dx)
    small = stacked_small(small0)

    full_shapes = [small[n].shape for n in SMALL]
    send, recv, csum, land, _ = flying["small"]
    csum, land = _exchange_wait(send, recv, csum, land, join_tok[0], _small_copies, "small_to_chips_wait")
    red = _unpack(_small_total(csum[0], land[0]), full_shapes)
    norm1_0 = _allreduce_small(_pack([small0["norm1_g"]], 32))[:D // 128].reshape(D)
    reduce_adam("0a", norm1_0)
    red[SMALL.index("norm1_g")] = red[SMALL.index("norm1_g")].at[0].set(norm1_0)
    g_small = []
    for n, g in zip(SMALL, red):
        if n in CHIP_SHARDED_SMALL:
            g = lax.dynamic_slice_in_dim(g, chip * w[n].shape[-1], w[n].shape[-1], axis=g.ndim - 1)
        g_small.append(g)
    shapes = [w[n].shape for n in SMALL]
    upd = [_unpack(u, shapes) for u in _adam(_pack(g_small, 32), *state_packs, "adam_small")]
    for i, n in enumerate(SMALL):
        out[n] = [g_small[i], upd[0][i], upd[1][i], upd[2][i]]

    return (loss, dx[None]) + tuple(out[n][i] for i in range(4) for n in WEIGHTS)
```

```python
import functools

import jax
import jax.numpy as jnp
from jax import lax
from jax.experimental import pallas as pl
from jax.experimental.pallas import tpu as pltpu

F32 = jnp.float32
BF16 = jnp.bfloat16
MESH = pl.DeviceIdType.MESH

D = 1024
NH = 8
HD = 128
CHUNK = 128
GMLP_ROWS = 512
N_IN_T = 12
DFF = 2816
DFF_SH = 1408
EPS = 1e-6
LRU_C = 8.0
ADAM_LR, ADAM_B1, ADAM_B2, ADAM_EPS, ADAM_WD, ADAM_STEP = 0.001, 0.9, 0.999, 1e-08, 0.01, 10

TM = 512
TM_BIG = 1024
RT = 128
PADR = 8
VMEM_LIMIT = 56 * 1024 * 1024


def _cp(sem=None, **kw):
    if sem is not None:
        kw["dimension_semantics"] = sem
    return pltpu.CompilerParams(vmem_limit_bytes=VMEM_LIMIT, **kw)


_GC = 0.7978845608028654


def _sigmoid(x):
    return 0.5 * jnp.tanh(0.5 * x) + 0.5


_GK = 0.044715


def _gelu(x):
    t = jnp.tanh(x * (_GC + (_GC * _GK) * (x * x)))
    return x * (0.5 + 0.5 * t)


def _gelu_and_grad(x):
    x2 = x * x
    t = jnp.tanh(x * (_GC + (_GC * _GK) * x2))
    h = 0.5 + 0.5 * t
    return x * h, h + x * (1.0 - t * t) * (0.5 * _GC + (1.5 * _GC * _GK) * x2)


def _softplus_neg(lam):
    y = jnp.exp(-jnp.abs(lam))
    u = 1.0 + y
    l1p = jnp.where(u == 1.0, y, jnp.log(u) * y / (u - 1.0))
    return jnp.maximum(-lam, 0.0) + l1p


def _dot(a, b):
    return jnp.dot(a, b, preferred_element_type=F32)


def _dot_nt(a, b):
    return lax.dot_general(a, b, (((1,), (1,)), ((), ())), preferred_element_type=F32)


def _dot_tn(a, b):
    return lax.dot_general(a, b, (((0,), (0,)), ((), ())), preferred_element_type=F32)


def _rms_hat(x):
    r = lax.rsqrt(jnp.mean(x * x, axis=-1, keepdims=True) + EPS)
    return x * r, r


def _rms_bwd(dh, x, g):
    xh, r = _rms_hat(x)
    dxh = dh * g
    dx = r * (dxh - xh * jnp.mean(dxh * xh, axis=-1, keepdims=True))
    return dx, jnp.sum(dh * xh, axis=0, keepdims=True)


def _norm_into(x_ref, g_ref, h_ref):
    xh, _ = _rms_hat(x_ref[...])
    h_ref[...] = (xh * g_ref[...]).astype(BF16)


def _in_tile(j):
    m, hf = j // 2, j % 2
    orig = jnp.where(m < 2, m, jnp.where(m == 2, 4, jnp.where(m < 5, m - 1, 5)))
    t = orig * 2 + hf
    return t // 3, t % 3


ANY = pl.BlockSpec(memory_space=pl.ANY)


def _mm_in(x, g, w_in, l, after=()):
    S = x.shape[0]
    tm = min(2 * TM_BIG, S)

    def body(x_ref, g_ref, w0_ref, w1_ref, *rest):
        o_ref, h_ref = rest[-2:]

        @pl.when(pl.program_id(1) == 0)
        def _():
            _norm_into(x_ref, g_ref, h_ref)
        rp = min(TM, tm)
        for r0 in range(0, tm, rp):
            hv = h_ref[r0:r0 + rp, :]
            o_ref[r0:r0 + rp, 0:512] = _dot(hv, w0_ref[...]).astype(BF16)
            o_ref[r0:r0 + rp, 512:1024] = _dot(hv, w1_ref[...]).astype(BF16)

    def w_tile(hf):
        def w_map(i, m):
            sh, tl = _in_tile(2 * m + hf)
            return (sh, 0, tl)
        return pl.BlockSpec((None, D, 512), w_map)

    return pl.pallas_call(
        body, name=f"mm_in_{l}", grid=(S // tm, 6),
        in_specs=[pl.BlockSpec((tm, D), lambda i, m: (i, 0)), pl.BlockSpec((1, D), lambda i, m: (0, 0)),
                  w_tile(0), w_tile(1)] + [ANY] * len(after),
        out_specs=[pl.BlockSpec((None, tm, D), lambda i, m: (m, i, 0)), pl.BlockSpec((tm, D), lambda i, m: (i, 0))],
        out_shape=[jax.ShapeDtypeStruct((6, S, D), BF16), jax.ShapeDtypeStruct((S, D), BF16)],
        compiler_params=_cp(("parallel", "arbitrary")),
    )(x, g, w_in, w_in, *after)


def _mm_res(a, w, res, l, name, after=()):
    S, K = a.shape

    tm = TM

    def body(a_ref, w_ref, r_ref, *rest):
        rest[-1][...] = r_ref[...] + _dot(a_ref[...], w_ref[...])

    return pl.pallas_call(
        body, name=f"{name}_{l}", grid=(S // tm,),
        in_specs=[pl.BlockSpec((tm, K), lambda i: (i, 0)), pl.BlockSpec((K, D), lambda i: (0, 0)),
                  pl.BlockSpec((tm, D), lambda i: (i, 0))] + [ANY] * len(after),
        out_specs=pl.BlockSpec((tm, D), lambda i: (i, 0)),
        out_shape=jax.ShapeDtypeStruct((S, D), F32),
        compiler_params=_cp(("parallel",)),
    )(a, w, res, *after)


def _mm_ffn_in(x, g, w_fi, l):
    S = x.shape[0]

    tm = min(TM_BIG, S)

    def body(x_ref, g_ref, w_ref, gu_ref, ff_ref, h_ref):
        @pl.when(pl.program_id(1) == 0)
        def _():
            _norm_into(x_ref, g_ref, h_ref)
        for r0 in range(0, tm, TM):
            rows = slice(r0, r0 + TM)
            hv = h_ref[rows, :]
            ga = _dot(hv, w_ref[0])
            gb = _dot(hv, w_ref[1])
            sg = _sigmoid(ga)
            silu = ga * sg
            gu_ref[0, rows, :] = (gb * (sg + silu * (1.0 - sg))).astype(BF16)
            gu_ref[1, rows, :] = silu.astype(BF16)
            ff_ref[rows, :] = (silu * gb).astype(BF16)

    gu, ff, h = pl.pallas_call(
        body, name=f"mm_ffn_in_{l}", grid=(S // tm, 2),
        in_specs=[pl.BlockSpec((tm, D), lambda i, s: (i, 0)), pl.BlockSpec((1, D), lambda i, s: (0, 0)),
                  pl.BlockSpec((2, None, D, DFF_SH), lambda i, s: (0, s, 0, 0))],
        out_specs=[pl.BlockSpec((2, None, tm, DFF_SH), lambda i, s: (0, s, i, 0)),
                   pl.BlockSpec((tm, DFF_SH), lambda i, s: (i, s)),
                   pl.BlockSpec((tm, D), lambda i, s: (i, 0))],
        out_shape=[jax.ShapeDtypeStruct((2, 2, S, DFF_SH), BF16), jax.ShapeDtypeStruct((S, DFF), BF16),
                   jax.ShapeDtypeStruct((S, D), BF16)],
        compiler_params=_cp(("parallel", "arbitrary")),
    )(x, g, w_fi.reshape(2, 2, D, DFF_SH))
    return gu.reshape(4, S, DFF_SH), ff, h


def _gmlp_fwd(z6, ws_b, bs_b, lg, lb):
    S = z6.shape[1]

    ts = min(GMLP_ROWS, S)

    def body(z_ref, ws_ref, bs_ref, lg_ref, lb_ref, o_ref, mix):
        for r0 in range(0, ts, CHUNK):
            rows = slice(r0, r0 + CHUNK)
            gv = _gelu(z_ref[1, rows, :].astype(F32))
            xc = gv - jnp.mean(gv, axis=-1, keepdims=True)
            rs = lax.rsqrt(jnp.mean(xc * xc, axis=-1, keepdims=True) + EPS)
            vb = (xc * rs * lg_ref[...] + lb_ref[...]).astype(BF16)
            for gi in range(NH):
                cs = slice(gi * HD, (gi + 1) * HD)
                mix[rows, cs] = _dot(ws_ref[gi], vb[:, cs])
            o_ref[rows, :] = (_sigmoid(z_ref[2, rows, :].astype(F32)) * _gelu(z_ref[0, rows, :].astype(F32))
                              * (mix[rows, :] + bs_ref[...])).astype(BF16)

    return pl.pallas_call(
        body, name="gmlp_fwd", grid=(S // ts,),
        in_specs=[pl.BlockSpec((3, ts, D), lambda i: (0, i, 0)), pl.BlockSpec((NH, CHUNK, CHUNK), lambda i: (0, 0, 0)),
                  pl.BlockSpec((CHUNK, D), lambda i: (0, 0)), pl.BlockSpec((1, D), lambda i: (0, 0)),
                  pl.BlockSpec((1, D), lambda i: (0, 0))],
        out_specs=pl.BlockSpec((ts, D), lambda i: (i, 0)),
        out_shape=jax.ShapeDtypeStruct((S, D), BF16),
        scratch_shapes=[pltpu.VMEM((ts, D), F32)],
        compiler_params=_cp(("parallel",)),
    )(z6, ws_b, bs_b, lg, lb)


def _row_iota():
    return lax.broadcasted_iota(jnp.int32, (RT, HD), 0)


SUB = 8
UNROLL = 8
GRAD_ROWS = 512


def _scan_up(a, b, carry):
    row = lax.broadcasted_iota(jnp.int32, (SUB, HD), 0)
    masks = [(d, row >= d) for d in (1, 2, 4)]
    c = jnp.broadcast_to(carry, (SUB, HD))
    hs = []
    for j in range(RT // SUB):
        aj, bj = a[SUB * j:SUB * (j + 1)], b[SUB * j:SUB * (j + 1)]
        for d, m in masks:
            bj = bj + aj * jnp.where(m, pltpu.roll(bj, d, 0), 0.0)
            aj = aj * jnp.where(m, pltpu.roll(aj, d, 0), 1.0)
        h = bj + aj * c
        hs.append(h)
        c = jnp.broadcast_to(h[SUB - 1:SUB, :], (SUB, HD))
    return jnp.concatenate(hs, axis=0), hs[-1][SUB - 1:SUB, :]


def _scan_down(a, b, carry):
    row = lax.broadcasted_iota(jnp.int32, (SUB, HD), 0)
    masks = [(d, row < SUB - d) for d in (1, 2, 4)]
    c = jnp.broadcast_to(carry, (SUB, HD))
    hs = []
    for j in reversed(range(RT // SUB)):
        aj, bj = a[SUB * j:SUB * (j + 1)], b[SUB * j:SUB * (j + 1)]
        for d, m in masks:
            bj = bj + aj * jnp.where(m, pltpu.roll(bj, SUB - d, 0), 0.0)
            aj = aj * jnp.where(m, pltpu.roll(aj, SUB - d, 0), 1.0)
        h = bj + aj * c
        hs.append(h)
        c = jnp.broadcast_to(h[0:1, :], (SUB, HD))
    return jnp.concatenate(hs[::-1], axis=0), hs[-1][0:1, :]


def _decay(r, sp_d):
    log_a = -LRU_C * r * sp_d
    a = jnp.exp(log_a)
    return a, jnp.sqrt(jnp.maximum(-jnp.tanh(log_a) * (a * a + 1.0), 0.0))


def _decay_bwd(r, sp_d):
    log_a = -LRU_C * r * sp_d
    a = jnp.exp(log_a)
    m2 = jnp.maximum(-jnp.tanh(log_a) * (a * a + 1.0), 0.0)
    inv = jnp.where(m2 > 0.0, lax.rsqrt(m2), 0.0)
    return a, m2 * inv, inv


def _lru_gates(xc, d, wr_ref, br_ref, wi_ref, bi_ref, sp):
    xb = xc.astype(BF16)
    r = _sigmoid(_dot(xb, wr_ref[d]) + br_ref[d:d + 1, :])
    i = _sigmoid(_dot(xb, wi_ref[d]) + bi_ref[d:d + 1, :])
    a, mult = _decay(r, sp[d:d + 1, :])
    return r, i, a, mult


def _shifted(win, k):
    w = RT + 2 * PADR
    v = win if k == 0 else pltpu.roll(win, (-k) % w, 0)
    return v[PADR:PADR + RT]


def _conv_taps(win):
    return [_shifted(win, k) for k in (-1, 0, 1, 2)]


def _fill_padded(dst, src_ref, S):
    zeros = jnp.zeros((PADR, HD), F32)
    dst[0:PADR, :] = zeros
    dst[PADR + S:2 * PADR + S, :] = zeros

    def cp(i, c):
        t0 = pl.multiple_of(i * RT, RT)
        dst[pl.ds(t0 + PADR, RT), :] = src_ref[pl.ds(t0, RT), :].astype(F32)
        return c
    lax.fori_loop(0, S // RT, cp, 0)


def _conv_fwd_all(zxp, xc_s, cw_ref, cb_ref, S):
    def cv(i, c):
        t0 = pl.multiple_of(i * RT, RT)
        xm1, x0, xp1, xp2 = _conv_taps(zxp[pl.ds(t0, RT + 2 * PADR), :])
        xc_s[pl.ds(t0, RT), :] = (cb_ref[...] + xm1 * cw_ref[0:1, :] + x0 * cw_ref[1:2, :]
                                  + xp1 * cw_ref[2:3, :] + xp2 * cw_ref[3:4, :])
        return c
    lax.fori_loop(0, S // RT, cv, 0)


def _lru_specs(S):
    head = lambda h: (0, h)
    return [pl.BlockSpec((4, HD), head), pl.BlockSpec((1, HD), head),
            pl.BlockSpec((2, None, HD, HD), lambda h: (0, h, 0, 0)), pl.BlockSpec((2, HD), head),
            pl.BlockSpec((2, None, HD, HD), lambda h: (0, h, 0, 0)), pl.BlockSpec((2, HD), head),
            pl.BlockSpec((2, HD), head)]


def _lru_fwd(z6, ya, cw, cb, wr, br, wi, bi, lam):
    S = z6.shape[1]
    nt = S // RT

    def body(z_ref, ya_ref, cw_ref, cb_ref, wr_ref, br_ref, wi_ref, bi_ref, lam_ref, mg_ref, h0_ref, h1_ref, zxp, xc_s):
        sp = _softplus_neg(lam_ref[...])
        _fill_padded(zxp, z_ref.at[0], S)
        _conv_fwd_all(zxp, xc_s, cw_ref, cb_ref, S)

        def scans(i, carry):
            cu, cd = carry
            for u in range(UNROLL):
                j = i * UNROLL + u
                ru = pl.ds(pl.multiple_of(j * RT, RT), RT)
                rd = pl.ds(pl.multiple_of((nt - 1 - j) * RT, RT), RT)
                xu, xd = xc_s[ru, :], xc_s[rd, :]
                _, gi, a, mult = _lru_gates(xu, 0, wr_ref, br_ref, wi_ref, bi_ref, sp)
                hu, cu = _scan_up(a, mult * gi * xu, cu)
                h0_ref[ru, :] = hu
                _, gi, a, mult = _lru_gates(xd, 1, wr_ref, br_ref, wi_ref, bi_ref, sp)
                hd, cd = _scan_down(a, mult * gi * xd, cd)
                h1_ref[rd, :] = hd
            return cu, cd
        z1 = jnp.zeros((1, HD), F32)
        lax.fori_loop(0, nt // UNROLL, scans, (z1, z1))

        def merge(i, c):
            rows = pl.ds(pl.multiple_of(i * RT, RT), RT)
            yb = (h0_ref[rows, :] + h1_ref[rows, :]) * _gelu(z_ref[1, rows, :].astype(F32))
            mg_ref[rows, :] = (ya_ref[rows, :].astype(F32) + _sigmoid(z_ref[2, rows, :].astype(F32)) * yb).astype(BF16)
            return c
        lax.fori_loop(0, nt, merge, 0)

    col = pl.BlockSpec((S, HD), lambda h: (0, h))
    return pl.pallas_call(
        body, name="lru_fwd", grid=(NH,),
        in_specs=[pl.BlockSpec((3, S, HD), lambda h: (1, 0, h)), col] + _lru_specs(S),
        out_specs=[col, col, col],
        out_shape=[jax.ShapeDtypeStruct((S, D), BF16), jax.ShapeDtypeStruct((S, D), F32), jax.ShapeDtypeStruct((S, D), F32)],
        scratch_shapes=[pltpu.VMEM((S + 2 * PADR, HD), F32), pltpu.VMEM((S, HD), F32)],
        compiler_params=_cp(("parallel",)),
    )(z6, ya, cw, cb, wr, br, wi, bi, lam)


def _mm_res_loss(a, w, res, tgt, g):
    S, K = a.shape

    def body(a_ref, w_ref, r_ref, t_ref, g_ref, dx_ref, loss_ref, dg_ref):
        @pl.when(pl.program_id(0) == 0)
        def _():
            loss_ref[...] = jnp.zeros_like(loss_ref)
            dg_ref[...] = jnp.zeros_like(dg_ref)
        xv = r_ref[...] + _dot(a_ref[...], w_ref[...])
        xh, _ = _rms_hat(xv)
        e = xh * g_ref[...] - t_ref[...]
        loss_ref[...] += jnp.sum(e * e) * (0.5 / D)
        dx, dgs = _rms_bwd(e * (1.0 / D), xv, g_ref[...])
        dx_ref[...] = dx
        dg_ref[...] += dgs

    row = pl.BlockSpec((TM, D), lambda i: (i, 0))
    vec = pl.BlockSpec((1, D), lambda i: (0, 0))
    return pl.pallas_call(
        body, name="mm_ffn_out_loss", grid=(S // TM,),
        in_specs=[pl.BlockSpec((TM, K), lambda i: (i, 0)), pl.BlockSpec((K, D), lambda i: (0, 0)), row, row, vec],
        out_specs=[row, pl.BlockSpec((1, 128), lambda i: (0, 0)), vec],
        out_shape=[jax.ShapeDtypeStruct((S, D), F32), jax.ShapeDtypeStruct((1, 128), F32), jax.ShapeDtypeStruct((1, D), F32)],
        compiler_params=_cp(("arbitrary",)),
    )(a, w, res, tgt, g)


def _bwd_ffn_out(dx, w_fo, gu, l, after=()):
    S = dx.shape[0]

    tm = min(TM_BIG, S)

    def body(dx_ref, w_ref, gu_ref, *rest):
        o_ref = rest[-1]
        for r0 in range(0, tm, TM):
            rows = slice(r0, r0 + TM)
            d = _dot_nt(dx_ref[rows, :].astype(BF16), w_ref[...])
            o_ref[0, rows, :] = (d * gu_ref[0, rows, :].astype(F32)).astype(BF16)
            o_ref[1, rows, :] = (d * gu_ref[1, rows, :].astype(F32)).astype(BF16)

    pair = pl.BlockSpec((2, None, tm, DFF_SH), lambda i, s: (0, s, i, 0))
    dgu = pl.pallas_call(
        body, name=f"bwd_ffn_out_{l}", grid=(S // tm, 2),
        in_specs=[pl.BlockSpec((tm, D), lambda i, s: (i, 0)), pl.BlockSpec((DFF_SH, D), lambda i, s: (s, 0)), pair]
        + [ANY] * len(after),
        out_specs=pair,
        out_shape=jax.ShapeDtypeStruct((2, 2, S, DFF_SH), BF16),
        compiler_params=_cp(("parallel", "arbitrary")),
    )(dx, w_fo, gu.reshape(2, 2, S, DFF_SH), *after)
    return dgu.reshape(4, S, DFF_SH)


def _mm_tn(a, b, m_blk, tk, name):
    S, M = a.shape

    def body(a_ref, b_ref, o_ref):
        @pl.when(pl.program_id(1) == 0)
        def _():
            o_ref[...] = jnp.zeros_like(o_ref)
        o_ref[...] += _dot_tn(a_ref[...], b_ref[...].astype(BF16))

    return pl.pallas_call(
        body, name=name, grid=(M // m_blk, S // tk),
        in_specs=[pl.BlockSpec((tk, m_blk), lambda m, k: (k, m)), pl.BlockSpec((tk, D), lambda m, k: (k, 0))],
        out_specs=pl.BlockSpec((m_blk, D), lambda m, k: (m, 0)),
        out_shape=jax.ShapeDtypeStruct((M, D), F32),
        compiler_params=_cp(("parallel", "arbitrary")),
    )(a, b)


def _mm_nt_rms_bwd(a, a_specs, w, w_specs, nk, tm, x, g, dres, name, after=()):
    S = x.shape[0]
    sub = len(a_specs)

    def body(*refs):
        a_refs, w_refs = refs[:sub], refs[sub:2 * sub]
        x_ref, g_ref, r_ref = refs[2 * sub:2 * sub + 3]
        dx_ref, dg_ref, acc = refs[-3:]
        i, k = pl.program_id(0), pl.program_id(1)
        @pl.when(k == 0)
        def _():
            acc[...] = jnp.zeros_like(acc)
        for j in range(sub):
            acc[...] += _dot_nt(a_refs[j][...], w_refs[j][...])

        @pl.when(jnp.logical_and(i == 0, k == 0))
        def _():
            dg_ref[...] = jnp.zeros_like(dg_ref)

        @pl.when(k == nk - 1)
        def _():
            dx, dgs = _rms_bwd(acc[...], x_ref[...], g_ref[...])
            dx_ref[...] = r_ref[...] + dx
            dg_ref[...] += dgs

    row = pl.BlockSpec((tm, D), lambda i, k: (i, 0))
    vec = pl.BlockSpec((1, D), lambda i, k: (0, 0))
    return pl.pallas_call(
        body, name=name, grid=(S // tm, nk),
        in_specs=list(a_specs) + list(w_specs) + [row, vec, row] + [ANY] * len(after),
        out_specs=[row, vec],
        out_shape=[jax.ShapeDtypeStruct((S, D), F32), jax.ShapeDtypeStruct((1, D), F32)],
        scratch_shapes=[pltpu.VMEM((tm, D), F32)],
        compiler_params=_cp(("arbitrary", "arbitrary")),
    )(*[a] * sub, *[w] * sub, x, g, dres, *after)


def _dw_ffn_in(h, dgu, l):
    S = h.shape[0]

    def body(h_ref, b_ref, o_ref):
        @pl.when(pl.program_id(1) == 0)
        def _():
            o_ref[...] = jnp.zeros_like(o_ref)
        o_ref[...] += _dot_tn(h_ref[...], b_ref[...])

    tk = min(2 * TM_BIG, S)
    return pl.pallas_call(
        body, name=f"dw_ffn_in_{l}", grid=(4, S // tk),
        in_specs=[pl.BlockSpec((tk, D), lambda j, k: (k, 0)), pl.BlockSpec((None, tk, DFF_SH), lambda j, k: (j, k, 0))],
        out_specs=pl.BlockSpec((None, D, DFF_SH), lambda j, k: (j, 0, 0)),
        out_shape=jax.ShapeDtypeStruct((4, D, DFF_SH), F32),
        compiler_params=_cp(("parallel", "arbitrary")),
    )(h, dgu)


_HALF_COMPS = ((0, 1, 3), (4, 2, 5))


def _dw_in(h, dz6, l, after=()):
    S = h.shape[0]

    def body(h_ref, d0_ref, d1_ref, d2_ref, *rest):
        o_ref = rest[-1]

        @pl.when(pl.program_id(1) == 0)
        def _():
            o_ref[...] = jnp.zeros_like(o_ref)
        hv = h_ref[...]
        for q, d_ref in enumerate((d0_ref, d1_ref, d2_ref)):
            for hf in range(2):
                col = 1024 * q + 512 * hf
                o_ref[col // 1536, :, col % 1536:col % 1536 + 512] += _dot_tn(hv, d_ref[:, 512 * hf:512 * (hf + 1)])

    tk = min(TM_BIG, S)

    def comp(q):
        return pl.BlockSpec((None, tk, D), lambda p, k: (jnp.where(p == 0, _HALF_COMPS[0][q], _HALF_COMPS[1][q]), k, 0))

    return pl.pallas_call(
        body, name=f"dw_in_{l}", grid=(2, S // tk),
        in_specs=[pl.BlockSpec((tk, D), lambda p, k: (k, 0)), comp(0), comp(1), comp(2)] + [ANY] * len(after),
        out_specs=pl.BlockSpec((2, D, 1536), lambda p, k: (p, 0, 0)),
        out_shape=jax.ShapeDtypeStruct((4, D, 1536), F32),
        compiler_params=_cp(("parallel", "arbitrary")),
    )(h, dz6, dz6, dz6, *after)


def _bwd_out(dx, w_o, merged, l):
    S = dx.shape[0]

    def body(dx_ref, w_ref, m_ref, dm_ref, dw_ref):
        @pl.when(pl.program_id(0) == 0)
        def _():
            dw_ref[...] = jnp.zeros_like(dw_ref)
        dxb = dx_ref[...].astype(BF16)
        dm_ref[...] = _dot_nt(dxb, w_ref[...]).astype(BF16)
        dw_ref[...] += _dot_tn(m_ref[...], dxb)

    tm = TM
    row = pl.BlockSpec((tm, D), lambda i: (i, 0))
    return pl.pallas_call(
        body, name=f"bwd_out_{l}", grid=(S // tm,),
        in_specs=[row, pl.BlockSpec((D, D), lambda i: (0, 0)), row],
        out_specs=[row, pl.BlockSpec((D, D), lambda i: (0, 0))],
        out_shape=[jax.ShapeDtypeStruct((S, D), BF16), jax.ShapeDtypeStruct((D, D), F32)],
        compiler_params=_cp(("arbitrary",)),
    )(dx, w_o, merged)


def _gmlp_bwd(dm, z6, ws_b, wst_b, bs_b, lg, lb, after=()):
    S = z6.shape[1]
    ts = min(GMLP_ROWS, S)

    def body(dm_ref, z_ref, ws_ref, wst_ref, bs_ref, lg_ref, lb_ref, *rest):
        dz_ref, dws_ref, dbs_ref, dlg_ref, dlb_ref, mix, dv = rest[-7:]

        @pl.when(pl.program_id(0) == 0)
        def _():
            dws_ref[...] = jnp.zeros_like(dws_ref)
            dbs_ref[...] = jnp.zeros_like(dbs_ref)
            dlg_ref[...] = jnp.zeros_like(dlg_ref)
            dlb_ref[...] = jnp.zeros_like(dlb_ref)
        for r0 in range(0, ts, CHUNK):
            rows = slice(r0, r0 + CHUNK)
            gv, dgelu_v = _gelu_and_grad(z_ref[1, rows, :].astype(F32))
            xc = gv - jnp.mean(gv, axis=-1, keepdims=True)
            rs = lax.rsqrt(jnp.mean(xc * xc, axis=-1, keepdims=True) + EPS)
            vh = xc * rs
            vb = (vh * lg_ref[...] + lb_ref[...]).astype(BF16)
            for gi in range(NH):
                cs = slice(gi * HD, (gi + 1) * HD)
                mix[rows, cs] = _dot(ws_ref[gi], vb[:, cs])
            u, dgelu_u = _gelu_and_grad(z_ref[0, rows, :].astype(F32))
            sa = _sigmoid(z_ref[2, rows, :].astype(F32))
            dya = dm_ref[rows, :].astype(F32) * sa
            dym = dya * (mix[rows, :] + bs_ref[...])
            dz_ref[2, rows, :] = (dym * u * (1.0 - sa)).astype(BF16)
            dz_ref[0, rows, :] = (dym * dgelu_u).astype(BF16)
            dmix = dya * u
            dmb = dmix.astype(BF16)
            for gi in range(NH):
                cs = slice(gi * HD, (gi + 1) * HD)
                dv[rows, cs] = _dot(wst_ref[gi], dmb[:, cs])
                dws_ref[gi] += _dot_nt(dmb[:, cs], vb[:, cs])
                dbs_ref[gi] += jnp.broadcast_to(jnp.sum(dmix[:, cs], axis=1, keepdims=True), (CHUNK, HD))
            dvv = dv[rows, :]
            dlg_ref[...] += jnp.sum(dvv * vh, axis=0, keepdims=True)
            dlb_ref[...] += jnp.sum(dvv, axis=0, keepdims=True)
            dvh = dvv * lg_ref[...]
            dgv = rs * (dvh - jnp.mean(dvh, axis=-1, keepdims=True) - vh * jnp.mean(dvh * vh, axis=-1, keepdims=True))
            dz_ref[1, rows, :] = (dgv * dgelu_v).astype(BF16)

    vec = pl.BlockSpec((1, D), lambda i: (0, 0))
    mat = pl.BlockSpec((NH, CHUNK, CHUNK), lambda i: (0, 0, 0))
    return pl.pallas_call(
        body, name="gmlp_bwd", grid=(S // ts,),
        in_specs=[pl.BlockSpec((ts, D), lambda i: (i, 0)), pl.BlockSpec((3, ts, D), lambda i: (0, i, 0)), mat, mat,
                  pl.BlockSpec((CHUNK, D), lambda i: (0, 0)), vec, vec] + [ANY] * len(after),
        out_specs=[pl.BlockSpec((3, ts, D), lambda i: (0, i, 0)), mat, mat, vec, vec],
        out_shape=[jax.ShapeDtypeStruct((6, S, D), BF16), jax.ShapeDtypeStruct((NH, CHUNK, CHUNK), F32),
                   jax.ShapeDtypeStruct((NH, CHUNK, HD), F32), jax.ShapeDtypeStruct((1, D), F32), jax.ShapeDtypeStruct((1, D), F32)],
        scratch_shapes=[pltpu.VMEM((ts, D), F32), pltpu.VMEM((ts, D), F32)],
        compiler_params=_cp(("arbitrary",)),
    )(dm, z6, ws_b, wst_b, bs_b, lg, lb, *after)


def _lru_bwd(dz6, dm, z6, h0, h1, cw, cb, wr, br, wi, bi, lam, after=()):
    S = z6.shape[1]
    nt = S // RT

    def body(dz_in, dm_ref, z_ref, h0_ref, h1_ref, cw_ref, cb_ref, wr_ref, br_ref, wi_ref, bi_ref, lam_ref, *rest):
        dz_ref, dcw_ref, dcb_ref, dwr_ref, dbr_ref, dwi_ref, dbi_ref, dlam_ref, zxp, xc_s, dhs_s, dxcp, r_s, lam_s = rest[-14:]
        del dz_in
        lam = lam_ref[...]
        sp = _softplus_neg(lam)
        row = _row_iota()
        _fill_padded(zxp, z_ref.at[0], S)
        _conv_fwd_all(zxp, xc_s, cw_ref, cb_ref, S)
        zeros = jnp.zeros((PADR, HD), F32)
        dxcp[0:PADR, :] = zeros
        dxcp[PADR + S:2 * PADR + S, :] = zeros
        dwr_ref[...] = jnp.zeros_like(dwr_ref)
        dwi_ref[...] = jnp.zeros_like(dwi_ref)

        def pre(i, c):
            rows = pl.ds(pl.multiple_of(i * RT, RT), RT)
            hs = h0_ref[rows, :] + h1_ref[rows, :]
            dmv = dm_ref[rows, :].astype(F32)
            sb = _sigmoid(z_ref[2, rows, :].astype(F32))
            gg, dgg = _gelu_and_grad(z_ref[1, rows, :].astype(F32))
            dz_ref[2, rows, :] = (dmv * hs * gg * sb * (1.0 - sb)).astype(BF16)
            dyb = dmv * sb
            dz_ref[1, rows, :] = (dyb * hs * dgg).astype(BF16)
            dhs_s[rows, :] = dyb * gg
            return c
        lax.fori_loop(0, nt, pre, 0)

        def gate_bwd(d, gates, lamv, da, xc):
            r, gi, a, mult, inv_mult = gates
            lx, lm = lamv * xc, lamv * mult
            dlog_r = (da - (lx * gi) * (a * inv_mult)) * a * r
            dpr = dlog_r * (1.0 - r) * (-LRU_C * sp[d:d + 1, :])
            dpi = (lx * mult) * gi * (1.0 - gi)
            xb, dprb, dpib = xc.astype(BF16), dpr.astype(BF16), dpi.astype(BF16)
            dwr_ref[d] += _dot_tn(xb, dprb)
            dwi_ref[d] += _dot_tn(xb, dpib)
            dxc = lm * gi + _dot_nt(dprb, wr_ref[d]) + _dot_nt(dpib, wi_ref[d])
            return dxc, (jnp.sum(dlog_r, axis=0, keepdims=True) * (-LRU_C), jnp.sum(dpr, axis=0, keepdims=True),
                         jnp.sum(dpi, axis=0, keepdims=True))

        def rgates(i, c):
            for u in range(UNROLL):
                rows = pl.ds(pl.multiple_of((i * UNROLL + u) * RT, RT), RT)
                xb = xc_s[rows, :].astype(BF16)
                for d in range(2):
                    r_s[d, rows, :] = _sigmoid(_dot(xb, wr_ref[d]) + br_ref[d:d + 1, :])
            return c
        lax.fori_loop(0, nt // UNROLL, rgates, 0)

        def chains(i, carry):
            qn, qp = carry
            for u in range(UNROLL):
                j = i * UNROLL + u
                rd = pl.ds(pl.multiple_of((nt - 1 - j) * RT, RT), RT)
                a, dhs = _decay(r_s[0, rd, :], sp[0:1, :])[0], dhs_s[rd, :]
                q, q_first = _scan_down(a, a * dhs, qn)
                lam_s[0, rd, :] = dhs + jnp.where(row == RT - 1, qn, pltpu.roll(q, RT - 1, 0))
                qn = q_first
                ru = pl.ds(pl.multiple_of(j * RT, RT), RT)
                a, dhs = _decay(r_s[1, ru, :], sp[1:2, :])[0], dhs_s[ru, :]
                q, q_last = _scan_up(a, a * dhs, qp)
                lam_s[1, ru, :] = dhs + jnp.where(row == 0, qp, pltpu.roll(q, 1, 0))
                qp = q_last
            return qn, qp

        z1 = jnp.zeros((1, HD), F32)
        lax.fori_loop(0, nt // UNROLL, chains, (z1, z1))

        ct = min(GRAD_ROWS, S)
        crow = lax.broadcasted_iota(jnp.int32, (ct, HD), 0)

        def tile_grads(i, acc):
            t0 = pl.multiple_of(i * ct, ct)
            rows = pl.ds(t0, ct)
            xc = xc_s[rows, :]
            xb = xc.astype(BF16)
            tp = pl.multiple_of(jnp.maximum(t0 - PADR, 0), PADR)
            prev = jnp.where(t0 > 0, h0_ref[pl.ds(tp, PADR), :][PADR - 1:PADR, :], 0.0)
            tn = pl.multiple_of(jnp.minimum(t0 + ct, S - PADR), PADR)
            nxt = jnp.where(t0 + ct < S, h1_ref[pl.ds(tn, PADR), :][0:1, :], 0.0)
            hside = (jnp.where(crow == 0, prev, pltpu.roll(h0_ref[rows, :], 1, 0)),
                     jnp.where(crow == ct - 1, nxt, pltpu.roll(h1_ref[rows, :], ct - 1, 0)))
            dxc, sums = 0.0, ()
            for d in range(2):
                r = r_s[d, rows, :]
                gi = _sigmoid(_dot(xb, wi_ref[d]) + bi_ref[d:d + 1, :])
                lamv = lam_s[d, rows, :]
                dxc_d, s_d = gate_bwd(d, (r, gi) + _decay_bwd(r, sp[d:d + 1, :]), lamv, lamv * hside[d], xc)
                dxc = dxc + dxc_d
                sums = sums + s_d
            dxcp[pl.ds(t0 + PADR, ct), :] = dxc
            return tuple(x + y for x, y in zip(acc, sums))

        s_sp0, s_br0, s_bi0, s_sp1, s_br1, s_bi1 = lax.fori_loop(0, S // ct, tile_grads, (z1,) * 6)

        dsp = jnp.concatenate([s_sp0, s_sp1], axis=0)
        dlam_ref[...] = -dsp * _sigmoid(-lam)
        dbr_ref[...] = jnp.concatenate([s_br0, s_br1], axis=0)
        dbi_ref[...] = jnp.concatenate([s_bi0, s_bi1], axis=0)

        def conv_bwd(i, carry):
            c0, c1, c2, c3, cb_ = carry
            t0 = pl.multiple_of(i * RT, RT)
            dwin = dxcp[pl.ds(t0, RT + 2 * PADR), :]
            d0 = _shifted(dwin, 0)
            dz_ref[0, pl.ds(t0, RT), :] = (_shifted(dwin, 1) * cw_ref[0:1, :] + d0 * cw_ref[1:2, :]
                                           + _shifted(dwin, -1) * cw_ref[2:3, :] + _shifted(dwin, -2) * cw_ref[3:4, :]).astype(BF16)
            xm1, x0, xp1, xp2 = _conv_taps(zxp[pl.ds(t0, RT + 2 * PADR), :])
            sm = lambda v: jnp.sum(v, axis=0, keepdims=True)
            return c0 + sm(d0 * xm1), c1 + sm(d0 * x0), c2 + sm(d0 * xp1), c3 + sm(d0 * xp2), cb_ + sm(d0)

        c0, c1, c2, c3, cb_ = lax.fori_loop(0, nt, conv_bwd, (z1, z1, z1, z1, z1))
        dcw_ref[...] = jnp.concatenate([c0, c1, c2, c3], axis=0)
        dcb_ref[...] = cb_

    col = pl.BlockSpec((S, HD), lambda h: (0, h))
    head = lambda h: (0, h)
    wspec = pl.BlockSpec((2, None, HD, HD), lambda h: (0, h, 0, 0))
    return pl.pallas_call(
        body, name="lru_bwd", grid=(NH,),
        in_specs=[pl.BlockSpec(memory_space=pl.ANY), col, pl.BlockSpec((3, S, HD), lambda h: (1, 0, h)), col, col] + _lru_specs(S)
        + [ANY] * len(after),
        out_specs=[pl.BlockSpec((3, S, HD), lambda h: (1, 0, h)), pl.BlockSpec((4, HD), head), pl.BlockSpec((1, HD), head),
                   wspec, pl.BlockSpec((2, HD), head), wspec, pl.BlockSpec((2, HD), head), pl.BlockSpec((2, HD), head)],
        out_shape=[jax.ShapeDtypeStruct((6, S, D), BF16), jax.ShapeDtypeStruct((4, D), F32), jax.ShapeDtypeStruct((1, D), F32),
                   jax.ShapeDtypeStruct((2, NH, HD, HD), F32), jax.ShapeDtypeStruct((2, D), F32),
                   jax.ShapeDtypeStruct((2, NH, HD, HD), F32), jax.ShapeDtypeStruct((2, D), F32), jax.ShapeDtypeStruct((2, D), F32)],
        scratch_shapes=[pltpu.VMEM((S + 2 * PADR, HD), F32), pltpu.VMEM((S, HD), F32), pltpu.VMEM((S, HD), F32),
                        pltpu.VMEM((S + 2 * PADR, HD), F32), pltpu.VMEM((2, S, HD), F32), pltpu.VMEM((2, S, HD), F32)],
        input_output_aliases={0: 0},
        compiler_params=_cp(("parallel",)),
    )(dz6, dm, z6, h0, h1, cw, cb, wr, br, wi, bi, lam, *after)


LAYER_SMALL = ("norm1_g", "gmlp_ln_g", "gmlp_ln_b", "gmlp_w_s", "gmlp_b_s", "conv_w", "conv_b",
               "lru_w_r", "lru_b_r", "lru_w_i", "lru_b_i", "lru_lambda", "norm2_g")


def _layer_operands(l, p):
    ws_b = p["gmlp_w_s"][l].astype(BF16)
    tm = dict(ws_b=ws_b, wst_b=jnp.swapaxes(ws_b, 1, 2), bs_b=jnp.repeat(p["gmlp_b_s"][l].T, HD, axis=1),
              lg=p["gmlp_ln_g"][l][None], lb=p["gmlp_ln_b"][l][None])
    lru = (p["conv_w"][l], p["conv_b"][l][None], p["lru_w_r"][l].astype(BF16), p["lru_b_r"][l],
           p["lru_w_i"][l].astype(BF16), p["lru_b_i"][l], p["lru_lambda"][l])
    return (p["norm1_g"][l][None], p["norm2_g"][l][None]), tm, lru


def _forward_layer(l, x, p, wb, after=(), rest=None, near_end=None, operands=None, loss=None):
    (g1, g2), tm, lru = _layer_operands(l, p) if operands is None else operands
    z6, hn1 = _mm_in(x, g1, wb["w_in"], l, after)
    ya = _gmlp_fwd(z6, tm["ws_b"], tm["bs_b"], tm["lg"], tm["lb"])
    merged, h0, h1 = _lru_fwd(z6, ya, *lru)
    if rest is not None:
        wb = dict(wb, **rest(merged))
    x1 = _mm_res(merged, wb["w_out"], x, l, "mm_out")
    gu, ff, hn2 = _mm_ffn_in(x1, g2, wb["w_ffn_in"], l)
    if loss is None:
        x2 = _mm_res(ff, wb["w_ffn_out"], x1, l, "mm_ffn_out", () if near_end is None else tuple(near_end(gu)))
    else:
        x2 = _mm_res_loss(ff, wb["w_ffn_out"], x1, *loss)
    return x2, dict(x=x, z6=z6, h0=h0, h1=h1, merged=merged, x1=x1, gu=gu, ff=ff, g1=g1, g2=g2, tm=tm, lru=lru,
                    hn1=hn1, hn2=hn2, wb=wb)


def _backward_layer(l, dx, s, after=(), midway=None, midway2=None, midway3=None, late=None):
    S = dx.shape[0]
    tm, wb = s["tm"], s["wb"]
    g2 = s["g2"]
    dgu = _bwd_ffn_out(dx, wb["w_ffn_out"], s["gu"], l, after)
    tmb = min(TM_BIG, S)
    dwfo = _mm_tn(s["ff"], dx, DFF_SH, tmb, f"dw_ffn_out_{l}")
    dx1, dg2 = _mm_nt_rms_bwd(
        dgu, [pl.BlockSpec((None, tmb, DFF_SH), lambda i, k: (k, i, 0))],
        wb["w_ffn_in"], [pl.BlockSpec((None, D, DFF_SH), lambda i, k: (k, 0, 0))],
        4, tmb, s["x1"], g2, dx, f"bwd_ffn_in_{l}")
    dwfi = _dw_ffn_in(s["hn2"], dgu, l)
    dmg, dwo = _bwd_out(dx1, wb["w_out"], s["merged"], l)
    mid = () if midway is None else tuple(midway([dwo, dwfi, dwfo]))
    dz6, dws, dbs, dlg, dlb = _gmlp_bwd(dmg, s["z6"], tm["ws_b"], tm["wst_b"], tm["bs_b"], tm["lg"], tm["lb"], mid)
    mid2 = () if midway2 is None else tuple(midway2(dws))
    dz6, dcw, dcb, dwr, dbr, dwi, dbi, dlam = _lru_bwd(dz6, dmg, s["z6"], s["h0"], s["h1"], *s["lru"], after=mid2)

    sub = 3

    def dz_tile(j):
        return pl.BlockSpec((None, tmb, 512), lambda i, k: ((sub * k + j) // 2, i, (sub * k + j) % 2))

    def w_tile(j):
        def w_map(i, k):
            sh, tl = _in_tile(sub * k + j)
            return (sh, 0, tl)
        return pl.BlockSpec((None, D, 512), w_map)

    small = dict(gmlp_ln_g=dlg[0], gmlp_ln_b=dlb[0], gmlp_w_s=dws, gmlp_b_s=dbs[:, :, 0], conv_w=dcw, conv_b=dcb[0],
                 lru_w_r=dwr, lru_b_r=dbr, lru_w_i=dwi, lru_b_i=dbi, lru_lambda=dlam, norm2_g=dg2[0])
    mid3 = () if midway3 is None else tuple(midway3(small))
    dwin = _dw_in(s["hn1"], dz6, l, mid3)
    tail = () if late is None else tuple(late([dwin]))
    dx0, dg1 = _mm_nt_rms_bwd(
        dz6, [dz_tile(j) for j in range(sub)], wb["w_in"], [w_tile(j) for j in range(sub)],
        N_IN_T // sub, tmb, s["x"], s["g1"], dx1, f"bwd_in_{l}", tail)
    return dx0, [dwin, dwo, dwfi, dwfo], dict(small, norm1_g=dg1[0])


def _local_step(x, tgt, p, wbs):
    saved = []
    for l in range(2):
        x, s = _forward_layer(l, x, p, wbs[l], loss=(tgt, p["final_g"][None]) if l else None)
        saved.append(s)
    dx, loss_v, dfg = x
    big, smalls = [None, None], [None, None]
    for l in (1, 0):
        dx, big[l], smalls[l] = _backward_layer(l, dx, saved[l])
    small = {k: jnp.stack([smalls[0][k], smalls[1][k]]) for k in LAYER_SMALL}
    small["final_g"] = dfg[0]
    return loss_v, dx, big, small


def _place():
    x, y, c = lax.axis_index("x"), lax.axis_index("y"), lax.axis_index("c")
    return x, y, c, 2 * x + y


def _chip_at(x, y, d):
    px = 1 - x if d & 2 else x
    py = 1 - y if d & 1 else y
    return px, py, 2 * px + py


HBM = pl.BlockSpec(memory_space=pltpu.HBM)
SEM = pl.BlockSpec(memory_space=pltpu.SEMAPHORE)
DATAFLOW = pltpu.SideEffectType.DATAFLOW_SIDE_EFFECTING


def _in_hbm(a):
    return pltpu.with_memory_space_constraint(a, pltpu.HBM)


def _cast_into(wfs, l, chip_arr, name):
    n = len(wfs)

    def body(ch_ref, *refs):
        for w_ref, o_ref in zip(refs[:n], refs[n:]):
            o_ref[...] = w_ref[...].astype(BF16)

    halves = [(wf.shape[1] // 2, wf.shape[2]) for wf in wfs]
    return pl.pallas_call(
        body, name=name, out_shape=[jax.ShapeDtypeStruct((4, 2, rh, cols), BF16) for rh, cols in halves],
        grid_spec=pltpu.PrefetchScalarGridSpec(
            num_scalar_prefetch=1, grid=(2,),
            in_specs=[pl.BlockSpec((None, None, rh, cols), lambda h, ch: (l, h, 0, 0)) for rh, cols in halves],
            out_specs=[pl.BlockSpec((None, None, rh, cols), lambda h, ch: (ch[0], h, 0, 0)) for rh, cols in halves]),
        compiler_params=_cp(("parallel",)),
    )(chip_arr, *[wf.reshape(2, 2, rh, cols) for wf, (rh, cols) in zip(wfs, halves)])


def _half_block(ref, chip, half, to, send_sem, recv_sem):
    blk = ref.at[chip, half]
    return pltpu.make_async_remote_copy(src_ref=blk, dst_ref=blk, send_sem=send_sem, recv_sem=recv_sem,
                                        device_id=to, device_id_type=MESH)


def _gather_weights(bufs, tiny):
    nt = len(bufs)
    n_ici = max(nt * 3, 1)

    def body(*refs):
        tiny_ref = refs[nt]
        o_refs, tiny_o = refs[nt + 1:2 * nt + 1], refs[2 * nt + 1]
        send, recv, fsend, frecv, tsend, trecv, lsem = refs[2 * nt + 2:]
        x, y, c, chip = _place()
        local = pltpu.make_async_copy(tiny_ref, tiny_o.at[chip], lsem)
        local.start()

        def tin(d, origin_chip, to):
            return pltpu.make_async_remote_copy(
                src_ref=tiny_ref, dst_ref=tiny_o.at[origin_chip], send_sem=tsend.at[d - 1], recv_sem=trecv.at[d - 1],
                device_id=to, device_id_type=MESH)

        sends = []
        for t in range(nt):
            for d in (1, 2, 3):
                px, py, _ = _chip_at(x, y, d)
                sends.append(_half_block(o_refs[t], chip, c, (px, py, c), send.at[3 * t + d - 1], recv.at[3 * t + d - 1]))
        for d in (1, 2, 3):
            px, py, _ = _chip_at(x, y, d)
            sends.append(tin(d, chip, (px, py, c)))
        for cp in sends:
            cp.start()
        passed = []
        for t in range(nt):
            for d in (1, 2, 3):
                k = 3 * t + d - 1
                _, _, pchip = _chip_at(x, y, d)
                _half_block(o_refs[t], pchip, c, (x, y, c), send.at[k], recv.at[k]).wait_recv()
                f = _half_block(o_refs[t], pchip, c, (x, y, 1 - c), fsend.at[k], frecv.at[k])
                f.start()
                passed.append(f)
        for t in range(nt):
            for d in (1, 2, 3):
                k = 3 * t + d - 1
                _, _, pchip = _chip_at(x, y, d)
                _half_block(o_refs[t], pchip, 1 - c, (x, y, 1 - c), fsend.at[k], frecv.at[k]).wait_recv()
        for d in (1, 2, 3):
            _, _, pchip = _chip_at(x, y, d)
            tin(d, pchip, (x, y, c)).wait_recv()
        for cp in sends + passed:
            cp.wait_send()
        local.wait()

    out_shape = [jax.ShapeDtypeStruct(b.shape, b.dtype) for b in bufs]
    out_shape.append(jax.ShapeDtypeStruct((4,) + tiny.shape, tiny.dtype))
    outs = pl.pallas_call(
        body, name="gather_weights_0", out_shape=out_shape,
        in_specs=[ANY] * (nt + 1), out_specs=[ANY] * (nt + 1),
        scratch_shapes=[pltpu.SemaphoreType.DMA((n_ici,)), pltpu.SemaphoreType.DMA((n_ici,)),
                        pltpu.SemaphoreType.DMA((n_ici,)), pltpu.SemaphoreType.DMA((n_ici,)),
                        pltpu.SemaphoreType.DMA((3,)), pltpu.SemaphoreType.DMA((3,)), pltpu.SemaphoreType.DMA],
        input_output_aliases={t: t for t in range(nt)},
        compiler_params=_cp(has_side_effects=True),
    )(*bufs, tiny)
    return outs[:nt], outs[nt]


def _gather_start(bufs, tag, after=()):
    nt, na = len(bufs), len(after)

    def body(*refs):
        b_refs = refs[:nt]
        send, recv = refs[nt + na], refs[nt + na + 1]
        token = refs[2 * nt + na + 2]
        x, y, c, chip = _place()
        for t in range(nt):
            for d in (1, 2, 3):
                px, py, _ = _chip_at(x, y, d)
                _half_block(b_refs[t], chip, c, (px, py, c), send.at[3 * t + d - 1], recv.at[3 * t + d - 1]).start()
        token[...] = jnp.zeros_like(token)

    outs = pl.pallas_call(
        body, name=f"gather_start_{tag}",
        out_shape=(pltpu.SemaphoreType.DMA((3 * nt,)), pltpu.SemaphoreType.DMA((3 * nt,)),
                   *[pltpu.HBM(b.shape, b.dtype) for b in bufs], jax.ShapeDtypeStruct((8, 128), F32)),
        in_specs=[HBM] * nt + [ANY] * na, out_specs=(SEM, SEM, *[HBM] * nt, pl.BlockSpec(memory_space=pltpu.VMEM)),
        input_output_aliases={t: 2 + t for t in range(nt)},
        compiler_params=pltpu.CompilerParams(has_side_effects=DATAFLOW),
    )(*[_in_hbm(b) for b in bufs], *after)
    return outs[0], outs[1], list(outs[2:2 + nt]), outs[2 + nt]


def _gather_wait(send, recv, bufs, after, tag):
    nt = len(bufs)

    def body(*refs):
        b_refs = refs[:nt]
        send_ref, recv_ref = refs[nt], refs[nt + 1]
        x, y, c, chip = _place()
        for t in range(nt):
            for d in (1, 2, 3):
                k = 3 * t + d - 1
                px, py, pchip = _chip_at(x, y, d)
                _half_block(b_refs[t], chip, c, (px, py, c), send_ref.at[k], recv_ref.at[k]).wait_send()
                _half_block(b_refs[t], pchip, c, (px, py, c), send_ref.at[k], recv_ref.at[k]).wait_recv()

    after = tuple(after) if isinstance(after, (tuple, list)) else (after,)
    outs = pl.pallas_call(
        body, name=f"gather_wait_{tag}", out_shape=[pltpu.HBM(b.shape, b.dtype) for b in bufs],
        in_specs=[HBM] * nt + [SEM, SEM] + [ANY] * len(after), out_specs=[HBM] * nt,
        input_output_aliases={t: t for t in range(nt)},
        compiler_params=pltpu.CompilerParams(has_side_effects=DATAFLOW),
    )(*bufs, send, recv, *after)
    return list(outs)


def _gather_pass_on(bufs, tag):
    nt = len(bufs)

    def body(*refs):
        o_refs = refs[nt:2 * nt]
        fsend, frecv = refs[2 * nt:]
        x, y, c, _ = _place()
        cps = []
        for t in range(nt):
            for d in (1, 2, 3):
                k = 3 * t + d - 1
                _, _, pchip = _chip_at(x, y, d)
                cps.append(_half_block(o_refs[t], pchip, c, (x, y, 1 - c), fsend.at[k], frecv.at[k]))
        for cp in cps:
            cp.start()
        for t in range(nt):
            for d in (1, 2, 3):
                k = 3 * t + d - 1
                _, _, pchip = _chip_at(x, y, d)
                _half_block(o_refs[t], pchip, 1 - c, (x, y, 1 - c), fsend.at[k], frecv.at[k]).wait_recv()
        for cp in cps:
            cp.wait_send()

    return pl.pallas_call(
        body, name=f"gather_pass_on_{tag}", out_shape=[jax.ShapeDtypeStruct(b.shape, b.dtype) for b in bufs],
        in_specs=[ANY] * nt, out_specs=[ANY] * nt,
        scratch_shapes=[pltpu.SemaphoreType.DMA((3 * nt,)), pltpu.SemaphoreType.DMA((3 * nt,))],
        input_output_aliases={t: t for t in range(nt)},
        compiler_params=_cp(has_side_effects=True),
    )(*bufs)


def _chip_copy(c_ref, land_ref, x, y, c, d, send_sem, recv_sem):
    px, py, pchip = _chip_at(x, y, d)
    return pltpu.make_async_remote_copy(src_ref=c_ref.at[pchip], dst_ref=land_ref.at[d - 1], send_sem=send_sem, recv_sem=recv_sem,
                                        device_id=(px, py, c), device_id_type=MESH)


def _exchange_start(srcs, lands, copies, nsem, name):
    ns, n = len(srcs), len(srcs) + len(lands)

    def body(*refs):
        for cp in copies(refs[:ns], refs[ns:n], refs[n], refs[n + 1]):
            cp.start()
        token = refs[2 * n + 2]
        token[...] = jnp.zeros_like(token)

    outs = pl.pallas_call(
        body, name=name,
        out_shape=(pltpu.SemaphoreType.DMA((nsem,)), pltpu.SemaphoreType.DMA((nsem,)),
                   *[pltpu.HBM(a.shape, a.dtype) for a in list(srcs) + list(lands)], jax.ShapeDtypeStruct((8, 128), F32)),
        in_specs=[HBM] * n, out_specs=(SEM, SEM, *[HBM] * n, pl.BlockSpec(memory_space=pltpu.VMEM)),
        input_output_aliases={i: 2 + i for i in range(n)},
        compiler_params=pltpu.CompilerParams(has_side_effects=DATAFLOW),
    )(*[_in_hbm(a) for a in list(srcs) + list(lands)])
    return outs[0], outs[1], list(outs[2:2 + ns]), list(outs[2 + ns:2 + n]), outs[2 + n]


def _exchange_wait(send, recv, srcs, lands, after, copies, name):
    ns, n = len(srcs), len(srcs) + len(lands)

    def body(*refs):
        for cp in copies(refs[:ns], refs[ns:n], refs[n], refs[n + 1]):
            cp.wait_send()
            cp.wait_recv()

    outs = pl.pallas_call(
        body, name=name, out_shape=[pltpu.HBM(a.shape, a.dtype) for a in list(srcs) + list(lands)],
        in_specs=[HBM] * n + [SEM, SEM, ANY], out_specs=[HBM] * n,
        input_output_aliases={i: i for i in range(n)},
        compiler_params=pltpu.CompilerParams(has_side_effects=DATAFLOW),
    )(*srcs, *lands, send, recv, after)
    return list(outs[:ns]), list(outs[ns:])


def _pass_on_copies(b_refs, land_refs, send, recv):
    del land_refs
    x, y, c, _ = _place()
    return [_half_block(b_refs[t], _chip_at(x, y, d)[2], c, (x, y, 1 - c), send.at[3 * t + d - 1], recv.at[3 * t + d - 1])
            for t in range(len(b_refs)) for d in (1, 2, 3)]


def _chips_copies(c_refs, land_refs, send, recv):
    x, y, c, _ = _place()
    return [_chip_copy(c_refs[t], land_refs[t], x, y, c, d, send.at[3 * t + d - 1], recv.at[3 * t + d - 1])
            for t in range(len(c_refs)) for d in (1, 2, 3)]


def _sibling_copies(g_refs, land_refs, send, recv):
    x, y, c, _ = _place()
    return [pltpu.make_async_remote_copy(
        src_ref=g_refs[t].at[k, 1 - c], dst_ref=land_refs[t].at[k], send_sem=send.at[4 * t + k], recv_sem=recv.at[4 * t + k],
        device_id=(x, y, 1 - c), device_id_type=MESH) for t in range(len(g_refs)) for k in range(4)]


def _join_copies(f_refs, land_refs, send, recv):
    del land_refs
    x, y, c, _ = _place()
    return [pltpu.make_async_remote_copy(
        src_ref=f_refs[t].at[c], dst_ref=f_refs[t].at[c], send_sem=send.at[t], recv_sem=recv.at[t],
        device_id=(x, y, 1 - c), device_id_type=MESH) for t in range(len(f_refs))]


def _add_half(gs, rs, c_arr, name):
    n = len(gs)

    def body(c_ref, *refs):
        for g_ref, r_ref, o_ref in zip(refs[:n], refs[n:2 * n], refs[2 * n:]):
            o_ref[...] = (g_ref[...] + r_ref[...]).astype(BF16)

    def own(g):
        return pl.BlockSpec((None, None) + g.shape[2:], lambda k, cr: (k, cr[0], 0, 0))

    def blk(g):
        return pl.BlockSpec((None,) + g.shape[2:], lambda k, cr: (k, 0, 0))

    return pl.pallas_call(
        body, name=name, out_shape=[jax.ShapeDtypeStruct((4,) + g.shape[2:], BF16) for g in gs],
        grid_spec=pltpu.PrefetchScalarGridSpec(
            num_scalar_prefetch=1, grid=(4,),
            in_specs=[own(g) for g in gs] + [blk(g) for g in gs], out_specs=[blk(g) for g in gs]),
        compiler_params=_cp(("parallel",)),
    )(c_arr, *gs, *rs)


def _sum_chips(css, r3s, place_arr, name):
    n = len(css)

    def body(pl_ref, *refs):
        up = lambda ref: ref[...].astype(F32)
        for t in range(n):
            a_ref, (r0_ref, r1_ref, r2_ref), o_ref = refs[t], refs[n + 3 * t:n + 3 * t + 3], refs[4 * n + t]
            o_ref[...] = ((up(a_ref) + up(r0_ref)) + up(r1_ref)) + up(r2_ref)

    def blk(cs, first):
        _, rh, cols = cs.shape
        return pl.BlockSpec((None, rh // 2, cols), lambda i, pa: (first(pa), i, 0))

    in_specs = [blk(cs, lambda pa: pa[0]) for cs in css]
    for cs in css:
        in_specs += [blk(cs, lambda pa, d=d: d) for d in range(3)]
    return pl.pallas_call(
        body, name=name, out_shape=[jax.ShapeDtypeStruct((2,) + cs.shape[1:], F32) for cs in css],
        grid_spec=pltpu.PrefetchScalarGridSpec(
            num_scalar_prefetch=1, grid=(2,), in_specs=in_specs, out_specs=[blk(cs, lambda pa: pa[1]) for cs in css]),
        compiler_params=_cp(("parallel",)),
    )(place_arr, *css, *[r3 for r3 in r3s for _ in range(3)])


def _allreduce_small(pack):
    rows = pack.shape[0]
    hr = rows // 2

    def body(p_ref, o_ref, sib, slots, s1, r1, s2, r2, s3, r3):
        x, y, c, chip = _place()
        sibling = (x, y, 1 - c)
        ex = pltpu.make_async_remote_copy(src_ref=p_ref, dst_ref=sib, send_sem=s1, recv_sem=r1,
                                          device_id=sibling, device_id_type=MESH)
        ex.start()
        ex.wait()
        half = pl.ds(pl.multiple_of(c * hr, 16), hr)
        slots[0] = (p_ref[half, :] + sib[half, :]).astype(BF16)
        cps = []
        for d in (1, 2, 3):
            px, py, _ = _chip_at(x, y, d)
            cps.append(pltpu.make_async_remote_copy(
                src_ref=slots.at[0], dst_ref=slots.at[d], send_sem=s2.at[d - 1], recv_sem=r2.at[d - 1],
                device_id=(px, py, c), device_id_type=MESH))
        for cp in cps:
            cp.start()
        for cp in cps:
            cp.wait()
        tot = slots[chip].astype(F32)
        for k in (1, 2, 3):
            tot = tot + slots[jnp.bitwise_xor(chip, k)].astype(F32)
        o_ref[half, :] = tot
        back = pltpu.make_async_remote_copy(src_ref=o_ref.at[half, :], dst_ref=o_ref.at[half, :], send_sem=s3, recv_sem=r3,
                                            device_id=sibling, device_id_type=MESH)
        back.start()
        back.wait()

    vm = pl.BlockSpec(memory_space=pltpu.VMEM)
    return pl.pallas_call(
        body, name="allreduce_small", out_shape=jax.ShapeDtypeStruct((rows, 128), F32),
        in_specs=[vm], out_specs=vm,
        scratch_shapes=[pltpu.VMEM((rows, 128), F32), pltpu.VMEM((4, hr, 128), BF16),
                        pltpu.SemaphoreType.DMA, pltpu.SemaphoreType.DMA, pltpu.SemaphoreType.DMA((3,)), pltpu.SemaphoreType.DMA((3,)),
                        pltpu.SemaphoreType.DMA, pltpu.SemaphoreType.DMA],
        compiler_params=_cp(has_side_effects=True),
    )(pack)


def _small_chip_sum(pack, after=()):
    rows = pack.shape[0]
    hr = rows // 2

    def body(p_ref, *rest):
        o_ref, sib, s1, r1 = rest[-4:]
        x, y, c, _ = _place()
        ex = pltpu.make_async_remote_copy(src_ref=p_ref, dst_ref=sib, send_sem=s1, recv_sem=r1,
                                          device_id=(x, y, 1 - c), device_id_type=MESH)
        ex.start()
        ex.wait()
        half = pl.ds(pl.multiple_of(c * hr, 16), hr)
        o_ref[...] = (p_ref[half, :] + sib[half, :]).astype(BF16)

    vm = pl.BlockSpec(memory_space=pltpu.VMEM)
    return pl.pallas_call(
        body, name="small_chip_sum", out_shape=jax.ShapeDtypeStruct((hr, 128), BF16),
        in_specs=[vm] + [ANY] * len(after), out_specs=vm,
        scratch_shapes=[pltpu.VMEM((rows, 128), F32), pltpu.SemaphoreType.DMA, pltpu.SemaphoreType.DMA],
        compiler_params=_cp(has_side_effects=True),
    )(pack, *after)


def _small_copies(c_refs, land_refs, send, recv):
    x, y, c, _ = _place()
    cps = []
    for d in (1, 2, 3):
        px, py, _ = _chip_at(x, y, d)
        cps.append(pltpu.make_async_remote_copy(src_ref=c_refs[0], dst_ref=land_refs[0].at[d - 1], send_sem=send.at[d - 1],
                                                recv_sem=recv.at[d - 1], device_id=(px, py, c), device_id_type=MESH))
    return cps


def _small_total(csum, land):
    hr = csum.shape[0]

    def body(c_ref, l_ref, o_ref, slots, s3, r3):
        x, y, c, chip = _place()
        slots[0] = c_ref[...]
        for d in (1, 2, 3):
            slots[d] = l_ref[d - 1]
        tot = slots[chip].astype(F32)
        for k in (1, 2, 3):
            tot = tot + slots[jnp.bitwise_xor(chip, k)].astype(F32)
        half = pl.ds(pl.multiple_of(c * hr, 16), hr)
        o_ref[half, :] = tot
        back = pltpu.make_async_remote_copy(src_ref=o_ref.at[half, :], dst_ref=o_ref.at[half, :], send_sem=s3, recv_sem=r3,
                                            device_id=(x, y, 1 - c), device_id_type=MESH)
        back.start()
        back.wait()

    vm = pl.BlockSpec(memory_space=pltpu.VMEM)
    return pl.pallas_call(
        body, name="small_total", out_shape=jax.ShapeDtypeStruct((2 * hr, 128), F32), in_specs=[vm, vm], out_specs=vm,
        scratch_shapes=[pltpu.VMEM((4, hr, 128), BF16), pltpu.SemaphoreType.DMA, pltpu.SemaphoreType.DMA],
        compiler_params=_cp(has_side_effects=True),
    )(csum, land)


def _adam_math(gv, wv, mv, vv):
    m2 = ADAM_B1 * mv + (1.0 - ADAM_B1) * gv
    v2 = ADAM_B2 * vv + (1.0 - ADAM_B2) * (gv * gv)
    m_hat = m2 / (1.0 - ADAM_B1 ** ADAM_STEP)
    v_hat = v2 / (1.0 - ADAM_B2 ** ADAM_STEP)
    return -ADAM_LR * (m_hat / (jnp.sqrt(v_hat) + ADAM_EPS) + ADAM_WD * wv), m2, v2


def _adam(g, w, m, v, name):
    rows, cols = g.shape
    rb = rows // 4

    def body(g_ref, w_ref, m_ref, v_ref, d_ref, m2_ref, v2_ref):
        d_ref[...], m2_ref[...], v2_ref[...] = _adam_math(g_ref[...], w_ref[...], m_ref[...], v_ref[...])

    blk = pl.BlockSpec((rb, cols), lambda i: (i, 0))
    shp = jax.ShapeDtypeStruct((rows, cols), F32)
    return pl.pallas_call(
        body, name=name, grid=(4,), in_specs=[blk] * 4, out_specs=[blk] * 3, out_shape=[shp] * 3,
        compiler_params=_cp(("parallel",)),
    )(g, w, m, v)


def _adam_layer(gs, ws, ms, vs, l, prevs, name):
    n = len(gs)
    prev = [a for p4 in prevs if p4 is not None for a in p4]

    def body(*refs):
        outs = refs[len(refs) - 4 * n:]
        for t in range(n):
            g_ref, w_ref, m_ref, v_ref = refs[4 * t:4 * t + 4]
            go_ref, d_ref, m2_ref, v2_ref = outs[4 * t:4 * t + 4]
            gv = g_ref[...]
            go_ref[...] = gv
            d_ref[...], m2_ref[...], v2_ref[...] = _adam_math(gv, w_ref[...], m_ref[...], v_ref[...])

    in_specs, out_specs, out_shape, operands, aliases = [], [], [], [], {}
    for t, g in enumerate(gs):
        rows, cols = g.shape
        lay = pl.BlockSpec((None, rows // 4, cols), lambda i: (l, i, 0))
        in_specs += [pl.BlockSpec((rows // 4, cols), lambda i: (i, 0)), lay, lay, lay]
        operands += [g, ws[t], ms[t], vs[t]]
        out_specs += [lay] * 4
        out_shape += [jax.ShapeDtypeStruct((2, rows, cols), F32)] * 4
    k = 4 * n
    for t, p4 in enumerate(prevs):
        if p4 is not None:
            for j in range(4):
                aliases[k] = 4 * t + j
                k += 1
    outs = pl.pallas_call(
        body, name=name, grid=(4,), in_specs=in_specs + [ANY] * len(prev), out_specs=out_specs, out_shape=out_shape,
        input_output_aliases=aliases, compiler_params=_cp(("parallel",)),
    )(*operands, *prev)
    return [list(outs[4 * t:4 * t + 4]) for t in range(n)]


def _rows128(a):
    return a.reshape(-1, 128)


def _pack(arrs, mult):
    parts = [_rows128(a) for a in arrs]
    rows = sum(q.shape[0] for q in parts)
    pad = -rows % mult
    if pad:
        parts.append(jnp.zeros((pad, 128), F32))
    return jnp.concatenate(parts, axis=0)


def _unpack(pack, shapes):
    out, o = [], 0
    for s in shapes:
        n = 1
        for e in s:
            n *= e
        out.append(pack[o:o + n // 128].reshape(s))
        o += n // 128
    return out


WEIGHTS = ['norm1_g', 'w_in', 'gmlp_ln_g', 'gmlp_ln_b', 'gmlp_w_s', 'gmlp_b_s', 'conv_w', 'conv_b', 'lru_w_r', 'lru_b_r', 'lru_w_i',
           'lru_b_i', 'lru_lambda', 'w_out', 'norm2_g', 'w_ffn_in', 'w_ffn_out', 'final_g']
BIG = ['w_in', 'w_out', 'w_ffn_in', 'w_ffn_out']
SMALL = [n for n in WEIGHTS if n not in BIG]
CHIP_SHARDED_SMALL = ['conv_w', 'lru_b_r', 'lru_b_i', 'lru_lambda']


def kernel(x, norm1_g, w_in, gmlp_ln_g, gmlp_ln_b, gmlp_w_s, gmlp_b_s, conv_w, conv_b, lru_w_r, lru_b_r, lru_w_i, lru_b_i, lru_lambda, w_out, norm2_g, w_ffn_in, w_ffn_out, final_g, loss_target, m_norm1_g, m_w_in, m_gmlp_ln_g, m_gmlp_ln_b, m_gmlp_w_s, m_gmlp_b_s, m_conv_w, m_conv_b, m_lru_w_r, m_lru_b_r, m_lru_w_i, m_lru_b_i, m_lru_lambda, m_w_out, m_norm2_g, m_w_ffn_in, m_w_ffn_out, m_final_g, v_norm1_g, v_w_in, v_gmlp_ln_g, v_gmlp_ln_b, v_gmlp_w_s, v_gmlp_b_s, v_conv_w, v_conv_b, v_lru_w_r, v_lru_b_r, v_lru_w_i, v_lru_b_i, v_lru_lambda, v_w_out, v_norm2_g, v_w_ffn_in, v_w_ffn_out, v_final_g):
    a = dict(locals())
    w = {n: a[n] for n in WEIGHTS}
    mom = {n: a["m_" + n] for n in WEIGHTS}
    var = {n: a["v_" + n] for n in WEIGHTS}
    _, _, c, chip = _place()
    c_arr, chip_arr = jnp.reshape(c, (1,)).astype(jnp.int32), jnp.reshape(chip, (1,)).astype(jnp.int32)
    place_arr = jnp.stack([chip, c]).astype(jnp.int32)

    first, rest = BIG[:1], BIG[1:]

    def as_weights(names, full):
        wb = {n: f.reshape(4, 2 * f.shape[2], f.shape[3]) for n, f in zip(names, full)}
        if "w_out" in wb:
            wb["w_out"] = wb["w_out"].reshape(D, D)
            wb["w_ffn_out"] = wb["w_ffn_out"].reshape(DFF, D)
        return wb

    def cast(names, l, tag):
        return _cast_into([w[n] for n in names], l, chip_arr, f"cast_{tag}")

    def landed(fly, names, after, tag):
        return as_weights(names, _gather_pass_on(_gather_wait(fly[0], fly[1], fly[2], after, tag), tag))

    tiny = _pack([w[n] for n in CHIP_SHARDED_SMALL], 8)
    _, tiny_full = _gather_weights([], tiny)
    fly_in = _gather_start(cast(first, 0, "in"), "in", after=(tiny_full,))
    fly0 = _gather_start(cast(rest, 0, "0"), "0", after=(fly_in[3],))
    fly1 = _gather_start(cast(BIG, 1, "1"), "1", after=(fly0[3],))
    p = {n: w[n] for n in SMALL}
    parts = [_unpack(tiny_full[k], [w[n].shape for n in CHIP_SHARDED_SMALL]) for k in range(4)]
    for i, n in enumerate(CHIP_SHARDED_SMALL):
        p[n] = jnp.concatenate([parts[k][i] for k in range(4)], axis=-1)

    operands = [_layer_operands(l, p) for l in range(2)]
    state_packs = [_pack([src[n] for n in SMALL], 32) for src in (w, mom, var)]
    ahead = tuple(jax.tree.leaves(operands)) + tuple(state_packs)

    passing = {}

    def pass_on_1(gu):
        bufs = _gather_wait(fly1[0], fly1[1], fly1[2], gu, "1")
        passing[1] = _exchange_start(bufs, [], _pass_on_copies, 3 * len(bufs), "gather_pass_on_start_1")
        return (passing[1][-1],)

    xa, saved0 = _forward_layer(0, x[0], p, landed(fly_in, first, (fly1[3],) + ahead, "in"), after=(fly0[3], fly1[3]),
                                rest=lambda merged: landed(fly0, rest, merged, "0"), near_end=pass_on_1, operands=operands[0])
    send, recv, bufs1, _, _ = passing[1]
    xb, saved1 = _forward_layer(
        1, xa, p, as_weights(BIG, _exchange_wait(send, recv, bufs1, [], xa, _pass_on_copies, "gather_pass_on_wait_1")[0]),
        operands=operands[1], loss=(loss_target[0], p["final_g"][None]))
    dxb, loss_v, dfg = xb
    loss = lax.psum(loss_v[0, 0], ("x", "y", "c"))

    out, flying = {}, {}

    def halves(grads):
        return [g.reshape(4, 2, -1, g.shape[-1]) for g in grads]

    def sibling_start(grads, names, l, tag):
        gs = halves(grads)
        lands = [lax.empty((4,) + g.shape[2:], g.dtype) for g in gs]
        flying["s" + tag] = (names, l) + tuple(
            _exchange_start(gs, lands, _sibling_copies, 4 * len(gs), f"grads_to_sibling_start_{tag}"))
        return (flying["s" + tag][-1],)

    def chips_start(gs, from_sib, names, l, tag):
        cs = _add_half(gs, from_sib, c_arr, f"add_half_{tag}")
        lands = [lax.empty((3,) + a.shape[1:], a.dtype) for a in cs]
        flying[tag] = (names, l) + tuple(_exchange_start(cs, lands, _chips_copies, 3 * len(cs), f"grads_to_chips_start_{tag}"))
        return (flying[tag][-1],)

    def sibling_finish(tag, after):
        names, l, send, recv, gs, lands, _ = flying["s" + tag]
        gs, from_sib = _exchange_wait(send, recv, gs, lands, after, _sibling_copies, f"grads_to_sibling_wait_{tag}")
        return chips_start(gs, from_sib, names, l, tag)

    def reduce_sums(tags, after):
        groups, ts = [], []
        for tag in tags:
            names, l, send, recv, cs, lands, _ = flying[tag]
            cs, lands = _exchange_wait(send, recv, cs, lands, after, _chips_copies, f"grads_to_chips_wait_{tag}")
            ts += _sum_chips(cs, lands, place_arr, f"sum_chips_{tag}")
            groups.append((tag, names))
        flying["j" + tags[0]] = (groups, l) + tuple(_exchange_start(ts, [], _join_copies, len(ts), f"grads_join_start_{tags[0]}"))
        return (flying["j" + tags[0]][-1],)

    def reduce_adam(tag0, after):
        groups, l, send, recv, ts, _, _ = flying["j" + tag0]
        joined = _exchange_wait(send, recv, ts, [], after, _join_copies, f"grads_join_wait_{tag0}")[0]
        for tag, names in groups:
            gs, joined = [j.reshape(w[n].shape[1:]) for n, j in zip(names, joined)], joined[len(names):]
            res = _adam_layer(gs, [w[n] for n in names], [mom[n] for n in names], [var[n] for n in names], l,
                              [out.get(n) for n in names], f"adam_{tag}")
            out.update(zip(names, res))

    def late1(grads):
        return sibling_finish("1a", grads[0]) + sibling_start(grads, first, 1, "1b")

    def midway0(grads):
        return reduce_sums(("1a", "1b"), grads[0]) + sibling_start(grads, rest, 0, "0a")

    def stacked_small(small0):
        small = {k: jnp.stack([small0[k], small1[k]]) for k in LAYER_SMALL}
        return dict(small, final_g=dfg[0])

    def midway3_0(small0):
        small = stacked_small(dict(small0, norm1_g=jnp.zeros((D,), F32)))
        csum = _small_chip_sum(_pack([small[n] for n in SMALL], 32))
        flying["small"] = _exchange_start([csum], [lax.empty((3,) + csum.shape, BF16)], _small_copies, 3, "small_to_chips_start")
        return (flying["small"][-1],)

    def late0(grads):
        tok = sibling_start(grads, first, 0, "0b")
        reduce_adam("1a", tok[0])
        return sibling_finish("0b", out[first[0]][0])

    dxa, big1, small1 = _backward_layer(1, dxb, saved1, midway=lambda grads: sibling_start(grads, rest, 1, "1a"), late=late1)
    dx, big0, small0 = _backward_layer(0, dxa, saved0, after=sibling_finish("1b", dxa), midway=midway0,
                                       midway2=lambda dws: sibling_finish("0a", dws), midway3=midway3_0, late=late0)
    join_tok = reduce_sums(("0a", "0b"), dx)
    small = stacked_small(small0)

    full_shapes = [small[n].shape for n in SMALL]
    send, recv, csum, land, _ = flying["small"]
    csum, land = _exchange_wait(send, recv, csum, land, join_tok[0], _small_copies, "small_to_chips_wait")
    red = _unpack(_small_total(csum[0], land[0]), full_shapes)
    norm1_0 = _allreduce_small(_pack([small0["norm1_g"]], 32))[:D // 128].reshape(D)
    reduce_adam("0a", norm1_0)
    red[SMALL.index("norm1_g")] = red[SMALL.index("norm1_g")].at[0].set(norm1_0)
    g_small = []
    for n, g in zip(SMALL, red):
        if n in CHIP_SHARDED_SMALL:
            g = lax.dynamic_slice_in_dim(g, chip * w[n].shape[-1], w[n].shape[-1], axis=g.ndim - 1)
        g_small.append(g)
    shapes = [w[n].shape for n in SMALL]
    upd = [_unpack(u, shapes) for u in _adam(_pack(g_small, 32), *state_packs, "adam_small")]
    for i, n in enumerate(SMALL):
        out[n] = [g_small[i], upd[0][i], upd[1][i], upd[2][i]]

    return (loss, dx[None]) + tuple(out[n][i] for i in range(4) for n in WEIGHTS)
```

```python
import functools

import jax
import jax.numpy as jnp
from jax import lax
from jax.experimental import pallas as pl
from jax.experimental.pallas import tpu as pltpu

F32 = jnp.float32
BF16 = jnp.bfloat16
MESH = pl.DeviceIdType.MESH

D = 1024
NH = 8
HD = 128
CHUNK = 128
GMLP_ROWS = 512
N_IN_T = 12
DFF = 2816
DFF_SH = 1408
EPS = 1e-6
LRU_C = 8.0
ADAM_LR, ADAM_B1, ADAM_B2, ADAM_EPS, ADAM_WD, ADAM_STEP = 0.001, 0.9, 0.999, 1e-08, 0.01, 10

TM = 512
TM_BIG = 1024
RT = 128
PADR = 8
VMEM_LIMIT = 56 * 1024 * 1024


def _cp(sem=None, **kw):
    if sem is not None:
        kw["dimension_semantics"] = sem
    return pltpu.CompilerParams(vmem_limit_bytes=VMEM_LIMIT, **kw)


_GC = 0.7978845608028654


def _sigmoid(x):
    return 0.5 * jnp.tanh(0.5 * x) + 0.5


_GK = 0.044715


def _gelu(x):
    t = jnp.tanh(x * (_GC + (_GC * _GK) * (x * x)))
    return x * (0.5 + 0.5 * t)


def _gelu_and_grad(x):
    x2 = x * x
    t = jnp.tanh(x * (_GC + (_GC * _GK) * x2))
    h = 0.5 + 0.5 * t
    return x * h, h + x * (1.0 - t * t) * (0.5 * _GC + (1.5 * _GC * _GK) * x2)


def _softplus_neg(lam):
    y = jnp.exp(-jnp.abs(lam))
    u = 1.0 + y
    l1p = jnp.where(u == 1.0, y, jnp.log(u) * y / (u - 1.0))
    return jnp.maximum(-lam, 0.0) + l1p


def _dot(a, b):
    return jnp.dot(a, b, preferred_element_type=F32)


def _dot_nt(a, b):
    return lax.dot_general(a, b, (((1,), (1,)), ((), ())), preferred_element_type=F32)


def _dot_tn(a, b):
    return lax.dot_general(a, b, (((0,), (0,)), ((), ())), preferred_element_type=F32)


def _rms_hat(x):
    r = lax.rsqrt(jnp.mean(x * x, axis=-1, keepdims=True) + EPS)
    return x * r, r


def _rms_bwd(dh, x, g):
    xh, r = _rms_hat(x)
    dxh = dh * g
    dx = r * (dxh - xh * jnp.mean(dxh * xh, axis=-1, keepdims=True))
    return dx, jnp.sum(dh * xh, axis=0, keepdims=True)


def _norm_into(x_ref, g_ref, h_ref):
    xh, _ = _rms_hat(x_ref[...])
    h_ref[...] = (xh * g_ref[...]).astype(BF16)


def _in_tile(j):
    m, hf = j // 2, j % 2
    orig = jnp.where(m < 2, m, jnp.where(m == 2, 4, jnp.where(m < 5, m - 1, 5)))
    t = orig * 2 + hf
    return t // 3, t % 3


ANY = pl.BlockSpec(memory_space=pl.ANY)


def _mm_in(x, g, w_in, l, after=()):
    S = x.shape[0]
    tm = min(2 * TM_BIG, S)

    def body(x_ref, g_ref, w0_ref, w1_ref, *rest):
        o_ref, h_ref = rest[-2:]

        @pl.when(pl.program_id(1) == 0)
        def _():
            _norm_into(x_ref, g_ref, h_ref)
        rp = min(TM, tm)
        for r0 in range(0, tm, rp):
            hv = h_ref[r0:r0 + rp, :]
            o_ref[r0:r0 + rp, 0:512] = _dot(hv, w0_ref[...]).astype(BF16)
            o_ref[r0:r0 + rp, 512:1024] = _dot(hv, w1_ref[...]).astype(BF16)

    def w_tile(hf):
        def w_map(i, m):
            sh, tl = _in_tile(2 * m + hf)
            return (sh, 0, tl)
        return pl.BlockSpec((None, D, 512), w_map)

    return pl.pallas_call(
        body, name=f"mm_in_{l}", grid=(S // tm, 6),
        in_specs=[pl.BlockSpec((tm, D), lambda i, m: (i, 0)), pl.BlockSpec((1, D), lambda i, m: (0, 0)),
                  w_tile(0), w_tile(1)] + [ANY] * len(after),
        out_specs=[pl.BlockSpec((None, tm, D), lambda i, m: (m, i, 0)), pl.BlockSpec((tm, D), lambda i, m: (i, 0))],
        out_shape=[jax.ShapeDtypeStruct((6, S, D), BF16), jax.ShapeDtypeStruct((S, D), BF16)],
        compiler_params=_cp(("parallel", "arbitrary")),
    )(x, g, w_in, w_in, *after)


def _mm_res(a, w, res, l, name, after=()):
    S, K = a.shape

    tm = TM

    def body(a_ref, w_ref, r_ref, *rest):
        rest[-1][...] = r_ref[...] + _dot(a_ref[...], w_ref[...])

    return pl.pallas_call(
        body, name=f"{name}_{l}", grid=(S // tm,),
        in_specs=[pl.BlockSpec((tm, K), lambda i: (i, 0)), pl.BlockSpec((K, D), lambda i: (0, 0)),
                  pl.BlockSpec((tm, D), lambda i: (i, 0))] + [ANY] * len(after),
        out_specs=pl.BlockSpec((tm, D), lambda i: (i, 0)),
        out_shape=jax.ShapeDtypeStruct((S, D), F32),
        compiler_params=_cp(("parallel",)),
    )(a, w, res, *after)


def _mm_ffn_in(x, g, w_fi, l):
    S = x.shape[0]

    tm = min(TM_BIG, S)

    def body(x_ref, g_ref, w_ref, gu_ref, ff_ref, h_ref):
        @pl.when(pl.program_id(1) == 0)
        def _():
            _norm_into(x_ref, g_ref, h_ref)
        for r0 in range(0, tm, TM):
            rows = slice(r0, r0 + TM)
            hv = h_ref[rows, :]
            ga = _dot(hv, w_ref[0])
            gb = _dot(hv, w_ref[1])
            sg = _sigmoid(ga)
            silu = ga * sg
            gu_ref[0, rows, :] = (gb * (sg + silu * (1.0 - sg))).astype(BF16)
            gu_ref[1, rows, :] = silu.astype(BF16)
            ff_ref[rows, :] = (silu * gb).astype(BF16)

    gu, ff, h = pl.pallas_call(
        body, name=f"mm_ffn_in_{l}", grid=(S // tm, 2),
        in_specs=[pl.BlockSpec((tm, D), lambda i, s: (i, 0)), pl.BlockSpec((1, D), lambda i, s: (0, 0)),
                  pl.BlockSpec((2, None, D, DFF_SH), lambda i, s: (0, s, 0, 0))],
        out_specs=[pl.BlockSpec((2, None, tm, DFF_SH), lambda i, s: (0, s, i, 0)),
                   pl.BlockSpec((tm, DFF_SH), lambda i, s: (i, s)),
                   pl.BlockSpec((tm, D), lambda i, s: (i, 0))],
        out_shape=[jax.ShapeDtypeStruct((2, 2, S, DFF_SH), BF16), jax.ShapeDtypeStruct((S, DFF), BF16),
                   jax.ShapeDtypeStruct((S, D), BF16)],
        compiler_params=_cp(("parallel", "arbitrary")),
    )(x, g, w_fi.reshape(2, 2, D, DFF_SH))
    return gu.reshape(4, S, DFF_SH), ff, h


def _gmlp_fwd(z6, ws_b, bs_b, lg, lb):
    S = z6.shape[1]

    ts = min(GMLP_ROWS, S)

    def body(z_ref, ws_ref, bs_ref, lg_ref, lb_ref, o_ref, mix):
        for r0 in range(0, ts, CHUNK):
            rows = slice(r0, r0 + CHUNK)
            gv = _gelu(z_ref[1, rows, :].astype(F32))
            xc = gv - jnp.mean(gv, axis=-1, keepdims=True)
            rs = lax.rsqrt(jnp.mean(xc * xc, axis=-1, keepdims=True) + EPS)
            vb = (xc * rs * lg_ref[...] + lb_ref[...]).astype(BF16)
            for gi in range(NH):
                cs = slice(gi * HD, (gi + 1) * HD)
                mix[rows, cs] = _dot(ws_ref[gi], vb[:, cs])
            o_ref[rows, :] = (_sigmoid(z_ref[2, rows, :].astype(F32)) * _gelu(z_ref[0, rows, :].astype(F32))
                              * (mix[rows, :] + bs_ref[...])).astype(BF16)

    return pl.pallas_call(
        body, name="gmlp_fwd", grid=(S // ts,),
        in_specs=[pl.BlockSpec((3, ts, D), lambda i: (0, i, 0)), pl.BlockSpec((NH, CHUNK, CHUNK), lambda i: (0, 0, 0)),
                  pl.BlockSpec((CHUNK, D), lambda i: (0, 0)), pl.BlockSpec((1, D), lambda i: (0, 0)),
                  pl.BlockSpec((1, D), lambda i: (0, 0))],
        out_specs=pl.BlockSpec((ts, D), lambda i: (i, 0)),
        out_shape=jax.ShapeDtypeStruct((S, D), BF16),
        scratch_shapes=[pltpu.VMEM((ts, D), F32)],
        compiler_params=_cp(("parallel",)),
    )(z6, ws_b, bs_b, lg, lb)


def _row_iota():
    return lax.broadcasted_iota(jnp.int32, (RT, HD), 0)


SUB = 8
UNROLL = 8
GRAD_ROWS = 512


def _scan_up(a, b, carry):
    row = lax.broadcasted_iota(jnp.int32, (SUB, HD), 0)
    masks = [(d, row >= d) for d in (1, 2, 4)]
    c = jnp.broadcast_to(carry, (SUB, HD))
    hs = []
    for j in range(RT // SUB):
        aj, bj = a[SUB * j:SUB * (j + 1)], b[SUB * j:SUB * (j + 1)]
        for d, m in masks:
            bj = bj + aj * jnp.where(m, pltpu.roll(bj, d, 0), 0.0)
            aj = aj * jnp.where(m, pltpu.roll(aj, d, 0), 1.0)
        h = bj + aj * c
        hs.append(h)
        c = jnp.broadcast_to(h[SUB - 1:SUB, :], (SUB, HD))
    return jnp.concatenate(hs, axis=0), hs[-1][SUB - 1:SUB, :]


def _scan_down(a, b, carry):
    row = lax.broadcasted_iota(jnp.int32, (SUB, HD), 0)
    masks = [(d, row < SUB - d) for d in (1, 2, 4)]
    c = jnp.broadcast_to(carry, (SUB, HD))
    hs = []
    for j in reversed(range(RT // SUB)):
        aj, bj = a[SUB * j:SUB * (j + 1)], b[SUB * j:SUB * (j + 1)]
        for d, m in masks:
            bj = bj + aj * jnp.where(m, pltpu.roll(bj, SUB - d, 0), 0.0)
            aj = aj * jnp.where(m, pltpu.roll(aj, SUB - d, 0), 1.0)
        h = bj + aj * c
        hs.append(h)
        c = jnp.broadcast_to(h[0:1, :], (SUB, HD))
    return jnp.concatenate(hs[::-1], axis=0), hs[-1][0:1, :]


def _decay(r, sp_d):
    log_a = -LRU_C * r * sp_d
    a = jnp.exp(log_a)
    return a, jnp.sqrt(jnp.maximum(-jnp.tanh(log_a) * (a * a + 1.0), 0.0))


def _decay_bwd(r, sp_d):
    log_a = -LRU_C * r * sp_d
    a = jnp.exp(log_a)
    m2 = jnp.maximum(-jnp.tanh(log_a) * (a * a + 1.0), 0.0)
    inv = jnp.where(m2 > 0.0, lax.rsqrt(m2), 0.0)
    return a, m2 * inv, inv


def _lru_gates(xc, d, wr_ref, br_ref, wi_ref, bi_ref, sp):
    xb = xc.astype(BF16)
    r = _sigmoid(_dot(xb, wr_ref[d]) + br_ref[d:d + 1, :])
    i = _sigmoid(_dot(xb, wi_ref[d]) + bi_ref[d:d + 1, :])
    a, mult = _decay(r, sp[d:d + 1, :])
    return r, i, a, mult


def _shifted(win, k):
    w = RT + 2 * PADR
    v = win if k == 0 else pltpu.roll(win, (-k) % w, 0)
    return v[PADR:PADR + RT]


def _conv_taps(win):
    return [_shifted(win, k) for k in (-1, 0, 1, 2)]


def _fill_padded(dst, src_ref, S):
    zeros = jnp.zeros((PADR, HD), F32)
    dst[0:PADR, :] = zeros
    dst[PADR + S:2 * PADR + S, :] = zeros

    def cp(i, c):
        t0 = pl.multiple_of(i * RT, RT)
        dst[pl.ds(t0 + PADR, RT), :] = src_ref[pl.ds(t0, RT), :].astype(F32)
        return c
    lax.fori_loop(0, S // RT, cp, 0)


def _conv_fwd_all(zxp, xc_s, cw_ref, cb_ref, S):
    def cv(i, c):
        t0 = pl.multiple_of(i * RT, RT)
        xm1, x0, xp1, xp2 = _conv_taps(zxp[pl.ds(t0, RT + 2 * PADR), :])
        xc_s[pl.ds(t0, RT), :] = (cb_ref[...] + xm1 * cw_ref[0:1, :] + x0 * cw_ref[1:2, :]
                                  + xp1 * cw_ref[2:3, :] + xp2 * cw_ref[3:4, :])
        return c
    lax.fori_loop(0, S // RT, cv, 0)


def _lru_specs(S):
    head = lambda h: (0, h)
    return [pl.BlockSpec((4, HD), head), pl.BlockSpec((1, HD), head),
            pl.BlockSpec((2, None, HD, HD), lambda h: (0, h, 0, 0)), pl.BlockSpec((2, HD), head),
            pl.BlockSpec((2, None, HD, HD), lambda h: (0, h, 0, 0)), pl.BlockSpec((2, HD), head),
            pl.BlockSpec((2, HD), head)]


def _lru_fwd(z6, ya, cw, cb, wr, br, wi, bi, lam):
    S = z6.shape[1]
    nt = S // RT

    def body(z_ref, ya_ref, cw_ref, cb_ref, wr_ref, br_ref, wi_ref, bi_ref, lam_ref, mg_ref, h0_ref, h1_ref, zxp, xc_s):
        sp = _softplus_neg(lam_ref[...])
        _fill_padded(zxp, z_ref.at[0], S)
        _conv_fwd_all(zxp, xc_s, cw_ref, cb_ref, S)

        def scans(i, carry):
            cu, cd = carry
            for u in range(UNROLL):
                j = i * UNROLL + u
                ru = pl.ds(pl.multiple_of(j * RT, RT), RT)
                rd = pl.ds(pl.multiple_of((nt - 1 - j) * RT, RT), RT)
                xu, xd = xc_s[ru, :], xc_s[rd, :]
                _, gi, a, mult = _lru_gates(xu, 0, wr_ref, br_ref, wi_ref, bi_ref, sp)
                hu, cu = _scan_up(a, mult * gi * xu, cu)
                h0_ref[ru, :] = hu
                _, gi, a, mult = _lru_gates(xd, 1, wr_ref, br_ref, wi_ref, bi_ref, sp)
                hd, cd = _scan_down(a, mult * gi * xd, cd)
                h1_ref[rd, :] = hd
            return cu, cd
        z1 = jnp.zeros((1, HD), F32)
        lax.fori_loop(0, nt // UNROLL, scans, (z1, z1))

        def merge(i, c):
            rows = pl.ds(pl.multiple_of(i * RT, RT), RT)
            yb = (h0_ref[rows, :] + h1_ref[rows, :]) * _gelu(z_ref[1, rows, :].astype(F32))
            mg_ref[rows, :] = (ya_ref[rows, :].astype(F32) + _sigmoid(z_ref[2, rows, :].astype(F32)) * yb).astype(BF16)
            return c
        lax.fori_loop(0, nt, merge, 0)

    col = pl.BlockSpec((S, HD), lambda h: (0, h))
    return pl.pallas_call(
        body, name="lru_fwd", grid=(NH,),
        in_specs=[pl.BlockSpec((3, S, HD), lambda h: (1, 0, h)), col] + _lru_specs(S),
        out_specs=[col, col, col],
        out_shape=[jax.ShapeDtypeStruct((S, D), BF16), jax.ShapeDtypeStruct((S, D), F32), jax.ShapeDtypeStruct((S, D), F32)],
        scratch_shapes=[pltpu.VMEM((S + 2 * PADR, HD), F32), pltpu.VMEM((S, HD), F32)],
        compiler_params=_cp(("parallel",)),
    )(z6, ya, cw, cb, wr, br, wi, bi, lam)


def _mm_res_loss(a, w, res, tgt, g):
    S, K = a.shape

    def body(a_ref, w_ref, r_ref, t_ref, g_ref, dx_ref, loss_ref, dg_ref):
        @pl.when(pl.program_id(0) == 0)
        def _():
            loss_ref[...] = jnp.zeros_like(loss_ref)
            dg_ref[...] = jnp.zeros_like(dg_ref)
        xv = r_ref[...] + _dot(a_ref[...], w_ref[...])
        xh, _ = _rms_hat(xv)
        e = xh * g_ref[...] - t_ref[...]
        loss_ref[...] += jnp.sum(e * e) * (0.5 / D)
        dx, dgs = _rms_bwd(e * (1.0 / D), xv, g_ref[...])
        dx_ref[...] = dx
        dg_ref[...] += dgs

    row = pl.BlockSpec((TM, D), lambda i: (i, 0))
    vec = pl.BlockSpec((1, D), lambda i: (0, 0))
    return pl.pallas_call(
        body, name="mm_ffn_out_loss", grid=(S // TM,),
        in_specs=[pl.BlockSpec((TM, K), lambda i: (i, 0)), pl.BlockSpec((K, D), lambda i: (0, 0)), row, row, vec],
        out_specs=[row, pl.BlockSpec((1, 128), lambda i: (0, 0)), vec],
        out_shape=[jax.ShapeDtypeStruct((S, D), F32), jax.ShapeDtypeStruct((1, 128), F32), jax.ShapeDtypeStruct((1, D), F32)],
        compiler_params=_cp(("arbitrary",)),
    )(a, w, res, tgt, g)


def _bwd_ffn_out(dx, w_fo, gu, l, after=()):
    S = dx.shape[0]

    tm = min(TM_BIG, S)

    def body(dx_ref, w_ref, gu_ref, *rest):
        o_ref = rest[-1]
        for r0 in range(0, tm, TM):
            rows = slice(r0, r0 + TM)
            d = _dot_nt(dx_ref[rows, :].astype(BF16), w_ref[...])
            o_ref[0, rows, :] = (d * gu_ref[0, rows, :].astype(F32)).astype(BF16)
            o_ref[1, rows, :] = (d * gu_ref[1, rows, :].astype(F32)).astype(BF16)

    pair = pl.BlockSpec((2, None, tm, DFF_SH), lambda i, s: (0, s, i, 0))
    dgu = pl.pallas_call(
        body, name=f"bwd_ffn_out_{l}", grid=(S // tm, 2),
        in_specs=[pl.BlockSpec((tm, D), lambda i, s: (i, 0)), pl.BlockSpec((DFF_SH, D), lambda i, s: (s, 0)), pair]
        + [ANY] * len(after),
        out_specs=pair,
        out_shape=jax.ShapeDtypeStruct((2, 2, S, DFF_SH), BF16),
        compiler_params=_cp(("parallel", "arbitrary")),
    )(dx, w_fo, gu.reshape(2, 2, S, DFF_SH), *after)
    return dgu.reshape(4, S, DFF_SH)


def _mm_tn(a, b, m_blk, tk, name):
    S, M = a.shape

    def body(a_ref, b_ref, o_ref):
        @pl.when(pl.program_id(1) == 0)
        def _():
            o_ref[...] = jnp.zeros_like(o_ref)
        o_ref[...] += _dot_tn(a_ref[...], b_ref[...].astype(BF16))

    return pl.pallas_call(
        body, name=name, grid=(M // m_blk, S // tk),
        in_specs=[pl.BlockSpec((tk, m_blk), lambda m, k: (k, m)), pl.BlockSpec((tk, D), lambda m, k: (k, 0))],
        out_specs=pl.BlockSpec((m_blk, D), lambda m, k: (m, 0)),
        out_shape=jax.ShapeDtypeStruct((M, D), F32),
        compiler_params=_cp(("parallel", "arbitrary")),
    )(a, b)


def _mm_nt_rms_bwd(a, a_specs, w, w_specs, nk, tm, x, g, dres, name, after=()):
    S = x.shape[0]
    sub = len(a_specs)

    def body(*refs):
        a_refs, w_refs = refs[:sub], refs[sub:2 * sub]
        x_ref, g_ref, r_ref = refs[2 * sub:2 * sub + 3]
        dx_ref, dg_ref, acc = refs[-3:]
        i, k = pl.program_id(0), pl.program_id(1)
        @pl.when(k == 0)
        def _():
            acc[...] = jnp.zeros_like(acc)
        for j in range(sub):
            acc[...] += _dot_nt(a_refs[j][...], w_refs[j][...])

        @pl.when(jnp.logical_and(i == 0, k == 0))
        def _():
            dg_ref[...] = jnp.zeros_like(dg_ref)

        @pl.when(k == nk - 1)
        def _():
            dx, dgs = _rms_bwd(acc[...], x_ref[...], g_ref[...])
            dx_ref[...] = r_ref[...] + dx
            dg_ref[...] += dgs

    row = pl.BlockSpec((tm, D), lambda i, k: (i, 0))
    vec = pl.BlockSpec((1, D), lambda i, k: (0, 0))
    return pl.pallas_call(
        body, name=name, grid=(S // tm, nk),
        in_specs=list(a_specs) + list(w_specs) + [row, vec, row] + [ANY] * len(after),
        out_specs=[row, vec],
        out_shape=[jax.ShapeDtypeStruct((S, D), F32), jax.ShapeDtypeStruct((1, D), F32)],
        scratch_shapes=[pltpu.VMEM((tm, D), F32)],
        compiler_params=_cp(("arbitrary", "arbitrary")),
    )(*[a] * sub, *[w] * sub, x, g, dres, *after)


def _dw_ffn_in(h, dgu, l):
    S = h.shape[0]

    def body(h_ref, b_ref, o_ref):
        @pl.when(pl.program_id(1) == 0)
        def _():
            o_ref[...] = jnp.zeros_like(o_ref)
        o_ref[...] += _dot_tn(h_ref[...], b_ref[...])

    tk = min(2 * TM_BIG, S)
    return pl.pallas_call(
        body, name=f"dw_ffn_in_{l}", grid=(4, S // tk),
        in_specs=[pl.BlockSpec((tk, D), lambda j, k: (k, 0)), pl.BlockSpec((None, tk, DFF_SH), lambda j, k: (j, k, 0))],
        out_specs=pl.BlockSpec((None, D, DFF_SH), lambda j, k: (j, 0, 0)),
        out_shape=jax.ShapeDtypeStruct((4, D, DFF_SH), F32),
        compiler_params=_cp(("parallel", "arbitrary")),
    )(h, dgu)


_HALF_COMPS = ((0, 1, 3), (4, 2, 5))


def _dw_in(h, dz6, l, after=()):
    S = h.shape[0]

    def body(h_ref, d0_ref, d1_ref, d2_ref, *rest):
        o_ref = rest[-1]

        @pl.when(pl.program_id(1) == 0)
        def _():
            o_ref[...] = jnp.zeros_like(o_ref)
        hv = h_ref[...]
        for q, d_ref in enumerate((d0_ref, d1_ref, d2_ref)):
            for hf in range(2):
                col = 1024 * q + 512 * hf
                o_ref[col // 1536, :, col % 1536:col % 1536 + 512] += _dot_tn(hv, d_ref[:, 512 * hf:512 * (hf + 1)])

    tk = min(TM_BIG, S)

    def comp(q):
        return pl.BlockSpec((None, tk, D), lambda p, k: (jnp.where(p == 0, _HALF_COMPS[0][q], _HALF_COMPS[1][q]), k, 0))

    return pl.pallas_call(
        body, name=f"dw_in_{l}", grid=(2, S // tk),
        in_specs=[pl.BlockSpec((tk, D), lambda p, k: (k, 0)), comp(0), comp(1), comp(2)] + [ANY] * len(after),
        out_specs=pl.BlockSpec((2, D, 1536), lambda p, k: (p, 0, 0)),
        out_shape=jax.ShapeDtypeStruct((4, D, 1536), F32),
        compiler_params=_cp(("parallel", "arbitrary")),
    )(h, dz6, dz6, dz6, *after)


def _bwd_out(dx, w_o, merged, l):
    S = dx.shape[0]

    def body(dx_ref, w_ref, m_ref, dm_ref, dw_ref):
        @pl.when(pl.program_id(0) == 0)
        def _():
            dw_ref[...] = jnp.zeros_like(dw_ref)
        dxb = dx_ref[...].astype(BF16)
        dm_ref[...] = _dot_nt(dxb, w_ref[...]).astype(BF16)
        dw_ref[...] += _dot_tn(m_ref[...], dxb)

    tm = TM
    row = pl.BlockSpec((tm, D), lambda i: (i, 0))
    return pl.pallas_call(
        body, name=f"bwd_out_{l}", grid=(S // tm,),
        in_specs=[row, pl.BlockSpec((D, D), lambda i: (0, 0)), row],
        out_specs=[row, pl.BlockSpec((D, D), lambda i: (0, 0))],
        out_shape=[jax.ShapeDtypeStruct((S, D), BF16), jax.ShapeDtypeStruct((D, D), F32)],
        compiler_params=_cp(("arbitrary",)),
    )(dx, w_o, merged)


def _gmlp_bwd(dm, z6, ws_b, wst_b, bs_b, lg, lb, after=()):
    S = z6.shape[1]
    ts = min(GMLP_ROWS, S)

    def body(dm_ref, z_ref, ws_ref, wst_ref, bs_ref, lg_ref, lb_ref, *rest):
        dz_ref, dws_ref, dbs_ref, dlg_ref, dlb_ref, mix, dv = rest[-7:]

        @pl.when(pl.program_id(0) == 0)
        def _():
            dws_ref[...] = jnp.zeros_like(dws_ref)
            dbs_ref[...] = jnp.zeros_like(dbs_ref)
            dlg_ref[...] = jnp.zeros_like(dlg_ref)
            dlb_ref[...] = jnp.zeros_like(dlb_ref)
        for r0 in range(0, ts, CHUNK):
            rows = slice(r0, r0 + CHUNK)
            gv, dgelu_v = _gelu_and_grad(z_ref[1, rows, :].astype(F32))
            xc = gv - jnp.mean(gv, axis=-1, keepdims=True)
            rs = lax.rsqrt(jnp.mean(xc * xc, axis=-1, keepdims=True) + EPS)
            vh = xc * rs
            vb = (vh * lg_ref[...] + lb_ref[...]).astype(BF16)
            for gi in range(NH):
                cs = slice(gi * HD, (gi + 1) * HD)
                mix[rows, cs] = _dot(ws_ref[gi], vb[:, cs])
            u, dgelu_u = _gelu_and_grad(z_ref[0, rows, :].astype(F32))
            sa = _sigmoid(z_ref[2, rows, :].astype(F32))
            dya = dm_ref[rows, :].astype(F32) * sa
            dym = dya * (mix[rows, :] + bs_ref[...])
            dz_ref[2, rows, :] = (dym * u * (1.0 - sa)).astype(BF16)
            dz_ref[0, rows, :] = (dym * dgelu_u).astype(BF16)
            dmix = dya * u
            dmb = dmix.astype(BF16)
            for gi in range(NH):
                cs = slice(gi * HD, (gi + 1) * HD)
                dv[rows, cs] = _dot(wst_ref[gi], dmb[:, cs])
                dws_ref[gi] += _dot_nt(dmb[:, cs], vb[:, cs])
                dbs_ref[gi] += jnp.broadcast_to(jnp.sum(dmix[:, cs], axis=1, keepdims=True), (CHUNK, HD))
            dvv = dv[rows, :]
            dlg_ref[...] += jnp.sum(dvv * vh, axis=0, keepdims=True)
            dlb_ref[...] += jnp.sum(dvv, axis=0, keepdims=True)
            dvh = dvv * lg_ref[...]
            dgv = rs * (dvh - jnp.mean(dvh, axis=-1, keepdims=True) - vh * jnp.mean(dvh * vh, axis=-1, keepdims=True))
            dz_ref[1, rows, :] = (dgv * dgelu_v).astype(BF16)

    vec = pl.BlockSpec((1, D), lambda i: (0, 0))
    mat = pl.BlockSpec((NH, CHUNK, CHUNK), lambda i: (0, 0, 0))
    return pl.pallas_call(
        body, name="gmlp_bwd", grid=(S // ts,),
        in_specs=[pl.BlockSpec((ts, D), lambda i: (i, 0)), pl.BlockSpec((3, ts, D), lambda i: (0, i, 0)), mat, mat,
                  pl.BlockSpec((CHUNK, D), lambda i: (0, 0)), vec, vec] + [ANY] * len(after),
        out_specs=[pl.BlockSpec((3, ts, D), lambda i: (0, i, 0)), mat, mat, vec, vec],
        out_shape=[jax.ShapeDtypeStruct((6, S, D), BF16), jax.ShapeDtypeStruct((NH, CHUNK, CHUNK), F32),
                   jax.ShapeDtypeStruct((NH, CHUNK, HD), F32), jax.ShapeDtypeStruct((1, D), F32), jax.ShapeDtypeStruct((1, D), F32)],
        scratch_shapes=[pltpu.VMEM((ts, D), F32), pltpu.VMEM((ts, D), F32)],
        compiler_params=_cp(("arbitrary",)),
    )(dm, z6, ws_b, wst_b, bs_b, lg, lb, *after)


def _lru_bwd(dz6, dm, z6, h0, h1, cw, cb, wr, br, wi, bi, lam, after=()):
    S = z6.shape[1]
    nt = S // RT

    def body(dz_in, dm_ref, z_ref, h0_ref, h1_ref, cw_ref, cb_ref, wr_ref, br_ref, wi_ref, bi_ref, lam_ref, *rest):
        dz_ref, dcw_ref, dcb_ref, dwr_ref, dbr_ref, dwi_ref, dbi_ref, dlam_ref, zxp, xc_s, dhs_s, dxcp, r_s, lam_s = rest[-14:]
        del dz_in
        lam = lam_ref[...]
        sp = _softplus_neg(lam)
        row = _row_iota()
        _fill_padded(zxp, z_ref.at[0], S)
        _conv_fwd_all(zxp, xc_s, cw_ref, cb_ref, S)
        zeros = jnp.zeros((PADR, HD), F32)
        dxcp[0:PADR, :] = zeros
        dxcp[PADR + S:2 * PADR + S, :] = zeros
        dwr_ref[...] = jnp.zeros_like(dwr_ref)
        dwi_ref[...] = jnp.zeros_like(dwi_ref)

        def pre(i, c):
            rows = pl.ds(pl.multiple_of(i * RT, RT), RT)
            hs = h0_ref[rows, :] + h1_ref[rows, :]
            dmv = dm_ref[rows, :].astype(F32)
            sb = _sigmoid(z_ref[2, rows, :].astype(F32))
            gg, dgg = _gelu_and_grad(z_ref[1, rows, :].astype(F32))
            dz_ref[2, rows, :] = (dmv * hs * gg * sb * (1.0 - sb)).astype(BF16)
            dyb = dmv * sb
            dz_ref[1, rows, :] = (dyb * hs * dgg).astype(BF16)
            dhs_s[rows, :] = dyb * gg
            return c
        lax.fori_loop(0, nt, pre, 0)

        def gate_bwd(d, gates, lamv, da, xc):
            r, gi, a, mult, inv_mult = gates
            lx, lm = lamv * xc, lamv * mult
            dlog_r = (da - (lx * gi) * (a * inv_mult)) * a * r
            dpr = dlog_r * (1.0 - r) * (-LRU_C * sp[d:d + 1, :])
            dpi = (lx * mult) * gi * (1.0 - gi)
            xb, dprb, dpib = xc.astype(BF16), dpr.astype(BF16), dpi.astype(BF16)
            dwr_ref[d] += _dot_tn(xb, dprb)
            dwi_ref[d] += _dot_tn(xb, dpib)
            dxc = lm * gi + _dot_nt(dprb, wr_ref[d]) + _dot_nt(dpib, wi_ref[d])
            return dxc, (jnp.sum(dlog_r, axis=0, keepdims=True) * (-LRU_C), jnp.sum(dpr, axis=0, keepdims=True),
                         jnp.sum(dpi, axis=0, keepdims=True))

        def rgates(i, c):
            for u in range(UNROLL):
                rows = pl.ds(pl.multiple_of((i * UNROLL + u) * RT, RT), RT)
                xb = xc_s[rows, :].astype(BF16)
                for d in range(2):
                    r_s[d, rows, :] = _sigmoid(_dot(xb, wr_ref[d]) + br_ref[d:d + 1, :])
            return c
        lax.fori_loop(0, nt // UNROLL, rgates, 0)

        def chains(i, carry):
            qn, qp = carry
            for u in range(UNROLL):
                j = i * UNROLL + u
                rd = pl.ds(pl.multiple_of((nt - 1 - j) * RT, RT), RT)
                a, dhs = _decay(r_s[0, rd, :], sp[0:1, :])[0], dhs_s[rd, :]
                q, q_first = _scan_down(a, a * dhs, qn)
                lam_s[0, rd, :] = dhs + jnp.where(row == RT - 1, qn, pltpu.roll(q, RT - 1, 0))
                qn = q_first
                ru = pl.ds(pl.multiple_of(j * RT, RT), RT)
                a, dhs = _decay(r_s[1, ru, :], sp[1:2, :])[0], dhs_s[ru, :]
                q, q_last = _scan_up(a, a * dhs, qp)
                lam_s[1, ru, :] = dhs + jnp.where(row == 0, qp, pltpu.roll(q, 1, 0))
                qp = q_last
            return qn, qp

        z1 = jnp.zeros((1, HD), F32)
        lax.fori_loop(0, nt // UNROLL, chains, (z1, z1))

        ct = min(GRAD_ROWS, S)
        crow = lax.broadcasted_iota(jnp.int32, (ct, HD), 0)

        def tile_grads(i, acc):
            t0 = pl.multiple_of(i * ct, ct)
            rows = pl.ds(t0, ct)
            xc = xc_s[rows, :]
            xb = xc.astype(BF16)
            tp = pl.multiple_of(jnp.maximum(t0 - PADR, 0), PADR)
            prev = jnp.where(t0 > 0, h0_ref[pl.ds(tp, PADR), :][PADR - 1:PADR, :], 0.0)
            tn = pl.multiple_of(jnp.minimum(t0 + ct, S - PADR), PADR)
            nxt = jnp.where(t0 + ct < S, h1_ref[pl.ds(tn, PADR), :][0:1, :], 0.0)
            hside = (jnp.where(crow == 0, prev, pltpu.roll(h0_ref[rows, :], 1, 0)),
                     jnp.where(crow == ct - 1, nxt, pltpu.roll(h1_ref[rows, :], ct - 1, 0)))
            dxc, sums = 0.0, ()
            for d in range(2):
                r = r_s[d, rows, :]
                gi = _sigmoid(_dot(xb, wi_ref[d]) + bi_ref[d:d + 1, :])
                lamv = lam_s[d, rows, :]
                dxc_d, s_d = gate_bwd(d, (r, gi) + _decay_bwd(r, sp[d:d + 1, :]), lamv, lamv * hside[d], xc)
                dxc = dxc + dxc_d
                sums = sums + s_d
            dxcp[pl.ds(t0 + PADR, ct), :] = dxc
            return tuple(x + y for x, y in zip(acc, sums))

        s_sp0, s_br0, s_bi0, s_sp1, s_br1, s_bi1 = lax.fori_loop(0, S // ct, tile_grads, (z1,) * 6)

        dsp = jnp.concatenate([s_sp0, s_sp1], axis=0)
        dlam_ref[...] = -dsp * _sigmoid(-lam)
        dbr_ref[...] = jnp.concatenate([s_br0, s_br1], axis=0)
        dbi_ref[...] = jnp.concatenate([s_bi0, s_bi1], axis=0)

        def conv_bwd(i, carry):
            c0, c1, c2, c3, cb_ = carry
            t0 = pl.multiple_of(i * RT, RT)
            dwin = dxcp[pl.ds(t0, RT + 2 * PADR), :]
            d0 = _shifted(dwin, 0)
            dz_ref[0, pl.ds(t0, RT), :] = (_shifted(dwin, 1) * cw_ref[0:1, :] + d0 * cw_ref[1:2, :]
                                           + _shifted(dwin, -1) * cw_ref[2:3, :] + _shifted(dwin, -2) * cw_ref[3:4, :]).astype(BF16)
            xm1, x0, xp1, xp2 = _conv_taps(zxp[pl.ds(t0, RT + 2 * PADR), :])
            sm = lambda v: jnp.sum(v, axis=0, keepdims=True)
            return c0 + sm(d0 * xm1), c1 + sm(d0 * x0), c2 + sm(d0 * xp1), c3 + sm(d0 * xp2), cb_ + sm(d0)

        c0, c1, c2, c3, cb_ = lax.fori_loop(0, nt, conv_bwd, (z1, z1, z1, z1, z1))
        dcw_ref[...] = jnp.concatenate([c0, c1, c2, c3], axis=0)
        dcb_ref[...] = cb_

    col = pl.BlockSpec((S, HD), lambda h: (0, h))
    head = lambda h: (0, h)
    wspec = pl.BlockSpec((2, None, HD, HD), lambda h: (0, h, 0, 0))
    return pl.pallas_call(
        body, name="lru_bwd", grid=(NH,),
        in_specs=[pl.BlockSpec(memory_space=pl.ANY), col, pl.BlockSpec((3, S, HD), lambda h: (1, 0, h)), col, col] + _lru_specs(S)
        + [ANY] * len(after),
        out_specs=[pl.BlockSpec((3, S, HD), lambda h: (1, 0, h)), pl.BlockSpec((4, HD), head), pl.BlockSpec((1, HD), head),
                   wspec, pl.BlockSpec((2, HD), head), wspec, pl.BlockSpec((2, HD), head), pl.BlockSpec((2, HD), head)],
        out_shape=[jax.ShapeDtypeStruct((6, S, D), BF16), jax.ShapeDtypeStruct((4, D), F32), jax.ShapeDtypeStruct((1, D), F32),
                   jax.ShapeDtypeStruct((2, NH, HD, HD), F32), jax.ShapeDtypeStruct((2, D), F32),
                   jax.ShapeDtypeStruct((2, NH, HD, HD), F32), jax.ShapeDtypeStruct((2, D), F32), jax.ShapeDtypeStruct((2, D), F32)],
        scratch_shapes=[pltpu.VMEM((S + 2 * PADR, HD), F32), pltpu.VMEM((S, HD), F32), pltpu.VMEM((S, HD), F32),
                        pltpu.VMEM((S + 2 * PADR, HD), F32), pltpu.VMEM((2, S, HD), F32), pltpu.VMEM((2, S, HD), F32)],
        input_output_aliases={0: 0},
        compiler_params=_cp(("parallel",)),
    )(dz6, dm, z6, h0, h1, cw, cb, wr, br, wi, bi, lam, *after)


LAYER_SMALL = ("norm1_g", "gmlp_ln_g", "gmlp_ln_b", "gmlp_w_s", "gmlp_b_s", "conv_w", "conv_b",
               "lru_w_r", "lru_b_r", "lru_w_i", "lru_b_i", "lru_lambda", "norm2_g")


def _layer_operands(l, p):
    ws_b = p["gmlp_w_s"][l].astype(BF16)
    tm = dict(ws_b=ws_b, wst_b=jnp.swapaxes(ws_b, 1, 2), bs_b=jnp.repeat(p["gmlp_b_s"][l].T, HD, axis=1),
              lg=p["gmlp_ln_g"][l][None], lb=p["gmlp_ln_b"][l][None])
    lru = (p["conv_w"][l], p["conv_b"][l][None], p["lru_w_r"][l].astype(BF16), p["lru_b_r"][l],
           p["lru_w_i"][l].astype(BF16), p["lru_b_i"][l], p["lru_lambda"][l])
    return (p["norm1_g"][l][None], p["norm2_g"][l][None]), tm, lru


def _forward_layer(l, x, p, wb, after=(), rest=None, near_end=None, operands=None, loss=None):
    (g1, g2), tm, lru = _layer_operands(l, p) if operands is None else operands
    z6, hn1 = _mm_in(x, g1, wb["w_in"], l, after)
    ya = _gmlp_fwd(z6, tm["ws_b"], tm["bs_b"], tm["lg"], tm["lb"])
    merged, h0, h1 = _lru_fwd(z6, ya, *lru)
    if rest is not None:
        wb = dict(wb, **rest(merged))
    x1 = _mm_res(merged, wb["w_out"], x, l, "mm_out")
    gu, ff, hn2 = _mm_ffn_in(x1, g2, wb["w_ffn_in"], l)
    if loss is None:
        x2 = _mm_res(ff, wb["w_ffn_out"], x1, l, "mm_ffn_out", () if near_end is None else tuple(near_end(gu)))
    else:
        x2 = _mm_res_loss(ff, wb["w_ffn_out"], x1, *loss)
    return x2, dict(x=x, z6=z6, h0=h0, h1=h1, merged=merged, x1=x1, gu=gu, ff=ff, g1=g1, g2=g2, tm=tm, lru=lru,
                    hn1=hn1, hn2=hn2, wb=wb)


def _backward_layer(l, dx, s, after=(), midway=None, midway2=None, midway3=None, late=None):
    S = dx.shape[0]
    tm, wb = s["tm"], s["wb"]
    g2 = s["g2"]
    dgu = _bwd_ffn_out(dx, wb["w_ffn_out"], s["gu"], l, after)
    tmb = min(TM_BIG, S)
    dwfo = _mm_tn(s["ff"], dx, DFF_SH, tmb, f"dw_ffn_out_{l}")
    dx1, dg2 = _mm_nt_rms_bwd(
        dgu, [pl.BlockSpec((None, tmb, DFF_SH), lambda i, k: (k, i, 0))],
        wb["w_ffn_in"], [pl.BlockSpec((None, D, DFF_SH), lambda i, k: (k, 0, 0))],
        4, tmb, s["x1"], g2, dx, f"bwd_ffn_in_{l}")
    dwfi = _dw_ffn_in(s["hn2"], dgu, l)
    dmg, dwo = _bwd_out(dx1, wb["w_out"], s["merged"], l)
    mid = () if midway is None else tuple(midway([dwo, dwfi, dwfo]))
    dz6, dws, dbs, dlg, dlb = _gmlp_bwd(dmg, s["z6"], tm["ws_b"], tm["wst_b"], tm["bs_b"], tm["lg"], tm["lb"], mid)
    mid2 = () if midway2 is None else tuple(midway2(dws))
    dz6, dcw, dcb, dwr, dbr, dwi, dbi, dlam = _lru_bwd(dz6, dmg, s["z6"], s["h0"], s["h1"], *s["lru"], after=mid2)

    sub = 3

    def dz_tile(j):
        return pl.BlockSpec((None, tmb, 512), lambda i, k: ((sub * k + j) // 2, i, (sub * k + j) % 2))

    def w_tile(j):
        def w_map(i, k):
            sh, tl = _in_tile(sub * k + j)
            return (sh, 0, tl)
        return pl.BlockSpec((None, D, 512), w_map)

    small = dict(gmlp_ln_g=dlg[0], gmlp_ln_b=dlb[0], gmlp_w_s=dws, gmlp_b_s=dbs[:, :, 0], conv_w=dcw, conv_b=dcb[0],
                 lru_w_r=dwr, lru_b_r=dbr, lru_w_i=dwi, lru_b_i=dbi, lru_lambda=dlam, norm2_g=dg2[0])
    mid3 = () if midway3 is None else tuple(midway3(small))
    dwin = _dw_in(s["hn1"], dz6, l, mid3)
    tail = () if late is None else tuple(late([dwin]))
    dx0, dg1 = _mm_nt_rms_bwd(
        dz6, [dz_tile(j) for j in range(sub)], wb["w_in"], [w_tile(j) for j in range(sub)],
        N_IN_T // sub, tmb, s["x"], s["g1"], dx1, f"bwd_in_{l}", tail)
    return dx0, [dwin, dwo, dwfi, dwfo], dict(small, norm1_g=dg1[0])


def _local_step(x, tgt, p, wbs):
    saved = []
    for l in range(2):
        x, s = _forward_layer(l, x, p, wbs[l], loss=(tgt, p["final_g"][None]) if l else None)
        saved.append(s)
    dx, loss_v, dfg = x
    big, smalls = [None, None], [None, None]
    for l in (1, 0):
        dx, big[l], smalls[l] = _backward_layer(l, dx, saved[l])
    small = {k: jnp.stack([smalls[0][k], smalls[1][k]]) for k in LAYER_SMALL}
    small["final_g"] = dfg[0]
    return loss_v, dx, big, small


def _place():
    x, y, c = lax.axis_index("x"), lax.axis_index("y"), lax.axis_index("c")
    return x, y, c, 2 * x + y


def _chip_at(x, y, d):
    px = 1 - x if d & 2 else x
    py = 1 - y if d & 1 else y
    return px, py, 2 * px + py


HBM = pl.BlockSpec(memory_space=pltpu.HBM)
SEM = pl.BlockSpec(memory_space=pltpu.SEMAPHORE)
DATAFLOW = pltpu.SideEffectType.DATAFLOW_SIDE_EFFECTING


def _in_hbm(a):
    return pltpu.with_memory_space_constraint(a, pltpu.HBM)


def _cast_into(wfs, l, chip_arr, name):
    n = len(wfs)

    def body(ch_ref, *refs):
        for w_ref, o_ref in zip(refs[:n], refs[n:]):
            o_ref[...] = w_ref[...].astype(BF16)

    halves = [(wf.shape[1] // 2, wf.shape[2]) for wf in wfs]
    return pl.pallas_call(
        body, name=name, out_shape=[jax.ShapeDtypeStruct((4, 2, rh, cols), BF16) for rh, cols in halves],
        grid_spec=pltpu.PrefetchScalarGridSpec(
            num_scalar_prefetch=1, grid=(2,),
            in_specs=[pl.BlockSpec((None, None, rh, cols), lambda h, ch: (l, h, 0, 0)) for rh, cols in halves],
            out_specs=[pl.BlockSpec((None, None, rh, cols), lambda h, ch: (ch[0], h, 0, 0)) for rh, cols in halves]),
        compiler_params=_cp(("parallel",)),
    )(chip_arr, *[wf.reshape(2, 2, rh, cols) for wf, (rh, cols) in zip(wfs, halves)])


def _half_block(ref, chip, half, to, send_sem, recv_sem):
    blk = ref.at[chip, half]
    return pltpu.make_async_remote_copy(src_ref=blk, dst_ref=blk, send_sem=send_sem, recv_sem=recv_sem,
                                        device_id=to, device_id_type=MESH)


def _gather_weights(bufs, tiny):
    nt = len(bufs)
    n_ici = max(nt * 3, 1)

    def body(*refs):
        tiny_ref = refs[nt]
        o_refs, tiny_o = refs[nt + 1:2 * nt + 1], refs[2 * nt + 1]
        send, recv, fsend, frecv, tsend, trecv, lsem = refs[2 * nt + 2:]
        x, y, c, chip = _place()
        local = pltpu.make_async_copy(tiny_ref, tiny_o.at[chip], lsem)
        local.start()

        def tin(d, origin_chip, to):
            return pltpu.make_async_remote_copy(
                src_ref=tiny_ref, dst_ref=tiny_o.at[origin_chip], send_sem=tsend.at[d - 1], recv_sem=trecv.at[d - 1],
                device_id=to, device_id_type=MESH)

        sends = []
        for t in range(nt):
            for d in (1, 2, 3):
                px, py, _ = _chip_at(x, y, d)
                sends.append(_half_block(o_refs[t], chip, c, (px, py, c), send.at[3 * t + d - 1], recv.at[3 * t + d - 1]))
        for d in (1, 2, 3):
            px, py, _ = _chip_at(x, y, d)
            sends.append(tin(d, chip, (px, py, c)))
        for cp in sends:
            cp.start()
        passed = []
        for t in range(nt):
            for d in (1, 2, 3):
                k = 3 * t + d - 1
                _, _, pchip = _chip_at(x, y, d)
                _half_block(o_refs[t], pchip, c, (x, y, c), send.at[k], recv.at[k]).wait_recv()
                f = _half_block(o_refs[t], pchip, c, (x, y, 1 - c), fsend.at[k], frecv.at[k])
                f.start()
                passed.append(f)
        for t in range(nt):
            for d in (1, 2, 3):
                k = 3 * t + d - 1
                _, _, pchip = _chip_at(x, y, d)
                _half_block(o_refs[t], pchip, 1 - c, (x, y, 1 - c), fsend.at[k], frecv.at[k]).wait_recv()
        for d in (1, 2, 3):
            _, _, pchip = _chip_at(x, y, d)
            tin(d, pchip, (x, y, c)).wait_recv()
        for cp in sends + passed:
            cp.wait_send()
        local.wait()

    out_shape = [jax.ShapeDtypeStruct(b.shape, b.dtype) for b in bufs]
    out_shape.append(jax.ShapeDtypeStruct((4,) + tiny.shape, tiny.dtype))
    outs = pl.pallas_call(
        body, name="gather_weights_0", out_shape=out_shape,
        in_specs=[ANY] * (nt + 1), out_specs=[ANY] * (nt + 1),
        scratch_shapes=[pltpu.SemaphoreType.DMA((n_ici,)), pltpu.SemaphoreType.DMA((n_ici,)),
                        pltpu.SemaphoreType.DMA((n_ici,)), pltpu.SemaphoreType.DMA((n_ici,)),
                        pltpu.SemaphoreType.DMA((3,)), pltpu.SemaphoreType.DMA((3,)), pltpu.SemaphoreType.DMA],
        input_output_aliases={t: t for t in range(nt)},
        compiler_params=_cp(has_side_effects=True),
    )(*bufs, tiny)
    return outs[:nt], outs[nt]


def _gather_start(bufs, tag, after=()):
    nt, na = len(bufs), len(after)

    def body(*refs):
        b_refs = refs[:nt]
        send, recv = refs[nt + na], refs[nt + na + 1]
        token = refs[2 * nt + na + 2]
        x, y, c, chip = _place()
        for t in range(nt):
            for d in (1, 2, 3):
                px, py, _ = _chip_at(x, y, d)
                _half_block(b_refs[t], chip, c, (px, py, c), send.at[3 * t + d - 1], recv.at[3 * t + d - 1]).start()
        token[...] = jnp.zeros_like(token)

    outs = pl.pallas_call(
        body, name=f"gather_start_{tag}",
        out_shape=(pltpu.SemaphoreType.DMA((3 * nt,)), pltpu.SemaphoreType.DMA((3 * nt,)),
                   *[pltpu.HBM(b.shape, b.dtype) for b in bufs], jax.ShapeDtypeStruct((8, 128), F32)),
        in_specs=[HBM] * nt + [ANY] * na, out_specs=(SEM, SEM, *[HBM] * nt, pl.BlockSpec(memory_space=pltpu.VMEM)),
        input_output_aliases={t: 2 + t for t in range(nt)},
        compiler_params=pltpu.CompilerParams(has_side_effects=DATAFLOW),
    )(*[_in_hbm(b) for b in bufs], *after)
    return outs[0], outs[1], list(outs[2:2 + nt]), outs[2 + nt]


def _gather_wait(send, recv, bufs, after, tag):
    nt = len(bufs)

    def body(*refs):
        b_refs = refs[:nt]
        send_ref, recv_ref = refs[nt], refs[nt + 1]
        x, y, c, chip = _place()
        for t in range(nt):
            for d in (1, 2, 3):
                k = 3 * t + d - 1
                px, py, pchip = _chip_at(x, y, d)
                _half_block(b_refs[t], chip, c, (px, py, c), send_ref.at[k], recv_ref.at[k]).wait_send()
                _half_block(b_refs[t], pchip, c, (px, py, c), send_ref.at[k], recv_ref.at[k]).wait_recv()

    after = tuple(after) if isinstance(after, (tuple, list)) else (after,)
    outs = pl.pallas_call(
        body, name=f"gather_wait_{tag}", out_shape=[pltpu.HBM(b.shape, b.dtype) for b in bufs],
        in_specs=[HBM] * nt + [SEM, SEM] + [ANY] * len(after), out_specs=[HBM] * nt,
        input_output_aliases={t: t for t in range(nt)},
        compiler_params=pltpu.CompilerParams(has_side_effects=DATAFLOW),
    )(*bufs, send, recv, *after)
    return list(outs)


def _gather_pass_on(bufs, tag):
    nt = len(bufs)

    def body(*refs):
        o_refs = refs[nt:2 * nt]
        fsend, frecv = refs[2 * nt:]
        x, y, c, _ = _place()
        cps = []
        for t in range(nt):
            for d in (1, 2, 3):
                k = 3 * t + d - 1
                _, _, pchip = _chip_at(x, y, d)
                cps.append(_half_block(o_refs[t], pchip, c, (x, y, 1 - c), fsend.at[k], frecv.at[k]))
        for cp in cps:
            cp.start()
        for t in range(nt):
            for d in (1, 2, 3):
                k = 3 * t + d - 1
                _, _, pchip = _chip_at(x, y, d)
                _half_block(o_refs[t], pchip, 1 - c, (x, y, 1 - c), fsend.at[k], frecv.at[k]).wait_recv()
        for cp in cps:
            cp.wait_send()

    return pl.pallas_call(
        body, name=f"gather_pass_on_{tag}", out_shape=[jax.ShapeDtypeStruct(b.shape, b.dtype) for b in bufs],
        in_specs=[ANY] * nt, out_specs=[ANY] * nt,
        scratch_shapes=[pltpu.SemaphoreType.DMA((3 * nt,)), pltpu.SemaphoreType.DMA((3 * nt,))],
        input_output_aliases={t: t for t in range(nt)},
        compiler_params=_cp(has_side_effects=True),
    )(*bufs)


def _chip_copy(c_ref, land_ref, x, y, c, d, send_sem, recv_sem):
    px, py, pchip = _chip_at(x, y, d)
    return pltpu.make_async_remote_copy(src_ref=c_ref.at[pchip], dst_ref=land_ref.at[d - 1], send_sem=send_sem, recv_sem=recv_sem,
                                        device_id=(px, py, c), device_id_type=MESH)


def _exchange_start(srcs, lands, copies, nsem, name):
    ns, n = len(srcs), len(srcs) + len(lands)

    def body(*refs):
        for cp in copies(refs[:ns], refs[ns:n], refs[n], refs[n + 1]):
            cp.start()
        token = refs[2 * n + 2]
        token[...] = jnp.zeros_like(token)

    outs = pl.pallas_call(
        body, name=name,
        out_shape=(pltpu.SemaphoreType.DMA((nsem,)), pltpu.SemaphoreType.DMA((nsem,)),
                   *[pltpu.HBM(a.shape, a.dtype) for a in list(srcs) + list(lands)], jax.ShapeDtypeStruct((8, 128), F32)),
        in_specs=[HBM] * n, out_specs=(SEM, SEM, *[HBM] * n, pl.BlockSpec(memory_space=pltpu.VMEM)),
        input_output_aliases={i: 2 + i for i in range(n)},
        compiler_params=pltpu.CompilerParams(has_side_effects=DATAFLOW),
    )(*[_in_hbm(a) for a in list(srcs) + list(lands)])
    return outs[0], outs[1], list(outs[2:2 + ns]), list(outs[2 + ns:2 + n]), outs[2 + n]


def _exchange_wait(send, recv, srcs, lands, after, copies, name):
    ns, n = len(srcs), len(srcs) + len(lands)

    def body(*refs):
        for cp in copies(refs[:ns], refs[ns:n], refs[n], refs[n + 1]):
            cp.wait_send()
            cp.wait_recv()

    outs = pl.pallas_call(
        body, name=name, out_shape=[pltpu.HBM(a.shape, a.dtype) for a in list(srcs) + list(lands)],
        in_specs=[HBM] * n + [SEM, SEM, ANY], out_specs=[HBM] * n,
        input_output_aliases={i: i for i in range(n)},
        compiler_params=pltpu.CompilerParams(has_side_effects=DATAFLOW),
    )(*srcs, *lands, send, recv, after)
    return list(outs[:ns]), list(outs[ns:])


def _pass_on_copies(b_refs, land_refs, send, recv):
    del land_refs
    x, y, c, _ = _place()
    return [_half_block(b_refs[t], _chip_at(x, y, d)[2], c, (x, y, 1 - c), send.at[3 * t + d - 1], recv.at[3 * t + d - 1])
            for t in range(len(b_refs)) for d in (1, 2, 3)]


def _chips_copies(c_refs, land_refs, send, recv):
    x, y, c, _ = _place()
    return [_chip_copy(c_refs[t], land_refs[t], x, y, c, d, send.at[3 * t + d - 1], recv.at[3 * t + d - 1])
            for t in range(len(c_refs)) for d in (1, 2, 3)]


def _sibling_copies(g_refs, land_refs, send, recv):
    x, y, c, _ = _place()
    return [pltpu.make_async_remote_copy(
        src_ref=g_refs[t].at[k, 1 - c], dst_ref=land_refs[t].at[k], send_sem=send.at[4 * t + k], recv_sem=recv.at[4 * t + k],
        device_id=(x, y, 1 - c), device_id_type=MESH) for t in range(len(g_refs)) for k in range(4)]


def _join_copies(f_refs, land_refs, send, recv):
    del land_refs
    x, y, c, _ = _place()
    return [pltpu.make_async_remote_copy(
        src_ref=f_refs[t].at[c], dst_ref=f_refs[t].at[c], send_sem=send.at[t], recv_sem=recv.at[t],
        device_id=(x, y, 1 - c), device_id_type=MESH) for t in range(len(f_refs))]


def _add_half(gs, rs, c_arr, name):
    n = len(gs)

    def body(c_ref, *refs):
        for g_ref, r_ref, o_ref in zip(refs[:n], refs[n:2 * n], refs[2 * n:]):
            o_ref[...] = (g_ref[...] + r_ref[...]).astype(BF16)

    def own(g):
        return pl.BlockSpec((None, None) + g.shape[2:], lambda k, cr: (k, cr[0], 0, 0))

    def blk(g):
        return pl.BlockSpec((None,) + g.shape[2:], lambda k, cr: (k, 0, 0))

    return pl.pallas_call(
        body, name=name, out_shape=[jax.ShapeDtypeStruct((4,) + g.shape[2:], BF16) for g in gs],
        grid_spec=pltpu.PrefetchScalarGridSpec(
            num_scalar_prefetch=1, grid=(4,),
            in_specs=[own(g) for g in gs] + [blk(g) for g in gs], out_specs=[blk(g) for g in gs]),
        compiler_params=_cp(("parallel",)),
    )(c_arr, *gs, *rs)


def _sum_chips(css, r3s, place_arr, name):
    n = len(css)

    def body(pl_ref, *refs):
        up = lambda ref: ref[...].astype(F32)
        for t in range(n):
            a_ref, (r0_ref, r1_ref, r2_ref), o_ref = refs[t], refs[n + 3 * t:n + 3 * t + 3], refs[4 * n + t]
            o_ref[...] = ((up(a_ref) + up(r0_ref)) + up(r1_ref)) + up(r2_ref)

    def blk(cs, first):
        _, rh, cols = cs.shape
        return pl.BlockSpec((None, rh // 2, cols), lambda i, pa: (first(pa), i, 0))

    in_specs = [blk(cs, lambda pa: pa[0]) for cs in css]
    for cs in css:
        in_specs += [blk(cs, lambda pa, d=d: d) for d in range(3)]
    return pl.pallas_call(
        body, name=name, out_shape=[jax.ShapeDtypeStruct((2,) + cs.shape[1:], F32) for cs in css],
        grid_spec=pltpu.PrefetchScalarGridSpec(
            num_scalar_prefetch=1, grid=(2,), in_specs=in_specs, out_specs=[blk(cs, lambda pa: pa[1]) for cs in css]),
        compiler_params=_cp(("parallel",)),
    )(place_arr, *css, *[r3 for r3 in r3s for _ in range(3)])


def _allreduce_small(pack):
    rows = pack.shape[0]
    hr = rows // 2

    def body(p_ref, o_ref, sib, slots, s1, r1, s2, r2, s3, r3):
        x, y, c, chip = _place()
        sibling = (x, y, 1 - c)
        ex = pltpu.make_async_remote_copy(src_ref=p_ref, dst_ref=sib, send_sem=s1, recv_sem=r1,
                                          device_id=sibling, device_id_type=MESH)
        ex.start()
        ex.wait()
        half = pl.ds(pl.multiple_of(c * hr, 16), hr)
        slots[0] = p_ref[half, :] + sib[half, :]
        cps = []
        for d in (1, 2, 3):
            px, py, _ = _chip_at(x, y, d)
            cps.append(pltpu.make_async_remote_copy(
                src_ref=slots.at[0], dst_ref=slots.at[d], send_sem=s2.at[d - 1], recv_sem=r2.at[d - 1],
                device_id=(px, py, c), device_id_type=MESH))
        for cp in cps:
            cp.start()
        for cp in cps:
            cp.wait()
        tot = slots[chip]
        for k in (1, 2, 3):
            tot = tot + slots[jnp.bitwise_xor(chip, k)]
        o_ref[half, :] = tot
        back = pltpu.make_async_remote_copy(src_ref=o_ref.at[half, :], dst_ref=o_ref.at[half, :], send_sem=s3, recv_sem=r3,
                                            device_id=sibling, device_id_type=MESH)
        back.start()
        back.wait()

    vm = pl.BlockSpec(memory_space=pltpu.VMEM)
    return pl.pallas_call(
        body, name="allreduce_small", out_shape=jax.ShapeDtypeStruct((rows, 128), F32),
        in_specs=[vm], out_specs=vm,
        scratch_shapes=[pltpu.VMEM((rows, 128), F32), pltpu.VMEM((4, hr, 128), F32),
                        pltpu.SemaphoreType.DMA, pltpu.SemaphoreType.DMA, pltpu.SemaphoreType.DMA((3,)), pltpu.SemaphoreType.DMA((3,)),
                        pltpu.SemaphoreType.DMA, pltpu.SemaphoreType.DMA],
        compiler_params=_cp(has_side_effects=True),
    )(pack)


def _small_chip_sum(pack, after=()):
    rows = pack.shape[0]
    hr = rows // 2

    def body(p_ref, *rest):
        o_ref, sib, s1, r1 = rest[-4:]
        x, y, c, _ = _place()
        ex = pltpu.make_async_remote_copy(src_ref=p_ref, dst_ref=sib, send_sem=s1, recv_sem=r1,
                                          device_id=(x, y, 1 - c), device_id_type=MESH)
        ex.start()
        ex.wait()
        half = pl.ds(pl.multiple_of(c * hr, 16), hr)
        o_ref[...] = (p_ref[half, :] + sib[half, :]).astype(BF16)

    vm = pl.BlockSpec(memory_space=pltpu.VMEM)
    return pl.pallas_call(
        body, name="small_chip_sum", out_shape=jax.ShapeDtypeStruct((hr, 128), BF16),
        in_specs=[vm] + [ANY] * len(after), out_specs=vm,
        scratch_shapes=[pltpu.VMEM((rows, 128), F32), pltpu.SemaphoreType.DMA, pltpu.SemaphoreType.DMA],
        compiler_params=_cp(has_side_effects=True),
    )(pack, *after)


def _small_copies(c_refs, land_refs, send, recv):
    x, y, c, _ = _place()
    cps = []
    for d in (1, 2, 3):
        px, py, _ = _chip_at(x, y, d)
        cps.append(pltpu.make_async_remote_copy(src_ref=c_refs[0], dst_ref=land_refs[0].at[d - 1], send_sem=send.at[d - 1],
                                                recv_sem=recv.at[d - 1], device_id=(px, py, c), device_id_type=MESH))
    return cps


def _small_total(csum, land):
    hr = csum.shape[0]

    def body(c_ref, l_ref, o_ref, slots, s3, r3):
        x, y, c, chip = _place()
        slots[0] = c_ref[...]
        for d in (1, 2, 3):
            slots[d] = l_ref[d - 1]
        tot = slots[chip].astype(F32)
        for k in (1, 2, 3):
            tot = tot + slots[jnp.bitwise_xor(chip, k)].astype(F32)
        half = pl.ds(pl.multiple_of(c * hr, 16), hr)
        o_ref[half, :] = tot
        back = pltpu.make_async_remote_copy(src_ref=o_ref.at[half, :], dst_ref=o_ref.at[half, :], send_sem=s3, recv_sem=r3,
                                            device_id=(x, y, 1 - c), device_id_type=MESH)
        back.start()
        back.wait()

    vm = pl.BlockSpec(memory_space=pltpu.VMEM)
    return pl.pallas_call(
        body, name="small_total", out_shape=jax.ShapeDtypeStruct((2 * hr, 128), F32), in_specs=[vm, vm], out_specs=vm,
        scratch_shapes=[pltpu.VMEM((4, hr, 128), BF16), pltpu.SemaphoreType.DMA, pltpu.SemaphoreType.DMA],
        compiler_params=_cp(has_side_effects=True),
    )(csum, land)


def _adam_math(gv, wv, mv, vv):
    m2 = ADAM_B1 * mv + (1.0 - ADAM_B1) * gv
    v2 = ADAM_B2 * vv + (1.0 - ADAM_B2) * (gv * gv)
    m_hat = m2 / (1.0 - ADAM_B1 ** ADAM_STEP)
    v_hat = v2 / (1.0 - ADAM_B2 ** ADAM_STEP)
    return -ADAM_LR * (m_hat / (jnp.sqrt(v_hat) + ADAM_EPS) + ADAM_WD * wv), m2, v2


def _adam(g, w, m, v, name):
    rows, cols = g.shape
    rb = rows // 4

    def body(g_ref, w_ref, m_ref, v_ref, d_ref, m2_ref, v2_ref):
        d_ref[...], m2_ref[...], v2_ref[...] = _adam_math(g_ref[...], w_ref[...], m_ref[...], v_ref[...])

    blk = pl.BlockSpec((rb, cols), lambda i: (i, 0))
    shp = jax.ShapeDtypeStruct((rows, cols), F32)
    return pl.pallas_call(
        body, name=name, grid=(4,), in_specs=[blk] * 4, out_specs=[blk] * 3, out_shape=[shp] * 3,
        compiler_params=_cp(("parallel",)),
    )(g, w, m, v)


def _adam_layer(gs, ws, ms, vs, l, prevs, name):
    n = len(gs)
    prev = [a for p4 in prevs if p4 is not None for a in p4]

    def body(*refs):
        outs = refs[len(refs) - 4 * n:]
        for t in range(n):
            g_ref, w_ref, m_ref, v_ref = refs[4 * t:4 * t + 4]
            go_ref, d_ref, m2_ref, v2_ref = outs[4 * t:4 * t + 4]
            gv = g_ref[...]
            go_ref[...] = gv
            d_ref[...], m2_ref[...], v2_ref[...] = _adam_math(gv, w_ref[...], m_ref[...], v_ref[...])

    in_specs, out_specs, out_shape, operands, aliases = [], [], [], [], {}
    for t, g in enumerate(gs):
        rows, cols = g.shape
        lay = pl.BlockSpec((None, rows // 4, cols), lambda i: (l, i, 0))
        in_specs += [pl.BlockSpec((rows // 4, cols), lambda i: (i, 0)), lay, lay, lay]
        operands += [g, ws[t], ms[t], vs[t]]
        out_specs += [lay] * 4
        out_shape += [jax.ShapeDtypeStruct((2, rows, cols), F32)] * 4
    k = 4 * n
    for t, p4 in enumerate(prevs):
        if p4 is not None:
            for j in range(4):
                aliases[k] = 4 * t + j
                k += 1
    outs = pl.pallas_call(
        body, name=name, grid=(4,), in_specs=in_specs + [ANY] * len(prev), out_specs=out_specs, out_shape=out_shape,
        input_output_aliases=aliases, compiler_params=_cp(("parallel",)),
    )(*operands, *prev)
    return [list(outs[4 * t:4 * t + 4]) for t in range(n)]


def _rows128(a):
    return a.reshape(-1, 128)


def _pack(arrs, mult):
    parts = [_rows128(a) for a in arrs]
    rows = sum(q.shape[0] for q in parts)
    pad = -rows % mult
    if pad:
        parts.append(jnp.zeros((pad, 128), F32))
    return jnp.concatenate(parts, axis=0)


def _unpack(pack, shapes):
    out, o = [], 0
    for s in shapes:
        n = 1
        for e in s:
            n *= e
        out.append(pack[o:o + n // 128].reshape(s))
        o += n // 128
    return out


WEIGHTS = ['norm1_g', 'w_in', 'gmlp_ln_g', 'gmlp_ln_b', 'gmlp_w_s', 'gmlp_b_s', 'conv_w', 'conv_b', 'lru_w_r', 'lru_b_r', 'lru_w_i',
           'lru_b_i', 'lru_lambda', 'w_out', 'norm2_g', 'w_ffn_in', 'w_ffn_out', 'final_g']
BIG = ['w_in', 'w_out', 'w_ffn_in', 'w_ffn_out']
SMALL = [n for n in WEIGHTS if n not in BIG]
CHIP_SHARDED_SMALL = ['conv_w', 'lru_b_r', 'lru_b_i', 'lru_lambda']


def kernel(x, norm1_g, w_in, gmlp_ln_g, gmlp_ln_b, gmlp_w_s, gmlp_b_s, conv_w, conv_b, lru_w_r, lru_b_r, lru_w_i, lru_b_i, lru_lambda, w_out, norm2_g, w_ffn_in, w_ffn_out, final_g, loss_target, m_norm1_g, m_w_in, m_gmlp_ln_g, m_gmlp_ln_b, m_gmlp_w_s, m_gmlp_b_s, m_conv_w, m_conv_b, m_lru_w_r, m_lru_b_r, m_lru_w_i, m_lru_b_i, m_lru_lambda, m_w_out, m_norm2_g, m_w_ffn_in, m_w_ffn_out, m_final_g, v_norm1_g, v_w_in, v_gmlp_ln_g, v_gmlp_ln_b, v_gmlp_w_s, v_gmlp_b_s, v_conv_w, v_conv_b, v_lru_w_r, v_lru_b_r, v_lru_w_i, v_lru_b_i, v_lru_lambda, v_w_out, v_norm2_g, v_w_ffn_in, v_w_ffn_out, v_final_g):
    a = dict(locals())
    w = {n: a[n] for n in WEIGHTS}
    mom = {n: a["m_" + n] for n in WEIGHTS}
    var = {n: a["v_" + n] for n in WEIGHTS}
    _, _, c, chip = _place()
    c_arr, chip_arr = jnp.reshape(c, (1,)).astype(jnp.int32), jnp.reshape(chip, (1,)).astype(jnp.int32)
    place_arr = jnp.stack([chip, c]).astype(jnp.int32)

    first, rest = BIG[:1], BIG[1:]

    def as_weights(names, full):
        wb = {n: f.reshape(4, 2 * f.shape[2], f.shape[3]) for n, f in zip(names, full)}
        if "w_out" in wb:
            wb["w_out"] = wb["w_out"].reshape(D, D)
            wb["w_ffn_out"] = wb["w_ffn_out"].reshape(DFF, D)
        return wb

    def cast(names, l, tag):
        return _cast_into([w[n] for n in names], l, chip_arr, f"cast_{tag}")

    def landed(fly, names, after, tag):
        return as_weights(names, _gather_pass_on(_gather_wait(fly[0], fly[1], fly[2], after, tag), tag))

    tiny = _pack([w[n] for n in CHIP_SHARDED_SMALL], 8)
    _, tiny_full = _gather_weights([], tiny)
    fly_in = _gather_start(cast(first, 0, "in"), "in", after=(tiny_full,))
    fly0 = _gather_start(cast(rest, 0, "0"), "0", after=(fly_in[3],))
    fly1 = _gather_start(cast(BIG, 1, "1"), "1", after=(fly0[3],))
    p = {n: w[n] for n in SMALL}
    parts = [_unpack(tiny_full[k], [w[n].shape for n in CHIP_SHARDED_SMALL]) for k in range(4)]
    for i, n in enumerate(CHIP_SHARDED_SMALL):
        p[n] = jnp.concatenate([parts[k][i] for k in range(4)], axis=-1)

    operands = [_layer_operands(l, p) for l in range(2)]
    state_packs = [_pack([src[n] for n in SMALL], 32) for src in (w, mom, var)]
    ahead = tuple(jax.tree.leaves(operands)) + tuple(state_packs)

    passing = {}

    def pass_on_1(gu):
        bufs = _gather_wait(fly1[0], fly1[1], fly1[2], gu, "1")
        passing[1] = _exchange_start(bufs, [], _pass_on_copies, 3 * len(bufs), "gather_pass_on_start_1")
        return (passing[1][-1],)

    xa, saved0 = _forward_layer(0, x[0], p, landed(fly_in, first, (fly1[3],) + ahead, "in"), after=(fly0[3], fly1[3]),
                                rest=lambda merged: landed(fly0, rest, merged, "0"), near_end=pass_on_1, operands=operands[0])
    send, recv, bufs1, _, _ = passing[1]
    xb, saved1 = _forward_layer(
        1, xa, p, as_weights(BIG, _exchange_wait(send, recv, bufs1, [], xa, _pass_on_copies, "gather_pass_on_wait_1")[0]),
        operands=operands[1], loss=(loss_target[0], p["final_g"][None]))
    dxb, loss_v, dfg = xb

    out, flying = {}, {}

    def halves(grads):
        return [g.reshape(4, 2, -1, g.shape[-1]) for g in grads]

    def sibling_start(grads, names, l, tag):
        gs = halves(grads)
        lands = [lax.empty((4,) + g.shape[2:], g.dtype) for g in gs]
        flying["s" + tag] = (names, l) + tuple(
            _exchange_start(gs, lands, _sibling_copies, 4 * len(gs), f"grads_to_sibling_start_{tag}"))
        return (flying["s" + tag][-1],)

    def chips_start(gs, from_sib, names, l, tag):
        cs = _add_half(gs, from_sib, c_arr, f"add_half_{tag}")
        lands = [lax.empty((3,) + a.shape[1:], a.dtype) for a in cs]
        flying[tag] = (names, l) + tuple(_exchange_start(cs, lands, _chips_copies, 3 * len(cs), f"grads_to_chips_start_{tag}"))
        return (flying[tag][-1],)

    def sibling_finish(tag, after):
        names, l, send, recv, gs, lands, _ = flying["s" + tag]
        gs, from_sib = _exchange_wait(send, recv, gs, lands, after, _sibling_copies, f"grads_to_sibling_wait_{tag}")
        return chips_start(gs, from_sib, names, l, tag)

    def reduce_sums(tags, after):
        groups, ts = [], []
        for tag in tags:
            names, l, send, recv, cs, lands, _ = flying[tag]
            cs, lands = _exchange_wait(send, recv, cs, lands, after, _chips_copies, f"grads_to_chips_wait_{tag}")
            ts += _sum_chips(cs, lands, place_arr, f"sum_chips_{tag}")
            groups.append((tag, names))
        flying["j" + tags[0]] = (groups, l) + tuple(_exchange_start(ts, [], _join_copies, len(ts), f"grads_join_start_{tags[0]}"))
        return (flying["j" + tags[0]][-1],)

    def reduce_adam(tag0, after):
        groups, l, send, recv, ts, _, _ = flying["j" + tag0]
        joined = _exchange_wait(send, recv, ts, [], after, _join_copies, f"grads_join_wait_{tag0}")[0]
        for tag, names in groups:
            gs, joined = [j.reshape(w[n].shape[1:]) for n, j in zip(names, joined)], joined[len(names):]
            res = _adam_layer(gs, [w[n] for n in names], [mom[n] for n in names], [var[n] for n in names], l,
                              [out.get(n) for n in names], f"adam_{tag}")
            out.update(zip(names, res))

    def late1(grads):
        return sibling_finish("1a", grads[0]) + sibling_start(grads, first, 1, "1b")

    def midway0(grads):
        return reduce_sums(("1a", "1b"), grads[0]) + sibling_start(grads, rest, 0, "0a")

    def stacked_small(small0):
        small = {k: jnp.stack([small0[k], small1[k]]) for k in LAYER_SMALL}
        return dict(small, final_g=dfg[0])

    def midway3_0(small0):
        small = stacked_small(dict(small0, norm1_g=jnp.zeros((D,), F32)))
        csum = _small_chip_sum(_pack([small[n] for n in SMALL], 32))
        flying["small"] = _exchange_start([csum], [lax.empty((3,) + csum.shape, BF16)], _small_copies, 3, "small_to_chips_start")
        return (flying["small"][-1],)

    def late0(grads):
        tok = sibling_start(grads, first, 0, "0b")
        reduce_adam("1a", tok[0])
        return sibling_finish("0b", out[first[0]][0])

    dxa, big1, small1 = _backward_layer(1, dxb, saved1, midway=lambda grads: sibling_start(grads, rest, 1, "1a"), late=late1)
    dx, big0, small0 = _backward_layer(0, dxa, saved0, after=sibling_finish("1b", dxa), midway=midway0,
                                       midway2=lambda dws: sibling_finish("0a", dws), midway3=midway3_0, late=late0)
    join_tok = reduce_sums(("0a", "0b"), dx)
    small = stacked_small(small0)

    full_shapes = [small[n].shape for n in SMALL]
    send, recv, csum, land, _ = flying["small"]
    csum, land = _exchange_wait(send, recv, csum, land, join_tok[0], _small_copies, "small_to_chips_wait")
    red = _unpack(_small_total(csum[0], land[0]), full_shapes)
    last = _allreduce_small(_pack([small0["norm1_g"], jnp.tile(loss_v, (8, 1))], 32))
    norm1_0, loss = last[:D // 128].reshape(D), last[D // 128, 0]
    reduce_adam("0a", norm1_0)
    red[SMALL.index("norm1_g")] = red[SMALL.index("norm1_g")].at[0].set(norm1_0)
    g_small = []
    for n, g in zip(SMALL, red):
        if n in CHIP_SHARDED_SMALL:
            g = lax.dynamic_slice_in_dim(g, chip * w[n].shape[-1], w[n].shape[-1], axis=g.ndim - 1)
        g_small.append(g)
    shapes = [w[n].shape for n in SMALL]
    upd = [_unpack(u, shapes) for u in _adam(_pack(g_small, 32), *state_packs, "adam_small")]
    for i, n in enumerate(SMALL):
        out[n] = [g_small[i], upd[0][i], upd[1][i], upd[2][i]]

    return (loss, dx[None]) + tuple(out[n][i] for i in range(4) for n in WEIGHTS)
```

```python
import functools

import jax
import jax.numpy as jnp
from jax import lax
from jax.experimental import pallas as pl
from jax.experimental.pallas import tpu as pltpu

F32 = jnp.float32
BF16 = jnp.bfloat16
MESH = pl.DeviceIdType.MESH

D = 1024
NH = 8
HD = 128
CHUNK = 128
GMLP_ROWS = 512
N_IN_T = 12
DFF = 2816
DFF_SH = 1408
EPS = 1e-6
LRU_C = 8.0
ADAM_LR, ADAM_B1, ADAM_B2, ADAM_EPS, ADAM_WD, ADAM_STEP = 0.001, 0.9, 0.999, 1e-08, 0.01, 10

TM = 512
TM_BIG = 1024
RT = 128
PADR = 8
VMEM_LIMIT = 56 * 1024 * 1024


def _cp(sem=None, **kw):
    if sem is not None:
        kw["dimension_semantics"] = sem
    return pltpu.CompilerParams(vmem_limit_bytes=VMEM_LIMIT, **kw)


_GC = 0.7978845608028654


def _sigmoid(x):
    return 0.5 * jnp.tanh(0.5 * x) + 0.5


_GK = 0.044715


def _gelu(x):
    t = jnp.tanh(x * (_GC + (_GC * _GK) * (x * x)))
    return x * (0.5 + 0.5 * t)


def _gelu_and_grad(x):
    x2 = x * x
    t = jnp.tanh(x * (_GC + (_GC * _GK) * x2))
    h = 0.5 + 0.5 * t
    return x * h, h + x * (1.0 - t * t) * (0.5 * _GC + (1.5 * _GC * _GK) * x2)


def _softplus_neg(lam):
    y = jnp.exp(-jnp.abs(lam))
    u = 1.0 + y
    l1p = jnp.where(u == 1.0, y, jnp.log(u) * y / (u - 1.0))
    return jnp.maximum(-lam, 0.0) + l1p


def _dot(a, b):
    return jnp.dot(a, b, preferred_element_type=F32)


def _dot_nt(a, b):
    return lax.dot_general(a, b, (((1,), (1,)), ((), ())), preferred_element_type=F32)


def _dot_tn(a, b):
    return lax.dot_general(a, b, (((0,), (0,)), ((), ())), preferred_element_type=F32)


def _rms_hat(x):
    r = lax.rsqrt(jnp.mean(x * x, axis=-1, keepdims=True) + EPS)
    return x * r, r


def _rms_bwd(dh, x, g):
    xh, r = _rms_hat(x)
    dxh = dh * g
    dx = r * (dxh - xh * jnp.mean(dxh * xh, axis=-1, keepdims=True))
    return dx, jnp.sum(dh * xh, axis=0, keepdims=True)


def _norm_into(x_ref, g_ref, h_ref):
    xh, _ = _rms_hat(x_ref[...])
    h_ref[...] = (xh * g_ref[...]).astype(BF16)


def _in_tile(j):
    m, hf = j // 2, j % 2
    orig = jnp.where(m < 2, m, jnp.where(m == 2, 4, jnp.where(m < 5, m - 1, 5)))
    t = orig * 2 + hf
    return t // 3, t % 3


ANY = pl.BlockSpec(memory_space=pl.ANY)


def _mm_in(x, g, w_in, l, after=()):
    S = x.shape[0]
    tm = min(2 * TM_BIG, S)

    def body(x_ref, g_ref, w0_ref, w1_ref, *rest):
        o_ref, h_ref = rest[-2:]

        @pl.when(pl.program_id(1) == 0)
        def _():
            _norm_into(x_ref, g_ref, h_ref)
        rp = min(TM, tm)
        for r0 in range(0, tm, rp):
            hv = h_ref[r0:r0 + rp, :]
            o_ref[r0:r0 + rp, 0:512] = _dot(hv, w0_ref[...]).astype(BF16)
            o_ref[r0:r0 + rp, 512:1024] = _dot(hv, w1_ref[...]).astype(BF16)

    def w_tile(hf):
        def w_map(i, m):
            sh, tl = _in_tile(2 * m + hf)
            return (sh, 0, tl)
        return pl.BlockSpec((None, D, 512), w_map)

    return pl.pallas_call(
        body, name=f"mm_in_{l}", grid=(S // tm, 6),
        in_specs=[pl.BlockSpec((tm, D), lambda i, m: (i, 0)), pl.BlockSpec((1, D), lambda i, m: (0, 0)),
                  w_tile(0), w_tile(1)] + [ANY] * len(after),
        out_specs=[pl.BlockSpec((None, tm, D), lambda i, m: (m, i, 0)), pl.BlockSpec((tm, D), lambda i, m: (i, 0))],
        out_shape=[jax.ShapeDtypeStruct((6, S, D), BF16), jax.ShapeDtypeStruct((S, D), BF16)],
        compiler_params=_cp(("parallel", "arbitrary")),
    )(x, g, w_in, w_in, *after)


def _mm_res(a, w, res, l, name, after=()):
    S, K = a.shape

    tm = TM

    def body(a_ref, w_ref, r_ref, *rest):
        rest[-1][...] = r_ref[...] + _dot(a_ref[...], w_ref[...])

    return pl.pallas_call(
        body, name=f"{name}_{l}", grid=(S // tm,),
        in_specs=[pl.BlockSpec((tm, K), lambda i: (i, 0)), pl.BlockSpec((K, D), lambda i: (0, 0)),
                  pl.BlockSpec((tm, D), lambda i: (i, 0))] + [ANY] * len(after),
        out_specs=pl.BlockSpec((tm, D), lambda i: (i, 0)),
        out_shape=jax.ShapeDtypeStruct((S, D), F32),
        compiler_params=_cp(("parallel",)),
    )(a, w, res, *after)


def _mm_ffn_in(x, g, w_fi, l):
    S = x.shape[0]

    tm = min(TM_BIG, S)

    def body(x_ref, g_ref, w_ref, gu_ref, ff_ref, h_ref):
        @pl.when(pl.program_id(1) == 0)
        def _():
            _norm_into(x_ref, g_ref, h_ref)
        for r0 in range(0, tm, TM):
            rows = slice(r0, r0 + TM)
            hv = h_ref[rows, :]
            ga = _dot(hv, w_ref[0])
            gb = _dot(hv, w_ref[1])
            sg = _sigmoid(ga)
            silu = ga * sg
            gu_ref[0, rows, :] = (gb * (sg + silu * (1.0 - sg))).astype(BF16)
            gu_ref[1, rows, :] = silu.astype(BF16)
            ff_ref[rows, :] = (silu * gb).astype(BF16)

    gu, ff, h = pl.pallas_call(
        body, name=f"mm_ffn_in_{l}", grid=(S // tm, 2),
        in_specs=[pl.BlockSpec((tm, D), lambda i, s: (i, 0)), pl.BlockSpec((1, D), lambda i, s: (0, 0)),
                  pl.BlockSpec((2, None, D, DFF_SH), lambda i, s: (0, s, 0, 0))],
        out_specs=[pl.BlockSpec((2, None, tm, DFF_SH), lambda i, s: (0, s, i, 0)),
                   pl.BlockSpec((tm, DFF_SH), lambda i, s: (i, s)),
                   pl.BlockSpec((tm, D), lambda i, s: (i, 0))],
        out_shape=[jax.ShapeDtypeStruct((2, 2, S, DFF_SH), BF16), jax.ShapeDtypeStruct((S, DFF), BF16),
                   jax.ShapeDtypeStruct((S, D), BF16)],
        compiler_params=_cp(("parallel", "arbitrary")),
    )(x, g, w_fi.reshape(2, 2, D, DFF_SH))
    return gu.reshape(4, S, DFF_SH), ff, h


def _gmlp_fwd(z6, ws_b, bs_b, lg, lb):
    S = z6.shape[1]

    ts = min(GMLP_ROWS, S)

    def body(z_ref, ws_ref, bs_ref, lg_ref, lb_ref, o_ref, mix):
        for r0 in range(0, ts, CHUNK):
            rows = slice(r0, r0 + CHUNK)
            gv = _gelu(z_ref[1, rows, :].astype(F32))
            xc = gv - jnp.mean(gv, axis=-1, keepdims=True)
            rs = lax.rsqrt(jnp.mean(xc * xc, axis=-1, keepdims=True) + EPS)
            vb = (xc * rs * lg_ref[...] + lb_ref[...]).astype(BF16)
            for gi in range(NH):
                cs = slice(gi * HD, (gi + 1) * HD)
                mix[rows, cs] = _dot(ws_ref[gi], vb[:, cs])
            o_ref[rows, :] = (_sigmoid(z_ref[2, rows, :].astype(F32)) * _gelu(z_ref[0, rows, :].astype(F32))
                              * (mix[rows, :] + bs_ref[...])).astype(BF16)

    return pl.pallas_call(
        body, name="gmlp_fwd", grid=(S // ts,),
        in_specs=[pl.BlockSpec((3, ts, D), lambda i: (0, i, 0)), pl.BlockSpec((NH, CHUNK, CHUNK), lambda i: (0, 0, 0)),
                  pl.BlockSpec((CHUNK, D), lambda i: (0, 0)), pl.BlockSpec((1, D), lambda i: (0, 0)),
                  pl.BlockSpec((1, D), lambda i: (0, 0))],
        out_specs=pl.BlockSpec((ts, D), lambda i: (i, 0)),
        out_shape=jax.ShapeDtypeStruct((S, D), BF16),
        scratch_shapes=[pltpu.VMEM((ts, D), F32)],
        compiler_params=_cp(("parallel",)),
    )(z6, ws_b, bs_b, lg, lb)


def _row_iota():
    return lax.broadcasted_iota(jnp.int32, (RT, HD), 0)


SUB = 8
UNROLL = 8
GRAD_ROWS = 512


def _scan_up(a, b, carry):
    row = lax.broadcasted_iota(jnp.int32, (SUB, HD), 0)
    masks = [(d, row >= d) for d in (1, 2, 4)]
    c = jnp.broadcast_to(carry, (SUB, HD))
    hs = []
    for j in range(RT // SUB):
        aj, bj = a[SUB * j:SUB * (j + 1)], b[SUB * j:SUB * (j + 1)]
        for d, m in masks:
            bj = bj + aj * jnp.where(m, pltpu.roll(bj, d, 0), 0.0)
            aj = aj * jnp.where(m, pltpu.roll(aj, d, 0), 1.0)
        h = bj + aj * c
        hs.append(h)
        c = jnp.broadcast_to(h[SUB - 1:SUB, :], (SUB, HD))
    return jnp.concatenate(hs, axis=0), hs[-1][SUB - 1:SUB, :]


def _scan_down(a, b, carry):
    row = lax.broadcasted_iota(jnp.int32, (SUB, HD), 0)
    masks = [(d, row < SUB - d) for d in (1, 2, 4)]
    c = jnp.broadcast_to(carry, (SUB, HD))
    hs = []
    for j in reversed(range(RT // SUB)):
        aj, bj = a[SUB * j:SUB * (j + 1)], b[SUB * j:SUB * (j + 1)]
        for d, m in masks:
            bj = bj + aj * jnp.where(m, pltpu.roll(bj, SUB - d, 0), 0.0)
            aj = aj * jnp.where(m, pltpu.roll(aj, SUB - d, 0), 1.0)
        h = bj + aj * c
        hs.append(h)
        c = jnp.broadcast_to(h[0:1, :], (SUB, HD))
    return jnp.concatenate(hs[::-1], axis=0), hs[-1][0:1, :]


def _decay(r, sp_d):
    log_a = -LRU_C * r * sp_d
    a = jnp.exp(log_a)
    return a, jnp.sqrt(jnp.maximum(-jnp.tanh(log_a) * (a * a + 1.0), 0.0))


def _decay_bwd(r, sp_d):
    log_a = -LRU_C * r * sp_d
    a = jnp.exp(log_a)
    m2 = jnp.maximum(-jnp.tanh(log_a) * (a * a + 1.0), 0.0)
    inv = jnp.where(m2 > 0.0, lax.rsqrt(m2), 0.0)
    return a, m2 * inv, inv


def _lru_gates(xc, d, wr_ref, br_ref, wi_ref, bi_ref, sp):
    xb = xc.astype(BF16)
    r = _sigmoid(_dot(xb, wr_ref[d]) + br_ref[d:d + 1, :])
    i = _sigmoid(_dot(xb, wi_ref[d]) + bi_ref[d:d + 1, :])
    a, mult = _decay(r, sp[d:d + 1, :])
    return r, i, a, mult


def _shifted(win, k):
    w = RT + 2 * PADR
    v = win if k == 0 else pltpu.roll(win, (-k) % w, 0)
    return v[PADR:PADR + RT]


def _conv_taps(win):
    return [_shifted(win, k) for k in (-1, 0, 1, 2)]


def _fill_padded(dst, src_ref, S):
    zeros = jnp.zeros((PADR, HD), F32)
    dst[0:PADR, :] = zeros
    dst[PADR + S:2 * PADR + S, :] = zeros

    def cp(i, c):
        t0 = pl.multiple_of(i * RT, RT)
        dst[pl.ds(t0 + PADR, RT), :] = src_ref[pl.ds(t0, RT), :].astype(F32)
        return c
    lax.fori_loop(0, S // RT, cp, 0)


def _conv_fwd_all(zxp, xc_s, cw_ref, cb_ref, S):
    def cv(i, c):
        t0 = pl.multiple_of(i * RT, RT)
        xm1, x0, xp1, xp2 = _conv_taps(zxp[pl.ds(t0, RT + 2 * PADR), :])
        xc_s[pl.ds(t0, RT), :] = (cb_ref[...] + xm1 * cw_ref[0:1, :] + x0 * cw_ref[1:2, :]
                                  + xp1 * cw_ref[2:3, :] + xp2 * cw_ref[3:4, :])
        return c
    lax.fori_loop(0, S // RT, cv, 0)


def _lru_specs(S):
    head = lambda h: (0, h)
    return [pl.BlockSpec((4, HD), head), pl.BlockSpec((1, HD), head),
            pl.BlockSpec((2, None, HD, HD), lambda h: (0, h, 0, 0)), pl.BlockSpec((2, HD), head),
            pl.BlockSpec((2, None, HD, HD), lambda h: (0, h, 0, 0)), pl.BlockSpec((2, HD), head),
            pl.BlockSpec((2, HD), head)]


def _lru_fwd(z6, ya, cw, cb, wr, br, wi, bi, lam, after=()):
    S = z6.shape[1]
    nt = S // RT

    def body(z_ref, ya_ref, cw_ref, cb_ref, wr_ref, br_ref, wi_ref, bi_ref, lam_ref, *rest):
        mg_ref, h0_ref, h1_ref, zxp, xc_s = rest[-5:]
        sp = _softplus_neg(lam_ref[...])
        _fill_padded(zxp, z_ref.at[0], S)
        _conv_fwd_all(zxp, xc_s, cw_ref, cb_ref, S)

        def scans(i, carry):
            cu, cd = carry
            for u in range(UNROLL):
                j = i * UNROLL + u
                ru = pl.ds(pl.multiple_of(j * RT, RT), RT)
                rd = pl.ds(pl.multiple_of((nt - 1 - j) * RT, RT), RT)
                xu, xd = xc_s[ru, :], xc_s[rd, :]
                _, gi, a, mult = _lru_gates(xu, 0, wr_ref, br_ref, wi_ref, bi_ref, sp)
                hu, cu = _scan_up(a, mult * gi * xu, cu)
                h0_ref[ru, :] = hu
                _, gi, a, mult = _lru_gates(xd, 1, wr_ref, br_ref, wi_ref, bi_ref, sp)
                hd, cd = _scan_down(a, mult * gi * xd, cd)
                h1_ref[rd, :] = hd
            return cu, cd
        z1 = jnp.zeros((1, HD), F32)
        lax.fori_loop(0, nt // UNROLL, scans, (z1, z1))

        def merge(i, c):
            rows = pl.ds(pl.multiple_of(i * RT, RT), RT)
            yb = (h0_ref[rows, :] + h1_ref[rows, :]) * _gelu(z_ref[1, rows, :].astype(F32))
            mg_ref[rows, :] = (ya_ref[rows, :].astype(F32) + _sigmoid(z_ref[2, rows, :].astype(F32)) * yb).astype(BF16)
            return c
        lax.fori_loop(0, nt, merge, 0)

    col = pl.BlockSpec((S, HD), lambda h: (0, h))
    return pl.pallas_call(
        body, name="lru_fwd", grid=(NH,),
        in_specs=[pl.BlockSpec((3, S, HD), lambda h: (1, 0, h)), col] + _lru_specs(S) + [ANY] * len(after),
        out_specs=[col, col, col],
        out_shape=[jax.ShapeDtypeStruct((S, D), BF16), jax.ShapeDtypeStruct((S, D), F32), jax.ShapeDtypeStruct((S, D), F32)],
        scratch_shapes=[pltpu.VMEM((S + 2 * PADR, HD), F32), pltpu.VMEM((S, HD), F32)],
        compiler_params=_cp(("parallel",)),
    )(z6, ya, cw, cb, wr, br, wi, bi, lam, *after)


def _mm_res_loss(a, w, res, tgt, g):
    S, K = a.shape

    def body(a_ref, w_ref, r_ref, t_ref, g_ref, dx_ref, loss_ref, dg_ref):
        @pl.when(pl.program_id(0) == 0)
        def _():
            loss_ref[...] = jnp.zeros_like(loss_ref)
            dg_ref[...] = jnp.zeros_like(dg_ref)
        xv = r_ref[...] + _dot(a_ref[...], w_ref[...])
        xh, _ = _rms_hat(xv)
        e = xh * g_ref[...] - t_ref[...]
        loss_ref[...] += jnp.sum(e * e) * (0.5 / D)
        dx, dgs = _rms_bwd(e * (1.0 / D), xv, g_ref[...])
        dx_ref[...] = dx
        dg_ref[...] += dgs

    row = pl.BlockSpec((TM, D), lambda i: (i, 0))
    vec = pl.BlockSpec((1, D), lambda i: (0, 0))
    return pl.pallas_call(
        body, name="mm_ffn_out_loss", grid=(S // TM,),
        in_specs=[pl.BlockSpec((TM, K), lambda i: (i, 0)), pl.BlockSpec((K, D), lambda i: (0, 0)), row, row, vec],
        out_specs=[row, pl.BlockSpec((1, 128), lambda i: (0, 0)), vec],
        out_shape=[jax.ShapeDtypeStruct((S, D), F32), jax.ShapeDtypeStruct((1, 128), F32), jax.ShapeDtypeStruct((1, D), F32)],
        compiler_params=_cp(("arbitrary",)),
    )(a, w, res, tgt, g)


def _bwd_ffn_out(dx, w_fo, gu, l, after=()):
    S = dx.shape[0]

    tm = min(TM_BIG, S)

    def body(dx_ref, w_ref, gu_ref, *rest):
        o_ref = rest[-1]
        for r0 in range(0, tm, TM):
            rows = slice(r0, r0 + TM)
            d = _dot_nt(dx_ref[rows, :].astype(BF16), w_ref[...])
            o_ref[0, rows, :] = (d * gu_ref[0, rows, :].astype(F32)).astype(BF16)
            o_ref[1, rows, :] = (d * gu_ref[1, rows, :].astype(F32)).astype(BF16)

    pair = pl.BlockSpec((2, None, tm, DFF_SH), lambda i, s: (0, s, i, 0))
    dgu = pl.pallas_call(
        body, name=f"bwd_ffn_out_{l}", grid=(S // tm, 2),
        in_specs=[pl.BlockSpec((tm, D), lambda i, s: (i, 0)), pl.BlockSpec((DFF_SH, D), lambda i, s: (s, 0)), pair]
        + [ANY] * len(after),
        out_specs=pair,
        out_shape=jax.ShapeDtypeStruct((2, 2, S, DFF_SH), BF16),
        compiler_params=_cp(("parallel", "arbitrary")),
    )(dx, w_fo, gu.reshape(2, 2, S, DFF_SH), *after)
    return dgu.reshape(4, S, DFF_SH)


def _mm_tn(a, b, m_blk, tk, name):
    S, M = a.shape

    def body(a_ref, b_ref, o_ref):
        @pl.when(pl.program_id(1) == 0)
        def _():
            o_ref[...] = jnp.zeros_like(o_ref)
        o_ref[...] += _dot_tn(a_ref[...], b_ref[...].astype(BF16))

    return pl.pallas_call(
        body, name=name, grid=(M // m_blk, S // tk),
        in_specs=[pl.BlockSpec((tk, m_blk), lambda m, k: (k, m)), pl.BlockSpec((tk, D), lambda m, k: (k, 0))],
        out_specs=pl.BlockSpec((m_blk, D), lambda m, k: (m, 0)),
        out_shape=jax.ShapeDtypeStruct((M, D), F32),
        compiler_params=_cp(("parallel", "arbitrary")),
    )(a, b)


def _mm_nt_rms_bwd(a, a_specs, w, w_specs, nk, tm, x, g, dres, name, after=()):
    S = x.shape[0]
    sub = len(a_specs)

    def body(*refs):
        a_refs, w_refs = refs[:sub], refs[sub:2 * sub]
        x_ref, g_ref, r_ref = refs[2 * sub:2 * sub + 3]
        dx_ref, dg_ref, acc = refs[-3:]
        i, k = pl.program_id(0), pl.program_id(1)
        @pl.when(k == 0)
        def _():
            acc[...] = jnp.zeros_like(acc)
        for j in range(sub):
            acc[...] += _dot_nt(a_refs[j][...], w_refs[j][...])

        @pl.when(jnp.logical_and(i == 0, k == 0))
        def _():
            dg_ref[...] = jnp.zeros_like(dg_ref)

        @pl.when(k == nk - 1)
        def _():
            dx, dgs = _rms_bwd(acc[...], x_ref[...], g_ref[...])
            dx_ref[...] = r_ref[...] + dx
            dg_ref[...] += dgs

    row = pl.BlockSpec((tm, D), lambda i, k: (i, 0))
    vec = pl.BlockSpec((1, D), lambda i, k: (0, 0))
    return pl.pallas_call(
        body, name=name, grid=(S // tm, nk),
        in_specs=list(a_specs) + list(w_specs) + [row, vec, row] + [ANY] * len(after),
        out_specs=[row, vec],
        out_shape=[jax.ShapeDtypeStruct((S, D), F32), jax.ShapeDtypeStruct((1, D), F32)],
        scratch_shapes=[pltpu.VMEM((tm, D), F32)],
        compiler_params=_cp(("arbitrary", "arbitrary")),
    )(*[a] * sub, *[w] * sub, x, g, dres, *after)


def _dw_ffn_in(h, dgu, l):
    S = h.shape[0]

    def body(h_ref, b_ref, o_ref):
        @pl.when(pl.program_id(1) == 0)
        def _():
            o_ref[...] = jnp.zeros_like(o_ref)
        o_ref[...] += _dot_tn(h_ref[...], b_ref[...])

    tk = min(2 * TM_BIG, S)
    return pl.pallas_call(
        body, name=f"dw_ffn_in_{l}", grid=(4, S // tk),
        in_specs=[pl.BlockSpec((tk, D), lambda j, k: (k, 0)), pl.BlockSpec((None, tk, DFF_SH), lambda j, k: (j, k, 0))],
        out_specs=pl.BlockSpec((None, D, DFF_SH), lambda j, k: (j, 0, 0)),
        out_shape=jax.ShapeDtypeStruct((4, D, DFF_SH), F32),
        compiler_params=_cp(("parallel", "arbitrary")),
    )(h, dgu)


_HALF_COMPS = ((0, 1, 3), (4, 2, 5))


def _dw_in(h, dz6, l, after=()):
    S = h.shape[0]

    def body(h_ref, d0_ref, d1_ref, d2_ref, *rest):
        o_ref = rest[-1]

        @pl.when(pl.program_id(1) == 0)
        def _():
            o_ref[...] = jnp.zeros_like(o_ref)
        hv = h_ref[...]
        for q, d_ref in enumerate((d0_ref, d1_ref, d2_ref)):
            for hf in range(2):
                col = 1024 * q + 512 * hf
                o_ref[col // 1536, :, col % 1536:col % 1536 + 512] += _dot_tn(hv, d_ref[:, 512 * hf:512 * (hf + 1)])

    tk = min(TM_BIG, S)

    def comp(q):
        return pl.BlockSpec((None, tk, D), lambda p, k: (jnp.where(p == 0, _HALF_COMPS[0][q], _HALF_COMPS[1][q]), k, 0))

    return pl.pallas_call(
        body, name=f"dw_in_{l}", grid=(2, S // tk),
        in_specs=[pl.BlockSpec((tk, D), lambda p, k: (k, 0)), comp(0), comp(1), comp(2)] + [ANY] * len(after),
        out_specs=pl.BlockSpec((2, D, 1536), lambda p, k: (p, 0, 0)),
        out_shape=jax.ShapeDtypeStruct((4, D, 1536), F32),
        compiler_params=_cp(("parallel", "arbitrary")),
    )(h, dz6, dz6, dz6, *after)


def _bwd_out(dx, w_o, merged, l):
    S = dx.shape[0]

    def body(dx_ref, w_ref, m_ref, dm_ref, dw_ref):
        @pl.when(pl.program_id(0) == 0)
        def _():
            dw_ref[...] = jnp.zeros_like(dw_ref)
        dxb = dx_ref[...].astype(BF16)
        dm_ref[...] = _dot_nt(dxb, w_ref[...]).astype(BF16)
        dw_ref[...] += _dot_tn(m_ref[...], dxb)

    tm = TM
    row = pl.BlockSpec((tm, D), lambda i: (i, 0))
    return pl.pallas_call(
        body, name=f"bwd_out_{l}", grid=(S // tm,),
        in_specs=[row, pl.BlockSpec((D, D), lambda i: (0, 0)), row],
        out_specs=[row, pl.BlockSpec((D, D), lambda i: (0, 0))],
        out_shape=[jax.ShapeDtypeStruct((S, D), BF16), jax.ShapeDtypeStruct((D, D), F32)],
        compiler_params=_cp(("arbitrary",)),
    )(dx, w_o, merged)


def _gmlp_bwd(dm, z6, ws_b, wst_b, bs_b, lg, lb, after=()):
    S = z6.shape[1]
    ts = min(GMLP_ROWS, S)

    def body(dm_ref, z_ref, ws_ref, wst_ref, bs_ref, lg_ref, lb_ref, *rest):
        dz_ref, dws_ref, dbs_ref, dlg_ref, dlb_ref, mix, dv = rest[-7:]

        @pl.when(pl.program_id(0) == 0)
        def _():
            dws_ref[...] = jnp.zeros_like(dws_ref)
            dbs_ref[...] = jnp.zeros_like(dbs_ref)
            dlg_ref[...] = jnp.zeros_like(dlg_ref)
            dlb_ref[...] = jnp.zeros_like(dlb_ref)
        for r0 in range(0, ts, CHUNK):
            rows = slice(r0, r0 + CHUNK)
            gv, dgelu_v = _gelu_and_grad(z_ref[1, rows, :].astype(F32))
            xc = gv - jnp.mean(gv, axis=-1, keepdims=True)
            rs = lax.rsqrt(jnp.mean(xc * xc, axis=-1, keepdims=True) + EPS)
            vh = xc * rs
            vb = (vh * lg_ref[...] + lb_ref[...]).astype(BF16)
            for gi in range(NH):
                cs = slice(gi * HD, (gi + 1) * HD)
                mix[rows, cs] = _dot(ws_ref[gi], vb[:, cs])
            u, dgelu_u = _gelu_and_grad(z_ref[0, rows, :].astype(F32))
            sa = _sigmoid(z_ref[2, rows, :].astype(F32))
            dya = dm_ref[rows, :].astype(F32) * sa
            dym = dya * (mix[rows, :] + bs_ref[...])
            dz_ref[2, rows, :] = (dym * u * (1.0 - sa)).astype(BF16)
            dz_ref[0, rows, :] = (dym * dgelu_u).astype(BF16)
            dmix = dya * u
            dmb = dmix.astype(BF16)
            for gi in range(NH):
                cs = slice(gi * HD, (gi + 1) * HD)
                dv[rows, cs] = _dot(wst_ref[gi], dmb[:, cs])
                dws_ref[gi] += _dot_nt(dmb[:, cs], vb[:, cs])
                dbs_ref[gi] += jnp.broadcast_to(jnp.sum(dmix[:, cs], axis=1, keepdims=True), (CHUNK, HD))
            dvv = dv[rows, :]
            dlg_ref[...] += jnp.sum(dvv * vh, axis=0, keepdims=True)
            dlb_ref[...] += jnp.sum(dvv, axis=0, keepdims=True)
            dvh = dvv * lg_ref[...]
            dgv = rs * (dvh - jnp.mean(dvh, axis=-1, keepdims=True) - vh * jnp.mean(dvh * vh, axis=-1, keepdims=True))
            dz_ref[1, rows, :] = (dgv * dgelu_v).astype(BF16)

    vec = pl.BlockSpec((1, D), lambda i: (0, 0))
    mat = pl.BlockSpec((NH, CHUNK, CHUNK), lambda i: (0, 0, 0))
    return pl.pallas_call(
        body, name="gmlp_bwd", grid=(S // ts,),
        in_specs=[pl.BlockSpec((ts, D), lambda i: (i, 0)), pl.BlockSpec((3, ts, D), lambda i: (0, i, 0)), mat, mat,
                  pl.BlockSpec((CHUNK, D), lambda i: (0, 0)), vec, vec] + [ANY] * len(after),
        out_specs=[pl.BlockSpec((3, ts, D), lambda i: (0, i, 0)), mat, mat, vec, vec],
        out_shape=[jax.ShapeDtypeStruct((6, S, D), BF16), jax.ShapeDtypeStruct((NH, CHUNK, CHUNK), F32),
                   jax.ShapeDtypeStruct((NH, CHUNK, HD), F32), jax.ShapeDtypeStruct((1, D), F32), jax.ShapeDtypeStruct((1, D), F32)],
        scratch_shapes=[pltpu.VMEM((ts, D), F32), pltpu.VMEM((ts, D), F32)],
        compiler_params=_cp(("arbitrary",)),
    )(dm, z6, ws_b, wst_b, bs_b, lg, lb, *after)


def _lru_bwd(dz6, dm, z6, h0, h1, cw, cb, wr, br, wi, bi, lam, after=()):
    S = z6.shape[1]
    nt = S // RT

    def body(dz_in, dm_ref, z_ref, h0_ref, h1_ref, cw_ref, cb_ref, wr_ref, br_ref, wi_ref, bi_ref, lam_ref, *rest):
        dz_ref, dcw_ref, dcb_ref, dwr_ref, dbr_ref, dwi_ref, dbi_ref, dlam_ref, zxp, xc_s, dhs_s, dxcp, r_s, lam_s = rest[-14:]
        del dz_in
        lam = lam_ref[...]
        sp = _softplus_neg(lam)
        row = _row_iota()
        _fill_padded(zxp, z_ref.at[0], S)
        _conv_fwd_all(zxp, xc_s, cw_ref, cb_ref, S)
        zeros = jnp.zeros((PADR, HD), F32)
        dxcp[0:PADR, :] = zeros
        dxcp[PADR + S:2 * PADR + S, :] = zeros
        dwr_ref[...] = jnp.zeros_like(dwr_ref)
        dwi_ref[...] = jnp.zeros_like(dwi_ref)

        def pre(i, c):
            rows = pl.ds(pl.multiple_of(i * RT, RT), RT)
            hs = h0_ref[rows, :] + h1_ref[rows, :]
            dmv = dm_ref[rows, :].astype(F32)
            sb = _sigmoid(z_ref[2, rows, :].astype(F32))
            gg, dgg = _gelu_and_grad(z_ref[1, rows, :].astype(F32))
            dz_ref[2, rows, :] = (dmv * hs * gg * sb * (1.0 - sb)).astype(BF16)
            dyb = dmv * sb
            dz_ref[1, rows, :] = (dyb * hs * dgg).astype(BF16)
            dhs_s[rows, :] = dyb * gg
            return c
        lax.fori_loop(0, nt, pre, 0)

        def gate_bwd(d, gates, lamv, da, xc):
            r, gi, a, mult, inv_mult = gates
            lx, lm = lamv * xc, lamv * mult
            dlog_r = (da - (lx * gi) * (a * inv_mult)) * a * r
            dpr = dlog_r * (1.0 - r) * (-LRU_C * sp[d:d + 1, :])
            dpi = (lx * mult) * gi * (1.0 - gi)
            xb, dprb, dpib = xc.astype(BF16), dpr.astype(BF16), dpi.astype(BF16)
            dwr_ref[d] += _dot_tn(xb, dprb)
            dwi_ref[d] += _dot_tn(xb, dpib)
            dxc = lm * gi + _dot_nt(dprb, wr_ref[d]) + _dot_nt(dpib, wi_ref[d])
            return dxc, (jnp.sum(dlog_r, axis=0, keepdims=True) * (-LRU_C), jnp.sum(dpr, axis=0, keepdims=True),
                         jnp.sum(dpi, axis=0, keepdims=True))

        def rgates(i, c):
            for u in range(UNROLL):
                rows = pl.ds(pl.multiple_of((i * UNROLL + u) * RT, RT), RT)
                xb = xc_s[rows, :].astype(BF16)
                for d in range(2):
                    r_s[d, rows, :] = _sigmoid(_dot(xb, wr_ref[d]) + br_ref[d:d + 1, :])
            return c
        lax.fori_loop(0, nt // UNROLL, rgates, 0)

        def chains(i, carry):
            qn, qp = carry
            for u in range(UNROLL):
                j = i * UNROLL + u
                rd = pl.ds(pl.multiple_of((nt - 1 - j) * RT, RT), RT)
                a, dhs = _decay(r_s[0, rd, :], sp[0:1, :])[0], dhs_s[rd, :]
                q, q_first = _scan_down(a, a * dhs, qn)
                lam_s[0, rd, :] = dhs + jnp.where(row == RT - 1, qn, pltpu.roll(q, RT - 1, 0))
                qn = q_first
                ru = pl.ds(pl.multiple_of(j * RT, RT), RT)
                a, dhs = _decay(r_s[1, ru, :], sp[1:2, :])[0], dhs_s[ru, :]
                q, q_last = _scan_up(a, a * dhs, qp)
                lam_s[1, ru, :] = dhs + jnp.where(row == 0, qp, pltpu.roll(q, 1, 0))
                qp = q_last
            return qn, qp

        z1 = jnp.zeros((1, HD), F32)
        lax.fori_loop(0, nt // UNROLL, chains, (z1, z1))

        ct = min(GRAD_ROWS, S)
        crow = lax.broadcasted_iota(jnp.int32, (ct, HD), 0)

        def tile_grads(i, acc):
            t0 = pl.multiple_of(i * ct, ct)
            rows = pl.ds(t0, ct)
            xc = xc_s[rows, :]
            xb = xc.astype(BF16)
            tp = pl.multiple_of(jnp.maximum(t0 - PADR, 0), PADR)
            prev = jnp.where(t0 > 0, h0_ref[pl.ds(tp, PADR), :][PADR - 1:PADR, :], 0.0)
            tn = pl.multiple_of(jnp.minimum(t0 + ct, S - PADR), PADR)
            nxt = jnp.where(t0 + ct < S, h1_ref[pl.ds(tn, PADR), :][0:1, :], 0.0)
            hside = (jnp.where(crow == 0, prev, pltpu.roll(h0_ref[rows, :], 1, 0)),
                     jnp.where(crow == ct - 1, nxt, pltpu.roll(h1_ref[rows, :], ct - 1, 0)))
            dxc, sums = 0.0, ()
            for d in range(2):
                r = r_s[d, rows, :]
                gi = _sigmoid(_dot(xb, wi_ref[d]) + bi_ref[d:d + 1, :])
                lamv = lam_s[d, rows, :]
                dxc_d, s_d = gate_bwd(d, (r, gi) + _decay_bwd(r, sp[d:d + 1, :]), lamv, lamv * hside[d], xc)
                dxc = dxc + dxc_d
                sums = sums + s_d
            dxcp[pl.ds(t0 + PADR, ct), :] = dxc
            return tuple(x + y for x, y in zip(acc, sums))

        s_sp0, s_br0, s_bi0, s_sp1, s_br1, s_bi1 = lax.fori_loop(0, S // ct, tile_grads, (z1,) * 6)

        dsp = jnp.concatenate([s_sp0, s_sp1], axis=0)
        dlam_ref[...] = -dsp * _sigmoid(-lam)
        dbr_ref[...] = jnp.concatenate([s_br0, s_br1], axis=0)
        dbi_ref[...] = jnp.concatenate([s_bi0, s_bi1], axis=0)

        def conv_bwd(i, carry):
            c0, c1, c2, c3, cb_ = carry
            t0 = pl.multiple_of(i * RT, RT)
            dwin = dxcp[pl.ds(t0, RT + 2 * PADR), :]
            d0 = _shifted(dwin, 0)
            dz_ref[0, pl.ds(t0, RT), :] = (_shifted(dwin, 1) * cw_ref[0:1, :] + d0 * cw_ref[1:2, :]
                                           + _shifted(dwin, -1) * cw_ref[2:3, :] + _shifted(dwin, -2) * cw_ref[3:4, :]).astype(BF16)
            xm1, x0, xp1, xp2 = _conv_taps(zxp[pl.ds(t0, RT + 2 * PADR), :])
            sm = lambda v: jnp.sum(v, axis=0, keepdims=True)
            return c0 + sm(d0 * xm1), c1 + sm(d0 * x0), c2 + sm(d0 * xp1), c3 + sm(d0 * xp2), cb_ + sm(d0)

        c0, c1, c2, c3, cb_ = lax.fori_loop(0, nt, conv_bwd, (z1, z1, z1, z1, z1))
        dcw_ref[...] = jnp.concatenate([c0, c1, c2, c3], axis=0)
        dcb_ref[...] = cb_

    col = pl.BlockSpec((S, HD), lambda h: (0, h))
    head = lambda h: (0, h)
    wspec = pl.BlockSpec((2, None, HD, HD), lambda h: (0, h, 0, 0))
    return pl.pallas_call(
        body, name="lru_bwd", grid=(NH,),
        in_specs=[pl.BlockSpec(memory_space=pl.ANY), col, pl.BlockSpec((3, S, HD), lambda h: (1, 0, h)), col, col] + _lru_specs(S)
        + [ANY] * len(after),
        out_specs=[pl.BlockSpec((3, S, HD), lambda h: (1, 0, h)), pl.BlockSpec((4, HD), head), pl.BlockSpec((1, HD), head),
                   wspec, pl.BlockSpec((2, HD), head), wspec, pl.BlockSpec((2, HD), head), pl.BlockSpec((2, HD), head)],
        out_shape=[jax.ShapeDtypeStruct((6, S, D), BF16), jax.ShapeDtypeStruct((4, D), F32), jax.ShapeDtypeStruct((1, D), F32),
                   jax.ShapeDtypeStruct((2, NH, HD, HD), F32), jax.ShapeDtypeStruct((2, D), F32),
                   jax.ShapeDtypeStruct((2, NH, HD, HD), F32), jax.ShapeDtypeStruct((2, D), F32), jax.ShapeDtypeStruct((2, D), F32)],
        scratch_shapes=[pltpu.VMEM((S + 2 * PADR, HD), F32), pltpu.VMEM((S, HD), F32), pltpu.VMEM((S, HD), F32),
                        pltpu.VMEM((S + 2 * PADR, HD), F32), pltpu.VMEM((2, S, HD), F32), pltpu.VMEM((2, S, HD), F32)],
        input_output_aliases={0: 0},
        compiler_params=_cp(("parallel",)),
    )(dz6, dm, z6, h0, h1, cw, cb, wr, br, wi, bi, lam, *after)


LAYER_SMALL = ("norm1_g", "gmlp_ln_g", "gmlp_ln_b", "gmlp_w_s", "gmlp_b_s", "conv_w", "conv_b",
               "lru_w_r", "lru_b_r", "lru_w_i", "lru_b_i", "lru_lambda", "norm2_g")


def _layer_operands(l, p):
    ws_b = p["gmlp_w_s"][l].astype(BF16)
    tm = dict(ws_b=ws_b, wst_b=jnp.swapaxes(ws_b, 1, 2), bs_b=jnp.repeat(p["gmlp_b_s"][l].T, HD, axis=1),
              lg=p["gmlp_ln_g"][l][None], lb=p["gmlp_ln_b"][l][None])
    lru = (p["conv_w"][l], p["conv_b"][l][None], p["lru_w_r"][l].astype(BF16), p["lru_b_r"][l],
           p["lru_w_i"][l].astype(BF16), p["lru_b_i"][l], p["lru_lambda"][l])
    return (p["norm1_g"][l][None], p["norm2_g"][l][None]), tm, lru


def _forward_layer(l, x, p, wb, after=(), early=None, rest=None, near_end=None, operands=None, loss=None):
    (g1, g2), tm, lru = _layer_operands(l, p) if operands is None else operands
    z6, hn1 = _mm_in(x, g1, wb["w_in"], l, after)
    ya = _gmlp_fwd(z6, tm["ws_b"], tm["bs_b"], tm["lg"], tm["lb"])
    merged, h0, h1 = _lru_fwd(z6, ya, *lru, after=() if early is None else tuple(early(ya)))
    if rest is not None:
        wb = dict(wb, **rest(merged))
    x1 = _mm_res(merged, wb["w_out"], x, l, "mm_out")
    gu, ff, hn2 = _mm_ffn_in(x1, g2, wb["w_ffn_in"], l)
    if loss is None:
        x2 = _mm_res(ff, wb["w_ffn_out"], x1, l, "mm_ffn_out", () if near_end is None else tuple(near_end(gu)))
    else:
        x2 = _mm_res_loss(ff, wb["w_ffn_out"], x1, *loss)
    return x2, dict(x=x, z6=z6, h0=h0, h1=h1, merged=merged, x1=x1, gu=gu, ff=ff, g1=g1, g2=g2, tm=tm, lru=lru,
                    hn1=hn1, hn2=hn2, wb=wb)


def _backward_layer(l, dx, s, after=(), midway=None, midway2=None, midway3=None, late=None):
    S = dx.shape[0]
    tm, wb = s["tm"], s["wb"]
    g2 = s["g2"]
    dgu = _bwd_ffn_out(dx, wb["w_ffn_out"], s["gu"], l, after)
    tmb = min(TM_BIG, S)
    dwfo = _mm_tn(s["ff"], dx, DFF_SH, tmb, f"dw_ffn_out_{l}")
    dx1, dg2 = _mm_nt_rms_bwd(
        dgu, [pl.BlockSpec((None, tmb, DFF_SH), lambda i, k: (k, i, 0))],
        wb["w_ffn_in"], [pl.BlockSpec((None, D, DFF_SH), lambda i, k: (k, 0, 0))],
        4, tmb, s["x1"], g2, dx, f"bwd_ffn_in_{l}")
    dwfi = _dw_ffn_in(s["hn2"], dgu, l)
    dmg, dwo = _bwd_out(dx1, wb["w_out"], s["merged"], l)
    mid = () if midway is None else tuple(midway([dwo, dwfi, dwfo]))
    dz6, dws, dbs, dlg, dlb = _gmlp_bwd(dmg, s["z6"], tm["ws_b"], tm["wst_b"], tm["bs_b"], tm["lg"], tm["lb"], mid)
    mid2 = () if midway2 is None else tuple(midway2(dws))
    dz6, dcw, dcb, dwr, dbr, dwi, dbi, dlam = _lru_bwd(dz6, dmg, s["z6"], s["h0"], s["h1"], *s["lru"], after=mid2)

    sub = 3

    def dz_tile(j):
        return pl.BlockSpec((None, tmb, 512), lambda i, k: ((sub * k + j) // 2, i, (sub * k + j) % 2))

    def w_tile(j):
        def w_map(i, k):
            sh, tl = _in_tile(sub * k + j)
            return (sh, 0, tl)
        return pl.BlockSpec((None, D, 512), w_map)

    small = dict(gmlp_ln_g=dlg[0], gmlp_ln_b=dlb[0], gmlp_w_s=dws, gmlp_b_s=dbs[:, :, 0], conv_w=dcw, conv_b=dcb[0],
                 lru_w_r=dwr, lru_b_r=dbr, lru_w_i=dwi, lru_b_i=dbi, lru_lambda=dlam, norm2_g=dg2[0])
    mid3 = () if midway3 is None else tuple(midway3(small))
    dwin = _dw_in(s["hn1"], dz6, l, mid3)
    tail = () if late is None else tuple(late([dwin]))
    dx0, dg1 = _mm_nt_rms_bwd(
        dz6, [dz_tile(j) for j in range(sub)], wb["w_in"], [w_tile(j) for j in range(sub)],
        N_IN_T // sub, tmb, s["x"], s["g1"], dx1, f"bwd_in_{l}", tail)
    return dx0, [dwin, dwo, dwfi, dwfo], dict(small, norm1_g=dg1[0])


def _local_step(x, tgt, p, wbs):
    saved = []
    for l in range(2):
        x, s = _forward_layer(l, x, p, wbs[l], loss=(tgt, p["final_g"][None]) if l else None)
        saved.append(s)
    dx, loss_v, dfg = x
    big, smalls = [None, None], [None, None]
    for l in (1, 0):
        dx, big[l], smalls[l] = _backward_layer(l, dx, saved[l])
    small = {k: jnp.stack([smalls[0][k], smalls[1][k]]) for k in LAYER_SMALL}
    small["final_g"] = dfg[0]
    return loss_v, dx, big, small


def _place():
    x, y, c = lax.axis_index("x"), lax.axis_index("y"), lax.axis_index("c")
    return x, y, c, 2 * x + y


def _chip_at(x, y, d):
    px = 1 - x if d & 2 else x
    py = 1 - y if d & 1 else y
    return px, py, 2 * px + py


HBM = pl.BlockSpec(memory_space=pltpu.HBM)
SEM = pl.BlockSpec(memory_space=pltpu.SEMAPHORE)
DATAFLOW = pltpu.SideEffectType.DATAFLOW_SIDE_EFFECTING


def _in_hbm(a):
    return pltpu.with_memory_space_constraint(a, pltpu.HBM)


def _cast_into(wfs, l, chip_arr, name):
    n = len(wfs)

    def body(ch_ref, *refs):
        for w_ref, o_ref in zip(refs[:n], refs[n:]):
            o_ref[...] = w_ref[...].astype(BF16)

    halves = [(wf.shape[1] // 2, wf.shape[2]) for wf in wfs]
    return pl.pallas_call(
        body, name=name, out_shape=[jax.ShapeDtypeStruct((4, 2, rh, cols), BF16) for rh, cols in halves],
        grid_spec=pltpu.PrefetchScalarGridSpec(
            num_scalar_prefetch=1, grid=(2,),
            in_specs=[pl.BlockSpec((None, None, rh, cols), lambda h, ch: (l, h, 0, 0)) for rh, cols in halves],
            out_specs=[pl.BlockSpec((None, None, rh, cols), lambda h, ch: (ch[0], h, 0, 0)) for rh, cols in halves]),
        compiler_params=_cp(("parallel",)),
    )(chip_arr, *[wf.reshape(2, 2, rh, cols) for wf, (rh, cols) in zip(wfs, halves)])


def _half_block(ref, chip, half, to, send_sem, recv_sem):
    blk = ref.at[chip, half]
    return pltpu.make_async_remote_copy(src_ref=blk, dst_ref=blk, send_sem=send_sem, recv_sem=recv_sem,
                                        device_id=to, device_id_type=MESH)


def _gather_weights(bufs, tiny):
    nt = len(bufs)
    n_ici = max(nt * 3, 1)

    def body(*refs):
        tiny_ref = refs[nt]
        o_refs, tiny_o = refs[nt + 1:2 * nt + 1], refs[2 * nt + 1]
        send, recv, fsend, frecv, tsend, trecv, lsem = refs[2 * nt + 2:]
        x, y, c, chip = _place()
        local = pltpu.make_async_copy(tiny_ref, tiny_o.at[chip], lsem)
        local.start()

        def tin(d, origin_chip, to):
            return pltpu.make_async_remote_copy(
                src_ref=tiny_ref, dst_ref=tiny_o.at[origin_chip], send_sem=tsend.at[d - 1], recv_sem=trecv.at[d - 1],
                device_id=to, device_id_type=MESH)

        sends = []
        for t in range(nt):
            for d in (1, 2, 3):
                px, py, _ = _chip_at(x, y, d)
                sends.append(_half_block(o_refs[t], chip, c, (px, py, c), send.at[3 * t + d - 1], recv.at[3 * t + d - 1]))
        for d in (1, 2, 3):
            px, py, _ = _chip_at(x, y, d)
            sends.append(tin(d, chip, (px, py, c)))
        for cp in sends:
            cp.start()
        passed = []
        for t in range(nt):
            for d in (1, 2, 3):
                k = 3 * t + d - 1
                _, _, pchip = _chip_at(x, y, d)
                _half_block(o_refs[t], pchip, c, (x, y, c), send.at[k], recv.at[k]).wait_recv()
                f = _half_block(o_refs[t], pchip, c, (x, y, 1 - c), fsend.at[k], frecv.at[k])
                f.start()
                passed.append(f)
        for t in range(nt):
            for d in (1, 2, 3):
                k = 3 * t + d - 1
                _, _, pchip = _chip_at(x, y, d)
                _half_block(o_refs[t], pchip, 1 - c, (x, y, 1 - c), fsend.at[k], frecv.at[k]).wait_recv()
        for d in (1, 2, 3):
            _, _, pchip = _chip_at(x, y, d)
            tin(d, pchip, (x, y, c)).wait_recv()
        for cp in sends + passed:
            cp.wait_send()
        local.wait()

    out_shape = [jax.ShapeDtypeStruct(b.shape, b.dtype) for b in bufs]
    out_shape.append(jax.ShapeDtypeStruct((4,) + tiny.shape, tiny.dtype))
    outs = pl.pallas_call(
        body, name="gather_weights_0", out_shape=out_shape,
        in_specs=[ANY] * (nt + 1), out_specs=[ANY] * (nt + 1),
        scratch_shapes=[pltpu.SemaphoreType.DMA((n_ici,)), pltpu.SemaphoreType.DMA((n_ici,)),
                        pltpu.SemaphoreType.DMA((n_ici,)), pltpu.SemaphoreType.DMA((n_ici,)),
                        pltpu.SemaphoreType.DMA((3,)), pltpu.SemaphoreType.DMA((3,)), pltpu.SemaphoreType.DMA],
        input_output_aliases={t: t for t in range(nt)},
        compiler_params=_cp(has_side_effects=True),
    )(*bufs, tiny)
    return outs[:nt], outs[nt]


def _gather_start(bufs, tag, after=()):
    nt, na = len(bufs), len(after)

    def body(*refs):
        b_refs = refs[:nt]
        send, recv = refs[nt + na], refs[nt + na + 1]
        token = refs[2 * nt + na + 2]
        x, y, c, chip = _place()
        for t in range(nt):
            for d in (1, 2, 3):
                px, py, _ = _chip_at(x, y, d)
                _half_block(b_refs[t], chip, c, (px, py, c), send.at[3 * t + d - 1], recv.at[3 * t + d - 1]).start()
        token[...] = jnp.zeros_like(token)

    outs = pl.pallas_call(
        body, name=f"gather_start_{tag}",
        out_shape=(pltpu.SemaphoreType.DMA((3 * nt,)), pltpu.SemaphoreType.DMA((3 * nt,)),
                   *[pltpu.HBM(b.shape, b.dtype) for b in bufs], jax.ShapeDtypeStruct((8, 128), F32)),
        in_specs=[HBM] * nt + [ANY] * na, out_specs=(SEM, SEM, *[HBM] * nt, pl.BlockSpec(memory_space=pltpu.VMEM)),
        input_output_aliases={t: 2 + t for t in range(nt)},
        compiler_params=pltpu.CompilerParams(has_side_effects=DATAFLOW),
    )(*[_in_hbm(b) for b in bufs], *after)
    return outs[0], outs[1], list(outs[2:2 + nt]), outs[2 + nt]


def _gather_wait(send, recv, bufs, after, tag):
    nt = len(bufs)

    def body(*refs):
        b_refs = refs[:nt]
        send_ref, recv_ref = refs[nt], refs[nt + 1]
        x, y, c, chip = _place()
        for t in range(nt):
            for d in (1, 2, 3):
                k = 3 * t + d - 1
                px, py, pchip = _chip_at(x, y, d)
                _half_block(b_refs[t], chip, c, (px, py, c), send_ref.at[k], recv_ref.at[k]).wait_send()
                _half_block(b_refs[t], pchip, c, (px, py, c), send_ref.at[k], recv_ref.at[k]).wait_recv()

    after = tuple(after) if isinstance(after, (tuple, list)) else (after,)
    outs = pl.pallas_call(
        body, name=f"gather_wait_{tag}", out_shape=[pltpu.HBM(b.shape, b.dtype) for b in bufs],
        in_specs=[HBM] * nt + [SEM, SEM] + [ANY] * len(after), out_specs=[HBM] * nt,
        input_output_aliases={t: t for t in range(nt)},
        compiler_params=pltpu.CompilerParams(has_side_effects=DATAFLOW),
    )(*bufs, send, recv, *after)
    return list(outs)


def _gather_pass_on(bufs, tag):
    nt = len(bufs)

    def body(*refs):
        o_refs = refs[nt:2 * nt]
        fsend, frecv = refs[2 * nt:]
        x, y, c, _ = _place()
        cps = []
        for t in range(nt):
            for d in (1, 2, 3):
                k = 3 * t + d - 1
                _, _, pchip = _chip_at(x, y, d)
                cps.append(_half_block(o_refs[t], pchip, c, (x, y, 1 - c), fsend.at[k], frecv.at[k]))
        for cp in cps:
            cp.start()
        for t in range(nt):
            for d in (1, 2, 3):
                k = 3 * t + d - 1
                _, _, pchip = _chip_at(x, y, d)
                _half_block(o_refs[t], pchip, 1 - c, (x, y, 1 - c), fsend.at[k], frecv.at[k]).wait_recv()
        for cp in cps:
            cp.wait_send()

    return pl.pallas_call(
        body, name=f"gather_pass_on_{tag}", out_shape=[jax.ShapeDtypeStruct(b.shape, b.dtype) for b in bufs],
        in_specs=[ANY] * nt, out_specs=[ANY] * nt,
        scratch_shapes=[pltpu.SemaphoreType.DMA((3 * nt,)), pltpu.SemaphoreType.DMA((3 * nt,))],
        input_output_aliases={t: t for t in range(nt)},
        compiler_params=_cp(has_side_effects=True),
    )(*bufs)


def _chip_copy(c_ref, land_ref, x, y, c, d, send_sem, recv_sem):
    px, py, pchip = _chip_at(x, y, d)
    return pltpu.make_async_remote_copy(src_ref=c_ref.at[pchip], dst_ref=land_ref.at[d - 1], send_sem=send_sem, recv_sem=recv_sem,
                                        device_id=(px, py, c), device_id_type=MESH)


def _exchange_start(srcs, lands, copies, nsem, name):
    ns, n = len(srcs), len(srcs) + len(lands)

    def body(*refs):
        for cp in copies(refs[:ns], refs[ns:n], refs[n], refs[n + 1]):
            cp.start()
        token = refs[2 * n + 2]
        token[...] = jnp.zeros_like(token)

    outs = pl.pallas_call(
        body, name=name,
        out_shape=(pltpu.SemaphoreType.DMA((nsem,)), pltpu.SemaphoreType.DMA((nsem,)),
                   *[pltpu.HBM(a.shape, a.dtype) for a in list(srcs) + list(lands)], jax.ShapeDtypeStruct((8, 128), F32)),
        in_specs=[HBM] * n, out_specs=(SEM, SEM, *[HBM] * n, pl.BlockSpec(memory_space=pltpu.VMEM)),
        input_output_aliases={i: 2 + i for i in range(n)},
        compiler_params=pltpu.CompilerParams(has_side_effects=DATAFLOW),
    )(*[_in_hbm(a) for a in list(srcs) + list(lands)])
    return outs[0], outs[1], list(outs[2:2 + ns]), list(outs[2 + ns:2 + n]), outs[2 + n]


def _exchange_wait(send, recv, srcs, lands, after, copies, name):
    ns, n = len(srcs), len(srcs) + len(lands)

    def body(*refs):
        for cp in copies(refs[:ns], refs[ns:n], refs[n], refs[n + 1]):
            cp.wait_send()
            cp.wait_recv()

    outs = pl.pallas_call(
        body, name=name, out_shape=[pltpu.HBM(a.shape, a.dtype) for a in list(srcs) + list(lands)],
        in_specs=[HBM] * n + [SEM, SEM, ANY], out_specs=[HBM] * n,
        input_output_aliases={i: i for i in range(n)},
        compiler_params=pltpu.CompilerParams(has_side_effects=DATAFLOW),
    )(*srcs, *lands, send, recv, after)
    return list(outs[:ns]), list(outs[ns:])


def _pass_on_copies(b_refs, land_refs, send, recv):
    del land_refs
    x, y, c, _ = _place()
    return [_half_block(b_refs[t], _chip_at(x, y, d)[2], c, (x, y, 1 - c), send.at[3 * t + d - 1], recv.at[3 * t + d - 1])
            for t in range(len(b_refs)) for d in (1, 2, 3)]


def _chips_copies(c_refs, land_refs, send, recv):
    x, y, c, _ = _place()
    return [_chip_copy(c_refs[t], land_refs[t], x, y, c, d, send.at[3 * t + d - 1], recv.at[3 * t + d - 1])
            for t in range(len(c_refs)) for d in (1, 2, 3)]


def _sibling_copies(g_refs, land_refs, send, recv):
    x, y, c, _ = _place()
    return [pltpu.make_async_remote_copy(
        src_ref=g_refs[t].at[k, 1 - c], dst_ref=land_refs[t].at[k], send_sem=send.at[4 * t + k], recv_sem=recv.at[4 * t + k],
        device_id=(x, y, 1 - c), device_id_type=MESH) for t in range(len(g_refs)) for k in range(4)]


def _join_copies(f_refs, land_refs, send, recv):
    del land_refs
    x, y, c, _ = _place()
    return [pltpu.make_async_remote_copy(
        src_ref=f_refs[t].at[c], dst_ref=f_refs[t].at[c], send_sem=send.at[t], recv_sem=recv.at[t],
        device_id=(x, y, 1 - c), device_id_type=MESH) for t in range(len(f_refs))]


def _add_half(gs, rs, c_arr, name):
    n = len(gs)

    def body(c_ref, *refs):
        for g_ref, r_ref, o_ref in zip(refs[:n], refs[n:2 * n], refs[2 * n:]):
            o_ref[...] = (g_ref[...] + r_ref[...]).astype(BF16)

    def own(g):
        return pl.BlockSpec((None, None) + g.shape[2:], lambda k, cr: (k, cr[0], 0, 0))

    def blk(g):
        return pl.BlockSpec((None,) + g.shape[2:], lambda k, cr: (k, 0, 0))

    return pl.pallas_call(
        body, name=name, out_shape=[jax.ShapeDtypeStruct((4,) + g.shape[2:], BF16) for g in gs],
        grid_spec=pltpu.PrefetchScalarGridSpec(
            num_scalar_prefetch=1, grid=(4,),
            in_specs=[own(g) for g in gs] + [blk(g) for g in gs], out_specs=[blk(g) for g in gs]),
        compiler_params=_cp(("parallel",)),
    )(c_arr, *gs, *rs)


def _sum_chips(css, r3s, place_arr, name):
    n = len(css)

    def body(pl_ref, *refs):
        up = lambda ref: ref[...].astype(F32)
        for t in range(n):
            a_ref, (r0_ref, r1_ref, r2_ref), o_ref = refs[t], refs[n + 3 * t:n + 3 * t + 3], refs[4 * n + t]
            o_ref[...] = ((up(a_ref) + up(r0_ref)) + up(r1_ref)) + up(r2_ref)

    def blk(cs, first):
        _, rh, cols = cs.shape
        return pl.BlockSpec((None, rh // 2, cols), lambda i, pa: (first(pa), i, 0))

    in_specs = [blk(cs, lambda pa: pa[0]) for cs in css]
    for cs in css:
        in_specs += [blk(cs, lambda pa, d=d: d) for d in range(3)]
    return pl.pallas_call(
        body, name=name, out_shape=[jax.ShapeDtypeStruct((2,) + cs.shape[1:], F32) for cs in css],
        grid_spec=pltpu.PrefetchScalarGridSpec(
            num_scalar_prefetch=1, grid=(2,), in_specs=in_specs, out_specs=[blk(cs, lambda pa: pa[1]) for cs in css]),
        compiler_params=_cp(("parallel",)),
    )(place_arr, *css, *[r3 for r3 in r3s for _ in range(3)])


def _allreduce_small(pack):
    rows = pack.shape[0]
    hr = rows // 2

    def body(p_ref, o_ref, sib, slots, s1, r1, s2, r2, s3, r3):
        x, y, c, chip = _place()
        sibling = (x, y, 1 - c)
        ex = pltpu.make_async_remote_copy(src_ref=p_ref, dst_ref=sib, send_sem=s1, recv_sem=r1,
                                          device_id=sibling, device_id_type=MESH)
        ex.start()
        ex.wait()
        half = pl.ds(pl.multiple_of(c * hr, 16), hr)
        slots[0] = p_ref[half, :] + sib[half, :]
        cps = []
        for d in (1, 2, 3):
            px, py, _ = _chip_at(x, y, d)
            cps.append(pltpu.make_async_remote_copy(
                src_ref=slots.at[0], dst_ref=slots.at[d], send_sem=s2.at[d - 1], recv_sem=r2.at[d - 1],
                device_id=(px, py, c), device_id_type=MESH))
        for cp in cps:
            cp.start()
        for cp in cps:
            cp.wait()
        tot = slots[chip]
        for k in (1, 2, 3):
            tot = tot + slots[jnp.bitwise_xor(chip, k)]
        o_ref[half, :] = tot
        back = pltpu.make_async_remote_copy(src_ref=o_ref.at[half, :], dst_ref=o_ref.at[half, :], send_sem=s3, recv_sem=r3,
                                            device_id=sibling, device_id_type=MESH)
        back.start()
        back.wait()

    vm = pl.BlockSpec(memory_space=pltpu.VMEM)
    return pl.pallas_call(
        body, name="allreduce_small", out_shape=jax.ShapeDtypeStruct((rows, 128), F32),
        in_specs=[vm], out_specs=vm,
        scratch_shapes=[pltpu.VMEM((rows, 128), F32), pltpu.VMEM((4, hr, 128), F32),
                        pltpu.SemaphoreType.DMA, pltpu.SemaphoreType.DMA, pltpu.SemaphoreType.DMA((3,)), pltpu.SemaphoreType.DMA((3,)),
                        pltpu.SemaphoreType.DMA, pltpu.SemaphoreType.DMA],
        compiler_params=_cp(has_side_effects=True),
    )(pack)


def _small_chip_sum(pack, after=()):
    rows = pack.shape[0]
    hr = rows // 2

    def body(p_ref, *rest):
        o_ref, sib, s1, r1 = rest[-4:]
        x, y, c, _ = _place()
        ex = pltpu.make_async_remote_copy(src_ref=p_ref, dst_ref=sib, send_sem=s1, recv_sem=r1,
                                          device_id=(x, y, 1 - c), device_id_type=MESH)
        ex.start()
        ex.wait()
        half = pl.ds(pl.multiple_of(c * hr, 16), hr)
        o_ref[...] = (p_ref[half, :] + sib[half, :]).astype(BF16)

    vm = pl.BlockSpec(memory_space=pltpu.VMEM)
    return pl.pallas_call(
        body, name="small_chip_sum", out_shape=jax.ShapeDtypeStruct((hr, 128), BF16),
        in_specs=[vm] + [ANY] * len(after), out_specs=vm,
        scratch_shapes=[pltpu.VMEM((rows, 128), F32), pltpu.SemaphoreType.DMA, pltpu.SemaphoreType.DMA],
        compiler_params=_cp(has_side_effects=True),
    )(pack, *after)


def _small_copies(c_refs, land_refs, send, recv):
    x, y, c, _ = _place()
    cps = []
    for d in (1, 2, 3):
        px, py, _ = _chip_at(x, y, d)
        cps.append(pltpu.make_async_remote_copy(src_ref=c_refs[0], dst_ref=land_refs[0].at[d - 1], send_sem=send.at[d - 1],
                                                recv_sem=recv.at[d - 1], device_id=(px, py, c), device_id_type=MESH))
    return cps


def _small_total(csum, land):
    hr = csum.shape[0]

    def body(c_ref, l_ref, o_ref, slots, s3, r3):
        x, y, c, chip = _place()
        slots[0] = c_ref[...]
        for d in (1, 2, 3):
            slots[d] = l_ref[d - 1]
        tot = slots[chip].astype(F32)
        for k in (1, 2, 3):
            tot = tot + slots[jnp.bitwise_xor(chip, k)].astype(F32)
        half = pl.ds(pl.multiple_of(c * hr, 16), hr)
        o_ref[half, :] = tot
        back = pltpu.make_async_remote_copy(src_ref=o_ref.at[half, :], dst_ref=o_ref.at[half, :], send_sem=s3, recv_sem=r3,
                                            device_id=(x, y, 1 - c), device_id_type=MESH)
        back.start()
        back.wait()

    vm = pl.BlockSpec(memory_space=pltpu.VMEM)
    return pl.pallas_call(
        body, name="small_total", out_shape=jax.ShapeDtypeStruct((2 * hr, 128), F32), in_specs=[vm, vm], out_specs=vm,
        scratch_shapes=[pltpu.VMEM((4, hr, 128), BF16), pltpu.SemaphoreType.DMA, pltpu.SemaphoreType.DMA],
        compiler_params=_cp(has_side_effects=True),
    )(csum, land)


def _adam_math(gv, wv, mv, vv):
    m2 = ADAM_B1 * mv + (1.0 - ADAM_B1) * gv
    v2 = ADAM_B2 * vv + (1.0 - ADAM_B2) * (gv * gv)
    m_hat = m2 / (1.0 - ADAM_B1 ** ADAM_STEP)
    v_hat = v2 / (1.0 - ADAM_B2 ** ADAM_STEP)
    return -ADAM_LR * (m_hat / (jnp.sqrt(v_hat) + ADAM_EPS) + ADAM_WD * wv), m2, v2


def _adam(g, w, m, v, name):
    rows, cols = g.shape
    rb = rows // 4

    def body(g_ref, w_ref, m_ref, v_ref, d_ref, m2_ref, v2_ref):
        d_ref[...], m2_ref[...], v2_ref[...] = _adam_math(g_ref[...], w_ref[...], m_ref[...], v_ref[...])

    blk = pl.BlockSpec((rb, cols), lambda i: (i, 0))
    shp = jax.ShapeDtypeStruct((rows, cols), F32)
    return pl.pallas_call(
        body, name=name, grid=(4,), in_specs=[blk] * 4, out_specs=[blk] * 3, out_shape=[shp] * 3,
        compiler_params=_cp(("parallel",)),
    )(g, w, m, v)


def _adam_layer(gs, ws, ms, vs, l, prevs, name):
    n = len(gs)
    prev = [a for p4 in prevs if p4 is not None for a in p4]

    def body(*refs):
        outs = refs[len(refs) - 4 * n:]
        for t in range(n):
            g_ref, w_ref, m_ref, v_ref = refs[4 * t:4 * t + 4]
            go_ref, d_ref, m2_ref, v2_ref = outs[4 * t:4 * t + 4]
            gv = g_ref[...]
            go_ref[...] = gv
            d_ref[...], m2_ref[...], v2_ref[...] = _adam_math(gv, w_ref[...], m_ref[...], v_ref[...])

    in_specs, out_specs, out_shape, operands, aliases = [], [], [], [], {}
    for t, g in enumerate(gs):
        rows, cols = g.shape
        lay = pl.BlockSpec((None, rows // 4, cols), lambda i: (l, i, 0))
        in_specs += [pl.BlockSpec((rows // 4, cols), lambda i: (i, 0)), lay, lay, lay]
        operands += [g, ws[t], ms[t], vs[t]]
        out_specs += [lay] * 4
        out_shape += [jax.ShapeDtypeStruct((2, rows, cols), F32)] * 4
    k = 4 * n
    for t, p4 in enumerate(prevs):
        if p4 is not None:
            for j in range(4):
                aliases[k] = 4 * t + j
                k += 1
    outs = pl.pallas_call(
        body, name=name, grid=(4,), in_specs=in_specs + [ANY] * len(prev), out_specs=out_specs, out_shape=out_shape,
        input_output_aliases=aliases, compiler_params=_cp(("parallel",)),
    )(*operands, *prev)
    return [list(outs[4 * t:4 * t + 4]) for t in range(n)]


def _rows128(a):
    return a.reshape(-1, 128)


def _pack(arrs, mult):
    parts = [_rows128(a) for a in arrs]
    rows = sum(q.shape[0] for q in parts)
    pad = -rows % mult
    if pad:
        parts.append(jnp.zeros((pad, 128), F32))
    return jnp.concatenate(parts, axis=0)


def _unpack(pack, shapes):
    out, o = [], 0
    for s in shapes:
        n = 1
        for e in s:
            n *= e
        out.append(pack[o:o + n // 128].reshape(s))
        o += n // 128
    return out


WEIGHTS = ['norm1_g', 'w_in', 'gmlp_ln_g', 'gmlp_ln_b', 'gmlp_w_s', 'gmlp_b_s', 'conv_w', 'conv_b', 'lru_w_r', 'lru_b_r', 'lru_w_i',
           'lru_b_i', 'lru_lambda', 'w_out', 'norm2_g', 'w_ffn_in', 'w_ffn_out', 'final_g']
BIG = ['w_in', 'w_out', 'w_ffn_in', 'w_ffn_out']
SMALL = [n for n in WEIGHTS if n not in BIG]
CHIP_SHARDED_SMALL = ['conv_w', 'lru_b_r', 'lru_b_i', 'lru_lambda']


def kernel(x, norm1_g, w_in, gmlp_ln_g, gmlp_ln_b, gmlp_w_s, gmlp_b_s, conv_w, conv_b, lru_w_r, lru_b_r, lru_w_i, lru_b_i, lru_lambda, w_out, norm2_g, w_ffn_in, w_ffn_out, final_g, loss_target, m_norm1_g, m_w_in, m_gmlp_ln_g, m_gmlp_ln_b, m_gmlp_w_s, m_gmlp_b_s, m_conv_w, m_conv_b, m_lru_w_r, m_lru_b_r, m_lru_w_i, m_lru_b_i, m_lru_lambda, m_w_out, m_norm2_g, m_w_ffn_in, m_w_ffn_out, m_final_g, v_norm1_g, v_w_in, v_gmlp_ln_g, v_gmlp_ln_b, v_gmlp_w_s, v_gmlp_b_s, v_conv_w, v_conv_b, v_lru_w_r, v_lru_b_r, v_lru_w_i, v_lru_b_i, v_lru_lambda, v_w_out, v_norm2_g, v_w_ffn_in, v_w_ffn_out, v_final_g):
    a = dict(locals())
    w = {n: a[n] for n in WEIGHTS}
    mom = {n: a["m_" + n] for n in WEIGHTS}
    var = {n: a["v_" + n] for n in WEIGHTS}
    _, _, c, chip = _place()
    c_arr, chip_arr = jnp.reshape(c, (1,)).astype(jnp.int32), jnp.reshape(chip, (1,)).astype(jnp.int32)
    place_arr = jnp.stack([chip, c]).astype(jnp.int32)

    first, rest = BIG[:1], BIG[1:]

    def as_weights(names, full):
        wb = {n: f.reshape(4, 2 * f.shape[2], f.shape[3]) for n, f in zip(names, full)}
        if "w_out" in wb:
            wb["w_out"] = wb["w_out"].reshape(D, D)
            wb["w_ffn_out"] = wb["w_ffn_out"].reshape(DFF, D)
        return wb

    def cast(names, l, tag):
        return _cast_into([w[n] for n in names], l, chip_arr, f"cast_{tag}")

    def landed(fly, names, after, tag):
        return as_weights(names, _gather_pass_on(_gather_wait(fly[0], fly[1], fly[2], after, tag), tag))

    tiny = _pack([w[n] for n in CHIP_SHARDED_SMALL], 8)
    _, tiny_full = _gather_weights([], tiny)
    fly_in = _gather_start(cast(first, 0, "in"), "in", after=(tiny_full,))
    fly0 = _gather_start(cast(rest, 0, "0"), "0", after=(fly_in[3],))
    fly1 = _gather_start(cast(BIG, 1, "1"), "1", after=(fly0[3],))
    p = {n: w[n] for n in SMALL}
    parts = [_unpack(tiny_full[k], [w[n].shape for n in CHIP_SHARDED_SMALL]) for k in range(4)]
    for i, n in enumerate(CHIP_SHARDED_SMALL):
        p[n] = jnp.concatenate([parts[k][i] for k in range(4)], axis=-1)

    operands = [_layer_operands(l, p) for l in range(2)]
    state_packs = [_pack([src[n] for n in SMALL], 32) for src in (w, mom, var)]
    ahead = tuple(jax.tree.leaves(operands)) + tuple(state_packs)

    passing = {}

    def pass_on_1(gu):
        bufs = _gather_wait(fly1[0], fly1[1], fly1[2], gu, "1")
        passing[1] = _exchange_start(bufs, [], _pass_on_copies, 3 * len(bufs), "gather_pass_on_start_1")
        return (passing[1][-1],)

    def pass_on_0(ya):
        bufs = _gather_wait(fly0[0], fly0[1], fly0[2], ya, "0")
        passing[0] = _exchange_start(bufs, [], _pass_on_copies, 3 * len(bufs), "gather_pass_on_start_0")
        return (passing[0][-1],)

    def rest0(merged):
        send, recv, bufs, _, _ = passing[0]
        return as_weights(rest, _exchange_wait(send, recv, bufs, [], merged, _pass_on_copies, "gather_pass_on_wait_0")[0])

    xa, saved0 = _forward_layer(0, x[0], p, landed(fly_in, first, (fly1[3],) + ahead, "in"), after=(fly0[3], fly1[3]),
                                early=pass_on_0, rest=rest0, near_end=pass_on_1, operands=operands[0])
    send, recv, bufs1, _, _ = passing[1]
    xb, saved1 = _forward_layer(
        1, xa, p, as_weights(BIG, _exchange_wait(send, recv, bufs1, [], xa, _pass_on_copies, "gather_pass_on_wait_1")[0]),
        operands=operands[1], loss=(loss_target[0], p["final_g"][None]))
    dxb, loss_v, dfg = xb
    loss = lax.psum(loss_v[0, 0], ("x", "y", "c"))

    out, flying = {}, {}

    def halves(grads):
        return [g.reshape(4, 2, -1, g.shape[-1]) for g in grads]

    def sibling_start(grads, names, l, tag):
        gs = halves(grads)
        lands = [lax.empty((4,) + g.shape[2:], g.dtype) for g in gs]
        flying["s" + tag] = (names, l) + tuple(
            _exchange_start(gs, lands, _sibling_copies, 4 * len(gs), f"grads_to_sibling_start_{tag}"))
        return (flying["s" + tag][-1],)

    def chips_start(gs, from_sib, names, l, tag):
        cs = _add_half(gs, from_sib, c_arr, f"add_half_{tag}")
        lands = [lax.empty((3,) + a.shape[1:], a.dtype) for a in cs]
        flying[tag] = (names, l) + tuple(_exchange_start(cs, lands, _chips_copies, 3 * len(cs), f"grads_to_chips_start_{tag}"))
        return (flying[tag][-1],)

    def sibling_finish(tag, after):
        names, l, send, recv, gs, lands, _ = flying["s" + tag]
        gs, from_sib = _exchange_wait(send, recv, gs, lands, after, _sibling_copies, f"grads_to_sibling_wait_{tag}")
        return chips_start(gs, from_sib, names, l, tag)

    def reduce_sums(tags, after):
        groups, ts = [], []
        for tag in tags:
            names, l, send, recv, cs, lands, _ = flying[tag]
            cs, lands = _exchange_wait(send, recv, cs, lands, after, _chips_copies, f"grads_to_chips_wait_{tag}")
            ts += _sum_chips(cs, lands, place_arr, f"sum_chips_{tag}")
            groups.append((tag, names))
        flying["j" + tags[0]] = (groups, l) + tuple(_exchange_start(ts, [], _join_copies, len(ts), f"grads_join_start_{tags[0]}"))
        return (flying["j" + tags[0]][-1],)

    def reduce_adam(tag0, after):
        groups, l, send, recv, ts, _, _ = flying["j" + tag0]
        joined = _exchange_wait(send, recv, ts, [], after, _join_copies, f"grads_join_wait_{tag0}")[0]
        for tag, names in groups:
            gs, joined = [j.reshape(w[n].shape[1:]) for n, j in zip(names, joined)], joined[len(names):]
            res = _adam_layer(gs, [w[n] for n in names], [mom[n] for n in names], [var[n] for n in names], l,
                              [out.get(n) for n in names], f"adam_{tag}")
            out.update(zip(names, res))

    def late1(grads):
        return sibling_finish("1a", grads[0]) + sibling_start(grads, first, 1, "1b")

    def midway0(grads):
        return reduce_sums(("1a", "1b"), grads[0]) + sibling_start(grads, rest, 0, "0a")

    def stacked_small(small0):
        small = {k: jnp.stack([small0[k], small1[k]]) for k in LAYER_SMALL}
        return dict(small, final_g=dfg[0])

    def midway3_0(small0):
        small = stacked_small(dict(small0, norm1_g=jnp.zeros((D,), F32)))
        csum = _small_chip_sum(_pack([small[n] for n in SMALL], 32))
        flying["small"] = _exchange_start([csum], [lax.empty((3,) + csum.shape, BF16)], _small_copies, 3, "small_to_chips_start")
        return (flying["small"][-1],)

    def late0(grads):
        tok = sibling_start(grads, first, 0, "0b")
        reduce_adam("1a", tok[0])
        return sibling_finish("0b", out[first[0]][0])

    dxa, big1, small1 = _backward_layer(1, dxb, saved1, midway=lambda grads: sibling_start(grads, rest, 1, "1a"), late=late1)
    dx, big0, small0 = _backward_layer(0, dxa, saved0, after=sibling_finish("1b", dxa), midway=midway0,
                                       midway2=lambda dws: sibling_finish("0a", dws), midway3=midway3_0, late=late0)
    join_tok = reduce_sums(("0a", "0b"), dx)
    small = stacked_small(small0)

    full_shapes = [small[n].shape for n in SMALL]
    send, recv, csum, land, _ = flying["small"]
    csum, land = _exchange_wait(send, recv, csum, land, join_tok[0], _small_copies, "small_to_chips_wait")
    red = _unpack(_small_total(csum[0], land[0]), full_shapes)
    norm1_0 = _allreduce_small(_pack([small0["norm1_g"]], 32))[:D // 128].reshape(D)
    reduce_adam("0a", norm1_0)
    red[SMALL.index("norm1_g")] = red[SMALL.index("norm1_g")].at[0].set(norm1_0)
    g_small = []
    for n, g in zip(SMALL, red):
        if n in CHIP_SHARDED_SMALL:
            g = lax.dynamic_slice_in_dim(g, chip * w[n].shape[-1], w[n].shape[-1], axis=g.ndim - 1)
        g_small.append(g)
    shapes = [w[n].shape for n in SMALL]
    upd = [_unpack(u, shapes) for u in _adam(_pack(g_small, 32), *state_packs, "adam_small")]
    for i, n in enumerate(SMALL):
        out[n] = [g_small[i], upd[0][i], upd[1][i], upd[2][i]]

    return (loss, dx[None]) + tuple(out[n][i] for i in range(4) for n in WEIGHTS)
```

```python
import functools

import jax
import jax.numpy as jnp
from jax import lax
from jax.experimental import pallas as pl
from jax.experimental.pallas import tpu as pltpu

F32 = jnp.float32
BF16 = jnp.bfloat16
MESH = pl.DeviceIdType.MESH

D = 1024
NH = 8
HD = 128
CHUNK = 128
GMLP_ROWS = 512
N_IN_T = 12
DFF = 2816
DFF_SH = 1408
EPS = 1e-6
LRU_C = 8.0
ADAM_LR, ADAM_B1, ADAM_B2, ADAM_EPS, ADAM_WD, ADAM_STEP = 0.001, 0.9, 0.999, 1e-08, 0.01, 10

TM = 512
TM_BIG = 1024
RT = 128
PADR = 8
VMEM_LIMIT = 56 * 1024 * 1024


def _cp(sem=None, **kw):
    if sem is not None:
        kw["dimension_semantics"] = sem
    return pltpu.CompilerParams(vmem_limit_bytes=VMEM_LIMIT, **kw)


_GC = 0.7978845608028654


def _sigmoid(x):
    return 0.5 * jnp.tanh(0.5 * x) + 0.5


_GK = 0.044715


def _gelu(x):
    t = jnp.tanh(x * (_GC + (_GC * _GK) * (x * x)))
    return x * (0.5 + 0.5 * t)


def _gelu_and_grad(x):
    x2 = x * x
    t = jnp.tanh(x * (_GC + (_GC * _GK) * x2))
    h = 0.5 + 0.5 * t
    return x * h, h + x * (1.0 - t * t) * (0.5 * _GC + (1.5 * _GC * _GK) * x2)


def _softplus_neg(lam):
    y = jnp.exp(-jnp.abs(lam))
    u = 1.0 + y
    l1p = jnp.where(u == 1.0, y, jnp.log(u) * y / (u - 1.0))
    return jnp.maximum(-lam, 0.0) + l1p


def _dot(a, b):
    return jnp.dot(a, b, preferred_element_type=F32)


def _dot_nt(a, b):
    return lax.dot_general(a, b, (((1,), (1,)), ((), ())), preferred_element_type=F32)


def _dot_tn(a, b):
    return lax.dot_general(a, b, (((0,), (0,)), ((), ())), preferred_element_type=F32)


def _rms_hat(x):
    r = lax.rsqrt(jnp.mean(x * x, axis=-1, keepdims=True) + EPS)
    return x * r, r


def _rms_bwd(dh, x, g):
    xh, r = _rms_hat(x)
    dxh = dh * g
    dx = r * (dxh - xh * jnp.mean(dxh * xh, axis=-1, keepdims=True))
    return dx, jnp.sum(dh * xh, axis=0, keepdims=True)


def _norm_into(x_ref, g_ref, h_ref):
    xh, _ = _rms_hat(x_ref[...])
    h_ref[...] = (xh * g_ref[...]).astype(BF16)


def _in_tile(j):
    m, hf = j // 2, j % 2
    orig = jnp.where(m < 2, m, jnp.where(m == 2, 4, jnp.where(m < 5, m - 1, 5)))
    t = orig * 2 + hf
    return t // 3, t % 3


ANY = pl.BlockSpec(memory_space=pl.ANY)


def _mm_in(x, g, w_in, l, after=()):
    S = x.shape[0]
    tm = min(2 * TM_BIG, S)

    def body(x_ref, g_ref, w0_ref, w1_ref, *rest):
        o_ref, h_ref = rest[-2:]

        @pl.when(pl.program_id(1) == 0)
        def _():
            _norm_into(x_ref, g_ref, h_ref)
        rp = min(TM, tm)
        for r0 in range(0, tm, rp):
            hv = h_ref[r0:r0 + rp, :]
            o_ref[r0:r0 + rp, 0:512] = _dot(hv, w0_ref[...]).astype(BF16)
            o_ref[r0:r0 + rp, 512:1024] = _dot(hv, w1_ref[...]).astype(BF16)

    def w_tile(hf):
        def w_map(i, m):
            sh, tl = _in_tile(2 * m + hf)
            return (sh, 0, tl)
        return pl.BlockSpec((None, D, 512), w_map)

    return pl.pallas_call(
        body, name=f"mm_in_{l}", grid=(S // tm, 6),
        in_specs=[pl.BlockSpec((tm, D), lambda i, m: (i, 0)), pl.BlockSpec((1, D), lambda i, m: (0, 0)),
                  w_tile(0), w_tile(1)] + [ANY] * len(after),
        out_specs=[pl.BlockSpec((None, tm, D), lambda i, m: (m, i, 0)), pl.BlockSpec((tm, D), lambda i, m: (i, 0))],
        out_shape=[jax.ShapeDtypeStruct((6, S, D), BF16), jax.ShapeDtypeStruct((S, D), BF16)],
        compiler_params=_cp(("parallel", "arbitrary")),
    )(x, g, w_in, w_in, *after)


def _mm_res(a, w, res, l, name, after=()):
    S, K = a.shape

    tm = TM

    def body(a_ref, w_ref, r_ref, *rest):
        rest[-1][...] = r_ref[...] + _dot(a_ref[...], w_ref[...])

    return pl.pallas_call(
        body, name=f"{name}_{l}", grid=(S // tm,),
        in_specs=[pl.BlockSpec((tm, K), lambda i: (i, 0)), pl.BlockSpec((K, D), lambda i: (0, 0)),
                  pl.BlockSpec((tm, D), lambda i: (i, 0))] + [ANY] * len(after),
        out_specs=pl.BlockSpec((tm, D), lambda i: (i, 0)),
        out_shape=jax.ShapeDtypeStruct((S, D), F32),
        compiler_params=_cp(("parallel",)),
    )(a, w, res, *after)


def _mm_ffn_in(x, g, w_fi, l):
    S = x.shape[0]

    tm = min(TM_BIG, S)

    def body(x_ref, g_ref, w_ref, gu_ref, ff_ref, h_ref):
        @pl.when(pl.program_id(1) == 0)
        def _():
            _norm_into(x_ref, g_ref, h_ref)
        for r0 in range(0, tm, TM):
            rows = slice(r0, r0 + TM)
            hv = h_ref[rows, :]
            ga = _dot(hv, w_ref[0])
            gb = _dot(hv, w_ref[1])
            sg = _sigmoid(ga)
            silu = ga * sg
            gu_ref[0, rows, :] = (gb * (sg + silu * (1.0 - sg))).astype(BF16)
            gu_ref[1, rows, :] = silu.astype(BF16)
            ff_ref[rows, :] = (silu * gb).astype(BF16)

    gu, ff, h = pl.pallas_call(
        body, name=f"mm_ffn_in_{l}", grid=(S // tm, 2),
        in_specs=[pl.BlockSpec((tm, D), lambda i, s: (i, 0)), pl.BlockSpec((1, D), lambda i, s: (0, 0)),
                  pl.BlockSpec((2, None, D, DFF_SH), lambda i, s: (0, s, 0, 0))],
        out_specs=[pl.BlockSpec((2, None, tm, DFF_SH), lambda i, s: (0, s, i, 0)),
                   pl.BlockSpec((tm, DFF_SH), lambda i, s: (i, s)),
                   pl.BlockSpec((tm, D), lambda i, s: (i, 0))],
        out_shape=[jax.ShapeDtypeStruct((2, 2, S, DFF_SH), BF16), jax.ShapeDtypeStruct((S, DFF), BF16),
                   jax.ShapeDtypeStruct((S, D), BF16)],
        compiler_params=_cp(("parallel", "arbitrary")),
    )(x, g, w_fi.reshape(2, 2, D, DFF_SH))
    return gu.reshape(4, S, DFF_SH), ff, h


def _gmlp_fwd(z6, ws_b, bs_b, lg, lb):
    S = z6.shape[1]

    ts = min(GMLP_ROWS, S)

    def body(z_ref, ws_ref, bs_ref, lg_ref, lb_ref, o_ref, mix):
        for r0 in range(0, ts, CHUNK):
            rows = slice(r0, r0 + CHUNK)
            gv = _gelu(z_ref[1, rows, :].astype(F32))
            xc = gv - jnp.mean(gv, axis=-1, keepdims=True)
            rs = lax.rsqrt(jnp.mean(xc * xc, axis=-1, keepdims=True) + EPS)
            vb = (xc * rs * lg_ref[...] + lb_ref[...]).astype(BF16)
            for gi in range(NH):
                cs = slice(gi * HD, (gi + 1) * HD)
                mix[rows, cs] = _dot(ws_ref[gi], vb[:, cs])
            o_ref[rows, :] = (_sigmoid(z_ref[2, rows, :].astype(F32)) * _gelu(z_ref[0, rows, :].astype(F32))
                              * (mix[rows, :] + bs_ref[...])).astype(BF16)

    return pl.pallas_call(
        body, name="gmlp_fwd", grid=(S // ts,),
        in_specs=[pl.BlockSpec((3, ts, D), lambda i: (0, i, 0)), pl.BlockSpec((NH, CHUNK, CHUNK), lambda i: (0, 0, 0)),
                  pl.BlockSpec((CHUNK, D), lambda i: (0, 0)), pl.BlockSpec((1, D), lambda i: (0, 0)),
                  pl.BlockSpec((1, D), lambda i: (0, 0))],
        out_specs=pl.BlockSpec((ts, D), lambda i: (i, 0)),
        out_shape=jax.ShapeDtypeStruct((S, D), BF16),
        scratch_shapes=[pltpu.VMEM((ts, D), F32)],
        compiler_params=_cp(("parallel",)),
    )(z6, ws_b, bs_b, lg, lb)


def _row_iota():
    return lax.broadcasted_iota(jnp.int32, (RT, HD), 0)


SUB = 8
UNROLL = 8
GRAD_ROWS = 1024


def _scan_up(a, b, carry):
    row = lax.broadcasted_iota(jnp.int32, (SUB, HD), 0)
    masks = [(d, row >= d) for d in (1, 2, 4)]
    c = jnp.broadcast_to(carry, (SUB, HD))
    hs = []
    for j in range(RT // SUB):
        aj, bj = a[SUB * j:SUB * (j + 1)], b[SUB * j:SUB * (j + 1)]
        for d, m in masks:
            bj = bj + aj * jnp.where(m, pltpu.roll(bj, d, 0), 0.0)
            aj = aj * jnp.where(m, pltpu.roll(aj, d, 0), 1.0)
        h = bj + aj * c
        hs.append(h)
        c = jnp.broadcast_to(h[SUB - 1:SUB, :], (SUB, HD))
    return jnp.concatenate(hs, axis=0), hs[-1][SUB - 1:SUB, :]


def _scan_down(a, b, carry):
    row = lax.broadcasted_iota(jnp.int32, (SUB, HD), 0)
    masks = [(d, row < SUB - d) for d in (1, 2, 4)]
    c = jnp.broadcast_to(carry, (SUB, HD))
    hs = []
    for j in reversed(range(RT // SUB)):
        aj, bj = a[SUB * j:SUB * (j + 1)], b[SUB * j:SUB * (j + 1)]
        for d, m in masks:
            bj = bj + aj * jnp.where(m, pltpu.roll(bj, SUB - d, 0), 0.0)
            aj = aj * jnp.where(m, pltpu.roll(aj, SUB - d, 0), 1.0)
        h = bj + aj * c
        hs.append(h)
        c = jnp.broadcast_to(h[0:1, :], (SUB, HD))
    return jnp.concatenate(hs[::-1], axis=0), hs[-1][0:1, :]


def _decay(r, sp_d):
    log_a = -LRU_C * r * sp_d
    a = jnp.exp(log_a)
    return a, jnp.sqrt(jnp.maximum(-jnp.tanh(log_a) * (a * a + 1.0), 0.0))


def _decay_bwd(r, sp_d):
    log_a = -LRU_C * r * sp_d
    a = jnp.exp(log_a)
    m2 = jnp.maximum(-jnp.tanh(log_a) * (a * a + 1.0), 0.0)
    inv = jnp.where(m2 > 0.0, lax.rsqrt(m2), 0.0)
    return a, m2 * inv, inv


def _lru_gates(xc, d, wr_ref, br_ref, wi_ref, bi_ref, sp):
    xb = xc.astype(BF16)
    r = _sigmoid(_dot(xb, wr_ref[d]) + br_ref[d:d + 1, :])
    i = _sigmoid(_dot(xb, wi_ref[d]) + bi_ref[d:d + 1, :])
    a, mult = _decay(r, sp[d:d + 1, :])
    return r, i, a, mult


def _shifted(win, k):
    w = RT + 2 * PADR
    v = win if k == 0 else pltpu.roll(win, (-k) % w, 0)
    return v[PADR:PADR + RT]


def _conv_taps(win):
    return [_shifted(win, k) for k in (-1, 0, 1, 2)]


def _fill_padded(dst, src_ref, S):
    zeros = jnp.zeros((PADR, HD), F32)
    dst[0:PADR, :] = zeros
    dst[PADR + S:2 * PADR + S, :] = zeros

    def cp(i, c):
        t0 = pl.multiple_of(i * RT, RT)
        dst[pl.ds(t0 + PADR, RT), :] = src_ref[pl.ds(t0, RT), :].astype(F32)
        return c
    lax.fori_loop(0, S // RT, cp, 0)


def _conv_fwd_all(zxp, xc_s, cw_ref, cb_ref, S):
    def cv(i, c):
        t0 = pl.multiple_of(i * RT, RT)
        xm1, x0, xp1, xp2 = _conv_taps(zxp[pl.ds(t0, RT + 2 * PADR), :])
        xc_s[pl.ds(t0, RT), :] = (cb_ref[...] + xm1 * cw_ref[0:1, :] + x0 * cw_ref[1:2, :]
                                  + xp1 * cw_ref[2:3, :] + xp2 * cw_ref[3:4, :])
        return c
    lax.fori_loop(0, S // RT, cv, 0)


def _lru_specs(S):
    head = lambda h: (0, h)
    return [pl.BlockSpec((4, HD), head), pl.BlockSpec((1, HD), head),
            pl.BlockSpec((2, None, HD, HD), lambda h: (0, h, 0, 0)), pl.BlockSpec((2, HD), head),
            pl.BlockSpec((2, None, HD, HD), lambda h: (0, h, 0, 0)), pl.BlockSpec((2, HD), head),
            pl.BlockSpec((2, HD), head)]


def _lru_fwd(z6, ya, cw, cb, wr, br, wi, bi, lam, after=()):
    S = z6.shape[1]
    nt = S // RT

    def body(z_ref, ya_ref, cw_ref, cb_ref, wr_ref, br_ref, wi_ref, bi_ref, lam_ref, *rest):
        mg_ref, h0_ref, h1_ref, zxp, xc_s = rest[-5:]
        sp = _softplus_neg(lam_ref[...])
        _fill_padded(zxp, z_ref.at[0], S)
        _conv_fwd_all(zxp, xc_s, cw_ref, cb_ref, S)

        def scans(i, carry):
            cu, cd = carry
            for u in range(UNROLL):
                j = i * UNROLL + u
                ru = pl.ds(pl.multiple_of(j * RT, RT), RT)
                rd = pl.ds(pl.multiple_of((nt - 1 - j) * RT, RT), RT)
                xu, xd = xc_s[ru, :], xc_s[rd, :]
                _, gi, a, mult = _lru_gates(xu, 0, wr_ref, br_ref, wi_ref, bi_ref, sp)
                hu, cu = _scan_up(a, mult * gi * xu, cu)
                h0_ref[ru, :] = hu
                _, gi, a, mult = _lru_gates(xd, 1, wr_ref, br_ref, wi_ref, bi_ref, sp)
                hd, cd = _scan_down(a, mult * gi * xd, cd)
                h1_ref[rd, :] = hd
            return cu, cd
        z1 = jnp.zeros((1, HD), F32)
        lax.fori_loop(0, nt // UNROLL, scans, (z1, z1))

        def merge(i, c):
            rows = pl.ds(pl.multiple_of(i * RT, RT), RT)
            yb = (h0_ref[rows, :] + h1_ref[rows, :]) * _gelu(z_ref[1, rows, :].astype(F32))
            mg_ref[rows, :] = (ya_ref[rows, :].astype(F32) + _sigmoid(z_ref[2, rows, :].astype(F32)) * yb).astype(BF16)
            return c
        lax.fori_loop(0, nt, merge, 0)

    col = pl.BlockSpec((S, HD), lambda h: (0, h))
    return pl.pallas_call(
        body, name="lru_fwd", grid=(NH,),
        in_specs=[pl.BlockSpec((3, S, HD), lambda h: (1, 0, h)), col] + _lru_specs(S) + [ANY] * len(after),
        out_specs=[col, col, col],
        out_shape=[jax.ShapeDtypeStruct((S, D), BF16), jax.ShapeDtypeStruct((S, D), F32), jax.ShapeDtypeStruct((S, D), F32)],
        scratch_shapes=[pltpu.VMEM((S + 2 * PADR, HD), F32), pltpu.VMEM((S, HD), F32)],
        compiler_params=_cp(("parallel",)),
    )(z6, ya, cw, cb, wr, br, wi, bi, lam, *after)


def _mm_res_loss(a, w, res, tgt, g):
    S, K = a.shape

    def body(a_ref, w_ref, r_ref, t_ref, g_ref, dx_ref, loss_ref, dg_ref):
        @pl.when(pl.program_id(0) == 0)
        def _():
            loss_ref[...] = jnp.zeros_like(loss_ref)
            dg_ref[...] = jnp.zeros_like(dg_ref)
        xv = r_ref[...] + _dot(a_ref[...], w_ref[...])
        xh, _ = _rms_hat(xv)
        e = xh * g_ref[...] - t_ref[...]
        loss_ref[...] += jnp.sum(e * e) * (0.5 / D)
        dx, dgs = _rms_bwd(e * (1.0 / D), xv, g_ref[...])
        dx_ref[...] = dx
        dg_ref[...] += dgs

    row = pl.BlockSpec((TM, D), lambda i: (i, 0))
    vec = pl.BlockSpec((1, D), lambda i: (0, 0))
    return pl.pallas_call(
        body, name="mm_ffn_out_loss", grid=(S // TM,),
        in_specs=[pl.BlockSpec((TM, K), lambda i: (i, 0)), pl.BlockSpec((K, D), lambda i: (0, 0)), row, row, vec],
        out_specs=[row, pl.BlockSpec((1, 128), lambda i: (0, 0)), vec],
        out_shape=[jax.ShapeDtypeStruct((S, D), F32), jax.ShapeDtypeStruct((1, 128), F32), jax.ShapeDtypeStruct((1, D), F32)],
        compiler_params=_cp(("arbitrary",)),
    )(a, w, res, tgt, g)


def _bwd_ffn_out(dx, w_fo, gu, l, after=()):
    S = dx.shape[0]

    tm = min(TM_BIG, S)

    def body(dx_ref, w_ref, gu_ref, *rest):
        o_ref = rest[-1]
        for r0 in range(0, tm, TM):
            rows = slice(r0, r0 + TM)
            d = _dot_nt(dx_ref[rows, :].astype(BF16), w_ref[...])
            o_ref[0, rows, :] = (d * gu_ref[0, rows, :].astype(F32)).astype(BF16)
            o_ref[1, rows, :] = (d * gu_ref[1, rows, :].astype(F32)).astype(BF16)

    pair = pl.BlockSpec((2, None, tm, DFF_SH), lambda i, s: (0, s, i, 0))
    dgu = pl.pallas_call(
        body, name=f"bwd_ffn_out_{l}", grid=(S // tm, 2),
        in_specs=[pl.BlockSpec((tm, D), lambda i, s: (i, 0)), pl.BlockSpec((DFF_SH, D), lambda i, s: (s, 0)), pair]
        + [ANY] * len(after),
        out_specs=pair,
        out_shape=jax.ShapeDtypeStruct((2, 2, S, DFF_SH), BF16),
        compiler_params=_cp(("parallel", "arbitrary")),
    )(dx, w_fo, gu.reshape(2, 2, S, DFF_SH), *after)
    return dgu.reshape(4, S, DFF_SH)


def _mm_tn(a, b, m_blk, tk, name):
    S, M = a.shape

    def body(a_ref, b_ref, o_ref):
        @pl.when(pl.program_id(1) == 0)
        def _():
            o_ref[...] = jnp.zeros_like(o_ref)
        o_ref[...] += _dot_tn(a_ref[...], b_ref[...].astype(BF16))

    return pl.pallas_call(
        body, name=name, grid=(M // m_blk, S // tk),
        in_specs=[pl.BlockSpec((tk, m_blk), lambda m, k: (k, m)), pl.BlockSpec((tk, D), lambda m, k: (k, 0))],
        out_specs=pl.BlockSpec((m_blk, D), lambda m, k: (m, 0)),
        out_shape=jax.ShapeDtypeStruct((M, D), F32),
        compiler_params=_cp(("parallel", "arbitrary")),
    )(a, b)


def _mm_nt_rms_bwd(a, a_specs, w, w_specs, nk, tm, x, g, dres, name, after=()):
    S = x.shape[0]
    sub = len(a_specs)

    def body(*refs):
        a_refs, w_refs = refs[:sub], refs[sub:2 * sub]
        x_ref, g_ref, r_ref = refs[2 * sub:2 * sub + 3]
        dx_ref, dg_ref, acc = refs[-3:]
        i, k = pl.program_id(0), pl.program_id(1)
        @pl.when(k == 0)
        def _():
            acc[...] = jnp.zeros_like(acc)
        for j in range(sub):
            acc[...] += _dot_nt(a_refs[j][...], w_refs[j][...])

        @pl.when(jnp.logical_and(i == 0, k == 0))
        def _():
            dg_ref[...] = jnp.zeros_like(dg_ref)

        @pl.when(k == nk - 1)
        def _():
            dx, dgs = _rms_bwd(acc[...], x_ref[...], g_ref[...])
            dx_ref[...] = r_ref[...] + dx
            dg_ref[...] += dgs

    row = pl.BlockSpec((tm, D), lambda i, k: (i, 0))
    vec = pl.BlockSpec((1, D), lambda i, k: (0, 0))
    return pl.pallas_call(
        body, name=name, grid=(S // tm, nk),
        in_specs=list(a_specs) + list(w_specs) + [row, vec, row] + [ANY] * len(after),
        out_specs=[row, vec],
        out_shape=[jax.ShapeDtypeStruct((S, D), F32), jax.ShapeDtypeStruct((1, D), F32)],
        scratch_shapes=[pltpu.VMEM((tm, D), F32)],
        compiler_params=_cp(("arbitrary", "arbitrary")),
    )(*[a] * sub, *[w] * sub, x, g, dres, *after)


def _dw_ffn_in(h, dgu, l):
    S = h.shape[0]

    def body(h_ref, b_ref, o_ref):
        @pl.when(pl.program_id(1) == 0)
        def _():
            o_ref[...] = jnp.zeros_like(o_ref)
        o_ref[...] += _dot_tn(h_ref[...], b_ref[...])

    tk = min(2 * TM_BIG, S)
    return pl.pallas_call(
        body, name=f"dw_ffn_in_{l}", grid=(4, S // tk),
        in_specs=[pl.BlockSpec((tk, D), lambda j, k: (k, 0)), pl.BlockSpec((None, tk, DFF_SH), lambda j, k: (j, k, 0))],
        out_specs=pl.BlockSpec((None, D, DFF_SH), lambda j, k: (j, 0, 0)),
        out_shape=jax.ShapeDtypeStruct((4, D, DFF_SH), F32),
        compiler_params=_cp(("parallel", "arbitrary")),
    )(h, dgu)


_HALF_COMPS = ((0, 1, 3), (4, 2, 5))


def _dw_in(h, dz6, l, after=()):
    S = h.shape[0]

    def body(h_ref, d0_ref, d1_ref, d2_ref, *rest):
        o_ref = rest[-1]

        @pl.when(pl.program_id(1) == 0)
        def _():
            o_ref[...] = jnp.zeros_like(o_ref)
        hv = h_ref[...]
        for q, d_ref in enumerate((d0_ref, d1_ref, d2_ref)):
            for hf in range(2):
                col = 1024 * q + 512 * hf
                o_ref[col // 1536, :, col % 1536:col % 1536 + 512] += _dot_tn(hv, d_ref[:, 512 * hf:512 * (hf + 1)])

    tk = min(TM_BIG, S)

    def comp(q):
        return pl.BlockSpec((None, tk, D), lambda p, k: (jnp.where(p == 0, _HALF_COMPS[0][q], _HALF_COMPS[1][q]), k, 0))

    return pl.pallas_call(
        body, name=f"dw_in_{l}", grid=(2, S // tk),
        in_specs=[pl.BlockSpec((tk, D), lambda p, k: (k, 0)), comp(0), comp(1), comp(2)] + [ANY] * len(after),
        out_specs=pl.BlockSpec((2, D, 1536), lambda p, k: (p, 0, 0)),
        out_shape=jax.ShapeDtypeStruct((4, D, 1536), F32),
        compiler_params=_cp(("parallel", "arbitrary")),
    )(h, dz6, dz6, dz6, *after)


def _bwd_out(dx, w_o, merged, l):
    S = dx.shape[0]

    def body(dx_ref, w_ref, m_ref, dm_ref, dw_ref):
        @pl.when(pl.program_id(0) == 0)
        def _():
            dw_ref[...] = jnp.zeros_like(dw_ref)
        dxb = dx_ref[...].astype(BF16)
        dm_ref[...] = _dot_nt(dxb, w_ref[...]).astype(BF16)
        dw_ref[...] += _dot_tn(m_ref[...], dxb)

    tm = TM
    row = pl.BlockSpec((tm, D), lambda i: (i, 0))
    return pl.pallas_call(
        body, name=f"bwd_out_{l}", grid=(S // tm,),
        in_specs=[row, pl.BlockSpec((D, D), lambda i: (0, 0)), row],
        out_specs=[row, pl.BlockSpec((D, D), lambda i: (0, 0))],
        out_shape=[jax.ShapeDtypeStruct((S, D), BF16), jax.ShapeDtypeStruct((D, D), F32)],
        compiler_params=_cp(("arbitrary",)),
    )(dx, w_o, merged)


def _gmlp_bwd(dm, z6, ws_b, wst_b, bs_b, lg, lb, after=()):
    S = z6.shape[1]
    ts = min(GMLP_ROWS, S)

    def body(dm_ref, z_ref, ws_ref, wst_ref, bs_ref, lg_ref, lb_ref, *rest):
        dz_ref, dws_ref, dbs_ref, dlg_ref, dlb_ref, mix, dv = rest[-7:]

        @pl.when(pl.program_id(0) == 0)
        def _():
            dws_ref[...] = jnp.zeros_like(dws_ref)
            dbs_ref[...] = jnp.zeros_like(dbs_ref)
            dlg_ref[...] = jnp.zeros_like(dlg_ref)
            dlb_ref[...] = jnp.zeros_like(dlb_ref)
        for r0 in range(0, ts, CHUNK):
            rows = slice(r0, r0 + CHUNK)
            gv, dgelu_v = _gelu_and_grad(z_ref[1, rows, :].astype(F32))
            xc = gv - jnp.mean(gv, axis=-1, keepdims=True)
            rs = lax.rsqrt(jnp.mean(xc * xc, axis=-1, keepdims=True) + EPS)
            vh = xc * rs
            vb = (vh * lg_ref[...] + lb_ref[...]).astype(BF16)
            for gi in range(NH):
                cs = slice(gi * HD, (gi + 1) * HD)
                mix[rows, cs] = _dot(ws_ref[gi], vb[:, cs])
            u, dgelu_u = _gelu_and_grad(z_ref[0, rows, :].astype(F32))
            sa = _sigmoid(z_ref[2, rows, :].astype(F32))
            dya = dm_ref[rows, :].astype(F32) * sa
            dym = dya * (mix[rows, :] + bs_ref[...])
            dz_ref[2, rows, :] = (dym * u * (1.0 - sa)).astype(BF16)
            dz_ref[0, rows, :] = (dym * dgelu_u).astype(BF16)
            dmix = dya * u
            dmb = dmix.astype(BF16)
            for gi in range(NH):
                cs = slice(gi * HD, (gi + 1) * HD)
                dv[rows, cs] = _dot(wst_ref[gi], dmb[:, cs])
                dws_ref[gi] += _dot_nt(dmb[:, cs], vb[:, cs])
                dbs_ref[gi] += jnp.broadcast_to(jnp.sum(dmix[:, cs], axis=1, keepdims=True), (CHUNK, HD))
            dvv = dv[rows, :]
            dlg_ref[...] += jnp.sum(dvv * vh, axis=0, keepdims=True)
            dlb_ref[...] += jnp.sum(dvv, axis=0, keepdims=True)
            dvh = dvv * lg_ref[...]
            dgv = rs * (dvh - jnp.mean(dvh, axis=-1, keepdims=True) - vh * jnp.mean(dvh * vh, axis=-1, keepdims=True))
            dz_ref[1, rows, :] = (dgv * dgelu_v).astype(BF16)

    vec = pl.BlockSpec((1, D), lambda i: (0, 0))
    mat = pl.BlockSpec((NH, CHUNK, CHUNK), lambda i: (0, 0, 0))
    return pl.pallas_call(
        body, name="gmlp_bwd", grid=(S // ts,),
        in_specs=[pl.BlockSpec((ts, D), lambda i: (i, 0)), pl.BlockSpec((3, ts, D), lambda i: (0, i, 0)), mat, mat,
                  pl.BlockSpec((CHUNK, D), lambda i: (0, 0)), vec, vec] + [ANY] * len(after),
        out_specs=[pl.BlockSpec((3, ts, D), lambda i: (0, i, 0)), mat, mat, vec, vec],
        out_shape=[jax.ShapeDtypeStruct((6, S, D), BF16), jax.ShapeDtypeStruct((NH, CHUNK, CHUNK), F32),
                   jax.ShapeDtypeStruct((NH, CHUNK, HD), F32), jax.ShapeDtypeStruct((1, D), F32), jax.ShapeDtypeStruct((1, D), F32)],
        scratch_shapes=[pltpu.VMEM((ts, D), F32), pltpu.VMEM((ts, D), F32)],
        compiler_params=_cp(("arbitrary",)),
    )(dm, z6, ws_b, wst_b, bs_b, lg, lb, *after)


def _lru_bwd(dz6, dm, z6, h0, h1, cw, cb, wr, br, wi, bi, lam, after=()):
    S = z6.shape[1]
    nt = S // RT

    def body(dz_in, dm_ref, z_ref, h0_ref, h1_ref, cw_ref, cb_ref, wr_ref, br_ref, wi_ref, bi_ref, lam_ref, *rest):
        dz_ref, dcw_ref, dcb_ref, dwr_ref, dbr_ref, dwi_ref, dbi_ref, dlam_ref, zxp, xc_s, dhs_s, dxcp, r_s, lam_s = rest[-14:]
        del dz_in
        lam = lam_ref[...]
        sp = _softplus_neg(lam)
        row = _row_iota()
        _fill_padded(zxp, z_ref.at[0], S)
        _conv_fwd_all(zxp, xc_s, cw_ref, cb_ref, S)
        zeros = jnp.zeros((PADR, HD), F32)
        dxcp[0:PADR, :] = zeros
        dxcp[PADR + S:2 * PADR + S, :] = zeros
        dwr_ref[...] = jnp.zeros_like(dwr_ref)
        dwi_ref[...] = jnp.zeros_like(dwi_ref)

        def pre(i, c):
            rows = pl.ds(pl.multiple_of(i * RT, RT), RT)
            hs = h0_ref[rows, :] + h1_ref[rows, :]
            dmv = dm_ref[rows, :].astype(F32)
            sb = _sigmoid(z_ref[2, rows, :].astype(F32))
            gg, dgg = _gelu_and_grad(z_ref[1, rows, :].astype(F32))
            dz_ref[2, rows, :] = (dmv * hs * gg * sb * (1.0 - sb)).astype(BF16)
            dyb = dmv * sb
            dz_ref[1, rows, :] = (dyb * hs * dgg).astype(BF16)
            dhs_s[rows, :] = dyb * gg
            return c
        lax.fori_loop(0, nt, pre, 0)

        def gate_bwd(d, gates, lamv, da, xc):
            r, gi, a, mult, inv_mult = gates
            lx, lm = lamv * xc, lamv * mult
            dlog_r = (da - (lx * gi) * (a * inv_mult)) * a * r
            dpr = dlog_r * (1.0 - r) * (-LRU_C * sp[d:d + 1, :])
            dpi = (lx * mult) * gi * (1.0 - gi)
            xb, dprb, dpib = xc.astype(BF16), dpr.astype(BF16), dpi.astype(BF16)
            dwr_ref[d] += _dot_tn(xb, dprb)
            dwi_ref[d] += _dot_tn(xb, dpib)
            dxc = lm * gi + _dot_nt(dprb, wr_ref[d]) + _dot_nt(dpib, wi_ref[d])
            return dxc, (jnp.sum(dlog_r, axis=0, keepdims=True) * (-LRU_C), jnp.sum(dpr, axis=0, keepdims=True),
                         jnp.sum(dpi, axis=0, keepdims=True))

        def rgates(i, c):
            for u in range(UNROLL):
                rows = pl.ds(pl.multiple_of((i * UNROLL + u) * RT, RT), RT)
                xb = xc_s[rows, :].astype(BF16)
                for d in range(2):
                    r_s[d, rows, :] = _sigmoid(_dot(xb, wr_ref[d]) + br_ref[d:d + 1, :])
            return c
        lax.fori_loop(0, nt // UNROLL, rgates, 0)

        def chains(i, carry):
            qn, qp = carry
            for u in range(UNROLL):
                j = i * UNROLL + u
                rd = pl.ds(pl.multiple_of((nt - 1 - j) * RT, RT), RT)
                a, dhs = _decay(r_s[0, rd, :], sp[0:1, :])[0], dhs_s[rd, :]
                q, q_first = _scan_down(a, a * dhs, qn)
                lam_s[0, rd, :] = dhs + jnp.where(row == RT - 1, qn, pltpu.roll(q, RT - 1, 0))
                qn = q_first
                ru = pl.ds(pl.multiple_of(j * RT, RT), RT)
                a, dhs = _decay(r_s[1, ru, :], sp[1:2, :])[0], dhs_s[ru, :]
                q, q_last = _scan_up(a, a * dhs, qp)
                lam_s[1, ru, :] = dhs + jnp.where(row == 0, qp, pltpu.roll(q, 1, 0))
                qp = q_last
            return qn, qp

        z1 = jnp.zeros((1, HD), F32)
        lax.fori_loop(0, nt // UNROLL, chains, (z1, z1))

        ct = min(GRAD_ROWS, S)
        crow = lax.broadcasted_iota(jnp.int32, (ct, HD), 0)

        def tile_grads(i, acc):
            t0 = pl.multiple_of(i * ct, ct)
            rows = pl.ds(t0, ct)
            xc = xc_s[rows, :]
            xb = xc.astype(BF16)
            tp = pl.multiple_of(jnp.maximum(t0 - PADR, 0), PADR)
            prev = jnp.where(t0 > 0, h0_ref[pl.ds(tp, PADR), :][PADR - 1:PADR, :], 0.0)
            tn = pl.multiple_of(jnp.minimum(t0 + ct, S - PADR), PADR)
            nxt = jnp.where(t0 + ct < S, h1_ref[pl.ds(tn, PADR), :][0:1, :], 0.0)
            hside = (jnp.where(crow == 0, prev, pltpu.roll(h0_ref[rows, :], 1, 0)),
                     jnp.where(crow == ct - 1, nxt, pltpu.roll(h1_ref[rows, :], ct - 1, 0)))
            dxc, sums = 0.0, ()
            for d in range(2):
                r = r_s[d, rows, :]
                gi = _sigmoid(_dot(xb, wi_ref[d]) + bi_ref[d:d + 1, :])
                lamv = lam_s[d, rows, :]
                dxc_d, s_d = gate_bwd(d, (r, gi) + _decay_bwd(r, sp[d:d + 1, :]), lamv, lamv * hside[d], xc)
                dxc = dxc + dxc_d
                sums = sums + s_d
            dxcp[pl.ds(t0 + PADR, ct), :] = dxc
            return tuple(x + y for x, y in zip(acc, sums))

        s_sp0, s_br0, s_bi0, s_sp1, s_br1, s_bi1 = lax.fori_loop(0, S // ct, tile_grads, (z1,) * 6)

        dsp = jnp.concatenate([s_sp0, s_sp1], axis=0)
        dlam_ref[...] = -dsp * _sigmoid(-lam)
        dbr_ref[...] = jnp.concatenate([s_br0, s_br1], axis=0)
        dbi_ref[...] = jnp.concatenate([s_bi0, s_bi1], axis=0)

        def conv_bwd(i, carry):
            c0, c1, c2, c3, cb_ = carry
            t0 = pl.multiple_of(i * RT, RT)
            dwin = dxcp[pl.ds(t0, RT + 2 * PADR), :]
            d0 = _shifted(dwin, 0)
            dz_ref[0, pl.ds(t0, RT), :] = (_shifted(dwin, 1) * cw_ref[0:1, :] + d0 * cw_ref[1:2, :]
                                           + _shifted(dwin, -1) * cw_ref[2:3, :] + _shifted(dwin, -2) * cw_ref[3:4, :]).astype(BF16)
            xm1, x0, xp1, xp2 = _conv_taps(zxp[pl.ds(t0, RT + 2 * PADR), :])
            sm = lambda v: jnp.sum(v, axis=0, keepdims=True)
            return c0 + sm(d0 * xm1), c1 + sm(d0 * x0), c2 + sm(d0 * xp1), c3 + sm(d0 * xp2), cb_ + sm(d0)

        c0, c1, c2, c3, cb_ = lax.fori_loop(0, nt, conv_bwd, (z1, z1, z1, z1, z1))
        dcw_ref[...] = jnp.concatenate([c0, c1, c2, c3], axis=0)
        dcb_ref[...] = cb_

    col = pl.BlockSpec((S, HD), lambda h: (0, h))
    head = lambda h: (0, h)
    wspec = pl.BlockSpec((2, None, HD, HD), lambda h: (0, h, 0, 0))
    return pl.pallas_call(
        body, name="lru_bwd", grid=(NH,),
        in_specs=[pl.BlockSpec(memory_space=pl.ANY), col, pl.BlockSpec((3, S, HD), lambda h: (1, 0, h)), col, col] + _lru_specs(S)
        + [ANY] * len(after),
        out_specs=[pl.BlockSpec((3, S, HD), lambda h: (1, 0, h)), pl.BlockSpec((4, HD), head), pl.BlockSpec((1, HD), head),
                   wspec, pl.BlockSpec((2, HD), head), wspec, pl.BlockSpec((2, HD), head), pl.BlockSpec((2, HD), head)],
        out_shape=[jax.ShapeDtypeStruct((6, S, D), BF16), jax.ShapeDtypeStruct((4, D), F32), jax.ShapeDtypeStruct((1, D), F32),
                   jax.ShapeDtypeStruct((2, NH, HD, HD), F32), jax.ShapeDtypeStruct((2, D), F32),
                   jax.ShapeDtypeStruct((2, NH, HD, HD), F32), jax.ShapeDtypeStruct((2, D), F32), jax.ShapeDtypeStruct((2, D), F32)],
        scratch_shapes=[pltpu.VMEM((S + 2 * PADR, HD), F32), pltpu.VMEM((S, HD), F32), pltpu.VMEM((S, HD), F32),
                        pltpu.VMEM((S + 2 * PADR, HD), F32), pltpu.VMEM((2, S, HD), F32), pltpu.VMEM((2, S, HD), F32)],
        input_output_aliases={0: 0},
        compiler_params=_cp(("parallel",)),
    )(dz6, dm, z6, h0, h1, cw, cb, wr, br, wi, bi, lam, *after)


LAYER_SMALL = ("norm1_g", "gmlp_ln_g", "gmlp_ln_b", "gmlp_w_s", "gmlp_b_s", "conv_w", "conv_b",
               "lru_w_r", "lru_b_r", "lru_w_i", "lru_b_i", "lru_lambda", "norm2_g")


def _layer_operands(l, p):
    ws_b = p["gmlp_w_s"][l].astype(BF16)
    tm = dict(ws_b=ws_b, wst_b=jnp.swapaxes(ws_b, 1, 2), bs_b=jnp.repeat(p["gmlp_b_s"][l].T, HD, axis=1),
              lg=p["gmlp_ln_g"][l][None], lb=p["gmlp_ln_b"][l][None])
    lru = (p["conv_w"][l], p["conv_b"][l][None], p["lru_w_r"][l].astype(BF16), p["lru_b_r"][l],
           p["lru_w_i"][l].astype(BF16), p["lru_b_i"][l], p["lru_lambda"][l])
    return (p["norm1_g"][l][None], p["norm2_g"][l][None]), tm, lru


def _forward_layer(l, x, p, wb, after=(), early=None, rest=None, near_end=None, operands=None, loss=None):
    (g1, g2), tm, lru = _layer_operands(l, p) if operands is None else operands
    z6, hn1 = _mm_in(x, g1, wb["w_in"], l, after)
    ya = _gmlp_fwd(z6, tm["ws_b"], tm["bs_b"], tm["lg"], tm["lb"])
    merged, h0, h1 = _lru_fwd(z6, ya, *lru, after=() if early is None else tuple(early(ya)))
    if rest is not None:
        wb = dict(wb, **rest(merged))
    x1 = _mm_res(merged, wb["w_out"], x, l, "mm_out")
    gu, ff, hn2 = _mm_ffn_in(x1, g2, wb["w_ffn_in"], l)
    if loss is None:
        x2 = _mm_res(ff, wb["w_ffn_out"], x1, l, "mm_ffn_out", () if near_end is None else tuple(near_end(gu)))
    else:
        x2 = _mm_res_loss(ff, wb["w_ffn_out"], x1, *loss)
    return x2, dict(x=x, z6=z6, h0=h0, h1=h1, merged=merged, x1=x1, gu=gu, ff=ff, g1=g1, g2=g2, tm=tm, lru=lru,
                    hn1=hn1, hn2=hn2, wb=wb)


def _backward_layer(l, dx, s, after=(), midway=None, midway2=None, midway3=None, late=None):
    S = dx.shape[0]
    tm, wb = s["tm"], s["wb"]
    g2 = s["g2"]
    dgu = _bwd_ffn_out(dx, wb["w_ffn_out"], s["gu"], l, after)
    tmb = min(TM_BIG, S)
    dwfo = _mm_tn(s["ff"], dx, DFF_SH, tmb, f"dw_ffn_out_{l}")
    dx1, dg2 = _mm_nt_rms_bwd(
        dgu, [pl.BlockSpec((None, tmb, DFF_SH), lambda i, k: (k, i, 0))],
        wb["w_ffn_in"], [pl.BlockSpec((None, D, DFF_SH), lambda i, k: (k, 0, 0))],
        4, tmb, s["x1"], g2, dx, f"bwd_ffn_in_{l}")
    dwfi = _dw_ffn_in(s["hn2"], dgu, l)
    dmg, dwo = _bwd_out(dx1, wb["w_out"], s["merged"], l)
    mid = () if midway is None else tuple(midway([dwo, dwfi, dwfo]))
    dz6, dws, dbs, dlg, dlb = _gmlp_bwd(dmg, s["z6"], tm["ws_b"], tm["wst_b"], tm["bs_b"], tm["lg"], tm["lb"], mid)
    mid2 = () if midway2 is None else tuple(midway2(dws))
    dz6, dcw, dcb, dwr, dbr, dwi, dbi, dlam = _lru_bwd(dz6, dmg, s["z6"], s["h0"], s["h1"], *s["lru"], after=mid2)

    sub = 3

    def dz_tile(j):
        return pl.BlockSpec((None, tmb, 512), lambda i, k: ((sub * k + j) // 2, i, (sub * k + j) % 2))

    def w_tile(j):
        def w_map(i, k):
            sh, tl = _in_tile(sub * k + j)
            return (sh, 0, tl)
        return pl.BlockSpec((None, D, 512), w_map)

    small = dict(gmlp_ln_g=dlg[0], gmlp_ln_b=dlb[0], gmlp_w_s=dws, gmlp_b_s=dbs[:, :, 0], conv_w=dcw, conv_b=dcb[0],
                 lru_w_r=dwr, lru_b_r=dbr, lru_w_i=dwi, lru_b_i=dbi, lru_lambda=dlam, norm2_g=dg2[0])
    mid3 = () if midway3 is None else tuple(midway3(small))
    dwin = _dw_in(s["hn1"], dz6, l, mid3)
    tail = () if late is None else tuple(late([dwin]))
    dx0, dg1 = _mm_nt_rms_bwd(
        dz6, [dz_tile(j) for j in range(sub)], wb["w_in"], [w_tile(j) for j in range(sub)],
        N_IN_T // sub, tmb, s["x"], s["g1"], dx1, f"bwd_in_{l}", tail)
    return dx0, [dwin, dwo, dwfi, dwfo], dict(small, norm1_g=dg1[0])


def _local_step(x, tgt, p, wbs):
    saved = []
    for l in range(2):
        x, s = _forward_layer(l, x, p, wbs[l], loss=(tgt, p["final_g"][None]) if l else None)
        saved.append(s)
    dx, loss_v, dfg = x
    big, smalls = [None, None], [None, None]
    for l in (1, 0):
        dx, big[l], smalls[l] = _backward_layer(l, dx, saved[l])
    small = {k: jnp.stack([smalls[0][k], smalls[1][k]]) for k in LAYER_SMALL}
    small["final_g"] = dfg[0]
    return loss_v, dx, big, small


def _place():
    x, y, c = lax.axis_index("x"), lax.axis_index("y"), lax.axis_index("c")
    return x, y, c, 2 * x + y


def _chip_at(x, y, d):
    px = 1 - x if d & 2 else x
    py = 1 - y if d & 1 else y
    return px, py, 2 * px + py


HBM = pl.BlockSpec(memory_space=pltpu.HBM)
SEM = pl.BlockSpec(memory_space=pltpu.SEMAPHORE)
DATAFLOW = pltpu.SideEffectType.DATAFLOW_SIDE_EFFECTING


def _in_hbm(a):
    return pltpu.with_memory_space_constraint(a, pltpu.HBM)


def _cast_into(wfs, l, chip_arr, name):
    n = len(wfs)

    def body(ch_ref, *refs):
        for w_ref, o_ref in zip(refs[:n], refs[n:]):
            o_ref[...] = w_ref[...].astype(BF16)

    halves = [(wf.shape[1] // 2, wf.shape[2]) for wf in wfs]
    return pl.pallas_call(
        body, name=name, out_shape=[jax.ShapeDtypeStruct((4, 2, rh, cols), BF16) for rh, cols in halves],
        grid_spec=pltpu.PrefetchScalarGridSpec(
            num_scalar_prefetch=1, grid=(2,),
            in_specs=[pl.BlockSpec((None, None, rh, cols), lambda h, ch: (l, h, 0, 0)) for rh, cols in halves],
            out_specs=[pl.BlockSpec((None, None, rh, cols), lambda h, ch: (ch[0], h, 0, 0)) for rh, cols in halves]),
        compiler_params=_cp(("parallel",)),
    )(chip_arr, *[wf.reshape(2, 2, rh, cols) for wf, (rh, cols) in zip(wfs, halves)])


def _half_block(ref, chip, half, to, send_sem, recv_sem):
    blk = ref.at[chip, half]
    return pltpu.make_async_remote_copy(src_ref=blk, dst_ref=blk, send_sem=send_sem, recv_sem=recv_sem,
                                        device_id=to, device_id_type=MESH)


def _gather_weights(bufs, tiny):
    nt = len(bufs)
    n_ici = max(nt * 3, 1)

    def body(*refs):
        tiny_ref = refs[nt]
        o_refs, tiny_o = refs[nt + 1:2 * nt + 1], refs[2 * nt + 1]
        send, recv, fsend, frecv, tsend, trecv, lsem = refs[2 * nt + 2:]
        x, y, c, chip = _place()
        local = pltpu.make_async_copy(tiny_ref, tiny_o.at[chip], lsem)
        local.start()

        def tin(d, origin_chip, to):
            return pltpu.make_async_remote_copy(
                src_ref=tiny_ref, dst_ref=tiny_o.at[origin_chip], send_sem=tsend.at[d - 1], recv_sem=trecv.at[d - 1],
                device_id=to, device_id_type=MESH)

        sends = []
        for t in range(nt):
            for d in (1, 2, 3):
                px, py, _ = _chip_at(x, y, d)
                sends.append(_half_block(o_refs[t], chip, c, (px, py, c), send.at[3 * t + d - 1], recv.at[3 * t + d - 1]))
        for d in (1, 2, 3):
            px, py, _ = _chip_at(x, y, d)
            sends.append(tin(d, chip, (px, py, c)))
        for cp in sends:
            cp.start()
        passed = []
        for t in range(nt):
            for d in (1, 2, 3):
                k = 3 * t + d - 1
                _, _, pchip = _chip_at(x, y, d)
                _half_block(o_refs[t], pchip, c, (x, y, c), send.at[k], recv.at[k]).wait_recv()
                f = _half_block(o_refs[t], pchip, c, (x, y, 1 - c), fsend.at[k], frecv.at[k])
                f.start()
                passed.append(f)
        for t in range(nt):
            for d in (1, 2, 3):
                k = 3 * t + d - 1
                _, _, pchip = _chip_at(x, y, d)
                _half_block(o_refs[t], pchip, 1 - c, (x, y, 1 - c), fsend.at[k], frecv.at[k]).wait_recv()
        for d in (1, 2, 3):
            _, _, pchip = _chip_at(x, y, d)
            tin(d, pchip, (x, y, c)).wait_recv()
        for cp in sends + passed:
            cp.wait_send()
        local.wait()

    out_shape = [jax.ShapeDtypeStruct(b.shape, b.dtype) for b in bufs]
    out_shape.append(jax.ShapeDtypeStruct((4,) + tiny.shape, tiny.dtype))
    outs = pl.pallas_call(
        body, name="gather_weights_0", out_shape=out_shape,
        in_specs=[ANY] * (nt + 1), out_specs=[ANY] * (nt + 1),
        scratch_shapes=[pltpu.SemaphoreType.DMA((n_ici,)), pltpu.SemaphoreType.DMA((n_ici,)),
                        pltpu.SemaphoreType.DMA((n_ici,)), pltpu.SemaphoreType.DMA((n_ici,)),
                        pltpu.SemaphoreType.DMA((3,)), pltpu.SemaphoreType.DMA((3,)), pltpu.SemaphoreType.DMA],
        input_output_aliases={t: t for t in range(nt)},
        compiler_params=_cp(has_side_effects=True),
    )(*bufs, tiny)
    return outs[:nt], outs[nt]


def _gather_start(bufs, tag, after=()):
    nt, na = len(bufs), len(after)

    def body(*refs):
        b_refs = refs[:nt]
        send, recv = refs[nt + na], refs[nt + na + 1]
        token = refs[2 * nt + na + 2]
        x, y, c, chip = _place()
        for t in range(nt):
            for d in (1, 2, 3):
                px, py, _ = _chip_at(x, y, d)
                _half_block(b_refs[t], chip, c, (px, py, c), send.at[3 * t + d - 1], recv.at[3 * t + d - 1]).start()
        token[...] = jnp.zeros_like(token)

    outs = pl.pallas_call(
        body, name=f"gather_start_{tag}",
        out_shape=(pltpu.SemaphoreType.DMA((3 * nt,)), pltpu.SemaphoreType.DMA((3 * nt,)),
                   *[pltpu.HBM(b.shape, b.dtype) for b in bufs], jax.ShapeDtypeStruct((8, 128), F32)),
        in_specs=[HBM] * nt + [ANY] * na, out_specs=(SEM, SEM, *[HBM] * nt, pl.BlockSpec(memory_space=pltpu.VMEM)),
        input_output_aliases={t: 2 + t for t in range(nt)},
        compiler_params=pltpu.CompilerParams(has_side_effects=DATAFLOW),
    )(*[_in_hbm(b) for b in bufs], *after)
    return outs[0], outs[1], list(outs[2:2 + nt]), outs[2 + nt]


def _gather_wait(send, recv, bufs, after, tag):
    nt = len(bufs)

    def body(*refs):
        b_refs = refs[:nt]
        send_ref, recv_ref = refs[nt], refs[nt + 1]
        x, y, c, chip = _place()
        for t in range(nt):
            for d in (1, 2, 3):
                k = 3 * t + d - 1
                px, py, pchip = _chip_at(x, y, d)
                _half_block(b_refs[t], chip, c, (px, py, c), send_ref.at[k], recv_ref.at[k]).wait_send()
                _half_block(b_refs[t], pchip, c, (px, py, c), send_ref.at[k], recv_ref.at[k]).wait_recv()

    after = tuple(after) if isinstance(after, (tuple, list)) else (after,)
    outs = pl.pallas_call(
        body, name=f"gather_wait_{tag}", out_shape=[pltpu.HBM(b.shape, b.dtype) for b in bufs],
        in_specs=[HBM] * nt + [SEM, SEM] + [ANY] * len(after), out_specs=[HBM] * nt,
        input_output_aliases={t: t for t in range(nt)},
        compiler_params=pltpu.CompilerParams(has_side_effects=DATAFLOW),
    )(*bufs, send, recv, *after)
    return list(outs)


def _gather_pass_on(bufs, tag):
    nt = len(bufs)

    def body(*refs):
        o_refs = refs[nt:2 * nt]
        fsend, frecv = refs[2 * nt:]
        x, y, c, _ = _place()
        cps = []
        for t in range(nt):
            for d in (1, 2, 3):
                k = 3 * t + d - 1
                _, _, pchip = _chip_at(x, y, d)
                cps.append(_half_block(o_refs[t], pchip, c, (x, y, 1 - c), fsend.at[k], frecv.at[k]))
        for cp in cps:
            cp.start()
        for t in range(nt):
            for d in (1, 2, 3):
                k = 3 * t + d - 1
                _, _, pchip = _chip_at(x, y, d)
                _half_block(o_refs[t], pchip, 1 - c, (x, y, 1 - c), fsend.at[k], frecv.at[k]).wait_recv()
        for cp in cps:
            cp.wait_send()

    return pl.pallas_call(
        body, name=f"gather_pass_on_{tag}", out_shape=[jax.ShapeDtypeStruct(b.shape, b.dtype) for b in bufs],
        in_specs=[ANY] * nt, out_specs=[ANY] * nt,
        scratch_shapes=[pltpu.SemaphoreType.DMA((3 * nt,)), pltpu.SemaphoreType.DMA((3 * nt,))],
        input_output_aliases={t: t for t in range(nt)},
        compiler_params=_cp(has_side_effects=True),
    )(*bufs)


def _chip_copy(c_ref, land_ref, x, y, c, d, send_sem, recv_sem):
    px, py, pchip = _chip_at(x, y, d)
    return pltpu.make_async_remote_copy(src_ref=c_ref.at[pchip], dst_ref=land_ref.at[d - 1], send_sem=send_sem, recv_sem=recv_sem,
                                        device_id=(px, py, c), device_id_type=MESH)


def _exchange_start(srcs, lands, copies, nsem, name):
    ns, n = len(srcs), len(srcs) + len(lands)

    def body(*refs):
        for cp in copies(refs[:ns], refs[ns:n], refs[n], refs[n + 1]):
            cp.start()
        token = refs[2 * n + 2]
        token[...] = jnp.zeros_like(token)

    outs = pl.pallas_call(
        body, name=name,
        out_shape=(pltpu.SemaphoreType.DMA((nsem,)), pltpu.SemaphoreType.DMA((nsem,)),
                   *[pltpu.HBM(a.shape, a.dtype) for a in list(srcs) + list(lands)], jax.ShapeDtypeStruct((8, 128), F32)),
        in_specs=[HBM] * n, out_specs=(SEM, SEM, *[HBM] * n, pl.BlockSpec(memory_space=pltpu.VMEM)),
        input_output_aliases={i: 2 + i for i in range(n)},
        compiler_params=pltpu.CompilerParams(has_side_effects=DATAFLOW),
    )(*[_in_hbm(a) for a in list(srcs) + list(lands)])
    return outs[0], outs[1], list(outs[2:2 + ns]), list(outs[2 + ns:2 + n]), outs[2 + n]


def _exchange_wait(send, recv, srcs, lands, after, copies, name):
    ns, n = len(srcs), len(srcs) + len(lands)

    def body(*refs):
        for cp in copies(refs[:ns], refs[ns:n], refs[n], refs[n + 1]):
            cp.wait_send()
            cp.wait_recv()

    outs = pl.pallas_call(
        body, name=name, out_shape=[pltpu.HBM(a.shape, a.dtype) for a in list(srcs) + list(lands)],
        in_specs=[HBM] * n + [SEM, SEM, ANY], out_specs=[HBM] * n,
        input_output_aliases={i: i for i in range(n)},
        compiler_params=pltpu.CompilerParams(has_side_effects=DATAFLOW),
    )(*srcs, *lands, send, recv, after)
    return list(outs[:ns]), list(outs[ns:])


def _pass_on_copies(b_refs, land_refs, send, recv):
    del land_refs
    x, y, c, _ = _place()
    return [_half_block(b_refs[t], _chip_at(x, y, d)[2], c, (x, y, 1 - c), send.at[3 * t + d - 1], recv.at[3 * t + d - 1])
            for t in range(len(b_refs)) for d in (1, 2, 3)]


def _chips_copies(c_refs, land_refs, send, recv):
    x, y, c, _ = _place()
    return [_chip_copy(c_refs[t], land_refs[t], x, y, c, d, send.at[3 * t + d - 1], recv.at[3 * t + d - 1])
            for t in range(len(c_refs)) for d in (1, 2, 3)]


def _sibling_copies(g_refs, land_refs, send, recv):
    x, y, c, _ = _place()
    return [pltpu.make_async_remote_copy(
        src_ref=g_refs[t].at[k, 1 - c], dst_ref=land_refs[t].at[k], send_sem=send.at[4 * t + k], recv_sem=recv.at[4 * t + k],
        device_id=(x, y, 1 - c), device_id_type=MESH) for t in range(len(g_refs)) for k in range(4)]


def _join_copies(f_refs, land_refs, send, recv):
    del land_refs
    x, y, c, _ = _place()
    return [pltpu.make_async_remote_copy(
        src_ref=f_refs[t].at[c], dst_ref=f_refs[t].at[c], send_sem=send.at[t], recv_sem=recv.at[t],
        device_id=(x, y, 1 - c), device_id_type=MESH) for t in range(len(f_refs))]


def _add_half(gs, rs, c_arr, name):
    n = len(gs)

    def body(c_ref, *refs):
        for g_ref, r_ref, o_ref in zip(refs[:n], refs[n:2 * n], refs[2 * n:]):
            o_ref[...] = (g_ref[...] + r_ref[...]).astype(BF16)

    def own(g):
        return pl.BlockSpec((None, None) + g.shape[2:], lambda k, cr: (k, cr[0], 0, 0))

    def blk(g):
        return pl.BlockSpec((None,) + g.shape[2:], lambda k, cr: (k, 0, 0))

    return pl.pallas_call(
        body, name=name, out_shape=[jax.ShapeDtypeStruct((4,) + g.shape[2:], BF16) for g in gs],
        grid_spec=pltpu.PrefetchScalarGridSpec(
            num_scalar_prefetch=1, grid=(4,),
            in_specs=[own(g) for g in gs] + [blk(g) for g in gs], out_specs=[blk(g) for g in gs]),
        compiler_params=_cp(("parallel",)),
    )(c_arr, *gs, *rs)


def _sum_chips(css, r3s, place_arr, name):
    n = len(css)

    def body(pl_ref, *refs):
        up = lambda ref: ref[...].astype(F32)
        for t in range(n):
            a_ref, (r0_ref, r1_ref, r2_ref), o_ref = refs[t], refs[n + 3 * t:n + 3 * t + 3], refs[4 * n + t]
            o_ref[...] = ((up(a_ref) + up(r0_ref)) + up(r1_ref)) + up(r2_ref)

    def blk(cs, first):
        _, rh, cols = cs.shape
        return pl.BlockSpec((None, rh // 2, cols), lambda i, pa: (first(pa), i, 0))

    in_specs = [blk(cs, lambda pa: pa[0]) for cs in css]
    for cs in css:
        in_specs += [blk(cs, lambda pa, d=d: d) for d in range(3)]
    return pl.pallas_call(
        body, name=name, out_shape=[jax.ShapeDtypeStruct((2,) + cs.shape[1:], F32) for cs in css],
        grid_spec=pltpu.PrefetchScalarGridSpec(
            num_scalar_prefetch=1, grid=(2,), in_specs=in_specs, out_specs=[blk(cs, lambda pa: pa[1]) for cs in css]),
        compiler_params=_cp(("parallel",)),
    )(place_arr, *css, *[r3 for r3 in r3s for _ in range(3)])


def _allreduce_small(pack):
    rows = pack.shape[0]
    hr = rows // 2

    def body(p_ref, o_ref, sib, slots, s1, r1, s2, r2, s3, r3):
        x, y, c, chip = _place()
        sibling = (x, y, 1 - c)
        ex = pltpu.make_async_remote_copy(src_ref=p_ref, dst_ref=sib, send_sem=s1, recv_sem=r1,
                                          device_id=sibling, device_id_type=MESH)
        ex.start()
        ex.wait()
        half = pl.ds(pl.multiple_of(c * hr, 16), hr)
        slots[0] = p_ref[half, :] + sib[half, :]
        cps = []
        for d in (1, 2, 3):
            px, py, _ = _chip_at(x, y, d)
            cps.append(pltpu.make_async_remote_copy(
                src_ref=slots.at[0], dst_ref=slots.at[d], send_sem=s2.at[d - 1], recv_sem=r2.at[d - 1],
                device_id=(px, py, c), device_id_type=MESH))
        for cp in cps:
            cp.start()
        for cp in cps:
            cp.wait()
        tot = slots[chip]
        for k in (1, 2, 3):
            tot = tot + slots[jnp.bitwise_xor(chip, k)]
        o_ref[half, :] = tot
        back = pltpu.make_async_remote_copy(src_ref=o_ref.at[half, :], dst_ref=o_ref.at[half, :], send_sem=s3, recv_sem=r3,
                                            device_id=sibling, device_id_type=MESH)
        back.start()
        back.wait()

    vm = pl.BlockSpec(memory_space=pltpu.VMEM)
    return pl.pallas_call(
        body, name="allreduce_small", out_shape=jax.ShapeDtypeStruct((rows, 128), F32),
        in_specs=[vm], out_specs=vm,
        scratch_shapes=[pltpu.VMEM((rows, 128), F32), pltpu.VMEM((4, hr, 128), F32),
                        pltpu.SemaphoreType.DMA, pltpu.SemaphoreType.DMA, pltpu.SemaphoreType.DMA((3,)), pltpu.SemaphoreType.DMA((3,)),
                        pltpu.SemaphoreType.DMA, pltpu.SemaphoreType.DMA],
        compiler_params=_cp(has_side_effects=True),
    )(pack)


def _small_chip_sum(pack, after=()):
    rows = pack.shape[0]
    hr = rows // 2

    def body(p_ref, *rest):
        o_ref, sib, s1, r1 = rest[-4:]
        x, y, c, _ = _place()
        ex = pltpu.make_async_remote_copy(src_ref=p_ref, dst_ref=sib, send_sem=s1, recv_sem=r1,
                                          device_id=(x, y, 1 - c), device_id_type=MESH)
        ex.start()
        ex.wait()
        half = pl.ds(pl.multiple_of(c * hr, 16), hr)
        o_ref[...] = (p_ref[half, :] + sib[half, :]).astype(BF16)

    vm = pl.BlockSpec(memory_space=pltpu.VMEM)
    return pl.pallas_call(
        body, name="small_chip_sum", out_shape=jax.ShapeDtypeStruct((hr, 128), BF16),
        in_specs=[vm] + [ANY] * len(after), out_specs=vm,
        scratch_shapes=[pltpu.VMEM((rows, 128), F32), pltpu.SemaphoreType.DMA, pltpu.SemaphoreType.DMA],
        compiler_params=_cp(has_side_effects=True),
    )(pack, *after)


def _small_copies(c_refs, land_refs, send, recv):
    x, y, c, _ = _place()
    cps = []
    for d in (1, 2, 3):
        px, py, _ = _chip_at(x, y, d)
        cps.append(pltpu.make_async_remote_copy(src_ref=c_refs[0], dst_ref=land_refs[0].at[d - 1], send_sem=send.at[d - 1],
                                                recv_sem=recv.at[d - 1], device_id=(px, py, c), device_id_type=MESH))
    return cps


def _small_total(csum, land):
    hr = csum.shape[0]

    def body(c_ref, l_ref, o_ref, slots, s3, r3):
        x, y, c, chip = _place()
        slots[0] = c_ref[...]
        for d in (1, 2, 3):
            slots[d] = l_ref[d - 1]
        tot = slots[chip].astype(F32)
        for k in (1, 2, 3):
            tot = tot + slots[jnp.bitwise_xor(chip, k)].astype(F32)
        half = pl.ds(pl.multiple_of(c * hr, 16), hr)
        o_ref[half, :] = tot
        back = pltpu.make_async_remote_copy(src_ref=o_ref.at[half, :], dst_ref=o_ref.at[half, :], send_sem=s3, recv_sem=r3,
                                            device_id=(x, y, 1 - c), device_id_type=MESH)
        back.start()
        back.wait()

    vm = pl.BlockSpec(memory_space=pltpu.VMEM)
    return pl.pallas_call(
        body, name="small_total", out_shape=jax.ShapeDtypeStruct((2 * hr, 128), F32), in_specs=[vm, vm], out_specs=vm,
        scratch_shapes=[pltpu.VMEM((4, hr, 128), BF16), pltpu.SemaphoreType.DMA, pltpu.SemaphoreType.DMA],
        compiler_params=_cp(has_side_effects=True),
    )(csum, land)


def _adam_math(gv, wv, mv, vv):
    m2 = ADAM_B1 * mv + (1.0 - ADAM_B1) * gv
    v2 = ADAM_B2 * vv + (1.0 - ADAM_B2) * (gv * gv)
    m_hat = m2 / (1.0 - ADAM_B1 ** ADAM_STEP)
    v_hat = v2 / (1.0 - ADAM_B2 ** ADAM_STEP)
    return -ADAM_LR * (m_hat / (jnp.sqrt(v_hat) + ADAM_EPS) + ADAM_WD * wv), m2, v2


def _adam(g, w, m, v, name):
    rows, cols = g.shape
    rb = rows // 4

    def body(g_ref, w_ref, m_ref, v_ref, d_ref, m2_ref, v2_ref):
        d_ref[...], m2_ref[...], v2_ref[...] = _adam_math(g_ref[...], w_ref[...], m_ref[...], v_ref[...])

    blk = pl.BlockSpec((rb, cols), lambda i: (i, 0))
    shp = jax.ShapeDtypeStruct((rows, cols), F32)
    return pl.pallas_call(
        body, name=name, grid=(4,), in_specs=[blk] * 4, out_specs=[blk] * 3, out_shape=[shp] * 3,
        compiler_params=_cp(("parallel",)),
    )(g, w, m, v)


def _adam_layer(gs, ws, ms, vs, l, prevs, name):
    n = len(gs)
    prev = [a for p4 in prevs if p4 is not None for a in p4]

    def body(*refs):
        outs = refs[len(refs) - 4 * n:]
        for t in range(n):
            g_ref, w_ref, m_ref, v_ref = refs[4 * t:4 * t + 4]
            go_ref, d_ref, m2_ref, v2_ref = outs[4 * t:4 * t + 4]
            gv = g_ref[...]
            go_ref[...] = gv
            d_ref[...], m2_ref[...], v2_ref[...] = _adam_math(gv, w_ref[...], m_ref[...], v_ref[...])

    in_specs, out_specs, out_shape, operands, aliases = [], [], [], [], {}
    for t, g in enumerate(gs):
        rows, cols = g.shape
        lay = pl.BlockSpec((None, rows // 4, cols), lambda i: (l, i, 0))
        in_specs += [pl.BlockSpec((rows // 4, cols), lambda i: (i, 0)), lay, lay, lay]
        operands += [g, ws[t], ms[t], vs[t]]
        out_specs += [lay] * 4
        out_shape += [jax.ShapeDtypeStruct((2, rows, cols), F32)] * 4
    k = 4 * n
    for t, p4 in enumerate(prevs):
        if p4 is not None:
            for j in range(4):
                aliases[k] = 4 * t + j
                k += 1
    outs = pl.pallas_call(
        body, name=name, grid=(4,), in_specs=in_specs + [ANY] * len(prev), out_specs=out_specs, out_shape=out_shape,
        input_output_aliases=aliases, compiler_params=_cp(("parallel",)),
    )(*operands, *prev)
    return [list(outs[4 * t:4 * t + 4]) for t in range(n)]


def _rows128(a):
    return a.reshape(-1, 128)


def _pack(arrs, mult):
    parts = [_rows128(a) for a in arrs]
    rows = sum(q.shape[0] for q in parts)
    pad = -rows % mult
    if pad:
        parts.append(jnp.zeros((pad, 128), F32))
    return jnp.concatenate(parts, axis=0)


def _unpack(pack, shapes):
    out, o = [], 0
    for s in shapes:
        n = 1
        for e in s:
            n *= e
        out.append(pack[o:o + n // 128].reshape(s))
        o += n // 128
    return out


WEIGHTS = ['norm1_g', 'w_in', 'gmlp_ln_g', 'gmlp_ln_b', 'gmlp_w_s', 'gmlp_b_s', 'conv_w', 'conv_b', 'lru_w_r', 'lru_b_r', 'lru_w_i',
           'lru_b_i', 'lru_lambda', 'w_out', 'norm2_g', 'w_ffn_in', 'w_ffn_out', 'final_g']
BIG = ['w_in', 'w_out', 'w_ffn_in', 'w_ffn_out']
SMALL = [n for n in WEIGHTS if n not in BIG]
CHIP_SHARDED_SMALL = ['conv_w', 'lru_b_r', 'lru_b_i', 'lru_lambda']


def kernel(x, norm1_g, w_in, gmlp_ln_g, gmlp_ln_b, gmlp_w_s, gmlp_b_s, conv_w, conv_b, lru_w_r, lru_b_r, lru_w_i, lru_b_i, lru_lambda, w_out, norm2_g, w_ffn_in, w_ffn_out, final_g, loss_target, m_norm1_g, m_w_in, m_gmlp_ln_g, m_gmlp_ln_b, m_gmlp_w_s, m_gmlp_b_s, m_conv_w, m_conv_b, m_lru_w_r, m_lru_b_r, m_lru_w_i, m_lru_b_i, m_lru_lambda, m_w_out, m_norm2_g, m_w_ffn_in, m_w_ffn_out, m_final_g, v_norm1_g, v_w_in, v_gmlp_ln_g, v_gmlp_ln_b, v_gmlp_w_s, v_gmlp_b_s, v_conv_w, v_conv_b, v_lru_w_r, v_lru_b_r, v_lru_w_i, v_lru_b_i, v_lru_lambda, v_w_out, v_norm2_g, v_w_ffn_in, v_w_ffn_out, v_final_g):
    a = dict(locals())
    w = {n: a[n] for n in WEIGHTS}
    mom = {n: a["m_" + n] for n in WEIGHTS}
    var = {n: a["v_" + n] for n in WEIGHTS}
    _, _, c, chip = _place()
    c_arr, chip_arr = jnp.reshape(c, (1,)).astype(jnp.int32), jnp.reshape(chip, (1,)).astype(jnp.int32)
    place_arr = jnp.stack([chip, c]).astype(jnp.int32)

    first, rest = BIG[:1], BIG[1:]

    def as_weights(names, full):
        wb = {n: f.reshape(4, 2 * f.shape[2], f.shape[3]) for n, f in zip(names, full)}
        if "w_out" in wb:
            wb["w_out"] = wb["w_out"].reshape(D, D)
            wb["w_ffn_out"] = wb["w_ffn_out"].reshape(DFF, D)
        return wb

    def cast(names, l, tag):
        return _cast_into([w[n] for n in names], l, chip_arr, f"cast_{tag}")

    def landed(fly, names, after, tag):
        return as_weights(names, _gather_pass_on(_gather_wait(fly[0], fly[1], fly[2], after, tag), tag))

    tiny = _pack([w[n] for n in CHIP_SHARDED_SMALL], 8)
    _, tiny_full = _gather_weights([], tiny)
    fly_in = _gather_start(cast(first, 0, "in"), "in", after=(tiny_full,))
    fly0 = _gather_start(cast(rest, 0, "0"), "0", after=(fly_in[3],))
    fly1 = _gather_start(cast(BIG, 1, "1"), "1", after=(fly0[3],))
    p = {n: w[n] for n in SMALL}
    parts = [_unpack(tiny_full[k], [w[n].shape for n in CHIP_SHARDED_SMALL]) for k in range(4)]
    for i, n in enumerate(CHIP_SHARDED_SMALL):
        p[n] = jnp.concatenate([parts[k][i] for k in range(4)], axis=-1)

    operands = [_layer_operands(l, p) for l in range(2)]
    state_packs = [_pack([src[n] for n in SMALL], 32) for src in (w, mom, var)]
    ahead = tuple(jax.tree.leaves(operands)) + tuple(state_packs)

    passing = {}

    def pass_on_1(gu):
        bufs = _gather_wait(fly1[0], fly1[1], fly1[2], gu, "1")
        passing[1] = _exchange_start(bufs, [], _pass_on_copies, 3 * len(bufs), "gather_pass_on_start_1")
        return (passing[1][-1],)

    def pass_on_0(ya):
        bufs = _gather_wait(fly0[0], fly0[1], fly0[2], ya, "0")
        passing[0] = _exchange_start(bufs, [], _pass_on_copies, 3 * len(bufs), "gather_pass_on_start_0")
        return (passing[0][-1],)

    def rest0(merged):
        send, recv, bufs, _, _ = passing[0]
        return as_weights(rest, _exchange_wait(send, recv, bufs, [], merged, _pass_on_copies, "gather_pass_on_wait_0")[0])

    xa, saved0 = _forward_layer(0, x[0], p, landed(fly_in, first, (fly1[3],) + ahead, "in"), after=(fly0[3], fly1[3]),
                                early=pass_on_0, rest=rest0, near_end=pass_on_1, operands=operands[0])
    send, recv, bufs1, _, _ = passing[1]
    xb, saved1 = _forward_layer(
        1, xa, p, as_weights(BIG, _exchange_wait(send, recv, bufs1, [], xa, _pass_on_copies, "gather_pass_on_wait_1")[0]),
        operands=operands[1], loss=(loss_target[0], p["final_g"][None]))
    dxb, loss_v, dfg = xb
    loss = lax.psum(loss_v[0, 0], ("x", "y", "c"))

    out, flying = {}, {}

    def halves(grads):
        return [g.reshape(4, 2, -1, g.shape[-1]) for g in grads]

    def sibling_start(grads, names, l, tag):
        gs = halves(grads)
        lands = [lax.empty((4,) + g.shape[2:], g.dtype) for g in gs]
        flying["s" + tag] = (names, l) + tuple(
            _exchange_start(gs, lands, _sibling_copies, 4 * len(gs), f"grads_to_sibling_start_{tag}"))
        return (flying["s" + tag][-1],)

    def chips_start(gs, from_sib, names, l, tag):
        cs = _add_half(gs, from_sib, c_arr, f"add_half_{tag}")
        lands = [lax.empty((3,) + a.shape[1:], a.dtype) for a in cs]
        flying[tag] = (names, l) + tuple(_exchange_start(cs, lands, _chips_copies, 3 * len(cs), f"grads_to_chips_start_{tag}"))
        return (flying[tag][-1],)

    def sibling_finish(tag, after):
        names, l, send, recv, gs, lands, _ = flying["s" + tag]
        gs, from_sib = _exchange_wait(send, recv, gs, lands, after, _sibling_copies, f"grads_to_sibling_wait_{tag}")
        return chips_start(gs, from_sib, names, l, tag)

    def reduce_sums(tags, after):
        groups, ts = [], []
        for tag in tags:
            names, l, send, recv, cs, lands, _ = flying[tag]
            cs, lands = _exchange_wait(send, recv, cs, lands, after, _chips_copies, f"grads_to_chips_wait_{tag}")
            ts += _sum_chips(cs, lands, place_arr, f"sum_chips_{tag}")
            groups.append((tag, names))
        flying["j" + tags[0]] = (groups, l) + tuple(_exchange_start(ts, [], _join_copies, len(ts), f"grads_join_start_{tags[0]}"))
        return (flying["j" + tags[0]][-1],)

    def reduce_adam(tag0, after):
        groups, l, send, recv, ts, _, _ = flying["j" + tag0]
        joined = _exchange_wait(send, recv, ts, [], after, _join_copies, f"grads_join_wait_{tag0}")[0]
        for tag, names in groups:
            gs, joined = [j.reshape(w[n].shape[1:]) for n, j in zip(names, joined)], joined[len(names):]
            res = _adam_layer(gs, [w[n] for n in names], [mom[n] for n in names], [var[n] for n in names], l,
                              [out.get(n) for n in names], f"adam_{tag}")
            out.update(zip(names, res))

    def late1(grads):
        return sibling_finish("1a", grads[0]) + sibling_start(grads, first, 1, "1b")

    def midway0(grads):
        return reduce_sums(("1a", "1b"), grads[0]) + sibling_start(grads, rest, 0, "0a")

    def stacked_small(small0):
        small = {k: jnp.stack([small0[k], small1[k]]) for k in LAYER_SMALL}
        return dict(small, final_g=dfg[0])

    def midway3_0(small0):
        small = stacked_small(dict(small0, norm1_g=jnp.zeros((D,), F32)))
        csum = _small_chip_sum(_pack([small[n] for n in SMALL], 32))
        flying["small"] = _exchange_start([csum], [lax.empty((3,) + csum.shape, BF16)], _small_copies, 3, "small_to_chips_start")
        return (flying["small"][-1],)

    def late0(grads):
        tok = sibling_start(grads, first, 0, "0b")
        reduce_adam("1a", tok[0])
        return sibling_finish("0b", out[first[0]][0])

    dxa, big1, small1 = _backward_layer(1, dxb, saved1, midway=lambda grads: sibling_start(grads, rest, 1, "1a"), late=late1)
    dx, big0, small0 = _backward_layer(0, dxa, saved0, after=sibling_finish("1b", dxa), midway=midway0,
                                       midway2=lambda dws: sibling_finish("0a", dws), midway3=midway3_0, late=late0)
    join_tok = reduce_sums(("0a", "0b"), dx)
    small = stacked_small(small0)

    full_shapes = [small[n].shape for n in SMALL]
    send, recv, csum, land, _ = flying["small"]
    csum, land = _exchange_wait(send, recv, csum, land, join_tok[0], _small_copies, "small_to_chips_wait")
    red = _unpack(_small_total(csum[0], land[0]), full_shapes)
    norm1_0 = _allreduce_small(_pack([small0["norm1_g"]], 32))[:D // 128].reshape(D)
    reduce_adam("0a", norm1_0)
    red[SMALL.index("norm1_g")] = red[SMALL.index("norm1_g")].at[0].set(norm1_0)
    g_small = []
    for n, g in zip(SMALL, red):
        if n in CHIP_SHARDED_SMALL:
            g = lax.dynamic_slice_in_dim(g, chip * w[n].shape[-1], w[n].shape[-1], axis=g.ndim - 1)
        g_small.append(g)
    shapes = [w[n].shape for n in SMALL]
    upd = [_unpack(u, shapes) for u in _adam(_pack(g_small, 32), *state_packs, "adam_small")]
    for i, n in enumerate(SMALL):
        out[n] = [g_small[i], upd[0][i], upd[1][i], upd[2][i]]

    return (loss, dx[None]) + tuple(out[n][i] for i in range(4) for n in WEIGHTS)
```

```python
import functools

import jax
import jax.numpy as jnp
from jax import lax
from jax.experimental import pallas as pl
from jax.experimental.pallas import tpu as pltpu

F32 = jnp.float32
BF16 = jnp.bfloat16
MESH = pl.DeviceIdType.MESH

D = 1024
NH = 8
HD = 128
CHUNK = 128
GMLP_ROWS = 512
N_IN_T = 12
DFF = 2816
DFF_SH = 1408
EPS = 1e-6
LRU_C = 8.0
ADAM_LR, ADAM_B1, ADAM_B2, ADAM_EPS, ADAM_WD, ADAM_STEP = 0.001, 0.9, 0.999, 1e-08, 0.01, 10

TM = 512
TM_BIG = 1024
RT = 128
PADR = 8
VMEM_LIMIT = 56 * 1024 * 1024


def _cp(sem=None, **kw):
    if sem is not None:
        kw["dimension_semantics"] = sem
    return pltpu.CompilerParams(vmem_limit_bytes=VMEM_LIMIT, **kw)


_GC = 0.7978845608028654


def _sigmoid(x):
    return 0.5 * jnp.tanh(0.5 * x) + 0.5


_GK = 0.044715


def _gelu(x):
    t = jnp.tanh(x * (_GC + (_GC * _GK) * (x * x)))
    return x * (0.5 + 0.5 * t)


def _gelu_and_grad(x):
    x2 = x * x
    t = jnp.tanh(x * (_GC + (_GC * _GK) * x2))
    h = 0.5 + 0.5 * t
    return x * h, h + x * (1.0 - t * t) * (0.5 * _GC + (1.5 * _GC * _GK) * x2)


def _softplus_neg(lam):
    y = jnp.exp(-jnp.abs(lam))
    u = 1.0 + y
    l1p = jnp.where(u == 1.0, y, jnp.log(u) * y / (u - 1.0))
    return jnp.maximum(-lam, 0.0) + l1p


def _dot(a, b):
    return jnp.dot(a, b, preferred_element_type=F32)


def _dot_nt(a, b):
    return lax.dot_general(a, b, (((1,), (1,)), ((), ())), preferred_element_type=F32)


def _dot_tn(a, b):
    return lax.dot_general(a, b, (((0,), (0,)), ((), ())), preferred_element_type=F32)


def _rms_hat(x):
    r = lax.rsqrt(jnp.mean(x * x, axis=-1, keepdims=True) + EPS)
    return x * r, r


def _rms_bwd(dh, x, g):
    xh, r = _rms_hat(x)
    dxh = dh * g
    dx = r * (dxh - xh * jnp.mean(dxh * xh, axis=-1, keepdims=True))
    return dx, jnp.sum(dh * xh, axis=0, keepdims=True)


def _norm_into(x_ref, g_ref, h_ref):
    xh, _ = _rms_hat(x_ref[...])
    h_ref[...] = (xh * g_ref[...]).astype(BF16)


def _in_tile(j):
    m, hf = j // 2, j % 2
    orig = jnp.where(m < 2, m, jnp.where(m == 2, 4, jnp.where(m < 5, m - 1, 5)))
    t = orig * 2 + hf
    return t // 3, t % 3


ANY = pl.BlockSpec(memory_space=pl.ANY)


def _mm_in(x, g, w_in, l, after=()):
    S = x.shape[0]
    tm = min(2 * TM_BIG, S)

    def body(x_ref, g_ref, w0_ref, w1_ref, *rest):
        o_ref, h_ref = rest[-2:]

        @pl.when(pl.program_id(1) == 0)
        def _():
            _norm_into(x_ref, g_ref, h_ref)
        rp = min(TM, tm)
        for r0 in range(0, tm, rp):
            hv = h_ref[r0:r0 + rp, :]
            o_ref[r0:r0 + rp, 0:512] = _dot(hv, w0_ref[...]).astype(BF16)
            o_ref[r0:r0 + rp, 512:1024] = _dot(hv, w1_ref[...]).astype(BF16)

    def w_tile(hf):
        def w_map(i, m):
            sh, tl = _in_tile(2 * m + hf)
            return (sh, 0, tl)
        return pl.BlockSpec((None, D, 512), w_map)

    return pl.pallas_call(
        body, name=f"mm_in_{l}", grid=(S // tm, 6),
        in_specs=[pl.BlockSpec((tm, D), lambda i, m: (i, 0)), pl.BlockSpec((1, D), lambda i, m: (0, 0)),
                  w_tile(0), w_tile(1)] + [ANY] * len(after),
        out_specs=[pl.BlockSpec((None, tm, D), lambda i, m: (m, i, 0)), pl.BlockSpec((tm, D), lambda i, m: (i, 0))],
        out_shape=[jax.ShapeDtypeStruct((6, S, D), BF16), jax.ShapeDtypeStruct((S, D), BF16)],
        compiler_params=_cp(("parallel", "arbitrary")),
    )(x, g, w_in, w_in, *after)


def _mm_res(a, w, res, l, name, after=()):
    S, K = a.shape

    tm = TM

    def body(a_ref, w_ref, r_ref, *rest):
        rest[-1][...] = r_ref[...] + _dot(a_ref[...], w_ref[...])

    return pl.pallas_call(
        body, name=f"{name}_{l}", grid=(S // tm,),
        in_specs=[pl.BlockSpec((tm, K), lambda i: (i, 0)), pl.BlockSpec((K, D), lambda i: (0, 0)),
                  pl.BlockSpec((tm, D), lambda i: (i, 0))] + [ANY] * len(after),
        out_specs=pl.BlockSpec((tm, D), lambda i: (i, 0)),
        out_shape=jax.ShapeDtypeStruct((S, D), F32),
        compiler_params=_cp(("parallel",)),
    )(a, w, res, *after)


def _mm_ffn_in(x, g, w_fi, l):
    S = x.shape[0]

    tm = min(TM_BIG, S)

    def body(x_ref, g_ref, w_ref, gu_ref, ff_ref, h_ref):
        @pl.when(pl.program_id(1) == 0)
        def _():
            _norm_into(x_ref, g_ref, h_ref)
        for r0 in range(0, tm, TM):
            rows = slice(r0, r0 + TM)
            hv = h_ref[rows, :]
            ga = _dot(hv, w_ref[0])
            gb = _dot(hv, w_ref[1])
            sg = _sigmoid(ga)
            silu = ga * sg
            gu_ref[0, rows, :] = (gb * (sg + silu * (1.0 - sg))).astype(BF16)
            gu_ref[1, rows, :] = silu.astype(BF16)
            ff_ref[rows, :] = (silu * gb).astype(BF16)

    gu, ff, h = pl.pallas_call(
        body, name=f"mm_ffn_in_{l}", grid=(S // tm, 2),
        in_specs=[pl.BlockSpec((tm, D), lambda i, s: (i, 0)), pl.BlockSpec((1, D), lambda i, s: (0, 0)),
                  pl.BlockSpec((2, None, D, DFF_SH), lambda i, s: (0, s, 0, 0))],
        out_specs=[pl.BlockSpec((2, None, tm, DFF_SH), lambda i, s: (0, s, i, 0)),
                   pl.BlockSpec((tm, DFF_SH), lambda i, s: (i, s)),
                   pl.BlockSpec((tm, D), lambda i, s: (i, 0))],
        out_shape=[jax.ShapeDtypeStruct((2, 2, S, DFF_SH), BF16), jax.ShapeDtypeStruct((S, DFF), BF16),
                   jax.ShapeDtypeStruct((S, D), BF16)],
        compiler_params=_cp(("parallel", "arbitrary")),
    )(x, g, w_fi.reshape(2, 2, D, DFF_SH))
    return gu.reshape(4, S, DFF_SH), ff, h


def _gmlp_fwd(z6, ws_b, bs_b, lg, lb):
    S = z6.shape[1]

    ts = min(GMLP_ROWS, S)

    def body(z_ref, ws_ref, bs_ref, lg_ref, lb_ref, o_ref, mix):
        for r0 in range(0, ts, CHUNK):
            rows = slice(r0, r0 + CHUNK)
            gv = _gelu(z_ref[1, rows, :].astype(F32))
            xc = gv - jnp.mean(gv, axis=-1, keepdims=True)
            rs = lax.rsqrt(jnp.mean(xc * xc, axis=-1, keepdims=True) + EPS)
            vb = (xc * rs * lg_ref[...] + lb_ref[...]).astype(BF16)
            for gi in range(NH):
                cs = slice(gi * HD, (gi + 1) * HD)
                mix[rows, cs] = _dot(ws_ref[gi], vb[:, cs])
            o_ref[rows, :] = (_sigmoid(z_ref[2, rows, :].astype(F32)) * _gelu(z_ref[0, rows, :].astype(F32))
                              * (mix[rows, :] + bs_ref[...])).astype(BF16)

    return pl.pallas_call(
        body, name="gmlp_fwd", grid=(S // ts,),
        in_specs=[pl.BlockSpec((3, ts, D), lambda i: (0, i, 0)), pl.BlockSpec((NH, CHUNK, CHUNK), lambda i: (0, 0, 0)),
                  pl.BlockSpec((CHUNK, D), lambda i: (0, 0)), pl.BlockSpec((1, D), lambda i: (0, 0)),
                  pl.BlockSpec((1, D), lambda i: (0, 0))],
        out_specs=pl.BlockSpec((ts, D), lambda i: (i, 0)),
        out_shape=jax.ShapeDtypeStruct((S, D), BF16),
        scratch_shapes=[pltpu.VMEM((ts, D), F32)],
        compiler_params=_cp(("parallel",)),
    )(z6, ws_b, bs_b, lg, lb)


def _row_iota():
    return lax.broadcasted_iota(jnp.int32, (RT, HD), 0)


SUB = 8
UNROLL = 8
GRAD_ROWS = 2048


def _scan_up(a, b, carry):
    row = lax.broadcasted_iota(jnp.int32, (SUB, HD), 0)
    masks = [(d, row >= d) for d in (1, 2, 4)]
    c = jnp.broadcast_to(carry, (SUB, HD))
    hs = []
    for j in range(RT // SUB):
        aj, bj = a[SUB * j:SUB * (j + 1)], b[SUB * j:SUB * (j + 1)]
        for d, m in masks:
            bj = bj + aj * jnp.where(m, pltpu.roll(bj, d, 0), 0.0)
            aj = aj * jnp.where(m, pltpu.roll(aj, d, 0), 1.0)
        h = bj + aj * c
        hs.append(h)
        c = jnp.broadcast_to(h[SUB - 1:SUB, :], (SUB, HD))
    return jnp.concatenate(hs, axis=0), hs[-1][SUB - 1:SUB, :]


def _scan_down(a, b, carry):
    row = lax.broadcasted_iota(jnp.int32, (SUB, HD), 0)
    masks = [(d, row < SUB - d) for d in (1, 2, 4)]
    c = jnp.broadcast_to(carry, (SUB, HD))
    hs = []
    for j in reversed(range(RT // SUB)):
        aj, bj = a[SUB * j:SUB * (j + 1)], b[SUB * j:SUB * (j + 1)]
        for d, m in masks:
            bj = bj + aj * jnp.where(m, pltpu.roll(bj, SUB - d, 0), 0.0)
            aj = aj * jnp.where(m, pltpu.roll(aj, SUB - d, 0), 1.0)
        h = bj + aj * c
        hs.append(h)
        c = jnp.broadcast_to(h[0:1, :], (SUB, HD))
    return jnp.concatenate(hs[::-1], axis=0), hs[-1][0:1, :]


def _decay(r, sp_d):
    log_a = -LRU_C * r * sp_d
    a = jnp.exp(log_a)
    return a, jnp.sqrt(jnp.maximum(-jnp.tanh(log_a) * (a * a + 1.0), 0.0))


def _decay_bwd(r, sp_d):
    log_a = -LRU_C * r * sp_d
    a = jnp.exp(log_a)
    m2 = jnp.maximum(-jnp.tanh(log_a) * (a * a + 1.0), 0.0)
    inv = jnp.where(m2 > 0.0, lax.rsqrt(m2), 0.0)
    return a, m2 * inv, inv


def _lru_gates(xc, d, wr_ref, br_ref, wi_ref, bi_ref, sp):
    xb = xc.astype(BF16)
    r = _sigmoid(_dot(xb, wr_ref[d]) + br_ref[d:d + 1, :])
    i = _sigmoid(_dot(xb, wi_ref[d]) + bi_ref[d:d + 1, :])
    a, mult = _decay(r, sp[d:d + 1, :])
    return r, i, a, mult


def _shifted(win, k):
    w = RT + 2 * PADR
    v = win if k == 0 else pltpu.roll(win, (-k) % w, 0)
    return v[PADR:PADR + RT]


def _conv_taps(win):
    return [_shifted(win, k) for k in (-1, 0, 1, 2)]


def _fill_padded(dst, src_ref, S):
    zeros = jnp.zeros((PADR, HD), F32)
    dst[0:PADR, :] = zeros
    dst[PADR + S:2 * PADR + S, :] = zeros

    def cp(i, c):
        t0 = pl.multiple_of(i * RT, RT)
        dst[pl.ds(t0 + PADR, RT), :] = src_ref[pl.ds(t0, RT), :].astype(F32)
        return c
    lax.fori_loop(0, S // RT, cp, 0)


def _conv_fwd_all(zxp, xc_s, cw_ref, cb_ref, S):
    def cv(i, c):
        t0 = pl.multiple_of(i * RT, RT)
        xm1, x0, xp1, xp2 = _conv_taps(zxp[pl.ds(t0, RT + 2 * PADR), :])
        xc_s[pl.ds(t0, RT), :] = (cb_ref[...] + xm1 * cw_ref[0:1, :] + x0 * cw_ref[1:2, :]
                                  + xp1 * cw_ref[2:3, :] + xp2 * cw_ref[3:4, :])
        return c
    lax.fori_loop(0, S // RT, cv, 0)


def _lru_specs(S):
    head = lambda h: (0, h)
    return [pl.BlockSpec((4, HD), head), pl.BlockSpec((1, HD), head),
            pl.BlockSpec((2, None, HD, HD), lambda h: (0, h, 0, 0)), pl.BlockSpec((2, HD), head),
            pl.BlockSpec((2, None, HD, HD), lambda h: (0, h, 0, 0)), pl.BlockSpec((2, HD), head),
            pl.BlockSpec((2, HD), head)]


def _lru_fwd(z6, ya, cw, cb, wr, br, wi, bi, lam, after=()):
    S = z6.shape[1]
    nt = S // RT

    def body(z_ref, ya_ref, cw_ref, cb_ref, wr_ref, br_ref, wi_ref, bi_ref, lam_ref, *rest):
        mg_ref, h0_ref, h1_ref, zxp, xc_s = rest[-5:]
        sp = _softplus_neg(lam_ref[...])
        _fill_padded(zxp, z_ref.at[0], S)
        _conv_fwd_all(zxp, xc_s, cw_ref, cb_ref, S)

        def scans(i, carry):
            cu, cd = carry
            for u in range(UNROLL):
                j = i * UNROLL + u
                ru = pl.ds(pl.multiple_of(j * RT, RT), RT)
                rd = pl.ds(pl.multiple_of((nt - 1 - j) * RT, RT), RT)
                xu, xd = xc_s[ru, :], xc_s[rd, :]
                _, gi, a, mult = _lru_gates(xu, 0, wr_ref, br_ref, wi_ref, bi_ref, sp)
                hu, cu = _scan_up(a, mult * gi * xu, cu)
                h0_ref[ru, :] = hu
                _, gi, a, mult = _lru_gates(xd, 1, wr_ref, br_ref, wi_ref, bi_ref, sp)
                hd, cd = _scan_down(a, mult * gi * xd, cd)
                h1_ref[rd, :] = hd
            return cu, cd
        z1 = jnp.zeros((1, HD), F32)
        lax.fori_loop(0, nt // UNROLL, scans, (z1, z1))

        def merge(i, c):
            rows = pl.ds(pl.multiple_of(i * RT, RT), RT)
            yb = (h0_ref[rows, :] + h1_ref[rows, :]) * _gelu(z_ref[1, rows, :].astype(F32))
            mg_ref[rows, :] = (ya_ref[rows, :].astype(F32) + _sigmoid(z_ref[2, rows, :].astype(F32)) * yb).astype(BF16)
            return c
        lax.fori_loop(0, nt, merge, 0)

    col = pl.BlockSpec((S, HD), lambda h: (0, h))
    return pl.pallas_call(
        body, name="lru_fwd", grid=(NH,),
        in_specs=[pl.BlockSpec((3, S, HD), lambda h: (1, 0, h)), col] + _lru_specs(S) + [ANY] * len(after),
        out_specs=[col, col, col],
        out_shape=[jax.ShapeDtypeStruct((S, D), BF16), jax.ShapeDtypeStruct((S, D), F32), jax.ShapeDtypeStruct((S, D), F32)],
        scratch_shapes=[pltpu.VMEM((S + 2 * PADR, HD), F32), pltpu.VMEM((S, HD), F32)],
        compiler_params=_cp(("parallel",)),
    )(z6, ya, cw, cb, wr, br, wi, bi, lam, *after)


def _mm_res_loss(a, w, res, tgt, g):
    S, K = a.shape

    def body(a_ref, w_ref, r_ref, t_ref, g_ref, dx_ref, loss_ref, dg_ref):
        @pl.when(pl.program_id(0) == 0)
        def _():
            loss_ref[...] = jnp.zeros_like(loss_ref)
            dg_ref[...] = jnp.zeros_like(dg_ref)
        xv = r_ref[...] + _dot(a_ref[...], w_ref[...])
        xh, _ = _rms_hat(xv)
        e = xh * g_ref[...] - t_ref[...]
        loss_ref[...] += jnp.sum(e * e) * (0.5 / D)
        dx, dgs = _rms_bwd(e * (1.0 / D), xv, g_ref[...])
        dx_ref[...] = dx
        dg_ref[...] += dgs

    row = pl.BlockSpec((TM, D), lambda i: (i, 0))
    vec = pl.BlockSpec((1, D), lambda i: (0, 0))
    return pl.pallas_call(
        body, name="mm_ffn_out_loss", grid=(S // TM,),
        in_specs=[pl.BlockSpec((TM, K), lambda i: (i, 0)), pl.BlockSpec((K, D), lambda i: (0, 0)), row, row, vec],
        out_specs=[row, pl.BlockSpec((1, 128), lambda i: (0, 0)), vec],
        out_shape=[jax.ShapeDtypeStruct((S, D), F32), jax.ShapeDtypeStruct((1, 128), F32), jax.ShapeDtypeStruct((1, D), F32)],
        compiler_params=_cp(("arbitrary",)),
    )(a, w, res, tgt, g)


def _bwd_ffn_out(dx, w_fo, gu, l, after=()):
    S = dx.shape[0]

    tm = min(TM_BIG, S)

    def body(dx_ref, w_ref, gu_ref, *rest):
        o_ref = rest[-1]
        for r0 in range(0, tm, TM):
            rows = slice(r0, r0 + TM)
            d = _dot_nt(dx_ref[rows, :].astype(BF16), w_ref[...])
            o_ref[0, rows, :] = (d * gu_ref[0, rows, :].astype(F32)).astype(BF16)
            o_ref[1, rows, :] = (d * gu_ref[1, rows, :].astype(F32)).astype(BF16)

    pair = pl.BlockSpec((2, None, tm, DFF_SH), lambda i, s: (0, s, i, 0))
    dgu = pl.pallas_call(
        body, name=f"bwd_ffn_out_{l}", grid=(S // tm, 2),
        in_specs=[pl.BlockSpec((tm, D), lambda i, s: (i, 0)), pl.BlockSpec((DFF_SH, D), lambda i, s: (s, 0)), pair]
        + [ANY] * len(after),
        out_specs=pair,
        out_shape=jax.ShapeDtypeStruct((2, 2, S, DFF_SH), BF16),
        compiler_params=_cp(("parallel", "arbitrary")),
    )(dx, w_fo, gu.reshape(2, 2, S, DFF_SH), *after)
    return dgu.reshape(4, S, DFF_SH)


def _mm_tn(a, b, m_blk, tk, name):
    S, M = a.shape

    def body(a_ref, b_ref, o_ref):
        @pl.when(pl.program_id(1) == 0)
        def _():
            o_ref[...] = jnp.zeros_like(o_ref)
        o_ref[...] += _dot_tn(a_ref[...], b_ref[...].astype(BF16))

    return pl.pallas_call(
        body, name=name, grid=(M // m_blk, S // tk),
        in_specs=[pl.BlockSpec((tk, m_blk), lambda m, k: (k, m)), pl.BlockSpec((tk, D), lambda m, k: (k, 0))],
        out_specs=pl.BlockSpec((m_blk, D), lambda m, k: (m, 0)),
        out_shape=jax.ShapeDtypeStruct((M, D), F32),
        compiler_params=_cp(("parallel", "arbitrary")),
    )(a, b)


def _mm_nt_rms_bwd(a, a_specs, w, w_specs, nk, tm, x, g, dres, name, after=()):
    S = x.shape[0]
    sub = len(a_specs)

    def body(*refs):
        a_refs, w_refs = refs[:sub], refs[sub:2 * sub]
        x_ref, g_ref, r_ref = refs[2 * sub:2 * sub + 3]
        dx_ref, dg_ref, acc = refs[-3:]
        i, k = pl.program_id(0), pl.program_id(1)
        @pl.when(k == 0)
        def _():
            acc[...] = jnp.zeros_like(acc)
        for j in range(sub):
            acc[...] += _dot_nt(a_refs[j][...], w_refs[j][...])

        @pl.when(jnp.logical_and(i == 0, k == 0))
        def _():
            dg_ref[...] = jnp.zeros_like(dg_ref)

        @pl.when(k == nk - 1)
        def _():
            dx, dgs = _rms_bwd(acc[...], x_ref[...], g_ref[...])
            dx_ref[...] = r_ref[...] + dx
            dg_ref[...] += dgs

    row = pl.BlockSpec((tm, D), lambda i, k: (i, 0))
    vec = pl.BlockSpec((1, D), lambda i, k: (0, 0))
    return pl.pallas_call(
        body, name=name, grid=(S // tm, nk),
        in_specs=list(a_specs) + list(w_specs) + [row, vec, row] + [ANY] * len(after),
        out_specs=[row, vec],
        out_shape=[jax.ShapeDtypeStruct((S, D), F32), jax.ShapeDtypeStruct((1, D), F32)],
        scratch_shapes=[pltpu.VMEM((tm, D), F32)],
        compiler_params=_cp(("arbitrary", "arbitrary")),
    )(*[a] * sub, *[w] * sub, x, g, dres, *after)


def _dw_ffn_in(h, dgu, l):
    S = h.shape[0]

    def body(h_ref, b_ref, o_ref):
        @pl.when(pl.program_id(1) == 0)
        def _():
            o_ref[...] = jnp.zeros_like(o_ref)
        o_ref[...] += _dot_tn(h_ref[...], b_ref[...])

    tk = min(2 * TM_BIG, S)
    return pl.pallas_call(
        body, name=f"dw_ffn_in_{l}", grid=(4, S // tk),
        in_specs=[pl.BlockSpec((tk, D), lambda j, k: (k, 0)), pl.BlockSpec((None, tk, DFF_SH), lambda j, k: (j, k, 0))],
        out_specs=pl.BlockSpec((None, D, DFF_SH), lambda j, k: (j, 0, 0)),
        out_shape=jax.ShapeDtypeStruct((4, D, DFF_SH), F32),
        compiler_params=_cp(("parallel", "arbitrary")),
    )(h, dgu)


_HALF_COMPS = ((0, 1, 3), (4, 2, 5))


def _dw_in(h, dz6, l, after=()):
    S = h.shape[0]

    def body(h_ref, d0_ref, d1_ref, d2_ref, *rest):
        o_ref = rest[-1]

        @pl.when(pl.program_id(1) == 0)
        def _():
            o_ref[...] = jnp.zeros_like(o_ref)
        hv = h_ref[...]
        for q, d_ref in enumerate((d0_ref, d1_ref, d2_ref)):
            for hf in range(2):
                col = 1024 * q + 512 * hf
                o_ref[col // 1536, :, col % 1536:col % 1536 + 512] += _dot_tn(hv, d_ref[:, 512 * hf:512 * (hf + 1)])

    tk = min(TM_BIG, S)

    def comp(q):
        return pl.BlockSpec((None, tk, D), lambda p, k: (jnp.where(p == 0, _HALF_COMPS[0][q], _HALF_COMPS[1][q]), k, 0))

    return pl.pallas_call(
        body, name=f"dw_in_{l}", grid=(2, S // tk),
        in_specs=[pl.BlockSpec((tk, D), lambda p, k: (k, 0)), comp(0), comp(1), comp(2)] + [ANY] * len(after),
        out_specs=pl.BlockSpec((2, D, 1536), lambda p, k: (p, 0, 0)),
        out_shape=jax.ShapeDtypeStruct((4, D, 1536), F32),
        compiler_params=_cp(("parallel", "arbitrary")),
    )(h, dz6, dz6, dz6, *after)


def _bwd_out(dx, w_o, merged, l):
    S = dx.shape[0]

    def body(dx_ref, w_ref, m_ref, dm_ref, dw_ref):
        @pl.when(pl.program_id(0) == 0)
        def _():
            dw_ref[...] = jnp.zeros_like(dw_ref)
        dxb = dx_ref[...].astype(BF16)
        dm_ref[...] = _dot_nt(dxb, w_ref[...]).astype(BF16)
        dw_ref[...] += _dot_tn(m_ref[...], dxb)

    tm = TM
    row = pl.BlockSpec((tm, D), lambda i: (i, 0))
    return pl.pallas_call(
        body, name=f"bwd_out_{l}", grid=(S // tm,),
        in_specs=[row, pl.BlockSpec((D, D), lambda i: (0, 0)), row],
        out_specs=[row, pl.BlockSpec((D, D), lambda i: (0, 0))],
        out_shape=[jax.ShapeDtypeStruct((S, D), BF16), jax.ShapeDtypeStruct((D, D), F32)],
        compiler_params=_cp(("arbitrary",)),
    )(dx, w_o, merged)


def _gmlp_bwd(dm, z6, ws_b, wst_b, bs_b, lg, lb, after=()):
    S = z6.shape[1]
    ts = min(GMLP_ROWS, S)

    def body(dm_ref, z_ref, ws_ref, wst_ref, bs_ref, lg_ref, lb_ref, *rest):
        dz_ref, dws_ref, dbs_ref, dlg_ref, dlb_ref, mix, dv = rest[-7:]

        @pl.when(pl.program_id(0) == 0)
        def _():
            dws_ref[...] = jnp.zeros_like(dws_ref)
            dbs_ref[...] = jnp.zeros_like(dbs_ref)
            dlg_ref[...] = jnp.zeros_like(dlg_ref)
            dlb_ref[...] = jnp.zeros_like(dlb_ref)
        for r0 in range(0, ts, CHUNK):
            rows = slice(r0, r0 + CHUNK)
            gv, dgelu_v = _gelu_and_grad(z_ref[1, rows, :].astype(F32))
            xc = gv - jnp.mean(gv, axis=-1, keepdims=True)
            rs = lax.rsqrt(jnp.mean(xc * xc, axis=-1, keepdims=True) + EPS)
            vh = xc * rs
            vb = (vh * lg_ref[...] + lb_ref[...]).astype(BF16)
            for gi in range(NH):
                cs = slice(gi * HD, (gi + 1) * HD)
                mix[rows, cs] = _dot(ws_ref[gi], vb[:, cs])
            u, dgelu_u = _gelu_and_grad(z_ref[0, rows, :].astype(F32))
            sa = _sigmoid(z_ref[2, rows, :].astype(F32))
            dya = dm_ref[rows, :].astype(F32) * sa
            dym = dya * (mix[rows, :] + bs_ref[...])
            dz_ref[2, rows, :] = (dym * u * (1.0 - sa)).astype(BF16)
            dz_ref[0, rows, :] = (dym * dgelu_u).astype(BF16)
            dmix = dya * u
            dmb = dmix.astype(BF16)
            for gi in range(NH):
                cs = slice(gi * HD, (gi + 1) * HD)
                dv[rows, cs] = _dot(wst_ref[gi], dmb[:, cs])
                dws_ref[gi] += _dot_nt(dmb[:, cs], vb[:, cs])
                dbs_ref[gi] += jnp.broadcast_to(jnp.sum(dmix[:, cs], axis=1, keepdims=True), (CHUNK, HD))
            dvv = dv[rows, :]
            dlg_ref[...] += jnp.sum(dvv * vh, axis=0, keepdims=True)
            dlb_ref[...] += jnp.sum(dvv, axis=0, keepdims=True)
            dvh = dvv * lg_ref[...]
            dgv = rs * (dvh - jnp.mean(dvh, axis=-1, keepdims=True) - vh * jnp.mean(dvh * vh, axis=-1, keepdims=True))
            dz_ref[1, rows, :] = (dgv * dgelu_v).astype(BF16)

    vec = pl.BlockSpec((1, D), lambda i: (0, 0))
    mat = pl.BlockSpec((NH, CHUNK, CHUNK), lambda i: (0, 0, 0))
    return pl.pallas_call(
        body, name="gmlp_bwd", grid=(S // ts,),
        in_specs=[pl.BlockSpec((ts, D), lambda i: (i, 0)), pl.BlockSpec((3, ts, D), lambda i: (0, i, 0)), mat, mat,
                  pl.BlockSpec((CHUNK, D), lambda i: (0, 0)), vec, vec] + [ANY] * len(after),
        out_specs=[pl.BlockSpec((3, ts, D), lambda i: (0, i, 0)), mat, mat, vec, vec],
        out_shape=[jax.ShapeDtypeStruct((6, S, D), BF16), jax.ShapeDtypeStruct((NH, CHUNK, CHUNK), F32),
                   jax.ShapeDtypeStruct((NH, CHUNK, HD), F32), jax.ShapeDtypeStruct((1, D), F32), jax.ShapeDtypeStruct((1, D), F32)],
        scratch_shapes=[pltpu.VMEM((ts, D), F32), pltpu.VMEM((ts, D), F32)],
        compiler_params=_cp(("arbitrary",)),
    )(dm, z6, ws_b, wst_b, bs_b, lg, lb, *after)


def _lru_bwd(dz6, dm, z6, h0, h1, cw, cb, wr, br, wi, bi, lam, after=()):
    S = z6.shape[1]
    nt = S // RT

    def body(dz_in, dm_ref, z_ref, h0_ref, h1_ref, cw_ref, cb_ref, wr_ref, br_ref, wi_ref, bi_ref, lam_ref, *rest):
        dz_ref, dcw_ref, dcb_ref, dwr_ref, dbr_ref, dwi_ref, dbi_ref, dlam_ref, zxp, xc_s, dhs_s, dxcp, r_s, lam_s = rest[-14:]
        del dz_in
        lam = lam_ref[...]
        sp = _softplus_neg(lam)
        row = _row_iota()
        _fill_padded(zxp, z_ref.at[0], S)
        _conv_fwd_all(zxp, xc_s, cw_ref, cb_ref, S)
        zeros = jnp.zeros((PADR, HD), F32)
        dxcp[0:PADR, :] = zeros
        dxcp[PADR + S:2 * PADR + S, :] = zeros
        dwr_ref[...] = jnp.zeros_like(dwr_ref)
        dwi_ref[...] = jnp.zeros_like(dwi_ref)

        def pre(i, c):
            rows = pl.ds(pl.multiple_of(i * RT, RT), RT)
            hs = h0_ref[rows, :] + h1_ref[rows, :]
            dmv = dm_ref[rows, :].astype(F32)
            sb = _sigmoid(z_ref[2, rows, :].astype(F32))
            gg, dgg = _gelu_and_grad(z_ref[1, rows, :].astype(F32))
            dz_ref[2, rows, :] = (dmv * hs * gg * sb * (1.0 - sb)).astype(BF16)
            dyb = dmv * sb
            dz_ref[1, rows, :] = (dyb * hs * dgg).astype(BF16)
            dhs_s[rows, :] = dyb * gg
            return c
        lax.fori_loop(0, nt, pre, 0)

        def gate_bwd(d, gates, lamv, da, xc):
            r, gi, a, mult, inv_mult = gates
            lx, lm = lamv * xc, lamv * mult
            dlog_r = (da - (lx * gi) * (a * inv_mult)) * a * r
            dpr = dlog_r * (1.0 - r) * (-LRU_C * sp[d:d + 1, :])
            dpi = (lx * mult) * gi * (1.0 - gi)
            xb, dprb, dpib = xc.astype(BF16), dpr.astype(BF16), dpi.astype(BF16)
            dwr_ref[d] += _dot_tn(xb, dprb)
            dwi_ref[d] += _dot_tn(xb, dpib)
            dxc = lm * gi + _dot_nt(dprb, wr_ref[d]) + _dot_nt(dpib, wi_ref[d])
            return dxc, (jnp.sum(dlog_r, axis=0, keepdims=True) * (-LRU_C), jnp.sum(dpr, axis=0, keepdims=True),
                         jnp.sum(dpi, axis=0, keepdims=True))

        def rgates(i, c):
            for u in range(UNROLL):
                rows = pl.ds(pl.multiple_of((i * UNROLL + u) * RT, RT), RT)
                xb = xc_s[rows, :].astype(BF16)
                for d in range(2):
                    r_s[d, rows, :] = _sigmoid(_dot(xb, wr_ref[d]) + br_ref[d:d + 1, :])
            return c
        lax.fori_loop(0, nt // UNROLL, rgates, 0)

        def chains(i, carry):
            qn, qp = carry
            for u in range(UNROLL):
                j = i * UNROLL + u
                rd = pl.ds(pl.multiple_of((nt - 1 - j) * RT, RT), RT)
                a, dhs = _decay(r_s[0, rd, :], sp[0:1, :])[0], dhs_s[rd, :]
                q, q_first = _scan_down(a, a * dhs, qn)
                lam_s[0, rd, :] = dhs + jnp.where(row == RT - 1, qn, pltpu.roll(q, RT - 1, 0))
                qn = q_first
                ru = pl.ds(pl.multiple_of(j * RT, RT), RT)
                a, dhs = _decay(r_s[1, ru, :], sp[1:2, :])[0], dhs_s[ru, :]
                q, q_last = _scan_up(a, a * dhs, qp)
                lam_s[1, ru, :] = dhs + jnp.where(row == 0, qp, pltpu.roll(q, 1, 0))
                qp = q_last
            return qn, qp

        z1 = jnp.zeros((1, HD), F32)
        lax.fori_loop(0, nt // UNROLL, chains, (z1, z1))

        ct = min(GRAD_ROWS, S)
        crow = lax.broadcasted_iota(jnp.int32, (ct, HD), 0)

        def tile_grads(i, acc):
            t0 = pl.multiple_of(i * ct, ct)
            rows = pl.ds(t0, ct)
            xc = xc_s[rows, :]
            xb = xc.astype(BF16)
            tp = pl.multiple_of(jnp.maximum(t0 - PADR, 0), PADR)
            prev = jnp.where(t0 > 0, h0_ref[pl.ds(tp, PADR), :][PADR - 1:PADR, :], 0.0)
            tn = pl.multiple_of(jnp.minimum(t0 + ct, S - PADR), PADR)
            nxt = jnp.where(t0 + ct < S, h1_ref[pl.ds(tn, PADR), :][0:1, :], 0.0)
            hside = (jnp.where(crow == 0, prev, pltpu.roll(h0_ref[rows, :], 1, 0)),
                     jnp.where(crow == ct - 1, nxt, pltpu.roll(h1_ref[rows, :], ct - 1, 0)))
            dxc, sums = 0.0, ()
            for d in range(2):
                r = r_s[d, rows, :]
                gi = _sigmoid(_dot(xb, wi_ref[d]) + bi_ref[d:d + 1, :])
                lamv = lam_s[d, rows, :]
                dxc_d, s_d = gate_bwd(d, (r, gi) + _decay_bwd(r, sp[d:d + 1, :]), lamv, lamv * hside[d], xc)
                dxc = dxc + dxc_d
                sums = sums + s_d
            dxcp[pl.ds(t0 + PADR, ct), :] = dxc
            return tuple(x + y for x, y in zip(acc, sums))

        s_sp0, s_br0, s_bi0, s_sp1, s_br1, s_bi1 = lax.fori_loop(0, S // ct, tile_grads, (z1,) * 6)

        dsp = jnp.concatenate([s_sp0, s_sp1], axis=0)
        dlam_ref[...] = -dsp * _sigmoid(-lam)
        dbr_ref[...] = jnp.concatenate([s_br0, s_br1], axis=0)
        dbi_ref[...] = jnp.concatenate([s_bi0, s_bi1], axis=0)

        def conv_bwd(i, carry):
            c0, c1, c2, c3, cb_ = carry
            t0 = pl.multiple_of(i * RT, RT)
            dwin = dxcp[pl.ds(t0, RT + 2 * PADR), :]
            d0 = _shifted(dwin, 0)
            dz_ref[0, pl.ds(t0, RT), :] = (_shifted(dwin, 1) * cw_ref[0:1, :] + d0 * cw_ref[1:2, :]
                                           + _shifted(dwin, -1) * cw_ref[2:3, :] + _shifted(dwin, -2) * cw_ref[3:4, :]).astype(BF16)
            xm1, x0, xp1, xp2 = _conv_taps(zxp[pl.ds(t0, RT + 2 * PADR), :])
            sm = lambda v: jnp.sum(v, axis=0, keepdims=True)
            return c0 + sm(d0 * xm1), c1 + sm(d0 * x0), c2 + sm(d0 * xp1), c3 + sm(d0 * xp2), cb_ + sm(d0)

        c0, c1, c2, c3, cb_ = lax.fori_loop(0, nt, conv_bwd, (z1, z1, z1, z1, z1))
        dcw_ref[...] = jnp.concatenate([c0, c1, c2, c3], axis=0)
        dcb_ref[...] = cb_

    col = pl.BlockSpec((S, HD), lambda h: (0, h))
    head = lambda h: (0, h)
    wspec = pl.BlockSpec((2, None, HD, HD), lambda h: (0, h, 0, 0))
    return pl.pallas_call(
        body, name="lru_bwd", grid=(NH,),
        in_specs=[pl.BlockSpec(memory_space=pl.ANY), col, pl.BlockSpec((3, S, HD), lambda h: (1, 0, h)), col, col] + _lru_specs(S)
        + [ANY] * len(after),
        out_specs=[pl.BlockSpec((3, S, HD), lambda h: (1, 0, h)), pl.BlockSpec((4, HD), head), pl.BlockSpec((1, HD), head),
                   wspec, pl.BlockSpec((2, HD), head), wspec, pl.BlockSpec((2, HD), head), pl.BlockSpec((2, HD), head)],
        out_shape=[jax.ShapeDtypeStruct((6, S, D), BF16), jax.ShapeDtypeStruct((4, D), F32), jax.ShapeDtypeStruct((1, D), F32),
                   jax.ShapeDtypeStruct((2, NH, HD, HD), F32), jax.ShapeDtypeStruct((2, D), F32),
                   jax.ShapeDtypeStruct((2, NH, HD, HD), F32), jax.ShapeDtypeStruct((2, D), F32), jax.ShapeDtypeStruct((2, D), F32)],
        scratch_shapes=[pltpu.VMEM((S + 2 * PADR, HD), F32), pltpu.VMEM((S, HD), F32), pltpu.VMEM((S, HD), F32),
                        pltpu.VMEM((S + 2 * PADR, HD), F32), pltpu.VMEM((2, S, HD), F32), pltpu.VMEM((2, S, HD), F32)],
        input_output_aliases={0: 0},
        compiler_params=_cp(("parallel",)),
    )(dz6, dm, z6, h0, h1, cw, cb, wr, br, wi, bi, lam, *after)


LAYER_SMALL = ("norm1_g", "gmlp_ln_g", "gmlp_ln_b", "gmlp_w_s", "gmlp_b_s", "conv_w", "conv_b",
               "lru_w_r", "lru_b_r", "lru_w_i", "lru_b_i", "lru_lambda", "norm2_g")


def _layer_operands(l, p):
    ws_b = p["gmlp_w_s"][l].astype(BF16)
    tm = dict(ws_b=ws_b, wst_b=jnp.swapaxes(ws_b, 1, 2), bs_b=jnp.repeat(p["gmlp_b_s"][l].T, HD, axis=1),
              lg=p["gmlp_ln_g"][l][None], lb=p["gmlp_ln_b"][l][None])
    lru = (p["conv_w"][l], p["conv_b"][l][None], p["lru_w_r"][l].astype(BF16), p["lru_b_r"][l],
           p["lru_w_i"][l].astype(BF16), p["lru_b_i"][l], p["lru_lambda"][l])
    return (p["norm1_g"][l][None], p["norm2_g"][l][None]), tm, lru


def _forward_layer(l, x, p, wb, after=(), early=None, rest=None, near_end=None, operands=None, loss=None):
    (g1, g2), tm, lru = _layer_operands(l, p) if operands is None else operands
    z6, hn1 = _mm_in(x, g1, wb["w_in"], l, after)
    ya = _gmlp_fwd(z6, tm["ws_b"], tm["bs_b"], tm["lg"], tm["lb"])
    merged, h0, h1 = _lru_fwd(z6, ya, *lru, after=() if early is None else tuple(early(ya)))
    if rest is not None:
        wb = dict(wb, **rest(merged))
    x1 = _mm_res(merged, wb["w_out"], x, l, "mm_out")
    gu, ff, hn2 = _mm_ffn_in(x1, g2, wb["w_ffn_in"], l)
    if loss is None:
        x2 = _mm_res(ff, wb["w_ffn_out"], x1, l, "mm_ffn_out", () if near_end is None else tuple(near_end(gu)))
    else:
        x2 = _mm_res_loss(ff, wb["w_ffn_out"], x1, *loss)
    return x2, dict(x=x, z6=z6, h0=h0, h1=h1, merged=merged, x1=x1, gu=gu, ff=ff, g1=g1, g2=g2, tm=tm, lru=lru,
                    hn1=hn1, hn2=hn2, wb=wb)


def _backward_layer(l, dx, s, after=(), midway=None, midway2=None, midway3=None, late=None):
    S = dx.shape[0]
    tm, wb = s["tm"], s["wb"]
    g2 = s["g2"]
    dgu = _bwd_ffn_out(dx, wb["w_ffn_out"], s["gu"], l, after)
    tmb = min(TM_BIG, S)
    dwfo = _mm_tn(s["ff"], dx, DFF_SH, tmb, f"dw_ffn_out_{l}")
    dx1, dg2 = _mm_nt_rms_bwd(
        dgu, [pl.BlockSpec((None, tmb, DFF_SH), lambda i, k: (k, i, 0))],
        wb["w_ffn_in"], [pl.BlockSpec((None, D, DFF_SH), lambda i, k: (k, 0, 0))],
        4, tmb, s["x1"], g2, dx, f"bwd_ffn_in_{l}")
    dwfi = _dw_ffn_in(s["hn2"], dgu, l)
    dmg, dwo = _bwd_out(dx1, wb["w_out"], s["merged"], l)
    mid = () if midway is None else tuple(midway([dwo, dwfi, dwfo]))
    dz6, dws, dbs, dlg, dlb = _gmlp_bwd(dmg, s["z6"], tm["ws_b"], tm["wst_b"], tm["bs_b"], tm["lg"], tm["lb"], mid)
    mid2 = () if midway2 is None else tuple(midway2(dws))
    dz6, dcw, dcb, dwr, dbr, dwi, dbi, dlam = _lru_bwd(dz6, dmg, s["z6"], s["h0"], s["h1"], *s["lru"], after=mid2)

    sub = 3

    def dz_tile(j):
        return pl.BlockSpec((None, tmb, 512), lambda i, k: ((sub * k + j) // 2, i, (sub * k + j) % 2))

    def w_tile(j):
        def w_map(i, k):
            sh, tl = _in_tile(sub * k + j)
            return (sh, 0, tl)
        return pl.BlockSpec((None, D, 512), w_map)

    small = dict(gmlp_ln_g=dlg[0], gmlp_ln_b=dlb[0], gmlp_w_s=dws, gmlp_b_s=dbs[:, :, 0], conv_w=dcw, conv_b=dcb[0],
                 lru_w_r=dwr, lru_b_r=dbr, lru_w_i=dwi, lru_b_i=dbi, lru_lambda=dlam, norm2_g=dg2[0])
    mid3 = () if midway3 is None else tuple(midway3(small))
    dwin = _dw_in(s["hn1"], dz6, l, mid3)
    tail = () if late is None else tuple(late([dwin]))
    dx0, dg1 = _mm_nt_rms_bwd(
        dz6, [dz_tile(j) for j in range(sub)], wb["w_in"], [w_tile(j) for j in range(sub)],
        N_IN_T // sub, tmb, s["x"], s["g1"], dx1, f"bwd_in_{l}", tail)
    return dx0, [dwin, dwo, dwfi, dwfo], dict(small, norm1_g=dg1[0])


def _local_step(x, tgt, p, wbs):
    saved = []
    for l in range(2):
        x, s = _forward_layer(l, x, p, wbs[l], loss=(tgt, p["final_g"][None]) if l else None)
        saved.append(s)
    dx, loss_v, dfg = x
    big, smalls = [None, None], [None, None]
    for l in (1, 0):
        dx, big[l], smalls[l] = _backward_layer(l, dx, saved[l])
    small = {k: jnp.stack([smalls[0][k], smalls[1][k]]) for k in LAYER_SMALL}
    small["final_g"] = dfg[0]
    return loss_v, dx, big, small


def _place():
    x, y, c = lax.axis_index("x"), lax.axis_index("y"), lax.axis_index("c")
    return x, y, c, 2 * x + y


def _chip_at(x, y, d):
    px = 1 - x if d & 2 else x
    py = 1 - y if d & 1 else y
    return px, py, 2 * px + py


HBM = pl.BlockSpec(memory_space=pltpu.HBM)
SEM = pl.BlockSpec(memory_space=pltpu.SEMAPHORE)
DATAFLOW = pltpu.SideEffectType.DATAFLOW_SIDE_EFFECTING


def _in_hbm(a):
    return pltpu.with_memory_space_constraint(a, pltpu.HBM)


def _cast_into(wfs, l, chip_arr, name):
    n = len(wfs)

    def body(ch_ref, *refs):
        for w_ref, o_ref in zip(refs[:n], refs[n:]):
            o_ref[...] = w_ref[...].astype(BF16)

    halves = [(wf.shape[1] // 2, wf.shape[2]) for wf in wfs]
    return pl.pallas_call(
        body, name=name, out_shape=[jax.ShapeDtypeStruct((4, 2, rh, cols), BF16) for rh, cols in halves],
        grid_spec=pltpu.PrefetchScalarGridSpec(
            num_scalar_prefetch=1, grid=(2,),
            in_specs=[pl.BlockSpec((None, None, rh, cols), lambda h, ch: (l, h, 0, 0)) for rh, cols in halves],
            out_specs=[pl.BlockSpec((None, None, rh, cols), lambda h, ch: (ch[0], h, 0, 0)) for rh, cols in halves]),
        compiler_params=_cp(("parallel",)),
    )(chip_arr, *[wf.reshape(2, 2, rh, cols) for wf, (rh, cols) in zip(wfs, halves)])


def _half_block(ref, chip, half, to, send_sem, recv_sem):
    blk = ref.at[chip, half]
    return pltpu.make_async_remote_copy(src_ref=blk, dst_ref=blk, send_sem=send_sem, recv_sem=recv_sem,
                                        device_id=to, device_id_type=MESH)


def _gather_weights(bufs, tiny):
    nt = len(bufs)
    n_ici = max(nt * 3, 1)

    def body(*refs):
        tiny_ref = refs[nt]
        o_refs, tiny_o = refs[nt + 1:2 * nt + 1], refs[2 * nt + 1]
        send, recv, fsend, frecv, tsend, trecv, lsem = refs[2 * nt + 2:]
        x, y, c, chip = _place()
        local = pltpu.make_async_copy(tiny_ref, tiny_o.at[chip], lsem)
        local.start()

        def tin(d, origin_chip, to):
            return pltpu.make_async_remote_copy(
                src_ref=tiny_ref, dst_ref=tiny_o.at[origin_chip], send_sem=tsend.at[d - 1], recv_sem=trecv.at[d - 1],
                device_id=to, device_id_type=MESH)

        sends = []
        for t in range(nt):
            for d in (1, 2, 3):
                px, py, _ = _chip_at(x, y, d)
                sends.append(_half_block(o_refs[t], chip, c, (px, py, c), send.at[3 * t + d - 1], recv.at[3 * t + d - 1]))
        for d in (1, 2, 3):
            px, py, _ = _chip_at(x, y, d)
            sends.append(tin(d, chip, (px, py, c)))
        for cp in sends:
            cp.start()
        passed = []
        for t in range(nt):
            for d in (1, 2, 3):
                k = 3 * t + d - 1
                _, _, pchip = _chip_at(x, y, d)
                _half_block(o_refs[t], pchip, c, (x, y, c), send.at[k], recv.at[k]).wait_recv()
                f = _half_block(o_refs[t], pchip, c, (x, y, 1 - c), fsend.at[k], frecv.at[k])
                f.start()
                passed.append(f)
        for t in range(nt):
            for d in (1, 2, 3):
                k = 3 * t + d - 1
                _, _, pchip = _chip_at(x, y, d)
                _half_block(o_refs[t], pchip, 1 - c, (x, y, 1 - c), fsend.at[k], frecv.at[k]).wait_recv()
        for d in (1, 2, 3):
            _, _, pchip = _chip_at(x, y, d)
            tin(d, pchip, (x, y, c)).wait_recv()
        for cp in sends + passed:
            cp.wait_send()
        local.wait()

    out_shape = [jax.ShapeDtypeStruct(b.shape, b.dtype) for b in bufs]
    out_shape.append(jax.ShapeDtypeStruct((4,) + tiny.shape, tiny.dtype))
    outs = pl.pallas_call(
        body, name="gather_weights_0", out_shape=out_shape,
        in_specs=[ANY] * (nt + 1), out_specs=[ANY] * (nt + 1),
        scratch_shapes=[pltpu.SemaphoreType.DMA((n_ici,)), pltpu.SemaphoreType.DMA((n_ici,)),
                        pltpu.SemaphoreType.DMA((n_ici,)), pltpu.SemaphoreType.DMA((n_ici,)),
                        pltpu.SemaphoreType.DMA((3,)), pltpu.SemaphoreType.DMA((3,)), pltpu.SemaphoreType.DMA],
        input_output_aliases={t: t for t in range(nt)},
        compiler_params=_cp(has_side_effects=True),
    )(*bufs, tiny)
    return outs[:nt], outs[nt]


def _gather_start(bufs, tag, after=()):
    nt, na = len(bufs), len(after)

    def body(*refs):
        b_refs = refs[:nt]
        send, recv = refs[nt + na], refs[nt + na + 1]
        token = refs[2 * nt + na + 2]
        x, y, c, chip = _place()
        for t in range(nt):
            for d in (1, 2, 3):
                px, py, _ = _chip_at(x, y, d)
                _half_block(b_refs[t], chip, c, (px, py, c), send.at[3 * t + d - 1], recv.at[3 * t + d - 1]).start()
        token[...] = jnp.zeros_like(token)

    outs = pl.pallas_call(
        body, name=f"gather_start_{tag}",
        out_shape=(pltpu.SemaphoreType.DMA((3 * nt,)), pltpu.SemaphoreType.DMA((3 * nt,)),
                   *[pltpu.HBM(b.shape, b.dtype) for b in bufs], jax.ShapeDtypeStruct((8, 128), F32)),
        in_specs=[HBM] * nt + [ANY] * na, out_specs=(SEM, SEM, *[HBM] * nt, pl.BlockSpec(memory_space=pltpu.VMEM)),
        input_output_aliases={t: 2 + t for t in range(nt)},
        compiler_params=pltpu.CompilerParams(has_side_effects=DATAFLOW),
    )(*[_in_hbm(b) for b in bufs], *after)
    return outs[0], outs[1], list(outs[2:2 + nt]), outs[2 + nt]


def _gather_wait(send, recv, bufs, after, tag):
    nt = len(bufs)

    def body(*refs):
        b_refs = refs[:nt]
        send_ref, recv_ref = refs[nt], refs[nt + 1]
        x, y, c, chip = _place()
        for t in range(nt):
            for d in (1, 2, 3):
                k = 3 * t + d - 1
                px, py, pchip = _chip_at(x, y, d)
                _half_block(b_refs[t], chip, c, (px, py, c), send_ref.at[k], recv_ref.at[k]).wait_send()
                _half_block(b_refs[t], pchip, c, (px, py, c), send_ref.at[k], recv_ref.at[k]).wait_recv()

    after = tuple(after) if isinstance(after, (tuple, list)) else (after,)
    outs = pl.pallas_call(
        body, name=f"gather_wait_{tag}", out_shape=[pltpu.HBM(b.shape, b.dtype) for b in bufs],
        in_specs=[HBM] * nt + [SEM, SEM] + [ANY] * len(after), out_specs=[HBM] * nt,
        input_output_aliases={t: t for t in range(nt)},
        compiler_params=pltpu.CompilerParams(has_side_effects=DATAFLOW),
    )(*bufs, send, recv, *after)
    return list(outs)


def _gather_pass_on(bufs, tag):
    nt = len(bufs)

    def body(*refs):
        o_refs = refs[nt:2 * nt]
        fsend, frecv = refs[2 * nt:]
        x, y, c, _ = _place()
        cps = []
        for t in range(nt):
            for d in (1, 2, 3):
                k = 3 * t + d - 1
                _, _, pchip = _chip_at(x, y, d)
                cps.append(_half_block(o_refs[t], pchip, c, (x, y, 1 - c), fsend.at[k], frecv.at[k]))
        for cp in cps:
            cp.start()
        for t in range(nt):
            for d in (1, 2, 3):
                k = 3 * t + d - 1
                _, _, pchip = _chip_at(x, y, d)
                _half_block(o_refs[t], pchip, 1 - c, (x, y, 1 - c), fsend.at[k], frecv.at[k]).wait_recv()
        for cp in cps:
            cp.wait_send()

    return pl.pallas_call(
        body, name=f"gather_pass_on_{tag}", out_shape=[jax.ShapeDtypeStruct(b.shape, b.dtype) for b in bufs],
        in_specs=[ANY] * nt, out_specs=[ANY] * nt,
        scratch_shapes=[pltpu.SemaphoreType.DMA((3 * nt,)), pltpu.SemaphoreType.DMA((3 * nt,))],
        input_output_aliases={t: t for t in range(nt)},
        compiler_params=_cp(has_side_effects=True),
    )(*bufs)


def _chip_copy(c_ref, land_ref, x, y, c, d, send_sem, recv_sem):
    px, py, pchip = _chip_at(x, y, d)
    return pltpu.make_async_remote_copy(src_ref=c_ref.at[pchip], dst_ref=land_ref.at[d - 1], send_sem=send_sem, recv_sem=recv_sem,
                                        device_id=(px, py, c), device_id_type=MESH)


def _exchange_start(srcs, lands, copies, nsem, name):
    ns, n = len(srcs), len(srcs) + len(lands)

    def body(*refs):
        for cp in copies(refs[:ns], refs[ns:n], refs[n], refs[n + 1]):
            cp.start()
        token = refs[2 * n + 2]
        token[...] = jnp.zeros_like(token)

    outs = pl.pallas_call(
        body, name=name,
        out_shape=(pltpu.SemaphoreType.DMA((nsem,)), pltpu.SemaphoreType.DMA((nsem,)),
                   *[pltpu.HBM(a.shape, a.dtype) for a in list(srcs) + list(lands)], jax.ShapeDtypeStruct((8, 128), F32)),
        in_specs=[HBM] * n, out_specs=(SEM, SEM, *[HBM] * n, pl.BlockSpec(memory_space=pltpu.VMEM)),
        input_output_aliases={i: 2 + i for i in range(n)},
        compiler_params=pltpu.CompilerParams(has_side_effects=DATAFLOW),
    )(*[_in_hbm(a) for a in list(srcs) + list(lands)])
    return outs[0], outs[1], list(outs[2:2 + ns]), list(outs[2 + ns:2 + n]), outs[2 + n]


def _exchange_wait(send, recv, srcs, lands, after, copies, name):
    ns, n = len(srcs), len(srcs) + len(lands)

    def body(*refs):
        for cp in copies(refs[:ns], refs[ns:n], refs[n], refs[n + 1]):
            cp.wait_send()
            cp.wait_recv()

    outs = pl.pallas_call(
        body, name=name, out_shape=[pltpu.HBM(a.shape, a.dtype) for a in list(srcs) + list(lands)],
        in_specs=[HBM] * n + [SEM, SEM, ANY], out_specs=[HBM] * n,
        input_output_aliases={i: i for i in range(n)},
        compiler_params=pltpu.CompilerParams(has_side_effects=DATAFLOW),
    )(*srcs, *lands, send, recv, after)
    return list(outs[:ns]), list(outs[ns:])


def _pass_on_copies(b_refs, land_refs, send, recv):
    del land_refs
    x, y, c, _ = _place()
    return [_half_block(b_refs[t], _chip_at(x, y, d)[2], c, (x, y, 1 - c), send.at[3 * t + d - 1], recv.at[3 * t + d - 1])
            for t in range(len(b_refs)) for d in (1, 2, 3)]


def _chips_copies(c_refs, land_refs, send, recv):
    x, y, c, _ = _place()
    return [_chip_copy(c_refs[t], land_refs[t], x, y, c, d, send.at[3 * t + d - 1], recv.at[3 * t + d - 1])
            for t in range(len(c_refs)) for d in (1, 2, 3)]


def _sibling_copies(g_refs, land_refs, send, recv):
    x, y, c, _ = _place()
    return [pltpu.make_async_remote_copy(
        src_ref=g_refs[t].at[k, 1 - c], dst_ref=land_refs[t].at[k], send_sem=send.at[4 * t + k], recv_sem=recv.at[4 * t + k],
        device_id=(x, y, 1 - c), device_id_type=MESH) for t in range(len(g_refs)) for k in range(4)]


def _join_copies(f_refs, land_refs, send, recv):
    del land_refs
    x, y, c, _ = _place()
    return [pltpu.make_async_remote_copy(
        src_ref=f_refs[t].at[c], dst_ref=f_refs[t].at[c], send_sem=send.at[t], recv_sem=recv.at[t],
        device_id=(x, y, 1 - c), device_id_type=MESH) for t in range(len(f_refs))]


def _add_half(gs, rs, c_arr, name):
    n = len(gs)

    def body(c_ref, *refs):
        for g_ref, r_ref, o_ref in zip(refs[:n], refs[n:2 * n], refs[2 * n:]):
            o_ref[...] = (g_ref[...] + r_ref[...]).astype(BF16)

    def own(g):
        return pl.BlockSpec((None, None) + g.shape[2:], lambda k, cr: (k, cr[0], 0, 0))

    def blk(g):
        return pl.BlockSpec((None,) + g.shape[2:], lambda k, cr: (k, 0, 0))

    return pl.pallas_call(
        body, name=name, out_shape=[jax.ShapeDtypeStruct((4,) + g.shape[2:], BF16) for g in gs],
        grid_spec=pltpu.PrefetchScalarGridSpec(
            num_scalar_prefetch=1, grid=(4,),
            in_specs=[own(g) for g in gs] + [blk(g) for g in gs], out_specs=[blk(g) for g in gs]),
        compiler_params=_cp(("parallel",)),
    )(c_arr, *gs, *rs)


def _sum_chips(css, r3s, place_arr, name):
    n = len(css)

    def body(pl_ref, *refs):
        up = lambda ref: ref[...].astype(F32)
        for t in range(n):
            a_ref, (r0_ref, r1_ref, r2_ref), o_ref = refs[t], refs[n + 3 * t:n + 3 * t + 3], refs[4 * n + t]
            o_ref[...] = ((up(a_ref) + up(r0_ref)) + up(r1_ref)) + up(r2_ref)

    def blk(cs, first):
        _, rh, cols = cs.shape
        return pl.BlockSpec((None, rh // 2, cols), lambda i, pa: (first(pa), i, 0))

    in_specs = [blk(cs, lambda pa: pa[0]) for cs in css]
    for cs in css:
        in_specs += [blk(cs, lambda pa, d=d: d) for d in range(3)]
    return pl.pallas_call(
        body, name=name, out_shape=[jax.ShapeDtypeStruct((2,) + cs.shape[1:], F32) for cs in css],
        grid_spec=pltpu.PrefetchScalarGridSpec(
            num_scalar_prefetch=1, grid=(2,), in_specs=in_specs, out_specs=[blk(cs, lambda pa: pa[1]) for cs in css]),
        compiler_params=_cp(("parallel",)),
    )(place_arr, *css, *[r3 for r3 in r3s for _ in range(3)])


def _allreduce_small(pack):
    rows = pack.shape[0]
    hr = rows // 2

    def body(p_ref, o_ref, sib, slots, s1, r1, s2, r2, s3, r3):
        x, y, c, chip = _place()
        sibling = (x, y, 1 - c)
        ex = pltpu.make_async_remote_copy(src_ref=p_ref, dst_ref=sib, send_sem=s1, recv_sem=r1,
                                          device_id=sibling, device_id_type=MESH)
        ex.start()
        ex.wait()
        half = pl.ds(pl.multiple_of(c * hr, 16), hr)
        slots[0] = p_ref[half, :] + sib[half, :]
        cps = []
        for d in (1, 2, 3):
            px, py, _ = _chip_at(x, y, d)
            cps.append(pltpu.make_async_remote_copy(
                src_ref=slots.at[0], dst_ref=slots.at[d], send_sem=s2.at[d - 1], recv_sem=r2.at[d - 1],
                device_id=(px, py, c), device_id_type=MESH))
        for cp in cps:
            cp.start()
        for cp in cps:
            cp.wait()
        tot = slots[chip]
        for k in (1, 2, 3):
            tot = tot + slots[jnp.bitwise_xor(chip, k)]
        o_ref[half, :] = tot
        back = pltpu.make_async_remote_copy(src_ref=o_ref.at[half, :], dst_ref=o_ref.at[half, :], send_sem=s3, recv_sem=r3,
                                            device_id=sibling, device_id_type=MESH)
        back.start()
        back.wait()

    vm = pl.BlockSpec(memory_space=pltpu.VMEM)
    return pl.pallas_call(
        body, name="allreduce_small", out_shape=jax.ShapeDtypeStruct((rows, 128), F32),
        in_specs=[vm], out_specs=vm,
        scratch_shapes=[pltpu.VMEM((rows, 128), F32), pltpu.VMEM((4, hr, 128), F32),
                        pltpu.SemaphoreType.DMA, pltpu.SemaphoreType.DMA, pltpu.SemaphoreType.DMA((3,)), pltpu.SemaphoreType.DMA((3,)),
                        pltpu.SemaphoreType.DMA, pltpu.SemaphoreType.DMA],
        compiler_params=_cp(has_side_effects=True),
    )(pack)


def _small_chip_sum(pack, after=()):
    rows = pack.shape[0]
    hr = rows // 2

    def body(p_ref, *rest):
        o_ref, sib, s1, r1 = rest[-4:]
        x, y, c, _ = _place()
        ex = pltpu.make_async_remote_copy(src_ref=p_ref, dst_ref=sib, send_sem=s1, recv_sem=r1,
                                          device_id=(x, y, 1 - c), device_id_type=MESH)
        ex.start()
        ex.wait()
        half = pl.ds(pl.multiple_of(c * hr, 16), hr)
        o_ref[...] = (p_ref[half, :] + sib[half, :]).astype(BF16)

    vm = pl.BlockSpec(memory_space=pltpu.VMEM)
    return pl.pallas_call(
        body, name="small_chip_sum", out_shape=jax.ShapeDtypeStruct((hr, 128), BF16),
        in_specs=[vm] + [ANY] * len(after), out_specs=vm,
        scratch_shapes=[pltpu.VMEM((rows, 128), F32), pltpu.SemaphoreType.DMA, pltpu.SemaphoreType.DMA],
        compiler_params=_cp(has_side_effects=True),
    )(pack, *after)


def _small_copies(c_refs, land_refs, send, recv):
    x, y, c, _ = _place()
    cps = []
    for d in (1, 2, 3):
        px, py, _ = _chip_at(x, y, d)
        cps.append(pltpu.make_async_remote_copy(src_ref=c_refs[0], dst_ref=land_refs[0].at[d - 1], send_sem=send.at[d - 1],
                                                recv_sem=recv.at[d - 1], device_id=(px, py, c), device_id_type=MESH))
    return cps


def _small_total(csum, land):
    hr = csum.shape[0]

    def body(c_ref, l_ref, o_ref, slots, s3, r3):
        x, y, c, chip = _place()
        slots[0] = c_ref[...]
        for d in (1, 2, 3):
            slots[d] = l_ref[d - 1]
        tot = slots[chip].astype(F32)
        for k in (1, 2, 3):
            tot = tot + slots[jnp.bitwise_xor(chip, k)].astype(F32)
        half = pl.ds(pl.multiple_of(c * hr, 16), hr)
        o_ref[half, :] = tot
        back = pltpu.make_async_remote_copy(src_ref=o_ref.at[half, :], dst_ref=o_ref.at[half, :], send_sem=s3, recv_sem=r3,
                                            device_id=(x, y, 1 - c), device_id_type=MESH)
        back.start()
        back.wait()

    vm = pl.BlockSpec(memory_space=pltpu.VMEM)
    return pl.pallas_call(
        body, name="small_total", out_shape=jax.ShapeDtypeStruct((2 * hr, 128), F32), in_specs=[vm, vm], out_specs=vm,
        scratch_shapes=[pltpu.VMEM((4, hr, 128), BF16), pltpu.SemaphoreType.DMA, pltpu.SemaphoreType.DMA],
        compiler_params=_cp(has_side_effects=True),
    )(csum, land)


def _adam_math(gv, wv, mv, vv):
    m2 = ADAM_B1 * mv + (1.0 - ADAM_B1) * gv
    v2 = ADAM_B2 * vv + (1.0 - ADAM_B2) * (gv * gv)
    m_hat = m2 / (1.0 - ADAM_B1 ** ADAM_STEP)
    v_hat = v2 / (1.0 - ADAM_B2 ** ADAM_STEP)
    return -ADAM_LR * (m_hat / (jnp.sqrt(v_hat) + ADAM_EPS) + ADAM_WD * wv), m2, v2


def _adam(g, w, m, v, name):
    rows, cols = g.shape
    rb = rows // 4

    def body(g_ref, w_ref, m_ref, v_ref, d_ref, m2_ref, v2_ref):
        d_ref[...], m2_ref[...], v2_ref[...] = _adam_math(g_ref[...], w_ref[...], m_ref[...], v_ref[...])

    blk = pl.BlockSpec((rb, cols), lambda i: (i, 0))
    shp = jax.ShapeDtypeStruct((rows, cols), F32)
    return pl.pallas_call(
        body, name=name, grid=(4,), in_specs=[blk] * 4, out_specs=[blk] * 3, out_shape=[shp] * 3,
        compiler_params=_cp(("parallel",)),
    )(g, w, m, v)


def _adam_layer(gs, ws, ms, vs, l, prevs, name):
    n = len(gs)
    prev = [a for p4 in prevs if p4 is not None for a in p4]

    def body(*refs):
        outs = refs[len(refs) - 4 * n:]
        for t in range(n):
            g_ref, w_ref, m_ref, v_ref = refs[4 * t:4 * t + 4]
            go_ref, d_ref, m2_ref, v2_ref = outs[4 * t:4 * t + 4]
            gv = g_ref[...]
            go_ref[...] = gv
            d_ref[...], m2_ref[...], v2_ref[...] = _adam_math(gv, w_ref[...], m_ref[...], v_ref[...])

    in_specs, out_specs, out_shape, operands, aliases = [], [], [], [], {}
    for t, g in enumerate(gs):
        rows, cols = g.shape
        lay = pl.BlockSpec((None, rows // 4, cols), lambda i: (l, i, 0))
        in_specs += [pl.BlockSpec((rows // 4, cols), lambda i: (i, 0)), lay, lay, lay]
        operands += [g, ws[t], ms[t], vs[t]]
        out_specs += [lay] * 4
        out_shape += [jax.ShapeDtypeStruct((2, rows, cols), F32)] * 4
    k = 4 * n
    for t, p4 in enumerate(prevs):
        if p4 is not None:
            for j in range(4):
                aliases[k] = 4 * t + j
                k += 1
    outs = pl.pallas_call(
        body, name=name, grid=(4,), in_specs=in_specs + [ANY] * len(prev), out_specs=out_specs, out_shape=out_shape,
        input_output_aliases=aliases, compiler_params=_cp(("parallel",)),
    )(*operands, *prev)
    return [list(outs[4 * t:4 * t + 4]) for t in range(n)]


def _rows128(a):
    return a.reshape(-1, 128)


def _pack(arrs, mult):
    parts = [_rows128(a) for a in arrs]
    rows = sum(q.shape[0] for q in parts)
    pad = -rows % mult
    if pad:
        parts.append(jnp.zeros((pad, 128), F32))
    return jnp.concatenate(parts, axis=0)


def _unpack(pack, shapes):
    out, o = [], 0
    for s in shapes:
        n = 1
        for e in s:
            n *= e
        out.append(pack[o:o + n // 128].reshape(s))
        o += n // 128
    return out


WEIGHTS = ['norm1_g', 'w_in', 'gmlp_ln_g', 'gmlp_ln_b', 'gmlp_w_s', 'gmlp_b_s', 'conv_w', 'conv_b', 'lru_w_r', 'lru_b_r', 'lru_w_i',
           'lru_b_i', 'lru_lambda', 'w_out', 'norm2_g', 'w_ffn_in', 'w_ffn_out', 'final_g']
BIG = ['w_in', 'w_out', 'w_ffn_in', 'w_ffn_out']
SMALL = [n for n in WEIGHTS if n not in BIG]
CHIP_SHARDED_SMALL = ['conv_w', 'lru_b_r', 'lru_b_i', 'lru_lambda']


def kernel(x, norm1_g, w_in, gmlp_ln_g, gmlp_ln_b, gmlp_w_s, gmlp_b_s, conv_w, conv_b, lru_w_r, lru_b_r, lru_w_i, lru_b_i, lru_lambda, w_out, norm2_g, w_ffn_in, w_ffn_out, final_g, loss_target, m_norm1_g, m_w_in, m_gmlp_ln_g, m_gmlp_ln_b, m_gmlp_w_s, m_gmlp_b_s, m_conv_w, m_conv_b, m_lru_w_r, m_lru_b_r, m_lru_w_i, m_lru_b_i, m_lru_lambda, m_w_out, m_norm2_g, m_w_ffn_in, m_w_ffn_out, m_final_g, v_norm1_g, v_w_in, v_gmlp_ln_g, v_gmlp_ln_b, v_gmlp_w_s, v_gmlp_b_s, v_conv_w, v_conv_b, v_lru_w_r, v_lru_b_r, v_lru_w_i, v_lru_b_i, v_lru_lambda, v_w_out, v_norm2_g, v_w_ffn_in, v_w_ffn_out, v_final_g):
    a = dict(locals())
    w = {n: a[n] for n in WEIGHTS}
    mom = {n: a["m_" + n] for n in WEIGHTS}
    var = {n: a["v_" + n] for n in WEIGHTS}
    _, _, c, chip = _place()
    c_arr, chip_arr = jnp.reshape(c, (1,)).astype(jnp.int32), jnp.reshape(chip, (1,)).astype(jnp.int32)
    place_arr = jnp.stack([chip, c]).astype(jnp.int32)

    first, rest = BIG[:1], BIG[1:]

    def as_weights(names, full):
        wb = {n: f.reshape(4, 2 * f.shape[2], f.shape[3]) for n, f in zip(names, full)}
        if "w_out" in wb:
            wb["w_out"] = wb["w_out"].reshape(D, D)
            wb["w_ffn_out"] = wb["w_ffn_out"].reshape(DFF, D)
        return wb

    def cast(names, l, tag):
        return _cast_into([w[n] for n in names], l, chip_arr, f"cast_{tag}")

    def landed(fly, names, after, tag):
        return as_weights(names, _gather_pass_on(_gather_wait(fly[0], fly[1], fly[2], after, tag), tag))

    tiny = _pack([w[n] for n in CHIP_SHARDED_SMALL], 8)
    _, tiny_full = _gather_weights([], tiny)
    fly_in = _gather_start(cast(first, 0, "in"), "in", after=(tiny_full,))
    fly0 = _gather_start(cast(rest, 0, "0"), "0", after=(fly_in[3],))
    fly1 = _gather_start(cast(BIG, 1, "1"), "1", after=(fly0[3],))
    p = {n: w[n] for n in SMALL}
    parts = [_unpack(tiny_full[k], [w[n].shape for n in CHIP_SHARDED_SMALL]) for k in range(4)]
    for i, n in enumerate(CHIP_SHARDED_SMALL):
        p[n] = jnp.concatenate([parts[k][i] for k in range(4)], axis=-1)

    operands = [_layer_operands(l, p) for l in range(2)]
    state_packs = [_pack([src[n] for n in SMALL], 32) for src in (w, mom, var)]
    ahead = tuple(jax.tree.leaves(operands)) + tuple(state_packs)

    passing = {}

    def pass_on_1(gu):
        bufs = _gather_wait(fly1[0], fly1[1], fly1[2], gu, "1")
        passing[1] = _exchange_start(bufs, [], _pass_on_copies, 3 * len(bufs), "gather_pass_on_start_1")
        return (passing[1][-1],)

    def pass_on_0(ya):
        bufs = _gather_wait(fly0[0], fly0[1], fly0[2], ya, "0")
        passing[0] = _exchange_start(bufs, [], _pass_on_copies, 3 * len(bufs), "gather_pass_on_start_0")
        return (passing[0][-1],)

    def rest0(merged):
        send, recv, bufs, _, _ = passing[0]
        return as_weights(rest, _exchange_wait(send, recv, bufs, [], merged, _pass_on_copies, "gather_pass_on_wait_0")[0])

    xa, saved0 = _forward_layer(0, x[0], p, landed(fly_in, first, (fly1[3],) + ahead, "in"), after=(fly0[3], fly1[3]),
                                early=pass_on_0, rest=rest0, near_end=pass_on_1, operands=operands[0])
    send, recv, bufs1, _, _ = passing[1]
    xb, saved1 = _forward_layer(
        1, xa, p, as_weights(BIG, _exchange_wait(send, recv, bufs1, [], xa, _pass_on_copies, "gather_pass_on_wait_1")[0]),
        operands=operands[1], loss=(loss_target[0], p["final_g"][None]))
    dxb, loss_v, dfg = xb
    loss = lax.psum(loss_v[0, 0], ("x", "y", "c"))

    out, flying = {}, {}

    def halves(grads):
        return [g.reshape(4, 2, -1, g.shape[-1]) for g in grads]

    def sibling_start(grads, names, l, tag):
        gs = halves(grads)
        lands = [lax.empty((4,) + g.shape[2:], g.dtype) for g in gs]
        flying["s" + tag] = (names, l) + tuple(
            _exchange_start(gs, lands, _sibling_copies, 4 * len(gs), f"grads_to_sibling_start_{tag}"))
        return (flying["s" + tag][-1],)

    def chips_start(gs, from_sib, names, l, tag):
        cs = _add_half(gs, from_sib, c_arr, f"add_half_{tag}")
        lands = [lax.empty((3,) + a.shape[1:], a.dtype) for a in cs]
        flying[tag] = (names, l) + tuple(_exchange_start(cs, lands, _chips_copies, 3 * len(cs), f"grads_to_chips_start_{tag}"))
        return (flying[tag][-1],)

    def sibling_finish(tag, after):
        names, l, send, recv, gs, lands, _ = flying["s" + tag]
        gs, from_sib = _exchange_wait(send, recv, gs, lands, after, _sibling_copies, f"grads_to_sibling_wait_{tag}")
        return chips_start(gs, from_sib, names, l, tag)

    def reduce_sums(tags, after):
        groups, ts = [], []
        for tag in tags:
            names, l, send, recv, cs, lands, _ = flying[tag]
            cs, lands = _exchange_wait(send, recv, cs, lands, after, _chips_copies, f"grads_to_chips_wait_{tag}")
            ts += _sum_chips(cs, lands, place_arr, f"sum_chips_{tag}")
            groups.append((tag, names))
        flying["j" + tags[0]] = (groups, l) + tuple(_exchange_start(ts, [], _join_copies, len(ts), f"grads_join_start_{tags[0]}"))
        return (flying["j" + tags[0]][-1],)

    def reduce_adam(tag0, after):
        groups, l, send, recv, ts, _, _ = flying["j" + tag0]
        joined = _exchange_wait(send, recv, ts, [], after, _join_copies, f"grads_join_wait_{tag0}")[0]
        for tag, names in groups:
            gs, joined = [j.reshape(w[n].shape[1:]) for n, j in zip(names, joined)], joined[len(names):]
            res = _adam_layer(gs, [w[n] for n in names], [mom[n] for n in names], [var[n] for n in names], l,
                              [out.get(n) for n in names], f"adam_{tag}")
            out.update(zip(names, res))

    def late1(grads):
        return sibling_finish("1a", grads[0]) + sibling_start(grads, first, 1, "1b")

    def midway0(grads):
        return reduce_sums(("1a", "1b"), grads[0]) + sibling_start(grads, rest, 0, "0a")

    def stacked_small(small0):
        small = {k: jnp.stack([small0[k], small1[k]]) for k in LAYER_SMALL}
        return dict(small, final_g=dfg[0])

    def midway3_0(small0):
        small = stacked_small(dict(small0, norm1_g=jnp.zeros((D,), F32)))
        csum = _small_chip_sum(_pack([small[n] for n in SMALL], 32))
        flying["small"] = _exchange_start([csum], [lax.empty((3,) + csum.shape, BF16)], _small_copies, 3, "small_to_chips_start")
        return (flying["small"][-1],)

    def late0(grads):
        tok = sibling_start(grads, first, 0, "0b")
        reduce_adam("1a", tok[0])
        return sibling_finish("0b", out[first[0]][0])

    dxa, big1, small1 = _backward_layer(1, dxb, saved1, midway=lambda grads: sibling_start(grads, rest, 1, "1a"), late=late1)
    dx, big0, small0 = _backward_layer(0, dxa, saved0, after=sibling_finish("1b", dxa), midway=midway0,
                                       midway2=lambda dws: sibling_finish("0a", dws), midway3=midway3_0, late=late0)
    join_tok = reduce_sums(("0a", "0b"), dx)
    small = stacked_small(small0)

    full_shapes = [small[n].shape for n in SMALL]
    send, recv, csum, land, _ = flying["small"]
    csum, land = _exchange_wait(send, recv, csum, land, join_tok[0], _small_copies, "small_to_chips_wait")
    red = _unpack(_small_total(csum[0], land[0]), full_shapes)
    norm1_0 = _allreduce_small(_pack([small0["norm1_g"]], 32))[:D // 128].reshape(D)
    reduce_adam("0a", norm1_0)
    red[SMALL.index("norm1_g")] = red[SMALL.index("norm1_g")].at[0].set(norm1_0)
    g_small = []
    for n, g in zip(SMALL, red):
        if n in CHIP_SHARDED_SMALL:
            g = lax.dynamic_slice_in_dim(g, chip * w[n].shape[-1], w[n].shape[-1], axis=g.ndim - 1)
        g_small.append(g)
    shapes = [w[n].shape for n in SMALL]
    upd = [_unpack(u, shapes) for u in _adam(_pack(g_small, 32), *state_packs, "adam_small")]
    for i, n in enumerate(SMALL):
        out[n] = [g_small[i], upd[0][i], upd[1][i], upd[2][i]]

    return (loss, dx[None]) + tuple(out[n][i] for i in range(4) for n in WEIGHTS)
```

```python
import functools

import jax
import jax.numpy as jnp
from jax import lax
from jax.experimental import pallas as pl
from jax.experimental.pallas import tpu as pltpu

F32 = jnp.float32
BF16 = jnp.bfloat16
MESH = pl.DeviceIdType.MESH

D = 1024
NH = 8
HD = 128
CHUNK = 128
GMLP_ROWS = 512
N_IN_T = 12
DFF = 2816
DFF_SH = 1408
EPS = 1e-6
LRU_C = 8.0
ADAM_LR, ADAM_B1, ADAM_B2, ADAM_EPS, ADAM_WD, ADAM_STEP = 0.001, 0.9, 0.999, 1e-08, 0.01, 10

TM = 512
TM_BIG = 1024
RT = 128
PADR = 8
VMEM_LIMIT = 56 * 1024 * 1024


def _cp(sem=None, **kw):
    if sem is not None:
        kw["dimension_semantics"] = sem
    return pltpu.CompilerParams(vmem_limit_bytes=VMEM_LIMIT, **kw)


_GC = 0.7978845608028654


def _sigmoid(x):
    return 0.5 * jnp.tanh(0.5 * x) + 0.5


_GK = 0.044715


def _gelu(x):
    t = jnp.tanh(x * (_GC + (_GC * _GK) * (x * x)))
    return x * (0.5 + 0.5 * t)


def _gelu_and_grad(x):
    x2 = x * x
    t = jnp.tanh(x * (_GC + (_GC * _GK) * x2))
    h = 0.5 + 0.5 * t
    return x * h, h + x * (1.0 - t * t) * (0.5 * _GC + (1.5 * _GC * _GK) * x2)


def _softplus_neg(lam):
    y = jnp.exp(-jnp.abs(lam))
    u = 1.0 + y
    l1p = jnp.where(u == 1.0, y, jnp.log(u) * y / (u - 1.0))
    return jnp.maximum(-lam, 0.0) + l1p


def _dot(a, b):
    return jnp.dot(a, b, preferred_element_type=F32)


def _dot_nt(a, b):
    return lax.dot_general(a, b, (((1,), (1,)), ((), ())), preferred_element_type=F32)


def _dot_tn(a, b):
    return lax.dot_general(a, b, (((0,), (0,)), ((), ())), preferred_element_type=F32)


def _rms_hat(x):
    r = lax.rsqrt(jnp.mean(x * x, axis=-1, keepdims=True) + EPS)
    return x * r, r


def _rms_bwd(dh, x, g):
    xh, r = _rms_hat(x)
    dxh = dh * g
    dx = r * (dxh - xh * jnp.mean(dxh * xh, axis=-1, keepdims=True))
    return dx, jnp.sum(dh * xh, axis=0, keepdims=True)


def _norm_into(x_ref, g_ref, h_ref):
    xh, _ = _rms_hat(x_ref[...])
    h_ref[...] = (xh * g_ref[...]).astype(BF16)


def _in_tile(j):
    m, hf = j // 2, j % 2
    orig = jnp.where(m < 2, m, jnp.where(m == 2, 4, jnp.where(m < 5, m - 1, 5)))
    t = orig * 2 + hf
    return t // 3, t % 3


ANY = pl.BlockSpec(memory_space=pl.ANY)


def _mm_in(x, g, w_in, l, after=()):
    S = x.shape[0]
    tm = min(2 * TM_BIG, S)

    def body(x_ref, g_ref, w0_ref, w1_ref, *rest):
        o_ref, h_ref = rest[-2:]

        @pl.when(pl.program_id(1) == 0)
        def _():
            _norm_into(x_ref, g_ref, h_ref)
        rp = min(TM, tm)
        for r0 in range(0, tm, rp):
            hv = h_ref[r0:r0 + rp, :]
            o_ref[r0:r0 + rp, 0:512] = _dot(hv, w0_ref[...]).astype(BF16)
            o_ref[r0:r0 + rp, 512:1024] = _dot(hv, w1_ref[...]).astype(BF16)

    def w_tile(hf):
        def w_map(i, m):
            sh, tl = _in_tile(2 * m + hf)
            return (sh, 0, tl)
        return pl.BlockSpec((None, D, 512), w_map)

    return pl.pallas_call(
        body, name=f"mm_in_{l}", grid=(S // tm, 6),
        in_specs=[pl.BlockSpec((tm, D), lambda i, m: (i, 0)), pl.BlockSpec((1, D), lambda i, m: (0, 0)),
                  w_tile(0), w_tile(1)] + [ANY] * len(after),
        out_specs=[pl.BlockSpec((None, tm, D), lambda i, m: (m, i, 0)), pl.BlockSpec((tm, D), lambda i, m: (i, 0))],
        out_shape=[jax.ShapeDtypeStruct((6, S, D), BF16), jax.ShapeDtypeStruct((S, D), BF16)],
        compiler_params=_cp(("parallel", "arbitrary")),
    )(x, g, w_in, w_in, *after)


def _mm_res(a, w, res, l, name, after=()):
    S, K = a.shape

    tm = TM

    def body(a_ref, w_ref, r_ref, *rest):
        rest[-1][...] = r_ref[...] + _dot(a_ref[...], w_ref[...])

    return pl.pallas_call(
        body, name=f"{name}_{l}", grid=(S // tm,),
        in_specs=[pl.BlockSpec((tm, K), lambda i: (i, 0)), pl.BlockSpec((K, D), lambda i: (0, 0)),
                  pl.BlockSpec((tm, D), lambda i: (i, 0))] + [ANY] * len(after),
        out_specs=pl.BlockSpec((tm, D), lambda i: (i, 0)),
        out_shape=jax.ShapeDtypeStruct((S, D), F32),
        compiler_params=_cp(("parallel",)),
    )(a, w, res, *after)


def _mm_ffn_in(x, g, w_fi, l):
    S = x.shape[0]

    tm = min(TM_BIG, S)

    def body(x_ref, g_ref, w_ref, gu_ref, ff_ref, h_ref):
        @pl.when(pl.program_id(1) == 0)
        def _():
            _norm_into(x_ref, g_ref, h_ref)
        for r0 in range(0, tm, TM):
            rows = slice(r0, r0 + TM)
            hv = h_ref[rows, :]
            ga = _dot(hv, w_ref[0])
            gb = _dot(hv, w_ref[1])
            sg = _sigmoid(ga)
            silu = ga * sg
            gu_ref[0, rows, :] = (gb * (sg + silu * (1.0 - sg))).astype(BF16)
            gu_ref[1, rows, :] = silu.astype(BF16)
            ff_ref[rows, :] = (silu * gb).astype(BF16)

    gu, ff, h = pl.pallas_call(
        body, name=f"mm_ffn_in_{l}", grid=(S // tm, 2),
        in_specs=[pl.BlockSpec((tm, D), lambda i, s: (i, 0)), pl.BlockSpec((1, D), lambda i, s: (0, 0)),
                  pl.BlockSpec((2, None, D, DFF_SH), lambda i, s: (0, s, 0, 0))],
        out_specs=[pl.BlockSpec((2, None, tm, DFF_SH), lambda i, s: (0, s, i, 0)),
                   pl.BlockSpec((tm, DFF_SH), lambda i, s: (i, s)),
                   pl.BlockSpec((tm, D), lambda i, s: (i, 0))],
        out_shape=[jax.ShapeDtypeStruct((2, 2, S, DFF_SH), BF16), jax.ShapeDtypeStruct((S, DFF), BF16),
                   jax.ShapeDtypeStruct((S, D), BF16)],
        compiler_params=_cp(("parallel", "arbitrary")),
    )(x, g, w_fi.reshape(2, 2, D, DFF_SH))
    return gu.reshape(4, S, DFF_SH), ff, h


def _gmlp_fwd(z6, ws_b, bs_b, lg, lb):
    S = z6.shape[1]

    ts = min(GMLP_ROWS, S)

    def body(z_ref, ws_ref, bs_ref, lg_ref, lb_ref, o_ref, mix):
        for r0 in range(0, ts, CHUNK):
            rows = slice(r0, r0 + CHUNK)
            gv = _gelu(z_ref[1, rows, :].astype(F32))
            xc = gv - jnp.mean(gv, axis=-1, keepdims=True)
            rs = lax.rsqrt(jnp.mean(xc * xc, axis=-1, keepdims=True) + EPS)
            vb = (xc * rs * lg_ref[...] + lb_ref[...]).astype(BF16)
            for gi in range(NH):
                cs = slice(gi * HD, (gi + 1) * HD)
                mix[rows, cs] = _dot(ws_ref[gi], vb[:, cs])
            o_ref[rows, :] = (_sigmoid(z_ref[2, rows, :].astype(F32)) * _gelu(z_ref[0, rows, :].astype(F32))
                              * (mix[rows, :] + bs_ref[...])).astype(BF16)

    return pl.pallas_call(
        body, name="gmlp_fwd", grid=(S // ts,),
        in_specs=[pl.BlockSpec((3, ts, D), lambda i: (0, i, 0)), pl.BlockSpec((NH, CHUNK, CHUNK), lambda i: (0, 0, 0)),
                  pl.BlockSpec((CHUNK, D), lambda i: (0, 0)), pl.BlockSpec((1, D), lambda i: (0, 0)),
                  pl.BlockSpec((1, D), lambda i: (0, 0))],
        out_specs=pl.BlockSpec((ts, D), lambda i: (i, 0)),
        out_shape=jax.ShapeDtypeStruct((S, D), BF16),
        scratch_shapes=[pltpu.VMEM((ts, D), F32)],
        compiler_params=_cp(("parallel",)),
    )(z6, ws_b, bs_b, lg, lb)


def _row_iota():
    return lax.broadcasted_iota(jnp.int32, (RT, HD), 0)


SUB = 8
UNROLL = 8
GRAD_ROWS = 2048


def _scan_up(a, b, carry):
    row = lax.broadcasted_iota(jnp.int32, (SUB, HD), 0)
    masks = [(d, row >= d) for d in (1, 2, 4)]
    c = jnp.broadcast_to(carry, (SUB, HD))
    hs = []
    for j in range(RT // SUB):
        aj, bj = a[SUB * j:SUB * (j + 1)], b[SUB * j:SUB * (j + 1)]
        for d, m in masks:
            bj = bj + aj * jnp.where(m, pltpu.roll(bj, d, 0), 0.0)
            aj = aj * jnp.where(m, pltpu.roll(aj, d, 0), 1.0)
        h = bj + aj * c
        hs.append(h)
        c = jnp.broadcast_to(h[SUB - 1:SUB, :], (SUB, HD))
    return jnp.concatenate(hs, axis=0), hs[-1][SUB - 1:SUB, :]


def _scan_down(a, b, carry):
    row = lax.broadcasted_iota(jnp.int32, (SUB, HD), 0)
    masks = [(d, row < SUB - d) for d in (1, 2, 4)]
    c = jnp.broadcast_to(carry, (SUB, HD))
    hs = []
    for j in reversed(range(RT // SUB)):
        aj, bj = a[SUB * j:SUB * (j + 1)], b[SUB * j:SUB * (j + 1)]
        for d, m in masks:
            bj = bj + aj * jnp.where(m, pltpu.roll(bj, SUB - d, 0), 0.0)
            aj = aj * jnp.where(m, pltpu.roll(aj, SUB - d, 0), 1.0)
        h = bj + aj * c
        hs.append(h)
        c = jnp.broadcast_to(h[0:1, :], (SUB, HD))
    return jnp.concatenate(hs[::-1], axis=0), hs[-1][0:1, :]


def _decay(r, sp_d):
    log_a = -LRU_C * r * sp_d
    a = jnp.exp(log_a)
    return a, jnp.sqrt(jnp.maximum(-jnp.tanh(log_a) * (a * a + 1.0), 0.0))


def _decay_bwd(r, sp_d):
    log_a = -LRU_C * r * sp_d
    a = jnp.exp(log_a)
    m2 = jnp.maximum(-jnp.tanh(log_a) * (a * a + 1.0), 0.0)
    inv = jnp.where(m2 > 0.0, lax.rsqrt(m2), 0.0)
    return a, m2 * inv, inv


def _lru_gates(xc, d, wr_ref, br_ref, wi_ref, bi_ref, sp):
    xb = xc.astype(BF16)
    r = _sigmoid(_dot(xb, wr_ref[d]) + br_ref[d:d + 1, :])
    i = _sigmoid(_dot(xb, wi_ref[d]) + bi_ref[d:d + 1, :])
    a, mult = _decay(r, sp[d:d + 1, :])
    return r, i, a, mult


def _shifted(win, k):
    w = RT + 2 * PADR
    v = win if k == 0 else pltpu.roll(win, (-k) % w, 0)
    return v[PADR:PADR + RT]


def _conv_taps(win):
    return [_shifted(win, k) for k in (-1, 0, 1, 2)]


def _fill_padded(dst, src_ref, S):
    zeros = jnp.zeros((PADR, HD), F32)
    dst[0:PADR, :] = zeros
    dst[PADR + S:2 * PADR + S, :] = zeros

    def cp(i, c):
        t0 = pl.multiple_of(i * RT, RT)
        dst[pl.ds(t0 + PADR, RT), :] = src_ref[pl.ds(t0, RT), :].astype(F32)
        return c
    lax.fori_loop(0, S // RT, cp, 0)


def _conv_fwd_all(zxp, xc_s, cw_ref, cb_ref, S):
    def cv(i, c):
        t0 = pl.multiple_of(i * RT, RT)
        xm1, x0, xp1, xp2 = _conv_taps(zxp[pl.ds(t0, RT + 2 * PADR), :])
        xc_s[pl.ds(t0, RT), :] = (cb_ref[...] + xm1 * cw_ref[0:1, :] + x0 * cw_ref[1:2, :]
                                  + xp1 * cw_ref[2:3, :] + xp2 * cw_ref[3:4, :])
        return c
    lax.fori_loop(0, S // RT, cv, 0)


def _lru_specs(S):
    head = lambda h: (0, h)
    return [pl.BlockSpec((4, HD), head), pl.BlockSpec((1, HD), head),
            pl.BlockSpec((2, None, HD, HD), lambda h: (0, h, 0, 0)), pl.BlockSpec((2, HD), head),
            pl.BlockSpec((2, None, HD, HD), lambda h: (0, h, 0, 0)), pl.BlockSpec((2, HD), head),
            pl.BlockSpec((2, HD), head)]


def _lru_fwd(z6, ya, cw, cb, wr, br, wi, bi, lam, after=()):
    S = z6.shape[1]
    nt = S // RT

    def body(z_ref, ya_ref, cw_ref, cb_ref, wr_ref, br_ref, wi_ref, bi_ref, lam_ref, *rest):
        mg_ref, h0_ref, h1_ref, zxp, xc_s = rest[-5:]
        sp = _softplus_neg(lam_ref[...])
        _fill_padded(zxp, z_ref.at[0], S)
        _conv_fwd_all(zxp, xc_s, cw_ref, cb_ref, S)

        def scans(i, carry):
            cu, cd = carry
            for u in range(UNROLL):
                j = i * UNROLL + u
                ru = pl.ds(pl.multiple_of(j * RT, RT), RT)
                rd = pl.ds(pl.multiple_of((nt - 1 - j) * RT, RT), RT)
                xu, xd = xc_s[ru, :], xc_s[rd, :]
                _, gi, a, mult = _lru_gates(xu, 0, wr_ref, br_ref, wi_ref, bi_ref, sp)
                hu, cu = _scan_up(a, mult * gi * xu, cu)
                h0_ref[ru, :] = hu
                _, gi, a, mult = _lru_gates(xd, 1, wr_ref, br_ref, wi_ref, bi_ref, sp)
                hd, cd = _scan_down(a, mult * gi * xd, cd)
                h1_ref[rd, :] = hd
            return cu, cd
        z1 = jnp.zeros((1, HD), F32)
        lax.fori_loop(0, nt // UNROLL, scans, (z1, z1))

        def merge(i, c):
            rows = pl.ds(pl.multiple_of(i * RT, RT), RT)
            yb = (h0_ref[rows, :] + h1_ref[rows, :]) * _gelu(z_ref[1, rows, :].astype(F32))
            mg_ref[rows, :] = (ya_ref[rows, :].astype(F32) + _sigmoid(z_ref[2, rows, :].astype(F32)) * yb).astype(BF16)
            return c
        lax.fori_loop(0, nt, merge, 0)

    col = pl.BlockSpec((S, HD), lambda h: (0, h))
    return pl.pallas_call(
        body, name="lru_fwd", grid=(NH,),
        in_specs=[pl.BlockSpec((3, S, HD), lambda h: (1, 0, h)), col] + _lru_specs(S) + [ANY] * len(after),
        out_specs=[col, col, col],
        out_shape=[jax.ShapeDtypeStruct((S, D), BF16), jax.ShapeDtypeStruct((S, D), F32), jax.ShapeDtypeStruct((S, D), F32)],
        scratch_shapes=[pltpu.VMEM((S + 2 * PADR, HD), F32), pltpu.VMEM((S, HD), F32)],
        compiler_params=_cp(("parallel",)),
    )(z6, ya, cw, cb, wr, br, wi, bi, lam, *after)


def _mm_res_loss(a, w, res, tgt, g):
    S, K = a.shape

    def body(a_ref, w_ref, r_ref, t_ref, g_ref, dx_ref, loss_ref, dg_ref):
        @pl.when(pl.program_id(0) == 0)
        def _():
            loss_ref[...] = jnp.zeros_like(loss_ref)
            dg_ref[...] = jnp.zeros_like(dg_ref)
        xv = r_ref[...] + _dot(a_ref[...], w_ref[...])
        xh, _ = _rms_hat(xv)
        e = xh * g_ref[...] - t_ref[...]
        loss_ref[...] += jnp.sum(e * e) * (0.5 / D)
        dx, dgs = _rms_bwd(e * (1.0 / D), xv, g_ref[...])
        dx_ref[...] = dx
        dg_ref[...] += dgs

    row = pl.BlockSpec((TM, D), lambda i: (i, 0))
    vec = pl.BlockSpec((1, D), lambda i: (0, 0))
    return pl.pallas_call(
        body, name="mm_ffn_out_loss", grid=(S // TM,),
        in_specs=[pl.BlockSpec((TM, K), lambda i: (i, 0)), pl.BlockSpec((K, D), lambda i: (0, 0)), row, row, vec],
        out_specs=[row, pl.BlockSpec((1, 128), lambda i: (0, 0)), vec],
        out_shape=[jax.ShapeDtypeStruct((S, D), F32), jax.ShapeDtypeStruct((1, 128), F32), jax.ShapeDtypeStruct((1, D), F32)],
        compiler_params=_cp(("arbitrary",)),
    )(a, w, res, tgt, g)


def _bwd_ffn_out(dx, w_fo, gu, l, after=()):
    S = dx.shape[0]

    tm = min(TM_BIG, S)

    def body(dx_ref, w_ref, gu_ref, *rest):
        o_ref = rest[-1]
        for r0 in range(0, tm, TM):
            rows = slice(r0, r0 + TM)
            d = _dot_nt(dx_ref[rows, :].astype(BF16), w_ref[...])
            if l == 1:
                d = d.astype(BF16)
                o_ref[0, rows, :] = d * gu_ref[0, rows, :]
                o_ref[1, rows, :] = d * gu_ref[1, rows, :]
                continue
            o_ref[0, rows, :] = (d * gu_ref[0, rows, :].astype(F32)).astype(BF16)
            o_ref[1, rows, :] = (d * gu_ref[1, rows, :].astype(F32)).astype(BF16)

    pair = pl.BlockSpec((2, None, tm, DFF_SH), lambda i, s: (0, s, i, 0))
    dgu = pl.pallas_call(
        body, name=f"bwd_ffn_out_{l}", grid=(S // tm, 2),
        in_specs=[pl.BlockSpec((tm, D), lambda i, s: (i, 0)), pl.BlockSpec((DFF_SH, D), lambda i, s: (s, 0)), pair]
        + [ANY] * len(after),
        out_specs=pair,
        out_shape=jax.ShapeDtypeStruct((2, 2, S, DFF_SH), BF16),
        compiler_params=_cp(("parallel", "arbitrary")),
    )(dx, w_fo, gu.reshape(2, 2, S, DFF_SH), *after)
    return dgu.reshape(4, S, DFF_SH)


def _mm_tn(a, b, m_blk, tk, name):
    S, M = a.shape

    def body(a_ref, b_ref, o_ref):
        @pl.when(pl.program_id(1) == 0)
        def _():
            o_ref[...] = jnp.zeros_like(o_ref)
        o_ref[...] += _dot_tn(a_ref[...], b_ref[...].astype(BF16))

    return pl.pallas_call(
        body, name=name, grid=(M // m_blk, S // tk),
        in_specs=[pl.BlockSpec((tk, m_blk), lambda m, k: (k, m)), pl.BlockSpec((tk, D), lambda m, k: (k, 0))],
        out_specs=pl.BlockSpec((m_blk, D), lambda m, k: (m, 0)),
        out_shape=jax.ShapeDtypeStruct((M, D), F32),
        compiler_params=_cp(("parallel", "arbitrary")),
    )(a, b)


def _mm_nt_rms_bwd(a, a_specs, w, w_specs, nk, tm, x, g, dres, name, after=()):
    S = x.shape[0]
    sub = len(a_specs)

    def body(*refs):
        a_refs, w_refs = refs[:sub], refs[sub:2 * sub]
        x_ref, g_ref, r_ref = refs[2 * sub:2 * sub + 3]
        dx_ref, dg_ref, acc = refs[-3:]
        i, k = pl.program_id(0), pl.program_id(1)
        @pl.when(k == 0)
        def _():
            acc[...] = jnp.zeros_like(acc)
        for j in range(sub):
            acc[...] += _dot_nt(a_refs[j][...], w_refs[j][...])

        @pl.when(jnp.logical_and(i == 0, k == 0))
        def _():
            dg_ref[...] = jnp.zeros_like(dg_ref)

        @pl.when(k == nk - 1)
        def _():
            dx, dgs = _rms_bwd(acc[...], x_ref[...], g_ref[...])
            dx_ref[...] = r_ref[...] + dx
            dg_ref[...] += dgs

    row = pl.BlockSpec((tm, D), lambda i, k: (i, 0))
    vec = pl.BlockSpec((1, D), lambda i, k: (0, 0))
    return pl.pallas_call(
        body, name=name, grid=(S // tm, nk),
        in_specs=list(a_specs) + list(w_specs) + [row, vec, row] + [ANY] * len(after),
        out_specs=[row, vec],
        out_shape=[jax.ShapeDtypeStruct((S, D), F32), jax.ShapeDtypeStruct((1, D), F32)],
        scratch_shapes=[pltpu.VMEM((tm, D), F32)],
        compiler_params=_cp(("arbitrary", "arbitrary")),
    )(*[a] * sub, *[w] * sub, x, g, dres, *after)


def _dw_ffn_in(h, dgu, l):
    S = h.shape[0]

    def body(h_ref, b_ref, o_ref):
        @pl.when(pl.program_id(1) == 0)
        def _():
            o_ref[...] = jnp.zeros_like(o_ref)
        o_ref[...] += _dot_tn(h_ref[...], b_ref[...])

    tk = min(2 * TM_BIG, S)
    return pl.pallas_call(
        body, name=f"dw_ffn_in_{l}", grid=(4, S // tk),
        in_specs=[pl.BlockSpec((tk, D), lambda j, k: (k, 0)), pl.BlockSpec((None, tk, DFF_SH), lambda j, k: (j, k, 0))],
        out_specs=pl.BlockSpec((None, D, DFF_SH), lambda j, k: (j, 0, 0)),
        out_shape=jax.ShapeDtypeStruct((4, D, DFF_SH), F32),
        compiler_params=_cp(("parallel", "arbitrary")),
    )(h, dgu)


_HALF_COMPS = ((0, 1, 3), (4, 2, 5))


def _dw_in(h, dz6, l, after=()):
    S = h.shape[0]

    def body(h_ref, d0_ref, d1_ref, d2_ref, *rest):
        o_ref = rest[-1]

        @pl.when(pl.program_id(1) == 0)
        def _():
            o_ref[...] = jnp.zeros_like(o_ref)
        hv = h_ref[...]
        for q, d_ref in enumerate((d0_ref, d1_ref, d2_ref)):
            for hf in range(2):
                col = 1024 * q + 512 * hf
                o_ref[col // 1536, :, col % 1536:col % 1536 + 512] += _dot_tn(hv, d_ref[:, 512 * hf:512 * (hf + 1)])

    tk = min(TM_BIG, S)

    def comp(q):
        return pl.BlockSpec((None, tk, D), lambda p, k: (jnp.where(p == 0, _HALF_COMPS[0][q], _HALF_COMPS[1][q]), k, 0))

    return pl.pallas_call(
        body, name=f"dw_in_{l}", grid=(2, S // tk),
        in_specs=[pl.BlockSpec((tk, D), lambda p, k: (k, 0)), comp(0), comp(1), comp(2)] + [ANY] * len(after),
        out_specs=pl.BlockSpec((2, D, 1536), lambda p, k: (p, 0, 0)),
        out_shape=jax.ShapeDtypeStruct((4, D, 1536), F32),
        compiler_params=_cp(("parallel", "arbitrary")),
    )(h, dz6, dz6, dz6, *after)


def _bwd_out(dx, w_o, merged, l):
    S = dx.shape[0]

    def body(dx_ref, w_ref, m_ref, dm_ref, dw_ref):
        @pl.when(pl.program_id(0) == 0)
        def _():
            dw_ref[...] = jnp.zeros_like(dw_ref)
        dxb = dx_ref[...].astype(BF16)
        dm_ref[...] = _dot_nt(dxb, w_ref[...]).astype(BF16)
        dw_ref[...] += _dot_tn(m_ref[...], dxb)

    tm = TM
    row = pl.BlockSpec((tm, D), lambda i: (i, 0))
    return pl.pallas_call(
        body, name=f"bwd_out_{l}", grid=(S // tm,),
        in_specs=[row, pl.BlockSpec((D, D), lambda i: (0, 0)), row],
        out_specs=[row, pl.BlockSpec((D, D), lambda i: (0, 0))],
        out_shape=[jax.ShapeDtypeStruct((S, D), BF16), jax.ShapeDtypeStruct((D, D), F32)],
        compiler_params=_cp(("arbitrary",)),
    )(dx, w_o, merged)


def _gmlp_bwd(dm, z6, ws_b, wst_b, bs_b, lg, lb, after=()):
    S = z6.shape[1]
    ts = min(GMLP_ROWS, S)

    def body(dm_ref, z_ref, ws_ref, wst_ref, bs_ref, lg_ref, lb_ref, *rest):
        dz_ref, dws_ref, dbs_ref, dlg_ref, dlb_ref, mix, dv = rest[-7:]

        @pl.when(pl.program_id(0) == 0)
        def _():
            dws_ref[...] = jnp.zeros_like(dws_ref)
            dbs_ref[...] = jnp.zeros_like(dbs_ref)
            dlg_ref[...] = jnp.zeros_like(dlg_ref)
            dlb_ref[...] = jnp.zeros_like(dlb_ref)
        for r0 in range(0, ts, CHUNK):
            rows = slice(r0, r0 + CHUNK)
            gv, dgelu_v = _gelu_and_grad(z_ref[1, rows, :].astype(F32))
            xc = gv - jnp.mean(gv, axis=-1, keepdims=True)
            rs = lax.rsqrt(jnp.mean(xc * xc, axis=-1, keepdims=True) + EPS)
            vh = xc * rs
            vb = (vh * lg_ref[...] + lb_ref[...]).astype(BF16)
            for gi in range(NH):
                cs = slice(gi * HD, (gi + 1) * HD)
                mix[rows, cs] = _dot(ws_ref[gi], vb[:, cs])
            u, dgelu_u = _gelu_and_grad(z_ref[0, rows, :].astype(F32))
            sa = _sigmoid(z_ref[2, rows, :].astype(F32))
            dya = dm_ref[rows, :].astype(F32) * sa
            dym = dya * (mix[rows, :] + bs_ref[...])
            dz_ref[2, rows, :] = (dym * u * (1.0 - sa)).astype(BF16)
            dz_ref[0, rows, :] = (dym * dgelu_u).astype(BF16)
            dmix = dya * u
            dmb = dmix.astype(BF16)
            for gi in range(NH):
                cs = slice(gi * HD, (gi + 1) * HD)
                dv[rows, cs] = _dot(wst_ref[gi], dmb[:, cs])
                dws_ref[gi] += _dot_nt(dmb[:, cs], vb[:, cs])
                dbs_ref[gi] += jnp.broadcast_to(jnp.sum(dmix[:, cs], axis=1, keepdims=True), (CHUNK, HD))
            dvv = dv[rows, :]
            dlg_ref[...] += jnp.sum(dvv * vh, axis=0, keepdims=True)
            dlb_ref[...] += jnp.sum(dvv, axis=0, keepdims=True)
            dvh = dvv * lg_ref[...]
            dgv = rs * (dvh - jnp.mean(dvh, axis=-1, keepdims=True) - vh * jnp.mean(dvh * vh, axis=-1, keepdims=True))
            dz_ref[1, rows, :] = (dgv * dgelu_v).astype(BF16)

    vec = pl.BlockSpec((1, D), lambda i: (0, 0))
    mat = pl.BlockSpec((NH, CHUNK, CHUNK), lambda i: (0, 0, 0))
    return pl.pallas_call(
        body, name="gmlp_bwd", grid=(S // ts,),
        in_specs=[pl.BlockSpec((ts, D), lambda i: (i, 0)), pl.BlockSpec((3, ts, D), lambda i: (0, i, 0)), mat, mat,
                  pl.BlockSpec((CHUNK, D), lambda i: (0, 0)), vec, vec] + [ANY] * len(after),
        out_specs=[pl.BlockSpec((3, ts, D), lambda i: (0, i, 0)), mat, mat, vec, vec],
        out_shape=[jax.ShapeDtypeStruct((6, S, D), BF16), jax.ShapeDtypeStruct((NH, CHUNK, CHUNK), F32),
                   jax.ShapeDtypeStruct((NH, CHUNK, HD), F32), jax.ShapeDtypeStruct((1, D), F32), jax.ShapeDtypeStruct((1, D), F32)],
        scratch_shapes=[pltpu.VMEM((ts, D), F32), pltpu.VMEM((ts, D), F32)],
        compiler_params=_cp(("arbitrary",)),
    )(dm, z6, ws_b, wst_b, bs_b, lg, lb, *after)


def _lru_bwd(dz6, dm, z6, h0, h1, cw, cb, wr, br, wi, bi, lam, after=()):
    S = z6.shape[1]
    nt = S // RT

    def body(dz_in, dm_ref, z_ref, h0_ref, h1_ref, cw_ref, cb_ref, wr_ref, br_ref, wi_ref, bi_ref, lam_ref, *rest):
        dz_ref, dcw_ref, dcb_ref, dwr_ref, dbr_ref, dwi_ref, dbi_ref, dlam_ref, zxp, xc_s, dhs_s, dxcp, r_s, lam_s = rest[-14:]
        del dz_in
        lam = lam_ref[...]
        sp = _softplus_neg(lam)
        row = _row_iota()
        _fill_padded(zxp, z_ref.at[0], S)
        _conv_fwd_all(zxp, xc_s, cw_ref, cb_ref, S)
        zeros = jnp.zeros((PADR, HD), F32)
        dxcp[0:PADR, :] = zeros
        dxcp[PADR + S:2 * PADR + S, :] = zeros
        dwr_ref[...] = jnp.zeros_like(dwr_ref)
        dwi_ref[...] = jnp.zeros_like(dwi_ref)

        def pre(i, c):
            rows = pl.ds(pl.multiple_of(i * RT, RT), RT)
            hs = h0_ref[rows, :] + h1_ref[rows, :]
            dmv = dm_ref[rows, :].astype(F32)
            sb = _sigmoid(z_ref[2, rows, :].astype(F32))
            gg, dgg = _gelu_and_grad(z_ref[1, rows, :].astype(F32))
            dz_ref[2, rows, :] = (dmv * hs * gg * sb * (1.0 - sb)).astype(BF16)
            dyb = dmv * sb
            dz_ref[1, rows, :] = (dyb * hs * dgg).astype(BF16)
            dhs_s[rows, :] = dyb * gg
            return c
        lax.fori_loop(0, nt, pre, 0)

        def gate_bwd(d, gates, lamv, da, xc):
            r, gi, a, mult, inv_mult = gates
            lx, lm = lamv * xc, lamv * mult
            dlog_r = (da - (lx * gi) * (a * inv_mult)) * a * r
            dpr = dlog_r * (1.0 - r) * (-LRU_C * sp[d:d + 1, :])
            dpi = (lx * mult) * gi * (1.0 - gi)
            xb, dprb, dpib = xc.astype(BF16), dpr.astype(BF16), dpi.astype(BF16)
            dwr_ref[d] += _dot_tn(xb, dprb)
            dwi_ref[d] += _dot_tn(xb, dpib)
            dxc = lm * gi + _dot_nt(dprb, wr_ref[d]) + _dot_nt(dpib, wi_ref[d])
            return dxc, (jnp.sum(dlog_r, axis=0, keepdims=True) * (-LRU_C), jnp.sum(dpr, axis=0, keepdims=True),
                         jnp.sum(dpi, axis=0, keepdims=True))

        def rgates(i, c):
            for u in range(UNROLL):
                rows = pl.ds(pl.multiple_of((i * UNROLL + u) * RT, RT), RT)
                xb = xc_s[rows, :].astype(BF16)
                for d in range(2):
                    r_s[d, rows, :] = _sigmoid(_dot(xb, wr_ref[d]) + br_ref[d:d + 1, :])
            return c
        lax.fori_loop(0, nt // UNROLL, rgates, 0)

        def chains(i, carry):
            qn, qp = carry
            for u in range(UNROLL):
                j = i * UNROLL + u
                rd = pl.ds(pl.multiple_of((nt - 1 - j) * RT, RT), RT)
                a, dhs = _decay(r_s[0, rd, :], sp[0:1, :])[0], dhs_s[rd, :]
                q, q_first = _scan_down(a, a * dhs, qn)
                lam_s[0, rd, :] = dhs + jnp.where(row == RT - 1, qn, pltpu.roll(q, RT - 1, 0))
                qn = q_first
                ru = pl.ds(pl.multiple_of(j * RT, RT), RT)
                a, dhs = _decay(r_s[1, ru, :], sp[1:2, :])[0], dhs_s[ru, :]
                q, q_last = _scan_up(a, a * dhs, qp)
                lam_s[1, ru, :] = dhs + jnp.where(row == 0, qp, pltpu.roll(q, 1, 0))
                qp = q_last
            return qn, qp

        z1 = jnp.zeros((1, HD), F32)
        lax.fori_loop(0, nt // UNROLL, chains, (z1, z1))

        ct = min(GRAD_ROWS, S)
        crow = lax.broadcasted_iota(jnp.int32, (ct, HD), 0)

        def tile_grads(i, acc):
            t0 = pl.multiple_of(i * ct, ct)
            rows = pl.ds(t0, ct)
            xc = xc_s[rows, :]
            xb = xc.astype(BF16)
            tp = pl.multiple_of(jnp.maximum(t0 - PADR, 0), PADR)
            prev = jnp.where(t0 > 0, h0_ref[pl.ds(tp, PADR), :][PADR - 1:PADR, :], 0.0)
            tn = pl.multiple_of(jnp.minimum(t0 + ct, S - PADR), PADR)
            nxt = jnp.where(t0 + ct < S, h1_ref[pl.ds(tn, PADR), :][0:1, :], 0.0)
            hside = (jnp.where(crow == 0, prev, pltpu.roll(h0_ref[rows, :], 1, 0)),
                     jnp.where(crow == ct - 1, nxt, pltpu.roll(h1_ref[rows, :], ct - 1, 0)))
            dxc, sums = 0.0, ()
            for d in range(2):
                r = r_s[d, rows, :]
                gi = _sigmoid(_dot(xb, wi_ref[d]) + bi_ref[d:d + 1, :])
                lamv = lam_s[d, rows, :]
                dxc_d, s_d = gate_bwd(d, (r, gi) + _decay_bwd(r, sp[d:d + 1, :]), lamv, lamv * hside[d], xc)
                dxc = dxc + dxc_d
                sums = sums + s_d
            dxcp[pl.ds(t0 + PADR, ct), :] = dxc
            return tuple(x + y for x, y in zip(acc, sums))

        s_sp0, s_br0, s_bi0, s_sp1, s_br1, s_bi1 = lax.fori_loop(0, S // ct, tile_grads, (z1,) * 6)

        dsp = jnp.concatenate([s_sp0, s_sp1], axis=0)
        dlam_ref[...] = -dsp * _sigmoid(-lam)
        dbr_ref[...] = jnp.concatenate([s_br0, s_br1], axis=0)
        dbi_ref[...] = jnp.concatenate([s_bi0, s_bi1], axis=0)

        def conv_bwd(i, carry):
            c0, c1, c2, c3, cb_ = carry
            t0 = pl.multiple_of(i * RT, RT)
            dwin = dxcp[pl.ds(t0, RT + 2 * PADR), :]
            d0 = _shifted(dwin, 0)
            dz_ref[0, pl.ds(t0, RT), :] = (_shifted(dwin, 1) * cw_ref[0:1, :] + d0 * cw_ref[1:2, :]
                                           + _shifted(dwin, -1) * cw_ref[2:3, :] + _shifted(dwin, -2) * cw_ref[3:4, :]).astype(BF16)
            xm1, x0, xp1, xp2 = _conv_taps(zxp[pl.ds(t0, RT + 2 * PADR), :])
            sm = lambda v: jnp.sum(v, axis=0, keepdims=True)
            return c0 + sm(d0 * xm1), c1 + sm(d0 * x0), c2 + sm(d0 * xp1), c3 + sm(d0 * xp2), cb_ + sm(d0)

        c0, c1, c2, c3, cb_ = lax.fori_loop(0, nt, conv_bwd, (z1, z1, z1, z1, z1))
        dcw_ref[...] = jnp.concatenate([c0, c1, c2, c3], axis=0)
        dcb_ref[...] = cb_

    col = pl.BlockSpec((S, HD), lambda h: (0, h))
    head = lambda h: (0, h)
    wspec = pl.BlockSpec((2, None, HD, HD), lambda h: (0, h, 0, 0))
    return pl.pallas_call(
        body, name="lru_bwd", grid=(NH,),
        in_specs=[pl.BlockSpec(memory_space=pl.ANY), col, pl.BlockSpec((3, S, HD), lambda h: (1, 0, h)), col, col] + _lru_specs(S)
        + [ANY] * len(after),
        out_specs=[pl.BlockSpec((3, S, HD), lambda h: (1, 0, h)), pl.BlockSpec((4, HD), head), pl.BlockSpec((1, HD), head),
                   wspec, pl.BlockSpec((2, HD), head), wspec, pl.BlockSpec((2, HD), head), pl.BlockSpec((2, HD), head)],
        out_shape=[jax.ShapeDtypeStruct((6, S, D), BF16), jax.ShapeDtypeStruct((4, D), F32), jax.ShapeDtypeStruct((1, D), F32),
                   jax.ShapeDtypeStruct((2, NH, HD, HD), F32), jax.ShapeDtypeStruct((2, D), F32),
                   jax.ShapeDtypeStruct((2, NH, HD, HD), F32), jax.ShapeDtypeStruct((2, D), F32), jax.ShapeDtypeStruct((2, D), F32)],
        scratch_shapes=[pltpu.VMEM((S + 2 * PADR, HD), F32), pltpu.VMEM((S, HD), F32), pltpu.VMEM((S, HD), F32),
                        pltpu.VMEM((S + 2 * PADR, HD), F32), pltpu.VMEM((2, S, HD), F32), pltpu.VMEM((2, S, HD), F32)],
        input_output_aliases={0: 0},
        compiler_params=_cp(("parallel",)),
    )(dz6, dm, z6, h0, h1, cw, cb, wr, br, wi, bi, lam, *after)


LAYER_SMALL = ("norm1_g", "gmlp_ln_g", "gmlp_ln_b", "gmlp_w_s", "gmlp_b_s", "conv_w", "conv_b",
               "lru_w_r", "lru_b_r", "lru_w_i", "lru_b_i", "lru_lambda", "norm2_g")


def _layer_operands(l, p):
    ws_b = p["gmlp_w_s"][l].astype(BF16)
    tm = dict(ws_b=ws_b, wst_b=jnp.swapaxes(ws_b, 1, 2), bs_b=jnp.repeat(p["gmlp_b_s"][l].T, HD, axis=1),
              lg=p["gmlp_ln_g"][l][None], lb=p["gmlp_ln_b"][l][None])
    lru = (p["conv_w"][l], p["conv_b"][l][None], p["lru_w_r"][l].astype(BF16), p["lru_b_r"][l],
           p["lru_w_i"][l].astype(BF16), p["lru_b_i"][l], p["lru_lambda"][l])
    return (p["norm1_g"][l][None], p["norm2_g"][l][None]), tm, lru


def _forward_layer(l, x, p, wb, after=(), early=None, rest=None, near_end=None, operands=None, loss=None):
    (g1, g2), tm, lru = _layer_operands(l, p) if operands is None else operands
    z6, hn1 = _mm_in(x, g1, wb["w_in"], l, after)
    ya = _gmlp_fwd(z6, tm["ws_b"], tm["bs_b"], tm["lg"], tm["lb"])
    merged, h0, h1 = _lru_fwd(z6, ya, *lru, after=() if early is None else tuple(early(ya)))
    if rest is not None:
        wb = dict(wb, **rest(merged))
    x1 = _mm_res(merged, wb["w_out"], x, l, "mm_out")
    gu, ff, hn2 = _mm_ffn_in(x1, g2, wb["w_ffn_in"], l)
    if loss is None:
        x2 = _mm_res(ff, wb["w_ffn_out"], x1, l, "mm_ffn_out", () if near_end is None else tuple(near_end(gu)))
    else:
        x2 = _mm_res_loss(ff, wb["w_ffn_out"], x1, *loss)
    return x2, dict(x=x, z6=z6, h0=h0, h1=h1, merged=merged, x1=x1, gu=gu, ff=ff, g1=g1, g2=g2, tm=tm, lru=lru,
                    hn1=hn1, hn2=hn2, wb=wb)


def _backward_layer(l, dx, s, after=(), midway=None, midway2=None, midway3=None, late=None):
    S = dx.shape[0]
    tm, wb = s["tm"], s["wb"]
    g2 = s["g2"]
    dgu = _bwd_ffn_out(dx, wb["w_ffn_out"], s["gu"], l, after)
    tmb = min(TM_BIG, S)
    dwfo = _mm_tn(s["ff"], dx, DFF_SH, tmb, f"dw_ffn_out_{l}")
    dx1, dg2 = _mm_nt_rms_bwd(
        dgu, [pl.BlockSpec((None, tmb, DFF_SH), lambda i, k: (k, i, 0))],
        wb["w_ffn_in"], [pl.BlockSpec((None, D, DFF_SH), lambda i, k: (k, 0, 0))],
        4, tmb, s["x1"], g2, dx, f"bwd_ffn_in_{l}")
    dwfi = _dw_ffn_in(s["hn2"], dgu, l)
    dmg, dwo = _bwd_out(dx1, wb["w_out"], s["merged"], l)
    mid = () if midway is None else tuple(midway([dwo, dwfi, dwfo]))
    dz6, dws, dbs, dlg, dlb = _gmlp_bwd(dmg, s["z6"], tm["ws_b"], tm["wst_b"], tm["bs_b"], tm["lg"], tm["lb"], mid)
    mid2 = () if midway2 is None else tuple(midway2(dws))
    dz6, dcw, dcb, dwr, dbr, dwi, dbi, dlam = _lru_bwd(dz6, dmg, s["z6"], s["h0"], s["h1"], *s["lru"], after=mid2)

    sub = 3

    def dz_tile(j):
        return pl.BlockSpec((None, tmb, 512), lambda i, k: ((sub * k + j) // 2, i, (sub * k + j) % 2))

    def w_tile(j):
        def w_map(i, k):
            sh, tl = _in_tile(sub * k + j)
            return (sh, 0, tl)
        return pl.BlockSpec((None, D, 512), w_map)

    small = dict(gmlp_ln_g=dlg[0], gmlp_ln_b=dlb[0], gmlp_w_s=dws, gmlp_b_s=dbs[:, :, 0], conv_w=dcw, conv_b=dcb[0],
                 lru_w_r=dwr, lru_b_r=dbr, lru_w_i=dwi, lru_b_i=dbi, lru_lambda=dlam, norm2_g=dg2[0])
    mid3 = () if midway3 is None else tuple(midway3(small))
    dwin = _dw_in(s["hn1"], dz6, l, mid3)
    tail = () if late is None else tuple(late([dwin]))
    dx0, dg1 = _mm_nt_rms_bwd(
        dz6, [dz_tile(j) for j in range(sub)], wb["w_in"], [w_tile(j) for j in range(sub)],
        N_IN_T // sub, tmb, s["x"], s["g1"], dx1, f"bwd_in_{l}", tail)
    return dx0, [dwin, dwo, dwfi, dwfo], dict(small, norm1_g=dg1[0])


def _local_step(x, tgt, p, wbs):
    saved = []
    for l in range(2):
        x, s = _forward_layer(l, x, p, wbs[l], loss=(tgt, p["final_g"][None]) if l else None)
        saved.append(s)
    dx, loss_v, dfg = x
    big, smalls = [None, None], [None, None]
    for l in (1, 0):
        dx, big[l], smalls[l] = _backward_layer(l, dx, saved[l])
    small = {k: jnp.stack([smalls[0][k], smalls[1][k]]) for k in LAYER_SMALL}
    small["final_g"] = dfg[0]
    return loss_v, dx, big, small


def _place():
    x, y, c = lax.axis_index("x"), lax.axis_index("y"), lax.axis_index("c")
    return x, y, c, 2 * x + y


def _chip_at(x, y, d):
    px = 1 - x if d & 2 else x
    py = 1 - y if d & 1 else y
    return px, py, 2 * px + py


HBM = pl.BlockSpec(memory_space=pltpu.HBM)
SEM = pl.BlockSpec(memory_space=pltpu.SEMAPHORE)
DATAFLOW = pltpu.SideEffectType.DATAFLOW_SIDE_EFFECTING


def _in_hbm(a):
    return pltpu.with_memory_space_constraint(a, pltpu.HBM)


def _cast_into(wfs, l, chip_arr, name):
    n = len(wfs)

    def body(ch_ref, *refs):
        for w_ref, o_ref in zip(refs[:n], refs[n:]):
            o_ref[...] = w_ref[...].astype(BF16)

    halves = [(wf.shape[1] // 2, wf.shape[2]) for wf in wfs]
    return pl.pallas_call(
        body, name=name, out_shape=[jax.ShapeDtypeStruct((4, 2, rh, cols), BF16) for rh, cols in halves],
        grid_spec=pltpu.PrefetchScalarGridSpec(
            num_scalar_prefetch=1, grid=(2,),
            in_specs=[pl.BlockSpec((None, None, rh, cols), lambda h, ch: (l, h, 0, 0)) for rh, cols in halves],
            out_specs=[pl.BlockSpec((None, None, rh, cols), lambda h, ch: (ch[0], h, 0, 0)) for rh, cols in halves]),
        compiler_params=_cp(("parallel",)),
    )(chip_arr, *[wf.reshape(2, 2, rh, cols) for wf, (rh, cols) in zip(wfs, halves)])


def _half_block(ref, chip, half, to, send_sem, recv_sem):
    blk = ref.at[chip, half]
    return pltpu.make_async_remote_copy(src_ref=blk, dst_ref=blk, send_sem=send_sem, recv_sem=recv_sem,
                                        device_id=to, device_id_type=MESH)


def _gather_weights(bufs, tiny):
    nt = len(bufs)
    n_ici = max(nt * 3, 1)

    def body(*refs):
        tiny_ref = refs[nt]
        o_refs, tiny_o = refs[nt + 1:2 * nt + 1], refs[2 * nt + 1]
        send, recv, fsend, frecv, tsend, trecv, lsem = refs[2 * nt + 2:]
        x, y, c, chip = _place()
        local = pltpu.make_async_copy(tiny_ref, tiny_o.at[chip], lsem)
        local.start()

        def tin(d, origin_chip, to):
            return pltpu.make_async_remote_copy(
                src_ref=tiny_ref, dst_ref=tiny_o.at[origin_chip], send_sem=tsend.at[d - 1], recv_sem=trecv.at[d - 1],
                device_id=to, device_id_type=MESH)

        sends = []
        for t in range(nt):
            for d in (1, 2, 3):
                px, py, _ = _chip_at(x, y, d)
                sends.append(_half_block(o_refs[t], chip, c, (px, py, c), send.at[3 * t + d - 1], recv.at[3 * t + d - 1]))
        for d in (1, 2, 3):
            px, py, _ = _chip_at(x, y, d)
            sends.append(tin(d, chip, (px, py, c)))
        for cp in sends:
            cp.start()
        passed = []
        for t in range(nt):
            for d in (1, 2, 3):
                k = 3 * t + d - 1
                _, _, pchip = _chip_at(x, y, d)
                _half_block(o_refs[t], pchip, c, (x, y, c), send.at[k], recv.at[k]).wait_recv()
                f = _half_block(o_refs[t], pchip, c, (x, y, 1 - c), fsend.at[k], frecv.at[k])
                f.start()
                passed.append(f)
        for t in range(nt):
            for d in (1, 2, 3):
                k = 3 * t + d - 1
                _, _, pchip = _chip_at(x, y, d)
                _half_block(o_refs[t], pchip, 1 - c, (x, y, 1 - c), fsend.at[k], frecv.at[k]).wait_recv()
        for d in (1, 2, 3):
            _, _, pchip = _chip_at(x, y, d)
            tin(d, pchip, (x, y, c)).wait_recv()
        for cp in sends + passed:
            cp.wait_send()
        local.wait()

    out_shape = [jax.ShapeDtypeStruct(b.shape, b.dtype) for b in bufs]
    out_shape.append(jax.ShapeDtypeStruct((4,) + tiny.shape, tiny.dtype))
    outs = pl.pallas_call(
        body, name="gather_weights_0", out_shape=out_shape,
        in_specs=[ANY] * (nt + 1), out_specs=[ANY] * (nt + 1),
        scratch_shapes=[pltpu.SemaphoreType.DMA((n_ici,)), pltpu.SemaphoreType.DMA((n_ici,)),
                        pltpu.SemaphoreType.DMA((n_ici,)), pltpu.SemaphoreType.DMA((n_ici,)),
                        pltpu.SemaphoreType.DMA((3,)), pltpu.SemaphoreType.DMA((3,)), pltpu.SemaphoreType.DMA],
        input_output_aliases={t: t for t in range(nt)},
        compiler_params=_cp(has_side_effects=True),
    )(*bufs, tiny)
    return outs[:nt], outs[nt]


def _gather_start(bufs, tag, after=()):
    nt, na = len(bufs), len(after)

    def body(*refs):
        b_refs = refs[:nt]
        send, recv = refs[nt + na], refs[nt + na + 1]
        token = refs[2 * nt + na + 2]
        x, y, c, chip = _place()
        for t in range(nt):
            for d in (1, 2, 3):
                px, py, _ = _chip_at(x, y, d)
                _half_block(b_refs[t], chip, c, (px, py, c), send.at[3 * t + d - 1], recv.at[3 * t + d - 1]).start()
        token[...] = jnp.zeros_like(token)

    outs = pl.pallas_call(
        body, name=f"gather_start_{tag}",
        out_shape=(pltpu.SemaphoreType.DMA((3 * nt,)), pltpu.SemaphoreType.DMA((3 * nt,)),
                   *[pltpu.HBM(b.shape, b.dtype) for b in bufs], jax.ShapeDtypeStruct((8, 128), F32)),
        in_specs=[HBM] * nt + [ANY] * na, out_specs=(SEM, SEM, *[HBM] * nt, pl.BlockSpec(memory_space=pltpu.VMEM)),
        input_output_aliases={t: 2 + t for t in range(nt)},
        compiler_params=pltpu.CompilerParams(has_side_effects=DATAFLOW),
    )(*[_in_hbm(b) for b in bufs], *after)
    return outs[0], outs[1], list(outs[2:2 + nt]), outs[2 + nt]


def _gather_wait(send, recv, bufs, after, tag):
    nt = len(bufs)

    def body(*refs):
        b_refs = refs[:nt]
        send_ref, recv_ref = refs[nt], refs[nt + 1]
        x, y, c, chip = _place()
        for t in range(nt):
            for d in (1, 2, 3):
                k = 3 * t + d - 1
                px, py, pchip = _chip_at(x, y, d)
                _half_block(b_refs[t], chip, c, (px, py, c), send_ref.at[k], recv_ref.at[k]).wait_send()
                _half_block(b_refs[t], pchip, c, (px, py, c), send_ref.at[k], recv_ref.at[k]).wait_recv()

    after = tuple(after) if isinstance(after, (tuple, list)) else (after,)
    outs = pl.pallas_call(
        body, name=f"gather_wait_{tag}", out_shape=[pltpu.HBM(b.shape, b.dtype) for b in bufs],
        in_specs=[HBM] * nt + [SEM, SEM] + [ANY] * len(after), out_specs=[HBM] * nt,
        input_output_aliases={t: t for t in range(nt)},
        compiler_params=pltpu.CompilerParams(has_side_effects=DATAFLOW),
    )(*bufs, send, recv, *after)
    return list(outs)


def _gather_pass_on(bufs, tag):
    nt = len(bufs)

    def body(*refs):
        o_refs = refs[nt:2 * nt]
        fsend, frecv = refs[2 * nt:]
        x, y, c, _ = _place()
        cps = []
        for t in range(nt):
            for d in (1, 2, 3):
                k = 3 * t + d - 1
                _, _, pchip = _chip_at(x, y, d)
                cps.append(_half_block(o_refs[t], pchip, c, (x, y, 1 - c), fsend.at[k], frecv.at[k]))
        for cp in cps:
            cp.start()
        for t in range(nt):
            for d in (1, 2, 3):
                k = 3 * t + d - 1
                _, _, pchip = _chip_at(x, y, d)
                _half_block(o_refs[t], pchip, 1 - c, (x, y, 1 - c), fsend.at[k], frecv.at[k]).wait_recv()
        for cp in cps:
            cp.wait_send()

    return pl.pallas_call(
        body, name=f"gather_pass_on_{tag}", out_shape=[jax.ShapeDtypeStruct(b.shape, b.dtype) for b in bufs],
        in_specs=[ANY] * nt, out_specs=[ANY] * nt,
        scratch_shapes=[pltpu.SemaphoreType.DMA((3 * nt,)), pltpu.SemaphoreType.DMA((3 * nt,))],
        input_output_aliases={t: t for t in range(nt)},
        compiler_params=_cp(has_side_effects=True),
    )(*bufs)


def _chip_copy(c_ref, land_ref, x, y, c, d, send_sem, recv_sem):
    px, py, pchip = _chip_at(x, y, d)
    return pltpu.make_async_remote_copy(src_ref=c_ref.at[pchip], dst_ref=land_ref.at[d - 1], send_sem=send_sem, recv_sem=recv_sem,
                                        device_id=(px, py, c), device_id_type=MESH)


def _exchange_start(srcs, lands, copies, nsem, name):
    ns, n = len(srcs), len(srcs) + len(lands)

    def body(*refs):
        for cp in copies(refs[:ns], refs[ns:n], refs[n], refs[n + 1]):
            cp.start()
        token = refs[2 * n + 2]
        token[...] = jnp.zeros_like(token)

    outs = pl.pallas_call(
        body, name=name,
        out_shape=(pltpu.SemaphoreType.DMA((nsem,)), pltpu.SemaphoreType.DMA((nsem,)),
                   *[pltpu.HBM(a.shape, a.dtype) for a in list(srcs) + list(lands)], jax.ShapeDtypeStruct((8, 128), F32)),
        in_specs=[HBM] * n, out_specs=(SEM, SEM, *[HBM] * n, pl.BlockSpec(memory_space=pltpu.VMEM)),
        input_output_aliases={i: 2 + i for i in range(n)},
        compiler_params=pltpu.CompilerParams(has_side_effects=DATAFLOW),
    )(*[_in_hbm(a) for a in list(srcs) + list(lands)])
    return outs[0], outs[1], list(outs[2:2 + ns]), list(outs[2 + ns:2 + n]), outs[2 + n]


def _exchange_wait(send, recv, srcs, lands, after, copies, name):
    ns, n = len(srcs), len(srcs) + len(lands)

    def body(*refs):
        for cp in copies(refs[:ns], refs[ns:n], refs[n], refs[n + 1]):
            cp.wait_send()
            cp.wait_recv()

    outs = pl.pallas_call(
        body, name=name, out_shape=[pltpu.HBM(a.shape, a.dtype) for a in list(srcs) + list(lands)],
        in_specs=[HBM] * n + [SEM, SEM, ANY], out_specs=[HBM] * n,
        input_output_aliases={i: i for i in range(n)},
        compiler_params=pltpu.CompilerParams(has_side_effects=DATAFLOW),
    )(*srcs, *lands, send, recv, after)
    return list(outs[:ns]), list(outs[ns:])


def _pass_on_copies(b_refs, land_refs, send, recv):
    del land_refs
    x, y, c, _ = _place()
    return [_half_block(b_refs[t], _chip_at(x, y, d)[2], c, (x, y, 1 - c), send.at[3 * t + d - 1], recv.at[3 * t + d - 1])
            for t in range(len(b_refs)) for d in (1, 2, 3)]


def _chips_copies(c_refs, land_refs, send, recv):
    x, y, c, _ = _place()
    return [_chip_copy(c_refs[t], land_refs[t], x, y, c, d, send.at[3 * t + d - 1], recv.at[3 * t + d - 1])
            for t in range(len(c_refs)) for d in (1, 2, 3)]


def _sibling_copies(g_refs, land_refs, send, recv):
    x, y, c, _ = _place()
    return [pltpu.make_async_remote_copy(
        src_ref=g_refs[t].at[k, 1 - c], dst_ref=land_refs[t].at[k], send_sem=send.at[4 * t + k], recv_sem=recv.at[4 * t + k],
        device_id=(x, y, 1 - c), device_id_type=MESH) for t in range(len(g_refs)) for k in range(4)]


def _join_copies(f_refs, land_refs, send, recv):
    del land_refs
    x, y, c, _ = _place()
    return [pltpu.make_async_remote_copy(
        src_ref=f_refs[t].at[c], dst_ref=f_refs[t].at[c], send_sem=send.at[t], recv_sem=recv.at[t],
        device_id=(x, y, 1 - c), device_id_type=MESH) for t in range(len(f_refs))]


def _add_half(gs, rs, c_arr, name):
    n = len(gs)

    def body(c_ref, *refs):
        for g_ref, r_ref, o_ref in zip(refs[:n], refs[n:2 * n], refs[2 * n:]):
            o_ref[...] = (g_ref[...] + r_ref[...]).astype(BF16)

    def own(g):
        return pl.BlockSpec((None, None) + g.shape[2:], lambda k, cr: (k, cr[0], 0, 0))

    def blk(g):
        return pl.BlockSpec((None,) + g.shape[2:], lambda k, cr: (k, 0, 0))

    return pl.pallas_call(
        body, name=name, out_shape=[jax.ShapeDtypeStruct((4,) + g.shape[2:], BF16) for g in gs],
        grid_spec=pltpu.PrefetchScalarGridSpec(
            num_scalar_prefetch=1, grid=(4,),
            in_specs=[own(g) for g in gs] + [blk(g) for g in gs], out_specs=[blk(g) for g in gs]),
        compiler_params=_cp(("parallel",)),
    )(c_arr, *gs, *rs)


def _sum_chips(css, r3s, place_arr, name):
    n = len(css)

    def body(pl_ref, *refs):
        up = lambda ref: ref[...].astype(F32)
        for t in range(n):
            a_ref, (r0_ref, r1_ref, r2_ref), o_ref = refs[t], refs[n + 3 * t:n + 3 * t + 3], refs[4 * n + t]
            o_ref[...] = ((up(a_ref) + up(r0_ref)) + up(r1_ref)) + up(r2_ref)

    def blk(cs, first):
        _, rh, cols = cs.shape
        return pl.BlockSpec((None, rh // 2, cols), lambda i, pa: (first(pa), i, 0))

    in_specs = [blk(cs, lambda pa: pa[0]) for cs in css]
    for cs in css:
        in_specs += [blk(cs, lambda pa, d=d: d) for d in range(3)]
    return pl.pallas_call(
        body, name=name, out_shape=[jax.ShapeDtypeStruct((2,) + cs.shape[1:], F32) for cs in css],
        grid_spec=pltpu.PrefetchScalarGridSpec(
            num_scalar_prefetch=1, grid=(2,), in_specs=in_specs, out_specs=[blk(cs, lambda pa: pa[1]) for cs in css]),
        compiler_params=_cp(("parallel",)),
    )(place_arr, *css, *[r3 for r3 in r3s for _ in range(3)])


def _allreduce_small(pack):
    rows = pack.shape[0]
    hr = rows // 2

    def body(p_ref, o_ref, sib, slots, s1, r1, s2, r2, s3, r3):
        x, y, c, chip = _place()
        sibling = (x, y, 1 - c)
        ex = pltpu.make_async_remote_copy(src_ref=p_ref, dst_ref=sib, send_sem=s1, recv_sem=r1,
                                          device_id=sibling, device_id_type=MESH)
        ex.start()
        ex.wait()
        half = pl.ds(pl.multiple_of(c * hr, 16), hr)
        slots[0] = p_ref[half, :] + sib[half, :]
        cps = []
        for d in (1, 2, 3):
            px, py, _ = _chip_at(x, y, d)
            cps.append(pltpu.make_async_remote_copy(
                src_ref=slots.at[0], dst_ref=slots.at[d], send_sem=s2.at[d - 1], recv_sem=r2.at[d - 1],
                device_id=(px, py, c), device_id_type=MESH))
        for cp in cps:
            cp.start()
        for cp in cps:
            cp.wait()
        tot = slots[chip]
        for k in (1, 2, 3):
            tot = tot + slots[jnp.bitwise_xor(chip, k)]
        o_ref[half, :] = tot
        back = pltpu.make_async_remote_copy(src_ref=o_ref.at[half, :], dst_ref=o_ref.at[half, :], send_sem=s3, recv_sem=r3,
                                            device_id=sibling, device_id_type=MESH)
        back.start()
        back.wait()

    vm = pl.BlockSpec(memory_space=pltpu.VMEM)
    return pl.pallas_call(
        body, name="allreduce_small", out_shape=jax.ShapeDtypeStruct((rows, 128), F32),
        in_specs=[vm], out_specs=vm,
        scratch_shapes=[pltpu.VMEM((rows, 128), F32), pltpu.VMEM((4, hr, 128), F32),
                        pltpu.SemaphoreType.DMA, pltpu.SemaphoreType.DMA, pltpu.SemaphoreType.DMA((3,)), pltpu.SemaphoreType.DMA((3,)),
                        pltpu.SemaphoreType.DMA, pltpu.SemaphoreType.DMA],
        compiler_params=_cp(has_side_effects=True),
    )(pack)


def _small_chip_sum(pack, after=()):
    rows = pack.shape[0]
    hr = rows // 2

    def body(p_ref, *rest):
        o_ref, sib, s1, r1 = rest[-4:]
        x, y, c, _ = _place()
        ex = pltpu.make_async_remote_copy(src_ref=p_ref, dst_ref=sib, send_sem=s1, recv_sem=r1,
                                          device_id=(x, y, 1 - c), device_id_type=MESH)
        ex.start()
        ex.wait()
        half = pl.ds(pl.multiple_of(c * hr, 16), hr)
        o_ref[...] = (p_ref[half, :] + sib[half, :]).astype(BF16)

    vm = pl.BlockSpec(memory_space=pltpu.VMEM)
    return pl.pallas_call(
        body, name="small_chip_sum", out_shape=jax.ShapeDtypeStruct((hr, 128), BF16),
        in_specs=[vm] + [ANY] * len(after), out_specs=vm,
        scratch_shapes=[pltpu.VMEM((rows, 128), F32), pltpu.SemaphoreType.DMA, pltpu.SemaphoreType.DMA],
        compiler_params=_cp(has_side_effects=True),
    )(pack, *after)


def _small_copies(c_refs, land_refs, send, recv):
    x, y, c, _ = _place()
    cps = []
    for d in (1, 2, 3):
        px, py, _ = _chip_at(x, y, d)
        cps.append(pltpu.make_async_remote_copy(src_ref=c_refs[0], dst_ref=land_refs[0].at[d - 1], send_sem=send.at[d - 1],
                                                recv_sem=recv.at[d - 1], device_id=(px, py, c), device_id_type=MESH))
    return cps


def _small_total(csum, land):
    hr = csum.shape[0]

    def body(c_ref, l_ref, o_ref, slots, s3, r3):
        x, y, c, chip = _place()
        slots[0] = c_ref[...]
        for d in (1, 2, 3):
            slots[d] = l_ref[d - 1]
        tot = slots[chip].astype(F32)
        for k in (1, 2, 3):
            tot = tot + slots[jnp.bitwise_xor(chip, k)].astype(F32)
        half = pl.ds(pl.multiple_of(c * hr, 16), hr)
        o_ref[half, :] = tot
        back = pltpu.make_async_remote_copy(src_ref=o_ref.at[half, :], dst_ref=o_ref.at[half, :], send_sem=s3, recv_sem=r3,
                                            device_id=(x, y, 1 - c), device_id_type=MESH)
        back.start()
        back.wait()

    vm = pl.BlockSpec(memory_space=pltpu.VMEM)
    return pl.pallas_call(
        body, name="small_total", out_shape=jax.ShapeDtypeStruct((2 * hr, 128), F32), in_specs=[vm, vm], out_specs=vm,
        scratch_shapes=[pltpu.VMEM((4, hr, 128), BF16), pltpu.SemaphoreType.DMA, pltpu.SemaphoreType.DMA],
        compiler_params=_cp(has_side_effects=True),
    )(csum, land)


def _adam_math(gv, wv, mv, vv):
    m2 = ADAM_B1 * mv + (1.0 - ADAM_B1) * gv
    v2 = ADAM_B2 * vv + (1.0 - ADAM_B2) * (gv * gv)
    m_hat = m2 / (1.0 - ADAM_B1 ** ADAM_STEP)
    v_hat = v2 / (1.0 - ADAM_B2 ** ADAM_STEP)
    return -ADAM_LR * (m_hat / (jnp.sqrt(v_hat) + ADAM_EPS) + ADAM_WD * wv), m2, v2


def _adam(g, w, m, v, name):
    rows, cols = g.shape
    rb = rows // 4

    def body(g_ref, w_ref, m_ref, v_ref, d_ref, m2_ref, v2_ref):
        d_ref[...], m2_ref[...], v2_ref[...] = _adam_math(g_ref[...], w_ref[...], m_ref[...], v_ref[...])

    blk = pl.BlockSpec((rb, cols), lambda i: (i, 0))
    shp = jax.ShapeDtypeStruct((rows, cols), F32)
    return pl.pallas_call(
        body, name=name, grid=(4,), in_specs=[blk] * 4, out_specs=[blk] * 3, out_shape=[shp] * 3,
        compiler_params=_cp(("parallel",)),
    )(g, w, m, v)


def _adam_layer(gs, ws, ms, vs, l, prevs, name):
    n = len(gs)
    prev = [a for p4 in prevs if p4 is not None for a in p4]

    def body(*refs):
        outs = refs[len(refs) - 4 * n:]
        for t in range(n):
            g_ref, w_ref, m_ref, v_ref = refs[4 * t:4 * t + 4]
            go_ref, d_ref, m2_ref, v2_ref = outs[4 * t:4 * t + 4]
            gv = g_ref[...]
            go_ref[...] = gv
            d_ref[...], m2_ref[...], v2_ref[...] = _adam_math(gv, w_ref[...], m_ref[...], v_ref[...])

    in_specs, out_specs, out_shape, operands, aliases = [], [], [], [], {}
    for t, g in enumerate(gs):
        rows, cols = g.shape
        lay = pl.BlockSpec((None, rows // 4, cols), lambda i: (l, i, 0))
        in_specs += [pl.BlockSpec((rows // 4, cols), lambda i: (i, 0)), lay, lay, lay]
        operands += [g, ws[t], ms[t], vs[t]]
        out_specs += [lay] * 4
        out_shape += [jax.ShapeDtypeStruct((2, rows, cols), F32)] * 4
    k = 4 * n
    for t, p4 in enumerate(prevs):
        if p4 is not None:
            for j in range(4):
                aliases[k] = 4 * t + j
                k += 1
    outs = pl.pallas_call(
        body, name=name, grid=(4,), in_specs=in_specs + [ANY] * len(prev), out_specs=out_specs, out_shape=out_shape,
        input_output_aliases=aliases, compiler_params=_cp(("parallel",)),
    )(*operands, *prev)
    return [list(outs[4 * t:4 * t + 4]) for t in range(n)]


def _rows128(a):
    return a.reshape(-1, 128)


def _pack(arrs, mult):
    parts = [_rows128(a) for a in arrs]
    rows = sum(q.shape[0] for q in parts)
    pad = -rows % mult
    if pad:
        parts.append(jnp.zeros((pad, 128), F32))
    return jnp.concatenate(parts, axis=0)


def _unpack(pack, shapes):
    out, o = [], 0
    for s in shapes:
        n = 1
        for e in s:
            n *= e
        out.append(pack[o:o + n // 128].reshape(s))
        o += n // 128
    return out


WEIGHTS = ['norm1_g', 'w_in', 'gmlp_ln_g', 'gmlp_ln_b', 'gmlp_w_s', 'gmlp_b_s', 'conv_w', 'conv_b', 'lru_w_r', 'lru_b_r', 'lru_w_i',
           'lru_b_i', 'lru_lambda', 'w_out', 'norm2_g', 'w_ffn_in', 'w_ffn_out', 'final_g']
BIG = ['w_in', 'w_out', 'w_ffn_in', 'w_ffn_out']
SMALL = [n for n in WEIGHTS if n not in BIG]
CHIP_SHARDED_SMALL = ['conv_w', 'lru_b_r', 'lru_b_i', 'lru_lambda']


def kernel(x, norm1_g, w_in, gmlp_ln_g, gmlp_ln_b, gmlp_w_s, gmlp_b_s, conv_w, conv_b, lru_w_r, lru_b_r, lru_w_i, lru_b_i, lru_lambda, w_out, norm2_g, w_ffn_in, w_ffn_out, final_g, loss_target, m_norm1_g, m_w_in, m_gmlp_ln_g, m_gmlp_ln_b, m_gmlp_w_s, m_gmlp_b_s, m_conv_w, m_conv_b, m_lru_w_r, m_lru_b_r, m_lru_w_i, m_lru_b_i, m_lru_lambda, m_w_out, m_norm2_g, m_w_ffn_in, m_w_ffn_out, m_final_g, v_norm1_g, v_w_in, v_gmlp_ln_g, v_gmlp_ln_b, v_gmlp_w_s, v_gmlp_b_s, v_conv_w, v_conv_b, v_lru_w_r, v_lru_b_r, v_lru_w_i, v_lru_b_i, v_lru_lambda, v_w_out, v_norm2_g, v_w_ffn_in, v_w_ffn_out, v_final_g):
    a = dict(locals())
    w = {n: a[n] for n in WEIGHTS}
    mom = {n: a["m_" + n] for n in WEIGHTS}
    var = {n: a["v_" + n] for n in WEIGHTS}
    _, _, c, chip = _place()
    c_arr, chip_arr = jnp.reshape(c, (1,)).astype(jnp.int32), jnp.reshape(chip, (1,)).astype(jnp.int32)
    place_arr = jnp.stack([chip, c]).astype(jnp.int32)

    first, rest = BIG[:1], BIG[1:]

    def as_weights(names, full):
        wb = {n: f.reshape(4, 2 * f.shape[2], f.shape[3]) for n, f in zip(names, full)}
        if "w_out" in wb:
            wb["w_out"] = wb["w_out"].reshape(D, D)
            wb["w_ffn_out"] = wb["w_ffn_out"].reshape(DFF, D)
        return wb

    def cast(names, l, tag):
        return _cast_into([w[n] for n in names], l, chip_arr, f"cast_{tag}")

    def landed(fly, names, after, tag):
        return as_weights(names, _gather_pass_on(_gather_wait(fly[0], fly[1], fly[2], after, tag), tag))

    tiny = _pack([w[n] for n in CHIP_SHARDED_SMALL], 8)
    _, tiny_full = _gather_weights([], tiny)
    fly_in = _gather_start(cast(first, 0, "in"), "in", after=(tiny_full,))
    fly0 = _gather_start(cast(rest, 0, "0"), "0", after=(fly_in[3],))
    fly1 = _gather_start(cast(BIG, 1, "1"), "1", after=(fly0[3],))
    p = {n: w[n] for n in SMALL}
    parts = [_unpack(tiny_full[k], [w[n].shape for n in CHIP_SHARDED_SMALL]) for k in range(4)]
    for i, n in enumerate(CHIP_SHARDED_SMALL):
        p[n] = jnp.concatenate([parts[k][i] for k in range(4)], axis=-1)

    operands = [_layer_operands(l, p) for l in range(2)]
    state_packs = [_pack([src[n] for n in SMALL], 32) for src in (w, mom, var)]
    ahead = tuple(jax.tree.leaves(operands)) + tuple(state_packs)

    passing = {}

    def pass_on_1(gu):
        bufs = _gather_wait(fly1[0], fly1[1], fly1[2], gu, "1")
        passing[1] = _exchange_start(bufs, [], _pass_on_copies, 3 * len(bufs), "gather_pass_on_start_1")
        return (passing[1][-1],)

    def pass_on_0(ya):
        bufs = _gather_wait(fly0[0], fly0[1], fly0[2], ya, "0")
        passing[0] = _exchange_start(bufs, [], _pass_on_copies, 3 * len(bufs), "gather_pass_on_start_0")
        return (passing[0][-1],)

    def rest0(merged):
        send, recv, bufs, _, _ = passing[0]
        return as_weights(rest, _exchange_wait(send, recv, bufs, [], merged, _pass_on_copies, "gather_pass_on_wait_0")[0])

    xa, saved0 = _forward_layer(0, x[0], p, landed(fly_in, first, (fly1[3],) + ahead, "in"), after=(fly0[3], fly1[3]),
                                early=pass_on_0, rest=rest0, near_end=pass_on_1, operands=operands[0])
    send, recv, bufs1, _, _ = passing[1]
    xb, saved1 = _forward_layer(
        1, xa, p, as_weights(BIG, _exchange_wait(send, recv, bufs1, [], xa, _pass_on_copies, "gather_pass_on_wait_1")[0]),
        operands=operands[1], loss=(loss_target[0], p["final_g"][None]))
    dxb, loss_v, dfg = xb
    loss = lax.psum(loss_v[0, 0], ("x", "y", "c"))

    out, flying = {}, {}

    def halves(grads):
        return [g.reshape(4, 2, -1, g.shape[-1]) for g in grads]

    def sibling_start(grads, names, l, tag):
        gs = halves(grads)
        lands = [lax.empty((4,) + g.shape[2:], g.dtype) for g in gs]
        flying["s" + tag] = (names, l) + tuple(
            _exchange_start(gs, lands, _sibling_copies, 4 * len(gs), f"grads_to_sibling_start_{tag}"))
        return (flying["s" + tag][-1],)

    def chips_start(gs, from_sib, names, l, tag):
        cs = _add_half(gs, from_sib, c_arr, f"add_half_{tag}")
        lands = [lax.empty((3,) + a.shape[1:], a.dtype) for a in cs]
        flying[tag] = (names, l) + tuple(_exchange_start(cs, lands, _chips_copies, 3 * len(cs), f"grads_to_chips_start_{tag}"))
        return (flying[tag][-1],)

    def sibling_finish(tag, after):
        names, l, send, recv, gs, lands, _ = flying["s" + tag]
        gs, from_sib = _exchange_wait(send, recv, gs, lands, after, _sibling_copies, f"grads_to_sibling_wait_{tag}")
        return chips_start(gs, from_sib, names, l, tag)

    def reduce_sums(tags, after):
        groups, ts = [], []
        for tag in tags:
            names, l, send, recv, cs, lands, _ = flying[tag]
            cs, lands = _exchange_wait(send, recv, cs, lands, after, _chips_copies, f"grads_to_chips_wait_{tag}")
            ts += _sum_chips(cs, lands, place_arr, f"sum_chips_{tag}")
            groups.append((tag, names))
        flying["j" + tags[0]] = (groups, l) + tuple(_exchange_start(ts, [], _join_copies, len(ts), f"grads_join_start_{tags[0]}"))
        return (flying["j" + tags[0]][-1],)

    def reduce_adam(tag0, after):
        groups, l, send, recv, ts, _, _ = flying["j" + tag0]
        joined = _exchange_wait(send, recv, ts, [], after, _join_copies, f"grads_join_wait_{tag0}")[0]
        for tag, names in groups:
            gs, joined = [j.reshape(w[n].shape[1:]) for n, j in zip(names, joined)], joined[len(names):]
            res = _adam_layer(gs, [w[n] for n in names], [mom[n] for n in names], [var[n] for n in names], l,
                              [out.get(n) for n in names], f"adam_{tag}")
            out.update(zip(names, res))

    def late1(grads):
        return sibling_finish("1a", grads[0]) + sibling_start(grads, first, 1, "1b")

    def midway0(grads):
        return reduce_sums(("1a", "1b"), grads[0]) + sibling_start(grads, rest, 0, "0a")

    def stacked_small(small0):
        small = {k: jnp.stack([small0[k], small1[k]]) for k in LAYER_SMALL}
        return dict(small, final_g=dfg[0])

    def midway3_0(small0):
        small = stacked_small(dict(small0, norm1_g=jnp.zeros((D,), F32)))
        csum = _small_chip_sum(_pack([small[n] for n in SMALL], 32))
        flying["small"] = _exchange_start([csum], [lax.empty((3,) + csum.shape, BF16)], _small_copies, 3, "small_to_chips_start")
        return (flying["small"][-1],)

    def late0(grads):
        tok = sibling_start(grads, first, 0, "0b")
        reduce_adam("1a", tok[0])
        return sibling_finish("0b", out[first[0]][0])

    dxa, big1, small1 = _backward_layer(1, dxb, saved1, midway=lambda grads: sibling_start(grads, rest, 1, "1a"), late=late1)
    dx, big0, small0 = _backward_layer(0, dxa, saved0, after=sibling_finish("1b", dxa), midway=midway0,
                                       midway2=lambda dws: sibling_finish("0a", dws), midway3=midway3_0, late=late0)
    join_tok = reduce_sums(("0a", "0b"), dx)
    small = stacked_small(small0)

    full_shapes = [small[n].shape for n in SMALL]
    send, recv, csum, land, _ = flying["small"]
    csum, land = _exchange_wait(send, recv, csum, land, join_tok[0], _small_copies, "small_to_chips_wait")
    red = _unpack(_small_total(csum[0], land[0]), full_shapes)
    norm1_0 = _allreduce_small(_pack([small0["norm1_g"]], 32))[:D // 128].reshape(D)
    reduce_adam("0a", norm1_0)
    red[SMALL.index("norm1_g")] = red[SMALL.index("norm1_g")].at[0].set(norm1_0)
    g_small = []
    for n, g in zip(SMALL, red):
        if n in CHIP_SHARDED_SMALL:
            g = lax.dynamic_slice_in_dim(g, chip * w[n].shape[-1], w[n].shape[-1], axis=g.ndim - 1)
        g_small.append(g)
    shapes = [w[n].shape for n in SMALL]
    upd = [_unpack(u, shapes) for u in _adam(_pack(g_small, 32), *state_packs, "adam_small")]
    for i, n in enumerate(SMALL):
        out[n] = [g_small[i], upd[0][i], upd[1][i], upd[2][i]]

    return (loss, dx[None]) + tuple(out[n][i] for i in range(4) for n in WEIGHTS)
```

```python
import functools

import jax
import jax.numpy as jnp
from jax import lax
from jax.experimental import pallas as pl
from jax.experimental.pallas import tpu as pltpu

F32 = jnp.float32
BF16 = jnp.bfloat16
MESH = pl.DeviceIdType.MESH

D = 1024
NH = 8
HD = 128
CHUNK = 128
GMLP_ROWS = 512
N_IN_T = 12
DFF = 2816
DFF_SH = 1408
EPS = 1e-6
LRU_C = 8.0
ADAM_LR, ADAM_B1, ADAM_B2, ADAM_EPS, ADAM_WD, ADAM_STEP = 0.001, 0.9, 0.999, 1e-08, 0.01, 10

TM = 512
TM_BIG = 1024
RT = 128
PADR = 8
VMEM_LIMIT = 56 * 1024 * 1024


def _cp(sem=None, **kw):
    if sem is not None:
        kw["dimension_semantics"] = sem
    return pltpu.CompilerParams(vmem_limit_bytes=VMEM_LIMIT, **kw)


_GC = 0.7978845608028654


def _sigmoid(x):
    return 0.5 * jnp.tanh(0.5 * x) + 0.5


_GK = 0.044715


def _gelu(x):
    t = jnp.tanh(x * (_GC + (_GC * _GK) * (x * x)))
    return x * (0.5 + 0.5 * t)


def _gelu_and_grad(x):
    x2 = x * x
    t = jnp.tanh(x * (_GC + (_GC * _GK) * x2))
    h = 0.5 + 0.5 * t
    return x * h, h + x * (1.0 - t * t) * (0.5 * _GC + (1.5 * _GC * _GK) * x2)


def _softplus_neg(lam):
    y = jnp.exp(-jnp.abs(lam))
    u = 1.0 + y
    l1p = jnp.where(u == 1.0, y, jnp.log(u) * y / (u - 1.0))
    return jnp.maximum(-lam, 0.0) + l1p


def _dot(a, b):
    return jnp.dot(a, b, preferred_element_type=F32)


def _dot_nt(a, b):
    return lax.dot_general(a, b, (((1,), (1,)), ((), ())), preferred_element_type=F32)


def _dot_tn(a, b):
    return lax.dot_general(a, b, (((0,), (0,)), ((), ())), preferred_element_type=F32)


def _rms_hat(x):
    r = lax.rsqrt(jnp.mean(x * x, axis=-1, keepdims=True) + EPS)
    return x * r, r


def _rms_bwd(dh, x, g):
    xh, r = _rms_hat(x)
    dxh = dh * g
    dx = r * (dxh - xh * jnp.mean(dxh * xh, axis=-1, keepdims=True))
    return dx, jnp.sum(dh * xh, axis=0, keepdims=True)


def _norm_into(x_ref, g_ref, h_ref):
    xh, _ = _rms_hat(x_ref[...])
    h_ref[...] = (xh * g_ref[...]).astype(BF16)


def _in_tile(j):
    m, hf = j // 2, j % 2
    orig = jnp.where(m < 2, m, jnp.where(m == 2, 4, jnp.where(m < 5, m - 1, 5)))
    t = orig * 2 + hf
    return t // 3, t % 3


ANY = pl.BlockSpec(memory_space=pl.ANY)


def _mm_in(x, g, w_in, l, after=()):
    S = x.shape[0]
    tm = min(2 * TM_BIG, S)

    def body(x_ref, g_ref, w0_ref, w1_ref, *rest):
        o_ref, h_ref = rest[-2:]

        @pl.when(pl.program_id(1) == 0)
        def _():
            _norm_into(x_ref, g_ref, h_ref)
        rp = min(TM, tm)
        for r0 in range(0, tm, rp):
            hv = h_ref[r0:r0 + rp, :]
            o_ref[r0:r0 + rp, 0:512] = _dot(hv, w0_ref[...]).astype(BF16)
            o_ref[r0:r0 + rp, 512:1024] = _dot(hv, w1_ref[...]).astype(BF16)

    def w_tile(hf):
        def w_map(i, m):
            sh, tl = _in_tile(2 * m + hf)
            return (sh, 0, tl)
        return pl.BlockSpec((None, D, 512), w_map)

    return pl.pallas_call(
        body, name=f"mm_in_{l}", grid=(S // tm, 6),
        in_specs=[pl.BlockSpec((tm, D), lambda i, m: (i, 0)), pl.BlockSpec((1, D), lambda i, m: (0, 0)),
                  w_tile(0), w_tile(1)] + [ANY] * len(after),
        out_specs=[pl.BlockSpec((None, tm, D), lambda i, m: (m, i, 0)), pl.BlockSpec((tm, D), lambda i, m: (i, 0))],
        out_shape=[jax.ShapeDtypeStruct((6, S, D), BF16), jax.ShapeDtypeStruct((S, D), BF16)],
        compiler_params=_cp(("parallel", "arbitrary")),
    )(x, g, w_in, w_in, *after)


def _mm_res(a, w, res, l, name, after=()):
    S, K = a.shape

    tm = TM

    def body(a_ref, w_ref, r_ref, *rest):
        rest[-1][...] = r_ref[...] + _dot(a_ref[...], w_ref[...])

    return pl.pallas_call(
        body, name=f"{name}_{l}", grid=(S // tm,),
        in_specs=[pl.BlockSpec((tm, K), lambda i: (i, 0)), pl.BlockSpec((K, D), lambda i: (0, 0)),
                  pl.BlockSpec((tm, D), lambda i: (i, 0))] + [ANY] * len(after),
        out_specs=pl.BlockSpec((tm, D), lambda i: (i, 0)),
        out_shape=jax.ShapeDtypeStruct((S, D), F32),
        compiler_params=_cp(("parallel",)),
    )(a, w, res, *after)


def _mm_ffn_in(x, g, w_fi, l):
    S = x.shape[0]

    tm = min(TM_BIG, S)

    def body(x_ref, g_ref, w_ref, gu_ref, ff_ref, h_ref):
        @pl.when(pl.program_id(1) == 0)
        def _():
            _norm_into(x_ref, g_ref, h_ref)
        for r0 in range(0, tm, TM):
            rows = slice(r0, r0 + TM)
            hv = h_ref[rows, :]
            ga = _dot(hv, w_ref[0])
            gb = _dot(hv, w_ref[1])
            sg = _sigmoid(ga)
            silu = ga * sg
            gu_ref[0, rows, :] = (gb * (sg + silu * (1.0 - sg))).astype(BF16)
            gu_ref[1, rows, :] = silu.astype(BF16)
            ff_ref[rows, :] = (silu * gb).astype(BF16)

    gu, ff, h = pl.pallas_call(
        body, name=f"mm_ffn_in_{l}", grid=(S // tm, 2),
        in_specs=[pl.BlockSpec((tm, D), lambda i, s: (i, 0)), pl.BlockSpec((1, D), lambda i, s: (0, 0)),
                  pl.BlockSpec((2, None, D, DFF_SH), lambda i, s: (0, s, 0, 0))],
        out_specs=[pl.BlockSpec((2, None, tm, DFF_SH), lambda i, s: (0, s, i, 0)),
                   pl.BlockSpec((tm, DFF_SH), lambda i, s: (i, s)),
                   pl.BlockSpec((tm, D), lambda i, s: (i, 0))],
        out_shape=[jax.ShapeDtypeStruct((2, 2, S, DFF_SH), BF16), jax.ShapeDtypeStruct((S, DFF), BF16),
                   jax.ShapeDtypeStruct((S, D), BF16)],
        compiler_params=_cp(("parallel", "arbitrary")),
    )(x, g, w_fi.reshape(2, 2, D, DFF_SH))
    return gu.reshape(4, S, DFF_SH), ff, h


def _gmlp_fwd(z6, ws_b, bs_b, lg, lb):
    S = z6.shape[1]

    ts = min(GMLP_ROWS, S)

    def body(z_ref, ws_ref, bs_ref, lg_ref, lb_ref, o_ref, mix):
        for r0 in range(0, ts, CHUNK):
            rows = slice(r0, r0 + CHUNK)
            gv = _gelu(z_ref[1, rows, :].astype(F32))
            xc = gv - jnp.mean(gv, axis=-1, keepdims=True)
            rs = lax.rsqrt(jnp.mean(xc * xc, axis=-1, keepdims=True) + EPS)
            vb = (xc * rs * lg_ref[...] + lb_ref[...]).astype(BF16)
            for gi in range(NH):
                cs = slice(gi * HD, (gi + 1) * HD)
                mix[rows, cs] = _dot(ws_ref[gi], vb[:, cs])
            o_ref[rows, :] = (_sigmoid(z_ref[2, rows, :].astype(F32)) * _gelu(z_ref[0, rows, :].astype(F32))
                              * (mix[rows, :] + bs_ref[...])).astype(BF16)

    return pl.pallas_call(
        body, name="gmlp_fwd", grid=(S // ts,),
        in_specs=[pl.BlockSpec((3, ts, D), lambda i: (0, i, 0)), pl.BlockSpec((NH, CHUNK, CHUNK), lambda i: (0, 0, 0)),
                  pl.BlockSpec((CHUNK, D), lambda i: (0, 0)), pl.BlockSpec((1, D), lambda i: (0, 0)),
                  pl.BlockSpec((1, D), lambda i: (0, 0))],
        out_specs=pl.BlockSpec((ts, D), lambda i: (i, 0)),
        out_shape=jax.ShapeDtypeStruct((S, D), BF16),
        scratch_shapes=[pltpu.VMEM((ts, D), F32)],
        compiler_params=_cp(("parallel",)),
    )(z6, ws_b, bs_b, lg, lb)


def _row_iota():
    return lax.broadcasted_iota(jnp.int32, (RT, HD), 0)


SUB = 8
UNROLL = 8
GRAD_ROWS = 2048


def _scan_up(a, b, carry):
    row = lax.broadcasted_iota(jnp.int32, (SUB, HD), 0)
    masks = [(d, row >= d) for d in (1, 2, 4)]
    c = jnp.broadcast_to(carry, (SUB, HD))
    hs = []
    for j in range(RT // SUB):
        aj, bj = a[SUB * j:SUB * (j + 1)], b[SUB * j:SUB * (j + 1)]
        for d, m in masks:
            bj = bj + aj * jnp.where(m, pltpu.roll(bj, d, 0), 0.0)
            aj = aj * jnp.where(m, pltpu.roll(aj, d, 0), 1.0)
        h = bj + aj * c
        hs.append(h)
        c = jnp.broadcast_to(h[SUB - 1:SUB, :], (SUB, HD))
    return jnp.concatenate(hs, axis=0), hs[-1][SUB - 1:SUB, :]


def _scan_down(a, b, carry):
    row = lax.broadcasted_iota(jnp.int32, (SUB, HD), 0)
    masks = [(d, row < SUB - d) for d in (1, 2, 4)]
    c = jnp.broadcast_to(carry, (SUB, HD))
    hs = []
    for j in reversed(range(RT // SUB)):
        aj, bj = a[SUB * j:SUB * (j + 1)], b[SUB * j:SUB * (j + 1)]
        for d, m in masks:
            bj = bj + aj * jnp.where(m, pltpu.roll(bj, SUB - d, 0), 0.0)
            aj = aj * jnp.where(m, pltpu.roll(aj, SUB - d, 0), 1.0)
        h = bj + aj * c
        hs.append(h)
        c = jnp.broadcast_to(h[0:1, :], (SUB, HD))
    return jnp.concatenate(hs[::-1], axis=0), hs[-1][0:1, :]


def _decay(r, sp_d):
    log_a = -LRU_C * r * sp_d
    a = jnp.exp(log_a)
    return a, jnp.sqrt(jnp.maximum(-jnp.tanh(log_a) * (a * a + 1.0), 0.0))


def _decay_bwd(r, sp_d):
    log_a = -LRU_C * r * sp_d
    a = jnp.exp(log_a)
    m2 = jnp.maximum(-jnp.tanh(log_a) * (a * a + 1.0), 0.0)
    inv = jnp.where(m2 > 0.0, lax.rsqrt(m2), 0.0)
    return a, m2 * inv, inv


def _lru_gates(xc, d, wr_ref, br_ref, wi_ref, bi_ref, sp):
    xb = xc.astype(BF16)
    r = _sigmoid(_dot(xb, wr_ref[d]) + br_ref[d:d + 1, :])
    i = _sigmoid(_dot(xb, wi_ref[d]) + bi_ref[d:d + 1, :])
    a, mult = _decay(r, sp[d:d + 1, :])
    return r, i, a, mult


def _shifted(win, k):
    w = RT + 2 * PADR
    v = win if k == 0 else pltpu.roll(win, (-k) % w, 0)
    return v[PADR:PADR + RT]


def _conv_taps(win):
    return [_shifted(win, k) for k in (-1, 0, 1, 2)]


def _fill_padded(dst, src_ref, S):
    zeros = jnp.zeros((PADR, HD), F32)
    dst[0:PADR, :] = zeros
    dst[PADR + S:2 * PADR + S, :] = zeros

    def cp(i, c):
        t0 = pl.multiple_of(i * RT, RT)
        dst[pl.ds(t0 + PADR, RT), :] = src_ref[pl.ds(t0, RT), :].astype(F32)
        return c
    lax.fori_loop(0, S // RT, cp, 0)


def _conv_fwd_all(zxp, xc_s, cw_ref, cb_ref, S):
    def cv(i, c):
        t0 = pl.multiple_of(i * RT, RT)
        xm1, x0, xp1, xp2 = _conv_taps(zxp[pl.ds(t0, RT + 2 * PADR), :])
        xc_s[pl.ds(t0, RT), :] = (cb_ref[...] + xm1 * cw_ref[0:1, :] + x0 * cw_ref[1:2, :]
                                  + xp1 * cw_ref[2:3, :] + xp2 * cw_ref[3:4, :])
        return c
    lax.fori_loop(0, S // RT, cv, 0)


def _lru_specs(S):
    head = lambda h: (0, h)
    return [pl.BlockSpec((4, HD), head), pl.BlockSpec((1, HD), head),
            pl.BlockSpec((2, None, HD, HD), lambda h: (0, h, 0, 0)), pl.BlockSpec((2, HD), head),
            pl.BlockSpec((2, None, HD, HD), lambda h: (0, h, 0, 0)), pl.BlockSpec((2, HD), head),
            pl.BlockSpec((2, HD), head)]


def _lru_fwd(z6, ya, cw, cb, wr, br, wi, bi, lam, after=()):
    S = z6.shape[1]
    nt = S // RT

    def body(z_ref, ya_ref, cw_ref, cb_ref, wr_ref, br_ref, wi_ref, bi_ref, lam_ref, *rest):
        mg_ref, h0_ref, h1_ref, zxp, xc_s = rest[-5:]
        sp = _softplus_neg(lam_ref[...])
        _fill_padded(zxp, z_ref.at[0], S)
        _conv_fwd_all(zxp, xc_s, cw_ref, cb_ref, S)

        def scans(i, carry):
            cu, cd = carry
            for u in range(UNROLL):
                j = i * UNROLL + u
                ru = pl.ds(pl.multiple_of(j * RT, RT), RT)
                rd = pl.ds(pl.multiple_of((nt - 1 - j) * RT, RT), RT)
                xu, xd = xc_s[ru, :], xc_s[rd, :]
                _, gi, a, mult = _lru_gates(xu, 0, wr_ref, br_ref, wi_ref, bi_ref, sp)
                hu, cu = _scan_up(a, mult * gi * xu, cu)
                h0_ref[ru, :] = hu
                _, gi, a, mult = _lru_gates(xd, 1, wr_ref, br_ref, wi_ref, bi_ref, sp)
                hd, cd = _scan_down(a, mult * gi * xd, cd)
                h1_ref[rd, :] = hd
            return cu, cd
        z1 = jnp.zeros((1, HD), F32)
        lax.fori_loop(0, nt // UNROLL, scans, (z1, z1))

        def merge(i, c):
            rows = pl.ds(pl.multiple_of(i * RT, RT), RT)
            yb = (h0_ref[rows, :] + h1_ref[rows, :]) * _gelu(z_ref[1, rows, :].astype(F32))
            mg_ref[rows, :] = (ya_ref[rows, :].astype(F32) + _sigmoid(z_ref[2, rows, :].astype(F32)) * yb).astype(BF16)
            return c
        lax.fori_loop(0, nt, merge, 0)

    col = pl.BlockSpec((S, HD), lambda h: (0, h))
    return pl.pallas_call(
        body, name="lru_fwd", grid=(NH,),
        in_specs=[pl.BlockSpec((3, S, HD), lambda h: (1, 0, h)), col] + _lru_specs(S) + [ANY] * len(after),
        out_specs=[col, col, col],
        out_shape=[jax.ShapeDtypeStruct((S, D), BF16), jax.ShapeDtypeStruct((S, D), F32), jax.ShapeDtypeStruct((S, D), F32)],
        scratch_shapes=[pltpu.VMEM((S + 2 * PADR, HD), F32), pltpu.VMEM((S, HD), F32)],
        compiler_params=_cp(("parallel",)),
    )(z6, ya, cw, cb, wr, br, wi, bi, lam, *after)


def _mm_res_loss(a, w, res, tgt, g):
    S, K = a.shape

    def body(a_ref, w_ref, r_ref, t_ref, g_ref, dx_ref, loss_ref, dg_ref):
        @pl.when(pl.program_id(0) == 0)
        def _():
            loss_ref[...] = jnp.zeros_like(loss_ref)
            dg_ref[...] = jnp.zeros_like(dg_ref)
        xv = r_ref[...] + _dot(a_ref[...], w_ref[...])
        xh, _ = _rms_hat(xv)
        e = xh * g_ref[...] - t_ref[...]
        loss_ref[...] += jnp.sum(e * e) * (0.5 / D)
        dx, dgs = _rms_bwd(e * (1.0 / D), xv, g_ref[...])
        dx_ref[...] = dx
        dg_ref[...] += dgs

    row = pl.BlockSpec((TM, D), lambda i: (i, 0))
    vec = pl.BlockSpec((1, D), lambda i: (0, 0))
    return pl.pallas_call(
        body, name="mm_ffn_out_loss", grid=(S // TM,),
        in_specs=[pl.BlockSpec((TM, K), lambda i: (i, 0)), pl.BlockSpec((K, D), lambda i: (0, 0)), row, row, vec],
        out_specs=[row, pl.BlockSpec((1, 128), lambda i: (0, 0)), vec],
        out_shape=[jax.ShapeDtypeStruct((S, D), F32), jax.ShapeDtypeStruct((1, 128), F32), jax.ShapeDtypeStruct((1, D), F32)],
        compiler_params=_cp(("arbitrary",)),
    )(a, w, res, tgt, g)


def _bwd_ffn_out(dx, w_fo, gu, l, after=()):
    S = dx.shape[0]

    tm = min(TM_BIG, S)

    def body(dx_ref, w_ref, gu_ref, *rest):
        o_ref = rest[-1]
        for r0 in range(0, tm, TM):
            rows = slice(r0, r0 + TM)
            d = _dot_nt(dx_ref[rows, :].astype(BF16), w_ref[...])
            d = d.astype(BF16)
            o_ref[0, rows, :] = d * gu_ref[0, rows, :]
            o_ref[1, rows, :] = d * gu_ref[1, rows, :]

    pair = pl.BlockSpec((2, None, tm, DFF_SH), lambda i, s: (0, s, i, 0))
    dgu = pl.pallas_call(
        body, name=f"bwd_ffn_out_{l}", grid=(S // tm, 2),
        in_specs=[pl.BlockSpec((tm, D), lambda i, s: (i, 0)), pl.BlockSpec((DFF_SH, D), lambda i, s: (s, 0)), pair]
        + [ANY] * len(after),
        out_specs=pair,
        out_shape=jax.ShapeDtypeStruct((2, 2, S, DFF_SH), BF16),
        compiler_params=_cp(("parallel", "arbitrary")),
    )(dx, w_fo, gu.reshape(2, 2, S, DFF_SH), *after)
    return dgu.reshape(4, S, DFF_SH)


def _mm_tn(a, b, m_blk, tk, name):
    S, M = a.shape

    def body(a_ref, b_ref, o_ref):
        @pl.when(pl.program_id(1) == 0)
        def _():
            o_ref[...] = jnp.zeros_like(o_ref)
        o_ref[...] += _dot_tn(a_ref[...], b_ref[...].astype(BF16))

    return pl.pallas_call(
        body, name=name, grid=(M // m_blk, S // tk),
        in_specs=[pl.BlockSpec((tk, m_blk), lambda m, k: (k, m)), pl.BlockSpec((tk, D), lambda m, k: (k, 0))],
        out_specs=pl.BlockSpec((m_blk, D), lambda m, k: (m, 0)),
        out_shape=jax.ShapeDtypeStruct((M, D), F32),
        compiler_params=_cp(("parallel", "arbitrary")),
    )(a, b)


def _mm_nt_rms_bwd(a, a_specs, w, w_specs, nk, tm, x, g, dres, name, after=()):
    S = x.shape[0]
    sub = len(a_specs)

    def body(*refs):
        a_refs, w_refs = refs[:sub], refs[sub:2 * sub]
        x_ref, g_ref, r_ref = refs[2 * sub:2 * sub + 3]
        dx_ref, dg_ref, acc = refs[-3:]
        i, k = pl.program_id(0), pl.program_id(1)
        @pl.when(k == 0)
        def _():
            acc[...] = jnp.zeros_like(acc)
        for j in range(sub):
            acc[...] += _dot_nt(a_refs[j][...], w_refs[j][...])

        @pl.when(jnp.logical_and(i == 0, k == 0))
        def _():
            dg_ref[...] = jnp.zeros_like(dg_ref)

        @pl.when(k == nk - 1)
        def _():
            dx, dgs = _rms_bwd(acc[...], x_ref[...], g_ref[...])
            dx_ref[...] = r_ref[...] + dx
            dg_ref[...] += dgs

    row = pl.BlockSpec((tm, D), lambda i, k: (i, 0))
    vec = pl.BlockSpec((1, D), lambda i, k: (0, 0))
    return pl.pallas_call(
        body, name=name, grid=(S // tm, nk),
        in_specs=list(a_specs) + list(w_specs) + [row, vec, row] + [ANY] * len(after),
        out_specs=[row, vec],
        out_shape=[jax.ShapeDtypeStruct((S, D), F32), jax.ShapeDtypeStruct((1, D), F32)],
        scratch_shapes=[pltpu.VMEM((tm, D), F32)],
        compiler_params=_cp(("arbitrary", "arbitrary")),
    )(*[a] * sub, *[w] * sub, x, g, dres, *after)


def _dw_ffn_in(h, dgu, l):
    S = h.shape[0]

    def body(h_ref, b_ref, o_ref):
        @pl.when(pl.program_id(1) == 0)
        def _():
            o_ref[...] = jnp.zeros_like(o_ref)
        o_ref[...] += _dot_tn(h_ref[...], b_ref[...])

    tk = min(2 * TM_BIG, S)
    return pl.pallas_call(
        body, name=f"dw_ffn_in_{l}", grid=(4, S // tk),
        in_specs=[pl.BlockSpec((tk, D), lambda j, k: (k, 0)), pl.BlockSpec((None, tk, DFF_SH), lambda j, k: (j, k, 0))],
        out_specs=pl.BlockSpec((None, D, DFF_SH), lambda j, k: (j, 0, 0)),
        out_shape=jax.ShapeDtypeStruct((4, D, DFF_SH), F32),
        compiler_params=_cp(("parallel", "arbitrary")),
    )(h, dgu)


_HALF_COMPS = ((0, 1, 3), (4, 2, 5))


def _dw_in(h, dz6, l, after=()):
    S = h.shape[0]

    def body(h_ref, d0_ref, d1_ref, d2_ref, *rest):
        o_ref = rest[-1]

        @pl.when(pl.program_id(1) == 0)
        def _():
            o_ref[...] = jnp.zeros_like(o_ref)
        hv = h_ref[...]
        for q, d_ref in enumerate((d0_ref, d1_ref, d2_ref)):
            for hf in range(2):
                col = 1024 * q + 512 * hf
                o_ref[col // 1536, :, col % 1536:col % 1536 + 512] += _dot_tn(hv, d_ref[:, 512 * hf:512 * (hf + 1)])

    tk = min(TM_BIG, S)

    def comp(q):
        return pl.BlockSpec((None, tk, D), lambda p, k: (jnp.where(p == 0, _HALF_COMPS[0][q], _HALF_COMPS[1][q]), k, 0))

    return pl.pallas_call(
        body, name=f"dw_in_{l}", grid=(2, S // tk),
        in_specs=[pl.BlockSpec((tk, D), lambda p, k: (k, 0)), comp(0), comp(1), comp(2)] + [ANY] * len(after),
        out_specs=pl.BlockSpec((2, D, 1536), lambda p, k: (p, 0, 0)),
        out_shape=jax.ShapeDtypeStruct((4, D, 1536), F32),
        compiler_params=_cp(("parallel", "arbitrary")),
    )(h, dz6, dz6, dz6, *after)


def _bwd_out(dx, w_o, merged, l):
    S = dx.shape[0]

    def body(dx_ref, w_ref, m_ref, dm_ref, dw_ref):
        @pl.when(pl.program_id(0) == 0)
        def _():
            dw_ref[...] = jnp.zeros_like(dw_ref)
        dxb = dx_ref[...].astype(BF16)
        dm_ref[...] = _dot_nt(dxb, w_ref[...]).astype(BF16)
        dw_ref[...] += _dot_tn(m_ref[...], dxb)

    tm = TM
    row = pl.BlockSpec((tm, D), lambda i: (i, 0))
    return pl.pallas_call(
        body, name=f"bwd_out_{l}", grid=(S // tm,),
        in_specs=[row, pl.BlockSpec((D, D), lambda i: (0, 0)), row],
        out_specs=[row, pl.BlockSpec((D, D), lambda i: (0, 0))],
        out_shape=[jax.ShapeDtypeStruct((S, D), BF16), jax.ShapeDtypeStruct((D, D), F32)],
        compiler_params=_cp(("arbitrary",)),
    )(dx, w_o, merged)


def _gmlp_bwd(dm, z6, ws_b, wst_b, bs_b, lg, lb, after=()):
    S = z6.shape[1]
    ts = min(GMLP_ROWS, S)

    def body(dm_ref, z_ref, ws_ref, wst_ref, bs_ref, lg_ref, lb_ref, *rest):
        dz_ref, dws_ref, dbs_ref, dlg_ref, dlb_ref, mix, dv = rest[-7:]

        @pl.when(pl.program_id(0) == 0)
        def _():
            dws_ref[...] = jnp.zeros_like(dws_ref)
            dbs_ref[...] = jnp.zeros_like(dbs_ref)
            dlg_ref[...] = jnp.zeros_like(dlg_ref)
            dlb_ref[...] = jnp.zeros_like(dlb_ref)
        for r0 in range(0, ts, CHUNK):
            rows = slice(r0, r0 + CHUNK)
            gv, dgelu_v = _gelu_and_grad(z_ref[1, rows, :].astype(F32))
            xc = gv - jnp.mean(gv, axis=-1, keepdims=True)
            rs = lax.rsqrt(jnp.mean(xc * xc, axis=-1, keepdims=True) + EPS)
            vh = xc * rs
            vb = (vh * lg_ref[...] + lb_ref[...]).astype(BF16)
            for gi in range(NH):
                cs = slice(gi * HD, (gi + 1) * HD)
                mix[rows, cs] = _dot(ws_ref[gi], vb[:, cs])
            u, dgelu_u = _gelu_and_grad(z_ref[0, rows, :].astype(F32))
            sa = _sigmoid(z_ref[2, rows, :].astype(F32))
            dya = dm_ref[rows, :].astype(F32) * sa
            dym = dya * (mix[rows, :] + bs_ref[...])
            dz_ref[2, rows, :] = (dym * u * (1.0 - sa)).astype(BF16)
            dz_ref[0, rows, :] = (dym * dgelu_u).astype(BF16)
            dmix = dya * u
            dmb = dmix.astype(BF16)
            for gi in range(NH):
                cs = slice(gi * HD, (gi + 1) * HD)
                dv[rows, cs] = _dot(wst_ref[gi], dmb[:, cs])
                dws_ref[gi] += _dot_nt(dmb[:, cs], vb[:, cs])
                dbs_ref[gi] += jnp.broadcast_to(jnp.sum(dmix[:, cs], axis=1, keepdims=True), (CHUNK, HD))
            dvv = dv[rows, :]
            dlg_ref[...] += jnp.sum(dvv * vh, axis=0, keepdims=True)
            dlb_ref[...] += jnp.sum(dvv, axis=0, keepdims=True)
            dvh = dvv * lg_ref[...]
            dgv = rs * (dvh - jnp.mean(dvh, axis=-1, keepdims=True) - vh * jnp.mean(dvh * vh, axis=-1, keepdims=True))
            dz_ref[1, rows, :] = (dgv * dgelu_v).astype(BF16)

    vec = pl.BlockSpec((1, D), lambda i: (0, 0))
    mat = pl.BlockSpec((NH, CHUNK, CHUNK), lambda i: (0, 0, 0))
    return pl.pallas_call(
        body, name="gmlp_bwd", grid=(S // ts,),
        in_specs=[pl.BlockSpec((ts, D), lambda i: (i, 0)), pl.BlockSpec((3, ts, D), lambda i: (0, i, 0)), mat, mat,
                  pl.BlockSpec((CHUNK, D), lambda i: (0, 0)), vec, vec] + [ANY] * len(after),
        out_specs=[pl.BlockSpec((3, ts, D), lambda i: (0, i, 0)), mat, mat, vec, vec],
        out_shape=[jax.ShapeDtypeStruct((6, S, D), BF16), jax.ShapeDtypeStruct((NH, CHUNK, CHUNK), F32),
                   jax.ShapeDtypeStruct((NH, CHUNK, HD), F32), jax.ShapeDtypeStruct((1, D), F32), jax.ShapeDtypeStruct((1, D), F32)],
        scratch_shapes=[pltpu.VMEM((ts, D), F32), pltpu.VMEM((ts, D), F32)],
        compiler_params=_cp(("arbitrary",)),
    )(dm, z6, ws_b, wst_b, bs_b, lg, lb, *after)


def _lru_bwd(dz6, dm, z6, h0, h1, cw, cb, wr, br, wi, bi, lam, after=()):
    S = z6.shape[1]
    nt = S // RT

    def body(dz_in, dm_ref, z_ref, h0_ref, h1_ref, cw_ref, cb_ref, wr_ref, br_ref, wi_ref, bi_ref, lam_ref, *rest):
        dz_ref, dcw_ref, dcb_ref, dwr_ref, dbr_ref, dwi_ref, dbi_ref, dlam_ref, zxp, xc_s, dhs_s, dxcp, r_s, lam_s = rest[-14:]
        del dz_in
        lam = lam_ref[...]
        sp = _softplus_neg(lam)
        row = _row_iota()
        _fill_padded(zxp, z_ref.at[0], S)
        _conv_fwd_all(zxp, xc_s, cw_ref, cb_ref, S)
        zeros = jnp.zeros((PADR, HD), F32)
        dxcp[0:PADR, :] = zeros
        dxcp[PADR + S:2 * PADR + S, :] = zeros
        dwr_ref[...] = jnp.zeros_like(dwr_ref)
        dwi_ref[...] = jnp.zeros_like(dwi_ref)

        def pre(i, c):
            rows = pl.ds(pl.multiple_of(i * RT, RT), RT)
            hs = h0_ref[rows, :] + h1_ref[rows, :]
            dmv = dm_ref[rows, :].astype(F32)
            sb = _sigmoid(z_ref[2, rows, :].astype(F32))
            gg, dgg = _gelu_and_grad(z_ref[1, rows, :].astype(F32))
            dz_ref[2, rows, :] = (dmv * hs * gg * sb * (1.0 - sb)).astype(BF16)
            dyb = dmv * sb
            dz_ref[1, rows, :] = (dyb * hs * dgg).astype(BF16)
            dhs_s[rows, :] = dyb * gg
            return c
        lax.fori_loop(0, nt, pre, 0)

        def gate_bwd(d, gates, lamv, da, xc):
            r, gi, a, mult, inv_mult = gates
            lx, lm = lamv * xc, lamv * mult
            dlog_r = (da - (lx * gi) * (a * inv_mult)) * a * r
            dpr = dlog_r * (1.0 - r) * (-LRU_C * sp[d:d + 1, :])
            dpi = (lx * mult) * gi * (1.0 - gi)
            xb, dprb, dpib = xc.astype(BF16), dpr.astype(BF16), dpi.astype(BF16)
            dwr_ref[d] += _dot_tn(xb, dprb)
            dwi_ref[d] += _dot_tn(xb, dpib)
            dxc = lm * gi + _dot_nt(dprb, wr_ref[d]) + _dot_nt(dpib, wi_ref[d])
            return dxc, (jnp.sum(dlog_r, axis=0, keepdims=True) * (-LRU_C), jnp.sum(dpr, axis=0, keepdims=True),
                         jnp.sum(dpi, axis=0, keepdims=True))

        def rgates(i, c):
            for u in range(UNROLL):
                rows = pl.ds(pl.multiple_of((i * UNROLL + u) * RT, RT), RT)
                xb = xc_s[rows, :].astype(BF16)
                for d in range(2):
                    r_s[d, rows, :] = _sigmoid(_dot(xb, wr_ref[d]) + br_ref[d:d + 1, :])
            return c
        lax.fori_loop(0, nt // UNROLL, rgates, 0)

        def chains(i, carry):
            qn, qp = carry
            for u in range(UNROLL):
                j = i * UNROLL + u
                rd = pl.ds(pl.multiple_of((nt - 1 - j) * RT, RT), RT)
                a, dhs = _decay(r_s[0, rd, :], sp[0:1, :])[0], dhs_s[rd, :]
                q, q_first = _scan_down(a, a * dhs, qn)
                lam_s[0, rd, :] = dhs + jnp.where(row == RT - 1, qn, pltpu.roll(q, RT - 1, 0))
                qn = q_first
                ru = pl.ds(pl.multiple_of(j * RT, RT), RT)
                a, dhs = _decay(r_s[1, ru, :], sp[1:2, :])[0], dhs_s[ru, :]
                q, q_last = _scan_up(a, a * dhs, qp)
                lam_s[1, ru, :] = dhs + jnp.where(row == 0, qp, pltpu.roll(q, 1, 0))
                qp = q_last
            return qn, qp

        z1 = jnp.zeros((1, HD), F32)
        lax.fori_loop(0, nt // UNROLL, chains, (z1, z1))

        ct = min(GRAD_ROWS, S)
        crow = lax.broadcasted_iota(jnp.int32, (ct, HD), 0)

        def tile_grads(i, acc):
            t0 = pl.multiple_of(i * ct, ct)
            rows = pl.ds(t0, ct)
            xc = xc_s[rows, :]
            xb = xc.astype(BF16)
            tp = pl.multiple_of(jnp.maximum(t0 - PADR, 0), PADR)
            prev = jnp.where(t0 > 0, h0_ref[pl.ds(tp, PADR), :][PADR - 1:PADR, :], 0.0)
            tn = pl.multiple_of(jnp.minimum(t0 + ct, S - PADR), PADR)
            nxt = jnp.where(t0 + ct < S, h1_ref[pl.ds(tn, PADR), :][0:1, :], 0.0)
            hside = (jnp.where(crow == 0, prev, pltpu.roll(h0_ref[rows, :], 1, 0)),
                     jnp.where(crow == ct - 1, nxt, pltpu.roll(h1_ref[rows, :], ct - 1, 0)))
            dxc, sums = 0.0, ()
            for d in range(2):
                r = r_s[d, rows, :]
                gi = _sigmoid(_dot(xb, wi_ref[d]) + bi_ref[d:d + 1, :])
                lamv = lam_s[d, rows, :]
                dxc_d, s_d = gate_bwd(d, (r, gi) + _decay_bwd(r, sp[d:d + 1, :]), lamv, lamv * hside[d], xc)
                dxc = dxc + dxc_d
                sums = sums + s_d
            dxcp[pl.ds(t0 + PADR, ct), :] = dxc
            return tuple(x + y for x, y in zip(acc, sums))

        s_sp0, s_br0, s_bi0, s_sp1, s_br1, s_bi1 = lax.fori_loop(0, S // ct, tile_grads, (z1,) * 6)

        dsp = jnp.concatenate([s_sp0, s_sp1], axis=0)
        dlam_ref[...] = -dsp * _sigmoid(-lam)
        dbr_ref[...] = jnp.concatenate([s_br0, s_br1], axis=0)
        dbi_ref[...] = jnp.concatenate([s_bi0, s_bi1], axis=0)

        def conv_bwd(i, carry):
            c0, c1, c2, c3, cb_ = carry
            t0 = pl.multiple_of(i * RT, RT)
            dwin = dxcp[pl.ds(t0, RT + 2 * PADR), :]
            d0 = _shifted(dwin, 0)
            dz_ref[0, pl.ds(t0, RT), :] = (_shifted(dwin, 1) * cw_ref[0:1, :] + d0 * cw_ref[1:2, :]
                                           + _shifted(dwin, -1) * cw_ref[2:3, :] + _shifted(dwin, -2) * cw_ref[3:4, :]).astype(BF16)
            xm1, x0, xp1, xp2 = _conv_taps(zxp[pl.ds(t0, RT + 2 * PADR), :])
            sm = lambda v: jnp.sum(v, axis=0, keepdims=True)
            return c0 + sm(d0 * xm1), c1 + sm(d0 * x0), c2 + sm(d0 * xp1), c3 + sm(d0 * xp2), cb_ + sm(d0)

        c0, c1, c2, c3, cb_ = lax.fori_loop(0, nt, conv_bwd, (z1, z1, z1, z1, z1))
        dcw_ref[...] = jnp.concatenate([c0, c1, c2, c3], axis=0)
        dcb_ref[...] = cb_

    col = pl.BlockSpec((S, HD), lambda h: (0, h))
    head = lambda h: (0, h)
    wspec = pl.BlockSpec((2, None, HD, HD), lambda h: (0, h, 0, 0))
    return pl.pallas_call(
        body, name="lru_bwd", grid=(NH,),
        in_specs=[pl.BlockSpec(memory_space=pl.ANY), col, pl.BlockSpec((3, S, HD), lambda h: (1, 0, h)), col, col] + _lru_specs(S)
        + [ANY] * len(after),
        out_specs=[pl.BlockSpec((3, S, HD), lambda h: (1, 0, h)), pl.BlockSpec((4, HD), head), pl.BlockSpec((1, HD), head),
                   wspec, pl.BlockSpec((2, HD), head), wspec, pl.BlockSpec((2, HD), head), pl.BlockSpec((2, HD), head)],
        out_shape=[jax.ShapeDtypeStruct((6, S, D), BF16), jax.ShapeDtypeStruct((4, D), F32), jax.ShapeDtypeStruct((1, D), F32),
                   jax.ShapeDtypeStruct((2, NH, HD, HD), F32), jax.ShapeDtypeStruct((2, D), F32),
                   jax.ShapeDtypeStruct((2, NH, HD, HD), F32), jax.ShapeDtypeStruct((2, D), F32), jax.ShapeDtypeStruct((2, D), F32)],
        scratch_shapes=[pltpu.VMEM((S + 2 * PADR, HD), F32), pltpu.VMEM((S, HD), F32), pltpu.VMEM((S, HD), F32),
                        pltpu.VMEM((S + 2 * PADR, HD), F32), pltpu.VMEM((2, S, HD), F32), pltpu.VMEM((2, S, HD), F32)],
        input_output_aliases={0: 0},
        compiler_params=_cp(("parallel",)),
    )(dz6, dm, z6, h0, h1, cw, cb, wr, br, wi, bi, lam, *after)


LAYER_SMALL = ("norm1_g", "gmlp_ln_g", "gmlp_ln_b", "gmlp_w_s", "gmlp_b_s", "conv_w", "conv_b",
               "lru_w_r", "lru_b_r", "lru_w_i", "lru_b_i", "lru_lambda", "norm2_g")


def _layer_operands(l, p):
    ws_b = p["gmlp_w_s"][l].astype(BF16)
    tm = dict(ws_b=ws_b, wst_b=jnp.swapaxes(ws_b, 1, 2), bs_b=jnp.repeat(p["gmlp_b_s"][l].T, HD, axis=1),
              lg=p["gmlp_ln_g"][l][None], lb=p["gmlp_ln_b"][l][None])
    lru = (p["conv_w"][l], p["conv_b"][l][None], p["lru_w_r"][l].astype(BF16), p["lru_b_r"][l],
           p["lru_w_i"][l].astype(BF16), p["lru_b_i"][l], p["lru_lambda"][l])
    return (p["norm1_g"][l][None], p["norm2_g"][l][None]), tm, lru


def _forward_layer(l, x, p, wb, after=(), early=None, rest=None, near_end=None, operands=None, loss=None):
    (g1, g2), tm, lru = _layer_operands(l, p) if operands is None else operands
    z6, hn1 = _mm_in(x, g1, wb["w_in"], l, after)
    ya = _gmlp_fwd(z6, tm["ws_b"], tm["bs_b"], tm["lg"], tm["lb"])
    merged, h0, h1 = _lru_fwd(z6, ya, *lru, after=() if early is None else tuple(early(ya)))
    if rest is not None:
        wb = dict(wb, **rest(merged))
    x1 = _mm_res(merged, wb["w_out"], x, l, "mm_out")
    gu, ff, hn2 = _mm_ffn_in(x1, g2, wb["w_ffn_in"], l)
    if loss is None:
        x2 = _mm_res(ff, wb["w_ffn_out"], x1, l, "mm_ffn_out", () if near_end is None else tuple(near_end(gu)))
    else:
        x2 = _mm_res_loss(ff, wb["w_ffn_out"], x1, *loss)
    return x2, dict(x=x, z6=z6, h0=h0, h1=h1, merged=merged, x1=x1, gu=gu, ff=ff, g1=g1, g2=g2, tm=tm, lru=lru,
                    hn1=hn1, hn2=hn2, wb=wb)


def _backward_layer(l, dx, s, after=(), midway=None, midway2=None, midway3=None, late=None):
    S = dx.shape[0]
    tm, wb = s["tm"], s["wb"]
    g2 = s["g2"]
    dgu = _bwd_ffn_out(dx, wb["w_ffn_out"], s["gu"], l, after)
    tmb = min(TM_BIG, S)
    dwfo = _mm_tn(s["ff"], dx, DFF_SH, tmb, f"dw_ffn_out_{l}")
    dx1, dg2 = _mm_nt_rms_bwd(
        dgu, [pl.BlockSpec((None, tmb, DFF_SH), lambda i, k: (k, i, 0))],
        wb["w_ffn_in"], [pl.BlockSpec((None, D, DFF_SH), lambda i, k: (k, 0, 0))],
        4, tmb, s["x1"], g2, dx, f"bwd_ffn_in_{l}")
    dwfi = _dw_ffn_in(s["hn2"], dgu, l)
    dmg, dwo = _bwd_out(dx1, wb["w_out"], s["merged"], l)
    mid = () if midway is None else tuple(midway([dwo, dwfi, dwfo]))
    dz6, dws, dbs, dlg, dlb = _gmlp_bwd(dmg, s["z6"], tm["ws_b"], tm["wst_b"], tm["bs_b"], tm["lg"], tm["lb"], mid)
    mid2 = () if midway2 is None else tuple(midway2(dws))
    dz6, dcw, dcb, dwr, dbr, dwi, dbi, dlam = _lru_bwd(dz6, dmg, s["z6"], s["h0"], s["h1"], *s["lru"], after=mid2)

    sub = 3

    def dz_tile(j):
        return pl.BlockSpec((None, tmb, 512), lambda i, k: ((sub * k + j) // 2, i, (sub * k + j) % 2))

    def w_tile(j):
        def w_map(i, k):
            sh, tl = _in_tile(sub * k + j)
            return (sh, 0, tl)
        return pl.BlockSpec((None, D, 512), w_map)

    small = dict(gmlp_ln_g=dlg[0], gmlp_ln_b=dlb[0], gmlp_w_s=dws, gmlp_b_s=dbs[:, :, 0], conv_w=dcw, conv_b=dcb[0],
                 lru_w_r=dwr, lru_b_r=dbr, lru_w_i=dwi, lru_b_i=dbi, lru_lambda=dlam, norm2_g=dg2[0])
    mid3 = () if midway3 is None else tuple(midway3(small))
    dwin = _dw_in(s["hn1"], dz6, l, mid3)
    tail = () if late is None else tuple(late([dwin]))
    dx0, dg1 = _mm_nt_rms_bwd(
        dz6, [dz_tile(j) for j in range(sub)], wb["w_in"], [w_tile(j) for j in range(sub)],
        N_IN_T // sub, tmb, s["x"], s["g1"], dx1, f"bwd_in_{l}", tail)
    return dx0, [dwin, dwo, dwfi, dwfo], dict(small, norm1_g=dg1[0])


def _local_step(x, tgt, p, wbs):
    saved = []
    for l in range(2):
        x, s = _forward_layer(l, x, p, wbs[l], loss=(tgt, p["final_g"][None]) if l else None)
        saved.append(s)
    dx, loss_v, dfg = x
    big, smalls = [None, None], [None, None]
    for l in (1, 0):
        dx, big[l], smalls[l] = _backward_layer(l, dx, saved[l])
    small = {k: jnp.stack([smalls[0][k], smalls[1][k]]) for k in LAYER_SMALL}
    small["final_g"] = dfg[0]
    return loss_v, dx, big, small


def _place():
    x, y, c = lax.axis_index("x"), lax.axis_index("y"), lax.axis_index("c")
    return x, y, c, 2 * x + y


def _chip_at(x, y, d):
    px = 1 - x if d & 2 else x
    py = 1 - y if d & 1 else y
    return px, py, 2 * px + py


HBM = pl.BlockSpec(memory_space=pltpu.HBM)
SEM = pl.BlockSpec(memory_space=pltpu.SEMAPHORE)
DATAFLOW = pltpu.SideEffectType.DATAFLOW_SIDE_EFFECTING


def _in_hbm(a):
    return pltpu.with_memory_space_constraint(a, pltpu.HBM)


def _cast_into(wfs, l, chip_arr, name):
    n = len(wfs)

    def body(ch_ref, *refs):
        for w_ref, o_ref in zip(refs[:n], refs[n:]):
            o_ref[...] = w_ref[...].astype(BF16)

    halves = [(wf.shape[1] // 2, wf.shape[2]) for wf in wfs]
    return pl.pallas_call(
        body, name=name, out_shape=[jax.ShapeDtypeStruct((4, 2, rh, cols), BF16) for rh, cols in halves],
        grid_spec=pltpu.PrefetchScalarGridSpec(
            num_scalar_prefetch=1, grid=(2,),
            in_specs=[pl.BlockSpec((None, None, rh, cols), lambda h, ch: (l, h, 0, 0)) for rh, cols in halves],
            out_specs=[pl.BlockSpec((None, None, rh, cols), lambda h, ch: (ch[0], h, 0, 0)) for rh, cols in halves]),
        compiler_params=_cp(("parallel",)),
    )(chip_arr, *[wf.reshape(2, 2, rh, cols) for wf, (rh, cols) in zip(wfs, halves)])


def _half_block(ref, chip, half, to, send_sem, recv_sem):
    blk = ref.at[chip, half]
    return pltpu.make_async_remote_copy(src_ref=blk, dst_ref=blk, send_sem=send_sem, recv_sem=recv_sem,
                                        device_id=to, device_id_type=MESH)


def _gather_weights(bufs, tiny):
    nt = len(bufs)
    n_ici = max(nt * 3, 1)

    def body(*refs):
        tiny_ref = refs[nt]
        o_refs, tiny_o = refs[nt + 1:2 * nt + 1], refs[2 * nt + 1]
        send, recv, fsend, frecv, tsend, trecv, lsem = refs[2 * nt + 2:]
        x, y, c, chip = _place()
        local = pltpu.make_async_copy(tiny_ref, tiny_o.at[chip], lsem)
        local.start()

        def tin(d, origin_chip, to):
            return pltpu.make_async_remote_copy(
                src_ref=tiny_ref, dst_ref=tiny_o.at[origin_chip], send_sem=tsend.at[d - 1], recv_sem=trecv.at[d - 1],
                device_id=to, device_id_type=MESH)

        sends = []
        for t in range(nt):
            for d in (1, 2, 3):
                px, py, _ = _chip_at(x, y, d)
                sends.append(_half_block(o_refs[t], chip, c, (px, py, c), send.at[3 * t + d - 1], recv.at[3 * t + d - 1]))
        for d in (1, 2, 3):
            px, py, _ = _chip_at(x, y, d)
            sends.append(tin(d, chip, (px, py, c)))
        for cp in sends:
            cp.start()
        passed = []
        for t in range(nt):
            for d in (1, 2, 3):
                k = 3 * t + d - 1
                _, _, pchip = _chip_at(x, y, d)
                _half_block(o_refs[t], pchip, c, (x, y, c), send.at[k], recv.at[k]).wait_recv()
                f = _half_block(o_refs[t], pchip, c, (x, y, 1 - c), fsend.at[k], frecv.at[k])
                f.start()
                passed.append(f)
        for t in range(nt):
            for d in (1, 2, 3):
                k = 3 * t + d - 1
                _, _, pchip = _chip_at(x, y, d)
                _half_block(o_refs[t], pchip, 1 - c, (x, y, 1 - c), fsend.at[k], frecv.at[k]).wait_recv()
        for d in (1, 2, 3):
            _, _, pchip = _chip_at(x, y, d)
            tin(d, pchip, (x, y, c)).wait_recv()
        for cp in sends + passed:
            cp.wait_send()
        local.wait()

    out_shape = [jax.ShapeDtypeStruct(b.shape, b.dtype) for b in bufs]
    out_shape.append(jax.ShapeDtypeStruct((4,) + tiny.shape, tiny.dtype))
    outs = pl.pallas_call(
        body, name="gather_weights_0", out_shape=out_shape,
        in_specs=[ANY] * (nt + 1), out_specs=[ANY] * (nt + 1),
        scratch_shapes=[pltpu.SemaphoreType.DMA((n_ici,)), pltpu.SemaphoreType.DMA((n_ici,)),
                        pltpu.SemaphoreType.DMA((n_ici,)), pltpu.SemaphoreType.DMA((n_ici,)),
                        pltpu.SemaphoreType.DMA((3,)), pltpu.SemaphoreType.DMA((3,)), pltpu.SemaphoreType.DMA],
        input_output_aliases={t: t for t in range(nt)},
        compiler_params=_cp(has_side_effects=True),
    )(*bufs, tiny)
    return outs[:nt], outs[nt]


def _gather_start(bufs, tag, after=()):
    nt, na = len(bufs), len(after)

    def body(*refs):
        b_refs = refs[:nt]
        send, recv = refs[nt + na], refs[nt + na + 1]
        token = refs[2 * nt + na + 2]
        x, y, c, chip = _place()
        for t in range(nt):
            for d in (1, 2, 3):
                px, py, _ = _chip_at(x, y, d)
                _half_block(b_refs[t], chip, c, (px, py, c), send.at[3 * t + d - 1], recv.at[3 * t + d - 1]).start()
        token[...] = jnp.zeros_like(token)

    outs = pl.pallas_call(
        body, name=f"gather_start_{tag}",
        out_shape=(pltpu.SemaphoreType.DMA((3 * nt,)), pltpu.SemaphoreType.DMA((3 * nt,)),
                   *[pltpu.HBM(b.shape, b.dtype) for b in bufs], jax.ShapeDtypeStruct((8, 128), F32)),
        in_specs=[HBM] * nt + [ANY] * na, out_specs=(SEM, SEM, *[HBM] * nt, pl.BlockSpec(memory_space=pltpu.VMEM)),
        input_output_aliases={t: 2 + t for t in range(nt)},
        compiler_params=pltpu.CompilerParams(has_side_effects=DATAFLOW),
    )(*[_in_hbm(b) for b in bufs], *after)
    return outs[0], outs[1], list(outs[2:2 + nt]), outs[2 + nt]


def _gather_wait(send, recv, bufs, after, tag):
    nt = len(bufs)

    def body(*refs):
        b_refs = refs[:nt]
        send_ref, recv_ref = refs[nt], refs[nt + 1]
        x, y, c, chip = _place()
        for t in range(nt):
            for d in (1, 2, 3):
                k = 3 * t + d - 1
                px, py, pchip = _chip_at(x, y, d)
                _half_block(b_refs[t], chip, c, (px, py, c), send_ref.at[k], recv_ref.at[k]).wait_send()
                _half_block(b_refs[t], pchip, c, (px, py, c), send_ref.at[k], recv_ref.at[k]).wait_recv()

    after = tuple(after) if isinstance(after, (tuple, list)) else (after,)
    outs = pl.pallas_call(
        body, name=f"gather_wait_{tag}", out_shape=[pltpu.HBM(b.shape, b.dtype) for b in bufs],
        in_specs=[HBM] * nt + [SEM, SEM] + [ANY] * len(after), out_specs=[HBM] * nt,
        input_output_aliases={t: t for t in range(nt)},
        compiler_params=pltpu.CompilerParams(has_side_effects=DATAFLOW),
    )(*bufs, send, recv, *after)
    return list(outs)


def _gather_pass_on(bufs, tag):
    nt = len(bufs)

    def body(*refs):
        o_refs = refs[nt:2 * nt]
        fsend, frecv = refs[2 * nt:]
        x, y, c, _ = _place()
        cps = []
        for t in range(nt):
            for d in (1, 2, 3):
                k = 3 * t + d - 1
                _, _, pchip = _chip_at(x, y, d)
                cps.append(_half_block(o_refs[t], pchip, c, (x, y, 1 - c), fsend.at[k], frecv.at[k]))
        for cp in cps:
            cp.start()
        for t in range(nt):
            for d in (1, 2, 3):
                k = 3 * t + d - 1
                _, _, pchip = _chip_at(x, y, d)
                _half_block(o_refs[t], pchip, 1 - c, (x, y, 1 - c), fsend.at[k], frecv.at[k]).wait_recv()
        for cp in cps:
            cp.wait_send()

    return pl.pallas_call(
        body, name=f"gather_pass_on_{tag}", out_shape=[jax.ShapeDtypeStruct(b.shape, b.dtype) for b in bufs],
        in_specs=[ANY] * nt, out_specs=[ANY] * nt,
        scratch_shapes=[pltpu.SemaphoreType.DMA((3 * nt,)), pltpu.SemaphoreType.DMA((3 * nt,))],
        input_output_aliases={t: t for t in range(nt)},
        compiler_params=_cp(has_side_effects=True),
    )(*bufs)


def _chip_copy(c_ref, land_ref, x, y, c, d, send_sem, recv_sem):
    px, py, pchip = _chip_at(x, y, d)
    return pltpu.make_async_remote_copy(src_ref=c_ref.at[pchip], dst_ref=land_ref.at[d - 1], send_sem=send_sem, recv_sem=recv_sem,
                                        device_id=(px, py, c), device_id_type=MESH)


def _exchange_start(srcs, lands, copies, nsem, name):
    ns, n = len(srcs), len(srcs) + len(lands)

    def body(*refs):
        for cp in copies(refs[:ns], refs[ns:n], refs[n], refs[n + 1]):
            cp.start()
        token = refs[2 * n + 2]
        token[...] = jnp.zeros_like(token)

    outs = pl.pallas_call(
        body, name=name,
        out_shape=(pltpu.SemaphoreType.DMA((nsem,)), pltpu.SemaphoreType.DMA((nsem,)),
                   *[pltpu.HBM(a.shape, a.dtype) for a in list(srcs) + list(lands)], jax.ShapeDtypeStruct((8, 128), F32)),
        in_specs=[HBM] * n, out_specs=(SEM, SEM, *[HBM] * n, pl.BlockSpec(memory_space=pltpu.VMEM)),
        input_output_aliases={i: 2 + i for i in range(n)},
        compiler_params=pltpu.CompilerParams(has_side_effects=DATAFLOW),
    )(*[_in_hbm(a) for a in list(srcs) + list(lands)])
    return outs[0], outs[1], list(outs[2:2 + ns]), list(outs[2 + ns:2 + n]), outs[2 + n]


def _exchange_wait(send, recv, srcs, lands, after, copies, name):
    ns, n = len(srcs), len(srcs) + len(lands)

    def body(*refs):
        for cp in copies(refs[:ns], refs[ns:n], refs[n], refs[n + 1]):
            cp.wait_send()
            cp.wait_recv()

    outs = pl.pallas_call(
        body, name=name, out_shape=[pltpu.HBM(a.shape, a.dtype) for a in list(srcs) + list(lands)],
        in_specs=[HBM] * n + [SEM, SEM, ANY], out_specs=[HBM] * n,
        input_output_aliases={i: i for i in range(n)},
        compiler_params=pltpu.CompilerParams(has_side_effects=DATAFLOW),
    )(*srcs, *lands, send, recv, after)
    return list(outs[:ns]), list(outs[ns:])


def _pass_on_copies(b_refs, land_refs, send, recv):
    del land_refs
    x, y, c, _ = _place()
    return [_half_block(b_refs[t], _chip_at(x, y, d)[2], c, (x, y, 1 - c), send.at[3 * t + d - 1], recv.at[3 * t + d - 1])
            for t in range(len(b_refs)) for d in (1, 2, 3)]


def _chips_copies(c_refs, land_refs, send, recv):
    x, y, c, _ = _place()
    return [_chip_copy(c_refs[t], land_refs[t], x, y, c, d, send.at[3 * t + d - 1], recv.at[3 * t + d - 1])
            for t in range(len(c_refs)) for d in (1, 2, 3)]


def _sibling_copies(g_refs, land_refs, send, recv):
    x, y, c, _ = _place()
    return [pltpu.make_async_remote_copy(
        src_ref=g_refs[t].at[k, 1 - c], dst_ref=land_refs[t].at[k], send_sem=send.at[4 * t + k], recv_sem=recv.at[4 * t + k],
        device_id=(x, y, 1 - c), device_id_type=MESH) for t in range(len(g_refs)) for k in range(4)]


def _join_copies(f_refs, land_refs, send, recv):
    del land_refs
    x, y, c, _ = _place()
    return [pltpu.make_async_remote_copy(
        src_ref=f_refs[t].at[c], dst_ref=f_refs[t].at[c], send_sem=send.at[t], recv_sem=recv.at[t],
        device_id=(x, y, 1 - c), device_id_type=MESH) for t in range(len(f_refs))]


def _add_half(gs, rs, c_arr, name):
    n = len(gs)

    def body(c_ref, *refs):
        for g_ref, r_ref, o_ref in zip(refs[:n], refs[n:2 * n], refs[2 * n:]):
            o_ref[...] = (g_ref[...] + r_ref[...]).astype(BF16)

    def own(g):
        return pl.BlockSpec((None, None) + g.shape[2:], lambda k, cr: (k, cr[0], 0, 0))

    def blk(g):
        return pl.BlockSpec((None,) + g.shape[2:], lambda k, cr: (k, 0, 0))

    return pl.pallas_call(
        body, name=name, out_shape=[jax.ShapeDtypeStruct((4,) + g.shape[2:], BF16) for g in gs],
        grid_spec=pltpu.PrefetchScalarGridSpec(
            num_scalar_prefetch=1, grid=(4,),
            in_specs=[own(g) for g in gs] + [blk(g) for g in gs], out_specs=[blk(g) for g in gs]),
        compiler_params=_cp(("parallel",)),
    )(c_arr, *gs, *rs)


def _sum_chips(css, r3s, place_arr, name):
    n = len(css)

    def body(pl_ref, *refs):
        up = lambda ref: ref[...].astype(F32)
        for t in range(n):
            a_ref, (r0_ref, r1_ref, r2_ref), o_ref = refs[t], refs[n + 3 * t:n + 3 * t + 3], refs[4 * n + t]
            o_ref[...] = ((up(a_ref) + up(r0_ref)) + up(r1_ref)) + up(r2_ref)

    def blk(cs, first):
        _, rh, cols = cs.shape
        return pl.BlockSpec((None, rh // 2, cols), lambda i, pa: (first(pa), i, 0))

    in_specs = [blk(cs, lambda pa: pa[0]) for cs in css]
    for cs in css:
        in_specs += [blk(cs, lambda pa, d=d: d) for d in range(3)]
    return pl.pallas_call(
        body, name=name, out_shape=[jax.ShapeDtypeStruct((2,) + cs.shape[1:], F32) for cs in css],
        grid_spec=pltpu.PrefetchScalarGridSpec(
            num_scalar_prefetch=1, grid=(2,), in_specs=in_specs, out_specs=[blk(cs, lambda pa: pa[1]) for cs in css]),
        compiler_params=_cp(("parallel",)),
    )(place_arr, *css, *[r3 for r3 in r3s for _ in range(3)])


def _allreduce_small(pack):
    rows = pack.shape[0]
    hr = rows // 2

    def body(p_ref, o_ref, sib, slots, s1, r1, s2, r2, s3, r3):
        x, y, c, chip = _place()
        sibling = (x, y, 1 - c)
        ex = pltpu.make_async_remote_copy(src_ref=p_ref, dst_ref=sib, send_sem=s1, recv_sem=r1,
                                          device_id=sibling, device_id_type=MESH)
        ex.start()
        ex.wait()
        half = pl.ds(pl.multiple_of(c * hr, 16), hr)
        slots[0] = p_ref[half, :] + sib[half, :]
        cps = []
        for d in (1, 2, 3):
            px, py, _ = _chip_at(x, y, d)
            cps.append(pltpu.make_async_remote_copy(
                src_ref=slots.at[0], dst_ref=slots.at[d], send_sem=s2.at[d - 1], recv_sem=r2.at[d - 1],
                device_id=(px, py, c), device_id_type=MESH))
        for cp in cps:
            cp.start()
        for cp in cps:
            cp.wait()
        tot = slots[chip]
        for k in (1, 2, 3):
            tot = tot + slots[jnp.bitwise_xor(chip, k)]
        o_ref[half, :] = tot
        back = pltpu.make_async_remote_copy(src_ref=o_ref.at[half, :], dst_ref=o_ref.at[half, :], send_sem=s3, recv_sem=r3,
                                            device_id=sibling, device_id_type=MESH)
        back.start()
        back.wait()

    vm = pl.BlockSpec(memory_space=pltpu.VMEM)
    return pl.pallas_call(
        body, name="allreduce_small", out_shape=jax.ShapeDtypeStruct((rows, 128), F32),
        in_specs=[vm], out_specs=vm,
        scratch_shapes=[pltpu.VMEM((rows, 128), F32), pltpu.VMEM((4, hr, 128), F32),
                        pltpu.SemaphoreType.DMA, pltpu.SemaphoreType.DMA, pltpu.SemaphoreType.DMA((3,)), pltpu.SemaphoreType.DMA((3,)),
                        pltpu.SemaphoreType.DMA, pltpu.SemaphoreType.DMA],
        compiler_params=_cp(has_side_effects=True),
    )(pack)


def _small_chip_sum(pack, after=()):
    rows = pack.shape[0]
    hr = rows // 2

    def body(p_ref, *rest):
        o_ref, sib, s1, r1 = rest[-4:]
        x, y, c, _ = _place()
        ex = pltpu.make_async_remote_copy(src_ref=p_ref, dst_ref=sib, send_sem=s1, recv_sem=r1,
                                          device_id=(x, y, 1 - c), device_id_type=MESH)
        ex.start()
        ex.wait()
        half = pl.ds(pl.multiple_of(c * hr, 16), hr)
        o_ref[...] = (p_ref[half, :] + sib[half, :]).astype(BF16)

    vm = pl.BlockSpec(memory_space=pltpu.VMEM)
    return pl.pallas_call(
        body, name="small_chip_sum", out_shape=jax.ShapeDtypeStruct((hr, 128), BF16),
        in_specs=[vm] + [ANY] * len(after), out_specs=vm,
        scratch_shapes=[pltpu.VMEM((rows, 128), F32), pltpu.SemaphoreType.DMA, pltpu.SemaphoreType.DMA],
        compiler_params=_cp(has_side_effects=True),
    )(pack, *after)


def _small_copies(c_refs, land_refs, send, recv):
    x, y, c, _ = _place()
    cps = []
    for d in (1, 2, 3):
        px, py, _ = _chip_at(x, y, d)
        cps.append(pltpu.make_async_remote_copy(src_ref=c_refs[0], dst_ref=land_refs[0].at[d - 1], send_sem=send.at[d - 1],
                                                recv_sem=recv.at[d - 1], device_id=(px, py, c), device_id_type=MESH))
    return cps


def _small_total(csum, land):
    hr = csum.shape[0]

    def body(c_ref, l_ref, o_ref, slots, s3, r3):
        x, y, c, chip = _place()
        slots[0] = c_ref[...]
        for d in (1, 2, 3):
            slots[d] = l_ref[d - 1]
        tot = slots[chip].astype(F32)
        for k in (1, 2, 3):
            tot = tot + slots[jnp.bitwise_xor(chip, k)].astype(F32)
        half = pl.ds(pl.multiple_of(c * hr, 16), hr)
        o_ref[half, :] = tot
        back = pltpu.make_async_remote_copy(src_ref=o_ref.at[half, :], dst_ref=o_ref.at[half, :], send_sem=s3, recv_sem=r3,
                                            device_id=(x, y, 1 - c), device_id_type=MESH)
        back.start()
        back.wait()

    vm = pl.BlockSpec(memory_space=pltpu.VMEM)
    return pl.pallas_call(
        body, name="small_total", out_shape=jax.ShapeDtypeStruct((2 * hr, 128), F32), in_specs=[vm, vm], out_specs=vm,
        scratch_shapes=[pltpu.VMEM((4, hr, 128), BF16), pltpu.SemaphoreType.DMA, pltpu.SemaphoreType.DMA],
        compiler_params=_cp(has_side_effects=True),
    )(csum, land)


def _adam_math(gv, wv, mv, vv):
    m2 = ADAM_B1 * mv + (1.0 - ADAM_B1) * gv
    v2 = ADAM_B2 * vv + (1.0 - ADAM_B2) * (gv * gv)
    m_hat = m2 / (1.0 - ADAM_B1 ** ADAM_STEP)
    v_hat = v2 / (1.0 - ADAM_B2 ** ADAM_STEP)
    return -ADAM_LR * (m_hat / (jnp.sqrt(v_hat) + ADAM_EPS) + ADAM_WD * wv), m2, v2


def _adam(g, w, m, v, name):
    rows, cols = g.shape
    rb = rows // 4

    def body(g_ref, w_ref, m_ref, v_ref, d_ref, m2_ref, v2_ref):
        d_ref[...], m2_ref[...], v2_ref[...] = _adam_math(g_ref[...], w_ref[...], m_ref[...], v_ref[...])

    blk = pl.BlockSpec((rb, cols), lambda i: (i, 0))
    shp = jax.ShapeDtypeStruct((rows, cols), F32)
    return pl.pallas_call(
        body, name=name, grid=(4,), in_specs=[blk] * 4, out_specs=[blk] * 3, out_shape=[shp] * 3,
        compiler_params=_cp(("parallel",)),
    )(g, w, m, v)


def _adam_layer(gs, ws, ms, vs, l, prevs, name):
    n = len(gs)
    prev = [a for p4 in prevs if p4 is not None for a in p4]

    def body(*refs):
        outs = refs[len(refs) - 4 * n:]
        for t in range(n):
            g_ref, w_ref, m_ref, v_ref = refs[4 * t:4 * t + 4]
            go_ref, d_ref, m2_ref, v2_ref = outs[4 * t:4 * t + 4]
            gv = g_ref[...]
            go_ref[...] = gv
            d_ref[...], m2_ref[...], v2_ref[...] = _adam_math(gv, w_ref[...], m_ref[...], v_ref[...])

    in_specs, out_specs, out_shape, operands, aliases = [], [], [], [], {}
    for t, g in enumerate(gs):
        rows, cols = g.shape
        lay = pl.BlockSpec((None, rows // 4, cols), lambda i: (l, i, 0))
        in_specs += [pl.BlockSpec((rows // 4, cols), lambda i: (i, 0)), lay, lay, lay]
        operands += [g, ws[t], ms[t], vs[t]]
        out_specs += [lay] * 4
        out_shape += [jax.ShapeDtypeStruct((2, rows, cols), F32)] * 4
    k = 4 * n
    for t, p4 in enumerate(prevs):
        if p4 is not None:
            for j in range(4):
                aliases[k] = 4 * t + j
                k += 1
    outs = pl.pallas_call(
        body, name=name, grid=(4,), in_specs=in_specs + [ANY] * len(prev), out_specs=out_specs, out_shape=out_shape,
        input_output_aliases=aliases, compiler_params=_cp(("parallel",)),
    )(*operands, *prev)
    return [list(outs[4 * t:4 * t + 4]) for t in range(n)]


def _rows128(a):
    return a.reshape(-1, 128)


def _pack(arrs, mult):
    parts = [_rows128(a) for a in arrs]
    rows = sum(q.shape[0] for q in parts)
    pad = -rows % mult
    if pad:
        parts.append(jnp.zeros((pad, 128), F32))
    return jnp.concatenate(parts, axis=0)


def _unpack(pack, shapes):
    out, o = [], 0
    for s in shapes:
        n = 1
        for e in s:
            n *= e
        out.append(pack[o:o + n // 128].reshape(s))
        o += n // 128
    return out


WEIGHTS = ['norm1_g', 'w_in', 'gmlp_ln_g', 'gmlp_ln_b', 'gmlp_w_s', 'gmlp_b_s', 'conv_w', 'conv_b', 'lru_w_r', 'lru_b_r', 'lru_w_i',
           'lru_b_i', 'lru_lambda', 'w_out', 'norm2_g', 'w_ffn_in', 'w_ffn_out', 'final_g']
BIG = ['w_in', 'w_out', 'w_ffn_in', 'w_ffn_out']
SMALL = [n for n in WEIGHTS if n not in BIG]
CHIP_SHARDED_SMALL = ['conv_w', 'lru_b_r', 'lru_b_i', 'lru_lambda']


def kernel(x, norm1_g, w_in, gmlp_ln_g, gmlp_ln_b, gmlp_w_s, gmlp_b_s, conv_w, conv_b, lru_w_r, lru_b_r, lru_w_i, lru_b_i, lru_lambda, w_out, norm2_g, w_ffn_in, w_ffn_out, final_g, loss_target, m_norm1_g, m_w_in, m_gmlp_ln_g, m_gmlp_ln_b, m_gmlp_w_s, m_gmlp_b_s, m_conv_w, m_conv_b, m_lru_w_r, m_lru_b_r, m_lru_w_i, m_lru_b_i, m_lru_lambda, m_w_out, m_norm2_g, m_w_ffn_in, m_w_ffn_out, m_final_g, v_norm1_g, v_w_in, v_gmlp_ln_g, v_gmlp_ln_b, v_gmlp_w_s, v_gmlp_b_s, v_conv_w, v_conv_b, v_lru_w_r, v_lru_b_r, v_lru_w_i, v_lru_b_i, v_lru_lambda, v_w_out, v_norm2_g, v_w_ffn_in, v_w_ffn_out, v_final_g):
    a = dict(locals())
    w = {n: a[n] for n in WEIGHTS}
    mom = {n: a["m_" + n] for n in WEIGHTS}
    var = {n: a["v_" + n] for n in WEIGHTS}
    _, _, c, chip = _place()
    c_arr, chip_arr = jnp.reshape(c, (1,)).astype(jnp.int32), jnp.reshape(chip, (1,)).astype(jnp.int32)
    place_arr = jnp.stack([chip, c]).astype(jnp.int32)

    first, rest = BIG[:1], BIG[1:]

    def as_weights(names, full):
        wb = {n: f.reshape(4, 2 * f.shape[2], f.shape[3]) for n, f in zip(names, full)}
        if "w_out" in wb:
            wb["w_out"] = wb["w_out"].reshape(D, D)
            wb["w_ffn_out"] = wb["w_ffn_out"].reshape(DFF, D)
        return wb

    def cast(names, l, tag):
        return _cast_into([w[n] for n in names], l, chip_arr, f"cast_{tag}")

    def landed(fly, names, after, tag):
        return as_weights(names, _gather_pass_on(_gather_wait(fly[0], fly[1], fly[2], after, tag), tag))

    tiny = _pack([w[n] for n in CHIP_SHARDED_SMALL], 8)
    _, tiny_full = _gather_weights([], tiny)
    fly_in = _gather_start(cast(first, 0, "in"), "in", after=(tiny_full,))
    fly0 = _gather_start(cast(rest, 0, "0"), "0", after=(fly_in[3],))
    fly1 = _gather_start(cast(BIG, 1, "1"), "1", after=(fly0[3],))
    p = {n: w[n] for n in SMALL}
    parts = [_unpack(tiny_full[k], [w[n].shape for n in CHIP_SHARDED_SMALL]) for k in range(4)]
    for i, n in enumerate(CHIP_SHARDED_SMALL):
        p[n] = jnp.concatenate([parts[k][i] for k in range(4)], axis=-1)

    operands = [_layer_operands(l, p) for l in range(2)]
    state_packs = [_pack([src[n] for n in SMALL], 32) for src in (w, mom, var)]
    ahead = tuple(jax.tree.leaves(operands)) + tuple(state_packs)

    passing = {}

    def pass_on_1(gu):
        bufs = _gather_wait(fly1[0], fly1[1], fly1[2], gu, "1")
        passing[1] = _exchange_start(bufs, [], _pass_on_copies, 3 * len(bufs), "gather_pass_on_start_1")
        return (passing[1][-1],)

    def pass_on_0(ya):
        bufs = _gather_wait(fly0[0], fly0[1], fly0[2], ya, "0")
        passing[0] = _exchange_start(bufs, [], _pass_on_copies, 3 * len(bufs), "gather_pass_on_start_0")
        return (passing[0][-1],)

    def rest0(merged):
        send, recv, bufs, _, _ = passing[0]
        return as_weights(rest, _exchange_wait(send, recv, bufs, [], merged, _pass_on_copies, "gather_pass_on_wait_0")[0])

    xa, saved0 = _forward_layer(0, x[0], p, landed(fly_in, first, (fly1[3],) + ahead, "in"), after=(fly0[3], fly1[3]),
                                early=pass_on_0, rest=rest0, near_end=pass_on_1, operands=operands[0])
    send, recv, bufs1, _, _ = passing[1]
    xb, saved1 = _forward_layer(
        1, xa, p, as_weights(BIG, _exchange_wait(send, recv, bufs1, [], xa, _pass_on_copies, "gather_pass_on_wait_1")[0]),
        operands=operands[1], loss=(loss_target[0], p["final_g"][None]))
    dxb, loss_v, dfg = xb
    loss = lax.psum(loss_v[0, 0], ("x", "y", "c"))

    out, flying = {}, {}

    def halves(grads):
        return [g.reshape(4, 2, -1, g.shape[-1]) for g in grads]

    def sibling_start(grads, names, l, tag):
        gs = halves(grads)
        lands = [lax.empty((4,) + g.shape[2:], g.dtype) for g in gs]
        flying["s" + tag] = (names, l) + tuple(
            _exchange_start(gs, lands, _sibling_copies, 4 * len(gs), f"grads_to_sibling_start_{tag}"))
        return (flying["s" + tag][-1],)

    def chips_start(gs, from_sib, names, l, tag):
        cs = _add_half(gs, from_sib, c_arr, f"add_half_{tag}")
        lands = [lax.empty((3,) + a.shape[1:], a.dtype) for a in cs]
        flying[tag] = (names, l) + tuple(_exchange_start(cs, lands, _chips_copies, 3 * len(cs), f"grads_to_chips_start_{tag}"))
        return (flying[tag][-1],)

    def sibling_finish(tag, after):
        names, l, send, recv, gs, lands, _ = flying["s" + tag]
        gs, from_sib = _exchange_wait(send, recv, gs, lands, after, _sibling_copies, f"grads_to_sibling_wait_{tag}")
        return chips_start(gs, from_sib, names, l, tag)

    def reduce_sums(tags, after):
        groups, ts = [], []
        for tag in tags:
            names, l, send, recv, cs, lands, _ = flying[tag]
            cs, lands = _exchange_wait(send, recv, cs, lands, after, _chips_copies, f"grads_to_chips_wait_{tag}")
            ts += _sum_chips(cs, lands, place_arr, f"sum_chips_{tag}")
            groups.append((tag, names))
        flying["j" + tags[0]] = (groups, l) + tuple(_exchange_start(ts, [], _join_copies, len(ts), f"grads_join_start_{tags[0]}"))
        return (flying["j" + tags[0]][-1],)

    def reduce_adam(tag0, after):
        groups, l, send, recv, ts, _, _ = flying["j" + tag0]
        joined = _exchange_wait(send, recv, ts, [], after, _join_copies, f"grads_join_wait_{tag0}")[0]
        for tag, names in groups:
            gs, joined = [j.reshape(w[n].shape[1:]) for n, j in zip(names, joined)], joined[len(names):]
            res = _adam_layer(gs, [w[n] for n in names], [mom[n] for n in names], [var[n] for n in names], l,
                              [out.get(n) for n in names], f"adam_{tag}")
            out.update(zip(names, res))

    def late1(grads):
        return sibling_finish("1a", grads[0]) + sibling_start(grads, first, 1, "1b")

    def midway0(grads):
        return reduce_sums(("1a", "1b"), grads[0]) + sibling_start(grads, rest, 0, "0a")

    def stacked_small(small0):
        small = {k: jnp.stack([small0[k], small1[k]]) for k in LAYER_SMALL}
        return dict(small, final_g=dfg[0])

    def midway3_0(small0):
        small = stacked_small(dict(small0, norm1_g=jnp.zeros((D,), F32)))
        csum = _small_chip_sum(_pack([small[n] for n in SMALL], 32))
        flying["small"] = _exchange_start([csum], [lax.empty((3,) + csum.shape, BF16)], _small_copies, 3, "small_to_chips_start")
        return (flying["small"][-1],)

    def late0(grads):
        tok = sibling_start(grads, first, 0, "0b")
        reduce_adam("1a", tok[0])
        return sibling_finish("0b", out[first[0]][0])

    dxa, big1, small1 = _backward_layer(1, dxb, saved1, midway=lambda grads: sibling_start(grads, rest, 1, "1a"), late=late1)
    dx, big0, small0 = _backward_layer(0, dxa, saved0, after=sibling_finish("1b", dxa), midway=midway0,
                                       midway2=lambda dws: sibling_finish("0a", dws), midway3=midway3_0, late=late0)
    join_tok = reduce_sums(("0a", "0b"), dx)
    small = stacked_small(small0)

    full_shapes = [small[n].shape for n in SMALL]
    send, recv, csum, land, _ = flying["small"]
    csum, land = _exchange_wait(send, recv, csum, land, join_tok[0], _small_copies, "small_to_chips_wait")
    red = _unpack(_small_total(csum[0], land[0]), full_shapes)
    norm1_0 = _allreduce_small(_pack([small0["norm1_g"]], 32))[:D // 128].reshape(D)
    reduce_adam("0a", norm1_0)
    red[SMALL.index("norm1_g")] = red[SMALL.index("norm1_g")].at[0].set(norm1_0)
    g_small = []
    for n, g in zip(SMALL, red):
        if n in CHIP_SHARDED_SMALL:
            g = lax.dynamic_slice_in_dim(g, chip * w[n].shape[-1], w[n].shape[-1], axis=g.ndim - 1)
        g_small.append(g)
    shapes = [w[n].shape for n in SMALL]
    upd = [_unpack(u, shapes) for u in _adam(_pack(g_small, 32), *state_packs, "adam_small")]
    for i, n in enumerate(SMALL):
        out[n] = [g_small[i], upd[0][i], upd[1][i], upd[2][i]]

    return (loss, dx[None]) + tuple(out[n][i] for i in range(4) for n in WEIGHTS)
```
